```python
import math
import jax, jax.numpy as jnp
from jax import lax
import numpy as np

D_MODEL = 1024
BATCH = 8
SEQ = 4096
DEPTH = 1

N_MEM = 256
D_FF = 2816
D_POOL = 512
POOL_WINDOWS = (2, 4, 8, 16)
N_POOL_GROUPS = len(POOL_WINDOWS)
POOL_GROUP = D_POOL // N_POOL_GROUPS
D_SSM = 256
SSM_GROUP = 16
N_SSM_GROUPS = D_SSM // SSM_GROUP
SSM_STATE = 64
N_XHEADS = 4
XHEAD_DIM = D_MODEL // N_XHEADS
D_IN = D_POOL + D_SSM + 2 * D_MODEL
EPS = 1e-6

kernel_name = "hybrid_pool_s5_gated_encoder_layer"


def _rms_norm(v, g):
    vf = v.astype(jnp.float32)
    r = lax.rsqrt(jnp.mean(vf * vf, axis=-1, keepdims=True) + EPS)
    return (vf * r).astype(v.dtype) * g


def _swiglu(u, w_gate, w_up, w_down):
    return (jax.nn.silu(u @ w_gate) * (u @ w_up)) @ w_down


def _centred_pool_minus_self(v, window):
    L = v.shape[1]
    left = window // 2
    right = window - 1 - left
    c = jnp.concatenate([jnp.zeros_like(v[:, :1]), jnp.cumsum(v, axis=1)], axis=1)
    t = jnp.arange(L)
    lo = jnp.clip(t - left, 0, L)
    hi = jnp.clip(t + right + 1, 0, L)
    s = jnp.take(c, hi, axis=1) - jnp.take(c, lo, axis=1)
    cnt = (hi - lo).astype(jnp.float32)[None, :, None]
    return s / cnt - v


def _pool_mixer(p, pool_w, pool_scale):
    B_, L, _ = p.shape
    pf = p.astype(jnp.float32)
    groups = [
        _centred_pool_minus_self(pf[..., g * POOL_GROUP:(g + 1) * POOL_GROUP], w)
        for g, w in enumerate(POOL_WINDOWS)
    ]
    pooled = jnp.stack(groups, axis=2).astype(p.dtype)
    mixed = jnp.einsum('blgc,gcd->blgd', pooled, pool_w)
    return mixed.reshape(B_, L, D_POOL) * pool_scale


def _complex_linear_combine(e1, e2):
    a1r, a1i, b1r, b1i = e1
    a2r, a2i, b2r, b2i = e2
    ar = a2r * a1r - a2i * a1i
    ai = a2r * a1i + a2i * a1r
    br = a2r * b1r - a2i * b1i + b2r
    bi = a2r * b1i + a2i * b1r + b2i
    return (ar, ai, br, bi)


def _s5_bidirectional(s, a_re, a_im, log_dt, b_re, b_im, c_re, c_im, d_skip):
    B_, L, _ = s.shape
    uf = s.astype(jnp.float32).reshape(B_, L, N_SSM_GROUPS, SSM_GROUP)
    y = uf * d_skip.astype(jnp.float32).reshape(N_SSM_GROUPS, SSM_GROUP)
    for direction in range(2):
        ar = a_re[direction].astype(jnp.float32)
        ai = a_im[direction].astype(jnp.float32)
        dt = jnp.exp(log_dt[direction].astype(jnp.float32))[:, None]
        mag = jnp.exp(dt * ar)
        ang = dt * ai
        abr = mag * jnp.cos(ang)
        abi = mag * jnp.sin(ang)
        den = ar * ar + ai * ai
        nr = abr - 1.0
        qr = (nr * ar + abi * ai) / den
        qi = (abi * ar - nr * ai) / den
        br = b_re[direction].astype(jnp.float32)
        bi = b_im[direction].astype(jnp.float32)
        bbr = qr[..., None] * br - qi[..., None] * bi
        bbi = qr[..., None] * bi + qi[..., None] * br
        bur = jnp.einsum('gph,blgh->blgp', bbr, uf)
        bui = jnp.einsum('gph,blgh->blgp', bbi, uf)
        a_r = jnp.broadcast_to(abr, bur.shape)
        a_i = jnp.broadcast_to(abi, bur.shape)
        _, _, xr, xi = lax.associative_scan(
            _complex_linear_combine, (a_r, a_i, bur, bui), axis=1, reverse=(direction == 1))
        cr = c_re[direction].astype(jnp.float32)
        ci = c_im[direction].astype(jnp.float32)
        y = y + jnp.einsum('ghp,blgp->blgh', cr, xr) - jnp.einsum('ghp,blgp->blgh', ci, xi)
    return y.reshape(B_, L, D_SSM).astype(s.dtype)


def _cross_attention(u, mem_n, w_q, w_kv, w_xo):
    B_, L, _ = u.shape
    M = mem_n.shape[1]
    q = (u @ w_q).reshape(B_, L, N_XHEADS, XHEAD_DIM)
    kv = mem_n @ w_kv
    k = kv[..., :D_MODEL].reshape(B_, M, N_XHEADS, XHEAD_DIM)
    v = kv[..., D_MODEL:].reshape(B_, M, N_XHEADS, XHEAD_DIM)
    scores = jnp.einsum('blhd,bmhd->bhlm', q.astype(jnp.float32), k.astype(jnp.float32)) / math.sqrt(XHEAD_DIM)
    probs = jax.nn.softmax(scores, axis=-1).astype(u.dtype)
    o = jnp.einsum('bhlm,bmhd->blhd', probs, v).reshape(B_, L, D_MODEL)
    return o @ w_xo


def _fwd_setup_inputs(seed: int = 0) -> dict:
    key = jax.random.key(seed)
    ks = iter(jax.random.split(key, 48))
    f32 = jnp.float32

    def nrm(shape, scale):
        return jax.random.normal(next(ks), shape, f32) * scale

    def gain(shape):
        return 1.0 + 0.02 * jax.random.normal(next(ks), shape, f32)

    L_ = DEPTH
    G, P, H = N_SSM_GROUPS, SSM_STATE, SSM_GROUP
    inp = {}
    inp['x'] = jax.random.normal(next(ks), (BATCH, SEQ, D_MODEL), f32)
    inp['mem'] = jax.random.normal(next(ks), (BATCH, N_MEM, D_MODEL), f32)
    inp['ffn1_norm'] = gain((L_, D_MODEL))
    inp['ffn1_w_gate'] = nrm((L_, D_MODEL, D_FF), D_MODEL ** -0.5)
    inp['ffn1_w_up'] = nrm((L_, D_MODEL, D_FF), D_MODEL ** -0.5)
    inp['ffn1_w_down'] = nrm((L_, D_FF, D_MODEL), D_FF ** -0.5)
    inp['mix_norm'] = gain((L_, D_MODEL))
    inp['w_in'] = nrm((L_, D_MODEL, D_IN), D_MODEL ** -0.5)
    inp['pool_w'] = nrm((L_, N_POOL_GROUPS, POOL_GROUP, POOL_GROUP), POOL_GROUP ** -0.5)
    inp['pool_scale'] = gain((L_, D_POOL))
    inp['w_pool_proj'] = nrm((L_, D_POOL, D_MODEL), D_POOL ** -0.5)
    a_re = -0.5 + 0.01 * jax.random.normal(next(ks), (L_, 2, G, P), f32)
    a_im = math.pi * jnp.arange(P, dtype=f32) + 0.01 * jax.random.normal(next(ks), (L_, 2, G, P), f32)
    inp['ssm_a_re'] = a_re
    inp['ssm_a_im'] = a_im
    inp['ssm_log_dt'] = jax.random.uniform(next(ks), (L_, 2, G), f32, math.log(1e-3), math.log(1e-1))
    inp['ssm_b_re'] = nrm((L_, 2, G, P, H), (2.0 * H) ** -0.5)
    inp['ssm_b_im'] = nrm((L_, 2, G, P, H), (2.0 * H) ** -0.5)
    inp['ssm_c_re'] = nrm((L_, 2, G, H, P), (2.0 * P) ** -0.5)
    inp['ssm_c_im'] = nrm((L_, 2, G, H, P), (2.0 * P) ** -0.5)
    inp['ssm_d'] = nrm((L_, D_SSM), 1.0)
    inp['w_glu_val'] = nrm((L_, D_SSM, D_MODEL), D_SSM ** -0.5)
    inp['w_glu_gate'] = nrm((L_, D_SSM, D_MODEL), D_SSM ** -0.5)
    inp['w_mix_out'] = nrm((L_, D_MODEL, D_MODEL), D_MODEL ** -0.5)
    inp['xattn_norm'] = gain((L_, D_MODEL))
    inp['mem_norm'] = gain((L_, D_MODEL))
    inp['w_q'] = nrm((L_, D_MODEL, D_MODEL), D_MODEL ** -0.5)
    inp['w_kv'] = nrm((L_, D_MODEL, 2 * D_MODEL), D_MODEL ** -0.5)
    inp['w_xo'] = nrm((L_, D_MODEL, D_MODEL), D_MODEL ** -0.5)
    inp['ffn2_norm'] = gain((L_, D_MODEL))
    inp['ffn2_w_gate'] = nrm((L_, D_MODEL, D_FF), D_MODEL ** -0.5)
    inp['ffn2_w_up'] = nrm((L_, D_MODEL, D_FF), D_MODEL ** -0.5)
    inp['ffn2_w_down'] = nrm((L_, D_FF, D_MODEL), D_FF ** -0.5)
    inp['final_norm'] = gain((D_MODEL,))
    return inp


def _fwd_reference(x, mem, ffn1_norm, ffn1_w_gate, ffn1_w_up, ffn1_w_down,
              mix_norm, w_in, pool_w, pool_scale, w_pool_proj,
              ssm_a_re, ssm_a_im, ssm_log_dt, ssm_b_re, ssm_b_im, ssm_c_re, ssm_c_im, ssm_d,
              w_glu_val, w_glu_gate, w_mix_out,
              xattn_norm, mem_norm, w_q, w_kv, w_xo,
              ffn2_norm, ffn2_w_gate, ffn2_w_up, ffn2_w_down, final_norm):
    h = x
    for l in range(DEPTH):
        h = h + 0.5 * _swiglu(_rms_norm(h, ffn1_norm[l]), ffn1_w_gate[l], ffn1_w_up[l], ffn1_w_down[l])

        u = _rms_norm(h, mix_norm[l])
        proj = u @ w_in[l]
        p = proj[..., :D_POOL]
        s = proj[..., D_POOL:D_POOL + D_SSM]
        g_pool = proj[..., D_POOL + D_SSM:D_POOL + D_SSM + D_MODEL]
        g_ssm = proj[..., D_POOL + D_SSM + D_MODEL:]

        z_pool = _pool_mixer(p, pool_w[l], pool_scale[l]) @ w_pool_proj[l]
        y_ssm = jax.nn.gelu(_s5_bidirectional(s, ssm_a_re[l], ssm_a_im[l], ssm_log_dt[l],
                                              ssm_b_re[l], ssm_b_im[l], ssm_c_re[l], ssm_c_im[l], ssm_d[l]))
        z_ssm = (y_ssm @ w_glu_val[l]) * jax.nn.sigmoid(y_ssm @ w_glu_gate[l])

        merged = jax.nn.sigmoid(g_pool) * z_pool + jax.nn.sigmoid(g_ssm) * z_ssm
        h = h + merged @ w_mix_out[l]

        h = h + _cross_attention(_rms_norm(h, xattn_norm[l]), _rms_norm(mem, mem_norm[l]),
                                 w_q[l], w_kv[l], w_xo[l])

        h = h + 0.5 * _swiglu(_rms_norm(h, ffn2_norm[l]), ffn2_w_gate[l], ffn2_w_up[l], ffn2_w_down[l])
    return _rms_norm(h, final_norm)


import jax as _jax
import jax.numpy as _jnp

TWIN_FORMAT = 'train_step'
FWD_PARAMS = ['x', 'mem', 'ffn1_norm', 'ffn1_w_gate', 'ffn1_w_up', 'ffn1_w_down', 'mix_norm', 'w_in', 'pool_w', 'pool_scale', 'w_pool_proj', 'ssm_a_re', 'ssm_a_im', 'ssm_log_dt', 'ssm_b_re', 'ssm_b_im', 'ssm_c_re', 'ssm_c_im', 'ssm_d', 'w_glu_val', 'w_glu_gate', 'w_mix_out', 'xattn_norm', 'mem_norm', 'w_q', 'w_kv', 'w_xo', 'ffn2_norm', 'ffn2_w_gate', 'ffn2_w_up', 'ffn2_w_down', 'final_norm']
TWIN_WEIGHTS = ['ffn1_norm', 'ffn1_w_gate', 'ffn1_w_up', 'ffn1_w_down', 'mix_norm', 'w_in', 'pool_w', 'pool_scale', 'w_pool_proj', 'ssm_a_re', 'ssm_a_im', 'ssm_log_dt', 'ssm_b_re', 'ssm_b_im', 'ssm_c_re', 'ssm_c_im', 'ssm_d', 'w_glu_val', 'w_glu_gate', 'w_mix_out', 'xattn_norm', 'mem_norm', 'w_q', 'w_kv', 'w_xo', 'ffn2_norm', 'ffn2_w_gate', 'ffn2_w_up', 'ffn2_w_down', 'final_norm']
TWIN_DIFF_INPUT = 'x'
TWIN_INPUTS = ['x', 'mem', 'ffn1_norm', 'ffn1_w_gate', 'ffn1_w_up', 'ffn1_w_down', 'mix_norm', 'w_in', 'pool_w', 'pool_scale', 'w_pool_proj', 'ssm_a_re', 'ssm_a_im', 'ssm_log_dt', 'ssm_b_re', 'ssm_b_im', 'ssm_c_re', 'ssm_c_im', 'ssm_d', 'w_glu_val', 'w_glu_gate', 'w_mix_out', 'xattn_norm', 'mem_norm', 'w_q', 'w_kv', 'w_xo', 'ffn2_norm', 'ffn2_w_gate', 'ffn2_w_up', 'ffn2_w_down', 'final_norm', 'loss_target', 'm_ffn1_norm', 'm_ffn1_w_gate', 'm_ffn1_w_up', 'm_ffn1_w_down', 'm_mix_norm', 'm_w_in', 'm_pool_w', 'm_pool_scale', 'm_w_pool_proj', 'm_ssm_a_re', 'm_ssm_a_im', 'm_ssm_log_dt', 'm_ssm_b_re', 'm_ssm_b_im', 'm_ssm_c_re', 'm_ssm_c_im', 'm_ssm_d', 'm_w_glu_val', 'm_w_glu_gate', 'm_w_mix_out', 'm_xattn_norm', 'm_mem_norm', 'm_w_q', 'm_w_kv', 'm_w_xo', 'm_ffn2_norm', 'm_ffn2_w_gate', 'm_ffn2_w_up', 'm_ffn2_w_down', 'm_final_norm', 'v_ffn1_norm', 'v_ffn1_w_gate', 'v_ffn1_w_up', 'v_ffn1_w_down', 'v_mix_norm', 'v_w_in', 'v_pool_w', 'v_pool_scale', 'v_w_pool_proj', 'v_ssm_a_re', 'v_ssm_a_im', 'v_ssm_log_dt', 'v_ssm_b_re', 'v_ssm_b_im', 'v_ssm_c_re', 'v_ssm_c_im', 'v_ssm_d', 'v_w_glu_val', 'v_w_glu_gate', 'v_w_mix_out', 'v_xattn_norm', 'v_mem_norm', 'v_w_q', 'v_w_kv', 'v_w_xo', 'v_ffn2_norm', 'v_ffn2_w_gate', 'v_ffn2_w_up', 'v_ffn2_w_down', 'v_final_norm']
TWIN_OUTPUTS = ['loss', 'grad_x', 'grad_ffn1_norm', 'grad_ffn1_w_gate', 'grad_ffn1_w_up', 'grad_ffn1_w_down', 'grad_mix_norm', 'grad_w_in', 'grad_pool_w', 'grad_pool_scale', 'grad_w_pool_proj', 'grad_ssm_a_re', 'grad_ssm_a_im', 'grad_ssm_log_dt', 'grad_ssm_b_re', 'grad_ssm_b_im', 'grad_ssm_c_re', 'grad_ssm_c_im', 'grad_ssm_d', 'grad_w_glu_val', 'grad_w_glu_gate', 'grad_w_mix_out', 'grad_xattn_norm', 'grad_mem_norm', 'grad_w_q', 'grad_w_kv', 'grad_w_xo', 'grad_ffn2_norm', 'grad_ffn2_w_gate', 'grad_ffn2_w_up', 'grad_ffn2_w_down', 'grad_final_norm', 'delta_ffn1_norm', 'delta_ffn1_w_gate', 'delta_ffn1_w_up', 'delta_ffn1_w_down', 'delta_mix_norm', 'delta_w_in', 'delta_pool_w', 'delta_pool_scale', 'delta_w_pool_proj', 'delta_ssm_a_re', 'delta_ssm_a_im', 'delta_ssm_log_dt', 'delta_ssm_b_re', 'delta_ssm_b_im', 'delta_ssm_c_re', 'delta_ssm_c_im', 'delta_ssm_d', 'delta_w_glu_val', 'delta_w_glu_gate', 'delta_w_mix_out', 'delta_xattn_norm', 'delta_mem_norm', 'delta_w_q', 'delta_w_kv', 'delta_w_xo', 'delta_ffn2_norm', 'delta_ffn2_w_gate', 'delta_ffn2_w_up', 'delta_ffn2_w_down', 'delta_final_norm', 'new_m_ffn1_norm', 'new_m_ffn1_w_gate', 'new_m_ffn1_w_up', 'new_m_ffn1_w_down', 'new_m_mix_norm', 'new_m_w_in', 'new_m_pool_w', 'new_m_pool_scale', 'new_m_w_pool_proj', 'new_m_ssm_a_re', 'new_m_ssm_a_im', 'new_m_ssm_log_dt', 'new_m_ssm_b_re', 'new_m_ssm_b_im', 'new_m_ssm_c_re', 'new_m_ssm_c_im', 'new_m_ssm_d', 'new_m_w_glu_val', 'new_m_w_glu_gate', 'new_m_w_mix_out', 'new_m_xattn_norm', 'new_m_mem_norm', 'new_m_w_q', 'new_m_w_kv', 'new_m_w_xo', 'new_m_ffn2_norm', 'new_m_ffn2_w_gate', 'new_m_ffn2_w_up', 'new_m_ffn2_w_down', 'new_m_final_norm', 'new_v_ffn1_norm', 'new_v_ffn1_w_gate', 'new_v_ffn1_w_up', 'new_v_ffn1_w_down', 'new_v_mix_norm', 'new_v_w_in', 'new_v_pool_w', 'new_v_pool_scale', 'new_v_w_pool_proj', 'new_v_ssm_a_re', 'new_v_ssm_a_im', 'new_v_ssm_log_dt', 'new_v_ssm_b_re', 'new_v_ssm_b_im', 'new_v_ssm_c_re', 'new_v_ssm_c_im', 'new_v_ssm_d', 'new_v_w_glu_val', 'new_v_w_glu_gate', 'new_v_w_mix_out', 'new_v_xattn_norm', 'new_v_mem_norm', 'new_v_w_q', 'new_v_w_kv', 'new_v_w_xo', 'new_v_ffn2_norm', 'new_v_ffn2_w_gate', 'new_v_ffn2_w_up', 'new_v_ffn2_w_down', 'new_v_final_norm']
TWIN_LEAF_KINDS = {'loss': 'loss', 'grad_x': 'grad_x', 'grad_ffn1_norm': 'grad_w', 'grad_ffn1_w_gate': 'grad_w', 'grad_ffn1_w_up': 'grad_w', 'grad_ffn1_w_down': 'grad_w', 'grad_mix_norm': 'grad_w', 'grad_w_in': 'grad_w', 'grad_pool_w': 'grad_w', 'grad_pool_scale': 'grad_w', 'grad_w_pool_proj': 'grad_w', 'grad_ssm_a_re': 'grad_w', 'grad_ssm_a_im': 'grad_w', 'grad_ssm_log_dt': 'grad_w', 'grad_ssm_b_re': 'grad_w', 'grad_ssm_b_im': 'grad_w', 'grad_ssm_c_re': 'grad_w', 'grad_ssm_c_im': 'grad_w', 'grad_ssm_d': 'grad_w', 'grad_w_glu_val': 'grad_w', 'grad_w_glu_gate': 'grad_w', 'grad_w_mix_out': 'grad_w', 'grad_xattn_norm': 'grad_w', 'grad_mem_norm': 'grad_w', 'grad_w_q': 'grad_w', 'grad_w_kv': 'grad_w', 'grad_w_xo': 'grad_w', 'grad_ffn2_norm': 'grad_w', 'grad_ffn2_w_gate': 'grad_w', 'grad_ffn2_w_up': 'grad_w', 'grad_ffn2_w_down': 'grad_w', 'grad_final_norm': 'grad_w', 'delta_ffn1_norm': 'delta_w', 'delta_ffn1_w_gate': 'delta_w', 'delta_ffn1_w_up': 'delta_w', 'delta_ffn1_w_down': 'delta_w', 'delta_mix_norm': 'delta_w', 'delta_w_in': 'delta_w', 'delta_pool_w': 'delta_w', 'delta_pool_scale': 'delta_w', 'delta_w_pool_proj': 'delta_w', 'delta_ssm_a_re': 'delta_w', 'delta_ssm_a_im': 'delta_w', 'delta_ssm_log_dt': 'delta_w', 'delta_ssm_b_re': 'delta_w', 'delta_ssm_b_im': 'delta_w', 'delta_ssm_c_re': 'delta_w', 'delta_ssm_c_im': 'delta_w', 'delta_ssm_d': 'delta_w', 'delta_w_glu_val': 'delta_w', 'delta_w_glu_gate': 'delta_w', 'delta_w_mix_out': 'delta_w', 'delta_xattn_norm': 'delta_w', 'delta_mem_norm': 'delta_w', 'delta_w_q': 'delta_w', 'delta_w_kv': 'delta_w', 'delta_w_xo': 'delta_w', 'delta_ffn2_norm': 'delta_w', 'delta_ffn2_w_gate': 'delta_w', 'delta_ffn2_w_up': 'delta_w', 'delta_ffn2_w_down': 'delta_w', 'delta_final_norm': 'delta_w', 'new_m_ffn1_norm': 'new_m', 'new_m_ffn1_w_gate': 'new_m', 'new_m_ffn1_w_up': 'new_m', 'new_m_ffn1_w_down': 'new_m', 'new_m_mix_norm': 'new_m', 'new_m_w_in': 'new_m', 'new_m_pool_w': 'new_m', 'new_m_pool_scale': 'new_m', 'new_m_w_pool_proj': 'new_m', 'new_m_ssm_a_re': 'new_m', 'new_m_ssm_a_im': 'new_m', 'new_m_ssm_log_dt': 'new_m', 'new_m_ssm_b_re': 'new_m', 'new_m_ssm_b_im': 'new_m', 'new_m_ssm_c_re': 'new_m', 'new_m_ssm_c_im': 'new_m', 'new_m_ssm_d': 'new_m', 'new_m_w_glu_val': 'new_m', 'new_m_w_glu_gate': 'new_m', 'new_m_w_mix_out': 'new_m', 'new_m_xattn_norm': 'new_m', 'new_m_mem_norm': 'new_m', 'new_m_w_q': 'new_m', 'new_m_w_kv': 'new_m', 'new_m_w_xo': 'new_m', 'new_m_ffn2_norm': 'new_m', 'new_m_ffn2_w_gate': 'new_m', 'new_m_ffn2_w_up': 'new_m', 'new_m_ffn2_w_down': 'new_m', 'new_m_final_norm': 'new_m', 'new_v_ffn1_norm': 'new_v', 'new_v_ffn1_w_gate': 'new_v', 'new_v_ffn1_w_up': 'new_v', 'new_v_ffn1_w_down': 'new_v', 'new_v_mix_norm': 'new_v', 'new_v_w_in': 'new_v', 'new_v_pool_w': 'new_v', 'new_v_pool_scale': 'new_v', 'new_v_w_pool_proj': 'new_v', 'new_v_ssm_a_re': 'new_v', 'new_v_ssm_a_im': 'new_v', 'new_v_ssm_log_dt': 'new_v', 'new_v_ssm_b_re': 'new_v', 'new_v_ssm_b_im': 'new_v', 'new_v_ssm_c_re': 'new_v', 'new_v_ssm_c_im': 'new_v', 'new_v_ssm_d': 'new_v', 'new_v_w_glu_val': 'new_v', 'new_v_w_glu_gate': 'new_v', 'new_v_w_mix_out': 'new_v', 'new_v_xattn_norm': 'new_v', 'new_v_mem_norm': 'new_v', 'new_v_w_q': 'new_v', 'new_v_w_kv': 'new_v', 'new_v_w_xo': 'new_v', 'new_v_ffn2_norm': 'new_v', 'new_v_ffn2_w_gate': 'new_v', 'new_v_ffn2_w_up': 'new_v', 'new_v_ffn2_w_down': 'new_v', 'new_v_final_norm': 'new_v'}


def _forward(args):
    return _fwd_reference(*[args[k] for k in FWD_PARAMS])


def _output_shape():
    out = _jax.eval_shape(lambda: _forward(_fwd_setup_inputs(0)))
    return out.shape, out.dtype

N_MICROBATCH = 1
ADAM_LR = 0.001
ADAM_B1 = 0.9
ADAM_B2 = 0.999
ADAM_EPS = 1e-08
ADAM_WD = 0.01
ADAM_STEP = 10
PER_EXAMPLE_BATCH_AXIS = {'x': 0, 'mem': 0, 'loss_target': 0}
SHARED_INPUTS = []
_WEIGHT_DTYPES = {'ffn1_norm': _jnp.float32, 'ffn1_w_gate': _jnp.float32, 'ffn1_w_up': _jnp.float32, 'ffn1_w_down': _jnp.float32, 'mix_norm': _jnp.float32, 'w_in': _jnp.float32, 'pool_w': _jnp.float32, 'pool_scale': _jnp.float32, 'w_pool_proj': _jnp.float32, 'ssm_a_re': _jnp.float32, 'ssm_a_im': _jnp.float32, 'ssm_log_dt': _jnp.float32, 'ssm_b_re': _jnp.float32, 'ssm_b_im': _jnp.float32, 'ssm_c_re': _jnp.float32, 'ssm_c_im': _jnp.float32, 'ssm_d': _jnp.float32, 'w_glu_val': _jnp.float32, 'w_glu_gate': _jnp.float32, 'w_mix_out': _jnp.float32, 'xattn_norm': _jnp.float32, 'mem_norm': _jnp.float32, 'w_q': _jnp.float32, 'w_kv': _jnp.float32, 'w_xo': _jnp.float32, 'ffn2_norm': _jnp.float32, 'ffn2_w_gate': _jnp.float32, 'ffn2_w_up': _jnp.float32, 'ffn2_w_down': _jnp.float32, 'final_norm': _jnp.float32}
MOMENT_SCALE = {'ffn1_norm': 7.785124e-02, 'ffn1_w_gate': 3.309312e-02, 'ffn1_w_up': 3.210060e-02, 'ffn1_w_down': 5.310671e-02, 'mix_norm': 9.083585e-02, 'w_in': 5.234799e-02, 'pool_w': 1.055933e-01, 'pool_scale': 1.138935e-01, 'w_pool_proj': 7.438316e-02, 'ssm_a_re': 2.787469e-03, 'ssm_a_im': 3.404608e-03, 'ssm_log_dt': 4.026050e+00, 'ssm_b_re': 2.112511e-03, 'ssm_b_im': 2.135030e-03, 'ssm_c_re': 4.349738e-03, 'ssm_c_im': 4.188195e-03, 'ssm_d': 6.351159e-02, 'w_glu_val': 3.012176e-02, 'w_glu_gate': 8.876288e-03, 'w_mix_out': 8.023321e-02, 'xattn_norm': 1.669474e-02, 'mem_norm': 2.384439e-02, 'w_q': 1.635833e-02, 'w_kv': 1.662187e-02, 'w_xo': 1.659220e-02, 'ffn2_norm': 6.533182e-02, 'ffn2_w_gate': 2.730474e-02, 'ffn2_w_up': 2.646254e-02, 'ffn2_w_down': 4.413238e-02, 'final_norm': 3.194131e+01}


def _to_microbatches(a, axis):
    t = _jnp.moveaxis(a, axis, 0)
    t = t.reshape((N_MICROBATCH, t.shape[0] // N_MICROBATCH) + t.shape[1:])
    return _jnp.moveaxis(t, 1, axis + 1)


def setup_inputs(seed: int = 0) -> dict:
    inp = _fwd_setup_inputs(seed)
    key = _jax.random.fold_in(_jax.random.key(seed), 7919)
    shape, _ = _output_shape()
    out = dict(inp)
    out["loss_target"] = _jax.random.normal(_jax.random.fold_in(key, 0), shape, _jnp.float32)
    for i, name in enumerate(TWIN_WEIGHTS):
        w = inp[name].astype(_jnp.float32)
        if MOMENT_SCALE is None:
            s = _jnp.sqrt(_jnp.mean(_jnp.square(w)) + 1e-30)
        else:
            s = MOMENT_SCALE[name]
        km, kv = _jax.random.split(_jax.random.fold_in(key, i + 1))
        out[name] = w
        out["m_" + name] = s * _jax.random.normal(km, w.shape, _jnp.float32)
        out["v_" + name] = (s * s) * _jax.random.uniform(kv, w.shape, _jnp.float32, 0.5, 1.5)
    if N_MICROBATCH > 1:
        for name, axis in PER_EXAMPLE_BATCH_AXIS.items():
            out[name] = _to_microbatches(out[name], axis)
    return {'x': out['x'], 'mem': out['mem'], 'ffn1_norm': out['ffn1_norm'], 'ffn1_w_gate': out['ffn1_w_gate'], 'ffn1_w_up': out['ffn1_w_up'], 'ffn1_w_down': out['ffn1_w_down'], 'mix_norm': out['mix_norm'], 'w_in': out['w_in'], 'pool_w': out['pool_w'], 'pool_scale': out['pool_scale'], 'w_pool_proj': out['w_pool_proj'], 'ssm_a_re': out['ssm_a_re'], 'ssm_a_im': out['ssm_a_im'], 'ssm_log_dt': out['ssm_log_dt'], 'ssm_b_re': out['ssm_b_re'], 'ssm_b_im': out['ssm_b_im'], 'ssm_c_re': out['ssm_c_re'], 'ssm_c_im': out['ssm_c_im'], 'ssm_d': out['ssm_d'], 'w_glu_val': out['w_glu_val'], 'w_glu_gate': out['w_glu_gate'], 'w_mix_out': out['w_mix_out'], 'xattn_norm': out['xattn_norm'], 'mem_norm': out['mem_norm'], 'w_q': out['w_q'], 'w_kv': out['w_kv'], 'w_xo': out['w_xo'], 'ffn2_norm': out['ffn2_norm'], 'ffn2_w_gate': out['ffn2_w_gate'], 'ffn2_w_up': out['ffn2_w_up'], 'ffn2_w_down': out['ffn2_w_down'], 'final_norm': out['final_norm'], 'loss_target': out['loss_target'], 'm_ffn1_norm': out['m_ffn1_norm'], 'm_ffn1_w_gate': out['m_ffn1_w_gate'], 'm_ffn1_w_up': out['m_ffn1_w_up'], 'm_ffn1_w_down': out['m_ffn1_w_down'], 'm_mix_norm': out['m_mix_norm'], 'm_w_in': out['m_w_in'], 'm_pool_w': out['m_pool_w'], 'm_pool_scale': out['m_pool_scale'], 'm_w_pool_proj': out['m_w_pool_proj'], 'm_ssm_a_re': out['m_ssm_a_re'], 'm_ssm_a_im': out['m_ssm_a_im'], 'm_ssm_log_dt': out['m_ssm_log_dt'], 'm_ssm_b_re': out['m_ssm_b_re'], 'm_ssm_b_im': out['m_ssm_b_im'], 'm_ssm_c_re': out['m_ssm_c_re'], 'm_ssm_c_im': out['m_ssm_c_im'], 'm_ssm_d': out['m_ssm_d'], 'm_w_glu_val': out['m_w_glu_val'], 'm_w_glu_gate': out['m_w_glu_gate'], 'm_w_mix_out': out['m_w_mix_out'], 'm_xattn_norm': out['m_xattn_norm'], 'm_mem_norm': out['m_mem_norm'], 'm_w_q': out['m_w_q'], 'm_w_kv': out['m_w_kv'], 'm_w_xo': out['m_w_xo'], 'm_ffn2_norm': out['m_ffn2_norm'], 'm_ffn2_w_gate': out['m_ffn2_w_gate'], 'm_ffn2_w_up': out['m_ffn2_w_up'], 'm_ffn2_w_down': out['m_ffn2_w_down'], 'm_final_norm': out['m_final_norm'], 'v_ffn1_norm': out['v_ffn1_norm'], 'v_ffn1_w_gate': out['v_ffn1_w_gate'], 'v_ffn1_w_up': out['v_ffn1_w_up'], 'v_ffn1_w_down': out['v_ffn1_w_down'], 'v_mix_norm': out['v_mix_norm'], 'v_w_in': out['v_w_in'], 'v_pool_w': out['v_pool_w'], 'v_pool_scale': out['v_pool_scale'], 'v_w_pool_proj': out['v_w_pool_proj'], 'v_ssm_a_re': out['v_ssm_a_re'], 'v_ssm_a_im': out['v_ssm_a_im'], 'v_ssm_log_dt': out['v_ssm_log_dt'], 'v_ssm_b_re': out['v_ssm_b_re'], 'v_ssm_b_im': out['v_ssm_b_im'], 'v_ssm_c_re': out['v_ssm_c_re'], 'v_ssm_c_im': out['v_ssm_c_im'], 'v_ssm_d': out['v_ssm_d'], 'v_w_glu_val': out['v_w_glu_val'], 'v_w_glu_gate': out['v_w_glu_gate'], 'v_w_mix_out': out['v_w_mix_out'], 'v_xattn_norm': out['v_xattn_norm'], 'v_mem_norm': out['v_mem_norm'], 'v_w_q': out['v_w_q'], 'v_w_kv': out['v_w_kv'], 'v_w_xo': out['v_w_xo'], 'v_ffn2_norm': out['v_ffn2_norm'], 'v_ffn2_w_gate': out['v_ffn2_w_gate'], 'v_ffn2_w_up': out['v_ffn2_w_up'], 'v_ffn2_w_down': out['v_ffn2_w_down'], 'v_final_norm': out['v_final_norm']}


def _loss(weights, diff, rest, loss_target):
    with _jax.named_scope("forward"):
        args = {**rest, TWIN_DIFF_INPUT: diff, **{k: w.astype(_WEIGHT_DTYPES[k]) for k, w in weights.items()}}
        y = _forward(args)
    with _jax.named_scope("loss_head"):
        err = _jnp.square(y.astype(_jnp.float32) - loss_target)
        return 0.5 * _jnp.sum(_jnp.mean(err, axis=-1)) if err.ndim else 0.5 * err


def _adamw(w, g, m, v):
    m = ADAM_B1 * m + (1.0 - ADAM_B1) * g
    v = ADAM_B2 * v + (1.0 - ADAM_B2) * _jnp.square(g)
    m_hat = m / (1.0 - ADAM_B1 ** ADAM_STEP)
    v_hat = v / (1.0 - ADAM_B2 ** ADAM_STEP)
    delta = -ADAM_LR * (m_hat / (_jnp.sqrt(v_hat) + ADAM_EPS) + ADAM_WD * w)
    return delta, m, v


def reference(x, mem, ffn1_norm, ffn1_w_gate, ffn1_w_up, ffn1_w_down, mix_norm, w_in, pool_w, pool_scale, w_pool_proj, ssm_a_re, ssm_a_im, ssm_log_dt, ssm_b_re, ssm_b_im, ssm_c_re, ssm_c_im, ssm_d, w_glu_val, w_glu_gate, w_mix_out, xattn_norm, mem_norm, w_q, w_kv, w_xo, ffn2_norm, ffn2_w_gate, ffn2_w_up, ffn2_w_down, final_norm, loss_target, m_ffn1_norm, m_ffn1_w_gate, m_ffn1_w_up, m_ffn1_w_down, m_mix_norm, m_w_in, m_pool_w, m_pool_scale, m_w_pool_proj, m_ssm_a_re, m_ssm_a_im, m_ssm_log_dt, m_ssm_b_re, m_ssm_b_im, m_ssm_c_re, m_ssm_c_im, m_ssm_d, m_w_glu_val, m_w_glu_gate, m_w_mix_out, m_xattn_norm, m_mem_norm, m_w_q, m_w_kv, m_w_xo, m_ffn2_norm, m_ffn2_w_gate, m_ffn2_w_up, m_ffn2_w_down, m_final_norm, v_ffn1_norm, v_ffn1_w_gate, v_ffn1_w_up, v_ffn1_w_down, v_mix_norm, v_w_in, v_pool_w, v_pool_scale, v_w_pool_proj, v_ssm_a_re, v_ssm_a_im, v_ssm_log_dt, v_ssm_b_re, v_ssm_b_im, v_ssm_c_re, v_ssm_c_im, v_ssm_d, v_w_glu_val, v_w_glu_gate, v_w_mix_out, v_xattn_norm, v_mem_norm, v_w_q, v_w_kv, v_w_xo, v_ffn2_norm, v_ffn2_w_gate, v_ffn2_w_up, v_ffn2_w_down, v_final_norm):
    given = dict(x=x, mem=mem, ffn1_norm=ffn1_norm, ffn1_w_gate=ffn1_w_gate, ffn1_w_up=ffn1_w_up, ffn1_w_down=ffn1_w_down, mix_norm=mix_norm, w_in=w_in, pool_w=pool_w, pool_scale=pool_scale, w_pool_proj=w_pool_proj, ssm_a_re=ssm_a_re, ssm_a_im=ssm_a_im, ssm_log_dt=ssm_log_dt, ssm_b_re=ssm_b_re, ssm_b_im=ssm_b_im, ssm_c_re=ssm_c_re, ssm_c_im=ssm_c_im, ssm_d=ssm_d, w_glu_val=w_glu_val, w_glu_gate=w_glu_gate, w_mix_out=w_mix_out, xattn_norm=xattn_norm, mem_norm=mem_norm, w_q=w_q, w_kv=w_kv, w_xo=w_xo, ffn2_norm=ffn2_norm, ffn2_w_gate=ffn2_w_gate, ffn2_w_up=ffn2_w_up, ffn2_w_down=ffn2_w_down, final_norm=final_norm, loss_target=loss_target, m_ffn1_norm=m_ffn1_norm, m_ffn1_w_gate=m_ffn1_w_gate, m_ffn1_w_up=m_ffn1_w_up, m_ffn1_w_down=m_ffn1_w_down, m_mix_norm=m_mix_norm, m_w_in=m_w_in, m_pool_w=m_pool_w, m_pool_scale=m_pool_scale, m_w_pool_proj=m_w_pool_proj, m_ssm_a_re=m_ssm_a_re, m_ssm_a_im=m_ssm_a_im, m_ssm_log_dt=m_ssm_log_dt, m_ssm_b_re=m_ssm_b_re, m_ssm_b_im=m_ssm_b_im, m_ssm_c_re=m_ssm_c_re, m_ssm_c_im=m_ssm_c_im, m_ssm_d=m_ssm_d, m_w_glu_val=m_w_glu_val, m_w_glu_gate=m_w_glu_gate, m_w_mix_out=m_w_mix_out, m_xattn_norm=m_xattn_norm, m_mem_norm=m_mem_norm, m_w_q=m_w_q, m_w_kv=m_w_kv, m_w_xo=m_w_xo, m_ffn2_norm=m_ffn2_norm, m_ffn2_w_gate=m_ffn2_w_gate, m_ffn2_w_up=m_ffn2_w_up, m_ffn2_w_down=m_ffn2_w_down, m_final_norm=m_final_norm, v_ffn1_norm=v_ffn1_norm, v_ffn1_w_gate=v_ffn1_w_gate, v_ffn1_w_up=v_ffn1_w_up, v_ffn1_w_down=v_ffn1_w_down, v_mix_norm=v_mix_norm, v_w_in=v_w_in, v_pool_w=v_pool_w, v_pool_scale=v_pool_scale, v_w_pool_proj=v_w_pool_proj, v_ssm_a_re=v_ssm_a_re, v_ssm_a_im=v_ssm_a_im, v_ssm_log_dt=v_ssm_log_dt, v_ssm_b_re=v_ssm_b_re, v_ssm_b_im=v_ssm_b_im, v_ssm_c_re=v_ssm_c_re, v_ssm_c_im=v_ssm_c_im, v_ssm_d=v_ssm_d, v_w_glu_val=v_w_glu_val, v_w_glu_gate=v_w_glu_gate, v_w_mix_out=v_w_mix_out, v_xattn_norm=v_xattn_norm, v_mem_norm=v_mem_norm, v_w_q=v_w_q, v_w_kv=v_w_kv, v_w_xo=v_w_xo, v_ffn2_norm=v_ffn2_norm, v_ffn2_w_gate=v_ffn2_w_gate, v_ffn2_w_up=v_ffn2_w_up, v_ffn2_w_down=v_ffn2_w_down, v_final_norm=v_final_norm)
    weights = {n: given[n] for n in TWIN_WEIGHTS}
    shared = {n: given[n] for n in SHARED_INPUTS}
    per_example = {n: given[n] for n in ['x', 'mem']}
    grad_fn = _jax.value_and_grad(_loss, argnums=(0, 1))

    def one_microbatch(ex, loss_target):
        ex = dict(ex)
        diff = ex.pop(TWIN_DIFF_INPUT)
        return grad_fn(weights, diff, {**shared, **ex}, loss_target)

    if N_MICROBATCH == 1:
        loss, (grad_w, grad_x) = one_microbatch(per_example, given["loss_target"])
    else:
        def body(carry, xs):
            loss_sum, grad_sum = carry
            l_k, (gw_k, gx_k) = one_microbatch(xs[0], xs[1])
            with _jax.named_scope("update"):
                return (loss_sum + l_k, _jax.tree.map(_jnp.add, grad_sum, gw_k)), gx_k

        init = (_jnp.zeros((), _jnp.float32), _jax.tree.map(_jnp.zeros_like, weights))
        (loss, grad_w), grad_x = _jax.lax.scan(body, init, (per_example, given["loss_target"]))
    with _jax.named_scope("update"):
        delta_w, new_m, new_v = {}, {}, {}
        for n in TWIN_WEIGHTS:
            delta_w[n], new_m[n], new_v[n] = _adamw(weights[n], grad_w[n], given["m_" + n], given["v_" + n])
    return (loss, grad_x, *[grad_w[n] for n in TWIN_WEIGHTS], *[delta_w[n] for n in TWIN_WEIGHTS],
            *[new_m[n] for n in TWIN_WEIGHTS], *[new_v[n] for n in TWIN_WEIGHTS])
```

```python
import functools
import math

import jax
import jax.numpy as jnp
from jax import lax
from jax.experimental import pallas as pl
from jax.experimental.pallas import tpu as pltpu

F32 = jnp.float32
BF16 = jnp.bfloat16
SDS = jax.ShapeDtypeStruct
BS = pl.BlockSpec
MESH = pl.DeviceIdType.MESH

D_MODEL = 1024
D_FF = 2816
N_SHARD = 4
FF_SH = D_FF // N_SHARD
D_POOL = 512
POOL_WINDOWS = (2, 4, 8, 16)
POOL_GROUP = 128
D_SSM = 256
SSM_GROUPS = 16
SSM_GROUP = 16
SSM_STATE = 64
SSM_CH = SSM_GROUPS * SSM_STATE
N_HEADS = 4
HEAD_DIM = 256
EPS = 1e-6
ADAM_LR, ADAM_B1, ADAM_B2, ADAM_EPS, ADAM_WD, ADAM_STEP = 0.001, 0.9, 0.999, 1e-08, 0.01, 10

VMEM_LIMIT_V7X = 52 * 1024 * 1024
TM = 512

NN = (((1,), (0,)), ((), ()))
NT = (((1,), (1,)), ((), ()))
TN = (((0,), (0,)), ((), ()))


def _params(*sem):
    return pltpu.CompilerParams(dimension_semantics=sem if sem else None, vmem_limit_bytes=VMEM_LIMIT_V7X)


def _dot(a, b, dims=NN):
    return lax.dot_general(a.astype(BF16), b.astype(BF16), dims, preferred_element_type=F32)


def _sigmoid(v):
    return 1.0 / (1.0 + jnp.exp(-v))


def _block_dims(spec):
    return tuple(d for d in spec.block_shape if d is not None)


def _mm(name, pairs, *, grid, out_shape, out_spec, red_axis=None, extras=(), epilogue=None):
    n_pairs, n_extra = len(pairs), len(extras)
    n_red = grid[red_axis] if red_axis is not None else 1
    dims = [p[4] for p in pairs]

    def body(*refs):
        ab = refs[:2 * n_pairs]
        ex = refs[2 * n_pairs:2 * n_pairs + n_extra]
        o_ref = refs[2 * n_pairs + n_extra]

        def partial():
            acc = None
            for p in range(n_pairs):
                t = _dot(ab[2 * p][...], ab[2 * p + 1][...], dims[p])
                acc = t if acc is None else acc + t
            return acc

        def finish(acc):
            res = epilogue(acc, *[e[...] for e in ex]) if epilogue is not None else acc
            o_ref[...] = res.astype(o_ref.dtype)

        if n_red == 1:
            finish(partial())
        else:
            acc_ref = refs[-1]
            k = pl.program_id(red_axis)

            @pl.when(k == 0)
            def _():
                acc_ref[...] = jnp.zeros_like(acc_ref)

            acc_ref[...] += partial()

            @pl.when(k == n_red - 1)
            def _():
                finish(acc_ref[...])

    operands, in_specs = [], []
    for a, a_spec, b, b_spec, _ in pairs:
        operands += [a, b]
        in_specs += [a_spec, b_spec]
    for e, e_spec in extras:
        operands.append(e)
        in_specs.append(e_spec)
    scratch = [pltpu.VMEM(_block_dims(out_spec), F32)] if n_red > 1 else []
    sem = tuple("arbitrary" if ax == red_axis else "parallel" for ax in range(len(grid)))
    return pl.pallas_call(body, out_shape=out_shape, grid=grid, in_specs=in_specs, out_specs=out_spec,
                          scratch_shapes=scratch, name=name, compiler_params=_params(*sem))(*operands)


def _rmsnorm(name, h, gain, tm):
    t, d = h.shape

    def body(h_ref, g_ref, u_ref):
        hv = h_ref[...]
        r = lax.rsqrt(jnp.mean(hv * hv, axis=-1, keepdims=True) + EPS)
        u_ref[...] = ((hv * r) * g_ref[...]).astype(u_ref.dtype)

    return pl.pallas_call(
        body, out_shape=SDS((t, d), BF16), grid=(t // tm,),
        in_specs=[BS((tm, d), lambda i: (i, 0)), BS((1, d), lambda i: (0, 0))],
        out_specs=BS((tm, d), lambda i: (i, 0)), name=name, compiler_params=_params("parallel"))(h, gain)


def _rmsnorm_bwd(name, h, gain, du, dh_in, tm):
    t, d = h.shape
    has_in = dh_in is not None

    def body(*refs):
        if has_in:
            h_ref, g_ref, du_ref, dhin_ref, dh_ref, dhb_ref, dg_ref = refs
        else:
            h_ref, g_ref, du_ref, dh_ref, dhb_ref, dg_ref = refs
        i = pl.program_id(0)
        hv = h_ref[...]
        r = lax.rsqrt(jnp.mean(hv * hv, axis=-1, keepdims=True) + EPS)
        n = hv * r
        duv = du_ref[...].astype(F32)
        dn = duv * g_ref[...]
        dh = r * (dn - n * jnp.mean(dn * n, axis=-1, keepdims=True))
        if has_in:
            dh = dhin_ref[...] + dh
        dh_ref[...] = dh
        dhb_ref[...] = dh.astype(BF16)

        @pl.when(i == 0)
        def _():
            dg_ref[...] = jnp.zeros_like(dg_ref)

        dg_ref[...] += jnp.sum(duv * n, axis=0, keepdims=True)

    row = BS((tm, d), lambda i: (i, 0))
    vec = BS((1, d), lambda i: (0, 0))
    operands = [h, gain, du] + ([dh_in] if has_in else [])
    in_specs = [row, vec, row] + ([row] if has_in else [])
    return pl.pallas_call(
        body, out_shape=(SDS((t, d), F32), SDS((t, d), BF16), SDS((1, d), F32)), grid=(t // tm,),
        in_specs=in_specs, out_specs=(row, row, vec), name=name, compiler_params=_params("arbitrary"))(*operands)


def _loss_head(h, gain, target, tm):
    t, d = h.shape

    def body(h_ref, g_ref, t_ref, loss_ref, dh_ref, dhb_ref, dg_ref):
        i = pl.program_id(0)
        hv = h_ref[...]
        g = g_ref[...]
        r = lax.rsqrt(jnp.mean(hv * hv, axis=-1, keepdims=True) + EPS)
        n = hv * r
        err = n * g - t_ref[...]
        dy = err * (1.0 / d)
        dn = dy * g
        dh = r * (dn - n * jnp.mean(dn * n, axis=-1, keepdims=True))
        dh_ref[...] = dh
        dhb_ref[...] = dh.astype(BF16)

        @pl.when(i == 0)
        def _():
            dg_ref[...] = jnp.zeros_like(dg_ref)
            loss_ref[...] = jnp.zeros_like(loss_ref)

        dg_ref[...] += jnp.sum(dy * n, axis=0, keepdims=True)
        part = 0.5 * jnp.sum(jnp.mean(err * err, axis=-1, keepdims=True), axis=0, keepdims=True)
        loss_ref[...] += jnp.broadcast_to(part, loss_ref.shape)

    row = BS((tm, d), lambda i: (i, 0))
    vec = BS((1, d), lambda i: (0, 0))
    return pl.pallas_call(
        body, out_shape=(SDS((1, 128), F32), SDS((t, d), F32), SDS((t, d), BF16), SDS((1, d), F32)),
        grid=(t // tm,), in_specs=[row, vec, row],
        out_specs=(BS((1, 128), lambda i: (0, 0)), row, row, vec),
        name="loss_head", compiler_params=_params("arbitrary"))(h, gain, target)


def _ffn_up(name, u, w_a, i_gate, i_up, tm):
    t, d = u.shape

    def body(u_ref, wg_ref, wu_ref, g_ref, up_ref, a_ref):
        uv = u_ref[...]
        g = _dot(uv, wg_ref[...])
        up = _dot(uv, wu_ref[...])
        g_ref[...] = g.astype(BF16)
        up_ref[...] = up.astype(BF16)
        a_ref[...] = (g * _sigmoid(g) * up).astype(BF16)

    hid = BS((None, tm, FF_SH), lambda s, i: (s, i, 0))
    shape = SDS((N_SHARD, t, FF_SH), BF16)
    return pl.pallas_call(
        body, out_shape=(shape, shape, shape), grid=(N_SHARD, t // tm),
        in_specs=[BS((tm, d), lambda s, i: (i, 0)),
                  BS((None, None, d, FF_SH), lambda s, i: (s, i_gate, 0, 0)),
                  BS((None, None, d, FF_SH), lambda s, i: (s, i_up, 0, 0))],
        out_specs=(hid, hid, hid), name=name, compiler_params=_params("parallel", "parallel"))(u, w_a, w_a)


def _ffn_down(name, a, w_b, i_down, resid, tm):
    t, d = resid.shape
    return _mm(name, [(a, BS((None, tm, FF_SH), lambda i, s: (s, i, 0)),
                       w_b, BS((None, None, FF_SH, d), lambda i, s: (s, i_down, 0, 0)), NN)],
               grid=(t // tm, N_SHARD), red_axis=1, out_shape=SDS((t, d), F32),
               out_spec=BS((tm, d), lambda i, s: (i, 0)),
               extras=[(resid, BS((tm, d), lambda i, s: (i, 0)))],
               epilogue=lambda acc, res: res + 0.5 * acc)


def _ffn_bwd_act(name, dh_b, w_b, i_down, g, up, tm):
    t, d = dh_b.shape

    def body(dh_ref, wd_ref, g_ref, up_ref, dg_ref, dup_ref):
        da = 0.5 * _dot(dh_ref[...], wd_ref[...], NT)
        gv = g_ref[...].astype(F32)
        uv = up_ref[...].astype(F32)
        sg = _sigmoid(gv)
        silu = gv * sg
        dg_ref[...] = (da * uv * (sg + silu * (1.0 - sg))).astype(BF16)
        dup_ref[...] = (da * silu).astype(BF16)

    hid = BS((None, tm, FF_SH), lambda s, i: (s, i, 0))
    shape = SDS((N_SHARD, t, FF_SH), BF16)
    return pl.pallas_call(
        body, out_shape=(shape, shape), grid=(N_SHARD, t // tm),
        in_specs=[BS((tm, d), lambda s, i: (i, 0)),
                  BS((None, None, FF_SH, d), lambda s, i: (s, i_down, 0, 0)), hid, hid],
        out_specs=(hid, hid), name=name, compiler_params=_params("parallel", "parallel"))(dh_b, w_b, g, up)


def _ffn_dw_up(name, u, dg, dup, tm):
    t, d = u.shape
    n_t = t // tm

    def body(u_ref, dg_ref, dup_ref, wg_ref, wu_ref, accg, accu):
        i = pl.program_id(1)

        @pl.when(i == 0)
        def _():
            accg[...] = jnp.zeros_like(accg)
            accu[...] = jnp.zeros_like(accu)

        uv = u_ref[...]
        accg[...] += _dot(uv, dg_ref[...], TN)
        accu[...] += _dot(uv, dup_ref[...], TN)

        @pl.when(i == n_t - 1)
        def _():
            wg_ref[...] = accg[...].astype(BF16)
            wu_ref[...] = accu[...].astype(BF16)

    hid = BS((None, tm, FF_SH), lambda s, i: (s, i, 0))
    out = BS((None, d, FF_SH), lambda s, i: (s, 0, 0))
    shape = SDS((N_SHARD, d, FF_SH), BF16)
    return pl.pallas_call(
        body, out_shape=(shape, shape), grid=(N_SHARD, n_t),
        in_specs=[BS((tm, d), lambda s, i: (i, 0)), hid, hid], out_specs=(out, out),
        scratch_shapes=[pltpu.VMEM((d, FF_SH), F32), pltpu.VMEM((d, FF_SH), F32)],
        name=name, compiler_params=_params("parallel", "arbitrary"))(u, dg, dup)


def _ffn_dw_down(name, a, dh_b, tm):
    t, d = dh_b.shape
    return _mm(name, [(a, BS((None, tm, FF_SH), lambda s, i: (s, i, 0)), dh_b, BS((tm, d), lambda s, i: (i, 0)), TN)],
               grid=(N_SHARD, t // tm), red_axis=1, out_shape=SDS((N_SHARD, FF_SH, d), BF16),
               out_spec=BS((None, FF_SH, d), lambda s, i: (s, 0, 0)), epilogue=lambda acc: 0.5 * acc)


def _ffn_dx(name, dg, dup, w_a, i_gate, i_up, tm):
    t = dg.shape[1]
    d = w_a.shape[2]
    hid = BS((None, tm, FF_SH), lambda i, s: (s, i, 0))
    return _mm(name, [(dg, hid, w_a, BS((None, None, d, FF_SH), lambda i, s: (s, i_gate, 0, 0)), NT),
                      (dup, hid, w_a, BS((None, None, d, FF_SH), lambda i, s: (s, i_up, 0, 0)), NT)],
               grid=(t // tm, N_SHARD), red_axis=1, out_shape=SDS((t, d), F32),
               out_spec=BS((tm, d), lambda i, s: (i, 0)))


def _plain_mm(name, a, b, dims, out_dtype, tm, resid=None):
    t = a.shape[0]
    n = b.shape[1] if dims == NN else b.shape[0]
    extras = [(resid, BS((tm, n), lambda i: (i, 0)))] if resid is not None else []
    epi = (lambda acc, res: res + acc) if resid is not None else None
    return _mm(name, [(a, BS((tm, a.shape[1]), lambda i: (i, 0)), b, BS(b.shape, lambda i: (0, 0)), dims)],
               grid=(t // tm,), out_shape=SDS((t, n), out_dtype), out_spec=BS((tm, n), lambda i: (i, 0)),
               extras=extras, epilogue=epi)


def _dw_mm(name, a, b, tm, out_dtype=BF16):
    t, k = a.shape
    n = b.shape[1]
    return _mm(name, [(a, BS((tm, k), lambda i: (i, 0)), b, BS((tm, n), lambda i: (i, 0)), TN)],
               grid=(t // tm,), red_axis=0, out_shape=SDS((k, n), out_dtype), out_spec=BS((k, n), lambda i: (0, 0)))


POOL_CHUNK = 256
POOL_HALO = 8


def _window_sum(v, width, lead):
    n = v.shape[0]
    s = v
    k = 1
    while k < width:
        s = s + pltpu.roll(s, n - k, 0)
        k *= 2
    return pltpu.roll(s, lead, 0) if lead else s


def _pool_count(base, left, right, t, shape):
    pos = base + lax.broadcasted_iota(jnp.int32, shape, 0)
    lo = jnp.maximum(pos - left, 0)
    hi = jnp.minimum(pos + right + 1, t)
    return (hi - lo).astype(F32)


def _pool_fwd(proj, pool_w, pool_scale):
    t = proj.shape[0]
    c, h = POOL_CHUNK, POOL_HALO
    n_chunks = t // c

    def body(proj_hbm, pw_ref, sc_ref, pooled_ref, mixed_ref, ms_ref, pad_ref, sem):
        cp = pltpu.make_async_copy(proj_hbm.at[:, pl.ds(0, D_POOL)], pad_ref.at[pl.ds(h, t), :], sem)
        cp.start()
        pad_ref[pl.ds(0, h), :] = jnp.zeros((h, D_POOL), F32)
        pad_ref[pl.ds(t + h, h), :] = jnp.zeros((h, D_POOL), F32)
        cp.wait()
        for g, width in enumerate(POOL_WINDOWS):
            left = width // 2
            right = width - 1 - left
            cols = slice(g * POOL_GROUP, (g + 1) * POOL_GROUP)
            wmat = pw_ref[g].astype(BF16)
            scale = sc_ref[:, cols]

            def chunk(ci, carry, left=left, right=right, width=width, cols=cols, wmat=wmat, scale=scale):
                base = pl.multiple_of(ci * c, c)
                v = pad_ref[pl.ds(base, c + 2 * h), cols]
                win = _window_sum(v, width, left)[h:h + c]
                cnt = _pool_count(base, left, right, t, (c, POOL_GROUP))
                pooled = (win / cnt - v[h:h + c]).astype(BF16)
                mixed = _dot(pooled, wmat)
                pooled_ref[pl.ds(base, c), cols] = pooled
                mixed_ref[pl.ds(base, c), cols] = mixed.astype(BF16)
                ms_ref[pl.ds(base, c), cols] = (mixed * scale).astype(BF16)
                return carry

            lax.fori_loop(0, n_chunks, chunk, 0)

    vm = BS(memory_space=pltpu.VMEM)
    shape = SDS((t, D_POOL), BF16)
    return pl.pallas_call(
        body, out_shape=(shape, shape, shape),
        in_specs=[BS(memory_space=pl.ANY), vm, vm], out_specs=(vm, vm, vm),
        scratch_shapes=[pltpu.VMEM((t + 2 * h, D_POOL), F32), pltpu.SemaphoreType.DMA],
        name="pool_fwd", compiler_params=_params())(proj, pool_w, pool_scale)


def _pool_bwd(d_ms, mixed, pooled, pool_w, pool_scale):
    t = d_ms.shape[0]
    c, h = POOL_CHUNK, POOL_HALO
    n_chunks = t // c

    def body(dms_ref, mixed_ref, pooled_ref, pw_ref, sc_ref, dp_ref, dsc_ref, dpw_ref, pad_ref):
        pad_ref[pl.ds(0, h), :] = jnp.zeros((h, D_POOL), F32)
        pad_ref[pl.ds(t + h, h), :] = jnp.zeros((h, D_POOL), F32)
        for g, width in enumerate(POOL_WINDOWS):
            left = width // 2
            right = width - 1 - left
            cols = slice(g * POOL_GROUP, (g + 1) * POOL_GROUP)
            wmat = pw_ref[g].astype(BF16)
            scale = sc_ref[:, cols]

            def first(ci, carry, left=left, right=right, cols=cols, wmat=wmat, scale=scale):
                dsc, dpw = carry
                base = pl.multiple_of(ci * c, c)
                dms = dms_ref[pl.ds(base, c), cols].astype(F32)
                dsc = dsc + jnp.sum(dms * mixed_ref[pl.ds(base, c), cols].astype(F32), axis=0, keepdims=True)
                dmix = (dms * scale).astype(BF16)
                dpw = dpw + _dot(pooled_ref[pl.ds(base, c), cols], dmix, TN)
                dpooled = _dot(dmix, wmat, NT)
                cnt = _pool_count(base, left, right, t, (c, POOL_GROUP))
                pad_ref[pl.ds(base + h, c), cols] = dpooled / cnt
                return dsc, dpw

            dsc, dpw = lax.fori_loop(0, n_chunks, first,
                                     (jnp.zeros((1, POOL_GROUP), F32), jnp.zeros((POOL_GROUP, POOL_GROUP), F32)))
            dsc_ref[:, cols] = dsc
            dpw_ref[g] = dpw

            def second(ci, carry, left=left, right=right, width=width, cols=cols):
                base = pl.multiple_of(ci * c, c)
                v = pad_ref[pl.ds(base, c + 2 * h), cols]
                win = _window_sum(v, width, right)[h:h + c]
                cnt = _pool_count(base, left, right, t, (c, POOL_GROUP))
                dp_ref[pl.ds(base, c), cols] = (win - v[h:h + c] * cnt).astype(BF16)
                return carry

            lax.fori_loop(0, n_chunks, second, 0)

    vm = BS(memory_space=pltpu.VMEM)
    return pl.pallas_call(
        body, out_shape=(SDS((t, D_POOL), BF16), SDS((1, D_POOL), F32), SDS((4, POOL_GROUP, POOL_GROUP), F32)),
        in_specs=[vm] * 5, out_specs=(vm, vm, vm),
        scratch_shapes=[pltpu.VMEM((t + 2 * h, D_POOL), F32)],
        name="pool_bwd", compiler_params=_params())(d_ms, mixed, pooled, pool_w, pool_scale)


def _ssm_disc(ar, ai, ldt):
    def body(ar_ref, ai_ref, ldt_ref, abr_ref, abi_ref, qr_ref, qi_ref):
        a_r, a_i = ar_ref[...], ai_ref[...]
        dt = jnp.exp(ldt_ref[...])
        mag = jnp.exp(dt * a_r)
        ang = dt * a_i
        abr = mag * jnp.cos(ang)
        abi = mag * jnp.sin(ang)
        den = a_r * a_r + a_i * a_i
        nr = abr - 1.0
        abr_ref[...] = abr
        abi_ref[...] = abi
        qr_ref[...] = (nr * a_r + abi * a_i) / den
        qi_ref[...] = (abi * a_r - nr * a_i) / den

    vm = BS(memory_space=pltpu.VMEM)
    shape = SDS(ar.shape, F32)
    return pl.pallas_call(body, out_shape=(shape,) * 4, in_specs=[vm] * 3, out_specs=(vm,) * 4,
                          name="ssm_disc", compiler_params=_params())(ar, ai, ldt)


def _ssm_disc_bwd(ar, ai, ldt, d_abr, d_abi, d_qr, d_qi):
    def body(ar_ref, ai_ref, ldt_ref, gabr_ref, gabi_ref, gqr_ref, gqi_ref, dar_ref, dai_ref, dldt_ref):
        a_r, a_i = ar_ref[...], ai_ref[...]
        dt = jnp.exp(ldt_ref[...])
        mag = jnp.exp(dt * a_r)
        ang = dt * a_i
        cs, sn = jnp.cos(ang), jnp.sin(ang)
        abr, abi = mag * cs, mag * sn
        den = a_r * a_r + a_i * a_i
        nr = abr - 1.0
        qr = (nr * a_r + abi * a_i) / den
        qi = (abi * a_r - nr * a_i) / den
        gqr, gqi = gqr_ref[...], gqi_ref[...]
        g_nr_num = gqr / den
        g_ni_num = gqi / den
        g_den = -(gqr * qr + gqi * qi) / den
        g_nr = g_nr_num * a_r - g_ni_num * a_i
        g_abi = g_nr_num * a_i + g_ni_num * a_r
        d_ar = g_nr_num * nr + g_ni_num * abi + 2.0 * a_r * g_den
        d_ai = g_nr_num * abi - g_ni_num * nr + 2.0 * a_i * g_den
        g_abr = gabr_ref[...] + g_nr
        g_abi = gabi_ref[...] + g_abi
        g_mag = g_abr * cs + g_abi * sn
        g_ang = mag * (g_abi * cs - g_abr * sn)
        g_e = g_mag * mag
        d_ar = d_ar + g_e * dt
        d_ai = d_ai + g_ang * dt
        g_dt = g_e * a_r + g_ang * a_i
        dar_ref[...] = d_ar
        dai_ref[...] = d_ai
        dldt_ref[...] = jnp.sum(g_dt * dt, axis=1, keepdims=True)

    vm = BS(memory_space=pltpu.VMEM)
    return pl.pallas_call(body, out_shape=(SDS(ar.shape, F32), SDS(ar.shape, F32), SDS(ldt.shape, F32)),
                          in_specs=[vm] * 7, out_specs=(vm,) * 3, name="ssm_disc_bwd",
                          compiler_params=_params())(ar, ai, ldt, d_abr, d_abi, d_qr, d_qi)


def _ssm_bbar(qr, qi, br, bi):
    def body(qr_ref, qi_ref, br_ref, bi_ref, bbr_ref, bbi_ref):
        q_r, q_i, b_r, b_i = qr_ref[...], qi_ref[...], br_ref[...], bi_ref[...]
        bbr_ref[...] = q_r * b_r - q_i * b_i
        bbi_ref[...] = q_r * b_i + q_i * b_r

    vm = BS(memory_space=pltpu.VMEM)
    shape = SDS(br.shape, F32)
    return pl.pallas_call(body, out_shape=(shape, shape), in_specs=[vm] * 4, out_specs=(vm, vm),
                          name="ssm_bbar", compiler_params=_params())(qr, qi, br, bi)


def _ssm_bbar_bwd(qr, qi, br, bi, g_bbr, g_bbi):
    def body(qr_ref, qi_ref, br_ref, bi_ref, gr_ref, gi_ref, dqr_ref, dqi_ref, dbr_ref, dbi_ref):
        q_r, q_i, b_r, b_i = qr_ref[...], qi_ref[...], br_ref[...], bi_ref[...]
        g_r, g_i = gr_ref[...], gi_ref[...]
        dqr_ref[...] = jnp.sum(g_r * b_r + g_i * b_i, axis=1, keepdims=True)
        dqi_ref[...] = jnp.sum(g_i * b_r - g_r * b_i, axis=1, keepdims=True)
        dbr_ref[...] = g_r * q_r + g_i * q_i
        dbi_ref[...] = g_i * q_r - g_r * q_i

    vm = BS(memory_space=pltpu.VMEM)
    return pl.pallas_call(
        body, out_shape=(SDS(qr.shape, F32), SDS(qr.shape, F32), SDS(br.shape, F32), SDS(br.shape, F32)),
        in_specs=[vm] * 6, out_specs=(vm,) * 4, name="ssm_bbar_bwd",
        compiler_params=_params())(qr, qi, br, bi, g_bbr, g_bbi)


SCAN_ROWS = 256


def _ssm_scan(name, inp, w1, a_r, a_i, w2, reverse):
    t = inp.shape[0]
    rows = SCAN_ROWS
    n = t // rows
    n_groups = rows // 8
    ch = SSM_CH
    at = (lambda i: (n - 1 - i, 0)) if reverse else (lambda i: (i, 0))

    def body(in_ref, w1_ref, ar_ref, ai_ref, w2_ref, st_ref, out_ref, cr_ref, ci_ref, k_ref):
        i = pl.program_id(0)

        @pl.when(i == 0)
        def _():
            ar8 = jnp.broadcast_to(ar_ref[...], (8, ch))
            ai8 = jnp.broadcast_to(ai_ref[...], (8, ch))
            row = lax.broadcasted_iota(jnp.int32, (8, ch), 0)
            rank = (7 - row) if reverse else row
            powers = [(ar8, ai8)]
            for _ in range(7):
                p_r, p_i = powers[-1]
                powers.append((p_r * ar8 - p_i * ai8, p_r * ai8 + p_i * ar8))
            zero = jnp.zeros((8, ch), F32)
            for slot, k in enumerate((1, 2, 4)):
                k_ref[2 * slot] = jnp.where(rank >= k, powers[k - 1][0], zero)
                k_ref[2 * slot + 1] = jnp.where(rank >= k, powers[k - 1][1], zero)
            carry_r, carry_i = zero, zero
            for j in range(8):
                carry_r = jnp.where(rank == j, powers[j][0], carry_r)
                carry_i = jnp.where(rank == j, powers[j][1], carry_i)
            k_ref[6] = carry_r
            k_ref[7] = carry_i
            cr_ref[...] = zero
            ci_ref[...] = zero

        st_ref[...] = _dot(in_ref[...], w1_ref[...])

        def group(gi, carry):
            c_r, c_i = carry
            g = (n_groups - 1 - gi) if reverse else gi
            r0 = pl.multiple_of(g * 8, 8)
            x_r = st_ref[pl.ds(r0, 8), 0:ch]
            x_i = st_ref[pl.ds(r0, 8), ch:2 * ch]
            for slot, k in enumerate((1, 2, 4)):
                shift = (8 - k) if reverse else k
                s_r = pltpu.roll(x_r, shift, 0)
                s_i = pltpu.roll(x_i, shift, 0)
                m_r, m_i = k_ref[2 * slot], k_ref[2 * slot + 1]
                x_r, x_i = x_r + m_r * s_r - m_i * s_i, x_i + m_r * s_i + m_i * s_r
            p_r, p_i = k_ref[6], k_ref[7]
            x_r, x_i = x_r + p_r * c_r - p_i * c_i, x_i + p_r * c_i + p_i * c_r
            st_ref[pl.ds(r0, 8), 0:ch] = x_r
            st_ref[pl.ds(r0, 8), ch:2 * ch] = x_i
            last = 0 if reverse else 7
            return (jnp.broadcast_to(x_r[last:last + 1, :], (8, ch)), jnp.broadcast_to(x_i[last:last + 1, :], (8, ch)))

        c_r, c_i = lax.fori_loop(0, n_groups, group, (cr_ref[...], ci_ref[...]))
        cr_ref[...] = c_r
        ci_ref[...] = c_i
        out_ref[...] = _dot(st_ref[...], w2_ref[...])

    return pl.pallas_call(
        body, out_shape=(SDS((t, 2 * ch), F32), SDS((t, D_SSM), F32)), grid=(n,),
        in_specs=[BS((rows, D_SSM), at), BS((D_SSM, 2 * ch), lambda i: (0, 0)), BS((1, ch), lambda i: (0, 0)),
                  BS((1, ch), lambda i: (0, 0)), BS((2 * ch, D_SSM), lambda i: (0, 0))],
        out_specs=(BS((rows, 2 * ch), at), BS((rows, D_SSM), at)),
        scratch_shapes=[pltpu.VMEM((8, ch), F32), pltpu.VMEM((8, ch), F32), pltpu.VMEM((8, 8, ch), F32)],
        name=name, compiler_params=_params("arbitrary"))(inp, w1, a_r, a_i, w2)


DA_ROWS = 512


def _ssm_da(name, lam, states, reverse):
    t = lam.shape[0]
    rows = DA_ROWS
    n = t // rows
    nb = rows // 8
    ch = SSM_CH
    if reverse:
        halo_at = lambda i: (jnp.minimum((i + 1) * nb, t // 8 - 1), 0)
    else:
        halo_at = lambda i: (jnp.maximum(i * nb - 1, 0), 0)

    def body(lam_ref, x_ref, halo_ref, dr_ref, di_ref):
        i = pl.program_id(0)

        @pl.when(i == 0)
        def _():
            dr_ref[...] = jnp.zeros_like(dr_ref)
            di_ref[...] = jnp.zeros_like(di_ref)

        row = lax.broadcasted_iota(jnp.int32, (rows, ch), 0)
        if reverse:
            edge, shift, h_row, live = rows - 1, rows - 1, 0, i < n - 1
        else:
            edge, shift, h_row, live = 0, 1, 7, i > 0

        def neighbour(lo):
            halo = jnp.where(live, halo_ref[h_row:h_row + 1, lo:lo + ch], 0.0)
            return jnp.where(row == edge, jnp.broadcast_to(halo, (rows, ch)), pltpu.roll(x_ref[:, lo:lo + ch], shift, 0))

        xp_r, xp_i = neighbour(0), neighbour(ch)
        l_r, l_i = lam_ref[:, 0:ch], lam_ref[:, ch:2 * ch]
        dr_ref[...] += jnp.sum(l_r * xp_r + l_i * xp_i, axis=0, keepdims=True)
        di_ref[...] += jnp.sum(l_i * xp_r - l_r * xp_i, axis=0, keepdims=True)

    blk = BS((rows, 2 * ch), lambda i: (i, 0))
    vec = BS((1, ch), lambda i: (0, 0))
    return pl.pallas_call(
        body, out_shape=(SDS((1, ch), F32), SDS((1, ch), F32)), grid=(n,),
        in_specs=[blk, blk, BS((8, 2 * ch), halo_at)], out_specs=(vec, vec),
        name=name, compiler_params=_params("arbitrary"))(lam, states, states)


GELU_C = math.sqrt(2.0 / math.pi)
GELU_K = 0.044715


def _ssm_combine(proj, y_dirs, d_skip, tm):
    t = proj.shape[0]

    def body(s_ref, y_ref, d_ref, yt_ref, g_ref):
        y = s_ref[...] * d_ref[...] + y_ref[0] + y_ref[1]
        yt_ref[...] = y
        th = jnp.tanh(GELU_C * (y + GELU_K * y * y * y))
        g_ref[...] = (0.5 * y * (1.0 + th)).astype(BF16)

    blk = BS((tm, D_SSM), lambda i: (i, 0))
    return pl.pallas_call(
        body, out_shape=(SDS((t, D_SSM), F32), SDS((t, D_SSM), BF16)), grid=(t // tm,),
        in_specs=[BS((tm, D_SSM), lambda i: (i, D_POOL // D_SSM)), BS((2, tm, D_SSM), lambda i: (0, i, 0)),
                  BS((1, D_SSM), lambda i: (0, 0))],
        out_specs=(blk, blk), name="ssm_combine", compiler_params=_params("parallel"))(proj, y_dirs, d_skip)


def _ssm_ds(proj, d_yt, du_dirs, d_skip, tm):
    t = proj.shape[0]

    def body(s_ref, dy_ref, du_ref, d_ref, ds_ref, dd_ref):
        i = pl.program_id(0)
        dy = dy_ref[...]
        ds_ref[...] = (dy * d_ref[...] + du_ref[0] + du_ref[1]).astype(BF16)

        @pl.when(i == 0)
        def _():
            dd_ref[...] = jnp.zeros_like(dd_ref)

        dd_ref[...] += jnp.sum(dy * s_ref[...], axis=0, keepdims=True)

    blk = BS((tm, D_SSM), lambda i: (i, 0))
    vec = BS((1, D_SSM), lambda i: (0, 0))
    return pl.pallas_call(
        body, out_shape=(SDS((t, D_SSM), BF16), SDS((1, D_SSM), F32)), grid=(t // tm,),
        in_specs=[BS((tm, D_SSM), lambda i: (i, D_POOL // D_SSM)), blk, BS((2, tm, D_SSM), lambda i: (0, i, 0)), vec],
        out_specs=(blk, vec), name="ssm_ds", compiler_params=_params("arbitrary"))(proj, d_yt, du_dirs, d_skip)


GP_BLOCK = (D_POOL + D_SSM) // 256
GS_BLOCK = GP_BLOCK + D_MODEL // 256


def _merge_specs(tm):
    return [BS((tm, D_POOL), lambda s, i: (i, 0)), BS((tm, D_SSM), lambda s, i: (i, 0)),
            BS((None, D_POOL, 256), lambda s, i: (s, 0, 0)), BS((None, D_SSM, 256), lambda s, i: (s, 2, 0)),
            BS((None, D_SSM, 256), lambda s, i: (s, 3, 0)),
            BS((tm, 256), lambda s, i: (i, GP_BLOCK + s)), BS((tm, 256), lambda s, i: (i, GS_BLOCK + s))]


def _mixer_merge(ms, yssm, w_e, proj, tm):
    t = ms.shape[0]

    def body(ms_ref, y_ref, wpp_ref, wgv_ref, wgg_ref, gp_ref, gs_ref, o_ref):
        zp = _dot(ms_ref[...], wpp_ref[...])
        yv = y_ref[...]
        zv = _dot(yv, wgv_ref[...])
        zg = _dot(yv, wgg_ref[...])
        o_ref[...] = (_sigmoid(gp_ref[...]) * zp + _sigmoid(gs_ref[...]) * zv * _sigmoid(zg)).astype(BF16)

    col = BS((tm, 256), lambda s, i: (i, s))
    return pl.pallas_call(
        body, out_shape=SDS((t, D_MODEL), BF16), grid=(N_SHARD, t // tm), in_specs=_merge_specs(tm), out_specs=col,
        name="mixer_merge", compiler_params=_params("parallel", "parallel"))(ms, yssm, w_e, w_e, w_e, proj, proj)


def _mixer_merge_bwd(ms, yssm, w_e, proj, dmerged, tm):
    t = ms.shape[0]

    def body(ms_ref, y_ref, wpp_ref, wgv_ref, wgg_ref, gp_ref, gs_ref, dm_ref,
             dgp_ref, dgs_ref, dzp_ref, dzv_ref, dzg_ref):
        zp = _dot(ms_ref[...], wpp_ref[...])
        yv = y_ref[...]
        zv = _dot(yv, wgv_ref[...])
        zg = _dot(yv, wgg_ref[...])
        dm = dm_ref[...].astype(F32)
        sp, ss, sg = _sigmoid(gp_ref[...]), _sigmoid(gs_ref[...]), _sigmoid(zg)
        dgp_ref[...] = (dm * zp * sp * (1.0 - sp)).astype(BF16)
        dgs_ref[...] = (dm * zv * sg * ss * (1.0 - ss)).astype(BF16)
        dzp_ref[...] = (dm * sp).astype(BF16)
        dz = dm * ss
        dzv_ref[...] = (dz * sg).astype(BF16)
        dzg_ref[...] = (dz * zv * sg * (1.0 - sg)).astype(BF16)

    col = BS((tm, 256), lambda s, i: (i, s))
    shape = SDS((t, D_MODEL), BF16)
    return pl.pallas_call(
        body, out_shape=(shape,) * 5, grid=(N_SHARD, t // tm), in_specs=_merge_specs(tm) + [col],
        out_specs=(col,) * 5, name="mixer_merge_bwd",
        compiler_params=_params("parallel", "parallel"))(ms, yssm, w_e, w_e, w_e, proj, proj, dmerged)


def _mixer_dw(ms, yssm, dzp, dzv, dzg, tm):
    t = ms.shape[0]
    n_t = t // tm

    def body(ms_ref, y_ref, dzp_ref, dzv_ref, dzg_ref, o_ref, acc):
        i = pl.program_id(1)

        @pl.when(i == 0)
        def _():
            acc[...] = jnp.zeros_like(acc)

        yv = y_ref[...]
        acc[0:D_POOL, :] += _dot(ms_ref[...], dzp_ref[...], TN)
        acc[D_POOL:D_POOL + D_SSM, :] += _dot(yv, dzv_ref[...], TN)
        acc[D_POOL + D_SSM:, :] += _dot(yv, dzg_ref[...], TN)

        @pl.when(i == n_t - 1)
        def _():
            o_ref[...] = acc[...].astype(BF16)

    col = BS((tm, 256), lambda s, i: (i, s))
    return pl.pallas_call(
        body, out_shape=SDS((N_SHARD, 1024, 256), BF16), grid=(N_SHARD, n_t),
        in_specs=[BS((tm, D_POOL), lambda s, i: (i, 0)), BS((tm, D_SSM), lambda s, i: (i, 0)), col, col, col],
        out_specs=BS((None, 1024, 256), lambda s, i: (s, 0, 0)), scratch_shapes=[pltpu.VMEM((1024, 256), F32)],
        name="mixer_dw", compiler_params=_params("parallel", "arbitrary"))(ms, yssm, dzp, dzv, dzg)


def _mixer_dx(dzp, dzv, dzg, w_e, y_total, tm):
    t = dzp.shape[0]

    def body(dzp_ref, dzv_ref, dzg_ref, wpp_ref, wgv_ref, wgg_ref, yt_ref, dms_ref, dy_ref, acc_ms, acc_y):
        s = pl.program_id(1)

        @pl.when(s == 0)
        def _():
            acc_ms[...] = jnp.zeros_like(acc_ms)
            acc_y[...] = jnp.zeros_like(acc_y)

        acc_ms[...] += _dot(dzp_ref[...], wpp_ref[...], NT)
        acc_y[...] += _dot(dzv_ref[...], wgv_ref[...], NT) + _dot(dzg_ref[...], wgg_ref[...], NT)

        @pl.when(s == N_SHARD - 1)
        def _():
            dms_ref[...] = acc_ms[...].astype(BF16)
            y = yt_ref[...]
            inner = GELU_C * (y + GELU_K * y * y * y)
            th = jnp.tanh(inner)
            dgelu = 0.5 * (1.0 + th) + 0.5 * y * (1.0 - th * th) * GELU_C * (1.0 + 3.0 * GELU_K * y * y)
            dy_ref[...] = acc_y[...] * dgelu

    col = BS((tm, 256), lambda i, s: (i, s))
    return pl.pallas_call(
        body, out_shape=(SDS((t, D_POOL), BF16), SDS((t, D_SSM), F32)), grid=(t // tm, N_SHARD),
        in_specs=[col, col, col, BS((None, D_POOL, 256), lambda i, s: (s, 0, 0)),
                  BS((None, D_SSM, 256), lambda i, s: (s, 2, 0)), BS((None, D_SSM, 256), lambda i, s: (s, 3, 0)),
                  BS((tm, D_SSM), lambda i, s: (i, 0))],
        out_specs=(BS((tm, D_POOL), lambda i, s: (i, 0)), BS((tm, D_SSM), lambda i, s: (i, 0))),
        scratch_shapes=[pltpu.VMEM((tm, D_POOL), F32), pltpu.VMEM((tm, D_SSM), F32)],
        name="mixer_dx", compiler_params=_params("parallel", "arbitrary"))(dzp, dzv, dzg, w_e, w_e, w_e, y_total)


def _attn_probs(q_h, k_h):
    s = _dot(q_h, k_h, NT) * (1.0 / math.sqrt(HEAD_DIM))
    e = jnp.exp(s - jnp.max(s, axis=-1, keepdims=True))
    return e / jnp.sum(e, axis=-1, keepdims=True)


def _attn_fwd(q, kv, tm):
    t = q.shape[0]
    m = kv.shape[0]

    def body(q_ref, kv_ref, o_ref):
        for hd in range(N_HEADS):
            lo = hd * HEAD_DIM
            p = _attn_probs(q_ref[:, lo:lo + HEAD_DIM], kv_ref[:, lo:lo + HEAD_DIM])
            o_ref[:, lo:lo + HEAD_DIM] = _dot(p, kv_ref[:, D_MODEL + lo:D_MODEL + lo + HEAD_DIM]).astype(BF16)

    return pl.pallas_call(
        body, out_shape=SDS((t, D_MODEL), BF16), grid=(t // tm,),
        in_specs=[BS((tm, D_MODEL), lambda i: (i, 0)), BS((m, 2 * D_MODEL), lambda i: (0, 0))],
        out_specs=BS((tm, D_MODEL), lambda i: (i, 0)), name="attn_fwd", compiler_params=_params("parallel"))(q, kv)


def _attn_bwd(q, kv, d_o, tm):
    t = q.shape[0]
    m = kv.shape[0]

    def body(q_ref, kv_ref, do_ref, dq_ref, dkv_ref):
        i = pl.program_id(0)

        @pl.when(i == 0)
        def _():
            dkv_ref[...] = jnp.zeros_like(dkv_ref)

        for hd in range(N_HEADS):
            lo = hd * HEAD_DIM
            q_h = q_ref[:, lo:lo + HEAD_DIM]
            k_h = kv_ref[:, lo:lo + HEAD_DIM]
            v_h = kv_ref[:, D_MODEL + lo:D_MODEL + lo + HEAD_DIM]
            do_h = do_ref[:, lo:lo + HEAD_DIM]
            p = _attn_probs(q_h, k_h)
            dkv_ref[:, D_MODEL + lo:D_MODEL + lo + HEAD_DIM] += _dot(p, do_h, TN)
            dp = _dot(do_h, v_h, NT)
            ds = p * (dp - jnp.sum(dp * p, axis=-1, keepdims=True)) * (1.0 / math.sqrt(HEAD_DIM))
            dq_ref[:, lo:lo + HEAD_DIM] = _dot(ds, k_h).astype(BF16)
            dkv_ref[:, lo:lo + HEAD_DIM] += _dot(ds, q_h, TN)

    row = BS((tm, D_MODEL), lambda i: (i, 0))
    full = BS((m, 2 * D_MODEL), lambda i: (0, 0))
    return pl.pallas_call(
        body, out_shape=(SDS((t, D_MODEL), BF16), SDS((m, 2 * D_MODEL), F32)), grid=(t // tm,),
        in_specs=[row, full, row], out_specs=(row, full), name="attn_bwd",
        compiler_params=_params("arbitrary"))(q, kv, d_o)


CLASS_A = ("ffn1_w_gate", "ffn1_w_up", "ffn2_w_gate", "ffn2_w_up", "w_in")
CLASS_B = ("ffn1_w_down", "ffn2_w_down")
CLASS_C = ("w_mix_out", "w_q", "w_xo")
CLASS_D = ("w_kv",)
CLASS_E = ("w_pool_proj", "w_glu_val", "w_glu_gate")
CLASSES = (CLASS_A, CLASS_B, CLASS_C, CLASS_D, CLASS_E)
BIG = tuple(n for cl in CLASSES for n in cl)
SMALL = ("ffn1_norm", "mix_norm", "pool_w", "pool_scale", "ssm_a_re", "ssm_a_im", "ssm_log_dt", "ssm_b_re",
         "ssm_b_im", "ssm_c_re", "ssm_c_im", "ssm_d", "xattn_norm", "mem_norm", "ffn2_norm", "final_norm")
WEIGHTS = ("ffn1_norm", "ffn1_w_gate", "ffn1_w_up", "ffn1_w_down", "mix_norm", "w_in", "pool_w", "pool_scale",
           "w_pool_proj", "ssm_a_re", "ssm_a_im", "ssm_log_dt", "ssm_b_re", "ssm_b_im", "ssm_c_re", "ssm_c_im",
           "ssm_d", "w_glu_val", "w_glu_gate", "w_mix_out", "xattn_norm", "mem_norm", "w_q", "w_kv", "w_xo",
           "ffn2_norm", "ffn2_w_gate", "ffn2_w_up", "ffn2_w_down", "final_norm")


def _block_diag_in(bb):
    eye = jnp.eye(SSM_GROUPS, dtype=bb.dtype)
    return jnp.einsum("dgph,gk->dghkp", bb, eye).reshape(2, D_SSM, SSM_CH)


def _block_diag_out(cc):
    eye = jnp.eye(SSM_GROUPS, dtype=cc.dtype)
    return jnp.einsum("dghp,gk->dgpkh", cc, eye).reshape(2, SSM_CH, D_SSM)


def _diag_blocks_in(m):
    return jnp.einsum("dghgp->dgph", m.reshape(2, SSM_GROUPS, SSM_GROUP, SSM_GROUPS, SSM_STATE))


def _diag_blocks_out(m):
    return jnp.einsum("dgpgh->dghp", m.reshape(2, SSM_GROUPS, SSM_STATE, SSM_GROUPS, SSM_GROUP))


def _device_step(x, mem, target, gw, sp):
    t = x.shape[0]
    tm = min(TM, t)
    w_a, w_b, w_c, w_d, w_e4 = gw
    w_e = w_e4[:, 0]
    w_in = jnp.transpose(w_a[:, 4], (1, 0, 2)).reshape(D_MODEL, D_FF)
    w_mo, w_q, w_xo = (w_c[:, k].reshape(D_MODEL, D_MODEL) for k in range(3))
    g = {}

    u1 = _rmsnorm("norm_ffn1", x, sp["ffn1_norm"], tm)
    g1, up1, a1 = _ffn_up("ffn1_up", u1, w_a, 0, 1, tm)
    h1 = _ffn_down("ffn1_down", a1, w_b, 0, x, tm)

    u2 = _rmsnorm("norm_mix", h1, sp["mix_norm"], tm)
    proj = _mm("mix_in", [(u2, BS((tm, D_MODEL), lambda j, i: (i, 0)), w_in, BS((D_MODEL, D_FF // 2), lambda j, i: (0, j)), NN)],
               grid=(2, t // tm), out_shape=SDS((t, D_FF), F32), out_spec=BS((tm, D_FF // 2), lambda j, i: (i, j)))
    pooled, mixed, ms = _pool_fwd(proj, sp["pool_w"][0], sp["pool_scale"])

    ar = sp["ssm_a_re"].reshape(2 * SSM_GROUPS, SSM_STATE)
    ai = sp["ssm_a_im"].reshape(2 * SSM_GROUPS, SSM_STATE)
    ldt = sp["ssm_log_dt"].reshape(2 * SSM_GROUPS, 1)
    abr, abi, qr, qi = _ssm_disc(ar, ai, ldt)
    b_r = sp["ssm_b_re"].reshape(2 * SSM_CH, SSM_GROUP)
    b_i = sp["ssm_b_im"].reshape(2 * SSM_CH, SSM_GROUP)
    qr_col, qi_col = qr.reshape(2 * SSM_CH, 1), qi.reshape(2 * SSM_CH, 1)
    bbr, bbi = _ssm_bbar(qr_col, qi_col, b_r, b_i)
    shape_b = (2, SSM_GROUPS, SSM_STATE, SSM_GROUP)
    b_mat = jnp.concatenate([_block_diag_in(bbr.reshape(shape_b)), _block_diag_in(bbi.reshape(shape_b))], axis=-1).astype(BF16)
    c_mat = jnp.concatenate([_block_diag_out(sp["ssm_c_re"][0]), -_block_diag_out(sp["ssm_c_im"][0])], axis=1).astype(BF16)
    b_mat_t = jnp.swapaxes(b_mat, 1, 2)
    c_mat_t = jnp.swapaxes(c_mat, 1, 2)
    a_r = abr.reshape(2, 1, SSM_CH)
    a_i = abi.reshape(2, 1, SSM_CH)

    s_in =proj[:, D_POOL:D_POOL + D_SSM].astype(BF16)
    states, y_dirs = [], []
    for dr in range(2):
        st, yd = _ssm_scan(f"ssm_scan_fwd{dr}", s_in, b_mat[dr], a_r[dr], a_i[dr], c_mat[dr], reverse=(dr == 1))
        states.append(st)
        y_dirs.append(yd)
    y_total, yssm = _ssm_combine(proj, jnp.stack(y_dirs), sp["ssm_d"], tm)

    merged = _mixer_merge(ms, yssm, w_e, proj, tm)
    h2 = _plain_mm("mix_out", merged, w_mo, NN, F32, tm, resid=h1)

    u3 = _rmsnorm("norm_xattn", h2, sp["xattn_norm"], tm)
    mem_n = _rmsnorm("norm_mem", mem, sp["mem_norm"], mem.shape[0])
    q = _plain_mm("attn_q", u3, w_q, NN, BF16, tm)
    n_mem = mem.shape[0]
    kv = _mm("attn_kv", [(mem_n, BS((n_mem, D_MODEL), lambda s: (0, 0)), w_d, BS((None, None, D_MODEL, 512), lambda s: (s, 0, 0, 0)), NN)],
             grid=(N_SHARD,), out_shape=SDS((n_mem, 2 * D_MODEL), BF16), out_spec=BS((n_mem, 512), lambda s: (0, s)))
    o = _attn_fwd(q, kv, tm)
    h3 = _plain_mm("attn_out", o, w_xo, NN, F32, tm, resid=h2)

    u4 = _rmsnorm("norm_ffn2", h3, sp["ffn2_norm"], tm)
    g2, up2, a2 = _ffn_up("ffn2_up", u4, w_a, 2, 3, tm)
    h4 = _ffn_down("ffn2_down", a2, w_b, 1, h3, tm)

    loss, dh4, dh4_b, g["final_norm"] = _loss_head(h4, sp["final_norm"].reshape(1, D_MODEL), target, tm)

    dg2, dup2 = _ffn_bwd_act("ffn2_bwd_act", dh4_b, w_b, 1, g2, up2, tm)
    dw_d2 = _ffn_dw_down("ffn2_dw_down", a2, dh4_b, tm)
    dw_g2, dw_u2 = _ffn_dw_up("ffn2_dw_up", u4, dg2, dup2, tm)
    du4 = _ffn_dx("ffn2_dx", dg2, dup2, w_a, 2, 3, tm)
    dh3, dh3_b, g["ffn2_norm"] = _rmsnorm_bwd("norm_ffn2_bwd", h3, sp["ffn2_norm"], du4, dh4, tm)

    d_o = _plain_mm("attn_out_dx", dh3_b, w_xo, NT, BF16, tm)
    dw_xo = _dw_mm("attn_out_dw", o, dh3_b, tm)
    dq, dkv = _attn_bwd(q, kv, d_o, tm)
    dw_q = _dw_mm("attn_q_dw", u3, dq, tm)
    du3 = _plain_mm("attn_q_dx", dq, w_q, NT, F32, tm)
    dw_kv = _mm("attn_kv_dw", [(mem_n, BS((n_mem, D_MODEL), lambda s: (0, 0)), dkv, BS((n_mem, 512), lambda s: (0, s)), TN)],
                grid=(N_SHARD,), out_shape=SDS((N_SHARD, D_MODEL, 512), BF16), out_spec=BS((None, D_MODEL, 512), lambda s: (s, 0, 0)))
    dmem_n = _mm("attn_kv_dx", [(dkv, BS((n_mem, 512), lambda s: (0, s)), w_d, BS((None, None, D_MODEL, 512), lambda s: (s, 0, 0, 0)), NT)],
                 grid=(N_SHARD,), red_axis=0, out_shape=SDS((n_mem, D_MODEL), F32), out_spec=BS((n_mem, D_MODEL), lambda s: (0, 0)))
    _, _, g["mem_norm"] = _rmsnorm_bwd("norm_mem_bwd", mem, sp["mem_norm"], dmem_n, None, n_mem)
    dh2, dh2_b, g["xattn_norm"] = _rmsnorm_bwd("norm_xattn_bwd", h2, sp["xattn_norm"], du3, dh3, tm)

    dmerged = _plain_mm("mix_out_dx", dh2_b, w_mo, NT, BF16, tm)
    dw_mo = _dw_mm("mix_out_dw", merged, dh2_b, tm)
    d_gp, d_gs, dzp, dzv, dzg = _mixer_merge_bwd(ms, yssm, w_e, proj, dmerged, tm)
    dw_e = _mixer_dw(ms, yssm, dzp, dzv, dzg, tm)
    d_ms, d_yt = _mixer_dx(dzp, dzv, dzg, w_e, y_total, tm)
    dp, d_scale, d_pw = _pool_bwd(d_ms, mixed, pooled, sp["pool_w"][0], sp["pool_scale"])
    g["pool_scale"] = d_scale
    g["pool_w"] = d_pw[None]

    d_yt_b = d_yt.astype(BF16)
    du_dirs, d_abr, d_abi, d_cm, d_bm = [], [], [], [], []
    for dr in range(2):
        lam, du = _ssm_scan(f"ssm_scan_bwd{dr}", d_yt_b, c_mat_t[dr], a_r[dr], -a_i[dr], b_mat_t[dr], reverse=(dr == 0))
        du_dirs.append(du)
        da_r, da_i = _ssm_da(f"ssm_da{dr}", lam, states[dr], reverse=(dr == 1))
        d_abr.append(da_r)
        d_abi.append(da_i)
        d_cm.append(_dw_mm(f"ssm_dc{dr}", states[dr], d_yt_b, tm, F32))
        d_bm.append(_dw_mm(f"ssm_db{dr}", s_in, lam, tm, F32))
    d_cm = jnp.stack(d_cm)
    d_bm = jnp.stack(d_bm)
    g["ssm_c_re"] = _diag_blocks_out(d_cm[:, :SSM_CH])[None]
    g["ssm_c_im"] = -_diag_blocks_out(d_cm[:, SSM_CH:])[None]
    g_bbr = _diag_blocks_in(d_bm[:, :, :SSM_CH]).reshape(2 * SSM_CH, SSM_GROUP)
    g_bbi = _diag_blocks_in(d_bm[:, :, SSM_CH:]).reshape(2 * SSM_CH, SSM_GROUP)
    d_qr, d_qi, d_br, d_bi = _ssm_bbar_bwd(qr_col, qi_col, b_r, b_i, g_bbr, g_bbi)
    g["ssm_b_re"] = d_br.reshape(sp["ssm_b_re"].shape)
    g["ssm_b_im"] = d_bi.reshape(sp["ssm_b_im"].shape)
    d_ar, d_ai, d_ldt = _ssm_disc_bwd(ar, ai, ldt, jnp.stack(d_abr).reshape(ar.shape), jnp.stack(d_abi).reshape(ar.shape),
                                      d_qr.reshape(ar.shape), d_qi.reshape(ar.shape))
    g["ssm_a_re"] = d_ar.reshape(sp["ssm_a_re"].shape)
    g["ssm_a_im"] = d_ai.reshape(sp["ssm_a_im"].shape)
    g["ssm_log_dt"] = d_ldt.reshape(sp["ssm_log_dt"].shape)
    ds, g["ssm_d"] = _ssm_ds(proj, d_yt, jnp.stack(du_dirs), sp["ssm_d"], tm)

    d_proj = jnp.concatenate([dp, ds, d_gp, d_gs], axis=1)
    dw_in = _mm("mix_in_dw", [(u2, BS((tm, D_MODEL), lambda j, i: (i, 0)), d_proj, BS((tm, D_FF // 2), lambda j, i: (i, j)), TN)],
                grid=(2, t // tm), red_axis=1, out_shape=SDS((D_MODEL, D_FF), BF16), out_spec=BS((D_MODEL, D_FF // 2), lambda j, i: (0, j)))
    du2 = _plain_mm("mix_in_dx", d_proj, w_in, NT, F32, tm)
    dh1, dh1_b, g["mix_norm"] = _rmsnorm_bwd("norm_mix_bwd", h1, sp["mix_norm"], du2, dh2, tm)

    dg1, dup1 = _ffn_bwd_act("ffn1_bwd_act", dh1_b, w_b, 0, g1, up1, tm)
    dw_d1 = _ffn_dw_down("ffn1_dw_down", a1, dh1_b, tm)
    dw_g1, dw_u1 = _ffn_dw_up("ffn1_dw_up", u1, dg1, dup1, tm)
    du1 = _ffn_dx("ffn1_dx", dg1, dup1, w_a, 0, 1, tm)
    grad_x, _, g["ffn1_norm"] = _rmsnorm_bwd("norm_ffn1_bwd", x, sp["ffn1_norm"], du1, dh1, tm)

    dw_in_sm = jnp.transpose(dw_in.reshape(D_MODEL, N_SHARD, FF_SH), (1, 0, 2))
    big = (
        jnp.stack([dw_g1, dw_u1, dw_g2, dw_u2, dw_in_sm], axis=1).reshape(N_SHARD, 5 * D_MODEL, FF_SH),
        jnp.stack([dw_d1, dw_d2], axis=1).reshape(N_SHARD, 2 * FF_SH, D_MODEL),
        jnp.stack([dw_mo.reshape(N_SHARD, 256, D_MODEL), dw_q.reshape(N_SHARD, 256, D_MODEL),
                   dw_xo.reshape(N_SHARD, 256, D_MODEL)], axis=1).reshape(N_SHARD, 3 * 256, D_MODEL),
        dw_kv,
        dw_e,
    )
    g["final_norm"] = g["final_norm"].reshape(D_MODEL)
    return loss, grad_x, big, g


def _mesh_place():
    x, y, c = lax.axis_index("x"), lax.axis_index("y"), lax.axis_index("c")
    chips = [(1 - x, y), (x, 1 - y), (1 - x, 1 - y)]
    return x, y, c, chips


def _remote(src, dst, send_sems, recv_sems, k, to):
    return pltpu.make_async_remote_copy(src_ref=src, dst_ref=dst, send_sem=send_sems.at[k], recv_sem=recv_sems.at[k],
                                        device_id=to, device_id_type=MESH)


def _allgather_shards(shards):
    n = len(shards)

    def body(*refs):
        ins, outs = refs[:n], refs[n:2 * n]
        send_sems, recv_sems, local_sems = refs[2 * n:]
        x, y, c, chips = _mesh_place()
        me = 2 * x + y
        sibling = (x, y, 1 - c)
        started, local_copies = [], []
        for k in range(n):
            half = shards[k].shape[0] // 2
            mine = pl.ds(pl.multiple_of(c * half, 16), half)
            local = pltpu.make_async_copy(ins[k], outs[k].at[me], local_sems.at[k])
            local.start()
            local_copies.append(local)
            for j, (px, py) in enumerate(chips):
                cp = _remote(ins[k].at[mine, :], outs[k].at[me, mine, :], send_sems, recv_sems, 6 * k + j, (px, py, c))
                cp.start()
                started.append(cp)
        for k in range(n):
            half = shards[k].shape[0] // 2
            mine = pl.ds(pl.multiple_of(c * half, 16), half)
            for j, (px, py) in enumerate(chips):
                blk = outs[k].at[2 * px + py, mine, :]
                _remote(blk, blk, send_sems, recv_sems, 6 * k + j, (px, py, c)).wait_recv()
                fwd = _remote(blk, blk, send_sems, recv_sems, 6 * k + 3 + j, sibling)
                fwd.start()
                started.append(fwd)
        for k in range(n):
            half = shards[k].shape[0] // 2
            theirs = pl.ds(pl.multiple_of((1 - c) * half, 16), half)
            for j, (px, py) in enumerate(chips):
                blk = outs[k].at[2 * px + py, theirs, :]
                _remote(blk, blk, send_sems, recv_sems, 6 * k + 3 + j, sibling).wait_recv()
        for cp in started:
            cp.wait_send()
        for cp in local_copies:
            cp.wait()

    hbm = BS(memory_space=pl.ANY)
    return pl.pallas_call(
        body, out_shape=tuple(SDS((N_SHARD,) + s.shape, s.dtype) for s in shards),
        in_specs=[hbm] * n, out_specs=(hbm,) * n,
        scratch_shapes=[pltpu.SemaphoreType.DMA((6 * n,)), pltpu.SemaphoreType.DMA((6 * n,)), pltpu.SemaphoreType.DMA((n,))],
        name="allgather_weights", compiler_params=_params())(*shards)


def _sibling_swap_halves(grads):
    n = len(grads)

    def body(*refs):
        ins, outs = refs[:n], refs[n:2 * n]
        send_sems, recv_sems = refs[2 * n:]
        x, y, c, _ = _mesh_place()
        sibling = (x, y, 1 - c)
        copies = []
        for k in range(n):
            half = grads[k].shape[1] // 2
            theirs = pl.ds(pl.multiple_of((1 - c) * half, 16), half)
            cp = _remote(ins[k].at[:, theirs, :], outs[k], send_sems, recv_sems, k, sibling)
            cp.start()
            copies.append(cp)
        for cp in copies:
            cp.wait_recv()
        for cp in copies:
            cp.wait_send()

    hbm = BS(memory_space=pl.ANY)
    return pl.pallas_call(
        body, out_shape=tuple(SDS((g.shape[0], g.shape[1] // 2, g.shape[2]), g.dtype) for g in grads),
        in_specs=[hbm] * n, out_specs=(hbm,) * n,
        scratch_shapes=[pltpu.SemaphoreType.DMA((n,)), pltpu.SemaphoreType.DMA((n,))],
        name="reduce_sibling_send", compiler_params=_params())(*grads)


def _row_tile(rows, cap=512):
    return max(r for r in range(16, cap + 1, 16) if rows % r == 0)


def _chip_presum(k, grad, got, c_idx):
    n_sh, rows, cols = grad.shape
    half = rows // 2
    tr = _row_tile(half)
    grad4 = grad.reshape(n_sh, 2, half, cols)

    def body(c_ref, a_ref, b_ref, o_ref):
        o_ref[...] = (a_ref[...].astype(F32) + b_ref[...].astype(F32)).astype(o_ref.dtype)

    return pl.pallas_call(
        body, out_shape=SDS((n_sh, half, cols), BF16),
        grid_spec=pltpu.PrefetchScalarGridSpec(
            num_scalar_prefetch=1, grid=(n_sh, half // tr),
            in_specs=[BS((None, None, tr, cols), lambda s, i, c_ref: (s, c_ref[0], i, 0)),
                      BS((None, tr, cols), lambda s, i, c_ref: (s, i, 0))],
            out_specs=BS((None, tr, cols), lambda s, i, c_ref: (s, i, 0))),
        name=f"reduce_presum{k}", compiler_params=_params("parallel", "parallel"))(c_idx, grad4, got)


def _chip_exchange(parts):
    n = len(parts)

    def body(*refs):
        ins, outs = refs[:n], refs[n:2 * n]
        send_sems, recv_sems = refs[2 * n:]
        x, y, c, chips = _mesh_place()
        copies = []
        for k in range(n):
            for j, (px, py) in enumerate(chips):
                cp = _remote(ins[k].at[2 * px + py], outs[k].at[j], send_sems, recv_sems, 3 * k + j, (px, py, c))
                cp.start()
                copies.append(cp)
        for cp in copies:
            cp.wait_recv()
        for cp in copies:
            cp.wait_send()

    hbm = BS(memory_space=pl.ANY)
    return pl.pallas_call(
        body, out_shape=tuple(SDS((3,) + p.shape[1:], p.dtype) for p in parts),
        in_specs=[hbm] * n, out_specs=(hbm,) * n,
        scratch_shapes=[pltpu.SemaphoreType.DMA((3 * n,)), pltpu.SemaphoreType.DMA((3 * n,))],
        name="reduce_chip_exchange", compiler_params=_params())(*parts)


def _chip_sum(k, part, got, me_idx):
    _, half, cols = part.shape
    tr = _row_tile(half)

    def body(me_ref, a_ref, b_ref, o_ref):
        acc = a_ref[...].astype(F32)
        for j in range(3):
            acc = acc + b_ref[j].astype(F32)
        o_ref[...] = acc

    return pl.pallas_call(
        body, out_shape=SDS((half, cols), F32),
        grid_spec=pltpu.PrefetchScalarGridSpec(
            num_scalar_prefetch=1, grid=(half // tr,),
            in_specs=[BS((None, tr, cols), lambda i, me_ref: (me_ref[0], i, 0)),
                      BS((3, tr, cols), lambda i, me_ref: (0, i, 0))],
            out_specs=BS((tr, cols), lambda i, me_ref: (i, 0))),
        name=f"reduce_sum{k}", compiler_params=_params("parallel"))(me_idx, part, got)


def _sibling_join_halves(halves):
    n = len(halves)

    def body(*refs):
        ins, outs = refs[:n], refs[n:2 * n]
        send_sems, recv_sems, local_sems = refs[2 * n:]
        x, y, c, _ = _mesh_place()
        sibling = (x, y, 1 - c)
        remote, local = [], []
        for k in range(n):
            half = halves[k].shape[0]
            mine = pl.ds(pl.multiple_of(c * half, 8), half)
            lc = pltpu.make_async_copy(ins[k], outs[k].at[mine, :], local_sems.at[k])
            lc.start()
            local.append(lc)
            cp = _remote(ins[k], outs[k].at[mine, :], send_sems, recv_sems, k, sibling)
            cp.start()
            remote.append(cp)
        for k in range(n):
            half = halves[k].shape[0]
            theirs = outs[k].at[pl.ds(pl.multiple_of((1 - c) * half, 8), half), :]
            _remote(theirs, theirs, send_sems, recv_sems, k, sibling).wait_recv()
        for cp in remote:
            cp.wait_send()
        for lc in local:
            lc.wait()

    hbm = BS(memory_space=pl.ANY)
    return pl.pallas_call(
        body, out_shape=tuple(SDS((2 * h.shape[0], h.shape[1]), h.dtype) for h in halves),
        in_specs=[hbm] * n, out_specs=(hbm,) * n,
        scratch_shapes=[pltpu.SemaphoreType.DMA((n,)), pltpu.SemaphoreType.DMA((n,)), pltpu.SemaphoreType.DMA((n,))],
        name="reduce_sibling_join", compiler_params=_params())(*halves)


N_DEV = 8


def _allreduce_small(part):
    rows, lanes = part.shape

    def body(x_ref, out_ref, all_ref, send_sems, recv_sems, local_sem):
        x, y, c, chips = _mesh_place()
        me, sibling = (x, y, c), (x, y, 1 - c)

        def blk(px, py, pc):
            return all_ref.at[pl.ds(pl.multiple_of((4 * px + 2 * py + pc) * rows, 8), rows), :]

        def copy(k, block, to, src=None):
            return _remote(blk(*block) if src is None else src, blk(*block), send_sems, recv_sems, k, to)

        mine = pltpu.make_async_copy(x_ref, blk(*me), local_sem)
        mine.start()
        first = [copy(0, me, sibling, src=x_ref)]
        first += [copy(1 + j, me, (*chip, c), src=x_ref) for j, chip in enumerate(chips)]
        for cp in first:
            cp.start()
        passed = [copy(4 + j, (*chip, c), sibling) for j, chip in enumerate(chips)]
        for j, chip in enumerate(chips):
            copy(1 + j, (*chip, c), me).wait_recv()
            passed[j].start()
        copy(0, sibling, me).wait_recv()
        for j, chip in enumerate(chips):
            copy(4 + j, (*chip, 1 - c), me).wait_recv()
        for cp in first + passed:
            cp.wait_send()
        mine.wait()
        acc = all_ref[pl.ds(0, rows), :]
        for dev in range(1, N_DEV):
            acc = acc + all_ref[pl.ds(dev * rows, rows), :]
        out_ref[...] = acc

    vm = BS(memory_space=pltpu.VMEM)
    return pl.pallas_call(
        body, out_shape=SDS((rows, lanes), F32), in_specs=[vm], out_specs=vm,
        scratch_shapes=[pltpu.VMEM((N_DEV * rows, lanes), F32), pltpu.SemaphoreType.DMA((7,)),
                        pltpu.SemaphoreType.DMA((7,)), pltpu.SemaphoreType.DMA],
        name="allreduce_small", compiler_params=_params())(part)


def _adamw(name, w, grad, row0, m, v):
    rows, cols = w.shape
    tr = rows if rows < 16 else _row_tile(rows, 256)
    bc1 = 1.0 - ADAM_B1 ** ADAM_STEP
    bc2 = 1.0 - ADAM_B2 ** ADAM_STEP

    def body(w_ref, g_ref, m_ref, v_ref, go_ref, d_ref, mo_ref, vo_ref):
        g = g_ref[...]
        m_new = ADAM_B1 * m_ref[...] + (1.0 - ADAM_B1) * g
        v_new = ADAM_B2 * v_ref[...] + (1.0 - ADAM_B2) * (g * g)
        go_ref[...] = g
        mo_ref[...] = m_new
        vo_ref[...] = v_new
        d_ref[...] = -ADAM_LR * ((m_new / bc1) / (jnp.sqrt(v_new / bc2) + ADAM_EPS) + ADAM_WD * w_ref[...])

    blk = BS((tr, cols), lambda i: (i, 0))
    shape = SDS((rows, cols), F32)
    return pl.pallas_call(
        body, out_shape=(shape,) * 4, grid=(rows // tr,),
        in_specs=[blk, BS((tr, cols), lambda i: (row0 // tr + i, 0)), blk, blk], out_specs=(blk,) * 4,
        name=name, compiler_params=_params("parallel"))(w, grad, m, v)


SMALL_LANES = 128


def _pack_small(parts):
    flat = jnp.concatenate([jnp.ravel(p) for p in parts])
    rows = -(-flat.shape[0] // (64 * SMALL_LANES)) * 64
    return jnp.pad(flat, (0, rows * SMALL_LANES - flat.shape[0])).reshape(rows, SMALL_LANES)


def _unpack_small(packed, like):
    flat = jnp.ravel(packed)
    out, at = [], 0
    for p in like:
        out.append(flat[at:at + p.size].reshape(p.shape))
        at += p.size
    return out


def kernel(x, mem, ffn1_norm, ffn1_w_gate, ffn1_w_up, ffn1_w_down, mix_norm, w_in, pool_w, pool_scale, w_pool_proj, ssm_a_re, ssm_a_im, ssm_log_dt, ssm_b_re, ssm_b_im, ssm_c_re, ssm_c_im, ssm_d, w_glu_val, w_glu_gate, w_mix_out, xattn_norm, mem_norm, w_q, w_kv, w_xo, ffn2_norm, ffn2_w_gate, ffn2_w_up, ffn2_w_down, final_norm, loss_target, m_ffn1_norm, m_ffn1_w_gate, m_ffn1_w_up, m_ffn1_w_down, m_mix_norm, m_w_in, m_pool_w, m_pool_scale, m_w_pool_proj, m_ssm_a_re, m_ssm_a_im, m_ssm_log_dt, m_ssm_b_re, m_ssm_b_im, m_ssm_c_re, m_ssm_c_im, m_ssm_d, m_w_glu_val, m_w_glu_gate, m_w_mix_out, m_xattn_norm, m_mem_norm, m_w_q, m_w_kv, m_w_xo, m_ffn2_norm, m_ffn2_w_gate, m_ffn2_w_up, m_ffn2_w_down, m_final_norm, v_ffn1_norm, v_ffn1_w_gate, v_ffn1_w_up, v_ffn1_w_down, v_mix_norm, v_w_in, v_pool_w, v_pool_scale, v_w_pool_proj, v_ssm_a_re, v_ssm_a_im, v_ssm_log_dt, v_ssm_b_re, v_ssm_b_im, v_ssm_c_re, v_ssm_c_im, v_ssm_d, v_w_glu_val, v_w_glu_gate, v_w_mix_out, v_xattn_norm, v_mem_norm, v_w_q, v_w_kv, v_w_xo, v_ffn2_norm, v_ffn2_w_gate, v_ffn2_w_up, v_ffn2_w_down, v_final_norm):
    given = dict(locals())
    w = {n: given[n] for n in WEIGHTS}
    m = {n: given["m_" + n] for n in WEIGHTS}
    v = {n: given["v_" + n] for n in WEIGHTS}

    shards = [jnp.concatenate([w[n][0].astype(BF16) for n in cl], axis=0) for cl in CLASSES]
    gathered = _allgather_shards(shards)
    gw = tuple(gth.reshape((N_SHARD, 1 if cl is CLASS_E else len(cl), -1, gth.shape[-1]))
               for cl, gth in zip(CLASSES, gathered))

    loss_part, grad_x, big, small = _device_step(x[0], mem[0], loss_target[0], gw, {n: w[n] for n in SMALL})
    loss = lax.psum(loss_part[0, 0], ("x", "y", "c"))

    c_idx = lax.axis_index("c").astype(jnp.int32).reshape(1)
    me_idx = (2 * lax.axis_index("x") + lax.axis_index("y")).astype(jnp.int32).reshape(1)
    from_sibling = _sibling_swap_halves(big)
    chip_parts = [_chip_presum(k, big[k], from_sibling[k], c_idx) for k in range(len(CLASSES))]
    from_chips = _chip_exchange(chip_parts)
    halves = [_chip_sum(k, chip_parts[k], from_chips[k], me_idx) for k in range(len(CLASSES))]
    reduced = _sibling_join_halves(halves)

    grads, delta, new_m, new_v = {}, {}, {}, {}
    for cl, red in zip(CLASSES, reduced):
        row0 = 0
        for n in cl:
            shape = w[n].shape
            two_d = shape[1:]
            grads[n], delta[n], new_m[n], new_v[n] = (
                o.reshape(shape) for o in _adamw("adamw_" + n, w[n].reshape(two_d), red, row0, m[n].reshape(two_d), v[n].reshape(two_d)))
            row0 += two_d[0]

    small_like = [w[n] for n in SMALL]
    g_small = _allreduce_small(_pack_small([small[n] for n in SMALL]))
    packed = _adamw("adamw_small", _pack_small(small_like), g_small, 0,
                    _pack_small([m[n] for n in SMALL]), _pack_small([v[n] for n in SMALL]))
    for out, store in zip(packed, (grads, delta, new_m, new_v)):
        for n, val in zip(SMALL, _unpack_small(out, small_like)):
            store[n] = val

    return (loss, grad_x[None], *[grads[n] for n in WEIGHTS], *[delta[n] for n in WEIGHTS],
            *[new_m[n] for n in WEIGHTS], *[new_v[n] for n in WEIGHTS])
```

```python
import functools
import math

import jax
import jax.numpy as jnp
from jax import lax
from jax.experimental import pallas as pl
from jax.experimental.pallas import tpu as pltpu

F32 = jnp.float32
BF16 = jnp.bfloat16
SDS = jax.ShapeDtypeStruct
BS = pl.BlockSpec
MESH = pl.DeviceIdType.MESH

D_MODEL = 1024
D_FF = 2816
N_SHARD = 4
FF_SH = D_FF // N_SHARD
D_POOL = 512
POOL_WINDOWS = (2, 4, 8, 16)
POOL_GROUP = 128
D_SSM = 256
SSM_GROUPS = 16
SSM_GROUP = 16
SSM_STATE = 64
SSM_CH = SSM_GROUPS * SSM_STATE
N_HEADS = 4
HEAD_DIM = 256
EPS = 1e-6
ADAM_LR, ADAM_B1, ADAM_B2, ADAM_EPS, ADAM_WD, ADAM_STEP = 0.001, 0.9, 0.999, 1e-08, 0.01, 10

VMEM_LIMIT_V7X = 52 * 1024 * 1024
TM = 512

NN = (((1,), (0,)), ((), ()))
NT = (((1,), (1,)), ((), ()))
TN = (((0,), (0,)), ((), ()))


def _params(*sem):
    return pltpu.CompilerParams(dimension_semantics=sem if sem else None, vmem_limit_bytes=VMEM_LIMIT_V7X)


def _dot(a, b, dims=NN):
    return lax.dot_general(a.astype(BF16), b.astype(BF16), dims, preferred_element_type=F32)


def _sigmoid(v):
    return 1.0 / (1.0 + jnp.exp(-v))


def _block_dims(spec):
    return tuple(d for d in spec.block_shape if d is not None)


def _mm(name, pairs, *, grid, out_shape, out_spec, red_axis=None, extras=(), epilogue=None):
    n_pairs, n_extra = len(pairs), len(extras)
    n_red = grid[red_axis] if red_axis is not None else 1
    dims = [p[4] for p in pairs]

    def body(*refs):
        ab = refs[:2 * n_pairs]
        ex = refs[2 * n_pairs:2 * n_pairs + n_extra]
        o_ref = refs[2 * n_pairs + n_extra]

        def partial():
            acc = None
            for p in range(n_pairs):
                t = _dot(ab[2 * p][...], ab[2 * p + 1][...], dims[p])
                acc = t if acc is None else acc + t
            return acc

        def finish(acc):
            res = epilogue(acc, *[e[...] for e in ex]) if epilogue is not None else acc
            o_ref[...] = res.astype(o_ref.dtype)

        if n_red == 1:
            finish(partial())
        else:
            acc_ref = refs[-1]
            k = pl.program_id(red_axis)

            @pl.when(k == 0)
            def _():
                acc_ref[...] = jnp.zeros_like(acc_ref)

            acc_ref[...] += partial()

            @pl.when(k == n_red - 1)
            def _():
                finish(acc_ref[...])

    operands, in_specs = [], []
    for a, a_spec, b, b_spec, _ in pairs:
        operands += [a, b]
        in_specs += [a_spec, b_spec]
    for e, e_spec in extras:
        operands.append(e)
        in_specs.append(e_spec)
    scratch = [pltpu.VMEM(_block_dims(out_spec), F32)] if n_red > 1 else []
    sem = tuple("arbitrary" if ax == red_axis else "parallel" for ax in range(len(grid)))
    return pl.pallas_call(body, out_shape=out_shape, grid=grid, in_specs=in_specs, out_specs=out_spec,
                          scratch_shapes=scratch, name=name, compiler_params=_params(*sem))(*operands)


def _rmsnorm(name, h, gain, tm):
    t, d = h.shape

    def body(h_ref, g_ref, u_ref):
        hv = h_ref[...]
        r = lax.rsqrt(jnp.mean(hv * hv, axis=-1, keepdims=True) + EPS)
        u_ref[...] = ((hv * r) * g_ref[...]).astype(u_ref.dtype)

    return pl.pallas_call(
        body, out_shape=SDS((t, d), BF16), grid=(t // tm,),
        in_specs=[BS((tm, d), lambda i: (i, 0)), BS((1, d), lambda i: (0, 0))],
        out_specs=BS((tm, d), lambda i: (i, 0)), name=name, compiler_params=_params("parallel"))(h, gain)


def _rmsnorm_bwd(name, h, gain, du, dh_in, tm):
    t, d = h.shape
    has_in = dh_in is not None

    def body(*refs):
        if has_in:
            h_ref, g_ref, du_ref, dhin_ref, dh_ref, dhb_ref, dg_ref = refs
        else:
            h_ref, g_ref, du_ref, dh_ref, dhb_ref, dg_ref = refs
        i = pl.program_id(0)
        hv = h_ref[...]
        r = lax.rsqrt(jnp.mean(hv * hv, axis=-1, keepdims=True) + EPS)
        n = hv * r
        duv = du_ref[...].astype(F32)
        dn = duv * g_ref[...]
        dh = r * (dn - n * jnp.mean(dn * n, axis=-1, keepdims=True))
        if has_in:
            dh = dhin_ref[...] + dh
        dh_ref[...] = dh
        dhb_ref[...] = dh.astype(BF16)

        @pl.when(i == 0)
        def _():
            dg_ref[...] = jnp.zeros_like(dg_ref)

        dg_ref[...] += jnp.sum(duv * n, axis=0, keepdims=True)

    row = BS((tm, d), lambda i: (i, 0))
    vec = BS((1, d), lambda i: (0, 0))
    operands = [h, gain, du] + ([dh_in] if has_in else [])
    in_specs = [row, vec, row] + ([row] if has_in else [])
    return pl.pallas_call(
        body, out_shape=(SDS((t, d), F32), SDS((t, d), BF16), SDS((1, d), F32)), grid=(t // tm,),
        in_specs=in_specs, out_specs=(row, row, vec), name=name, compiler_params=_params("arbitrary"))(*operands)


def _loss_head(h, gain, target, tm):
    t, d = h.shape

    def body(h_ref, g_ref, t_ref, loss_ref, dh_ref, dhb_ref, dg_ref):
        i = pl.program_id(0)
        hv = h_ref[...]
        g = g_ref[...]
        r = lax.rsqrt(jnp.mean(hv * hv, axis=-1, keepdims=True) + EPS)
        n = hv * r
        err = n * g - t_ref[...]
        dy = err * (1.0 / d)
        dn = dy * g
        dh = r * (dn - n * jnp.mean(dn * n, axis=-1, keepdims=True))
        dh_ref[...] = dh
        dhb_ref[...] = dh.astype(BF16)

        @pl.when(i == 0)
        def _():
            dg_ref[...] = jnp.zeros_like(dg_ref)
            loss_ref[...] = jnp.zeros_like(loss_ref)

        dg_ref[...] += jnp.sum(dy * n, axis=0, keepdims=True)
        part = 0.5 * jnp.sum(jnp.mean(err * err, axis=-1, keepdims=True), axis=0, keepdims=True)
        loss_ref[...] += jnp.broadcast_to(part, loss_ref.shape)

    row = BS((tm, d), lambda i: (i, 0))
    vec = BS((1, d), lambda i: (0, 0))
    return pl.pallas_call(
        body, out_shape=(SDS((1, 128), F32), SDS((t, d), F32), SDS((t, d), BF16), SDS((1, d), F32)),
        grid=(t // tm,), in_specs=[row, vec, row],
        out_specs=(BS((1, 128), lambda i: (0, 0)), row, row, vec),
        name="loss_head", compiler_params=_params("arbitrary"))(h, gain, target)


def _ffn_up(name, u, w_a, i_gate, i_up, tm):
    t, d = u.shape

    def body(u_ref, wg_ref, wu_ref, g_ref, up_ref, a_ref):
        uv = u_ref[...]
        g = _dot(uv, wg_ref[...])
        up = _dot(uv, wu_ref[...])
        g_ref[...] = g.astype(BF16)
        up_ref[...] = up.astype(BF16)
        a_ref[...] = (g * _sigmoid(g) * up).astype(BF16)

    hid = BS((None, tm, FF_SH), lambda s, i: (s, i, 0))
    shape = SDS((N_SHARD, t, FF_SH), BF16)
    return pl.pallas_call(
        body, out_shape=(shape, shape, shape), grid=(N_SHARD, t // tm),
        in_specs=[BS((tm, d), lambda s, i: (i, 0)),
                  BS((None, None, d, FF_SH), lambda s, i: (s, i_gate, 0, 0)),
                  BS((None, None, d, FF_SH), lambda s, i: (s, i_up, 0, 0))],
        out_specs=(hid, hid, hid), name=name, compiler_params=_params("parallel", "parallel"))(u, w_a, w_a)


def _ffn_down(name, a, w_b, i_down, resid, tm):
    t, d = resid.shape
    return _mm(name, [(a, BS((None, tm, FF_SH), lambda i, s: (s, i, 0)),
                       w_b, BS((None, None, FF_SH, d), lambda i, s: (s, i_down, 0, 0)), NN)],
               grid=(t // tm, N_SHARD), red_axis=1, out_shape=SDS((t, d), F32),
               out_spec=BS((tm, d), lambda i, s: (i, 0)),
               extras=[(resid, BS((tm, d), lambda i, s: (i, 0)))],
               epilogue=lambda acc, res: res + 0.5 * acc)


def _ffn_bwd_act(name, dh_b, w_b, i_down, g, up, tm):
    t, d = dh_b.shape

    def body(dh_ref, wd_ref, g_ref, up_ref, dg_ref, dup_ref):
        da = 0.5 * _dot(dh_ref[...], wd_ref[...], NT)
        gv = g_ref[...].astype(F32)
        uv = up_ref[...].astype(F32)
        sg = _sigmoid(gv)
        silu = gv * sg
        dg_ref[...] = (da * uv * (sg + silu * (1.0 - sg))).astype(BF16)
        dup_ref[...] = (da * silu).astype(BF16)

    hid = BS((None, tm, FF_SH), lambda s, i: (s, i, 0))
    shape = SDS((N_SHARD, t, FF_SH), BF16)
    return pl.pallas_call(
        body, out_shape=(shape, shape), grid=(N_SHARD, t // tm),
        in_specs=[BS((tm, d), lambda s, i: (i, 0)),
                  BS((None, None, FF_SH, d), lambda s, i: (s, i_down, 0, 0)), hid, hid],
        out_specs=(hid, hid), name=name, compiler_params=_params("parallel", "parallel"))(dh_b, w_b, g, up)


def _ffn_dw_up(name, u, dg, dup, tm):
    t, d = u.shape
    n_t = t // tm

    def body(u_ref, dg_ref, dup_ref, wg_ref, wu_ref, accg, accu):
        i = pl.program_id(1)

        @pl.when(i == 0)
        def _():
            accg[...] = jnp.zeros_like(accg)
            accu[...] = jnp.zeros_like(accu)

        uv = u_ref[...]
        accg[...] += _dot(uv, dg_ref[...], TN)
        accu[...] += _dot(uv, dup_ref[...], TN)

        @pl.when(i == n_t - 1)
        def _():
            wg_ref[...] = accg[...].astype(BF16)
            wu_ref[...] = accu[...].astype(BF16)

    hid = BS((None, tm, FF_SH), lambda s, i: (s, i, 0))
    out = BS((None, d, FF_SH), lambda s, i: (s, 0, 0))
    shape = SDS((N_SHARD, d, FF_SH), BF16)
    return pl.pallas_call(
        body, out_shape=(shape, shape), grid=(N_SHARD, n_t),
        in_specs=[BS((tm, d), lambda s, i: (i, 0)), hid, hid], out_specs=(out, out),
        scratch_shapes=[pltpu.VMEM((d, FF_SH), F32), pltpu.VMEM((d, FF_SH), F32)],
        name=name, compiler_params=_params("parallel", "arbitrary"))(u, dg, dup)


def _ffn_dw_down(name, a, dh_b, tm):
    t, d = dh_b.shape
    return _mm(name, [(a, BS((None, tm, FF_SH), lambda s, i: (s, i, 0)), dh_b, BS((tm, d), lambda s, i: (i, 0)), TN)],
               grid=(N_SHARD, t // tm), red_axis=1, out_shape=SDS((N_SHARD, FF_SH, d), BF16),
               out_spec=BS((None, FF_SH, d), lambda s, i: (s, 0, 0)), epilogue=lambda acc: 0.5 * acc)


def _ffn_dx(name, dg, dup, w_a, i_gate, i_up, tm):
    t = dg.shape[1]
    d = w_a.shape[2]
    hid = BS((None, tm, FF_SH), lambda i, s: (s, i, 0))
    return _mm(name, [(dg, hid, w_a, BS((None, None, d, FF_SH), lambda i, s: (s, i_gate, 0, 0)), NT),
                      (dup, hid, w_a, BS((None, None, d, FF_SH), lambda i, s: (s, i_up, 0, 0)), NT)],
               grid=(t // tm, N_SHARD), red_axis=1, out_shape=SDS((t, d), F32),
               out_spec=BS((tm, d), lambda i, s: (i, 0)))


def _plain_mm(name, a, b, dims, out_dtype, tm, resid=None):
    t = a.shape[0]
    n = b.shape[1] if dims == NN else b.shape[0]
    extras = [(resid, BS((tm, n), lambda i: (i, 0)))] if resid is not None else []
    epi = (lambda acc, res: res + acc) if resid is not None else None
    return _mm(name, [(a, BS((tm, a.shape[1]), lambda i: (i, 0)), b, BS(b.shape, lambda i: (0, 0)), dims)],
               grid=(t // tm,), out_shape=SDS((t, n), out_dtype), out_spec=BS((tm, n), lambda i: (i, 0)),
               extras=extras, epilogue=epi)


def _dw_mm(name, a, b, tm, out_dtype=BF16):
    t, k = a.shape
    n = b.shape[1]
    return _mm(name, [(a, BS((tm, k), lambda i: (i, 0)), b, BS((tm, n), lambda i: (i, 0)), TN)],
               grid=(t // tm,), red_axis=0, out_shape=SDS((k, n), out_dtype), out_spec=BS((k, n), lambda i: (0, 0)))


POOL_CHUNK = 256
POOL_HALO = 8


def _window_sum(v, width, lead):
    n = v.shape[0]
    s = v
    k = 1
    while k < width:
        s = s + pltpu.roll(s, n - k, 0)
        k *= 2
    return pltpu.roll(s, lead, 0) if lead else s


def _pool_count(base, left, right, t, shape):
    pos = base + lax.broadcasted_iota(jnp.int32, shape, 0)
    lo = jnp.maximum(pos - left, 0)
    hi = jnp.minimum(pos + right + 1, t)
    return (hi - lo).astype(F32)


def _pool_fwd(proj, pool_w, pool_scale):
    t = proj.shape[0]
    c, h = POOL_CHUNK, POOL_HALO
    n_chunks = t // c

    def body(proj_hbm, pw_ref, sc_ref, pooled_ref, mixed_ref, ms_ref, pad_ref, sem):
        cp = pltpu.make_async_copy(proj_hbm.at[:, pl.ds(0, D_POOL)], pad_ref.at[pl.ds(h, t), :], sem)
        cp.start()
        pad_ref[pl.ds(0, h), :] = jnp.zeros((h, D_POOL), F32)
        pad_ref[pl.ds(t + h, h), :] = jnp.zeros((h, D_POOL), F32)
        cp.wait()
        for g, width in enumerate(POOL_WINDOWS):
            left = width // 2
            right = width - 1 - left
            cols = slice(g * POOL_GROUP, (g + 1) * POOL_GROUP)
            wmat = pw_ref[g].astype(BF16)
            scale = sc_ref[:, cols]

            def chunk(ci, carry, left=left, right=right, width=width, cols=cols, wmat=wmat, scale=scale):
                base = pl.multiple_of(ci * c, c)
                v = pad_ref[pl.ds(base, c + 2 * h), cols]
                win = _window_sum(v, width, left)[h:h + c]
                cnt = _pool_count(base, left, right, t, (c, POOL_GROUP))
                pooled = (win / cnt - v[h:h + c]).astype(BF16)
                mixed = _dot(pooled, wmat)
                pooled_ref[pl.ds(base, c), cols] = pooled
                mixed_ref[pl.ds(base, c), cols] = mixed.astype(BF16)
                ms_ref[pl.ds(base, c), cols] = (mixed * scale).astype(BF16)
                return carry

            lax.fori_loop(0, n_chunks, chunk, 0)

    vm = BS(memory_space=pltpu.VMEM)
    shape = SDS((t, D_POOL), BF16)
    return pl.pallas_call(
        body, out_shape=(shape, shape, shape),
        in_specs=[BS(memory_space=pl.ANY), vm, vm], out_specs=(vm, vm, vm),
        scratch_shapes=[pltpu.VMEM((t + 2 * h, D_POOL), F32), pltpu.SemaphoreType.DMA],
        name="pool_fwd", compiler_params=_params())(proj, pool_w, pool_scale)


def _pool_bwd(d_ms, mixed, pooled, pool_w, pool_scale):
    t = d_ms.shape[0]
    c, h = POOL_CHUNK, POOL_HALO
    n_chunks = t // c

    def body(dms_ref, mixed_ref, pooled_ref, pw_ref, sc_ref, dp_ref, dsc_ref, dpw_ref, pad_ref):
        pad_ref[pl.ds(0, h), :] = jnp.zeros((h, D_POOL), F32)
        pad_ref[pl.ds(t + h, h), :] = jnp.zeros((h, D_POOL), F32)
        for g, width in enumerate(POOL_WINDOWS):
            left = width // 2
            right = width - 1 - left
            cols = slice(g * POOL_GROUP, (g + 1) * POOL_GROUP)
            wmat = pw_ref[g].astype(BF16)
            scale = sc_ref[:, cols]

            def first(ci, carry, left=left, right=right, cols=cols, wmat=wmat, scale=scale):
                dsc, dpw = carry
                base = pl.multiple_of(ci * c, c)
                dms = dms_ref[pl.ds(base, c), cols].astype(F32)
                dsc = dsc + jnp.sum(dms * mixed_ref[pl.ds(base, c), cols].astype(F32), axis=0, keepdims=True)
                dmix = (dms * scale).astype(BF16)
                dpw = dpw + _dot(pooled_ref[pl.ds(base, c), cols], dmix, TN)
                dpooled = _dot(dmix, wmat, NT)
                cnt = _pool_count(base, left, right, t, (c, POOL_GROUP))
                pad_ref[pl.ds(base + h, c), cols] = dpooled / cnt
                return dsc, dpw

            dsc, dpw = lax.fori_loop(0, n_chunks, first,
                                     (jnp.zeros((1, POOL_GROUP), F32), jnp.zeros((POOL_GROUP, POOL_GROUP), F32)))
            dsc_ref[:, cols] = dsc
            dpw_ref[g] = dpw

            def second(ci, carry, left=left, right=right, width=width, cols=cols):
                base = pl.multiple_of(ci * c, c)
                v = pad_ref[pl.ds(base, c + 2 * h), cols]
                win = _window_sum(v, width, right)[h:h + c]
                cnt = _pool_count(base, left, right, t, (c, POOL_GROUP))
                dp_ref[pl.ds(base, c), cols] = (win - v[h:h + c] * cnt).astype(BF16)
                return carry

            lax.fori_loop(0, n_chunks, second, 0)

    vm = BS(memory_space=pltpu.VMEM)
    return pl.pallas_call(
        body, out_shape=(SDS((t, D_POOL), BF16), SDS((1, D_POOL), F32), SDS((4, POOL_GROUP, POOL_GROUP), F32)),
        in_specs=[vm] * 5, out_specs=(vm, vm, vm),
        scratch_shapes=[pltpu.VMEM((t + 2 * h, D_POOL), F32)],
        name="pool_bwd", compiler_params=_params())(d_ms, mixed, pooled, pool_w, pool_scale)


def _ssm_disc(ar, ai, ldt):
    def body(ar_ref, ai_ref, ldt_ref, abr_ref, abi_ref, qr_ref, qi_ref):
        a_r, a_i = ar_ref[...], ai_ref[...]
        dt = jnp.exp(ldt_ref[...])
        mag = jnp.exp(dt * a_r)
        ang = dt * a_i
        abr = mag * jnp.cos(ang)
        abi = mag * jnp.sin(ang)
        den = a_r * a_r + a_i * a_i
        nr = abr - 1.0
        abr_ref[...] = abr
        abi_ref[...] = abi
        qr_ref[...] = (nr * a_r + abi * a_i) / den
        qi_ref[...] = (abi * a_r - nr * a_i) / den

    vm = BS(memory_space=pltpu.VMEM)
    shape = SDS(ar.shape, F32)
    return pl.pallas_call(body, out_shape=(shape,) * 4, in_specs=[vm] * 3, out_specs=(vm,) * 4,
                          name="ssm_disc", compiler_params=_params())(ar, ai, ldt)


def _ssm_disc_bwd(ar, ai, ldt, d_abr, d_abi, d_qr, d_qi):
    def body(ar_ref, ai_ref, ldt_ref, gabr_ref, gabi_ref, gqr_ref, gqi_ref, dar_ref, dai_ref, dldt_ref):
        a_r, a_i = ar_ref[...], ai_ref[...]
        dt = jnp.exp(ldt_ref[...])
        mag = jnp.exp(dt * a_r)
        ang = dt * a_i
        cs, sn = jnp.cos(ang), jnp.sin(ang)
        abr, abi = mag * cs, mag * sn
        den = a_r * a_r + a_i * a_i
        nr = abr - 1.0
        qr = (nr * a_r + abi * a_i) / den
        qi = (abi * a_r - nr * a_i) / den
        gqr, gqi = gqr_ref[...], gqi_ref[...]
        g_nr_num = gqr / den
        g_ni_num = gqi / den
        g_den = -(gqr * qr + gqi * qi) / den
        g_nr = g_nr_num * a_r - g_ni_num * a_i
        g_abi = g_nr_num * a_i + g_ni_num * a_r
        d_ar = g_nr_num * nr + g_ni_num * abi + 2.0 * a_r * g_den
        d_ai = g_nr_num * abi - g_ni_num * nr + 2.0 * a_i * g_den
        g_abr = gabr_ref[...] + g_nr
        g_abi = gabi_ref[...] + g_abi
        g_mag = g_abr * cs + g_abi * sn
        g_ang = mag * (g_abi * cs - g_abr * sn)
        g_e = g_mag * mag
        d_ar = d_ar + g_e * dt
        d_ai = d_ai + g_ang * dt
        g_dt = g_e * a_r + g_ang * a_i
        dar_ref[...] = d_ar
        dai_ref[...] = d_ai
        dldt_ref[...] = jnp.sum(g_dt * dt, axis=1, keepdims=True)

    vm = BS(memory_space=pltpu.VMEM)
    return pl.pallas_call(body, out_shape=(SDS(ar.shape, F32), SDS(ar.shape, F32), SDS(ldt.shape, F32)),
                          in_specs=[vm] * 7, out_specs=(vm,) * 3, name="ssm_disc_bwd",
                          compiler_params=_params())(ar, ai, ldt, d_abr, d_abi, d_qr, d_qi)


def _ssm_bbar(qr, qi, br, bi):
    def body(qr_ref, qi_ref, br_ref, bi_ref, bbr_ref, bbi_ref):
        q_r, q_i, b_r, b_i = qr_ref[...], qi_ref[...], br_ref[...], bi_ref[...]
        bbr_ref[...] = q_r * b_r - q_i * b_i
        bbi_ref[...] = q_r * b_i + q_i * b_r

    vm = BS(memory_space=pltpu.VMEM)
    shape = SDS(br.shape, F32)
    return pl.pallas_call(body, out_shape=(shape, shape), in_specs=[vm] * 4, out_specs=(vm, vm),
                          name="ssm_bbar", compiler_params=_params())(qr, qi, br, bi)


def _ssm_bbar_bwd(qr, qi, br, bi, g_bbr, g_bbi):
    def body(qr_ref, qi_ref, br_ref, bi_ref, gr_ref, gi_ref, dqr_ref, dqi_ref, dbr_ref, dbi_ref):
        q_r, q_i, b_r, b_i = qr_ref[...], qi_ref[...], br_ref[...], bi_ref[...]
        g_r, g_i = gr_ref[...], gi_ref[...]
        dqr_ref[...] = jnp.sum(g_r * b_r + g_i * b_i, axis=1, keepdims=True)
        dqi_ref[...] = jnp.sum(g_i * b_r - g_r * b_i, axis=1, keepdims=True)
        dbr_ref[...] = g_r * q_r + g_i * q_i
        dbi_ref[...] = g_i * q_r - g_r * q_i

    vm = BS(memory_space=pltpu.VMEM)
    return pl.pallas_call(
        body, out_shape=(SDS(qr.shape, F32), SDS(qr.shape, F32), SDS(br.shape, F32), SDS(br.shape, F32)),
        in_specs=[vm] * 6, out_specs=(vm,) * 4, name="ssm_bbar_bwd",
        compiler_params=_params())(qr, qi, br, bi, g_bbr, g_bbi)


SCAN_ROWS = 256


def _ssm_scan(name, inp, w1, a_r, a_i, w2, reverse):
    t = inp.shape[0]
    rows = SCAN_ROWS
    n = t // rows
    n_groups = rows // 8
    ch = SSM_CH
    at = (lambda i: (n - 1 - i, 0)) if reverse else (lambda i: (i, 0))

    def body(in_ref, w1_ref, ar_ref, ai_ref, w2_ref, st_ref, out_ref, cr_ref, ci_ref, k_ref):
        i = pl.program_id(0)

        @pl.when(i == 0)
        def _():
            ar8 = jnp.broadcast_to(ar_ref[...], (8, ch))
            ai8 = jnp.broadcast_to(ai_ref[...], (8, ch))
            row = lax.broadcasted_iota(jnp.int32, (8, ch), 0)
            rank = (7 - row) if reverse else row
            powers = [(ar8, ai8)]
            for _ in range(7):
                p_r, p_i = powers[-1]
                powers.append((p_r * ar8 - p_i * ai8, p_r * ai8 + p_i * ar8))
            zero = jnp.zeros((8, ch), F32)
            for slot, k in enumerate((1, 2, 4)):
                k_ref[2 * slot] = jnp.where(rank >= k, powers[k - 1][0], zero)
                k_ref[2 * slot + 1] = jnp.where(rank >= k, powers[k - 1][1], zero)
            carry_r, carry_i = zero, zero
            for j in range(8):
                carry_r = jnp.where(rank == j, powers[j][0], carry_r)
                carry_i = jnp.where(rank == j, powers[j][1], carry_i)
            k_ref[6] = carry_r
            k_ref[7] = carry_i
            cr_ref[...] = zero
            ci_ref[...] = zero

        st_ref[...] = _dot(in_ref[...], w1_ref[...])

        def group(gi, carry):
            c_r, c_i = carry
            g = (n_groups - 1 - gi) if reverse else gi
            r0 = pl.multiple_of(g * 8, 8)
            x_r = st_ref[pl.ds(r0, 8), 0:ch]
            x_i = st_ref[pl.ds(r0, 8), ch:2 * ch]
            for slot, k in enumerate((1, 2, 4)):
                shift = (8 - k) if reverse else k
                s_r = pltpu.roll(x_r, shift, 0)
                s_i = pltpu.roll(x_i, shift, 0)
                m_r, m_i = k_ref[2 * slot], k_ref[2 * slot + 1]
                x_r, x_i = x_r + m_r * s_r - m_i * s_i, x_i + m_r * s_i + m_i * s_r
            p_r, p_i = k_ref[6], k_ref[7]
            x_r, x_i = x_r + p_r * c_r - p_i * c_i, x_i + p_r * c_i + p_i * c_r
            st_ref[pl.ds(r0, 8), 0:ch] = x_r
            st_ref[pl.ds(r0, 8), ch:2 * ch] = x_i
            last = 0 if reverse else 7
            return (jnp.broadcast_to(x_r[last:last + 1, :], (8, ch)), jnp.broadcast_to(x_i[last:last + 1, :], (8, ch)))

        c_r, c_i = lax.fori_loop(0, n_groups, group, (cr_ref[...], ci_ref[...]))
        cr_ref[...] = c_r
        ci_ref[...] = c_i
        out_ref[...] = _dot(st_ref[...], w2_ref[...])

    return pl.pallas_call(
        body, out_shape=(SDS((t, 2 * ch), F32), SDS((t, D_SSM), F32)), grid=(n,),
        in_specs=[BS((rows, D_SSM), at), BS((D_SSM, 2 * ch), lambda i: (0, 0)), BS((1, ch), lambda i: (0, 0)),
                  BS((1, ch), lambda i: (0, 0)), BS((2 * ch, D_SSM), lambda i: (0, 0))],
        out_specs=(BS((rows, 2 * ch), at), BS((rows, D_SSM), at)),
        scratch_shapes=[pltpu.VMEM((8, ch), F32), pltpu.VMEM((8, ch), F32), pltpu.VMEM((8, 8, ch), F32)],
        name=name, compiler_params=_params("arbitrary"))(inp, w1, a_r, a_i, w2)


DA_ROWS = 512


def _ssm_da(name, lam, states, reverse):
    t = lam.shape[0]
    rows = DA_ROWS
    n = t // rows
    nb = rows // 8
    ch = SSM_CH
    if reverse:
        halo_at = lambda i: (jnp.minimum((i + 1) * nb, t // 8 - 1), 0)
    else:
        halo_at = lambda i: (jnp.maximum(i * nb - 1, 0), 0)

    def body(lam_ref, x_ref, halo_ref, dr_ref, di_ref):
        i = pl.program_id(0)

        @pl.when(i == 0)
        def _():
            dr_ref[...] = jnp.zeros_like(dr_ref)
            di_ref[...] = jnp.zeros_like(di_ref)

        row = lax.broadcasted_iota(jnp.int32, (rows, ch), 0)
        if reverse:
            edge, shift, h_row, live = rows - 1, rows - 1, 0, i < n - 1
        else:
            edge, shift, h_row, live = 0, 1, 7, i > 0

        def neighbour(lo):
            halo = jnp.where(live, halo_ref[h_row:h_row + 1, lo:lo + ch], 0.0)
            return jnp.where(row == edge, jnp.broadcast_to(halo, (rows, ch)), pltpu.roll(x_ref[:, lo:lo + ch], shift, 0))

        xp_r, xp_i = neighbour(0), neighbour(ch)
        l_r, l_i = lam_ref[:, 0:ch], lam_ref[:, ch:2 * ch]
        dr_ref[...] += jnp.sum(l_r * xp_r + l_i * xp_i, axis=0, keepdims=True)
        di_ref[...] += jnp.sum(l_i * xp_r - l_r * xp_i, axis=0, keepdims=True)

    blk = BS((rows, 2 * ch), lambda i: (i, 0))
    vec = BS((1, ch), lambda i: (0, 0))
    return pl.pallas_call(
        body, out_shape=(SDS((1, ch), F32), SDS((1, ch), F32)), grid=(n,),
        in_specs=[blk, blk, BS((8, 2 * ch), halo_at)], out_specs=(vec, vec),
        name=name, compiler_params=_params("arbitrary"))(lam, states, states)


GELU_C = math.sqrt(2.0 / math.pi)
GELU_K = 0.044715


def _ssm_combine(proj, y_dirs, d_skip, tm):
    t = proj.shape[0]

    def body(s_ref, y_ref, d_ref, yt_ref, g_ref):
        y = s_ref[...] * d_ref[...] + y_ref[0] + y_ref[1]
        yt_ref[...] = y
        th = jnp.tanh(GELU_C * (y + GELU_K * y * y * y))
        g_ref[...] = (0.5 * y * (1.0 + th)).astype(BF16)

    blk = BS((tm, D_SSM), lambda i: (i, 0))
    return pl.pallas_call(
        body, out_shape=(SDS((t, D_SSM), F32), SDS((t, D_SSM), BF16)), grid=(t // tm,),
        in_specs=[BS((tm, D_SSM), lambda i: (i, D_POOL // D_SSM)), BS((2, tm, D_SSM), lambda i: (0, i, 0)),
                  BS((1, D_SSM), lambda i: (0, 0))],
        out_specs=(blk, blk), name="ssm_combine", compiler_params=_params("parallel"))(proj, y_dirs, d_skip)


def _ssm_ds(proj, d_yt, du_dirs, d_skip, tm):
    t = proj.shape[0]

    def body(s_ref, dy_ref, du_ref, d_ref, ds_ref, dd_ref):
        i = pl.program_id(0)
        dy = dy_ref[...]
        ds_ref[...] = (dy * d_ref[...] + du_ref[0] + du_ref[1]).astype(BF16)

        @pl.when(i == 0)
        def _():
            dd_ref[...] = jnp.zeros_like(dd_ref)

        dd_ref[...] += jnp.sum(dy * s_ref[...], axis=0, keepdims=True)

    blk = BS((tm, D_SSM), lambda i: (i, 0))
    vec = BS((1, D_SSM), lambda i: (0, 0))
    return pl.pallas_call(
        body, out_shape=(SDS((t, D_SSM), BF16), SDS((1, D_SSM), F32)), grid=(t // tm,),
        in_specs=[BS((tm, D_SSM), lambda i: (i, D_POOL // D_SSM)), blk, BS((2, tm, D_SSM), lambda i: (0, i, 0)), vec],
        out_specs=(blk, vec), name="ssm_ds", compiler_params=_params("arbitrary"))(proj, d_yt, du_dirs, d_skip)


GP_BLOCK = (D_POOL + D_SSM) // 256
GS_BLOCK = GP_BLOCK + D_MODEL // 256


def _merge_specs(tm):
    return [BS((tm, D_POOL), lambda s, i: (i, 0)), BS((tm, D_SSM), lambda s, i: (i, 0)),
            BS((None, D_POOL, 256), lambda s, i: (s, 0, 0)), BS((None, D_SSM, 256), lambda s, i: (s, 2, 0)),
            BS((None, D_SSM, 256), lambda s, i: (s, 3, 0)),
            BS((tm, 256), lambda s, i: (i, GP_BLOCK + s)), BS((tm, 256), lambda s, i: (i, GS_BLOCK + s))]


def _mixer_merge(ms, yssm, w_e, proj, tm):
    t = ms.shape[0]

    def body(ms_ref, y_ref, wpp_ref, wgv_ref, wgg_ref, gp_ref, gs_ref, o_ref):
        zp = _dot(ms_ref[...], wpp_ref[...])
        yv = y_ref[...]
        zv = _dot(yv, wgv_ref[...])
        zg = _dot(yv, wgg_ref[...])
        o_ref[...] = (_sigmoid(gp_ref[...]) * zp + _sigmoid(gs_ref[...]) * zv * _sigmoid(zg)).astype(BF16)

    col = BS((tm, 256), lambda s, i: (i, s))
    return pl.pallas_call(
        body, out_shape=SDS((t, D_MODEL), BF16), grid=(N_SHARD, t // tm), in_specs=_merge_specs(tm), out_specs=col,
        name="mixer_merge", compiler_params=_params("parallel", "parallel"))(ms, yssm, w_e, w_e, w_e, proj, proj)


def _mixer_merge_bwd(ms, yssm, w_e, proj, dmerged, tm):
    t = ms.shape[0]

    def body(ms_ref, y_ref, wpp_ref, wgv_ref, wgg_ref, gp_ref, gs_ref, dm_ref,
             dgp_ref, dgs_ref, dzp_ref, dzv_ref, dzg_ref):
        zp = _dot(ms_ref[...], wpp_ref[...])
        yv = y_ref[...]
        zv = _dot(yv, wgv_ref[...])
        zg = _dot(yv, wgg_ref[...])
        dm = dm_ref[...].astype(F32)
        sp, ss, sg = _sigmoid(gp_ref[...]), _sigmoid(gs_ref[...]), _sigmoid(zg)
        dgp_ref[...] = (dm * zp * sp * (1.0 - sp)).astype(BF16)
        dgs_ref[...] = (dm * zv * sg * ss * (1.0 - ss)).astype(BF16)
        dzp_ref[...] = (dm * sp).astype(BF16)
        dz = dm * ss
        dzv_ref[...] = (dz * sg).astype(BF16)
        dzg_ref[...] = (dz * zv * sg * (1.0 - sg)).astype(BF16)

    col = BS((tm, 256), lambda s, i: (i, s))
    shape = SDS((t, D_MODEL), BF16)
    return pl.pallas_call(
        body, out_shape=(shape,) * 5, grid=(N_SHARD, t // tm), in_specs=_merge_specs(tm) + [col],
        out_specs=(col,) * 5, name="mixer_merge_bwd",
        compiler_params=_params("parallel", "parallel"))(ms, yssm, w_e, w_e, w_e, proj, proj, dmerged)


def _mixer_dw(ms, yssm, dzp, dzv, dzg, tm):
    t = ms.shape[0]
    n_t = t // tm

    def body(ms_ref, y_ref, dzp_ref, dzv_ref, dzg_ref, o_ref, acc):
        i = pl.program_id(1)

        @pl.when(i == 0)
        def _():
            acc[...] = jnp.zeros_like(acc)

        yv = y_ref[...]
        acc[0:D_POOL, :] += _dot(ms_ref[...], dzp_ref[...], TN)
        acc[D_POOL:D_POOL + D_SSM, :] += _dot(yv, dzv_ref[...], TN)
        acc[D_POOL + D_SSM:, :] += _dot(yv, dzg_ref[...], TN)

        @pl.when(i == n_t - 1)
        def _():
            o_ref[...] = acc[...].astype(BF16)

    col = BS((tm, 256), lambda s, i: (i, s))
    return pl.pallas_call(
        body, out_shape=SDS((N_SHARD, 1024, 256), BF16), grid=(N_SHARD, n_t),
        in_specs=[BS((tm, D_POOL), lambda s, i: (i, 0)), BS((tm, D_SSM), lambda s, i: (i, 0)), col, col, col],
        out_specs=BS((None, 1024, 256), lambda s, i: (s, 0, 0)), scratch_shapes=[pltpu.VMEM((1024, 256), F32)],
        name="mixer_dw", compiler_params=_params("parallel", "arbitrary"))(ms, yssm, dzp, dzv, dzg)


def _mixer_dx(dzp, dzv, dzg, w_e, y_total, tm):
    t = dzp.shape[0]

    def body(dzp_ref, dzv_ref, dzg_ref, wpp_ref, wgv_ref, wgg_ref, yt_ref, dms_ref, dy_ref, acc_ms, acc_y):
        s = pl.program_id(1)

        @pl.when(s == 0)
        def _():
            acc_ms[...] = jnp.zeros_like(acc_ms)
            acc_y[...] = jnp.zeros_like(acc_y)

        acc_ms[...] += _dot(dzp_ref[...], wpp_ref[...], NT)
        acc_y[...] += _dot(dzv_ref[...], wgv_ref[...], NT) + _dot(dzg_ref[...], wgg_ref[...], NT)

        @pl.when(s == N_SHARD - 1)
        def _():
            dms_ref[...] = acc_ms[...].astype(BF16)
            y = yt_ref[...]
            inner = GELU_C * (y + GELU_K * y * y * y)
            th = jnp.tanh(inner)
            dgelu = 0.5 * (1.0 + th) + 0.5 * y * (1.0 - th * th) * GELU_C * (1.0 + 3.0 * GELU_K * y * y)
            dy_ref[...] = acc_y[...] * dgelu

    col = BS((tm, 256), lambda i, s: (i, s))
    return pl.pallas_call(
        body, out_shape=(SDS((t, D_POOL), BF16), SDS((t, D_SSM), F32)), grid=(t // tm, N_SHARD),
        in_specs=[col, col, col, BS((None, D_POOL, 256), lambda i, s: (s, 0, 0)),
                  BS((None, D_SSM, 256), lambda i, s: (s, 2, 0)), BS((None, D_SSM, 256), lambda i, s: (s, 3, 0)),
                  BS((tm, D_SSM), lambda i, s: (i, 0))],
        out_specs=(BS((tm, D_POOL), lambda i, s: (i, 0)), BS((tm, D_SSM), lambda i, s: (i, 0))),
        scratch_shapes=[pltpu.VMEM((tm, D_POOL), F32), pltpu.VMEM((tm, D_SSM), F32)],
        name="mixer_dx", compiler_params=_params("parallel", "arbitrary"))(dzp, dzv, dzg, w_e, w_e, w_e, y_total)


def _attn_probs(q_h, k_h):
    s = _dot(q_h, k_h, NT) * (1.0 / math.sqrt(HEAD_DIM))
    e = jnp.exp(s - jnp.max(s, axis=-1, keepdims=True))
    return e / jnp.sum(e, axis=-1, keepdims=True)


def _attn_fwd(q, kv, tm):
    t = q.shape[0]
    m = kv.shape[0]

    def body(q_ref, kv_ref, o_ref):
        for hd in range(N_HEADS):
            lo = hd * HEAD_DIM
            p = _attn_probs(q_ref[:, lo:lo + HEAD_DIM], kv_ref[:, lo:lo + HEAD_DIM])
            o_ref[:, lo:lo + HEAD_DIM] = _dot(p, kv_ref[:, D_MODEL + lo:D_MODEL + lo + HEAD_DIM]).astype(BF16)

    return pl.pallas_call(
        body, out_shape=SDS((t, D_MODEL), BF16), grid=(t // tm,),
        in_specs=[BS((tm, D_MODEL), lambda i: (i, 0)), BS((m, 2 * D_MODEL), lambda i: (0, 0))],
        out_specs=BS((tm, D_MODEL), lambda i: (i, 0)), name="attn_fwd", compiler_params=_params("parallel"))(q, kv)


def _attn_bwd(q, kv, d_o, tm):
    t = q.shape[0]
    m = kv.shape[0]

    def body(q_ref, kv_ref, do_ref, dq_ref, dkv_ref):
        i = pl.program_id(0)

        @pl.when(i == 0)
        def _():
            dkv_ref[...] = jnp.zeros_like(dkv_ref)

        for hd in range(N_HEADS):
            lo = hd * HEAD_DIM
            q_h = q_ref[:, lo:lo + HEAD_DIM]
            k_h = kv_ref[:, lo:lo + HEAD_DIM]
            v_h = kv_ref[:, D_MODEL + lo:D_MODEL + lo + HEAD_DIM]
            do_h = do_ref[:, lo:lo + HEAD_DIM]
            p = _attn_probs(q_h, k_h)
            dkv_ref[:, D_MODEL + lo:D_MODEL + lo + HEAD_DIM] += _dot(p, do_h, TN)
            dp = _dot(do_h, v_h, NT)
            ds = p * (dp - jnp.sum(dp * p, axis=-1, keepdims=True)) * (1.0 / math.sqrt(HEAD_DIM))
            dq_ref[:, lo:lo + HEAD_DIM] = _dot(ds, k_h).astype(BF16)
            dkv_ref[:, lo:lo + HEAD_DIM] += _dot(ds, q_h, TN)

    row = BS((tm, D_MODEL), lambda i: (i, 0))
    full = BS((m, 2 * D_MODEL), lambda i: (0, 0))
    return pl.pallas_call(
        body, out_shape=(SDS((t, D_MODEL), BF16), SDS((m, 2 * D_MODEL), F32)), grid=(t // tm,),
        in_specs=[row, full, row], out_specs=(row, full), name="attn_bwd",
        compiler_params=_params("arbitrary"))(q, kv, d_o)


CLASS_A = ("ffn1_w_gate", "ffn1_w_up", "ffn2_w_gate", "ffn2_w_up", "w_in")
CLASS_B = ("ffn1_w_down", "ffn2_w_down")
CLASS_C = ("w_mix_out", "w_q", "w_xo")
CLASS_D = ("w_kv",)
CLASS_E = ("w_pool_proj", "w_glu_val", "w_glu_gate")
CLASSES = (CLASS_A, CLASS_B, CLASS_C, CLASS_D, CLASS_E)
BIG = tuple(n for cl in CLASSES for n in cl)
SMALL = ("ffn1_norm", "mix_norm", "pool_w", "pool_scale", "ssm_a_re", "ssm_a_im", "ssm_log_dt", "ssm_b_re",
         "ssm_b_im", "ssm_c_re", "ssm_c_im", "ssm_d", "xattn_norm", "mem_norm", "ffn2_norm", "final_norm")
WEIGHTS = ("ffn1_norm", "ffn1_w_gate", "ffn1_w_up", "ffn1_w_down", "mix_norm", "w_in", "pool_w", "pool_scale",
           "w_pool_proj", "ssm_a_re", "ssm_a_im", "ssm_log_dt", "ssm_b_re", "ssm_b_im", "ssm_c_re", "ssm_c_im",
           "ssm_d", "w_glu_val", "w_glu_gate", "w_mix_out", "xattn_norm", "mem_norm", "w_q", "w_kv", "w_xo",
           "ffn2_norm", "ffn2_w_gate", "ffn2_w_up", "ffn2_w_down", "final_norm")


def _block_diag_in(bb):
    eye = jnp.eye(SSM_GROUPS, dtype=bb.dtype)
    return jnp.einsum("dgph,gk->dghkp", bb, eye).reshape(2, D_SSM, SSM_CH)


def _block_diag_out(cc):
    eye = jnp.eye(SSM_GROUPS, dtype=cc.dtype)
    return jnp.einsum("dghp,gk->dgpkh", cc, eye).reshape(2, SSM_CH, D_SSM)


def _diag_blocks_in(m):
    return jnp.einsum("dghgp->dgph", m.reshape(2, SSM_GROUPS, SSM_GROUP, SSM_GROUPS, SSM_STATE))


def _diag_blocks_out(m):
    return jnp.einsum("dgpgh->dghp", m.reshape(2, SSM_GROUPS, SSM_STATE, SSM_GROUPS, SSM_GROUP))


def _device_step(x, mem, target, gw, sp):
    t = x.shape[0]
    tm = min(TM, t)
    w_a, w_b, w_c, w_d, w_e4 = gw
    w_e = w_e4[:, 0]
    w_in = jnp.transpose(w_a[:, 4], (1, 0, 2)).reshape(D_MODEL, D_FF)
    w_mo, w_q, w_xo = (w_c[:, k].reshape(D_MODEL, D_MODEL) for k in range(3))
    g = {}

    u1 = _rmsnorm("norm_ffn1", x, sp["ffn1_norm"], tm)
    g1, up1, a1 = _ffn_up("ffn1_up", u1, w_a, 0, 1, tm)
    h1 = _ffn_down("ffn1_down", a1, w_b, 0, x, tm)

    u2 = _rmsnorm("norm_mix", h1, sp["mix_norm"], tm)
    proj = _mm("mix_in", [(u2, BS((tm, D_MODEL), lambda j, i: (i, 0)), w_in, BS((D_MODEL, D_FF // 2), lambda j, i: (0, j)), NN)],
               grid=(2, t // tm), out_shape=SDS((t, D_FF), F32), out_spec=BS((tm, D_FF // 2), lambda j, i: (i, j)))
    pooled, mixed, ms = _pool_fwd(proj, sp["pool_w"][0], sp["pool_scale"])

    ar = sp["ssm_a_re"].reshape(2 * SSM_GROUPS, SSM_STATE)
    ai = sp["ssm_a_im"].reshape(2 * SSM_GROUPS, SSM_STATE)
    ldt = sp["ssm_log_dt"].reshape(2 * SSM_GROUPS, 1)
    abr, abi, qr, qi = _ssm_disc(ar, ai, ldt)
    b_r = sp["ssm_b_re"].reshape(2 * SSM_CH, SSM_GROUP)
    b_i = sp["ssm_b_im"].reshape(2 * SSM_CH, SSM_GROUP)
    qr_col, qi_col = qr.reshape(2 * SSM_CH, 1), qi.reshape(2 * SSM_CH, 1)
    bbr, bbi = _ssm_bbar(qr_col, qi_col, b_r, b_i)
    shape_b = (2, SSM_GROUPS, SSM_STATE, SSM_GROUP)
    b_mat = jnp.concatenate([_block_diag_in(bbr.reshape(shape_b)), _block_diag_in(bbi.reshape(shape_b))], axis=-1).astype(BF16)
    c_mat = jnp.concatenate([_block_diag_out(sp["ssm_c_re"][0]), -_block_diag_out(sp["ssm_c_im"][0])], axis=1).astype(BF16)
    b_mat_t = jnp.swapaxes(b_mat, 1, 2)
    c_mat_t = jnp.swapaxes(c_mat, 1, 2)
    a_r = abr.reshape(2, 1, SSM_CH)
    a_i = abi.reshape(2, 1, SSM_CH)

    s_in =proj[:, D_POOL:D_POOL + D_SSM].astype(BF16)
    states, y_dirs = [], []
    for dr in range(2):
        st, yd = _ssm_scan(f"ssm_scan_fwd{dr}", s_in, b_mat[dr], a_r[dr], a_i[dr], c_mat[dr], reverse=(dr == 1))
        states.append(st)
        y_dirs.append(yd)
    y_total, yssm = _ssm_combine(proj, jnp.stack(y_dirs), sp["ssm_d"], tm)

    merged = _mixer_merge(ms, yssm, w_e, proj, tm)
    h2 = _plain_mm("mix_out", merged, w_mo, NN, F32, tm, resid=h1)

    u3 = _rmsnorm("norm_xattn", h2, sp["xattn_norm"], tm)
    mem_n = _rmsnorm("norm_mem", mem, sp["mem_norm"], mem.shape[0])
    q = _plain_mm("attn_q", u3, w_q, NN, BF16, tm)
    n_mem = mem.shape[0]
    kv = _mm("attn_kv", [(mem_n, BS((n_mem, D_MODEL), lambda s: (0, 0)), w_d, BS((None, None, D_MODEL, 512), lambda s: (s, 0, 0, 0)), NN)],
             grid=(N_SHARD,), out_shape=SDS((n_mem, 2 * D_MODEL), BF16), out_spec=BS((n_mem, 512), lambda s: (0, s)))
    o = _attn_fwd(q, kv, tm)
    h3 = _plain_mm("attn_out", o, w_xo, NN, F32, tm, resid=h2)

    u4 = _rmsnorm("norm_ffn2", h3, sp["ffn2_norm"], tm)
    g2, up2, a2 = _ffn_up("ffn2_up", u4, w_a, 2, 3, tm)
    h4 = _ffn_down("ffn2_down", a2, w_b, 1, h3, tm)

    loss, dh4, dh4_b, g["final_norm"] = _loss_head(h4, sp["final_norm"].reshape(1, D_MODEL), target, tm)

    dg2, dup2 = _ffn_bwd_act("ffn2_bwd_act", dh4_b, w_b, 1, g2, up2, tm)
    dw_d2 = _ffn_dw_down("ffn2_dw_down", a2, dh4_b, tm)
    dw_g2, dw_u2 = _ffn_dw_up("ffn2_dw_up", u4, dg2, dup2, tm)
    du4 = _ffn_dx("ffn2_dx", dg2, dup2, w_a, 2, 3, tm)
    dh3, dh3_b, g["ffn2_norm"] = _rmsnorm_bwd("norm_ffn2_bwd", h3, sp["ffn2_norm"], du4, dh4, tm)

    d_o = _plain_mm("attn_out_dx", dh3_b, w_xo, NT, BF16, tm)
    dw_xo = _dw_mm("attn_out_dw", o, dh3_b, tm)
    dq, dkv = _attn_bwd(q, kv, d_o, tm)
    dw_q = _dw_mm("attn_q_dw", u3, dq, tm)
    du3 = _plain_mm("attn_q_dx", dq, w_q, NT, F32, tm)
    dw_kv = _mm("attn_kv_dw", [(mem_n, BS((n_mem, D_MODEL), lambda s: (0, 0)), dkv, BS((n_mem, 512), lambda s: (0, s)), TN)],
                grid=(N_SHARD,), out_shape=SDS((N_SHARD, D_MODEL, 512), BF16), out_spec=BS((None, D_MODEL, 512), lambda s: (s, 0, 0)))
    dmem_n = _mm("attn_kv_dx", [(dkv, BS((n_mem, 512), lambda s: (0, s)), w_d, BS((None, None, D_MODEL, 512), lambda s: (s, 0, 0, 0)), NT)],
                 grid=(N_SHARD,), red_axis=0, out_shape=SDS((n_mem, D_MODEL), F32), out_spec=BS((n_mem, D_MODEL), lambda s: (0, 0)))
    _, _, g["mem_norm"] = _rmsnorm_bwd("norm_mem_bwd", mem, sp["mem_norm"], dmem_n, None, n_mem)
    dh2, dh2_b, g["xattn_norm"] = _rmsnorm_bwd("norm_xattn_bwd", h2, sp["xattn_norm"], du3, dh3, tm)

    dmerged = _plain_mm("mix_out_dx", dh2_b, w_mo, NT, BF16, tm)
    dw_mo = _dw_mm("mix_out_dw", merged, dh2_b, tm)
    d_gp, d_gs, dzp, dzv, dzg = _mixer_merge_bwd(ms, yssm, w_e, proj, dmerged, tm)
    dw_e = _mixer_dw(ms, yssm, dzp, dzv, dzg, tm)
    d_ms, d_yt = _mixer_dx(dzp, dzv, dzg, w_e, y_total, tm)
    dp, d_scale, d_pw = _pool_bwd(d_ms, mixed, pooled, sp["pool_w"][0], sp["pool_scale"])
    g["pool_scale"] = d_scale
    g["pool_w"] = d_pw[None]

    d_yt_b = d_yt.astype(BF16)
    du_dirs, d_abr, d_abi, d_cm, d_bm = [], [], [], [], []
    for dr in range(2):
        lam, du = _ssm_scan(f"ssm_scan_bwd{dr}", d_yt_b, c_mat_t[dr], a_r[dr], -a_i[dr], b_mat_t[dr], reverse=(dr == 0))
        du_dirs.append(du)
        da_r, da_i = _ssm_da(f"ssm_da{dr}", lam, states[dr], reverse=(dr == 1))
        d_abr.append(da_r)
        d_abi.append(da_i)
        d_cm.append(_dw_mm(f"ssm_dc{dr}", states[dr], d_yt_b, tm, F32))
        d_bm.append(_dw_mm(f"ssm_db{dr}", s_in, lam, tm, F32))
    d_cm = jnp.stack(d_cm)
    d_bm = jnp.stack(d_bm)
    g["ssm_c_re"] = _diag_blocks_out(d_cm[:, :SSM_CH])[None]
    g["ssm_c_im"] = -_diag_blocks_out(d_cm[:, SSM_CH:])[None]
    g_bbr = _diag_blocks_in(d_bm[:, :, :SSM_CH]).reshape(2 * SSM_CH, SSM_GROUP)
    g_bbi = _diag_blocks_in(d_bm[:, :, SSM_CH:]).reshape(2 * SSM_CH, SSM_GROUP)
    d_qr, d_qi, d_br, d_bi = _ssm_bbar_bwd(qr_col, qi_col, b_r, b_i, g_bbr, g_bbi)
    g["ssm_b_re"] = d_br.reshape(sp["ssm_b_re"].shape)
    g["ssm_b_im"] = d_bi.reshape(sp["ssm_b_im"].shape)
    d_ar, d_ai, d_ldt = _ssm_disc_bwd(ar, ai, ldt, jnp.stack(d_abr).reshape(ar.shape), jnp.stack(d_abi).reshape(ar.shape),
                                      d_qr.reshape(ar.shape), d_qi.reshape(ar.shape))
    g["ssm_a_re"] = d_ar.reshape(sp["ssm_a_re"].shape)
    g["ssm_a_im"] = d_ai.reshape(sp["ssm_a_im"].shape)
    g["ssm_log_dt"] = d_ldt.reshape(sp["ssm_log_dt"].shape)
    ds, g["ssm_d"] = _ssm_ds(proj, d_yt, jnp.stack(du_dirs), sp["ssm_d"], tm)

    d_proj = jnp.concatenate([dp, ds, d_gp, d_gs], axis=1)
    dw_in = _mm("mix_in_dw", [(u2, BS((tm, D_MODEL), lambda j, i: (i, 0)), d_proj, BS((tm, D_FF // 2), lambda j, i: (i, j)), TN)],
                grid=(2, t // tm), red_axis=1, out_shape=SDS((D_MODEL, D_FF), BF16), out_spec=BS((D_MODEL, D_FF // 2), lambda j, i: (0, j)))
    du2 = _plain_mm("mix_in_dx", d_proj, w_in, NT, F32, tm)
    dh1, dh1_b, g["mix_norm"] = _rmsnorm_bwd("norm_mix_bwd", h1, sp["mix_norm"], du2, dh2, tm)

    dg1, dup1 = _ffn_bwd_act("ffn1_bwd_act", dh1_b, w_b, 0, g1, up1, tm)
    dw_d1 = _ffn_dw_down("ffn1_dw_down", a1, dh1_b, tm)
    dw_g1, dw_u1 = _ffn_dw_up("ffn1_dw_up", u1, dg1, dup1, tm)
    du1 = _ffn_dx("ffn1_dx", dg1, dup1, w_a, 0, 1, tm)
    grad_x, _, g["ffn1_norm"] = _rmsnorm_bwd("norm_ffn1_bwd", x, sp["ffn1_norm"], du1, dh1, tm)

    dw_in_sm = jnp.transpose(dw_in.reshape(D_MODEL, N_SHARD, FF_SH), (1, 0, 2))
    big = (
        jnp.stack([dw_g1, dw_u1, dw_g2, dw_u2, dw_in_sm], axis=1).reshape(N_SHARD, 5 * D_MODEL, FF_SH),
        jnp.stack([dw_d1, dw_d2], axis=1).reshape(N_SHARD, 2 * FF_SH, D_MODEL),
        jnp.stack([dw_mo.reshape(N_SHARD, 256, D_MODEL), dw_q.reshape(N_SHARD, 256, D_MODEL),
                   dw_xo.reshape(N_SHARD, 256, D_MODEL)], axis=1).reshape(N_SHARD, 3 * 256, D_MODEL),
        dw_kv,
        dw_e,
    )
    g["final_norm"] = g["final_norm"].reshape(D_MODEL)
    return loss, grad_x, big, g


def _mesh_place():
    x, y, c = lax.axis_index("x"), lax.axis_index("y"), lax.axis_index("c")
    chips = [(1 - x, y), (x, 1 - y), (1 - x, 1 - y)]
    return x, y, c, chips


def _remote(src, dst, send_sems, recv_sems, k, to):
    return pltpu.make_async_remote_copy(src_ref=src, dst_ref=dst, send_sem=send_sems.at[k], recv_sem=recv_sems.at[k],
                                        device_id=to, device_id_type=MESH)


def _allgather_shards(shards):
    n = len(shards)

    def body(*refs):
        ins, outs = refs[:n], refs[n:2 * n]
        send_sems, recv_sems, local_sems = refs[2 * n:]
        x, y, c, chips = _mesh_place()
        me = 2 * x + y
        sibling = (x, y, 1 - c)
        started, local_copies = [], []
        for k in range(n):
            half = shards[k].shape[0] // 2
            mine = pl.ds(pl.multiple_of(c * half, 16), half)
            local = pltpu.make_async_copy(ins[k], outs[k].at[me], local_sems.at[k])
            local.start()
            local_copies.append(local)
            for j, (px, py) in enumerate(chips):
                cp = _remote(ins[k].at[mine, :], outs[k].at[me, mine, :], send_sems, recv_sems, 6 * k + j, (px, py, c))
                cp.start()
                started.append(cp)
        for k in range(n):
            half = shards[k].shape[0] // 2
            mine = pl.ds(pl.multiple_of(c * half, 16), half)
            for j, (px, py) in enumerate(chips):
                blk = outs[k].at[2 * px + py, mine, :]
                _remote(blk, blk, send_sems, recv_sems, 6 * k + j, (px, py, c)).wait_recv()
                fwd = _remote(blk, blk, send_sems, recv_sems, 6 * k + 3 + j, sibling)
                fwd.start()
                started.append(fwd)
        for k in range(n):
            half = shards[k].shape[0] // 2
            theirs = pl.ds(pl.multiple_of((1 - c) * half, 16), half)
            for j, (px, py) in enumerate(chips):
                blk = outs[k].at[2 * px + py, theirs, :]
                _remote(blk, blk, send_sems, recv_sems, 6 * k + 3 + j, sibling).wait_recv()
        for cp in started:
            cp.wait_send()
        for cp in local_copies:
            cp.wait()

    hbm = BS(memory_space=pl.ANY)
    return pl.pallas_call(
        body, out_shape=tuple(SDS((N_SHARD,) + s.shape, s.dtype) for s in shards),
        in_specs=[hbm] * n, out_specs=(hbm,) * n,
        scratch_shapes=[pltpu.SemaphoreType.DMA((6 * n,)), pltpu.SemaphoreType.DMA((6 * n,)), pltpu.SemaphoreType.DMA((n,))],
        name="allgather_weights", compiler_params=_params())(*shards)


def _sibling_swap_halves(grads):
    n = len(grads)

    def body(*refs):
        ins, outs = refs[:n], refs[n:2 * n]
        send_sems, recv_sems = refs[2 * n:]
        x, y, c, _ = _mesh_place()
        sibling = (x, y, 1 - c)
        copies = []
        for k in range(n):
            half = grads[k].shape[1] // 2
            theirs = pl.ds(pl.multiple_of((1 - c) * half, 16), half)
            cp = _remote(ins[k].at[:, theirs, :], outs[k], send_sems, recv_sems, k, sibling)
            cp.start()
            copies.append(cp)
        for cp in copies:
            cp.wait_recv()
        for cp in copies:
            cp.wait_send()

    hbm = BS(memory_space=pl.ANY)
    return pl.pallas_call(
        body, out_shape=tuple(SDS((g.shape[0], g.shape[1] // 2, g.shape[2]), g.dtype) for g in grads),
        in_specs=[hbm] * n, out_specs=(hbm,) * n,
        scratch_shapes=[pltpu.SemaphoreType.DMA((n,)), pltpu.SemaphoreType.DMA((n,))],
        name="reduce_sibling_send", compiler_params=_params())(*grads)


def _row_tile(rows, cap=512):
    return max(r for r in range(16, cap + 1, 16) if rows % r == 0)


def _chip_presum(k, grad, got, c_idx):
    n_sh, rows, cols = grad.shape
    half = rows // 2
    tr = _row_tile(half)
    grad4 = grad.reshape(n_sh, 2, half, cols)

    def body(c_ref, a_ref, b_ref, o_ref):
        o_ref[...] = (a_ref[...].astype(F32) + b_ref[...].astype(F32)).astype(o_ref.dtype)

    return pl.pallas_call(
        body, out_shape=SDS((n_sh, half, cols), BF16),
        grid_spec=pltpu.PrefetchScalarGridSpec(
            num_scalar_prefetch=1, grid=(n_sh, half // tr),
            in_specs=[BS((None, None, tr, cols), lambda s, i, c_ref: (s, c_ref[0], i, 0)),
                      BS((None, tr, cols), lambda s, i, c_ref: (s, i, 0))],
            out_specs=BS((None, tr, cols), lambda s, i, c_ref: (s, i, 0))),
        name=f"reduce_presum{k}", compiler_params=_params("parallel", "parallel"))(c_idx, grad4, got)


def _chip_exchange(parts):
    n = len(parts)

    def body(*refs):
        ins, outs = refs[:n], refs[n:2 * n]
        send_sems, recv_sems = refs[2 * n:]
        x, y, c, chips = _mesh_place()
        copies = []
        for k in range(n):
            for j, (px, py) in enumerate(chips):
                cp = _remote(ins[k].at[2 * px + py], outs[k].at[j], send_sems, recv_sems, 3 * k + j, (px, py, c))
                cp.start()
                copies.append(cp)
        for cp in copies:
            cp.wait_recv()
        for cp in copies:
            cp.wait_send()

    hbm = BS(memory_space=pl.ANY)
    return pl.pallas_call(
        body, out_shape=tuple(SDS((3,) + p.shape[1:], p.dtype) for p in parts),
        in_specs=[hbm] * n, out_specs=(hbm,) * n,
        scratch_shapes=[pltpu.SemaphoreType.DMA((3 * n,)), pltpu.SemaphoreType.DMA((3 * n,))],
        name="reduce_chip_exchange", compiler_params=_params())(*parts)


def _chip_sum(k, part, got, place):
    _, half, cols = part.shape
    tr = _row_tile(half)
    n_t = half // tr

    def body(place_ref, a_ref, b_ref, o_ref):
        acc = a_ref[...].astype(F32)
        for j in range(3):
            acc = acc + b_ref[j].astype(F32)
        o_ref[...] = acc

    return pl.pallas_call(
        body, out_shape=SDS((2 * half, cols), F32),
        grid_spec=pltpu.PrefetchScalarGridSpec(
            num_scalar_prefetch=1, grid=(n_t,),
            in_specs=[BS((None, tr, cols), lambda i, place_ref: (place_ref[0], i, 0)),
                      BS((3, tr, cols), lambda i, place_ref: (0, i, 0))],
            out_specs=BS((tr, cols), lambda i, place_ref: (place_ref[1] * n_t + i, 0))),
        name=f"reduce_sum{k}", compiler_params=_params("parallel"))(place, part, got)


def _sibling_join_halves(fulls):
    n = len(fulls)

    def body(*refs):
        outs = refs[n:2 * n]
        send_sems, recv_sems = refs[2 * n:]
        x, y, c, _ = _mesh_place()
        sibling = (x, y, 1 - c)
        sent = []
        for k in range(n):
            half = fulls[k].shape[0] // 2
            mine = outs[k].at[pl.ds(pl.multiple_of(c * half, 8), half), :]
            cp = _remote(mine, mine, send_sems, recv_sems, k, sibling)
            cp.start()
            sent.append(cp)
        for k in range(n):
            half = fulls[k].shape[0] // 2
            theirs = outs[k].at[pl.ds(pl.multiple_of((1 - c) * half, 8), half), :]
            _remote(theirs, theirs, send_sems, recv_sems, k, sibling).wait_recv()
        for cp in sent:
            cp.wait_send()

    hbm = BS(memory_space=pl.ANY)
    return pl.pallas_call(
        body, out_shape=tuple(SDS(f.shape, f.dtype) for f in fulls),
        in_specs=[hbm] * n, out_specs=(hbm,) * n, input_output_aliases={k: k for k in range(n)},
        scratch_shapes=[pltpu.SemaphoreType.DMA((n,)), pltpu.SemaphoreType.DMA((n,))],
        name="reduce_sibling_join", compiler_params=_params())(*fulls)


N_DEV = 8


def _allreduce_small(part):
    rows, lanes = part.shape

    def body(x_ref, out_ref, all_ref, send_sems, recv_sems, local_sem):
        x, y, c, chips = _mesh_place()
        me, sibling = (x, y, c), (x, y, 1 - c)

        def blk(px, py, pc):
            return all_ref.at[pl.ds(pl.multiple_of((4 * px + 2 * py + pc) * rows, 8), rows), :]

        def copy(k, block, to, src=None):
            return _remote(blk(*block) if src is None else src, blk(*block), send_sems, recv_sems, k, to)

        mine = pltpu.make_async_copy(x_ref, blk(*me), local_sem)
        mine.start()
        first = [copy(0, me, sibling, src=x_ref)]
        first += [copy(1 + j, me, (*chip, c), src=x_ref) for j, chip in enumerate(chips)]
        for cp in first:
            cp.start()
        passed = [copy(4 + j, (*chip, c), sibling) for j, chip in enumerate(chips)]
        for j, chip in enumerate(chips):
            copy(1 + j, (*chip, c), me).wait_recv()
            passed[j].start()
        copy(0, sibling, me).wait_recv()
        for j, chip in enumerate(chips):
            copy(4 + j, (*chip, 1 - c), me).wait_recv()
        for cp in first + passed:
            cp.wait_send()
        mine.wait()
        acc = all_ref[pl.ds(0, rows), :]
        for dev in range(1, N_DEV):
            acc = acc + all_ref[pl.ds(dev * rows, rows), :]
        out_ref[...] = acc

    vm = BS(memory_space=pltpu.VMEM)
    return pl.pallas_call(
        body, out_shape=SDS((rows, lanes), F32), in_specs=[vm], out_specs=vm,
        scratch_shapes=[pltpu.VMEM((N_DEV * rows, lanes), F32), pltpu.SemaphoreType.DMA((7,)),
                        pltpu.SemaphoreType.DMA((7,)), pltpu.SemaphoreType.DMA],
        name="allreduce_small", compiler_params=_params())(part)


def _adamw(name, w, grad, row0, m, v):
    rows, cols = w.shape
    tr = rows if rows < 16 else _row_tile(rows, 256)
    bc1 = 1.0 - ADAM_B1 ** ADAM_STEP
    bc2 = 1.0 - ADAM_B2 ** ADAM_STEP

    def body(w_ref, g_ref, m_ref, v_ref, go_ref, d_ref, mo_ref, vo_ref):
        g = g_ref[...]
        m_new = ADAM_B1 * m_ref[...] + (1.0 - ADAM_B1) * g
        v_new = ADAM_B2 * v_ref[...] + (1.0 - ADAM_B2) * (g * g)
        go_ref[...] = g
        mo_ref[...] = m_new
        vo_ref[...] = v_new
        d_ref[...] = -ADAM_LR * ((m_new / bc1) / (jnp.sqrt(v_new / bc2) + ADAM_EPS) + ADAM_WD * w_ref[...])

    blk = BS((tr, cols), lambda i: (i, 0))
    shape = SDS((rows, cols), F32)
    return pl.pallas_call(
        body, out_shape=(shape,) * 4, grid=(rows // tr,),
        in_specs=[blk, BS((tr, cols), lambda i: (row0 // tr + i, 0)), blk, blk], out_specs=(blk,) * 4,
        name=name, compiler_params=_params("parallel"))(w, grad, m, v)


SMALL_LANES = 128


def _pack_small(parts):
    flat = jnp.concatenate([jnp.ravel(p) for p in parts])
    rows = -(-flat.shape[0] // (64 * SMALL_LANES)) * 64
    return jnp.pad(flat, (0, rows * SMALL_LANES - flat.shape[0])).reshape(rows, SMALL_LANES)


def _unpack_small(packed, like):
    flat = jnp.ravel(packed)
    out, at = [], 0
    for p in like:
        out.append(flat[at:at + p.size].reshape(p.shape))
        at += p.size
    return out


def kernel(x, mem, ffn1_norm, ffn1_w_gate, ffn1_w_up, ffn1_w_down, mix_norm, w_in, pool_w, pool_scale, w_pool_proj, ssm_a_re, ssm_a_im, ssm_log_dt, ssm_b_re, ssm_b_im, ssm_c_re, ssm_c_im, ssm_d, w_glu_val, w_glu_gate, w_mix_out, xattn_norm, mem_norm, w_q, w_kv, w_xo, ffn2_norm, ffn2_w_gate, ffn2_w_up, ffn2_w_down, final_norm, loss_target, m_ffn1_norm, m_ffn1_w_gate, m_ffn1_w_up, m_ffn1_w_down, m_mix_norm, m_w_in, m_pool_w, m_pool_scale, m_w_pool_proj, m_ssm_a_re, m_ssm_a_im, m_ssm_log_dt, m_ssm_b_re, m_ssm_b_im, m_ssm_c_re, m_ssm_c_im, m_ssm_d, m_w_glu_val, m_w_glu_gate, m_w_mix_out, m_xattn_norm, m_mem_norm, m_w_q, m_w_kv, m_w_xo, m_ffn2_norm, m_ffn2_w_gate, m_ffn2_w_up, m_ffn2_w_down, m_final_norm, v_ffn1_norm, v_ffn1_w_gate, v_ffn1_w_up, v_ffn1_w_down, v_mix_norm, v_w_in, v_pool_w, v_pool_scale, v_w_pool_proj, v_ssm_a_re, v_ssm_a_im, v_ssm_log_dt, v_ssm_b_re, v_ssm_b_im, v_ssm_c_re, v_ssm_c_im, v_ssm_d, v_w_glu_val, v_w_glu_gate, v_w_mix_out, v_xattn_norm, v_mem_norm, v_w_q, v_w_kv, v_w_xo, v_ffn2_norm, v_ffn2_w_gate, v_ffn2_w_up, v_ffn2_w_down, v_final_norm):
    given = dict(locals())
    w = {n: given[n] for n in WEIGHTS}
    m = {n: given["m_" + n] for n in WEIGHTS}
    v = {n: given["v_" + n] for n in WEIGHTS}

    shards = [jnp.concatenate([w[n][0].astype(BF16) for n in cl], axis=0) for cl in CLASSES]
    gathered = _allgather_shards(shards)
    gw = tuple(gth.reshape((N_SHARD, 1 if cl is CLASS_E else len(cl), -1, gth.shape[-1]))
               for cl, gth in zip(CLASSES, gathered))

    loss_part, grad_x, big, small = _device_step(x[0], mem[0], loss_target[0], gw, {n: w[n] for n in SMALL})
    loss = lax.psum(loss_part[0, 0], ("x", "y", "c"))

    c_idx = lax.axis_index("c").astype(jnp.int32).reshape(1)
    place = jnp.stack([2 * lax.axis_index("x") + lax.axis_index("y"), lax.axis_index("c")]).astype(jnp.int32)
    from_sibling = _sibling_swap_halves(big)
    chip_parts = [_chip_presum(k, big[k], from_sibling[k], c_idx) for k in range(len(CLASSES))]
    from_chips = _chip_exchange(chip_parts)
    halves = [_chip_sum(k, chip_parts[k], from_chips[k], place) for k in range(len(CLASSES))]
    reduced = _sibling_join_halves(halves)

    grads, delta, new_m, new_v = {}, {}, {}, {}
    for cl, red in zip(CLASSES, reduced):
        row0 = 0
        for n in cl:
            shape = w[n].shape
            two_d = shape[1:]
            grads[n], delta[n], new_m[n], new_v[n] = (
                o.reshape(shape) for o in _adamw("adamw_" + n, w[n].reshape(two_d), red, row0, m[n].reshape(two_d), v[n].reshape(two_d)))
            row0 += two_d[0]

    small_like = [w[n] for n in SMALL]
    g_small = _allreduce_small(_pack_small([small[n] for n in SMALL]))
    packed = _adamw("adamw_small", _pack_small(small_like), g_small, 0,
                    _pack_small([m[n] for n in SMALL]), _pack_small([v[n] for n in SMALL]))
    for out, store in zip(packed, (grads, delta, new_m, new_v)):
        for n, val in zip(SMALL, _unpack_small(out, small_like)):
            store[n] = val

    return (loss, grad_x[None], *[grads[n] for n in WEIGHTS], *[delta[n] for n in WEIGHTS],
            *[new_m[n] for n in WEIGHTS], *[new_v[n] for n in WEIGHTS])
```

```python
import functools
import math

import jax
import jax.numpy as jnp
from jax import lax
from jax.experimental import pallas as pl
from jax.experimental.pallas import tpu as pltpu

F32 = jnp.float32
BF16 = jnp.bfloat16
SDS = jax.ShapeDtypeStruct
BS = pl.BlockSpec
MESH = pl.DeviceIdType.MESH

D_MODEL = 1024
D_FF = 2816
N_SHARD = 4
FF_SH = D_FF // N_SHARD
D_POOL = 512
POOL_WINDOWS = (2, 4, 8, 16)
POOL_GROUP = 128
D_SSM = 256
SSM_GROUPS = 16
SSM_GROUP = 16
SSM_STATE = 64
SSM_CH = SSM_GROUPS * SSM_STATE
N_HEADS = 4
HEAD_DIM = 256
EPS = 1e-6
ADAM_LR, ADAM_B1, ADAM_B2, ADAM_EPS, ADAM_WD, ADAM_STEP = 0.001, 0.9, 0.999, 1e-08, 0.01, 10

VMEM_LIMIT_V7X = 52 * 1024 * 1024
TM = 512

NN = (((1,), (0,)), ((), ()))
NT = (((1,), (1,)), ((), ()))
TN = (((0,), (0,)), ((), ()))


def _params(*sem):
    return pltpu.CompilerParams(dimension_semantics=sem if sem else None, vmem_limit_bytes=VMEM_LIMIT_V7X)


def _dot(a, b, dims=NN):
    return lax.dot_general(a.astype(BF16), b.astype(BF16), dims, preferred_element_type=F32)


def _sigmoid(v):
    return 1.0 / (1.0 + jnp.exp(-v))


def _block_dims(spec):
    return tuple(d for d in spec.block_shape if d is not None)


def _mm(name, pairs, *, grid, out_shape, out_spec, red_axis=None, extras=(), epilogue=None):
    n_pairs, n_extra = len(pairs), len(extras)
    n_red = grid[red_axis] if red_axis is not None else 1
    dims = [p[4] for p in pairs]

    def body(*refs):
        ab = refs[:2 * n_pairs]
        ex = refs[2 * n_pairs:2 * n_pairs + n_extra]
        o_ref = refs[2 * n_pairs + n_extra]

        def partial():
            acc = None
            for p in range(n_pairs):
                t = _dot(ab[2 * p][...], ab[2 * p + 1][...], dims[p])
                acc = t if acc is None else acc + t
            return acc

        def finish(acc):
            res = epilogue(acc, *[e[...] for e in ex]) if epilogue is not None else acc
            o_ref[...] = res.astype(o_ref.dtype)

        if n_red == 1:
            finish(partial())
        else:
            acc_ref = refs[-1]
            k = pl.program_id(red_axis)

            @pl.when(k == 0)
            def _():
                acc_ref[...] = jnp.zeros_like(acc_ref)

            acc_ref[...] += partial()

            @pl.when(k == n_red - 1)
            def _():
                finish(acc_ref[...])

    operands, in_specs = [], []
    for a, a_spec, b, b_spec, _ in pairs:
        operands += [a, b]
        in_specs += [a_spec, b_spec]
    for e, e_spec in extras:
        operands.append(e)
        in_specs.append(e_spec)
    scratch = [pltpu.VMEM(_block_dims(out_spec), F32)] if n_red > 1 else []
    sem = tuple("arbitrary" if ax == red_axis else "parallel" for ax in range(len(grid)))
    return pl.pallas_call(body, out_shape=out_shape, grid=grid, in_specs=in_specs, out_specs=out_spec,
                          scratch_shapes=scratch, name=name, compiler_params=_params(*sem))(*operands)


def _rmsnorm(name, h, gain, tm):
    t, d = h.shape

    def body(h_ref, g_ref, u_ref):
        hv = h_ref[...]
        r = lax.rsqrt(jnp.mean(hv * hv, axis=-1, keepdims=True) + EPS)
        u_ref[...] = ((hv * r) * g_ref[...]).astype(u_ref.dtype)

    return pl.pallas_call(
        body, out_shape=SDS((t, d), BF16), grid=(t // tm,),
        in_specs=[BS((tm, d), lambda i: (i, 0)), BS((1, d), lambda i: (0, 0))],
        out_specs=BS((tm, d), lambda i: (i, 0)), name=name, compiler_params=_params("parallel"))(h, gain)


def _rmsnorm_bwd(name, h, gain, du, dh_in, tm):
    t, d = h.shape
    has_in = dh_in is not None

    def body(*refs):
        if has_in:
            h_ref, g_ref, du_ref, dhin_ref, dh_ref, dhb_ref, dg_ref = refs
        else:
            h_ref, g_ref, du_ref, dh_ref, dhb_ref, dg_ref = refs
        i = pl.program_id(0)
        hv = h_ref[...]
        r = lax.rsqrt(jnp.mean(hv * hv, axis=-1, keepdims=True) + EPS)
        n = hv * r
        duv = du_ref[...].astype(F32)
        dn = duv * g_ref[...]
        dh = r * (dn - n * jnp.mean(dn * n, axis=-1, keepdims=True))
        if has_in:
            dh = dhin_ref[...] + dh
        dh_ref[...] = dh
        dhb_ref[...] = dh.astype(BF16)

        @pl.when(i == 0)
        def _():
            dg_ref[...] = jnp.zeros_like(dg_ref)

        dg_ref[...] += jnp.sum(duv * n, axis=0, keepdims=True)

    row = BS((tm, d), lambda i: (i, 0))
    vec = BS((1, d), lambda i: (0, 0))
    operands = [h, gain, du] + ([dh_in] if has_in else [])
    in_specs = [row, vec, row] + ([row] if has_in else [])
    return pl.pallas_call(
        body, out_shape=(SDS((t, d), F32), SDS((t, d), BF16), SDS((1, d), F32)), grid=(t // tm,),
        in_specs=in_specs, out_specs=(row, row, vec), name=name, compiler_params=_params("arbitrary"))(*operands)


def _loss_head(h, gain, target, tm):
    t, d = h.shape

    def body(h_ref, g_ref, t_ref, loss_ref, dh_ref, dhb_ref, dg_ref):
        i = pl.program_id(0)
        hv = h_ref[...]
        g = g_ref[...]
        r = lax.rsqrt(jnp.mean(hv * hv, axis=-1, keepdims=True) + EPS)
        n = hv * r
        err = n * g - t_ref[...]
        dy = err * (1.0 / d)
        dn = dy * g
        dh = r * (dn - n * jnp.mean(dn * n, axis=-1, keepdims=True))
        dh_ref[...] = dh
        dhb_ref[...] = dh.astype(BF16)

        @pl.when(i == 0)
        def _():
            dg_ref[...] = jnp.zeros_like(dg_ref)
            loss_ref[...] = jnp.zeros_like(loss_ref)

        dg_ref[...] += jnp.sum(dy * n, axis=0, keepdims=True)
        part = 0.5 * jnp.sum(jnp.mean(err * err, axis=-1, keepdims=True), axis=0, keepdims=True)
        loss_ref[...] += jnp.broadcast_to(part, loss_ref.shape)

    row = BS((tm, d), lambda i: (i, 0))
    vec = BS((1, d), lambda i: (0, 0))
    return pl.pallas_call(
        body, out_shape=(SDS((1, 128), F32), SDS((t, d), F32), SDS((t, d), BF16), SDS((1, d), F32)),
        grid=(t // tm,), in_specs=[row, vec, row],
        out_specs=(BS((1, 128), lambda i: (0, 0)), row, row, vec),
        name="loss_head", compiler_params=_params("arbitrary"))(h, gain, target)


FFN_GATE, FFN_UP, FFN_DOWN = 0, 1, 2


def _ffn_w_spec(member, index):
    return BS((None, None, FF_SH, D_MODEL), lambda *g: (g[index], member, 0, 0))


def _ffn_up(name, u, w_f, tm):
    t, d = u.shape

    def body(u_ref, wg_ref, wu_ref, g_ref, up_ref, a_ref):
        uv = u_ref[...]
        g = _dot(uv, wg_ref[...], NT)
        up = _dot(uv, wu_ref[...], NT)
        g_ref[...] = g.astype(BF16)
        up_ref[...] = up.astype(BF16)
        a_ref[...] = (g * _sigmoid(g) * up).astype(BF16)

    hid = BS((None, tm, FF_SH), lambda s, i: (s, i, 0))
    shape = SDS((N_SHARD, t, FF_SH), BF16)
    return pl.pallas_call(
        body, out_shape=(shape, shape, shape), grid=(N_SHARD, t // tm),
        in_specs=[BS((tm, d), lambda s, i: (i, 0)), _ffn_w_spec(FFN_GATE, 0), _ffn_w_spec(FFN_UP, 0)],
        out_specs=(hid, hid, hid), name=name, compiler_params=_params("parallel", "parallel"))(u, w_f, w_f)


def _ffn_down(name, a, w_f, resid, tm):
    t, d = resid.shape
    return _mm(name, [(a, BS((None, tm, FF_SH), lambda i, s: (s, i, 0)), w_f, _ffn_w_spec(FFN_DOWN, 1), NN)],
               grid=(t // tm, N_SHARD), red_axis=1, out_shape=SDS((t, d), F32),
               out_spec=BS((tm, d), lambda i, s: (i, 0)),
               extras=[(resid, BS((tm, d), lambda i, s: (i, 0)))],
               epilogue=lambda acc, res: res + 0.5 * acc)


def _ffn_bwd_act(name, dh_b, w_f, g, up, tm):
    t, d = dh_b.shape

    def body(dh_ref, wd_ref, g_ref, up_ref, dg_ref, dup_ref):
        da = 0.5 * _dot(dh_ref[...], wd_ref[...], NT)
        gv = g_ref[...].astype(F32)
        uv = up_ref[...].astype(F32)
        sg = _sigmoid(gv)
        silu = gv * sg
        dg_ref[...] = (da * uv * (sg + silu * (1.0 - sg))).astype(BF16)
        dup_ref[...] = (da * silu).astype(BF16)

    hid = BS((None, tm, FF_SH), lambda s, i: (s, i, 0))
    shape = SDS((N_SHARD, t, FF_SH), BF16)
    return pl.pallas_call(
        body, out_shape=(shape, shape), grid=(N_SHARD, t // tm),
        in_specs=[BS((tm, d), lambda s, i: (i, 0)), _ffn_w_spec(FFN_DOWN, 0), hid, hid],
        out_specs=(hid, hid), name=name, compiler_params=_params("parallel", "parallel"))(dh_b, w_f, g, up)


def _ffn_dw(name, u, dg, dup, a, dh_b, tm):
    t, d = u.shape
    n_t = t // tm

    def body(u_ref, dg_ref, dup_ref, a_ref, dh_ref, o_ref, acc):
        i = pl.program_id(1)

        @pl.when(i == 0)
        def _():
            acc[...] = jnp.zeros_like(acc)

        uv = u_ref[...]
        acc[FFN_GATE] += _dot(dg_ref[...], uv, TN)
        acc[FFN_UP] += _dot(dup_ref[...], uv, TN)
        acc[FFN_DOWN] += _dot(a_ref[...], dh_ref[...], TN)

        @pl.when(i == n_t - 1)
        def _():
            o_ref[FFN_GATE] = acc[FFN_GATE].astype(BF16)
            o_ref[FFN_UP] = acc[FFN_UP].astype(BF16)
            o_ref[FFN_DOWN] = (0.5 * acc[FFN_DOWN]).astype(BF16)

    hid = BS((None, tm, FF_SH), lambda s, i: (s, i, 0))
    row = BS((tm, d), lambda s, i: (i, 0))
    return pl.pallas_call(
        body, out_shape=SDS((N_SHARD, 3, FF_SH, d), BF16), grid=(N_SHARD, n_t),
        in_specs=[row, hid, hid, hid, row], out_specs=BS((None, 3, FF_SH, d), lambda s, i: (s, 0, 0, 0)),
        scratch_shapes=[pltpu.VMEM((3, FF_SH, d), F32)],
        name=name, compiler_params=_params("parallel", "arbitrary"))(u, dg, dup, a, dh_b)


def _ffn_dx(name, dg, dup, w_f, tm):
    t = dg.shape[1]
    hid = BS((None, tm, FF_SH), lambda i, s: (s, i, 0))
    return _mm(name, [(dg, hid, w_f, _ffn_w_spec(FFN_GATE, 1), NN), (dup, hid, w_f, _ffn_w_spec(FFN_UP, 1), NN)],
               grid=(t // tm, N_SHARD), red_axis=1, out_shape=SDS((t, D_MODEL), F32),
               out_spec=BS((tm, D_MODEL), lambda i, s: (i, 0)))


def _plain_mm(name, a, b, dims, out_dtype, tm, resid=None):
    t = a.shape[0]
    n = b.shape[1] if dims == NN else b.shape[0]
    extras = [(resid, BS((tm, n), lambda i: (i, 0)))] if resid is not None else []
    epi = (lambda acc, res: res + acc) if resid is not None else None
    return _mm(name, [(a, BS((tm, a.shape[1]), lambda i: (i, 0)), b, BS(b.shape, lambda i: (0, 0)), dims)],
               grid=(t // tm,), out_shape=SDS((t, n), out_dtype), out_spec=BS((tm, n), lambda i: (i, 0)),
               extras=extras, epilogue=epi)


def _dw_mm(name, a, b, tm, out_dtype=BF16):
    t, k = a.shape
    n = b.shape[1]
    return _mm(name, [(a, BS((tm, k), lambda i: (i, 0)), b, BS((tm, n), lambda i: (i, 0)), TN)],
               grid=(t // tm,), red_axis=0, out_shape=SDS((k, n), out_dtype), out_spec=BS((k, n), lambda i: (0, 0)))


POOL_CHUNK = 256
POOL_HALO = 8


def _window_sum(v, width, lead):
    n = v.shape[0]
    s = v
    k = 1
    while k < width:
        s = s + pltpu.roll(s, n - k, 0)
        k *= 2
    return pltpu.roll(s, lead, 0) if lead else s


def _pool_count(base, left, right, t, shape):
    pos = base + lax.broadcasted_iota(jnp.int32, shape, 0)
    lo = jnp.maximum(pos - left, 0)
    hi = jnp.minimum(pos + right + 1, t)
    return (hi - lo).astype(F32)


def _pool_fwd(proj, pool_w, pool_scale):
    t = proj.shape[0]
    c, h = POOL_CHUNK, POOL_HALO
    n_chunks = t // c

    def body(proj_hbm, pw_ref, sc_ref, pooled_ref, mixed_ref, ms_ref, pad_ref, sem):
        cp = pltpu.make_async_copy(proj_hbm.at[:, pl.ds(0, D_POOL)], pad_ref.at[pl.ds(h, t), :], sem)
        cp.start()
        pad_ref[pl.ds(0, h), :] = jnp.zeros((h, D_POOL), F32)
        pad_ref[pl.ds(t + h, h), :] = jnp.zeros((h, D_POOL), F32)
        cp.wait()
        for g, width in enumerate(POOL_WINDOWS):
            left = width // 2
            right = width - 1 - left
            cols = slice(g * POOL_GROUP, (g + 1) * POOL_GROUP)
            wmat = pw_ref[g].astype(BF16)
            scale = sc_ref[:, cols]

            def chunk(ci, carry, left=left, right=right, width=width, cols=cols, wmat=wmat, scale=scale):
                base = pl.multiple_of(ci * c, c)
                v = pad_ref[pl.ds(base, c + 2 * h), cols]
                win = _window_sum(v, width, left)[h:h + c]
                cnt = _pool_count(base, left, right, t, (c, POOL_GROUP))
                pooled = (win / cnt - v[h:h + c]).astype(BF16)
                mixed = _dot(pooled, wmat)
                pooled_ref[pl.ds(base, c), cols] = pooled
                mixed_ref[pl.ds(base, c), cols] = mixed.astype(BF16)
                ms_ref[pl.ds(base, c), cols] = (mixed * scale).astype(BF16)
                return carry

            lax.fori_loop(0, n_chunks, chunk, 0)

    vm = BS(memory_space=pltpu.VMEM)
    shape = SDS((t, D_POOL), BF16)
    return pl.pallas_call(
        body, out_shape=(shape, shape, shape),
        in_specs=[BS(memory_space=pl.ANY), vm, vm], out_specs=(vm, vm, vm),
        scratch_shapes=[pltpu.VMEM((t + 2 * h, D_POOL), F32), pltpu.SemaphoreType.DMA],
        name="pool_fwd", compiler_params=_params())(proj, pool_w, pool_scale)


def _pool_bwd(d_ms, mixed, pooled, pool_w, pool_scale):
    t = d_ms.shape[0]
    c, h = POOL_CHUNK, POOL_HALO
    n_chunks = t // c

    def body(dms_ref, mixed_ref, pooled_ref, pw_ref, sc_ref, dp_ref, dsc_ref, dpw_ref, pad_ref):
        pad_ref[pl.ds(0, h), :] = jnp.zeros((h, D_POOL), F32)
        pad_ref[pl.ds(t + h, h), :] = jnp.zeros((h, D_POOL), F32)
        for g, width in enumerate(POOL_WINDOWS):
            left = width // 2
            right = width - 1 - left
            cols = slice(g * POOL_GROUP, (g + 1) * POOL_GROUP)
            wmat = pw_ref[g].astype(BF16)
            scale = sc_ref[:, cols]

            def first(ci, carry, left=left, right=right, cols=cols, wmat=wmat, scale=scale):
                dsc, dpw = carry
                base = pl.multiple_of(ci * c, c)
                dms = dms_ref[pl.ds(base, c), cols].astype(F32)
                dsc = dsc + jnp.sum(dms * mixed_ref[pl.ds(base, c), cols].astype(F32), axis=0, keepdims=True)
                dmix = (dms * scale).astype(BF16)
                dpw = dpw + _dot(pooled_ref[pl.ds(base, c), cols], dmix, TN)
                dpooled = _dot(dmix, wmat, NT)
                cnt = _pool_count(base, left, right, t, (c, POOL_GROUP))
                pad_ref[pl.ds(base + h, c), cols] = dpooled / cnt
                return dsc, dpw

            dsc, dpw = lax.fori_loop(0, n_chunks, first,
                                     (jnp.zeros((1, POOL_GROUP), F32), jnp.zeros((POOL_GROUP, POOL_GROUP), F32)))
            dsc_ref[:, cols] = dsc
            dpw_ref[g] = dpw

            def second(ci, carry, left=left, right=right, width=width, cols=cols):
                base = pl.multiple_of(ci * c, c)
                v = pad_ref[pl.ds(base, c + 2 * h), cols]
                win = _window_sum(v, width, right)[h:h + c]
                cnt = _pool_count(base, left, right, t, (c, POOL_GROUP))
                dp_ref[pl.ds(base, c), cols] = (win - v[h:h + c] * cnt).astype(BF16)
                return carry

            lax.fori_loop(0, n_chunks, second, 0)

    vm = BS(memory_space=pltpu.VMEM)
    return pl.pallas_call(
        body, out_shape=(SDS((t, D_POOL), BF16), SDS((1, D_POOL), F32), SDS((4, POOL_GROUP, POOL_GROUP), F32)),
        in_specs=[vm] * 5, out_specs=(vm, vm, vm),
        scratch_shapes=[pltpu.VMEM((t + 2 * h, D_POOL), F32)],
        name="pool_bwd", compiler_params=_params())(d_ms, mixed, pooled, pool_w, pool_scale)


def _ssm_disc(ar, ai, ldt):
    def body(ar_ref, ai_ref, ldt_ref, abr_ref, abi_ref, qr_ref, qi_ref):
        a_r, a_i = ar_ref[...], ai_ref[...]
        dt = jnp.exp(ldt_ref[...])
        mag = jnp.exp(dt * a_r)
        ang = dt * a_i
        abr = mag * jnp.cos(ang)
        abi = mag * jnp.sin(ang)
        den = a_r * a_r + a_i * a_i
        nr = abr - 1.0
        abr_ref[...] = abr
        abi_ref[...] = abi
        qr_ref[...] = (nr * a_r + abi * a_i) / den
        qi_ref[...] = (abi * a_r - nr * a_i) / den

    vm = BS(memory_space=pltpu.VMEM)
    shape = SDS(ar.shape, F32)
    return pl.pallas_call(body, out_shape=(shape,) * 4, in_specs=[vm] * 3, out_specs=(vm,) * 4,
                          name="ssm_disc", compiler_params=_params())(ar, ai, ldt)


def _ssm_disc_bwd(ar, ai, ldt, d_abr, d_abi, d_qr, d_qi):
    def body(ar_ref, ai_ref, ldt_ref, gabr_ref, gabi_ref, gqr_ref, gqi_ref, dar_ref, dai_ref, dldt_ref):
        a_r, a_i = ar_ref[...], ai_ref[...]
        dt = jnp.exp(ldt_ref[...])
        mag = jnp.exp(dt * a_r)
        ang = dt * a_i
        cs, sn = jnp.cos(ang), jnp.sin(ang)
        abr, abi = mag * cs, mag * sn
        den = a_r * a_r + a_i * a_i
        nr = abr - 1.0
        qr = (nr * a_r + abi * a_i) / den
        qi = (abi * a_r - nr * a_i) / den
        gqr, gqi = gqr_ref[...], gqi_ref[...]
        g_nr_num = gqr / den
        g_ni_num = gqi / den
        g_den = -(gqr * qr + gqi * qi) / den
        g_nr = g_nr_num * a_r - g_ni_num * a_i
        g_abi = g_nr_num * a_i + g_ni_num * a_r
        d_ar = g_nr_num * nr + g_ni_num * abi + 2.0 * a_r * g_den
        d_ai = g_nr_num * abi - g_ni_num * nr + 2.0 * a_i * g_den
        g_abr = gabr_ref[...] + g_nr
        g_abi = gabi_ref[...] + g_abi
        g_mag = g_abr * cs + g_abi * sn
        g_ang = mag * (g_abi * cs - g_abr * sn)
        g_e = g_mag * mag
        d_ar = d_ar + g_e * dt
        d_ai = d_ai + g_ang * dt
        g_dt = g_e * a_r + g_ang * a_i
        dar_ref[...] = d_ar
        dai_ref[...] = d_ai
        dldt_ref[...] = jnp.sum(g_dt * dt, axis=1, keepdims=True)

    vm = BS(memory_space=pltpu.VMEM)
    return pl.pallas_call(body, out_shape=(SDS(ar.shape, F32), SDS(ar.shape, F32), SDS(ldt.shape, F32)),
                          in_specs=[vm] * 7, out_specs=(vm,) * 3, name="ssm_disc_bwd",
                          compiler_params=_params())(ar, ai, ldt, d_abr, d_abi, d_qr, d_qi)


def _ssm_bbar(qr, qi, br, bi):
    def body(qr_ref, qi_ref, br_ref, bi_ref, bbr_ref, bbi_ref):
        q_r, q_i, b_r, b_i = qr_ref[...], qi_ref[...], br_ref[...], bi_ref[...]
        bbr_ref[...] = q_r * b_r - q_i * b_i
        bbi_ref[...] = q_r * b_i + q_i * b_r

    vm = BS(memory_space=pltpu.VMEM)
    shape = SDS(br.shape, F32)
    return pl.pallas_call(body, out_shape=(shape, shape), in_specs=[vm] * 4, out_specs=(vm, vm),
                          name="ssm_bbar", compiler_params=_params())(qr, qi, br, bi)


def _ssm_bbar_bwd(qr, qi, br, bi, g_bbr, g_bbi):
    def body(qr_ref, qi_ref, br_ref, bi_ref, gr_ref, gi_ref, dqr_ref, dqi_ref, dbr_ref, dbi_ref):
        q_r, q_i, b_r, b_i = qr_ref[...], qi_ref[...], br_ref[...], bi_ref[...]
        g_r, g_i = gr_ref[...], gi_ref[...]
        dqr_ref[...] = jnp.sum(g_r * b_r + g_i * b_i, axis=1, keepdims=True)
        dqi_ref[...] = jnp.sum(g_i * b_r - g_r * b_i, axis=1, keepdims=True)
        dbr_ref[...] = g_r * q_r + g_i * q_i
        dbi_ref[...] = g_i * q_r - g_r * q_i

    vm = BS(memory_space=pltpu.VMEM)
    return pl.pallas_call(
        body, out_shape=(SDS(qr.shape, F32), SDS(qr.shape, F32), SDS(br.shape, F32), SDS(br.shape, F32)),
        in_specs=[vm] * 6, out_specs=(vm,) * 4, name="ssm_bbar_bwd",
        compiler_params=_params())(qr, qi, br, bi, g_bbr, g_bbi)


SCAN_ROWS = 256


def _ssm_scan(name, inp, w1, a_r, a_i, w2, reverse):
    t = inp.shape[0]
    rows = SCAN_ROWS
    n = t // rows
    n_groups = rows // 8
    ch = SSM_CH
    at = (lambda i: (n - 1 - i, 0)) if reverse else (lambda i: (i, 0))

    def body(in_ref, w1_ref, ar_ref, ai_ref, w2_ref, st_ref, out_ref, cr_ref, ci_ref, k_ref):
        i = pl.program_id(0)

        @pl.when(i == 0)
        def _():
            ar8 = jnp.broadcast_to(ar_ref[...], (8, ch))
            ai8 = jnp.broadcast_to(ai_ref[...], (8, ch))
            row = lax.broadcasted_iota(jnp.int32, (8, ch), 0)
            rank = (7 - row) if reverse else row
            powers = [(ar8, ai8)]
            for _ in range(7):
                p_r, p_i = powers[-1]
                powers.append((p_r * ar8 - p_i * ai8, p_r * ai8 + p_i * ar8))
            zero = jnp.zeros((8, ch), F32)
            for slot, k in enumerate((1, 2, 4)):
                k_ref[2 * slot] = jnp.where(rank >= k, powers[k - 1][0], zero)
                k_ref[2 * slot + 1] = jnp.where(rank >= k, powers[k - 1][1], zero)
            carry_r, carry_i = zero, zero
            for j in range(8):
                carry_r = jnp.where(rank == j, powers[j][0], carry_r)
                carry_i = jnp.where(rank == j, powers[j][1], carry_i)
            k_ref[6] = carry_r
            k_ref[7] = carry_i
            cr_ref[...] = zero
            ci_ref[...] = zero

        st_ref[...] = _dot(in_ref[...], w1_ref[...])

        def group(gi, carry):
            c_r, c_i = carry
            g = (n_groups - 1 - gi) if reverse else gi
            r0 = pl.multiple_of(g * 8, 8)
            x_r = st_ref[pl.ds(r0, 8), 0:ch]
            x_i = st_ref[pl.ds(r0, 8), ch:2 * ch]
            for slot, k in enumerate((1, 2, 4)):
                shift = (8 - k) if reverse else k
                s_r = pltpu.roll(x_r, shift, 0)
                s_i = pltpu.roll(x_i, shift, 0)
                m_r, m_i = k_ref[2 * slot], k_ref[2 * slot + 1]
                x_r, x_i = x_r + m_r * s_r - m_i * s_i, x_i + m_r * s_i + m_i * s_r
            p_r, p_i = k_ref[6], k_ref[7]
            x_r, x_i = x_r + p_r * c_r - p_i * c_i, x_i + p_r * c_i + p_i * c_r
            st_ref[pl.ds(r0, 8), 0:ch] = x_r
            st_ref[pl.ds(r0, 8), ch:2 * ch] = x_i
            last = 0 if reverse else 7
            return (jnp.broadcast_to(x_r[last:last + 1, :], (8, ch)), jnp.broadcast_to(x_i[last:last + 1, :], (8, ch)))

        c_r, c_i = lax.fori_loop(0, n_groups, group, (cr_ref[...], ci_ref[...]))
        cr_ref[...] = c_r
        ci_ref[...] = c_i
        out_ref[...] = _dot(st_ref[...], w2_ref[...])

    return pl.pallas_call(
        body, out_shape=(SDS((t, 2 * ch), F32), SDS((t, D_SSM), F32)), grid=(n,),
        in_specs=[BS((rows, D_SSM), at), BS((D_SSM, 2 * ch), lambda i: (0, 0)), BS((1, ch), lambda i: (0, 0)),
                  BS((1, ch), lambda i: (0, 0)), BS((2 * ch, D_SSM), lambda i: (0, 0))],
        out_specs=(BS((rows, 2 * ch), at), BS((rows, D_SSM), at)),
        scratch_shapes=[pltpu.VMEM((8, ch), F32), pltpu.VMEM((8, ch), F32), pltpu.VMEM((8, 8, ch), F32)],
        name=name, compiler_params=_params("arbitrary"))(inp, w1, a_r, a_i, w2)


DA_ROWS = 512


def _ssm_da(name, lam, states, reverse):
    t = lam.shape[0]
    rows = DA_ROWS
    n = t // rows
    nb = rows // 8
    ch = SSM_CH
    if reverse:
        halo_at = lambda i: (jnp.minimum((i + 1) * nb, t // 8 - 1), 0)
    else:
        halo_at = lambda i: (jnp.maximum(i * nb - 1, 0), 0)

    def body(lam_ref, x_ref, halo_ref, dr_ref, di_ref):
        i = pl.program_id(0)

        @pl.when(i == 0)
        def _():
            dr_ref[...] = jnp.zeros_like(dr_ref)
            di_ref[...] = jnp.zeros_like(di_ref)

        row = lax.broadcasted_iota(jnp.int32, (rows, ch), 0)
        if reverse:
            edge, shift, h_row, live = rows - 1, rows - 1, 0, i < n - 1
        else:
            edge, shift, h_row, live = 0, 1, 7, i > 0

        def neighbour(lo):
            halo = jnp.where(live, halo_ref[h_row:h_row + 1, lo:lo + ch], 0.0)
            return jnp.where(row == edge, jnp.broadcast_to(halo, (rows, ch)), pltpu.roll(x_ref[:, lo:lo + ch], shift, 0))

        xp_r, xp_i = neighbour(0), neighbour(ch)
        l_r, l_i = lam_ref[:, 0:ch], lam_ref[:, ch:2 * ch]
        dr_ref[...] += jnp.sum(l_r * xp_r + l_i * xp_i, axis=0, keepdims=True)
        di_ref[...] += jnp.sum(l_i * xp_r - l_r * xp_i, axis=0, keepdims=True)

    blk = BS((rows, 2 * ch), lambda i: (i, 0))
    vec = BS((1, ch), lambda i: (0, 0))
    return pl.pallas_call(
        body, out_shape=(SDS((1, ch), F32), SDS((1, ch), F32)), grid=(n,),
        in_specs=[blk, blk, BS((8, 2 * ch), halo_at)], out_specs=(vec, vec),
        name=name, compiler_params=_params("arbitrary"))(lam, states, states)


GELU_C = math.sqrt(2.0 / math.pi)
GELU_K = 0.044715


def _ssm_combine(proj, y_dirs, d_skip, tm):
    t = proj.shape[0]

    def body(s_ref, y_ref, d_ref, yt_ref, g_ref):
        y = s_ref[...] * d_ref[...] + y_ref[0] + y_ref[1]
        yt_ref[...] = y
        th = jnp.tanh(GELU_C * (y + GELU_K * y * y * y))
        g_ref[...] = (0.5 * y * (1.0 + th)).astype(BF16)

    blk = BS((tm, D_SSM), lambda i: (i, 0))
    return pl.pallas_call(
        body, out_shape=(SDS((t, D_SSM), F32), SDS((t, D_SSM), BF16)), grid=(t // tm,),
        in_specs=[BS((tm, D_SSM), lambda i: (i, D_POOL // D_SSM)), BS((2, tm, D_SSM), lambda i: (0, i, 0)),
                  BS((1, D_SSM), lambda i: (0, 0))],
        out_specs=(blk, blk), name="ssm_combine", compiler_params=_params("parallel"))(proj, y_dirs, d_skip)


def _ssm_ds(proj, d_yt, du_dirs, d_skip, tm):
    t = proj.shape[0]

    def body(s_ref, dy_ref, du_ref, d_ref, ds_ref, dd_ref):
        i = pl.program_id(0)
        dy = dy_ref[...]
        ds_ref[...] = (dy * d_ref[...] + du_ref[0] + du_ref[1]).astype(BF16)

        @pl.when(i == 0)
        def _():
            dd_ref[...] = jnp.zeros_like(dd_ref)

        dd_ref[...] += jnp.sum(dy * s_ref[...], axis=0, keepdims=True)

    blk = BS((tm, D_SSM), lambda i: (i, 0))
    vec = BS((1, D_SSM), lambda i: (0, 0))
    return pl.pallas_call(
        body, out_shape=(SDS((t, D_SSM), BF16), SDS((1, D_SSM), F32)), grid=(t // tm,),
        in_specs=[BS((tm, D_SSM), lambda i: (i, D_POOL // D_SSM)), blk, BS((2, tm, D_SSM), lambda i: (0, i, 0)), vec],
        out_specs=(blk, vec), name="ssm_ds", compiler_params=_params("arbitrary"))(proj, d_yt, du_dirs, d_skip)


GP_BLOCK = (D_POOL + D_SSM) // 256
GS_BLOCK = GP_BLOCK + D_MODEL // 256


def _merge_specs(tm):
    return [BS((tm, D_POOL), lambda s, i: (i, 0)), BS((tm, D_SSM), lambda s, i: (i, 0)),
            BS((None, D_POOL, 256), lambda s, i: (s, 0, 0)), BS((None, D_SSM, 256), lambda s, i: (s, 2, 0)),
            BS((None, D_SSM, 256), lambda s, i: (s, 3, 0)),
            BS((tm, 256), lambda s, i: (i, GP_BLOCK + s)), BS((tm, 256), lambda s, i: (i, GS_BLOCK + s))]


def _mixer_merge(ms, yssm, w_e, proj, tm):
    t = ms.shape[0]

    def body(ms_ref, y_ref, wpp_ref, wgv_ref, wgg_ref, gp_ref, gs_ref, o_ref):
        zp = _dot(ms_ref[...], wpp_ref[...])
        yv = y_ref[...]
        zv = _dot(yv, wgv_ref[...])
        zg = _dot(yv, wgg_ref[...])
        o_ref[...] = (_sigmoid(gp_ref[...]) * zp + _sigmoid(gs_ref[...]) * zv * _sigmoid(zg)).astype(BF16)

    col = BS((tm, 256), lambda s, i: (i, s))
    return pl.pallas_call(
        body, out_shape=SDS((t, D_MODEL), BF16), grid=(N_SHARD, t // tm), in_specs=_merge_specs(tm), out_specs=col,
        name="mixer_merge", compiler_params=_params("parallel", "parallel"))(ms, yssm, w_e, w_e, w_e, proj, proj)


def _mixer_merge_bwd(ms, yssm, w_e, proj, dmerged, tm):
    t = ms.shape[0]

    def body(ms_ref, y_ref, wpp_ref, wgv_ref, wgg_ref, gp_ref, gs_ref, dm_ref,
             dgp_ref, dgs_ref, dzp_ref, dzv_ref, dzg_ref):
        zp = _dot(ms_ref[...], wpp_ref[...])
        yv = y_ref[...]
        zv = _dot(yv, wgv_ref[...])
        zg = _dot(yv, wgg_ref[...])
        dm = dm_ref[...].astype(F32)
        sp, ss, sg = _sigmoid(gp_ref[...]), _sigmoid(gs_ref[...]), _sigmoid(zg)
        dgp_ref[...] = (dm * zp * sp * (1.0 - sp)).astype(BF16)
        dgs_ref[...] = (dm * zv * sg * ss * (1.0 - ss)).astype(BF16)
        dzp_ref[...] = (dm * sp).astype(BF16)
        dz = dm * ss
        dzv_ref[...] = (dz * sg).astype(BF16)
        dzg_ref[...] = (dz * zv * sg * (1.0 - sg)).astype(BF16)

    col = BS((tm, 256), lambda s, i: (i, s))
    shape = SDS((t, D_MODEL), BF16)
    return pl.pallas_call(
        body, out_shape=(shape,) * 5, grid=(N_SHARD, t // tm), in_specs=_merge_specs(tm) + [col],
        out_specs=(col,) * 5, name="mixer_merge_bwd",
        compiler_params=_params("parallel", "parallel"))(ms, yssm, w_e, w_e, w_e, proj, proj, dmerged)


def _mixer_dw(ms, yssm, dzp, dzv, dzg, tm):
    t = ms.shape[0]
    n_t = t // tm

    def body(ms_ref, y_ref, dzp_ref, dzv_ref, dzg_ref, o_ref, acc):
        i = pl.program_id(1)

        @pl.when(i == 0)
        def _():
            acc[...] = jnp.zeros_like(acc)

        yv = y_ref[...]
        acc[0:D_POOL, :] += _dot(ms_ref[...], dzp_ref[...], TN)
        acc[D_POOL:D_POOL + D_SSM, :] += _dot(yv, dzv_ref[...], TN)
        acc[D_POOL + D_SSM:, :] += _dot(yv, dzg_ref[...], TN)

        @pl.when(i == n_t - 1)
        def _():
            o_ref[...] = acc[...].astype(BF16)

    col = BS((tm, 256), lambda s, i: (i, s))
    return pl.pallas_call(
        body, out_shape=SDS((N_SHARD, 1024, 256), BF16), grid=(N_SHARD, n_t),
        in_specs=[BS((tm, D_POOL), lambda s, i: (i, 0)), BS((tm, D_SSM), lambda s, i: (i, 0)), col, col, col],
        out_specs=BS((None, 1024, 256), lambda s, i: (s, 0, 0)), scratch_shapes=[pltpu.VMEM((1024, 256), F32)],
        name="mixer_dw", compiler_params=_params("parallel", "arbitrary"))(ms, yssm, dzp, dzv, dzg)


def _mixer_dx(dzp, dzv, dzg, w_e, y_total, tm):
    t = dzp.shape[0]

    def body(dzp_ref, dzv_ref, dzg_ref, wpp_ref, wgv_ref, wgg_ref, yt_ref, dms_ref, dy_ref, acc_ms, acc_y):
        s = pl.program_id(1)

        @pl.when(s == 0)
        def _():
            acc_ms[...] = jnp.zeros_like(acc_ms)
            acc_y[...] = jnp.zeros_like(acc_y)

        acc_ms[...] += _dot(dzp_ref[...], wpp_ref[...], NT)
        acc_y[...] += _dot(dzv_ref[...], wgv_ref[...], NT) + _dot(dzg_ref[...], wgg_ref[...], NT)

        @pl.when(s == N_SHARD - 1)
        def _():
            dms_ref[...] = acc_ms[...].astype(BF16)
            y = yt_ref[...]
            inner = GELU_C * (y + GELU_K * y * y * y)
            th = jnp.tanh(inner)
            dgelu = 0.5 * (1.0 + th) + 0.5 * y * (1.0 - th * th) * GELU_C * (1.0 + 3.0 * GELU_K * y * y)
            dy_ref[...] = acc_y[...] * dgelu

    col = BS((tm, 256), lambda i, s: (i, s))
    return pl.pallas_call(
        body, out_shape=(SDS((t, D_POOL), BF16), SDS((t, D_SSM), F32)), grid=(t // tm, N_SHARD),
        in_specs=[col, col, col, BS((None, D_POOL, 256), lambda i, s: (s, 0, 0)),
                  BS((None, D_SSM, 256), lambda i, s: (s, 2, 0)), BS((None, D_SSM, 256), lambda i, s: (s, 3, 0)),
                  BS((tm, D_SSM), lambda i, s: (i, 0))],
        out_specs=(BS((tm, D_POOL), lambda i, s: (i, 0)), BS((tm, D_SSM), lambda i, s: (i, 0))),
        scratch_shapes=[pltpu.VMEM((tm, D_POOL), F32), pltpu.VMEM((tm, D_SSM), F32)],
        name="mixer_dx", compiler_params=_params("parallel", "arbitrary"))(dzp, dzv, dzg, w_e, w_e, w_e, y_total)


def _attn_probs(q_h, k_h):
    s = _dot(q_h, k_h, NT) * (1.0 / math.sqrt(HEAD_DIM))
    e = jnp.exp(s - jnp.max(s, axis=-1, keepdims=True))
    return e / jnp.sum(e, axis=-1, keepdims=True)


def _attn_fwd(q, kv, tm):
    t = q.shape[0]
    m = kv.shape[0]

    def body(q_ref, kv_ref, o_ref):
        for hd in range(N_HEADS):
            lo = hd * HEAD_DIM
            p = _attn_probs(q_ref[:, lo:lo + HEAD_DIM], kv_ref[:, lo:lo + HEAD_DIM])
            o_ref[:, lo:lo + HEAD_DIM] = _dot(p, kv_ref[:, D_MODEL + lo:D_MODEL + lo + HEAD_DIM]).astype(BF16)

    return pl.pallas_call(
        body, out_shape=SDS((t, D_MODEL), BF16), grid=(t // tm,),
        in_specs=[BS((tm, D_MODEL), lambda i: (i, 0)), BS((m, 2 * D_MODEL), lambda i: (0, 0))],
        out_specs=BS((tm, D_MODEL), lambda i: (i, 0)), name="attn_fwd", compiler_params=_params("parallel"))(q, kv)


def _attn_bwd(q, kv, d_o, tm):
    t = q.shape[0]
    m = kv.shape[0]

    def body(q_ref, kv_ref, do_ref, dq_ref, dkv_ref):
        i = pl.program_id(0)

        @pl.when(i == 0)
        def _():
            dkv_ref[...] = jnp.zeros_like(dkv_ref)

        for hd in range(N_HEADS):
            lo = hd * HEAD_DIM
            q_h = q_ref[:, lo:lo + HEAD_DIM]
            k_h = kv_ref[:, lo:lo + HEAD_DIM]
            v_h = kv_ref[:, D_MODEL + lo:D_MODEL + lo + HEAD_DIM]
            do_h = do_ref[:, lo:lo + HEAD_DIM]
            p = _attn_probs(q_h, k_h)
            dkv_ref[:, D_MODEL + lo:D_MODEL + lo + HEAD_DIM] += _dot(p, do_h, TN)
            dp = _dot(do_h, v_h, NT)
            ds = p * (dp - jnp.sum(dp * p, axis=-1, keepdims=True)) * (1.0 / math.sqrt(HEAD_DIM))
            dq_ref[:, lo:lo + HEAD_DIM] = _dot(ds, k_h).astype(BF16)
            dkv_ref[:, lo:lo + HEAD_DIM] += _dot(ds, q_h, TN)

    row = BS((tm, D_MODEL), lambda i: (i, 0))
    full = BS((m, 2 * D_MODEL), lambda i: (0, 0))
    return pl.pallas_call(
        body, out_shape=(SDS((t, D_MODEL), BF16), SDS((m, 2 * D_MODEL), F32)), grid=(t // tm,),
        in_specs=[row, full, row], out_specs=(row, full), name="attn_bwd",
        compiler_params=_params("arbitrary"))(q, kv, d_o)


TRANSPOSED = ("ffn1_w_gate", "ffn1_w_up", "ffn2_w_gate", "ffn2_w_up", "w_in")
GATHER_GROUPS = (("ffn1_w_gate", "ffn1_w_up", "ffn1_w_down"),
                 ("w_in", "w_mix_out", "w_q", "w_xo"),
                 ("w_kv",),
                 ("w_pool_proj", "w_glu_val", "w_glu_gate"),
                 ("ffn2_w_gate", "ffn2_w_up", "ffn2_w_down"))
REDUCE_GROUPS = (("ffn2_w_gate", "ffn2_w_up", "ffn2_w_down"), ("w_xo",), ("w_q",), ("w_kv",), ("w_mix_out",),
                 ("w_pool_proj", "w_glu_val", "w_glu_gate"), ("w_in",), ("ffn1_w_gate", "ffn1_w_up", "ffn1_w_down"))
SMALL = ("ffn1_norm", "mix_norm", "pool_w", "pool_scale", "ssm_a_re", "ssm_a_im", "ssm_log_dt", "ssm_b_re",
         "ssm_b_im", "ssm_c_re", "ssm_c_im", "ssm_d", "xattn_norm", "mem_norm", "ffn2_norm", "final_norm")
WEIGHTS = ("ffn1_norm", "ffn1_w_gate", "ffn1_w_up", "ffn1_w_down", "mix_norm", "w_in", "pool_w", "pool_scale",
           "w_pool_proj", "ssm_a_re", "ssm_a_im", "ssm_log_dt", "ssm_b_re", "ssm_b_im", "ssm_c_re", "ssm_c_im",
           "ssm_d", "w_glu_val", "w_glu_gate", "w_mix_out", "xattn_norm", "mem_norm", "w_q", "w_kv", "w_xo",
           "ffn2_norm", "ffn2_w_gate", "ffn2_w_up", "ffn2_w_down", "final_norm")


def _block_diag_in(bb):
    eye = jnp.eye(SSM_GROUPS, dtype=bb.dtype)
    return jnp.einsum("dgph,gk->dghkp", bb, eye).reshape(2, D_SSM, SSM_CH)


def _block_diag_out(cc):
    eye = jnp.eye(SSM_GROUPS, dtype=cc.dtype)
    return jnp.einsum("dghp,gk->dgpkh", cc, eye).reshape(2, SSM_CH, D_SSM)


def _diag_blocks_in(m):
    return jnp.einsum("dghgp->dgph", m.reshape(2, SSM_GROUPS, SSM_GROUP, SSM_GROUPS, SSM_STATE))


def _diag_blocks_out(m):
    return jnp.einsum("dgpgh->dghp", m.reshape(2, SSM_GROUPS, SSM_STATE, SSM_GROUPS, SSM_GROUP))


def _device_step(x, mem, target, gw, sp):
    t = x.shape[0]
    tm = min(TM, t)
    w_f1, w_mix, w_kv, w_e, w_f2 = gw
    w_f1 = w_f1.reshape(N_SHARD, 3, FF_SH, D_MODEL)
    w_f2 = w_f2.reshape(N_SHARD, 3, FF_SH, D_MODEL)
    w_d = w_kv[:, None]
    w_in_t = w_mix[:, :FF_SH].reshape(D_FF, D_MODEL)
    w_mo, w_q, w_xo = (w_mix[:, FF_SH + 256 * k:FF_SH + 256 * (k + 1)].reshape(D_MODEL, D_MODEL) for k in range(3))
    g = {}

    u1 = _rmsnorm("norm_ffn1", x, sp["ffn1_norm"], tm)
    g1, up1, a1 = _ffn_up("ffn1_up", u1, w_f1, tm)
    h1 = _ffn_down("ffn1_down", a1, w_f1, x, tm)

    u2 = _rmsnorm("norm_mix", h1, sp["mix_norm"], tm)
    proj = _mm("mix_in", [(u2, BS((tm, D_MODEL), lambda j, i: (i, 0)), w_in_t, BS((D_FF // 2, D_MODEL), lambda j, i: (j, 0)), NT)],
               grid=(2, t // tm), out_shape=SDS((t, D_FF), F32), out_spec=BS((tm, D_FF // 2), lambda j, i: (i, j)))
    pooled, mixed, ms = _pool_fwd(proj, sp["pool_w"][0], sp["pool_scale"])

    ar = sp["ssm_a_re"].reshape(2 * SSM_GROUPS, SSM_STATE)
    ai = sp["ssm_a_im"].reshape(2 * SSM_GROUPS, SSM_STATE)
    ldt = sp["ssm_log_dt"].reshape(2 * SSM_GROUPS, 1)
    abr, abi, qr, qi = _ssm_disc(ar, ai, ldt)
    b_r = sp["ssm_b_re"].reshape(2 * SSM_CH, SSM_GROUP)
    b_i = sp["ssm_b_im"].reshape(2 * SSM_CH, SSM_GROUP)
    qr_col, qi_col = qr.reshape(2 * SSM_CH, 1), qi.reshape(2 * SSM_CH, 1)
    bbr, bbi = _ssm_bbar(qr_col, qi_col, b_r, b_i)
    shape_b = (2, SSM_GROUPS, SSM_STATE, SSM_GROUP)
    b_mat = jnp.concatenate([_block_diag_in(bbr.reshape(shape_b)), _block_diag_in(bbi.reshape(shape_b))], axis=-1).astype(BF16)
    c_mat = jnp.concatenate([_block_diag_out(sp["ssm_c_re"][0]), -_block_diag_out(sp["ssm_c_im"][0])], axis=1).astype(BF16)
    b_mat_t = jnp.swapaxes(b_mat, 1, 2)
    c_mat_t = jnp.swapaxes(c_mat, 1, 2)
    a_r = abr.reshape(2, 1, SSM_CH)
    a_i = abi.reshape(2, 1, SSM_CH)

    s_in =proj[:, D_POOL:D_POOL + D_SSM].astype(BF16)
    states, y_dirs = [], []
    for dr in range(2):
        st, yd = _ssm_scan(f"ssm_scan_fwd{dr}", s_in, b_mat[dr], a_r[dr], a_i[dr], c_mat[dr], reverse=(dr == 1))
        states.append(st)
        y_dirs.append(yd)
    y_total, yssm = _ssm_combine(proj, jnp.stack(y_dirs), sp["ssm_d"], tm)

    merged = _mixer_merge(ms, yssm, w_e, proj, tm)
    h2 = _plain_mm("mix_out", merged, w_mo, NN, F32, tm, resid=h1)

    u3 = _rmsnorm("norm_xattn", h2, sp["xattn_norm"], tm)
    mem_n = _rmsnorm("norm_mem", mem, sp["mem_norm"], mem.shape[0])
    q = _plain_mm("attn_q", u3, w_q, NN, BF16, tm)
    n_mem = mem.shape[0]
    kv = _mm("attn_kv", [(mem_n, BS((n_mem, D_MODEL), lambda s: (0, 0)), w_d, BS((None, None, D_MODEL, 512), lambda s: (s, 0, 0, 0)), NN)],
             grid=(N_SHARD,), out_shape=SDS((n_mem, 2 * D_MODEL), BF16), out_spec=BS((n_mem, 512), lambda s: (0, s)))
    o = _attn_fwd(q, kv, tm)
    h3 = _plain_mm("attn_out", o, w_xo, NN, F32, tm, resid=h2)

    u4 = _rmsnorm("norm_ffn2", h3, sp["ffn2_norm"], tm)
    g2, up2, a2 = _ffn_up("ffn2_up", u4, w_f2, tm)
    h4 = _ffn_down("ffn2_down", a2, w_f2, h3, tm)

    loss, dh4, dh4_b, g["final_norm"] = _loss_head(h4, sp["final_norm"].reshape(1, D_MODEL), target, tm)

    dg2, dup2 = _ffn_bwd_act("ffn2_bwd_act", dh4_b, w_f2, g2, up2, tm)
    dw_f2 = _ffn_dw("ffn2_dw", u4, dg2, dup2, a2, dh4_b, tm)
    du4 = _ffn_dx("ffn2_dx", dg2, dup2, w_f2, tm)
    dh3, dh3_b, g["ffn2_norm"] = _rmsnorm_bwd("norm_ffn2_bwd", h3, sp["ffn2_norm"], du4, dh4, tm)

    d_o = _plain_mm("attn_out_dx", dh3_b, w_xo, NT, BF16, tm)
    dw_xo = _dw_mm("attn_out_dw", o, dh3_b, tm)
    dq, dkv = _attn_bwd(q, kv, d_o, tm)
    dw_q = _dw_mm("attn_q_dw", u3, dq, tm)
    du3 = _plain_mm("attn_q_dx", dq, w_q, NT, F32, tm)
    dw_kv = _mm("attn_kv_dw", [(mem_n, BS((n_mem, D_MODEL), lambda s: (0, 0)), dkv, BS((n_mem, 512), lambda s: (0, s)), TN)],
                grid=(N_SHARD,), out_shape=SDS((N_SHARD, D_MODEL, 512), BF16), out_spec=BS((None, D_MODEL, 512), lambda s: (s, 0, 0)))
    dmem_n = _mm("attn_kv_dx", [(dkv, BS((n_mem, 512), lambda s: (0, s)), w_d, BS((None, None, D_MODEL, 512), lambda s: (s, 0, 0, 0)), NT)],
                 grid=(N_SHARD,), red_axis=0, out_shape=SDS((n_mem, D_MODEL), F32), out_spec=BS((n_mem, D_MODEL), lambda s: (0, 0)))
    _, _, g["mem_norm"] = _rmsnorm_bwd("norm_mem_bwd", mem, sp["mem_norm"], dmem_n, None, n_mem)
    dh2, dh2_b, g["xattn_norm"] = _rmsnorm_bwd("norm_xattn_bwd", h2, sp["xattn_norm"], du3, dh3, tm)

    dmerged = _plain_mm("mix_out_dx", dh2_b, w_mo, NT, BF16, tm)
    dw_mo = _dw_mm("mix_out_dw", merged, dh2_b, tm)
    d_gp, d_gs, dzp, dzv, dzg = _mixer_merge_bwd(ms, yssm, w_e, proj, dmerged, tm)
    dw_e = _mixer_dw(ms, yssm, dzp, dzv, dzg, tm)
    d_ms, d_yt = _mixer_dx(dzp, dzv, dzg, w_e, y_total, tm)
    dp, d_scale, d_pw = _pool_bwd(d_ms, mixed, pooled, sp["pool_w"][0], sp["pool_scale"])
    g["pool_scale"] = d_scale
    g["pool_w"] = d_pw[None]

    d_yt_b = d_yt.astype(BF16)
    du_dirs, d_abr, d_abi, d_cm, d_bm = [], [], [], [], []
    for dr in range(2):
        lam, du = _ssm_scan(f"ssm_scan_bwd{dr}", d_yt_b, c_mat_t[dr], a_r[dr], -a_i[dr], b_mat_t[dr], reverse=(dr == 0))
        du_dirs.append(du)
        da_r, da_i = _ssm_da(f"ssm_da{dr}", lam, states[dr], reverse=(dr == 1))
        d_abr.append(da_r)
        d_abi.append(da_i)
        d_cm.append(_dw_mm(f"ssm_dc{dr}", states[dr], d_yt_b, tm, F32))
        d_bm.append(_dw_mm(f"ssm_db{dr}", s_in, lam, tm, F32))
    d_cm = jnp.stack(d_cm)
    d_bm = jnp.stack(d_bm)
    g["ssm_c_re"] = _diag_blocks_out(d_cm[:, :SSM_CH])[None]
    g["ssm_c_im"] = -_diag_blocks_out(d_cm[:, SSM_CH:])[None]
    g_bbr = _diag_blocks_in(d_bm[:, :, :SSM_CH]).reshape(2 * SSM_CH, SSM_GROUP)
    g_bbi = _diag_blocks_in(d_bm[:, :, SSM_CH:]).reshape(2 * SSM_CH, SSM_GROUP)
    d_qr, d_qi, d_br, d_bi = _ssm_bbar_bwd(qr_col, qi_col, b_r, b_i, g_bbr, g_bbi)
    g["ssm_b_re"] = d_br.reshape(sp["ssm_b_re"].shape)
    g["ssm_b_im"] = d_bi.reshape(sp["ssm_b_im"].shape)
    d_ar, d_ai, d_ldt = _ssm_disc_bwd(ar, ai, ldt, jnp.stack(d_abr).reshape(ar.shape), jnp.stack(d_abi).reshape(ar.shape),
                                      d_qr.reshape(ar.shape), d_qi.reshape(ar.shape))
    g["ssm_a_re"] = d_ar.reshape(sp["ssm_a_re"].shape)
    g["ssm_a_im"] = d_ai.reshape(sp["ssm_a_im"].shape)
    g["ssm_log_dt"] = d_ldt.reshape(sp["ssm_log_dt"].shape)
    ds, g["ssm_d"] = _ssm_ds(proj, d_yt, jnp.stack(du_dirs), sp["ssm_d"], tm)

    d_proj = jnp.concatenate([dp, ds, d_gp, d_gs], axis=1)
    dw_in_t = _mm("mix_in_dw", [(d_proj, BS((tm, D_FF // 2), lambda j, i: (i, j)), u2, BS((tm, D_MODEL), lambda j, i: (i, 0)), TN)],
                  grid=(2, t // tm), red_axis=1, out_shape=SDS((D_FF, D_MODEL), BF16), out_spec=BS((D_FF // 2, D_MODEL), lambda j, i: (j, 0)))
    du2 = _plain_mm("mix_in_dx", d_proj, w_in_t, NN, F32, tm)
    dh1, dh1_b, g["mix_norm"] = _rmsnorm_bwd("norm_mix_bwd", h1, sp["mix_norm"], du2, dh2, tm)

    dg1, dup1 = _ffn_bwd_act("ffn1_bwd_act", dh1_b, w_f1, g1, up1, tm)
    dw_f1 = _ffn_dw("ffn1_dw", u1, dg1, dup1, a1, dh1_b, tm)
    du1 = _ffn_dx("ffn1_dx", dg1, dup1, w_f1, tm)
    grad_x, _, g["ffn1_norm"] = _rmsnorm_bwd("norm_ffn1_bwd", x, sp["ffn1_norm"], du1, dh1, tm)

    square = (N_SHARD, D_MODEL // N_SHARD, D_MODEL)
    big = (dw_f2.reshape(N_SHARD, 3 * FF_SH, D_MODEL), dw_xo.reshape(square), dw_q.reshape(square), dw_kv,
           dw_mo.reshape(square), dw_e, dw_in_t.reshape(N_SHARD, FF_SH, D_MODEL), dw_f1.reshape(N_SHARD, 3 * FF_SH, D_MODEL))
    g["final_norm"] = g["final_norm"].reshape(D_MODEL)
    return loss, grad_x, big, g


def _mesh_place():
    x, y, c = lax.axis_index("x"), lax.axis_index("y"), lax.axis_index("c")
    chips = [(1 - x, y), (x, 1 - y), (1 - x, 1 - y)]
    return x, y, c, chips


def _remote(src, dst, send_sems, recv_sems, k, to):
    return pltpu.make_async_remote_copy(src_ref=src, dst_ref=dst, send_sem=send_sems.at[k], recv_sem=recv_sems.at[k],
                                        device_id=to, device_id_type=MESH)


def _allgather_shards(shards):
    n = len(shards)

    def body(*refs):
        ins, outs = refs[:n], refs[n:2 * n]
        send_sems, recv_sems, local_sems = refs[2 * n:]
        x, y, c, chips = _mesh_place()
        me = 2 * x + y
        sibling = (x, y, 1 - c)
        started, local_copies = [], []
        for k in range(n):
            half = shards[k].shape[0] // 2
            mine = pl.ds(pl.multiple_of(c * half, 16), half)
            local = pltpu.make_async_copy(ins[k], outs[k].at[me], local_sems.at[k])
            local.start()
            local_copies.append(local)
            for j, (px, py) in enumerate(chips):
                cp = _remote(ins[k].at[mine, :], outs[k].at[me, mine, :], send_sems, recv_sems, 6 * k + j, (px, py, c))
                cp.start()
                started.append(cp)
        for k in range(n):
            half = shards[k].shape[0] // 2
            mine = pl.ds(pl.multiple_of(c * half, 16), half)
            for j, (px, py) in enumerate(chips):
                blk = outs[k].at[2 * px + py, mine, :]
                _remote(blk, blk, send_sems, recv_sems, 6 * k + j, (px, py, c)).wait_recv()
                fwd = _remote(blk, blk, send_sems, recv_sems, 6 * k + 3 + j, sibling)
                fwd.start()
                started.append(fwd)
        for k in range(n):
            half = shards[k].shape[0] // 2
            theirs = pl.ds(pl.multiple_of((1 - c) * half, 16), half)
            for j, (px, py) in enumerate(chips):
                blk = outs[k].at[2 * px + py, theirs, :]
                _remote(blk, blk, send_sems, recv_sems, 6 * k + 3 + j, sibling).wait_recv()
        for cp in started:
            cp.wait_send()
        for cp in local_copies:
            cp.wait()

    hbm = BS(memory_space=pl.ANY)
    return pl.pallas_call(
        body, out_shape=tuple(SDS((N_SHARD,) + s.shape, s.dtype) for s in shards),
        in_specs=[hbm] * n, out_specs=(hbm,) * n,
        scratch_shapes=[pltpu.SemaphoreType.DMA((6 * n,)), pltpu.SemaphoreType.DMA((6 * n,)), pltpu.SemaphoreType.DMA((n,))],
        name="allgather_weights", compiler_params=_params())(*shards)


def _sibling_swap_halves(grads):
    n = len(grads)

    def body(*refs):
        ins, outs = refs[:n], refs[n:2 * n]
        send_sems, recv_sems = refs[2 * n:]
        x, y, c, _ = _mesh_place()
        sibling = (x, y, 1 - c)
        copies = []
        for k in range(n):
            half = grads[k].shape[1] // 2
            theirs = pl.ds(pl.multiple_of((1 - c) * half, 16), half)
            cp = _remote(ins[k].at[:, theirs, :], outs[k], send_sems, recv_sems, k, sibling)
            cp.start()
            copies.append(cp)
        for cp in copies:
            cp.wait_recv()
        for cp in copies:
            cp.wait_send()

    hbm = BS(memory_space=pl.ANY)
    return pl.pallas_call(
        body, out_shape=tuple(SDS((g.shape[0], g.shape[1] // 2, g.shape[2]), g.dtype) for g in grads),
        in_specs=[hbm] * n, out_specs=(hbm,) * n,
        scratch_shapes=[pltpu.SemaphoreType.DMA((n,)), pltpu.SemaphoreType.DMA((n,))],
        name="reduce_sibling_send", compiler_params=_params())(*grads)


def _row_tile(rows, cap=512):
    return max(r for r in range(16, cap + 1, 16) if rows % r == 0)


def _chip_presum(k, grad, got, c_idx):
    n_sh, rows, cols = grad.shape
    half = rows // 2
    tr = _row_tile(half)
    grad4 = grad.reshape(n_sh, 2, half, cols)

    def body(c_ref, a_ref, b_ref, o_ref):
        o_ref[...] = (a_ref[...].astype(F32) + b_ref[...].astype(F32)).astype(o_ref.dtype)

    return pl.pallas_call(
        body, out_shape=SDS((n_sh, half, cols), BF16),
        grid_spec=pltpu.PrefetchScalarGridSpec(
            num_scalar_prefetch=1, grid=(n_sh, half // tr),
            in_specs=[BS((None, None, tr, cols), lambda s, i, c_ref: (s, c_ref[0], i, 0)),
                      BS((None, tr, cols), lambda s, i, c_ref: (s, i, 0))],
            out_specs=BS((None, tr, cols), lambda s, i, c_ref: (s, i, 0))),
        name=f"reduce_presum{k}", compiler_params=_params("parallel", "parallel"))(c_idx, grad4, got)


def _chip_exchange(parts):
    n = len(parts)

    def body(*refs):
        ins, outs = refs[:n], refs[n:2 * n]
        send_sems, recv_sems = refs[2 * n:]
        x, y, c, chips = _mesh_place()
        copies = []
        for k in range(n):
            for j, (px, py) in enumerate(chips):
                cp = _remote(ins[k].at[2 * px + py], outs[k].at[j], send_sems, recv_sems, 3 * k + j, (px, py, c))
                cp.start()
                copies.append(cp)
        for cp in copies:
            cp.wait_recv()
        for cp in copies:
            cp.wait_send()

    hbm = BS(memory_space=pl.ANY)
    return pl.pallas_call(
        body, out_shape=tuple(SDS((3,) + p.shape[1:], p.dtype) for p in parts),
        in_specs=[hbm] * n, out_specs=(hbm,) * n,
        scratch_shapes=[pltpu.SemaphoreType.DMA((3 * n,)), pltpu.SemaphoreType.DMA((3 * n,))],
        name="reduce_chip_exchange", compiler_params=_params())(*parts)


def _chip_sum(k, part, got, place):
    _, half, cols = part.shape
    tr = _row_tile(half)
    n_t = half // tr

    def body(place_ref, a_ref, b_ref, o_ref):
        acc = a_ref[...].astype(F32)
        for j in range(3):
            acc = acc + b_ref[j].astype(F32)
        o_ref[...] = acc

    return pl.pallas_call(
        body, out_shape=SDS((2 * half, cols), F32),
        grid_spec=pltpu.PrefetchScalarGridSpec(
            num_scalar_prefetch=1, grid=(n_t,),
            in_specs=[BS((None, tr, cols), lambda i, place_ref: (place_ref[0], i, 0)),
                      BS((3, tr, cols), lambda i, place_ref: (0, i, 0))],
            out_specs=BS((tr, cols), lambda i, place_ref: (place_ref[1] * n_t + i, 0))),
        name=f"reduce_sum{k}", compiler_params=_params("parallel"))(place, part, got)


def _sibling_join_halves(fulls):
    n = len(fulls)

    def body(*refs):
        outs = refs[n:2 * n]
        send_sems, recv_sems = refs[2 * n:]
        x, y, c, _ = _mesh_place()
        sibling = (x, y, 1 - c)
        sent = []
        for k in range(n):
            half = fulls[k].shape[0] // 2
            mine = outs[k].at[pl.ds(pl.multiple_of(c * half, 8), half), :]
            cp = _remote(mine, mine, send_sems, recv_sems, k, sibling)
            cp.start()
            sent.append(cp)
        for k in range(n):
            half = fulls[k].shape[0] // 2
            theirs = outs[k].at[pl.ds(pl.multiple_of((1 - c) * half, 8), half), :]
            _remote(theirs, theirs, send_sems, recv_sems, k, sibling).wait_recv()
        for cp in sent:
            cp.wait_send()

    hbm = BS(memory_space=pl.ANY)
    return pl.pallas_call(
        body, out_shape=tuple(SDS(f.shape, f.dtype) for f in fulls),
        in_specs=[hbm] * n, out_specs=(hbm,) * n, input_output_aliases={k: k for k in range(n)},
        scratch_shapes=[pltpu.SemaphoreType.DMA((n,)), pltpu.SemaphoreType.DMA((n,))],
        name="reduce_sibling_join", compiler_params=_params())(*fulls)


N_DEV = 8


def _allreduce_small(part):
    rows, lanes = part.shape

    def body(x_ref, out_ref, all_ref, send_sems, recv_sems, local_sem):
        x, y, c, chips = _mesh_place()
        me, sibling = (x, y, c), (x, y, 1 - c)

        def blk(px, py, pc):
            return all_ref.at[pl.ds(pl.multiple_of((4 * px + 2 * py + pc) * rows, 8), rows), :]

        def copy(k, block, to, src=None):
            return _remote(blk(*block) if src is None else src, blk(*block), send_sems, recv_sems, k, to)

        mine = pltpu.make_async_copy(x_ref, blk(*me), local_sem)
        mine.start()
        first = [copy(0, me, sibling, src=x_ref)]
        first += [copy(1 + j, me, (*chip, c), src=x_ref) for j, chip in enumerate(chips)]
        for cp in first:
            cp.start()
        passed = [copy(4 + j, (*chip, c), sibling) for j, chip in enumerate(chips)]
        for j, chip in enumerate(chips):
            copy(1 + j, (*chip, c), me).wait_recv()
            passed[j].start()
        copy(0, sibling, me).wait_recv()
        for j, chip in enumerate(chips):
            copy(4 + j, (*chip, 1 - c), me).wait_recv()
        for cp in first + passed:
            cp.wait_send()
        mine.wait()
        acc = all_ref[pl.ds(0, rows), :]
        for dev in range(1, N_DEV):
            acc = acc + all_ref[pl.ds(dev * rows, rows), :]
        out_ref[...] = acc

    vm = BS(memory_space=pltpu.VMEM)
    return pl.pallas_call(
        body, out_shape=SDS((rows, lanes), F32), in_specs=[vm], out_specs=vm,
        scratch_shapes=[pltpu.VMEM((N_DEV * rows, lanes), F32), pltpu.SemaphoreType.DMA((7,)),
                        pltpu.SemaphoreType.DMA((7,)), pltpu.SemaphoreType.DMA],
        name="allreduce_small", compiler_params=_params())(part)


def _adamw(name, w, grad, row0, m, v):
    rows, cols = w.shape
    tr = rows if rows < 16 else _row_tile(rows, 256)
    bc1 = 1.0 - ADAM_B1 ** ADAM_STEP
    bc2 = 1.0 - ADAM_B2 ** ADAM_STEP

    def body(w_ref, g_ref, m_ref, v_ref, go_ref, d_ref, mo_ref, vo_ref):
        g = g_ref[...]
        m_new = ADAM_B1 * m_ref[...] + (1.0 - ADAM_B1) * g
        v_new = ADAM_B2 * v_ref[...] + (1.0 - ADAM_B2) * (g * g)
        go_ref[...] = g
        mo_ref[...] = m_new
        vo_ref[...] = v_new
        d_ref[...] = -ADAM_LR * ((m_new / bc1) / (jnp.sqrt(v_new / bc2) + ADAM_EPS) + ADAM_WD * w_ref[...])

    blk = BS((tr, cols), lambda i: (i, 0))
    shape = SDS((rows, cols), F32)
    return pl.pallas_call(
        body, out_shape=(shape,) * 4, grid=(rows // tr,),
        in_specs=[blk, BS((tr, cols), lambda i: (row0 // tr + i, 0)), blk, blk], out_specs=(blk,) * 4,
        name=name, compiler_params=_params("parallel"))(w, grad, m, v)


SMALL_LANES = 128


def _pack_small(parts):
    flat = jnp.concatenate([jnp.ravel(p) for p in parts])
    rows = -(-flat.shape[0] // (64 * SMALL_LANES)) * 64
    return jnp.pad(flat, (0, rows * SMALL_LANES - flat.shape[0])).reshape(rows, SMALL_LANES)


def _unpack_small(packed, like):
    flat = jnp.ravel(packed)
    out, at = [], 0
    for p in like:
        out.append(flat[at:at + p.size].reshape(p.shape))
        at += p.size
    return out


def kernel(x, mem, ffn1_norm, ffn1_w_gate, ffn1_w_up, ffn1_w_down, mix_norm, w_in, pool_w, pool_scale, w_pool_proj, ssm_a_re, ssm_a_im, ssm_log_dt, ssm_b_re, ssm_b_im, ssm_c_re, ssm_c_im, ssm_d, w_glu_val, w_glu_gate, w_mix_out, xattn_norm, mem_norm, w_q, w_kv, w_xo, ffn2_norm, ffn2_w_gate, ffn2_w_up, ffn2_w_down, final_norm, loss_target, m_ffn1_norm, m_ffn1_w_gate, m_ffn1_w_up, m_ffn1_w_down, m_mix_norm, m_w_in, m_pool_w, m_pool_scale, m_w_pool_proj, m_ssm_a_re, m_ssm_a_im, m_ssm_log_dt, m_ssm_b_re, m_ssm_b_im, m_ssm_c_re, m_ssm_c_im, m_ssm_d, m_w_glu_val, m_w_glu_gate, m_w_mix_out, m_xattn_norm, m_mem_norm, m_w_q, m_w_kv, m_w_xo, m_ffn2_norm, m_ffn2_w_gate, m_ffn2_w_up, m_ffn2_w_down, m_final_norm, v_ffn1_norm, v_ffn1_w_gate, v_ffn1_w_up, v_ffn1_w_down, v_mix_norm, v_w_in, v_pool_w, v_pool_scale, v_w_pool_proj, v_ssm_a_re, v_ssm_a_im, v_ssm_log_dt, v_ssm_b_re, v_ssm_b_im, v_ssm_c_re, v_ssm_c_im, v_ssm_d, v_w_glu_val, v_w_glu_gate, v_w_mix_out, v_xattn_norm, v_mem_norm, v_w_q, v_w_kv, v_w_xo, v_ffn2_norm, v_ffn2_w_gate, v_ffn2_w_up, v_ffn2_w_down, v_final_norm):
    given = dict(locals())
    w = {n: given[n] for n in WEIGHTS}
    m = {n: given["m_" + n] for n in WEIGHTS}
    v = {n: given["v_" + n] for n in WEIGHTS}

    def shard_view(a, n):
        return a[0].T if n in TRANSPOSED else a[0]

    def shard_unview(a, n):
        return (a.T if n in TRANSPOSED else a)[None]

    shards = [jnp.concatenate([shard_view(w[n], n).astype(BF16) for n in grp], axis=0) for grp in GATHER_GROUPS]
    gathered = _allgather_shards(shards)

    loss_part, grad_x, big, small = _device_step(x[0], mem[0], loss_target[0], gathered, {n: w[n] for n in SMALL})
    loss = lax.psum(loss_part[0, 0], ("x", "y", "c"))

    c_idx = lax.axis_index("c").astype(jnp.int32).reshape(1)
    place = jnp.stack([2 * lax.axis_index("x") + lax.axis_index("y"), lax.axis_index("c")]).astype(jnp.int32)
    from_sibling = _sibling_swap_halves(big)
    chip_parts = [_chip_presum(k, big[k], from_sibling[k], c_idx) for k in range(len(big))]
    from_chips = _chip_exchange(chip_parts)
    halves = [_chip_sum(k, chip_parts[k], from_chips[k], place) for k in range(len(big))]
    reduced = _sibling_join_halves(halves)

    grads, delta, new_m, new_v = {}, {}, {}, {}
    for grp, red in zip(REDUCE_GROUPS, reduced):
        row0 = 0
        for n in grp:
            w_n = shard_view(w[n], n)
            grads[n], delta[n], new_m[n], new_v[n] = (
                shard_unview(o, n) for o in _adamw("adamw_" + n, w_n, red, row0, shard_view(m[n], n), shard_view(v[n], n)))
            row0 += w_n.shape[0]

    g_small = dict(zip(SMALL, _unpack_small(_allreduce_small(_pack_small([small[n] for n in SMALL])), [w[n] for n in SMALL])))
    narrow = [n for n in SMALL if w[n].ndim > 3]
    dense = [n for n in SMALL if n not in narrow]
    for n in narrow:
        two_d = (-1, w[n].shape[-1])
        outs = _adamw("adamw_" + n, w[n].reshape(two_d), g_small[n].reshape(two_d), 0, m[n].reshape(two_d), v[n].reshape(two_d))
        grads[n], delta[n], new_m[n], new_v[n] = (o.reshape(w[n].shape) for o in outs)
    dense_like = [w[n] for n in dense]
    packed = _adamw("adamw_small", _pack_small(dense_like), _pack_small([g_small[n] for n in dense]), 0,
                    _pack_small([m[n] for n in dense]), _pack_small([v[n] for n in dense]))
    for out, store in zip(packed, (grads, delta, new_m, new_v)):
        for n, val in zip(dense, _unpack_small(out, dense_like)):
            store[n] = val

    return (loss, grad_x[None], *[grads[n] for n in WEIGHTS], *[delta[n] for n in WEIGHTS],
            *[new_m[n] for n in WEIGHTS], *[new_v[n] for n in WEIGHTS])
```

```python
import functools
import math

import jax
import jax.numpy as jnp
from jax import lax
from jax.experimental import pallas as pl
from jax.experimental.pallas import tpu as pltpu

F32 = jnp.float32
BF16 = jnp.bfloat16
SDS = jax.ShapeDtypeStruct
BS = pl.BlockSpec
MESH = pl.DeviceIdType.MESH

D_MODEL = 1024
D_FF = 2816
N_SHARD = 4
FF_SH = D_FF // N_SHARD
D_POOL = 512
POOL_WINDOWS = (2, 4, 8, 16)
POOL_GROUP = 128
D_SSM = 256
SSM_GROUPS = 16
SSM_GROUP = 16
SSM_STATE = 64
SSM_CH = SSM_GROUPS * SSM_STATE
N_HEADS = 4
HEAD_DIM = 256
EPS = 1e-6
ADAM_LR, ADAM_B1, ADAM_B2, ADAM_EPS, ADAM_WD, ADAM_STEP = 0.001, 0.9, 0.999, 1e-08, 0.01, 10

VMEM_LIMIT_V7X = 52 * 1024 * 1024
TM = 512

NN = (((1,), (0,)), ((), ()))
NT = (((1,), (1,)), ((), ()))
TN = (((0,), (0,)), ((), ()))


def _params(*sem):
    return pltpu.CompilerParams(dimension_semantics=sem if sem else None, vmem_limit_bytes=VMEM_LIMIT_V7X)


def _dot(a, b, dims=NN):
    return lax.dot_general(a.astype(BF16), b.astype(BF16), dims, preferred_element_type=F32)


def _sigmoid(v):
    return 1.0 / (1.0 + jnp.exp(-v))


def _block_dims(spec):
    return tuple(d for d in spec.block_shape if d is not None)


def _mm(name, pairs, *, grid, out_shape, out_spec, red_axis=None, extras=(), epilogue=None):
    n_pairs, n_extra = len(pairs), len(extras)
    n_red = grid[red_axis] if red_axis is not None else 1
    dims = [p[4] for p in pairs]

    def body(*refs):
        ab = refs[:2 * n_pairs]
        ex = refs[2 * n_pairs:2 * n_pairs + n_extra]
        o_ref = refs[2 * n_pairs + n_extra]

        def partial():
            acc = None
            for p in range(n_pairs):
                t = _dot(ab[2 * p][...], ab[2 * p + 1][...], dims[p])
                acc = t if acc is None else acc + t
            return acc

        def finish(acc):
            res = epilogue(acc, *[e[...] for e in ex]) if epilogue is not None else acc
            o_ref[...] = res.astype(o_ref.dtype)

        if n_red == 1:
            finish(partial())
        else:
            acc_ref = refs[-1]
            k = pl.program_id(red_axis)

            @pl.when(k == 0)
            def _():
                acc_ref[...] = jnp.zeros_like(acc_ref)

            acc_ref[...] += partial()

            @pl.when(k == n_red - 1)
            def _():
                finish(acc_ref[...])

    operands, in_specs = [], []
    for a, a_spec, b, b_spec, _ in pairs:
        operands += [a, b]
        in_specs += [a_spec, b_spec]
    for e, e_spec in extras:
        operands.append(e)
        in_specs.append(e_spec)
    scratch = [pltpu.VMEM(_block_dims(out_spec), F32)] if n_red > 1 else []
    sem = tuple("arbitrary" if ax == red_axis else "parallel" for ax in range(len(grid)))
    return pl.pallas_call(body, out_shape=out_shape, grid=grid, in_specs=in_specs, out_specs=out_spec,
                          scratch_shapes=scratch, name=name, compiler_params=_params(*sem))(*operands)


def _rmsnorm(name, h, gain, tm):
    t, d = h.shape

    def body(h_ref, g_ref, u_ref):
        hv = h_ref[...]
        r = lax.rsqrt(jnp.mean(hv * hv, axis=-1, keepdims=True) + EPS)
        u_ref[...] = ((hv * r) * g_ref[...]).astype(u_ref.dtype)

    return pl.pallas_call(
        body, out_shape=SDS((t, d), BF16), grid=(t // tm,),
        in_specs=[BS((tm, d), lambda i: (i, 0)), BS((1, d), lambda i: (0, 0))],
        out_specs=BS((tm, d), lambda i: (i, 0)), name=name, compiler_params=_params("parallel"))(h, gain)


def _rmsnorm_bwd(name, h, gain, du, dh_in, tm):
    t, d = h.shape
    has_in = dh_in is not None

    def body(*refs):
        if has_in:
            h_ref, g_ref, du_ref, dhin_ref, dh_ref, dhb_ref, dg_ref = refs
        else:
            h_ref, g_ref, du_ref, dh_ref, dhb_ref, dg_ref = refs
        i = pl.program_id(0)
        hv = h_ref[...]
        r = lax.rsqrt(jnp.mean(hv * hv, axis=-1, keepdims=True) + EPS)
        n = hv * r
        duv = du_ref[...].astype(F32)
        dn = duv * g_ref[...]
        dh = r * (dn - n * jnp.mean(dn * n, axis=-1, keepdims=True))
        if has_in:
            dh = dhin_ref[...] + dh
        dh_ref[...] = dh
        dhb_ref[...] = dh.astype(BF16)

        @pl.when(i == 0)
        def _():
            dg_ref[...] = jnp.zeros_like(dg_ref)

        dg_ref[...] += jnp.sum(duv * n, axis=0, keepdims=True)

    row = BS((tm, d), lambda i: (i, 0))
    vec = BS((1, d), lambda i: (0, 0))
    operands = [h, gain, du] + ([dh_in] if has_in else [])
    in_specs = [row, vec, row] + ([row] if has_in else [])
    return pl.pallas_call(
        body, out_shape=(SDS((t, d), F32), SDS((t, d), BF16), SDS((1, d), F32)), grid=(t // tm,),
        in_specs=in_specs, out_specs=(row, row, vec), name=name, compiler_params=_params("arbitrary"))(*operands)


def _loss_head(h, gain, target, tm):
    t, d = h.shape

    def body(h_ref, g_ref, t_ref, loss_ref, dh_ref, dhb_ref, dg_ref):
        i = pl.program_id(0)
        hv = h_ref[...]
        g = g_ref[...]
        r = lax.rsqrt(jnp.mean(hv * hv, axis=-1, keepdims=True) + EPS)
        n = hv * r
        err = n * g - t_ref[...]
        dy = err * (1.0 / d)
        dn = dy * g
        dh = r * (dn - n * jnp.mean(dn * n, axis=-1, keepdims=True))
        dh_ref[...] = dh
        dhb_ref[...] = dh.astype(BF16)

        @pl.when(i == 0)
        def _():
            dg_ref[...] = jnp.zeros_like(dg_ref)
            loss_ref[...] = jnp.zeros_like(loss_ref)

        dg_ref[...] += jnp.sum(dy * n, axis=0, keepdims=True)
        part = 0.5 * jnp.sum(jnp.mean(err * err, axis=-1, keepdims=True), axis=0, keepdims=True)
        loss_ref[...] += jnp.broadcast_to(part, loss_ref.shape)

    row = BS((tm, d), lambda i: (i, 0))
    vec = BS((1, d), lambda i: (0, 0))
    return pl.pallas_call(
        body, out_shape=(SDS((1, 128), F32), SDS((t, d), F32), SDS((t, d), BF16), SDS((1, d), F32)),
        grid=(t // tm,), in_specs=[row, vec, row],
        out_specs=(BS((1, 128), lambda i: (0, 0)), row, row, vec),
        name="loss_head", compiler_params=_params("arbitrary"))(h, gain, target)


FFN_GATE, FFN_UP, FFN_DOWN = 0, 1, 2


def _ffn_w_spec(member, index):
    return BS((None, None, FF_SH, D_MODEL), lambda *g: (g[index], member, 0, 0))


def _ffn_up(name, u, w_f, tm):
    t, d = u.shape

    def body(u_ref, wg_ref, wu_ref, g_ref, up_ref, a_ref):
        uv = u_ref[...]
        g = _dot(uv, wg_ref[...], NT)
        up = _dot(uv, wu_ref[...], NT)
        g_ref[...] = g.astype(BF16)
        up_ref[...] = up.astype(BF16)
        a_ref[...] = (g * _sigmoid(g) * up).astype(BF16)

    hid = BS((None, tm, FF_SH), lambda s, i: (s, i, 0))
    shape = SDS((N_SHARD, t, FF_SH), BF16)
    return pl.pallas_call(
        body, out_shape=(shape, shape, shape), grid=(N_SHARD, t // tm),
        in_specs=[BS((tm, d), lambda s, i: (i, 0)), _ffn_w_spec(FFN_GATE, 0), _ffn_w_spec(FFN_UP, 0)],
        out_specs=(hid, hid, hid), name=name, compiler_params=_params("parallel", "parallel"))(u, w_f, w_f)


def _ffn_down(name, a, w_f, resid, tm):
    t, d = resid.shape
    return _mm(name, [(a, BS((None, tm, FF_SH), lambda i, s: (s, i, 0)), w_f, _ffn_w_spec(FFN_DOWN, 1), NN)],
               grid=(t // tm, N_SHARD), red_axis=1, out_shape=SDS((t, d), F32),
               out_spec=BS((tm, d), lambda i, s: (i, 0)),
               extras=[(resid, BS((tm, d), lambda i, s: (i, 0)))],
               epilogue=lambda acc, res: res + 0.5 * acc)


def _ffn_bwd_act(name, dh_b, w_f, g, up, tm):
    t, d = dh_b.shape

    def body(dh_ref, wd_ref, g_ref, up_ref, dg_ref, dup_ref):
        da = 0.5 * _dot(dh_ref[...], wd_ref[...], NT)
        gv = g_ref[...].astype(F32)
        uv = up_ref[...].astype(F32)
        sg = _sigmoid(gv)
        silu = gv * sg
        dg_ref[...] = (da * uv * (sg + silu * (1.0 - sg))).astype(BF16)
        dup_ref[...] = (da * silu).astype(BF16)

    hid = BS((None, tm, FF_SH), lambda s, i: (s, i, 0))
    shape = SDS((N_SHARD, t, FF_SH), BF16)
    return pl.pallas_call(
        body, out_shape=(shape, shape), grid=(N_SHARD, t // tm),
        in_specs=[BS((tm, d), lambda s, i: (i, 0)), _ffn_w_spec(FFN_DOWN, 0), hid, hid],
        out_specs=(hid, hid), name=name, compiler_params=_params("parallel", "parallel"))(dh_b, w_f, g, up)


def _ffn_dw(name, u, dg, dup, a, dh_b, tm):
    t, d = u.shape
    n_t = t // tm

    def body(u_ref, dg_ref, dup_ref, a_ref, dh_ref, o_ref, acc):
        i = pl.program_id(1)

        @pl.when(i == 0)
        def _():
            acc[...] = jnp.zeros_like(acc)

        uv = u_ref[...]
        acc[FFN_GATE] += _dot(dg_ref[...], uv, TN)
        acc[FFN_UP] += _dot(dup_ref[...], uv, TN)
        acc[FFN_DOWN] += _dot(a_ref[...], dh_ref[...], TN)

        @pl.when(i == n_t - 1)
        def _():
            o_ref[FFN_GATE] = acc[FFN_GATE].astype(BF16)
            o_ref[FFN_UP] = acc[FFN_UP].astype(BF16)
            o_ref[FFN_DOWN] = (0.5 * acc[FFN_DOWN]).astype(BF16)

    hid = BS((None, tm, FF_SH), lambda s, i: (s, i, 0))
    row = BS((tm, d), lambda s, i: (i, 0))
    return pl.pallas_call(
        body, out_shape=SDS((N_SHARD, 3, FF_SH, d), BF16), grid=(N_SHARD, n_t),
        in_specs=[row, hid, hid, hid, row], out_specs=BS((None, 3, FF_SH, d), lambda s, i: (s, 0, 0, 0)),
        scratch_shapes=[pltpu.VMEM((3, FF_SH, d), F32)],
        name=name, compiler_params=_params("parallel", "arbitrary"))(u, dg, dup, a, dh_b)


def _ffn_dx(name, dg, dup, w_f, tm):
    t = dg.shape[1]
    hid = BS((None, tm, FF_SH), lambda i, s: (s, i, 0))
    return _mm(name, [(dg, hid, w_f, _ffn_w_spec(FFN_GATE, 1), NN), (dup, hid, w_f, _ffn_w_spec(FFN_UP, 1), NN)],
               grid=(t // tm, N_SHARD), red_axis=1, out_shape=SDS((t, D_MODEL), F32),
               out_spec=BS((tm, D_MODEL), lambda i, s: (i, 0)))


def _plain_mm(name, a, b, dims, out_dtype, tm, resid=None):
    t = a.shape[0]
    n = b.shape[1] if dims == NN else b.shape[0]
    extras = [(resid, BS((tm, n), lambda i: (i, 0)))] if resid is not None else []
    epi = (lambda acc, res: res + acc) if resid is not None else None
    return _mm(name, [(a, BS((tm, a.shape[1]), lambda i: (i, 0)), b, BS(b.shape, lambda i: (0, 0)), dims)],
               grid=(t // tm,), out_shape=SDS((t, n), out_dtype), out_spec=BS((tm, n), lambda i: (i, 0)),
               extras=extras, epilogue=epi)


def _dw_mm(name, a, b, tm, out_dtype=BF16):
    t, k = a.shape
    n = b.shape[1]
    return _mm(name, [(a, BS((tm, k), lambda i: (i, 0)), b, BS((tm, n), lambda i: (i, 0)), TN)],
               grid=(t // tm,), red_axis=0, out_shape=SDS((k, n), out_dtype), out_spec=BS((k, n), lambda i: (0, 0)))


POOL_CHUNK = 256
POOL_HALO = 8


def _window_sum(v, width, lead):
    n = v.shape[0]
    s = v
    k = 1
    while k < width:
        s = s + pltpu.roll(s, n - k, 0)
        k *= 2
    return pltpu.roll(s, lead, 0) if lead else s


def _pool_count(base, left, right, t, shape):
    pos = base + lax.broadcasted_iota(jnp.int32, shape, 0)
    lo = jnp.maximum(pos - left, 0)
    hi = jnp.minimum(pos + right + 1, t)
    return (hi - lo).astype(F32)


def _pool_fwd(proj, pool_w, pool_scale):
    t = proj.shape[0]
    c, h = POOL_CHUNK, POOL_HALO
    n_chunks = t // c

    def body(proj_hbm, pw_ref, sc_ref, pooled_ref, mixed_ref, ms_ref, pad_ref, sem):
        cp = pltpu.make_async_copy(proj_hbm.at[:, pl.ds(0, D_POOL)], pad_ref.at[pl.ds(h, t), :], sem)
        cp.start()
        pad_ref[pl.ds(0, h), :] = jnp.zeros((h, D_POOL), F32)
        pad_ref[pl.ds(t + h, h), :] = jnp.zeros((h, D_POOL), F32)
        cp.wait()
        for g, width in enumerate(POOL_WINDOWS):
            left = width // 2
            right = width - 1 - left
            cols = slice(g * POOL_GROUP, (g + 1) * POOL_GROUP)
            wmat = pw_ref[g].astype(BF16)
            scale = sc_ref[:, cols]

            def chunk(ci, carry, left=left, right=right, width=width, cols=cols, wmat=wmat, scale=scale):
                base = pl.multiple_of(ci * c, c)
                v = pad_ref[pl.ds(base, c + 2 * h), cols]
                win = _window_sum(v, width, left)[h:h + c]
                cnt = _pool_count(base, left, right, t, (c, POOL_GROUP))
                pooled = (win / cnt - v[h:h + c]).astype(BF16)
                mixed = _dot(pooled, wmat)
                pooled_ref[pl.ds(base, c), cols] = pooled
                mixed_ref[pl.ds(base, c), cols] = mixed.astype(BF16)
                ms_ref[pl.ds(base, c), cols] = (mixed * scale).astype(BF16)
                return carry

            lax.fori_loop(0, n_chunks, chunk, 0)

    vm = BS(memory_space=pltpu.VMEM)
    shape = SDS((t, D_POOL), BF16)
    return pl.pallas_call(
        body, out_shape=(shape, shape, shape),
        in_specs=[BS(memory_space=pl.ANY), vm, vm], out_specs=(vm, vm, vm),
        scratch_shapes=[pltpu.VMEM((t + 2 * h, D_POOL), F32), pltpu.SemaphoreType.DMA],
        name="pool_fwd", compiler_params=_params())(proj, pool_w, pool_scale)


def _pool_bwd(d_ms, mixed, pooled, pool_w, pool_scale):
    t = d_ms.shape[0]
    c, h = POOL_CHUNK, POOL_HALO
    n_chunks = t // c

    def body(dms_ref, mixed_ref, pooled_ref, pw_ref, sc_ref, dp_ref, dsc_ref, dpw_ref, pad_ref):
        pad_ref[pl.ds(0, h), :] = jnp.zeros((h, D_POOL), F32)
        pad_ref[pl.ds(t + h, h), :] = jnp.zeros((h, D_POOL), F32)
        for g, width in enumerate(POOL_WINDOWS):
            left = width // 2
            right = width - 1 - left
            cols = slice(g * POOL_GROUP, (g + 1) * POOL_GROUP)
            wmat = pw_ref[g].astype(BF16)
            scale = sc_ref[:, cols]

            def first(ci, carry, left=left, right=right, cols=cols, wmat=wmat, scale=scale):
                dsc, dpw = carry
                base = pl.multiple_of(ci * c, c)
                dms = dms_ref[pl.ds(base, c), cols].astype(F32)
                dsc = dsc + jnp.sum(dms * mixed_ref[pl.ds(base, c), cols].astype(F32), axis=0, keepdims=True)
                dmix = (dms * scale).astype(BF16)
                dpw = dpw + _dot(pooled_ref[pl.ds(base, c), cols], dmix, TN)
                dpooled = _dot(dmix, wmat, NT)
                cnt = _pool_count(base, left, right, t, (c, POOL_GROUP))
                pad_ref[pl.ds(base + h, c), cols] = dpooled / cnt
                return dsc, dpw

            dsc, dpw = lax.fori_loop(0, n_chunks, first,
                                     (jnp.zeros((1, POOL_GROUP), F32), jnp.zeros((POOL_GROUP, POOL_GROUP), F32)))
            dsc_ref[:, cols] = dsc
            dpw_ref[g] = dpw

            def second(ci, carry, left=left, right=right, width=width, cols=cols):
                base = pl.multiple_of(ci * c, c)
                v = pad_ref[pl.ds(base, c + 2 * h), cols]
                win = _window_sum(v, width, right)[h:h + c]
                cnt = _pool_count(base, left, right, t, (c, POOL_GROUP))
                dp_ref[pl.ds(base, c), cols] = (win - v[h:h + c] * cnt).astype(BF16)
                return carry

            lax.fori_loop(0, n_chunks, second, 0)

    vm = BS(memory_space=pltpu.VMEM)
    return pl.pallas_call(
        body, out_shape=(SDS((t, D_POOL), BF16), SDS((1, D_POOL), F32), SDS((4, POOL_GROUP, POOL_GROUP), F32)),
        in_specs=[vm] * 5, out_specs=(vm, vm, vm),
        scratch_shapes=[pltpu.VMEM((t + 2 * h, D_POOL), F32)],
        name="pool_bwd", compiler_params=_params())(d_ms, mixed, pooled, pool_w, pool_scale)


def _ssm_disc(ar, ai, ldt):
    def body(ar_ref, ai_ref, ldt_ref, abr_ref, abi_ref, qr_ref, qi_ref):
        a_r, a_i = ar_ref[...], ai_ref[...]
        dt = jnp.exp(ldt_ref[...])
        mag = jnp.exp(dt * a_r)
        ang = dt * a_i
        abr = mag * jnp.cos(ang)
        abi = mag * jnp.sin(ang)
        den = a_r * a_r + a_i * a_i
        nr = abr - 1.0
        abr_ref[...] = abr
        abi_ref[...] = abi
        qr_ref[...] = (nr * a_r + abi * a_i) / den
        qi_ref[...] = (abi * a_r - nr * a_i) / den

    vm = BS(memory_space=pltpu.VMEM)
    shape = SDS(ar.shape, F32)
    return pl.pallas_call(body, out_shape=(shape,) * 4, in_specs=[vm] * 3, out_specs=(vm,) * 4,
                          name="ssm_disc", compiler_params=_params())(ar, ai, ldt)


def _ssm_disc_bwd(ar, ai, ldt, d_abr, d_abi, d_qr, d_qi):
    def body(ar_ref, ai_ref, ldt_ref, gabr_ref, gabi_ref, gqr_ref, gqi_ref, dar_ref, dai_ref, dldt_ref):
        a_r, a_i = ar_ref[...], ai_ref[...]
        dt = jnp.exp(ldt_ref[...])
        mag = jnp.exp(dt * a_r)
        ang = dt * a_i
        cs, sn = jnp.cos(ang), jnp.sin(ang)
        abr, abi = mag * cs, mag * sn
        den = a_r * a_r + a_i * a_i
        nr = abr - 1.0
        qr = (nr * a_r + abi * a_i) / den
        qi = (abi * a_r - nr * a_i) / den
        gqr, gqi = gqr_ref[...], gqi_ref[...]
        g_nr_num = gqr / den
        g_ni_num = gqi / den
        g_den = -(gqr * qr + gqi * qi) / den
        g_nr = g_nr_num * a_r - g_ni_num * a_i
        g_abi = g_nr_num * a_i + g_ni_num * a_r
        d_ar = g_nr_num * nr + g_ni_num * abi + 2.0 * a_r * g_den
        d_ai = g_nr_num * abi - g_ni_num * nr + 2.0 * a_i * g_den
        g_abr = gabr_ref[...] + g_nr
        g_abi = gabi_ref[...] + g_abi
        g_mag = g_abr * cs + g_abi * sn
        g_ang = mag * (g_abi * cs - g_abr * sn)
        g_e = g_mag * mag
        d_ar = d_ar + g_e * dt
        d_ai = d_ai + g_ang * dt
        g_dt = g_e * a_r + g_ang * a_i
        dar_ref[...] = d_ar
        dai_ref[...] = d_ai
        dldt_ref[...] = jnp.sum(g_dt * dt, axis=1, keepdims=True)

    vm = BS(memory_space=pltpu.VMEM)
    return pl.pallas_call(body, out_shape=(SDS(ar.shape, F32), SDS(ar.shape, F32), SDS(ldt.shape, F32)),
                          in_specs=[vm] * 7, out_specs=(vm,) * 3, name="ssm_disc_bwd",
                          compiler_params=_params())(ar, ai, ldt, d_abr, d_abi, d_qr, d_qi)


def _ssm_bbar(qr, qi, br, bi):
    def body(qr_ref, qi_ref, br_ref, bi_ref, bbr_ref, bbi_ref):
        q_r, q_i, b_r, b_i = qr_ref[...], qi_ref[...], br_ref[...], bi_ref[...]
        bbr_ref[...] = q_r * b_r - q_i * b_i
        bbi_ref[...] = q_r * b_i + q_i * b_r

    vm = BS(memory_space=pltpu.VMEM)
    shape = SDS(br.shape, F32)
    return pl.pallas_call(body, out_shape=(shape, shape), in_specs=[vm] * 4, out_specs=(vm, vm),
                          name="ssm_bbar", compiler_params=_params())(qr, qi, br, bi)


def _ssm_bbar_bwd(qr, qi, br, bi, g_bbr, g_bbi):
    def body(qr_ref, qi_ref, br_ref, bi_ref, gr_ref, gi_ref, dqr_ref, dqi_ref, dbr_ref, dbi_ref):
        q_r, q_i, b_r, b_i = qr_ref[...], qi_ref[...], br_ref[...], bi_ref[...]
        g_r, g_i = gr_ref[...], gi_ref[...]
        dqr_ref[...] = jnp.sum(g_r * b_r + g_i * b_i, axis=1, keepdims=True)
        dqi_ref[...] = jnp.sum(g_i * b_r - g_r * b_i, axis=1, keepdims=True)
        dbr_ref[...] = g_r * q_r + g_i * q_i
        dbi_ref[...] = g_i * q_r - g_r * q_i

    vm = BS(memory_space=pltpu.VMEM)
    return pl.pallas_call(
        body, out_shape=(SDS(qr.shape, F32), SDS(qr.shape, F32), SDS(br.shape, F32), SDS(br.shape, F32)),
        in_specs=[vm] * 6, out_specs=(vm,) * 4, name="ssm_bbar_bwd",
        compiler_params=_params())(qr, qi, br, bi, g_bbr, g_bbi)


SCAN_ROWS = 256


def _ssm_scan(name, inp, w1, a_r, a_i, w2, reverse):
    t = inp.shape[0]
    rows = SCAN_ROWS
    n = t // rows
    n_groups = rows // 8
    ch = SSM_CH
    at = (lambda i: (n - 1 - i, 0)) if reverse else (lambda i: (i, 0))

    def body(in_ref, w1_ref, ar_ref, ai_ref, w2_ref, st_ref, out_ref, cr_ref, ci_ref, k_ref):
        i = pl.program_id(0)

        @pl.when(i == 0)
        def _():
            ar8 = jnp.broadcast_to(ar_ref[...], (8, ch))
            ai8 = jnp.broadcast_to(ai_ref[...], (8, ch))
            row = lax.broadcasted_iota(jnp.int32, (8, ch), 0)
            rank = (7 - row) if reverse else row
            powers = [(ar8, ai8)]
            for _ in range(7):
                p_r, p_i = powers[-1]
                powers.append((p_r * ar8 - p_i * ai8, p_r * ai8 + p_i * ar8))
            zero = jnp.zeros((8, ch), F32)
            for slot, k in enumerate((1, 2, 4)):
                k_ref[2 * slot] = jnp.where(rank >= k, powers[k - 1][0], zero)
                k_ref[2 * slot + 1] = jnp.where(rank >= k, powers[k - 1][1], zero)
            carry_r, carry_i = zero, zero
            for j in range(8):
                carry_r = jnp.where(rank == j, powers[j][0], carry_r)
                carry_i = jnp.where(rank == j, powers[j][1], carry_i)
            k_ref[6] = carry_r
            k_ref[7] = carry_i
            cr_ref[...] = zero
            ci_ref[...] = zero

        st_ref[...] = _dot(in_ref[...], w1_ref[...])

        def group(gi, carry):
            c_r, c_i = carry
            g = (n_groups - 1 - gi) if reverse else gi
            r0 = pl.multiple_of(g * 8, 8)
            x_r = st_ref[pl.ds(r0, 8), 0:ch]
            x_i = st_ref[pl.ds(r0, 8), ch:2 * ch]
            for slot, k in enumerate((1, 2, 4)):
                shift = (8 - k) if reverse else k
                s_r = pltpu.roll(x_r, shift, 0)
                s_i = pltpu.roll(x_i, shift, 0)
                m_r, m_i = k_ref[2 * slot], k_ref[2 * slot + 1]
                x_r, x_i = x_r + m_r * s_r - m_i * s_i, x_i + m_r * s_i + m_i * s_r
            p_r, p_i = k_ref[6], k_ref[7]
            x_r, x_i = x_r + p_r * c_r - p_i * c_i, x_i + p_r * c_i + p_i * c_r
            st_ref[pl.ds(r0, 8), 0:ch] = x_r
            st_ref[pl.ds(r0, 8), ch:2 * ch] = x_i
            last = 0 if reverse else 7
            return (jnp.broadcast_to(x_r[last:last + 1, :], (8, ch)), jnp.broadcast_to(x_i[last:last + 1, :], (8, ch)))

        c_r, c_i = lax.fori_loop(0, n_groups, group, (cr_ref[...], ci_ref[...]))
        cr_ref[...] = c_r
        ci_ref[...] = c_i
        out_ref[...] = _dot(st_ref[...], w2_ref[...])

    return pl.pallas_call(
        body, out_shape=(SDS((t, 2 * ch), F32), SDS((t, D_SSM), F32)), grid=(n,),
        in_specs=[BS((rows, D_SSM), at), BS((D_SSM, 2 * ch), lambda i: (0, 0)), BS((1, ch), lambda i: (0, 0)),
                  BS((1, ch), lambda i: (0, 0)), BS((2 * ch, D_SSM), lambda i: (0, 0))],
        out_specs=(BS((rows, 2 * ch), at), BS((rows, D_SSM), at)),
        scratch_shapes=[pltpu.VMEM((8, ch), F32), pltpu.VMEM((8, ch), F32), pltpu.VMEM((8, 8, ch), F32)],
        name=name, compiler_params=_params("arbitrary"))(inp, w1, a_r, a_i, w2)


DA_ROWS = 512


def _ssm_da(name, lam, states, reverse):
    t = lam.shape[0]
    rows = DA_ROWS
    n = t // rows
    nb = rows // 8
    ch = SSM_CH
    if reverse:
        halo_at = lambda i: (jnp.minimum((i + 1) * nb, t // 8 - 1), 0)
    else:
        halo_at = lambda i: (jnp.maximum(i * nb - 1, 0), 0)

    def body(lam_ref, x_ref, halo_ref, dr_ref, di_ref):
        i = pl.program_id(0)

        @pl.when(i == 0)
        def _():
            dr_ref[...] = jnp.zeros_like(dr_ref)
            di_ref[...] = jnp.zeros_like(di_ref)

        row = lax.broadcasted_iota(jnp.int32, (rows, ch), 0)
        if reverse:
            edge, shift, h_row, live = rows - 1, rows - 1, 0, i < n - 1
        else:
            edge, shift, h_row, live = 0, 1, 7, i > 0

        def neighbour(lo):
            halo = jnp.where(live, halo_ref[h_row:h_row + 1, lo:lo + ch], 0.0)
            return jnp.where(row == edge, jnp.broadcast_to(halo, (rows, ch)), pltpu.roll(x_ref[:, lo:lo + ch], shift, 0))

        xp_r, xp_i = neighbour(0), neighbour(ch)
        l_r, l_i = lam_ref[:, 0:ch], lam_ref[:, ch:2 * ch]
        dr_ref[...] += jnp.sum(l_r * xp_r + l_i * xp_i, axis=0, keepdims=True)
        di_ref[...] += jnp.sum(l_i * xp_r - l_r * xp_i, axis=0, keepdims=True)

    blk = BS((rows, 2 * ch), lambda i: (i, 0))
    vec = BS((1, ch), lambda i: (0, 0))
    return pl.pallas_call(
        body, out_shape=(SDS((1, ch), F32), SDS((1, ch), F32)), grid=(n,),
        in_specs=[blk, blk, BS((8, 2 * ch), halo_at)], out_specs=(vec, vec),
        name=name, compiler_params=_params("arbitrary"))(lam, states, states)


GELU_C = math.sqrt(2.0 / math.pi)
GELU_K = 0.044715


def _ssm_combine(proj, y_dirs, d_skip, tm):
    t = proj.shape[0]

    def body(s_ref, y_ref, d_ref, yt_ref, g_ref):
        y = s_ref[...] * d_ref[...] + y_ref[0] + y_ref[1]
        yt_ref[...] = y
        th = jnp.tanh(GELU_C * (y + GELU_K * y * y * y))
        g_ref[...] = (0.5 * y * (1.0 + th)).astype(BF16)

    blk = BS((tm, D_SSM), lambda i: (i, 0))
    return pl.pallas_call(
        body, out_shape=(SDS((t, D_SSM), F32), SDS((t, D_SSM), BF16)), grid=(t // tm,),
        in_specs=[BS((tm, D_SSM), lambda i: (i, D_POOL // D_SSM)), BS((2, tm, D_SSM), lambda i: (0, i, 0)),
                  BS((1, D_SSM), lambda i: (0, 0))],
        out_specs=(blk, blk), name="ssm_combine", compiler_params=_params("parallel"))(proj, y_dirs, d_skip)


def _ssm_ds(proj, d_yt, du_dirs, d_skip, tm):
    t = proj.shape[0]

    def body(s_ref, dy_ref, du_ref, d_ref, ds_ref, dd_ref):
        i = pl.program_id(0)
        dy = dy_ref[...]
        ds_ref[...] = (dy * d_ref[...] + du_ref[0] + du_ref[1]).astype(BF16)

        @pl.when(i == 0)
        def _():
            dd_ref[...] = jnp.zeros_like(dd_ref)

        dd_ref[...] += jnp.sum(dy * s_ref[...], axis=0, keepdims=True)

    blk = BS((tm, D_SSM), lambda i: (i, 0))
    vec = BS((1, D_SSM), lambda i: (0, 0))
    return pl.pallas_call(
        body, out_shape=(SDS((t, D_SSM), BF16), SDS((1, D_SSM), F32)), grid=(t // tm,),
        in_specs=[BS((tm, D_SSM), lambda i: (i, D_POOL // D_SSM)), blk, BS((2, tm, D_SSM), lambda i: (0, i, 0)), vec],
        out_specs=(blk, vec), name="ssm_ds", compiler_params=_params("arbitrary"))(proj, d_yt, du_dirs, d_skip)


GP_BLOCK = (D_POOL + D_SSM) // 256
GS_BLOCK = GP_BLOCK + D_MODEL // 256


def _merge_specs(tm):
    return [BS((tm, D_POOL), lambda s, i: (i, 0)), BS((tm, D_SSM), lambda s, i: (i, 0)),
            BS((None, D_POOL, 256), lambda s, i: (s, 0, 0)), BS((None, D_SSM, 256), lambda s, i: (s, 2, 0)),
            BS((None, D_SSM, 256), lambda s, i: (s, 3, 0)),
            BS((tm, 256), lambda s, i: (i, GP_BLOCK + s)), BS((tm, 256), lambda s, i: (i, GS_BLOCK + s))]


def _mixer_merge(ms, yssm, w_e, proj, tm):
    t = ms.shape[0]

    def body(ms_ref, y_ref, wpp_ref, wgv_ref, wgg_ref, gp_ref, gs_ref, o_ref):
        zp = _dot(ms_ref[...], wpp_ref[...])
        yv = y_ref[...]
        zv = _dot(yv, wgv_ref[...])
        zg = _dot(yv, wgg_ref[...])
        o_ref[...] = (_sigmoid(gp_ref[...]) * zp + _sigmoid(gs_ref[...]) * zv * _sigmoid(zg)).astype(BF16)

    col = BS((tm, 256), lambda s, i: (i, s))
    return pl.pallas_call(
        body, out_shape=SDS((t, D_MODEL), BF16), grid=(N_SHARD, t // tm), in_specs=_merge_specs(tm), out_specs=col,
        name="mixer_merge", compiler_params=_params("parallel", "parallel"))(ms, yssm, w_e, w_e, w_e, proj, proj)


def _mixer_merge_bwd(ms, yssm, w_e, proj, dmerged, tm):
    t = ms.shape[0]

    def body(ms_ref, y_ref, wpp_ref, wgv_ref, wgg_ref, gp_ref, gs_ref, dm_ref,
             dgp_ref, dgs_ref, dzp_ref, dzv_ref, dzg_ref):
        zp = _dot(ms_ref[...], wpp_ref[...])
        yv = y_ref[...]
        zv = _dot(yv, wgv_ref[...])
        zg = _dot(yv, wgg_ref[...])
        dm = dm_ref[...].astype(F32)
        sp, ss, sg = _sigmoid(gp_ref[...]), _sigmoid(gs_ref[...]), _sigmoid(zg)
        dgp_ref[...] = (dm * zp * sp * (1.0 - sp)).astype(BF16)
        dgs_ref[...] = (dm * zv * sg * ss * (1.0 - ss)).astype(BF16)
        dzp_ref[...] = (dm * sp).astype(BF16)
        dz = dm * ss
        dzv_ref[...] = (dz * sg).astype(BF16)
        dzg_ref[...] = (dz * zv * sg * (1.0 - sg)).astype(BF16)

    col = BS((tm, 256), lambda s, i: (i, s))
    shape = SDS((t, D_MODEL), BF16)
    return pl.pallas_call(
        body, out_shape=(shape,) * 5, grid=(N_SHARD, t // tm), in_specs=_merge_specs(tm) + [col],
        out_specs=(col,) * 5, name="mixer_merge_bwd",
        compiler_params=_params("parallel", "parallel"))(ms, yssm, w_e, w_e, w_e, proj, proj, dmerged)


def _mixer_dw(ms, yssm, dzp, dzv, dzg, tm):
    t = ms.shape[0]
    n_t = t // tm

    def body(ms_ref, y_ref, dzp_ref, dzv_ref, dzg_ref, o_ref, acc):
        i = pl.program_id(1)

        @pl.when(i == 0)
        def _():
            acc[...] = jnp.zeros_like(acc)

        yv = y_ref[...]
        acc[0:D_POOL, :] += _dot(ms_ref[...], dzp_ref[...], TN)
        acc[D_POOL:D_POOL + D_SSM, :] += _dot(yv, dzv_ref[...], TN)
        acc[D_POOL + D_SSM:, :] += _dot(yv, dzg_ref[...], TN)

        @pl.when(i == n_t - 1)
        def _():
            o_ref[...] = acc[...].astype(BF16)

    col = BS((tm, 256), lambda s, i: (i, s))
    return pl.pallas_call(
        body, out_shape=SDS((N_SHARD, 1024, 256), BF16), grid=(N_SHARD, n_t),
        in_specs=[BS((tm, D_POOL), lambda s, i: (i, 0)), BS((tm, D_SSM), lambda s, i: (i, 0)), col, col, col],
        out_specs=BS((None, 1024, 256), lambda s, i: (s, 0, 0)), scratch_shapes=[pltpu.VMEM((1024, 256), F32)],
        name="mixer_dw", compiler_params=_params("parallel", "arbitrary"))(ms, yssm, dzp, dzv, dzg)


def _mixer_dx(dzp, dzv, dzg, w_e, y_total, tm):
    t = dzp.shape[0]

    def body(dzp_ref, dzv_ref, dzg_ref, wpp_ref, wgv_ref, wgg_ref, yt_ref, dms_ref, dy_ref, acc_ms, acc_y):
        s = pl.program_id(1)

        @pl.when(s == 0)
        def _():
            acc_ms[...] = jnp.zeros_like(acc_ms)
            acc_y[...] = jnp.zeros_like(acc_y)

        acc_ms[...] += _dot(dzp_ref[...], wpp_ref[...], NT)
        acc_y[...] += _dot(dzv_ref[...], wgv_ref[...], NT) + _dot(dzg_ref[...], wgg_ref[...], NT)

        @pl.when(s == N_SHARD - 1)
        def _():
            dms_ref[...] = acc_ms[...].astype(BF16)
            y = yt_ref[...]
            inner = GELU_C * (y + GELU_K * y * y * y)
            th = jnp.tanh(inner)
            dgelu = 0.5 * (1.0 + th) + 0.5 * y * (1.0 - th * th) * GELU_C * (1.0 + 3.0 * GELU_K * y * y)
            dy_ref[...] = acc_y[...] * dgelu

    col = BS((tm, 256), lambda i, s: (i, s))
    return pl.pallas_call(
        body, out_shape=(SDS((t, D_POOL), BF16), SDS((t, D_SSM), F32)), grid=(t // tm, N_SHARD),
        in_specs=[col, col, col, BS((None, D_POOL, 256), lambda i, s: (s, 0, 0)),
                  BS((None, D_SSM, 256), lambda i, s: (s, 2, 0)), BS((None, D_SSM, 256), lambda i, s: (s, 3, 0)),
                  BS((tm, D_SSM), lambda i, s: (i, 0))],
        out_specs=(BS((tm, D_POOL), lambda i, s: (i, 0)), BS((tm, D_SSM), lambda i, s: (i, 0))),
        scratch_shapes=[pltpu.VMEM((tm, D_POOL), F32), pltpu.VMEM((tm, D_SSM), F32)],
        name="mixer_dx", compiler_params=_params("parallel", "arbitrary"))(dzp, dzv, dzg, w_e, w_e, w_e, y_total)


def _attn_probs(q_h, k_h):
    s = _dot(q_h, k_h, NT) * (1.0 / math.sqrt(HEAD_DIM))
    e = jnp.exp(s - jnp.max(s, axis=-1, keepdims=True))
    return e / jnp.sum(e, axis=-1, keepdims=True)


def _attn_fwd(q, kv, tm):
    t = q.shape[0]
    m = kv.shape[0]

    def body(q_ref, kv_ref, o_ref):
        for hd in range(N_HEADS):
            lo = hd * HEAD_DIM
            p = _attn_probs(q_ref[:, lo:lo + HEAD_DIM], kv_ref[:, lo:lo + HEAD_DIM])
            o_ref[:, lo:lo + HEAD_DIM] = _dot(p, kv_ref[:, D_MODEL + lo:D_MODEL + lo + HEAD_DIM]).astype(BF16)

    return pl.pallas_call(
        body, out_shape=SDS((t, D_MODEL), BF16), grid=(t // tm,),
        in_specs=[BS((tm, D_MODEL), lambda i: (i, 0)), BS((m, 2 * D_MODEL), lambda i: (0, 0))],
        out_specs=BS((tm, D_MODEL), lambda i: (i, 0)), name="attn_fwd", compiler_params=_params("parallel"))(q, kv)


def _attn_bwd(q, kv, d_o, tm):
    t = q.shape[0]
    m = kv.shape[0]

    def body(q_ref, kv_ref, do_ref, dq_ref, dkv_ref):
        i = pl.program_id(0)

        @pl.when(i == 0)
        def _():
            dkv_ref[...] = jnp.zeros_like(dkv_ref)

        for hd in range(N_HEADS):
            lo = hd * HEAD_DIM
            q_h = q_ref[:, lo:lo + HEAD_DIM]
            k_h = kv_ref[:, lo:lo + HEAD_DIM]
            v_h = kv_ref[:, D_MODEL + lo:D_MODEL + lo + HEAD_DIM]
            do_h = do_ref[:, lo:lo + HEAD_DIM]
            p = _attn_probs(q_h, k_h)
            dkv_ref[:, D_MODEL + lo:D_MODEL + lo + HEAD_DIM] += _dot(p, do_h, TN)
            dp = _dot(do_h, v_h, NT)
            ds = p * (dp - jnp.sum(dp * p, axis=-1, keepdims=True)) * (1.0 / math.sqrt(HEAD_DIM))
            dq_ref[:, lo:lo + HEAD_DIM] = _dot(ds, k_h).astype(BF16)
            dkv_ref[:, lo:lo + HEAD_DIM] += _dot(ds, q_h, TN)

    row = BS((tm, D_MODEL), lambda i: (i, 0))
    full = BS((m, 2 * D_MODEL), lambda i: (0, 0))
    return pl.pallas_call(
        body, out_shape=(SDS((t, D_MODEL), BF16), SDS((m, 2 * D_MODEL), F32)), grid=(t // tm,),
        in_specs=[row, full, row], out_specs=(row, full), name="attn_bwd",
        compiler_params=_params("arbitrary"))(q, kv, d_o)


TRANSPOSED = ("ffn1_w_gate", "ffn1_w_up", "ffn2_w_gate", "ffn2_w_up", "w_in")
GATHER_GROUPS = (("ffn1_w_gate", "ffn1_w_up", "ffn1_w_down"),
                 ("w_in", "w_mix_out", "w_q", "w_xo"),
                 ("w_kv",),
                 ("w_pool_proj", "w_glu_val", "w_glu_gate"),
                 ("ffn2_w_gate", "ffn2_w_up", "ffn2_w_down"))
REDUCE_GROUPS = (("ffn2_w_gate", "ffn2_w_up", "ffn2_w_down"), ("w_xo",), ("w_q",), ("w_kv",), ("w_mix_out",),
                 ("w_pool_proj", "w_glu_val", "w_glu_gate"), ("w_in",), ("ffn1_w_gate", "ffn1_w_up", "ffn1_w_down"))
SMALL = ("ffn1_norm", "mix_norm", "pool_w", "pool_scale", "ssm_a_re", "ssm_a_im", "ssm_log_dt", "ssm_b_re",
         "ssm_b_im", "ssm_c_re", "ssm_c_im", "ssm_d", "xattn_norm", "mem_norm", "ffn2_norm", "final_norm")
WEIGHTS = ("ffn1_norm", "ffn1_w_gate", "ffn1_w_up", "ffn1_w_down", "mix_norm", "w_in", "pool_w", "pool_scale",
           "w_pool_proj", "ssm_a_re", "ssm_a_im", "ssm_log_dt", "ssm_b_re", "ssm_b_im", "ssm_c_re", "ssm_c_im",
           "ssm_d", "w_glu_val", "w_glu_gate", "w_mix_out", "xattn_norm", "mem_norm", "w_q", "w_kv", "w_xo",
           "ffn2_norm", "ffn2_w_gate", "ffn2_w_up", "ffn2_w_down", "final_norm")


def _block_diag_in(bb):
    eye = jnp.eye(SSM_GROUPS, dtype=bb.dtype)
    return jnp.einsum("dgph,gk->dghkp", bb, eye).reshape(2, D_SSM, SSM_CH)


def _block_diag_out(cc):
    eye = jnp.eye(SSM_GROUPS, dtype=cc.dtype)
    return jnp.einsum("dghp,gk->dgpkh", cc, eye).reshape(2, SSM_CH, D_SSM)


def _diag_blocks_in(m):
    return jnp.einsum("dghgp->dgph", m.reshape(2, SSM_GROUPS, SSM_GROUP, SSM_GROUPS, SSM_STATE))


def _diag_blocks_out(m):
    return jnp.einsum("dgpgh->dghp", m.reshape(2, SSM_GROUPS, SSM_STATE, SSM_GROUPS, SSM_GROUP))


def _device_step(x, mem, target, gw, sp, reducer=None):
    t = x.shape[0]
    tm = min(TM, t)
    w_f1, w_mix, w_kv, w_e, w_f2 = gw
    w_f1 = w_f1.reshape(N_SHARD, 3, FF_SH, D_MODEL)
    w_f2 = w_f2.reshape(N_SHARD, 3, FF_SH, D_MODEL)
    w_d = w_kv[:, None]
    w_in_t = w_mix[:, :FF_SH].reshape(D_FF, D_MODEL)
    w_mo, w_q, w_xo = (w_mix[:, FF_SH + 256 * k:FF_SH + 256 * (k + 1)].reshape(D_MODEL, D_MODEL) for k in range(3))
    g = {}

    u1 = _rmsnorm("norm_ffn1", x, sp["ffn1_norm"], tm)
    g1, up1, a1 = _ffn_up("ffn1_up", u1, w_f1, tm)
    h1 = _ffn_down("ffn1_down", a1, w_f1, x, tm)

    u2 = _rmsnorm("norm_mix", h1, sp["mix_norm"], tm)
    proj = _mm("mix_in", [(u2, BS((tm, D_MODEL), lambda j, i: (i, 0)), w_in_t, BS((D_FF // 2, D_MODEL), lambda j, i: (j, 0)), NT)],
               grid=(2, t // tm), out_shape=SDS((t, D_FF), F32), out_spec=BS((tm, D_FF // 2), lambda j, i: (i, j)))
    pooled, mixed, ms = _pool_fwd(proj, sp["pool_w"][0], sp["pool_scale"])

    ar = sp["ssm_a_re"].reshape(2 * SSM_GROUPS, SSM_STATE)
    ai = sp["ssm_a_im"].reshape(2 * SSM_GROUPS, SSM_STATE)
    ldt = sp["ssm_log_dt"].reshape(2 * SSM_GROUPS, 1)
    abr, abi, qr, qi = _ssm_disc(ar, ai, ldt)
    b_r = sp["ssm_b_re"].reshape(2 * SSM_CH, SSM_GROUP)
    b_i = sp["ssm_b_im"].reshape(2 * SSM_CH, SSM_GROUP)
    qr_col, qi_col = qr.reshape(2 * SSM_CH, 1), qi.reshape(2 * SSM_CH, 1)
    bbr, bbi = _ssm_bbar(qr_col, qi_col, b_r, b_i)
    shape_b = (2, SSM_GROUPS, SSM_STATE, SSM_GROUP)
    b_mat = jnp.concatenate([_block_diag_in(bbr.reshape(shape_b)), _block_diag_in(bbi.reshape(shape_b))], axis=-1).astype(BF16)
    c_mat = jnp.concatenate([_block_diag_out(sp["ssm_c_re"][0]), -_block_diag_out(sp["ssm_c_im"][0])], axis=1).astype(BF16)
    b_mat_t = jnp.swapaxes(b_mat, 1, 2)
    c_mat_t = jnp.swapaxes(c_mat, 1, 2)
    a_r = abr.reshape(2, 1, SSM_CH)
    a_i = abi.reshape(2, 1, SSM_CH)

    s_in =proj[:, D_POOL:D_POOL + D_SSM].astype(BF16)
    states, y_dirs = [], []
    for dr in range(2):
        st, yd = _ssm_scan(f"ssm_scan_fwd{dr}", s_in, b_mat[dr], a_r[dr], a_i[dr], c_mat[dr], reverse=(dr == 1))
        states.append(st)
        y_dirs.append(yd)
    y_total, yssm = _ssm_combine(proj, jnp.stack(y_dirs), sp["ssm_d"], tm)

    merged = _mixer_merge(ms, yssm, w_e, proj, tm)
    h2 = _plain_mm("mix_out", merged, w_mo, NN, F32, tm, resid=h1)

    u3 = _rmsnorm("norm_xattn", h2, sp["xattn_norm"], tm)
    mem_n = _rmsnorm("norm_mem", mem, sp["mem_norm"], mem.shape[0])
    q = _plain_mm("attn_q", u3, w_q, NN, BF16, tm)
    n_mem = mem.shape[0]
    kv = _mm("attn_kv", [(mem_n, BS((n_mem, D_MODEL), lambda s: (0, 0)), w_d, BS((None, None, D_MODEL, 512), lambda s: (s, 0, 0, 0)), NN)],
             grid=(N_SHARD,), out_shape=SDS((n_mem, 2 * D_MODEL), BF16), out_spec=BS((n_mem, 512), lambda s: (0, s)))
    o = _attn_fwd(q, kv, tm)
    h3 = _plain_mm("attn_out", o, w_xo, NN, F32, tm, resid=h2)

    u4 = _rmsnorm("norm_ffn2", h3, sp["ffn2_norm"], tm)
    g2, up2, a2 = _ffn_up("ffn2_up", u4, w_f2, tm)
    h4 = _ffn_down("ffn2_down", a2, w_f2, h3, tm)

    loss, dh4, dh4_b, g["final_norm"] = _loss_head(h4, sp["final_norm"].reshape(1, D_MODEL), target, tm)

    dg2, dup2 = _ffn_bwd_act("ffn2_bwd_act", dh4_b, w_f2, g2, up2, tm)
    dw_f2 = _ffn_dw("ffn2_dw", u4, dg2, dup2, a2, dh4_b, tm)
    du4 = _ffn_dx("ffn2_dx", dg2, dup2, w_f2, tm)
    dh3, dh3_b, g["ffn2_norm"] = _rmsnorm_bwd("norm_ffn2_bwd", h3, sp["ffn2_norm"], du4, dh4, tm)

    d_o = _plain_mm("attn_out_dx", dh3_b, w_xo, NT, BF16, tm)
    dw_xo = _dw_mm("attn_out_dw", o, dh3_b, tm)
    dq, dkv = _attn_bwd(q, kv, d_o, tm)
    dw_q = _dw_mm("attn_q_dw", u3, dq, tm)
    du3 = _plain_mm("attn_q_dx", dq, w_q, NT, F32, tm)
    dw_kv = _mm("attn_kv_dw", [(mem_n, BS((n_mem, D_MODEL), lambda s: (0, 0)), dkv, BS((n_mem, 512), lambda s: (0, s)), TN)],
                grid=(N_SHARD,), out_shape=SDS((N_SHARD, D_MODEL, 512), BF16), out_spec=BS((None, D_MODEL, 512), lambda s: (s, 0, 0)))
    dmem_n = _mm("attn_kv_dx", [(dkv, BS((n_mem, 512), lambda s: (0, s)), w_d, BS((None, None, D_MODEL, 512), lambda s: (s, 0, 0, 0)), NT)],
                 grid=(N_SHARD,), red_axis=0, out_shape=SDS((n_mem, D_MODEL), F32), out_spec=BS((n_mem, D_MODEL), lambda s: (0, 0)))
    _, _, g["mem_norm"] = _rmsnorm_bwd("norm_mem_bwd", mem, sp["mem_norm"], dmem_n, None, n_mem)
    dh2, dh2_b, g["xattn_norm"] = _rmsnorm_bwd("norm_xattn_bwd", h2, sp["xattn_norm"], du3, dh3, tm)

    dmerged = _plain_mm("mix_out_dx", dh2_b, w_mo, NT, BF16, tm)
    dw_mo = _dw_mm("mix_out_dw", merged, dh2_b, tm)
    d_gp, d_gs, dzp, dzv, dzg = _mixer_merge_bwd(ms, yssm, w_e, proj, dmerged, tm)
    dw_e = _mixer_dw(ms, yssm, dzp, dzv, dzg, tm)
    d_ms, d_yt = _mixer_dx(dzp, dzv, dzg, w_e, y_total, tm)
    dp, d_scale, d_pw = _pool_bwd(d_ms, mixed, pooled, sp["pool_w"][0], sp["pool_scale"])
    g["pool_scale"] = d_scale
    g["pool_w"] = d_pw[None]

    d_yt_b = d_yt.astype(BF16)
    du_dirs, d_abr, d_abi, d_cm, d_bm = [], [], [], [], []
    for dr in range(2):
        lam, du = _ssm_scan(f"ssm_scan_bwd{dr}", d_yt_b, c_mat_t[dr], a_r[dr], -a_i[dr], b_mat_t[dr], reverse=(dr == 0))
        du_dirs.append(du)
        da_r, da_i = _ssm_da(f"ssm_da{dr}", lam, states[dr], reverse=(dr == 1))
        d_abr.append(da_r)
        d_abi.append(da_i)
        d_cm.append(_dw_mm(f"ssm_dc{dr}", states[dr], d_yt_b, tm, F32))
        d_bm.append(_dw_mm(f"ssm_db{dr}", s_in, lam, tm, F32))
    d_cm = jnp.stack(d_cm)
    d_bm = jnp.stack(d_bm)
    g["ssm_c_re"] = _diag_blocks_out(d_cm[:, :SSM_CH])[None]
    g["ssm_c_im"] = -_diag_blocks_out(d_cm[:, SSM_CH:])[None]
    g_bbr = _diag_blocks_in(d_bm[:, :, :SSM_CH]).reshape(2 * SSM_CH, SSM_GROUP)
    g_bbi = _diag_blocks_in(d_bm[:, :, SSM_CH:]).reshape(2 * SSM_CH, SSM_GROUP)
    d_qr, d_qi, d_br, d_bi = _ssm_bbar_bwd(qr_col, qi_col, b_r, b_i, g_bbr, g_bbi)
    g["ssm_b_re"] = d_br.reshape(sp["ssm_b_re"].shape)
    g["ssm_b_im"] = d_bi.reshape(sp["ssm_b_im"].shape)
    d_ar, d_ai, d_ldt = _ssm_disc_bwd(ar, ai, ldt, jnp.stack(d_abr).reshape(ar.shape), jnp.stack(d_abi).reshape(ar.shape),
                                      d_qr.reshape(ar.shape), d_qi.reshape(ar.shape))
    g["ssm_a_re"] = d_ar.reshape(sp["ssm_a_re"].shape)
    g["ssm_a_im"] = d_ai.reshape(sp["ssm_a_im"].shape)
    g["ssm_log_dt"] = d_ldt.reshape(sp["ssm_log_dt"].shape)
    ds, g["ssm_d"] = _ssm_ds(proj, d_yt, jnp.stack(du_dirs), sp["ssm_d"], tm)

    d_proj = jnp.concatenate([dp, ds, d_gp, d_gs], axis=1)
    dw_in_t = _mm("mix_in_dw", [(d_proj, BS((tm, D_FF // 2), lambda j, i: (i, j)), u2, BS((tm, D_MODEL), lambda j, i: (i, 0)), TN)],
                  grid=(2, t // tm), red_axis=1, out_shape=SDS((D_FF, D_MODEL), BF16), out_spec=BS((D_FF // 2, D_MODEL), lambda j, i: (j, 0)))
    du2 = _plain_mm("mix_in_dx", d_proj, w_in_t, NN, F32, tm)
    dh1, dh1_b, g["mix_norm"] = _rmsnorm_bwd("norm_mix_bwd", h1, sp["mix_norm"], du2, dh2, tm)

    square = (N_SHARD, D_MODEL // N_SHARD, D_MODEL)
    early = [dw_f2.reshape(N_SHARD, 3 * FF_SH, D_MODEL), dw_xo.reshape(square), dw_q.reshape(square), dw_kv,
             dw_mo.reshape(square), dw_e, dw_in_t.reshape(N_SHARD, FF_SH, D_MODEL)]
    g["final_norm"] = g["final_norm"].reshape(D_MODEL)

    if reducer is not None:
        dh1_b = _tie(dh1_b, reducer.start("a", early))
    dg1, dup1 = _ffn_bwd_act("ffn1_bwd_act", dh1_b, w_f1, g1, up1, tm)
    dw_f1 = _ffn_dw("ffn1_dw", u1, dg1, dup1, a1, dh1_b, tm).reshape(N_SHARD, 3 * FF_SH, D_MODEL)
    if reducer is not None:
        landed = reducer.finish("a", dw_f1)
        dg1 = _tie(dg1, reducer.start("b", [_tie(dw_f1, landed)]))
    du1 = _ffn_dx("ffn1_dx", dg1, dup1, w_f1, tm)
    grad_x, _, g["ffn1_norm"] = _rmsnorm_bwd("norm_ffn1_bwd", x, sp["ffn1_norm"], du1, dh1, tm)
    if reducer is not None:
        reducer.finish("b", grad_x)
    return loss, grad_x, early + [dw_f1], g


def _mesh_place():
    x, y, c = lax.axis_index("x"), lax.axis_index("y"), lax.axis_index("c")
    chips = [(1 - x, y), (x, 1 - y), (1 - x, 1 - y)]
    return x, y, c, chips


def _remote(src, dst, send_sems, recv_sems, k, to):
    return pltpu.make_async_remote_copy(src_ref=src, dst_ref=dst, send_sem=send_sems.at[k], recv_sem=recv_sems.at[k],
                                        device_id=to, device_id_type=MESH)


def _allgather_shards(shards):
    n = len(shards)

    def body(*refs):
        ins, outs = refs[:n], refs[n:2 * n]
        send_sems, recv_sems, local_sems = refs[2 * n:]
        x, y, c, chips = _mesh_place()
        me = 2 * x + y
        sibling = (x, y, 1 - c)
        started, local_copies = [], []
        for k in range(n):
            half = shards[k].shape[0] // 2
            mine = pl.ds(pl.multiple_of(c * half, 16), half)
            local = pltpu.make_async_copy(ins[k], outs[k].at[me], local_sems.at[k])
            local.start()
            local_copies.append(local)
            for j, (px, py) in enumerate(chips):
                cp = _remote(ins[k].at[mine, :], outs[k].at[me, mine, :], send_sems, recv_sems, 6 * k + j, (px, py, c))
                cp.start()
                started.append(cp)
        for k in range(n):
            half = shards[k].shape[0] // 2
            mine = pl.ds(pl.multiple_of(c * half, 16), half)
            for j, (px, py) in enumerate(chips):
                blk = outs[k].at[2 * px + py, mine, :]
                _remote(blk, blk, send_sems, recv_sems, 6 * k + j, (px, py, c)).wait_recv()
                fwd = _remote(blk, blk, send_sems, recv_sems, 6 * k + 3 + j, sibling)
                fwd.start()
                started.append(fwd)
        for k in range(n):
            half = shards[k].shape[0] // 2
            theirs = pl.ds(pl.multiple_of((1 - c) * half, 16), half)
            for j, (px, py) in enumerate(chips):
                blk = outs[k].at[2 * px + py, theirs, :]
                _remote(blk, blk, send_sems, recv_sems, 6 * k + 3 + j, sibling).wait_recv()
        for cp in started:
            cp.wait_send()
        for cp in local_copies:
            cp.wait()

    hbm = BS(memory_space=pl.ANY)
    return pl.pallas_call(
        body, out_shape=tuple(SDS((N_SHARD,) + s.shape, s.dtype) for s in shards),
        in_specs=[hbm] * n, out_specs=(hbm,) * n,
        scratch_shapes=[pltpu.SemaphoreType.DMA((6 * n,)), pltpu.SemaphoreType.DMA((6 * n,)), pltpu.SemaphoreType.DMA((n,))],
        name="allgather_weights", compiler_params=_params())(*shards)


def _sibling_swap_halves(tag, grads):
    n = len(grads)

    def body(*refs):
        ins, outs = refs[:n], refs[n:2 * n]
        send_sems, recv_sems = refs[2 * n:]
        x, y, c, _ = _mesh_place()
        sibling = (x, y, 1 - c)
        copies = []
        for k in range(n):
            half = grads[k].shape[1] // 2
            theirs = pl.ds(pl.multiple_of((1 - c) * half, 16), half)
            cp = _remote(ins[k].at[:, theirs, :], outs[k], send_sems, recv_sems, k, sibling)
            cp.start()
            copies.append(cp)
        for cp in copies:
            cp.wait_recv()
        for cp in copies:
            cp.wait_send()

    hbm = BS(memory_space=pl.ANY)
    return pl.pallas_call(
        body, out_shape=tuple(SDS((g.shape[0], g.shape[1] // 2, g.shape[2]), g.dtype) for g in grads),
        in_specs=[hbm] * n, out_specs=(hbm,) * n,
        scratch_shapes=[pltpu.SemaphoreType.DMA((n,)), pltpu.SemaphoreType.DMA((n,))],
        name="reduce_sibling_send_" + tag, compiler_params=_params())(*grads)


def _row_tile(rows, cap=512):
    return max(r for r in range(16, cap + 1, 16) if rows % r == 0)


def _chip_presum(k, grad, got, c_idx):
    n_sh, rows, cols = grad.shape
    half = rows // 2
    tr = _row_tile(half)
    grad4 = grad.reshape(n_sh, 2, half, cols)

    def body(c_ref, a_ref, b_ref, o_ref):
        o_ref[...] = (a_ref[...].astype(F32) + b_ref[...].astype(F32)).astype(o_ref.dtype)

    return pl.pallas_call(
        body, out_shape=SDS((n_sh, half, cols), BF16),
        grid_spec=pltpu.PrefetchScalarGridSpec(
            num_scalar_prefetch=1, grid=(n_sh, half // tr),
            in_specs=[BS((None, None, tr, cols), lambda s, i, c_ref: (s, c_ref[0], i, 0)),
                      BS((None, tr, cols), lambda s, i, c_ref: (s, i, 0))],
            out_specs=BS((None, tr, cols), lambda s, i, c_ref: (s, i, 0))),
        name=f"reduce_presum{k}", compiler_params=_params("parallel", "parallel"))(c_idx, grad4, got)


HBM_SPEC = BS(memory_space=pltpu.HBM)
SEM_SPEC = BS(memory_space=pltpu.SEMAPHORE)
DATAFLOW = pltpu.SideEffectType.DATAFLOW_SIDE_EFFECTING


def _chip_exchange_copies(parts, lands, send_sems, recv_sems):
    _, _, c, chips = _mesh_place()
    return [_remote(parts[k].at[2 * px + py], lands[k].at[j], send_sems, recv_sems, 3 * k + j, (px, py, c))
            for k in range(len(parts)) for j, (px, py) in enumerate(chips)]


def _chip_exchange_start(tag, parts):
    n = len(parts)

    def body(*refs):
        ins, lands = refs[:n], refs[n:2 * n]
        send_sems, recv_sems = refs[2 * n], refs[2 * n + 1]
        token = refs[-1]
        for cp in _chip_exchange_copies(ins, lands, send_sems, recv_sems):
            cp.start()
        token[...] = jnp.zeros_like(token)

    lands = [pltpu.with_memory_space_constraint(lax.empty((3,) + p.shape[1:], p.dtype), pltpu.HBM) for p in parts]
    parts = [pltpu.with_memory_space_constraint(p, pltpu.HBM) for p in parts]
    thru = [pltpu.HBM(a.shape, a.dtype) for a in parts + lands]
    out = pl.pallas_call(
        body, name="reduce_exchange_start_" + tag,
        out_shape=(pltpu.SemaphoreType.DMA((3 * n,)), pltpu.SemaphoreType.DMA((3 * n,)), *thru, SDS((8, 128), F32)),
        in_specs=[HBM_SPEC] * (2 * n), out_specs=(SEM_SPEC, SEM_SPEC, *[HBM_SPEC] * (2 * n), BS(memory_space=pltpu.VMEM)),
        input_output_aliases={i: 2 + i for i in range(2 * n)},
        compiler_params=pltpu.CompilerParams(has_side_effects=DATAFLOW))(*parts, *lands)
    return out[0], out[1], list(out[2:2 + n]), list(out[2 + n:2 + 2 * n]), out[-1]


def _chip_exchange_wait(tag, send_sems, recv_sems, parts, lands, after):
    n = len(parts)

    def body(*refs):
        ins, zones = refs[:n], refs[n:2 * n]
        for cp in _chip_exchange_copies(ins, zones, refs[2 * n], refs[2 * n + 1]):
            cp.wait_send()
            cp.wait_recv()

    out = pl.pallas_call(
        body, name="reduce_exchange_wait_" + tag,
        out_shape=tuple(pltpu.HBM(a.shape, a.dtype) for a in parts + lands),
        in_specs=[HBM_SPEC] * (2 * n) + [SEM_SPEC, SEM_SPEC, BS(memory_space=pl.ANY)], out_specs=(HBM_SPEC,) * (2 * n),
        input_output_aliases={i: i for i in range(2 * n)},
        compiler_params=pltpu.CompilerParams(has_side_effects=DATAFLOW))(*parts, *lands, send_sems, recv_sems, after)
    return list(out[:n]), list(out[n:])


def _tie(value, token):
    return lax.optimization_barrier((value, token))[0]


class _GradReducer:
    def __init__(self):
        self.c_idx = lax.axis_index("c").astype(jnp.int32).reshape(1)
        self.parts, self.landed, self.open = [], [], {}

    def start(self, tag, grads):
        got = _sibling_swap_halves(tag, grads)
        parts = [_chip_presum(f"{tag}{k}", g, s, self.c_idx) for k, (g, s) in enumerate(zip(grads, got))]
        self.open[tag] = _chip_exchange_start(tag, parts)
        return self.open[tag][-1]

    def finish(self, tag, after):
        send_sems, recv_sems, parts, lands, _ = self.open.pop(tag)
        parts, landed = _chip_exchange_wait(tag, send_sems, recv_sems, parts, lands, after)
        self.parts += parts
        self.landed += landed
        return landed[0]


def _chip_sum(k, part, got, place):
    _, half, cols = part.shape
    tr = _row_tile(half)
    n_t = half // tr

    def body(place_ref, a_ref, b_ref, o_ref):
        acc = a_ref[...].astype(F32)
        for j in range(3):
            acc = acc + b_ref[j].astype(F32)
        o_ref[...] = acc

    return pl.pallas_call(
        body, out_shape=SDS((2 * half, cols), F32),
        grid_spec=pltpu.PrefetchScalarGridSpec(
            num_scalar_prefetch=1, grid=(n_t,),
            in_specs=[BS((None, tr, cols), lambda i, place_ref: (place_ref[0], i, 0)),
                      BS((3, tr, cols), lambda i, place_ref: (0, i, 0))],
            out_specs=BS((tr, cols), lambda i, place_ref: (place_ref[1] * n_t + i, 0))),
        name=f"reduce_sum{k}", compiler_params=_params("parallel"))(place, part, got)


def _sibling_join_halves(fulls):
    n = len(fulls)

    def body(*refs):
        outs = refs[n:2 * n]
        send_sems, recv_sems = refs[2 * n:]
        x, y, c, _ = _mesh_place()
        sibling = (x, y, 1 - c)
        sent = []
        for k in range(n):
            half = fulls[k].shape[0] // 2
            mine = outs[k].at[pl.ds(pl.multiple_of(c * half, 8), half), :]
            cp = _remote(mine, mine, send_sems, recv_sems, k, sibling)
            cp.start()
            sent.append(cp)
        for k in range(n):
            half = fulls[k].shape[0] // 2
            theirs = outs[k].at[pl.ds(pl.multiple_of((1 - c) * half, 8), half), :]
            _remote(theirs, theirs, send_sems, recv_sems, k, sibling).wait_recv()
        for cp in sent:
            cp.wait_send()

    hbm = BS(memory_space=pl.ANY)
    return pl.pallas_call(
        body, out_shape=tuple(SDS(f.shape, f.dtype) for f in fulls),
        in_specs=[hbm] * n, out_specs=(hbm,) * n, input_output_aliases={k: k for k in range(n)},
        scratch_shapes=[pltpu.SemaphoreType.DMA((n,)), pltpu.SemaphoreType.DMA((n,))],
        name="reduce_sibling_join", compiler_params=_params())(*fulls)


N_DEV = 8


def _allreduce_small(part):
    rows, lanes = part.shape

    def body(x_ref, out_ref, all_ref, send_sems, recv_sems, local_sem):
        x, y, c, chips = _mesh_place()
        me, sibling = (x, y, c), (x, y, 1 - c)

        def blk(px, py, pc):
            return all_ref.at[pl.ds(pl.multiple_of((4 * px + 2 * py + pc) * rows, 8), rows), :]

        def copy(k, block, to, src=None):
            return _remote(blk(*block) if src is None else src, blk(*block), send_sems, recv_sems, k, to)

        mine = pltpu.make_async_copy(x_ref, blk(*me), local_sem)
        mine.start()
        first = [copy(0, me, sibling, src=x_ref)]
        first += [copy(1 + j, me, (*chip, c), src=x_ref) for j, chip in enumerate(chips)]
        for cp in first:
            cp.start()
        passed = [copy(4 + j, (*chip, c), sibling) for j, chip in enumerate(chips)]
        for j, chip in enumerate(chips):
            copy(1 + j, (*chip, c), me).wait_recv()
            passed[j].start()
        copy(0, sibling, me).wait_recv()
        for j, chip in enumerate(chips):
            copy(4 + j, (*chip, 1 - c), me).wait_recv()
        for cp in first + passed:
            cp.wait_send()
        mine.wait()
        acc = all_ref[pl.ds(0, rows), :]
        for dev in range(1, N_DEV):
            acc = acc + all_ref[pl.ds(dev * rows, rows), :]
        out_ref[...] = acc

    vm = BS(memory_space=pltpu.VMEM)
    return pl.pallas_call(
        body, out_shape=SDS((rows, lanes), F32), in_specs=[vm], out_specs=vm,
        scratch_shapes=[pltpu.VMEM((N_DEV * rows, lanes), F32), pltpu.SemaphoreType.DMA((7,)),
                        pltpu.SemaphoreType.DMA((7,)), pltpu.SemaphoreType.DMA],
        name="allreduce_small", compiler_params=_params())(part)


def _adamw(name, w, grad, row0, m, v):
    rows, cols = w.shape
    tr = rows if rows < 16 else _row_tile(rows, 256)
    bc1 = 1.0 - ADAM_B1 ** ADAM_STEP
    bc2 = 1.0 - ADAM_B2 ** ADAM_STEP

    def body(w_ref, g_ref, m_ref, v_ref, go_ref, d_ref, mo_ref, vo_ref):
        g = g_ref[...]
        m_new = ADAM_B1 * m_ref[...] + (1.0 - ADAM_B1) * g
        v_new = ADAM_B2 * v_ref[...] + (1.0 - ADAM_B2) * (g * g)
        go_ref[...] = g
        mo_ref[...] = m_new
        vo_ref[...] = v_new
        d_ref[...] = -ADAM_LR * ((m_new / bc1) / (jnp.sqrt(v_new / bc2) + ADAM_EPS) + ADAM_WD * w_ref[...])

    blk = BS((tr, cols), lambda i: (i, 0))
    shape = SDS((rows, cols), F32)
    return pl.pallas_call(
        body, out_shape=(shape,) * 4, grid=(rows // tr,),
        in_specs=[blk, BS((tr, cols), lambda i: (row0 // tr + i, 0)), blk, blk], out_specs=(blk,) * 4,
        name=name, compiler_params=_params("parallel"))(w, grad, m, v)


SMALL_LANES = 128


def _pack_small(parts):
    flat = jnp.concatenate([jnp.ravel(p) for p in parts])
    rows = -(-flat.shape[0] // (64 * SMALL_LANES)) * 64
    return jnp.pad(flat, (0, rows * SMALL_LANES - flat.shape[0])).reshape(rows, SMALL_LANES)


def _unpack_small(packed, like):
    flat = jnp.ravel(packed)
    out, at = [], 0
    for p in like:
        out.append(flat[at:at + p.size].reshape(p.shape))
        at += p.size
    return out


def kernel(x, mem, ffn1_norm, ffn1_w_gate, ffn1_w_up, ffn1_w_down, mix_norm, w_in, pool_w, pool_scale, w_pool_proj, ssm_a_re, ssm_a_im, ssm_log_dt, ssm_b_re, ssm_b_im, ssm_c_re, ssm_c_im, ssm_d, w_glu_val, w_glu_gate, w_mix_out, xattn_norm, mem_norm, w_q, w_kv, w_xo, ffn2_norm, ffn2_w_gate, ffn2_w_up, ffn2_w_down, final_norm, loss_target, m_ffn1_norm, m_ffn1_w_gate, m_ffn1_w_up, m_ffn1_w_down, m_mix_norm, m_w_in, m_pool_w, m_pool_scale, m_w_pool_proj, m_ssm_a_re, m_ssm_a_im, m_ssm_log_dt, m_ssm_b_re, m_ssm_b_im, m_ssm_c_re, m_ssm_c_im, m_ssm_d, m_w_glu_val, m_w_glu_gate, m_w_mix_out, m_xattn_norm, m_mem_norm, m_w_q, m_w_kv, m_w_xo, m_ffn2_norm, m_ffn2_w_gate, m_ffn2_w_up, m_ffn2_w_down, m_final_norm, v_ffn1_norm, v_ffn1_w_gate, v_ffn1_w_up, v_ffn1_w_down, v_mix_norm, v_w_in, v_pool_w, v_pool_scale, v_w_pool_proj, v_ssm_a_re, v_ssm_a_im, v_ssm_log_dt, v_ssm_b_re, v_ssm_b_im, v_ssm_c_re, v_ssm_c_im, v_ssm_d, v_w_glu_val, v_w_glu_gate, v_w_mix_out, v_xattn_norm, v_mem_norm, v_w_q, v_w_kv, v_w_xo, v_ffn2_norm, v_ffn2_w_gate, v_ffn2_w_up, v_ffn2_w_down, v_final_norm):
    given = dict(locals())
    w = {n: given[n] for n in WEIGHTS}
    m = {n: given["m_" + n] for n in WEIGHTS}
    v = {n: given["v_" + n] for n in WEIGHTS}

    def shard_view(a, n):
        return a[0].T if n in TRANSPOSED else a[0]

    def shard_unview(a, n):
        return (a.T if n in TRANSPOSED else a)[None]

    shards = [jnp.concatenate([shard_view(w[n], n).astype(BF16) for n in grp], axis=0) for grp in GATHER_GROUPS]
    gathered = _allgather_shards(shards)

    reducer = _GradReducer()
    loss_part, grad_x, _, small = _device_step(x[0], mem[0], loss_target[0], gathered, {n: w[n] for n in SMALL}, reducer)
    loss = lax.psum(loss_part[0, 0], ("x", "y", "c"))

    place = jnp.stack([2 * lax.axis_index("x") + lax.axis_index("y"), lax.axis_index("c")]).astype(jnp.int32)
    halves = [_chip_sum(k, part, got, place) for k, (part, got) in enumerate(zip(reducer.parts, reducer.landed))]
    reduced = _sibling_join_halves(halves)

    grads, delta, new_m, new_v = {}, {}, {}, {}
    for grp, red in zip(REDUCE_GROUPS, reduced):
        row0 = 0
        for n in grp:
            w_n = shard_view(w[n], n)
            grads[n], delta[n], new_m[n], new_v[n] = (
                shard_unview(o, n) for o in _adamw("adamw_" + n, w_n, red, row0, shard_view(m[n], n), shard_view(v[n], n)))
            row0 += w_n.shape[0]

    g_small = dict(zip(SMALL, _unpack_small(_allreduce_small(_pack_small([small[n] for n in SMALL])), [w[n] for n in SMALL])))
    narrow = [n for n in SMALL if w[n].ndim > 3]
    dense = [n for n in SMALL if n not in narrow]
    for n in narrow:
        two_d = (-1, w[n].shape[-1])
        outs = _adamw("adamw_" + n, w[n].reshape(two_d), g_small[n].reshape(two_d), 0, m[n].reshape(two_d), v[n].reshape(two_d))
        grads[n], delta[n], new_m[n], new_v[n] = (o.reshape(w[n].shape) for o in outs)
    dense_like = [w[n] for n in dense]
    packed = _adamw("adamw_small", _pack_small(dense_like), _pack_small([g_small[n] for n in dense]), 0,
                    _pack_small([m[n] for n in dense]), _pack_small([v[n] for n in dense]))
    for out, store in zip(packed, (grads, delta, new_m, new_v)):
        for n, val in zip(dense, _unpack_small(out, dense_like)):
            store[n] = val

    return (loss, grad_x[None], *[grads[n] for n in WEIGHTS], *[delta[n] for n in WEIGHTS],
            *[new_m[n] for n in WEIGHTS], *[new_v[n] for n in WEIGHTS])
```

```python
import functools
import math

import jax
import jax.numpy as jnp
from jax import lax
from jax.experimental import pallas as pl
from jax.experimental.pallas import tpu as pltpu

F32 = jnp.float32
BF16 = jnp.bfloat16
SDS = jax.ShapeDtypeStruct
BS = pl.BlockSpec
MESH = pl.DeviceIdType.MESH

D_MODEL = 1024
D_FF = 2816
N_SHARD = 4
FF_SH = D_FF // N_SHARD
D_POOL = 512
POOL_WINDOWS = (2, 4, 8, 16)
POOL_GROUP = 128
D_SSM = 256
SSM_GROUPS = 16
SSM_GROUP = 16
SSM_STATE = 64
SSM_CH = SSM_GROUPS * SSM_STATE
N_HEADS = 4
HEAD_DIM = 256
EPS = 1e-6
ADAM_LR, ADAM_B1, ADAM_B2, ADAM_EPS, ADAM_WD, ADAM_STEP = 0.001, 0.9, 0.999, 1e-08, 0.01, 10

VMEM_LIMIT_V7X = 52 * 1024 * 1024
TM = 512

NN = (((1,), (0,)), ((), ()))
NT = (((1,), (1,)), ((), ()))
TN = (((0,), (0,)), ((), ()))


def _params(*sem):
    return pltpu.CompilerParams(dimension_semantics=sem if sem else None, vmem_limit_bytes=VMEM_LIMIT_V7X)


def _dot(a, b, dims=NN):
    return lax.dot_general(a.astype(BF16), b.astype(BF16), dims, preferred_element_type=F32)


def _sigmoid(v):
    return 1.0 / (1.0 + jnp.exp(-v))


def _block_dims(spec):
    return tuple(d for d in spec.block_shape if d is not None)


def _after_operands(after):
    return list(after), [BS(memory_space=pl.ANY)] * len(after)


def _mm(name, pairs, *, grid, out_shape, out_spec, red_axis=None, extras=(), epilogue=None, after=()):
    n_pairs, n_extra = len(pairs), len(extras)
    n_red = grid[red_axis] if red_axis is not None else 1
    dims = [p[4] for p in pairs]

    def body(*refs):
        ab = refs[:2 * n_pairs]
        ex = refs[2 * n_pairs:2 * n_pairs + n_extra]
        o_ref = refs[2 * n_pairs + n_extra + len(after)]

        def partial():
            acc = None
            for p in range(n_pairs):
                t = _dot(ab[2 * p][...], ab[2 * p + 1][...], dims[p])
                acc = t if acc is None else acc + t
            return acc

        def finish(acc):
            res = epilogue(acc, *[e[...] for e in ex]) if epilogue is not None else acc
            o_ref[...] = res.astype(o_ref.dtype)

        if n_red == 1:
            finish(partial())
        else:
            acc_ref = refs[-1]
            k = pl.program_id(red_axis)

            @pl.when(k == 0)
            def _():
                acc_ref[...] = jnp.zeros_like(acc_ref)

            acc_ref[...] += partial()

            @pl.when(k == n_red - 1)
            def _():
                finish(acc_ref[...])

    operands, in_specs = [], []
    for a, a_spec, b, b_spec, _ in pairs:
        operands += [a, b]
        in_specs += [a_spec, b_spec]
    for e, e_spec in extras:
        operands.append(e)
        in_specs.append(e_spec)
    after_ops, after_specs = _after_operands(after)
    operands += after_ops
    in_specs += after_specs
    scratch = [pltpu.VMEM(_block_dims(out_spec), F32)] if n_red > 1 else []
    sem = tuple("arbitrary" if ax == red_axis else "parallel" for ax in range(len(grid)))
    return pl.pallas_call(body, out_shape=out_shape, grid=grid, in_specs=in_specs, out_specs=out_spec,
                          scratch_shapes=scratch, name=name, compiler_params=_params(*sem))(*operands)


def _rmsnorm(name, h, gain, tm):
    t, d = h.shape

    def body(h_ref, g_ref, u_ref):
        hv = h_ref[...]
        r = lax.rsqrt(jnp.mean(hv * hv, axis=-1, keepdims=True) + EPS)
        u_ref[...] = ((hv * r) * g_ref[...]).astype(u_ref.dtype)

    return pl.pallas_call(
        body, out_shape=SDS((t, d), BF16), grid=(t // tm,),
        in_specs=[BS((tm, d), lambda i: (i, 0)), BS((1, d), lambda i: (0, 0))],
        out_specs=BS((tm, d), lambda i: (i, 0)), name=name, compiler_params=_params("parallel"))(h, gain)


def _rmsnorm_bwd(name, h, gain, du, dh_in, tm):
    t, d = h.shape
    has_in = dh_in is not None

    def body(*refs):
        if has_in:
            h_ref, g_ref, du_ref, dhin_ref, dh_ref, dhb_ref, dg_ref = refs
        else:
            h_ref, g_ref, du_ref, dh_ref, dhb_ref, dg_ref = refs
        i = pl.program_id(0)
        hv = h_ref[...]
        r = lax.rsqrt(jnp.mean(hv * hv, axis=-1, keepdims=True) + EPS)
        n = hv * r
        duv = du_ref[...].astype(F32)
        dn = duv * g_ref[...]
        dh = r * (dn - n * jnp.mean(dn * n, axis=-1, keepdims=True))
        if has_in:
            dh = dhin_ref[...] + dh
        dh_ref[...] = dh
        dhb_ref[...] = dh.astype(BF16)

        @pl.when(i == 0)
        def _():
            dg_ref[...] = jnp.zeros_like(dg_ref)

        dg_ref[...] += jnp.sum(duv * n, axis=0, keepdims=True)

    row = BS((tm, d), lambda i: (i, 0))
    vec = BS((1, d), lambda i: (0, 0))
    operands = [h, gain, du] + ([dh_in] if has_in else [])
    in_specs = [row, vec, row] + ([row] if has_in else [])
    return pl.pallas_call(
        body, out_shape=(SDS((t, d), F32), SDS((t, d), BF16), SDS((1, d), F32)), grid=(t // tm,),
        in_specs=in_specs, out_specs=(row, row, vec), name=name, compiler_params=_params("arbitrary"))(*operands)


def _loss_head(h, gain, target, tm):
    t, d = h.shape

    def body(h_ref, g_ref, t_ref, loss_ref, dh_ref, dhb_ref, dg_ref):
        i = pl.program_id(0)
        hv = h_ref[...]
        g = g_ref[...]
        r = lax.rsqrt(jnp.mean(hv * hv, axis=-1, keepdims=True) + EPS)
        n = hv * r
        err = n * g - t_ref[...]
        dy = err * (1.0 / d)
        dn = dy * g
        dh = r * (dn - n * jnp.mean(dn * n, axis=-1, keepdims=True))
        dh_ref[...] = dh
        dhb_ref[...] = dh.astype(BF16)

        @pl.when(i == 0)
        def _():
            dg_ref[...] = jnp.zeros_like(dg_ref)
            loss_ref[...] = jnp.zeros_like(loss_ref)

        dg_ref[...] += jnp.sum(dy * n, axis=0, keepdims=True)
        part = 0.5 * jnp.sum(jnp.mean(err * err, axis=-1, keepdims=True), axis=0, keepdims=True)
        loss_ref[...] += jnp.broadcast_to(part, loss_ref.shape)

    row = BS((tm, d), lambda i: (i, 0))
    vec = BS((1, d), lambda i: (0, 0))
    return pl.pallas_call(
        body, out_shape=(SDS((1, 128), F32), SDS((t, d), F32), SDS((t, d), BF16), SDS((1, d), F32)),
        grid=(t // tm,), in_specs=[row, vec, row],
        out_specs=(BS((1, 128), lambda i: (0, 0)), row, row, vec),
        name="loss_head", compiler_params=_params("arbitrary"))(h, gain, target)


FFN_GATE, FFN_UP, FFN_DOWN = 0, 1, 2


def _ffn_w_spec(member, index):
    return BS((None, None, FF_SH, D_MODEL), lambda *g: (g[index], member, 0, 0))


def _ffn_up(name, u, w_f, tm):
    t, d = u.shape

    def body(u_ref, wg_ref, wu_ref, g_ref, up_ref, a_ref):
        uv = u_ref[...]
        g = _dot(uv, wg_ref[...], NT)
        up = _dot(uv, wu_ref[...], NT)
        g_ref[...] = g.astype(BF16)
        up_ref[...] = up.astype(BF16)
        a_ref[...] = (g * _sigmoid(g) * up).astype(BF16)

    hid = BS((None, tm, FF_SH), lambda s, i: (s, i, 0))
    shape = SDS((N_SHARD, t, FF_SH), BF16)
    return pl.pallas_call(
        body, out_shape=(shape, shape, shape), grid=(N_SHARD, t // tm),
        in_specs=[BS((tm, d), lambda s, i: (i, 0)), _ffn_w_spec(FFN_GATE, 0), _ffn_w_spec(FFN_UP, 0)],
        out_specs=(hid, hid, hid), name=name, compiler_params=_params("parallel", "parallel"))(u, w_f, w_f)


def _ffn_down(name, a, w_f, resid, tm):
    t, d = resid.shape
    return _mm(name, [(a, BS((None, tm, FF_SH), lambda i, s: (s, i, 0)), w_f, _ffn_w_spec(FFN_DOWN, 1), NN)],
               grid=(t // tm, N_SHARD), red_axis=1, out_shape=SDS((t, d), F32),
               out_spec=BS((tm, d), lambda i, s: (i, 0)),
               extras=[(resid, BS((tm, d), lambda i, s: (i, 0)))],
               epilogue=lambda acc, res: res + 0.5 * acc)


def _ffn_bwd_act(name, dh_b, w_f, g, up, tm, after=()):
    t, d = dh_b.shape
    after_ops, after_specs = _after_operands(after)

    def body(dh_ref, wd_ref, g_ref, up_ref, *rest):
        dg_ref, dup_ref = rest[len(after_ops):]
        da = 0.5 * _dot(dh_ref[...], wd_ref[...], NT)
        gv = g_ref[...].astype(F32)
        uv = up_ref[...].astype(F32)
        sg = _sigmoid(gv)
        silu = gv * sg
        dg_ref[...] = (da * uv * (sg + silu * (1.0 - sg))).astype(BF16)
        dup_ref[...] = (da * silu).astype(BF16)

    hid = BS((None, tm, FF_SH), lambda s, i: (s, i, 0))
    shape = SDS((N_SHARD, t, FF_SH), BF16)
    return pl.pallas_call(
        body, out_shape=(shape, shape), grid=(N_SHARD, t // tm),
        in_specs=[BS((tm, d), lambda s, i: (i, 0)), _ffn_w_spec(FFN_DOWN, 0), hid, hid] + after_specs,
        out_specs=(hid, hid), name=name, compiler_params=_params("parallel", "parallel"))(dh_b, w_f, g, up, *after_ops)


def _ffn_dw(name, u, dg, dup, a, dh_b, tm):
    t, d = u.shape
    n_t = t // tm

    def body(u_ref, dg_ref, dup_ref, a_ref, dh_ref, o_ref, acc):
        i = pl.program_id(1)

        @pl.when(i == 0)
        def _():
            acc[...] = jnp.zeros_like(acc)

        uv = u_ref[...]
        acc[FFN_GATE] += _dot(dg_ref[...], uv, TN)
        acc[FFN_UP] += _dot(dup_ref[...], uv, TN)
        acc[FFN_DOWN] += _dot(a_ref[...], dh_ref[...], TN)

        @pl.when(i == n_t - 1)
        def _():
            o_ref[FFN_GATE] = acc[FFN_GATE].astype(BF16)
            o_ref[FFN_UP] = acc[FFN_UP].astype(BF16)
            o_ref[FFN_DOWN] = (0.5 * acc[FFN_DOWN]).astype(BF16)

    hid = BS((None, tm, FF_SH), lambda s, i: (s, i, 0))
    row = BS((tm, d), lambda s, i: (i, 0))
    return pl.pallas_call(
        body, out_shape=SDS((N_SHARD, 3, FF_SH, d), BF16), grid=(N_SHARD, n_t),
        in_specs=[row, hid, hid, hid, row], out_specs=BS((None, 3, FF_SH, d), lambda s, i: (s, 0, 0, 0)),
        scratch_shapes=[pltpu.VMEM((3, FF_SH, d), F32)],
        name=name, compiler_params=_params("parallel", "arbitrary"))(u, dg, dup, a, dh_b)


def _ffn_dx(name, dg, dup, w_f, tm, after=()):
    t = dg.shape[1]
    hid = BS((None, tm, FF_SH), lambda i, s: (s, i, 0))
    return _mm(name, [(dg, hid, w_f, _ffn_w_spec(FFN_GATE, 1), NN), (dup, hid, w_f, _ffn_w_spec(FFN_UP, 1), NN)],
               grid=(t // tm, N_SHARD), red_axis=1, out_shape=SDS((t, D_MODEL), F32),
               out_spec=BS((tm, D_MODEL), lambda i, s: (i, 0)), after=after)


def _plain_mm(name, a, b, dims, out_dtype, tm, resid=None):
    t = a.shape[0]
    n = b.shape[1] if dims == NN else b.shape[0]
    extras = [(resid, BS((tm, n), lambda i: (i, 0)))] if resid is not None else []
    epi = (lambda acc, res: res + acc) if resid is not None else None
    return _mm(name, [(a, BS((tm, a.shape[1]), lambda i: (i, 0)), b, BS(b.shape, lambda i: (0, 0)), dims)],
               grid=(t // tm,), out_shape=SDS((t, n), out_dtype), out_spec=BS((tm, n), lambda i: (i, 0)),
               extras=extras, epilogue=epi)


def _dw_mm(name, a, b, tm, out_dtype=BF16):
    t, k = a.shape
    n = b.shape[1]
    return _mm(name, [(a, BS((tm, k), lambda i: (i, 0)), b, BS((tm, n), lambda i: (i, 0)), TN)],
               grid=(t // tm,), red_axis=0, out_shape=SDS((k, n), out_dtype), out_spec=BS((k, n), lambda i: (0, 0)))


POOL_CHUNK = 256
POOL_HALO = 8


def _window_sum(v, width, lead):
    n = v.shape[0]
    s = v
    k = 1
    while k < width:
        s = s + pltpu.roll(s, n - k, 0)
        k *= 2
    return pltpu.roll(s, lead, 0) if lead else s


def _pool_count(base, left, right, t, shape):
    pos = base + lax.broadcasted_iota(jnp.int32, shape, 0)
    lo = jnp.maximum(pos - left, 0)
    hi = jnp.minimum(pos + right + 1, t)
    return (hi - lo).astype(F32)


def _pool_fwd(proj, pool_w, pool_scale):
    t = proj.shape[0]
    c, h = POOL_CHUNK, POOL_HALO
    n_chunks = t // c

    def body(proj_hbm, pw_ref, sc_ref, pooled_ref, mixed_ref, ms_ref, pad_ref, sem):
        cp = pltpu.make_async_copy(proj_hbm.at[:, pl.ds(0, D_POOL)], pad_ref.at[pl.ds(h, t), :], sem)
        cp.start()
        pad_ref[pl.ds(0, h), :] = jnp.zeros((h, D_POOL), F32)
        pad_ref[pl.ds(t + h, h), :] = jnp.zeros((h, D_POOL), F32)
        cp.wait()
        for g, width in enumerate(POOL_WINDOWS):
            left = width // 2
            right = width - 1 - left
            cols = slice(g * POOL_GROUP, (g + 1) * POOL_GROUP)
            wmat = pw_ref[g].astype(BF16)
            scale = sc_ref[:, cols]

            def chunk(ci, carry, left=left, right=right, width=width, cols=cols, wmat=wmat, scale=scale):
                base = pl.multiple_of(ci * c, c)
                v = pad_ref[pl.ds(base, c + 2 * h), cols]
                win = _window_sum(v, width, left)[h:h + c]
                cnt = _pool_count(base, left, right, t, (c, POOL_GROUP))
                pooled = (win / cnt - v[h:h + c]).astype(BF16)
                mixed = _dot(pooled, wmat)
                pooled_ref[pl.ds(base, c), cols] = pooled
                mixed_ref[pl.ds(base, c), cols] = mixed.astype(BF16)
                ms_ref[pl.ds(base, c), cols] = (mixed * scale).astype(BF16)
                return carry

            lax.fori_loop(0, n_chunks, chunk, 0)

    vm = BS(memory_space=pltpu.VMEM)
    shape = SDS((t, D_POOL), BF16)
    return pl.pallas_call(
        body, out_shape=(shape, shape, shape),
        in_specs=[BS(memory_space=pl.ANY), vm, vm], out_specs=(vm, vm, vm),
        scratch_shapes=[pltpu.VMEM((t + 2 * h, D_POOL), F32), pltpu.SemaphoreType.DMA],
        name="pool_fwd", compiler_params=_params())(proj, pool_w, pool_scale)


def _pool_bwd(d_ms, mixed, pooled, pool_w, pool_scale):
    t = d_ms.shape[0]
    c, h = POOL_CHUNK, POOL_HALO
    n_chunks = t // c

    def body(dms_ref, mixed_ref, pooled_ref, pw_ref, sc_ref, dp_ref, dsc_ref, dpw_ref, pad_ref):
        pad_ref[pl.ds(0, h), :] = jnp.zeros((h, D_POOL), F32)
        pad_ref[pl.ds(t + h, h), :] = jnp.zeros((h, D_POOL), F32)
        for g, width in enumerate(POOL_WINDOWS):
            left = width // 2
            right = width - 1 - left
            cols = slice(g * POOL_GROUP, (g + 1) * POOL_GROUP)
            wmat = pw_ref[g].astype(BF16)
            scale = sc_ref[:, cols]

            def first(ci, carry, left=left, right=right, cols=cols, wmat=wmat, scale=scale):
                dsc, dpw = carry
                base = pl.multiple_of(ci * c, c)
                dms = dms_ref[pl.ds(base, c), cols].astype(F32)
                dsc = dsc + jnp.sum(dms * mixed_ref[pl.ds(base, c), cols].astype(F32), axis=0, keepdims=True)
                dmix = (dms * scale).astype(BF16)
                dpw = dpw + _dot(pooled_ref[pl.ds(base, c), cols], dmix, TN)
                dpooled = _dot(dmix, wmat, NT)
                cnt = _pool_count(base, left, right, t, (c, POOL_GROUP))
                pad_ref[pl.ds(base + h, c), cols] = dpooled / cnt
                return dsc, dpw

            dsc, dpw = lax.fori_loop(0, n_chunks, first,
                                     (jnp.zeros((1, POOL_GROUP), F32), jnp.zeros((POOL_GROUP, POOL_GROUP), F32)))
            dsc_ref[:, cols] = dsc
            dpw_ref[g] = dpw

            def second(ci, carry, left=left, right=right, width=width, cols=cols):
                base = pl.multiple_of(ci * c, c)
                v = pad_ref[pl.ds(base, c + 2 * h), cols]
                win = _window_sum(v, width, right)[h:h + c]
                cnt = _pool_count(base, left, right, t, (c, POOL_GROUP))
                dp_ref[pl.ds(base, c), cols] = (win - v[h:h + c] * cnt).astype(BF16)
                return carry

            lax.fori_loop(0, n_chunks, second, 0)

    vm = BS(memory_space=pltpu.VMEM)
    return pl.pallas_call(
        body, out_shape=(SDS((t, D_POOL), BF16), SDS((1, D_POOL), F32), SDS((4, POOL_GROUP, POOL_GROUP), F32)),
        in_specs=[vm] * 5, out_specs=(vm, vm, vm),
        scratch_shapes=[pltpu.VMEM((t + 2 * h, D_POOL), F32)],
        name="pool_bwd", compiler_params=_params())(d_ms, mixed, pooled, pool_w, pool_scale)


def _ssm_disc(ar, ai, ldt):
    def body(ar_ref, ai_ref, ldt_ref, abr_ref, abi_ref, qr_ref, qi_ref):
        a_r, a_i = ar_ref[...], ai_ref[...]
        dt = jnp.exp(ldt_ref[...])
        mag = jnp.exp(dt * a_r)
        ang = dt * a_i
        abr = mag * jnp.cos(ang)
        abi = mag * jnp.sin(ang)
        den = a_r * a_r + a_i * a_i
        nr = abr - 1.0
        abr_ref[...] = abr
        abi_ref[...] = abi
        qr_ref[...] = (nr * a_r + abi * a_i) / den
        qi_ref[...] = (abi * a_r - nr * a_i) / den

    vm = BS(memory_space=pltpu.VMEM)
    shape = SDS(ar.shape, F32)
    return pl.pallas_call(body, out_shape=(shape,) * 4, in_specs=[vm] * 3, out_specs=(vm,) * 4,
                          name="ssm_disc", compiler_params=_params())(ar, ai, ldt)


def _ssm_disc_bwd(ar, ai, ldt, d_abr, d_abi, d_qr, d_qi):
    def body(ar_ref, ai_ref, ldt_ref, gabr_ref, gabi_ref, gqr_ref, gqi_ref, dar_ref, dai_ref, dldt_ref):
        a_r, a_i = ar_ref[...], ai_ref[...]
        dt = jnp.exp(ldt_ref[...])
        mag = jnp.exp(dt * a_r)
        ang = dt * a_i
        cs, sn = jnp.cos(ang), jnp.sin(ang)
        abr, abi = mag * cs, mag * sn
        den = a_r * a_r + a_i * a_i
        nr = abr - 1.0
        qr = (nr * a_r + abi * a_i) / den
        qi = (abi * a_r - nr * a_i) / den
        gqr, gqi = gqr_ref[...], gqi_ref[...]
        g_nr_num = gqr / den
        g_ni_num = gqi / den
        g_den = -(gqr * qr + gqi * qi) / den
        g_nr = g_nr_num * a_r - g_ni_num * a_i
        g_abi = g_nr_num * a_i + g_ni_num * a_r
        d_ar = g_nr_num * nr + g_ni_num * abi + 2.0 * a_r * g_den
        d_ai = g_nr_num * abi - g_ni_num * nr + 2.0 * a_i * g_den
        g_abr = gabr_ref[...] + g_nr
        g_abi = gabi_ref[...] + g_abi
        g_mag = g_abr * cs + g_abi * sn
        g_ang = mag * (g_abi * cs - g_abr * sn)
        g_e = g_mag * mag
        d_ar = d_ar + g_e * dt
        d_ai = d_ai + g_ang * dt
        g_dt = g_e * a_r + g_ang * a_i
        dar_ref[...] = d_ar
        dai_ref[...] = d_ai
        dldt_ref[...] = jnp.sum(g_dt * dt, axis=1, keepdims=True)

    vm = BS(memory_space=pltpu.VMEM)
    return pl.pallas_call(body, out_shape=(SDS(ar.shape, F32), SDS(ar.shape, F32), SDS(ldt.shape, F32)),
                          in_specs=[vm] * 7, out_specs=(vm,) * 3, name="ssm_disc_bwd",
                          compiler_params=_params())(ar, ai, ldt, d_abr, d_abi, d_qr, d_qi)


def _ssm_bbar(qr, qi, br, bi):
    def body(qr_ref, qi_ref, br_ref, bi_ref, bbr_ref, bbi_ref):
        q_r, q_i, b_r, b_i = qr_ref[...], qi_ref[...], br_ref[...], bi_ref[...]
        bbr_ref[...] = q_r * b_r - q_i * b_i
        bbi_ref[...] = q_r * b_i + q_i * b_r

    vm = BS(memory_space=pltpu.VMEM)
    shape = SDS(br.shape, F32)
    return pl.pallas_call(body, out_shape=(shape, shape), in_specs=[vm] * 4, out_specs=(vm, vm),
                          name="ssm_bbar", compiler_params=_params())(qr, qi, br, bi)


def _ssm_bbar_bwd(qr, qi, br, bi, g_bbr, g_bbi):
    def body(qr_ref, qi_ref, br_ref, bi_ref, gr_ref, gi_ref, dqr_ref, dqi_ref, dbr_ref, dbi_ref):
        q_r, q_i, b_r, b_i = qr_ref[...], qi_ref[...], br_ref[...], bi_ref[...]
        g_r, g_i = gr_ref[...], gi_ref[...]
        dqr_ref[...] = jnp.sum(g_r * b_r + g_i * b_i, axis=1, keepdims=True)
        dqi_ref[...] = jnp.sum(g_i * b_r - g_r * b_i, axis=1, keepdims=True)
        dbr_ref[...] = g_r * q_r + g_i * q_i
        dbi_ref[...] = g_i * q_r - g_r * q_i

    vm = BS(memory_space=pltpu.VMEM)
    return pl.pallas_call(
        body, out_shape=(SDS(qr.shape, F32), SDS(qr.shape, F32), SDS(br.shape, F32), SDS(br.shape, F32)),
        in_specs=[vm] * 6, out_specs=(vm,) * 4, name="ssm_bbar_bwd",
        compiler_params=_params())(qr, qi, br, bi, g_bbr, g_bbi)


SCAN_ROWS = 256


def _ssm_scan(name, inp, w1, a_r, a_i, w2, reverse):
    t = inp.shape[0]
    rows = SCAN_ROWS
    n = t // rows
    n_groups = rows // 8
    ch = SSM_CH
    at = (lambda i: (n - 1 - i, 0)) if reverse else (lambda i: (i, 0))

    def body(in_ref, w1_ref, ar_ref, ai_ref, w2_ref, st_ref, out_ref, cr_ref, ci_ref, k_ref):
        i = pl.program_id(0)

        @pl.when(i == 0)
        def _():
            ar8 = jnp.broadcast_to(ar_ref[...], (8, ch))
            ai8 = jnp.broadcast_to(ai_ref[...], (8, ch))
            row = lax.broadcasted_iota(jnp.int32, (8, ch), 0)
            rank = (7 - row) if reverse else row
            powers = [(ar8, ai8)]
            for _ in range(7):
                p_r, p_i = powers[-1]
                powers.append((p_r * ar8 - p_i * ai8, p_r * ai8 + p_i * ar8))
            zero = jnp.zeros((8, ch), F32)
            for slot, k in enumerate((1, 2, 4)):
                k_ref[2 * slot] = jnp.where(rank >= k, powers[k - 1][0], zero)
                k_ref[2 * slot + 1] = jnp.where(rank >= k, powers[k - 1][1], zero)
            carry_r, carry_i = zero, zero
            for j in range(8):
                carry_r = jnp.where(rank == j, powers[j][0], carry_r)
                carry_i = jnp.where(rank == j, powers[j][1], carry_i)
            k_ref[6] = carry_r
            k_ref[7] = carry_i
            cr_ref[...] = zero
            ci_ref[...] = zero

        st_ref[...] = _dot(in_ref[...], w1_ref[...])

        def group(gi, carry):
            c_r, c_i = carry
            g = (n_groups - 1 - gi) if reverse else gi
            r0 = pl.multiple_of(g * 8, 8)
            x_r = st_ref[pl.ds(r0, 8), 0:ch]
            x_i = st_ref[pl.ds(r0, 8), ch:2 * ch]
            for slot, k in enumerate((1, 2, 4)):
                shift = (8 - k) if reverse else k
                s_r = pltpu.roll(x_r, shift, 0)
                s_i = pltpu.roll(x_i, shift, 0)
                m_r, m_i = k_ref[2 * slot], k_ref[2 * slot + 1]
                x_r, x_i = x_r + m_r * s_r - m_i * s_i, x_i + m_r * s_i + m_i * s_r
            p_r, p_i = k_ref[6], k_ref[7]
            x_r, x_i = x_r + p_r * c_r - p_i * c_i, x_i + p_r * c_i + p_i * c_r
            st_ref[pl.ds(r0, 8), 0:ch] = x_r
            st_ref[pl.ds(r0, 8), ch:2 * ch] = x_i
            last = 0 if reverse else 7
            return (jnp.broadcast_to(x_r[last:last + 1, :], (8, ch)), jnp.broadcast_to(x_i[last:last + 1, :], (8, ch)))

        c_r, c_i = lax.fori_loop(0, n_groups, group, (cr_ref[...], ci_ref[...]))
        cr_ref[...] = c_r
        ci_ref[...] = c_i
        out_ref[...] = _dot(st_ref[...], w2_ref[...])

    return pl.pallas_call(
        body, out_shape=(SDS((t, 2 * ch), F32), SDS((t, D_SSM), F32)), grid=(n,),
        in_specs=[BS((rows, D_SSM), at), BS((D_SSM, 2 * ch), lambda i: (0, 0)), BS((1, ch), lambda i: (0, 0)),
                  BS((1, ch), lambda i: (0, 0)), BS((2 * ch, D_SSM), lambda i: (0, 0))],
        out_specs=(BS((rows, 2 * ch), at), BS((rows, D_SSM), at)),
        scratch_shapes=[pltpu.VMEM((8, ch), F32), pltpu.VMEM((8, ch), F32), pltpu.VMEM((8, 8, ch), F32)],
        name=name, compiler_params=_params("arbitrary"))(inp, w1, a_r, a_i, w2)


DA_ROWS = 512


def _ssm_da(name, lam, states, reverse):
    t = lam.shape[0]
    rows = DA_ROWS
    n = t // rows
    nb = rows // 8
    ch = SSM_CH
    if reverse:
        halo_at = lambda i: (jnp.minimum((i + 1) * nb, t // 8 - 1), 0)
    else:
        halo_at = lambda i: (jnp.maximum(i * nb - 1, 0), 0)

    def body(lam_ref, x_ref, halo_ref, dr_ref, di_ref):
        i = pl.program_id(0)

        @pl.when(i == 0)
        def _():
            dr_ref[...] = jnp.zeros_like(dr_ref)
            di_ref[...] = jnp.zeros_like(di_ref)

        row = lax.broadcasted_iota(jnp.int32, (rows, ch), 0)
        if reverse:
            edge, shift, h_row, live = rows - 1, rows - 1, 0, i < n - 1
        else:
            edge, shift, h_row, live = 0, 1, 7, i > 0

        def neighbour(lo):
            halo = jnp.where(live, halo_ref[h_row:h_row + 1, lo:lo + ch], 0.0)
            return jnp.where(row == edge, jnp.broadcast_to(halo, (rows, ch)), pltpu.roll(x_ref[:, lo:lo + ch], shift, 0))

        xp_r, xp_i = neighbour(0), neighbour(ch)
        l_r, l_i = lam_ref[:, 0:ch], lam_ref[:, ch:2 * ch]
        dr_ref[...] += jnp.sum(l_r * xp_r + l_i * xp_i, axis=0, keepdims=True)
        di_ref[...] += jnp.sum(l_i * xp_r - l_r * xp_i, axis=0, keepdims=True)

    blk = BS((rows, 2 * ch), lambda i: (i, 0))
    vec = BS((1, ch), lambda i: (0, 0))
    return pl.pallas_call(
        body, out_shape=(SDS((1, ch), F32), SDS((1, ch), F32)), grid=(n,),
        in_specs=[blk, blk, BS((8, 2 * ch), halo_at)], out_specs=(vec, vec),
        name=name, compiler_params=_params("arbitrary"))(lam, states, states)


GELU_C = math.sqrt(2.0 / math.pi)
GELU_K = 0.044715


def _ssm_combine(proj, y_dirs, d_skip, tm):
    t = proj.shape[0]

    def body(s_ref, y_ref, d_ref, yt_ref, g_ref):
        y = s_ref[...] * d_ref[...] + y_ref[0] + y_ref[1]
        yt_ref[...] = y
        th = jnp.tanh(GELU_C * (y + GELU_K * y * y * y))
        g_ref[...] = (0.5 * y * (1.0 + th)).astype(BF16)

    blk = BS((tm, D_SSM), lambda i: (i, 0))
    return pl.pallas_call(
        body, out_shape=(SDS((t, D_SSM), F32), SDS((t, D_SSM), BF16)), grid=(t // tm,),
        in_specs=[BS((tm, D_SSM), lambda i: (i, D_POOL // D_SSM)), BS((2, tm, D_SSM), lambda i: (0, i, 0)),
                  BS((1, D_SSM), lambda i: (0, 0))],
        out_specs=(blk, blk), name="ssm_combine", compiler_params=_params("parallel"))(proj, y_dirs, d_skip)


def _ssm_ds(proj, d_yt, du_dirs, d_skip, tm):
    t = proj.shape[0]

    def body(s_ref, dy_ref, du_ref, d_ref, ds_ref, dd_ref):
        i = pl.program_id(0)
        dy = dy_ref[...]
        ds_ref[...] = (dy * d_ref[...] + du_ref[0] + du_ref[1]).astype(BF16)

        @pl.when(i == 0)
        def _():
            dd_ref[...] = jnp.zeros_like(dd_ref)

        dd_ref[...] += jnp.sum(dy * s_ref[...], axis=0, keepdims=True)

    blk = BS((tm, D_SSM), lambda i: (i, 0))
    vec = BS((1, D_SSM), lambda i: (0, 0))
    return pl.pallas_call(
        body, out_shape=(SDS((t, D_SSM), BF16), SDS((1, D_SSM), F32)), grid=(t // tm,),
        in_specs=[BS((tm, D_SSM), lambda i: (i, D_POOL // D_SSM)), blk, BS((2, tm, D_SSM), lambda i: (0, i, 0)), vec],
        out_specs=(blk, vec), name="ssm_ds", compiler_params=_params("arbitrary"))(proj, d_yt, du_dirs, d_skip)


GP_BLOCK = (D_POOL + D_SSM) // 256
GS_BLOCK = GP_BLOCK + D_MODEL // 256


def _merge_specs(tm):
    return [BS((tm, D_POOL), lambda s, i: (i, 0)), BS((tm, D_SSM), lambda s, i: (i, 0)),
            BS((None, D_POOL, 256), lambda s, i: (s, 0, 0)), BS((None, D_SSM, 256), lambda s, i: (s, 2, 0)),
            BS((None, D_SSM, 256), lambda s, i: (s, 3, 0)),
            BS((tm, 256), lambda s, i: (i, GP_BLOCK + s)), BS((tm, 256), lambda s, i: (i, GS_BLOCK + s))]


def _mixer_merge(ms, yssm, w_e, proj, tm):
    t = ms.shape[0]

    def body(ms_ref, y_ref, wpp_ref, wgv_ref, wgg_ref, gp_ref, gs_ref, o_ref):
        zp = _dot(ms_ref[...], wpp_ref[...])
        yv = y_ref[...]
        zv = _dot(yv, wgv_ref[...])
        zg = _dot(yv, wgg_ref[...])
        o_ref[...] = (_sigmoid(gp_ref[...]) * zp + _sigmoid(gs_ref[...]) * zv * _sigmoid(zg)).astype(BF16)

    col = BS((tm, 256), lambda s, i: (i, s))
    return pl.pallas_call(
        body, out_shape=SDS((t, D_MODEL), BF16), grid=(N_SHARD, t // tm), in_specs=_merge_specs(tm), out_specs=col,
        name="mixer_merge", compiler_params=_params("parallel", "parallel"))(ms, yssm, w_e, w_e, w_e, proj, proj)


def _mixer_merge_bwd(ms, yssm, w_e, proj, dmerged, tm):
    t = ms.shape[0]

    def body(ms_ref, y_ref, wpp_ref, wgv_ref, wgg_ref, gp_ref, gs_ref, dm_ref,
             dgp_ref, dgs_ref, dzp_ref, dzv_ref, dzg_ref):
        zp = _dot(ms_ref[...], wpp_ref[...])
        yv = y_ref[...]
        zv = _dot(yv, wgv_ref[...])
        zg = _dot(yv, wgg_ref[...])
        dm = dm_ref[...].astype(F32)
        sp, ss, sg = _sigmoid(gp_ref[...]), _sigmoid(gs_ref[...]), _sigmoid(zg)
        dgp_ref[...] = (dm * zp * sp * (1.0 - sp)).astype(BF16)
        dgs_ref[...] = (dm * zv * sg * ss * (1.0 - ss)).astype(BF16)
        dzp_ref[...] = (dm * sp).astype(BF16)
        dz = dm * ss
        dzv_ref[...] = (dz * sg).astype(BF16)
        dzg_ref[...] = (dz * zv * sg * (1.0 - sg)).astype(BF16)

    col = BS((tm, 256), lambda s, i: (i, s))
    shape = SDS((t, D_MODEL), BF16)
    return pl.pallas_call(
        body, out_shape=(shape,) * 5, grid=(N_SHARD, t // tm), in_specs=_merge_specs(tm) + [col],
        out_specs=(col,) * 5, name="mixer_merge_bwd",
        compiler_params=_params("parallel", "parallel"))(ms, yssm, w_e, w_e, w_e, proj, proj, dmerged)


def _mixer_dw(ms, yssm, dzp, dzv, dzg, tm):
    t = ms.shape[0]
    n_t = t // tm

    def body(ms_ref, y_ref, dzp_ref, dzv_ref, dzg_ref, o_ref, acc):
        i = pl.program_id(1)

        @pl.when(i == 0)
        def _():
            acc[...] = jnp.zeros_like(acc)

        yv = y_ref[...]
        acc[0:D_POOL, :] += _dot(ms_ref[...], dzp_ref[...], TN)
        acc[D_POOL:D_POOL + D_SSM, :] += _dot(yv, dzv_ref[...], TN)
        acc[D_POOL + D_SSM:, :] += _dot(yv, dzg_ref[...], TN)

        @pl.when(i == n_t - 1)
        def _():
            o_ref[...] = acc[...].astype(BF16)

    col = BS((tm, 256), lambda s, i: (i, s))
    return pl.pallas_call(
        body, out_shape=SDS((N_SHARD, 1024, 256), BF16), grid=(N_SHARD, n_t),
        in_specs=[BS((tm, D_POOL), lambda s, i: (i, 0)), BS((tm, D_SSM), lambda s, i: (i, 0)), col, col, col],
        out_specs=BS((None, 1024, 256), lambda s, i: (s, 0, 0)), scratch_shapes=[pltpu.VMEM((1024, 256), F32)],
        name="mixer_dw", compiler_params=_params("parallel", "arbitrary"))(ms, yssm, dzp, dzv, dzg)


def _mixer_dx(dzp, dzv, dzg, w_e, y_total, tm):
    t = dzp.shape[0]

    def body(dzp_ref, dzv_ref, dzg_ref, wpp_ref, wgv_ref, wgg_ref, yt_ref, dms_ref, dy_ref, acc_ms, acc_y):
        s = pl.program_id(1)

        @pl.when(s == 0)
        def _():
            acc_ms[...] = jnp.zeros_like(acc_ms)
            acc_y[...] = jnp.zeros_like(acc_y)

        acc_ms[...] += _dot(dzp_ref[...], wpp_ref[...], NT)
        acc_y[...] += _dot(dzv_ref[...], wgv_ref[...], NT) + _dot(dzg_ref[...], wgg_ref[...], NT)

        @pl.when(s == N_SHARD - 1)
        def _():
            dms_ref[...] = acc_ms[...].astype(BF16)
            y = yt_ref[...]
            inner = GELU_C * (y + GELU_K * y * y * y)
            th = jnp.tanh(inner)
            dgelu = 0.5 * (1.0 + th) + 0.5 * y * (1.0 - th * th) * GELU_C * (1.0 + 3.0 * GELU_K * y * y)
            dy_ref[...] = acc_y[...] * dgelu

    col = BS((tm, 256), lambda i, s: (i, s))
    return pl.pallas_call(
        body, out_shape=(SDS((t, D_POOL), BF16), SDS((t, D_SSM), F32)), grid=(t // tm, N_SHARD),
        in_specs=[col, col, col, BS((None, D_POOL, 256), lambda i, s: (s, 0, 0)),
                  BS((None, D_SSM, 256), lambda i, s: (s, 2, 0)), BS((None, D_SSM, 256), lambda i, s: (s, 3, 0)),
                  BS((tm, D_SSM), lambda i, s: (i, 0))],
        out_specs=(BS((tm, D_POOL), lambda i, s: (i, 0)), BS((tm, D_SSM), lambda i, s: (i, 0))),
        scratch_shapes=[pltpu.VMEM((tm, D_POOL), F32), pltpu.VMEM((tm, D_SSM), F32)],
        name="mixer_dx", compiler_params=_params("parallel", "arbitrary"))(dzp, dzv, dzg, w_e, w_e, w_e, y_total)


def _attn_probs(q_h, k_h):
    s = _dot(q_h, k_h, NT) * (1.0 / math.sqrt(HEAD_DIM))
    e = jnp.exp(s - jnp.max(s, axis=-1, keepdims=True))
    return e / jnp.sum(e, axis=-1, keepdims=True)


def _attn_fwd(q, kv, tm):
    t = q.shape[0]
    m = kv.shape[0]

    def body(q_ref, kv_ref, o_ref):
        for hd in range(N_HEADS):
            lo = hd * HEAD_DIM
            p = _attn_probs(q_ref[:, lo:lo + HEAD_DIM], kv_ref[:, lo:lo + HEAD_DIM])
            o_ref[:, lo:lo + HEAD_DIM] = _dot(p, kv_ref[:, D_MODEL + lo:D_MODEL + lo + HEAD_DIM]).astype(BF16)

    return pl.pallas_call(
        body, out_shape=SDS((t, D_MODEL), BF16), grid=(t // tm,),
        in_specs=[BS((tm, D_MODEL), lambda i: (i, 0)), BS((m, 2 * D_MODEL), lambda i: (0, 0))],
        out_specs=BS((tm, D_MODEL), lambda i: (i, 0)), name="attn_fwd", compiler_params=_params("parallel"))(q, kv)


def _attn_bwd(q, kv, d_o, tm):
    t = q.shape[0]
    m = kv.shape[0]

    def body(q_ref, kv_ref, do_ref, dq_ref, dkv_ref):
        i = pl.program_id(0)

        @pl.when(i == 0)
        def _():
            dkv_ref[...] = jnp.zeros_like(dkv_ref)

        for hd in range(N_HEADS):
            lo = hd * HEAD_DIM
            q_h = q_ref[:, lo:lo + HEAD_DIM]
            k_h = kv_ref[:, lo:lo + HEAD_DIM]
            v_h = kv_ref[:, D_MODEL + lo:D_MODEL + lo + HEAD_DIM]
            do_h = do_ref[:, lo:lo + HEAD_DIM]
            p = _attn_probs(q_h, k_h)
            dkv_ref[:, D_MODEL + lo:D_MODEL + lo + HEAD_DIM] += _dot(p, do_h, TN)
            dp = _dot(do_h, v_h, NT)
            ds = p * (dp - jnp.sum(dp * p, axis=-1, keepdims=True)) * (1.0 / math.sqrt(HEAD_DIM))
            dq_ref[:, lo:lo + HEAD_DIM] = _dot(ds, k_h).astype(BF16)
            dkv_ref[:, lo:lo + HEAD_DIM] += _dot(ds, q_h, TN)

    row = BS((tm, D_MODEL), lambda i: (i, 0))
    full = BS((m, 2 * D_MODEL), lambda i: (0, 0))
    return pl.pallas_call(
        body, out_shape=(SDS((t, D_MODEL), BF16), SDS((m, 2 * D_MODEL), F32)), grid=(t // tm,),
        in_specs=[row, full, row], out_specs=(row, full), name="attn_bwd",
        compiler_params=_params("arbitrary"))(q, kv, d_o)


TRANSPOSED = ("ffn1_w_gate", "ffn1_w_up", "ffn2_w_gate", "ffn2_w_up", "w_in")
GATHER_GROUPS = (("ffn1_w_gate", "ffn1_w_up", "ffn1_w_down"),
                 ("w_in", "w_mix_out", "w_q", "w_xo"),
                 ("w_kv",),
                 ("w_pool_proj", "w_glu_val", "w_glu_gate"),
                 ("ffn2_w_gate", "ffn2_w_up", "ffn2_w_down"))
REDUCE_GROUPS = (("ffn2_w_gate", "ffn2_w_up", "ffn2_w_down"), ("w_xo",), ("w_q",), ("w_kv",), ("w_mix_out",),
                 ("w_pool_proj", "w_glu_val", "w_glu_gate"), ("w_in",), ("ffn1_w_gate", "ffn1_w_up", "ffn1_w_down"))
SMALL = ("ffn1_norm", "mix_norm", "pool_w", "pool_scale", "ssm_a_re", "ssm_a_im", "ssm_log_dt", "ssm_b_re",
         "ssm_b_im", "ssm_c_re", "ssm_c_im", "ssm_d", "xattn_norm", "mem_norm", "ffn2_norm", "final_norm")
WEIGHTS = ("ffn1_norm", "ffn1_w_gate", "ffn1_w_up", "ffn1_w_down", "mix_norm", "w_in", "pool_w", "pool_scale",
           "w_pool_proj", "ssm_a_re", "ssm_a_im", "ssm_log_dt", "ssm_b_re", "ssm_b_im", "ssm_c_re", "ssm_c_im",
           "ssm_d", "w_glu_val", "w_glu_gate", "w_mix_out", "xattn_norm", "mem_norm", "w_q", "w_kv", "w_xo",
           "ffn2_norm", "ffn2_w_gate", "ffn2_w_up", "ffn2_w_down", "final_norm")


def _block_diag_in(bb):
    eye = jnp.eye(SSM_GROUPS, dtype=bb.dtype)
    return jnp.einsum("dgph,gk->dghkp", bb, eye).reshape(2, D_SSM, SSM_CH)


def _block_diag_out(cc):
    eye = jnp.eye(SSM_GROUPS, dtype=cc.dtype)
    return jnp.einsum("dghp,gk->dgpkh", cc, eye).reshape(2, SSM_CH, D_SSM)


def _diag_blocks_in(m):
    return jnp.einsum("dghgp->dgph", m.reshape(2, SSM_GROUPS, SSM_GROUP, SSM_GROUPS, SSM_STATE))


def _diag_blocks_out(m):
    return jnp.einsum("dgpgh->dghp", m.reshape(2, SSM_GROUPS, SSM_STATE, SSM_GROUPS, SSM_GROUP))


def _device_step(x, mem, target, gw, sp, reducer=None):
    t = x.shape[0]
    tm = min(TM, t)
    w_f1, w_mix, w_kv, w_e, w_f2 = gw
    w_f1 = w_f1.reshape(N_SHARD, 3, FF_SH, D_MODEL)
    w_f2 = w_f2.reshape(N_SHARD, 3, FF_SH, D_MODEL)
    w_d = w_kv[:, None]
    w_in_t = w_mix[:, :FF_SH].reshape(D_FF, D_MODEL)
    w_mo, w_q, w_xo = (w_mix[:, FF_SH + 256 * k:FF_SH + 256 * (k + 1)].reshape(D_MODEL, D_MODEL) for k in range(3))
    g = {}

    u1 = _rmsnorm("norm_ffn1", x, sp["ffn1_norm"], tm)
    g1, up1, a1 = _ffn_up("ffn1_up", u1, w_f1, tm)
    h1 = _ffn_down("ffn1_down", a1, w_f1, x, tm)

    u2 = _rmsnorm("norm_mix", h1, sp["mix_norm"], tm)
    proj = _mm("mix_in", [(u2, BS((tm, D_MODEL), lambda j, i: (i, 0)), w_in_t, BS((D_FF // 2, D_MODEL), lambda j, i: (j, 0)), NT)],
               grid=(2, t // tm), out_shape=SDS((t, D_FF), F32), out_spec=BS((tm, D_FF // 2), lambda j, i: (i, j)))
    pooled, mixed, ms = _pool_fwd(proj, sp["pool_w"][0], sp["pool_scale"])

    ar = sp["ssm_a_re"].reshape(2 * SSM_GROUPS, SSM_STATE)
    ai = sp["ssm_a_im"].reshape(2 * SSM_GROUPS, SSM_STATE)
    ldt = sp["ssm_log_dt"].reshape(2 * SSM_GROUPS, 1)
    abr, abi, qr, qi = _ssm_disc(ar, ai, ldt)
    b_r = sp["ssm_b_re"].reshape(2 * SSM_CH, SSM_GROUP)
    b_i = sp["ssm_b_im"].reshape(2 * SSM_CH, SSM_GROUP)
    qr_col, qi_col = qr.reshape(2 * SSM_CH, 1), qi.reshape(2 * SSM_CH, 1)
    bbr, bbi = _ssm_bbar(qr_col, qi_col, b_r, b_i)
    shape_b = (2, SSM_GROUPS, SSM_STATE, SSM_GROUP)
    b_mat = jnp.concatenate([_block_diag_in(bbr.reshape(shape_b)), _block_diag_in(bbi.reshape(shape_b))], axis=-1).astype(BF16)
    c_mat = jnp.concatenate([_block_diag_out(sp["ssm_c_re"][0]), -_block_diag_out(sp["ssm_c_im"][0])], axis=1).astype(BF16)
    b_mat_t = jnp.swapaxes(b_mat, 1, 2)
    c_mat_t = jnp.swapaxes(c_mat, 1, 2)
    a_r = abr.reshape(2, 1, SSM_CH)
    a_i = abi.reshape(2, 1, SSM_CH)

    s_in =proj[:, D_POOL:D_POOL + D_SSM].astype(BF16)
    states, y_dirs = [], []
    for dr in range(2):
        st, yd = _ssm_scan(f"ssm_scan_fwd{dr}", s_in, b_mat[dr], a_r[dr], a_i[dr], c_mat[dr], reverse=(dr == 1))
        states.append(st)
        y_dirs.append(yd)
    y_total, yssm = _ssm_combine(proj, jnp.stack(y_dirs), sp["ssm_d"], tm)

    merged = _mixer_merge(ms, yssm, w_e, proj, tm)
    h2 = _plain_mm("mix_out", merged, w_mo, NN, F32, tm, resid=h1)

    u3 = _rmsnorm("norm_xattn", h2, sp["xattn_norm"], tm)
    mem_n = _rmsnorm("norm_mem", mem, sp["mem_norm"], mem.shape[0])
    q = _plain_mm("attn_q", u3, w_q, NN, BF16, tm)
    n_mem = mem.shape[0]
    kv = _mm("attn_kv", [(mem_n, BS((n_mem, D_MODEL), lambda s: (0, 0)), w_d, BS((None, None, D_MODEL, 512), lambda s: (s, 0, 0, 0)), NN)],
             grid=(N_SHARD,), out_shape=SDS((n_mem, 2 * D_MODEL), BF16), out_spec=BS((n_mem, 512), lambda s: (0, s)))
    o = _attn_fwd(q, kv, tm)
    h3 = _plain_mm("attn_out", o, w_xo, NN, F32, tm, resid=h2)

    u4 = _rmsnorm("norm_ffn2", h3, sp["ffn2_norm"], tm)
    g2, up2, a2 = _ffn_up("ffn2_up", u4, w_f2, tm)
    h4 = _ffn_down("ffn2_down", a2, w_f2, h3, tm)

    loss, dh4, dh4_b, g["final_norm"] = _loss_head(h4, sp["final_norm"].reshape(1, D_MODEL), target, tm)

    dg2, dup2 = _ffn_bwd_act("ffn2_bwd_act", dh4_b, w_f2, g2, up2, tm)
    dw_f2 = _ffn_dw("ffn2_dw", u4, dg2, dup2, a2, dh4_b, tm)
    du4 = _ffn_dx("ffn2_dx", dg2, dup2, w_f2, tm)
    dh3, dh3_b, g["ffn2_norm"] = _rmsnorm_bwd("norm_ffn2_bwd", h3, sp["ffn2_norm"], du4, dh4, tm)

    d_o = _plain_mm("attn_out_dx", dh3_b, w_xo, NT, BF16, tm)
    dw_xo = _dw_mm("attn_out_dw", o, dh3_b, tm)
    dq, dkv = _attn_bwd(q, kv, d_o, tm)
    dw_q = _dw_mm("attn_q_dw", u3, dq, tm)
    du3 = _plain_mm("attn_q_dx", dq, w_q, NT, F32, tm)
    dw_kv = _mm("attn_kv_dw", [(mem_n, BS((n_mem, D_MODEL), lambda s: (0, 0)), dkv, BS((n_mem, 512), lambda s: (0, s)), TN)],
                grid=(N_SHARD,), out_shape=SDS((N_SHARD, D_MODEL, 512), BF16), out_spec=BS((None, D_MODEL, 512), lambda s: (s, 0, 0)))
    dmem_n = _mm("attn_kv_dx", [(dkv, BS((n_mem, 512), lambda s: (0, s)), w_d, BS((None, None, D_MODEL, 512), lambda s: (s, 0, 0, 0)), NT)],
                 grid=(N_SHARD,), red_axis=0, out_shape=SDS((n_mem, D_MODEL), F32), out_spec=BS((n_mem, D_MODEL), lambda s: (0, 0)))
    _, _, g["mem_norm"] = _rmsnorm_bwd("norm_mem_bwd", mem, sp["mem_norm"], dmem_n, None, n_mem)
    dh2, dh2_b, g["xattn_norm"] = _rmsnorm_bwd("norm_xattn_bwd", h2, sp["xattn_norm"], du3, dh3, tm)

    dmerged = _plain_mm("mix_out_dx", dh2_b, w_mo, NT, BF16, tm)
    dw_mo = _dw_mm("mix_out_dw", merged, dh2_b, tm)
    d_gp, d_gs, dzp, dzv, dzg = _mixer_merge_bwd(ms, yssm, w_e, proj, dmerged, tm)
    dw_e = _mixer_dw(ms, yssm, dzp, dzv, dzg, tm)
    d_ms, d_yt = _mixer_dx(dzp, dzv, dzg, w_e, y_total, tm)
    dp, d_scale, d_pw = _pool_bwd(d_ms, mixed, pooled, sp["pool_w"][0], sp["pool_scale"])
    g["pool_scale"] = d_scale
    g["pool_w"] = d_pw[None]

    d_yt_b = d_yt.astype(BF16)
    du_dirs, d_abr, d_abi, d_cm, d_bm = [], [], [], [], []
    for dr in range(2):
        lam, du = _ssm_scan(f"ssm_scan_bwd{dr}", d_yt_b, c_mat_t[dr], a_r[dr], -a_i[dr], b_mat_t[dr], reverse=(dr == 0))
        du_dirs.append(du)
        da_r, da_i = _ssm_da(f"ssm_da{dr}", lam, states[dr], reverse=(dr == 1))
        d_abr.append(da_r)
        d_abi.append(da_i)
        d_cm.append(_dw_mm(f"ssm_dc{dr}", states[dr], d_yt_b, tm, F32))
        d_bm.append(_dw_mm(f"ssm_db{dr}", s_in, lam, tm, F32))
    d_cm = jnp.stack(d_cm)
    d_bm = jnp.stack(d_bm)
    g["ssm_c_re"] = _diag_blocks_out(d_cm[:, :SSM_CH])[None]
    g["ssm_c_im"] = -_diag_blocks_out(d_cm[:, SSM_CH:])[None]
    g_bbr = _diag_blocks_in(d_bm[:, :, :SSM_CH]).reshape(2 * SSM_CH, SSM_GROUP)
    g_bbi = _diag_blocks_in(d_bm[:, :, SSM_CH:]).reshape(2 * SSM_CH, SSM_GROUP)
    d_qr, d_qi, d_br, d_bi = _ssm_bbar_bwd(qr_col, qi_col, b_r, b_i, g_bbr, g_bbi)
    g["ssm_b_re"] = d_br.reshape(sp["ssm_b_re"].shape)
    g["ssm_b_im"] = d_bi.reshape(sp["ssm_b_im"].shape)
    d_ar, d_ai, d_ldt = _ssm_disc_bwd(ar, ai, ldt, jnp.stack(d_abr).reshape(ar.shape), jnp.stack(d_abi).reshape(ar.shape),
                                      d_qr.reshape(ar.shape), d_qi.reshape(ar.shape))
    g["ssm_a_re"] = d_ar.reshape(sp["ssm_a_re"].shape)
    g["ssm_a_im"] = d_ai.reshape(sp["ssm_a_im"].shape)
    g["ssm_log_dt"] = d_ldt.reshape(sp["ssm_log_dt"].shape)
    ds, g["ssm_d"] = _ssm_ds(proj, d_yt, jnp.stack(du_dirs), sp["ssm_d"], tm)

    d_proj = jnp.concatenate([dp, ds, d_gp, d_gs], axis=1)
    dw_in_t = _mm("mix_in_dw", [(d_proj, BS((tm, D_FF // 2), lambda j, i: (i, j)), u2, BS((tm, D_MODEL), lambda j, i: (i, 0)), TN)],
                  grid=(2, t // tm), red_axis=1, out_shape=SDS((D_FF, D_MODEL), BF16), out_spec=BS((D_FF // 2, D_MODEL), lambda j, i: (j, 0)))
    du2 = _plain_mm("mix_in_dx", d_proj, w_in_t, NN, F32, tm)
    dh1, dh1_b, g["mix_norm"] = _rmsnorm_bwd("norm_mix_bwd", h1, sp["mix_norm"], du2, dh2, tm)

    square = (N_SHARD, D_MODEL // N_SHARD, D_MODEL)
    early = [dw_f2.reshape(N_SHARD, 3 * FF_SH, D_MODEL), dw_xo.reshape(square), dw_q.reshape(square), dw_kv,
             dw_mo.reshape(square), dw_e, dw_in_t.reshape(N_SHARD, FF_SH, D_MODEL)]
    g["final_norm"] = g["final_norm"].reshape(D_MODEL)

    travelling = [reducer.start("a", early)] if reducer is not None else []
    dg1, dup1 = _ffn_bwd_act("ffn1_bwd_act", dh1_b, w_f1, g1, up1, tm, after=travelling)
    dw_f1 = _ffn_dw("ffn1_dw", u1, dg1, dup1, a1, dh1_b, tm).reshape(N_SHARD, 3 * FF_SH, D_MODEL)
    if reducer is not None:
        landed = reducer.finish("a", dw_f1)
        travelling = [reducer.start("b", [dw_f1], after=[landed])]
    du1 = _ffn_dx("ffn1_dx", dg1, dup1, w_f1, tm, after=travelling)
    grad_x, _, g["ffn1_norm"] = _rmsnorm_bwd("norm_ffn1_bwd", x, sp["ffn1_norm"], du1, dh1, tm)
    if reducer is not None:
        reducer.finish("b", grad_x)
    return loss, grad_x, early + [dw_f1], g


def _mesh_place():
    x, y, c = lax.axis_index("x"), lax.axis_index("y"), lax.axis_index("c")
    chips = [(1 - x, y), (x, 1 - y), (1 - x, 1 - y)]
    return x, y, c, chips


def _remote(src, dst, send_sems, recv_sems, k, to):
    return pltpu.make_async_remote_copy(src_ref=src, dst_ref=dst, send_sem=send_sems.at[k], recv_sem=recv_sems.at[k],
                                        device_id=to, device_id_type=MESH)


def _allgather_shards(shards):
    n = len(shards)

    def body(*refs):
        ins, outs = refs[:n], refs[n:2 * n]
        send_sems, recv_sems, local_sems = refs[2 * n:]
        x, y, c, chips = _mesh_place()
        me = 2 * x + y
        sibling = (x, y, 1 - c)
        started, local_copies = [], []
        for k in range(n):
            half = shards[k].shape[0] // 2
            mine = pl.ds(pl.multiple_of(c * half, 16), half)
            local = pltpu.make_async_copy(ins[k], outs[k].at[me], local_sems.at[k])
            local.start()
            local_copies.append(local)
            for j, (px, py) in enumerate(chips):
                cp = _remote(ins[k].at[mine, :], outs[k].at[me, mine, :], send_sems, recv_sems, 6 * k + j, (px, py, c))
                cp.start()
                started.append(cp)
        for k in range(n):
            half = shards[k].shape[0] // 2
            mine = pl.ds(pl.multiple_of(c * half, 16), half)
            for j, (px, py) in enumerate(chips):
                blk = outs[k].at[2 * px + py, mine, :]
                _remote(blk, blk, send_sems, recv_sems, 6 * k + j, (px, py, c)).wait_recv()
                fwd = _remote(blk, blk, send_sems, recv_sems, 6 * k + 3 + j, sibling)
                fwd.start()
                started.append(fwd)
        for k in range(n):
            half = shards[k].shape[0] // 2
            theirs = pl.ds(pl.multiple_of((1 - c) * half, 16), half)
            for j, (px, py) in enumerate(chips):
                blk = outs[k].at[2 * px + py, theirs, :]
                _remote(blk, blk, send_sems, recv_sems, 6 * k + 3 + j, sibling).wait_recv()
        for cp in started:
            cp.wait_send()
        for cp in local_copies:
            cp.wait()

    hbm = BS(memory_space=pl.ANY)
    return pl.pallas_call(
        body, out_shape=tuple(SDS((N_SHARD,) + s.shape, s.dtype) for s in shards),
        in_specs=[hbm] * n, out_specs=(hbm,) * n,
        scratch_shapes=[pltpu.SemaphoreType.DMA((6 * n,)), pltpu.SemaphoreType.DMA((6 * n,)), pltpu.SemaphoreType.DMA((n,))],
        name="allgather_weights", compiler_params=_params())(*shards)


def _sibling_swap_halves(tag, grads, after=()):
    n = len(grads)
    after_ops, after_specs = _after_operands(after)

    def body(*refs):
        ins, outs = refs[:n], refs[n + len(after_ops):2 * n + len(after_ops)]
        send_sems, recv_sems = refs[2 * n + len(after_ops):]
        x, y, c, _ = _mesh_place()
        sibling = (x, y, 1 - c)
        copies = []
        for k in range(n):
            half = grads[k].shape[1] // 2
            theirs = pl.ds(pl.multiple_of((1 - c) * half, 16), half)
            cp = _remote(ins[k].at[:, theirs, :], outs[k], send_sems, recv_sems, k, sibling)
            cp.start()
            copies.append(cp)
        for cp in copies:
            cp.wait_recv()
        for cp in copies:
            cp.wait_send()

    hbm = BS(memory_space=pl.ANY)
    return pl.pallas_call(
        body, out_shape=tuple(SDS((g.shape[0], g.shape[1] // 2, g.shape[2]), g.dtype) for g in grads),
        in_specs=[hbm] * n + after_specs, out_specs=(hbm,) * n,
        scratch_shapes=[pltpu.SemaphoreType.DMA((n,)), pltpu.SemaphoreType.DMA((n,))],
        name="reduce_sibling_send_" + tag, compiler_params=_params())(*grads, *after_ops)


def _row_tile(rows, cap=512):
    return max(r for r in range(16, cap + 1, 16) if rows % r == 0)


def _chip_presum(k, grad, got, c_idx):
    n_sh, rows, cols = grad.shape
    half = rows // 2
    tr = _row_tile(half)
    grad4 = grad.reshape(n_sh, 2, half, cols)

    def body(c_ref, a_ref, b_ref, o_ref):
        o_ref[...] = (a_ref[...].astype(F32) + b_ref[...].astype(F32)).astype(o_ref.dtype)

    return pl.pallas_call(
        body, out_shape=SDS((n_sh, half, cols), BF16),
        grid_spec=pltpu.PrefetchScalarGridSpec(
            num_scalar_prefetch=1, grid=(n_sh, half // tr),
            in_specs=[BS((None, None, tr, cols), lambda s, i, c_ref: (s, c_ref[0], i, 0)),
                      BS((None, tr, cols), lambda s, i, c_ref: (s, i, 0))],
            out_specs=BS((None, tr, cols), lambda s, i, c_ref: (s, i, 0))),
        name=f"reduce_presum{k}", compiler_params=_params("parallel", "parallel"))(c_idx, grad4, got)


HBM_SPEC = BS(memory_space=pltpu.HBM)
SEM_SPEC = BS(memory_space=pltpu.SEMAPHORE)
DATAFLOW = pltpu.SideEffectType.DATAFLOW_SIDE_EFFECTING


def _chip_exchange_copies(parts, lands, send_sems, recv_sems):
    _, _, c, chips = _mesh_place()
    return [_remote(parts[k].at[2 * px + py], lands[k].at[j], send_sems, recv_sems, 3 * k + j, (px, py, c))
            for k in range(len(parts)) for j, (px, py) in enumerate(chips)]


def _chip_exchange_start(tag, parts):
    n = len(parts)

    def body(*refs):
        ins, lands = refs[:n], refs[n:2 * n]
        send_sems, recv_sems = refs[2 * n], refs[2 * n + 1]
        token = refs[-1]
        for cp in _chip_exchange_copies(ins, lands, send_sems, recv_sems):
            cp.start()
        token[...] = jnp.zeros_like(token)

    lands = [pltpu.with_memory_space_constraint(lax.empty((3,) + p.shape[1:], p.dtype), pltpu.HBM) for p in parts]
    parts = [pltpu.with_memory_space_constraint(p, pltpu.HBM) for p in parts]
    thru = [pltpu.HBM(a.shape, a.dtype) for a in parts + lands]
    out = pl.pallas_call(
        body, name="reduce_exchange_start_" + tag,
        out_shape=(pltpu.SemaphoreType.DMA((3 * n,)), pltpu.SemaphoreType.DMA((3 * n,)), *thru, SDS((8, 128), F32)),
        in_specs=[HBM_SPEC] * (2 * n), out_specs=(SEM_SPEC, SEM_SPEC, *[HBM_SPEC] * (2 * n), BS(memory_space=pltpu.VMEM)),
        input_output_aliases={i: 2 + i for i in range(2 * n)},
        compiler_params=pltpu.CompilerParams(has_side_effects=DATAFLOW))(*parts, *lands)
    return out[0], out[1], list(out[2:2 + n]), list(out[2 + n:2 + 2 * n]), out[-1]


def _chip_exchange_wait(tag, send_sems, recv_sems, parts, lands, after):
    n = len(parts)

    def body(*refs):
        ins, zones = refs[:n], refs[n:2 * n]
        for cp in _chip_exchange_copies(ins, zones, refs[2 * n], refs[2 * n + 1]):
            cp.wait_send()
            cp.wait_recv()

    out = pl.pallas_call(
        body, name="reduce_exchange_wait_" + tag,
        out_shape=tuple(pltpu.HBM(a.shape, a.dtype) for a in parts + lands),
        in_specs=[HBM_SPEC] * (2 * n) + [SEM_SPEC, SEM_SPEC, BS(memory_space=pl.ANY)], out_specs=(HBM_SPEC,) * (2 * n),
        input_output_aliases={i: i for i in range(2 * n)},
        compiler_params=pltpu.CompilerParams(has_side_effects=DATAFLOW))(*parts, *lands, send_sems, recv_sems, after)
    return list(out[:n]), list(out[n:])


class _GradReducer:
    def __init__(self):
        self.c_idx = lax.axis_index("c").astype(jnp.int32).reshape(1)
        self.parts, self.landed, self.open = [], [], {}

    def start(self, tag, grads, after=()):
        got = _sibling_swap_halves(tag, grads, after)
        parts = [_chip_presum(f"{tag}{k}", g, s, self.c_idx) for k, (g, s) in enumerate(zip(grads, got))]
        self.open[tag] = _chip_exchange_start(tag, parts)
        return self.open[tag][-1]

    def finish(self, tag, after):
        send_sems, recv_sems, parts, lands, _ = self.open.pop(tag)
        parts, landed = _chip_exchange_wait(tag, send_sems, recv_sems, parts, lands, after)
        self.parts += parts
        self.landed += landed
        return landed[0]


def _chip_sum(k, part, got, place):
    _, half, cols = part.shape
    tr = _row_tile(half)
    n_t = half // tr

    def body(place_ref, a_ref, b_ref, o_ref):
        acc = a_ref[...].astype(F32)
        for j in range(3):
            acc = acc + b_ref[j].astype(F32)
        o_ref[...] = acc

    return pl.pallas_call(
        body, out_shape=SDS((2 * half, cols), F32),
        grid_spec=pltpu.PrefetchScalarGridSpec(
            num_scalar_prefetch=1, grid=(n_t,),
            in_specs=[BS((None, tr, cols), lambda i, place_ref: (place_ref[0], i, 0)),
                      BS((3, tr, cols), lambda i, place_ref: (0, i, 0))],
            out_specs=BS((tr, cols), lambda i, place_ref: (place_ref[1] * n_t + i, 0))),
        name=f"reduce_sum{k}", compiler_params=_params("parallel"))(place, part, got)


def _sibling_join_halves(fulls):
    n = len(fulls)

    def body(*refs):
        outs = refs[n:2 * n]
        send_sems, recv_sems = refs[2 * n:]
        x, y, c, _ = _mesh_place()
        sibling = (x, y, 1 - c)
        sent = []
        for k in range(n):
            half = fulls[k].shape[0] // 2
            mine = outs[k].at[pl.ds(pl.multiple_of(c * half, 8), half), :]
            cp = _remote(mine, mine, send_sems, recv_sems, k, sibling)
            cp.start()
            sent.append(cp)
        for k in range(n):
            half = fulls[k].shape[0] // 2
            theirs = outs[k].at[pl.ds(pl.multiple_of((1 - c) * half, 8), half), :]
            _remote(theirs, theirs, send_sems, recv_sems, k, sibling).wait_recv()
        for cp in sent:
            cp.wait_send()

    hbm = BS(memory_space=pl.ANY)
    return pl.pallas_call(
        body, out_shape=tuple(SDS(f.shape, f.dtype) for f in fulls),
        in_specs=[hbm] * n, out_specs=(hbm,) * n, input_output_aliases={k: k for k in range(n)},
        scratch_shapes=[pltpu.SemaphoreType.DMA((n,)), pltpu.SemaphoreType.DMA((n,))],
        name="reduce_sibling_join", compiler_params=_params())(*fulls)


N_DEV = 8


def _allreduce_small(part):
    rows, lanes = part.shape

    def body(x_ref, out_ref, all_ref, send_sems, recv_sems, local_sem):
        x, y, c, chips = _mesh_place()
        me, sibling = (x, y, c), (x, y, 1 - c)

        def blk(px, py, pc):
            return all_ref.at[pl.ds(pl.multiple_of((4 * px + 2 * py + pc) * rows, 8), rows), :]

        def copy(k, block, to, src=None):
            return _remote(blk(*block) if src is None else src, blk(*block), send_sems, recv_sems, k, to)

        mine = pltpu.make_async_copy(x_ref, blk(*me), local_sem)
        mine.start()
        first = [copy(0, me, sibling, src=x_ref)]
        first += [copy(1 + j, me, (*chip, c), src=x_ref) for j, chip in enumerate(chips)]
        for cp in first:
            cp.start()
        passed = [copy(4 + j, (*chip, c), sibling) for j, chip in enumerate(chips)]
        for j, chip in enumerate(chips):
            copy(1 + j, (*chip, c), me).wait_recv()
            passed[j].start()
        copy(0, sibling, me).wait_recv()
        for j, chip in enumerate(chips):
            copy(4 + j, (*chip, 1 - c), me).wait_recv()
        for cp in first + passed:
            cp.wait_send()
        mine.wait()
        acc = all_ref[pl.ds(0, rows), :]
        for dev in range(1, N_DEV):
            acc = acc + all_ref[pl.ds(dev * rows, rows), :]
        out_ref[...] = acc

    vm = BS(memory_space=pltpu.VMEM)
    return pl.pallas_call(
        body, out_shape=SDS((rows, lanes), F32), in_specs=[vm], out_specs=vm,
        scratch_shapes=[pltpu.VMEM((N_DEV * rows, lanes), F32), pltpu.SemaphoreType.DMA((7,)),
                        pltpu.SemaphoreType.DMA((7,)), pltpu.SemaphoreType.DMA],
        name="allreduce_small", compiler_params=_params())(part)


def _adamw(name, w, grad, row0, m, v):
    rows, cols = w.shape
    tr = rows if rows < 16 else _row_tile(rows, 256)
    bc1 = 1.0 - ADAM_B1 ** ADAM_STEP
    bc2 = 1.0 - ADAM_B2 ** ADAM_STEP

    def body(w_ref, g_ref, m_ref, v_ref, go_ref, d_ref, mo_ref, vo_ref):
        g = g_ref[...]
        m_new = ADAM_B1 * m_ref[...] + (1.0 - ADAM_B1) * g
        v_new = ADAM_B2 * v_ref[...] + (1.0 - ADAM_B2) * (g * g)
        go_ref[...] = g
        mo_ref[...] = m_new
        vo_ref[...] = v_new
        d_ref[...] = -ADAM_LR * ((m_new / bc1) / (jnp.sqrt(v_new / bc2) + ADAM_EPS) + ADAM_WD * w_ref[...])

    blk = BS((tr, cols), lambda i: (i, 0))
    shape = SDS((rows, cols), F32)
    return pl.pallas_call(
        body, out_shape=(shape,) * 4, grid=(rows // tr,),
        in_specs=[blk, BS((tr, cols), lambda i: (row0 // tr + i, 0)), blk, blk], out_specs=(blk,) * 4,
        name=name, compiler_params=_params("parallel"))(w, grad, m, v)


SMALL_LANES = 128


def _pack_small(parts):
    flat = jnp.concatenate([jnp.ravel(p) for p in parts])
    rows = -(-flat.shape[0] // (64 * SMALL_LANES)) * 64
    return jnp.pad(flat, (0, rows * SMALL_LANES - flat.shape[0])).reshape(rows, SMALL_LANES)


def _unpack_small(packed, like):
    flat = jnp.ravel(packed)
    out, at = [], 0
    for p in like:
        out.append(flat[at:at + p.size].reshape(p.shape))
        at += p.size
    return out


def kernel(x, mem, ffn1_norm, ffn1_w_gate, ffn1_w_up, ffn1_w_down, mix_norm, w_in, pool_w, pool_scale, w_pool_proj, ssm_a_re, ssm_a_im, ssm_log_dt, ssm_b_re, ssm_b_im, ssm_c_re, ssm_c_im, ssm_d, w_glu_val, w_glu_gate, w_mix_out, xattn_norm, mem_norm, w_q, w_kv, w_xo, ffn2_norm, ffn2_w_gate, ffn2_w_up, ffn2_w_down, final_norm, loss_target, m_ffn1_norm, m_ffn1_w_gate, m_ffn1_w_up, m_ffn1_w_down, m_mix_norm, m_w_in, m_pool_w, m_pool_scale, m_w_pool_proj, m_ssm_a_re, m_ssm_a_im, m_ssm_log_dt, m_ssm_b_re, m_ssm_b_im, m_ssm_c_re, m_ssm_c_im, m_ssm_d, m_w_glu_val, m_w_glu_gate, m_w_mix_out, m_xattn_norm, m_mem_norm, m_w_q, m_w_kv, m_w_xo, m_ffn2_norm, m_ffn2_w_gate, m_ffn2_w_up, m_ffn2_w_down, m_final_norm, v_ffn1_norm, v_ffn1_w_gate, v_ffn1_w_up, v_ffn1_w_down, v_mix_norm, v_w_in, v_pool_w, v_pool_scale, v_w_pool_proj, v_ssm_a_re, v_ssm_a_im, v_ssm_log_dt, v_ssm_b_re, v_ssm_b_im, v_ssm_c_re, v_ssm_c_im, v_ssm_d, v_w_glu_val, v_w_glu_gate, v_w_mix_out, v_xattn_norm, v_mem_norm, v_w_q, v_w_kv, v_w_xo, v_ffn2_norm, v_ffn2_w_gate, v_ffn2_w_up, v_ffn2_w_down, v_final_norm):
    given = dict(locals())
    w = {n: given[n] for n in WEIGHTS}
    m = {n: given["m_" + n] for n in WEIGHTS}
    v = {n: given["v_" + n] for n in WEIGHTS}

    def shard_view(a, n):
        return a[0].T if n in TRANSPOSED else a[0]

    def shard_unview(a, n):
        return (a.T if n in TRANSPOSED else a)[None]

    shards = [jnp.concatenate([shard_view(w[n], n).astype(BF16) for n in grp], axis=0) for grp in GATHER_GROUPS]
    gathered = _allgather_shards(shards)

    reducer = _GradReducer()
    loss_part, grad_x, _, small = _device_step(x[0], mem[0], loss_target[0], gathered, {n: w[n] for n in SMALL}, reducer)
    loss = lax.psum(loss_part[0, 0], ("x", "y", "c"))

    place = jnp.stack([2 * lax.axis_index("x") + lax.axis_index("y"), lax.axis_index("c")]).astype(jnp.int32)
    halves = [_chip_sum(k, part, got, place) for k, (part, got) in enumerate(zip(reducer.parts, reducer.landed))]
    reduced = _sibling_join_halves(halves)

    grads, delta, new_m, new_v = {}, {}, {}, {}
    for grp, red in zip(REDUCE_GROUPS, reduced):
        row0 = 0
        for n in grp:
            w_n = shard_view(w[n], n)
            grads[n], delta[n], new_m[n], new_v[n] = (
                shard_unview(o, n) for o in _adamw("adamw_" + n, w_n, red, row0, shard_view(m[n], n), shard_view(v[n], n)))
            row0 += w_n.shape[0]

    g_small = dict(zip(SMALL, _unpack_small(_allreduce_small(_pack_small([small[n] for n in SMALL])), [w[n] for n in SMALL])))
    narrow = [n for n in SMALL if w[n].ndim > 3]
    dense = [n for n in SMALL if n not in narrow]
    for n in narrow:
        two_d = (-1, w[n].shape[-1])
        outs = _adamw("adamw_" + n, w[n].reshape(two_d), g_small[n].reshape(two_d), 0, m[n].reshape(two_d), v[n].reshape(two_d))
        grads[n], delta[n], new_m[n], new_v[n] = (o.reshape(w[n].shape) for o in outs)
    dense_like = [w[n] for n in dense]
    packed = _adamw("adamw_small", _pack_small(dense_like), _pack_small([g_small[n] for n in dense]), 0,
                    _pack_small([m[n] for n in dense]), _pack_small([v[n] for n in dense]))
    for out, store in zip(packed, (grads, delta, new_m, new_v)):
        for n, val in zip(dense, _unpack_small(out, dense_like)):
            store[n] = val

    return (loss, grad_x[None], *[grads[n] for n in WEIGHTS], *[delta[n] for n in WEIGHTS],
            *[new_m[n] for n in WEIGHTS], *[new_v[n] for n in WEIGHTS])
```

```python
import functools
import math

import jax
import jax.numpy as jnp
from jax import lax
from jax.experimental import pallas as pl
from jax.experimental.pallas import tpu as pltpu

F32 = jnp.float32
BF16 = jnp.bfloat16
SDS = jax.ShapeDtypeStruct
BS = pl.BlockSpec
MESH = pl.DeviceIdType.MESH

D_MODEL = 1024
D_FF = 2816
N_SHARD = 4
FF_SH = D_FF // N_SHARD
D_POOL = 512
POOL_WINDOWS = (2, 4, 8, 16)
POOL_GROUP = 128
D_SSM = 256
SSM_GROUPS = 16
SSM_GROUP = 16
SSM_STATE = 64
SSM_CH = SSM_GROUPS * SSM_STATE
N_HEADS = 4
HEAD_DIM = 256
EPS = 1e-6
ADAM_LR, ADAM_B1, ADAM_B2, ADAM_EPS, ADAM_WD, ADAM_STEP = 0.001, 0.9, 0.999, 1e-08, 0.01, 10

VMEM_LIMIT_V7X = 52 * 1024 * 1024
TM = 512

NN = (((1,), (0,)), ((), ()))
NT = (((1,), (1,)), ((), ()))
TN = (((0,), (0,)), ((), ()))


def _params(*sem):
    return pltpu.CompilerParams(dimension_semantics=sem if sem else None, vmem_limit_bytes=VMEM_LIMIT_V7X)


def _dot(a, b, dims=NN):
    return lax.dot_general(a.astype(BF16), b.astype(BF16), dims, preferred_element_type=F32)


def _sigmoid(v):
    return 1.0 / (1.0 + jnp.exp(-v))


def _block_dims(spec):
    return tuple(d for d in spec.block_shape if d is not None)


def _after_operands(after):
    return list(after), [BS(memory_space=pl.ANY)] * len(after)


def _mm(name, pairs, *, grid, out_shape, out_spec, red_axis=None, extras=(), epilogue=None, after=()):
    n_pairs, n_extra = len(pairs), len(extras)
    n_red = grid[red_axis] if red_axis is not None else 1
    dims = [p[4] for p in pairs]

    def body(*refs):
        ab = refs[:2 * n_pairs]
        ex = refs[2 * n_pairs:2 * n_pairs + n_extra]
        o_ref = refs[2 * n_pairs + n_extra + len(after)]

        def partial():
            acc = None
            for p in range(n_pairs):
                t = _dot(ab[2 * p][...], ab[2 * p + 1][...], dims[p])
                acc = t if acc is None else acc + t
            return acc

        def finish(acc):
            res = epilogue(acc, *[e[...] for e in ex]) if epilogue is not None else acc
            o_ref[...] = res.astype(o_ref.dtype)

        if n_red == 1:
            finish(partial())
        else:
            acc_ref = refs[-1]
            k = pl.program_id(red_axis)

            @pl.when(k == 0)
            def _():
                acc_ref[...] = jnp.zeros_like(acc_ref)

            acc_ref[...] += partial()

            @pl.when(k == n_red - 1)
            def _():
                finish(acc_ref[...])

    operands, in_specs = [], []
    for a, a_spec, b, b_spec, _ in pairs:
        operands += [a, b]
        in_specs += [a_spec, b_spec]
    for e, e_spec in extras:
        operands.append(e)
        in_specs.append(e_spec)
    after_ops, after_specs = _after_operands(after)
    operands += after_ops
    in_specs += after_specs
    scratch = [pltpu.VMEM(_block_dims(out_spec), F32)] if n_red > 1 else []
    sem = tuple("arbitrary" if ax == red_axis else "parallel" for ax in range(len(grid)))
    return pl.pallas_call(body, out_shape=out_shape, grid=grid, in_specs=in_specs, out_specs=out_spec,
                          scratch_shapes=scratch, name=name, compiler_params=_params(*sem))(*operands)


def _rmsnorm(name, h, gain, tm, after=()):
    t, d = h.shape
    after_ops, after_specs = _after_operands(after)

    def body(h_ref, g_ref, *rest):
        u_ref = rest[-1]
        hv = h_ref[...]
        r = lax.rsqrt(jnp.mean(hv * hv, axis=-1, keepdims=True) + EPS)
        u_ref[...] = ((hv * r) * g_ref[...]).astype(u_ref.dtype)

    return pl.pallas_call(
        body, out_shape=SDS((t, d), BF16), grid=(t // tm,),
        in_specs=[BS((tm, d), lambda i: (i, 0)), BS((1, d), lambda i: (0, 0))] + after_specs,
        out_specs=BS((tm, d), lambda i: (i, 0)), name=name, compiler_params=_params("parallel"))(h, gain, *after_ops)


def _rmsnorm_bwd(name, h, gain, du, dh_in, tm):
    t, d = h.shape
    has_in = dh_in is not None

    def body(*refs):
        if has_in:
            h_ref, g_ref, du_ref, dhin_ref, dh_ref, dhb_ref, dg_ref = refs
        else:
            h_ref, g_ref, du_ref, dh_ref, dhb_ref, dg_ref = refs
        i = pl.program_id(0)
        hv = h_ref[...]
        r = lax.rsqrt(jnp.mean(hv * hv, axis=-1, keepdims=True) + EPS)
        n = hv * r
        duv = du_ref[...].astype(F32)
        dn = duv * g_ref[...]
        dh = r * (dn - n * jnp.mean(dn * n, axis=-1, keepdims=True))
        if has_in:
            dh = dhin_ref[...] + dh
        dh_ref[...] = dh
        dhb_ref[...] = dh.astype(BF16)

        @pl.when(i == 0)
        def _():
            dg_ref[...] = jnp.zeros_like(dg_ref)

        dg_ref[...] += jnp.sum(duv * n, axis=0, keepdims=True)

    row = BS((tm, d), lambda i: (i, 0))
    vec = BS((1, d), lambda i: (0, 0))
    operands = [h, gain, du] + ([dh_in] if has_in else [])
    in_specs = [row, vec, row] + ([row] if has_in else [])
    return pl.pallas_call(
        body, out_shape=(SDS((t, d), F32), SDS((t, d), BF16), SDS((1, d), F32)), grid=(t // tm,),
        in_specs=in_specs, out_specs=(row, row, vec), name=name, compiler_params=_params("arbitrary"))(*operands)


def _loss_head(h, gain, target, tm):
    t, d = h.shape

    def body(h_ref, g_ref, t_ref, loss_ref, dh_ref, dhb_ref, dg_ref):
        i = pl.program_id(0)
        hv = h_ref[...]
        g = g_ref[...]
        r = lax.rsqrt(jnp.mean(hv * hv, axis=-1, keepdims=True) + EPS)
        n = hv * r
        err = n * g - t_ref[...]
        dy = err * (1.0 / d)
        dn = dy * g
        dh = r * (dn - n * jnp.mean(dn * n, axis=-1, keepdims=True))
        dh_ref[...] = dh
        dhb_ref[...] = dh.astype(BF16)

        @pl.when(i == 0)
        def _():
            dg_ref[...] = jnp.zeros_like(dg_ref)
            loss_ref[...] = jnp.zeros_like(loss_ref)

        dg_ref[...] += jnp.sum(dy * n, axis=0, keepdims=True)
        part = 0.5 * jnp.sum(jnp.mean(err * err, axis=-1, keepdims=True), axis=0, keepdims=True)
        loss_ref[...] += jnp.broadcast_to(part, loss_ref.shape)

    row = BS((tm, d), lambda i: (i, 0))
    vec = BS((1, d), lambda i: (0, 0))
    return pl.pallas_call(
        body, out_shape=(SDS((1, 128), F32), SDS((t, d), F32), SDS((t, d), BF16), SDS((1, d), F32)),
        grid=(t // tm,), in_specs=[row, vec, row],
        out_specs=(BS((1, 128), lambda i: (0, 0)), row, row, vec),
        name="loss_head", compiler_params=_params("arbitrary"))(h, gain, target)


FFN_GATE, FFN_UP, FFN_DOWN = 0, 1, 2


def _ffn_w_spec(block, index):
    return BS((None, FF_SH, D_MODEL), lambda *g: (g[index], block, 0))


def _ffn_up(name, u, w_f, tm, after=()):
    t, d = u.shape
    after_ops, after_specs = _after_operands(after)

    def body(u_ref, wg_ref, wu_ref, *rest):
        g_ref, up_ref, a_ref = rest[len(after_ops):]
        uv = u_ref[...]
        g = _dot(uv, wg_ref[...], NT)
        up = _dot(uv, wu_ref[...], NT)
        g_ref[...] = g.astype(BF16)
        up_ref[...] = up.astype(BF16)
        a_ref[...] = (g * _sigmoid(g) * up).astype(BF16)

    hid = BS((None, tm, FF_SH), lambda s, i: (s, i, 0))
    shape = SDS((N_SHARD, t, FF_SH), BF16)
    return pl.pallas_call(
        body, out_shape=(shape, shape, shape), grid=(N_SHARD, t // tm),
        in_specs=[BS((tm, d), lambda s, i: (i, 0)), _ffn_w_spec(w_f["gate"][1], 0), _ffn_w_spec(w_f["up"][1], 0)] + after_specs,
        out_specs=(hid, hid, hid), name=name,
        compiler_params=_params("parallel", "parallel"))(u, w_f["gate"][0], w_f["up"][0], *after_ops)


def _ffn_down(name, a, w_f, resid, tm, after=()):
    t, d = resid.shape
    return _mm(name, [(a, BS((None, tm, FF_SH), lambda i, s: (s, i, 0)), w_f["down"][0], _ffn_w_spec(w_f["down"][1], 1), NN)],
               grid=(t // tm, N_SHARD), red_axis=1, out_shape=SDS((t, d), F32),
               out_spec=BS((tm, d), lambda i, s: (i, 0)),
               extras=[(resid, BS((tm, d), lambda i, s: (i, 0)))],
               epilogue=lambda acc, res: res + 0.5 * acc, after=after)


def _ffn_bwd_act(name, dh_b, w_f, g, up, tm, after=()):
    t, d = dh_b.shape
    after_ops, after_specs = _after_operands(after)

    def body(dh_ref, wd_ref, g_ref, up_ref, *rest):
        dg_ref, dup_ref = rest[len(after_ops):]
        da = 0.5 * _dot(dh_ref[...], wd_ref[...], NT)
        gv = g_ref[...].astype(F32)
        uv = up_ref[...].astype(F32)
        sg = _sigmoid(gv)
        silu = gv * sg
        dg_ref[...] = (da * uv * (sg + silu * (1.0 - sg))).astype(BF16)
        dup_ref[...] = (da * silu).astype(BF16)

    hid = BS((None, tm, FF_SH), lambda s, i: (s, i, 0))
    shape = SDS((N_SHARD, t, FF_SH), BF16)
    return pl.pallas_call(
        body, out_shape=(shape, shape), grid=(N_SHARD, t // tm),
        in_specs=[BS((tm, d), lambda s, i: (i, 0)), _ffn_w_spec(w_f["down"][1], 0), hid, hid] + after_specs,
        out_specs=(hid, hid), name=name,
        compiler_params=_params("parallel", "parallel"))(dh_b, w_f["down"][0], g, up, *after_ops)


def _ffn_dw(name, u, dg, dup, a, dh_b, tm):
    t, d = u.shape
    n_t = t // tm

    def body(u_ref, dg_ref, dup_ref, a_ref, dh_ref, o_ref, acc):
        i = pl.program_id(1)

        @pl.when(i == 0)
        def _():
            acc[...] = jnp.zeros_like(acc)

        uv = u_ref[...]
        acc[FFN_GATE] += _dot(dg_ref[...], uv, TN)
        acc[FFN_UP] += _dot(dup_ref[...], uv, TN)
        acc[FFN_DOWN] += _dot(a_ref[...], dh_ref[...], TN)

        @pl.when(i == n_t - 1)
        def _():
            o_ref[FFN_GATE] = acc[FFN_GATE].astype(BF16)
            o_ref[FFN_UP] = acc[FFN_UP].astype(BF16)
            o_ref[FFN_DOWN] = (0.5 * acc[FFN_DOWN]).astype(BF16)

    hid = BS((None, tm, FF_SH), lambda s, i: (s, i, 0))
    row = BS((tm, d), lambda s, i: (i, 0))
    return pl.pallas_call(
        body, out_shape=SDS((N_SHARD, 3, FF_SH, d), BF16), grid=(N_SHARD, n_t),
        in_specs=[row, hid, hid, hid, row], out_specs=BS((None, 3, FF_SH, d), lambda s, i: (s, 0, 0, 0)),
        scratch_shapes=[pltpu.VMEM((3, FF_SH, d), F32)],
        name=name, compiler_params=_params("parallel", "arbitrary"))(u, dg, dup, a, dh_b)


def _ffn_dx(name, dg, dup, w_f, tm, after=()):
    t = dg.shape[1]
    hid = BS((None, tm, FF_SH), lambda i, s: (s, i, 0))
    return _mm(name, [(dg, hid, w_f["gate"][0], _ffn_w_spec(w_f["gate"][1], 1), NN),
                      (dup, hid, w_f["up"][0], _ffn_w_spec(w_f["up"][1], 1), NN)],
               grid=(t // tm, N_SHARD), red_axis=1, out_shape=SDS((t, D_MODEL), F32),
               out_spec=BS((tm, D_MODEL), lambda i, s: (i, 0)), after=after)


def _plain_mm(name, a, b, dims, out_dtype, tm, resid=None):
    t = a.shape[0]
    n = b.shape[1] if dims == NN else b.shape[0]
    extras = [(resid, BS((tm, n), lambda i: (i, 0)))] if resid is not None else []
    epi = (lambda acc, res: res + acc) if resid is not None else None
    return _mm(name, [(a, BS((tm, a.shape[1]), lambda i: (i, 0)), b, BS(b.shape, lambda i: (0, 0)), dims)],
               grid=(t // tm,), out_shape=SDS((t, n), out_dtype), out_spec=BS((tm, n), lambda i: (i, 0)),
               extras=extras, epilogue=epi)


def _dw_mm(name, a, b, tm, out_dtype=BF16):
    t, k = a.shape
    n = b.shape[1]
    return _mm(name, [(a, BS((tm, k), lambda i: (i, 0)), b, BS((tm, n), lambda i: (i, 0)), TN)],
               grid=(t // tm,), red_axis=0, out_shape=SDS((k, n), out_dtype), out_spec=BS((k, n), lambda i: (0, 0)))


POOL_CHUNK = 256
POOL_HALO = 8


def _window_sum(v, width, lead):
    n = v.shape[0]
    s = v
    k = 1
    while k < width:
        s = s + pltpu.roll(s, n - k, 0)
        k *= 2
    return pltpu.roll(s, lead, 0) if lead else s


def _pool_count(base, left, right, t, shape):
    pos = base + lax.broadcasted_iota(jnp.int32, shape, 0)
    lo = jnp.maximum(pos - left, 0)
    hi = jnp.minimum(pos + right + 1, t)
    return (hi - lo).astype(F32)


def _pool_fwd(proj, pool_w, pool_scale):
    t = proj.shape[0]
    c, h = POOL_CHUNK, POOL_HALO
    n_chunks = t // c

    def body(proj_hbm, pw_ref, sc_ref, pooled_ref, mixed_ref, ms_ref, pad_ref, sem):
        cp = pltpu.make_async_copy(proj_hbm.at[:, pl.ds(0, D_POOL)], pad_ref.at[pl.ds(h, t), :], sem)
        cp.start()
        pad_ref[pl.ds(0, h), :] = jnp.zeros((h, D_POOL), F32)
        pad_ref[pl.ds(t + h, h), :] = jnp.zeros((h, D_POOL), F32)
        cp.wait()
        for g, width in enumerate(POOL_WINDOWS):
            left = width // 2
            right = width - 1 - left
            cols = slice(g * POOL_GROUP, (g + 1) * POOL_GROUP)
            wmat = pw_ref[g].astype(BF16)
            scale = sc_ref[:, cols]

            def chunk(ci, carry, left=left, right=right, width=width, cols=cols, wmat=wmat, scale=scale):
                base = pl.multiple_of(ci * c, c)
                v = pad_ref[pl.ds(base, c + 2 * h), cols]
                win = _window_sum(v, width, left)[h:h + c]
                cnt = _pool_count(base, left, right, t, (c, POOL_GROUP))
                pooled = (win / cnt - v[h:h + c]).astype(BF16)
                mixed = _dot(pooled, wmat)
                pooled_ref[pl.ds(base, c), cols] = pooled
                mixed_ref[pl.ds(base, c), cols] = mixed.astype(BF16)
                ms_ref[pl.ds(base, c), cols] = (mixed * scale).astype(BF16)
                return carry

            lax.fori_loop(0, n_chunks, chunk, 0)

    vm = BS(memory_space=pltpu.VMEM)
    shape = SDS((t, D_POOL), BF16)
    return pl.pallas_call(
        body, out_shape=(shape, shape, shape),
        in_specs=[BS(memory_space=pl.ANY), vm, vm], out_specs=(vm, vm, vm),
        scratch_shapes=[pltpu.VMEM((t + 2 * h, D_POOL), F32), pltpu.SemaphoreType.DMA],
        name="pool_fwd", compiler_params=_params())(proj, pool_w, pool_scale)


def _pool_bwd(d_ms, mixed, pooled, pool_w, pool_scale):
    t = d_ms.shape[0]
    c, h = POOL_CHUNK, POOL_HALO
    n_chunks = t // c

    def body(dms_ref, mixed_ref, pooled_ref, pw_ref, sc_ref, dp_ref, dsc_ref, dpw_ref, pad_ref):
        pad_ref[pl.ds(0, h), :] = jnp.zeros((h, D_POOL), F32)
        pad_ref[pl.ds(t + h, h), :] = jnp.zeros((h, D_POOL), F32)
        for g, width in enumerate(POOL_WINDOWS):
            left = width // 2
            right = width - 1 - left
            cols = slice(g * POOL_GROUP, (g + 1) * POOL_GROUP)
            wmat = pw_ref[g].astype(BF16)
            scale = sc_ref[:, cols]

            def first(ci, carry, left=left, right=right, cols=cols, wmat=wmat, scale=scale):
                dsc, dpw = carry
                base = pl.multiple_of(ci * c, c)
                dms = dms_ref[pl.ds(base, c), cols].astype(F32)
                dsc = dsc + jnp.sum(dms * mixed_ref[pl.ds(base, c), cols].astype(F32), axis=0, keepdims=True)
                dmix = (dms * scale).astype(BF16)
                dpw = dpw + _dot(pooled_ref[pl.ds(base, c), cols], dmix, TN)
                dpooled = _dot(dmix, wmat, NT)
                cnt = _pool_count(base, left, right, t, (c, POOL_GROUP))
                pad_ref[pl.ds(base + h, c), cols] = dpooled / cnt
                return dsc, dpw

            dsc, dpw = lax.fori_loop(0, n_chunks, first,
                                     (jnp.zeros((1, POOL_GROUP), F32), jnp.zeros((POOL_GROUP, POOL_GROUP), F32)))
            dsc_ref[:, cols] = dsc
            dpw_ref[g] = dpw

            def second(ci, carry, left=left, right=right, width=width, cols=cols):
                base = pl.multiple_of(ci * c, c)
                v = pad_ref[pl.ds(base, c + 2 * h), cols]
                win = _window_sum(v, width, right)[h:h + c]
                cnt = _pool_count(base, left, right, t, (c, POOL_GROUP))
                dp_ref[pl.ds(base, c), cols] = (win - v[h:h + c] * cnt).astype(BF16)
                return carry

            lax.fori_loop(0, n_chunks, second, 0)

    vm = BS(memory_space=pltpu.VMEM)
    return pl.pallas_call(
        body, out_shape=(SDS((t, D_POOL), BF16), SDS((1, D_POOL), F32), SDS((4, POOL_GROUP, POOL_GROUP), F32)),
        in_specs=[vm] * 5, out_specs=(vm, vm, vm),
        scratch_shapes=[pltpu.VMEM((t + 2 * h, D_POOL), F32)],
        name="pool_bwd", compiler_params=_params())(d_ms, mixed, pooled, pool_w, pool_scale)


def _ssm_disc(ar, ai, ldt):
    def body(ar_ref, ai_ref, ldt_ref, abr_ref, abi_ref, qr_ref, qi_ref):
        a_r, a_i = ar_ref[...], ai_ref[...]
        dt = jnp.exp(ldt_ref[...])
        mag = jnp.exp(dt * a_r)
        ang = dt * a_i
        abr = mag * jnp.cos(ang)
        abi = mag * jnp.sin(ang)
        den = a_r * a_r + a_i * a_i
        nr = abr - 1.0
        abr_ref[...] = abr
        abi_ref[...] = abi
        qr_ref[...] = (nr * a_r + abi * a_i) / den
        qi_ref[...] = (abi * a_r - nr * a_i) / den

    vm = BS(memory_space=pltpu.VMEM)
    shape = SDS(ar.shape, F32)
    return pl.pallas_call(body, out_shape=(shape,) * 4, in_specs=[vm] * 3, out_specs=(vm,) * 4,
                          name="ssm_disc", compiler_params=_params())(ar, ai, ldt)


def _ssm_disc_bwd(ar, ai, ldt, d_abr, d_abi, d_qr, d_qi):
    def body(ar_ref, ai_ref, ldt_ref, gabr_ref, gabi_ref, gqr_ref, gqi_ref, dar_ref, dai_ref, dldt_ref):
        a_r, a_i = ar_ref[...], ai_ref[...]
        dt = jnp.exp(ldt_ref[...])
        mag = jnp.exp(dt * a_r)
        ang = dt * a_i
        cs, sn = jnp.cos(ang), jnp.sin(ang)
        abr, abi = mag * cs, mag * sn
        den = a_r * a_r + a_i * a_i
        nr = abr - 1.0
        qr = (nr * a_r + abi * a_i) / den
        qi = (abi * a_r - nr * a_i) / den
        gqr, gqi = gqr_ref[...], gqi_ref[...]
        g_nr_num = gqr / den
        g_ni_num = gqi / den
        g_den = -(gqr * qr + gqi * qi) / den
        g_nr = g_nr_num * a_r - g_ni_num * a_i
        g_abi = g_nr_num * a_i + g_ni_num * a_r
        d_ar = g_nr_num * nr + g_ni_num * abi + 2.0 * a_r * g_den
        d_ai = g_nr_num * abi - g_ni_num * nr + 2.0 * a_i * g_den
        g_abr = gabr_ref[...] + g_nr
        g_abi = gabi_ref[...] + g_abi
        g_mag = g_abr * cs + g_abi * sn
        g_ang = mag * (g_abi * cs - g_abr * sn)
        g_e = g_mag * mag
        d_ar = d_ar + g_e * dt
        d_ai = d_ai + g_ang * dt
        g_dt = g_e * a_r + g_ang * a_i
        dar_ref[...] = d_ar
        dai_ref[...] = d_ai
        dldt_ref[...] = jnp.sum(g_dt * dt, axis=1, keepdims=True)

    vm = BS(memory_space=pltpu.VMEM)
    return pl.pallas_call(body, out_shape=(SDS(ar.shape, F32), SDS(ar.shape, F32), SDS(ldt.shape, F32)),
                          in_specs=[vm] * 7, out_specs=(vm,) * 3, name="ssm_disc_bwd",
                          compiler_params=_params())(ar, ai, ldt, d_abr, d_abi, d_qr, d_qi)


def _ssm_bbar(qr, qi, br, bi):
    def body(qr_ref, qi_ref, br_ref, bi_ref, bbr_ref, bbi_ref):
        q_r, q_i, b_r, b_i = qr_ref[...], qi_ref[...], br_ref[...], bi_ref[...]
        bbr_ref[...] = q_r * b_r - q_i * b_i
        bbi_ref[...] = q_r * b_i + q_i * b_r

    vm = BS(memory_space=pltpu.VMEM)
    shape = SDS(br.shape, F32)
    return pl.pallas_call(body, out_shape=(shape, shape), in_specs=[vm] * 4, out_specs=(vm, vm),
                          name="ssm_bbar", compiler_params=_params())(qr, qi, br, bi)


def _ssm_bbar_bwd(qr, qi, br, bi, g_bbr, g_bbi):
    def body(qr_ref, qi_ref, br_ref, bi_ref, gr_ref, gi_ref, dqr_ref, dqi_ref, dbr_ref, dbi_ref):
        q_r, q_i, b_r, b_i = qr_ref[...], qi_ref[...], br_ref[...], bi_ref[...]
        g_r, g_i = gr_ref[...], gi_ref[...]
        dqr_ref[...] = jnp.sum(g_r * b_r + g_i * b_i, axis=1, keepdims=True)
        dqi_ref[...] = jnp.sum(g_i * b_r - g_r * b_i, axis=1, keepdims=True)
        dbr_ref[...] = g_r * q_r + g_i * q_i
        dbi_ref[...] = g_i * q_r - g_r * q_i

    vm = BS(memory_space=pltpu.VMEM)
    return pl.pallas_call(
        body, out_shape=(SDS(qr.shape, F32), SDS(qr.shape, F32), SDS(br.shape, F32), SDS(br.shape, F32)),
        in_specs=[vm] * 6, out_specs=(vm,) * 4, name="ssm_bbar_bwd",
        compiler_params=_params())(qr, qi, br, bi, g_bbr, g_bbi)


SCAN_ROWS = 256


def _ssm_scan(name, inp, w1, a_r, a_i, w2, reverse):
    t = inp.shape[0]
    rows = SCAN_ROWS
    n = t // rows
    n_groups = rows // 8
    ch = SSM_CH
    at = (lambda i: (n - 1 - i, 0)) if reverse else (lambda i: (i, 0))

    def body(in_ref, w1_ref, ar_ref, ai_ref, w2_ref, st_ref, out_ref, cr_ref, ci_ref, k_ref):
        i = pl.program_id(0)

        @pl.when(i == 0)
        def _():
            ar8 = jnp.broadcast_to(ar_ref[...], (8, ch))
            ai8 = jnp.broadcast_to(ai_ref[...], (8, ch))
            row = lax.broadcasted_iota(jnp.int32, (8, ch), 0)
            rank = (7 - row) if reverse else row
            powers = [(ar8, ai8)]
            for _ in range(7):
                p_r, p_i = powers[-1]
                powers.append((p_r * ar8 - p_i * ai8, p_r * ai8 + p_i * ar8))
            zero = jnp.zeros((8, ch), F32)
            for slot, k in enumerate((1, 2, 4)):
                k_ref[2 * slot] = jnp.where(rank >= k, powers[k - 1][0], zero)
                k_ref[2 * slot + 1] = jnp.where(rank >= k, powers[k - 1][1], zero)
            carry_r, carry_i = zero, zero
            for j in range(8):
                carry_r = jnp.where(rank == j, powers[j][0], carry_r)
                carry_i = jnp.where(rank == j, powers[j][1], carry_i)
            k_ref[6] = carry_r
            k_ref[7] = carry_i
            cr_ref[...] = zero
            ci_ref[...] = zero

        st_ref[...] = _dot(in_ref[...], w1_ref[...])

        def group(gi, carry):
            c_r, c_i = carry
            g = (n_groups - 1 - gi) if reverse else gi
            r0 = pl.multiple_of(g * 8, 8)
            x_r = st_ref[pl.ds(r0, 8), 0:ch]
            x_i = st_ref[pl.ds(r0, 8), ch:2 * ch]
            for slot, k in enumerate((1, 2, 4)):
                shift = (8 - k) if reverse else k
                s_r = pltpu.roll(x_r, shift, 0)
                s_i = pltpu.roll(x_i, shift, 0)
                m_r, m_i = k_ref[2 * slot], k_ref[2 * slot + 1]
                x_r, x_i = x_r + m_r * s_r - m_i * s_i, x_i + m_r * s_i + m_i * s_r
            p_r, p_i = k_ref[6], k_ref[7]
            x_r, x_i = x_r + p_r * c_r - p_i * c_i, x_i + p_r * c_i + p_i * c_r
            st_ref[pl.ds(r0, 8), 0:ch] = x_r
            st_ref[pl.ds(r0, 8), ch:2 * ch] = x_i
            last = 0 if reverse else 7
            return (jnp.broadcast_to(x_r[last:last + 1, :], (8, ch)), jnp.broadcast_to(x_i[last:last + 1, :], (8, ch)))

        c_r, c_i = lax.fori_loop(0, n_groups, group, (cr_ref[...], ci_ref[...]))
        cr_ref[...] = c_r
        ci_ref[...] = c_i
        out_ref[...] = _dot(st_ref[...], w2_ref[...])

    return pl.pallas_call(
        body, out_shape=(SDS((t, 2 * ch), F32), SDS((t, D_SSM), F32)), grid=(n,),
        in_specs=[BS((rows, D_SSM), at), BS((D_SSM, 2 * ch), lambda i: (0, 0)), BS((1, ch), lambda i: (0, 0)),
                  BS((1, ch), lambda i: (0, 0)), BS((2 * ch, D_SSM), lambda i: (0, 0))],
        out_specs=(BS((rows, 2 * ch), at), BS((rows, D_SSM), at)),
        scratch_shapes=[pltpu.VMEM((8, ch), F32), pltpu.VMEM((8, ch), F32), pltpu.VMEM((8, 8, ch), F32)],
        name=name, compiler_params=_params("arbitrary"))(inp, w1, a_r, a_i, w2)


DA_ROWS = 512


def _ssm_da(name, lam, states, reverse):
    t = lam.shape[0]
    rows = DA_ROWS
    n = t // rows
    nb = rows // 8
    ch = SSM_CH
    if reverse:
        halo_at = lambda i: (jnp.minimum((i + 1) * nb, t // 8 - 1), 0)
    else:
        halo_at = lambda i: (jnp.maximum(i * nb - 1, 0), 0)

    def body(lam_ref, x_ref, halo_ref, dr_ref, di_ref):
        i = pl.program_id(0)

        @pl.when(i == 0)
        def _():
            dr_ref[...] = jnp.zeros_like(dr_ref)
            di_ref[...] = jnp.zeros_like(di_ref)

        row = lax.broadcasted_iota(jnp.int32, (rows, ch), 0)
        if reverse:
            edge, shift, h_row, live = rows - 1, rows - 1, 0, i < n - 1
        else:
            edge, shift, h_row, live = 0, 1, 7, i > 0

        def neighbour(lo):
            halo = jnp.where(live, halo_ref[h_row:h_row + 1, lo:lo + ch], 0.0)
            return jnp.where(row == edge, jnp.broadcast_to(halo, (rows, ch)), pltpu.roll(x_ref[:, lo:lo + ch], shift, 0))

        xp_r, xp_i = neighbour(0), neighbour(ch)
        l_r, l_i = lam_ref[:, 0:ch], lam_ref[:, ch:2 * ch]
        dr_ref[...] += jnp.sum(l_r * xp_r + l_i * xp_i, axis=0, keepdims=True)
        di_ref[...] += jnp.sum(l_i * xp_r - l_r * xp_i, axis=0, keepdims=True)

    blk = BS((rows, 2 * ch), lambda i: (i, 0))
    vec = BS((1, ch), lambda i: (0, 0))
    return pl.pallas_call(
        body, out_shape=(SDS((1, ch), F32), SDS((1, ch), F32)), grid=(n,),
        in_specs=[blk, blk, BS((8, 2 * ch), halo_at)], out_specs=(vec, vec),
        name=name, compiler_params=_params("arbitrary"))(lam, states, states)


GELU_C = math.sqrt(2.0 / math.pi)
GELU_K = 0.044715


def _ssm_combine(proj, y_fwd, y_bwd, d_skip, tm, after=()):
    t = proj.shape[0]
    after_ops, after_specs = _after_operands(after)

    def body(s_ref, yf_ref, yb_ref, d_ref, *rest):
        yt_ref, g_ref = rest[len(after_ops):]
        y = s_ref[...] * d_ref[...] + yf_ref[...] + yb_ref[...]
        yt_ref[...] = y
        th = jnp.tanh(GELU_C * (y + GELU_K * y * y * y))
        g_ref[...] = (0.5 * y * (1.0 + th)).astype(BF16)

    blk = BS((tm, D_SSM), lambda i: (i, 0))
    return pl.pallas_call(
        body, out_shape=(SDS((t, D_SSM), F32), SDS((t, D_SSM), BF16)), grid=(t // tm,),
        in_specs=[BS((tm, D_SSM), lambda i: (i, D_POOL // D_SSM)), blk, blk, BS((1, D_SSM), lambda i: (0, 0))] + after_specs,
        out_specs=(blk, blk), name="ssm_combine",
        compiler_params=_params("parallel"))(proj, y_fwd, y_bwd, d_skip, *after_ops)


def _ssm_ds(proj, d_yt, du_fwd, du_bwd, d_skip, tm):
    t = proj.shape[0]

    def body(s_ref, dy_ref, duf_ref, dub_ref, d_ref, ds_ref, dd_ref):
        i = pl.program_id(0)
        dy = dy_ref[...]
        ds_ref[...] = (dy * d_ref[...] + duf_ref[...] + dub_ref[...]).astype(BF16)

        @pl.when(i == 0)
        def _():
            dd_ref[...] = jnp.zeros_like(dd_ref)

        dd_ref[...] += jnp.sum(dy * s_ref[...], axis=0, keepdims=True)

    blk = BS((tm, D_SSM), lambda i: (i, 0))
    vec = BS((1, D_SSM), lambda i: (0, 0))
    return pl.pallas_call(
        body, out_shape=(SDS((t, D_SSM), BF16), SDS((1, D_SSM), F32)), grid=(t // tm,),
        in_specs=[BS((tm, D_SSM), lambda i: (i, D_POOL // D_SSM)), blk, blk, blk, vec],
        out_specs=(blk, vec), name="ssm_ds", compiler_params=_params("arbitrary"))(proj, d_yt, du_fwd, du_bwd, d_skip)


GP_BLOCK = (D_POOL + D_SSM) // 256
GS_BLOCK = GP_BLOCK + D_MODEL // 256


def _merge_specs(tm):
    return [BS((tm, D_POOL), lambda s, i: (i, 0)), BS((tm, D_SSM), lambda s, i: (i, 0)),
            BS((None, D_POOL, 256), lambda s, i: (s, 0, 0)), BS((None, D_SSM, 256), lambda s, i: (s, 2, 0)),
            BS((None, D_SSM, 256), lambda s, i: (s, 3, 0)),
            BS((tm, 256), lambda s, i: (i, GP_BLOCK + s)), BS((tm, 256), lambda s, i: (i, GS_BLOCK + s))]


def _mixer_merge(ms, yssm, w_e, proj, tm):
    t = ms.shape[0]

    def body(ms_ref, y_ref, wpp_ref, wgv_ref, wgg_ref, gp_ref, gs_ref, o_ref):
        zp = _dot(ms_ref[...], wpp_ref[...])
        yv = y_ref[...]
        zv = _dot(yv, wgv_ref[...])
        zg = _dot(yv, wgg_ref[...])
        o_ref[...] = (_sigmoid(gp_ref[...]) * zp + _sigmoid(gs_ref[...]) * zv * _sigmoid(zg)).astype(BF16)

    col = BS((tm, 256), lambda s, i: (i, s))
    return pl.pallas_call(
        body, out_shape=SDS((t, D_MODEL), BF16), grid=(N_SHARD, t // tm), in_specs=_merge_specs(tm), out_specs=col,
        name="mixer_merge", compiler_params=_params("parallel", "parallel"))(ms, yssm, w_e, w_e, w_e, proj, proj)


def _mixer_merge_bwd(ms, yssm, w_e, proj, dmerged, tm):
    t = ms.shape[0]

    def body(ms_ref, y_ref, wpp_ref, wgv_ref, wgg_ref, gp_ref, gs_ref, dm_ref,
             dgp_ref, dgs_ref, dzp_ref, dzv_ref, dzg_ref):
        zp = _dot(ms_ref[...], wpp_ref[...])
        yv = y_ref[...]
        zv = _dot(yv, wgv_ref[...])
        zg = _dot(yv, wgg_ref[...])
        dm = dm_ref[...].astype(F32)
        sp, ss, sg = _sigmoid(gp_ref[...]), _sigmoid(gs_ref[...]), _sigmoid(zg)
        dgp_ref[...] = (dm * zp * sp * (1.0 - sp)).astype(BF16)
        dgs_ref[...] = (dm * zv * sg * ss * (1.0 - ss)).astype(BF16)
        dzp_ref[...] = (dm * sp).astype(BF16)
        dz = dm * ss
        dzv_ref[...] = (dz * sg).astype(BF16)
        dzg_ref[...] = (dz * zv * sg * (1.0 - sg)).astype(BF16)

    col = BS((tm, 256), lambda s, i: (i, s))
    shape = SDS((t, D_MODEL), BF16)
    return pl.pallas_call(
        body, out_shape=(shape,) * 5, grid=(N_SHARD, t // tm), in_specs=_merge_specs(tm) + [col],
        out_specs=(col,) * 5, name="mixer_merge_bwd",
        compiler_params=_params("parallel", "parallel"))(ms, yssm, w_e, w_e, w_e, proj, proj, dmerged)


def _mixer_dw(ms, yssm, dzp, dzv, dzg, tm):
    t = ms.shape[0]
    n_t = t // tm

    def body(ms_ref, y_ref, dzp_ref, dzv_ref, dzg_ref, o_ref, acc):
        i = pl.program_id(1)

        @pl.when(i == 0)
        def _():
            acc[...] = jnp.zeros_like(acc)

        yv = y_ref[...]
        acc[0:D_POOL, :] += _dot(ms_ref[...], dzp_ref[...], TN)
        acc[D_POOL:D_POOL + D_SSM, :] += _dot(yv, dzv_ref[...], TN)
        acc[D_POOL + D_SSM:, :] += _dot(yv, dzg_ref[...], TN)

        @pl.when(i == n_t - 1)
        def _():
            o_ref[...] = acc[...].astype(BF16)

    col = BS((tm, 256), lambda s, i: (i, s))
    return pl.pallas_call(
        body, out_shape=SDS((N_SHARD, 1024, 256), BF16), grid=(N_SHARD, n_t),
        in_specs=[BS((tm, D_POOL), lambda s, i: (i, 0)), BS((tm, D_SSM), lambda s, i: (i, 0)), col, col, col],
        out_specs=BS((None, 1024, 256), lambda s, i: (s, 0, 0)), scratch_shapes=[pltpu.VMEM((1024, 256), F32)],
        name="mixer_dw", compiler_params=_params("parallel", "arbitrary"))(ms, yssm, dzp, dzv, dzg)


def _mixer_dx(dzp, dzv, dzg, w_e, y_total, tm):
    t = dzp.shape[0]

    def body(dzp_ref, dzv_ref, dzg_ref, wpp_ref, wgv_ref, wgg_ref, yt_ref, dms_ref, dy_ref, acc_ms, acc_y):
        s = pl.program_id(1)

        @pl.when(s == 0)
        def _():
            acc_ms[...] = jnp.zeros_like(acc_ms)
            acc_y[...] = jnp.zeros_like(acc_y)

        acc_ms[...] += _dot(dzp_ref[...], wpp_ref[...], NT)
        acc_y[...] += _dot(dzv_ref[...], wgv_ref[...], NT) + _dot(dzg_ref[...], wgg_ref[...], NT)

        @pl.when(s == N_SHARD - 1)
        def _():
            dms_ref[...] = acc_ms[...].astype(BF16)
            y = yt_ref[...]
            inner = GELU_C * (y + GELU_K * y * y * y)
            th = jnp.tanh(inner)
            dgelu = 0.5 * (1.0 + th) + 0.5 * y * (1.0 - th * th) * GELU_C * (1.0 + 3.0 * GELU_K * y * y)
            dy_ref[...] = acc_y[...] * dgelu

    col = BS((tm, 256), lambda i, s: (i, s))
    return pl.pallas_call(
        body, out_shape=(SDS((t, D_POOL), BF16), SDS((t, D_SSM), F32)), grid=(t // tm, N_SHARD),
        in_specs=[col, col, col, BS((None, D_POOL, 256), lambda i, s: (s, 0, 0)),
                  BS((None, D_SSM, 256), lambda i, s: (s, 2, 0)), BS((None, D_SSM, 256), lambda i, s: (s, 3, 0)),
                  BS((tm, D_SSM), lambda i, s: (i, 0))],
        out_specs=(BS((tm, D_POOL), lambda i, s: (i, 0)), BS((tm, D_SSM), lambda i, s: (i, 0))),
        scratch_shapes=[pltpu.VMEM((tm, D_POOL), F32), pltpu.VMEM((tm, D_SSM), F32)],
        name="mixer_dx", compiler_params=_params("parallel", "arbitrary"))(dzp, dzv, dzg, w_e, w_e, w_e, y_total)


def _attn_probs(q_h, k_h):
    s = _dot(q_h, k_h, NT) * (1.0 / math.sqrt(HEAD_DIM))
    e = jnp.exp(s - jnp.max(s, axis=-1, keepdims=True))
    return e / jnp.sum(e, axis=-1, keepdims=True)


def _attn_fwd(q, kv, tm):
    t = q.shape[0]
    m = kv.shape[0]

    def body(q_ref, kv_ref, o_ref):
        for hd in range(N_HEADS):
            lo = hd * HEAD_DIM
            p = _attn_probs(q_ref[:, lo:lo + HEAD_DIM], kv_ref[:, lo:lo + HEAD_DIM])
            o_ref[:, lo:lo + HEAD_DIM] = _dot(p, kv_ref[:, D_MODEL + lo:D_MODEL + lo + HEAD_DIM]).astype(BF16)

    return pl.pallas_call(
        body, out_shape=SDS((t, D_MODEL), BF16), grid=(t // tm,),
        in_specs=[BS((tm, D_MODEL), lambda i: (i, 0)), BS((m, 2 * D_MODEL), lambda i: (0, 0))],
        out_specs=BS((tm, D_MODEL), lambda i: (i, 0)), name="attn_fwd", compiler_params=_params("parallel"))(q, kv)


def _attn_bwd(q, kv, d_o, tm):
    t = q.shape[0]
    m = kv.shape[0]

    def body(q_ref, kv_ref, do_ref, dq_ref, dkv_ref):
        i = pl.program_id(0)

        @pl.when(i == 0)
        def _():
            dkv_ref[...] = jnp.zeros_like(dkv_ref)

        for hd in range(N_HEADS):
            lo = hd * HEAD_DIM
            q_h = q_ref[:, lo:lo + HEAD_DIM]
            k_h = kv_ref[:, lo:lo + HEAD_DIM]
            v_h = kv_ref[:, D_MODEL + lo:D_MODEL + lo + HEAD_DIM]
            do_h = do_ref[:, lo:lo + HEAD_DIM]
            p = _attn_probs(q_h, k_h)
            dkv_ref[:, D_MODEL + lo:D_MODEL + lo + HEAD_DIM] += _dot(p, do_h, TN)
            dp = _dot(do_h, v_h, NT)
            ds = p * (dp - jnp.sum(dp * p, axis=-1, keepdims=True)) * (1.0 / math.sqrt(HEAD_DIM))
            dq_ref[:, lo:lo + HEAD_DIM] = _dot(ds, k_h).astype(BF16)
            dkv_ref[:, lo:lo + HEAD_DIM] += _dot(ds, q_h, TN)

    row = BS((tm, D_MODEL), lambda i: (i, 0))
    full = BS((m, 2 * D_MODEL), lambda i: (0, 0))
    return pl.pallas_call(
        body, out_shape=(SDS((t, D_MODEL), BF16), SDS((m, 2 * D_MODEL), F32)), grid=(t // tm,),
        in_specs=[row, full, row], out_specs=(row, full), name="attn_bwd",
        compiler_params=_params("arbitrary"))(q, kv, d_o)


TRANSPOSED = ("ffn1_w_gate", "ffn1_w_up", "ffn2_w_gate", "ffn2_w_up", "w_in")
GATHER_PHASES = {"f1a": (("ffn1_w_gate", "ffn1_w_up"),),
                 "f1b": (("ffn1_w_down",),),
                 "win": (("w_in",),),
                 "mix": (("w_mix_out", "w_q", "w_xo"), ("w_kv",), ("w_pool_proj", "w_glu_val", "w_glu_gate")),
                 "f2": (("ffn2_w_gate", "ffn2_w_up", "ffn2_w_down"),)}
REDUCE_GROUPS = (("ffn2_w_gate", "ffn2_w_up", "ffn2_w_down"), ("w_xo",), ("w_q",), ("w_kv",), ("w_mix_out",),
                 ("w_pool_proj", "w_glu_val", "w_glu_gate"), ("w_in",), ("ffn1_w_gate", "ffn1_w_up", "ffn1_w_down"))
SMALL = ("ffn1_norm", "mix_norm", "pool_w", "pool_scale", "ssm_a_re", "ssm_a_im", "ssm_log_dt", "ssm_b_re",
         "ssm_b_im", "ssm_c_re", "ssm_c_im", "ssm_d", "xattn_norm", "mem_norm", "ffn2_norm", "final_norm")
WEIGHTS = ("ffn1_norm", "ffn1_w_gate", "ffn1_w_up", "ffn1_w_down", "mix_norm", "w_in", "pool_w", "pool_scale",
           "w_pool_proj", "ssm_a_re", "ssm_a_im", "ssm_log_dt", "ssm_b_re", "ssm_b_im", "ssm_c_re", "ssm_c_im",
           "ssm_d", "w_glu_val", "w_glu_gate", "w_mix_out", "xattn_norm", "mem_norm", "w_q", "w_kv", "w_xo",
           "ffn2_norm", "ffn2_w_gate", "ffn2_w_up", "ffn2_w_down", "final_norm")


def _block_diag_in(bb):
    eye = jnp.eye(SSM_GROUPS, dtype=bb.dtype)
    return jnp.einsum("dgph,gk->dghkp", bb, eye).reshape(2, D_SSM, SSM_CH)


def _block_diag_out(cc):
    eye = jnp.eye(SSM_GROUPS, dtype=cc.dtype)
    return jnp.einsum("dghp,gk->dgpkh", cc, eye).reshape(2, SSM_CH, D_SSM)


def _diag_blocks_in(m):
    return jnp.einsum("dghgp->dgph", m.reshape(2, SSM_GROUPS, SSM_GROUP, SSM_GROUPS, SSM_STATE))


def _diag_blocks_out(m):
    return jnp.einsum("dgpgh->dghp", m.reshape(2, SSM_GROUPS, SSM_STATE, SSM_GROUPS, SSM_GROUP))


def _device_step(x, mem, target, wts, sp, reducer=None):
    t = x.shape[0]
    tm = min(TM, t)
    g = {}

    u1 = _rmsnorm("norm_ffn1", x, sp["ffn1_norm"], tm, after=wts.start("f1a"))
    (w_gu,) = wts.finish("f1a", [u1])
    w_f1 = {"gate": (w_gu, FFN_GATE), "up": (w_gu, FFN_UP)}
    g1, up1, a1 = _ffn_up("ffn1_up", u1, w_f1, tm, after=wts.start("f1b", [w_gu]))
    (w_dn,) = wts.finish("f1b", [a1])
    w_f1["down"] = (w_dn, 0)
    h1 = _ffn_down("ffn1_down", a1, w_f1, x, tm, after=wts.start("win", [w_dn]))

    u2 = _rmsnorm("norm_mix", h1, sp["mix_norm"], tm)
    (w_in_g,) = wts.finish("win", [u2])
    w_in_t = w_in_g.reshape(D_FF, D_MODEL)
    proj = _mm("mix_in", [(u2, BS((tm, D_MODEL), lambda j, i: (i, 0)), w_in_t, BS((D_FF // 2, D_MODEL), lambda j, i: (j, 0)), NT)],
               grid=(2, t // tm), out_shape=SDS((t, D_FF), F32), out_spec=BS((tm, D_FF // 2), lambda j, i: (i, j)),
               after=wts.start("mix", [w_in_g]))
    pooled, mixed, ms = _pool_fwd(proj, sp["pool_w"][0], sp["pool_scale"])

    ar = sp["ssm_a_re"].reshape(2 * SSM_GROUPS, SSM_STATE)
    ai = sp["ssm_a_im"].reshape(2 * SSM_GROUPS, SSM_STATE)
    ldt = sp["ssm_log_dt"].reshape(2 * SSM_GROUPS, 1)
    abr, abi, qr, qi = _ssm_disc(ar, ai, ldt)
    b_r = sp["ssm_b_re"].reshape(2 * SSM_CH, SSM_GROUP)
    b_i = sp["ssm_b_im"].reshape(2 * SSM_CH, SSM_GROUP)
    qr_col, qi_col = qr.reshape(2 * SSM_CH, 1), qi.reshape(2 * SSM_CH, 1)
    bbr, bbi = _ssm_bbar(qr_col, qi_col, b_r, b_i)
    shape_b = (2, SSM_GROUPS, SSM_STATE, SSM_GROUP)
    b_mat = jnp.concatenate([_block_diag_in(bbr.reshape(shape_b)), _block_diag_in(bbi.reshape(shape_b))], axis=-1).astype(BF16)
    c_mat = jnp.concatenate([_block_diag_out(sp["ssm_c_re"][0]), -_block_diag_out(sp["ssm_c_im"][0])], axis=1).astype(BF16)
    b_mat_t = jnp.swapaxes(b_mat, 1, 2)
    c_mat_t = jnp.swapaxes(c_mat, 1, 2)
    a_r = abr.reshape(2, 1, SSM_CH)
    a_i = abi.reshape(2, 1, SSM_CH)

    s_in =proj[:, D_POOL:D_POOL + D_SSM].astype(BF16)
    states, y_dirs = [], []
    for dr in range(2):
        st, yd = _ssm_scan(f"ssm_scan_fwd{dr}", s_in, b_mat[dr], a_r[dr], a_i[dr], c_mat[dr], reverse=(dr == 1))
        states.append(st)
        y_dirs.append(yd)
    w_sq, w_kv, w_e = wts.finish("mix", y_dirs)
    w_mo, w_q, w_xo = (w_sq[:, 256 * k:256 * (k + 1)].reshape(D_MODEL, D_MODEL) for k in range(3))
    w_d = w_kv[:, None]
    y_total, yssm = _ssm_combine(proj, y_dirs[0], y_dirs[1], sp["ssm_d"], tm, after=wts.start("f2", [w_e]))

    merged = _mixer_merge(ms, yssm, w_e, proj, tm)
    h2 = _plain_mm("mix_out", merged, w_mo, NN, F32, tm, resid=h1)

    u3 = _rmsnorm("norm_xattn", h2, sp["xattn_norm"], tm)
    mem_n = _rmsnorm("norm_mem", mem, sp["mem_norm"], mem.shape[0])
    q = _plain_mm("attn_q", u3, w_q, NN, BF16, tm)
    n_mem = mem.shape[0]
    kv = _mm("attn_kv", [(mem_n, BS((n_mem, D_MODEL), lambda s: (0, 0)), w_d, BS((None, None, D_MODEL, 512), lambda s: (s, 0, 0, 0)), NN)],
             grid=(N_SHARD,), out_shape=SDS((n_mem, 2 * D_MODEL), BF16), out_spec=BS((n_mem, 512), lambda s: (0, s)))
    o = _attn_fwd(q, kv, tm)
    h3 = _plain_mm("attn_out", o, w_xo, NN, F32, tm, resid=h2)

    u4 = _rmsnorm("norm_ffn2", h3, sp["ffn2_norm"], tm)
    (w_2,) = wts.finish("f2", [u4])
    w_f2 = {"gate": (w_2, FFN_GATE), "up": (w_2, FFN_UP), "down": (w_2, FFN_DOWN)}
    g2, up2, a2 = _ffn_up("ffn2_up", u4, w_f2, tm)
    h4 = _ffn_down("ffn2_down", a2, w_f2, h3, tm)

    loss, dh4, dh4_b, g["final_norm"] = _loss_head(h4, sp["final_norm"].reshape(1, D_MODEL), target, tm)

    dg2, dup2 = _ffn_bwd_act("ffn2_bwd_act", dh4_b, w_f2, g2, up2, tm)
    dw_f2 = _ffn_dw("ffn2_dw", u4, dg2, dup2, a2, dh4_b, tm)
    du4 = _ffn_dx("ffn2_dx", dg2, dup2, w_f2, tm)
    dh3, dh3_b, g["ffn2_norm"] = _rmsnorm_bwd("norm_ffn2_bwd", h3, sp["ffn2_norm"], du4, dh4, tm)

    d_o = _plain_mm("attn_out_dx", dh3_b, w_xo, NT, BF16, tm)
    dw_xo = _dw_mm("attn_out_dw", o, dh3_b, tm)
    dq, dkv = _attn_bwd(q, kv, d_o, tm)
    dw_q = _dw_mm("attn_q_dw", u3, dq, tm)
    du3 = _plain_mm("attn_q_dx", dq, w_q, NT, F32, tm)
    dw_kv = _mm("attn_kv_dw", [(mem_n, BS((n_mem, D_MODEL), lambda s: (0, 0)), dkv, BS((n_mem, 512), lambda s: (0, s)), TN)],
                grid=(N_SHARD,), out_shape=SDS((N_SHARD, D_MODEL, 512), BF16), out_spec=BS((None, D_MODEL, 512), lambda s: (s, 0, 0)))
    dmem_n = _mm("attn_kv_dx", [(dkv, BS((n_mem, 512), lambda s: (0, s)), w_d, BS((None, None, D_MODEL, 512), lambda s: (s, 0, 0, 0)), NT)],
                 grid=(N_SHARD,), red_axis=0, out_shape=SDS((n_mem, D_MODEL), F32), out_spec=BS((n_mem, D_MODEL), lambda s: (0, 0)))
    _, _, g["mem_norm"] = _rmsnorm_bwd("norm_mem_bwd", mem, sp["mem_norm"], dmem_n, None, n_mem)
    dh2, dh2_b, g["xattn_norm"] = _rmsnorm_bwd("norm_xattn_bwd", h2, sp["xattn_norm"], du3, dh3, tm)

    dmerged = _plain_mm("mix_out_dx", dh2_b, w_mo, NT, BF16, tm)
    dw_mo = _dw_mm("mix_out_dw", merged, dh2_b, tm)
    d_gp, d_gs, dzp, dzv, dzg = _mixer_merge_bwd(ms, yssm, w_e, proj, dmerged, tm)
    dw_e = _mixer_dw(ms, yssm, dzp, dzv, dzg, tm)
    d_ms, d_yt = _mixer_dx(dzp, dzv, dzg, w_e, y_total, tm)
    dp, d_scale, d_pw = _pool_bwd(d_ms, mixed, pooled, sp["pool_w"][0], sp["pool_scale"])
    g["pool_scale"] = d_scale
    g["pool_w"] = d_pw[None]

    d_yt_b = d_yt.astype(BF16)
    du_dirs, d_abr, d_abi, d_cm, d_bm = [], [], [], [], []
    for dr in range(2):
        lam, du = _ssm_scan(f"ssm_scan_bwd{dr}", d_yt_b, c_mat_t[dr], a_r[dr], -a_i[dr], b_mat_t[dr], reverse=(dr == 0))
        du_dirs.append(du)
        da_r, da_i = _ssm_da(f"ssm_da{dr}", lam, states[dr], reverse=(dr == 1))
        d_abr.append(da_r)
        d_abi.append(da_i)
        d_cm.append(_dw_mm(f"ssm_dc{dr}", states[dr], d_yt_b, tm, F32))
        d_bm.append(_dw_mm(f"ssm_db{dr}", s_in, lam, tm, F32))
    d_cm = jnp.stack(d_cm)
    d_bm = jnp.stack(d_bm)
    g["ssm_c_re"] = _diag_blocks_out(d_cm[:, :SSM_CH])[None]
    g["ssm_c_im"] = -_diag_blocks_out(d_cm[:, SSM_CH:])[None]
    g_bbr = _diag_blocks_in(d_bm[:, :, :SSM_CH]).reshape(2 * SSM_CH, SSM_GROUP)
    g_bbi = _diag_blocks_in(d_bm[:, :, SSM_CH:]).reshape(2 * SSM_CH, SSM_GROUP)
    d_qr, d_qi, d_br, d_bi = _ssm_bbar_bwd(qr_col, qi_col, b_r, b_i, g_bbr, g_bbi)
    g["ssm_b_re"] = d_br.reshape(sp["ssm_b_re"].shape)
    g["ssm_b_im"] = d_bi.reshape(sp["ssm_b_im"].shape)
    d_ar, d_ai, d_ldt = _ssm_disc_bwd(ar, ai, ldt, jnp.stack(d_abr).reshape(ar.shape), jnp.stack(d_abi).reshape(ar.shape),
                                      d_qr.reshape(ar.shape), d_qi.reshape(ar.shape))
    g["ssm_a_re"] = d_ar.reshape(sp["ssm_a_re"].shape)
    g["ssm_a_im"] = d_ai.reshape(sp["ssm_a_im"].shape)
    g["ssm_log_dt"] = d_ldt.reshape(sp["ssm_log_dt"].shape)
    ds, g["ssm_d"] = _ssm_ds(proj, d_yt, du_dirs[0], du_dirs[1], sp["ssm_d"], tm)

    d_proj = jnp.concatenate([dp, ds, d_gp, d_gs], axis=1)
    dw_in_t = _mm("mix_in_dw", [(d_proj, BS((tm, D_FF // 2), lambda j, i: (i, j)), u2, BS((tm, D_MODEL), lambda j, i: (i, 0)), TN)],
                  grid=(2, t // tm), red_axis=1, out_shape=SDS((D_FF, D_MODEL), BF16), out_spec=BS((D_FF // 2, D_MODEL), lambda j, i: (j, 0)))
    du2 = _plain_mm("mix_in_dx", d_proj, w_in_t, NN, F32, tm)
    dh1, dh1_b, g["mix_norm"] = _rmsnorm_bwd("norm_mix_bwd", h1, sp["mix_norm"], du2, dh2, tm)

    square = (N_SHARD, D_MODEL // N_SHARD, D_MODEL)
    early = [dw_f2.reshape(N_SHARD, 3 * FF_SH, D_MODEL), dw_xo.reshape(square), dw_q.reshape(square), dw_kv,
             dw_mo.reshape(square), dw_e, dw_in_t.reshape(N_SHARD, FF_SH, D_MODEL)]
    g["final_norm"] = g["final_norm"].reshape(D_MODEL)

    travelling = reducer.start("a", early) if reducer is not None else []
    dg1, dup1 = _ffn_bwd_act("ffn1_bwd_act", dh1_b, w_f1, g1, up1, tm, after=travelling)
    dw_f1 = _ffn_dw("ffn1_dw", u1, dg1, dup1, a1, dh1_b, tm).reshape(N_SHARD, 3 * FF_SH, D_MODEL)
    if reducer is not None:
        travelling = reducer.start("b", [dw_f1], after=reducer.finish("a", [dw_f1]))
    du1 = _ffn_dx("ffn1_dx", dg1, dup1, w_f1, tm, after=travelling)
    grad_x, _, g["ffn1_norm"] = _rmsnorm_bwd("norm_ffn1_bwd", x, sp["ffn1_norm"], du1, dh1, tm)
    if reducer is not None:
        reducer.finish("b", [grad_x])
    return loss, grad_x, early + [dw_f1], g


def _mesh_place():
    x, y, c = lax.axis_index("x"), lax.axis_index("y"), lax.axis_index("c")
    chips = [(1 - x, y), (x, 1 - y), (1 - x, 1 - y)]
    return x, y, c, chips


def _remote(src, dst, send_sems, recv_sems, k, to):
    return pltpu.make_async_remote_copy(src_ref=src, dst_ref=dst, send_sem=send_sems.at[k], recv_sem=recv_sems.at[k],
                                        device_id=to, device_id_type=MESH)


def _sibling_swap_halves(tag, grads, after=()):
    n = len(grads)
    after_ops, after_specs = _after_operands(after)

    def body(*refs):
        ins, outs = refs[:n], refs[n + len(after_ops):2 * n + len(after_ops)]
        send_sems, recv_sems = refs[2 * n + len(after_ops):]
        x, y, c, _ = _mesh_place()
        sibling = (x, y, 1 - c)
        copies = []
        for k in range(n):
            half = grads[k].shape[1] // 2
            theirs = pl.ds(pl.multiple_of((1 - c) * half, 16), half)
            cp = _remote(ins[k].at[:, theirs, :], outs[k], send_sems, recv_sems, k, sibling)
            cp.start()
            copies.append(cp)
        for cp in copies:
            cp.wait_recv()
        for cp in copies:
            cp.wait_send()

    hbm = BS(memory_space=pl.ANY)
    return pl.pallas_call(
        body, out_shape=tuple(SDS((g.shape[0], g.shape[1] // 2, g.shape[2]), g.dtype) for g in grads),
        in_specs=[hbm] * n + after_specs, out_specs=(hbm,) * n,
        scratch_shapes=[pltpu.SemaphoreType.DMA((n,)), pltpu.SemaphoreType.DMA((n,))],
        name="reduce_sibling_send_" + tag, compiler_params=_params())(*grads, *after_ops)


def _row_tile(rows, cap=512):
    return max(r for r in range(16, cap + 1, 16) if rows % r == 0)


def _chip_presum(k, grad, got, c_idx):
    n_sh, rows, cols = grad.shape
    half = rows // 2
    tr = _row_tile(half)
    grad4 = grad.reshape(n_sh, 2, half, cols)

    def body(c_ref, a_ref, b_ref, o_ref):
        o_ref[...] = (a_ref[...].astype(F32) + b_ref[...].astype(F32)).astype(o_ref.dtype)

    return pl.pallas_call(
        body, out_shape=SDS((n_sh, half, cols), BF16),
        grid_spec=pltpu.PrefetchScalarGridSpec(
            num_scalar_prefetch=1, grid=(n_sh, half // tr),
            in_specs=[BS((None, None, tr, cols), lambda s, i, c_ref: (s, c_ref[0], i, 0)),
                      BS((None, tr, cols), lambda s, i, c_ref: (s, i, 0))],
            out_specs=BS((None, tr, cols), lambda s, i, c_ref: (s, i, 0))),
        name=f"reduce_presum{k}", compiler_params=_params("parallel", "parallel"))(c_idx, grad4, got)


HBM_SPEC = BS(memory_space=pltpu.HBM)
SEM_SPEC = BS(memory_space=pltpu.SEMAPHORE)
DATAFLOW = pltpu.SideEffectType.DATAFLOW_SIDE_EFFECTING


def _chip_exchange_copies(parts, lands, send_sems, recv_sems):
    _, _, c, chips = _mesh_place()
    return [_remote(parts[k].at[2 * px + py], lands[k].at[j], send_sems, recv_sems, 3 * k + j, (px, py, c))
            for k in range(len(parts)) for j, (px, py) in enumerate(chips)]


def _gather_copies(shards, lands, send_sems, recv_sems):
    x, y, c, chips = _mesh_place()
    return [_remote(shards[k], lands[k].at[2 * x + y], send_sems, recv_sems, 3 * k + j, (px, py, c))
            for k in range(len(shards)) for j, (px, py) in enumerate(chips)]


def _split_start(name, copies, sources, land_shapes, after=()):
    n = len(sources)
    after_ops, after_specs = _after_operands(after)

    def body(*refs):
        ins, lands = refs[:n], refs[n:2 * n]
        send_sems, recv_sems = refs[2 * n + len(after_ops)], refs[2 * n + len(after_ops) + 1]
        token = refs[-1]
        for cp in copies(ins, lands, send_sems, recv_sems):
            cp.start()
        token[...] = jnp.zeros_like(token)

    lands = [pltpu.with_memory_space_constraint(lax.empty(s, d), pltpu.HBM) for s, d in land_shapes]
    sources = [pltpu.with_memory_space_constraint(p, pltpu.HBM) for p in sources]
    thru = [pltpu.HBM(a.shape, a.dtype) for a in sources + lands]
    out = pl.pallas_call(
        body, name=name,
        out_shape=(pltpu.SemaphoreType.DMA((3 * n,)), pltpu.SemaphoreType.DMA((3 * n,)), *thru, SDS((8, 128), F32)),
        in_specs=[HBM_SPEC] * (2 * n) + after_specs,
        out_specs=(SEM_SPEC, SEM_SPEC, *[HBM_SPEC] * (2 * n), BS(memory_space=pltpu.VMEM)),
        input_output_aliases={i: 2 + i for i in range(2 * n)},
        compiler_params=pltpu.CompilerParams(has_side_effects=DATAFLOW))(*sources, *lands, *after_ops)
    return out[0], out[1], list(out[2:2 + n]), list(out[2 + n:2 + 2 * n]), out[-1]


def _split_wait(name, copies, send_sems, recv_sems, sources, lands, after):
    n = len(sources)
    after_ops, after_specs = _after_operands(after)

    def body(*refs):
        ins, zones = refs[:n], refs[n:2 * n]
        for cp in copies(ins, zones, refs[2 * n], refs[2 * n + 1]):
            cp.wait_send()
            cp.wait_recv()

    out = pl.pallas_call(
        body, name=name,
        out_shape=tuple(pltpu.HBM(a.shape, a.dtype) for a in sources + lands),
        in_specs=[HBM_SPEC] * (2 * n) + [SEM_SPEC, SEM_SPEC] + after_specs, out_specs=(HBM_SPEC,) * (2 * n),
        input_output_aliases={i: i for i in range(2 * n)},
        compiler_params=pltpu.CompilerParams(has_side_effects=DATAFLOW))(*sources, *lands, send_sems, recv_sems, *after_ops)
    return list(out[:n]), list(out[n:])


class _WeightGatherer:
    def __init__(self, shards):
        self.shards, self.open = shards, {}
        self.me = 2 * lax.axis_index("x") + lax.axis_index("y")

    def start(self, tag, after=()):
        shapes = [((N_SHARD,) + s.shape, s.dtype) for s in self.shards[tag]]
        self.open[tag] = _split_start("gather_start_" + tag, _gather_copies, self.shards[tag], shapes, after)
        return [self.open[tag][-1]]

    def finish(self, tag, after):
        send_sems, recv_sems, shards, lands, _ = self.open.pop(tag)
        shards, lands = _split_wait("gather_wait_" + tag, _gather_copies, send_sems, recv_sems, shards, lands, after)
        return [lax.dynamic_update_slice(zone, s[None], (self.me, 0, 0)) for zone, s in zip(lands, shards)]


class _GradReducer:
    def __init__(self):
        self.c_idx = lax.axis_index("c").astype(jnp.int32).reshape(1)
        self.parts, self.landed, self.open = [], [], {}

    def start(self, tag, grads, after=()):
        got = _sibling_swap_halves(tag, grads, after)
        parts = [_chip_presum(f"{tag}{k}", g, s, self.c_idx) for k, (g, s) in enumerate(zip(grads, got))]
        shapes = [((3,) + p.shape[1:], p.dtype) for p in parts]
        self.open[tag] = _split_start("reduce_exchange_start_" + tag, _chip_exchange_copies, parts, shapes)
        return [self.open[tag][-1]]

    def finish(self, tag, after):
        send_sems, recv_sems, parts, lands, _ = self.open.pop(tag)
        parts, landed = _split_wait("reduce_exchange_wait_" + tag, _chip_exchange_copies, send_sems, recv_sems, parts, lands, after)
        self.parts += parts
        self.landed += landed
        return landed[:1]


def _chip_sum(k, part, got, place):
    _, half, cols = part.shape
    tr = _row_tile(half)
    n_t = half // tr

    def body(place_ref, a_ref, b_ref, o_ref):
        acc = a_ref[...].astype(F32)
        for j in range(3):
            acc = acc + b_ref[j].astype(F32)
        o_ref[...] = acc

    return pl.pallas_call(
        body, out_shape=SDS((2 * half, cols), F32),
        grid_spec=pltpu.PrefetchScalarGridSpec(
            num_scalar_prefetch=1, grid=(n_t,),
            in_specs=[BS((None, tr, cols), lambda i, place_ref: (place_ref[0], i, 0)),
                      BS((3, tr, cols), lambda i, place_ref: (0, i, 0))],
            out_specs=BS((tr, cols), lambda i, place_ref: (place_ref[1] * n_t + i, 0))),
        name=f"reduce_sum{k}", compiler_params=_params("parallel"))(place, part, got)


def _sibling_join_halves(fulls):
    n = len(fulls)

    def body(*refs):
        outs = refs[n:2 * n]
        send_sems, recv_sems = refs[2 * n:]
        x, y, c, _ = _mesh_place()
        sibling = (x, y, 1 - c)
        sent = []
        for k in range(n):
            half = fulls[k].shape[0] // 2
            mine = outs[k].at[pl.ds(pl.multiple_of(c * half, 8), half), :]
            cp = _remote(mine, mine, send_sems, recv_sems, k, sibling)
            cp.start()
            sent.append(cp)
        for k in range(n):
            half = fulls[k].shape[0] // 2
            theirs = outs[k].at[pl.ds(pl.multiple_of((1 - c) * half, 8), half), :]
            _remote(theirs, theirs, send_sems, recv_sems, k, sibling).wait_recv()
        for cp in sent:
            cp.wait_send()

    hbm = BS(memory_space=pl.ANY)
    return pl.pallas_call(
        body, out_shape=tuple(SDS(f.shape, f.dtype) for f in fulls),
        in_specs=[hbm] * n, out_specs=(hbm,) * n, input_output_aliases={k: k for k in range(n)},
        scratch_shapes=[pltpu.SemaphoreType.DMA((n,)), pltpu.SemaphoreType.DMA((n,))],
        name="reduce_sibling_join", compiler_params=_params())(*fulls)


N_DEV = 8


def _allreduce_small(part):
    rows, lanes = part.shape

    def body(x_ref, out_ref, all_ref, send_sems, recv_sems, local_sem):
        x, y, c, chips = _mesh_place()
        me, sibling = (x, y, c), (x, y, 1 - c)

        def blk(px, py, pc):
            return all_ref.at[pl.ds(pl.multiple_of((4 * px + 2 * py + pc) * rows, 8), rows), :]

        def copy(k, block, to, src=None):
            return _remote(blk(*block) if src is None else src, blk(*block), send_sems, recv_sems, k, to)

        mine = pltpu.make_async_copy(x_ref, blk(*me), local_sem)
        mine.start()
        first = [copy(0, me, sibling, src=x_ref)]
        first += [copy(1 + j, me, (*chip, c), src=x_ref) for j, chip in enumerate(chips)]
        for cp in first:
            cp.start()
        passed = [copy(4 + j, (*chip, c), sibling) for j, chip in enumerate(chips)]
        for j, chip in enumerate(chips):
            copy(1 + j, (*chip, c), me).wait_recv()
            passed[j].start()
        copy(0, sibling, me).wait_recv()
        for j, chip in enumerate(chips):
            copy(4 + j, (*chip, 1 - c), me).wait_recv()
        for cp in first + passed:
            cp.wait_send()
        mine.wait()
        acc = all_ref[pl.ds(0, rows), :]
        for dev in range(1, N_DEV):
            acc = acc + all_ref[pl.ds(dev * rows, rows), :]
        out_ref[...] = acc

    vm = BS(memory_space=pltpu.VMEM)
    return pl.pallas_call(
        body, out_shape=SDS((rows, lanes), F32), in_specs=[vm], out_specs=vm,
        scratch_shapes=[pltpu.VMEM((N_DEV * rows, lanes), F32), pltpu.SemaphoreType.DMA((7,)),
                        pltpu.SemaphoreType.DMA((7,)), pltpu.SemaphoreType.DMA],
        name="allreduce_small", compiler_params=_params())(part)


def _adamw(name, w, grad, row0, m, v):
    rows, cols = w.shape
    tr = rows if rows < 16 else _row_tile(rows, 256)
    bc1 = 1.0 - ADAM_B1 ** ADAM_STEP
    bc2 = 1.0 - ADAM_B2 ** ADAM_STEP

    def body(w_ref, g_ref, m_ref, v_ref, go_ref, d_ref, mo_ref, vo_ref):
        g = g_ref[...]
        m_new = ADAM_B1 * m_ref[...] + (1.0 - ADAM_B1) * g
        v_new = ADAM_B2 * v_ref[...] + (1.0 - ADAM_B2) * (g * g)
        go_ref[...] = g
        mo_ref[...] = m_new
        vo_ref[...] = v_new
        d_ref[...] = -ADAM_LR * ((m_new / bc1) / (jnp.sqrt(v_new / bc2) + ADAM_EPS) + ADAM_WD * w_ref[...])

    blk = BS((tr, cols), lambda i: (i, 0))
    shape = SDS((rows, cols), F32)
    return pl.pallas_call(
        body, out_shape=(shape,) * 4, grid=(rows // tr,),
        in_specs=[blk, BS((tr, cols), lambda i: (row0 // tr + i, 0)), blk, blk], out_specs=(blk,) * 4,
        name=name, compiler_params=_params("parallel"))(w, grad, m, v)


SMALL_LANES = 128


def _pack_small(parts):
    flat = jnp.concatenate([jnp.ravel(p) for p in parts])
    rows = -(-flat.shape[0] // (64 * SMALL_LANES)) * 64
    return jnp.pad(flat, (0, rows * SMALL_LANES - flat.shape[0])).reshape(rows, SMALL_LANES)


def _unpack_small(packed, like):
    flat = jnp.ravel(packed)
    out, at = [], 0
    for p in like:
        out.append(flat[at:at + p.size].reshape(p.shape))
        at += p.size
    return out


def kernel(x, mem, ffn1_norm, ffn1_w_gate, ffn1_w_up, ffn1_w_down, mix_norm, w_in, pool_w, pool_scale, w_pool_proj, ssm_a_re, ssm_a_im, ssm_log_dt, ssm_b_re, ssm_b_im, ssm_c_re, ssm_c_im, ssm_d, w_glu_val, w_glu_gate, w_mix_out, xattn_norm, mem_norm, w_q, w_kv, w_xo, ffn2_norm, ffn2_w_gate, ffn2_w_up, ffn2_w_down, final_norm, loss_target, m_ffn1_norm, m_ffn1_w_gate, m_ffn1_w_up, m_ffn1_w_down, m_mix_norm, m_w_in, m_pool_w, m_pool_scale, m_w_pool_proj, m_ssm_a_re, m_ssm_a_im, m_ssm_log_dt, m_ssm_b_re, m_ssm_b_im, m_ssm_c_re, m_ssm_c_im, m_ssm_d, m_w_glu_val, m_w_glu_gate, m_w_mix_out, m_xattn_norm, m_mem_norm, m_w_q, m_w_kv, m_w_xo, m_ffn2_norm, m_ffn2_w_gate, m_ffn2_w_up, m_ffn2_w_down, m_final_norm, v_ffn1_norm, v_ffn1_w_gate, v_ffn1_w_up, v_ffn1_w_down, v_mix_norm, v_w_in, v_pool_w, v_pool_scale, v_w_pool_proj, v_ssm_a_re, v_ssm_a_im, v_ssm_log_dt, v_ssm_b_re, v_ssm_b_im, v_ssm_c_re, v_ssm_c_im, v_ssm_d, v_w_glu_val, v_w_glu_gate, v_w_mix_out, v_xattn_norm, v_mem_norm, v_w_q, v_w_kv, v_w_xo, v_ffn2_norm, v_ffn2_w_gate, v_ffn2_w_up, v_ffn2_w_down, v_final_norm):
    given = dict(locals())
    w = {n: given[n] for n in WEIGHTS}
    m = {n: given["m_" + n] for n in WEIGHTS}
    v = {n: given["v_" + n] for n in WEIGHTS}

    def shard_view(a, n):
        return a[0].T if n in TRANSPOSED else a[0]

    def shard_unview(a, n):
        return (a.T if n in TRANSPOSED else a)[None]

    shards = {tag: [jnp.concatenate([shard_view(w[n], n).astype(BF16) for n in grp], axis=0) for grp in arrays]
              for tag, arrays in GATHER_PHASES.items()}
    reducer = _GradReducer()
    loss_part, grad_x, _, small = _device_step(x[0], mem[0], loss_target[0], _WeightGatherer(shards),
                                               {n: w[n] for n in SMALL}, reducer)
    loss = lax.psum(loss_part[0, 0], ("x", "y", "c"))

    place = jnp.stack([2 * lax.axis_index("x") + lax.axis_index("y"), lax.axis_index("c")]).astype(jnp.int32)
    halves = [_chip_sum(k, part, got, place) for k, (part, got) in enumerate(zip(reducer.parts, reducer.landed))]
    reduced = _sibling_join_halves(halves)

    grads, delta, new_m, new_v = {}, {}, {}, {}
    for grp, red in zip(REDUCE_GROUPS, reduced):
        row0 = 0
        for n in grp:
            w_n = shard_view(w[n], n)
            grads[n], delta[n], new_m[n], new_v[n] = (
                shard_unview(o, n) for o in _adamw("adamw_" + n, w_n, red, row0, shard_view(m[n], n), shard_view(v[n], n)))
            row0 += w_n.shape[0]

    g_small = dict(zip(SMALL, _unpack_small(_allreduce_small(_pack_small([small[n] for n in SMALL])), [w[n] for n in SMALL])))
    narrow = [n for n in SMALL if w[n].ndim > 3]
    dense = [n for n in SMALL if n not in narrow]
    for n in narrow:
        two_d = (-1, w[n].shape[-1])
        outs = _adamw("adamw_" + n, w[n].reshape(two_d), g_small[n].reshape(two_d), 0, m[n].reshape(two_d), v[n].reshape(two_d))
        grads[n], delta[n], new_m[n], new_v[n] = (o.reshape(w[n].shape) for o in outs)
    dense_like = [w[n] for n in dense]
    packed = _adamw("adamw_small", _pack_small(dense_like), _pack_small([g_small[n] for n in dense]), 0,
                    _pack_small([m[n] for n in dense]), _pack_small([v[n] for n in dense]))
    for out, store in zip(packed, (grads, delta, new_m, new_v)):
        for n, val in zip(dense, _unpack_small(out, dense_like)):
            store[n] = val

    return (loss, grad_x[None], *[grads[n] for n in WEIGHTS], *[delta[n] for n in WEIGHTS],
            *[new_m[n] for n in WEIGHTS], *[new_v[n] for n in WEIGHTS])
```

```python
import functools
import math

import jax
import jax.numpy as jnp
from jax import lax
from jax.experimental import pallas as pl
from jax.experimental.pallas import tpu as pltpu

F32 = jnp.float32
BF16 = jnp.bfloat16
SDS = jax.ShapeDtypeStruct
BS = pl.BlockSpec
MESH = pl.DeviceIdType.MESH

D_MODEL = 1024
D_FF = 2816
N_SHARD = 4
FF_SH = D_FF // N_SHARD
D_POOL = 512
POOL_WINDOWS = (2, 4, 8, 16)
POOL_GROUP = 128
D_SSM = 256
SSM_GROUPS = 16
SSM_GROUP = 16
SSM_STATE = 64
SSM_CH = SSM_GROUPS * SSM_STATE
N_HEADS = 4
HEAD_DIM = 256
EPS = 1e-6
ADAM_LR, ADAM_B1, ADAM_B2, ADAM_EPS, ADAM_WD, ADAM_STEP = 0.001, 0.9, 0.999, 1e-08, 0.01, 10

VMEM_LIMIT_V7X = 52 * 1024 * 1024
TM = 512

NN = (((1,), (0,)), ((), ()))
NT = (((1,), (1,)), ((), ()))
TN = (((0,), (0,)), ((), ()))


def _params(*sem):
    return pltpu.CompilerParams(dimension_semantics=sem if sem else None, vmem_limit_bytes=VMEM_LIMIT_V7X)


def _dot(a, b, dims=NN):
    return lax.dot_general(a.astype(BF16), b.astype(BF16), dims, preferred_element_type=F32)


def _sigmoid(v):
    return 1.0 / (1.0 + jnp.exp(-v))


def _block_dims(spec):
    return tuple(d for d in spec.block_shape if d is not None)


def _after_operands(after):
    return list(after), [BS(memory_space=pl.ANY)] * len(after)


def _mm(name, pairs, *, grid, out_shape, out_spec, red_axis=None, extras=(), epilogue=None, after=()):
    n_pairs, n_extra = len(pairs), len(extras)
    n_red = grid[red_axis] if red_axis is not None else 1
    dims = [p[4] for p in pairs]

    def body(*refs):
        ab = refs[:2 * n_pairs]
        ex = refs[2 * n_pairs:2 * n_pairs + n_extra]
        o_ref = refs[2 * n_pairs + n_extra + len(after)]

        def partial():
            acc = None
            for p in range(n_pairs):
                t = _dot(ab[2 * p][...], ab[2 * p + 1][...], dims[p])
                acc = t if acc is None else acc + t
            return acc

        def finish(acc):
            res = epilogue(acc, *[e[...] for e in ex]) if epilogue is not None else acc
            o_ref[...] = res.astype(o_ref.dtype)

        if n_red == 1:
            finish(partial())
        else:
            acc_ref = refs[-1]
            k = pl.program_id(red_axis)

            @pl.when(k == 0)
            def _():
                acc_ref[...] = jnp.zeros_like(acc_ref)

            acc_ref[...] += partial()

            @pl.when(k == n_red - 1)
            def _():
                finish(acc_ref[...])

    operands, in_specs = [], []
    for a, a_spec, b, b_spec, _ in pairs:
        operands += [a, b]
        in_specs += [a_spec, b_spec]
    for e, e_spec in extras:
        operands.append(e)
        in_specs.append(e_spec)
    after_ops, after_specs = _after_operands(after)
    operands += after_ops
    in_specs += after_specs
    scratch = [pltpu.VMEM(_block_dims(out_spec), F32)] if n_red > 1 else []
    sem = tuple("arbitrary" if ax == red_axis else "parallel" for ax in range(len(grid)))
    return pl.pallas_call(body, out_shape=out_shape, grid=grid, in_specs=in_specs, out_specs=out_spec,
                          scratch_shapes=scratch, name=name, compiler_params=_params(*sem))(*operands)


def _rmsnorm(name, h, gain, tm, after=()):
    t, d = h.shape
    after_ops, after_specs = _after_operands(after)

    def body(h_ref, g_ref, *rest):
        u_ref = rest[-1]
        hv = h_ref[...]
        r = lax.rsqrt(jnp.mean(hv * hv, axis=-1, keepdims=True) + EPS)
        u_ref[...] = ((hv * r) * g_ref[...]).astype(u_ref.dtype)

    return pl.pallas_call(
        body, out_shape=SDS((t, d), BF16), grid=(t // tm,),
        in_specs=[BS((tm, d), lambda i: (i, 0)), BS((1, d), lambda i: (0, 0))] + after_specs,
        out_specs=BS((tm, d), lambda i: (i, 0)), name=name, compiler_params=_params("parallel"))(h, gain, *after_ops)


def _rmsnorm_bwd(name, h, gain, du, dh_in, tm):
    t, d = h.shape
    has_in = dh_in is not None

    def body(*refs):
        if has_in:
            h_ref, g_ref, du_ref, dhin_ref, dh_ref, dhb_ref, dg_ref = refs
        else:
            h_ref, g_ref, du_ref, dh_ref, dhb_ref, dg_ref = refs
        i = pl.program_id(0)
        hv = h_ref[...]
        r = lax.rsqrt(jnp.mean(hv * hv, axis=-1, keepdims=True) + EPS)
        n = hv * r
        duv = du_ref[...].astype(F32)
        dn = duv * g_ref[...]
        dh = r * (dn - n * jnp.mean(dn * n, axis=-1, keepdims=True))
        if has_in:
            dh = dhin_ref[...] + dh
        dh_ref[...] = dh
        dhb_ref[...] = dh.astype(BF16)

        @pl.when(i == 0)
        def _():
            dg_ref[...] = jnp.zeros_like(dg_ref)

        dg_ref[...] += jnp.sum(duv * n, axis=0, keepdims=True)

    row = BS((tm, d), lambda i: (i, 0))
    vec = BS((1, d), lambda i: (0, 0))
    operands = [h, gain, du] + ([dh_in] if has_in else [])
    in_specs = [row, vec, row] + ([row] if has_in else [])
    return pl.pallas_call(
        body, out_shape=(SDS((t, d), F32), SDS((t, d), BF16), SDS((1, d), F32)), grid=(t // tm,),
        in_specs=in_specs, out_specs=(row, row, vec), name=name, compiler_params=_params("arbitrary"))(*operands)


def _loss_head(h, gain, target, tm):
    t, d = h.shape

    def body(h_ref, g_ref, t_ref, loss_ref, dh_ref, dhb_ref, dg_ref):
        i = pl.program_id(0)
        hv = h_ref[...]
        g = g_ref[...]
        r = lax.rsqrt(jnp.mean(hv * hv, axis=-1, keepdims=True) + EPS)
        n = hv * r
        err = n * g - t_ref[...]
        dy = err * (1.0 / d)
        dn = dy * g
        dh = r * (dn - n * jnp.mean(dn * n, axis=-1, keepdims=True))
        dh_ref[...] = dh
        dhb_ref[...] = dh.astype(BF16)

        @pl.when(i == 0)
        def _():
            dg_ref[...] = jnp.zeros_like(dg_ref)
            loss_ref[...] = jnp.zeros_like(loss_ref)

        dg_ref[...] += jnp.sum(dy * n, axis=0, keepdims=True)
        part = 0.5 * jnp.sum(jnp.mean(err * err, axis=-1, keepdims=True), axis=0, keepdims=True)
        loss_ref[...] += jnp.broadcast_to(part, loss_ref.shape)

    row = BS((tm, d), lambda i: (i, 0))
    vec = BS((1, d), lambda i: (0, 0))
    return pl.pallas_call(
        body, out_shape=(SDS((1, 128), F32), SDS((t, d), F32), SDS((t, d), BF16), SDS((1, d), F32)),
        grid=(t // tm,), in_specs=[row, vec, row],
        out_specs=(BS((1, 128), lambda i: (0, 0)), row, row, vec),
        name="loss_head", compiler_params=_params("arbitrary"))(h, gain, target)


FFN_GATE, FFN_UP, FFN_DOWN = 0, 1, 2


def _ffn_w_spec(block, index):
    return BS((None, FF_SH, D_MODEL), lambda *g: (g[index], block, 0))


def _ffn_up(name, u, w_f, tm, after=()):
    t, d = u.shape
    after_ops, after_specs = _after_operands(after)

    def body(u_ref, wg_ref, wu_ref, *rest):
        g_ref, up_ref, a_ref = rest[len(after_ops):]
        uv = u_ref[...]
        g = _dot(uv, wg_ref[...], NT)
        up = _dot(uv, wu_ref[...], NT)
        g_ref[...] = g.astype(BF16)
        up_ref[...] = up.astype(BF16)
        a_ref[...] = (g * _sigmoid(g) * up).astype(BF16)

    hid = BS((None, tm, FF_SH), lambda s, i: (s, i, 0))
    shape = SDS((N_SHARD, t, FF_SH), BF16)
    return pl.pallas_call(
        body, out_shape=(shape, shape, shape), grid=(N_SHARD, t // tm),
        in_specs=[BS((tm, d), lambda s, i: (i, 0)), _ffn_w_spec(w_f["gate"][1], 0), _ffn_w_spec(w_f["up"][1], 0)] + after_specs,
        out_specs=(hid, hid, hid), name=name,
        compiler_params=_params("parallel", "parallel"))(u, w_f["gate"][0], w_f["up"][0], *after_ops)


def _ffn_down(name, a, w_f, resid, tm, after=()):
    t, d = resid.shape
    return _mm(name, [(a, BS((None, tm, FF_SH), lambda i, s: (s, i, 0)), w_f["down"][0], _ffn_w_spec(w_f["down"][1], 1), NN)],
               grid=(t // tm, N_SHARD), red_axis=1, out_shape=SDS((t, d), F32),
               out_spec=BS((tm, d), lambda i, s: (i, 0)),
               extras=[(resid, BS((tm, d), lambda i, s: (i, 0)))],
               epilogue=lambda acc, res: res + 0.5 * acc, after=after)


def _ffn_bwd_act(name, dh_b, w_f, g, up, tm, after=()):
    t, d = dh_b.shape
    after_ops, after_specs = _after_operands(after)

    def body(dh_ref, wd_ref, g_ref, up_ref, *rest):
        dg_ref, dup_ref = rest[len(after_ops):]
        da = 0.5 * _dot(dh_ref[...], wd_ref[...], NT)
        gv = g_ref[...].astype(F32)
        uv = up_ref[...].astype(F32)
        sg = _sigmoid(gv)
        silu = gv * sg
        dg_ref[...] = (da * uv * (sg + silu * (1.0 - sg))).astype(BF16)
        dup_ref[...] = (da * silu).astype(BF16)

    hid = BS((None, tm, FF_SH), lambda s, i: (s, i, 0))
    shape = SDS((N_SHARD, t, FF_SH), BF16)
    return pl.pallas_call(
        body, out_shape=(shape, shape), grid=(N_SHARD, t // tm),
        in_specs=[BS((tm, d), lambda s, i: (i, 0)), _ffn_w_spec(w_f["down"][1], 0), hid, hid] + after_specs,
        out_specs=(hid, hid), name=name,
        compiler_params=_params("parallel", "parallel"))(dh_b, w_f["down"][0], g, up, *after_ops)


def _ffn_dw(name, u, dg, dup, a, dh_b, tm):
    t, d = u.shape
    n_t = t // tm

    def body(u_ref, dg_ref, dup_ref, a_ref, dh_ref, o_ref, acc):
        i = pl.program_id(1)

        @pl.when(i == 0)
        def _():
            acc[...] = jnp.zeros_like(acc)

        uv = u_ref[...]
        acc[FFN_GATE] += _dot(dg_ref[...], uv, TN)
        acc[FFN_UP] += _dot(dup_ref[...], uv, TN)
        acc[FFN_DOWN] += _dot(a_ref[...], dh_ref[...], TN)

        @pl.when(i == n_t - 1)
        def _():
            o_ref[FFN_GATE] = acc[FFN_GATE].astype(BF16)
            o_ref[FFN_UP] = acc[FFN_UP].astype(BF16)
            o_ref[FFN_DOWN] = (0.5 * acc[FFN_DOWN]).astype(BF16)

    hid = BS((None, tm, FF_SH), lambda s, i: (s, i, 0))
    row = BS((tm, d), lambda s, i: (i, 0))
    return pl.pallas_call(
        body, out_shape=SDS((N_SHARD, 3, FF_SH, d), BF16), grid=(N_SHARD, n_t),
        in_specs=[row, hid, hid, hid, row], out_specs=BS((None, 3, FF_SH, d), lambda s, i: (s, 0, 0, 0)),
        scratch_shapes=[pltpu.VMEM((3, FF_SH, d), F32)],
        name=name, compiler_params=_params("parallel", "arbitrary"))(u, dg, dup, a, dh_b)


def _ffn_dx(name, dg, dup, w_f, tm, after=()):
    t = dg.shape[1]
    hid = BS((None, tm, FF_SH), lambda i, s: (s, i, 0))
    return _mm(name, [(dg, hid, w_f["gate"][0], _ffn_w_spec(w_f["gate"][1], 1), NN),
                      (dup, hid, w_f["up"][0], _ffn_w_spec(w_f["up"][1], 1), NN)],
               grid=(t // tm, N_SHARD), red_axis=1, out_shape=SDS((t, D_MODEL), F32),
               out_spec=BS((tm, D_MODEL), lambda i, s: (i, 0)), after=after)


def _plain_mm(name, a, b, dims, out_dtype, tm, resid=None, after=()):
    t = a.shape[0]
    n = b.shape[1] if dims == NN else b.shape[0]
    extras = [(resid, BS((tm, n), lambda i: (i, 0)))] if resid is not None else []
    epi = (lambda acc, res: res + acc) if resid is not None else None
    return _mm(name, [(a, BS((tm, a.shape[1]), lambda i: (i, 0)), b, BS(b.shape, lambda i: (0, 0)), dims)],
               grid=(t // tm,), out_shape=SDS((t, n), out_dtype), out_spec=BS((tm, n), lambda i: (i, 0)),
               extras=extras, epilogue=epi, after=after)


def _dw_mm(name, a, b, tm, out_dtype=BF16):
    t, k = a.shape
    n = b.shape[1]
    return _mm(name, [(a, BS((tm, k), lambda i: (i, 0)), b, BS((tm, n), lambda i: (i, 0)), TN)],
               grid=(t // tm,), red_axis=0, out_shape=SDS((k, n), out_dtype), out_spec=BS((k, n), lambda i: (0, 0)))


POOL_CHUNK = 256
POOL_HALO = 8


def _window_sum(v, width, lead):
    n = v.shape[0]
    s = v
    k = 1
    while k < width:
        s = s + pltpu.roll(s, n - k, 0)
        k *= 2
    return pltpu.roll(s, lead, 0) if lead else s


def _pool_count(base, left, right, t, shape):
    pos = base + lax.broadcasted_iota(jnp.int32, shape, 0)
    lo = jnp.maximum(pos - left, 0)
    hi = jnp.minimum(pos + right + 1, t)
    return (hi - lo).astype(F32)


def _pool_fwd(proj, pool_w, pool_scale):
    t = proj.shape[0]
    c, h = POOL_CHUNK, POOL_HALO
    n_chunks = t // c

    def body(proj_hbm, pw_ref, sc_ref, pooled_ref, mixed_ref, ms_ref, pad_ref, sem):
        cp = pltpu.make_async_copy(proj_hbm.at[:, pl.ds(0, D_POOL)], pad_ref.at[pl.ds(h, t), :], sem)
        cp.start()
        pad_ref[pl.ds(0, h), :] = jnp.zeros((h, D_POOL), F32)
        pad_ref[pl.ds(t + h, h), :] = jnp.zeros((h, D_POOL), F32)
        cp.wait()
        for g, width in enumerate(POOL_WINDOWS):
            left = width // 2
            right = width - 1 - left
            cols = slice(g * POOL_GROUP, (g + 1) * POOL_GROUP)
            wmat = pw_ref[g].astype(BF16)
            scale = sc_ref[:, cols]

            def chunk(ci, carry, left=left, right=right, width=width, cols=cols, wmat=wmat, scale=scale):
                base = pl.multiple_of(ci * c, c)
                v = pad_ref[pl.ds(base, c + 2 * h), cols]
                win = _window_sum(v, width, left)[h:h + c]
                cnt = _pool_count(base, left, right, t, (c, POOL_GROUP))
                pooled = (win / cnt - v[h:h + c]).astype(BF16)
                mixed = _dot(pooled, wmat)
                pooled_ref[pl.ds(base, c), cols] = pooled
                mixed_ref[pl.ds(base, c), cols] = mixed.astype(BF16)
                ms_ref[pl.ds(base, c), cols] = (mixed * scale).astype(BF16)
                return carry

            lax.fori_loop(0, n_chunks, chunk, 0)

    vm = BS(memory_space=pltpu.VMEM)
    shape = SDS((t, D_POOL), BF16)
    return pl.pallas_call(
        body, out_shape=(shape, shape, shape),
        in_specs=[BS(memory_space=pl.ANY), vm, vm], out_specs=(vm, vm, vm),
        scratch_shapes=[pltpu.VMEM((t + 2 * h, D_POOL), F32), pltpu.SemaphoreType.DMA],
        name="pool_fwd", compiler_params=_params())(proj, pool_w, pool_scale)


def _pool_bwd(d_ms, mixed, pooled, pool_w, pool_scale):
    t = d_ms.shape[0]
    c, h = POOL_CHUNK, POOL_HALO
    n_chunks = t // c

    def body(dms_ref, mixed_ref, pooled_ref, pw_ref, sc_ref, dp_ref, dsc_ref, dpw_ref, pad_ref):
        pad_ref[pl.ds(0, h), :] = jnp.zeros((h, D_POOL), F32)
        pad_ref[pl.ds(t + h, h), :] = jnp.zeros((h, D_POOL), F32)
        for g, width in enumerate(POOL_WINDOWS):
            left = width // 2
            right = width - 1 - left
            cols = slice(g * POOL_GROUP, (g + 1) * POOL_GROUP)
            wmat = pw_ref[g].astype(BF16)
            scale = sc_ref[:, cols]

            def first(ci, carry, left=left, right=right, cols=cols, wmat=wmat, scale=scale):
                dsc, dpw = carry
                base = pl.multiple_of(ci * c, c)
                dms = dms_ref[pl.ds(base, c), cols].astype(F32)
                dsc = dsc + jnp.sum(dms * mixed_ref[pl.ds(base, c), cols].astype(F32), axis=0, keepdims=True)
                dmix = (dms * scale).astype(BF16)
                dpw = dpw + _dot(pooled_ref[pl.ds(base, c), cols], dmix, TN)
                dpooled = _dot(dmix, wmat, NT)
                cnt = _pool_count(base, left, right, t, (c, POOL_GROUP))
                pad_ref[pl.ds(base + h, c), cols] = dpooled / cnt
                return dsc, dpw

            dsc, dpw = lax.fori_loop(0, n_chunks, first,
                                     (jnp.zeros((1, POOL_GROUP), F32), jnp.zeros((POOL_GROUP, POOL_GROUP), F32)))
            dsc_ref[:, cols] = dsc
            dpw_ref[g] = dpw

            def second(ci, carry, left=left, right=right, width=width, cols=cols):
                base = pl.multiple_of(ci * c, c)
                v = pad_ref[pl.ds(base, c + 2 * h), cols]
                win = _window_sum(v, width, right)[h:h + c]
                cnt = _pool_count(base, left, right, t, (c, POOL_GROUP))
                dp_ref[pl.ds(base, c), cols] = (win - v[h:h + c] * cnt).astype(BF16)
                return carry

            lax.fori_loop(0, n_chunks, second, 0)

    vm = BS(memory_space=pltpu.VMEM)
    return pl.pallas_call(
        body, out_shape=(SDS((t, D_POOL), BF16), SDS((1, D_POOL), F32), SDS((4, POOL_GROUP, POOL_GROUP), F32)),
        in_specs=[vm] * 5, out_specs=(vm, vm, vm),
        scratch_shapes=[pltpu.VMEM((t + 2 * h, D_POOL), F32)],
        name="pool_bwd", compiler_params=_params())(d_ms, mixed, pooled, pool_w, pool_scale)


def _ssm_disc(ar, ai, ldt, after=()):
    after_ops, after_specs = _after_operands(after)

    def body(ar_ref, ai_ref, ldt_ref, *rest):
        abr_ref, abi_ref, qr_ref, qi_ref = rest[len(after_ops):]
        a_r, a_i = ar_ref[...], ai_ref[...]
        dt = jnp.exp(ldt_ref[...])
        mag = jnp.exp(dt * a_r)
        ang = dt * a_i
        abr = mag * jnp.cos(ang)
        abi = mag * jnp.sin(ang)
        den = a_r * a_r + a_i * a_i
        nr = abr - 1.0
        abr_ref[...] = abr
        abi_ref[...] = abi
        qr_ref[...] = (nr * a_r + abi * a_i) / den
        qi_ref[...] = (abi * a_r - nr * a_i) / den

    vm = BS(memory_space=pltpu.VMEM)
    shape = SDS(ar.shape, F32)
    return pl.pallas_call(body, out_shape=(shape,) * 4, in_specs=[vm] * 3 + after_specs, out_specs=(vm,) * 4,
                          name="ssm_disc", compiler_params=_params())(ar, ai, ldt, *after_ops)


def _ssm_disc_bwd(ar, ai, ldt, d_abr, d_abi, d_qr, d_qi):
    def body(ar_ref, ai_ref, ldt_ref, gabr_ref, gabi_ref, gqr_ref, gqi_ref, dar_ref, dai_ref, dldt_ref):
        a_r, a_i = ar_ref[...], ai_ref[...]
        dt = jnp.exp(ldt_ref[...])
        mag = jnp.exp(dt * a_r)
        ang = dt * a_i
        cs, sn = jnp.cos(ang), jnp.sin(ang)
        abr, abi = mag * cs, mag * sn
        den = a_r * a_r + a_i * a_i
        nr = abr - 1.0
        qr = (nr * a_r + abi * a_i) / den
        qi = (abi * a_r - nr * a_i) / den
        gqr, gqi = gqr_ref[...], gqi_ref[...]
        g_nr_num = gqr / den
        g_ni_num = gqi / den
        g_den = -(gqr * qr + gqi * qi) / den
        g_nr = g_nr_num * a_r - g_ni_num * a_i
        g_abi = g_nr_num * a_i + g_ni_num * a_r
        d_ar = g_nr_num * nr + g_ni_num * abi + 2.0 * a_r * g_den
        d_ai = g_nr_num * abi - g_ni_num * nr + 2.0 * a_i * g_den
        g_abr = gabr_ref[...] + g_nr
        g_abi = gabi_ref[...] + g_abi
        g_mag = g_abr * cs + g_abi * sn
        g_ang = mag * (g_abi * cs - g_abr * sn)
        g_e = g_mag * mag
        d_ar = d_ar + g_e * dt
        d_ai = d_ai + g_ang * dt
        g_dt = g_e * a_r + g_ang * a_i
        dar_ref[...] = d_ar
        dai_ref[...] = d_ai
        dldt_ref[...] = jnp.sum(g_dt * dt, axis=1, keepdims=True)

    vm = BS(memory_space=pltpu.VMEM)
    return pl.pallas_call(body, out_shape=(SDS(ar.shape, F32), SDS(ar.shape, F32), SDS(ldt.shape, F32)),
                          in_specs=[vm] * 7, out_specs=(vm,) * 3, name="ssm_disc_bwd",
                          compiler_params=_params())(ar, ai, ldt, d_abr, d_abi, d_qr, d_qi)


def _ssm_bbar(qr, qi, br, bi):
    def body(qr_ref, qi_ref, br_ref, bi_ref, bbr_ref, bbi_ref):
        q_r, q_i, b_r, b_i = qr_ref[...], qi_ref[...], br_ref[...], bi_ref[...]
        bbr_ref[...] = q_r * b_r - q_i * b_i
        bbi_ref[...] = q_r * b_i + q_i * b_r

    vm = BS(memory_space=pltpu.VMEM)
    shape = SDS(br.shape, F32)
    return pl.pallas_call(body, out_shape=(shape, shape), in_specs=[vm] * 4, out_specs=(vm, vm),
                          name="ssm_bbar", compiler_params=_params())(qr, qi, br, bi)


def _ssm_bbar_bwd(qr, qi, br, bi, g_bbr, g_bbi):
    def body(qr_ref, qi_ref, br_ref, bi_ref, gr_ref, gi_ref, dqr_ref, dqi_ref, dbr_ref, dbi_ref):
        q_r, q_i, b_r, b_i = qr_ref[...], qi_ref[...], br_ref[...], bi_ref[...]
        g_r, g_i = gr_ref[...], gi_ref[...]
        dqr_ref[...] = jnp.sum(g_r * b_r + g_i * b_i, axis=1, keepdims=True)
        dqi_ref[...] = jnp.sum(g_i * b_r - g_r * b_i, axis=1, keepdims=True)
        dbr_ref[...] = g_r * q_r + g_i * q_i
        dbi_ref[...] = g_i * q_r - g_r * q_i

    vm = BS(memory_space=pltpu.VMEM)
    return pl.pallas_call(
        body, out_shape=(SDS(qr.shape, F32), SDS(qr.shape, F32), SDS(br.shape, F32), SDS(br.shape, F32)),
        in_specs=[vm] * 6, out_specs=(vm,) * 4, name="ssm_bbar_bwd",
        compiler_params=_params())(qr, qi, br, bi, g_bbr, g_bbi)


SCAN_ROWS = 256


def _ssm_scan(name, inp, w1, a_r, a_i, w2, reverse):
    t = inp.shape[0]
    rows = SCAN_ROWS
    n = t // rows
    n_groups = rows // 8
    ch = SSM_CH
    at = (lambda i: (n - 1 - i, 0)) if reverse else (lambda i: (i, 0))

    def body(in_ref, w1_ref, ar_ref, ai_ref, w2_ref, st_ref, out_ref, cr_ref, ci_ref, k_ref):
        i = pl.program_id(0)

        @pl.when(i == 0)
        def _():
            ar8 = jnp.broadcast_to(ar_ref[...], (8, ch))
            ai8 = jnp.broadcast_to(ai_ref[...], (8, ch))
            row = lax.broadcasted_iota(jnp.int32, (8, ch), 0)
            rank = (7 - row) if reverse else row
            powers = [(ar8, ai8)]
            for _ in range(7):
                p_r, p_i = powers[-1]
                powers.append((p_r * ar8 - p_i * ai8, p_r * ai8 + p_i * ar8))
            zero = jnp.zeros((8, ch), F32)
            for slot, k in enumerate((1, 2, 4)):
                k_ref[2 * slot] = jnp.where(rank >= k, powers[k - 1][0], zero)
                k_ref[2 * slot + 1] = jnp.where(rank >= k, powers[k - 1][1], zero)
            carry_r, carry_i = zero, zero
            for j in range(8):
                carry_r = jnp.where(rank == j, powers[j][0], carry_r)
                carry_i = jnp.where(rank == j, powers[j][1], carry_i)
            k_ref[6] = carry_r
            k_ref[7] = carry_i
            cr_ref[...] = zero
            ci_ref[...] = zero

        st_ref[...] = _dot(in_ref[...], w1_ref[...])

        def group(gi, carry):
            c_r, c_i = carry
            g = (n_groups - 1 - gi) if reverse else gi
            r0 = pl.multiple_of(g * 8, 8)
            x_r = st_ref[pl.ds(r0, 8), 0:ch]
            x_i = st_ref[pl.ds(r0, 8), ch:2 * ch]
            for slot, k in enumerate((1, 2, 4)):
                shift = (8 - k) if reverse else k
                s_r = pltpu.roll(x_r, shift, 0)
                s_i = pltpu.roll(x_i, shift, 0)
                m_r, m_i = k_ref[2 * slot], k_ref[2 * slot + 1]
                x_r, x_i = x_r + m_r * s_r - m_i * s_i, x_i + m_r * s_i + m_i * s_r
            p_r, p_i = k_ref[6], k_ref[7]
            x_r, x_i = x_r + p_r * c_r - p_i * c_i, x_i + p_r * c_i + p_i * c_r
            st_ref[pl.ds(r0, 8), 0:ch] = x_r
            st_ref[pl.ds(r0, 8), ch:2 * ch] = x_i
            last = 0 if reverse else 7
            return (jnp.broadcast_to(x_r[last:last + 1, :], (8, ch)), jnp.broadcast_to(x_i[last:last + 1, :], (8, ch)))

        c_r, c_i = lax.fori_loop(0, n_groups, group, (cr_ref[...], ci_ref[...]))
        cr_ref[...] = c_r
        ci_ref[...] = c_i
        out_ref[...] = _dot(st_ref[...], w2_ref[...])

    return pl.pallas_call(
        body, out_shape=(SDS((t, 2 * ch), F32), SDS((t, D_SSM), F32)), grid=(n,),
        in_specs=[BS((rows, D_SSM), at), BS((D_SSM, 2 * ch), lambda i: (0, 0)), BS((1, ch), lambda i: (0, 0)),
                  BS((1, ch), lambda i: (0, 0)), BS((2 * ch, D_SSM), lambda i: (0, 0))],
        out_specs=(BS((rows, 2 * ch), at), BS((rows, D_SSM), at)),
        scratch_shapes=[pltpu.VMEM((8, ch), F32), pltpu.VMEM((8, ch), F32), pltpu.VMEM((8, 8, ch), F32)],
        name=name, compiler_params=_params("arbitrary"))(inp, w1, a_r, a_i, w2)


DA_ROWS = 512


def _ssm_da(name, lam, states, reverse):
    t = lam.shape[0]
    rows = DA_ROWS
    n = t // rows
    nb = rows // 8
    ch = SSM_CH
    if reverse:
        halo_at = lambda i: (jnp.minimum((i + 1) * nb, t // 8 - 1), 0)
    else:
        halo_at = lambda i: (jnp.maximum(i * nb - 1, 0), 0)

    def body(lam_ref, x_ref, halo_ref, dr_ref, di_ref):
        i = pl.program_id(0)

        @pl.when(i == 0)
        def _():
            dr_ref[...] = jnp.zeros_like(dr_ref)
            di_ref[...] = jnp.zeros_like(di_ref)

        row = lax.broadcasted_iota(jnp.int32, (rows, ch), 0)
        if reverse:
            edge, shift, h_row, live = rows - 1, rows - 1, 0, i < n - 1
        else:
            edge, shift, h_row, live = 0, 1, 7, i > 0

        def neighbour(lo):
            halo = jnp.where(live, halo_ref[h_row:h_row + 1, lo:lo + ch], 0.0)
            return jnp.where(row == edge, jnp.broadcast_to(halo, (rows, ch)), pltpu.roll(x_ref[:, lo:lo + ch], shift, 0))

        xp_r, xp_i = neighbour(0), neighbour(ch)
        l_r, l_i = lam_ref[:, 0:ch], lam_ref[:, ch:2 * ch]
        dr_ref[...] += jnp.sum(l_r * xp_r + l_i * xp_i, axis=0, keepdims=True)
        di_ref[...] += jnp.sum(l_i * xp_r - l_r * xp_i, axis=0, keepdims=True)

    blk = BS((rows, 2 * ch), lambda i: (i, 0))
    vec = BS((1, ch), lambda i: (0, 0))
    return pl.pallas_call(
        body, out_shape=(SDS((1, ch), F32), SDS((1, ch), F32)), grid=(n,),
        in_specs=[blk, blk, BS((8, 2 * ch), halo_at)], out_specs=(vec, vec),
        name=name, compiler_params=_params("arbitrary"))(lam, states, states)


GELU_C = math.sqrt(2.0 / math.pi)
GELU_K = 0.044715


def _ssm_combine(proj, y_fwd, y_bwd, d_skip, tm, after=()):
    t = proj.shape[0]
    after_ops, after_specs = _after_operands(after)

    def body(s_ref, yf_ref, yb_ref, d_ref, *rest):
        yt_ref, g_ref = rest[len(after_ops):]
        y = s_ref[...] * d_ref[...] + yf_ref[...] + yb_ref[...]
        yt_ref[...] = y
        th = jnp.tanh(GELU_C * (y + GELU_K * y * y * y))
        g_ref[...] = (0.5 * y * (1.0 + th)).astype(BF16)

    blk = BS((tm, D_SSM), lambda i: (i, 0))
    return pl.pallas_call(
        body, out_shape=(SDS((t, D_SSM), F32), SDS((t, D_SSM), BF16)), grid=(t // tm,),
        in_specs=[BS((tm, D_SSM), lambda i: (i, D_POOL // D_SSM)), blk, blk, BS((1, D_SSM), lambda i: (0, 0))] + after_specs,
        out_specs=(blk, blk), name="ssm_combine",
        compiler_params=_params("parallel"))(proj, y_fwd, y_bwd, d_skip, *after_ops)


def _ssm_ds(proj, d_yt, du_fwd, du_bwd, d_skip, tm):
    t = proj.shape[0]

    def body(s_ref, dy_ref, duf_ref, dub_ref, d_ref, ds_ref, dd_ref):
        i = pl.program_id(0)
        dy = dy_ref[...]
        ds_ref[...] = (dy * d_ref[...] + duf_ref[...] + dub_ref[...]).astype(BF16)

        @pl.when(i == 0)
        def _():
            dd_ref[...] = jnp.zeros_like(dd_ref)

        dd_ref[...] += jnp.sum(dy * s_ref[...], axis=0, keepdims=True)

    blk = BS((tm, D_SSM), lambda i: (i, 0))
    vec = BS((1, D_SSM), lambda i: (0, 0))
    return pl.pallas_call(
        body, out_shape=(SDS((t, D_SSM), BF16), SDS((1, D_SSM), F32)), grid=(t // tm,),
        in_specs=[BS((tm, D_SSM), lambda i: (i, D_POOL // D_SSM)), blk, blk, blk, vec],
        out_specs=(blk, vec), name="ssm_ds", compiler_params=_params("arbitrary"))(proj, d_yt, du_fwd, du_bwd, d_skip)


GP_BLOCK = (D_POOL + D_SSM) // 256
GS_BLOCK = GP_BLOCK + D_MODEL // 256


def _merge_specs(tm):
    return [BS((tm, D_POOL), lambda s, i: (i, 0)), BS((tm, D_SSM), lambda s, i: (i, 0)),
            BS((None, D_POOL, 256), lambda s, i: (s, 0, 0)), BS((None, D_SSM, 256), lambda s, i: (s, 2, 0)),
            BS((None, D_SSM, 256), lambda s, i: (s, 3, 0)),
            BS((tm, 256), lambda s, i: (i, GP_BLOCK + s)), BS((tm, 256), lambda s, i: (i, GS_BLOCK + s))]


def _mixer_merge(ms, yssm, w_e, proj, tm):
    t = ms.shape[0]

    def body(ms_ref, y_ref, wpp_ref, wgv_ref, wgg_ref, gp_ref, gs_ref, o_ref):
        zp = _dot(ms_ref[...], wpp_ref[...])
        yv = y_ref[...]
        zv = _dot(yv, wgv_ref[...])
        zg = _dot(yv, wgg_ref[...])
        o_ref[...] = (_sigmoid(gp_ref[...]) * zp + _sigmoid(gs_ref[...]) * zv * _sigmoid(zg)).astype(BF16)

    col = BS((tm, 256), lambda s, i: (i, s))
    return pl.pallas_call(
        body, out_shape=SDS((t, D_MODEL), BF16), grid=(N_SHARD, t // tm), in_specs=_merge_specs(tm), out_specs=col,
        name="mixer_merge", compiler_params=_params("parallel", "parallel"))(ms, yssm, w_e, w_e, w_e, proj, proj)


def _mixer_merge_bwd(ms, yssm, w_e, proj, dmerged, tm):
    t = ms.shape[0]

    def body(ms_ref, y_ref, wpp_ref, wgv_ref, wgg_ref, gp_ref, gs_ref, dm_ref,
             dgp_ref, dgs_ref, dzp_ref, dzv_ref, dzg_ref):
        zp = _dot(ms_ref[...], wpp_ref[...])
        yv = y_ref[...]
        zv = _dot(yv, wgv_ref[...])
        zg = _dot(yv, wgg_ref[...])
        dm = dm_ref[...].astype(F32)
        sp, ss, sg = _sigmoid(gp_ref[...]), _sigmoid(gs_ref[...]), _sigmoid(zg)
        dgp_ref[...] = (dm * zp * sp * (1.0 - sp)).astype(BF16)
        dgs_ref[...] = (dm * zv * sg * ss * (1.0 - ss)).astype(BF16)
        dzp_ref[...] = (dm * sp).astype(BF16)
        dz = dm * ss
        dzv_ref[...] = (dz * sg).astype(BF16)
        dzg_ref[...] = (dz * zv * sg * (1.0 - sg)).astype(BF16)

    col = BS((tm, 256), lambda s, i: (i, s))
    shape = SDS((t, D_MODEL), BF16)
    return pl.pallas_call(
        body, out_shape=(shape,) * 5, grid=(N_SHARD, t // tm), in_specs=_merge_specs(tm) + [col],
        out_specs=(col,) * 5, name="mixer_merge_bwd",
        compiler_params=_params("parallel", "parallel"))(ms, yssm, w_e, w_e, w_e, proj, proj, dmerged)


def _mixer_dw(ms, yssm, dzp, dzv, dzg, tm):
    t = ms.shape[0]
    n_t = t // tm

    def body(ms_ref, y_ref, dzp_ref, dzv_ref, dzg_ref, o_ref, acc):
        i = pl.program_id(1)

        @pl.when(i == 0)
        def _():
            acc[...] = jnp.zeros_like(acc)

        yv = y_ref[...]
        acc[0:D_POOL, :] += _dot(ms_ref[...], dzp_ref[...], TN)
        acc[D_POOL:D_POOL + D_SSM, :] += _dot(yv, dzv_ref[...], TN)
        acc[D_POOL + D_SSM:, :] += _dot(yv, dzg_ref[...], TN)

        @pl.when(i == n_t - 1)
        def _():
            o_ref[...] = acc[...].astype(BF16)

    col = BS((tm, 256), lambda s, i: (i, s))
    return pl.pallas_call(
        body, out_shape=SDS((N_SHARD, 1024, 256), BF16), grid=(N_SHARD, n_t),
        in_specs=[BS((tm, D_POOL), lambda s, i: (i, 0)), BS((tm, D_SSM), lambda s, i: (i, 0)), col, col, col],
        out_specs=BS((None, 1024, 256), lambda s, i: (s, 0, 0)), scratch_shapes=[pltpu.VMEM((1024, 256), F32)],
        name="mixer_dw", compiler_params=_params("parallel", "arbitrary"))(ms, yssm, dzp, dzv, dzg)


def _mixer_dx(dzp, dzv, dzg, w_e, y_total, tm):
    t = dzp.shape[0]

    def body(dzp_ref, dzv_ref, dzg_ref, wpp_ref, wgv_ref, wgg_ref, yt_ref, dms_ref, dy_ref, acc_ms, acc_y):
        s = pl.program_id(1)

        @pl.when(s == 0)
        def _():
            acc_ms[...] = jnp.zeros_like(acc_ms)
            acc_y[...] = jnp.zeros_like(acc_y)

        acc_ms[...] += _dot(dzp_ref[...], wpp_ref[...], NT)
        acc_y[...] += _dot(dzv_ref[...], wgv_ref[...], NT) + _dot(dzg_ref[...], wgg_ref[...], NT)

        @pl.when(s == N_SHARD - 1)
        def _():
            dms_ref[...] = acc_ms[...].astype(BF16)
            y = yt_ref[...]
            inner = GELU_C * (y + GELU_K * y * y * y)
            th = jnp.tanh(inner)
            dgelu = 0.5 * (1.0 + th) + 0.5 * y * (1.0 - th * th) * GELU_C * (1.0 + 3.0 * GELU_K * y * y)
            dy_ref[...] = acc_y[...] * dgelu

    col = BS((tm, 256), lambda i, s: (i, s))
    return pl.pallas_call(
        body, out_shape=(SDS((t, D_POOL), BF16), SDS((t, D_SSM), F32)), grid=(t // tm, N_SHARD),
        in_specs=[col, col, col, BS((None, D_POOL, 256), lambda i, s: (s, 0, 0)),
                  BS((None, D_SSM, 256), lambda i, s: (s, 2, 0)), BS((None, D_SSM, 256), lambda i, s: (s, 3, 0)),
                  BS((tm, D_SSM), lambda i, s: (i, 0))],
        out_specs=(BS((tm, D_POOL), lambda i, s: (i, 0)), BS((tm, D_SSM), lambda i, s: (i, 0))),
        scratch_shapes=[pltpu.VMEM((tm, D_POOL), F32), pltpu.VMEM((tm, D_SSM), F32)],
        name="mixer_dx", compiler_params=_params("parallel", "arbitrary"))(dzp, dzv, dzg, w_e, w_e, w_e, y_total)


def _attn_probs(q_h, k_h):
    s = _dot(q_h, k_h, NT) * (1.0 / math.sqrt(HEAD_DIM))
    e = jnp.exp(s - jnp.max(s, axis=-1, keepdims=True))
    return e / jnp.sum(e, axis=-1, keepdims=True)


def _attn_fwd(q, kv, tm):
    t = q.shape[0]
    m = kv.shape[0]

    def body(q_ref, kv_ref, o_ref):
        for hd in range(N_HEADS):
            lo = hd * HEAD_DIM
            p = _attn_probs(q_ref[:, lo:lo + HEAD_DIM], kv_ref[:, lo:lo + HEAD_DIM])
            o_ref[:, lo:lo + HEAD_DIM] = _dot(p, kv_ref[:, D_MODEL + lo:D_MODEL + lo + HEAD_DIM]).astype(BF16)

    return pl.pallas_call(
        body, out_shape=SDS((t, D_MODEL), BF16), grid=(t // tm,),
        in_specs=[BS((tm, D_MODEL), lambda i: (i, 0)), BS((m, 2 * D_MODEL), lambda i: (0, 0))],
        out_specs=BS((tm, D_MODEL), lambda i: (i, 0)), name="attn_fwd", compiler_params=_params("parallel"))(q, kv)


def _attn_bwd(q, kv, d_o, tm):
    t = q.shape[0]
    m = kv.shape[0]

    def body(q_ref, kv_ref, do_ref, dq_ref, dkv_ref):
        i = pl.program_id(0)

        @pl.when(i == 0)
        def _():
            dkv_ref[...] = jnp.zeros_like(dkv_ref)

        for hd in range(N_HEADS):
            lo = hd * HEAD_DIM
            q_h = q_ref[:, lo:lo + HEAD_DIM]
            k_h = kv_ref[:, lo:lo + HEAD_DIM]
            v_h = kv_ref[:, D_MODEL + lo:D_MODEL + lo + HEAD_DIM]
            do_h = do_ref[:, lo:lo + HEAD_DIM]
            p = _attn_probs(q_h, k_h)
            dkv_ref[:, D_MODEL + lo:D_MODEL + lo + HEAD_DIM] += _dot(p, do_h, TN)
            dp = _dot(do_h, v_h, NT)
            ds = p * (dp - jnp.sum(dp * p, axis=-1, keepdims=True)) * (1.0 / math.sqrt(HEAD_DIM))
            dq_ref[:, lo:lo + HEAD_DIM] = _dot(ds, k_h).astype(BF16)
            dkv_ref[:, lo:lo + HEAD_DIM] += _dot(ds, q_h, TN)

    row = BS((tm, D_MODEL), lambda i: (i, 0))
    full = BS((m, 2 * D_MODEL), lambda i: (0, 0))
    return pl.pallas_call(
        body, out_shape=(SDS((t, D_MODEL), BF16), SDS((m, 2 * D_MODEL), F32)), grid=(t // tm,),
        in_specs=[row, full, row], out_specs=(row, full), name="attn_bwd",
        compiler_params=_params("arbitrary"))(q, kv, d_o)


TRANSPOSED = ("ffn1_w_gate", "ffn1_w_up", "ffn2_w_gate", "ffn2_w_up", "w_in")
GATHER_PHASES = {"f1a": (("ffn1_w_gate", "ffn1_w_up"),),
                 "f1b": (("ffn1_w_down",),),
                 "win": (("w_in",),),
                 "mix": (("w_mix_out", "w_q", "w_xo"), ("w_kv",), ("w_pool_proj", "w_glu_val", "w_glu_gate")),
                 "f2": (("ffn2_w_gate", "ffn2_w_up", "ffn2_w_down"),)}
REDUCE_GROUPS = (("ffn2_w_gate", "ffn2_w_up", "ffn2_w_down"), ("w_xo",), ("w_q",), ("w_kv",), ("w_mix_out",),
                 ("w_pool_proj", "w_glu_val", "w_glu_gate"), ("w_in",), ("ffn1_w_gate", "ffn1_w_up", "ffn1_w_down"))
SMALL = ("ffn1_norm", "mix_norm", "pool_w", "pool_scale", "ssm_a_re", "ssm_a_im", "ssm_log_dt", "ssm_b_re",
         "ssm_b_im", "ssm_c_re", "ssm_c_im", "ssm_d", "xattn_norm", "mem_norm", "ffn2_norm", "final_norm")
WEIGHTS = ("ffn1_norm", "ffn1_w_gate", "ffn1_w_up", "ffn1_w_down", "mix_norm", "w_in", "pool_w", "pool_scale",
           "w_pool_proj", "ssm_a_re", "ssm_a_im", "ssm_log_dt", "ssm_b_re", "ssm_b_im", "ssm_c_re", "ssm_c_im",
           "ssm_d", "w_glu_val", "w_glu_gate", "w_mix_out", "xattn_norm", "mem_norm", "w_q", "w_kv", "w_xo",
           "ffn2_norm", "ffn2_w_gate", "ffn2_w_up", "ffn2_w_down", "final_norm")


def _block_diag_in(bb):
    eye = jnp.eye(SSM_GROUPS, dtype=bb.dtype)
    return jnp.einsum("dgph,gk->dghkp", bb, eye).reshape(2, D_SSM, SSM_CH)


def _block_diag_out(cc):
    eye = jnp.eye(SSM_GROUPS, dtype=cc.dtype)
    return jnp.einsum("dghp,gk->dgpkh", cc, eye).reshape(2, SSM_CH, D_SSM)


def _diag_blocks_in(m):
    return jnp.einsum("dghgp->dgph", m.reshape(2, SSM_GROUPS, SSM_GROUP, SSM_GROUPS, SSM_STATE))


def _diag_blocks_out(m):
    return jnp.einsum("dgpgh->dghp", m.reshape(2, SSM_GROUPS, SSM_STATE, SSM_GROUPS, SSM_GROUP))


def _device_step(x, mem, target, wts, sp, reducer=None):
    t = x.shape[0]
    tm = min(TM, t)
    g = {}

    first_gather = wts.start("f1a")
    u1 = _rmsnorm("norm_ffn1", x, sp["ffn1_norm"], tm, after=first_gather)
    (w_gu,) = wts.finish("f1a", [u1])
    w_f1 = {"gate": (w_gu, FFN_GATE), "up": (w_gu, FFN_UP)}
    g1, up1, a1 = _ffn_up("ffn1_up", u1, w_f1, tm, after=wts.start("f1b", [w_gu]))
    (w_dn,) = wts.finish("f1b", [a1])
    w_f1["down"] = (w_dn, 0)
    h1 = _ffn_down("ffn1_down", a1, w_f1, x, tm, after=wts.start("win", [w_dn]))

    u2 = _rmsnorm("norm_mix", h1, sp["mix_norm"], tm)
    (w_in_g,) = wts.finish("win", [u2])
    w_in_t = w_in_g.reshape(D_FF, D_MODEL)
    proj = _mm("mix_in", [(u2, BS((tm, D_MODEL), lambda j, i: (i, 0)), w_in_t, BS((D_FF // 2, D_MODEL), lambda j, i: (j, 0)), NT)],
               grid=(2, t // tm), out_shape=SDS((t, D_FF), F32), out_spec=BS((tm, D_FF // 2), lambda j, i: (i, j)),
               after=wts.start("mix", [w_in_g]))
    pooled, mixed, ms = _pool_fwd(proj, sp["pool_w"][0], sp["pool_scale"])

    ar = sp["ssm_a_re"].reshape(2 * SSM_GROUPS, SSM_STATE)
    ai = sp["ssm_a_im"].reshape(2 * SSM_GROUPS, SSM_STATE)
    ldt = sp["ssm_log_dt"].reshape(2 * SSM_GROUPS, 1)
    abr, abi, qr, qi = _ssm_disc(ar, ai, ldt, after=first_gather)
    b_r = sp["ssm_b_re"].reshape(2 * SSM_CH, SSM_GROUP)
    b_i = sp["ssm_b_im"].reshape(2 * SSM_CH, SSM_GROUP)
    qr_col, qi_col = qr.reshape(2 * SSM_CH, 1), qi.reshape(2 * SSM_CH, 1)
    bbr, bbi = _ssm_bbar(qr_col, qi_col, b_r, b_i)
    shape_b = (2, SSM_GROUPS, SSM_STATE, SSM_GROUP)
    b_mat = jnp.concatenate([_block_diag_in(bbr.reshape(shape_b)), _block_diag_in(bbi.reshape(shape_b))], axis=-1).astype(BF16)
    c_mat = jnp.concatenate([_block_diag_out(sp["ssm_c_re"][0]), -_block_diag_out(sp["ssm_c_im"][0])], axis=1).astype(BF16)
    b_mat_t = jnp.swapaxes(b_mat, 1, 2)
    c_mat_t = jnp.swapaxes(c_mat, 1, 2)
    a_r = abr.reshape(2, 1, SSM_CH)
    a_i = abi.reshape(2, 1, SSM_CH)

    s_in =proj[:, D_POOL:D_POOL + D_SSM].astype(BF16)
    states, y_dirs = [], []
    for dr in range(2):
        st, yd = _ssm_scan(f"ssm_scan_fwd{dr}", s_in, b_mat[dr], a_r[dr], a_i[dr], c_mat[dr], reverse=(dr == 1))
        states.append(st)
        y_dirs.append(yd)
    w_sq, w_kv, w_e = wts.finish("mix", y_dirs)
    w_mo, w_q, w_xo = (w_sq[:, 256 * k:256 * (k + 1)].reshape(D_MODEL, D_MODEL) for k in range(3))
    w_d = w_kv[:, None]
    y_total, yssm = _ssm_combine(proj, y_dirs[0], y_dirs[1], sp["ssm_d"], tm, after=wts.start("f2", [w_e]))

    merged = _mixer_merge(ms, yssm, w_e, proj, tm)
    h2 = _plain_mm("mix_out", merged, w_mo, NN, F32, tm, resid=h1)

    u3 = _rmsnorm("norm_xattn", h2, sp["xattn_norm"], tm)
    mem_n = _rmsnorm("norm_mem", mem, sp["mem_norm"], mem.shape[0])
    q = _plain_mm("attn_q", u3, w_q, NN, BF16, tm)
    n_mem = mem.shape[0]
    kv = _mm("attn_kv", [(mem_n, BS((n_mem, D_MODEL), lambda s: (0, 0)), w_d, BS((None, None, D_MODEL, 512), lambda s: (s, 0, 0, 0)), NN)],
             grid=(N_SHARD,), out_shape=SDS((n_mem, 2 * D_MODEL), BF16), out_spec=BS((n_mem, 512), lambda s: (0, s)))
    o = _attn_fwd(q, kv, tm)
    h3 = _plain_mm("attn_out", o, w_xo, NN, F32, tm, resid=h2)

    u4 = _rmsnorm("norm_ffn2", h3, sp["ffn2_norm"], tm)
    (w_2,) = wts.finish("f2", [u4])
    w_f2 = {"gate": (w_2, FFN_GATE), "up": (w_2, FFN_UP), "down": (w_2, FFN_DOWN)}
    g2, up2, a2 = _ffn_up("ffn2_up", u4, w_f2, tm)
    h4 = _ffn_down("ffn2_down", a2, w_f2, h3, tm)

    loss, dh4, dh4_b, g["final_norm"] = _loss_head(h4, sp["final_norm"].reshape(1, D_MODEL), target, tm)

    dg2, dup2 = _ffn_bwd_act("ffn2_bwd_act", dh4_b, w_f2, g2, up2, tm)
    dw_f2 = _ffn_dw("ffn2_dw", u4, dg2, dup2, a2, dh4_b, tm)
    du4 = _ffn_dx("ffn2_dx", dg2, dup2, w_f2, tm)
    dh3, dh3_b, g["ffn2_norm"] = _rmsnorm_bwd("norm_ffn2_bwd", h3, sp["ffn2_norm"], du4, dh4, tm)

    d_o = _plain_mm("attn_out_dx", dh3_b, w_xo, NT, BF16, tm)
    dw_xo = _dw_mm("attn_out_dw", o, dh3_b, tm)
    dq, dkv = _attn_bwd(q, kv, d_o, tm)
    dw_q = _dw_mm("attn_q_dw", u3, dq, tm)
    du3 = _plain_mm("attn_q_dx", dq, w_q, NT, F32, tm)
    dw_kv = _mm("attn_kv_dw", [(mem_n, BS((n_mem, D_MODEL), lambda s: (0, 0)), dkv, BS((n_mem, 512), lambda s: (0, s)), TN)],
                grid=(N_SHARD,), out_shape=SDS((N_SHARD, D_MODEL, 512), BF16), out_spec=BS((None, D_MODEL, 512), lambda s: (s, 0, 0)))
    dmem_n = _mm("attn_kv_dx", [(dkv, BS((n_mem, 512), lambda s: (0, s)), w_d, BS((None, None, D_MODEL, 512), lambda s: (s, 0, 0, 0)), NT)],
                 grid=(N_SHARD,), red_axis=0, out_shape=SDS((n_mem, D_MODEL), F32), out_spec=BS((n_mem, D_MODEL), lambda s: (0, 0)))
    _, _, g["mem_norm"] = _rmsnorm_bwd("norm_mem_bwd", mem, sp["mem_norm"], dmem_n, None, n_mem)
    dh2, dh2_b, g["xattn_norm"] = _rmsnorm_bwd("norm_xattn_bwd", h2, sp["xattn_norm"], du3, dh3, tm)

    square = (N_SHARD, D_MODEL // N_SHARD, D_MODEL)
    early = [dw_f2.reshape(N_SHARD, 3 * FF_SH, D_MODEL), dw_xo.reshape(square), dw_q.reshape(square), dw_kv]
    swapping = reducer.swap_start("a1", early) if reducer is not None else []
    dmerged = _plain_mm("mix_out_dx", dh2_b, w_mo, NT, BF16, tm, after=swapping)
    dw_mo = _dw_mm("mix_out_dw", merged, dh2_b, tm)
    d_gp, d_gs, dzp, dzv, dzg = _mixer_merge_bwd(ms, yssm, w_e, proj, dmerged, tm)
    dw_e = _mixer_dw(ms, yssm, dzp, dzv, dzg, tm)
    d_ms, d_yt = _mixer_dx(dzp, dzv, dzg, w_e, y_total, tm)
    dp, d_scale, d_pw = _pool_bwd(d_ms, mixed, pooled, sp["pool_w"][0], sp["pool_scale"])
    g["pool_scale"] = d_scale
    g["pool_w"] = d_pw[None]

    d_yt_b = d_yt.astype(BF16)
    du_dirs, d_abr, d_abi, d_cm, d_bm = [], [], [], [], []
    for dr in range(2):
        lam, du = _ssm_scan(f"ssm_scan_bwd{dr}", d_yt_b, c_mat_t[dr], a_r[dr], -a_i[dr], b_mat_t[dr], reverse=(dr == 0))
        du_dirs.append(du)
        da_r, da_i = _ssm_da(f"ssm_da{dr}", lam, states[dr], reverse=(dr == 1))
        d_abr.append(da_r)
        d_abi.append(da_i)
        d_cm.append(_dw_mm(f"ssm_dc{dr}", states[dr], d_yt_b, tm, F32))
        d_bm.append(_dw_mm(f"ssm_db{dr}", s_in, lam, tm, F32))
    d_cm = jnp.stack(d_cm)
    d_bm = jnp.stack(d_bm)
    g["ssm_c_re"] = _diag_blocks_out(d_cm[:, :SSM_CH])[None]
    g["ssm_c_im"] = -_diag_blocks_out(d_cm[:, SSM_CH:])[None]
    g_bbr = _diag_blocks_in(d_bm[:, :, :SSM_CH]).reshape(2 * SSM_CH, SSM_GROUP)
    g_bbi = _diag_blocks_in(d_bm[:, :, SSM_CH:]).reshape(2 * SSM_CH, SSM_GROUP)
    d_qr, d_qi, d_br, d_bi = _ssm_bbar_bwd(qr_col, qi_col, b_r, b_i, g_bbr, g_bbi)
    g["ssm_b_re"] = d_br.reshape(sp["ssm_b_re"].shape)
    g["ssm_b_im"] = d_bi.reshape(sp["ssm_b_im"].shape)
    d_ar, d_ai, d_ldt = _ssm_disc_bwd(ar, ai, ldt, jnp.stack(d_abr).reshape(ar.shape), jnp.stack(d_abi).reshape(ar.shape),
                                      d_qr.reshape(ar.shape), d_qi.reshape(ar.shape))
    g["ssm_a_re"] = d_ar.reshape(sp["ssm_a_re"].shape)
    g["ssm_a_im"] = d_ai.reshape(sp["ssm_a_im"].shape)
    g["ssm_log_dt"] = d_ldt.reshape(sp["ssm_log_dt"].shape)
    ds, g["ssm_d"] = _ssm_ds(proj, d_yt, du_dirs[0], du_dirs[1], sp["ssm_d"], tm)

    d_proj = jnp.concatenate([dp, ds, d_gp, d_gs], axis=1)
    dw_in_t = _mm("mix_in_dw", [(d_proj, BS((tm, D_FF // 2), lambda j, i: (i, j)), u2, BS((tm, D_MODEL), lambda j, i: (i, 0)), TN)],
                  grid=(2, t // tm), red_axis=1, out_shape=SDS((D_FF, D_MODEL), BF16), out_spec=BS((D_FF // 2, D_MODEL), lambda j, i: (j, 0)))
    du2 = _plain_mm("mix_in_dx", d_proj, w_in_t, NN, F32, tm)
    dh1, dh1_b, g["mix_norm"] = _rmsnorm_bwd("norm_mix_bwd", h1, sp["mix_norm"], du2, dh2, tm)

    early += [dw_mo.reshape(square), dw_e, dw_in_t.reshape(N_SHARD, FF_SH, D_MODEL)]
    g["final_norm"] = g["final_norm"].reshape(D_MODEL)

    travelling = reducer.start("a", early[4:], swapped=["a1"]) if reducer is not None else []
    dg1, dup1 = _ffn_bwd_act("ffn1_bwd_act", dh1_b, w_f1, g1, up1, tm, after=travelling)
    dw_f1 = _ffn_dw("ffn1_dw", u1, dg1, dup1, a1, dh1_b, tm).reshape(N_SHARD, 3 * FF_SH, D_MODEL)
    if reducer is not None:
        travelling = reducer.start("b", [dw_f1], after=reducer.finish("a", [dw_f1]))
        travelling = travelling + reducer.join_start("a", after=travelling)
    du1 = _ffn_dx("ffn1_dx", dg1, dup1, w_f1, tm, after=travelling)
    grad_x, _, g["ffn1_norm"] = _rmsnorm_bwd("norm_ffn1_bwd", x, sp["ffn1_norm"], du1, dh1, tm)
    if reducer is not None:
        reducer.finish("b", [grad_x])
        reducer.join_finish("a", [grad_x])
    return loss, grad_x, early + [dw_f1], g


def _mesh_place():
    x, y, c = lax.axis_index("x"), lax.axis_index("y"), lax.axis_index("c")
    chips = [(1 - x, y), (x, 1 - y), (1 - x, 1 - y)]
    return x, y, c, chips


def _remote(src, dst, send_sems, recv_sems, k, to):
    return pltpu.make_async_remote_copy(src_ref=src, dst_ref=dst, send_sem=send_sems.at[k], recv_sem=recv_sems.at[k],
                                        device_id=to, device_id_type=MESH)


def _sibling_swap_halves(tag, grads, after=()):
    n = len(grads)
    after_ops, after_specs = _after_operands(after)

    def body(*refs):
        ins, outs = refs[:n], refs[n + len(after_ops):2 * n + len(after_ops)]
        send_sems, recv_sems = refs[2 * n + len(after_ops):]
        x, y, c, _ = _mesh_place()
        sibling = (x, y, 1 - c)
        copies = []
        for k in range(n):
            half = grads[k].shape[1] // 2
            theirs = pl.ds(pl.multiple_of((1 - c) * half, 16), half)
            cp = _remote(ins[k].at[:, theirs, :], outs[k], send_sems, recv_sems, k, sibling)
            cp.start()
            copies.append(cp)
        for cp in copies:
            cp.wait_recv()
        for cp in copies:
            cp.wait_send()

    hbm = BS(memory_space=pl.ANY)
    return pl.pallas_call(
        body, out_shape=tuple(SDS((g.shape[0], g.shape[1] // 2, g.shape[2]), g.dtype) for g in grads),
        in_specs=[hbm] * n + after_specs, out_specs=(hbm,) * n,
        scratch_shapes=[pltpu.SemaphoreType.DMA((n,)), pltpu.SemaphoreType.DMA((n,))],
        name="reduce_sibling_send_" + tag, compiler_params=_params())(*grads, *after_ops)


def _row_tile(rows, cap=512):
    return max(r for r in range(16, cap + 1, 16) if rows % r == 0)


def _chip_presum(k, grad, got, c_idx):
    n_sh, rows, cols = grad.shape
    half = rows // 2
    tr = _row_tile(half)
    grad4 = grad.reshape(n_sh, 2, half, cols)

    def body(c_ref, a_ref, b_ref, o_ref):
        o_ref[...] = (a_ref[...].astype(F32) + b_ref[...].astype(F32)).astype(o_ref.dtype)

    return pl.pallas_call(
        body, out_shape=SDS((n_sh, half, cols), BF16),
        grid_spec=pltpu.PrefetchScalarGridSpec(
            num_scalar_prefetch=1, grid=(n_sh, half // tr),
            in_specs=[BS((None, None, tr, cols), lambda s, i, c_ref: (s, c_ref[0], i, 0)),
                      BS((None, tr, cols), lambda s, i, c_ref: (s, i, 0))],
            out_specs=BS((None, tr, cols), lambda s, i, c_ref: (s, i, 0))),
        name=f"reduce_presum{k}", compiler_params=_params("parallel", "parallel"))(c_idx, grad4, got)


HBM_SPEC = BS(memory_space=pltpu.HBM)
SEM_SPEC = BS(memory_space=pltpu.SEMAPHORE)
DATAFLOW = pltpu.SideEffectType.DATAFLOW_SIDE_EFFECTING


def _chip_exchange_copies(parts, lands, send_sems, recv_sems):
    _, _, c, chips = _mesh_place()
    return [_remote(parts[k].at[2 * px + py], lands[k].at[j], send_sems, recv_sems, 3 * k + j, (px, py, c))
            for k in range(len(parts)) for j, (px, py) in enumerate(chips)]


def _gather_copies(shards, lands, send_sems, recv_sems):
    x, y, c, chips = _mesh_place()
    return [_remote(shards[k], lands[k].at[2 * x + y], send_sems, recv_sems, 3 * k + j, (px, py, c))
            for k in range(len(shards)) for j, (px, py) in enumerate(chips)]


def _swap_copies(grads, lands, send_sems, recv_sems):
    x, y, c, _ = _mesh_place()
    out = []
    for k in range(len(grads)):
        half = grads[k].shape[1] // 2
        theirs = pl.ds(pl.multiple_of((1 - c) * half, 16), half)
        out.append(_remote(grads[k].at[:, theirs, :], lands[k], send_sems, recv_sems, k, (x, y, 1 - c)))
    return out


def _join_copies(fulls, same, send_sems, recv_sems):
    x, y, c, _ = _mesh_place()
    out = []
    for k in range(len(fulls)):
        half = fulls[k].shape[0] // 2
        mine = fulls[k].at[pl.ds(pl.multiple_of(c * half, 8), half), :]
        out.append(_remote(mine, mine, send_sems, recv_sems, k, (x, y, 1 - c)))
    return out


def _everyone_copies(packs, lands, send_sems, recv_sems):
    x, y, c, _ = _mesh_place()
    out = []
    for k in range(len(packs)):
        for j in range(N_DEV - 1):
            bx, by, bc = (j + 1) >> 2 & 1, (j + 1) >> 1 & 1, (j + 1) & 1
            peer = (x ^ bx, y ^ by, c ^ bc)
            out.append(_remote(packs[k], lands[k].at[4 * x + 2 * y + c], send_sems, recv_sems, (N_DEV - 1) * k + j, peer))
    return out


def _split_start(name, copies, sources, land_shapes, after=(), fanout=3):
    n = len(sources)
    n_land = len(land_shapes)
    m = n + n_land
    n_sems = fanout * n
    after_ops, after_specs = _after_operands(after)

    def body(*refs):
        ins = refs[:n]
        lands = refs[n:m] if n_land else ins
        send_sems, recv_sems = refs[m + len(after_ops)], refs[m + len(after_ops) + 1]
        token = refs[-1]
        for cp in copies(ins, lands, send_sems, recv_sems):
            cp.start()
        token[...] = jnp.zeros_like(token)

    lands = [pltpu.with_memory_space_constraint(lax.empty(s, d), pltpu.HBM) for s, d in land_shapes]
    sources = [pltpu.with_memory_space_constraint(p, pltpu.HBM) for p in sources]
    thru = [pltpu.HBM(a.shape, a.dtype) for a in sources + lands]
    out = pl.pallas_call(
        body, name=name,
        out_shape=(pltpu.SemaphoreType.DMA((n_sems,)), pltpu.SemaphoreType.DMA((n_sems,)), *thru, SDS((8, 128), F32)),
        in_specs=[HBM_SPEC] * m + after_specs,
        out_specs=(SEM_SPEC, SEM_SPEC, *[HBM_SPEC] * m, BS(memory_space=pltpu.VMEM)),
        input_output_aliases={i: 2 + i for i in range(m)},
        compiler_params=pltpu.CompilerParams(has_side_effects=DATAFLOW))(*sources, *lands, *after_ops)
    return out[0], out[1], list(out[2:2 + n]), list(out[2 + n:2 + m]), out[-1]


def _split_wait(name, copies, send_sems, recv_sems, sources, lands, after):
    n = len(sources)
    m = n + len(lands)
    after_ops, after_specs = _after_operands(after)

    def body(*refs):
        ins = refs[:n]
        zones = refs[n:m] if m > n else ins
        for cp in copies(ins, zones, refs[m], refs[m + 1]):
            cp.wait_send()
            cp.wait_recv()

    out = pl.pallas_call(
        body, name=name,
        out_shape=tuple(pltpu.HBM(a.shape, a.dtype) for a in sources + lands),
        in_specs=[HBM_SPEC] * m + [SEM_SPEC, SEM_SPEC] + after_specs, out_specs=(HBM_SPEC,) * m,
        input_output_aliases={i: i for i in range(m)},
        compiler_params=pltpu.CompilerParams(has_side_effects=DATAFLOW))(*sources, *lands, send_sems, recv_sems, *after_ops)
    return list(out[:n]), list(out[n:])


class _WeightGatherer:
    def __init__(self, shards):
        self.shards, self.open = shards, {}
        self.me = 2 * lax.axis_index("x") + lax.axis_index("y")

    def start(self, tag, after=()):
        shapes = [((N_SHARD,) + s.shape, s.dtype) for s in self.shards[tag]]
        self.open[tag] = _split_start("gather_start_" + tag, _gather_copies, self.shards[tag], shapes, after)
        return [self.open[tag][-1]]

    def finish(self, tag, after):
        send_sems, recv_sems, shards, lands, _ = self.open.pop(tag)
        shards, lands = _split_wait("gather_wait_" + tag, _gather_copies, send_sems, recv_sems, shards, lands, after)
        return [lax.dynamic_update_slice(zone, s[None], (self.me, 0, 0)) for zone, s in zip(lands, shards)]


class _GradReducer:
    def __init__(self):
        self.c_idx = lax.axis_index("c").astype(jnp.int32).reshape(1)
        self.place = jnp.stack([2 * lax.axis_index("x") + lax.axis_index("y"), lax.axis_index("c")]).astype(jnp.int32)
        self.swaps, self.open, self.landed, self.joins, self.reduced = {}, {}, {}, {}, []

    def swap_start(self, tag, grads, after=()):
        shapes = [((g.shape[0], g.shape[1] // 2, g.shape[2]), g.dtype) for g in grads]
        self.swaps[tag] = _split_start("reduce_swap_start_" + tag, _swap_copies, grads, shapes, after, fanout=1)
        return [self.swaps[tag][-1]]

    def start(self, tag, grads, after=(), swapped=()):
        pairs = []
        for s in swapped:
            send_sems, recv_sems, early, lands, _ = self.swaps.pop(s)
            pairs += zip(*_split_wait("reduce_swap_wait_" + s, _swap_copies, send_sems, recv_sems, early, lands, grads[-1:]))
        pairs += zip(grads, _sibling_swap_halves(tag, grads, after))
        parts = [_chip_presum(f"{tag}{k}", g, s, self.c_idx) for k, (g, s) in enumerate(pairs)]
        shapes = [((3,) + p.shape[1:], p.dtype) for p in parts]
        self.open[tag] = _split_start("reduce_exchange_start_" + tag, _chip_exchange_copies, parts, shapes)
        return [self.open[tag][-1]]

    def finish(self, tag, after):
        send_sems, recv_sems, parts, lands, _ = self.open.pop(tag)
        self.landed[tag] = _split_wait("reduce_exchange_wait_" + tag, _chip_exchange_copies, send_sems, recv_sems, parts, lands, after)
        return self.landed[tag][1][:1]

    def _sums(self, tag, after=()):
        parts, landed = self.landed.pop(tag)
        return [_chip_sum(f"{tag}{k}", p, got, self.place, after) for k, (p, got) in enumerate(zip(parts, landed))]

    def join_start(self, tag, after=()):
        self.joins[tag] = _split_start("reduce_join_start_" + tag, _join_copies, self._sums(tag, after), [], fanout=1)
        return [self.joins[tag][-1]]

    def join_finish(self, tag, after):
        send_sems, recv_sems, fulls, _, _ = self.joins.pop(tag)
        self.reduced += _split_wait("reduce_join_wait_" + tag, _join_copies, send_sems, recv_sems, fulls, [], after)[0]

    def join(self, tag, after=()):
        self.reduced += _sibling_join_halves(self._sums(tag), after)


def _chip_sum(k, part, got, place, after=()):
    _, half, cols = part.shape
    tr = _row_tile(half)
    n_t = half // tr
    after_ops, after_specs = _after_operands(after)

    def body(place_ref, a_ref, b_ref, *rest):
        o_ref = rest[-1]
        acc = a_ref[...].astype(F32)
        for j in range(3):
            acc = acc + b_ref[j].astype(F32)
        o_ref[...] = acc

    return pl.pallas_call(
        body, out_shape=SDS((2 * half, cols), F32),
        grid_spec=pltpu.PrefetchScalarGridSpec(
            num_scalar_prefetch=1, grid=(n_t,),
            in_specs=[BS((None, tr, cols), lambda i, place_ref: (place_ref[0], i, 0)),
                      BS((3, tr, cols), lambda i, place_ref: (0, i, 0))] + after_specs,
            out_specs=BS((tr, cols), lambda i, place_ref: (place_ref[1] * n_t + i, 0))),
        name=f"reduce_sum{k}", compiler_params=_params("parallel"))(place, part, got, *after_ops)


def _sibling_join_halves(fulls, after=()):
    n = len(fulls)
    after_ops, after_specs = _after_operands(after)

    def body(*refs):
        outs = refs[n + len(after_ops):2 * n + len(after_ops)]
        send_sems, recv_sems = refs[2 * n + len(after_ops):]
        copies = _join_copies(outs, outs, send_sems, recv_sems)
        for cp in copies:
            cp.start()
        for cp in copies:
            cp.wait_recv()
        for cp in copies:
            cp.wait_send()

    hbm = BS(memory_space=pl.ANY)
    return list(pl.pallas_call(
        body, out_shape=tuple(SDS(f.shape, f.dtype) for f in fulls),
        in_specs=[hbm] * n + after_specs, out_specs=(hbm,) * n, input_output_aliases={k: k for k in range(n)},
        scratch_shapes=[pltpu.SemaphoreType.DMA((n,)), pltpu.SemaphoreType.DMA((n,))],
        name="reduce_sibling_join", compiler_params=_params())(*fulls, *after_ops))


N_DEV = 8


def _sum_devices(packs):
    _, rows, lanes = packs.shape

    def body(p_ref, o_ref):
        acc = p_ref[0]
        for dev in range(1, N_DEV):
            acc = acc + p_ref[dev]
        o_ref[...] = acc

    vm = BS(memory_space=pltpu.VMEM)
    return pl.pallas_call(body, out_shape=SDS((rows, lanes), F32), in_specs=[vm], out_specs=vm,
                          name="small_sum", compiler_params=_params())(packs)


def _adamw(name, w, grad, row0, m, v):
    rows, cols = w.shape
    tr = rows if rows < 16 else _row_tile(rows, 256)
    bc1 = 1.0 - ADAM_B1 ** ADAM_STEP
    bc2 = 1.0 - ADAM_B2 ** ADAM_STEP

    def body(w_ref, g_ref, m_ref, v_ref, go_ref, d_ref, mo_ref, vo_ref):
        g = g_ref[...]
        m_new = ADAM_B1 * m_ref[...] + (1.0 - ADAM_B1) * g
        v_new = ADAM_B2 * v_ref[...] + (1.0 - ADAM_B2) * (g * g)
        go_ref[...] = g
        mo_ref[...] = m_new
        vo_ref[...] = v_new
        d_ref[...] = -ADAM_LR * ((m_new / bc1) / (jnp.sqrt(v_new / bc2) + ADAM_EPS) + ADAM_WD * w_ref[...])

    blk = BS((tr, cols), lambda i: (i, 0))
    shape = SDS((rows, cols), F32)
    return pl.pallas_call(
        body, out_shape=(shape,) * 4, grid=(rows // tr,),
        in_specs=[blk, BS((tr, cols), lambda i: (row0 // tr + i, 0)), blk, blk], out_specs=(blk,) * 4,
        name=name, compiler_params=_params("parallel"))(w, grad, m, v)


SMALL_LANES = 128


def _pack_small(parts):
    flat = jnp.concatenate([jnp.ravel(p) for p in parts])
    rows = -(-flat.shape[0] // (64 * SMALL_LANES)) * 64
    return jnp.pad(flat, (0, rows * SMALL_LANES - flat.shape[0])).reshape(rows, SMALL_LANES)


def _unpack_small(packed, like):
    flat = jnp.ravel(packed)
    out, at = [], 0
    for p in like:
        out.append(flat[at:at + p.size].reshape(p.shape))
        at += p.size
    return out


def kernel(x, mem, ffn1_norm, ffn1_w_gate, ffn1_w_up, ffn1_w_down, mix_norm, w_in, pool_w, pool_scale, w_pool_proj, ssm_a_re, ssm_a_im, ssm_log_dt, ssm_b_re, ssm_b_im, ssm_c_re, ssm_c_im, ssm_d, w_glu_val, w_glu_gate, w_mix_out, xattn_norm, mem_norm, w_q, w_kv, w_xo, ffn2_norm, ffn2_w_gate, ffn2_w_up, ffn2_w_down, final_norm, loss_target, m_ffn1_norm, m_ffn1_w_gate, m_ffn1_w_up, m_ffn1_w_down, m_mix_norm, m_w_in, m_pool_w, m_pool_scale, m_w_pool_proj, m_ssm_a_re, m_ssm_a_im, m_ssm_log_dt, m_ssm_b_re, m_ssm_b_im, m_ssm_c_re, m_ssm_c_im, m_ssm_d, m_w_glu_val, m_w_glu_gate, m_w_mix_out, m_xattn_norm, m_mem_norm, m_w_q, m_w_kv, m_w_xo, m_ffn2_norm, m_ffn2_w_gate, m_ffn2_w_up, m_ffn2_w_down, m_final_norm, v_ffn1_norm, v_ffn1_w_gate, v_ffn1_w_up, v_ffn1_w_down, v_mix_norm, v_w_in, v_pool_w, v_pool_scale, v_w_pool_proj, v_ssm_a_re, v_ssm_a_im, v_ssm_log_dt, v_ssm_b_re, v_ssm_b_im, v_ssm_c_re, v_ssm_c_im, v_ssm_d, v_w_glu_val, v_w_glu_gate, v_w_mix_out, v_xattn_norm, v_mem_norm, v_w_q, v_w_kv, v_w_xo, v_ffn2_norm, v_ffn2_w_gate, v_ffn2_w_up, v_ffn2_w_down, v_final_norm):
    given = dict(locals())
    w = {n: given[n] for n in WEIGHTS}
    m = {n: given["m_" + n] for n in WEIGHTS}
    v = {n: given["v_" + n] for n in WEIGHTS}

    def shard_view(a, n):
        return a[0].T if n in TRANSPOSED else a[0]

    def shard_unview(a, n):
        return (a.T if n in TRANSPOSED else a)[None]

    shards = {tag: [jnp.concatenate([shard_view(w[n], n).astype(BF16) for n in grp], axis=0) for grp in arrays]
              for tag, arrays in GATHER_PHASES.items()}
    reducer = _GradReducer()
    loss_part, grad_x, _, small = _device_step(x[0], mem[0], loss_target[0], _WeightGatherer(shards),
                                               {n: w[n] for n in SMALL}, reducer)
    loss = lax.psum(loss_part[0, 0], ("x", "y", "c"))

    pack = _pack_small([small[n] for n in SMALL])
    everyone = _split_start("small_start", _everyone_copies, [pack], [((N_DEV,) + pack.shape, F32)], fanout=N_DEV - 1)
    reducer.join("b", after=everyone[-1:])

    grads, delta, new_m, new_v = {}, {}, {}, {}
    for grp, red in zip(REDUCE_GROUPS, reducer.reduced):
        row0 = 0
        for n in grp:
            w_n = shard_view(w[n], n)
            grads[n], delta[n], new_m[n], new_v[n] = (
                shard_unview(o, n) for o in _adamw("adamw_" + n, w_n, red, row0, shard_view(m[n], n), shard_view(v[n], n)))
            row0 += w_n.shape[0]

    send_sems, recv_sems, packs, landed, _ = everyone
    packs, landed = _split_wait("small_wait", _everyone_copies, send_sems, recv_sems, packs, landed, [delta[REDUCE_GROUPS[-1][-1]]])
    mine = 4 * lax.axis_index("x") + 2 * lax.axis_index("y") + lax.axis_index("c")
    summed = _sum_devices(lax.dynamic_update_slice(landed[0], packs[0][None], (mine, 0, 0)))
    g_small = dict(zip(SMALL, _unpack_small(summed, [w[n] for n in SMALL])))
    narrow = [n for n in SMALL if w[n].ndim > 3]
    dense = [n for n in SMALL if n not in narrow]
    for n in narrow:
        two_d = (-1, w[n].shape[-1])
        outs = _adamw("adamw_" + n, w[n].reshape(two_d), g_small[n].reshape(two_d), 0, m[n].reshape(two_d), v[n].reshape(two_d))
        grads[n], delta[n], new_m[n], new_v[n] = (o.reshape(w[n].shape) for o in outs)
    dense_like = [w[n] for n in dense]
    packed = _adamw("adamw_small", _pack_small(dense_like), _pack_small([g_small[n] for n in dense]), 0,
                    _pack_small([m[n] for n in dense]), _pack_small([v[n] for n in dense]))
    for out, store in zip(packed, (grads, delta, new_m, new_v)):
        for n, val in zip(dense, _unpack_small(out, dense_like)):
            store[n] = val

    return (loss, grad_x[None], *[grads[n] for n in WEIGHTS], *[delta[n] for n in WEIGHTS],
            *[new_m[n] for n in WEIGHTS], *[new_v[n] for n in WEIGHTS])
```

```python
import functools
import math

import jax
import jax.numpy as jnp
from jax import lax
from jax.experimental import pallas as pl
from jax.experimental.pallas import tpu as pltpu

F32 = jnp.float32
BF16 = jnp.bfloat16
SDS = jax.ShapeDtypeStruct
BS = pl.BlockSpec
MESH = pl.DeviceIdType.MESH

D_MODEL = 1024
D_FF = 2816
N_SHARD = 4
FF_SH = D_FF // N_SHARD
D_POOL = 512
POOL_WINDOWS = (2, 4, 8, 16)
POOL_GROUP = 128
D_SSM = 256
SSM_GROUPS = 16
SSM_GROUP = 16
SSM_STATE = 64
SSM_CH = SSM_GROUPS * SSM_STATE
N_HEADS = 4
HEAD_DIM = 256
EPS = 1e-6
ADAM_LR, ADAM_B1, ADAM_B2, ADAM_EPS, ADAM_WD, ADAM_STEP = 0.001, 0.9, 0.999, 1e-08, 0.01, 10

VMEM_LIMIT_V7X = 52 * 1024 * 1024
TM = 512

NN = (((1,), (0,)), ((), ()))
NT = (((1,), (1,)), ((), ()))
TN = (((0,), (0,)), ((), ()))


def _params(*sem):
    return pltpu.CompilerParams(dimension_semantics=sem if sem else None, vmem_limit_bytes=VMEM_LIMIT_V7X)


def _dot(a, b, dims=NN):
    return lax.dot_general(a.astype(BF16), b.astype(BF16), dims, preferred_element_type=F32)


def _sigmoid(v):
    return pl.reciprocal(1.0 + jnp.exp(-v), approx=True)


def _block_dims(spec):
    return tuple(d for d in spec.block_shape if d is not None)


def _after_operands(after):
    return list(after), [BS(memory_space=pl.ANY)] * len(after)


def _mm(name, pairs, *, grid, out_shape, out_spec, red_axis=None, extras=(), epilogue=None, after=()):
    n_pairs, n_extra = len(pairs), len(extras)
    n_red = grid[red_axis] if red_axis is not None else 1
    dims = [p[4] for p in pairs]

    def body(*refs):
        ab = refs[:2 * n_pairs]
        ex = refs[2 * n_pairs:2 * n_pairs + n_extra]
        o_ref = refs[2 * n_pairs + n_extra + len(after)]

        def partial():
            acc = None
            for p in range(n_pairs):
                t = _dot(ab[2 * p][...], ab[2 * p + 1][...], dims[p])
                acc = t if acc is None else acc + t
            return acc

        def finish(acc):
            res = epilogue(acc, *[e[...] for e in ex]) if epilogue is not None else acc
            o_ref[...] = res.astype(o_ref.dtype)

        if n_red == 1:
            finish(partial())
        else:
            acc_ref = refs[-1]
            k = pl.program_id(red_axis)

            @pl.when(k == 0)
            def _():
                acc_ref[...] = jnp.zeros_like(acc_ref)

            acc_ref[...] += partial()

            @pl.when(k == n_red - 1)
            def _():
                finish(acc_ref[...])

    operands, in_specs = [], []
    for a, a_spec, b, b_spec, _ in pairs:
        operands += [a, b]
        in_specs += [a_spec, b_spec]
    for e, e_spec in extras:
        operands.append(e)
        in_specs.append(e_spec)
    after_ops, after_specs = _after_operands(after)
    operands += after_ops
    in_specs += after_specs
    scratch = [pltpu.VMEM(_block_dims(out_spec), F32)] if n_red > 1 else []
    sem = tuple("arbitrary" if ax == red_axis else "parallel" for ax in range(len(grid)))
    return pl.pallas_call(body, out_shape=out_shape, grid=grid, in_specs=in_specs, out_specs=out_spec,
                          scratch_shapes=scratch, name=name, compiler_params=_params(*sem))(*operands)


def _rmsnorm(name, h, gain, tm, after=()):
    t, d = h.shape
    after_ops, after_specs = _after_operands(after)

    def body(h_ref, g_ref, *rest):
        u_ref = rest[-1]
        hv = h_ref[...]
        r = lax.rsqrt(jnp.mean(hv * hv, axis=-1, keepdims=True) + EPS)
        u_ref[...] = ((hv * r) * g_ref[...]).astype(u_ref.dtype)

    return pl.pallas_call(
        body, out_shape=SDS((t, d), BF16), grid=(t // tm,),
        in_specs=[BS((tm, d), lambda i: (i, 0)), BS((1, d), lambda i: (0, 0))] + after_specs,
        out_specs=BS((tm, d), lambda i: (i, 0)), name=name, compiler_params=_params("parallel"))(h, gain, *after_ops)


def _rmsnorm_bwd(name, h, gain, du, dh_in, tm):
    t, d = h.shape
    has_in = dh_in is not None

    def body(*refs):
        if has_in:
            h_ref, g_ref, du_ref, dhin_ref, dh_ref, dhb_ref, dg_ref = refs
        else:
            h_ref, g_ref, du_ref, dh_ref, dhb_ref, dg_ref = refs
        i = pl.program_id(0)
        hv = h_ref[...]
        r = lax.rsqrt(jnp.mean(hv * hv, axis=-1, keepdims=True) + EPS)
        n = hv * r
        duv = du_ref[...].astype(F32)
        dn = duv * g_ref[...]
        dh = r * (dn - n * jnp.mean(dn * n, axis=-1, keepdims=True))
        if has_in:
            dh = dhin_ref[...] + dh
        dh_ref[...] = dh
        dhb_ref[...] = dh.astype(BF16)

        @pl.when(i == 0)
        def _():
            dg_ref[...] = jnp.zeros_like(dg_ref)

        dg_ref[...] += jnp.sum(duv * n, axis=0, keepdims=True)

    row = BS((tm, d), lambda i: (i, 0))
    vec = BS((1, d), lambda i: (0, 0))
    operands = [h, gain, du] + ([dh_in] if has_in else [])
    in_specs = [row, vec, row] + ([row] if has_in else [])
    return pl.pallas_call(
        body, out_shape=(SDS((t, d), F32), SDS((t, d), BF16), SDS((1, d), F32)), grid=(t // tm,),
        in_specs=in_specs, out_specs=(row, row, vec), name=name, compiler_params=_params("arbitrary"))(*operands)


def _loss_head(h, gain, target, tm):
    t, d = h.shape

    def body(h_ref, g_ref, t_ref, loss_ref, dh_ref, dhb_ref, dg_ref):
        i = pl.program_id(0)
        hv = h_ref[...]
        g = g_ref[...]
        r = lax.rsqrt(jnp.mean(hv * hv, axis=-1, keepdims=True) + EPS)
        n = hv * r
        err = n * g - t_ref[...]
        dy = err * (1.0 / d)
        dn = dy * g
        dh = r * (dn - n * jnp.mean(dn * n, axis=-1, keepdims=True))
        dh_ref[...] = dh
        dhb_ref[...] = dh.astype(BF16)

        @pl.when(i == 0)
        def _():
            dg_ref[...] = jnp.zeros_like(dg_ref)
            loss_ref[...] = jnp.zeros_like(loss_ref)

        dg_ref[...] += jnp.sum(dy * n, axis=0, keepdims=True)
        part = 0.5 * jnp.sum(jnp.mean(err * err, axis=-1, keepdims=True), axis=0, keepdims=True)
        loss_ref[...] += jnp.broadcast_to(part, loss_ref.shape)

    row = BS((tm, d), lambda i: (i, 0))
    vec = BS((1, d), lambda i: (0, 0))
    return pl.pallas_call(
        body, out_shape=(SDS((1, 128), F32), SDS((t, d), F32), SDS((t, d), BF16), SDS((1, d), F32)),
        grid=(t // tm,), in_specs=[row, vec, row],
        out_specs=(BS((1, 128), lambda i: (0, 0)), row, row, vec),
        name="loss_head", compiler_params=_params("arbitrary"))(h, gain, target)


FFN_GATE, FFN_UP, FFN_DOWN = 0, 1, 2


def _ffn_w_spec(block, index):
    return BS((None, FF_SH, D_MODEL), lambda *g: (g[index], block, 0))


def _ffn_up(name, u, w_f, tm, after=()):
    t, d = u.shape
    after_ops, after_specs = _after_operands(after)

    def body(u_ref, wg_ref, wu_ref, *rest):
        g_ref, up_ref, a_ref = rest[len(after_ops):]
        uv = u_ref[...]
        g = _dot(uv, wg_ref[...], NT)
        up = _dot(uv, wu_ref[...], NT)
        g_ref[...] = g.astype(BF16)
        up_ref[...] = up.astype(BF16)
        a_ref[...] = (g * _sigmoid(g) * up).astype(BF16)

    hid = BS((None, tm, FF_SH), lambda s, i: (s, i, 0))
    shape = SDS((N_SHARD, t, FF_SH), BF16)
    return pl.pallas_call(
        body, out_shape=(shape, shape, shape), grid=(N_SHARD, t // tm),
        in_specs=[BS((tm, d), lambda s, i: (i, 0)), _ffn_w_spec(w_f["gate"][1], 0), _ffn_w_spec(w_f["up"][1], 0)] + after_specs,
        out_specs=(hid, hid, hid), name=name,
        compiler_params=_params("parallel", "parallel"))(u, w_f["gate"][0], w_f["up"][0], *after_ops)


def _ffn_all_shards_spec(block):
    return BS((N_SHARD, FF_SH, D_MODEL), lambda i: (0, block, 0))


def _ffn_down(name, a, w_f, resid, tm, after=()):
    t, d = resid.shape
    after_ops, after_specs = _after_operands(after)

    def body(a_ref, w_ref, res_ref, *rest):
        o_ref = rest[-1]
        acc = _dot(a_ref[0], w_ref[0])
        for s in range(1, N_SHARD):
            acc = acc + _dot(a_ref[s], w_ref[s])
        o_ref[...] = res_ref[...] + 0.5 * acc

    row = BS((tm, d), lambda i: (i, 0))
    return pl.pallas_call(
        body, out_shape=SDS((t, d), F32), grid=(t // tm,),
        in_specs=[BS((N_SHARD, tm, FF_SH), lambda i: (0, i, 0)), _ffn_all_shards_spec(w_f["down"][1]), row] + after_specs,
        out_specs=row, name=name, compiler_params=_params("parallel"))(a, w_f["down"][0], resid, *after_ops)


def _ffn_bwd_act(name, dh_b, w_f, g, up, tm, after=()):
    t, d = dh_b.shape
    after_ops, after_specs = _after_operands(after)

    def body(dh_ref, wd_ref, g_ref, up_ref, *rest):
        dg_ref, dup_ref = rest[len(after_ops):]
        da = 0.5 * _dot(dh_ref[...], wd_ref[...], NT)
        gv = g_ref[...].astype(F32)
        uv = up_ref[...].astype(F32)
        sg = _sigmoid(gv)
        das = da * sg
        dg_ref[...] = (das * uv * (1.0 + gv * (1.0 - sg))).astype(BF16)
        dup_ref[...] = (das * gv).astype(BF16)

    hid = BS((None, tm, FF_SH), lambda s, i: (s, i, 0))
    shape = SDS((N_SHARD, t, FF_SH), BF16)
    return pl.pallas_call(
        body, out_shape=(shape, shape), grid=(N_SHARD, t // tm),
        in_specs=[BS((tm, d), lambda s, i: (i, 0)), _ffn_w_spec(w_f["down"][1], 0), hid, hid] + after_specs,
        out_specs=(hid, hid), name=name,
        compiler_params=_params("parallel", "parallel"))(dh_b, w_f["down"][0], g, up, *after_ops)


def _ffn_dw(name, u, dg, dup, a, dh_b, tm):
    t, d = u.shape
    n_t = t // tm

    def body(u_ref, dg_ref, dup_ref, a_ref, dh_ref, o_ref, acc):
        i = pl.program_id(1)

        @pl.when(i == 0)
        def _():
            acc[...] = jnp.zeros_like(acc)

        uv = u_ref[...]
        acc[FFN_GATE] += _dot(dg_ref[...], uv, TN)
        acc[FFN_UP] += _dot(dup_ref[...], uv, TN)
        acc[FFN_DOWN] += _dot(a_ref[...], dh_ref[...], TN)

        @pl.when(i == n_t - 1)
        def _():
            o_ref[FFN_GATE] = acc[FFN_GATE].astype(BF16)
            o_ref[FFN_UP] = acc[FFN_UP].astype(BF16)
            o_ref[FFN_DOWN] = (0.5 * acc[FFN_DOWN]).astype(BF16)

    hid = BS((None, tm, FF_SH), lambda s, i: (s, i, 0))
    row = BS((tm, d), lambda s, i: (i, 0))
    return pl.pallas_call(
        body, out_shape=SDS((N_SHARD, 3, FF_SH, d), BF16), grid=(N_SHARD, n_t),
        in_specs=[row, hid, hid, hid, row], out_specs=BS((None, 3, FF_SH, d), lambda s, i: (s, 0, 0, 0)),
        scratch_shapes=[pltpu.VMEM((3, FF_SH, d), F32)],
        name=name, compiler_params=_params("parallel", "arbitrary"))(u, dg, dup, a, dh_b)


def _ffn_dx(name, dg, dup, w_f, tm, after=()):
    t = dg.shape[1]
    tm = tm // 2
    after_ops, after_specs = _after_operands(after)

    def body(dg_ref, dup_ref, wg_ref, wu_ref, *rest):
        o_ref = rest[-1]
        acc = _dot(dg_ref[0], wg_ref[0]) + _dot(dup_ref[0], wu_ref[0])
        for s in range(1, N_SHARD):
            acc = acc + _dot(dg_ref[s], wg_ref[s]) + _dot(dup_ref[s], wu_ref[s])
        o_ref[...] = acc

    hid = BS((N_SHARD, tm, FF_SH), lambda i: (0, i, 0))
    return pl.pallas_call(
        body, out_shape=SDS((t, D_MODEL), F32), grid=(t // tm,),
        in_specs=[hid, hid, _ffn_all_shards_spec(w_f["gate"][1]), _ffn_all_shards_spec(w_f["up"][1])] + after_specs,
        out_specs=BS((tm, D_MODEL), lambda i: (i, 0)), name=name,
        compiler_params=_params("parallel"))(dg, dup, w_f["gate"][0], w_f["up"][0], *after_ops)


def _plain_mm(name, a, b, dims, out_dtype, tm, resid=None, after=()):
    t = a.shape[0]
    n = b.shape[1] if dims == NN else b.shape[0]
    extras = [(resid, BS((tm, n), lambda i: (i, 0)))] if resid is not None else []
    epi = (lambda acc, res: res + acc) if resid is not None else None
    return _mm(name, [(a, BS((tm, a.shape[1]), lambda i: (i, 0)), b, BS(b.shape, lambda i: (0, 0)), dims)],
               grid=(t // tm,), out_shape=SDS((t, n), out_dtype), out_spec=BS((tm, n), lambda i: (i, 0)),
               extras=extras, epilogue=epi, after=after)


def _dw_mm(name, a, b, tm, out_dtype=BF16):
    t, k = a.shape
    n = b.shape[1]
    return _mm(name, [(a, BS((tm, k), lambda i: (i, 0)), b, BS((tm, n), lambda i: (i, 0)), TN)],
               grid=(t // tm,), red_axis=0, out_shape=SDS((k, n), out_dtype), out_spec=BS((k, n), lambda i: (0, 0)))


POOL_CHUNK = 256
POOL_HALO = 8


def _window_sum(v, width, lead):
    n = v.shape[0]
    s = v
    k = 1
    while k < width:
        s = s + pltpu.roll(s, n - k, 0)
        k *= 2
    return pltpu.roll(s, lead, 0) if lead else s


def _pool_count(base, left, right, t, shape):
    pos = base + lax.broadcasted_iota(jnp.int32, shape, 0)
    lo = jnp.maximum(pos - left, 0)
    hi = jnp.minimum(pos + right + 1, t)
    return (hi - lo).astype(F32)


def _pool_fwd(proj, pool_w, pool_scale):
    t = proj.shape[0]
    c, h = POOL_CHUNK, POOL_HALO
    n_chunks = t // c

    def body(proj_hbm, pw_ref, sc_ref, pooled_ref, mixed_ref, ms_ref, pad_ref, sem):
        cp = pltpu.make_async_copy(proj_hbm.at[:, pl.ds(0, D_POOL)], pad_ref.at[pl.ds(h, t), :], sem)
        cp.start()
        pad_ref[pl.ds(0, h), :] = jnp.zeros((h, D_POOL), F32)
        pad_ref[pl.ds(t + h, h), :] = jnp.zeros((h, D_POOL), F32)
        cp.wait()
        for g, width in enumerate(POOL_WINDOWS):
            left = width // 2
            right = width - 1 - left
            cols = slice(g * POOL_GROUP, (g + 1) * POOL_GROUP)
            wmat = pw_ref[g].astype(BF16)
            scale = sc_ref[:, cols]

            def chunk(ci, carry, left=left, right=right, width=width, cols=cols, wmat=wmat, scale=scale):
                base = pl.multiple_of(ci * c, c)
                v = pad_ref[pl.ds(base, c + 2 * h), cols]
                win = _window_sum(v, width, left)[h:h + c]
                cnt = _pool_count(base, left, right, t, (c, POOL_GROUP))
                pooled = (win / cnt - v[h:h + c]).astype(BF16)
                mixed = _dot(pooled, wmat)
                pooled_ref[pl.ds(base, c), cols] = pooled
                mixed_ref[pl.ds(base, c), cols] = mixed.astype(BF16)
                ms_ref[pl.ds(base, c), cols] = (mixed * scale).astype(BF16)
                return carry

            lax.fori_loop(0, n_chunks, chunk, 0)

    vm = BS(memory_space=pltpu.VMEM)
    shape = SDS((t, D_POOL), BF16)
    return pl.pallas_call(
        body, out_shape=(shape, shape, shape),
        in_specs=[BS(memory_space=pl.ANY), vm, vm], out_specs=(vm, vm, vm),
        scratch_shapes=[pltpu.VMEM((t + 2 * h, D_POOL), F32), pltpu.SemaphoreType.DMA],
        name="pool_fwd", compiler_params=_params())(proj, pool_w, pool_scale)


def _pool_bwd(d_ms, mixed, pooled, pool_w, pool_scale):
    t = d_ms.shape[0]
    c, h = POOL_CHUNK, POOL_HALO
    n_chunks = t // c

    def body(dms_ref, mixed_ref, pooled_ref, pw_ref, sc_ref, dp_ref, dsc_ref, dpw_ref, pad_ref):
        pad_ref[pl.ds(0, h), :] = jnp.zeros((h, D_POOL), F32)
        pad_ref[pl.ds(t + h, h), :] = jnp.zeros((h, D_POOL), F32)
        for g, width in enumerate(POOL_WINDOWS):
            left = width // 2
            right = width - 1 - left
            cols = slice(g * POOL_GROUP, (g + 1) * POOL_GROUP)
            wmat = pw_ref[g].astype(BF16)
            scale = sc_ref[:, cols]

            def first(ci, carry, left=left, right=right, cols=cols, wmat=wmat, scale=scale):
                dsc, dpw = carry
                base = pl.multiple_of(ci * c, c)
                dms = dms_ref[pl.ds(base, c), cols].astype(F32)
                dsc = dsc + jnp.sum(dms * mixed_ref[pl.ds(base, c), cols].astype(F32), axis=0, keepdims=True)
                dmix = (dms * scale).astype(BF16)
                dpw = dpw + _dot(pooled_ref[pl.ds(base, c), cols], dmix, TN)
                dpooled = _dot(dmix, wmat, NT)
                cnt = _pool_count(base, left, right, t, (c, POOL_GROUP))
                pad_ref[pl.ds(base + h, c), cols] = dpooled / cnt
                return dsc, dpw

            dsc, dpw = lax.fori_loop(0, n_chunks, first,
                                     (jnp.zeros((1, POOL_GROUP), F32), jnp.zeros((POOL_GROUP, POOL_GROUP), F32)))
            dsc_ref[:, cols] = dsc
            dpw_ref[g] = dpw

            def second(ci, carry, left=left, right=right, width=width, cols=cols):
                base = pl.multiple_of(ci * c, c)
                v = pad_ref[pl.ds(base, c + 2 * h), cols]
                win = _window_sum(v, width, right)[h:h + c]
                cnt = _pool_count(base, left, right, t, (c, POOL_GROUP))
                dp_ref[pl.ds(base, c), cols] = (win - v[h:h + c] * cnt).astype(BF16)
                return carry

            lax.fori_loop(0, n_chunks, second, 0)

    vm = BS(memory_space=pltpu.VMEM)
    return pl.pallas_call(
        body, out_shape=(SDS((t, D_POOL), BF16), SDS((1, D_POOL), F32), SDS((4, POOL_GROUP, POOL_GROUP), F32)),
        in_specs=[vm] * 5, out_specs=(vm, vm, vm),
        scratch_shapes=[pltpu.VMEM((t + 2 * h, D_POOL), F32)],
        name="pool_bwd", compiler_params=_params())(d_ms, mixed, pooled, pool_w, pool_scale)


def _ssm_disc(ar, ai, ldt, after=()):
    after_ops, after_specs = _after_operands(after)

    def body(ar_ref, ai_ref, ldt_ref, *rest):
        abr_ref, abi_ref, qr_ref, qi_ref = rest[len(after_ops):]
        a_r, a_i = ar_ref[...], ai_ref[...]
        dt = jnp.exp(ldt_ref[...])
        mag = jnp.exp(dt * a_r)
        ang = dt * a_i
        abr = mag * jnp.cos(ang)
        abi = mag * jnp.sin(ang)
        den = a_r * a_r + a_i * a_i
        nr = abr - 1.0
        abr_ref[...] = abr
        abi_ref[...] = abi
        qr_ref[...] = (nr * a_r + abi * a_i) / den
        qi_ref[...] = (abi * a_r - nr * a_i) / den

    vm = BS(memory_space=pltpu.VMEM)
    shape = SDS(ar.shape, F32)
    return pl.pallas_call(body, out_shape=(shape,) * 4, in_specs=[vm] * 3 + after_specs, out_specs=(vm,) * 4,
                          name="ssm_disc", compiler_params=_params())(ar, ai, ldt, *after_ops)


def _ssm_disc_bwd(ar, ai, ldt, d_abr, d_abi, d_qr, d_qi):
    def body(ar_ref, ai_ref, ldt_ref, gabr_ref, gabi_ref, gqr_ref, gqi_ref, dar_ref, dai_ref, dldt_ref):
        a_r, a_i = ar_ref[...], ai_ref[...]
        dt = jnp.exp(ldt_ref[...])
        mag = jnp.exp(dt * a_r)
        ang = dt * a_i
        cs, sn = jnp.cos(ang), jnp.sin(ang)
        abr, abi = mag * cs, mag * sn
        den = a_r * a_r + a_i * a_i
        nr = abr - 1.0
        qr = (nr * a_r + abi * a_i) / den
        qi = (abi * a_r - nr * a_i) / den
        gqr, gqi = gqr_ref[...], gqi_ref[...]
        g_nr_num = gqr / den
        g_ni_num = gqi / den
        g_den = -(gqr * qr + gqi * qi) / den
        g_nr = g_nr_num * a_r - g_ni_num * a_i
        g_abi = g_nr_num * a_i + g_ni_num * a_r
        d_ar = g_nr_num * nr + g_ni_num * abi + 2.0 * a_r * g_den
        d_ai = g_nr_num * abi - g_ni_num * nr + 2.0 * a_i * g_den
        g_abr = gabr_ref[...] + g_nr
        g_abi = gabi_ref[...] + g_abi
        g_mag = g_abr * cs + g_abi * sn
        g_ang = mag * (g_abi * cs - g_abr * sn)
        g_e = g_mag * mag
        d_ar = d_ar + g_e * dt
        d_ai = d_ai + g_ang * dt
        g_dt = g_e * a_r + g_ang * a_i
        dar_ref[...] = d_ar
        dai_ref[...] = d_ai
        dldt_ref[...] = jnp.sum(g_dt * dt, axis=1, keepdims=True)

    vm = BS(memory_space=pltpu.VMEM)
    return pl.pallas_call(body, out_shape=(SDS(ar.shape, F32), SDS(ar.shape, F32), SDS(ldt.shape, F32)),
                          in_specs=[vm] * 7, out_specs=(vm,) * 3, name="ssm_disc_bwd",
                          compiler_params=_params())(ar, ai, ldt, d_abr, d_abi, d_qr, d_qi)


def _ssm_bbar(qr, qi, br, bi):
    def body(qr_ref, qi_ref, br_ref, bi_ref, bbr_ref, bbi_ref):
        q_r, q_i, b_r, b_i = qr_ref[...], qi_ref[...], br_ref[...], bi_ref[...]
        bbr_ref[...] = q_r * b_r - q_i * b_i
        bbi_ref[...] = q_r * b_i + q_i * b_r

    vm = BS(memory_space=pltpu.VMEM)
    shape = SDS(br.shape, F32)
    return pl.pallas_call(body, out_shape=(shape, shape), in_specs=[vm] * 4, out_specs=(vm, vm),
                          name="ssm_bbar", compiler_params=_params())(qr, qi, br, bi)


def _ssm_bbar_bwd(qr, qi, br, bi, g_bbr, g_bbi):
    def body(qr_ref, qi_ref, br_ref, bi_ref, gr_ref, gi_ref, dqr_ref, dqi_ref, dbr_ref, dbi_ref):
        q_r, q_i, b_r, b_i = qr_ref[...], qi_ref[...], br_ref[...], bi_ref[...]
        g_r, g_i = gr_ref[...], gi_ref[...]
        dqr_ref[...] = jnp.sum(g_r * b_r + g_i * b_i, axis=1, keepdims=True)
        dqi_ref[...] = jnp.sum(g_i * b_r - g_r * b_i, axis=1, keepdims=True)
        dbr_ref[...] = g_r * q_r + g_i * q_i
        dbi_ref[...] = g_i * q_r - g_r * q_i

    vm = BS(memory_space=pltpu.VMEM)
    return pl.pallas_call(
        body, out_shape=(SDS(qr.shape, F32), SDS(qr.shape, F32), SDS(br.shape, F32), SDS(br.shape, F32)),
        in_specs=[vm] * 6, out_specs=(vm,) * 4, name="ssm_bbar_bwd",
        compiler_params=_params())(qr, qi, br, bi, g_bbr, g_bbi)


SCAN_ROWS = 256


def _ssm_scan(name, inp, w1, a_r, a_i, w2, reverse):
    t = inp.shape[0]
    rows = SCAN_ROWS
    n = t // rows
    n_groups = rows // 8
    ch = SSM_CH
    at = (lambda i: (n - 1 - i, 0)) if reverse else (lambda i: (i, 0))

    def body(in_ref, w1_ref, ar_ref, ai_ref, w2_ref, st_ref, out_ref, cr_ref, ci_ref, k_ref):
        i = pl.program_id(0)

        @pl.when(i == 0)
        def _():
            ar8 = jnp.broadcast_to(ar_ref[...], (8, ch))
            ai8 = jnp.broadcast_to(ai_ref[...], (8, ch))
            row = lax.broadcasted_iota(jnp.int32, (8, ch), 0)
            rank = (7 - row) if reverse else row
            powers = [(ar8, ai8)]
            for _ in range(7):
                p_r, p_i = powers[-1]
                powers.append((p_r * ar8 - p_i * ai8, p_r * ai8 + p_i * ar8))
            zero = jnp.zeros((8, ch), F32)
            for slot, k in enumerate((1, 2, 4)):
                k_ref[2 * slot] = jnp.where(rank >= k, powers[k - 1][0], zero)
                k_ref[2 * slot + 1] = jnp.where(rank >= k, powers[k - 1][1], zero)
            carry_r, carry_i = zero, zero
            for j in range(8):
                carry_r = jnp.where(rank == j, powers[j][0], carry_r)
                carry_i = jnp.where(rank == j, powers[j][1], carry_i)
            k_ref[6] = carry_r
            k_ref[7] = carry_i
            cr_ref[...] = zero
            ci_ref[...] = zero

        st_ref[...] = _dot(in_ref[...], w1_ref[...])

        def group(gi, carry):
            c_r, c_i = carry
            g = (n_groups - 1 - gi) if reverse else gi
            r0 = pl.multiple_of(g * 8, 8)
            x_r = st_ref[pl.ds(r0, 8), 0:ch]
            x_i = st_ref[pl.ds(r0, 8), ch:2 * ch]
            for slot, k in enumerate((1, 2, 4)):
                shift = (8 - k) if reverse else k
                s_r = pltpu.roll(x_r, shift, 0)
                s_i = pltpu.roll(x_i, shift, 0)
                m_r, m_i = k_ref[2 * slot], k_ref[2 * slot + 1]
                x_r, x_i = x_r + m_r * s_r - m_i * s_i, x_i + m_r * s_i + m_i * s_r
            p_r, p_i = k_ref[6], k_ref[7]
            x_r, x_i = x_r + p_r * c_r - p_i * c_i, x_i + p_r * c_i + p_i * c_r
            st_ref[pl.ds(r0, 8), 0:ch] = x_r
            st_ref[pl.ds(r0, 8), ch:2 * ch] = x_i
            last = 0 if reverse else 7
            return (jnp.broadcast_to(x_r[last:last + 1, :], (8, ch)), jnp.broadcast_to(x_i[last:last + 1, :], (8, ch)))

        c_r, c_i = lax.fori_loop(0, n_groups, group, (cr_ref[...], ci_ref[...]))
        cr_ref[...] = c_r
        ci_ref[...] = c_i
        out_ref[...] = _dot(st_ref[...], w2_ref[...])

    return pl.pallas_call(
        body, out_shape=(SDS((t, 2 * ch), F32), SDS((t, D_SSM), F32)), grid=(n,),
        in_specs=[BS((rows, D_SSM), at), BS((D_SSM, 2 * ch), lambda i: (0, 0)), BS((1, ch), lambda i: (0, 0)),
                  BS((1, ch), lambda i: (0, 0)), BS((2 * ch, D_SSM), lambda i: (0, 0))],
        out_specs=(BS((rows, 2 * ch), at), BS((rows, D_SSM), at)),
        scratch_shapes=[pltpu.VMEM((8, ch), F32), pltpu.VMEM((8, ch), F32), pltpu.VMEM((8, 8, ch), F32)],
        name=name, compiler_params=_params("arbitrary"))(inp, w1, a_r, a_i, w2)


DA_ROWS = 512


def _ssm_da(name, lam, states, reverse):
    t = lam.shape[0]
    rows = DA_ROWS
    n = t // rows
    nb = rows // 8
    ch = SSM_CH
    if reverse:
        halo_at = lambda i: (jnp.minimum((i + 1) * nb, t // 8 - 1), 0)
    else:
        halo_at = lambda i: (jnp.maximum(i * nb - 1, 0), 0)

    def body(lam_ref, x_ref, halo_ref, dr_ref, di_ref):
        i = pl.program_id(0)

        @pl.when(i == 0)
        def _():
            dr_ref[...] = jnp.zeros_like(dr_ref)
            di_ref[...] = jnp.zeros_like(di_ref)

        row = lax.broadcasted_iota(jnp.int32, (rows, ch), 0)
        if reverse:
            edge, shift, h_row, live = rows - 1, rows - 1, 0, i < n - 1
        else:
            edge, shift, h_row, live = 0, 1, 7, i > 0

        def neighbour(lo):
            halo = jnp.where(live, halo_ref[h_row:h_row + 1, lo:lo + ch], 0.0)
            return jnp.where(row == edge, jnp.broadcast_to(halo, (rows, ch)), pltpu.roll(x_ref[:, lo:lo + ch], shift, 0))

        xp_r, xp_i = neighbour(0), neighbour(ch)
        l_r, l_i = lam_ref[:, 0:ch], lam_ref[:, ch:2 * ch]
        dr_ref[...] += jnp.sum(l_r * xp_r + l_i * xp_i, axis=0, keepdims=True)
        di_ref[...] += jnp.sum(l_i * xp_r - l_r * xp_i, axis=0, keepdims=True)

    blk = BS((rows, 2 * ch), lambda i: (i, 0))
    vec = BS((1, ch), lambda i: (0, 0))
    return pl.pallas_call(
        body, out_shape=(SDS((1, ch), F32), SDS((1, ch), F32)), grid=(n,),
        in_specs=[blk, blk, BS((8, 2 * ch), halo_at)], out_specs=(vec, vec),
        name=name, compiler_params=_params("arbitrary"))(lam, states, states)


GELU_C = math.sqrt(2.0 / math.pi)
GELU_K = 0.044715


def _ssm_combine(proj, y_fwd, y_bwd, d_skip, tm, after=()):
    t = proj.shape[0]
    after_ops, after_specs = _after_operands(after)

    def body(s_ref, yf_ref, yb_ref, d_ref, *rest):
        yt_ref, g_ref = rest[len(after_ops):]
        y = s_ref[...] * d_ref[...] + yf_ref[...] + yb_ref[...]
        yt_ref[...] = y
        th = jnp.tanh(GELU_C * (y + GELU_K * y * y * y))
        g_ref[...] = (0.5 * y * (1.0 + th)).astype(BF16)

    blk = BS((tm, D_SSM), lambda i: (i, 0))
    return pl.pallas_call(
        body, out_shape=(SDS((t, D_SSM), F32), SDS((t, D_SSM), BF16)), grid=(t // tm,),
        in_specs=[BS((tm, D_SSM), lambda i: (i, D_POOL // D_SSM)), blk, blk, BS((1, D_SSM), lambda i: (0, 0))] + after_specs,
        out_specs=(blk, blk), name="ssm_combine",
        compiler_params=_params("parallel"))(proj, y_fwd, y_bwd, d_skip, *after_ops)


def _ssm_ds(proj, d_yt, du_fwd, du_bwd, d_skip, tm):
    t = proj.shape[0]

    def body(s_ref, dy_ref, duf_ref, dub_ref, d_ref, ds_ref, dd_ref):
        i = pl.program_id(0)
        dy = dy_ref[...]
        ds_ref[...] = (dy * d_ref[...] + duf_ref[...] + dub_ref[...]).astype(BF16)

        @pl.when(i == 0)
        def _():
            dd_ref[...] = jnp.zeros_like(dd_ref)

        dd_ref[...] += jnp.sum(dy * s_ref[...], axis=0, keepdims=True)

    blk = BS((tm, D_SSM), lambda i: (i, 0))
    vec = BS((1, D_SSM), lambda i: (0, 0))
    return pl.pallas_call(
        body, out_shape=(SDS((t, D_SSM), BF16), SDS((1, D_SSM), F32)), grid=(t // tm,),
        in_specs=[BS((tm, D_SSM), lambda i: (i, D_POOL // D_SSM)), blk, blk, blk, vec],
        out_specs=(blk, vec), name="ssm_ds", compiler_params=_params("arbitrary"))(proj, d_yt, du_fwd, du_bwd, d_skip)


GP_BLOCK = (D_POOL + D_SSM) // 256
GS_BLOCK = GP_BLOCK + D_MODEL // 256


def _merge_specs(tm):
    return [BS((tm, D_POOL), lambda s, i: (i, 0)), BS((tm, D_SSM), lambda s, i: (i, 0)),
            BS((None, D_POOL, 256), lambda s, i: (s, 0, 0)), BS((None, D_SSM, 256), lambda s, i: (s, 2, 0)),
            BS((None, D_SSM, 256), lambda s, i: (s, 3, 0)),
            BS((tm, 256), lambda s, i: (i, GP_BLOCK + s)), BS((tm, 256), lambda s, i: (i, GS_BLOCK + s))]


def _mixer_merge(ms, yssm, w_e, proj, tm):
    t = ms.shape[0]

    def body(ms_ref, y_ref, wpp_ref, wgv_ref, wgg_ref, gp_ref, gs_ref, o_ref):
        zp = _dot(ms_ref[...], wpp_ref[...])
        yv = y_ref[...]
        zv = _dot(yv, wgv_ref[...])
        zg = _dot(yv, wgg_ref[...])
        o_ref[...] = (_sigmoid(gp_ref[...]) * zp + _sigmoid(gs_ref[...]) * zv * _sigmoid(zg)).astype(BF16)

    col = BS((tm, 256), lambda s, i: (i, s))
    return pl.pallas_call(
        body, out_shape=SDS((t, D_MODEL), BF16), grid=(N_SHARD, t // tm), in_specs=_merge_specs(tm), out_specs=col,
        name="mixer_merge", compiler_params=_params("parallel", "parallel"))(ms, yssm, w_e, w_e, w_e, proj, proj)


def _mixer_merge_bwd(ms, yssm, w_e, proj, dmerged, tm):
    t = ms.shape[0]

    def body(ms_ref, y_ref, wpp_ref, wgv_ref, wgg_ref, gp_ref, gs_ref, dm_ref,
             dgp_ref, dgs_ref, dzp_ref, dzv_ref, dzg_ref):
        zp = _dot(ms_ref[...], wpp_ref[...])
        yv = y_ref[...]
        zv = _dot(yv, wgv_ref[...])
        zg = _dot(yv, wgg_ref[...])
        dm = dm_ref[...].astype(F32)
        sp, ss, sg = _sigmoid(gp_ref[...]), _sigmoid(gs_ref[...]), _sigmoid(zg)
        dgp_ref[...] = (dm * zp * sp * (1.0 - sp)).astype(BF16)
        dgs_ref[...] = (dm * zv * sg * ss * (1.0 - ss)).astype(BF16)
        dzp_ref[...] = (dm * sp).astype(BF16)
        dz = dm * ss
        dzv_ref[...] = (dz * sg).astype(BF16)
        dzg_ref[...] = (dz * zv * sg * (1.0 - sg)).astype(BF16)

    col = BS((tm, 256), lambda s, i: (i, s))
    shape = SDS((t, D_MODEL), BF16)
    return pl.pallas_call(
        body, out_shape=(shape,) * 5, grid=(N_SHARD, t // tm), in_specs=_merge_specs(tm) + [col],
        out_specs=(col,) * 5, name="mixer_merge_bwd",
        compiler_params=_params("parallel", "parallel"))(ms, yssm, w_e, w_e, w_e, proj, proj, dmerged)


def _mixer_dw(ms, yssm, dzp, dzv, dzg, tm):
    t = ms.shape[0]
    n_t = t // tm

    def body(ms_ref, y_ref, dzp_ref, dzv_ref, dzg_ref, o_ref, acc):
        i = pl.program_id(1)

        @pl.when(i == 0)
        def _():
            acc[...] = jnp.zeros_like(acc)

        yv = y_ref[...]
        acc[0:D_POOL, :] += _dot(ms_ref[...], dzp_ref[...], TN)
        acc[D_POOL:D_POOL + D_SSM, :] += _dot(yv, dzv_ref[...], TN)
        acc[D_POOL + D_SSM:, :] += _dot(yv, dzg_ref[...], TN)

        @pl.when(i == n_t - 1)
        def _():
            o_ref[...] = acc[...].astype(BF16)

    col = BS((tm, 256), lambda s, i: (i, s))
    return pl.pallas_call(
        body, out_shape=SDS((N_SHARD, 1024, 256), BF16), grid=(N_SHARD, n_t),
        in_specs=[BS((tm, D_POOL), lambda s, i: (i, 0)), BS((tm, D_SSM), lambda s, i: (i, 0)), col, col, col],
        out_specs=BS((None, 1024, 256), lambda s, i: (s, 0, 0)), scratch_shapes=[pltpu.VMEM((1024, 256), F32)],
        name="mixer_dw", compiler_params=_params("parallel", "arbitrary"))(ms, yssm, dzp, dzv, dzg)


def _mixer_dx(dzp, dzv, dzg, w_e, y_total, tm):
    t = dzp.shape[0]

    def body(dzp_ref, dzv_ref, dzg_ref, wpp_ref, wgv_ref, wgg_ref, yt_ref, dms_ref, dy_ref, acc_ms, acc_y):
        s = pl.program_id(1)

        @pl.when(s == 0)
        def _():
            acc_ms[...] = jnp.zeros_like(acc_ms)
            acc_y[...] = jnp.zeros_like(acc_y)

        acc_ms[...] += _dot(dzp_ref[...], wpp_ref[...], NT)
        acc_y[...] += _dot(dzv_ref[...], wgv_ref[...], NT) + _dot(dzg_ref[...], wgg_ref[...], NT)

        @pl.when(s == N_SHARD - 1)
        def _():
            dms_ref[...] = acc_ms[...].astype(BF16)
            y = yt_ref[...]
            inner = GELU_C * (y + GELU_K * y * y * y)
            th = jnp.tanh(inner)
            dgelu = 0.5 * (1.0 + th) + 0.5 * y * (1.0 - th * th) * GELU_C * (1.0 + 3.0 * GELU_K * y * y)
            dy_ref[...] = acc_y[...] * dgelu

    col = BS((tm, 256), lambda i, s: (i, s))
    return pl.pallas_call(
        body, out_shape=(SDS((t, D_POOL), BF16), SDS((t, D_SSM), F32)), grid=(t // tm, N_SHARD),
        in_specs=[col, col, col, BS((None, D_POOL, 256), lambda i, s: (s, 0, 0)),
                  BS((None, D_SSM, 256), lambda i, s: (s, 2, 0)), BS((None, D_SSM, 256), lambda i, s: (s, 3, 0)),
                  BS((tm, D_SSM), lambda i, s: (i, 0))],
        out_specs=(BS((tm, D_POOL), lambda i, s: (i, 0)), BS((tm, D_SSM), lambda i, s: (i, 0))),
        scratch_shapes=[pltpu.VMEM((tm, D_POOL), F32), pltpu.VMEM((tm, D_SSM), F32)],
        name="mixer_dx", compiler_params=_params("parallel", "arbitrary"))(dzp, dzv, dzg, w_e, w_e, w_e, y_total)


def _attn_probs(q_h, k_h):
    s = _dot(q_h, k_h, NT) * (1.0 / math.sqrt(HEAD_DIM))
    e = jnp.exp(s - jnp.max(s, axis=-1, keepdims=True))
    return e / jnp.sum(e, axis=-1, keepdims=True)


def _attn_fwd(q, kv, tm):
    t = q.shape[0]
    m = kv.shape[0]

    def body(q_ref, kv_ref, o_ref):
        for hd in range(N_HEADS):
            lo = hd * HEAD_DIM
            p = _attn_probs(q_ref[:, lo:lo + HEAD_DIM], kv_ref[:, lo:lo + HEAD_DIM])
            o_ref[:, lo:lo + HEAD_DIM] = _dot(p, kv_ref[:, D_MODEL + lo:D_MODEL + lo + HEAD_DIM]).astype(BF16)

    return pl.pallas_call(
        body, out_shape=SDS((t, D_MODEL), BF16), grid=(t // tm,),
        in_specs=[BS((tm, D_MODEL), lambda i: (i, 0)), BS((m, 2 * D_MODEL), lambda i: (0, 0))],
        out_specs=BS((tm, D_MODEL), lambda i: (i, 0)), name="attn_fwd", compiler_params=_params("parallel"))(q, kv)


def _attn_bwd(q, kv, d_o, tm):
    t = q.shape[0]
    m = kv.shape[0]

    def body(q_ref, kv_ref, do_ref, dq_ref, dkv_ref):
        i = pl.program_id(0)

        @pl.when(i == 0)
        def _():
            dkv_ref[...] = jnp.zeros_like(dkv_ref)

        for hd in range(N_HEADS):
            lo = hd * HEAD_DIM
            q_h = q_ref[:, lo:lo + HEAD_DIM]
            k_h = kv_ref[:, lo:lo + HEAD_DIM]
            v_h = kv_ref[:, D_MODEL + lo:D_MODEL + lo + HEAD_DIM]
            do_h = do_ref[:, lo:lo + HEAD_DIM]
            p = _attn_probs(q_h, k_h)
            dkv_ref[:, D_MODEL + lo:D_MODEL + lo + HEAD_DIM] += _dot(p, do_h, TN)
            dp = _dot(do_h, v_h, NT)
            ds = p * (dp - jnp.sum(dp * p, axis=-1, keepdims=True)) * (1.0 / math.sqrt(HEAD_DIM))
            dq_ref[:, lo:lo + HEAD_DIM] = _dot(ds, k_h).astype(BF16)
            dkv_ref[:, lo:lo + HEAD_DIM] += _dot(ds, q_h, TN)

    row = BS((tm, D_MODEL), lambda i: (i, 0))
    full = BS((m, 2 * D_MODEL), lambda i: (0, 0))
    return pl.pallas_call(
        body, out_shape=(SDS((t, D_MODEL), BF16), SDS((m, 2 * D_MODEL), F32)), grid=(t // tm,),
        in_specs=[row, full, row], out_specs=(row, full), name="attn_bwd",
        compiler_params=_params("arbitrary"))(q, kv, d_o)


TRANSPOSED = ("ffn1_w_gate", "ffn1_w_up", "ffn2_w_gate", "ffn2_w_up", "w_in")
GATHER_PHASES = {"f1a": (("ffn1_w_gate", "ffn1_w_up"),),
                 "f1b": (("ffn1_w_down",),),
                 "win": (("w_in",),),
                 "mix": (("w_mix_out", "w_q", "w_xo"), ("w_kv",), ("w_pool_proj", "w_glu_val", "w_glu_gate")),
                 "f2": (("ffn2_w_gate", "ffn2_w_up", "ffn2_w_down"),)}
REDUCE_GROUPS = (("ffn2_w_gate", "ffn2_w_up", "ffn2_w_down"), ("w_xo",), ("w_q",), ("w_kv",), ("w_mix_out",),
                 ("w_pool_proj", "w_glu_val", "w_glu_gate"), ("w_in",), ("ffn1_w_gate", "ffn1_w_up", "ffn1_w_down"))
SMALL = ("ffn1_norm", "mix_norm", "pool_w", "pool_scale", "ssm_a_re", "ssm_a_im", "ssm_log_dt", "ssm_b_re",
         "ssm_b_im", "ssm_c_re", "ssm_c_im", "ssm_d", "xattn_norm", "mem_norm", "ffn2_norm", "final_norm")
WEIGHTS = ("ffn1_norm", "ffn1_w_gate", "ffn1_w_up", "ffn1_w_down", "mix_norm", "w_in", "pool_w", "pool_scale",
           "w_pool_proj", "ssm_a_re", "ssm_a_im", "ssm_log_dt", "ssm_b_re", "ssm_b_im", "ssm_c_re", "ssm_c_im",
           "ssm_d", "w_glu_val", "w_glu_gate", "w_mix_out", "xattn_norm", "mem_norm", "w_q", "w_kv", "w_xo",
           "ffn2_norm", "ffn2_w_gate", "ffn2_w_up", "ffn2_w_down", "final_norm")


def _block_diag_in(bb):
    eye = jnp.eye(SSM_GROUPS, dtype=bb.dtype)
    return jnp.einsum("dgph,gk->dghkp", bb, eye).reshape(2, D_SSM, SSM_CH)


def _block_diag_out(cc):
    eye = jnp.eye(SSM_GROUPS, dtype=cc.dtype)
    return jnp.einsum("dghp,gk->dgpkh", cc, eye).reshape(2, SSM_CH, D_SSM)


def _diag_blocks_in(m):
    return jnp.einsum("dghgp->dgph", m.reshape(2, SSM_GROUPS, SSM_GROUP, SSM_GROUPS, SSM_STATE))


def _diag_blocks_out(m):
    return jnp.einsum("dgpgh->dghp", m.reshape(2, SSM_GROUPS, SSM_STATE, SSM_GROUPS, SSM_GROUP))


def _device_step(x, mem, target, wts, sp, reducer=None):
    t = x.shape[0]
    tm = min(TM, t)
    g = {}

    first_gather = wts.start("f1a")
    u1 = _rmsnorm("norm_ffn1", x, sp["ffn1_norm"], tm, after=first_gather)

    ar = sp["ssm_a_re"].reshape(2 * SSM_GROUPS, SSM_STATE)
    ai = sp["ssm_a_im"].reshape(2 * SSM_GROUPS, SSM_STATE)
    ldt = sp["ssm_log_dt"].reshape(2 * SSM_GROUPS, 1)
    abr, abi, qr, qi = _ssm_disc(ar, ai, ldt, after=first_gather)
    b_r = sp["ssm_b_re"].reshape(2 * SSM_CH, SSM_GROUP)
    b_i = sp["ssm_b_im"].reshape(2 * SSM_CH, SSM_GROUP)
    qr_col, qi_col = qr.reshape(2 * SSM_CH, 1), qi.reshape(2 * SSM_CH, 1)
    bbr, bbi = _ssm_bbar(qr_col, qi_col, b_r, b_i)
    shape_b = (2, SSM_GROUPS, SSM_STATE, SSM_GROUP)
    b_mat = jnp.concatenate([_block_diag_in(bbr.reshape(shape_b)), _block_diag_in(bbi.reshape(shape_b))], axis=-1).astype(BF16)
    c_mat = jnp.concatenate([_block_diag_out(sp["ssm_c_re"][0]), -_block_diag_out(sp["ssm_c_im"][0])], axis=1).astype(BF16)
    b_mat_t = jnp.swapaxes(b_mat, 1, 2)
    c_mat_t = jnp.swapaxes(c_mat, 1, 2)
    a_r = abr.reshape(2, 1, SSM_CH)
    a_i = abi.reshape(2, 1, SSM_CH)
    mem_n = _rmsnorm("norm_mem", mem, sp["mem_norm"], mem.shape[0], after=first_gather)

    (w_gu,) = wts.finish("f1a", [u1, b_mat, c_mat, b_mat_t, c_mat_t, mem_n])
    w_f1 = {"gate": (w_gu, FFN_GATE), "up": (w_gu, FFN_UP)}
    g1, up1, a1 = _ffn_up("ffn1_up", u1, w_f1, tm, after=wts.start("f1b", [w_gu]))
    (w_dn,) = wts.finish("f1b", [a1])
    w_f1["down"] = (w_dn, 0)
    h1 = _ffn_down("ffn1_down", a1, w_f1, x, tm, after=wts.start("win", [w_dn]))

    u2 = _rmsnorm("norm_mix", h1, sp["mix_norm"], tm)
    (w_in_g,) = wts.finish("win", [u2])
    w_in_t = w_in_g.reshape(D_FF, D_MODEL)
    proj = _mm("mix_in", [(u2, BS((tm, D_MODEL), lambda j, i: (i, 0)), w_in_t, BS((D_FF // 2, D_MODEL), lambda j, i: (j, 0)), NT)],
               grid=(2, t // tm), out_shape=SDS((t, D_FF), F32), out_spec=BS((tm, D_FF // 2), lambda j, i: (i, j)),
               after=wts.start("mix", [w_in_g]))
    pooled, mixed, ms = _pool_fwd(proj, sp["pool_w"][0], sp["pool_scale"])

    s_in = proj[:, D_POOL:D_POOL + D_SSM].astype(BF16)
    states, y_dirs = [], []
    for dr in range(2):
        st, yd = _ssm_scan(f"ssm_scan_fwd{dr}", s_in, b_mat[dr], a_r[dr], a_i[dr], c_mat[dr], reverse=(dr == 1))
        states.append(st)
        y_dirs.append(yd)
    w_sq, w_kv, w_e = wts.finish("mix", y_dirs)
    w_mo, w_q, w_xo = (w_sq[:, 256 * k:256 * (k + 1)].reshape(D_MODEL, D_MODEL) for k in range(3))
    w_d = w_kv[:, None]
    y_total, yssm = _ssm_combine(proj, y_dirs[0], y_dirs[1], sp["ssm_d"], tm, after=wts.start("f2", [w_e]))

    merged = _mixer_merge(ms, yssm, w_e, proj, tm)
    h2 = _plain_mm("mix_out", merged, w_mo, NN, F32, tm, resid=h1)

    u3 = _rmsnorm("norm_xattn", h2, sp["xattn_norm"], tm)
    q =_plain_mm("attn_q", u3, w_q, NN, BF16, tm)
    n_mem = mem.shape[0]
    kv = _mm("attn_kv", [(mem_n, BS((n_mem, D_MODEL), lambda s: (0, 0)), w_d, BS((None, None, D_MODEL, 512), lambda s: (s, 0, 0, 0)), NN)],
             grid=(N_SHARD,), out_shape=SDS((n_mem, 2 * D_MODEL), BF16), out_spec=BS((n_mem, 512), lambda s: (0, s)))
    o = _attn_fwd(q, kv, tm)
    h3 = _plain_mm("attn_out", o, w_xo, NN, F32, tm, resid=h2)

    u4 = _rmsnorm("norm_ffn2", h3, sp["ffn2_norm"], tm)
    (w_2,) = wts.finish("f2", [u4])
    w_f2 = {"gate": (w_2, FFN_GATE), "up": (w_2, FFN_UP), "down": (w_2, FFN_DOWN)}
    g2, up2, a2 = _ffn_up("ffn2_up", u4, w_f2, tm)
    h4 = _ffn_down("ffn2_down", a2, w_f2, h3, tm)

    loss, dh4, dh4_b, g["final_norm"] = _loss_head(h4, sp["final_norm"].reshape(1, D_MODEL), target, tm)

    dg2, dup2 = _ffn_bwd_act("ffn2_bwd_act", dh4_b, w_f2, g2, up2, tm)
    dw_f2 = _ffn_dw("ffn2_dw", u4, dg2, dup2, a2, dh4_b, tm)
    du4 = _ffn_dx("ffn2_dx", dg2, dup2, w_f2, tm)
    dh3, dh3_b, g["ffn2_norm"] = _rmsnorm_bwd("norm_ffn2_bwd", h3, sp["ffn2_norm"], du4, dh4, tm)

    d_o = _plain_mm("attn_out_dx", dh3_b, w_xo, NT, BF16, tm)
    dw_xo = _dw_mm("attn_out_dw", o, dh3_b, tm)
    dq, dkv = _attn_bwd(q, kv, d_o, tm)
    dw_q = _dw_mm("attn_q_dw", u3, dq, tm)
    du3 = _plain_mm("attn_q_dx", dq, w_q, NT, F32, tm)
    dw_kv = _mm("attn_kv_dw", [(mem_n, BS((n_mem, D_MODEL), lambda s: (0, 0)), dkv, BS((n_mem, 512), lambda s: (0, s)), TN)],
                grid=(N_SHARD,), out_shape=SDS((N_SHARD, D_MODEL, 512), BF16), out_spec=BS((None, D_MODEL, 512), lambda s: (s, 0, 0)))
    dmem_n = _mm("attn_kv_dx", [(dkv, BS((n_mem, 512), lambda s: (0, s)), w_d, BS((None, None, D_MODEL, 512), lambda s: (s, 0, 0, 0)), NT)],
                 grid=(N_SHARD,), red_axis=0, out_shape=SDS((n_mem, D_MODEL), F32), out_spec=BS((n_mem, D_MODEL), lambda s: (0, 0)))
    _, _, g["mem_norm"] = _rmsnorm_bwd("norm_mem_bwd", mem, sp["mem_norm"], dmem_n, None, n_mem)
    dh2, dh2_b, g["xattn_norm"] = _rmsnorm_bwd("norm_xattn_bwd", h2, sp["xattn_norm"], du3, dh3, tm)

    square = (N_SHARD, D_MODEL // N_SHARD, D_MODEL)
    early = [dw_f2.reshape(N_SHARD, 3 * FF_SH, D_MODEL), dw_xo.reshape(square), dw_q.reshape(square), dw_kv]
    swapping = reducer.swap_start("a1", early) if reducer is not None else []
    dmerged = _plain_mm("mix_out_dx", dh2_b, w_mo, NT, BF16, tm, after=swapping)
    dw_mo = _dw_mm("mix_out_dw", merged, dh2_b, tm)
    d_gp, d_gs, dzp, dzv, dzg = _mixer_merge_bwd(ms, yssm, w_e, proj, dmerged, tm)
    dw_e = _mixer_dw(ms, yssm, dzp, dzv, dzg, tm)
    d_ms, d_yt = _mixer_dx(dzp, dzv, dzg, w_e, y_total, tm)
    dp, d_scale, d_pw = _pool_bwd(d_ms, mixed, pooled, sp["pool_w"][0], sp["pool_scale"])
    g["pool_scale"] = d_scale
    g["pool_w"] = d_pw[None]

    d_yt_b = d_yt.astype(BF16)
    du_dirs, d_abr, d_abi, d_cm, d_bm = [], [], [], [], []
    for dr in range(2):
        lam, du = _ssm_scan(f"ssm_scan_bwd{dr}", d_yt_b, c_mat_t[dr], a_r[dr], -a_i[dr], b_mat_t[dr], reverse=(dr == 0))
        du_dirs.append(du)
        da_r, da_i = _ssm_da(f"ssm_da{dr}", lam, states[dr], reverse=(dr == 1))
        d_abr.append(da_r)
        d_abi.append(da_i)
        d_cm.append(_dw_mm(f"ssm_dc{dr}", states[dr], d_yt_b, tm, F32))
        d_bm.append(_dw_mm(f"ssm_db{dr}", s_in, lam, tm, F32))
    d_cm = jnp.stack(d_cm)
    d_bm = jnp.stack(d_bm)
    g["ssm_c_re"] = _diag_blocks_out(d_cm[:, :SSM_CH])[None]
    g["ssm_c_im"] = -_diag_blocks_out(d_cm[:, SSM_CH:])[None]
    g_bbr = _diag_blocks_in(d_bm[:, :, :SSM_CH]).reshape(2 * SSM_CH, SSM_GROUP)
    g_bbi = _diag_blocks_in(d_bm[:, :, SSM_CH:]).reshape(2 * SSM_CH, SSM_GROUP)
    d_qr, d_qi, d_br, d_bi = _ssm_bbar_bwd(qr_col, qi_col, b_r, b_i, g_bbr, g_bbi)
    g["ssm_b_re"] = d_br.reshape(sp["ssm_b_re"].shape)
    g["ssm_b_im"] = d_bi.reshape(sp["ssm_b_im"].shape)
    d_ar, d_ai, d_ldt = _ssm_disc_bwd(ar, ai, ldt, jnp.stack(d_abr).reshape(ar.shape), jnp.stack(d_abi).reshape(ar.shape),
                                      d_qr.reshape(ar.shape), d_qi.reshape(ar.shape))
    g["ssm_a_re"] = d_ar.reshape(sp["ssm_a_re"].shape)
    g["ssm_a_im"] = d_ai.reshape(sp["ssm_a_im"].shape)
    g["ssm_log_dt"] = d_ldt.reshape(sp["ssm_log_dt"].shape)
    ds, g["ssm_d"] = _ssm_ds(proj, d_yt, du_dirs[0], du_dirs[1], sp["ssm_d"], tm)

    d_proj = jnp.concatenate([dp, ds, d_gp, d_gs], axis=1)
    dw_in_t = _mm("mix_in_dw", [(d_proj, BS((tm, D_FF // 2), lambda j, i: (i, j)), u2, BS((tm, D_MODEL), lambda j, i: (i, 0)), TN)],
                  grid=(2, t // tm), red_axis=1, out_shape=SDS((D_FF, D_MODEL), BF16), out_spec=BS((D_FF // 2, D_MODEL), lambda j, i: (j, 0)))
    du2 = _plain_mm("mix_in_dx", d_proj, w_in_t, NN, F32, tm)
    dh1, dh1_b, g["mix_norm"] = _rmsnorm_bwd("norm_mix_bwd", h1, sp["mix_norm"], du2, dh2, tm)

    early += [dw_mo.reshape(square), dw_e, dw_in_t.reshape(N_SHARD, FF_SH, D_MODEL)]
    g["final_norm"] = g["final_norm"].reshape(D_MODEL)

    travelling = reducer.start("a", early[4:], swapped=["a1"], after=list(g.values())) if reducer is not None else []
    dg1, dup1 = _ffn_bwd_act("ffn1_bwd_act", dh1_b, w_f1, g1, up1, tm, after=travelling)
    dw_f1 = _ffn_dw("ffn1_dw", u1, dg1, dup1, a1, dh1_b, tm).reshape(N_SHARD, 3 * FF_SH, D_MODEL)
    if reducer is not None:
        travelling = reducer.start("b", [dw_f1], after=reducer.finish("a", [dw_f1]))
        travelling = travelling + reducer.join_start("a", after=travelling)
    du1 = _ffn_dx("ffn1_dx", dg1, dup1, w_f1, tm, after=travelling)
    grad_x, _, g["ffn1_norm"] = _rmsnorm_bwd("norm_ffn1_bwd", x, sp["ffn1_norm"], du1, dh1, tm)
    if reducer is not None:
        reducer.finish("b", [grad_x])
        reducer.join_finish("a", [grad_x])
    return loss, grad_x, early + [dw_f1], g


def _mesh_place():
    x, y, c = lax.axis_index("x"), lax.axis_index("y"), lax.axis_index("c")
    chips = [(1 - x, y), (x, 1 - y), (1 - x, 1 - y)]
    return x, y, c, chips


def _remote(src, dst, send_sems, recv_sems, k, to):
    return pltpu.make_async_remote_copy(src_ref=src, dst_ref=dst, send_sem=send_sems.at[k], recv_sem=recv_sems.at[k],
                                        device_id=to, device_id_type=MESH)


def _sibling_swap_halves(tag, grads, after=()):
    n = len(grads)
    after_ops, after_specs = _after_operands(after)

    def body(*refs):
        ins, outs = refs[:n], refs[n + len(after_ops):2 * n + len(after_ops)]
        send_sems, recv_sems = refs[2 * n + len(after_ops):]
        x, y, c, _ = _mesh_place()
        sibling = (x, y, 1 - c)
        copies = []
        for k in range(n):
            half = grads[k].shape[1] // 2
            theirs = pl.ds(pl.multiple_of((1 - c) * half, 16), half)
            cp = _remote(ins[k].at[:, theirs, :], outs[k], send_sems, recv_sems, k, sibling)
            cp.start()
            copies.append(cp)
        for cp in copies:
            cp.wait_recv()
        for cp in copies:
            cp.wait_send()

    hbm = BS(memory_space=pl.ANY)
    return pl.pallas_call(
        body, out_shape=tuple(SDS((g.shape[0], g.shape[1] // 2, g.shape[2]), g.dtype) for g in grads),
        in_specs=[hbm] * n + after_specs, out_specs=(hbm,) * n,
        scratch_shapes=[pltpu.SemaphoreType.DMA((n,)), pltpu.SemaphoreType.DMA((n,))],
        name="reduce_sibling_send_" + tag, compiler_params=_params())(*grads, *after_ops)


def _row_tile(rows, cap=512):
    return max(r for r in range(16, cap + 1, 16) if rows % r == 0)


def _chip_presum(k, grad, got, c_idx):
    n_sh, rows, cols = grad.shape
    half = rows // 2
    tr = _row_tile(half)
    grad4 = grad.reshape(n_sh, 2, half, cols)

    def body(c_ref, a_ref, b_ref, o_ref):
        o_ref[...] = (a_ref[...].astype(F32) + b_ref[...].astype(F32)).astype(o_ref.dtype)

    return pl.pallas_call(
        body, out_shape=SDS((n_sh, half, cols), BF16),
        grid_spec=pltpu.PrefetchScalarGridSpec(
            num_scalar_prefetch=1, grid=(n_sh, half // tr),
            in_specs=[BS((None, None, tr, cols), lambda s, i, c_ref: (s, c_ref[0], i, 0)),
                      BS((None, tr, cols), lambda s, i, c_ref: (s, i, 0))],
            out_specs=BS((None, tr, cols), lambda s, i, c_ref: (s, i, 0))),
        name=f"reduce_presum{k}", compiler_params=_params("parallel", "parallel"))(c_idx, grad4, got)


HBM_SPEC = BS(memory_space=pltpu.HBM)
SEM_SPEC = BS(memory_space=pltpu.SEMAPHORE)
DATAFLOW = pltpu.SideEffectType.DATAFLOW_SIDE_EFFECTING


def _chip_exchange_copies(parts, lands, send_sems, recv_sems):
    _, _, c, chips = _mesh_place()
    return [_remote(parts[k].at[2 * px + py], lands[k].at[j], send_sems, recv_sems, 3 * k + j, (px, py, c))
            for k in range(len(parts)) for j, (px, py) in enumerate(chips)]


def _gather_copies(shards, lands, send_sems, recv_sems):
    x, y, c, chips = _mesh_place()
    return [_remote(shards[k], lands[k].at[2 * x + y], send_sems, recv_sems, 3 * k + j, (px, py, c))
            for k in range(len(shards)) for j, (px, py) in enumerate(chips)]


def _swap_copies(grads, lands, send_sems, recv_sems):
    x, y, c, _ = _mesh_place()
    out = []
    for k in range(len(grads)):
        half = grads[k].shape[1] // 2
        theirs = pl.ds(pl.multiple_of((1 - c) * half, 16), half)
        out.append(_remote(grads[k].at[:, theirs, :], lands[k], send_sems, recv_sems, k, (x, y, 1 - c)))
    return out


def _join_copies(fulls, same, send_sems, recv_sems):
    x, y, c, _ = _mesh_place()
    out = []
    for k in range(len(fulls)):
        half = fulls[k].shape[0] // 2
        mine = fulls[k].at[pl.ds(pl.multiple_of(c * half, 8), half), :]
        out.append(_remote(mine, mine, send_sems, recv_sems, k, (x, y, 1 - c)))
    return out


def _everyone_copies(packs, lands, send_sems, recv_sems):
    x, y, c, _ = _mesh_place()
    out = []
    for k in range(len(packs)):
        for j in range(N_DEV - 1):
            bx, by, bc = (j + 1) >> 2 & 1, (j + 1) >> 1 & 1, (j + 1) & 1
            peer = (x ^ bx, y ^ by, c ^ bc)
            out.append(_remote(packs[k], lands[k].at[4 * x + 2 * y + c], send_sems, recv_sems, (N_DEV - 1) * k + j, peer))
    return out


def _split_start(name, copies, sources, land_shapes, after=(), fanout=3):
    n = len(sources)
    n_land = len(land_shapes)
    m = n + n_land
    n_sems = fanout * n
    after_ops, after_specs = _after_operands(after)

    def body(*refs):
        ins = refs[:n]
        lands = refs[n:m] if n_land else ins
        send_sems, recv_sems = refs[m + len(after_ops)], refs[m + len(after_ops) + 1]
        token = refs[-1]
        for cp in copies(ins, lands, send_sems, recv_sems):
            cp.start()
        token[...] = jnp.zeros_like(token)

    lands = [pltpu.with_memory_space_constraint(lax.empty(s, d), pltpu.HBM) for s, d in land_shapes]
    sources = [pltpu.with_memory_space_constraint(p, pltpu.HBM) for p in sources]
    thru = [pltpu.HBM(a.shape, a.dtype) for a in sources + lands]
    out = pl.pallas_call(
        body, name=name,
        out_shape=(pltpu.SemaphoreType.DMA((n_sems,)), pltpu.SemaphoreType.DMA((n_sems,)), *thru, SDS((8, 128), F32)),
        in_specs=[HBM_SPEC] * m + after_specs,
        out_specs=(SEM_SPEC, SEM_SPEC, *[HBM_SPEC] * m, BS(memory_space=pltpu.VMEM)),
        input_output_aliases={i: 2 + i for i in range(m)},
        compiler_params=pltpu.CompilerParams(has_side_effects=DATAFLOW))(*sources, *lands, *after_ops)
    return out[0], out[1], list(out[2:2 + n]), list(out[2 + n:2 + m]), out[-1]


def _split_wait(name, copies, send_sems, recv_sems, sources, lands, after):
    n = len(sources)
    m = n + len(lands)
    after_ops, after_specs = _after_operands(after)

    def body(*refs):
        ins = refs[:n]
        zones = refs[n:m] if m > n else ins
        for cp in copies(ins, zones, refs[m], refs[m + 1]):
            cp.wait_send()
            cp.wait_recv()

    out = pl.pallas_call(
        body, name=name,
        out_shape=tuple(pltpu.HBM(a.shape, a.dtype) for a in sources + lands),
        in_specs=[HBM_SPEC] * m + [SEM_SPEC, SEM_SPEC] + after_specs, out_specs=(HBM_SPEC,) * m,
        input_output_aliases={i: i for i in range(m)},
        compiler_params=pltpu.CompilerParams(has_side_effects=DATAFLOW))(*sources, *lands, send_sems, recv_sems, *after_ops)
    return list(out[:n]), list(out[n:])


class _WeightGatherer:
    def __init__(self, shards):
        self.shards, self.open = shards, {}
        self.me = 2 * lax.axis_index("x") + lax.axis_index("y")

    def start(self, tag, after=()):
        shapes = [((N_SHARD,) + s.shape, s.dtype) for s in self.shards[tag]]
        self.open[tag] = _split_start("gather_start_" + tag, _gather_copies, self.shards[tag], shapes, after)
        return [self.open[tag][-1]]

    def finish(self, tag, after):
        send_sems, recv_sems, shards, lands, _ = self.open.pop(tag)
        shards, lands = _split_wait("gather_wait_" + tag, _gather_copies, send_sems, recv_sems, shards, lands, after)
        return [lax.dynamic_update_slice(zone, s[None], (self.me, 0, 0)) for zone, s in zip(lands, shards)]


class _GradReducer:
    def __init__(self):
        self.c_idx = lax.axis_index("c").astype(jnp.int32).reshape(1)
        self.place = jnp.stack([2 * lax.axis_index("x") + lax.axis_index("y"), lax.axis_index("c")]).astype(jnp.int32)
        self.swaps, self.open, self.landed, self.joins, self.reduced = {}, {}, {}, {}, []

    def swap_start(self, tag, grads, after=()):
        shapes = [((g.shape[0], g.shape[1] // 2, g.shape[2]), g.dtype) for g in grads]
        self.swaps[tag] = _split_start("reduce_swap_start_" + tag, _swap_copies, grads, shapes, after, fanout=1)
        return [self.swaps[tag][-1]]

    def start(self, tag, grads, after=(), swapped=()):
        pairs = []
        for s in swapped:
            send_sems, recv_sems, early, lands, _ = self.swaps.pop(s)
            pairs += zip(*_split_wait("reduce_swap_wait_" + s, _swap_copies, send_sems, recv_sems, early, lands, grads[-1:]))
        pairs += zip(grads, _sibling_swap_halves(tag, grads, after))
        parts = [_chip_presum(f"{tag}{k}", g, s, self.c_idx) for k, (g, s) in enumerate(pairs)]
        shapes = [((3,) + p.shape[1:], p.dtype) for p in parts]
        self.open[tag] = _split_start("reduce_exchange_start_" + tag, _chip_exchange_copies, parts, shapes)
        return [self.open[tag][-1]]

    def finish(self, tag, after):
        send_sems, recv_sems, parts, lands, _ = self.open.pop(tag)
        self.landed[tag] = _split_wait("reduce_exchange_wait_" + tag, _chip_exchange_copies, send_sems, recv_sems, parts, lands, after)
        return self.landed[tag][1][:1]

    def _sums(self, tag, after=()):
        parts, landed = self.landed.pop(tag)
        return [_chip_sum(f"{tag}{k}", p, got, self.place, after) for k, (p, got) in enumerate(zip(parts, landed))]

    def join_start(self, tag, after=()):
        self.joins[tag] = _split_start("reduce_join_start_" + tag, _join_copies, self._sums(tag, after), [], fanout=1)
        return [self.joins[tag][-1]]

    def join_finish(self, tag, after):
        send_sems, recv_sems, fulls, _, _ = self.joins.pop(tag)
        self.reduced += _split_wait("reduce_join_wait_" + tag, _join_copies, send_sems, recv_sems, fulls, [], after)[0]

    def join(self, tag, after=()):
        self.reduced += _sibling_join_halves(self._sums(tag), after)


def _chip_sum(k, part, got, place, after=()):
    _, half, cols = part.shape
    tr = _row_tile(half)
    n_t = half // tr
    after_ops, after_specs = _after_operands(after)

    def body(place_ref, a_ref, b_ref, *rest):
        o_ref = rest[-1]
        acc = a_ref[...].astype(F32)
        for j in range(3):
            acc = acc + b_ref[j].astype(F32)
        o_ref[...] = acc

    return pl.pallas_call(
        body, out_shape=SDS((2 * half, cols), F32),
        grid_spec=pltpu.PrefetchScalarGridSpec(
            num_scalar_prefetch=1, grid=(n_t,),
            in_specs=[BS((None, tr, cols), lambda i, place_ref: (place_ref[0], i, 0)),
                      BS((3, tr, cols), lambda i, place_ref: (0, i, 0))] + after_specs,
            out_specs=BS((tr, cols), lambda i, place_ref: (place_ref[1] * n_t + i, 0))),
        name=f"reduce_sum{k}", compiler_params=_params("parallel"))(place, part, got, *after_ops)


def _sibling_join_halves(fulls, after=()):
    n = len(fulls)
    after_ops, after_specs = _after_operands(after)

    def body(*refs):
        outs = refs[n + len(after_ops):2 * n + len(after_ops)]
        send_sems, recv_sems = refs[2 * n + len(after_ops):]
        copies = _join_copies(outs, outs, send_sems, recv_sems)
        for cp in copies:
            cp.start()
        for cp in copies:
            cp.wait_recv()
        for cp in copies:
            cp.wait_send()

    hbm = BS(memory_space=pl.ANY)
    return list(pl.pallas_call(
        body, out_shape=tuple(SDS(f.shape, f.dtype) for f in fulls),
        in_specs=[hbm] * n + after_specs, out_specs=(hbm,) * n, input_output_aliases={k: k for k in range(n)},
        scratch_shapes=[pltpu.SemaphoreType.DMA((n,)), pltpu.SemaphoreType.DMA((n,))],
        name="reduce_sibling_join", compiler_params=_params())(*fulls, *after_ops))


N_DEV = 8


def _sum_devices(packs):
    _, rows, lanes = packs.shape

    def body(p_ref, o_ref):
        acc = p_ref[0]
        for dev in range(1, N_DEV):
            acc = acc + p_ref[dev]
        o_ref[...] = acc

    vm = BS(memory_space=pltpu.VMEM)
    return pl.pallas_call(body, out_shape=SDS((rows, lanes), F32), in_specs=[vm], out_specs=vm,
                          name="small_sum", compiler_params=_params())(packs)


def _adamw(name, w, grad, row0, m, v, after=()):
    rows, cols = w.shape
    tr = rows if rows < 16 else _row_tile(rows, 256)
    bc1 = 1.0 - ADAM_B1 ** ADAM_STEP
    bc2 = 1.0 - ADAM_B2 ** ADAM_STEP
    after_ops, after_specs = _after_operands(after)

    def body(w_ref, g_ref, m_ref, v_ref, *rest):
        go_ref, d_ref, mo_ref, vo_ref = rest[len(after_ops):]
        g = g_ref[...]
        m_new = ADAM_B1 * m_ref[...] + (1.0 - ADAM_B1) * g
        v_new = ADAM_B2 * v_ref[...] + (1.0 - ADAM_B2) * (g * g)
        go_ref[...] = g
        mo_ref[...] = m_new
        vo_ref[...] = v_new
        d_ref[...] = -ADAM_LR * ((m_new / bc1) / (jnp.sqrt(v_new / bc2) + ADAM_EPS) + ADAM_WD * w_ref[...])

    blk = BS((tr, cols), lambda i: (i, 0))
    shape = SDS((rows, cols), F32)
    return pl.pallas_call(
        body, out_shape=(shape,) * 4, grid=(rows // tr,),
        in_specs=[blk, BS((tr, cols), lambda i: (row0 // tr + i, 0)), blk, blk] + after_specs, out_specs=(blk,) * 4,
        name=name, compiler_params=_params("parallel"))(w, grad, m, v, *after_ops)


SMALL_LANES = 128


def _pack_small(parts):
    flat = jnp.concatenate([jnp.ravel(p) for p in parts])
    rows = -(-flat.shape[0] // (64 * SMALL_LANES)) * 64
    return jnp.pad(flat, (0, rows * SMALL_LANES - flat.shape[0])).reshape(rows, SMALL_LANES)


def _unpack_small(packed, like):
    flat = jnp.ravel(packed)
    out, at = [], 0
    for p in like:
        out.append(flat[at:at + p.size].reshape(p.shape))
        at += p.size
    return out


def kernel(x, mem, ffn1_norm, ffn1_w_gate, ffn1_w_up, ffn1_w_down, mix_norm, w_in, pool_w, pool_scale, w_pool_proj, ssm_a_re, ssm_a_im, ssm_log_dt, ssm_b_re, ssm_b_im, ssm_c_re, ssm_c_im, ssm_d, w_glu_val, w_glu_gate, w_mix_out, xattn_norm, mem_norm, w_q, w_kv, w_xo, ffn2_norm, ffn2_w_gate, ffn2_w_up, ffn2_w_down, final_norm, loss_target, m_ffn1_norm, m_ffn1_w_gate, m_ffn1_w_up, m_ffn1_w_down, m_mix_norm, m_w_in, m_pool_w, m_pool_scale, m_w_pool_proj, m_ssm_a_re, m_ssm_a_im, m_ssm_log_dt, m_ssm_b_re, m_ssm_b_im, m_ssm_c_re, m_ssm_c_im, m_ssm_d, m_w_glu_val, m_w_glu_gate, m_w_mix_out, m_xattn_norm, m_mem_norm, m_w_q, m_w_kv, m_w_xo, m_ffn2_norm, m_ffn2_w_gate, m_ffn2_w_up, m_ffn2_w_down, m_final_norm, v_ffn1_norm, v_ffn1_w_gate, v_ffn1_w_up, v_ffn1_w_down, v_mix_norm, v_w_in, v_pool_w, v_pool_scale, v_w_pool_proj, v_ssm_a_re, v_ssm_a_im, v_ssm_log_dt, v_ssm_b_re, v_ssm_b_im, v_ssm_c_re, v_ssm_c_im, v_ssm_d, v_w_glu_val, v_w_glu_gate, v_w_mix_out, v_xattn_norm, v_mem_norm, v_w_q, v_w_kv, v_w_xo, v_ffn2_norm, v_ffn2_w_gate, v_ffn2_w_up, v_ffn2_w_down, v_final_norm):
    given = dict(locals())
    w = {n: given[n] for n in WEIGHTS}
    m = {n: given["m_" + n] for n in WEIGHTS}
    v = {n: given["v_" + n] for n in WEIGHTS}

    def shard_view(a, n):
        return a[0].T if n in TRANSPOSED else a[0]

    def shard_unview(a, n):
        return (a.T if n in TRANSPOSED else a)[None]

    shards = {tag: [jnp.concatenate([shard_view(w[n], n).astype(BF16) for n in grp], axis=0) for grp in arrays]
              for tag, arrays in GATHER_PHASES.items()}
    reducer = _GradReducer()
    loss_part, grad_x, _, small = _device_step(x[0], mem[0], loss_target[0], _WeightGatherer(shards),
                                               {n: w[n] for n in SMALL}, reducer)
    loss = lax.psum(loss_part[0, 0], ("x", "y", "c"))

    pack = _pack_small([small[n] for n in SMALL])
    everyone = _split_start("small_start", _everyone_copies, [pack], [((N_DEV,) + pack.shape, F32)], fanout=N_DEV - 1)
    reducer.join("b", after=everyone[-1:])

    grads, delta, new_m, new_v = {}, {}, {}, {}
    for grp, red in zip(REDUCE_GROUPS, reducer.reduced):
        row0 = 0
        for n in grp:
            w_n = shard_view(w[n], n)
            outs = _adamw("adamw_" + n, w_n, red, row0, shard_view(m[n], n), shard_view(v[n], n), after=everyone[-1:])
            grads[n], delta[n], new_m[n], new_v[n] = (shard_unview(o, n) for o in outs)
            row0 += w_n.shape[0]

    send_sems, recv_sems, packs, landed, _ = everyone
    packs, landed = _split_wait("small_wait", _everyone_copies, send_sems, recv_sems, packs, landed,
                                [delta[n] for grp in REDUCE_GROUPS for n in grp])
    mine = 4 * lax.axis_index("x") + 2 * lax.axis_index("y") + lax.axis_index("c")
    summed = _sum_devices(lax.dynamic_update_slice(landed[0], packs[0][None], (mine, 0, 0)))
    g_small = dict(zip(SMALL, _unpack_small(summed, [w[n] for n in SMALL])))
    narrow = [n for n in SMALL if w[n].ndim > 3]
    dense = [n for n in SMALL if n not in narrow]
    for n in narrow:
        two_d = (-1, w[n].shape[-1])
        outs = _adamw("adamw_" + n, w[n].reshape(two_d), g_small[n].reshape(two_d), 0, m[n].reshape(two_d), v[n].reshape(two_d))
        grads[n], delta[n], new_m[n], new_v[n] = (o.reshape(w[n].shape) for o in outs)
    dense_like = [w[n] for n in dense]
    packed = _adamw("adamw_small", _pack_small(dense_like), _pack_small([g_small[n] for n in dense]), 0,
                    _pack_small([m[n] for n in dense]), _pack_small([v[n] for n in dense]))
    for out, store in zip(packed, (grads, delta, new_m, new_v)):
        for n, val in zip(dense, _unpack_small(out, dense_like)):
            store[n] = val

    return (loss, grad_x[None], *[grads[n] for n in WEIGHTS], *[delta[n] for n in WEIGHTS],
            *[new_m[n] for n in WEIGHTS], *[new_v[n] for n in WEIGHTS])
```

```python
import functools
import math

import jax
import jax.numpy as jnp
from jax import lax
from jax.experimental import pallas as pl
from jax.experimental.pallas import tpu as pltpu

F32 = jnp.float32
BF16 = jnp.bfloat16
SDS = jax.ShapeDtypeStruct
BS = pl.BlockSpec
MESH = pl.DeviceIdType.MESH

D_MODEL = 1024
D_FF = 2816
N_SHARD = 4
FF_SH = D_FF // N_SHARD
D_POOL = 512
POOL_WINDOWS = (2, 4, 8, 16)
POOL_GROUP = 128
D_SSM = 256
SSM_GROUPS = 16
SSM_GROUP = 16
SSM_STATE = 64
SSM_CH = SSM_GROUPS * SSM_STATE
N_HEADS = 4
HEAD_DIM = 256
EPS = 1e-6
ADAM_LR, ADAM_B1, ADAM_B2, ADAM_EPS, ADAM_WD, ADAM_STEP = 0.001, 0.9, 0.999, 1e-08, 0.01, 10

VMEM_LIMIT_V7X = 52 * 1024 * 1024
TM = 512

NN = (((1,), (0,)), ((), ()))
NT = (((1,), (1,)), ((), ()))
TN = (((0,), (0,)), ((), ()))


def _params(*sem):
    return pltpu.CompilerParams(dimension_semantics=sem if sem else None, vmem_limit_bytes=VMEM_LIMIT_V7X)


def _dot(a, b, dims=NN):
    return lax.dot_general(a.astype(BF16), b.astype(BF16), dims, preferred_element_type=F32)


def _sigmoid(v):
    return pl.reciprocal(1.0 + jnp.exp(-v), approx=True)


def _block_dims(spec):
    return tuple(d for d in spec.block_shape if d is not None)


def _after_operands(after):
    return list(after), [BS(memory_space=pl.ANY)] * len(after)


def _mm(name, pairs, *, grid, out_shape, out_spec, red_axis=None, extras=(), epilogue=None, after=()):
    n_pairs, n_extra = len(pairs), len(extras)
    n_red = grid[red_axis] if red_axis is not None else 1
    dims = [p[4] for p in pairs]

    def body(*refs):
        ab = refs[:2 * n_pairs]
        ex = refs[2 * n_pairs:2 * n_pairs + n_extra]
        o_ref = refs[2 * n_pairs + n_extra + len(after)]

        def partial():
            acc = None
            for p in range(n_pairs):
                t = _dot(ab[2 * p][...], ab[2 * p + 1][...], dims[p])
                acc = t if acc is None else acc + t
            return acc

        def finish(acc):
            res = epilogue(acc, *[e[...] for e in ex]) if epilogue is not None else acc
            o_ref[...] = res.astype(o_ref.dtype)

        if n_red == 1:
            finish(partial())
        else:
            acc_ref = refs[-1]
            k = pl.program_id(red_axis)

            @pl.when(k == 0)
            def _():
                acc_ref[...] = jnp.zeros_like(acc_ref)

            acc_ref[...] += partial()

            @pl.when(k == n_red - 1)
            def _():
                finish(acc_ref[...])

    operands, in_specs = [], []
    for a, a_spec, b, b_spec, _ in pairs:
        operands += [a, b]
        in_specs += [a_spec, b_spec]
    for e, e_spec in extras:
        operands.append(e)
        in_specs.append(e_spec)
    after_ops, after_specs = _after_operands(after)
    operands += after_ops
    in_specs += after_specs
    scratch = [pltpu.VMEM(_block_dims(out_spec), F32)] if n_red > 1 else []
    sem = tuple("arbitrary" if ax == red_axis else "parallel" for ax in range(len(grid)))
    return pl.pallas_call(body, out_shape=out_shape, grid=grid, in_specs=in_specs, out_specs=out_spec,
                          scratch_shapes=scratch, name=name, compiler_params=_params(*sem))(*operands)


def _rmsnorm(name, h, gain, tm, after=()):
    t, d = h.shape
    after_ops, after_specs = _after_operands(after)

    def body(h_ref, g_ref, *rest):
        u_ref = rest[-1]
        hv = h_ref[...]
        r = lax.rsqrt(jnp.mean(hv * hv, axis=-1, keepdims=True) + EPS)
        u_ref[...] = ((hv * r) * g_ref[...]).astype(u_ref.dtype)

    return pl.pallas_call(
        body, out_shape=SDS((t, d), BF16), grid=(t // tm,),
        in_specs=[BS((tm, d), lambda i: (i, 0)), BS((1, d), lambda i: (0, 0))] + after_specs,
        out_specs=BS((tm, d), lambda i: (i, 0)), name=name, compiler_params=_params("parallel"))(h, gain, *after_ops)


def _rmsnorm_bwd(name, h, gain, du, dh_in, tm):
    t, d = h.shape
    has_in = dh_in is not None

    def body(*refs):
        if has_in:
            h_ref, g_ref, du_ref, dhin_ref, dh_ref, dhb_ref, dg_ref = refs
        else:
            h_ref, g_ref, du_ref, dh_ref, dhb_ref, dg_ref = refs
        i = pl.program_id(0)
        hv = h_ref[...]
        r = lax.rsqrt(jnp.mean(hv * hv, axis=-1, keepdims=True) + EPS)
        n = hv * r
        duv = du_ref[...].astype(F32)
        dn = duv * g_ref[...]
        dh = r * (dn - n * jnp.mean(dn * n, axis=-1, keepdims=True))
        if has_in:
            dh = dhin_ref[...] + dh
        dh_ref[...] = dh
        dhb_ref[...] = dh.astype(BF16)

        @pl.when(i == 0)
        def _():
            dg_ref[...] = jnp.zeros_like(dg_ref)

        dg_ref[...] += jnp.sum(duv * n, axis=0, keepdims=True)

    row = BS((tm, d), lambda i: (i, 0))
    vec = BS((1, d), lambda i: (0, 0))
    operands = [h, gain, du] + ([dh_in] if has_in else [])
    in_specs = [row, vec, row] + ([row] if has_in else [])
    return pl.pallas_call(
        body, out_shape=(SDS((t, d), F32), SDS((t, d), BF16), SDS((1, d), F32)), grid=(t // tm,),
        in_specs=in_specs, out_specs=(row, row, vec), name=name, compiler_params=_params("arbitrary"))(*operands)


def _loss_head(h, gain, target, tm):
    t, d = h.shape

    def body(h_ref, g_ref, t_ref, loss_ref, dh_ref, dhb_ref, dg_ref):
        i = pl.program_id(0)
        hv = h_ref[...]
        g = g_ref[...]
        r = lax.rsqrt(jnp.mean(hv * hv, axis=-1, keepdims=True) + EPS)
        n = hv * r
        err = n * g - t_ref[...]
        dy = err * (1.0 / d)
        dn = dy * g
        dh = r * (dn - n * jnp.mean(dn * n, axis=-1, keepdims=True))
        dh_ref[...] = dh
        dhb_ref[...] = dh.astype(BF16)

        @pl.when(i == 0)
        def _():
            dg_ref[...] = jnp.zeros_like(dg_ref)
            loss_ref[...] = jnp.zeros_like(loss_ref)

        dg_ref[...] += jnp.sum(dy * n, axis=0, keepdims=True)
        part = 0.5 * jnp.sum(jnp.mean(err * err, axis=-1, keepdims=True), axis=0, keepdims=True)
        loss_ref[...] += jnp.broadcast_to(part, loss_ref.shape)

    row = BS((tm, d), lambda i: (i, 0))
    vec = BS((1, d), lambda i: (0, 0))
    return pl.pallas_call(
        body, out_shape=(SDS((1, 128), F32), SDS((t, d), F32), SDS((t, d), BF16), SDS((1, d), F32)),
        grid=(t // tm,), in_specs=[row, vec, row],
        out_specs=(BS((1, 128), lambda i: (0, 0)), row, row, vec),
        name="loss_head", compiler_params=_params("arbitrary"))(h, gain, target)


FFN_GATE, FFN_UP, FFN_DOWN = 0, 1, 2


def _ffn_w_spec(block, index):
    return BS((None, FF_SH, D_MODEL), lambda *g: (g[index], block, 0))


def _ffn_up(name, u, w_f, tm, after=()):
    t, d = u.shape
    after_ops, after_specs = _after_operands(after)

    def body(u_ref, wg_ref, wu_ref, *rest):
        g_ref, up_ref, a_ref = rest[len(after_ops):]
        uv = u_ref[...]
        g = _dot(uv, wg_ref[...], NT)
        up = _dot(uv, wu_ref[...], NT)
        g_ref[...] = g.astype(BF16)
        up_ref[...] = up.astype(BF16)
        a_ref[...] = (g * _sigmoid(g) * up).astype(BF16)

    hid = BS((None, tm, FF_SH), lambda s, i: (s, i, 0))
    shape = SDS((N_SHARD, t, FF_SH), BF16)
    return pl.pallas_call(
        body, out_shape=(shape, shape, shape), grid=(N_SHARD, t // tm),
        in_specs=[BS((tm, d), lambda s, i: (i, 0)), _ffn_w_spec(w_f["gate"][1], 0), _ffn_w_spec(w_f["up"][1], 0)] + after_specs,
        out_specs=(hid, hid, hid), name=name,
        compiler_params=_params("parallel", "parallel"))(u, w_f["gate"][0], w_f["up"][0], *after_ops)


def _ffn_all_shards_spec(block):
    return BS((N_SHARD, FF_SH, D_MODEL), lambda i: (0, block, 0))


def _ffn_down(name, a, w_f, resid, tm, after=()):
    t, d = resid.shape
    after_ops, after_specs = _after_operands(after)

    def body(a_ref, w_ref, res_ref, *rest):
        o_ref = rest[-1]
        acc = _dot(a_ref[0], w_ref[0])
        for s in range(1, N_SHARD):
            acc = acc + _dot(a_ref[s], w_ref[s])
        o_ref[...] = res_ref[...] + 0.5 * acc

    row = BS((tm, d), lambda i: (i, 0))
    return pl.pallas_call(
        body, out_shape=SDS((t, d), F32), grid=(t // tm,),
        in_specs=[BS((N_SHARD, tm, FF_SH), lambda i: (0, i, 0)), _ffn_all_shards_spec(w_f["down"][1]), row] + after_specs,
        out_specs=row, name=name, compiler_params=_params("parallel"))(a, w_f["down"][0], resid, *after_ops)


def _ffn_bwd_act(name, dh_b, w_f, g, up, tm, after=()):
    t, d = dh_b.shape
    after_ops, after_specs = _after_operands(after)

    def body(dh_ref, wd_ref, g_ref, up_ref, *rest):
        dg_ref, dup_ref = rest[len(after_ops):]
        da = 0.5 * _dot(dh_ref[...], wd_ref[...], NT)
        gv = g_ref[...].astype(F32)
        uv = up_ref[...].astype(F32)
        sg = _sigmoid(gv)
        das = da * sg
        dg_ref[...] = (das * uv * (1.0 + gv * (1.0 - sg))).astype(BF16)
        dup_ref[...] = (das * gv).astype(BF16)

    hid = BS((None, tm, FF_SH), lambda s, i: (s, i, 0))
    shape = SDS((N_SHARD, t, FF_SH), BF16)
    return pl.pallas_call(
        body, out_shape=(shape, shape), grid=(N_SHARD, t // tm),
        in_specs=[BS((tm, d), lambda s, i: (i, 0)), _ffn_w_spec(w_f["down"][1], 0), hid, hid] + after_specs,
        out_specs=(hid, hid), name=name,
        compiler_params=_params("parallel", "parallel"))(dh_b, w_f["down"][0], g, up, *after_ops)


def _ffn_dw(name, u, dg, dup, a, dh_b, tm):
    t, d = u.shape
    n_t = t // tm

    def body(u_ref, dg_ref, dup_ref, a_ref, dh_ref, o_ref, acc):
        i = pl.program_id(1)

        @pl.when(i == 0)
        def _():
            acc[...] = jnp.zeros_like(acc)

        uv = u_ref[...]
        acc[FFN_GATE] += _dot(dg_ref[...], uv, TN)
        acc[FFN_UP] += _dot(dup_ref[...], uv, TN)
        acc[FFN_DOWN] += _dot(a_ref[...], dh_ref[...], TN)

        @pl.when(i == n_t - 1)
        def _():
            o_ref[FFN_GATE] = acc[FFN_GATE].astype(BF16)
            o_ref[FFN_UP] = acc[FFN_UP].astype(BF16)
            o_ref[FFN_DOWN] = (0.5 * acc[FFN_DOWN]).astype(BF16)

    hid = BS((None, tm, FF_SH), lambda s, i: (s, i, 0))
    row = BS((tm, d), lambda s, i: (i, 0))
    return pl.pallas_call(
        body, out_shape=SDS((N_SHARD, 3, FF_SH, d), BF16), grid=(N_SHARD, n_t),
        in_specs=[row, hid, hid, hid, row], out_specs=BS((None, 3, FF_SH, d), lambda s, i: (s, 0, 0, 0)),
        scratch_shapes=[pltpu.VMEM((3, FF_SH, d), F32)],
        name=name, compiler_params=_params("parallel", "arbitrary"))(u, dg, dup, a, dh_b)


def _ffn_dx(name, dg, dup, w_f, tm, after=()):
    t = dg.shape[1]
    tm = tm // 2
    after_ops, after_specs = _after_operands(after)

    def body(dg_ref, dup_ref, wg_ref, wu_ref, *rest):
        o_ref = rest[-1]
        acc = _dot(dg_ref[0], wg_ref[0]) + _dot(dup_ref[0], wu_ref[0])
        for s in range(1, N_SHARD):
            acc = acc + _dot(dg_ref[s], wg_ref[s]) + _dot(dup_ref[s], wu_ref[s])
        o_ref[...] = acc

    hid = BS((N_SHARD, tm, FF_SH), lambda i: (0, i, 0))
    return pl.pallas_call(
        body, out_shape=SDS((t, D_MODEL), F32), grid=(t // tm,),
        in_specs=[hid, hid, _ffn_all_shards_spec(w_f["gate"][1]), _ffn_all_shards_spec(w_f["up"][1])] + after_specs,
        out_specs=BS((tm, D_MODEL), lambda i: (i, 0)), name=name,
        compiler_params=_params("parallel"))(dg, dup, w_f["gate"][0], w_f["up"][0], *after_ops)


def _plain_mm(name, a, b, dims, out_dtype, tm, resid=None, after=()):
    t = a.shape[0]
    n = b.shape[1] if dims == NN else b.shape[0]
    extras = [(resid, BS((tm, n), lambda i: (i, 0)))] if resid is not None else []
    epi = (lambda acc, res: res + acc) if resid is not None else None
    return _mm(name, [(a, BS((tm, a.shape[1]), lambda i: (i, 0)), b, BS(b.shape, lambda i: (0, 0)), dims)],
               grid=(t // tm,), out_shape=SDS((t, n), out_dtype), out_spec=BS((tm, n), lambda i: (i, 0)),
               extras=extras, epilogue=epi, after=after)


def _dw_mm(name, a, b, tm, out_dtype=BF16):
    t, k = a.shape
    n = b.shape[1]
    return _mm(name, [(a, BS((tm, k), lambda i: (i, 0)), b, BS((tm, n), lambda i: (i, 0)), TN)],
               grid=(t // tm,), red_axis=0, out_shape=SDS((k, n), out_dtype), out_spec=BS((k, n), lambda i: (0, 0)))


POOL_CHUNK = 256
POOL_HALO = 8


def _window_sum(v, width, lead):
    n = v.shape[0]
    s = v
    k = 1
    while k < width:
        s = s + pltpu.roll(s, n - k, 0)
        k *= 2
    return pltpu.roll(s, lead, 0) if lead else s


def _pool_count(base, left, right, t, shape):
    pos = base + lax.broadcasted_iota(jnp.int32, shape, 0)
    lo = jnp.maximum(pos - left, 0)
    hi = jnp.minimum(pos + right + 1, t)
    return (hi - lo).astype(F32)


def _pool_fwd(proj, pool_w, pool_scale):
    t = proj.shape[0]
    c, h = POOL_CHUNK, POOL_HALO
    n_chunks = t // c

    def body(proj_hbm, pw_ref, sc_ref, pooled_ref, mixed_ref, ms_ref, pad_ref, sem):
        cp = pltpu.make_async_copy(proj_hbm.at[:, pl.ds(0, D_POOL)], pad_ref.at[pl.ds(h, t), :], sem)
        cp.start()
        pad_ref[pl.ds(0, h), :] = jnp.zeros((h, D_POOL), F32)
        pad_ref[pl.ds(t + h, h), :] = jnp.zeros((h, D_POOL), F32)
        cp.wait()
        for g, width in enumerate(POOL_WINDOWS):
            left = width // 2
            right = width - 1 - left
            cols = slice(g * POOL_GROUP, (g + 1) * POOL_GROUP)
            wmat = pw_ref[g].astype(BF16)
            scale = sc_ref[:, cols]

            def chunk(ci, carry, left=left, right=right, width=width, cols=cols, wmat=wmat, scale=scale):
                base = pl.multiple_of(ci * c, c)
                v = pad_ref[pl.ds(base, c + 2 * h), cols]
                win = _window_sum(v, width, left)[h:h + c]
                cnt = _pool_count(base, left, right, t, (c, POOL_GROUP))
                pooled = (win / cnt - v[h:h + c]).astype(BF16)
                mixed = _dot(pooled, wmat)
                pooled_ref[pl.ds(base, c), cols] = pooled
                mixed_ref[pl.ds(base, c), cols] = mixed.astype(BF16)
                ms_ref[pl.ds(base, c), cols] = (mixed * scale).astype(BF16)
                return carry

            lax.fori_loop(0, n_chunks, chunk, 0)

    vm = BS(memory_space=pltpu.VMEM)
    shape = SDS((t, D_POOL), BF16)
    return pl.pallas_call(
        body, out_shape=(shape, shape, shape),
        in_specs=[BS(memory_space=pl.ANY), vm, vm], out_specs=(vm, vm, vm),
        scratch_shapes=[pltpu.VMEM((t + 2 * h, D_POOL), F32), pltpu.SemaphoreType.DMA],
        name="pool_fwd", compiler_params=_params())(proj, pool_w, pool_scale)


def _pool_bwd(d_ms, mixed, pooled, pool_w, pool_scale):
    t = d_ms.shape[0]
    c, h = POOL_CHUNK, POOL_HALO
    n_chunks = t // c

    def body(dms_ref, mixed_ref, pooled_ref, pw_ref, sc_ref, dp_ref, dsc_ref, dpw_ref, pad_ref):
        pad_ref[pl.ds(0, h), :] = jnp.zeros((h, D_POOL), F32)
        pad_ref[pl.ds(t + h, h), :] = jnp.zeros((h, D_POOL), F32)
        for g, width in enumerate(POOL_WINDOWS):
            left = width // 2
            right = width - 1 - left
            cols = slice(g * POOL_GROUP, (g + 1) * POOL_GROUP)
            wmat = pw_ref[g].astype(BF16)
            scale = sc_ref[:, cols]

            def first(ci, carry, left=left, right=right, cols=cols, wmat=wmat, scale=scale):
                dsc, dpw = carry
                base = pl.multiple_of(ci * c, c)
                dms = dms_ref[pl.ds(base, c), cols].astype(F32)
                dsc = dsc + jnp.sum(dms * mixed_ref[pl.ds(base, c), cols].astype(F32), axis=0, keepdims=True)
                dmix = (dms * scale).astype(BF16)
                dpw = dpw + _dot(pooled_ref[pl.ds(base, c), cols], dmix, TN)
                dpooled = _dot(dmix, wmat, NT)
                cnt = _pool_count(base, left, right, t, (c, POOL_GROUP))
                pad_ref[pl.ds(base + h, c), cols] = dpooled / cnt
                return dsc, dpw

            dsc, dpw = lax.fori_loop(0, n_chunks, first,
                                     (jnp.zeros((1, POOL_GROUP), F32), jnp.zeros((POOL_GROUP, POOL_GROUP), F32)))
            dsc_ref[:, cols] = dsc
            dpw_ref[g] = dpw

            def second(ci, carry, left=left, right=right, width=width, cols=cols):
                base = pl.multiple_of(ci * c, c)
                v = pad_ref[pl.ds(base, c + 2 * h), cols]
                win = _window_sum(v, width, right)[h:h + c]
                cnt = _pool_count(base, left, right, t, (c, POOL_GROUP))
                dp_ref[pl.ds(base, c), cols] = (win - v[h:h + c] * cnt).astype(BF16)
                return carry

            lax.fori_loop(0, n_chunks, second, 0)

    vm = BS(memory_space=pltpu.VMEM)
    return pl.pallas_call(
        body, out_shape=(SDS((t, D_POOL), BF16), SDS((1, D_POOL), F32), SDS((4, POOL_GROUP, POOL_GROUP), F32)),
        in_specs=[vm] * 5, out_specs=(vm, vm, vm),
        scratch_shapes=[pltpu.VMEM((t + 2 * h, D_POOL), F32)],
        name="pool_bwd", compiler_params=_params())(d_ms, mixed, pooled, pool_w, pool_scale)


def _ssm_disc(ar, ai, ldt, after=()):
    after_ops, after_specs = _after_operands(after)

    def body(ar_ref, ai_ref, ldt_ref, *rest):
        abr_ref, abi_ref, qr_ref, qi_ref = rest[len(after_ops):]
        a_r, a_i = ar_ref[...], ai_ref[...]
        dt = jnp.exp(ldt_ref[...])
        mag = jnp.exp(dt * a_r)
        ang = dt * a_i
        abr = mag * jnp.cos(ang)
        abi = mag * jnp.sin(ang)
        den = a_r * a_r + a_i * a_i
        nr = abr - 1.0
        abr_ref[...] = abr
        abi_ref[...] = abi
        qr_ref[...] = (nr * a_r + abi * a_i) / den
        qi_ref[...] = (abi * a_r - nr * a_i) / den

    vm = BS(memory_space=pltpu.VMEM)
    shape = SDS(ar.shape, F32)
    return pl.pallas_call(body, out_shape=(shape,) * 4, in_specs=[vm] * 3 + after_specs, out_specs=(vm,) * 4,
                          name="ssm_disc", compiler_params=_params())(ar, ai, ldt, *after_ops)


def _ssm_disc_bwd(ar, ai, ldt, d_abr, d_abi, d_qr, d_qi):
    def body(ar_ref, ai_ref, ldt_ref, gabr_ref, gabi_ref, gqr_ref, gqi_ref, dar_ref, dai_ref, dldt_ref):
        a_r, a_i = ar_ref[...], ai_ref[...]
        dt = jnp.exp(ldt_ref[...])
        mag = jnp.exp(dt * a_r)
        ang = dt * a_i
        cs, sn = jnp.cos(ang), jnp.sin(ang)
        abr, abi = mag * cs, mag * sn
        den = a_r * a_r + a_i * a_i
        nr = abr - 1.0
        qr = (nr * a_r + abi * a_i) / den
        qi = (abi * a_r - nr * a_i) / den
        gqr, gqi = gqr_ref[...], gqi_ref[...]
        g_nr_num = gqr / den
        g_ni_num = gqi / den
        g_den = -(gqr * qr + gqi * qi) / den
        g_nr = g_nr_num * a_r - g_ni_num * a_i
        g_abi = g_nr_num * a_i + g_ni_num * a_r
        d_ar = g_nr_num * nr + g_ni_num * abi + 2.0 * a_r * g_den
        d_ai = g_nr_num * abi - g_ni_num * nr + 2.0 * a_i * g_den
        g_abr = gabr_ref[...] + g_nr
        g_abi = gabi_ref[...] + g_abi
        g_mag = g_abr * cs + g_abi * sn
        g_ang = mag * (g_abi * cs - g_abr * sn)
        g_e = g_mag * mag
        d_ar = d_ar + g_e * dt
        d_ai = d_ai + g_ang * dt
        g_dt = g_e * a_r + g_ang * a_i
        dar_ref[...] = d_ar
        dai_ref[...] = d_ai
        dldt_ref[...] = jnp.sum(g_dt * dt, axis=1, keepdims=True)

    vm = BS(memory_space=pltpu.VMEM)
    return pl.pallas_call(body, out_shape=(SDS(ar.shape, F32), SDS(ar.shape, F32), SDS(ldt.shape, F32)),
                          in_specs=[vm] * 7, out_specs=(vm,) * 3, name="ssm_disc_bwd",
                          compiler_params=_params())(ar, ai, ldt, d_abr, d_abi, d_qr, d_qi)


def _ssm_bbar(qr, qi, br, bi):
    def body(qr_ref, qi_ref, br_ref, bi_ref, bbr_ref, bbi_ref):
        q_r, q_i, b_r, b_i = qr_ref[...], qi_ref[...], br_ref[...], bi_ref[...]
        bbr_ref[...] = q_r * b_r - q_i * b_i
        bbi_ref[...] = q_r * b_i + q_i * b_r

    vm = BS(memory_space=pltpu.VMEM)
    shape = SDS(br.shape, F32)
    return pl.pallas_call(body, out_shape=(shape, shape), in_specs=[vm] * 4, out_specs=(vm, vm),
                          name="ssm_bbar", compiler_params=_params())(qr, qi, br, bi)


def _ssm_bbar_bwd(qr, qi, br, bi, g_bbr, g_bbi):
    def body(qr_ref, qi_ref, br_ref, bi_ref, gr_ref, gi_ref, dqr_ref, dqi_ref, dbr_ref, dbi_ref):
        q_r, q_i, b_r, b_i = qr_ref[...], qi_ref[...], br_ref[...], bi_ref[...]
        g_r, g_i = gr_ref[...], gi_ref[...]
        dqr_ref[...] = jnp.sum(g_r * b_r + g_i * b_i, axis=1, keepdims=True)
        dqi_ref[...] = jnp.sum(g_i * b_r - g_r * b_i, axis=1, keepdims=True)
        dbr_ref[...] = g_r * q_r + g_i * q_i
        dbi_ref[...] = g_i * q_r - g_r * q_i

    vm = BS(memory_space=pltpu.VMEM)
    return pl.pallas_call(
        body, out_shape=(SDS(qr.shape, F32), SDS(qr.shape, F32), SDS(br.shape, F32), SDS(br.shape, F32)),
        in_specs=[vm] * 6, out_specs=(vm,) * 4, name="ssm_bbar_bwd",
        compiler_params=_params())(qr, qi, br, bi, g_bbr, g_bbi)


SCAN_ROWS = 256


def _ssm_scan(name, inp, w1, a_r, a_i, w2, reverse):
    t = inp.shape[0]
    rows = SCAN_ROWS
    n = t // rows
    n_groups = rows // 8
    ch = SSM_CH
    at = (lambda i: (n - 1 - i, 0)) if reverse else (lambda i: (i, 0))

    def body(in_ref, w1_ref, ar_ref, ai_ref, w2_ref, sb_ref, out_ref, cr_ref, ci_ref, k_ref, st_ref):
        i = pl.program_id(0)

        @pl.when(i == 0)
        def _():
            ar8 = jnp.broadcast_to(ar_ref[...], (8, ch))
            ai8 = jnp.broadcast_to(ai_ref[...], (8, ch))
            row = lax.broadcasted_iota(jnp.int32, (8, ch), 0)
            rank = (7 - row) if reverse else row
            powers = [(ar8, ai8)]
            for _ in range(7):
                p_r, p_i = powers[-1]
                powers.append((p_r * ar8 - p_i * ai8, p_r * ai8 + p_i * ar8))
            zero = jnp.zeros((8, ch), F32)
            for slot, k in enumerate((1, 2, 4)):
                k_ref[2 * slot] = jnp.where(rank >= k, powers[k - 1][0], zero)
                k_ref[2 * slot + 1] = jnp.where(rank >= k, powers[k - 1][1], zero)
            carry_r, carry_i = zero, zero
            for j in range(8):
                carry_r = jnp.where(rank == j, powers[j][0], carry_r)
                carry_i = jnp.where(rank == j, powers[j][1], carry_i)
            k_ref[6] = carry_r
            k_ref[7] = carry_i
            cr_ref[...] = zero
            ci_ref[...] = zero

        st_ref[...] = _dot(in_ref[...], w1_ref[...])

        def group(gi, carry):
            c_r, c_i = carry
            g = (n_groups - 1 - gi) if reverse else gi
            r0 = pl.multiple_of(g * 8, 8)
            x_r = st_ref[pl.ds(r0, 8), 0:ch]
            x_i = st_ref[pl.ds(r0, 8), ch:2 * ch]
            for slot, k in enumerate((1, 2, 4)):
                shift = (8 - k) if reverse else k
                s_r = pltpu.roll(x_r, shift, 0)
                s_i = pltpu.roll(x_i, shift, 0)
                m_r, m_i = k_ref[2 * slot], k_ref[2 * slot + 1]
                x_r, x_i = x_r + m_r * s_r - m_i * s_i, x_i + m_r * s_i + m_i * s_r
            p_r, p_i = k_ref[6], k_ref[7]
            x_r, x_i = x_r + p_r * c_r - p_i * c_i, x_i + p_r * c_i + p_i * c_r
            st_ref[pl.ds(r0, 8), 0:ch] = x_r
            st_ref[pl.ds(r0, 8), ch:2 * ch] = x_i
            last = 0 if reverse else 7
            return (jnp.broadcast_to(x_r[last:last + 1, :], (8, ch)), jnp.broadcast_to(x_i[last:last + 1, :], (8, ch)))

        c_r, c_i = lax.fori_loop(0, n_groups, group, (cr_ref[...], ci_ref[...]))
        cr_ref[...] = c_r
        ci_ref[...] = c_i
        states = st_ref[...].astype(BF16)
        sb_ref[...] = states
        out_ref[...] = _dot(states, w2_ref[...])

    return pl.pallas_call(
        body, out_shape=(SDS((t, 2 * ch), BF16), SDS((t, D_SSM), F32)), grid=(n,),
        in_specs=[BS((rows, D_SSM), at), BS((D_SSM, 2 * ch), lambda i: (0, 0)), BS((1, ch), lambda i: (0, 0)),
                  BS((1, ch), lambda i: (0, 0)), BS((2 * ch, D_SSM), lambda i: (0, 0))],
        out_specs=(BS((rows, 2 * ch), at), BS((rows, D_SSM), at)),
        scratch_shapes=[pltpu.VMEM((8, ch), F32), pltpu.VMEM((8, ch), F32), pltpu.VMEM((8, 8, ch), F32),
                        pltpu.VMEM((rows, 2 * ch), F32)],
        name=name, compiler_params=_params("arbitrary"))(inp, w1, a_r, a_i, w2)


DA_ROWS = 512


def _ssm_da(name, lam, states, reverse):
    t = lam.shape[0]
    rows = DA_ROWS
    n = t // rows
    halo_rows = 16
    nb = rows // halo_rows
    ch = SSM_CH
    if reverse:
        halo_at = lambda i: (jnp.minimum((i + 1) * nb, t // halo_rows - 1), 0)
    else:
        halo_at = lambda i: (jnp.maximum(i * nb - 1, 0), 0)

    def body(lam_ref, x_ref, halo_ref, dr_ref, di_ref):
        i = pl.program_id(0)

        @pl.when(i == 0)
        def _():
            dr_ref[...] = jnp.zeros_like(dr_ref)
            di_ref[...] = jnp.zeros_like(di_ref)

        row = lax.broadcasted_iota(jnp.int32, (rows, ch), 0)
        if reverse:
            edge, shift, h_row, live = rows - 1, rows - 1, 0, i < n - 1
        else:
            edge, shift, h_row, live = 0, 1, halo_rows - 1, i > 0

        def neighbour(lo):
            halo = halo_ref[:, lo:lo + ch].astype(F32)[h_row:h_row + 1]
            halo = jnp.where(live, halo, 0.0)
            x = x_ref[:, lo:lo + ch].astype(F32)
            return jnp.where(row == edge, jnp.broadcast_to(halo, (rows, ch)), pltpu.roll(x, shift, 0))

        xp_r, xp_i = neighbour(0), neighbour(ch)
        l_r, l_i = lam_ref[:, 0:ch].astype(F32), lam_ref[:, ch:2 * ch].astype(F32)
        dr_ref[...] += jnp.sum(l_r * xp_r + l_i * xp_i, axis=0, keepdims=True)
        di_ref[...] += jnp.sum(l_i * xp_r - l_r * xp_i, axis=0, keepdims=True)

    blk = BS((rows, 2 * ch), lambda i: (i, 0))
    vec = BS((1, ch), lambda i: (0, 0))
    return pl.pallas_call(
        body, out_shape=(SDS((1, ch), F32), SDS((1, ch), F32)), grid=(n,),
        in_specs=[blk, blk, BS((halo_rows, 2 * ch), halo_at)], out_specs=(vec, vec),
        name=name, compiler_params=_params("arbitrary"))(lam, states, states)


GELU_C = math.sqrt(2.0 / math.pi)
GELU_K = 0.044715


def _ssm_combine(proj, y_fwd, y_bwd, d_skip, tm, after=()):
    t = proj.shape[0]
    after_ops, after_specs = _after_operands(after)

    def body(s_ref, yf_ref, yb_ref, d_ref, *rest):
        yt_ref, g_ref = rest[len(after_ops):]
        y = s_ref[...] * d_ref[...] + yf_ref[...] + yb_ref[...]
        yt_ref[...] = y
        th = jnp.tanh(GELU_C * (y + GELU_K * y * y * y))
        g_ref[...] = (0.5 * y * (1.0 + th)).astype(BF16)

    blk = BS((tm, D_SSM), lambda i: (i, 0))
    return pl.pallas_call(
        body, out_shape=(SDS((t, D_SSM), F32), SDS((t, D_SSM), BF16)), grid=(t // tm,),
        in_specs=[BS((tm, D_SSM), lambda i: (i, D_POOL // D_SSM)), blk, blk, BS((1, D_SSM), lambda i: (0, 0))] + after_specs,
        out_specs=(blk, blk), name="ssm_combine",
        compiler_params=_params("parallel"))(proj, y_fwd, y_bwd, d_skip, *after_ops)


def _ssm_ds(proj, d_yt, du_fwd, du_bwd, d_skip, tm):
    t = proj.shape[0]

    def body(s_ref, dy_ref, duf_ref, dub_ref, d_ref, ds_ref, dd_ref):
        i = pl.program_id(0)
        dy = dy_ref[...]
        ds_ref[...] = (dy * d_ref[...] + duf_ref[...] + dub_ref[...]).astype(BF16)

        @pl.when(i == 0)
        def _():
            dd_ref[...] = jnp.zeros_like(dd_ref)

        dd_ref[...] += jnp.sum(dy * s_ref[...], axis=0, keepdims=True)

    blk = BS((tm, D_SSM), lambda i: (i, 0))
    vec = BS((1, D_SSM), lambda i: (0, 0))
    return pl.pallas_call(
        body, out_shape=(SDS((t, D_SSM), BF16), SDS((1, D_SSM), F32)), grid=(t // tm,),
        in_specs=[BS((tm, D_SSM), lambda i: (i, D_POOL // D_SSM)), blk, blk, blk, vec],
        out_specs=(blk, vec), name="ssm_ds", compiler_params=_params("arbitrary"))(proj, d_yt, du_fwd, du_bwd, d_skip)


GP_BLOCK = (D_POOL + D_SSM) // 256
GS_BLOCK = GP_BLOCK + D_MODEL // 256


def _merge_specs(tm):
    return [BS((tm, D_POOL), lambda s, i: (i, 0)), BS((tm, D_SSM), lambda s, i: (i, 0)),
            BS((None, D_POOL, 256), lambda s, i: (s, 0, 0)), BS((None, D_SSM, 256), lambda s, i: (s, 2, 0)),
            BS((None, D_SSM, 256), lambda s, i: (s, 3, 0)),
            BS((tm, 256), lambda s, i: (i, GP_BLOCK + s)), BS((tm, 256), lambda s, i: (i, GS_BLOCK + s))]


def _mixer_merge(ms, yssm, w_e, proj, tm):
    t = ms.shape[0]

    def body(ms_ref, y_ref, wpp_ref, wgv_ref, wgg_ref, gp_ref, gs_ref, o_ref):
        zp = _dot(ms_ref[...], wpp_ref[...])
        yv = y_ref[...]
        zv = _dot(yv, wgv_ref[...])
        zg = _dot(yv, wgg_ref[...])
        o_ref[...] = (_sigmoid(gp_ref[...]) * zp + _sigmoid(gs_ref[...]) * zv * _sigmoid(zg)).astype(BF16)

    col = BS((tm, 256), lambda s, i: (i, s))
    return pl.pallas_call(
        body, out_shape=SDS((t, D_MODEL), BF16), grid=(N_SHARD, t // tm), in_specs=_merge_specs(tm), out_specs=col,
        name="mixer_merge", compiler_params=_params("parallel", "parallel"))(ms, yssm, w_e, w_e, w_e, proj, proj)


def _mixer_merge_bwd(ms, yssm, w_e, proj, dmerged, tm):
    t = ms.shape[0]

    def body(ms_ref, y_ref, wpp_ref, wgv_ref, wgg_ref, gp_ref, gs_ref, dm_ref,
             dgp_ref, dgs_ref, dzp_ref, dzv_ref, dzg_ref):
        zp = _dot(ms_ref[...], wpp_ref[...])
        yv = y_ref[...]
        zv = _dot(yv, wgv_ref[...])
        zg = _dot(yv, wgg_ref[...])
        dm = dm_ref[...].astype(F32)
        sp, ss, sg = _sigmoid(gp_ref[...]), _sigmoid(gs_ref[...]), _sigmoid(zg)
        dgp_ref[...] = (dm * zp * sp * (1.0 - sp)).astype(BF16)
        dgs_ref[...] = (dm * zv * sg * ss * (1.0 - ss)).astype(BF16)
        dzp_ref[...] = (dm * sp).astype(BF16)
        dz = dm * ss
        dzv_ref[...] = (dz * sg).astype(BF16)
        dzg_ref[...] = (dz * zv * sg * (1.0 - sg)).astype(BF16)

    col = BS((tm, 256), lambda s, i: (i, s))
    shape = SDS((t, D_MODEL), BF16)
    return pl.pallas_call(
        body, out_shape=(shape,) * 5, grid=(N_SHARD, t // tm), in_specs=_merge_specs(tm) + [col],
        out_specs=(col,) * 5, name="mixer_merge_bwd",
        compiler_params=_params("parallel", "parallel"))(ms, yssm, w_e, w_e, w_e, proj, proj, dmerged)


def _mixer_dw(ms, yssm, dzp, dzv, dzg, tm):
    t = ms.shape[0]
    n_t = t // tm

    def body(ms_ref, y_ref, dzp_ref, dzv_ref, dzg_ref, o_ref, acc):
        i = pl.program_id(1)

        @pl.when(i == 0)
        def _():
            acc[...] = jnp.zeros_like(acc)

        yv = y_ref[...]
        acc[0:D_POOL, :] += _dot(ms_ref[...], dzp_ref[...], TN)
        acc[D_POOL:D_POOL + D_SSM, :] += _dot(yv, dzv_ref[...], TN)
        acc[D_POOL + D_SSM:, :] += _dot(yv, dzg_ref[...], TN)

        @pl.when(i == n_t - 1)
        def _():
            o_ref[...] = acc[...].astype(BF16)

    col = BS((tm, 256), lambda s, i: (i, s))
    return pl.pallas_call(
        body, out_shape=SDS((N_SHARD, 1024, 256), BF16), grid=(N_SHARD, n_t),
        in_specs=[BS((tm, D_POOL), lambda s, i: (i, 0)), BS((tm, D_SSM), lambda s, i: (i, 0)), col, col, col],
        out_specs=BS((None, 1024, 256), lambda s, i: (s, 0, 0)), scratch_shapes=[pltpu.VMEM((1024, 256), F32)],
        name="mixer_dw", compiler_params=_params("parallel", "arbitrary"))(ms, yssm, dzp, dzv, dzg)


def _mixer_dx(dzp, dzv, dzg, w_e, y_total, tm):
    t = dzp.shape[0]

    def body(dzp_ref, dzv_ref, dzg_ref, wpp_ref, wgv_ref, wgg_ref, yt_ref, dms_ref, dy_ref, acc_ms, acc_y):
        s = pl.program_id(1)

        @pl.when(s == 0)
        def _():
            acc_ms[...] = jnp.zeros_like(acc_ms)
            acc_y[...] = jnp.zeros_like(acc_y)

        acc_ms[...] += _dot(dzp_ref[...], wpp_ref[...], NT)
        acc_y[...] += _dot(dzv_ref[...], wgv_ref[...], NT) + _dot(dzg_ref[...], wgg_ref[...], NT)

        @pl.when(s == N_SHARD - 1)
        def _():
            dms_ref[...] = acc_ms[...].astype(BF16)
            y = yt_ref[...]
            inner = GELU_C * (y + GELU_K * y * y * y)
            th = jnp.tanh(inner)
            dgelu = 0.5 * (1.0 + th) + 0.5 * y * (1.0 - th * th) * GELU_C * (1.0 + 3.0 * GELU_K * y * y)
            dy_ref[...] = acc_y[...] * dgelu

    col = BS((tm, 256), lambda i, s: (i, s))
    return pl.pallas_call(
        body, out_shape=(SDS((t, D_POOL), BF16), SDS((t, D_SSM), F32)), grid=(t // tm, N_SHARD),
        in_specs=[col, col, col, BS((None, D_POOL, 256), lambda i, s: (s, 0, 0)),
                  BS((None, D_SSM, 256), lambda i, s: (s, 2, 0)), BS((None, D_SSM, 256), lambda i, s: (s, 3, 0)),
                  BS((tm, D_SSM), lambda i, s: (i, 0))],
        out_specs=(BS((tm, D_POOL), lambda i, s: (i, 0)), BS((tm, D_SSM), lambda i, s: (i, 0))),
        scratch_shapes=[pltpu.VMEM((tm, D_POOL), F32), pltpu.VMEM((tm, D_SSM), F32)],
        name="mixer_dx", compiler_params=_params("parallel", "arbitrary"))(dzp, dzv, dzg, w_e, w_e, w_e, y_total)


def _attn_probs(q_h, k_h):
    s = _dot(q_h, k_h, NT) * (1.0 / math.sqrt(HEAD_DIM))
    e = jnp.exp(s - jnp.max(s, axis=-1, keepdims=True))
    return e / jnp.sum(e, axis=-1, keepdims=True)


def _attn_fwd(q, kv, tm):
    t = q.shape[0]
    m = kv.shape[0]

    def body(q_ref, kv_ref, o_ref):
        for hd in range(N_HEADS):
            lo = hd * HEAD_DIM
            p = _attn_probs(q_ref[:, lo:lo + HEAD_DIM], kv_ref[:, lo:lo + HEAD_DIM])
            o_ref[:, lo:lo + HEAD_DIM] = _dot(p, kv_ref[:, D_MODEL + lo:D_MODEL + lo + HEAD_DIM]).astype(BF16)

    return pl.pallas_call(
        body, out_shape=SDS((t, D_MODEL), BF16), grid=(t // tm,),
        in_specs=[BS((tm, D_MODEL), lambda i: (i, 0)), BS((m, 2 * D_MODEL), lambda i: (0, 0))],
        out_specs=BS((tm, D_MODEL), lambda i: (i, 0)), name="attn_fwd", compiler_params=_params("parallel"))(q, kv)


def _attn_bwd(q, kv, d_o, tm):
    t = q.shape[0]
    m = kv.shape[0]

    def body(q_ref, kv_ref, do_ref, dq_ref, dkv_ref):
        i = pl.program_id(0)

        @pl.when(i == 0)
        def _():
            dkv_ref[...] = jnp.zeros_like(dkv_ref)

        for hd in range(N_HEADS):
            lo = hd * HEAD_DIM
            q_h = q_ref[:, lo:lo + HEAD_DIM]
            k_h = kv_ref[:, lo:lo + HEAD_DIM]
            v_h = kv_ref[:, D_MODEL + lo:D_MODEL + lo + HEAD_DIM]
            do_h = do_ref[:, lo:lo + HEAD_DIM]
            p = _attn_probs(q_h, k_h)
            dkv_ref[:, D_MODEL + lo:D_MODEL + lo + HEAD_DIM] += _dot(p, do_h, TN)
            dp = _dot(do_h, v_h, NT)
            ds = p * (dp - jnp.sum(dp * p, axis=-1, keepdims=True)) * (1.0 / math.sqrt(HEAD_DIM))
            dq_ref[:, lo:lo + HEAD_DIM] = _dot(ds, k_h).astype(BF16)
            dkv_ref[:, lo:lo + HEAD_DIM] += _dot(ds, q_h, TN)

    row = BS((tm, D_MODEL), lambda i: (i, 0))
    full = BS((m, 2 * D_MODEL), lambda i: (0, 0))
    return pl.pallas_call(
        body, out_shape=(SDS((t, D_MODEL), BF16), SDS((m, 2 * D_MODEL), F32)), grid=(t // tm,),
        in_specs=[row, full, row], out_specs=(row, full), name="attn_bwd",
        compiler_params=_params("arbitrary"))(q, kv, d_o)


TRANSPOSED = ("ffn1_w_gate", "ffn1_w_up", "ffn2_w_gate", "ffn2_w_up", "w_in")
GATHER_PHASES = {"f1a": (("ffn1_w_gate", "ffn1_w_up"),),
                 "f1b": (("ffn1_w_down",),),
                 "win": (("w_in",),),
                 "mix": (("w_mix_out", "w_q", "w_xo"), ("w_kv",), ("w_pool_proj", "w_glu_val", "w_glu_gate")),
                 "f2": (("ffn2_w_gate", "ffn2_w_up", "ffn2_w_down"),)}
REDUCE_GROUPS = (("ffn2_w_gate", "ffn2_w_up", "ffn2_w_down"), ("w_xo",), ("w_q",), ("w_kv",), ("w_mix_out",),
                 ("w_pool_proj", "w_glu_val", "w_glu_gate"), ("w_in",), ("ffn1_w_gate", "ffn1_w_up", "ffn1_w_down"))
SMALL = ("ffn1_norm", "mix_norm", "pool_w", "pool_scale", "ssm_a_re", "ssm_a_im", "ssm_log_dt", "ssm_b_re",
         "ssm_b_im", "ssm_c_re", "ssm_c_im", "ssm_d", "xattn_norm", "mem_norm", "ffn2_norm", "final_norm")
WEIGHTS = ("ffn1_norm", "ffn1_w_gate", "ffn1_w_up", "ffn1_w_down", "mix_norm", "w_in", "pool_w", "pool_scale",
           "w_pool_proj", "ssm_a_re", "ssm_a_im", "ssm_log_dt", "ssm_b_re", "ssm_b_im", "ssm_c_re", "ssm_c_im",
           "ssm_d", "w_glu_val", "w_glu_gate", "w_mix_out", "xattn_norm", "mem_norm", "w_q", "w_kv", "w_xo",
           "ffn2_norm", "ffn2_w_gate", "ffn2_w_up", "ffn2_w_down", "final_norm")


def _block_diag_in(bb):
    eye = jnp.eye(SSM_GROUPS, dtype=bb.dtype)
    return jnp.einsum("dgph,gk->dghkp", bb, eye).reshape(2, D_SSM, SSM_CH)


def _block_diag_out(cc):
    eye = jnp.eye(SSM_GROUPS, dtype=cc.dtype)
    return jnp.einsum("dghp,gk->dgpkh", cc, eye).reshape(2, SSM_CH, D_SSM)


def _diag_blocks_in(m):
    return jnp.einsum("dghgp->dgph", m.reshape(2, SSM_GROUPS, SSM_GROUP, SSM_GROUPS, SSM_STATE))


def _diag_blocks_out(m):
    return jnp.einsum("dgpgh->dghp", m.reshape(2, SSM_GROUPS, SSM_STATE, SSM_GROUPS, SSM_GROUP))


def _device_step(x, mem, target, wts, sp, reducer=None):
    t = x.shape[0]
    tm = min(TM, t)
    g = {}

    first_gather = wts.start("f1a")
    u1 = _rmsnorm("norm_ffn1", x, sp["ffn1_norm"], tm, after=first_gather)

    ar = sp["ssm_a_re"].reshape(2 * SSM_GROUPS, SSM_STATE)
    ai = sp["ssm_a_im"].reshape(2 * SSM_GROUPS, SSM_STATE)
    ldt = sp["ssm_log_dt"].reshape(2 * SSM_GROUPS, 1)
    abr, abi, qr, qi = _ssm_disc(ar, ai, ldt, after=first_gather)
    b_r = sp["ssm_b_re"].reshape(2 * SSM_CH, SSM_GROUP)
    b_i = sp["ssm_b_im"].reshape(2 * SSM_CH, SSM_GROUP)
    qr_col, qi_col = qr.reshape(2 * SSM_CH, 1), qi.reshape(2 * SSM_CH, 1)
    bbr, bbi = _ssm_bbar(qr_col, qi_col, b_r, b_i)
    shape_b = (2, SSM_GROUPS, SSM_STATE, SSM_GROUP)
    b_mat = jnp.concatenate([_block_diag_in(bbr.reshape(shape_b)), _block_diag_in(bbi.reshape(shape_b))], axis=-1).astype(BF16)
    c_mat = jnp.concatenate([_block_diag_out(sp["ssm_c_re"][0]), -_block_diag_out(sp["ssm_c_im"][0])], axis=1).astype(BF16)
    b_mat_t = jnp.swapaxes(b_mat, 1, 2)
    c_mat_t = jnp.swapaxes(c_mat, 1, 2)
    a_r = abr.reshape(2, 1, SSM_CH)
    a_i = abi.reshape(2, 1, SSM_CH)
    mem_n = _rmsnorm("norm_mem", mem, sp["mem_norm"], mem.shape[0], after=first_gather)

    (w_gu,) = wts.finish("f1a", [u1, b_mat, c_mat, b_mat_t, c_mat_t, mem_n])
    w_f1 = {"gate": (w_gu, FFN_GATE), "up": (w_gu, FFN_UP)}
    down_gather = wts.start("f1b", [w_gu])
    g1, up1, a1 = _ffn_up("ffn1_up", u1, w_f1, tm, after=wts.start("win", down_gather))
    (w_dn,) = wts.finish("f1b", [a1])
    w_f1["down"] = (w_dn, 0)
    h1 = _ffn_down("ffn1_down", a1, w_f1, x, tm)

    u2 = _rmsnorm("norm_mix", h1, sp["mix_norm"], tm)
    (w_in_g,) = wts.finish("win", [u2])
    w_in_t = w_in_g.reshape(D_FF, D_MODEL)
    proj = _mm("mix_in", [(u2, BS((tm, D_MODEL), lambda j, i: (i, 0)), w_in_t, BS((D_FF // 2, D_MODEL), lambda j, i: (j, 0)), NT)],
               grid=(2, t // tm), out_shape=SDS((t, D_FF), F32), out_spec=BS((tm, D_FF // 2), lambda j, i: (i, j)),
               after=wts.start("f2", wts.start("mix", [w_in_g])))
    pooled, mixed, ms = _pool_fwd(proj, sp["pool_w"][0], sp["pool_scale"])

    s_in = proj[:, D_POOL:D_POOL + D_SSM].astype(BF16)
    states, y_dirs = [], []
    for dr in range(2):
        st, yd = _ssm_scan(f"ssm_scan_fwd{dr}", s_in, b_mat[dr], a_r[dr], a_i[dr], c_mat[dr], reverse=(dr == 1))
        states.append(st)
        y_dirs.append(yd)
    w_sq, w_kv, w_e = wts.finish("mix", y_dirs)
    w_mo, w_q, w_xo = (w_sq[:, 256 * k:256 * (k + 1)].reshape(D_MODEL, D_MODEL) for k in range(3))
    w_d = w_kv[:, None]
    y_total, yssm = _ssm_combine(proj, y_dirs[0], y_dirs[1], sp["ssm_d"], tm)

    merged = _mixer_merge(ms, yssm, w_e, proj, tm)
    h2 = _plain_mm("mix_out", merged, w_mo, NN, F32, tm, resid=h1)

    u3 = _rmsnorm("norm_xattn", h2, sp["xattn_norm"], tm)
    q =_plain_mm("attn_q", u3, w_q, NN, BF16, tm)
    n_mem = mem.shape[0]
    kv = _mm("attn_kv", [(mem_n, BS((n_mem, D_MODEL), lambda s: (0, 0)), w_d, BS((None, None, D_MODEL, 512), lambda s: (s, 0, 0, 0)), NN)],
             grid=(N_SHARD,), out_shape=SDS((n_mem, 2 * D_MODEL), BF16), out_spec=BS((n_mem, 512), lambda s: (0, s)))
    o = _attn_fwd(q, kv, tm)
    h3 = _plain_mm("attn_out", o, w_xo, NN, F32, tm, resid=h2)

    u4 = _rmsnorm("norm_ffn2", h3, sp["ffn2_norm"], tm)
    (w_2,) = wts.finish("f2", [u4])
    w_f2 = {"gate": (w_2, FFN_GATE), "up": (w_2, FFN_UP), "down": (w_2, FFN_DOWN)}
    g2, up2, a2 = _ffn_up("ffn2_up", u4, w_f2, tm)
    h4 = _ffn_down("ffn2_down", a2, w_f2, h3, tm)

    loss, dh4, dh4_b, g["final_norm"] = _loss_head(h4, sp["final_norm"].reshape(1, D_MODEL), target, tm)

    dg2, dup2 = _ffn_bwd_act("ffn2_bwd_act", dh4_b, w_f2, g2, up2, tm)
    dw_f2 = _ffn_dw("ffn2_dw", u4, dg2, dup2, a2, dh4_b, tm)
    du4 = _ffn_dx("ffn2_dx", dg2, dup2, w_f2, tm)
    dh3, dh3_b, g["ffn2_norm"] = _rmsnorm_bwd("norm_ffn2_bwd", h3, sp["ffn2_norm"], du4, dh4, tm)

    d_o = _plain_mm("attn_out_dx", dh3_b, w_xo, NT, BF16, tm)
    dw_xo = _dw_mm("attn_out_dw", o, dh3_b, tm)
    dq, dkv = _attn_bwd(q, kv, d_o, tm)
    dw_q = _dw_mm("attn_q_dw", u3, dq, tm)
    du3 = _plain_mm("attn_q_dx", dq, w_q, NT, F32, tm)
    dw_kv = _mm("attn_kv_dw", [(mem_n, BS((n_mem, D_MODEL), lambda s: (0, 0)), dkv, BS((n_mem, 512), lambda s: (0, s)), TN)],
                grid=(N_SHARD,), out_shape=SDS((N_SHARD, D_MODEL, 512), BF16), out_spec=BS((None, D_MODEL, 512), lambda s: (s, 0, 0)))
    dmem_n = _mm("attn_kv_dx", [(dkv, BS((n_mem, 512), lambda s: (0, s)), w_d, BS((None, None, D_MODEL, 512), lambda s: (s, 0, 0, 0)), NT)],
                 grid=(N_SHARD,), red_axis=0, out_shape=SDS((n_mem, D_MODEL), F32), out_spec=BS((n_mem, D_MODEL), lambda s: (0, 0)))
    _, _, g["mem_norm"] = _rmsnorm_bwd("norm_mem_bwd", mem, sp["mem_norm"], dmem_n, None, n_mem)
    dh2, dh2_b, g["xattn_norm"] = _rmsnorm_bwd("norm_xattn_bwd", h2, sp["xattn_norm"], du3, dh3, tm)

    square = (N_SHARD, D_MODEL // N_SHARD, D_MODEL)
    early = [dw_f2.reshape(N_SHARD, 3 * FF_SH, D_MODEL), dw_xo.reshape(square), dw_q.reshape(square), dw_kv]
    swapping = reducer.swap_start("a1", early) if reducer is not None else []
    dmerged = _plain_mm("mix_out_dx", dh2_b, w_mo, NT, BF16, tm, after=swapping)
    dw_mo = _dw_mm("mix_out_dw", merged, dh2_b, tm)
    d_gp, d_gs, dzp, dzv, dzg = _mixer_merge_bwd(ms, yssm, w_e, proj, dmerged, tm)
    dw_e = _mixer_dw(ms, yssm, dzp, dzv, dzg, tm)
    d_ms, d_yt = _mixer_dx(dzp, dzv, dzg, w_e, y_total, tm)
    dp, d_scale, d_pw = _pool_bwd(d_ms, mixed, pooled, sp["pool_w"][0], sp["pool_scale"])
    g["pool_scale"] = d_scale
    g["pool_w"] = d_pw[None]

    d_yt_b = d_yt.astype(BF16)
    du_dirs, d_abr, d_abi, d_cm, d_bm = [], [], [], [], []
    for dr in range(2):
        lam, du = _ssm_scan(f"ssm_scan_bwd{dr}", d_yt_b, c_mat_t[dr], a_r[dr], -a_i[dr], b_mat_t[dr], reverse=(dr == 0))
        du_dirs.append(du)
        da_r, da_i = _ssm_da(f"ssm_da{dr}", lam, states[dr], reverse=(dr == 1))
        d_abr.append(da_r)
        d_abi.append(da_i)
        d_cm.append(_dw_mm(f"ssm_dc{dr}", states[dr], d_yt_b, tm, F32))
        d_bm.append(_dw_mm(f"ssm_db{dr}", s_in, lam, tm, F32))
    d_cm = jnp.stack(d_cm)
    d_bm = jnp.stack(d_bm)
    g["ssm_c_re"] = _diag_blocks_out(d_cm[:, :SSM_CH])[None]
    g["ssm_c_im"] = -_diag_blocks_out(d_cm[:, SSM_CH:])[None]
    g_bbr = _diag_blocks_in(d_bm[:, :, :SSM_CH]).reshape(2 * SSM_CH, SSM_GROUP)
    g_bbi = _diag_blocks_in(d_bm[:, :, SSM_CH:]).reshape(2 * SSM_CH, SSM_GROUP)
    d_qr, d_qi, d_br, d_bi = _ssm_bbar_bwd(qr_col, qi_col, b_r, b_i, g_bbr, g_bbi)
    g["ssm_b_re"] = d_br.reshape(sp["ssm_b_re"].shape)
    g["ssm_b_im"] = d_bi.reshape(sp["ssm_b_im"].shape)
    d_ar, d_ai, d_ldt = _ssm_disc_bwd(ar, ai, ldt, jnp.stack(d_abr).reshape(ar.shape), jnp.stack(d_abi).reshape(ar.shape),
                                      d_qr.reshape(ar.shape), d_qi.reshape(ar.shape))
    g["ssm_a_re"] = d_ar.reshape(sp["ssm_a_re"].shape)
    g["ssm_a_im"] = d_ai.reshape(sp["ssm_a_im"].shape)
    g["ssm_log_dt"] = d_ldt.reshape(sp["ssm_log_dt"].shape)
    ds, g["ssm_d"] = _ssm_ds(proj, d_yt, du_dirs[0], du_dirs[1], sp["ssm_d"], tm)

    d_proj = jnp.concatenate([dp, ds, d_gp, d_gs], axis=1)
    dw_in_t = _mm("mix_in_dw", [(d_proj, BS((tm, D_FF // 2), lambda j, i: (i, j)), u2, BS((tm, D_MODEL), lambda j, i: (i, 0)), TN)],
                  grid=(2, t // tm), red_axis=1, out_shape=SDS((D_FF, D_MODEL), BF16), out_spec=BS((D_FF // 2, D_MODEL), lambda j, i: (j, 0)))
    du2 = _plain_mm("mix_in_dx", d_proj, w_in_t, NN, F32, tm)
    dh1, dh1_b, g["mix_norm"] = _rmsnorm_bwd("norm_mix_bwd", h1, sp["mix_norm"], du2, dh2, tm)

    early += [dw_mo.reshape(square), dw_e, dw_in_t.reshape(N_SHARD, FF_SH, D_MODEL)]
    g["final_norm"] = g["final_norm"].reshape(D_MODEL)

    travelling = reducer.start("a", early[4:], swapped=["a1"], after=list(g.values())) if reducer is not None else []
    dg1, dup1 = _ffn_bwd_act("ffn1_bwd_act", dh1_b, w_f1, g1, up1, tm, after=travelling)
    dw_f1 = _ffn_dw("ffn1_dw", u1, dg1, dup1, a1, dh1_b, tm).reshape(N_SHARD, 3 * FF_SH, D_MODEL)
    if reducer is not None:
        travelling = reducer.start("b", [dw_f1], after=reducer.finish("a", [dw_f1]))
        travelling = travelling + reducer.join_start("a", after=travelling)
    du1 = _ffn_dx("ffn1_dx", dg1, dup1, w_f1, tm, after=travelling)
    grad_x, _, g["ffn1_norm"] = _rmsnorm_bwd("norm_ffn1_bwd", x, sp["ffn1_norm"], du1, dh1, tm)
    if reducer is not None:
        reducer.finish("b", [grad_x])
        reducer.join_finish("a", [grad_x])
    return loss, grad_x, early + [dw_f1], g


def _mesh_place():
    x, y, c = lax.axis_index("x"), lax.axis_index("y"), lax.axis_index("c")
    chips = [(1 - x, y), (x, 1 - y), (1 - x, 1 - y)]
    return x, y, c, chips


def _remote(src, dst, send_sems, recv_sems, k, to):
    return pltpu.make_async_remote_copy(src_ref=src, dst_ref=dst, send_sem=send_sems.at[k], recv_sem=recv_sems.at[k],
                                        device_id=to, device_id_type=MESH)


def _sibling_swap_halves(tag, grads, after=()):
    n = len(grads)
    after_ops, after_specs = _after_operands(after)

    def body(*refs):
        ins, outs = refs[:n], refs[n + len(after_ops):2 * n + len(after_ops)]
        send_sems, recv_sems = refs[2 * n + len(after_ops):]
        x, y, c, _ = _mesh_place()
        sibling = (x, y, 1 - c)
        copies = []
        for k in range(n):
            half = grads[k].shape[1] // 2
            theirs = pl.ds(pl.multiple_of((1 - c) * half, 16), half)
            cp = _remote(ins[k].at[:, theirs, :], outs[k], send_sems, recv_sems, k, sibling)
            cp.start()
            copies.append(cp)
        for cp in copies:
            cp.wait_recv()
        for cp in copies:
            cp.wait_send()

    hbm = BS(memory_space=pl.ANY)
    return pl.pallas_call(
        body, out_shape=tuple(SDS((g.shape[0], g.shape[1] // 2, g.shape[2]), g.dtype) for g in grads),
        in_specs=[hbm] * n + after_specs, out_specs=(hbm,) * n,
        scratch_shapes=[pltpu.SemaphoreType.DMA((n,)), pltpu.SemaphoreType.DMA((n,))],
        name="reduce_sibling_send_" + tag, compiler_params=_params())(*grads, *after_ops)


def _row_tile(rows, cap=512):
    return max(r for r in range(16, cap + 1, 16) if rows % r == 0)


def _chip_presum(k, grad, got, c_idx):
    n_sh, rows, cols = grad.shape
    half = rows // 2
    tr = _row_tile(half)
    grad4 = grad.reshape(n_sh, 2, half, cols)

    def body(c_ref, a_ref, b_ref, o_ref):
        o_ref[...] = (a_ref[...].astype(F32) + b_ref[...].astype(F32)).astype(o_ref.dtype)

    return pl.pallas_call(
        body, out_shape=SDS((n_sh, half, cols), BF16),
        grid_spec=pltpu.PrefetchScalarGridSpec(
            num_scalar_prefetch=1, grid=(n_sh, half // tr),
            in_specs=[BS((None, None, tr, cols), lambda s, i, c_ref: (s, c_ref[0], i, 0)),
                      BS((None, tr, cols), lambda s, i, c_ref: (s, i, 0))],
            out_specs=BS((None, tr, cols), lambda s, i, c_ref: (s, i, 0))),
        name=f"reduce_presum{k}", compiler_params=_params("parallel", "parallel"))(c_idx, grad4, got)


HBM_SPEC = BS(memory_space=pltpu.HBM)
SEM_SPEC = BS(memory_space=pltpu.SEMAPHORE)
DATAFLOW = pltpu.SideEffectType.DATAFLOW_SIDE_EFFECTING


def _chip_exchange_copies(parts, lands, send_sems, recv_sems):
    _, _, c, chips = _mesh_place()
    return [_remote(parts[k].at[2 * px + py], lands[k].at[j], send_sems, recv_sems, 3 * k + j, (px, py, c))
            for k in range(len(parts)) for j, (px, py) in enumerate(chips)]


def _gather_copies(shards, lands, send_sems, recv_sems):
    x, y, c, chips = _mesh_place()
    return [_remote(shards[k], lands[k].at[2 * x + y], send_sems, recv_sems, 3 * k + j, (px, py, c))
            for k in range(len(shards)) for j, (px, py) in enumerate(chips)]


def _swap_copies(grads, lands, send_sems, recv_sems):
    x, y, c, _ = _mesh_place()
    out = []
    for k in range(len(grads)):
        half = grads[k].shape[1] // 2
        theirs = pl.ds(pl.multiple_of((1 - c) * half, 16), half)
        out.append(_remote(grads[k].at[:, theirs, :], lands[k], send_sems, recv_sems, k, (x, y, 1 - c)))
    return out


def _join_copies(fulls, same, send_sems, recv_sems):
    x, y, c, _ = _mesh_place()
    out = []
    for k in range(len(fulls)):
        half = fulls[k].shape[0] // 2
        mine = fulls[k].at[pl.ds(pl.multiple_of(c * half, 8), half), :]
        out.append(_remote(mine, mine, send_sems, recv_sems, k, (x, y, 1 - c)))
    return out


def _everyone_copies(packs, lands, send_sems, recv_sems):
    x, y, c, _ = _mesh_place()
    out = []
    for k in range(len(packs)):
        for j in range(N_DEV - 1):
            bx, by, bc = (j + 1) >> 2 & 1, (j + 1) >> 1 & 1, (j + 1) & 1
            peer = (x ^ bx, y ^ by, c ^ bc)
            out.append(_remote(packs[k], lands[k].at[4 * x + 2 * y + c], send_sems, recv_sems, (N_DEV - 1) * k + j, peer))
    return out


def _split_start(name, copies, sources, land_shapes, after=(), fanout=3):
    n = len(sources)
    n_land = len(land_shapes)
    m = n + n_land
    n_sems = fanout * n
    after_ops, after_specs = _after_operands(after)

    def body(*refs):
        ins = refs[:n]
        lands = refs[n:m] if n_land else ins
        send_sems, recv_sems = refs[m + len(after_ops)], refs[m + len(after_ops) + 1]
        token = refs[-1]
        for cp in copies(ins, lands, send_sems, recv_sems):
            cp.start()
        token[...] = jnp.zeros_like(token)

    lands = [pltpu.with_memory_space_constraint(lax.empty(s, d), pltpu.HBM) for s, d in land_shapes]
    sources = [pltpu.with_memory_space_constraint(p, pltpu.HBM) for p in sources]
    thru = [pltpu.HBM(a.shape, a.dtype) for a in sources + lands]
    out = pl.pallas_call(
        body, name=name,
        out_shape=(pltpu.SemaphoreType.DMA((n_sems,)), pltpu.SemaphoreType.DMA((n_sems,)), *thru, SDS((8, 128), F32)),
        in_specs=[HBM_SPEC] * m + after_specs,
        out_specs=(SEM_SPEC, SEM_SPEC, *[HBM_SPEC] * m, BS(memory_space=pltpu.VMEM)),
        input_output_aliases={i: 2 + i for i in range(m)},
        compiler_params=pltpu.CompilerParams(has_side_effects=DATAFLOW))(*sources, *lands, *after_ops)
    return out[0], out[1], list(out[2:2 + n]), list(out[2 + n:2 + m]), out[-1]


def _split_wait(name, copies, send_sems, recv_sems, sources, lands, after):
    n = len(sources)
    m = n + len(lands)
    after_ops, after_specs = _after_operands(after)

    def body(*refs):
        ins = refs[:n]
        zones = refs[n:m] if m > n else ins
        for cp in copies(ins, zones, refs[m], refs[m + 1]):
            cp.wait_send()
            cp.wait_recv()

    out = pl.pallas_call(
        body, name=name,
        out_shape=tuple(pltpu.HBM(a.shape, a.dtype) for a in sources + lands),
        in_specs=[HBM_SPEC] * m + [SEM_SPEC, SEM_SPEC] + after_specs, out_specs=(HBM_SPEC,) * m,
        input_output_aliases={i: i for i in range(m)},
        compiler_params=pltpu.CompilerParams(has_side_effects=DATAFLOW))(*sources, *lands, send_sems, recv_sems, *after_ops)
    return list(out[:n]), list(out[n:])


class _WeightGatherer:
    def __init__(self, shards):
        self.shards, self.open = shards, {}
        self.me = 2 * lax.axis_index("x") + lax.axis_index("y")

    def start(self, tag, after=()):
        shapes = [((N_SHARD,) + s.shape, s.dtype) for s in self.shards[tag]]
        self.open[tag] = _split_start("gather_start_" + tag, _gather_copies, self.shards[tag], shapes, after)
        return [self.open[tag][-1]]

    def finish(self, tag, after):
        send_sems, recv_sems, shards, lands, _ = self.open.pop(tag)
        shards, lands = _split_wait("gather_wait_" + tag, _gather_copies, send_sems, recv_sems, shards, lands, after)
        return [lax.dynamic_update_slice(zone, s[None], (self.me, 0, 0)) for zone, s in zip(lands, shards)]


class _GradReducer:
    def __init__(self):
        self.c_idx = lax.axis_index("c").astype(jnp.int32).reshape(1)
        self.place = jnp.stack([2 * lax.axis_index("x") + lax.axis_index("y"), lax.axis_index("c")]).astype(jnp.int32)
        self.swaps, self.open, self.landed, self.joins, self.reduced = {}, {}, {}, {}, []

    def swap_start(self, tag, grads, after=()):
        shapes = [((g.shape[0], g.shape[1] // 2, g.shape[2]), g.dtype) for g in grads]
        self.swaps[tag] = _split_start("reduce_swap_start_" + tag, _swap_copies, grads, shapes, after, fanout=1)
        return [self.swaps[tag][-1]]

    def start(self, tag, grads, after=(), swapped=()):
        pairs = []
        for s in swapped:
            send_sems, recv_sems, early, lands, _ = self.swaps.pop(s)
            pairs += zip(*_split_wait("reduce_swap_wait_" + s, _swap_copies, send_sems, recv_sems, early, lands, grads[-1:]))
        pairs += zip(grads, _sibling_swap_halves(tag, grads, after))
        parts = [_chip_presum(f"{tag}{k}", g, s, self.c_idx) for k, (g, s) in enumerate(pairs)]
        shapes = [((3,) + p.shape[1:], p.dtype) for p in parts]
        self.open[tag] = _split_start("reduce_exchange_start_" + tag, _chip_exchange_copies, parts, shapes)
        return [self.open[tag][-1]]

    def finish(self, tag, after):
        send_sems, recv_sems, parts, lands, _ = self.open.pop(tag)
        self.landed[tag] = _split_wait("reduce_exchange_wait_" + tag, _chip_exchange_copies, send_sems, recv_sems, parts, lands, after)
        return self.landed[tag][1][:1]

    def _sums(self, tag, after=()):
        parts, landed = self.landed.pop(tag)
        return [_chip_sum(f"{tag}{k}", p, got, self.place, after) for k, (p, got) in enumerate(zip(parts, landed))]

    def join_start(self, tag, after=()):
        self.joins[tag] = _split_start("reduce_join_start_" + tag, _join_copies, self._sums(tag, after), [], fanout=1)
        return [self.joins[tag][-1]]

    def join_finish(self, tag, after):
        send_sems, recv_sems, fulls, _, _ = self.joins.pop(tag)
        self.reduced += _split_wait("reduce_join_wait_" + tag, _join_copies, send_sems, recv_sems, fulls, [], after)[0]

    def join(self, tag, after=()):
        self.reduced += _sibling_join_halves(self._sums(tag), after)


def _chip_sum(k, part, got, place, after=()):
    _, half, cols = part.shape
    tr = _row_tile(half)
    n_t = half // tr
    after_ops, after_specs = _after_operands(after)

    def body(place_ref, a_ref, b_ref, *rest):
        o_ref = rest[-1]
        acc = a_ref[...].astype(F32)
        for j in range(3):
            acc = acc + b_ref[j].astype(F32)
        o_ref[...] = acc

    return pl.pallas_call(
        body, out_shape=SDS((2 * half, cols), F32),
        grid_spec=pltpu.PrefetchScalarGridSpec(
            num_scalar_prefetch=1, grid=(n_t,),
            in_specs=[BS((None, tr, cols), lambda i, place_ref: (place_ref[0], i, 0)),
                      BS((3, tr, cols), lambda i, place_ref: (0, i, 0))] + after_specs,
            out_specs=BS((tr, cols), lambda i, place_ref: (place_ref[1] * n_t + i, 0))),
        name=f"reduce_sum{k}", compiler_params=_params("parallel"))(place, part, got, *after_ops)


def _sibling_join_halves(fulls, after=()):
    n = len(fulls)
    after_ops, after_specs = _after_operands(after)

    def body(*refs):
        outs = refs[n + len(after_ops):2 * n + len(after_ops)]
        send_sems, recv_sems = refs[2 * n + len(after_ops):]
        copies = _join_copies(outs, outs, send_sems, recv_sems)
        for cp in copies:
            cp.start()
        for cp in copies:
            cp.wait_recv()
        for cp in copies:
            cp.wait_send()

    hbm = BS(memory_space=pl.ANY)
    return list(pl.pallas_call(
        body, out_shape=tuple(SDS(f.shape, f.dtype) for f in fulls),
        in_specs=[hbm] * n + after_specs, out_specs=(hbm,) * n, input_output_aliases={k: k for k in range(n)},
        scratch_shapes=[pltpu.SemaphoreType.DMA((n,)), pltpu.SemaphoreType.DMA((n,))],
        name="reduce_sibling_join", compiler_params=_params())(*fulls, *after_ops))


N_DEV = 8


def _sum_devices(packs):
    _, rows, lanes = packs.shape

    def body(p_ref, o_ref):
        acc = p_ref[0]
        for dev in range(1, N_DEV):
            acc = acc + p_ref[dev]
        o_ref[...] = acc

    vm = BS(memory_space=pltpu.VMEM)
    return pl.pallas_call(body, out_shape=SDS((rows, lanes), F32), in_specs=[vm], out_specs=vm,
                          name="small_sum", compiler_params=_params())(packs)


def _adamw(name, w, grad, row0, m, v, after=()):
    rows, cols = w.shape
    tr = rows if rows < 16 else _row_tile(rows, 256)
    bc1 = 1.0 - ADAM_B1 ** ADAM_STEP
    bc2 = 1.0 - ADAM_B2 ** ADAM_STEP
    after_ops, after_specs = _after_operands(after)

    def body(w_ref, g_ref, m_ref, v_ref, *rest):
        go_ref, d_ref, mo_ref, vo_ref = rest[len(after_ops):]
        g = g_ref[...]
        m_new = ADAM_B1 * m_ref[...] + (1.0 - ADAM_B1) * g
        v_new = ADAM_B2 * v_ref[...] + (1.0 - ADAM_B2) * (g * g)
        go_ref[...] = g
        mo_ref[...] = m_new
        vo_ref[...] = v_new
        d_ref[...] = -ADAM_LR * ((m_new / bc1) / (jnp.sqrt(v_new / bc2) + ADAM_EPS) + ADAM_WD * w_ref[...])

    blk = BS((tr, cols), lambda i: (i, 0))
    shape = SDS((rows, cols), F32)
    return pl.pallas_call(
        body, out_shape=(shape,) * 4, grid=(rows // tr,),
        in_specs=[blk, BS((tr, cols), lambda i: (row0 // tr + i, 0)), blk, blk] + after_specs, out_specs=(blk,) * 4,
        name=name, compiler_params=_params("parallel"))(w, grad, m, v, *after_ops)


SMALL_LANES = 128


def _pack_small(parts):
    flat = jnp.concatenate([jnp.ravel(p) for p in parts])
    rows = -(-flat.shape[0] // (64 * SMALL_LANES)) * 64
    return jnp.pad(flat, (0, rows * SMALL_LANES - flat.shape[0])).reshape(rows, SMALL_LANES)


def _unpack_small(packed, like):
    flat = jnp.ravel(packed)
    out, at = [], 0
    for p in like:
        out.append(flat[at:at + p.size].reshape(p.shape))
        at += p.size
    return out


def kernel(x, mem, ffn1_norm, ffn1_w_gate, ffn1_w_up, ffn1_w_down, mix_norm, w_in, pool_w, pool_scale, w_pool_proj, ssm_a_re, ssm_a_im, ssm_log_dt, ssm_b_re, ssm_b_im, ssm_c_re, ssm_c_im, ssm_d, w_glu_val, w_glu_gate, w_mix_out, xattn_norm, mem_norm, w_q, w_kv, w_xo, ffn2_norm, ffn2_w_gate, ffn2_w_up, ffn2_w_down, final_norm, loss_target, m_ffn1_norm, m_ffn1_w_gate, m_ffn1_w_up, m_ffn1_w_down, m_mix_norm, m_w_in, m_pool_w, m_pool_scale, m_w_pool_proj, m_ssm_a_re, m_ssm_a_im, m_ssm_log_dt, m_ssm_b_re, m_ssm_b_im, m_ssm_c_re, m_ssm_c_im, m_ssm_d, m_w_glu_val, m_w_glu_gate, m_w_mix_out, m_xattn_norm, m_mem_norm, m_w_q, m_w_kv, m_w_xo, m_ffn2_norm, m_ffn2_w_gate, m_ffn2_w_up, m_ffn2_w_down, m_final_norm, v_ffn1_norm, v_ffn1_w_gate, v_ffn1_w_up, v_ffn1_w_down, v_mix_norm, v_w_in, v_pool_w, v_pool_scale, v_w_pool_proj, v_ssm_a_re, v_ssm_a_im, v_ssm_log_dt, v_ssm_b_re, v_ssm_b_im, v_ssm_c_re, v_ssm_c_im, v_ssm_d, v_w_glu_val, v_w_glu_gate, v_w_mix_out, v_xattn_norm, v_mem_norm, v_w_q, v_w_kv, v_w_xo, v_ffn2_norm, v_ffn2_w_gate, v_ffn2_w_up, v_ffn2_w_down, v_final_norm):
    given = dict(locals())
    w = {n: given[n] for n in WEIGHTS}
    m = {n: given["m_" + n] for n in WEIGHTS}
    v = {n: given["v_" + n] for n in WEIGHTS}

    def shard_view(a, n):
        return a[0].T if n in TRANSPOSED else a[0]

    def shard_unview(a, n):
        return (a.T if n in TRANSPOSED else a)[None]

    shards = {tag: [jnp.concatenate([shard_view(w[n], n).astype(BF16) for n in grp], axis=0) for grp in arrays]
              for tag, arrays in GATHER_PHASES.items()}
    reducer = _GradReducer()
    loss_part, grad_x, _, small = _device_step(x[0], mem[0], loss_target[0], _WeightGatherer(shards),
                                               {n: w[n] for n in SMALL}, reducer)
    loss = lax.psum(loss_part[0, 0], ("x", "y", "c"))

    pack = _pack_small([small[n] for n in SMALL])
    everyone = _split_start("small_start", _everyone_copies, [pack], [((N_DEV,) + pack.shape, F32)], fanout=N_DEV - 1)
    reducer.join("b", after=everyone[-1:])

    grads, delta, new_m, new_v = {}, {}, {}, {}
    for grp, red in zip(REDUCE_GROUPS, reducer.reduced):
        row0 = 0
        for n in grp:
            w_n = shard_view(w[n], n)
            outs = _adamw("adamw_" + n, w_n, red, row0, shard_view(m[n], n), shard_view(v[n], n), after=everyone[-1:])
            grads[n], delta[n], new_m[n], new_v[n] = (shard_unview(o, n) for o in outs)
            row0 += w_n.shape[0]

    send_sems, recv_sems, packs, landed, _ = everyone
    packs, landed = _split_wait("small_wait", _everyone_copies, send_sems, recv_sems, packs, landed,
                                [delta[n] for grp in REDUCE_GROUPS for n in grp])
    mine = 4 * lax.axis_index("x") + 2 * lax.axis_index("y") + lax.axis_index("c")
    summed = _sum_devices(lax.dynamic_update_slice(landed[0], packs[0][None], (mine, 0, 0)))
    g_small = dict(zip(SMALL, _unpack_small(summed, [w[n] for n in SMALL])))
    narrow = [n for n in SMALL if w[n].ndim > 3]
    dense = [n for n in SMALL if n not in narrow]
    for n in narrow:
        two_d = (-1, w[n].shape[-1])
        outs = _adamw("adamw_" + n, w[n].reshape(two_d), g_small[n].reshape(two_d), 0, m[n].reshape(two_d), v[n].reshape(two_d))
        grads[n], delta[n], new_m[n], new_v[n] = (o.reshape(w[n].shape) for o in outs)
    dense_like = [w[n] for n in dense]
    packed = _adamw("adamw_small", _pack_small(dense_like), _pack_small([g_small[n] for n in dense]), 0,
                    _pack_small([m[n] for n in dense]), _pack_small([v[n] for n in dense]))
    for out, store in zip(packed, (grads, delta, new_m, new_v)):
        for n, val in zip(dense, _unpack_small(out, dense_like)):
            store[n] = val

    return (loss, grad_x[None], *[grads[n] for n in WEIGHTS], *[delta[n] for n in WEIGHTS],
            *[new_m[n] for n in WEIGHTS], *[new_v[n] for n in WEIGHTS])
```

```python
import functools
import math

import jax
import jax.numpy as jnp
from jax import lax
from jax.experimental import pallas as pl
from jax.experimental.pallas import tpu as pltpu

F32 = jnp.float32
BF16 = jnp.bfloat16
SDS = jax.ShapeDtypeStruct
BS = pl.BlockSpec
MESH = pl.DeviceIdType.MESH

D_MODEL = 1024
D_FF = 2816
N_SHARD = 4
FF_SH = D_FF // N_SHARD
D_POOL = 512
POOL_WINDOWS = (2, 4, 8, 16)
POOL_GROUP = 128
D_SSM = 256
SSM_GROUPS = 16
SSM_GROUP = 16
SSM_STATE = 64
SSM_CH = SSM_GROUPS * SSM_STATE
N_HEADS = 4
HEAD_DIM = 256
EPS = 1e-6
ADAM_LR, ADAM_B1, ADAM_B2, ADAM_EPS, ADAM_WD, ADAM_STEP = 0.001, 0.9, 0.999, 1e-08, 0.01, 10

VMEM_LIMIT_V7X = 52 * 1024 * 1024
TM = 512

NN = (((1,), (0,)), ((), ()))
NT = (((1,), (1,)), ((), ()))
TN = (((0,), (0,)), ((), ()))


def _params(*sem):
    return pltpu.CompilerParams(dimension_semantics=sem if sem else None, vmem_limit_bytes=VMEM_LIMIT_V7X)


def _dot(a, b, dims=NN):
    return lax.dot_general(a.astype(BF16), b.astype(BF16), dims, preferred_element_type=F32)


def _sigmoid(v):
    return pl.reciprocal(1.0 + jnp.exp(-v), approx=True)


def _block_dims(spec):
    return tuple(d for d in spec.block_shape if d is not None)


def _after_operands(after):
    return list(after), [BS(memory_space=pl.ANY)] * len(after)


def _mm(name, pairs, *, grid, out_shape, out_spec, red_axis=None, extras=(), epilogue=None, after=()):
    n_pairs, n_extra = len(pairs), len(extras)
    n_red = grid[red_axis] if red_axis is not None else 1
    dims = [p[4] for p in pairs]

    def body(*refs):
        ab = refs[:2 * n_pairs]
        ex = refs[2 * n_pairs:2 * n_pairs + n_extra]
        o_ref = refs[2 * n_pairs + n_extra + len(after)]

        def partial():
            acc = None
            for p in range(n_pairs):
                t = _dot(ab[2 * p][...], ab[2 * p + 1][...], dims[p])
                acc = t if acc is None else acc + t
            return acc

        def finish(acc):
            res = epilogue(acc, *[e[...] for e in ex]) if epilogue is not None else acc
            o_ref[...] = res.astype(o_ref.dtype)

        if n_red == 1:
            finish(partial())
        else:
            acc_ref = refs[-1]
            k = pl.program_id(red_axis)

            @pl.when(k == 0)
            def _():
                acc_ref[...] = jnp.zeros_like(acc_ref)

            acc_ref[...] += partial()

            @pl.when(k == n_red - 1)
            def _():
                finish(acc_ref[...])

    operands, in_specs = [], []
    for a, a_spec, b, b_spec, _ in pairs:
        operands += [a, b]
        in_specs += [a_spec, b_spec]
    for e, e_spec in extras:
        operands.append(e)
        in_specs.append(e_spec)
    after_ops, after_specs = _after_operands(after)
    operands += after_ops
    in_specs += after_specs
    scratch = [pltpu.VMEM(_block_dims(out_spec), F32)] if n_red > 1 else []
    sem = tuple("arbitrary" if ax == red_axis else "parallel" for ax in range(len(grid)))
    return pl.pallas_call(body, out_shape=out_shape, grid=grid, in_specs=in_specs, out_specs=out_spec,
                          scratch_shapes=scratch, name=name, compiler_params=_params(*sem))(*operands)


def _rmsnorm(name, h, gain, tm, after=()):
    t, d = h.shape
    after_ops, after_specs = _after_operands(after)

    def body(h_ref, g_ref, *rest):
        u_ref = rest[-1]
        hv = h_ref[...]
        r = lax.rsqrt(jnp.mean(hv * hv, axis=-1, keepdims=True) + EPS)
        u_ref[...] = ((hv * r) * g_ref[...]).astype(u_ref.dtype)

    return pl.pallas_call(
        body, out_shape=SDS((t, d), BF16), grid=(t // tm,),
        in_specs=[BS((tm, d), lambda i: (i, 0)), BS((1, d), lambda i: (0, 0))] + after_specs,
        out_specs=BS((tm, d), lambda i: (i, 0)), name=name, compiler_params=_params("parallel"))(h, gain, *after_ops)


def _rmsnorm_bwd(name, h, gain, du, dh_in, tm):
    t, d = h.shape
    has_in = dh_in is not None

    def body(*refs):
        if has_in:
            h_ref, g_ref, du_ref, dhin_ref, dh_ref, dhb_ref, dg_ref = refs
        else:
            h_ref, g_ref, du_ref, dh_ref, dhb_ref, dg_ref = refs
        i = pl.program_id(0)
        hv = h_ref[...]
        r = lax.rsqrt(jnp.mean(hv * hv, axis=-1, keepdims=True) + EPS)
        n = hv * r
        duv = du_ref[...].astype(F32)
        dn = duv * g_ref[...]
        dh = r * (dn - n * jnp.mean(dn * n, axis=-1, keepdims=True))
        if has_in:
            dh = dhin_ref[...] + dh
        dh_ref[...] = dh
        dhb_ref[...] = dh.astype(BF16)

        @pl.when(i == 0)
        def _():
            dg_ref[...] = jnp.zeros_like(dg_ref)

        dg_ref[...] += jnp.sum(duv * n, axis=0, keepdims=True)

    row = BS((tm, d), lambda i: (i, 0))
    vec = BS((1, d), lambda i: (0, 0))
    operands = [h, gain, du] + ([dh_in] if has_in else [])
    in_specs = [row, vec, row] + ([row] if has_in else [])
    return pl.pallas_call(
        body, out_shape=(SDS((t, d), F32), SDS((t, d), BF16), SDS((1, d), F32)), grid=(t // tm,),
        in_specs=in_specs, out_specs=(row, row, vec), name=name, compiler_params=_params("arbitrary"))(*operands)


def _loss_head(h, gain, target, tm):
    t, d = h.shape

    def body(h_ref, g_ref, t_ref, loss_ref, dh_ref, dhb_ref, dg_ref):
        i = pl.program_id(0)
        hv = h_ref[...]
        g = g_ref[...]
        r = lax.rsqrt(jnp.mean(hv * hv, axis=-1, keepdims=True) + EPS)
        n = hv * r
        err = n * g - t_ref[...]
        dy = err * (1.0 / d)
        dn = dy * g
        dh = r * (dn - n * jnp.mean(dn * n, axis=-1, keepdims=True))
        dh_ref[...] = dh
        dhb_ref[...] = dh.astype(BF16)

        @pl.when(i == 0)
        def _():
            dg_ref[...] = jnp.zeros_like(dg_ref)
            loss_ref[...] = jnp.zeros_like(loss_ref)

        dg_ref[...] += jnp.sum(dy * n, axis=0, keepdims=True)
        part = 0.5 * jnp.sum(jnp.mean(err * err, axis=-1, keepdims=True), axis=0, keepdims=True)
        loss_ref[...] += jnp.broadcast_to(part, loss_ref.shape)

    row = BS((tm, d), lambda i: (i, 0))
    vec = BS((1, d), lambda i: (0, 0))
    return pl.pallas_call(
        body, out_shape=(SDS((1, 128), F32), SDS((t, d), F32), SDS((t, d), BF16), SDS((1, d), F32)),
        grid=(t // tm,), in_specs=[row, vec, row],
        out_specs=(BS((1, 128), lambda i: (0, 0)), row, row, vec),
        name="loss_head", compiler_params=_params("arbitrary"))(h, gain, target)


FFN_GATE, FFN_UP, FFN_DOWN = 0, 1, 2


def _ffn_w_spec(block, index):
    return BS((None, FF_SH, D_MODEL), lambda *g: (g[index], block, 0))


def _ffn_up(name, u, w_f, tm, after=()):
    t, d = u.shape
    after_ops, after_specs = _after_operands(after)

    def body(u_ref, wg_ref, wu_ref, *rest):
        g_ref, up_ref, a_ref = rest[len(after_ops):]
        uv = u_ref[...]
        g = _dot(uv, wg_ref[...], NT)
        up = _dot(uv, wu_ref[...], NT)
        g_ref[...] = g.astype(BF16)
        up_ref[...] = up.astype(BF16)
        a_ref[...] = (g * _sigmoid(g) * up).astype(BF16)

    hid = BS((None, tm, FF_SH), lambda s, i: (s, i, 0))
    shape = SDS((N_SHARD, t, FF_SH), BF16)
    return pl.pallas_call(
        body, out_shape=(shape, shape, shape), grid=(N_SHARD, t // tm),
        in_specs=[BS((tm, d), lambda s, i: (i, 0)), _ffn_w_spec(w_f["gate"][1], 0), _ffn_w_spec(w_f["up"][1], 0)] + after_specs,
        out_specs=(hid, hid, hid), name=name,
        compiler_params=_params("parallel", "parallel"))(u, w_f["gate"][0], w_f["up"][0], *after_ops)


def _ffn_all_shards_spec(block):
    return BS((N_SHARD, FF_SH, D_MODEL), lambda i: (0, block, 0))


def _ffn_down(name, a, w_f, resid, tm, after=()):
    t, d = resid.shape
    after_ops, after_specs = _after_operands(after)

    def body(a_ref, w_ref, res_ref, *rest):
        o_ref = rest[-1]
        acc = _dot(a_ref[0], w_ref[0])
        for s in range(1, N_SHARD):
            acc = acc + _dot(a_ref[s], w_ref[s])
        o_ref[...] = res_ref[...] + 0.5 * acc

    row = BS((tm, d), lambda i: (i, 0))
    return pl.pallas_call(
        body, out_shape=SDS((t, d), F32), grid=(t // tm,),
        in_specs=[BS((N_SHARD, tm, FF_SH), lambda i: (0, i, 0)), _ffn_all_shards_spec(w_f["down"][1]), row] + after_specs,
        out_specs=row, name=name, compiler_params=_params("parallel"))(a, w_f["down"][0], resid, *after_ops)


def _ffn_bwd_act(name, dh_b, w_f, g, up, tm, after=()):
    t, d = dh_b.shape
    after_ops, after_specs = _after_operands(after)

    def body(dh_ref, wd_ref, g_ref, up_ref, *rest):
        dg_ref, dup_ref = rest[len(after_ops):]
        da = 0.5 * _dot(dh_ref[...], wd_ref[...], NT)
        gv = g_ref[...].astype(F32)
        uv = up_ref[...].astype(F32)
        sg = _sigmoid(gv)
        das = da * sg
        dg_ref[...] = (das * uv * (1.0 + gv * (1.0 - sg))).astype(BF16)
        dup_ref[...] = (das * gv).astype(BF16)

    hid = BS((None, tm, FF_SH), lambda s, i: (s, i, 0))
    shape = SDS((N_SHARD, t, FF_SH), BF16)
    return pl.pallas_call(
        body, out_shape=(shape, shape), grid=(N_SHARD, t // tm),
        in_specs=[BS((tm, d), lambda s, i: (i, 0)), _ffn_w_spec(w_f["down"][1], 0), hid, hid] + after_specs,
        out_specs=(hid, hid), name=name,
        compiler_params=_params("parallel", "parallel"))(dh_b, w_f["down"][0], g, up, *after_ops)


def _ffn_dw(name, u, dg, dup, a, dh_b, tm):
    t, d = u.shape
    n_t = t // tm

    def body(u_ref, dg_ref, dup_ref, a_ref, dh_ref, o_ref, acc):
        i = pl.program_id(1)

        @pl.when(i == 0)
        def _():
            acc[...] = jnp.zeros_like(acc)

        uv = u_ref[...]
        acc[FFN_GATE] += _dot(dg_ref[...], uv, TN)
        acc[FFN_UP] += _dot(dup_ref[...], uv, TN)
        acc[FFN_DOWN] += _dot(a_ref[...], dh_ref[...], TN)

        @pl.when(i == n_t - 1)
        def _():
            o_ref[FFN_GATE] = acc[FFN_GATE].astype(BF16)
            o_ref[FFN_UP] = acc[FFN_UP].astype(BF16)
            o_ref[FFN_DOWN] = (0.5 * acc[FFN_DOWN]).astype(BF16)

    hid = BS((None, tm, FF_SH), lambda s, i: (s, i, 0))
    row = BS((tm, d), lambda s, i: (i, 0))
    return pl.pallas_call(
        body, out_shape=SDS((N_SHARD, 3, FF_SH, d), BF16), grid=(N_SHARD, n_t),
        in_specs=[row, hid, hid, hid, row], out_specs=BS((None, 3, FF_SH, d), lambda s, i: (s, 0, 0, 0)),
        scratch_shapes=[pltpu.VMEM((3, FF_SH, d), F32)],
        name=name, compiler_params=_params("parallel", "arbitrary"))(u, dg, dup, a, dh_b)


def _ffn_dx(name, dg, dup, w_f, tm, after=()):
    t = dg.shape[1]
    tm = tm // 2
    after_ops, after_specs = _after_operands(after)

    def body(dg_ref, dup_ref, wg_ref, wu_ref, *rest):
        o_ref = rest[-1]
        acc = _dot(dg_ref[0], wg_ref[0]) + _dot(dup_ref[0], wu_ref[0])
        for s in range(1, N_SHARD):
            acc = acc + _dot(dg_ref[s], wg_ref[s]) + _dot(dup_ref[s], wu_ref[s])
        o_ref[...] = acc

    hid = BS((N_SHARD, tm, FF_SH), lambda i: (0, i, 0))
    return pl.pallas_call(
        body, out_shape=SDS((t, D_MODEL), F32), grid=(t // tm,),
        in_specs=[hid, hid, _ffn_all_shards_spec(w_f["gate"][1]), _ffn_all_shards_spec(w_f["up"][1])] + after_specs,
        out_specs=BS((tm, D_MODEL), lambda i: (i, 0)), name=name,
        compiler_params=_params("parallel"))(dg, dup, w_f["gate"][0], w_f["up"][0], *after_ops)


def _plain_mm(name, a, b, dims, out_dtype, tm, resid=None, after=()):
    t = a.shape[0]
    n = b.shape[1] if dims == NN else b.shape[0]
    extras = [(resid, BS((tm, n), lambda i: (i, 0)))] if resid is not None else []
    epi = (lambda acc, res: res + acc) if resid is not None else None
    return _mm(name, [(a, BS((tm, a.shape[1]), lambda i: (i, 0)), b, BS(b.shape, lambda i: (0, 0)), dims)],
               grid=(t // tm,), out_shape=SDS((t, n), out_dtype), out_spec=BS((tm, n), lambda i: (i, 0)),
               extras=extras, epilogue=epi, after=after)


def _dw_mm(name, a, b, tm, out_dtype=BF16, after=()):
    t, k = a.shape
    n = b.shape[1]
    return _mm(name, [(a, BS((tm, k), lambda i: (i, 0)), b, BS((tm, n), lambda i: (i, 0)), TN)],
               grid=(t // tm,), red_axis=0, out_shape=SDS((k, n), out_dtype), out_spec=BS((k, n), lambda i: (0, 0)),
               after=after)


POOL_CHUNK = 256
POOL_HALO = 8


def _window_sum(v, width, lead):
    n = v.shape[0]
    s = v
    k = 1
    while k < width:
        s = s + pltpu.roll(s, n - k, 0)
        k *= 2
    return pltpu.roll(s, lead, 0) if lead else s


def _pool_count(base, left, right, t, shape):
    pos = base + lax.broadcasted_iota(jnp.int32, shape, 0)
    lo = jnp.maximum(pos - left, 0)
    hi = jnp.minimum(pos + right + 1, t)
    return (hi - lo).astype(F32)


def _pool_fwd(proj, pool_w, pool_scale):
    t = proj.shape[0]
    c, h = POOL_CHUNK, POOL_HALO
    n_chunks = t // c

    def body(proj_hbm, pw_ref, sc_ref, pooled_ref, mixed_ref, ms_ref, pad_ref, sem):
        cp = pltpu.make_async_copy(proj_hbm.at[:, pl.ds(0, D_POOL)], pad_ref.at[pl.ds(h, t), :], sem)
        cp.start()
        pad_ref[pl.ds(0, h), :] = jnp.zeros((h, D_POOL), F32)
        pad_ref[pl.ds(t + h, h), :] = jnp.zeros((h, D_POOL), F32)
        cp.wait()
        for g, width in enumerate(POOL_WINDOWS):
            left = width // 2
            right = width - 1 - left
            cols = slice(g * POOL_GROUP, (g + 1) * POOL_GROUP)
            wmat = pw_ref[g].astype(BF16)
            scale = sc_ref[:, cols]

            def chunk(ci, carry, left=left, right=right, width=width, cols=cols, wmat=wmat, scale=scale):
                base = pl.multiple_of(ci * c, c)
                v = pad_ref[pl.ds(base, c + 2 * h), cols]
                win = _window_sum(v, width, left)[h:h + c]
                cnt = _pool_count(base, left, right, t, (c, POOL_GROUP))
                pooled = (win / cnt - v[h:h + c]).astype(BF16)
                mixed = _dot(pooled, wmat)
                pooled_ref[pl.ds(base, c), cols] = pooled
                mixed_ref[pl.ds(base, c), cols] = mixed.astype(BF16)
                ms_ref[pl.ds(base, c), cols] = (mixed * scale).astype(BF16)
                return carry

            lax.fori_loop(0, n_chunks, chunk, 0)

    vm = BS(memory_space=pltpu.VMEM)
    shape = SDS((t, D_POOL), BF16)
    return pl.pallas_call(
        body, out_shape=(shape, shape, shape),
        in_specs=[BS(memory_space=pl.ANY), vm, vm], out_specs=(vm, vm, vm),
        scratch_shapes=[pltpu.VMEM((t + 2 * h, D_POOL), F32), pltpu.SemaphoreType.DMA],
        name="pool_fwd", compiler_params=_params())(proj, pool_w, pool_scale)


def _pool_bwd(d_ms, mixed, pooled, pool_w, pool_scale):
    t = d_ms.shape[0]
    c, h = POOL_CHUNK, POOL_HALO
    n_chunks = t // c

    def body(dms_ref, mixed_ref, pooled_ref, pw_ref, sc_ref, dp_ref, dsc_ref, dpw_ref, pad_ref):
        pad_ref[pl.ds(0, h), :] = jnp.zeros((h, D_POOL), F32)
        pad_ref[pl.ds(t + h, h), :] = jnp.zeros((h, D_POOL), F32)
        for g, width in enumerate(POOL_WINDOWS):
            left = width // 2
            right = width - 1 - left
            cols = slice(g * POOL_GROUP, (g + 1) * POOL_GROUP)
            wmat = pw_ref[g].astype(BF16)
            scale = sc_ref[:, cols]

            def first(ci, carry, left=left, right=right, cols=cols, wmat=wmat, scale=scale):
                dsc, dpw = carry
                base = pl.multiple_of(ci * c, c)
                dms = dms_ref[pl.ds(base, c), cols].astype(F32)
                dsc = dsc + jnp.sum(dms * mixed_ref[pl.ds(base, c), cols].astype(F32), axis=0, keepdims=True)
                dmix = (dms * scale).astype(BF16)
                dpw = dpw + _dot(pooled_ref[pl.ds(base, c), cols], dmix, TN)
                dpooled = _dot(dmix, wmat, NT)
                cnt = _pool_count(base, left, right, t, (c, POOL_GROUP))
                pad_ref[pl.ds(base + h, c), cols] = dpooled / cnt
                return dsc, dpw

            dsc, dpw = lax.fori_loop(0, n_chunks, first,
                                     (jnp.zeros((1, POOL_GROUP), F32), jnp.zeros((POOL_GROUP, POOL_GROUP), F32)))
            dsc_ref[:, cols] = dsc
            dpw_ref[g] = dpw

            def second(ci, carry, left=left, right=right, width=width, cols=cols):
                base = pl.multiple_of(ci * c, c)
                v = pad_ref[pl.ds(base, c + 2 * h), cols]
                win = _window_sum(v, width, right)[h:h + c]
                cnt = _pool_count(base, left, right, t, (c, POOL_GROUP))
                dp_ref[pl.ds(base, c), cols] = (win - v[h:h + c] * cnt).astype(BF16)
                return carry

            lax.fori_loop(0, n_chunks, second, 0)

    vm = BS(memory_space=pltpu.VMEM)
    return pl.pallas_call(
        body, out_shape=(SDS((t, D_POOL), BF16), SDS((1, D_POOL), F32), SDS((4, POOL_GROUP, POOL_GROUP), F32)),
        in_specs=[vm] * 5, out_specs=(vm, vm, vm),
        scratch_shapes=[pltpu.VMEM((t + 2 * h, D_POOL), F32)],
        name="pool_bwd", compiler_params=_params())(d_ms, mixed, pooled, pool_w, pool_scale)


def _ssm_disc(ar, ai, ldt, after=()):
    after_ops, after_specs = _after_operands(after)

    def body(ar_ref, ai_ref, ldt_ref, *rest):
        abr_ref, abi_ref, qr_ref, qi_ref = rest[len(after_ops):]
        a_r, a_i = ar_ref[...], ai_ref[...]
        dt = jnp.exp(ldt_ref[...])
        mag = jnp.exp(dt * a_r)
        ang = dt * a_i
        abr = mag * jnp.cos(ang)
        abi = mag * jnp.sin(ang)
        den = a_r * a_r + a_i * a_i
        nr = abr - 1.0
        abr_ref[...] = abr
        abi_ref[...] = abi
        qr_ref[...] = (nr * a_r + abi * a_i) / den
        qi_ref[...] = (abi * a_r - nr * a_i) / den

    vm = BS(memory_space=pltpu.VMEM)
    shape = SDS(ar.shape, F32)
    return pl.pallas_call(body, out_shape=(shape,) * 4, in_specs=[vm] * 3 + after_specs, out_specs=(vm,) * 4,
                          name="ssm_disc", compiler_params=_params())(ar, ai, ldt, *after_ops)


def _ssm_disc_bwd(ar, ai, ldt, d_abr, d_abi, d_qr, d_qi):
    def body(ar_ref, ai_ref, ldt_ref, gabr_ref, gabi_ref, gqr_ref, gqi_ref, dar_ref, dai_ref, dldt_ref):
        a_r, a_i = ar_ref[...], ai_ref[...]
        dt = jnp.exp(ldt_ref[...])
        mag = jnp.exp(dt * a_r)
        ang = dt * a_i
        cs, sn = jnp.cos(ang), jnp.sin(ang)
        abr, abi = mag * cs, mag * sn
        den = a_r * a_r + a_i * a_i
        nr = abr - 1.0
        qr = (nr * a_r + abi * a_i) / den
        qi = (abi * a_r - nr * a_i) / den
        gqr, gqi = gqr_ref[...], gqi_ref[...]
        g_nr_num = gqr / den
        g_ni_num = gqi / den
        g_den = -(gqr * qr + gqi * qi) / den
        g_nr = g_nr_num * a_r - g_ni_num * a_i
        g_abi = g_nr_num * a_i + g_ni_num * a_r
        d_ar = g_nr_num * nr + g_ni_num * abi + 2.0 * a_r * g_den
        d_ai = g_nr_num * abi - g_ni_num * nr + 2.0 * a_i * g_den
        g_abr = gabr_ref[...] + g_nr
        g_abi = gabi_ref[...] + g_abi
        g_mag = g_abr * cs + g_abi * sn
        g_ang = mag * (g_abi * cs - g_abr * sn)
        g_e = g_mag * mag
        d_ar = d_ar + g_e * dt
        d_ai = d_ai + g_ang * dt
        g_dt = g_e * a_r + g_ang * a_i
        dar_ref[...] = d_ar
        dai_ref[...] = d_ai
        dldt_ref[...] = jnp.sum(g_dt * dt, axis=1, keepdims=True)

    vm = BS(memory_space=pltpu.VMEM)
    return pl.pallas_call(body, out_shape=(SDS(ar.shape, F32), SDS(ar.shape, F32), SDS(ldt.shape, F32)),
                          in_specs=[vm] * 7, out_specs=(vm,) * 3, name="ssm_disc_bwd",
                          compiler_params=_params())(ar, ai, ldt, d_abr, d_abi, d_qr, d_qi)


def _ssm_bbar(qr, qi, br, bi):
    def body(qr_ref, qi_ref, br_ref, bi_ref, bbr_ref, bbi_ref):
        q_r, q_i, b_r, b_i = qr_ref[...], qi_ref[...], br_ref[...], bi_ref[...]
        bbr_ref[...] = q_r * b_r - q_i * b_i
        bbi_ref[...] = q_r * b_i + q_i * b_r

    vm = BS(memory_space=pltpu.VMEM)
    shape = SDS(br.shape, F32)
    return pl.pallas_call(body, out_shape=(shape, shape), in_specs=[vm] * 4, out_specs=(vm, vm),
                          name="ssm_bbar", compiler_params=_params())(qr, qi, br, bi)


def _ssm_bbar_bwd(qr, qi, br, bi, g_bbr, g_bbi):
    def body(qr_ref, qi_ref, br_ref, bi_ref, gr_ref, gi_ref, dqr_ref, dqi_ref, dbr_ref, dbi_ref):
        q_r, q_i, b_r, b_i = qr_ref[...], qi_ref[...], br_ref[...], bi_ref[...]
        g_r, g_i = gr_ref[...], gi_ref[...]
        dqr_ref[...] = jnp.sum(g_r * b_r + g_i * b_i, axis=1, keepdims=True)
        dqi_ref[...] = jnp.sum(g_i * b_r - g_r * b_i, axis=1, keepdims=True)
        dbr_ref[...] = g_r * q_r + g_i * q_i
        dbi_ref[...] = g_i * q_r - g_r * q_i

    vm = BS(memory_space=pltpu.VMEM)
    return pl.pallas_call(
        body, out_shape=(SDS(qr.shape, F32), SDS(qr.shape, F32), SDS(br.shape, F32), SDS(br.shape, F32)),
        in_specs=[vm] * 6, out_specs=(vm,) * 4, name="ssm_bbar_bwd",
        compiler_params=_params())(qr, qi, br, bi, g_bbr, g_bbi)


SCAN_ROWS = 256


def _ssm_scan(name, inp, w1, a_r, a_i, w2, reverse):
    t = inp.shape[0]
    rows = SCAN_ROWS
    n = t // rows
    n_groups = rows // 8
    ch = SSM_CH
    at = (lambda i: (n - 1 - i, 0)) if reverse else (lambda i: (i, 0))

    def body(in_ref, w1_ref, ar_ref, ai_ref, w2_ref, sb_ref, out_ref, cr_ref, ci_ref, k_ref, st_ref):
        i = pl.program_id(0)

        @pl.when(i == 0)
        def _():
            ar8 = jnp.broadcast_to(ar_ref[...], (8, ch))
            ai8 = jnp.broadcast_to(ai_ref[...], (8, ch))
            row = lax.broadcasted_iota(jnp.int32, (8, ch), 0)
            rank = (7 - row) if reverse else row
            powers = [(ar8, ai8)]
            for _ in range(7):
                p_r, p_i = powers[-1]
                powers.append((p_r * ar8 - p_i * ai8, p_r * ai8 + p_i * ar8))
            zero = jnp.zeros((8, ch), F32)
            for slot, k in enumerate((1, 2, 4)):
                k_ref[2 * slot] = jnp.where(rank >= k, powers[k - 1][0], zero)
                k_ref[2 * slot + 1] = jnp.where(rank >= k, powers[k - 1][1], zero)
            carry_r, carry_i = zero, zero
            for j in range(8):
                carry_r = jnp.where(rank == j, powers[j][0], carry_r)
                carry_i = jnp.where(rank == j, powers[j][1], carry_i)
            k_ref[6] = carry_r
            k_ref[7] = carry_i
            cr_ref[...] = zero
            ci_ref[...] = zero

        st_ref[...] = _dot(in_ref[...], w1_ref[...])

        def group(gi, carry):
            c_r, c_i = carry
            g = (n_groups - 1 - gi) if reverse else gi
            r0 = pl.multiple_of(g * 8, 8)
            x_r = st_ref[pl.ds(r0, 8), 0:ch]
            x_i = st_ref[pl.ds(r0, 8), ch:2 * ch]
            for slot, k in enumerate((1, 2, 4)):
                shift = (8 - k) if reverse else k
                s_r = pltpu.roll(x_r, shift, 0)
                s_i = pltpu.roll(x_i, shift, 0)
                m_r, m_i = k_ref[2 * slot], k_ref[2 * slot + 1]
                x_r, x_i = x_r + m_r * s_r - m_i * s_i, x_i + m_r * s_i + m_i * s_r
            p_r, p_i = k_ref[6], k_ref[7]
            x_r, x_i = x_r + p_r * c_r - p_i * c_i, x_i + p_r * c_i + p_i * c_r
            st_ref[pl.ds(r0, 8), 0:ch] = x_r
            st_ref[pl.ds(r0, 8), ch:2 * ch] = x_i
            last = 0 if reverse else 7
            return (jnp.broadcast_to(x_r[last:last + 1, :], (8, ch)), jnp.broadcast_to(x_i[last:last + 1, :], (8, ch)))

        c_r, c_i = lax.fori_loop(0, n_groups, group, (cr_ref[...], ci_ref[...]))
        cr_ref[...] = c_r
        ci_ref[...] = c_i
        states = st_ref[...].astype(BF16)
        sb_ref[...] = states
        out_ref[...] = _dot(states, w2_ref[...])

    return pl.pallas_call(
        body, out_shape=(SDS((t, 2 * ch), BF16), SDS((t, D_SSM), F32)), grid=(n,),
        in_specs=[BS((rows, D_SSM), at), BS((D_SSM, 2 * ch), lambda i: (0, 0)), BS((1, ch), lambda i: (0, 0)),
                  BS((1, ch), lambda i: (0, 0)), BS((2 * ch, D_SSM), lambda i: (0, 0))],
        out_specs=(BS((rows, 2 * ch), at), BS((rows, D_SSM), at)),
        scratch_shapes=[pltpu.VMEM((8, ch), F32), pltpu.VMEM((8, ch), F32), pltpu.VMEM((8, 8, ch), F32),
                        pltpu.VMEM((rows, 2 * ch), F32)],
        name=name, compiler_params=_params("arbitrary"))(inp, w1, a_r, a_i, w2)


DA_ROWS = 512


def _ssm_da(name, lam, states, reverse, after=()):
    t = lam.shape[0]
    rows = DA_ROWS
    n = t // rows
    halo_rows = 16
    nb = rows // halo_rows
    ch = SSM_CH
    if reverse:
        halo_at = lambda i: (jnp.minimum((i + 1) * nb, t // halo_rows - 1), 0)
    else:
        halo_at = lambda i: (jnp.maximum(i * nb - 1, 0), 0)

    after_ops, after_specs = _after_operands(after)

    def body(lam_ref, x_ref, halo_ref, *rest):
        dr_ref, di_ref = rest[len(after_ops):]
        i = pl.program_id(0)

        @pl.when(i == 0)
        def _():
            dr_ref[...] = jnp.zeros_like(dr_ref)
            di_ref[...] = jnp.zeros_like(di_ref)

        row = lax.broadcasted_iota(jnp.int32, (rows, ch), 0)
        if reverse:
            edge, shift, h_row, live = rows - 1, rows - 1, 0, i < n - 1
        else:
            edge, shift, h_row, live = 0, 1, halo_rows - 1, i > 0

        def neighbour(lo):
            halo = halo_ref[:, lo:lo + ch].astype(F32)[h_row:h_row + 1]
            halo = jnp.where(live, halo, 0.0)
            x = x_ref[:, lo:lo + ch].astype(F32)
            return jnp.where(row == edge, jnp.broadcast_to(halo, (rows, ch)), pltpu.roll(x, shift, 0))

        xp_r, xp_i = neighbour(0), neighbour(ch)
        l_r, l_i = lam_ref[:, 0:ch].astype(F32), lam_ref[:, ch:2 * ch].astype(F32)
        dr_ref[...] += jnp.sum(l_r * xp_r + l_i * xp_i, axis=0, keepdims=True)
        di_ref[...] += jnp.sum(l_i * xp_r - l_r * xp_i, axis=0, keepdims=True)

    blk = BS((rows, 2 * ch), lambda i: (i, 0))
    vec = BS((1, ch), lambda i: (0, 0))
    return pl.pallas_call(
        body, out_shape=(SDS((1, ch), F32), SDS((1, ch), F32)), grid=(n,),
        in_specs=[blk, blk, BS((halo_rows, 2 * ch), halo_at)] + after_specs, out_specs=(vec, vec),
        name=name, compiler_params=_params("arbitrary"))(lam, states, states, *after_ops)


GELU_C = math.sqrt(2.0 / math.pi)
GELU_K = 0.044715


def _ssm_combine(proj, y_fwd, y_bwd, d_skip, tm, after=()):
    t = proj.shape[0]
    after_ops, after_specs = _after_operands(after)

    def body(s_ref, yf_ref, yb_ref, d_ref, *rest):
        yt_ref, g_ref = rest[len(after_ops):]
        y = s_ref[...] * d_ref[...] + yf_ref[...] + yb_ref[...]
        yt_ref[...] = y
        th = jnp.tanh(GELU_C * (y + GELU_K * y * y * y))
        g_ref[...] = (0.5 * y * (1.0 + th)).astype(BF16)

    blk = BS((tm, D_SSM), lambda i: (i, 0))
    return pl.pallas_call(
        body, out_shape=(SDS((t, D_SSM), F32), SDS((t, D_SSM), BF16)), grid=(t // tm,),
        in_specs=[BS((tm, D_SSM), lambda i: (i, D_POOL // D_SSM)), blk, blk, BS((1, D_SSM), lambda i: (0, 0))] + after_specs,
        out_specs=(blk, blk), name="ssm_combine",
        compiler_params=_params("parallel"))(proj, y_fwd, y_bwd, d_skip, *after_ops)


def _ssm_ds(proj, d_yt, du_fwd, du_bwd, d_skip, tm):
    t = proj.shape[0]

    def body(s_ref, dy_ref, duf_ref, dub_ref, d_ref, ds_ref, dd_ref):
        i = pl.program_id(0)
        dy = dy_ref[...]
        ds_ref[...] = (dy * d_ref[...] + duf_ref[...] + dub_ref[...]).astype(BF16)

        @pl.when(i == 0)
        def _():
            dd_ref[...] = jnp.zeros_like(dd_ref)

        dd_ref[...] += jnp.sum(dy * s_ref[...], axis=0, keepdims=True)

    blk = BS((tm, D_SSM), lambda i: (i, 0))
    vec = BS((1, D_SSM), lambda i: (0, 0))
    return pl.pallas_call(
        body, out_shape=(SDS((t, D_SSM), BF16), SDS((1, D_SSM), F32)), grid=(t // tm,),
        in_specs=[BS((tm, D_SSM), lambda i: (i, D_POOL // D_SSM)), blk, blk, blk, vec],
        out_specs=(blk, vec), name="ssm_ds", compiler_params=_params("arbitrary"))(proj, d_yt, du_fwd, du_bwd, d_skip)


GP_BLOCK = (D_POOL + D_SSM) // 256
GS_BLOCK = GP_BLOCK + D_MODEL // 256


def _merge_specs(tm):
    return [BS((tm, D_POOL), lambda s, i: (i, 0)), BS((tm, D_SSM), lambda s, i: (i, 0)),
            BS((None, D_POOL, 256), lambda s, i: (s, 0, 0)), BS((None, D_SSM, 256), lambda s, i: (s, 2, 0)),
            BS((None, D_SSM, 256), lambda s, i: (s, 3, 0)),
            BS((tm, 256), lambda s, i: (i, GP_BLOCK + s)), BS((tm, 256), lambda s, i: (i, GS_BLOCK + s))]


def _mixer_merge(ms, yssm, w_e, proj, tm):
    t = ms.shape[0]

    def body(ms_ref, y_ref, wpp_ref, wgv_ref, wgg_ref, gp_ref, gs_ref, o_ref):
        zp = _dot(ms_ref[...], wpp_ref[...])
        yv = y_ref[...]
        zv = _dot(yv, wgv_ref[...])
        zg = _dot(yv, wgg_ref[...])
        o_ref[...] = (_sigmoid(gp_ref[...]) * zp + _sigmoid(gs_ref[...]) * zv * _sigmoid(zg)).astype(BF16)

    col = BS((tm, 256), lambda s, i: (i, s))
    return pl.pallas_call(
        body, out_shape=SDS((t, D_MODEL), BF16), grid=(N_SHARD, t // tm), in_specs=_merge_specs(tm), out_specs=col,
        name="mixer_merge", compiler_params=_params("parallel", "parallel"))(ms, yssm, w_e, w_e, w_e, proj, proj)


def _mixer_merge_bwd(ms, yssm, w_e, proj, dmerged, tm):
    t = ms.shape[0]

    def body(ms_ref, y_ref, wpp_ref, wgv_ref, wgg_ref, gp_ref, gs_ref, dm_ref,
             dgp_ref, dgs_ref, dzp_ref, dzv_ref, dzg_ref):
        zp = _dot(ms_ref[...], wpp_ref[...])
        yv = y_ref[...]
        zv = _dot(yv, wgv_ref[...])
        zg = _dot(yv, wgg_ref[...])
        dm = dm_ref[...].astype(F32)
        sp, ss, sg = _sigmoid(gp_ref[...]), _sigmoid(gs_ref[...]), _sigmoid(zg)
        dgp_ref[...] = (dm * zp * sp * (1.0 - sp)).astype(BF16)
        dgs_ref[...] = (dm * zv * sg * ss * (1.0 - ss)).astype(BF16)
        dzp_ref[...] = (dm * sp).astype(BF16)
        dz = dm * ss
        dzv_ref[...] = (dz * sg).astype(BF16)
        dzg_ref[...] = (dz * zv * sg * (1.0 - sg)).astype(BF16)

    col = BS((tm, 256), lambda s, i: (i, s))
    shape = SDS((t, D_MODEL), BF16)
    return pl.pallas_call(
        body, out_shape=(shape,) * 5, grid=(N_SHARD, t // tm), in_specs=_merge_specs(tm) + [col],
        out_specs=(col,) * 5, name="mixer_merge_bwd",
        compiler_params=_params("parallel", "parallel"))(ms, yssm, w_e, w_e, w_e, proj, proj, dmerged)


def _mixer_dw(ms, yssm, dzp, dzv, dzg, tm):
    t = ms.shape[0]
    n_t = t // tm

    def body(ms_ref, y_ref, dzp_ref, dzv_ref, dzg_ref, o_ref, acc):
        i = pl.program_id(1)

        @pl.when(i == 0)
        def _():
            acc[...] = jnp.zeros_like(acc)

        yv = y_ref[...]
        acc[0:D_POOL, :] += _dot(ms_ref[...], dzp_ref[...], TN)
        acc[D_POOL:D_POOL + D_SSM, :] += _dot(yv, dzv_ref[...], TN)
        acc[D_POOL + D_SSM:, :] += _dot(yv, dzg_ref[...], TN)

        @pl.when(i == n_t - 1)
        def _():
            o_ref[...] = acc[...].astype(BF16)

    col = BS((tm, 256), lambda s, i: (i, s))
    return pl.pallas_call(
        body, out_shape=SDS((N_SHARD, 1024, 256), BF16), grid=(N_SHARD, n_t),
        in_specs=[BS((tm, D_POOL), lambda s, i: (i, 0)), BS((tm, D_SSM), lambda s, i: (i, 0)), col, col, col],
        out_specs=BS((None, 1024, 256), lambda s, i: (s, 0, 0)), scratch_shapes=[pltpu.VMEM((1024, 256), F32)],
        name="mixer_dw", compiler_params=_params("parallel", "arbitrary"))(ms, yssm, dzp, dzv, dzg)


def _mixer_dx(dzp, dzv, dzg, w_e, y_total, tm):
    t = dzp.shape[0]

    def body(dzp_ref, dzv_ref, dzg_ref, wpp_ref, wgv_ref, wgg_ref, yt_ref, dms_ref, dy_ref, acc_ms, acc_y):
        s = pl.program_id(1)

        @pl.when(s == 0)
        def _():
            acc_ms[...] = jnp.zeros_like(acc_ms)
            acc_y[...] = jnp.zeros_like(acc_y)

        acc_ms[...] += _dot(dzp_ref[...], wpp_ref[...], NT)
        acc_y[...] += _dot(dzv_ref[...], wgv_ref[...], NT) + _dot(dzg_ref[...], wgg_ref[...], NT)

        @pl.when(s == N_SHARD - 1)
        def _():
            dms_ref[...] = acc_ms[...].astype(BF16)
            y = yt_ref[...]
            inner = GELU_C * (y + GELU_K * y * y * y)
            th = jnp.tanh(inner)
            dgelu = 0.5 * (1.0 + th) + 0.5 * y * (1.0 - th * th) * GELU_C * (1.0 + 3.0 * GELU_K * y * y)
            dy_ref[...] = acc_y[...] * dgelu

    col = BS((tm, 256), lambda i, s: (i, s))
    return pl.pallas_call(
        body, out_shape=(SDS((t, D_POOL), BF16), SDS((t, D_SSM), F32)), grid=(t // tm, N_SHARD),
        in_specs=[col, col, col, BS((None, D_POOL, 256), lambda i, s: (s, 0, 0)),
                  BS((None, D_SSM, 256), lambda i, s: (s, 2, 0)), BS((None, D_SSM, 256), lambda i, s: (s, 3, 0)),
                  BS((tm, D_SSM), lambda i, s: (i, 0))],
        out_specs=(BS((tm, D_POOL), lambda i, s: (i, 0)), BS((tm, D_SSM), lambda i, s: (i, 0))),
        scratch_shapes=[pltpu.VMEM((tm, D_POOL), F32), pltpu.VMEM((tm, D_SSM), F32)],
        name="mixer_dx", compiler_params=_params("parallel", "arbitrary"))(dzp, dzv, dzg, w_e, w_e, w_e, y_total)


def _attn_probs(q_h, k_h):
    s = _dot(q_h, k_h, NT) * (1.0 / math.sqrt(HEAD_DIM))
    e = jnp.exp(s - jnp.max(s, axis=-1, keepdims=True))
    return e / jnp.sum(e, axis=-1, keepdims=True)


def _attn_fwd(q, kv, tm):
    t = q.shape[0]
    m = kv.shape[0]

    def body(q_ref, kv_ref, o_ref):
        for hd in range(N_HEADS):
            lo = hd * HEAD_DIM
            p = _attn_probs(q_ref[:, lo:lo + HEAD_DIM], kv_ref[:, lo:lo + HEAD_DIM])
            o_ref[:, lo:lo + HEAD_DIM] = _dot(p, kv_ref[:, D_MODEL + lo:D_MODEL + lo + HEAD_DIM]).astype(BF16)

    return pl.pallas_call(
        body, out_shape=SDS((t, D_MODEL), BF16), grid=(t // tm,),
        in_specs=[BS((tm, D_MODEL), lambda i: (i, 0)), BS((m, 2 * D_MODEL), lambda i: (0, 0))],
        out_specs=BS((tm, D_MODEL), lambda i: (i, 0)), name="attn_fwd", compiler_params=_params("parallel"))(q, kv)


def _attn_bwd(q, kv, d_o, tm):
    t = q.shape[0]
    m = kv.shape[0]

    def body(q_ref, kv_ref, do_ref, dq_ref, dkv_ref):
        i = pl.program_id(0)

        @pl.when(i == 0)
        def _():
            dkv_ref[...] = jnp.zeros_like(dkv_ref)

        for hd in range(N_HEADS):
            lo = hd * HEAD_DIM
            q_h = q_ref[:, lo:lo + HEAD_DIM]
            k_h = kv_ref[:, lo:lo + HEAD_DIM]
            v_h = kv_ref[:, D_MODEL + lo:D_MODEL + lo + HEAD_DIM]
            do_h = do_ref[:, lo:lo + HEAD_DIM]
            p = _attn_probs(q_h, k_h)
            dkv_ref[:, D_MODEL + lo:D_MODEL + lo + HEAD_DIM] += _dot(p, do_h, TN)
            dp = _dot(do_h, v_h, NT)
            ds = p * (dp - jnp.sum(dp * p, axis=-1, keepdims=True)) * (1.0 / math.sqrt(HEAD_DIM))
            dq_ref[:, lo:lo + HEAD_DIM] = _dot(ds, k_h).astype(BF16)
            dkv_ref[:, lo:lo + HEAD_DIM] += _dot(ds, q_h, TN)

    row = BS((tm, D_MODEL), lambda i: (i, 0))
    full = BS((m, 2 * D_MODEL), lambda i: (0, 0))
    return pl.pallas_call(
        body, out_shape=(SDS((t, D_MODEL), BF16), SDS((m, 2 * D_MODEL), F32)), grid=(t // tm,),
        in_specs=[row, full, row], out_specs=(row, full), name="attn_bwd",
        compiler_params=_params("arbitrary"))(q, kv, d_o)


TRANSPOSED = ("ffn1_w_gate", "ffn1_w_up", "ffn2_w_gate", "ffn2_w_up", "w_in")
GATHER_PHASES = {"f1a": (("ffn1_w_gate", "ffn1_w_up"),),
                 "f1b": (("ffn1_w_down",),),
                 "win": (("w_in",),),
                 "mix": (("w_mix_out", "w_q", "w_xo"), ("w_kv",), ("w_pool_proj", "w_glu_val", "w_glu_gate")),
                 "f2": (("ffn2_w_gate", "ffn2_w_up", "ffn2_w_down"),)}
REDUCE_GROUPS = (("ffn2_w_gate", "ffn2_w_up", "ffn2_w_down"), ("w_xo",), ("w_q",), ("w_kv",), ("w_mix_out",),
                 ("w_pool_proj", "w_glu_val", "w_glu_gate"), ("w_in",), ("ffn1_w_gate", "ffn1_w_up", "ffn1_w_down"))
SMALL = ("ffn1_norm", "mix_norm", "pool_w", "pool_scale", "ssm_a_re", "ssm_a_im", "ssm_log_dt", "ssm_b_re",
         "ssm_b_im", "ssm_c_re", "ssm_c_im", "ssm_d", "xattn_norm", "mem_norm", "ffn2_norm", "final_norm")
WEIGHTS = ("ffn1_norm", "ffn1_w_gate", "ffn1_w_up", "ffn1_w_down", "mix_norm", "w_in", "pool_w", "pool_scale",
           "w_pool_proj", "ssm_a_re", "ssm_a_im", "ssm_log_dt", "ssm_b_re", "ssm_b_im", "ssm_c_re", "ssm_c_im",
           "ssm_d", "w_glu_val", "w_glu_gate", "w_mix_out", "xattn_norm", "mem_norm", "w_q", "w_kv", "w_xo",
           "ffn2_norm", "ffn2_w_gate", "ffn2_w_up", "ffn2_w_down", "final_norm")


def _block_diag_in(bb):
    eye = jnp.eye(SSM_GROUPS, dtype=bb.dtype)
    return jnp.einsum("dgph,gk->dghkp", bb, eye).reshape(2, D_SSM, SSM_CH)


def _block_diag_out(cc):
    eye = jnp.eye(SSM_GROUPS, dtype=cc.dtype)
    return jnp.einsum("dghp,gk->dgpkh", cc, eye).reshape(2, SSM_CH, D_SSM)


def _diag_blocks_in(m):
    return jnp.einsum("dghgp->dgph", m.reshape(2, SSM_GROUPS, SSM_GROUP, SSM_GROUPS, SSM_STATE))


def _diag_blocks_out(m):
    return jnp.einsum("dgpgh->dghp", m.reshape(2, SSM_GROUPS, SSM_STATE, SSM_GROUPS, SSM_GROUP))


def _device_step(x, mem, target, wts, sp, reducer=None):
    t = x.shape[0]
    tm = min(TM, t)
    g = {}

    first_gather = wts.start("f1a")
    u1 = _rmsnorm("norm_ffn1", x, sp["ffn1_norm"], tm, after=first_gather)

    ar = sp["ssm_a_re"].reshape(2 * SSM_GROUPS, SSM_STATE)
    ai = sp["ssm_a_im"].reshape(2 * SSM_GROUPS, SSM_STATE)
    ldt = sp["ssm_log_dt"].reshape(2 * SSM_GROUPS, 1)
    abr, abi, qr, qi = _ssm_disc(ar, ai, ldt, after=first_gather)
    b_r = sp["ssm_b_re"].reshape(2 * SSM_CH, SSM_GROUP)
    b_i = sp["ssm_b_im"].reshape(2 * SSM_CH, SSM_GROUP)
    qr_col, qi_col = qr.reshape(2 * SSM_CH, 1), qi.reshape(2 * SSM_CH, 1)
    bbr, bbi = _ssm_bbar(qr_col, qi_col, b_r, b_i)
    shape_b = (2, SSM_GROUPS, SSM_STATE, SSM_GROUP)
    b_mat = jnp.concatenate([_block_diag_in(bbr.reshape(shape_b)), _block_diag_in(bbi.reshape(shape_b))], axis=-1).astype(BF16)
    c_mat = jnp.concatenate([_block_diag_out(sp["ssm_c_re"][0]), -_block_diag_out(sp["ssm_c_im"][0])], axis=1).astype(BF16)
    b_mat_t = jnp.swapaxes(b_mat, 1, 2)
    c_mat_t = jnp.swapaxes(c_mat, 1, 2)
    a_r = abr.reshape(2, 1, SSM_CH)
    a_i = abi.reshape(2, 1, SSM_CH)
    mem_n = _rmsnorm("norm_mem", mem, sp["mem_norm"], mem.shape[0], after=first_gather)

    (w_gu,) = wts.finish("f1a", [u1, b_mat, c_mat, b_mat_t, c_mat_t, mem_n])
    w_f1 = {"gate": (w_gu, FFN_GATE), "up": (w_gu, FFN_UP)}
    down_gather = wts.start("f1b", [w_gu])
    g1, up1, a1 = _ffn_up("ffn1_up", u1, w_f1, tm, after=wts.start("win", down_gather))
    (w_dn,) = wts.finish("f1b", [a1])
    w_f1["down"] = (w_dn, 0)
    h1 = _ffn_down("ffn1_down", a1, w_f1, x, tm)

    u2 = _rmsnorm("norm_mix", h1, sp["mix_norm"], tm)
    (w_in_g,) = wts.finish("win", [u2])
    w_in_t = w_in_g.reshape(D_FF, D_MODEL)
    proj = _mm("mix_in", [(u2, BS((tm, D_MODEL), lambda j, i: (i, 0)), w_in_t, BS((D_FF // 2, D_MODEL), lambda j, i: (j, 0)), NT)],
               grid=(2, t // tm), out_shape=SDS((t, D_FF), F32), out_spec=BS((tm, D_FF // 2), lambda j, i: (i, j)),
               after=wts.start("f2", wts.start("mix", [w_in_g])))
    pooled, mixed, ms = _pool_fwd(proj, sp["pool_w"][0], sp["pool_scale"])

    s_in = proj[:, D_POOL:D_POOL + D_SSM].astype(BF16)
    states, y_dirs = [], []
    for dr in range(2):
        st, yd = _ssm_scan(f"ssm_scan_fwd{dr}", s_in, b_mat[dr], a_r[dr], a_i[dr], c_mat[dr], reverse=(dr == 1))
        states.append(st)
        y_dirs.append(yd)
    w_sq, w_kv, w_e = wts.finish("mix", y_dirs)
    w_mo, w_q, w_xo = (w_sq[:, 256 * k:256 * (k + 1)].reshape(D_MODEL, D_MODEL) for k in range(3))
    w_d = w_kv[:, None]
    y_total, yssm = _ssm_combine(proj, y_dirs[0], y_dirs[1], sp["ssm_d"], tm)

    merged = _mixer_merge(ms, yssm, w_e, proj, tm)
    h2 = _plain_mm("mix_out", merged, w_mo, NN, F32, tm, resid=h1)

    u3 = _rmsnorm("norm_xattn", h2, sp["xattn_norm"], tm)
    q =_plain_mm("attn_q", u3, w_q, NN, BF16, tm)
    n_mem = mem.shape[0]
    kv = _mm("attn_kv", [(mem_n, BS((n_mem, D_MODEL), lambda s: (0, 0)), w_d, BS((None, None, D_MODEL, 512), lambda s: (s, 0, 0, 0)), NN)],
             grid=(N_SHARD,), out_shape=SDS((n_mem, 2 * D_MODEL), BF16), out_spec=BS((n_mem, 512), lambda s: (0, s)))
    o = _attn_fwd(q, kv, tm)
    h3 = _plain_mm("attn_out", o, w_xo, NN, F32, tm, resid=h2)

    u4 = _rmsnorm("norm_ffn2", h3, sp["ffn2_norm"], tm)
    (w_2,) = wts.finish("f2", [u4])
    w_f2 = {"gate": (w_2, FFN_GATE), "up": (w_2, FFN_UP), "down": (w_2, FFN_DOWN)}
    g2, up2, a2 = _ffn_up("ffn2_up", u4, w_f2, tm)
    h4 = _ffn_down("ffn2_down", a2, w_f2, h3, tm)

    loss, dh4, dh4_b, g["final_norm"] = _loss_head(h4, sp["final_norm"].reshape(1, D_MODEL), target, tm)

    dg2, dup2 = _ffn_bwd_act("ffn2_bwd_act", dh4_b, w_f2, g2, up2, tm)
    dw_f2 = _ffn_dw("ffn2_dw", u4, dg2, dup2, a2, dh4_b, tm)
    du4 = _ffn_dx("ffn2_dx", dg2, dup2, w_f2, tm)
    dh3, dh3_b, g["ffn2_norm"] = _rmsnorm_bwd("norm_ffn2_bwd", h3, sp["ffn2_norm"], du4, dh4, tm)

    d_o = _plain_mm("attn_out_dx", dh3_b, w_xo, NT, BF16, tm)
    dw_xo = _dw_mm("attn_out_dw", o, dh3_b, tm)
    dq, dkv = _attn_bwd(q, kv, d_o, tm)
    dw_q = _dw_mm("attn_q_dw", u3, dq, tm)
    du3 = _plain_mm("attn_q_dx", dq, w_q, NT, F32, tm)
    dw_kv = _mm("attn_kv_dw", [(mem_n, BS((n_mem, D_MODEL), lambda s: (0, 0)), dkv, BS((n_mem, 512), lambda s: (0, s)), TN)],
                grid=(N_SHARD,), out_shape=SDS((N_SHARD, D_MODEL, 512), BF16), out_spec=BS((None, D_MODEL, 512), lambda s: (s, 0, 0)))
    dmem_n = _mm("attn_kv_dx", [(dkv, BS((n_mem, 512), lambda s: (0, s)), w_d, BS((None, None, D_MODEL, 512), lambda s: (s, 0, 0, 0)), NT)],
                 grid=(N_SHARD,), red_axis=0, out_shape=SDS((n_mem, D_MODEL), F32), out_spec=BS((n_mem, D_MODEL), lambda s: (0, 0)))
    _, _, g["mem_norm"] = _rmsnorm_bwd("norm_mem_bwd", mem, sp["mem_norm"], dmem_n, None, n_mem)
    dh2, dh2_b, g["xattn_norm"] = _rmsnorm_bwd("norm_xattn_bwd", h2, sp["xattn_norm"], du3, dh3, tm)

    square = (N_SHARD, D_MODEL // N_SHARD, D_MODEL)
    early = [dw_f2.reshape(N_SHARD, 3 * FF_SH, D_MODEL), dw_xo.reshape(square), dw_q.reshape(square), dw_kv]
    swapping = reducer.swap_start("a1", early) if reducer is not None else []
    dmerged = _plain_mm("mix_out_dx", dh2_b, w_mo, NT, BF16, tm, after=swapping)
    dw_mo = _dw_mm("mix_out_dw", merged, dh2_b, tm)
    d_gp, d_gs, dzp, dzv, dzg = _mixer_merge_bwd(ms, yssm, w_e, proj, dmerged, tm)
    dw_e = _mixer_dw(ms, yssm, dzp, dzv, dzg, tm)
    d_ms, d_yt = _mixer_dx(dzp, dzv, dzg, w_e, y_total, tm)
    dp, d_scale, d_pw = _pool_bwd(d_ms, mixed, pooled, sp["pool_w"][0], sp["pool_scale"])
    g["pool_scale"] = d_scale
    g["pool_w"] = d_pw[None]

    d_yt_b = d_yt.astype(BF16)
    du_dirs, lams = [], []
    for dr in range(2):
        lam, du = _ssm_scan(f"ssm_scan_bwd{dr}", d_yt_b, c_mat_t[dr], a_r[dr], -a_i[dr], b_mat_t[dr], reverse=(dr == 0))
        du_dirs.append(du)
        lams.append(lam)
    ds, g["ssm_d"] = _ssm_ds(proj, d_yt, du_dirs[0], du_dirs[1], sp["ssm_d"], tm)

    d_proj = jnp.concatenate([dp, ds, d_gp, d_gs], axis=1)
    dw_in_t = _mm("mix_in_dw", [(d_proj, BS((tm, D_FF // 2), lambda j, i: (i, j)), u2, BS((tm, D_MODEL), lambda j, i: (i, 0)), TN)],
                  grid=(2, t // tm), red_axis=1, out_shape=SDS((D_FF, D_MODEL), BF16), out_spec=BS((D_FF // 2, D_MODEL), lambda j, i: (j, 0)))
    du2 = _plain_mm("mix_in_dx", d_proj, w_in_t, NN, F32, tm)
    dh1, dh1_b, g["mix_norm"] = _rmsnorm_bwd("norm_mix_bwd", h1, sp["mix_norm"], du2, dh2, tm)

    early += [dw_mo.reshape(square), dw_e, dw_in_t.reshape(N_SHARD, FF_SH, D_MODEL)]
    g["final_norm"] = g["final_norm"].reshape(D_MODEL)

    travelling = reducer.start("a", early[4:], swapped=["a1"], after=list(g.values())) if reducer is not None else []
    d_abr, d_abi, d_cm, d_bm = [], [], [], []
    for dr in range(2):
        da_r, da_i = _ssm_da(f"ssm_da{dr}", lams[dr], states[dr], reverse=(dr == 1), after=travelling)
        d_abr.append(da_r)
        d_abi.append(da_i)
        d_cm.append(_dw_mm(f"ssm_dc{dr}", states[dr], d_yt_b, tm, F32, after=travelling))
        d_bm.append(_dw_mm(f"ssm_db{dr}", s_in, lams[dr], tm, F32, after=travelling))
    d_cm = jnp.stack(d_cm)
    d_bm = jnp.stack(d_bm)
    g["ssm_c_re"] = _diag_blocks_out(d_cm[:, :SSM_CH])[None]
    g["ssm_c_im"] = -_diag_blocks_out(d_cm[:, SSM_CH:])[None]
    g_bbr = _diag_blocks_in(d_bm[:, :, :SSM_CH]).reshape(2 * SSM_CH, SSM_GROUP)
    g_bbi = _diag_blocks_in(d_bm[:, :, SSM_CH:]).reshape(2 * SSM_CH, SSM_GROUP)
    d_qr, d_qi, d_br, d_bi = _ssm_bbar_bwd(qr_col, qi_col, b_r, b_i, g_bbr, g_bbi)
    g["ssm_b_re"] = d_br.reshape(sp["ssm_b_re"].shape)
    g["ssm_b_im"] = d_bi.reshape(sp["ssm_b_im"].shape)
    d_ar, d_ai, d_ldt = _ssm_disc_bwd(ar, ai, ldt, jnp.stack(d_abr).reshape(ar.shape), jnp.stack(d_abi).reshape(ar.shape),
                                      d_qr.reshape(ar.shape), d_qi.reshape(ar.shape))
    g["ssm_a_re"] = d_ar.reshape(sp["ssm_a_re"].shape)
    g["ssm_a_im"] = d_ai.reshape(sp["ssm_a_im"].shape)
    g["ssm_log_dt"] = d_ldt.reshape(sp["ssm_log_dt"].shape)
    if reducer is not None:
        travelling = travelling + [d_ar, d_ai, d_ldt, d_br, d_bi, d_cm]
    dg1, dup1 = _ffn_bwd_act("ffn1_bwd_act", dh1_b, w_f1, g1, up1, tm, after=travelling)
    dw_f1 = _ffn_dw("ffn1_dw", u1, dg1, dup1, a1, dh1_b, tm).reshape(N_SHARD, 3 * FF_SH, D_MODEL)
    if reducer is not None:
        travelling = reducer.start("b", [dw_f1], after=reducer.finish("a", [dw_f1]))
        travelling = travelling + reducer.join_start("a", after=travelling)
    du1 = _ffn_dx("ffn1_dx", dg1, dup1, w_f1, tm, after=travelling)
    grad_x, _, g["ffn1_norm"] = _rmsnorm_bwd("norm_ffn1_bwd", x, sp["ffn1_norm"], du1, dh1, tm)
    if reducer is not None:
        reducer.finish("b", [grad_x])
        reducer.join_finish("a", [grad_x])
    return loss, grad_x, early + [dw_f1], g


def _mesh_place():
    x, y, c = lax.axis_index("x"), lax.axis_index("y"), lax.axis_index("c")
    chips = [(1 - x, y), (x, 1 - y), (1 - x, 1 - y)]
    return x, y, c, chips


def _remote(src, dst, send_sems, recv_sems, k, to):
    return pltpu.make_async_remote_copy(src_ref=src, dst_ref=dst, send_sem=send_sems.at[k], recv_sem=recv_sems.at[k],
                                        device_id=to, device_id_type=MESH)


def _sibling_swap_halves(tag, grads, after=()):
    n = len(grads)
    after_ops, after_specs = _after_operands(after)

    def body(*refs):
        ins, outs = refs[:n], refs[n + len(after_ops):2 * n + len(after_ops)]
        send_sems, recv_sems = refs[2 * n + len(after_ops):]
        x, y, c, _ = _mesh_place()
        sibling = (x, y, 1 - c)
        copies = []
        for k in range(n):
            half = grads[k].shape[1] // 2
            theirs = pl.ds(pl.multiple_of((1 - c) * half, 16), half)
            cp = _remote(ins[k].at[:, theirs, :], outs[k], send_sems, recv_sems, k, sibling)
            cp.start()
            copies.append(cp)
        for cp in copies:
            cp.wait_recv()
        for cp in copies:
            cp.wait_send()

    hbm = BS(memory_space=pl.ANY)
    return pl.pallas_call(
        body, out_shape=tuple(SDS((g.shape[0], g.shape[1] // 2, g.shape[2]), g.dtype) for g in grads),
        in_specs=[hbm] * n + after_specs, out_specs=(hbm,) * n,
        scratch_shapes=[pltpu.SemaphoreType.DMA((n,)), pltpu.SemaphoreType.DMA((n,))],
        name="reduce_sibling_send_" + tag, compiler_params=_params())(*grads, *after_ops)


def _row_tile(rows, cap=512):
    return max(r for r in range(16, cap + 1, 16) if rows % r == 0)


def _chip_presum(k, grad, got, c_idx):
    n_sh, rows, cols = grad.shape
    half = rows // 2
    tr = _row_tile(half)
    grad4 = grad.reshape(n_sh, 2, half, cols)

    def body(c_ref, a_ref, b_ref, o_ref):
        o_ref[...] = (a_ref[...].astype(F32) + b_ref[...].astype(F32)).astype(o_ref.dtype)

    return pl.pallas_call(
        body, out_shape=SDS((n_sh, half, cols), BF16),
        grid_spec=pltpu.PrefetchScalarGridSpec(
            num_scalar_prefetch=1, grid=(n_sh, half // tr),
            in_specs=[BS((None, None, tr, cols), lambda s, i, c_ref: (s, c_ref[0], i, 0)),
                      BS((None, tr, cols), lambda s, i, c_ref: (s, i, 0))],
            out_specs=BS((None, tr, cols), lambda s, i, c_ref: (s, i, 0))),
        name=f"reduce_presum{k}", compiler_params=_params("parallel", "parallel"))(c_idx, grad4, got)


HBM_SPEC = BS(memory_space=pltpu.HBM)
SEM_SPEC = BS(memory_space=pltpu.SEMAPHORE)
DATAFLOW = pltpu.SideEffectType.DATAFLOW_SIDE_EFFECTING


def _chip_exchange_copies(parts, lands, send_sems, recv_sems):
    _, _, c, chips = _mesh_place()
    return [_remote(parts[k].at[2 * px + py], lands[k].at[j], send_sems, recv_sems, 3 * k + j, (px, py, c))
            for k in range(len(parts)) for j, (px, py) in enumerate(chips)]


def _gather_copies(shards, lands, send_sems, recv_sems):
    x, y, c, chips = _mesh_place()
    return [_remote(shards[k], lands[k].at[2 * x + y], send_sems, recv_sems, 3 * k + j, (px, py, c))
            for k in range(len(shards)) for j, (px, py) in enumerate(chips)]


def _gather_half_copies(shards, lands, send_sems, recv_sems):
    x, y, c, chips = _mesh_place()
    out = []
    for k in range(len(shards)):
        half = shards[k].shape[0] // 2
        mine = pl.ds(pl.multiple_of(c * half, 16), half)
        for j, (px, py) in enumerate(chips):
            out.append(_remote(shards[k].at[mine, :], lands[k].at[2 * x + y, mine, :], send_sems, recv_sems,
                               3 * k + j, (px, py, c)))
    return out


def _sibling_fill(tag, lands):
    n = len(lands)

    def body(*refs):
        outs = refs[n:2 * n]
        send_sems, recv_sems = refs[2 * n:]
        x, y, c, chips = _mesh_place()
        copies = []
        for k in range(n):
            half = lands[k].shape[1] // 2
            mine = pl.ds(pl.multiple_of(c * half, 16), half)
            for j, (px, py) in enumerate(chips):
                blk = outs[k].at[2 * px + py, mine, :]
                copies.append(_remote(blk, blk, send_sems, recv_sems, 3 * k + j, (x, y, 1 - c)))
        for cp in copies:
            cp.start()
        for cp in copies:
            cp.wait_recv()
        for cp in copies:
            cp.wait_send()

    hbm = BS(memory_space=pl.ANY)
    return list(pl.pallas_call(
        body, out_shape=tuple(SDS(a.shape, a.dtype) for a in lands),
        in_specs=[hbm] * n, out_specs=(hbm,) * n, input_output_aliases={k: k for k in range(n)},
        scratch_shapes=[pltpu.SemaphoreType.DMA((3 * n,)), pltpu.SemaphoreType.DMA((3 * n,))],
        name="gather_fill_" + tag, compiler_params=_params())(*lands))


def _swap_copies(grads, lands, send_sems, recv_sems):
    x, y, c, _ = _mesh_place()
    out = []
    for k in range(len(grads)):
        half = grads[k].shape[1] // 2
        theirs = pl.ds(pl.multiple_of((1 - c) * half, 16), half)
        out.append(_remote(grads[k].at[:, theirs, :], lands[k], send_sems, recv_sems, k, (x, y, 1 - c)))
    return out


def _join_copies(fulls, same, send_sems, recv_sems):
    x, y, c, _ = _mesh_place()
    out = []
    for k in range(len(fulls)):
        half = fulls[k].shape[0] // 2
        mine = fulls[k].at[pl.ds(pl.multiple_of(c * half, 8), half), :]
        out.append(_remote(mine, mine, send_sems, recv_sems, k, (x, y, 1 - c)))
    return out


def _everyone_copies(packs, lands, send_sems, recv_sems):
    x, y, c, _ = _mesh_place()
    out = []
    for k in range(len(packs)):
        for j in range(N_DEV - 1):
            bx, by, bc = (j + 1) >> 2 & 1, (j + 1) >> 1 & 1, (j + 1) & 1
            peer = (x ^ bx, y ^ by, c ^ bc)
            out.append(_remote(packs[k], lands[k].at[4 * x + 2 * y + c], send_sems, recv_sems, (N_DEV - 1) * k + j, peer))
    return out


def _split_start(name, copies, sources, land_shapes, after=(), fanout=3):
    n = len(sources)
    n_land = len(land_shapes)
    m = n + n_land
    n_sems = fanout * n
    after_ops, after_specs = _after_operands(after)

    def body(*refs):
        ins = refs[:n]
        lands = refs[n:m] if n_land else ins
        send_sems, recv_sems = refs[m + len(after_ops)], refs[m + len(after_ops) + 1]
        token = refs[-1]
        for cp in copies(ins, lands, send_sems, recv_sems):
            cp.start()
        token[...] = jnp.zeros_like(token)

    lands = [pltpu.with_memory_space_constraint(lax.empty(s, d), pltpu.HBM) for s, d in land_shapes]
    sources = [pltpu.with_memory_space_constraint(p, pltpu.HBM) for p in sources]
    thru = [pltpu.HBM(a.shape, a.dtype) for a in sources + lands]
    out = pl.pallas_call(
        body, name=name,
        out_shape=(pltpu.SemaphoreType.DMA((n_sems,)), pltpu.SemaphoreType.DMA((n_sems,)), *thru, SDS((8, 128), F32)),
        in_specs=[HBM_SPEC] * m + after_specs,
        out_specs=(SEM_SPEC, SEM_SPEC, *[HBM_SPEC] * m, BS(memory_space=pltpu.VMEM)),
        input_output_aliases={i: 2 + i for i in range(m)},
        compiler_params=pltpu.CompilerParams(has_side_effects=DATAFLOW))(*sources, *lands, *after_ops)
    return out[0], out[1], list(out[2:2 + n]), list(out[2 + n:2 + m]), out[-1]


def _split_wait(name, copies, send_sems, recv_sems, sources, lands, after):
    n = len(sources)
    m = n + len(lands)
    after_ops, after_specs = _after_operands(after)

    def body(*refs):
        ins = refs[:n]
        zones = refs[n:m] if m > n else ins
        for cp in copies(ins, zones, refs[m], refs[m + 1]):
            cp.wait_send()
            cp.wait_recv()

    out = pl.pallas_call(
        body, name=name,
        out_shape=tuple(pltpu.HBM(a.shape, a.dtype) for a in sources + lands),
        in_specs=[HBM_SPEC] * m + [SEM_SPEC, SEM_SPEC] + after_specs, out_specs=(HBM_SPEC,) * m,
        input_output_aliases={i: i for i in range(m)},
        compiler_params=pltpu.CompilerParams(has_side_effects=DATAFLOW))(*sources, *lands, send_sems, recv_sems, *after_ops)
    return list(out[:n]), list(out[n:])


class _WeightGatherer:
    def __init__(self, shards):
        self.shards, self.open = shards, {}
        self.me = 2 * lax.axis_index("x") + lax.axis_index("y")

    HALVED = ("f1a",)

    def start(self, tag, after=()):
        shapes = [((N_SHARD,) + s.shape, s.dtype) for s in self.shards[tag]]
        copies = _gather_half_copies if tag in self.HALVED else _gather_copies
        self.open[tag] = _split_start("gather_start_" + tag, copies, self.shards[tag], shapes, after)
        return [self.open[tag][-1]]

    def finish(self, tag, after):
        send_sems, recv_sems, shards, lands, _ = self.open.pop(tag)
        copies = _gather_half_copies if tag in self.HALVED else _gather_copies
        shards, lands = _split_wait("gather_wait_" + tag, copies, send_sems, recv_sems, shards, lands, after)
        if tag in self.HALVED:
            lands = _sibling_fill(tag, lands)
        return [lax.dynamic_update_slice(zone, s[None], (self.me, 0, 0)) for zone, s in zip(lands, shards)]


class _GradReducer:
    def __init__(self):
        self.c_idx = lax.axis_index("c").astype(jnp.int32).reshape(1)
        self.place = jnp.stack([2 * lax.axis_index("x") + lax.axis_index("y"), lax.axis_index("c")]).astype(jnp.int32)
        self.swaps, self.open, self.landed, self.joins, self.reduced = {}, {}, {}, {}, []

    def swap_start(self, tag, grads, after=()):
        shapes = [((g.shape[0], g.shape[1] // 2, g.shape[2]), g.dtype) for g in grads]
        self.swaps[tag] = _split_start("reduce_swap_start_" + tag, _swap_copies, grads, shapes, after, fanout=1)
        return [self.swaps[tag][-1]]

    def start(self, tag, grads, after=(), swapped=()):
        pairs = []
        for s in swapped:
            send_sems, recv_sems, early, lands, _ = self.swaps.pop(s)
            pairs += zip(*_split_wait("reduce_swap_wait_" + s, _swap_copies, send_sems, recv_sems, early, lands, grads[-1:]))
        pairs += zip(grads, _sibling_swap_halves(tag, grads, after))
        parts = [_chip_presum(f"{tag}{k}", g, s, self.c_idx) for k, (g, s) in enumerate(pairs)]
        shapes = [((3,) + p.shape[1:], p.dtype) for p in parts]
        self.open[tag] = _split_start("reduce_exchange_start_" + tag, _chip_exchange_copies, parts, shapes)
        return [self.open[tag][-1]]

    def finish(self, tag, after):
        send_sems, recv_sems, parts, lands, _ = self.open.pop(tag)
        self.landed[tag] = _split_wait("reduce_exchange_wait_" + tag, _chip_exchange_copies, send_sems, recv_sems, parts, lands, after)
        return self.landed[tag][1][:1]

    def _sums(self, tag, after=()):
        parts, landed = self.landed.pop(tag)
        return [_chip_sum(f"{tag}{k}", p, got, self.place, after) for k, (p, got) in enumerate(zip(parts, landed))]

    def join_start(self, tag, after=()):
        self.joins[tag] = _split_start("reduce_join_start_" + tag, _join_copies, self._sums(tag, after), [], fanout=1)
        return [self.joins[tag][-1]]

    def join_finish(self, tag, after):
        send_sems, recv_sems, fulls, _, _ = self.joins.pop(tag)
        self.reduced += _split_wait("reduce_join_wait_" + tag, _join_copies, send_sems, recv_sems, fulls, [], after)[0]

    def join(self, tag, after=()):
        self.reduced += _sibling_join_halves(self._sums(tag), after)


def _chip_sum(k, part, got, place, after=()):
    _, half, cols = part.shape
    tr = _row_tile(half)
    n_t = half // tr
    after_ops, after_specs = _after_operands(after)

    def body(place_ref, a_ref, b_ref, *rest):
        o_ref = rest[-1]
        acc = a_ref[...].astype(F32)
        for j in range(3):
            acc = acc + b_ref[j].astype(F32)
        o_ref[...] = acc

    return pl.pallas_call(
        body, out_shape=SDS((2 * half, cols), F32),
        grid_spec=pltpu.PrefetchScalarGridSpec(
            num_scalar_prefetch=1, grid=(n_t,),
            in_specs=[BS((None, tr, cols), lambda i, place_ref: (place_ref[0], i, 0)),
                      BS((3, tr, cols), lambda i, place_ref: (0, i, 0))] + after_specs,
            out_specs=BS((tr, cols), lambda i, place_ref: (place_ref[1] * n_t + i, 0))),
        name=f"reduce_sum{k}", compiler_params=_params("parallel"))(place, part, got, *after_ops)


def _sibling_join_halves(fulls, after=()):
    n = len(fulls)
    after_ops, after_specs = _after_operands(after)

    def body(*refs):
        outs = refs[n + len(after_ops):2 * n + len(after_ops)]
        send_sems, recv_sems = refs[2 * n + len(after_ops):]
        copies = _join_copies(outs, outs, send_sems, recv_sems)
        for cp in copies:
            cp.start()
        for cp in copies:
            cp.wait_recv()
        for cp in copies:
            cp.wait_send()

    hbm = BS(memory_space=pl.ANY)
    return list(pl.pallas_call(
        body, out_shape=tuple(SDS(f.shape, f.dtype) for f in fulls),
        in_specs=[hbm] * n + after_specs, out_specs=(hbm,) * n, input_output_aliases={k: k for k in range(n)},
        scratch_shapes=[pltpu.SemaphoreType.DMA((n,)), pltpu.SemaphoreType.DMA((n,))],
        name="reduce_sibling_join", compiler_params=_params())(*fulls, *after_ops))


N_DEV = 8


def _sum_devices(packs):
    _, rows, lanes = packs.shape

    def body(p_ref, o_ref):
        acc = p_ref[0]
        for dev in range(1, N_DEV):
            acc = acc + p_ref[dev]
        o_ref[...] = acc

    vm = BS(memory_space=pltpu.VMEM)
    return pl.pallas_call(body, out_shape=SDS((rows, lanes), F32), in_specs=[vm], out_specs=vm,
                          name="small_sum", compiler_params=_params())(packs)


def _adamw(name, w, grad, row0, m, v, after=()):
    rows, cols = w.shape
    tr = rows if rows < 16 else _row_tile(rows, 256)
    bc1 = 1.0 - ADAM_B1 ** ADAM_STEP
    bc2 = 1.0 - ADAM_B2 ** ADAM_STEP
    after_ops, after_specs = _after_operands(after)

    def body(w_ref, g_ref, m_ref, v_ref, *rest):
        go_ref, d_ref, mo_ref, vo_ref = rest[len(after_ops):]
        g = g_ref[...]
        m_new = ADAM_B1 * m_ref[...] + (1.0 - ADAM_B1) * g
        v_new = ADAM_B2 * v_ref[...] + (1.0 - ADAM_B2) * (g * g)
        go_ref[...] = g
        mo_ref[...] = m_new
        vo_ref[...] = v_new
        d_ref[...] = -ADAM_LR * ((m_new / bc1) / (jnp.sqrt(v_new / bc2) + ADAM_EPS) + ADAM_WD * w_ref[...])

    blk = BS((tr, cols), lambda i: (i, 0))
    shape = SDS((rows, cols), F32)
    return pl.pallas_call(
        body, out_shape=(shape,) * 4, grid=(rows // tr,),
        in_specs=[blk, BS((tr, cols), lambda i: (row0 // tr + i, 0)), blk, blk] + after_specs, out_specs=(blk,) * 4,
        name=name, compiler_params=_params("parallel"))(w, grad, m, v, *after_ops)


SMALL_LANES = 128


def _pack_small(parts):
    flat = jnp.concatenate([jnp.ravel(p) for p in parts])
    rows = -(-flat.shape[0] // (64 * SMALL_LANES)) * 64
    return jnp.pad(flat, (0, rows * SMALL_LANES - flat.shape[0])).reshape(rows, SMALL_LANES)


def _unpack_small(packed, like):
    flat = jnp.ravel(packed)
    out, at = [], 0
    for p in like:
        out.append(flat[at:at + p.size].reshape(p.shape))
        at += p.size
    return out


def kernel(x, mem, ffn1_norm, ffn1_w_gate, ffn1_w_up, ffn1_w_down, mix_norm, w_in, pool_w, pool_scale, w_pool_proj, ssm_a_re, ssm_a_im, ssm_log_dt, ssm_b_re, ssm_b_im, ssm_c_re, ssm_c_im, ssm_d, w_glu_val, w_glu_gate, w_mix_out, xattn_norm, mem_norm, w_q, w_kv, w_xo, ffn2_norm, ffn2_w_gate, ffn2_w_up, ffn2_w_down, final_norm, loss_target, m_ffn1_norm, m_ffn1_w_gate, m_ffn1_w_up, m_ffn1_w_down, m_mix_norm, m_w_in, m_pool_w, m_pool_scale, m_w_pool_proj, m_ssm_a_re, m_ssm_a_im, m_ssm_log_dt, m_ssm_b_re, m_ssm_b_im, m_ssm_c_re, m_ssm_c_im, m_ssm_d, m_w_glu_val, m_w_glu_gate, m_w_mix_out, m_xattn_norm, m_mem_norm, m_w_q, m_w_kv, m_w_xo, m_ffn2_norm, m_ffn2_w_gate, m_ffn2_w_up, m_ffn2_w_down, m_final_norm, v_ffn1_norm, v_ffn1_w_gate, v_ffn1_w_up, v_ffn1_w_down, v_mix_norm, v_w_in, v_pool_w, v_pool_scale, v_w_pool_proj, v_ssm_a_re, v_ssm_a_im, v_ssm_log_dt, v_ssm_b_re, v_ssm_b_im, v_ssm_c_re, v_ssm_c_im, v_ssm_d, v_w_glu_val, v_w_glu_gate, v_w_mix_out, v_xattn_norm, v_mem_norm, v_w_q, v_w_kv, v_w_xo, v_ffn2_norm, v_ffn2_w_gate, v_ffn2_w_up, v_ffn2_w_down, v_final_norm):
    given = dict(locals())
    w = {n: given[n] for n in WEIGHTS}
    m = {n: given["m_" + n] for n in WEIGHTS}
    v = {n: given["v_" + n] for n in WEIGHTS}

    def shard_view(a, n):
        return a[0].T if n in TRANSPOSED else a[0]

    def shard_unview(a, n):
        return (a.T if n in TRANSPOSED else a)[None]

    shards = {tag: [jnp.concatenate([shard_view(w[n], n).astype(BF16) for n in grp], axis=0) for grp in arrays]
              for tag, arrays in GATHER_PHASES.items()}
    reducer = _GradReducer()
    loss_part, grad_x, _, small = _device_step(x[0], mem[0], loss_target[0], _WeightGatherer(shards),
                                               {n: w[n] for n in SMALL}, reducer)
    loss = lax.psum(loss_part[0, 0], ("x", "y", "c"))

    pack = _pack_small([small[n] for n in SMALL])
    everyone = _split_start("small_start", _everyone_copies, [pack], [((N_DEV,) + pack.shape, F32)], fanout=N_DEV - 1)
    reducer.join("b", after=everyone[-1:])

    grads, delta, new_m, new_v = {}, {}, {}, {}
    big_done = []
    for grp, red in zip(REDUCE_GROUPS, reducer.reduced):
        row0 = 0
        for n in grp:
            w_n = shard_view(w[n], n)
            outs = _adamw("adamw_" + n, w_n, red, row0, shard_view(m[n], n), shard_view(v[n], n), after=everyone[-1:])
            grads[n], delta[n], new_m[n], new_v[n] = (shard_unview(o, n) for o in outs)
            big_done.append(outs[1])
            row0 += w_n.shape[0]

    send_sems, recv_sems, packs, landed, _ = everyone
    packs, landed = _split_wait("small_wait", _everyone_copies, send_sems, recv_sems, packs, landed, big_done)
    mine = 4 * lax.axis_index("x") + 2 * lax.axis_index("y") + lax.axis_index("c")
    summed = _sum_devices(lax.dynamic_update_slice(landed[0], packs[0][None], (mine, 0, 0)))
    g_small = dict(zip(SMALL, _unpack_small(summed, [w[n] for n in SMALL])))
    narrow = [n for n in SMALL if w[n].ndim > 3]
    dense = [n for n in SMALL if n not in narrow]
    for n in narrow:
        two_d = (-1, w[n].shape[-1])
        outs = _adamw("adamw_" + n, w[n].reshape(two_d), g_small[n].reshape(two_d), 0, m[n].reshape(two_d), v[n].reshape(two_d))
        grads[n], delta[n], new_m[n], new_v[n] = (o.reshape(w[n].shape) for o in outs)
    dense_like = [w[n] for n in dense]
    packed = _adamw("adamw_small", _pack_small(dense_like), _pack_small([g_small[n] for n in dense]), 0,
                    _pack_small([m[n] for n in dense]), _pack_small([v[n] for n in dense]))
    for out, store in zip(packed, (grads, delta, new_m, new_v)):
        for n, val in zip(dense, _unpack_small(out, dense_like)):
            store[n] = val

    return (loss, grad_x[None], *[grads[n] for n in WEIGHTS], *[delta[n] for n in WEIGHTS],
            *[new_m[n] for n in WEIGHTS], *[new_v[n] for n in WEIGHTS])
```

```python
import functools
import math

import jax
import jax.numpy as jnp
from jax import lax
from jax.experimental import pallas as pl
from jax.experimental.pallas import tpu as pltpu

F32 = jnp.float32
BF16 = jnp.bfloat16
SDS = jax.ShapeDtypeStruct
BS = pl.BlockSpec
MESH = pl.DeviceIdType.MESH

D_MODEL = 1024
D_FF = 2816
N_SHARD = 4
FF_SH = D_FF // N_SHARD
D_POOL = 512
POOL_WINDOWS = (2, 4, 8, 16)
POOL_GROUP = 128
D_SSM = 256
SSM_GROUPS = 16
SSM_GROUP = 16
SSM_STATE = 64
SSM_CH = SSM_GROUPS * SSM_STATE
N_HEADS = 4
HEAD_DIM = 256
EPS = 1e-6
ADAM_LR, ADAM_B1, ADAM_B2, ADAM_EPS, ADAM_WD, ADAM_STEP = 0.001, 0.9, 0.999, 1e-08, 0.01, 10

VMEM_LIMIT_V7X = 52 * 1024 * 1024
TM = 512

NN = (((1,), (0,)), ((), ()))
NT = (((1,), (1,)), ((), ()))
TN = (((0,), (0,)), ((), ()))


def _params(*sem):
    return pltpu.CompilerParams(dimension_semantics=sem if sem else None, vmem_limit_bytes=VMEM_LIMIT_V7X)


def _dot(a, b, dims=NN):
    return lax.dot_general(a.astype(BF16), b.astype(BF16), dims, preferred_element_type=F32)


def _sigmoid(v):
    return pl.reciprocal(1.0 + jnp.exp(-v), approx=True)


def _block_dims(spec):
    return tuple(d for d in spec.block_shape if d is not None)


def _after_operands(after):
    return list(after), [BS(memory_space=pl.ANY)] * len(after)


def _mm(name, pairs, *, grid, out_shape, out_spec, red_axis=None, extras=(), epilogue=None, after=()):
    n_pairs, n_extra = len(pairs), len(extras)
    n_red = grid[red_axis] if red_axis is not None else 1
    dims = [p[4] for p in pairs]

    def body(*refs):
        ab = refs[:2 * n_pairs]
        ex = refs[2 * n_pairs:2 * n_pairs + n_extra]
        o_ref = refs[2 * n_pairs + n_extra + len(after)]

        def partial():
            acc = None
            for p in range(n_pairs):
                t = _dot(ab[2 * p][...], ab[2 * p + 1][...], dims[p])
                acc = t if acc is None else acc + t
            return acc

        def finish(acc):
            res = epilogue(acc, *[e[...] for e in ex]) if epilogue is not None else acc
            o_ref[...] = res.astype(o_ref.dtype)

        if n_red == 1:
            finish(partial())
        else:
            acc_ref = refs[-1]
            k = pl.program_id(red_axis)

            @pl.when(k == 0)
            def _():
                acc_ref[...] = jnp.zeros_like(acc_ref)

            acc_ref[...] += partial()

            @pl.when(k == n_red - 1)
            def _():
                finish(acc_ref[...])

    operands, in_specs = [], []
    for a, a_spec, b, b_spec, _ in pairs:
        operands += [a, b]
        in_specs += [a_spec, b_spec]
    for e, e_spec in extras:
        operands.append(e)
        in_specs.append(e_spec)
    after_ops, after_specs = _after_operands(after)
    operands += after_ops
    in_specs += after_specs
    scratch = [pltpu.VMEM(_block_dims(out_spec), F32)] if n_red > 1 else []
    sem = tuple("arbitrary" if ax == red_axis else "parallel" for ax in range(len(grid)))
    return pl.pallas_call(body, out_shape=out_shape, grid=grid, in_specs=in_specs, out_specs=out_spec,
                          scratch_shapes=scratch, name=name, compiler_params=_params(*sem))(*operands)


def _rmsnorm(name, h, gain, tm, after=()):
    t, d = h.shape
    after_ops, after_specs = _after_operands(after)

    def body(h_ref, g_ref, *rest):
        u_ref = rest[-1]
        hv = h_ref[...]
        r = lax.rsqrt(jnp.mean(hv * hv, axis=-1, keepdims=True) + EPS)
        u_ref[...] = ((hv * r) * g_ref[...]).astype(u_ref.dtype)

    return pl.pallas_call(
        body, out_shape=SDS((t, d), BF16), grid=(t // tm,),
        in_specs=[BS((tm, d), lambda i: (i, 0)), BS((1, d), lambda i: (0, 0))] + after_specs,
        out_specs=BS((tm, d), lambda i: (i, 0)), name=name, compiler_params=_params("parallel"))(h, gain, *after_ops)


def _rmsnorm_bwd(name, h, gain, du, dh_in, tm):
    t, d = h.shape
    has_in = dh_in is not None

    def body(*refs):
        if has_in:
            h_ref, g_ref, du_ref, dhin_ref, dh_ref, dhb_ref, dg_ref = refs
        else:
            h_ref, g_ref, du_ref, dh_ref, dhb_ref, dg_ref = refs
        i = pl.program_id(0)
        hv = h_ref[...]
        r = lax.rsqrt(jnp.mean(hv * hv, axis=-1, keepdims=True) + EPS)
        n = hv * r
        duv = du_ref[...].astype(F32)
        dn = duv * g_ref[...]
        dh = r * (dn - n * jnp.mean(dn * n, axis=-1, keepdims=True))
        if has_in:
            dh = dhin_ref[...] + dh
        dh_ref[...] = dh
        dhb_ref[...] = dh.astype(BF16)

        @pl.when(i == 0)
        def _():
            dg_ref[...] = jnp.zeros_like(dg_ref)

        dg_ref[...] += jnp.sum(duv * n, axis=0, keepdims=True)

    row = BS((tm, d), lambda i: (i, 0))
    vec = BS((1, d), lambda i: (0, 0))
    operands = [h, gain, du] + ([dh_in] if has_in else [])
    in_specs = [row, vec, row] + ([row] if has_in else [])
    return pl.pallas_call(
        body, out_shape=(SDS((t, d), F32), SDS((t, d), BF16), SDS((1, d), F32)), grid=(t // tm,),
        in_specs=in_specs, out_specs=(row, row, vec), name=name, compiler_params=_params("arbitrary"))(*operands)


def _loss_head(h, gain, target, tm):
    t, d = h.shape

    def body(h_ref, g_ref, t_ref, loss_ref, dh_ref, dhb_ref, dg_ref):
        i = pl.program_id(0)
        hv = h_ref[...]
        g = g_ref[...]
        r = lax.rsqrt(jnp.mean(hv * hv, axis=-1, keepdims=True) + EPS)
        n = hv * r
        err = n * g - t_ref[...]
        dy = err * (1.0 / d)
        dn = dy * g
        dh = r * (dn - n * jnp.mean(dn * n, axis=-1, keepdims=True))
        dh_ref[...] = dh
        dhb_ref[...] = dh.astype(BF16)

        @pl.when(i == 0)
        def _():
            dg_ref[...] = jnp.zeros_like(dg_ref)
            loss_ref[...] = jnp.zeros_like(loss_ref)

        dg_ref[...] += jnp.sum(dy * n, axis=0, keepdims=True)
        part = 0.5 * jnp.sum(jnp.mean(err * err, axis=-1, keepdims=True), axis=0, keepdims=True)
        loss_ref[...] += jnp.broadcast_to(part, loss_ref.shape)

    row = BS((tm, d), lambda i: (i, 0))
    vec = BS((1, d), lambda i: (0, 0))
    return pl.pallas_call(
        body, out_shape=(SDS((1, 128), F32), SDS((t, d), F32), SDS((t, d), BF16), SDS((1, d), F32)),
        grid=(t // tm,), in_specs=[row, vec, row],
        out_specs=(BS((1, 128), lambda i: (0, 0)), row, row, vec),
        name="loss_head", compiler_params=_params("arbitrary"))(h, gain, target)


FFN_GATE, FFN_UP, FFN_DOWN = 0, 1, 2


def _ffn_w_spec(block, index):
    return BS((None, FF_SH, D_MODEL), lambda *g: (g[index], block, 0))


def _ffn_up(name, u, w_f, tm, after=()):
    t, d = u.shape
    after_ops, after_specs = _after_operands(after)

    def body(u_ref, wg_ref, wu_ref, *rest):
        pg_ref, pu_ref, a_ref = rest[len(after_ops):]
        uv = u_ref[...]
        g = _dot(uv, wg_ref[...], NT)
        up = _dot(uv, wu_ref[...], NT)
        sg = _sigmoid(g)
        silu = g * sg
        a_ref[...] = (silu * up).astype(BF16)
        pu_ref[...] = (0.5 * silu).astype(BF16)
        pg_ref[...] = (0.5 * sg * (1.0 + g * (1.0 - sg)) * up).astype(BF16)

    hid = BS((None, tm, FF_SH), lambda s, i: (s, i, 0))
    shape = SDS((N_SHARD, t, FF_SH), BF16)
    return pl.pallas_call(
        body, out_shape=(shape, shape, shape), grid=(N_SHARD, t // tm),
        in_specs=[BS((tm, d), lambda s, i: (i, 0)), _ffn_w_spec(w_f["gate"][1], 0), _ffn_w_spec(w_f["up"][1], 0)] + after_specs,
        out_specs=(hid, hid, hid), name=name,
        compiler_params=_params("parallel", "parallel"))(u, w_f["gate"][0], w_f["up"][0], *after_ops)


def _ffn_all_shards_spec(block):
    return BS((N_SHARD, FF_SH, D_MODEL), lambda i: (0, block, 0))


def _ffn_down(name, a, w_f, resid, tm, after=()):
    t, d = resid.shape
    after_ops, after_specs = _after_operands(after)

    def body(a_ref, w_ref, res_ref, *rest):
        o_ref = rest[-1]
        acc = _dot(a_ref[0], w_ref[0])
        for s in range(1, N_SHARD):
            acc = acc + _dot(a_ref[s], w_ref[s])
        o_ref[...] = res_ref[...] + 0.5 * acc

    row = BS((tm, d), lambda i: (i, 0))
    return pl.pallas_call(
        body, out_shape=SDS((t, d), F32), grid=(t // tm,),
        in_specs=[BS((N_SHARD, tm, FF_SH), lambda i: (0, i, 0)), _ffn_all_shards_spec(w_f["down"][1]), row] + after_specs,
        out_specs=row, name=name, compiler_params=_params("parallel"))(a, w_f["down"][0], resid, *after_ops)


def _ffn_bwd_act(name, dh_b, w_f, pg, pu, tm, after=()):
    t, d = dh_b.shape
    after_ops, after_specs = _after_operands(after)

    def body(dh_ref, wd_ref, pg_ref, pu_ref, *rest):
        dg_ref, dup_ref = rest[len(after_ops):]
        da = _dot(dh_ref[...], wd_ref[...], NT)
        dg_ref[...] = (da * pg_ref[...].astype(F32)).astype(BF16)
        dup_ref[...] = (da * pu_ref[...].astype(F32)).astype(BF16)

    hid = BS((None, tm, FF_SH), lambda s, i: (s, i, 0))
    shape = SDS((N_SHARD, t, FF_SH), BF16)
    return pl.pallas_call(
        body, out_shape=(shape, shape), grid=(N_SHARD, t // tm),
        in_specs=[BS((tm, d), lambda s, i: (i, 0)), _ffn_w_spec(w_f["down"][1], 0), hid, hid] + after_specs,
        out_specs=(hid, hid), name=name,
        compiler_params=_params("parallel", "parallel"))(dh_b, w_f["down"][0], pg, pu, *after_ops)


def _ffn_dw(name, u, dg, dup, a, dh_b, tm):
    t, d = u.shape
    n_t = t // tm

    def body(u_ref, dg_ref, dup_ref, a_ref, dh_ref, o_ref, acc):
        i = pl.program_id(1)

        @pl.when(i == 0)
        def _():
            acc[...] = jnp.zeros_like(acc)

        uv = u_ref[...]
        acc[FFN_GATE] += _dot(dg_ref[...], uv, TN)
        acc[FFN_UP] += _dot(dup_ref[...], uv, TN)
        acc[FFN_DOWN] += _dot(a_ref[...], dh_ref[...], TN)

        @pl.when(i == n_t - 1)
        def _():
            o_ref[FFN_GATE] = acc[FFN_GATE].astype(BF16)
            o_ref[FFN_UP] = acc[FFN_UP].astype(BF16)
            o_ref[FFN_DOWN] = (0.5 * acc[FFN_DOWN]).astype(BF16)

    hid = BS((None, tm, FF_SH), lambda s, i: (s, i, 0))
    row = BS((tm, d), lambda s, i: (i, 0))
    return pl.pallas_call(
        body, out_shape=SDS((N_SHARD, 3, FF_SH, d), BF16), grid=(N_SHARD, n_t),
        in_specs=[row, hid, hid, hid, row], out_specs=BS((None, 3, FF_SH, d), lambda s, i: (s, 0, 0, 0)),
        scratch_shapes=[pltpu.VMEM((3, FF_SH, d), F32)],
        name=name, compiler_params=_params("parallel", "arbitrary"))(u, dg, dup, a, dh_b)


def _norm_bwd_tile(i, du, h_ref, g_ref, dhin_ref, dh_ref, dhb_ref, dg_ref):
    hv = h_ref[...]
    r = lax.rsqrt(jnp.mean(hv * hv, axis=-1, keepdims=True) + EPS)
    n = hv * r
    dn = du * g_ref[...]
    dh = dhin_ref[...] + r * (dn - n * jnp.mean(dn * n, axis=-1, keepdims=True))
    dh_ref[...] = dh
    dhb_ref[...] = dh.astype(BF16)

    @pl.when(i == 0)
    def _():
        dg_ref[...] = jnp.zeros_like(dg_ref)

    dg_ref[...] += jnp.sum(du * n, axis=0, keepdims=True)


def _norm_bwd_specs(tm):
    row = BS((tm, D_MODEL), lambda i: (i, 0))
    vec = BS((1, D_MODEL), lambda i: (0, 0))
    return [row, vec, row], (row, row, vec)


def _norm_bwd_shapes(t):
    return SDS((t, D_MODEL), F32), SDS((t, D_MODEL), BF16), SDS((1, D_MODEL), F32)


def _ffn_dx(name, dg, dup, w_f, h, gain, dh_in, tm, after=()):
    t = dg.shape[1]
    tm = tm // 2
    after_ops, after_specs = _after_operands(after)

    def body(dg_ref, dup_ref, wg_ref, wu_ref, h_ref, g_ref, dhin_ref, *rest):
        acc = _dot(dg_ref[0], wg_ref[0]) + _dot(dup_ref[0], wu_ref[0])
        for s in range(1, N_SHARD):
            acc = acc + _dot(dg_ref[s], wg_ref[s]) + _dot(dup_ref[s], wu_ref[s])
        _norm_bwd_tile(pl.program_id(0), acc, h_ref, g_ref, dhin_ref, *rest[len(after_ops):])

    hid = BS((N_SHARD, tm, FF_SH), lambda i: (0, i, 0))
    norm_in, norm_out = _norm_bwd_specs(tm)
    return pl.pallas_call(
        body, out_shape=_norm_bwd_shapes(t), grid=(t // tm,),
        in_specs=[hid, hid, _ffn_all_shards_spec(w_f["gate"][1]), _ffn_all_shards_spec(w_f["up"][1])] + norm_in + after_specs,
        out_specs=norm_out, name=name,
        compiler_params=_params("arbitrary"))(dg, dup, w_f["gate"][0], w_f["up"][0], h, gain, dh_in, *after_ops)


def _mm_norm_bwd(name, a, b, dims, h, gain, dh_in, tm):
    t = a.shape[0]

    def body(a_ref, b_ref, h_ref, g_ref, dhin_ref, *outs):
        _norm_bwd_tile(pl.program_id(0), _dot(a_ref[...], b_ref[...], dims), h_ref, g_ref, dhin_ref, *outs)

    norm_in, norm_out = _norm_bwd_specs(tm)
    return pl.pallas_call(
        body, out_shape=_norm_bwd_shapes(t), grid=(t // tm,),
        in_specs=[BS((tm, a.shape[1]), lambda i: (i, 0)), BS(b.shape, lambda i: (0, 0))] + norm_in,
        out_specs=norm_out, name=name, compiler_params=_params("arbitrary"))(a, b, h, gain, dh_in)


def _plain_mm(name, a, b, dims, out_dtype, tm, resid=None, after=()):
    t = a.shape[0]
    n = b.shape[1] if dims == NN else b.shape[0]
    extras = [(resid, BS((tm, n), lambda i: (i, 0)))] if resid is not None else []
    epi = (lambda acc, res: res + acc) if resid is not None else None
    return _mm(name, [(a, BS((tm, a.shape[1]), lambda i: (i, 0)), b, BS(b.shape, lambda i: (0, 0)), dims)],
               grid=(t // tm,), out_shape=SDS((t, n), out_dtype), out_spec=BS((tm, n), lambda i: (i, 0)),
               extras=extras, epilogue=epi, after=after)


def _dw_mm(name, a, b, tm, out_dtype=BF16, after=()):
    t, k = a.shape
    n = b.shape[1]
    return _mm(name, [(a, BS((tm, k), lambda i: (i, 0)), b, BS((tm, n), lambda i: (i, 0)), TN)],
               grid=(t // tm,), red_axis=0, out_shape=SDS((k, n), out_dtype), out_spec=BS((k, n), lambda i: (0, 0)),
               after=after)


POOL_CHUNK = 256
POOL_HALO = 8


def _window_sum(v, width, lead):
    n = v.shape[0]
    s = v
    k = 1
    while k < width:
        s = s + pltpu.roll(s, n - k, 0)
        k *= 2
    return pltpu.roll(s, lead, 0) if lead else s


def _pool_count(base, left, right, t, shape):
    pos = base + lax.broadcasted_iota(jnp.int32, shape, 0)
    lo = jnp.maximum(pos - left, 0)
    hi = jnp.minimum(pos + right + 1, t)
    return (hi - lo).astype(F32)


def _pool_fwd(proj, pool_w, pool_scale):
    t = proj.shape[0]
    c, h = POOL_CHUNK, POOL_HALO
    n_chunks = t // c

    def body(proj_hbm, pw_ref, sc_ref, pooled_ref, mixed_ref, ms_ref, pad_ref, sem):
        cp = pltpu.make_async_copy(proj_hbm.at[:, pl.ds(0, D_POOL)], pad_ref.at[pl.ds(h, t), :], sem)
        cp.start()
        pad_ref[pl.ds(0, h), :] = jnp.zeros((h, D_POOL), F32)
        pad_ref[pl.ds(t + h, h), :] = jnp.zeros((h, D_POOL), F32)
        cp.wait()
        for g, width in enumerate(POOL_WINDOWS):
            left = width // 2
            right = width - 1 - left
            cols = slice(g * POOL_GROUP, (g + 1) * POOL_GROUP)
            wmat = pw_ref[g].astype(BF16)
            scale = sc_ref[:, cols]

            def chunk(ci, carry, left=left, right=right, width=width, cols=cols, wmat=wmat, scale=scale):
                base = pl.multiple_of(ci * c, c)
                v = pad_ref[pl.ds(base, c + 2 * h), cols]
                win = _window_sum(v, width, left)[h:h + c]
                cnt = _pool_count(base, left, right, t, (c, POOL_GROUP))
                pooled = (win / cnt - v[h:h + c]).astype(BF16)
                mixed = _dot(pooled, wmat)
                pooled_ref[pl.ds(base, c), cols] = pooled
                mixed_ref[pl.ds(base, c), cols] = mixed.astype(BF16)
                ms_ref[pl.ds(base, c), cols] = (mixed * scale).astype(BF16)
                return carry

            lax.fori_loop(0, n_chunks, chunk, 0)

    vm = BS(memory_space=pltpu.VMEM)
    shape = SDS((t, D_POOL), BF16)
    return pl.pallas_call(
        body, out_shape=(shape, shape, shape),
        in_specs=[BS(memory_space=pl.ANY), vm, vm], out_specs=(vm, vm, vm),
        scratch_shapes=[pltpu.VMEM((t + 2 * h, D_POOL), F32), pltpu.SemaphoreType.DMA],
        name="pool_fwd", compiler_params=_params())(proj, pool_w, pool_scale)


def _pool_bwd(d_ms, mixed, pooled, pool_w, pool_scale):
    t = d_ms.shape[0]
    c, h = POOL_CHUNK, POOL_HALO
    n_chunks = t // c

    def body(dms_ref, mixed_ref, pooled_ref, pw_ref, sc_ref, dp_ref, dsc_ref, dpw_ref, pad_ref):
        pad_ref[pl.ds(0, h), :] = jnp.zeros((h, D_POOL), F32)
        pad_ref[pl.ds(t + h, h), :] = jnp.zeros((h, D_POOL), F32)
        for g, width in enumerate(POOL_WINDOWS):
            left = width // 2
            right = width - 1 - left
            cols = slice(g * POOL_GROUP, (g + 1) * POOL_GROUP)
            wmat = pw_ref[g].astype(BF16)
            scale = sc_ref[:, cols]

            def first(ci, carry, left=left, right=right, cols=cols, wmat=wmat, scale=scale):
                dsc, dpw = carry
                base = pl.multiple_of(ci * c, c)
                dms = dms_ref[pl.ds(base, c), cols].astype(F32)
                dsc = dsc + jnp.sum(dms * mixed_ref[pl.ds(base, c), cols].astype(F32), axis=0, keepdims=True)
                dmix = (dms * scale).astype(BF16)
                dpw = dpw + _dot(pooled_ref[pl.ds(base, c), cols], dmix, TN)
                dpooled = _dot(dmix, wmat, NT)
                cnt = _pool_count(base, left, right, t, (c, POOL_GROUP))
                pad_ref[pl.ds(base + h, c), cols] = dpooled / cnt
                return dsc, dpw

            dsc, dpw = lax.fori_loop(0, n_chunks, first,
                                     (jnp.zeros((1, POOL_GROUP), F32), jnp.zeros((POOL_GROUP, POOL_GROUP), F32)))
            dsc_ref[:, cols] = dsc
            dpw_ref[g] = dpw

            def second(ci, carry, left=left, right=right, width=width, cols=cols):
                base = pl.multiple_of(ci * c, c)
                v = pad_ref[pl.ds(base, c + 2 * h), cols]
                win = _window_sum(v, width, right)[h:h + c]
                cnt = _pool_count(base, left, right, t, (c, POOL_GROUP))
                dp_ref[pl.ds(base, c), cols] = (win - v[h:h + c] * cnt).astype(BF16)
                return carry

            lax.fori_loop(0, n_chunks, second, 0)

    vm = BS(memory_space=pltpu.VMEM)
    return pl.pallas_call(
        body, out_shape=(SDS((t, D_POOL), BF16), SDS((1, D_POOL), F32), SDS((4, POOL_GROUP, POOL_GROUP), F32)),
        in_specs=[vm] * 5, out_specs=(vm, vm, vm),
        scratch_shapes=[pltpu.VMEM((t + 2 * h, D_POOL), F32)],
        name="pool_bwd", compiler_params=_params())(d_ms, mixed, pooled, pool_w, pool_scale)


def _ssm_disc(ar, ai, ldt, after=()):
    after_ops, after_specs = _after_operands(after)

    def body(ar_ref, ai_ref, ldt_ref, *rest):
        abr_ref, abi_ref, qr_ref, qi_ref = rest[len(after_ops):]
        a_r, a_i = ar_ref[...], ai_ref[...]
        dt = jnp.exp(ldt_ref[...])
        mag = jnp.exp(dt * a_r)
        ang = dt * a_i
        abr = mag * jnp.cos(ang)
        abi = mag * jnp.sin(ang)
        den = a_r * a_r + a_i * a_i
        nr = abr - 1.0
        abr_ref[...] = abr
        abi_ref[...] = abi
        qr_ref[...] = (nr * a_r + abi * a_i) / den
        qi_ref[...] = (abi * a_r - nr * a_i) / den

    vm = BS(memory_space=pltpu.VMEM)
    shape = SDS(ar.shape, F32)
    return pl.pallas_call(body, out_shape=(shape,) * 4, in_specs=[vm] * 3 + after_specs, out_specs=(vm,) * 4,
                          name="ssm_disc", compiler_params=_params())(ar, ai, ldt, *after_ops)


def _ssm_disc_bwd(ar, ai, ldt, d_abr, d_abi, d_qr, d_qi):
    def body(ar_ref, ai_ref, ldt_ref, gabr_ref, gabi_ref, gqr_ref, gqi_ref, dar_ref, dai_ref, dldt_ref):
        a_r, a_i = ar_ref[...], ai_ref[...]
        dt = jnp.exp(ldt_ref[...])
        mag = jnp.exp(dt * a_r)
        ang = dt * a_i
        cs, sn = jnp.cos(ang), jnp.sin(ang)
        abr, abi = mag * cs, mag * sn
        den = a_r * a_r + a_i * a_i
        nr = abr - 1.0
        qr = (nr * a_r + abi * a_i) / den
        qi = (abi * a_r - nr * a_i) / den
        gqr, gqi = gqr_ref[...], gqi_ref[...]
        g_nr_num = gqr / den
        g_ni_num = gqi / den
        g_den = -(gqr * qr + gqi * qi) / den
        g_nr = g_nr_num * a_r - g_ni_num * a_i
        g_abi = g_nr_num * a_i + g_ni_num * a_r
        d_ar = g_nr_num * nr + g_ni_num * abi + 2.0 * a_r * g_den
        d_ai = g_nr_num * abi - g_ni_num * nr + 2.0 * a_i * g_den
        g_abr = gabr_ref[...] + g_nr
        g_abi = gabi_ref[...] + g_abi
        g_mag = g_abr * cs + g_abi * sn
        g_ang = mag * (g_abi * cs - g_abr * sn)
        g_e = g_mag * mag
        d_ar = d_ar + g_e * dt
        d_ai = d_ai + g_ang * dt
        g_dt = g_e * a_r + g_ang * a_i
        dar_ref[...] = d_ar
        dai_ref[...] = d_ai
        dldt_ref[...] = jnp.sum(g_dt * dt, axis=1, keepdims=True)

    vm = BS(memory_space=pltpu.VMEM)
    return pl.pallas_call(body, out_shape=(SDS(ar.shape, F32), SDS(ar.shape, F32), SDS(ldt.shape, F32)),
                          in_specs=[vm] * 7, out_specs=(vm,) * 3, name="ssm_disc_bwd",
                          compiler_params=_params())(ar, ai, ldt, d_abr, d_abi, d_qr, d_qi)


def _ssm_bbar(qr, qi, br, bi):
    def body(qr_ref, qi_ref, br_ref, bi_ref, bbr_ref, bbi_ref):
        q_r, q_i, b_r, b_i = qr_ref[...], qi_ref[...], br_ref[...], bi_ref[...]
        bbr_ref[...] = q_r * b_r - q_i * b_i
        bbi_ref[...] = q_r * b_i + q_i * b_r

    vm = BS(memory_space=pltpu.VMEM)
    shape = SDS(br.shape, F32)
    return pl.pallas_call(body, out_shape=(shape, shape), in_specs=[vm] * 4, out_specs=(vm, vm),
                          name="ssm_bbar", compiler_params=_params())(qr, qi, br, bi)


def _ssm_bbar_bwd(qr, qi, br, bi, g_bbr, g_bbi):
    def body(qr_ref, qi_ref, br_ref, bi_ref, gr_ref, gi_ref, dqr_ref, dqi_ref, dbr_ref, dbi_ref):
        q_r, q_i, b_r, b_i = qr_ref[...], qi_ref[...], br_ref[...], bi_ref[...]
        g_r, g_i = gr_ref[...], gi_ref[...]
        dqr_ref[...] = jnp.sum(g_r * b_r + g_i * b_i, axis=1, keepdims=True)
        dqi_ref[...] = jnp.sum(g_i * b_r - g_r * b_i, axis=1, keepdims=True)
        dbr_ref[...] = g_r * q_r + g_i * q_i
        dbi_ref[...] = g_i * q_r - g_r * q_i

    vm = BS(memory_space=pltpu.VMEM)
    return pl.pallas_call(
        body, out_shape=(SDS(qr.shape, F32), SDS(qr.shape, F32), SDS(br.shape, F32), SDS(br.shape, F32)),
        in_specs=[vm] * 6, out_specs=(vm,) * 4, name="ssm_bbar_bwd",
        compiler_params=_params())(qr, qi, br, bi, g_bbr, g_bbi)


SCAN_ROWS = 256


def _ssm_scan(name, inp, w1, a_r, a_i, w2, reverse):
    t = inp.shape[0]
    rows = SCAN_ROWS
    n = t // rows
    n_groups = rows // 8
    ch = SSM_CH
    at = (lambda i: (n - 1 - i, 0)) if reverse else (lambda i: (i, 0))

    def body(in_ref, w1_ref, ar_ref, ai_ref, w2_ref, sb_ref, out_ref, cr_ref, ci_ref, k_ref, st_ref):
        i = pl.program_id(0)

        @pl.when(i == 0)
        def _():
            ar8 = jnp.broadcast_to(ar_ref[...], (8, ch))
            ai8 = jnp.broadcast_to(ai_ref[...], (8, ch))
            row = lax.broadcasted_iota(jnp.int32, (8, ch), 0)
            rank = (7 - row) if reverse else row
            powers = [(ar8, ai8)]
            for _ in range(7):
                p_r, p_i = powers[-1]
                powers.append((p_r * ar8 - p_i * ai8, p_r * ai8 + p_i * ar8))
            zero = jnp.zeros((8, ch), F32)
            for slot, k in enumerate((1, 2, 4)):
                k_ref[2 * slot] = jnp.where(rank >= k, powers[k - 1][0], zero)
                k_ref[2 * slot + 1] = jnp.where(rank >= k, powers[k - 1][1], zero)
            carry_r, carry_i = zero, zero
            for j in range(8):
                carry_r = jnp.where(rank == j, powers[j][0], carry_r)
                carry_i = jnp.where(rank == j, powers[j][1], carry_i)
            k_ref[6] = carry_r
            k_ref[7] = carry_i
            cr_ref[...] = zero
            ci_ref[...] = zero

        st_ref[...] = _dot(in_ref[...], w1_ref[...])

        def group(gi, carry):
            c_r, c_i = carry
            g = (n_groups - 1 - gi) if reverse else gi
            r0 = pl.multiple_of(g * 8, 8)
            x_r = st_ref[pl.ds(r0, 8), 0:ch]
            x_i = st_ref[pl.ds(r0, 8), ch:2 * ch]
            for slot, k in enumerate((1, 2, 4)):
                shift = (8 - k) if reverse else k
                s_r = pltpu.roll(x_r, shift, 0)
                s_i = pltpu.roll(x_i, shift, 0)
                m_r, m_i = k_ref[2 * slot], k_ref[2 * slot + 1]
                x_r, x_i = x_r + m_r * s_r - m_i * s_i, x_i + m_r * s_i + m_i * s_r
            p_r, p_i = k_ref[6], k_ref[7]
            x_r, x_i = x_r + p_r * c_r - p_i * c_i, x_i + p_r * c_i + p_i * c_r
            st_ref[pl.ds(r0, 8), 0:ch] = x_r
            st_ref[pl.ds(r0, 8), ch:2 * ch] = x_i
            last = 0 if reverse else 7
            return (jnp.broadcast_to(x_r[last:last + 1, :], (8, ch)), jnp.broadcast_to(x_i[last:last + 1, :], (8, ch)))

        c_r, c_i = lax.fori_loop(0, n_groups, group, (cr_ref[...], ci_ref[...]))
        cr_ref[...] = c_r
        ci_ref[...] = c_i
        states = st_ref[...].astype(BF16)
        sb_ref[...] = states
        out_ref[...] = _dot(states, w2_ref[...])

    return pl.pallas_call(
        body, out_shape=(SDS((t, 2 * ch), BF16), SDS((t, D_SSM), F32)), grid=(n,),
        in_specs=[BS((rows, D_SSM), at), BS((D_SSM, 2 * ch), lambda i: (0, 0)), BS((1, ch), lambda i: (0, 0)),
                  BS((1, ch), lambda i: (0, 0)), BS((2 * ch, D_SSM), lambda i: (0, 0))],
        out_specs=(BS((rows, 2 * ch), at), BS((rows, D_SSM), at)),
        scratch_shapes=[pltpu.VMEM((8, ch), F32), pltpu.VMEM((8, ch), F32), pltpu.VMEM((8, 8, ch), F32),
                        pltpu.VMEM((rows, 2 * ch), F32)],
        name=name, compiler_params=_params("arbitrary"))(inp, w1, a_r, a_i, w2)


DA_ROWS = 512


def _ssm_da(name, lam, states, reverse, after=()):
    t = lam.shape[0]
    rows = DA_ROWS
    n = t // rows
    halo_rows = 16
    nb = rows // halo_rows
    ch = SSM_CH
    if reverse:
        halo_at = lambda i: (jnp.minimum((i + 1) * nb, t // halo_rows - 1), 0)
    else:
        halo_at = lambda i: (jnp.maximum(i * nb - 1, 0), 0)

    after_ops, after_specs = _after_operands(after)

    def body(lam_ref, x_ref, halo_ref, *rest):
        dr_ref, di_ref = rest[len(after_ops):]
        i = pl.program_id(0)

        @pl.when(i == 0)
        def _():
            dr_ref[...] = jnp.zeros_like(dr_ref)
            di_ref[...] = jnp.zeros_like(di_ref)

        row = lax.broadcasted_iota(jnp.int32, (rows, ch), 0)
        if reverse:
            edge, shift, h_row, live = rows - 1, rows - 1, 0, i < n - 1
        else:
            edge, shift, h_row, live = 0, 1, halo_rows - 1, i > 0

        def neighbour(lo):
            halo = halo_ref[:, lo:lo + ch].astype(F32)[h_row:h_row + 1]
            halo = jnp.where(live, halo, 0.0)
            x = x_ref[:, lo:lo + ch].astype(F32)
            return jnp.where(row == edge, jnp.broadcast_to(halo, (rows, ch)), pltpu.roll(x, shift, 0))

        xp_r, xp_i = neighbour(0), neighbour(ch)
        l_r, l_i = lam_ref[:, 0:ch].astype(F32), lam_ref[:, ch:2 * ch].astype(F32)
        dr_ref[...] += jnp.sum(l_r * xp_r + l_i * xp_i, axis=0, keepdims=True)
        di_ref[...] += jnp.sum(l_i * xp_r - l_r * xp_i, axis=0, keepdims=True)

    blk = BS((rows, 2 * ch), lambda i: (i, 0))
    vec = BS((1, ch), lambda i: (0, 0))
    return pl.pallas_call(
        body, out_shape=(SDS((1, ch), F32), SDS((1, ch), F32)), grid=(n,),
        in_specs=[blk, blk, BS((halo_rows, 2 * ch), halo_at)] + after_specs, out_specs=(vec, vec),
        name=name, compiler_params=_params("arbitrary"))(lam, states, states, *after_ops)


GELU_C = math.sqrt(2.0 / math.pi)
GELU_K = 0.044715


def _ssm_combine(proj, y_fwd, y_bwd, d_skip, tm, after=()):
    t = proj.shape[0]
    after_ops, after_specs = _after_operands(after)

    def body(s_ref, yf_ref, yb_ref, d_ref, *rest):
        yt_ref, g_ref = rest[len(after_ops):]
        y = s_ref[...] * d_ref[...] + yf_ref[...] + yb_ref[...]
        yt_ref[...] = y
        th = jnp.tanh(GELU_C * (y + GELU_K * y * y * y))
        g_ref[...] = (0.5 * y * (1.0 + th)).astype(BF16)

    blk = BS((tm, D_SSM), lambda i: (i, 0))
    return pl.pallas_call(
        body, out_shape=(SDS((t, D_SSM), F32), SDS((t, D_SSM), BF16)), grid=(t // tm,),
        in_specs=[BS((tm, D_SSM), lambda i: (i, D_POOL // D_SSM)), blk, blk, BS((1, D_SSM), lambda i: (0, 0))] + after_specs,
        out_specs=(blk, blk), name="ssm_combine",
        compiler_params=_params("parallel"))(proj, y_fwd, y_bwd, d_skip, *after_ops)


def _ssm_ds(proj, d_yt, du_fwd, du_bwd, d_skip, tm):
    t = proj.shape[0]

    def body(s_ref, dy_ref, duf_ref, dub_ref, d_ref, ds_ref, dd_ref):
        i = pl.program_id(0)
        dy = dy_ref[...]
        ds_ref[...] = (dy * d_ref[...] + duf_ref[...] + dub_ref[...]).astype(BF16)

        @pl.when(i == 0)
        def _():
            dd_ref[...] = jnp.zeros_like(dd_ref)

        dd_ref[...] += jnp.sum(dy * s_ref[...], axis=0, keepdims=True)

    blk = BS((tm, D_SSM), lambda i: (i, 0))
    vec = BS((1, D_SSM), lambda i: (0, 0))
    return pl.pallas_call(
        body, out_shape=(SDS((t, D_SSM), BF16), SDS((1, D_SSM), F32)), grid=(t // tm,),
        in_specs=[BS((tm, D_SSM), lambda i: (i, D_POOL // D_SSM)), blk, blk, blk, vec],
        out_specs=(blk, vec), name="ssm_ds", compiler_params=_params("arbitrary"))(proj, d_yt, du_fwd, du_bwd, d_skip)


GP_BLOCK = (D_POOL + D_SSM) // 256
GS_BLOCK = GP_BLOCK + D_MODEL // 256


def _merge_specs(tm):
    return [BS((tm, D_POOL), lambda s, i: (i, 0)), BS((tm, D_SSM), lambda s, i: (i, 0)),
            BS((None, D_POOL, 256), lambda s, i: (s, 0, 0)), BS((None, D_SSM, 256), lambda s, i: (s, 2, 0)),
            BS((None, D_SSM, 256), lambda s, i: (s, 3, 0)),
            BS((tm, 256), lambda s, i: (i, GP_BLOCK + s)), BS((tm, 256), lambda s, i: (i, GS_BLOCK + s))]


def _mixer_merge(ms, yssm, w_e, proj, tm):
    t = ms.shape[0]

    def body(ms_ref, y_ref, wpp_ref, wgv_ref, wgg_ref, gp_ref, gs_ref, o_ref):
        zp = _dot(ms_ref[...], wpp_ref[...])
        yv = y_ref[...]
        zv = _dot(yv, wgv_ref[...])
        zg = _dot(yv, wgg_ref[...])
        o_ref[...] = (_sigmoid(gp_ref[...]) * zp + _sigmoid(gs_ref[...]) * zv * _sigmoid(zg)).astype(BF16)

    col = BS((tm, 256), lambda s, i: (i, s))
    return pl.pallas_call(
        body, out_shape=SDS((t, D_MODEL), BF16), grid=(N_SHARD, t // tm), in_specs=_merge_specs(tm), out_specs=col,
        name="mixer_merge", compiler_params=_params("parallel", "parallel"))(ms, yssm, w_e, w_e, w_e, proj, proj)


def _mixer_merge_bwd(ms, yssm, w_e, proj, dmerged, tm):
    t = ms.shape[0]

    def body(ms_ref, y_ref, wpp_ref, wgv_ref, wgg_ref, gp_ref, gs_ref, dm_ref,
             dgp_ref, dgs_ref, dzp_ref, dzv_ref, dzg_ref):
        zp = _dot(ms_ref[...], wpp_ref[...])
        yv = y_ref[...]
        zv = _dot(yv, wgv_ref[...])
        zg = _dot(yv, wgg_ref[...])
        dm = dm_ref[...].astype(F32)
        sp, ss, sg = _sigmoid(gp_ref[...]), _sigmoid(gs_ref[...]), _sigmoid(zg)
        dgp_ref[...] = (dm * zp * sp * (1.0 - sp)).astype(BF16)
        dgs_ref[...] = (dm * zv * sg * ss * (1.0 - ss)).astype(BF16)
        dzp_ref[...] = (dm * sp).astype(BF16)
        dz = dm * ss
        dzv_ref[...] = (dz * sg).astype(BF16)
        dzg_ref[...] = (dz * zv * sg * (1.0 - sg)).astype(BF16)

    col = BS((tm, 256), lambda s, i: (i, s))
    shape = SDS((t, D_MODEL), BF16)
    return pl.pallas_call(
        body, out_shape=(shape,) * 5, grid=(N_SHARD, t // tm), in_specs=_merge_specs(tm) + [col],
        out_specs=(col,) * 5, name="mixer_merge_bwd",
        compiler_params=_params("parallel", "parallel"))(ms, yssm, w_e, w_e, w_e, proj, proj, dmerged)


def _mixer_dw(ms, yssm, dzp, dzv, dzg, tm):
    t = ms.shape[0]
    n_t = t // tm

    def body(ms_ref, y_ref, dzp_ref, dzv_ref, dzg_ref, o_ref, acc):
        i = pl.program_id(1)

        @pl.when(i == 0)
        def _():
            acc[...] = jnp.zeros_like(acc)

        yv = y_ref[...]
        acc[0:D_POOL, :] += _dot(ms_ref[...], dzp_ref[...], TN)
        acc[D_POOL:D_POOL + D_SSM, :] += _dot(yv, dzv_ref[...], TN)
        acc[D_POOL + D_SSM:, :] += _dot(yv, dzg_ref[...], TN)

        @pl.when(i == n_t - 1)
        def _():
            o_ref[...] = acc[...].astype(BF16)

    col = BS((tm, 256), lambda s, i: (i, s))
    return pl.pallas_call(
        body, out_shape=SDS((N_SHARD, 1024, 256), BF16), grid=(N_SHARD, n_t),
        in_specs=[BS((tm, D_POOL), lambda s, i: (i, 0)), BS((tm, D_SSM), lambda s, i: (i, 0)), col, col, col],
        out_specs=BS((None, 1024, 256), lambda s, i: (s, 0, 0)), scratch_shapes=[pltpu.VMEM((1024, 256), F32)],
        name="mixer_dw", compiler_params=_params("parallel", "arbitrary"))(ms, yssm, dzp, dzv, dzg)


def _mixer_dx(dzp, dzv, dzg, w_e, y_total, tm):
    t = dzp.shape[0]

    def body(dzp_ref, dzv_ref, dzg_ref, wpp_ref, wgv_ref, wgg_ref, yt_ref, dms_ref, dy_ref, acc_ms, acc_y):
        s = pl.program_id(1)

        @pl.when(s == 0)
        def _():
            acc_ms[...] = jnp.zeros_like(acc_ms)
            acc_y[...] = jnp.zeros_like(acc_y)

        acc_ms[...] += _dot(dzp_ref[...], wpp_ref[...], NT)
        acc_y[...] += _dot(dzv_ref[...], wgv_ref[...], NT) + _dot(dzg_ref[...], wgg_ref[...], NT)

        @pl.when(s == N_SHARD - 1)
        def _():
            dms_ref[...] = acc_ms[...].astype(BF16)
            y = yt_ref[...]
            inner = GELU_C * (y + GELU_K * y * y * y)
            th = jnp.tanh(inner)
            dgelu = 0.5 * (1.0 + th) + 0.5 * y * (1.0 - th * th) * GELU_C * (1.0 + 3.0 * GELU_K * y * y)
            dy_ref[...] = acc_y[...] * dgelu

    col = BS((tm, 256), lambda i, s: (i, s))
    return pl.pallas_call(
        body, out_shape=(SDS((t, D_POOL), BF16), SDS((t, D_SSM), F32)), grid=(t // tm, N_SHARD),
        in_specs=[col, col, col, BS((None, D_POOL, 256), lambda i, s: (s, 0, 0)),
                  BS((None, D_SSM, 256), lambda i, s: (s, 2, 0)), BS((None, D_SSM, 256), lambda i, s: (s, 3, 0)),
                  BS((tm, D_SSM), lambda i, s: (i, 0))],
        out_specs=(BS((tm, D_POOL), lambda i, s: (i, 0)), BS((tm, D_SSM), lambda i, s: (i, 0))),
        scratch_shapes=[pltpu.VMEM((tm, D_POOL), F32), pltpu.VMEM((tm, D_SSM), F32)],
        name="mixer_dx", compiler_params=_params("parallel", "arbitrary"))(dzp, dzv, dzg, w_e, w_e, w_e, y_total)


def _attn_probs(q_h, k_h):
    s = _dot(q_h, k_h, NT) * (1.0 / math.sqrt(HEAD_DIM))
    e = jnp.exp(s - jnp.max(s, axis=-1, keepdims=True))
    return e / jnp.sum(e, axis=-1, keepdims=True)


def _attn_fwd(q, kv, tm):
    t = q.shape[0]
    m = kv.shape[0]

    def body(q_ref, kv_ref, o_ref):
        for hd in range(N_HEADS):
            lo = hd * HEAD_DIM
            p = _attn_probs(q_ref[:, lo:lo + HEAD_DIM], kv_ref[:, lo:lo + HEAD_DIM])
            o_ref[:, lo:lo + HEAD_DIM] = _dot(p, kv_ref[:, D_MODEL + lo:D_MODEL + lo + HEAD_DIM]).astype(BF16)

    return pl.pallas_call(
        body, out_shape=SDS((t, D_MODEL), BF16), grid=(t // tm,),
        in_specs=[BS((tm, D_MODEL), lambda i: (i, 0)), BS((m, 2 * D_MODEL), lambda i: (0, 0))],
        out_specs=BS((tm, D_MODEL), lambda i: (i, 0)), name="attn_fwd", compiler_params=_params("parallel"))(q, kv)


def _attn_bwd(q, kv, d_o, tm):
    t = q.shape[0]
    m = kv.shape[0]

    def body(q_ref, kv_ref, do_ref, dq_ref, dkv_ref):
        i = pl.program_id(0)

        @pl.when(i == 0)
        def _():
            dkv_ref[...] = jnp.zeros_like(dkv_ref)

        for hd in range(N_HEADS):
            lo = hd * HEAD_DIM
            q_h = q_ref[:, lo:lo + HEAD_DIM]
            k_h = kv_ref[:, lo:lo + HEAD_DIM]
            v_h = kv_ref[:, D_MODEL + lo:D_MODEL + lo + HEAD_DIM]
            do_h = do_ref[:, lo:lo + HEAD_DIM]
            p = _attn_probs(q_h, k_h)
            dkv_ref[:, D_MODEL + lo:D_MODEL + lo + HEAD_DIM] += _dot(p, do_h, TN)
            dp = _dot(do_h, v_h, NT)
            ds = p * (dp - jnp.sum(dp * p, axis=-1, keepdims=True)) * (1.0 / math.sqrt(HEAD_DIM))
            dq_ref[:, lo:lo + HEAD_DIM] = _dot(ds, k_h).astype(BF16)
            dkv_ref[:, lo:lo + HEAD_DIM] += _dot(ds, q_h, TN)

    row = BS((tm, D_MODEL), lambda i: (i, 0))
    full = BS((m, 2 * D_MODEL), lambda i: (0, 0))
    return pl.pallas_call(
        body, out_shape=(SDS((t, D_MODEL), BF16), SDS((m, 2 * D_MODEL), F32)), grid=(t // tm,),
        in_specs=[row, full, row], out_specs=(row, full), name="attn_bwd",
        compiler_params=_params("arbitrary"))(q, kv, d_o)


TRANSPOSED = ("ffn1_w_gate", "ffn1_w_up", "ffn2_w_gate", "ffn2_w_up", "w_in")
GATHER_PHASES = {"f1a": (("ffn1_w_gate", "ffn1_w_up"),),
                 "f1b": (("ffn1_w_down",),),
                 "win": (("w_in",),),
                 "mix": (("w_mix_out", "w_q", "w_xo"), ("w_kv",), ("w_pool_proj", "w_glu_val", "w_glu_gate")),
                 "f2": (("ffn2_w_gate", "ffn2_w_up", "ffn2_w_down"),)}
REDUCE_GROUPS = (("ffn2_w_gate", "ffn2_w_up", "ffn2_w_down"), ("w_xo",), ("w_q",), ("w_kv",), ("w_mix_out",),
                 ("w_pool_proj", "w_glu_val", "w_glu_gate"), ("w_in",), ("ffn1_w_gate", "ffn1_w_up", "ffn1_w_down"))
SMALL = ("ffn1_norm", "mix_norm", "pool_w", "pool_scale", "ssm_a_re", "ssm_a_im", "ssm_log_dt", "ssm_b_re",
         "ssm_b_im", "ssm_c_re", "ssm_c_im", "ssm_d", "xattn_norm", "mem_norm", "ffn2_norm", "final_norm")
WEIGHTS = ("ffn1_norm", "ffn1_w_gate", "ffn1_w_up", "ffn1_w_down", "mix_norm", "w_in", "pool_w", "pool_scale",
           "w_pool_proj", "ssm_a_re", "ssm_a_im", "ssm_log_dt", "ssm_b_re", "ssm_b_im", "ssm_c_re", "ssm_c_im",
           "ssm_d", "w_glu_val", "w_glu_gate", "w_mix_out", "xattn_norm", "mem_norm", "w_q", "w_kv", "w_xo",
           "ffn2_norm", "ffn2_w_gate", "ffn2_w_up", "ffn2_w_down", "final_norm")


def _block_diag_in(bb):
    eye = jnp.eye(SSM_GROUPS, dtype=bb.dtype)
    return jnp.einsum("dgph,gk->dghkp", bb, eye).reshape(2, D_SSM, SSM_CH)


def _block_diag_out(cc):
    eye = jnp.eye(SSM_GROUPS, dtype=cc.dtype)
    return jnp.einsum("dghp,gk->dgpkh", cc, eye).reshape(2, SSM_CH, D_SSM)


def _diag_blocks_in(m):
    return jnp.einsum("dghgp->dgph", m.reshape(2, SSM_GROUPS, SSM_GROUP, SSM_GROUPS, SSM_STATE))


def _diag_blocks_out(m):
    return jnp.einsum("dgpgh->dghp", m.reshape(2, SSM_GROUPS, SSM_STATE, SSM_GROUPS, SSM_GROUP))


def _device_step(x, mem, target, wts, sp, reducer=None):
    t = x.shape[0]
    tm = min(TM, t)
    g = {}

    first_gather = wts.start("f1a")
    u1 = _rmsnorm("norm_ffn1", x, sp["ffn1_norm"], tm, after=first_gather)

    ar = sp["ssm_a_re"].reshape(2 * SSM_GROUPS, SSM_STATE)
    ai = sp["ssm_a_im"].reshape(2 * SSM_GROUPS, SSM_STATE)
    ldt = sp["ssm_log_dt"].reshape(2 * SSM_GROUPS, 1)
    abr, abi, qr, qi = _ssm_disc(ar, ai, ldt, after=first_gather)
    b_r = sp["ssm_b_re"].reshape(2 * SSM_CH, SSM_GROUP)
    b_i = sp["ssm_b_im"].reshape(2 * SSM_CH, SSM_GROUP)
    qr_col, qi_col = qr.reshape(2 * SSM_CH, 1), qi.reshape(2 * SSM_CH, 1)
    bbr, bbi = _ssm_bbar(qr_col, qi_col, b_r, b_i)
    shape_b = (2, SSM_GROUPS, SSM_STATE, SSM_GROUP)
    b_mat = jnp.concatenate([_block_diag_in(bbr.reshape(shape_b)), _block_diag_in(bbi.reshape(shape_b))], axis=-1).astype(BF16)
    c_mat = jnp.concatenate([_block_diag_out(sp["ssm_c_re"][0]), -_block_diag_out(sp["ssm_c_im"][0])], axis=1).astype(BF16)
    b_mat_t = jnp.swapaxes(b_mat, 1, 2)
    c_mat_t = jnp.swapaxes(c_mat, 1, 2)
    a_r = abr.reshape(2, 1, SSM_CH)
    a_i = abi.reshape(2, 1, SSM_CH)
    mem_n = _rmsnorm("norm_mem", mem, sp["mem_norm"], mem.shape[0], after=first_gather)

    (w_gu,) = wts.finish("f1a", [u1, b_mat, c_mat, b_mat_t, c_mat_t, mem_n])
    w_f1 = {"gate": (w_gu, FFN_GATE), "up": (w_gu, FFN_UP)}
    down_gather = wts.start("f1b", [w_gu])
    g1, up1, a1 = _ffn_up("ffn1_up", u1, w_f1, tm, after=wts.start("win", down_gather))
    (w_dn,) = wts.finish("f1b", [a1])
    w_f1["down"] = (w_dn, 0)
    h1 = _ffn_down("ffn1_down", a1, w_f1, x, tm)

    u2 = _rmsnorm("norm_mix", h1, sp["mix_norm"], tm)
    (w_in_g,) = wts.finish("win", [u2])
    w_in_t = w_in_g.reshape(D_FF, D_MODEL)
    proj = _mm("mix_in", [(u2, BS((tm, D_MODEL), lambda j, i: (i, 0)), w_in_t, BS((D_FF // 2, D_MODEL), lambda j, i: (j, 0)), NT)],
               grid=(2, t // tm), out_shape=SDS((t, D_FF), F32), out_spec=BS((tm, D_FF // 2), lambda j, i: (i, j)),
               after=wts.start("f2", wts.start("mix", [w_in_g])))
    pooled, mixed, ms = _pool_fwd(proj, sp["pool_w"][0], sp["pool_scale"])

    s_in = proj[:, D_POOL:D_POOL + D_SSM].astype(BF16)
    states, y_dirs = [], []
    for dr in range(2):
        st, yd = _ssm_scan(f"ssm_scan_fwd{dr}", s_in, b_mat[dr], a_r[dr], a_i[dr], c_mat[dr], reverse=(dr == 1))
        states.append(st)
        y_dirs.append(yd)
    w_sq, w_kv, w_e = wts.finish("mix", y_dirs)
    w_mo, w_q, w_xo = (w_sq[:, 256 * k:256 * (k + 1)].reshape(D_MODEL, D_MODEL) for k in range(3))
    w_d = w_kv[:, None]
    y_total, yssm = _ssm_combine(proj, y_dirs[0], y_dirs[1], sp["ssm_d"], tm)

    merged = _mixer_merge(ms, yssm, w_e, proj, tm)
    h2 = _plain_mm("mix_out", merged, w_mo, NN, F32, tm, resid=h1)

    u3 = _rmsnorm("norm_xattn", h2, sp["xattn_norm"], tm)
    q =_plain_mm("attn_q", u3, w_q, NN, BF16, tm)
    n_mem = mem.shape[0]
    kv = _mm("attn_kv", [(mem_n, BS((n_mem, D_MODEL), lambda s: (0, 0)), w_d, BS((None, None, D_MODEL, 512), lambda s: (s, 0, 0, 0)), NN)],
             grid=(N_SHARD,), out_shape=SDS((n_mem, 2 * D_MODEL), BF16), out_spec=BS((n_mem, 512), lambda s: (0, s)))
    o = _attn_fwd(q, kv, tm)
    h3 = _plain_mm("attn_out", o, w_xo, NN, F32, tm, resid=h2)

    u4 = _rmsnorm("norm_ffn2", h3, sp["ffn2_norm"], tm)
    (w_2,) = wts.finish("f2", [u4])
    w_f2 = {"gate": (w_2, FFN_GATE), "up": (w_2, FFN_UP), "down": (w_2, FFN_DOWN)}
    g2, up2, a2 = _ffn_up("ffn2_up", u4, w_f2, tm)
    h4 = _ffn_down("ffn2_down", a2, w_f2, h3, tm)

    loss, dh4, dh4_b, g["final_norm"] = _loss_head(h4, sp["final_norm"].reshape(1, D_MODEL), target, tm)

    dg2, dup2 = _ffn_bwd_act("ffn2_bwd_act", dh4_b, w_f2, g2, up2, tm)
    dw_f2 = _ffn_dw("ffn2_dw", u4, dg2, dup2, a2, dh4_b, tm)
    dh3, dh3_b, g["ffn2_norm"] = _ffn_dx("ffn2_dx", dg2, dup2, w_f2, h3, sp["ffn2_norm"], dh4, tm)

    d_o = _plain_mm("attn_out_dx", dh3_b, w_xo, NT, BF16, tm)
    dw_xo = _dw_mm("attn_out_dw", o, dh3_b, tm)
    dq, dkv = _attn_bwd(q, kv, d_o, tm)
    dw_q = _dw_mm("attn_q_dw", u3, dq, tm)
    dh2, dh2_b, g["xattn_norm"] = _mm_norm_bwd("attn_q_dx", dq, w_q, NT, h2, sp["xattn_norm"], dh3, tm)
    dw_kv = _mm("attn_kv_dw", [(mem_n, BS((n_mem, D_MODEL), lambda s: (0, 0)), dkv, BS((n_mem, 512), lambda s: (0, s)), TN)],
                grid=(N_SHARD,), out_shape=SDS((N_SHARD, D_MODEL, 512), BF16), out_spec=BS((None, D_MODEL, 512), lambda s: (s, 0, 0)))
    dmem_n = _mm("attn_kv_dx", [(dkv, BS((n_mem, 512), lambda s: (0, s)), w_d, BS((None, None, D_MODEL, 512), lambda s: (s, 0, 0, 0)), NT)],
                 grid=(N_SHARD,), red_axis=0, out_shape=SDS((n_mem, D_MODEL), F32), out_spec=BS((n_mem, D_MODEL), lambda s: (0, 0)))
    _, _, g["mem_norm"] = _rmsnorm_bwd("norm_mem_bwd", mem, sp["mem_norm"], dmem_n, None, n_mem)

    square = (N_SHARD, D_MODEL // N_SHARD, D_MODEL)
    early = [dw_f2.reshape(N_SHARD, 3 * FF_SH, D_MODEL), dw_xo.reshape(square), dw_q.reshape(square), dw_kv]
    swapping = reducer.swap_start("a1", early) if reducer is not None else []
    dmerged = _plain_mm("mix_out_dx", dh2_b, w_mo, NT, BF16, tm, after=swapping)
    dw_mo = _dw_mm("mix_out_dw", merged, dh2_b, tm)
    d_gp, d_gs, dzp, dzv, dzg = _mixer_merge_bwd(ms, yssm, w_e, proj, dmerged, tm)
    dw_e = _mixer_dw(ms, yssm, dzp, dzv, dzg, tm)
    d_ms, d_yt = _mixer_dx(dzp, dzv, dzg, w_e, y_total, tm)
    dp, d_scale, d_pw = _pool_bwd(d_ms, mixed, pooled, sp["pool_w"][0], sp["pool_scale"])
    g["pool_scale"] = d_scale
    g["pool_w"] = d_pw[None]

    d_yt_b = d_yt.astype(BF16)
    du_dirs, lams = [], []
    for dr in range(2):
        lam, du = _ssm_scan(f"ssm_scan_bwd{dr}", d_yt_b, c_mat_t[dr], a_r[dr], -a_i[dr], b_mat_t[dr], reverse=(dr == 0))
        du_dirs.append(du)
        lams.append(lam)
    ds, g["ssm_d"] = _ssm_ds(proj, d_yt, du_dirs[0], du_dirs[1], sp["ssm_d"], tm)

    d_proj = jnp.concatenate([dp, ds, d_gp, d_gs], axis=1)
    dw_in_t = _mm("mix_in_dw", [(d_proj, BS((tm, D_FF // 2), lambda j, i: (i, j)), u2, BS((tm, D_MODEL), lambda j, i: (i, 0)), TN)],
                  grid=(2, t // tm), red_axis=1, out_shape=SDS((D_FF, D_MODEL), BF16), out_spec=BS((D_FF // 2, D_MODEL), lambda j, i: (j, 0)))
    dh1, dh1_b, g["mix_norm"] = _mm_norm_bwd("mix_in_dx", d_proj, w_in_t, NN, h1, sp["mix_norm"], dh2, tm)

    early += [dw_mo.reshape(square), dw_e, dw_in_t.reshape(N_SHARD, FF_SH, D_MODEL)]
    g["final_norm"] = g["final_norm"].reshape(D_MODEL)

    travelling = reducer.start("a", early[4:], swapped=["a1"], after=list(g.values())) if reducer is not None else []
    d_abr, d_abi, d_cm, d_bm = [], [], [], []
    for dr in range(2):
        da_r, da_i = _ssm_da(f"ssm_da{dr}", lams[dr], states[dr], reverse=(dr == 1), after=travelling)
        d_abr.append(da_r)
        d_abi.append(da_i)
        d_cm.append(_dw_mm(f"ssm_dc{dr}", states[dr], d_yt_b, tm, F32, after=travelling))
        d_bm.append(_dw_mm(f"ssm_db{dr}", s_in, lams[dr], tm, F32, after=travelling))
    d_cm = jnp.stack(d_cm)
    d_bm = jnp.stack(d_bm)
    g["ssm_c_re"] = _diag_blocks_out(d_cm[:, :SSM_CH])[None]
    g["ssm_c_im"] = -_diag_blocks_out(d_cm[:, SSM_CH:])[None]
    g_bbr = _diag_blocks_in(d_bm[:, :, :SSM_CH]).reshape(2 * SSM_CH, SSM_GROUP)
    g_bbi = _diag_blocks_in(d_bm[:, :, SSM_CH:]).reshape(2 * SSM_CH, SSM_GROUP)
    d_qr, d_qi, d_br, d_bi = _ssm_bbar_bwd(qr_col, qi_col, b_r, b_i, g_bbr, g_bbi)
    g["ssm_b_re"] = d_br.reshape(sp["ssm_b_re"].shape)
    g["ssm_b_im"] = d_bi.reshape(sp["ssm_b_im"].shape)
    d_ar, d_ai, d_ldt = _ssm_disc_bwd(ar, ai, ldt, jnp.stack(d_abr).reshape(ar.shape), jnp.stack(d_abi).reshape(ar.shape),
                                      d_qr.reshape(ar.shape), d_qi.reshape(ar.shape))
    g["ssm_a_re"] = d_ar.reshape(sp["ssm_a_re"].shape)
    g["ssm_a_im"] = d_ai.reshape(sp["ssm_a_im"].shape)
    g["ssm_log_dt"] = d_ldt.reshape(sp["ssm_log_dt"].shape)
    if reducer is not None:
        travelling = travelling + [d_ar, d_ai, d_ldt, d_br, d_bi, d_cm]
    dg1, dup1 = _ffn_bwd_act("ffn1_bwd_act", dh1_b, w_f1, g1, up1, tm, after=travelling)
    dw_f1 = _ffn_dw("ffn1_dw", u1, dg1, dup1, a1, dh1_b, tm).reshape(N_SHARD, 3 * FF_SH, D_MODEL)
    if reducer is not None:
        travelling = reducer.start("b", [dw_f1], after=reducer.finish("a", [dw_f1]))
        travelling = travelling + reducer.join_start("a", after=travelling)
    grad_x, _, g["ffn1_norm"] = _ffn_dx("ffn1_dx", dg1, dup1, w_f1, x, sp["ffn1_norm"], dh1, tm, after=travelling)
    if reducer is not None:
        reducer.finish("b", [grad_x])
        reducer.join_finish("a", [grad_x])
    return loss, grad_x, early + [dw_f1], g


def _mesh_place():
    x, y, c = lax.axis_index("x"), lax.axis_index("y"), lax.axis_index("c")
    chips = [(1 - x, y), (x, 1 - y), (1 - x, 1 - y)]
    return x, y, c, chips


def _remote(src, dst, send_sems, recv_sems, k, to):
    return pltpu.make_async_remote_copy(src_ref=src, dst_ref=dst, send_sem=send_sems.at[k], recv_sem=recv_sems.at[k],
                                        device_id=to, device_id_type=MESH)


def _sibling_swap_halves(tag, grads, after=()):
    n = len(grads)
    after_ops, after_specs = _after_operands(after)

    def body(*refs):
        ins, outs = refs[:n], refs[n + len(after_ops):2 * n + len(after_ops)]
        send_sems, recv_sems = refs[2 * n + len(after_ops):]
        x, y, c, _ = _mesh_place()
        sibling = (x, y, 1 - c)
        copies = []
        for k in range(n):
            half = grads[k].shape[1] // 2
            theirs = pl.ds(pl.multiple_of((1 - c) * half, 16), half)
            cp = _remote(ins[k].at[:, theirs, :], outs[k], send_sems, recv_sems, k, sibling)
            cp.start()
            copies.append(cp)
        for cp in copies:
            cp.wait_recv()
        for cp in copies:
            cp.wait_send()

    hbm = BS(memory_space=pl.ANY)
    return pl.pallas_call(
        body, out_shape=tuple(SDS((g.shape[0], g.shape[1] // 2, g.shape[2]), g.dtype) for g in grads),
        in_specs=[hbm] * n + after_specs, out_specs=(hbm,) * n,
        scratch_shapes=[pltpu.SemaphoreType.DMA((n,)), pltpu.SemaphoreType.DMA((n,))],
        name="reduce_sibling_send_" + tag, compiler_params=_params())(*grads, *after_ops)


def _row_tile(rows, cap=512):
    return max(r for r in range(16, cap + 1, 16) if rows % r == 0)


def _chip_presum(k, grad, got, c_idx):
    n_sh, rows, cols = grad.shape
    half = rows // 2
    tr = _row_tile(half)
    grad4 = grad.reshape(n_sh, 2, half, cols)

    def body(c_ref, a_ref, b_ref, o_ref):
        o_ref[...] = (a_ref[...].astype(F32) + b_ref[...].astype(F32)).astype(o_ref.dtype)

    return pl.pallas_call(
        body, out_shape=SDS((n_sh, half, cols), BF16),
        grid_spec=pltpu.PrefetchScalarGridSpec(
            num_scalar_prefetch=1, grid=(n_sh, half // tr),
            in_specs=[BS((None, None, tr, cols), lambda s, i, c_ref: (s, c_ref[0], i, 0)),
                      BS((None, tr, cols), lambda s, i, c_ref: (s, i, 0))],
            out_specs=BS((None, tr, cols), lambda s, i, c_ref: (s, i, 0))),
        name=f"reduce_presum{k}", compiler_params=_params("parallel", "parallel"))(c_idx, grad4, got)


HBM_SPEC = BS(memory_space=pltpu.HBM)
SEM_SPEC = BS(memory_space=pltpu.SEMAPHORE)
DATAFLOW = pltpu.SideEffectType.DATAFLOW_SIDE_EFFECTING


def _chip_exchange_copies(parts, lands, send_sems, recv_sems):
    _, _, c, chips = _mesh_place()
    return [_remote(parts[k].at[2 * px + py], lands[k].at[j], send_sems, recv_sems, 3 * k + j, (px, py, c))
            for k in range(len(parts)) for j, (px, py) in enumerate(chips)]


def _gather_copies(shards, lands, send_sems, recv_sems):
    x, y, c, chips = _mesh_place()
    return [_remote(shards[k], lands[k].at[2 * x + y], send_sems, recv_sems, 3 * k + j, (px, py, c))
            for k in range(len(shards)) for j, (px, py) in enumerate(chips)]


def _gather_half_copies(shards, lands, send_sems, recv_sems):
    x, y, c, chips = _mesh_place()
    out = []
    for k in range(len(shards)):
        half = shards[k].shape[0] // 2
        mine = pl.ds(pl.multiple_of(c * half, 16), half)
        for j, (px, py) in enumerate(chips):
            out.append(_remote(shards[k].at[mine, :], lands[k].at[2 * x + y, mine, :], send_sems, recv_sems,
                               3 * k + j, (px, py, c)))
    return out


def _sibling_fill(tag, lands):
    n = len(lands)

    def body(*refs):
        outs = refs[n:2 * n]
        send_sems, recv_sems = refs[2 * n:]
        x, y, c, chips = _mesh_place()
        copies = []
        for k in range(n):
            half = lands[k].shape[1] // 2
            mine = pl.ds(pl.multiple_of(c * half, 16), half)
            for j, (px, py) in enumerate(chips):
                blk = outs[k].at[2 * px + py, mine, :]
                copies.append(_remote(blk, blk, send_sems, recv_sems, 3 * k + j, (x, y, 1 - c)))
        for cp in copies:
            cp.start()
        for cp in copies:
            cp.wait_recv()
        for cp in copies:
            cp.wait_send()

    hbm = BS(memory_space=pl.ANY)
    return list(pl.pallas_call(
        body, out_shape=tuple(SDS(a.shape, a.dtype) for a in lands),
        in_specs=[hbm] * n, out_specs=(hbm,) * n, input_output_aliases={k: k for k in range(n)},
        scratch_shapes=[pltpu.SemaphoreType.DMA((3 * n,)), pltpu.SemaphoreType.DMA((3 * n,))],
        name="gather_fill_" + tag, compiler_params=_params())(*lands))


def _swap_copies(grads, lands, send_sems, recv_sems):
    x, y, c, _ = _mesh_place()
    out = []
    for k in range(len(grads)):
        half = grads[k].shape[1] // 2
        theirs = pl.ds(pl.multiple_of((1 - c) * half, 16), half)
        out.append(_remote(grads[k].at[:, theirs, :], lands[k], send_sems, recv_sems, k, (x, y, 1 - c)))
    return out


def _join_copies(fulls, same, send_sems, recv_sems):
    x, y, c, _ = _mesh_place()
    out = []
    for k in range(len(fulls)):
        half = fulls[k].shape[0] // 2
        mine = fulls[k].at[pl.ds(pl.multiple_of(c * half, 8), half), :]
        out.append(_remote(mine, mine, send_sems, recv_sems, k, (x, y, 1 - c)))
    return out


def _everyone_copies(packs, lands, send_sems, recv_sems):
    x, y, c, _ = _mesh_place()
    out = []
    for k in range(len(packs)):
        for j in range(N_DEV - 1):
            bx, by, bc = (j + 1) >> 2 & 1, (j + 1) >> 1 & 1, (j + 1) & 1
            peer = (x ^ bx, y ^ by, c ^ bc)
            out.append(_remote(packs[k], lands[k].at[4 * x + 2 * y + c], send_sems, recv_sems, (N_DEV - 1) * k + j, peer))
    return out


def _split_start(name, copies, sources, land_shapes, after=(), fanout=3):
    n = len(sources)
    n_land = len(land_shapes)
    m = n + n_land
    n_sems = fanout * n
    after_ops, after_specs = _after_operands(after)

    def body(*refs):
        ins = refs[:n]
        lands = refs[n:m] if n_land else ins
        send_sems, recv_sems = refs[m + len(after_ops)], refs[m + len(after_ops) + 1]
        token = refs[-1]
        for cp in copies(ins, lands, send_sems, recv_sems):
            cp.start()
        token[...] = jnp.zeros_like(token)

    lands = [pltpu.with_memory_space_constraint(lax.empty(s, d), pltpu.HBM) for s, d in land_shapes]
    sources = [pltpu.with_memory_space_constraint(p, pltpu.HBM) for p in sources]
    thru = [pltpu.HBM(a.shape, a.dtype) for a in sources + lands]
    out = pl.pallas_call(
        body, name=name,
        out_shape=(pltpu.SemaphoreType.DMA((n_sems,)), pltpu.SemaphoreType.DMA((n_sems,)), *thru, SDS((8, 128), F32)),
        in_specs=[HBM_SPEC] * m + after_specs,
        out_specs=(SEM_SPEC, SEM_SPEC, *[HBM_SPEC] * m, BS(memory_space=pltpu.VMEM)),
        input_output_aliases={i: 2 + i for i in range(m)},
        compiler_params=pltpu.CompilerParams(has_side_effects=DATAFLOW))(*sources, *lands, *after_ops)
    return out[0], out[1], list(out[2:2 + n]), list(out[2 + n:2 + m]), out[-1]


def _split_wait(name, copies, send_sems, recv_sems, sources, lands, after):
    n = len(sources)
    m = n + len(lands)
    after_ops, after_specs = _after_operands(after)

    def body(*refs):
        ins = refs[:n]
        zones = refs[n:m] if m > n else ins
        for cp in copies(ins, zones, refs[m], refs[m + 1]):
            cp.wait_send()
            cp.wait_recv()

    out = pl.pallas_call(
        body, name=name,
        out_shape=tuple(pltpu.HBM(a.shape, a.dtype) for a in sources + lands),
        in_specs=[HBM_SPEC] * m + [SEM_SPEC, SEM_SPEC] + after_specs, out_specs=(HBM_SPEC,) * m,
        input_output_aliases={i: i for i in range(m)},
        compiler_params=pltpu.CompilerParams(has_side_effects=DATAFLOW))(*sources, *lands, send_sems, recv_sems, *after_ops)
    return list(out[:n]), list(out[n:])


class _WeightGatherer:
    def __init__(self, shards):
        self.shards, self.open = shards, {}
        self.me = 2 * lax.axis_index("x") + lax.axis_index("y")

    HALVED = ("f1a",)

    def start(self, tag, after=()):
        shapes = [((N_SHARD,) + s.shape, s.dtype) for s in self.shards[tag]]
        copies = _gather_half_copies if tag in self.HALVED else _gather_copies
        self.open[tag] = _split_start("gather_start_" + tag, copies, self.shards[tag], shapes, after)
        return [self.open[tag][-1]]

    def finish(self, tag, after):
        send_sems, recv_sems, shards, lands, _ = self.open.pop(tag)
        copies = _gather_half_copies if tag in self.HALVED else _gather_copies
        shards, lands = _split_wait("gather_wait_" + tag, copies, send_sems, recv_sems, shards, lands, after)
        if tag in self.HALVED:
            lands = _sibling_fill(tag, lands)
        return [lax.dynamic_update_slice(zone, s[None], (self.me, 0, 0)) for zone, s in zip(lands, shards)]


class _GradReducer:
    def __init__(self):
        self.c_idx = lax.axis_index("c").astype(jnp.int32).reshape(1)
        self.place = jnp.stack([2 * lax.axis_index("x") + lax.axis_index("y"), lax.axis_index("c")]).astype(jnp.int32)
        self.swaps, self.open, self.landed, self.joins, self.reduced = {}, {}, {}, {}, []

    def swap_start(self, tag, grads, after=()):
        shapes = [((g.shape[0], g.shape[1] // 2, g.shape[2]), g.dtype) for g in grads]
        self.swaps[tag] = _split_start("reduce_swap_start_" + tag, _swap_copies, grads, shapes, after, fanout=1)
        return [self.swaps[tag][-1]]

    def start(self, tag, grads, after=(), swapped=()):
        pairs = []
        for s in swapped:
            send_sems, recv_sems, early, lands, _ = self.swaps.pop(s)
            pairs += zip(*_split_wait("reduce_swap_wait_" + s, _swap_copies, send_sems, recv_sems, early, lands, grads[-1:]))
        pairs += zip(grads, _sibling_swap_halves(tag, grads, after))
        parts = [_chip_presum(f"{tag}{k}", g, s, self.c_idx) for k, (g, s) in enumerate(pairs)]
        shapes = [((3,) + p.shape[1:], p.dtype) for p in parts]
        self.open[tag] = _split_start("reduce_exchange_start_" + tag, _chip_exchange_copies, parts, shapes)
        return [self.open[tag][-1]]

    def finish(self, tag, after):
        send_sems, recv_sems, parts, lands, _ = self.open.pop(tag)
        self.landed[tag] = _split_wait("reduce_exchange_wait_" + tag, _chip_exchange_copies, send_sems, recv_sems, parts, lands, after)
        return self.landed[tag][1][:1]

    def _sums(self, tag, after=()):
        parts, landed = self.landed.pop(tag)
        return [_chip_sum(f"{tag}{k}", p, got, self.place, after) for k, (p, got) in enumerate(zip(parts, landed))]

    def join_start(self, tag, after=()):
        self.joins[tag] = _split_start("reduce_join_start_" + tag, _join_copies, self._sums(tag, after), [], fanout=1)
        return [self.joins[tag][-1]]

    def join_finish(self, tag, after):
        send_sems, recv_sems, fulls, _, _ = self.joins.pop(tag)
        self.reduced += _split_wait("reduce_join_wait_" + tag, _join_copies, send_sems, recv_sems, fulls, [], after)[0]

    def join(self, tag, after=()):
        self.reduced += _sibling_join_halves(self._sums(tag), after)


def _chip_sum(k, part, got, place, after=()):
    _, half, cols = part.shape
    tr = _row_tile(half)
    n_t = half // tr
    after_ops, after_specs = _after_operands(after)

    def body(place_ref, a_ref, b_ref, *rest):
        o_ref = rest[-1]
        acc = a_ref[...].astype(F32)
        for j in range(3):
            acc = acc + b_ref[j].astype(F32)
        o_ref[...] = acc

    return pl.pallas_call(
        body, out_shape=SDS((2 * half, cols), F32),
        grid_spec=pltpu.PrefetchScalarGridSpec(
            num_scalar_prefetch=1, grid=(n_t,),
            in_specs=[BS((None, tr, cols), lambda i, place_ref: (place_ref[0], i, 0)),
                      BS((3, tr, cols), lambda i, place_ref: (0, i, 0))] + after_specs,
            out_specs=BS((tr, cols), lambda i, place_ref: (place_ref[1] * n_t + i, 0))),
        name=f"reduce_sum{k}", compiler_params=_params("parallel"))(place, part, got, *after_ops)


def _sibling_join_halves(fulls, after=()):
    n = len(fulls)
    after_ops, after_specs = _after_operands(after)

    def body(*refs):
        outs = refs[n + len(after_ops):2 * n + len(after_ops)]
        send_sems, recv_sems = refs[2 * n + len(after_ops):]
        copies = _join_copies(outs, outs, send_sems, recv_sems)
        for cp in copies:
            cp.start()
        for cp in copies:
            cp.wait_recv()
        for cp in copies:
            cp.wait_send()

    hbm = BS(memory_space=pl.ANY)
    return list(pl.pallas_call(
        body, out_shape=tuple(SDS(f.shape, f.dtype) for f in fulls),
        in_specs=[hbm] * n + after_specs, out_specs=(hbm,) * n, input_output_aliases={k: k for k in range(n)},
        scratch_shapes=[pltpu.SemaphoreType.DMA((n,)), pltpu.SemaphoreType.DMA((n,))],
        name="reduce_sibling_join", compiler_params=_params())(*fulls, *after_ops))


N_DEV = 8


def _sum_devices(packs):
    _, rows, lanes = packs.shape

    def body(p_ref, o_ref):
        acc = p_ref[0]
        for dev in range(1, N_DEV):
            acc = acc + p_ref[dev]
        o_ref[...] = acc

    vm = BS(memory_space=pltpu.VMEM)
    return pl.pallas_call(body, out_shape=SDS((rows, lanes), F32), in_specs=[vm], out_specs=vm,
                          name="small_sum", compiler_params=_params())(packs)


def _adamw(name, w, grad, row0, m, v, after=()):
    rows, cols = w.shape
    tr = rows if rows < 16 else _row_tile(rows, 256)
    bc1 = 1.0 - ADAM_B1 ** ADAM_STEP
    bc2 = 1.0 - ADAM_B2 ** ADAM_STEP
    after_ops, after_specs = _after_operands(after)

    def body(w_ref, g_ref, m_ref, v_ref, *rest):
        go_ref, d_ref, mo_ref, vo_ref = rest[len(after_ops):]
        g = g_ref[...]
        m_new = ADAM_B1 * m_ref[...] + (1.0 - ADAM_B1) * g
        v_new = ADAM_B2 * v_ref[...] + (1.0 - ADAM_B2) * (g * g)
        go_ref[...] = g
        mo_ref[...] = m_new
        vo_ref[...] = v_new
        d_ref[...] = -ADAM_LR * ((m_new / bc1) / (jnp.sqrt(v_new / bc2) + ADAM_EPS) + ADAM_WD * w_ref[...])

    blk = BS((tr, cols), lambda i: (i, 0))
    shape = SDS((rows, cols), F32)
    return pl.pallas_call(
        body, out_shape=(shape,) * 4, grid=(rows // tr,),
        in_specs=[blk, BS((tr, cols), lambda i: (row0 // tr + i, 0)), blk, blk] + after_specs, out_specs=(blk,) * 4,
        name=name, compiler_params=_params("parallel"))(w, grad, m, v, *after_ops)


SMALL_LANES = 128


def _pack_small(parts):
    flat = jnp.concatenate([jnp.ravel(p) for p in parts])
    rows = -(-flat.shape[0] // (64 * SMALL_LANES)) * 64
    return jnp.pad(flat, (0, rows * SMALL_LANES - flat.shape[0])).reshape(rows, SMALL_LANES)


def _unpack_small(packed, like):
    flat = jnp.ravel(packed)
    out, at = [], 0
    for p in like:
        out.append(flat[at:at + p.size].reshape(p.shape))
        at += p.size
    return out


def kernel(x, mem, ffn1_norm, ffn1_w_gate, ffn1_w_up, ffn1_w_down, mix_norm, w_in, pool_w, pool_scale, w_pool_proj, ssm_a_re, ssm_a_im, ssm_log_dt, ssm_b_re, ssm_b_im, ssm_c_re, ssm_c_im, ssm_d, w_glu_val, w_glu_gate, w_mix_out, xattn_norm, mem_norm, w_q, w_kv, w_xo, ffn2_norm, ffn2_w_gate, ffn2_w_up, ffn2_w_down, final_norm, loss_target, m_ffn1_norm, m_ffn1_w_gate, m_ffn1_w_up, m_ffn1_w_down, m_mix_norm, m_w_in, m_pool_w, m_pool_scale, m_w_pool_proj, m_ssm_a_re, m_ssm_a_im, m_ssm_log_dt, m_ssm_b_re, m_ssm_b_im, m_ssm_c_re, m_ssm_c_im, m_ssm_d, m_w_glu_val, m_w_glu_gate, m_w_mix_out, m_xattn_norm, m_mem_norm, m_w_q, m_w_kv, m_w_xo, m_ffn2_norm, m_ffn2_w_gate, m_ffn2_w_up, m_ffn2_w_down, m_final_norm, v_ffn1_norm, v_ffn1_w_gate, v_ffn1_w_up, v_ffn1_w_down, v_mix_norm, v_w_in, v_pool_w, v_pool_scale, v_w_pool_proj, v_ssm_a_re, v_ssm_a_im, v_ssm_log_dt, v_ssm_b_re, v_ssm_b_im, v_ssm_c_re, v_ssm_c_im, v_ssm_d, v_w_glu_val, v_w_glu_gate, v_w_mix_out, v_xattn_norm, v_mem_norm, v_w_q, v_w_kv, v_w_xo, v_ffn2_norm, v_ffn2_w_gate, v_ffn2_w_up, v_ffn2_w_down, v_final_norm):
    given = dict(locals())
    w = {n: given[n] for n in WEIGHTS}
    m = {n: given["m_" + n] for n in WEIGHTS}
    v = {n: given["v_" + n] for n in WEIGHTS}

    def shard_view(a, n):
        return a[0].T if n in TRANSPOSED else a[0]

    def shard_unview(a, n):
        return (a.T if n in TRANSPOSED else a)[None]

    shards = {tag: [jnp.concatenate([shard_view(w[n], n).astype(BF16) for n in grp], axis=0) for grp in arrays]
              for tag, arrays in GATHER_PHASES.items()}
    reducer = _GradReducer()
    loss_part, grad_x, _, small = _device_step(x[0], mem[0], loss_target[0], _WeightGatherer(shards),
                                               {n: w[n] for n in SMALL}, reducer)
    loss = lax.psum(loss_part[0, 0], ("x", "y", "c"))

    pack = _pack_small([small[n] for n in SMALL])
    everyone = _split_start("small_start", _everyone_copies, [pack], [((N_DEV,) + pack.shape, F32)], fanout=N_DEV - 1)
    reducer.join("b", after=everyone[-1:])

    grads, delta, new_m, new_v = {}, {}, {}, {}
    big_done = []
    for grp, red in zip(REDUCE_GROUPS, reducer.reduced):
        row0 = 0
        for n in grp:
            w_n = shard_view(w[n], n)
            outs = _adamw("adamw_" + n, w_n, red, row0, shard_view(m[n], n), shard_view(v[n], n), after=everyone[-1:])
            grads[n], delta[n], new_m[n], new_v[n] = (shard_unview(o, n) for o in outs)
            big_done.append(outs[1])
            row0 += w_n.shape[0]

    send_sems, recv_sems, packs, landed, _ = everyone
    packs, landed = _split_wait("small_wait", _everyone_copies, send_sems, recv_sems, packs, landed, big_done)
    mine = 4 * lax.axis_index("x") + 2 * lax.axis_index("y") + lax.axis_index("c")
    summed = _sum_devices(lax.dynamic_update_slice(landed[0], packs[0][None], (mine, 0, 0)))
    g_small = dict(zip(SMALL, _unpack_small(summed, [w[n] for n in SMALL])))
    narrow = [n for n in SMALL if w[n].ndim > 3]
    dense = [n for n in SMALL if n not in narrow]
    for n in narrow:
        two_d = (-1, w[n].shape[-1])
        outs = _adamw("adamw_" + n, w[n].reshape(two_d), g_small[n].reshape(two_d), 0, m[n].reshape(two_d), v[n].reshape(two_d))
        grads[n], delta[n], new_m[n], new_v[n] = (o.reshape(w[n].shape) for o in outs)
    dense_like = [w[n] for n in dense]
    packed = _adamw("adamw_small", _pack_small(dense_like), _pack_small([g_small[n] for n in dense]), 0,
                    _pack_small([m[n] for n in dense]), _pack_small([v[n] for n in dense]))
    for out, store in zip(packed, (grads, delta, new_m, new_v)):
        for n, val in zip(dense, _unpack_small(out, dense_like)):
            store[n] = val

    return (loss, grad_x[None], *[grads[n] for n in WEIGHTS], *[delta[n] for n in WEIGHTS],
            *[new_m[n] for n in WEIGHTS], *[new_v[n] for n in WEIGHTS])
```

```python
import functools
import math

import jax
import jax.numpy as jnp
from jax import lax
from jax.experimental import pallas as pl
from jax.experimental.pallas import tpu as pltpu

F32 = jnp.float32
BF16 = jnp.bfloat16
SDS = jax.ShapeDtypeStruct
BS = pl.BlockSpec
MESH = pl.DeviceIdType.MESH

D_MODEL = 1024
D_FF = 2816
N_SHARD = 4
FF_SH = D_FF // N_SHARD
D_POOL = 512
POOL_WINDOWS = (2, 4, 8, 16)
POOL_GROUP = 128
D_SSM = 256
SSM_GROUPS = 16
SSM_GROUP = 16
SSM_STATE = 64
SSM_CH = SSM_GROUPS * SSM_STATE
N_HEADS = 4
HEAD_DIM = 256
EPS = 1e-6
ADAM_LR, ADAM_B1, ADAM_B2, ADAM_EPS, ADAM_WD, ADAM_STEP = 0.001, 0.9, 0.999, 1e-08, 0.01, 10

VMEM_LIMIT_V7X = 52 * 1024 * 1024
TM = 512

NN = (((1,), (0,)), ((), ()))
NT = (((1,), (1,)), ((), ()))
TN = (((0,), (0,)), ((), ()))


def _params(*sem):
    return pltpu.CompilerParams(dimension_semantics=sem if sem else None, vmem_limit_bytes=VMEM_LIMIT_V7X)


def _dot(a, b, dims=NN):
    return lax.dot_general(a.astype(BF16), b.astype(BF16), dims, preferred_element_type=F32)


def _sigmoid(v):
    return pl.reciprocal(1.0 + jnp.exp(-v), approx=True)


def _block_dims(spec):
    return tuple(d for d in spec.block_shape if d is not None)


def _after_operands(after):
    return list(after), [BS(memory_space=pl.ANY)] * len(after)


def _mm(name, pairs, *, grid, out_shape, out_spec, red_axis=None, extras=(), epilogue=None, after=()):
    n_pairs, n_extra = len(pairs), len(extras)
    n_red = grid[red_axis] if red_axis is not None else 1
    dims = [p[4] for p in pairs]

    def body(*refs):
        ab = refs[:2 * n_pairs]
        ex = refs[2 * n_pairs:2 * n_pairs + n_extra]
        o_ref = refs[2 * n_pairs + n_extra + len(after)]

        def partial():
            acc = None
            for p in range(n_pairs):
                t = _dot(ab[2 * p][...], ab[2 * p + 1][...], dims[p])
                acc = t if acc is None else acc + t
            return acc

        def finish(acc):
            res = epilogue(acc, *[e[...] for e in ex]) if epilogue is not None else acc
            o_ref[...] = res.astype(o_ref.dtype)

        if n_red == 1:
            finish(partial())
        else:
            acc_ref = refs[-1]
            k = pl.program_id(red_axis)

            @pl.when(k == 0)
            def _():
                acc_ref[...] = jnp.zeros_like(acc_ref)

            acc_ref[...] += partial()

            @pl.when(k == n_red - 1)
            def _():
                finish(acc_ref[...])

    operands, in_specs = [], []
    for a, a_spec, b, b_spec, _ in pairs:
        operands += [a, b]
        in_specs += [a_spec, b_spec]
    for e, e_spec in extras:
        operands.append(e)
        in_specs.append(e_spec)
    after_ops, after_specs = _after_operands(after)
    operands += after_ops
    in_specs += after_specs
    scratch = [pltpu.VMEM(_block_dims(out_spec), F32)] if n_red > 1 else []
    sem = tuple("arbitrary" if ax == red_axis else "parallel" for ax in range(len(grid)))
    return pl.pallas_call(body, out_shape=out_shape, grid=grid, in_specs=in_specs, out_specs=out_spec,
                          scratch_shapes=scratch, name=name, compiler_params=_params(*sem))(*operands)


def _rmsnorm(name, h, gain, tm, after=()):
    t, d = h.shape
    after_ops, after_specs = _after_operands(after)

    def body(h_ref, g_ref, *rest):
        u_ref = rest[-1]
        hv = h_ref[...]
        r = lax.rsqrt(jnp.mean(hv * hv, axis=-1, keepdims=True) + EPS)
        u_ref[...] = ((hv * r) * g_ref[...]).astype(u_ref.dtype)

    return pl.pallas_call(
        body, out_shape=SDS((t, d), BF16), grid=(t // tm,),
        in_specs=[BS((tm, d), lambda i: (i, 0)), BS((1, d), lambda i: (0, 0))] + after_specs,
        out_specs=BS((tm, d), lambda i: (i, 0)), name=name, compiler_params=_params("parallel"))(h, gain, *after_ops)


def _rmsnorm_bwd(name, h, gain, du, dh_in, tm):
    t, d = h.shape
    has_in = dh_in is not None

    def body(*refs):
        if has_in:
            h_ref, g_ref, du_ref, dhin_ref, dh_ref, dhb_ref, dg_ref = refs
        else:
            h_ref, g_ref, du_ref, dh_ref, dhb_ref, dg_ref = refs
        i = pl.program_id(0)
        hv = h_ref[...]
        r = lax.rsqrt(jnp.mean(hv * hv, axis=-1, keepdims=True) + EPS)
        n = hv * r
        duv = du_ref[...].astype(F32)
        dn = duv * g_ref[...]
        dh = r * (dn - n * jnp.mean(dn * n, axis=-1, keepdims=True))
        if has_in:
            dh = dhin_ref[...] + dh
        dh_ref[...] = dh
        dhb_ref[...] = dh.astype(BF16)

        @pl.when(i == 0)
        def _():
            dg_ref[...] = jnp.zeros_like(dg_ref)

        dg_ref[...] += jnp.sum(duv * n, axis=0, keepdims=True)

    row = BS((tm, d), lambda i: (i, 0))
    vec = BS((1, d), lambda i: (0, 0))
    operands = [h, gain, du] + ([dh_in] if has_in else [])
    in_specs = [row, vec, row] + ([row] if has_in else [])
    return pl.pallas_call(
        body, out_shape=(SDS((t, d), F32), SDS((t, d), BF16), SDS((1, d), F32)), grid=(t // tm,),
        in_specs=in_specs, out_specs=(row, row, vec), name=name, compiler_params=_params("arbitrary"))(*operands)


def _loss_head(h, gain, target, tm):
    t, d = h.shape

    def body(h_ref, g_ref, t_ref, loss_ref, dh_ref, dhb_ref, dg_ref):
        i = pl.program_id(0)
        hv = h_ref[...]
        g = g_ref[...]
        r = lax.rsqrt(jnp.mean(hv * hv, axis=-1, keepdims=True) + EPS)
        n = hv * r
        err = n * g - t_ref[...]
        dy = err * (1.0 / d)
        dn = dy * g
        dh = r * (dn - n * jnp.mean(dn * n, axis=-1, keepdims=True))
        dh_ref[...] = dh
        dhb_ref[...] = dh.astype(BF16)

        @pl.when(i == 0)
        def _():
            dg_ref[...] = jnp.zeros_like(dg_ref)
            loss_ref[...] = jnp.zeros_like(loss_ref)

        dg_ref[...] += jnp.sum(dy * n, axis=0, keepdims=True)
        part = 0.5 * jnp.sum(jnp.mean(err * err, axis=-1, keepdims=True), axis=0, keepdims=True)
        loss_ref[...] += jnp.broadcast_to(part, loss_ref.shape)

    row = BS((tm, d), lambda i: (i, 0))
    vec = BS((1, d), lambda i: (0, 0))
    return pl.pallas_call(
        body, out_shape=(SDS((1, 128), F32), SDS((t, d), F32), SDS((t, d), BF16), SDS((1, d), F32)),
        grid=(t // tm,), in_specs=[row, vec, row],
        out_specs=(BS((1, 128), lambda i: (0, 0)), row, row, vec),
        name="loss_head", compiler_params=_params("arbitrary"))(h, gain, target)


FFN_GATE, FFN_UP, FFN_DOWN = 0, 1, 2


def _ffn_up(name, u, w_f, tm, after=()):
    t, d = u.shape
    after_ops, after_specs = _after_operands(after)

    def body(u_ref, wg_ref, wu_ref, *rest):
        pg_ref, pu_ref, a_ref = rest[len(after_ops):]
        uv = u_ref[...]
        for s in range(N_SHARD):
            g = _dot(uv, wg_ref[s], NT)
            up = _dot(uv, wu_ref[s], NT)
            sg = _sigmoid(g)
            silu = g * sg
            a_ref[s] = (silu * up).astype(BF16)
            pu_ref[s] = (0.5 * silu).astype(BF16)
            pg_ref[s] = (0.5 * sg * (1.0 + g * (1.0 - sg)) * up).astype(BF16)

    hid = BS((N_SHARD, tm, FF_SH), lambda i: (0, i, 0))
    shape = SDS((N_SHARD, t, FF_SH), BF16)
    return pl.pallas_call(
        body, out_shape=(shape, shape, shape), grid=(t // tm,),
        in_specs=[BS((tm, d), lambda i: (i, 0)), _ffn_all_shards_spec(w_f["gate"][1]),
                  _ffn_all_shards_spec(w_f["up"][1])] + after_specs,
        out_specs=(hid, hid, hid), name=name,
        compiler_params=_params("parallel"))(u, w_f["gate"][0], w_f["up"][0], *after_ops)


def _ffn_all_shards_spec(block):
    return BS((N_SHARD, FF_SH, D_MODEL), lambda i: (0, block, 0))


def _ffn_down(name, a, w_f, resid, tm, after=()):
    t, d = resid.shape
    after_ops, after_specs = _after_operands(after)

    def body(a_ref, w_ref, res_ref, *rest):
        o_ref = rest[-1]
        acc = _dot(a_ref[0], w_ref[0])
        for s in range(1, N_SHARD):
            acc = acc + _dot(a_ref[s], w_ref[s])
        o_ref[...] = res_ref[...] + 0.5 * acc

    row = BS((tm, d), lambda i: (i, 0))
    return pl.pallas_call(
        body, out_shape=SDS((t, d), F32), grid=(t // tm,),
        in_specs=[BS((N_SHARD, tm, FF_SH), lambda i: (0, i, 0)), _ffn_all_shards_spec(w_f["down"][1]), row] + after_specs,
        out_specs=row, name=name, compiler_params=_params("parallel"))(a, w_f["down"][0], resid, *after_ops)


def _ffn_bwd_act(name, dh_b, w_f, pg, pu, tm, after=()):
    t, d = dh_b.shape
    after_ops, after_specs = _after_operands(after)

    def body(dh_ref, wd_ref, pg_ref, pu_ref, *rest):
        dg_ref, dup_ref = rest[len(after_ops):]
        dh = dh_ref[...]
        for s in range(N_SHARD):
            da = _dot(dh, wd_ref[s], NT)
            dg_ref[s] = (da * pg_ref[s].astype(F32)).astype(BF16)
            dup_ref[s] = (da * pu_ref[s].astype(F32)).astype(BF16)

    hid = BS((N_SHARD, tm, FF_SH), lambda i: (0, i, 0))
    shape = SDS((N_SHARD, t, FF_SH), BF16)
    return pl.pallas_call(
        body, out_shape=(shape, shape), grid=(t // tm,),
        in_specs=[BS((tm, d), lambda i: (i, 0)), _ffn_all_shards_spec(w_f["down"][1]), hid, hid] + after_specs,
        out_specs=(hid, hid), name=name,
        compiler_params=_params("parallel"))(dh_b, w_f["down"][0], pg, pu, *after_ops)


def _ffn_dw(name, u, dg, dup, a, dh_b, tm):
    t, d = u.shape
    n_t = t // tm

    def body(u_ref, dg_ref, dup_ref, a_ref, dh_ref, o_ref, acc):
        i = pl.program_id(1)

        @pl.when(i == 0)
        def _():
            acc[...] = jnp.zeros_like(acc)

        uv = u_ref[...]
        acc[FFN_GATE] += _dot(dg_ref[...], uv, TN)
        acc[FFN_UP] += _dot(dup_ref[...], uv, TN)
        acc[FFN_DOWN] += _dot(a_ref[...], dh_ref[...], TN)

        @pl.when(i == n_t - 1)
        def _():
            o_ref[FFN_GATE] = acc[FFN_GATE].astype(BF16)
            o_ref[FFN_UP] = acc[FFN_UP].astype(BF16)
            o_ref[FFN_DOWN] = (0.5 * acc[FFN_DOWN]).astype(BF16)

    hid = BS((None, tm, FF_SH), lambda s, i: (s, i, 0))
    row = BS((tm, d), lambda s, i: (i, 0))
    return pl.pallas_call(
        body, out_shape=SDS((N_SHARD, 3, FF_SH, d), BF16), grid=(N_SHARD, n_t),
        in_specs=[row, hid, hid, hid, row], out_specs=BS((None, 3, FF_SH, d), lambda s, i: (s, 0, 0, 0)),
        scratch_shapes=[pltpu.VMEM((3, FF_SH, d), F32)],
        name=name, compiler_params=_params("parallel", "arbitrary"))(u, dg, dup, a, dh_b)


def _norm_bwd_tile(i, du, h_ref, g_ref, dhin_ref, dh_ref, dhb_ref, dg_ref):
    hv = h_ref[...]
    r = lax.rsqrt(jnp.mean(hv * hv, axis=-1, keepdims=True) + EPS)
    n = hv * r
    dn = du * g_ref[...]
    dh = dhin_ref[...] + r * (dn - n * jnp.mean(dn * n, axis=-1, keepdims=True))
    dh_ref[...] = dh
    dhb_ref[...] = dh.astype(BF16)

    @pl.when(i == 0)
    def _():
        dg_ref[...] = jnp.zeros_like(dg_ref)

    dg_ref[...] += jnp.sum(du * n, axis=0, keepdims=True)


def _norm_bwd_specs(tm):
    row = BS((tm, D_MODEL), lambda i: (i, 0))
    vec = BS((1, D_MODEL), lambda i: (0, 0))
    return [row, vec, row], (row, row, vec)


def _norm_bwd_shapes(t):
    return SDS((t, D_MODEL), F32), SDS((t, D_MODEL), BF16), SDS((1, D_MODEL), F32)


def _ffn_dx(name, dg, dup, w_f, h, gain, dh_in, tm, after=()):
    t = dg.shape[1]
    tm = tm // 2
    after_ops, after_specs = _after_operands(after)

    def body(dg_ref, dup_ref, wg_ref, wu_ref, h_ref, g_ref, dhin_ref, *rest):
        acc = _dot(dg_ref[0], wg_ref[0]) + _dot(dup_ref[0], wu_ref[0])
        for s in range(1, N_SHARD):
            acc = acc + _dot(dg_ref[s], wg_ref[s]) + _dot(dup_ref[s], wu_ref[s])
        _norm_bwd_tile(pl.program_id(0), acc, h_ref, g_ref, dhin_ref, *rest[len(after_ops):])

    hid = BS((N_SHARD, tm, FF_SH), lambda i: (0, i, 0))
    norm_in, norm_out = _norm_bwd_specs(tm)
    return pl.pallas_call(
        body, out_shape=_norm_bwd_shapes(t), grid=(t // tm,),
        in_specs=[hid, hid, _ffn_all_shards_spec(w_f["gate"][1]), _ffn_all_shards_spec(w_f["up"][1])] + norm_in + after_specs,
        out_specs=norm_out, name=name,
        compiler_params=_params("arbitrary"))(dg, dup, w_f["gate"][0], w_f["up"][0], h, gain, dh_in, *after_ops)


def _mm_norm_bwd(name, a, b, dims, h, gain, dh_in, tm):
    t = a.shape[0]

    def body(a_ref, b_ref, h_ref, g_ref, dhin_ref, *outs):
        _norm_bwd_tile(pl.program_id(0), _dot(a_ref[...], b_ref[...], dims), h_ref, g_ref, dhin_ref, *outs)

    norm_in, norm_out = _norm_bwd_specs(tm)
    return pl.pallas_call(
        body, out_shape=_norm_bwd_shapes(t), grid=(t // tm,),
        in_specs=[BS((tm, a.shape[1]), lambda i: (i, 0)), BS(b.shape, lambda i: (0, 0))] + norm_in,
        out_specs=norm_out, name=name, compiler_params=_params("arbitrary"))(a, b, h, gain, dh_in)


def _plain_mm(name, a, b, dims, out_dtype, tm, resid=None, after=()):
    t = a.shape[0]
    n = b.shape[1] if dims == NN else b.shape[0]
    extras = [(resid, BS((tm, n), lambda i: (i, 0)))] if resid is not None else []
    epi = (lambda acc, res: res + acc) if resid is not None else None
    return _mm(name, [(a, BS((tm, a.shape[1]), lambda i: (i, 0)), b, BS(b.shape, lambda i: (0, 0)), dims)],
               grid=(t // tm,), out_shape=SDS((t, n), out_dtype), out_spec=BS((tm, n), lambda i: (i, 0)),
               extras=extras, epilogue=epi, after=after)


def _dw_mm(name, a, b, tm, out_dtype=BF16, after=()):
    t, k = a.shape
    n = b.shape[1]
    return _mm(name, [(a, BS((tm, k), lambda i: (i, 0)), b, BS((tm, n), lambda i: (i, 0)), TN)],
               grid=(t // tm,), red_axis=0, out_shape=SDS((k, n), out_dtype), out_spec=BS((k, n), lambda i: (0, 0)),
               after=after)


POOL_CHUNK = 256
POOL_HALO = 8


def _window_sum(v, width, lead):
    n = v.shape[0]
    s = v
    k = 1
    while k < width:
        s = s + pltpu.roll(s, n - k, 0)
        k *= 2
    return pltpu.roll(s, lead, 0) if lead else s


def _pool_count(base, left, right, t, shape):
    pos = base + lax.broadcasted_iota(jnp.int32, shape, 0)
    lo = jnp.maximum(pos - left, 0)
    hi = jnp.minimum(pos + right + 1, t)
    return (hi - lo).astype(F32)


def _pool_fwd(proj, pool_w, pool_scale):
    t = proj.shape[0]
    c, h = POOL_CHUNK, POOL_HALO
    n_chunks = t // c

    def body(proj_hbm, pw_ref, sc_ref, pooled_ref, mixed_ref, ms_ref, pad_ref, sem):
        cp = pltpu.make_async_copy(proj_hbm.at[:, pl.ds(0, D_POOL)], pad_ref.at[pl.ds(h, t), :], sem)
        cp.start()
        pad_ref[pl.ds(0, h), :] = jnp.zeros((h, D_POOL), F32)
        pad_ref[pl.ds(t + h, h), :] = jnp.zeros((h, D_POOL), F32)
        cp.wait()
        for g, width in enumerate(POOL_WINDOWS):
            left = width // 2
            right = width - 1 - left
            cols = slice(g * POOL_GROUP, (g + 1) * POOL_GROUP)
            wmat = pw_ref[g].astype(BF16)
            scale = sc_ref[:, cols]

            def chunk(ci, carry, left=left, right=right, width=width, cols=cols, wmat=wmat, scale=scale):
                base = pl.multiple_of(ci * c, c)
                v = pad_ref[pl.ds(base, c + 2 * h), cols]
                win = _window_sum(v, width, left)[h:h + c]
                cnt = _pool_count(base, left, right, t, (c, POOL_GROUP))
                pooled = (win / cnt - v[h:h + c]).astype(BF16)
                mixed = _dot(pooled, wmat)
                pooled_ref[pl.ds(base, c), cols] = pooled
                mixed_ref[pl.ds(base, c), cols] = mixed.astype(BF16)
                ms_ref[pl.ds(base, c), cols] = (mixed * scale).astype(BF16)
                return carry

            lax.fori_loop(0, n_chunks, chunk, 0)

    vm = BS(memory_space=pltpu.VMEM)
    shape = SDS((t, D_POOL), BF16)
    return pl.pallas_call(
        body, out_shape=(shape, shape, shape),
        in_specs=[BS(memory_space=pl.ANY), vm, vm], out_specs=(vm, vm, vm),
        scratch_shapes=[pltpu.VMEM((t + 2 * h, D_POOL), F32), pltpu.SemaphoreType.DMA],
        name="pool_fwd", compiler_params=_params())(proj, pool_w, pool_scale)


def _pool_bwd(d_ms, mixed, pooled, pool_w, pool_scale):
    t = d_ms.shape[0]
    c, h = POOL_CHUNK, POOL_HALO
    n_chunks = t // c

    def body(dms_ref, mixed_ref, pooled_ref, pw_ref, sc_ref, dp_ref, dsc_ref, dpw_ref, pad_ref):
        pad_ref[pl.ds(0, h), :] = jnp.zeros((h, D_POOL), F32)
        pad_ref[pl.ds(t + h, h), :] = jnp.zeros((h, D_POOL), F32)
        for g, width in enumerate(POOL_WINDOWS):
            left = width // 2
            right = width - 1 - left
            cols = slice(g * POOL_GROUP, (g + 1) * POOL_GROUP)
            wmat = pw_ref[g].astype(BF16)
            scale = sc_ref[:, cols]

            def first(ci, carry, left=left, right=right, cols=cols, wmat=wmat, scale=scale):
                dsc, dpw = carry
                base = pl.multiple_of(ci * c, c)
                dms = dms_ref[pl.ds(base, c), cols].astype(F32)
                dsc = dsc + jnp.sum(dms * mixed_ref[pl.ds(base, c), cols].astype(F32), axis=0, keepdims=True)
                dmix = (dms * scale).astype(BF16)
                dpw = dpw + _dot(pooled_ref[pl.ds(base, c), cols], dmix, TN)
                dpooled = _dot(dmix, wmat, NT)
                cnt = _pool_count(base, left, right, t, (c, POOL_GROUP))
                pad_ref[pl.ds(base + h, c), cols] = dpooled / cnt
                return dsc, dpw

            dsc, dpw = lax.fori_loop(0, n_chunks, first,
                                     (jnp.zeros((1, POOL_GROUP), F32), jnp.zeros((POOL_GROUP, POOL_GROUP), F32)))
            dsc_ref[:, cols] = dsc
            dpw_ref[g] = dpw

            def second(ci, carry, left=left, right=right, width=width, cols=cols):
                base = pl.multiple_of(ci * c, c)
                v = pad_ref[pl.ds(base, c + 2 * h), cols]
                win = _window_sum(v, width, right)[h:h + c]
                cnt = _pool_count(base, left, right, t, (c, POOL_GROUP))
                dp_ref[pl.ds(base, c), cols] = (win - v[h:h + c] * cnt).astype(BF16)
                return carry

            lax.fori_loop(0, n_chunks, second, 0)

    vm = BS(memory_space=pltpu.VMEM)
    return pl.pallas_call(
        body, out_shape=(SDS((t, D_POOL), BF16), SDS((1, D_POOL), F32), SDS((4, POOL_GROUP, POOL_GROUP), F32)),
        in_specs=[vm] * 5, out_specs=(vm, vm, vm),
        scratch_shapes=[pltpu.VMEM((t + 2 * h, D_POOL), F32)],
        name="pool_bwd", compiler_params=_params())(d_ms, mixed, pooled, pool_w, pool_scale)


def _ssm_disc(ar, ai, ldt, after=()):
    after_ops, after_specs = _after_operands(after)

    def body(ar_ref, ai_ref, ldt_ref, *rest):
        abr_ref, abi_ref, qr_ref, qi_ref = rest[len(after_ops):]
        a_r, a_i = ar_ref[...], ai_ref[...]
        dt = jnp.exp(ldt_ref[...])
        mag = jnp.exp(dt * a_r)
        ang = dt * a_i
        abr = mag * jnp.cos(ang)
        abi = mag * jnp.sin(ang)
        den = a_r * a_r + a_i * a_i
        nr = abr - 1.0
        abr_ref[...] = abr
        abi_ref[...] = abi
        qr_ref[...] = (nr * a_r + abi * a_i) / den
        qi_ref[...] = (abi * a_r - nr * a_i) / den

    vm = BS(memory_space=pltpu.VMEM)
    shape = SDS(ar.shape, F32)
    return pl.pallas_call(body, out_shape=(shape,) * 4, in_specs=[vm] * 3 + after_specs, out_specs=(vm,) * 4,
                          name="ssm_disc", compiler_params=_params())(ar, ai, ldt, *after_ops)


def _ssm_disc_bwd(ar, ai, ldt, d_abr, d_abi, d_qr, d_qi):
    def body(ar_ref, ai_ref, ldt_ref, gabr_ref, gabi_ref, gqr_ref, gqi_ref, dar_ref, dai_ref, dldt_ref):
        a_r, a_i = ar_ref[...], ai_ref[...]
        dt = jnp.exp(ldt_ref[...])
        mag = jnp.exp(dt * a_r)
        ang = dt * a_i
        cs, sn = jnp.cos(ang), jnp.sin(ang)
        abr, abi = mag * cs, mag * sn
        den = a_r * a_r + a_i * a_i
        nr = abr - 1.0
        qr = (nr * a_r + abi * a_i) / den
        qi = (abi * a_r - nr * a_i) / den
        gqr, gqi = gqr_ref[...], gqi_ref[...]
        g_nr_num = gqr / den
        g_ni_num = gqi / den
        g_den = -(gqr * qr + gqi * qi) / den
        g_nr = g_nr_num * a_r - g_ni_num * a_i
        g_abi = g_nr_num * a_i + g_ni_num * a_r
        d_ar = g_nr_num * nr + g_ni_num * abi + 2.0 * a_r * g_den
        d_ai = g_nr_num * abi - g_ni_num * nr + 2.0 * a_i * g_den
        g_abr = gabr_ref[...] + g_nr
        g_abi = gabi_ref[...] + g_abi
        g_mag = g_abr * cs + g_abi * sn
        g_ang = mag * (g_abi * cs - g_abr * sn)
        g_e = g_mag * mag
        d_ar = d_ar + g_e * dt
        d_ai = d_ai + g_ang * dt
        g_dt = g_e * a_r + g_ang * a_i
        dar_ref[...] = d_ar
        dai_ref[...] = d_ai
        dldt_ref[...] = jnp.sum(g_dt * dt, axis=1, keepdims=True)

    vm = BS(memory_space=pltpu.VMEM)
    return pl.pallas_call(body, out_shape=(SDS(ar.shape, F32), SDS(ar.shape, F32), SDS(ldt.shape, F32)),
                          in_specs=[vm] * 7, out_specs=(vm,) * 3, name="ssm_disc_bwd",
                          compiler_params=_params())(ar, ai, ldt, d_abr, d_abi, d_qr, d_qi)


def _ssm_bbar(qr, qi, br, bi):
    def body(qr_ref, qi_ref, br_ref, bi_ref, bbr_ref, bbi_ref):
        q_r, q_i, b_r, b_i = qr_ref[...], qi_ref[...], br_ref[...], bi_ref[...]
        bbr_ref[...] = q_r * b_r - q_i * b_i
        bbi_ref[...] = q_r * b_i + q_i * b_r

    vm = BS(memory_space=pltpu.VMEM)
    shape = SDS(br.shape, F32)
    return pl.pallas_call(body, out_shape=(shape, shape), in_specs=[vm] * 4, out_specs=(vm, vm),
                          name="ssm_bbar", compiler_params=_params())(qr, qi, br, bi)


def _ssm_bbar_bwd(qr, qi, br, bi, g_bbr, g_bbi):
    def body(qr_ref, qi_ref, br_ref, bi_ref, gr_ref, gi_ref, dqr_ref, dqi_ref, dbr_ref, dbi_ref):
        q_r, q_i, b_r, b_i = qr_ref[...], qi_ref[...], br_ref[...], bi_ref[...]
        g_r, g_i = gr_ref[...], gi_ref[...]
        dqr_ref[...] = jnp.sum(g_r * b_r + g_i * b_i, axis=1, keepdims=True)
        dqi_ref[...] = jnp.sum(g_i * b_r - g_r * b_i, axis=1, keepdims=True)
        dbr_ref[...] = g_r * q_r + g_i * q_i
        dbi_ref[...] = g_i * q_r - g_r * q_i

    vm = BS(memory_space=pltpu.VMEM)
    return pl.pallas_call(
        body, out_shape=(SDS(qr.shape, F32), SDS(qr.shape, F32), SDS(br.shape, F32), SDS(br.shape, F32)),
        in_specs=[vm] * 6, out_specs=(vm,) * 4, name="ssm_bbar_bwd",
        compiler_params=_params())(qr, qi, br, bi, g_bbr, g_bbi)


SCAN_ROWS = 512


def _ssm_scan(name, inp, w1, a_r, a_i, w2, reverse):
    t = inp.shape[0]
    rows = min(SCAN_ROWS, t)
    n = t // rows
    n_groups = rows // 8
    ch = SSM_CH
    at = (lambda i: (n - 1 - i, 0)) if reverse else (lambda i: (i, 0))

    def body(in_ref, w1_ref, ar_ref, ai_ref, w2_ref, sb_ref, out_ref, cr_ref, ci_ref, k_ref, st_ref):
        i = pl.program_id(0)

        @pl.when(i == 0)
        def _():
            ar8 = jnp.broadcast_to(ar_ref[...], (8, ch))
            ai8 = jnp.broadcast_to(ai_ref[...], (8, ch))
            row = lax.broadcasted_iota(jnp.int32, (8, ch), 0)
            rank = (7 - row) if reverse else row
            powers = [(ar8, ai8)]
            for _ in range(7):
                p_r, p_i = powers[-1]
                powers.append((p_r * ar8 - p_i * ai8, p_r * ai8 + p_i * ar8))
            zero = jnp.zeros((8, ch), F32)
            for slot, k in enumerate((1, 2, 4)):
                k_ref[2 * slot] = jnp.where(rank >= k, powers[k - 1][0], zero)
                k_ref[2 * slot + 1] = jnp.where(rank >= k, powers[k - 1][1], zero)
            carry_r, carry_i = zero, zero
            for j in range(8):
                carry_r = jnp.where(rank == j, powers[j][0], carry_r)
                carry_i = jnp.where(rank == j, powers[j][1], carry_i)
            k_ref[6] = carry_r
            k_ref[7] = carry_i
            cr_ref[...] = zero
            ci_ref[...] = zero

        st_ref[...] = _dot(in_ref[...], w1_ref[...])

        def group(gi, carry):
            c_r, c_i = carry
            g = (n_groups - 1 - gi) if reverse else gi
            r0 = pl.multiple_of(g * 8, 8)
            x_r = st_ref[pl.ds(r0, 8), 0:ch]
            x_i = st_ref[pl.ds(r0, 8), ch:2 * ch]
            for slot, k in enumerate((1, 2, 4)):
                shift = (8 - k) if reverse else k
                s_r = pltpu.roll(x_r, shift, 0)
                s_i = pltpu.roll(x_i, shift, 0)
                m_r, m_i = k_ref[2 * slot], k_ref[2 * slot + 1]
                x_r, x_i = x_r + m_r * s_r - m_i * s_i, x_i + m_r * s_i + m_i * s_r
            p_r, p_i = k_ref[6], k_ref[7]
            x_r, x_i = x_r + p_r * c_r - p_i * c_i, x_i + p_r * c_i + p_i * c_r
            st_ref[pl.ds(r0, 8), 0:ch] = x_r
            st_ref[pl.ds(r0, 8), ch:2 * ch] = x_i
            last = 0 if reverse else 7
            return (jnp.broadcast_to(x_r[last:last + 1, :], (8, ch)), jnp.broadcast_to(x_i[last:last + 1, :], (8, ch)))

        c_r, c_i = lax.fori_loop(0, n_groups, group, (cr_ref[...], ci_ref[...]))
        cr_ref[...] = c_r
        ci_ref[...] = c_i
        states = st_ref[...].astype(BF16)
        sb_ref[...] = states
        out_ref[...] = _dot(states, w2_ref[...])

    return pl.pallas_call(
        body, out_shape=(SDS((t, 2 * ch), BF16), SDS((t, D_SSM), F32)), grid=(n,),
        in_specs=[BS((rows, D_SSM), at), BS((D_SSM, 2 * ch), lambda i: (0, 0)), BS((1, ch), lambda i: (0, 0)),
                  BS((1, ch), lambda i: (0, 0)), BS((2 * ch, D_SSM), lambda i: (0, 0))],
        out_specs=(BS((rows, 2 * ch), at), BS((rows, D_SSM), at)),
        scratch_shapes=[pltpu.VMEM((8, ch), F32), pltpu.VMEM((8, ch), F32), pltpu.VMEM((8, 8, ch), F32),
                        pltpu.VMEM((rows, 2 * ch), F32)],
        name=name, compiler_params=_params("arbitrary"))(inp, w1, a_r, a_i, w2)


DA_ROWS = 1024


def _ssm_da(name, lam, states, reverse, after=()):
    t = lam.shape[0]
    rows = min(DA_ROWS, t)
    n = t // rows
    halo_rows = 16
    nb = rows // halo_rows
    ch = SSM_CH
    if reverse:
        halo_at = lambda i: (jnp.minimum((i + 1) * nb, t // halo_rows - 1), 0)
    else:
        halo_at = lambda i: (jnp.maximum(i * nb - 1, 0), 0)

    after_ops, after_specs = _after_operands(after)

    def body(lam_ref, x_ref, halo_ref, *rest):
        dr_ref, di_ref = rest[len(after_ops):]
        i = pl.program_id(0)

        @pl.when(i == 0)
        def _():
            dr_ref[...] = jnp.zeros_like(dr_ref)
            di_ref[...] = jnp.zeros_like(di_ref)

        row = lax.broadcasted_iota(jnp.int32, (rows, ch), 0)
        if reverse:
            edge, shift, h_row, live = rows - 1, rows - 1, 0, i < n - 1
        else:
            edge, shift, h_row, live = 0, 1, halo_rows - 1, i > 0

        def neighbour(lo):
            halo = halo_ref[:, lo:lo + ch].astype(F32)[h_row:h_row + 1]
            halo = jnp.where(live, halo, 0.0)
            x = x_ref[:, lo:lo + ch].astype(F32)
            return jnp.where(row == edge, jnp.broadcast_to(halo, (rows, ch)), pltpu.roll(x, shift, 0))

        xp_r, xp_i = neighbour(0), neighbour(ch)
        l_r, l_i = lam_ref[:, 0:ch].astype(F32), lam_ref[:, ch:2 * ch].astype(F32)
        dr_ref[...] += jnp.sum(l_r * xp_r + l_i * xp_i, axis=0, keepdims=True)
        di_ref[...] += jnp.sum(l_i * xp_r - l_r * xp_i, axis=0, keepdims=True)

    blk = BS((rows, 2 * ch), lambda i: (i, 0))
    vec = BS((1, ch), lambda i: (0, 0))
    return pl.pallas_call(
        body, out_shape=(SDS((1, ch), F32), SDS((1, ch), F32)), grid=(n,),
        in_specs=[blk, blk, BS((halo_rows, 2 * ch), halo_at)] + after_specs, out_specs=(vec, vec),
        name=name, compiler_params=_params("arbitrary"))(lam, states, states, *after_ops)


GELU_C = math.sqrt(2.0 / math.pi)
GELU_K = 0.044715


def _ssm_combine(proj, y_fwd, y_bwd, d_skip, tm, after=()):
    t = proj.shape[0]
    after_ops, after_specs = _after_operands(after)

    def body(s_ref, yf_ref, yb_ref, d_ref, *rest):
        yt_ref, g_ref = rest[len(after_ops):]
        y = s_ref[...] * d_ref[...] + yf_ref[...] + yb_ref[...]
        yt_ref[...] = y
        th = jnp.tanh(GELU_C * (y + GELU_K * y * y * y))
        g_ref[...] = (0.5 * y * (1.0 + th)).astype(BF16)

    blk = BS((tm, D_SSM), lambda i: (i, 0))
    return pl.pallas_call(
        body, out_shape=(SDS((t, D_SSM), F32), SDS((t, D_SSM), BF16)), grid=(t // tm,),
        in_specs=[BS((tm, D_SSM), lambda i: (i, D_POOL // D_SSM)), blk, blk, BS((1, D_SSM), lambda i: (0, 0))] + after_specs,
        out_specs=(blk, blk), name="ssm_combine",
        compiler_params=_params("parallel"))(proj, y_fwd, y_bwd, d_skip, *after_ops)


def _ssm_ds(proj, d_yt, du_fwd, du_bwd, d_skip, tm):
    t = proj.shape[0]

    def body(s_ref, dy_ref, duf_ref, dub_ref, d_ref, ds_ref, dd_ref):
        i = pl.program_id(0)
        dy = dy_ref[...]
        ds_ref[...] = (dy * d_ref[...] + duf_ref[...] + dub_ref[...]).astype(BF16)

        @pl.when(i == 0)
        def _():
            dd_ref[...] = jnp.zeros_like(dd_ref)

        dd_ref[...] += jnp.sum(dy * s_ref[...], axis=0, keepdims=True)

    blk = BS((tm, D_SSM), lambda i: (i, 0))
    vec = BS((1, D_SSM), lambda i: (0, 0))
    return pl.pallas_call(
        body, out_shape=(SDS((t, D_SSM), BF16), SDS((1, D_SSM), F32)), grid=(t // tm,),
        in_specs=[BS((tm, D_SSM), lambda i: (i, D_POOL // D_SSM)), blk, blk, blk, vec],
        out_specs=(blk, vec), name="ssm_ds", compiler_params=_params("arbitrary"))(proj, d_yt, du_fwd, du_bwd, d_skip)


GP_BLOCK = (D_POOL + D_SSM) // 256
GS_BLOCK = GP_BLOCK + D_MODEL // 256


def _merge_specs(tm):
    return [BS((tm, D_POOL), lambda s, i: (i, 0)), BS((tm, D_SSM), lambda s, i: (i, 0)),
            BS((None, D_POOL, 256), lambda s, i: (s, 0, 0)), BS((None, D_SSM, 256), lambda s, i: (s, 2, 0)),
            BS((None, D_SSM, 256), lambda s, i: (s, 3, 0)),
            BS((tm, 256), lambda s, i: (i, GP_BLOCK + s)), BS((tm, 256), lambda s, i: (i, GS_BLOCK + s))]


def _mixer_merge(ms, yssm, w_e, proj, tm):
    t = ms.shape[0]

    def body(ms_ref, y_ref, wpp_ref, wgv_ref, wgg_ref, gp_ref, gs_ref, o_ref):
        zp = _dot(ms_ref[...], wpp_ref[...])
        yv = y_ref[...]
        zv = _dot(yv, wgv_ref[...])
        zg = _dot(yv, wgg_ref[...])
        o_ref[...] = (_sigmoid(gp_ref[...]) * zp + _sigmoid(gs_ref[...]) * zv * _sigmoid(zg)).astype(BF16)

    col = BS((tm, 256), lambda s, i: (i, s))
    return pl.pallas_call(
        body, out_shape=SDS((t, D_MODEL), BF16), grid=(N_SHARD, t // tm), in_specs=_merge_specs(tm), out_specs=col,
        name="mixer_merge", compiler_params=_params("parallel", "parallel"))(ms, yssm, w_e, w_e, w_e, proj, proj)


def _mixer_merge_bwd(ms, yssm, w_e, proj, dmerged, tm):
    t = ms.shape[0]

    def body(ms_ref, y_ref, wpp_ref, wgv_ref, wgg_ref, gp_ref, gs_ref, dm_ref,
             dgp_ref, dgs_ref, dzp_ref, dzv_ref, dzg_ref):
        zp = _dot(ms_ref[...], wpp_ref[...])
        yv = y_ref[...]
        zv = _dot(yv, wgv_ref[...])
        zg = _dot(yv, wgg_ref[...])
        dm = dm_ref[...].astype(F32)
        sp, ss, sg = _sigmoid(gp_ref[...]), _sigmoid(gs_ref[...]), _sigmoid(zg)
        dgp_ref[...] = (dm * zp * sp * (1.0 - sp)).astype(BF16)
        dgs_ref[...] = (dm * zv * sg * ss * (1.0 - ss)).astype(BF16)
        dzp_ref[...] = (dm * sp).astype(BF16)
        dz = dm * ss
        dzv_ref[...] = (dz * sg).astype(BF16)
        dzg_ref[...] = (dz * zv * sg * (1.0 - sg)).astype(BF16)

    col = BS((tm, 256), lambda s, i: (i, s))
    shape = SDS((t, D_MODEL), BF16)
    return pl.pallas_call(
        body, out_shape=(shape,) * 5, grid=(N_SHARD, t // tm), in_specs=_merge_specs(tm) + [col],
        out_specs=(col,) * 5, name="mixer_merge_bwd",
        compiler_params=_params("parallel", "parallel"))(ms, yssm, w_e, w_e, w_e, proj, proj, dmerged)


def _mixer_dw(ms, yssm, dzp, dzv, dzg, tm):
    t = ms.shape[0]
    n_t = t // tm

    def body(ms_ref, y_ref, dzp_ref, dzv_ref, dzg_ref, o_ref, acc):
        i = pl.program_id(1)

        @pl.when(i == 0)
        def _():
            acc[...] = jnp.zeros_like(acc)

        yv = y_ref[...]
        acc[0:D_POOL, :] += _dot(ms_ref[...], dzp_ref[...], TN)
        acc[D_POOL:D_POOL + D_SSM, :] += _dot(yv, dzv_ref[...], TN)
        acc[D_POOL + D_SSM:, :] += _dot(yv, dzg_ref[...], TN)

        @pl.when(i == n_t - 1)
        def _():
            o_ref[...] = acc[...].astype(BF16)

    col = BS((tm, 256), lambda s, i: (i, s))
    return pl.pallas_call(
        body, out_shape=SDS((N_SHARD, 1024, 256), BF16), grid=(N_SHARD, n_t),
        in_specs=[BS((tm, D_POOL), lambda s, i: (i, 0)), BS((tm, D_SSM), lambda s, i: (i, 0)), col, col, col],
        out_specs=BS((None, 1024, 256), lambda s, i: (s, 0, 0)), scratch_shapes=[pltpu.VMEM((1024, 256), F32)],
        name="mixer_dw", compiler_params=_params("parallel", "arbitrary"))(ms, yssm, dzp, dzv, dzg)


def _mixer_dx(dzp, dzv, dzg, w_e, y_total, tm):
    t = dzp.shape[0]

    def body(dzp_ref, dzv_ref, dzg_ref, wpp_ref, wgv_ref, wgg_ref, yt_ref, dms_ref, dy_ref, acc_ms, acc_y):
        s = pl.program_id(1)

        @pl.when(s == 0)
        def _():
            acc_ms[...] = jnp.zeros_like(acc_ms)
            acc_y[...] = jnp.zeros_like(acc_y)

        acc_ms[...] += _dot(dzp_ref[...], wpp_ref[...], NT)
        acc_y[...] += _dot(dzv_ref[...], wgv_ref[...], NT) + _dot(dzg_ref[...], wgg_ref[...], NT)

        @pl.when(s == N_SHARD - 1)
        def _():
            dms_ref[...] = acc_ms[...].astype(BF16)
            y = yt_ref[...]
            inner = GELU_C * (y + GELU_K * y * y * y)
            th = jnp.tanh(inner)
            dgelu = 0.5 * (1.0 + th) + 0.5 * y * (1.0 - th * th) * GELU_C * (1.0 + 3.0 * GELU_K * y * y)
            dy_ref[...] = acc_y[...] * dgelu

    col = BS((tm, 256), lambda i, s: (i, s))
    return pl.pallas_call(
        body, out_shape=(SDS((t, D_POOL), BF16), SDS((t, D_SSM), F32)), grid=(t // tm, N_SHARD),
        in_specs=[col, col, col, BS((None, D_POOL, 256), lambda i, s: (s, 0, 0)),
                  BS((None, D_SSM, 256), lambda i, s: (s, 2, 0)), BS((None, D_SSM, 256), lambda i, s: (s, 3, 0)),
                  BS((tm, D_SSM), lambda i, s: (i, 0))],
        out_specs=(BS((tm, D_POOL), lambda i, s: (i, 0)), BS((tm, D_SSM), lambda i, s: (i, 0))),
        scratch_shapes=[pltpu.VMEM((tm, D_POOL), F32), pltpu.VMEM((tm, D_SSM), F32)],
        name="mixer_dx", compiler_params=_params("parallel", "arbitrary"))(dzp, dzv, dzg, w_e, w_e, w_e, y_total)


def _attn_probs(q_h, k_h):
    s = _dot(q_h, k_h, NT) * (1.0 / math.sqrt(HEAD_DIM))
    e = jnp.exp(s - jnp.max(s, axis=-1, keepdims=True))
    return e / jnp.sum(e, axis=-1, keepdims=True)


def _attn_fwd(q, kv, tm):
    t = q.shape[0]
    m = kv.shape[0]

    def body(q_ref, kv_ref, o_ref):
        for hd in range(N_HEADS):
            lo = hd * HEAD_DIM
            p = _attn_probs(q_ref[:, lo:lo + HEAD_DIM], kv_ref[:, lo:lo + HEAD_DIM])
            o_ref[:, lo:lo + HEAD_DIM] = _dot(p, kv_ref[:, D_MODEL + lo:D_MODEL + lo + HEAD_DIM]).astype(BF16)

    return pl.pallas_call(
        body, out_shape=SDS((t, D_MODEL), BF16), grid=(t // tm,),
        in_specs=[BS((tm, D_MODEL), lambda i: (i, 0)), BS((m, 2 * D_MODEL), lambda i: (0, 0))],
        out_specs=BS((tm, D_MODEL), lambda i: (i, 0)), name="attn_fwd", compiler_params=_params("parallel"))(q, kv)


def _attn_bwd(q, kv, d_o, tm):
    t = q.shape[0]
    m = kv.shape[0]

    def body(q_ref, kv_ref, do_ref, dq_ref, dkv_ref):
        i = pl.program_id(0)

        @pl.when(i == 0)
        def _():
            dkv_ref[...] = jnp.zeros_like(dkv_ref)

        for hd in range(N_HEADS):
            lo = hd * HEAD_DIM
            q_h = q_ref[:, lo:lo + HEAD_DIM]
            k_h = kv_ref[:, lo:lo + HEAD_DIM]
            v_h = kv_ref[:, D_MODEL + lo:D_MODEL + lo + HEAD_DIM]
            do_h = do_ref[:, lo:lo + HEAD_DIM]
            p = _attn_probs(q_h, k_h)
            dkv_ref[:, D_MODEL + lo:D_MODEL + lo + HEAD_DIM] += _dot(p, do_h, TN)
            dp = _dot(do_h, v_h, NT)
            ds = p * (dp - jnp.sum(dp * p, axis=-1, keepdims=True)) * (1.0 / math.sqrt(HEAD_DIM))
            dq_ref[:, lo:lo + HEAD_DIM] = _dot(ds, k_h).astype(BF16)
            dkv_ref[:, lo:lo + HEAD_DIM] += _dot(ds, q_h, TN)

    row = BS((tm, D_MODEL), lambda i: (i, 0))
    full = BS((m, 2 * D_MODEL), lambda i: (0, 0))
    return pl.pallas_call(
        body, out_shape=(SDS((t, D_MODEL), BF16), SDS((m, 2 * D_MODEL), F32)), grid=(t // tm,),
        in_specs=[row, full, row], out_specs=(row, full), name="attn_bwd",
        compiler_params=_params("arbitrary"))(q, kv, d_o)


TRANSPOSED = ("ffn1_w_gate", "ffn1_w_up", "ffn2_w_gate", "ffn2_w_up", "w_in")
GATHER_PHASES = {"f1a": (("ffn1_w_gate", "ffn1_w_up"),),
                 "f1b": (("ffn1_w_down",),),
                 "win": (("w_in",),),
                 "mix": (("w_mix_out", "w_q", "w_xo"), ("w_kv",), ("w_pool_proj", "w_glu_val", "w_glu_gate")),
                 "f2": (("ffn2_w_gate", "ffn2_w_up", "ffn2_w_down"),)}
REDUCE_GROUPS = (("ffn2_w_gate", "ffn2_w_up", "ffn2_w_down"), ("w_xo",), ("w_q",), ("w_kv",), ("w_mix_out",),
                 ("w_pool_proj", "w_glu_val", "w_glu_gate"), ("w_in",), ("ffn1_w_gate", "ffn1_w_up", "ffn1_w_down"))
SMALL = ("ffn1_norm", "mix_norm", "pool_w", "pool_scale", "ssm_a_re", "ssm_a_im", "ssm_log_dt", "ssm_b_re",
         "ssm_b_im", "ssm_c_re", "ssm_c_im", "ssm_d", "xattn_norm", "mem_norm", "ffn2_norm", "final_norm")
WEIGHTS = ("ffn1_norm", "ffn1_w_gate", "ffn1_w_up", "ffn1_w_down", "mix_norm", "w_in", "pool_w", "pool_scale",
           "w_pool_proj", "ssm_a_re", "ssm_a_im", "ssm_log_dt", "ssm_b_re", "ssm_b_im", "ssm_c_re", "ssm_c_im",
           "ssm_d", "w_glu_val", "w_glu_gate", "w_mix_out", "xattn_norm", "mem_norm", "w_q", "w_kv", "w_xo",
           "ffn2_norm", "ffn2_w_gate", "ffn2_w_up", "ffn2_w_down", "final_norm")


def _block_diag_in(bb):
    eye = jnp.eye(SSM_GROUPS, dtype=bb.dtype)
    return jnp.einsum("dgph,gk->dghkp", bb, eye).reshape(2, D_SSM, SSM_CH)


def _block_diag_out(cc):
    eye = jnp.eye(SSM_GROUPS, dtype=cc.dtype)
    return jnp.einsum("dghp,gk->dgpkh", cc, eye).reshape(2, SSM_CH, D_SSM)


def _diag_blocks_in(m):
    return jnp.einsum("dghgp->dgph", m.reshape(2, SSM_GROUPS, SSM_GROUP, SSM_GROUPS, SSM_STATE))


def _diag_blocks_out(m):
    return jnp.einsum("dgpgh->dghp", m.reshape(2, SSM_GROUPS, SSM_STATE, SSM_GROUPS, SSM_GROUP))


def _device_step(x, mem, target, wts, sp, reducer=None):
    t = x.shape[0]
    tm = min(TM, t)
    g = {}

    first_gather = wts.start("f1a")
    u1 = _rmsnorm("norm_ffn1", x, sp["ffn1_norm"], tm, after=first_gather)

    ar = sp["ssm_a_re"].reshape(2 * SSM_GROUPS, SSM_STATE)
    ai = sp["ssm_a_im"].reshape(2 * SSM_GROUPS, SSM_STATE)
    ldt = sp["ssm_log_dt"].reshape(2 * SSM_GROUPS, 1)
    abr, abi, qr, qi = _ssm_disc(ar, ai, ldt, after=first_gather)
    b_r = sp["ssm_b_re"].reshape(2 * SSM_CH, SSM_GROUP)
    b_i = sp["ssm_b_im"].reshape(2 * SSM_CH, SSM_GROUP)
    qr_col, qi_col = qr.reshape(2 * SSM_CH, 1), qi.reshape(2 * SSM_CH, 1)
    bbr, bbi = _ssm_bbar(qr_col, qi_col, b_r, b_i)
    shape_b = (2, SSM_GROUPS, SSM_STATE, SSM_GROUP)
    b_mat = jnp.concatenate([_block_diag_in(bbr.reshape(shape_b)), _block_diag_in(bbi.reshape(shape_b))], axis=-1).astype(BF16)
    c_mat = jnp.concatenate([_block_diag_out(sp["ssm_c_re"][0]), -_block_diag_out(sp["ssm_c_im"][0])], axis=1).astype(BF16)
    b_mat_t = jnp.swapaxes(b_mat, 1, 2)
    c_mat_t = jnp.swapaxes(c_mat, 1, 2)
    a_r = abr.reshape(2, 1, SSM_CH)
    a_i = abi.reshape(2, 1, SSM_CH)
    mem_n = _rmsnorm("norm_mem", mem, sp["mem_norm"], mem.shape[0], after=first_gather)

    (w_gu,) = wts.finish("f1a", [u1, b_mat, c_mat, b_mat_t, c_mat_t, mem_n])
    w_f1 = {"gate": (w_gu, FFN_GATE), "up": (w_gu, FFN_UP)}
    down_gather = wts.start("f1b", [w_gu])
    g1, up1, a1 = _ffn_up("ffn1_up", u1, w_f1, tm, after=wts.start("win", down_gather))
    (w_dn,) = wts.finish("f1b", [a1])
    w_f1["down"] = (w_dn, 0)
    h1 = _ffn_down("ffn1_down", a1, w_f1, x, tm)

    u2 = _rmsnorm("norm_mix", h1, sp["mix_norm"], tm)
    (w_in_g,) = wts.finish("win", [u2])
    w_in_t = w_in_g.reshape(D_FF, D_MODEL)
    proj = _mm("mix_in", [(u2, BS((tm, D_MODEL), lambda j, i: (i, 0)), w_in_t, BS((D_FF // 2, D_MODEL), lambda j, i: (j, 0)), NT)],
               grid=(2, t // tm), out_shape=SDS((t, D_FF), F32), out_spec=BS((tm, D_FF // 2), lambda j, i: (i, j)),
               after=wts.start("f2", wts.start("mix", [w_in_g])))
    pooled, mixed, ms = _pool_fwd(proj, sp["pool_w"][0], sp["pool_scale"])

    s_in = proj[:, D_POOL:D_POOL + D_SSM].astype(BF16)
    states, y_dirs = [], []
    for dr in range(2):
        st, yd = _ssm_scan(f"ssm_scan_fwd{dr}", s_in, b_mat[dr], a_r[dr], a_i[dr], c_mat[dr], reverse=(dr == 1))
        states.append(st)
        y_dirs.append(yd)
    w_sq, w_kv, w_e = wts.finish("mix", y_dirs)
    w_mo, w_q, w_xo = (w_sq[:, 256 * k:256 * (k + 1)].reshape(D_MODEL, D_MODEL) for k in range(3))
    w_d = w_kv[:, None]
    y_total, yssm = _ssm_combine(proj, y_dirs[0], y_dirs[1], sp["ssm_d"], tm)

    merged = _mixer_merge(ms, yssm, w_e, proj, tm)
    h2 = _plain_mm("mix_out", merged, w_mo, NN, F32, tm, resid=h1)

    u3 = _rmsnorm("norm_xattn", h2, sp["xattn_norm"], tm)
    q =_plain_mm("attn_q", u3, w_q, NN, BF16, tm)
    n_mem = mem.shape[0]
    kv = _mm("attn_kv", [(mem_n, BS((n_mem, D_MODEL), lambda s: (0, 0)), w_d, BS((None, None, D_MODEL, 512), lambda s: (s, 0, 0, 0)), NN)],
             grid=(N_SHARD,), out_shape=SDS((n_mem, 2 * D_MODEL), BF16), out_spec=BS((n_mem, 512), lambda s: (0, s)))
    o = _attn_fwd(q, kv, tm)
    h3 = _plain_mm("attn_out", o, w_xo, NN, F32, tm, resid=h2)

    u4 = _rmsnorm("norm_ffn2", h3, sp["ffn2_norm"], tm)
    (w_2,) = wts.finish("f2", [u4])
    w_f2 = {"gate": (w_2, FFN_GATE), "up": (w_2, FFN_UP), "down": (w_2, FFN_DOWN)}
    g2, up2, a2 = _ffn_up("ffn2_up", u4, w_f2, tm)
    h4 = _ffn_down("ffn2_down", a2, w_f2, h3, tm)

    loss, dh4, dh4_b, g["final_norm"] = _loss_head(h4, sp["final_norm"].reshape(1, D_MODEL), target, tm)

    dg2, dup2 = _ffn_bwd_act("ffn2_bwd_act", dh4_b, w_f2, g2, up2, tm)
    dw_f2 = _ffn_dw("ffn2_dw", u4, dg2, dup2, a2, dh4_b, tm)
    dh3, dh3_b, g["ffn2_norm"] = _ffn_dx("ffn2_dx", dg2, dup2, w_f2, h3, sp["ffn2_norm"], dh4, tm)

    d_o = _plain_mm("attn_out_dx", dh3_b, w_xo, NT, BF16, tm)
    dw_xo = _dw_mm("attn_out_dw", o, dh3_b, tm)
    dq, dkv = _attn_bwd(q, kv, d_o, tm)
    dw_q = _dw_mm("attn_q_dw", u3, dq, tm)
    dh2, dh2_b, g["xattn_norm"] = _mm_norm_bwd("attn_q_dx", dq, w_q, NT, h2, sp["xattn_norm"], dh3, tm)
    dw_kv = _mm("attn_kv_dw", [(mem_n, BS((n_mem, D_MODEL), lambda s: (0, 0)), dkv, BS((n_mem, 512), lambda s: (0, s)), TN)],
                grid=(N_SHARD,), out_shape=SDS((N_SHARD, D_MODEL, 512), BF16), out_spec=BS((None, D_MODEL, 512), lambda s: (s, 0, 0)))
    dmem_n = _mm("attn_kv_dx", [(dkv, BS((n_mem, 512), lambda s: (0, s)), w_d, BS((None, None, D_MODEL, 512), lambda s: (s, 0, 0, 0)), NT)],
                 grid=(N_SHARD,), red_axis=0, out_shape=SDS((n_mem, D_MODEL), F32), out_spec=BS((n_mem, D_MODEL), lambda s: (0, 0)))
    _, _, g["mem_norm"] = _rmsnorm_bwd("norm_mem_bwd", mem, sp["mem_norm"], dmem_n, None, n_mem)

    square = (N_SHARD, D_MODEL // N_SHARD, D_MODEL)
    early = [dw_f2.reshape(N_SHARD, 3 * FF_SH, D_MODEL), dw_xo.reshape(square), dw_q.reshape(square), dw_kv]
    swapping = reducer.swap_start("a1", early) if reducer is not None else []
    dmerged = _plain_mm("mix_out_dx", dh2_b, w_mo, NT, BF16, tm, after=swapping)
    dw_mo = _dw_mm("mix_out_dw", merged, dh2_b, tm)
    d_gp, d_gs, dzp, dzv, dzg = _mixer_merge_bwd(ms, yssm, w_e, proj, dmerged, tm)
    dw_e = _mixer_dw(ms, yssm, dzp, dzv, dzg, tm)
    d_ms, d_yt = _mixer_dx(dzp, dzv, dzg, w_e, y_total, tm)
    dp, d_scale, d_pw = _pool_bwd(d_ms, mixed, pooled, sp["pool_w"][0], sp["pool_scale"])
    g["pool_scale"] = d_scale
    g["pool_w"] = d_pw[None]

    d_yt_b = d_yt.astype(BF16)
    du_dirs, lams = [], []
    for dr in range(2):
        lam, du = _ssm_scan(f"ssm_scan_bwd{dr}", d_yt_b, c_mat_t[dr], a_r[dr], -a_i[dr], b_mat_t[dr], reverse=(dr == 0))
        du_dirs.append(du)
        lams.append(lam)
    ds, g["ssm_d"] = _ssm_ds(proj, d_yt, du_dirs[0], du_dirs[1], sp["ssm_d"], tm)

    d_proj = jnp.concatenate([dp, ds, d_gp, d_gs], axis=1)
    dw_in_t = _mm("mix_in_dw", [(d_proj, BS((tm, D_FF // 2), lambda j, i: (i, j)), u2, BS((tm, D_MODEL), lambda j, i: (i, 0)), TN)],
                  grid=(2, t // tm), red_axis=1, out_shape=SDS((D_FF, D_MODEL), BF16), out_spec=BS((D_FF // 2, D_MODEL), lambda j, i: (j, 0)))
    dh1, dh1_b, g["mix_norm"] = _mm_norm_bwd("mix_in_dx", d_proj, w_in_t, NN, h1, sp["mix_norm"], dh2, tm)

    early += [dw_mo.reshape(square), dw_e, dw_in_t.reshape(N_SHARD, FF_SH, D_MODEL)]
    g["final_norm"] = g["final_norm"].reshape(D_MODEL)

    travelling = reducer.start("a", early[4:], swapped=["a1"], after=list(g.values())) if reducer is not None else []
    d_abr, d_abi, d_cm, d_bm = [], [], [], []
    for dr in range(2):
        da_r, da_i = _ssm_da(f"ssm_da{dr}", lams[dr], states[dr], reverse=(dr == 1), after=travelling)
        d_abr.append(da_r)
        d_abi.append(da_i)
        d_cm.append(_dw_mm(f"ssm_dc{dr}", states[dr], d_yt_b, tm, F32, after=travelling))
        d_bm.append(_dw_mm(f"ssm_db{dr}", s_in, lams[dr], tm, F32, after=travelling))
    d_cm = jnp.stack(d_cm)
    d_bm = jnp.stack(d_bm)
    g["ssm_c_re"] = _diag_blocks_out(d_cm[:, :SSM_CH])[None]
    g["ssm_c_im"] = -_diag_blocks_out(d_cm[:, SSM_CH:])[None]
    g_bbr = _diag_blocks_in(d_bm[:, :, :SSM_CH]).reshape(2 * SSM_CH, SSM_GROUP)
    g_bbi = _diag_blocks_in(d_bm[:, :, SSM_CH:]).reshape(2 * SSM_CH, SSM_GROUP)
    d_qr, d_qi, d_br, d_bi = _ssm_bbar_bwd(qr_col, qi_col, b_r, b_i, g_bbr, g_bbi)
    g["ssm_b_re"] = d_br.reshape(sp["ssm_b_re"].shape)
    g["ssm_b_im"] = d_bi.reshape(sp["ssm_b_im"].shape)
    d_ar, d_ai, d_ldt = _ssm_disc_bwd(ar, ai, ldt, jnp.stack(d_abr).reshape(ar.shape), jnp.stack(d_abi).reshape(ar.shape),
                                      d_qr.reshape(ar.shape), d_qi.reshape(ar.shape))
    g["ssm_a_re"] = d_ar.reshape(sp["ssm_a_re"].shape)
    g["ssm_a_im"] = d_ai.reshape(sp["ssm_a_im"].shape)
    g["ssm_log_dt"] = d_ldt.reshape(sp["ssm_log_dt"].shape)
    if reducer is not None:
        travelling = travelling + [d_ar, d_ai, d_ldt, d_br, d_bi, d_cm]
    dg1, dup1 = _ffn_bwd_act("ffn1_bwd_act", dh1_b, w_f1, g1, up1, tm, after=travelling)
    dw_f1 = _ffn_dw("ffn1_dw", u1, dg1, dup1, a1, dh1_b, tm).reshape(N_SHARD, 3 * FF_SH, D_MODEL)
    if reducer is not None:
        travelling = reducer.start("b", [dw_f1], after=reducer.finish("a", [dw_f1]))
        travelling = travelling + reducer.join_start("a", after=travelling)
    grad_x, _, g["ffn1_norm"] = _ffn_dx("ffn1_dx", dg1, dup1, w_f1, x, sp["ffn1_norm"], dh1, tm, after=travelling)
    if reducer is not None:
        reducer.finish("b", [grad_x])
        reducer.join_finish("a", [grad_x])
    return loss, grad_x, early + [dw_f1], g


def _mesh_place():
    x, y, c = lax.axis_index("x"), lax.axis_index("y"), lax.axis_index("c")
    chips = [(1 - x, y), (x, 1 - y), (1 - x, 1 - y)]
    return x, y, c, chips


def _remote(src, dst, send_sems, recv_sems, k, to):
    return pltpu.make_async_remote_copy(src_ref=src, dst_ref=dst, send_sem=send_sems.at[k], recv_sem=recv_sems.at[k],
                                        device_id=to, device_id_type=MESH)


def _sibling_swap_halves(tag, grads, after=()):
    n = len(grads)
    after_ops, after_specs = _after_operands(after)

    def body(*refs):
        ins, outs = refs[:n], refs[n + len(after_ops):2 * n + len(after_ops)]
        send_sems, recv_sems = refs[2 * n + len(after_ops):]
        x, y, c, _ = _mesh_place()
        sibling = (x, y, 1 - c)
        copies = []
        for k in range(n):
            half = grads[k].shape[1] // 2
            theirs = pl.ds(pl.multiple_of((1 - c) * half, 16), half)
            cp = _remote(ins[k].at[:, theirs, :], outs[k], send_sems, recv_sems, k, sibling)
            cp.start()
            copies.append(cp)
        for cp in copies:
            cp.wait_recv()
        for cp in copies:
            cp.wait_send()

    hbm = BS(memory_space=pl.ANY)
    return pl.pallas_call(
        body, out_shape=tuple(SDS((g.shape[0], g.shape[1] // 2, g.shape[2]), g.dtype) for g in grads),
        in_specs=[hbm] * n + after_specs, out_specs=(hbm,) * n,
        scratch_shapes=[pltpu.SemaphoreType.DMA((n,)), pltpu.SemaphoreType.DMA((n,))],
        name="reduce_sibling_send_" + tag, compiler_params=_params())(*grads, *after_ops)


def _row_tile(rows, cap=512):
    return max(r for r in range(16, cap + 1, 16) if rows % r == 0)


def _chip_presum(k, grad, got, c_idx):
    n_sh, rows, cols = grad.shape
    half = rows // 2
    tr = _row_tile(half)
    grad4 = grad.reshape(n_sh, 2, half, cols)

    def body(c_ref, a_ref, b_ref, o_ref):
        o_ref[...] = (a_ref[...].astype(F32) + b_ref[...].astype(F32)).astype(o_ref.dtype)

    return pl.pallas_call(
        body, out_shape=SDS((n_sh, half, cols), BF16),
        grid_spec=pltpu.PrefetchScalarGridSpec(
            num_scalar_prefetch=1, grid=(n_sh, half // tr),
            in_specs=[BS((None, None, tr, cols), lambda s, i, c_ref: (s, c_ref[0], i, 0)),
                      BS((None, tr, cols), lambda s, i, c_ref: (s, i, 0))],
            out_specs=BS((None, tr, cols), lambda s, i, c_ref: (s, i, 0))),
        name=f"reduce_presum{k}", compiler_params=_params("parallel", "parallel"))(c_idx, grad4, got)


HBM_SPEC = BS(memory_space=pltpu.HBM)
SEM_SPEC = BS(memory_space=pltpu.SEMAPHORE)
DATAFLOW = pltpu.SideEffectType.DATAFLOW_SIDE_EFFECTING


def _chip_exchange_copies(parts, lands, send_sems, recv_sems):
    _, _, c, chips = _mesh_place()
    return [_remote(parts[k].at[2 * px + py], lands[k].at[j], send_sems, recv_sems, 3 * k + j, (px, py, c))
            for k in range(len(parts)) for j, (px, py) in enumerate(chips)]


def _gather_copies(shards, lands, send_sems, recv_sems):
    x, y, c, chips = _mesh_place()
    return [_remote(shards[k], lands[k].at[2 * x + y], send_sems, recv_sems, 3 * k + j, (px, py, c))
            for k in range(len(shards)) for j, (px, py) in enumerate(chips)]


def _gather_half_copies(shards, lands, send_sems, recv_sems):
    x, y, c, chips = _mesh_place()
    out = []
    for k in range(len(shards)):
        half = shards[k].shape[0] // 2
        mine = pl.ds(pl.multiple_of(c * half, 16), half)
        for j, (px, py) in enumerate(chips):
            out.append(_remote(shards[k].at[mine, :], lands[k].at[2 * x + y, mine, :], send_sems, recv_sems,
                               3 * k + j, (px, py, c)))
    return out


def _sibling_fill(tag, lands):
    n = len(lands)

    def body(*refs):
        outs = refs[n:2 * n]
        send_sems, recv_sems = refs[2 * n:]
        x, y, c, chips = _mesh_place()
        copies = []
        for k in range(n):
            half = lands[k].shape[1] // 2
            mine = pl.ds(pl.multiple_of(c * half, 16), half)
            for j, (px, py) in enumerate(chips):
                blk = outs[k].at[2 * px + py, mine, :]
                copies.append(_remote(blk, blk, send_sems, recv_sems, 3 * k + j, (x, y, 1 - c)))
        for cp in copies:
            cp.start()
        for cp in copies:
            cp.wait_recv()
        for cp in copies:
            cp.wait_send()

    hbm = BS(memory_space=pl.ANY)
    return list(pl.pallas_call(
        body, out_shape=tuple(SDS(a.shape, a.dtype) for a in lands),
        in_specs=[hbm] * n, out_specs=(hbm,) * n, input_output_aliases={k: k for k in range(n)},
        scratch_shapes=[pltpu.SemaphoreType.DMA((3 * n,)), pltpu.SemaphoreType.DMA((3 * n,))],
        name="gather_fill_" + tag, compiler_params=_params())(*lands))


def _swap_copies(grads, lands, send_sems, recv_sems):
    x, y, c, _ = _mesh_place()
    out = []
    for k in range(len(grads)):
        half = grads[k].shape[1] // 2
        theirs = pl.ds(pl.multiple_of((1 - c) * half, 16), half)
        out.append(_remote(grads[k].at[:, theirs, :], lands[k], send_sems, recv_sems, k, (x, y, 1 - c)))
    return out


def _join_copies(fulls, same, send_sems, recv_sems):
    x, y, c, _ = _mesh_place()
    out = []
    for k in range(len(fulls)):
        half = fulls[k].shape[0] // 2
        mine = fulls[k].at[pl.ds(pl.multiple_of(c * half, 8), half), :]
        out.append(_remote(mine, mine, send_sems, recv_sems, k, (x, y, 1 - c)))
    return out


def _everyone_copies(packs, lands, send_sems, recv_sems):
    x, y, c, _ = _mesh_place()
    out = []
    for k in range(len(packs)):
        for j in range(N_DEV - 1):
            bx, by, bc = (j + 1) >> 2 & 1, (j + 1) >> 1 & 1, (j + 1) & 1
            peer = (x ^ bx, y ^ by, c ^ bc)
            out.append(_remote(packs[k], lands[k].at[4 * x + 2 * y + c], send_sems, recv_sems, (N_DEV - 1) * k + j, peer))
    return out


def _split_start(name, copies, sources, land_shapes, after=(), fanout=3):
    n = len(sources)
    n_land = len(land_shapes)
    m = n + n_land
    n_sems = fanout * n
    after_ops, after_specs = _after_operands(after)

    def body(*refs):
        ins = refs[:n]
        lands = refs[n:m] if n_land else ins
        send_sems, recv_sems = refs[m + len(after_ops)], refs[m + len(after_ops) + 1]
        token = refs[-1]
        for cp in copies(ins, lands, send_sems, recv_sems):
            cp.start()
        token[...] = jnp.zeros_like(token)

    lands = [pltpu.with_memory_space_constraint(lax.empty(s, d), pltpu.HBM) for s, d in land_shapes]
    sources = [pltpu.with_memory_space_constraint(p, pltpu.HBM) for p in sources]
    thru = [pltpu.HBM(a.shape, a.dtype) for a in sources + lands]
    out = pl.pallas_call(
        body, name=name,
        out_shape=(pltpu.SemaphoreType.DMA((n_sems,)), pltpu.SemaphoreType.DMA((n_sems,)), *thru, SDS((8, 128), F32)),
        in_specs=[HBM_SPEC] * m + after_specs,
        out_specs=(SEM_SPEC, SEM_SPEC, *[HBM_SPEC] * m, BS(memory_space=pltpu.VMEM)),
        input_output_aliases={i: 2 + i for i in range(m)},
        compiler_params=pltpu.CompilerParams(has_side_effects=DATAFLOW))(*sources, *lands, *after_ops)
    return out[0], out[1], list(out[2:2 + n]), list(out[2 + n:2 + m]), out[-1]


def _split_wait(name, copies, send_sems, recv_sems, sources, lands, after):
    n = len(sources)
    m = n + len(lands)
    after_ops, after_specs = _after_operands(after)

    def body(*refs):
        ins = refs[:n]
        zones = refs[n:m] if m > n else ins
        for cp in copies(ins, zones, refs[m], refs[m + 1]):
            cp.wait_send()
            cp.wait_recv()

    out = pl.pallas_call(
        body, name=name,
        out_shape=tuple(pltpu.HBM(a.shape, a.dtype) for a in sources + lands),
        in_specs=[HBM_SPEC] * m + [SEM_SPEC, SEM_SPEC] + after_specs, out_specs=(HBM_SPEC,) * m,
        input_output_aliases={i: i for i in range(m)},
        compiler_params=pltpu.CompilerParams(has_side_effects=DATAFLOW))(*sources, *lands, send_sems, recv_sems, *after_ops)
    return list(out[:n]), list(out[n:])


class _WeightGatherer:
    def __init__(self, shards):
        self.shards, self.open = shards, {}
        self.me = 2 * lax.axis_index("x") + lax.axis_index("y")

    HALVED = ("f1a",)

    def start(self, tag, after=()):
        shapes = [((N_SHARD,) + s.shape, s.dtype) for s in self.shards[tag]]
        copies = _gather_half_copies if tag in self.HALVED else _gather_copies
        self.open[tag] = _split_start("gather_start_" + tag, copies, self.shards[tag], shapes, after)
        return [self.open[tag][-1]]

    def finish(self, tag, after):
        send_sems, recv_sems, shards, lands, _ = self.open.pop(tag)
        copies = _gather_half_copies if tag in self.HALVED else _gather_copies
        shards, lands = _split_wait("gather_wait_" + tag, copies, send_sems, recv_sems, shards, lands, after)
        if tag in self.HALVED:
            lands = _sibling_fill(tag, lands)
        return [lax.dynamic_update_slice(zone, s[None], (self.me, 0, 0)) for zone, s in zip(lands, shards)]


class _GradReducer:
    def __init__(self):
        self.c_idx = lax.axis_index("c").astype(jnp.int32).reshape(1)
        self.place = jnp.stack([2 * lax.axis_index("x") + lax.axis_index("y"), lax.axis_index("c")]).astype(jnp.int32)
        self.swaps, self.open, self.landed, self.joins, self.reduced = {}, {}, {}, {}, []

    def swap_start(self, tag, grads, after=()):
        shapes = [((g.shape[0], g.shape[1] // 2, g.shape[2]), g.dtype) for g in grads]
        self.swaps[tag] = _split_start("reduce_swap_start_" + tag, _swap_copies, grads, shapes, after, fanout=1)
        return [self.swaps[tag][-1]]

    def start(self, tag, grads, after=(), swapped=()):
        pairs = []
        for s in swapped:
            send_sems, recv_sems, early, lands, _ = self.swaps.pop(s)
            pairs += zip(*_split_wait("reduce_swap_wait_" + s, _swap_copies, send_sems, recv_sems, early, lands, grads[-1:]))
        pairs += zip(grads, _sibling_swap_halves(tag, grads, after))
        parts = [_chip_presum(f"{tag}{k}", g, s, self.c_idx) for k, (g, s) in enumerate(pairs)]
        shapes = [((3,) + p.shape[1:], p.dtype) for p in parts]
        self.open[tag] = _split_start("reduce_exchange_start_" + tag, _chip_exchange_copies, parts, shapes)
        return [self.open[tag][-1]]

    def finish(self, tag, after):
        send_sems, recv_sems, parts, lands, _ = self.open.pop(tag)
        self.landed[tag] = _split_wait("reduce_exchange_wait_" + tag, _chip_exchange_copies, send_sems, recv_sems, parts, lands, after)
        return self.landed[tag][1][:1]

    def _sums(self, tag, after=()):
        parts, landed = self.landed.pop(tag)
        return [_chip_sum(f"{tag}{k}", p, got, self.place, after) for k, (p, got) in enumerate(zip(parts, landed))]

    def join_start(self, tag, after=()):
        self.joins[tag] = _split_start("reduce_join_start_" + tag, _join_copies, self._sums(tag, after), [], fanout=1)
        return [self.joins[tag][-1]]

    def join_finish(self, tag, after):
        send_sems, recv_sems, fulls, _, _ = self.joins.pop(tag)
        self.reduced += _split_wait("reduce_join_wait_" + tag, _join_copies, send_sems, recv_sems, fulls, [], after)[0]

    def join(self, tag, after=()):
        self.reduced += _sibling_join_halves(self._sums(tag), after)


def _chip_sum(k, part, got, place, after=()):
    _, half, cols = part.shape
    tr = _row_tile(half)
    n_t = half // tr
    after_ops, after_specs = _after_operands(after)

    def body(place_ref, a_ref, b_ref, *rest):
        o_ref = rest[-1]
        acc = a_ref[...].astype(F32)
        for j in range(3):
            acc = acc + b_ref[j].astype(F32)
        o_ref[...] = acc

    return pl.pallas_call(
        body, out_shape=SDS((2 * half, cols), F32),
        grid_spec=pltpu.PrefetchScalarGridSpec(
            num_scalar_prefetch=1, grid=(n_t,),
            in_specs=[BS((None, tr, cols), lambda i, place_ref: (place_ref[0], i, 0)),
                      BS((3, tr, cols), lambda i, place_ref: (0, i, 0))] + after_specs,
            out_specs=BS((tr, cols), lambda i, place_ref: (place_ref[1] * n_t + i, 0))),
        name=f"reduce_sum{k}", compiler_params=_params("parallel"))(place, part, got, *after_ops)


def _sibling_join_halves(fulls, after=()):
    n = len(fulls)
    after_ops, after_specs = _after_operands(after)

    def body(*refs):
        outs = refs[n + len(after_ops):2 * n + len(after_ops)]
        send_sems, recv_sems = refs[2 * n + len(after_ops):]
        copies = _join_copies(outs, outs, send_sems, recv_sems)
        for cp in copies:
            cp.start()
        for cp in copies:
            cp.wait_recv()
        for cp in copies:
            cp.wait_send()

    hbm = BS(memory_space=pl.ANY)
    return list(pl.pallas_call(
        body, out_shape=tuple(SDS(f.shape, f.dtype) for f in fulls),
        in_specs=[hbm] * n + after_specs, out_specs=(hbm,) * n, input_output_aliases={k: k for k in range(n)},
        scratch_shapes=[pltpu.SemaphoreType.DMA((n,)), pltpu.SemaphoreType.DMA((n,))],
        name="reduce_sibling_join", compiler_params=_params())(*fulls, *after_ops))


N_DEV = 8


def _sum_devices(packs):
    _, rows, lanes = packs.shape

    def body(p_ref, o_ref):
        acc = p_ref[0]
        for dev in range(1, N_DEV):
            acc = acc + p_ref[dev]
        o_ref[...] = acc

    vm = BS(memory_space=pltpu.VMEM)
    return pl.pallas_call(body, out_shape=SDS((rows, lanes), F32), in_specs=[vm], out_specs=vm,
                          name="small_sum", compiler_params=_params())(packs)


def _adamw(name, w, grad, row0, m, v, after=()):
    rows, cols = w.shape
    tr = rows if rows < 16 else _row_tile(rows, 256)
    bc1 = 1.0 - ADAM_B1 ** ADAM_STEP
    bc2 = 1.0 - ADAM_B2 ** ADAM_STEP
    after_ops, after_specs = _after_operands(after)

    def body(w_ref, g_ref, m_ref, v_ref, *rest):
        go_ref, d_ref, mo_ref, vo_ref = rest[len(after_ops):]
        g = g_ref[...]
        m_new = ADAM_B1 * m_ref[...] + (1.0 - ADAM_B1) * g
        v_new = ADAM_B2 * v_ref[...] + (1.0 - ADAM_B2) * (g * g)
        go_ref[...] = g
        mo_ref[...] = m_new
        vo_ref[...] = v_new
        d_ref[...] = -ADAM_LR * ((m_new / bc1) / (jnp.sqrt(v_new / bc2) + ADAM_EPS) + ADAM_WD * w_ref[...])

    blk = BS((tr, cols), lambda i: (i, 0))
    shape = SDS((rows, cols), F32)
    return pl.pallas_call(
        body, out_shape=(shape,) * 4, grid=(rows // tr,),
        in_specs=[blk, BS((tr, cols), lambda i: (row0 // tr + i, 0)), blk, blk] + after_specs, out_specs=(blk,) * 4,
        name=name, compiler_params=_params("parallel"))(w, grad, m, v, *after_ops)


SMALL_LANES = 128


def _pack_small(parts):
    flat = jnp.concatenate([jnp.ravel(p) for p in parts])
    rows = -(-flat.shape[0] // (64 * SMALL_LANES)) * 64
    return jnp.pad(flat, (0, rows * SMALL_LANES - flat.shape[0])).reshape(rows, SMALL_LANES)


def _unpack_small(packed, like):
    flat = jnp.ravel(packed)
    out, at = [], 0
    for p in like:
        out.append(flat[at:at + p.size].reshape(p.shape))
        at += p.size
    return out


def kernel(x, mem, ffn1_norm, ffn1_w_gate, ffn1_w_up, ffn1_w_down, mix_norm, w_in, pool_w, pool_scale, w_pool_proj, ssm_a_re, ssm_a_im, ssm_log_dt, ssm_b_re, ssm_b_im, ssm_c_re, ssm_c_im, ssm_d, w_glu_val, w_glu_gate, w_mix_out, xattn_norm, mem_norm, w_q, w_kv, w_xo, ffn2_norm, ffn2_w_gate, ffn2_w_up, ffn2_w_down, final_norm, loss_target, m_ffn1_norm, m_ffn1_w_gate, m_ffn1_w_up, m_ffn1_w_down, m_mix_norm, m_w_in, m_pool_w, m_pool_scale, m_w_pool_proj, m_ssm_a_re, m_ssm_a_im, m_ssm_log_dt, m_ssm_b_re, m_ssm_b_im, m_ssm_c_re, m_ssm_c_im, m_ssm_d, m_w_glu_val, m_w_glu_gate, m_w_mix_out, m_xattn_norm, m_mem_norm, m_w_q, m_w_kv, m_w_xo, m_ffn2_norm, m_ffn2_w_gate, m_ffn2_w_up, m_ffn2_w_down, m_final_norm, v_ffn1_norm, v_ffn1_w_gate, v_ffn1_w_up, v_ffn1_w_down, v_mix_norm, v_w_in, v_pool_w, v_pool_scale, v_w_pool_proj, v_ssm_a_re, v_ssm_a_im, v_ssm_log_dt, v_ssm_b_re, v_ssm_b_im, v_ssm_c_re, v_ssm_c_im, v_ssm_d, v_w_glu_val, v_w_glu_gate, v_w_mix_out, v_xattn_norm, v_mem_norm, v_w_q, v_w_kv, v_w_xo, v_ffn2_norm, v_ffn2_w_gate, v_ffn2_w_up, v_ffn2_w_down, v_final_norm):
    given = dict(locals())
    w = {n: given[n] for n in WEIGHTS}
    m = {n: given["m_" + n] for n in WEIGHTS}
    v = {n: given["v_" + n] for n in WEIGHTS}

    def shard_view(a, n):
        return a[0].T if n in TRANSPOSED else a[0]

    def shard_unview(a, n):
        return (a.T if n in TRANSPOSED else a)[None]

    shards = {tag: [jnp.concatenate([shard_view(w[n], n).astype(BF16) for n in grp], axis=0) for grp in arrays]
              for tag, arrays in GATHER_PHASES.items()}
    reducer = _GradReducer()
    loss_part, grad_x, _, small = _device_step(x[0], mem[0], loss_target[0], _WeightGatherer(shards),
                                               {n: w[n] for n in SMALL}, reducer)

    small_like = [w[n] for n in SMALL] + [loss_part[0, :1]]
    pack = _pack_small([small[n] for n in SMALL] + [loss_part[0, :1]])
    everyone = _split_start("small_start", _everyone_copies, [pack], [((N_DEV,) + pack.shape, F32)], fanout=N_DEV - 1)
    reducer.join("b", after=everyone[-1:])

    grads, delta, new_m, new_v = {}, {}, {}, {}
    big_done = []
    for grp, red in zip(REDUCE_GROUPS, reducer.reduced):
        row0 = 0
        for n in grp:
            w_n = shard_view(w[n], n)
            outs = _adamw("adamw_" + n, w_n, red, row0, shard_view(m[n], n), shard_view(v[n], n), after=everyone[-1:])
            grads[n], delta[n], new_m[n], new_v[n] = (shard_unview(o, n) for o in outs)
            big_done.append(outs[1])
            row0 += w_n.shape[0]

    send_sems, recv_sems, packs, landed, _ = everyone
    packs, landed = _split_wait("small_wait", _everyone_copies, send_sems, recv_sems, packs, landed, big_done)
    mine = 4 * lax.axis_index("x") + 2 * lax.axis_index("y") + lax.axis_index("c")
    summed = _sum_devices(lax.dynamic_update_slice(landed[0], packs[0][None], (mine, 0, 0)))
    g_small = dict(zip(SMALL + ("loss",), _unpack_small(summed, small_like)))
    loss = g_small.pop("loss").reshape(())
    narrow = [n for n in SMALL if w[n].ndim > 3]
    dense = [n for n in SMALL if n not in narrow]
    for n in narrow:
        two_d = (-1, w[n].shape[-1])
        outs = _adamw("adamw_" + n, w[n].reshape(two_d), g_small[n].reshape(two_d), 0, m[n].reshape(two_d), v[n].reshape(two_d))
        grads[n], delta[n], new_m[n], new_v[n] = (o.reshape(w[n].shape) for o in outs)
    dense_like = [w[n] for n in dense]
    packed = _adamw("adamw_small", _pack_small(dense_like), _pack_small([g_small[n] for n in dense]), 0,
                    _pack_small([m[n] for n in dense]), _pack_small([v[n] for n in dense]))
    for out, store in zip(packed, (grads, delta, new_m, new_v)):
        for n, val in zip(dense, _unpack_small(out, dense_like)):
            store[n] = val

    return (loss, grad_x[None], *[grads[n] for n in WEIGHTS], *[delta[n] for n in WEIGHTS],
            *[new_m[n] for n in WEIGHTS], *[new_v[n] for n in WEIGHTS])
```

```python
import functools
import math

import jax
import jax.numpy as jnp
from jax import lax
from jax.experimental import pallas as pl
from jax.experimental.pallas import tpu as pltpu

F32 = jnp.float32
BF16 = jnp.bfloat16
SDS = jax.ShapeDtypeStruct
BS = pl.BlockSpec
MESH = pl.DeviceIdType.MESH

D_MODEL = 1024
D_FF = 2816
N_SHARD = 4
FF_SH = D_FF // N_SHARD
D_POOL = 512
POOL_WINDOWS = (2, 4, 8, 16)
POOL_GROUP = 128
D_SSM = 256
SSM_GROUPS = 16
SSM_GROUP = 16
SSM_STATE = 64
SSM_CH = SSM_GROUPS * SSM_STATE
N_HEADS = 4
HEAD_DIM = 256
EPS = 1e-6
ADAM_LR, ADAM_B1, ADAM_B2, ADAM_EPS, ADAM_WD, ADAM_STEP = 0.001, 0.9, 0.999, 1e-08, 0.01, 10

VMEM_LIMIT_V7X = 52 * 1024 * 1024
TM = 512

NN = (((1,), (0,)), ((), ()))
NT = (((1,), (1,)), ((), ()))
TN = (((0,), (0,)), ((), ()))


def _params(*sem):
    return pltpu.CompilerParams(dimension_semantics=sem if sem else None, vmem_limit_bytes=VMEM_LIMIT_V7X)


def _dot(a, b, dims=NN):
    return lax.dot_general(a.astype(BF16), b.astype(BF16), dims, preferred_element_type=F32)


def _sigmoid(v):
    return pl.reciprocal(1.0 + jnp.exp(-v), approx=True)


def _block_dims(spec):
    return tuple(d for d in spec.block_shape if d is not None)


def _after_operands(after):
    return list(after), [BS(memory_space=pl.ANY)] * len(after)


def _mm(name, pairs, *, grid, out_shape, out_spec, red_axis=None, extras=(), epilogue=None, after=()):
    n_pairs, n_extra = len(pairs), len(extras)
    n_red = grid[red_axis] if red_axis is not None else 1
    dims = [p[4] for p in pairs]

    def body(*refs):
        ab = refs[:2 * n_pairs]
        ex = refs[2 * n_pairs:2 * n_pairs + n_extra]
        o_ref = refs[2 * n_pairs + n_extra + len(after)]

        def partial():
            acc = None
            for p in range(n_pairs):
                t = _dot(ab[2 * p][...], ab[2 * p + 1][...], dims[p])
                acc = t if acc is None else acc + t
            return acc

        def finish(acc):
            res = epilogue(acc, *[e[...] for e in ex]) if epilogue is not None else acc
            o_ref[...] = res.astype(o_ref.dtype)

        if n_red == 1:
            finish(partial())
        else:
            acc_ref = refs[-1]
            k = pl.program_id(red_axis)

            @pl.when(k == 0)
            def _():
                acc_ref[...] = jnp.zeros_like(acc_ref)

            acc_ref[...] += partial()

            @pl.when(k == n_red - 1)
            def _():
                finish(acc_ref[...])

    operands, in_specs = [], []
    for a, a_spec, b, b_spec, _ in pairs:
        operands += [a, b]
        in_specs += [a_spec, b_spec]
    for e, e_spec in extras:
        operands.append(e)
        in_specs.append(e_spec)
    after_ops, after_specs = _after_operands(after)
    operands += after_ops
    in_specs += after_specs
    scratch = [pltpu.VMEM(_block_dims(out_spec), F32)] if n_red > 1 else []
    sem = tuple("arbitrary" if ax == red_axis else "parallel" for ax in range(len(grid)))
    return pl.pallas_call(body, out_shape=out_shape, grid=grid, in_specs=in_specs, out_specs=out_spec,
                          scratch_shapes=scratch, name=name, compiler_params=_params(*sem))(*operands)


def _rmsnorm(name, h, gain, tm, after=()):
    t, d = h.shape
    after_ops, after_specs = _after_operands(after)

    def body(h_ref, g_ref, *rest):
        u_ref = rest[-1]
        hv = h_ref[...]
        r = lax.rsqrt(jnp.mean(hv * hv, axis=-1, keepdims=True) + EPS)
        u_ref[...] = ((hv * r) * g_ref[...]).astype(u_ref.dtype)

    return pl.pallas_call(
        body, out_shape=SDS((t, d), BF16), grid=(t // tm,),
        in_specs=[BS((tm, d), lambda i: (i, 0)), BS((1, d), lambda i: (0, 0))] + after_specs,
        out_specs=BS((tm, d), lambda i: (i, 0)), name=name, compiler_params=_params("parallel"))(h, gain, *after_ops)


def _rmsnorm_bwd(name, h, gain, du, dh_in, tm):
    t, d = h.shape
    has_in = dh_in is not None

    def body(*refs):
        if has_in:
            h_ref, g_ref, du_ref, dhin_ref, dh_ref, dhb_ref, dg_ref = refs
        else:
            h_ref, g_ref, du_ref, dh_ref, dhb_ref, dg_ref = refs
        i = pl.program_id(0)
        hv = h_ref[...]
        r = lax.rsqrt(jnp.mean(hv * hv, axis=-1, keepdims=True) + EPS)
        n = hv * r
        duv = du_ref[...].astype(F32)
        dn = duv * g_ref[...]
        dh = r * (dn - n * jnp.mean(dn * n, axis=-1, keepdims=True))
        if has_in:
            dh = dhin_ref[...] + dh
        dh_ref[...] = dh
        dhb_ref[...] = dh.astype(BF16)

        @pl.when(i == 0)
        def _():
            dg_ref[...] = jnp.zeros_like(dg_ref)

        dg_ref[...] += jnp.sum(duv * n, axis=0, keepdims=True)

    row = BS((tm, d), lambda i: (i, 0))
    vec = BS((1, d), lambda i: (0, 0))
    operands = [h, gain, du] + ([dh_in] if has_in else [])
    in_specs = [row, vec, row] + ([row] if has_in else [])
    return pl.pallas_call(
        body, out_shape=(SDS((t, d), F32), SDS((t, d), BF16), SDS((1, d), F32)), grid=(t // tm,),
        in_specs=in_specs, out_specs=(row, row, vec), name=name, compiler_params=_params("arbitrary"))(*operands)


def _loss_head_tile(i, hv, g_ref, t_ref, loss_ref, dh_ref, dhb_ref, dg_ref):
    g = g_ref[...]
    r = lax.rsqrt(jnp.mean(hv * hv, axis=-1, keepdims=True) + EPS)
    n = hv * r
    err = n * g - t_ref[...]
    dy = err * (1.0 / hv.shape[-1])
    dn = dy * g
    dh = r * (dn - n * jnp.mean(dn * n, axis=-1, keepdims=True))
    dh_ref[...] = dh
    dhb_ref[...] = dh.astype(BF16)

    @pl.when(i == 0)
    def _():
        dg_ref[...] = jnp.zeros_like(dg_ref)
        loss_ref[...] = jnp.zeros_like(loss_ref)

    dg_ref[...] += jnp.sum(dy * n, axis=0, keepdims=True)
    part = 0.5 * jnp.sum(jnp.mean(err * err, axis=-1, keepdims=True), axis=0, keepdims=True)
    loss_ref[...] += jnp.broadcast_to(part, loss_ref.shape)


def _norm_tile(h, g_ref, u_ref):
    r = lax.rsqrt(jnp.mean(h * h, axis=-1, keepdims=True) + EPS)
    u_ref[...] = ((h * r) * g_ref[...]).astype(u_ref.dtype)


FFN_GATE, FFN_UP, FFN_DOWN = 0, 1, 2


def _ffn_up(name, u, w_f, tm, after=()):
    t, d = u.shape
    after_ops, after_specs = _after_operands(after)

    def body(u_ref, wg_ref, wu_ref, *rest):
        pg_ref, pu_ref, a_ref = rest[len(after_ops):]
        uv = u_ref[...]
        for s in range(N_SHARD):
            g = _dot(uv, wg_ref[s], NT)
            up = _dot(uv, wu_ref[s], NT)
            sg = _sigmoid(g)
            silu = g * sg
            a_ref[s] = (silu * up).astype(BF16)
            pu_ref[s] = (0.5 * silu).astype(BF16)
            pg_ref[s] = (0.5 * sg * (1.0 + g * (1.0 - sg)) * up).astype(BF16)

    hid = BS((N_SHARD, tm, FF_SH), lambda i: (0, i, 0))
    shape = SDS((N_SHARD, t, FF_SH), BF16)
    return pl.pallas_call(
        body, out_shape=(shape, shape, shape), grid=(t // tm,),
        in_specs=[BS((tm, d), lambda i: (i, 0)), _ffn_all_shards_spec(w_f["gate"][1]),
                  _ffn_all_shards_spec(w_f["up"][1])] + after_specs,
        out_specs=(hid, hid, hid), name=name,
        compiler_params=_params("parallel"))(u, w_f["gate"][0], w_f["up"][0], *after_ops)


def _ffn_all_shards_spec(block):
    return BS((N_SHARD, FF_SH, D_MODEL), lambda i: (0, block, 0))


def _ffn_down(name, a, w_f, resid, tm, next_gain=None, head=None):
    t, d = resid.shape
    row = BS((tm, d), lambda i: (i, 0))
    vec = BS((1, d), lambda i: (0, 0))

    def body(a_ref, w_ref, res_ref, *rest):
        acc = _dot(a_ref[0], w_ref[0])
        for s in range(1, N_SHARD):
            acc = acc + _dot(a_ref[s], w_ref[s])
        h = res_ref[...] + 0.5 * acc
        if head is not None:
            _loss_head_tile(pl.program_id(0), h, *rest)
        else:
            g_ref, h_ref, u_ref = rest
            h_ref[...] = h
            _norm_tile(h, g_ref, u_ref)

    if head is not None:
        extra, extra_specs = list(head), [vec, row]
        out_shape = (SDS((1, 128), F32), SDS((t, d), F32), SDS((t, d), BF16), SDS((1, d), F32))
        out_specs = (BS((1, 128), lambda i: (0, 0)), row, row, vec)
    else:
        extra, extra_specs = [next_gain], [vec]
        out_shape = (SDS((t, d), F32), SDS((t, d), BF16))
        out_specs = (row, row)
    return pl.pallas_call(
        body, out_shape=out_shape, grid=(t // tm,),
        in_specs=[BS((N_SHARD, tm, FF_SH), lambda i: (0, i, 0)), _ffn_all_shards_spec(w_f["down"][1]), row] + extra_specs,
        out_specs=out_specs, name=name,
        compiler_params=_params("arbitrary" if head is not None else "parallel"))(a, w_f["down"][0], resid, *extra)


def _mm_resid_norm(name, a, b, resid, next_gain, tm):
    t, d = resid.shape

    def body(a_ref, b_ref, res_ref, g_ref, h_ref, u_ref):
        h = res_ref[...] + _dot(a_ref[...], b_ref[...])
        h_ref[...] = h
        _norm_tile(h, g_ref, u_ref)

    row = BS((tm, d), lambda i: (i, 0))
    return pl.pallas_call(
        body, out_shape=(SDS((t, d), F32), SDS((t, d), BF16)), grid=(t // tm,),
        in_specs=[BS((tm, a.shape[1]), lambda i: (i, 0)), BS(b.shape, lambda i: (0, 0)), row, BS((1, d), lambda i: (0, 0))],
        out_specs=(row, row), name=name, compiler_params=_params("parallel"))(a, b, resid, next_gain)


def _ffn_bwd_act(name, dh_b, w_f, pg, pu, tm, after=()):
    t, d = dh_b.shape
    after_ops, after_specs = _after_operands(after)

    def body(dh_ref, wd_ref, pg_ref, pu_ref, *rest):
        dg_ref, dup_ref = rest[len(after_ops):]
        dh = dh_ref[...]
        for s in range(N_SHARD):
            da = _dot(dh, wd_ref[s], NT)
            dg_ref[s] = (da * pg_ref[s].astype(F32)).astype(BF16)
            dup_ref[s] = (da * pu_ref[s].astype(F32)).astype(BF16)

    hid = BS((N_SHARD, tm, FF_SH), lambda i: (0, i, 0))
    shape = SDS((N_SHARD, t, FF_SH), BF16)
    return pl.pallas_call(
        body, out_shape=(shape, shape), grid=(t // tm,),
        in_specs=[BS((tm, d), lambda i: (i, 0)), _ffn_all_shards_spec(w_f["down"][1]), hid, hid] + after_specs,
        out_specs=(hid, hid), name=name,
        compiler_params=_params("parallel"))(dh_b, w_f["down"][0], pg, pu, *after_ops)


def _ffn_dw(name, u, dg, dup, a, dh_b, tm):
    t, d = u.shape
    n_t = t // tm

    def body(u_ref, dg_ref, dup_ref, a_ref, dh_ref, o_ref, acc):
        i = pl.program_id(1)

        @pl.when(i == 0)
        def _():
            acc[...] = jnp.zeros_like(acc)

        uv = u_ref[...]
        acc[FFN_GATE] += _dot(dg_ref[...], uv, TN)
        acc[FFN_UP] += _dot(dup_ref[...], uv, TN)
        acc[FFN_DOWN] += _dot(a_ref[...], dh_ref[...], TN)

        @pl.when(i == n_t - 1)
        def _():
            o_ref[FFN_GATE] = acc[FFN_GATE].astype(BF16)
            o_ref[FFN_UP] = acc[FFN_UP].astype(BF16)
            o_ref[FFN_DOWN] = (0.5 * acc[FFN_DOWN]).astype(BF16)

    hid = BS((None, tm, FF_SH), lambda s, i: (s, i, 0))
    row = BS((tm, d), lambda s, i: (i, 0))
    return pl.pallas_call(
        body, out_shape=SDS((N_SHARD, 3, FF_SH, d), BF16), grid=(N_SHARD, n_t),
        in_specs=[row, hid, hid, hid, row], out_specs=BS((None, 3, FF_SH, d), lambda s, i: (s, 0, 0, 0)),
        scratch_shapes=[pltpu.VMEM((3, FF_SH, d), F32)],
        name=name, compiler_params=_params("parallel", "arbitrary"))(u, dg, dup, a, dh_b)


def _norm_bwd_tile(i, du, h_ref, g_ref, dhin_ref, dh_ref, dhb_ref, dg_ref):
    hv = h_ref[...]
    r = lax.rsqrt(jnp.mean(hv * hv, axis=-1, keepdims=True) + EPS)
    n = hv * r
    dn = du * g_ref[...]
    dh = dhin_ref[...] + r * (dn - n * jnp.mean(dn * n, axis=-1, keepdims=True))
    dh_ref[...] = dh
    dhb_ref[...] = dh.astype(BF16)

    @pl.when(i == 0)
    def _():
        dg_ref[...] = jnp.zeros_like(dg_ref)

    dg_ref[...] += jnp.sum(du * n, axis=0, keepdims=True)


def _norm_bwd_specs(tm):
    row = BS((tm, D_MODEL), lambda i: (i, 0))
    vec = BS((1, D_MODEL), lambda i: (0, 0))
    return [row, vec, row], (row, row, vec)


def _norm_bwd_shapes(t):
    return SDS((t, D_MODEL), F32), SDS((t, D_MODEL), BF16), SDS((1, D_MODEL), F32)


def _ffn_dx(name, dg, dup, w_f, h, gain, dh_in, tm, after=()):
    t = dg.shape[1]
    tm = tm // 2
    after_ops, after_specs = _after_operands(after)

    def body(dg_ref, dup_ref, wg_ref, wu_ref, h_ref, g_ref, dhin_ref, *rest):
        acc = _dot(dg_ref[0], wg_ref[0]) + _dot(dup_ref[0], wu_ref[0])
        for s in range(1, N_SHARD):
            acc = acc + _dot(dg_ref[s], wg_ref[s]) + _dot(dup_ref[s], wu_ref[s])
        _norm_bwd_tile(pl.program_id(0), acc, h_ref, g_ref, dhin_ref, *rest[len(after_ops):])

    hid = BS((N_SHARD, tm, FF_SH), lambda i: (0, i, 0))
    norm_in, norm_out = _norm_bwd_specs(tm)
    return pl.pallas_call(
        body, out_shape=_norm_bwd_shapes(t), grid=(t // tm,),
        in_specs=[hid, hid, _ffn_all_shards_spec(w_f["gate"][1]), _ffn_all_shards_spec(w_f["up"][1])] + norm_in + after_specs,
        out_specs=norm_out, name=name,
        compiler_params=_params("arbitrary"))(dg, dup, w_f["gate"][0], w_f["up"][0], h, gain, dh_in, *after_ops)


def _mm_norm_bwd(name, a, b, dims, h, gain, dh_in, tm):
    t = a.shape[0]

    def body(a_ref, b_ref, h_ref, g_ref, dhin_ref, *outs):
        _norm_bwd_tile(pl.program_id(0), _dot(a_ref[...], b_ref[...], dims), h_ref, g_ref, dhin_ref, *outs)

    norm_in, norm_out = _norm_bwd_specs(tm)
    return pl.pallas_call(
        body, out_shape=_norm_bwd_shapes(t), grid=(t // tm,),
        in_specs=[BS((tm, a.shape[1]), lambda i: (i, 0)), BS(b.shape, lambda i: (0, 0))] + norm_in,
        out_specs=norm_out, name=name, compiler_params=_params("arbitrary"))(a, b, h, gain, dh_in)


def _plain_mm(name, a, b, dims, out_dtype, tm, resid=None, after=()):
    t = a.shape[0]
    n = b.shape[1] if dims == NN else b.shape[0]
    extras = [(resid, BS((tm, n), lambda i: (i, 0)))] if resid is not None else []
    epi = (lambda acc, res: res + acc) if resid is not None else None
    return _mm(name, [(a, BS((tm, a.shape[1]), lambda i: (i, 0)), b, BS(b.shape, lambda i: (0, 0)), dims)],
               grid=(t // tm,), out_shape=SDS((t, n), out_dtype), out_spec=BS((tm, n), lambda i: (i, 0)),
               extras=extras, epilogue=epi, after=after)


def _dw_mm(name, a, b, tm, out_dtype=BF16, after=()):
    t, k = a.shape
    n = b.shape[1]
    return _mm(name, [(a, BS((tm, k), lambda i: (i, 0)), b, BS((tm, n), lambda i: (i, 0)), TN)],
               grid=(t // tm,), red_axis=0, out_shape=SDS((k, n), out_dtype), out_spec=BS((k, n), lambda i: (0, 0)),
               after=after)


POOL_CHUNK = 256
POOL_HALO = 8


def _window_sum(v, width, lead):
    n = v.shape[0]
    s = v
    k = 1
    while k < width:
        s = s + pltpu.roll(s, n - k, 0)
        k *= 2
    return pltpu.roll(s, lead, 0) if lead else s


def _pool_count(base, left, right, t, shape):
    pos = base + lax.broadcasted_iota(jnp.int32, shape, 0)
    lo = jnp.maximum(pos - left, 0)
    hi = jnp.minimum(pos + right + 1, t)
    return (hi - lo).astype(F32)


def _pool_fwd(proj, pool_w, pool_scale):
    t = proj.shape[0]
    c, h = POOL_CHUNK, POOL_HALO
    n_chunks = t // c

    def body(proj_hbm, pw_ref, sc_ref, pooled_ref, mixed_ref, ms_ref, pad_ref, sem):
        cp = pltpu.make_async_copy(proj_hbm.at[:, pl.ds(0, D_POOL)], pad_ref.at[pl.ds(h, t), :], sem)
        cp.start()
        pad_ref[pl.ds(0, h), :] = jnp.zeros((h, D_POOL), F32)
        pad_ref[pl.ds(t + h, h), :] = jnp.zeros((h, D_POOL), F32)
        cp.wait()
        for g, width in enumerate(POOL_WINDOWS):
            left = width // 2
            right = width - 1 - left
            cols = slice(g * POOL_GROUP, (g + 1) * POOL_GROUP)
            wmat = pw_ref[g].astype(BF16)
            scale = sc_ref[:, cols]

            def chunk(ci, carry, left=left, right=right, width=width, cols=cols, wmat=wmat, scale=scale):
                base = pl.multiple_of(ci * c, c)
                v = pad_ref[pl.ds(base, c + 2 * h), cols]
                win = _window_sum(v, width, left)[h:h + c]
                cnt = _pool_count(base, left, right, t, (c, POOL_GROUP))
                pooled = (win / cnt - v[h:h + c]).astype(BF16)
                mixed = _dot(pooled, wmat)
                pooled_ref[pl.ds(base, c), cols] = pooled
                mixed_ref[pl.ds(base, c), cols] = mixed.astype(BF16)
                ms_ref[pl.ds(base, c), cols] = (mixed * scale).astype(BF16)
                return carry

            lax.fori_loop(0, n_chunks, chunk, 0)

    vm = BS(memory_space=pltpu.VMEM)
    shape = SDS((t, D_POOL), BF16)
    return pl.pallas_call(
        body, out_shape=(shape, shape, shape),
        in_specs=[BS(memory_space=pl.ANY), vm, vm], out_specs=(vm, vm, vm),
        scratch_shapes=[pltpu.VMEM((t + 2 * h, D_POOL), F32), pltpu.SemaphoreType.DMA],
        name="pool_fwd", compiler_params=_params())(proj, pool_w, pool_scale)


def _pool_bwd(d_ms, mixed, pooled, pool_w, pool_scale):
    t = d_ms.shape[0]
    c, h = POOL_CHUNK, POOL_HALO
    n_chunks = t // c

    def body(dms_ref, mixed_ref, pooled_ref, pw_ref, sc_ref, dp_ref, dsc_ref, dpw_ref, pad_ref):
        pad_ref[pl.ds(0, h), :] = jnp.zeros((h, D_POOL), F32)
        pad_ref[pl.ds(t + h, h), :] = jnp.zeros((h, D_POOL), F32)
        for g, width in enumerate(POOL_WINDOWS):
            left = width // 2
            right = width - 1 - left
            cols = slice(g * POOL_GROUP, (g + 1) * POOL_GROUP)
            wmat = pw_ref[g].astype(BF16)
            scale = sc_ref[:, cols]

            def first(ci, carry, left=left, right=right, cols=cols, wmat=wmat, scale=scale):
                dsc, dpw = carry
                base = pl.multiple_of(ci * c, c)
                dms = dms_ref[pl.ds(base, c), cols].astype(F32)
                dsc = dsc + jnp.sum(dms * mixed_ref[pl.ds(base, c), cols].astype(F32), axis=0, keepdims=True)
                dmix = (dms * scale).astype(BF16)
                dpw = dpw + _dot(pooled_ref[pl.ds(base, c), cols], dmix, TN)
                dpooled = _dot(dmix, wmat, NT)
                cnt = _pool_count(base, left, right, t, (c, POOL_GROUP))
                pad_ref[pl.ds(base + h, c), cols] = dpooled / cnt
                return dsc, dpw

            dsc, dpw = lax.fori_loop(0, n_chunks, first,
                                     (jnp.zeros((1, POOL_GROUP), F32), jnp.zeros((POOL_GROUP, POOL_GROUP), F32)))
            dsc_ref[:, cols] = dsc
            dpw_ref[g] = dpw

            def second(ci, carry, left=left, right=right, width=width, cols=cols):
                base = pl.multiple_of(ci * c, c)
                v = pad_ref[pl.ds(base, c + 2 * h), cols]
                win = _window_sum(v, width, right)[h:h + c]
                cnt = _pool_count(base, left, right, t, (c, POOL_GROUP))
                dp_ref[pl.ds(base, c), cols] = (win - v[h:h + c] * cnt).astype(BF16)
                return carry

            lax.fori_loop(0, n_chunks, second, 0)

    vm = BS(memory_space=pltpu.VMEM)
    return pl.pallas_call(
        body, out_shape=(SDS((t, D_POOL), BF16), SDS((1, D_POOL), F32), SDS((4, POOL_GROUP, POOL_GROUP), F32)),
        in_specs=[vm] * 5, out_specs=(vm, vm, vm),
        scratch_shapes=[pltpu.VMEM((t + 2 * h, D_POOL), F32)],
        name="pool_bwd", compiler_params=_params())(d_ms, mixed, pooled, pool_w, pool_scale)


def _ssm_disc(ar, ai, ldt, after=()):
    after_ops, after_specs = _after_operands(after)

    def body(ar_ref, ai_ref, ldt_ref, *rest):
        abr_ref, abi_ref, qr_ref, qi_ref = rest[len(after_ops):]
        a_r, a_i = ar_ref[...], ai_ref[...]
        dt = jnp.exp(ldt_ref[...])
        mag = jnp.exp(dt * a_r)
        ang = dt * a_i
        abr = mag * jnp.cos(ang)
        abi = mag * jnp.sin(ang)
        den = a_r * a_r + a_i * a_i
        nr = abr - 1.0
        abr_ref[...] = abr
        abi_ref[...] = abi
        qr_ref[...] = (nr * a_r + abi * a_i) / den
        qi_ref[...] = (abi * a_r - nr * a_i) / den

    vm = BS(memory_space=pltpu.VMEM)
    shape = SDS(ar.shape, F32)
    return pl.pallas_call(body, out_shape=(shape,) * 4, in_specs=[vm] * 3 + after_specs, out_specs=(vm,) * 4,
                          name="ssm_disc", compiler_params=_params())(ar, ai, ldt, *after_ops)


def _ssm_disc_bwd(ar, ai, ldt, d_abr, d_abi, d_qr, d_qi):
    def body(ar_ref, ai_ref, ldt_ref, gabr_ref, gabi_ref, gqr_ref, gqi_ref, dar_ref, dai_ref, dldt_ref):
        a_r, a_i = ar_ref[...], ai_ref[...]
        dt = jnp.exp(ldt_ref[...])
        mag = jnp.exp(dt * a_r)
        ang = dt * a_i
        cs, sn = jnp.cos(ang), jnp.sin(ang)
        abr, abi = mag * cs, mag * sn
        den = a_r * a_r + a_i * a_i
        nr = abr - 1.0
        qr = (nr * a_r + abi * a_i) / den
        qi = (abi * a_r - nr * a_i) / den
        gqr, gqi = gqr_ref[...], gqi_ref[...]
        g_nr_num = gqr / den
        g_ni_num = gqi / den
        g_den = -(gqr * qr + gqi * qi) / den
        g_nr = g_nr_num * a_r - g_ni_num * a_i
        g_abi = g_nr_num * a_i + g_ni_num * a_r
        d_ar = g_nr_num * nr + g_ni_num * abi + 2.0 * a_r * g_den
        d_ai = g_nr_num * abi - g_ni_num * nr + 2.0 * a_i * g_den
        g_abr = gabr_ref[...] + g_nr
        g_abi = gabi_ref[...] + g_abi
        g_mag = g_abr * cs + g_abi * sn
        g_ang = mag * (g_abi * cs - g_abr * sn)
        g_e = g_mag * mag
        d_ar = d_ar + g_e * dt
        d_ai = d_ai + g_ang * dt
        g_dt = g_e * a_r + g_ang * a_i
        dar_ref[...] = d_ar
        dai_ref[...] = d_ai
        dldt_ref[...] = jnp.sum(g_dt * dt, axis=1, keepdims=True)

    vm = BS(memory_space=pltpu.VMEM)
    return pl.pallas_call(body, out_shape=(SDS(ar.shape, F32), SDS(ar.shape, F32), SDS(ldt.shape, F32)),
                          in_specs=[vm] * 7, out_specs=(vm,) * 3, name="ssm_disc_bwd",
                          compiler_params=_params())(ar, ai, ldt, d_abr, d_abi, d_qr, d_qi)


def _ssm_bbar(qr, qi, br, bi):
    def body(qr_ref, qi_ref, br_ref, bi_ref, bbr_ref, bbi_ref):
        q_r, q_i, b_r, b_i = qr_ref[...], qi_ref[...], br_ref[...], bi_ref[...]
        bbr_ref[...] = q_r * b_r - q_i * b_i
        bbi_ref[...] = q_r * b_i + q_i * b_r

    vm = BS(memory_space=pltpu.VMEM)
    shape = SDS(br.shape, F32)
    return pl.pallas_call(body, out_shape=(shape, shape), in_specs=[vm] * 4, out_specs=(vm, vm),
                          name="ssm_bbar", compiler_params=_params())(qr, qi, br, bi)


def _ssm_bbar_bwd(qr, qi, br, bi, g_bbr, g_bbi):
    def body(qr_ref, qi_ref, br_ref, bi_ref, gr_ref, gi_ref, dqr_ref, dqi_ref, dbr_ref, dbi_ref):
        q_r, q_i, b_r, b_i = qr_ref[...], qi_ref[...], br_ref[...], bi_ref[...]
        g_r, g_i = gr_ref[...], gi_ref[...]
        dqr_ref[...] = jnp.sum(g_r * b_r + g_i * b_i, axis=1, keepdims=True)
        dqi_ref[...] = jnp.sum(g_i * b_r - g_r * b_i, axis=1, keepdims=True)
        dbr_ref[...] = g_r * q_r + g_i * q_i
        dbi_ref[...] = g_i * q_r - g_r * q_i

    vm = BS(memory_space=pltpu.VMEM)
    return pl.pallas_call(
        body, out_shape=(SDS(qr.shape, F32), SDS(qr.shape, F32), SDS(br.shape, F32), SDS(br.shape, F32)),
        in_specs=[vm] * 6, out_specs=(vm,) * 4, name="ssm_bbar_bwd",
        compiler_params=_params())(qr, qi, br, bi, g_bbr, g_bbi)


SCAN_ROWS = 512


def _ssm_scan(name, inp, w1, a_r, a_i, w2, reverse):
    t = inp.shape[0]
    rows = min(SCAN_ROWS, t)
    n = t // rows
    n_groups = rows // 8
    ch = SSM_CH
    at = (lambda i: (n - 1 - i, 0)) if reverse else (lambda i: (i, 0))

    def body(in_ref, w1_ref, ar_ref, ai_ref, w2_ref, sb_ref, out_ref, cr_ref, ci_ref, k_ref, st_ref):
        i = pl.program_id(0)

        @pl.when(i == 0)
        def _():
            ar8 = jnp.broadcast_to(ar_ref[...], (8, ch))
            ai8 = jnp.broadcast_to(ai_ref[...], (8, ch))
            row = lax.broadcasted_iota(jnp.int32, (8, ch), 0)
            rank = (7 - row) if reverse else row
            powers = [(ar8, ai8)]
            for _ in range(7):
                p_r, p_i = powers[-1]
                powers.append((p_r * ar8 - p_i * ai8, p_r * ai8 + p_i * ar8))
            zero = jnp.zeros((8, ch), F32)
            for slot, k in enumerate((1, 2, 4)):
                k_ref[2 * slot] = jnp.where(rank >= k, powers[k - 1][0], zero)
                k_ref[2 * slot + 1] = jnp.where(rank >= k, powers[k - 1][1], zero)
            carry_r, carry_i = zero, zero
            for j in range(8):
                carry_r = jnp.where(rank == j, powers[j][0], carry_r)
                carry_i = jnp.where(rank == j, powers[j][1], carry_i)
            k_ref[6] = carry_r
            k_ref[7] = carry_i
            cr_ref[...] = zero
            ci_ref[...] = zero

        st_ref[...] = _dot(in_ref[...], w1_ref[...])

        def group(gi, carry):
            c_r, c_i = carry
            g = (n_groups - 1 - gi) if reverse else gi
            r0 = pl.multiple_of(g * 8, 8)
            x_r = st_ref[pl.ds(r0, 8), 0:ch]
            x_i = st_ref[pl.ds(r0, 8), ch:2 * ch]
            for slot, k in enumerate((1, 2, 4)):
                shift = (8 - k) if reverse else k
                s_r = pltpu.roll(x_r, shift, 0)
                s_i = pltpu.roll(x_i, shift, 0)
                m_r, m_i = k_ref[2 * slot], k_ref[2 * slot + 1]
                x_r, x_i = x_r + m_r * s_r - m_i * s_i, x_i + m_r * s_i + m_i * s_r
            p_r, p_i = k_ref[6], k_ref[7]
            x_r, x_i = x_r + p_r * c_r - p_i * c_i, x_i + p_r * c_i + p_i * c_r
            st_ref[pl.ds(r0, 8), 0:ch] = x_r
            st_ref[pl.ds(r0, 8), ch:2 * ch] = x_i
            last = 0 if reverse else 7
            return (jnp.broadcast_to(x_r[last:last + 1, :], (8, ch)), jnp.broadcast_to(x_i[last:last + 1, :], (8, ch)))

        c_r, c_i = lax.fori_loop(0, n_groups, group, (cr_ref[...], ci_ref[...]))
        cr_ref[...] = c_r
        ci_ref[...] = c_i
        states = st_ref[...].astype(BF16)
        sb_ref[...] = states
        out_ref[...] = _dot(states, w2_ref[...])

    return pl.pallas_call(
        body, out_shape=(SDS((t, 2 * ch), BF16), SDS((t, D_SSM), F32)), grid=(n,),
        in_specs=[BS((rows, D_SSM), at), BS((D_SSM, 2 * ch), lambda i: (0, 0)), BS((1, ch), lambda i: (0, 0)),
                  BS((1, ch), lambda i: (0, 0)), BS((2 * ch, D_SSM), lambda i: (0, 0))],
        out_specs=(BS((rows, 2 * ch), at), BS((rows, D_SSM), at)),
        scratch_shapes=[pltpu.VMEM((8, ch), F32), pltpu.VMEM((8, ch), F32), pltpu.VMEM((8, 8, ch), F32),
                        pltpu.VMEM((rows, 2 * ch), F32)],
        name=name, compiler_params=_params("arbitrary"))(inp, w1, a_r, a_i, w2)


DA_ROWS = 1024


def _ssm_da(name, lam, states, reverse, after=()):
    t = lam.shape[0]
    rows = min(DA_ROWS, t)
    n = t // rows
    halo_rows = 16
    nb = rows // halo_rows
    ch = SSM_CH
    if reverse:
        halo_at = lambda i: (jnp.minimum((i + 1) * nb, t // halo_rows - 1), 0)
    else:
        halo_at = lambda i: (jnp.maximum(i * nb - 1, 0), 0)

    after_ops, after_specs = _after_operands(after)

    def body(lam_ref, x_ref, halo_ref, *rest):
        dr_ref, di_ref = rest[len(after_ops):]
        i = pl.program_id(0)

        @pl.when(i == 0)
        def _():
            dr_ref[...] = jnp.zeros_like(dr_ref)
            di_ref[...] = jnp.zeros_like(di_ref)

        row = lax.broadcasted_iota(jnp.int32, (rows, ch), 0)
        if reverse:
            edge, shift, h_row, live = rows - 1, rows - 1, 0, i < n - 1
        else:
            edge, shift, h_row, live = 0, 1, halo_rows - 1, i > 0

        def neighbour(lo):
            halo = halo_ref[:, lo:lo + ch].astype(F32)[h_row:h_row + 1]
            halo = jnp.where(live, halo, 0.0)
            x = x_ref[:, lo:lo + ch].astype(F32)
            return jnp.where(row == edge, jnp.broadcast_to(halo, (rows, ch)), pltpu.roll(x, shift, 0))

        xp_r, xp_i = neighbour(0), neighbour(ch)
        l_r, l_i = lam_ref[:, 0:ch].astype(F32), lam_ref[:, ch:2 * ch].astype(F32)
        dr_ref[...] += jnp.sum(l_r * xp_r + l_i * xp_i, axis=0, keepdims=True)
        di_ref[...] += jnp.sum(l_i * xp_r - l_r * xp_i, axis=0, keepdims=True)

    blk = BS((rows, 2 * ch), lambda i: (i, 0))
    vec = BS((1, ch), lambda i: (0, 0))
    return pl.pallas_call(
        body, out_shape=(SDS((1, ch), F32), SDS((1, ch), F32)), grid=(n,),
        in_specs=[blk, blk, BS((halo_rows, 2 * ch), halo_at)] + after_specs, out_specs=(vec, vec),
        name=name, compiler_params=_params("arbitrary"))(lam, states, states, *after_ops)


GELU_C = math.sqrt(2.0 / math.pi)
GELU_K = 0.044715


def _ssm_combine(proj, y_fwd, y_bwd, d_skip, tm, after=()):
    t = proj.shape[0]
    after_ops, after_specs = _after_operands(after)

    def body(s_ref, yf_ref, yb_ref, d_ref, *rest):
        yt_ref, g_ref = rest[len(after_ops):]
        y = s_ref[...] * d_ref[...] + yf_ref[...] + yb_ref[...]
        yt_ref[...] = y
        th = jnp.tanh(GELU_C * (y + GELU_K * y * y * y))
        g_ref[...] = (0.5 * y * (1.0 + th)).astype(BF16)

    blk = BS((tm, D_SSM), lambda i: (i, 0))
    return pl.pallas_call(
        body, out_shape=(SDS((t, D_SSM), F32), SDS((t, D_SSM), BF16)), grid=(t // tm,),
        in_specs=[BS((tm, D_SSM), lambda i: (i, D_POOL // D_SSM)), blk, blk, BS((1, D_SSM), lambda i: (0, 0))] + after_specs,
        out_specs=(blk, blk), name="ssm_combine",
        compiler_params=_params("parallel"))(proj, y_fwd, y_bwd, d_skip, *after_ops)


def _ssm_ds(proj, d_yt, du_fwd, du_bwd, d_skip, tm):
    t = proj.shape[0]

    def body(s_ref, dy_ref, duf_ref, dub_ref, d_ref, ds_ref, dd_ref):
        i = pl.program_id(0)
        dy = dy_ref[...]
        ds_ref[...] = (dy * d_ref[...] + duf_ref[...] + dub_ref[...]).astype(BF16)

        @pl.when(i == 0)
        def _():
            dd_ref[...] = jnp.zeros_like(dd_ref)

        dd_ref[...] += jnp.sum(dy * s_ref[...], axis=0, keepdims=True)

    blk = BS((tm, D_SSM), lambda i: (i, 0))
    vec = BS((1, D_SSM), lambda i: (0, 0))
    return pl.pallas_call(
        body, out_shape=(SDS((t, D_SSM), BF16), SDS((1, D_SSM), F32)), grid=(t // tm,),
        in_specs=[BS((tm, D_SSM), lambda i: (i, D_POOL // D_SSM)), blk, blk, blk, vec],
        out_specs=(blk, vec), name="ssm_ds", compiler_params=_params("arbitrary"))(proj, d_yt, du_fwd, du_bwd, d_skip)


GP_BLOCK = (D_POOL + D_SSM) // 256
GS_BLOCK = GP_BLOCK + D_MODEL // 256


def _merge_specs(tm):
    return [BS((tm, D_POOL), lambda s, i: (i, 0)), BS((tm, D_SSM), lambda s, i: (i, 0)),
            BS((None, D_POOL, 256), lambda s, i: (s, 0, 0)), BS((None, D_SSM, 256), lambda s, i: (s, 2, 0)),
            BS((None, D_SSM, 256), lambda s, i: (s, 3, 0)),
            BS((tm, 256), lambda s, i: (i, GP_BLOCK + s)), BS((tm, 256), lambda s, i: (i, GS_BLOCK + s))]


def _mixer_merge(ms, yssm, w_e, proj, tm):
    t = ms.shape[0]

    def body(ms_ref, y_ref, wpp_ref, wgv_ref, wgg_ref, gp_ref, gs_ref, o_ref):
        zp = _dot(ms_ref[...], wpp_ref[...])
        yv = y_ref[...]
        zv = _dot(yv, wgv_ref[...])
        zg = _dot(yv, wgg_ref[...])
        o_ref[...] = (_sigmoid(gp_ref[...]) * zp + _sigmoid(gs_ref[...]) * zv * _sigmoid(zg)).astype(BF16)

    col = BS((tm, 256), lambda s, i: (i, s))
    return pl.pallas_call(
        body, out_shape=SDS((t, D_MODEL), BF16), grid=(N_SHARD, t // tm), in_specs=_merge_specs(tm), out_specs=col,
        name="mixer_merge", compiler_params=_params("parallel", "parallel"))(ms, yssm, w_e, w_e, w_e, proj, proj)


def _mixer_merge_bwd(ms, yssm, w_e, proj, dmerged, tm):
    t = ms.shape[0]

    def body(ms_ref, y_ref, wpp_ref, wgv_ref, wgg_ref, gp_ref, gs_ref, dm_ref,
             dgp_ref, dgs_ref, dzp_ref, dzv_ref, dzg_ref):
        zp = _dot(ms_ref[...], wpp_ref[...])
        yv = y_ref[...]
        zv = _dot(yv, wgv_ref[...])
        zg = _dot(yv, wgg_ref[...])
        dm = dm_ref[...].astype(F32)
        sp, ss, sg = _sigmoid(gp_ref[...]), _sigmoid(gs_ref[...]), _sigmoid(zg)
        dgp_ref[...] = (dm * zp * sp * (1.0 - sp)).astype(BF16)
        dgs_ref[...] = (dm * zv * sg * ss * (1.0 - ss)).astype(BF16)
        dzp_ref[...] = (dm * sp).astype(BF16)
        dz = dm * ss
        dzv_ref[...] = (dz * sg).astype(BF16)
        dzg_ref[...] = (dz * zv * sg * (1.0 - sg)).astype(BF16)

    col = BS((tm, 256), lambda s, i: (i, s))
    shape = SDS((t, D_MODEL), BF16)
    return pl.pallas_call(
        body, out_shape=(shape,) * 5, grid=(N_SHARD, t // tm), in_specs=_merge_specs(tm) + [col],
        out_specs=(col,) * 5, name="mixer_merge_bwd",
        compiler_params=_params("parallel", "parallel"))(ms, yssm, w_e, w_e, w_e, proj, proj, dmerged)


def _mixer_dw(ms, yssm, dzp, dzv, dzg, tm):
    t = ms.shape[0]
    n_t = t // tm

    def body(ms_ref, y_ref, dzp_ref, dzv_ref, dzg_ref, o_ref, acc):
        i = pl.program_id(1)

        @pl.when(i == 0)
        def _():
            acc[...] = jnp.zeros_like(acc)

        yv = y_ref[...]
        acc[0:D_POOL, :] += _dot(ms_ref[...], dzp_ref[...], TN)
        acc[D_POOL:D_POOL + D_SSM, :] += _dot(yv, dzv_ref[...], TN)
        acc[D_POOL + D_SSM:, :] += _dot(yv, dzg_ref[...], TN)

        @pl.when(i == n_t - 1)
        def _():
            o_ref[...] = acc[...].astype(BF16)

    col = BS((tm, 256), lambda s, i: (i, s))
    return pl.pallas_call(
        body, out_shape=SDS((N_SHARD, 1024, 256), BF16), grid=(N_SHARD, n_t),
        in_specs=[BS((tm, D_POOL), lambda s, i: (i, 0)), BS((tm, D_SSM), lambda s, i: (i, 0)), col, col, col],
        out_specs=BS((None, 1024, 256), lambda s, i: (s, 0, 0)), scratch_shapes=[pltpu.VMEM((1024, 256), F32)],
        name="mixer_dw", compiler_params=_params("parallel", "arbitrary"))(ms, yssm, dzp, dzv, dzg)


def _mixer_dx(dzp, dzv, dzg, w_e, y_total, tm):
    t = dzp.shape[0]

    def body(dzp_ref, dzv_ref, dzg_ref, wpp_ref, wgv_ref, wgg_ref, yt_ref, dms_ref, dy_ref, acc_ms, acc_y):
        s = pl.program_id(1)

        @pl.when(s == 0)
        def _():
            acc_ms[...] = jnp.zeros_like(acc_ms)
            acc_y[...] = jnp.zeros_like(acc_y)

        acc_ms[...] += _dot(dzp_ref[...], wpp_ref[...], NT)
        acc_y[...] += _dot(dzv_ref[...], wgv_ref[...], NT) + _dot(dzg_ref[...], wgg_ref[...], NT)

        @pl.when(s == N_SHARD - 1)
        def _():
            dms_ref[...] = acc_ms[...].astype(BF16)
            y = yt_ref[...]
            inner = GELU_C * (y + GELU_K * y * y * y)
            th = jnp.tanh(inner)
            dgelu = 0.5 * (1.0 + th) + 0.5 * y * (1.0 - th * th) * GELU_C * (1.0 + 3.0 * GELU_K * y * y)
            dy_ref[...] = acc_y[...] * dgelu

    col = BS((tm, 256), lambda i, s: (i, s))
    return pl.pallas_call(
        body, out_shape=(SDS((t, D_POOL), BF16), SDS((t, D_SSM), F32)), grid=(t // tm, N_SHARD),
        in_specs=[col, col, col, BS((None, D_POOL, 256), lambda i, s: (s, 0, 0)),
                  BS((None, D_SSM, 256), lambda i, s: (s, 2, 0)), BS((None, D_SSM, 256), lambda i, s: (s, 3, 0)),
                  BS((tm, D_SSM), lambda i, s: (i, 0))],
        out_specs=(BS((tm, D_POOL), lambda i, s: (i, 0)), BS((tm, D_SSM), lambda i, s: (i, 0))),
        scratch_shapes=[pltpu.VMEM((tm, D_POOL), F32), pltpu.VMEM((tm, D_SSM), F32)],
        name="mixer_dx", compiler_params=_params("parallel", "arbitrary"))(dzp, dzv, dzg, w_e, w_e, w_e, y_total)


def _attn_probs(q_h, k_h):
    s = _dot(q_h, k_h, NT) * (1.0 / math.sqrt(HEAD_DIM))
    e = jnp.exp(s - jnp.max(s, axis=-1, keepdims=True))
    return e / jnp.sum(e, axis=-1, keepdims=True)


def _attn_fwd(q, kv, tm):
    t = q.shape[0]
    m = kv.shape[0]

    def body(q_ref, kv_ref, o_ref):
        for hd in range(N_HEADS):
            lo = hd * HEAD_DIM
            p = _attn_probs(q_ref[:, lo:lo + HEAD_DIM], kv_ref[:, lo:lo + HEAD_DIM])
            o_ref[:, lo:lo + HEAD_DIM] = _dot(p, kv_ref[:, D_MODEL + lo:D_MODEL + lo + HEAD_DIM]).astype(BF16)

    return pl.pallas_call(
        body, out_shape=SDS((t, D_MODEL), BF16), grid=(t // tm,),
        in_specs=[BS((tm, D_MODEL), lambda i: (i, 0)), BS((m, 2 * D_MODEL), lambda i: (0, 0))],
        out_specs=BS((tm, D_MODEL), lambda i: (i, 0)), name="attn_fwd", compiler_params=_params("parallel"))(q, kv)


def _attn_bwd(q, kv, d_o, tm):
    t = q.shape[0]
    m = kv.shape[0]

    def body(q_ref, kv_ref, do_ref, dq_ref, dkv_ref):
        i = pl.program_id(0)

        @pl.when(i == 0)
        def _():
            dkv_ref[...] = jnp.zeros_like(dkv_ref)

        for hd in range(N_HEADS):
            lo = hd * HEAD_DIM
            q_h = q_ref[:, lo:lo + HEAD_DIM]
            k_h = kv_ref[:, lo:lo + HEAD_DIM]
            v_h = kv_ref[:, D_MODEL + lo:D_MODEL + lo + HEAD_DIM]
            do_h = do_ref[:, lo:lo + HEAD_DIM]
            p = _attn_probs(q_h, k_h)
            dkv_ref[:, D_MODEL + lo:D_MODEL + lo + HEAD_DIM] += _dot(p, do_h, TN)
            dp = _dot(do_h, v_h, NT)
            ds = p * (dp - jnp.sum(dp * p, axis=-1, keepdims=True)) * (1.0 / math.sqrt(HEAD_DIM))
            dq_ref[:, lo:lo + HEAD_DIM] = _dot(ds, k_h).astype(BF16)
            dkv_ref[:, lo:lo + HEAD_DIM] += _dot(ds, q_h, TN)

    row = BS((tm, D_MODEL), lambda i: (i, 0))
    full = BS((m, 2 * D_MODEL), lambda i: (0, 0))
    return pl.pallas_call(
        body, out_shape=(SDS((t, D_MODEL), BF16), SDS((m, 2 * D_MODEL), F32)), grid=(t // tm,),
        in_specs=[row, full, row], out_specs=(row, full), name="attn_bwd",
        compiler_params=_params("arbitrary"))(q, kv, d_o)


TRANSPOSED = ("ffn1_w_gate", "ffn1_w_up", "ffn2_w_gate", "ffn2_w_up", "w_in")
GATHER_PHASES = {"f1a": (("ffn1_w_gate", "ffn1_w_up"),),
                 "f1b": (("ffn1_w_down",),),
                 "win": (("w_in",),),
                 "mix": (("w_mix_out", "w_q", "w_xo"), ("w_kv",), ("w_pool_proj", "w_glu_val", "w_glu_gate")),
                 "f2": (("ffn2_w_gate", "ffn2_w_up", "ffn2_w_down"),)}
REDUCE_GROUPS = (("ffn2_w_gate", "ffn2_w_up", "ffn2_w_down"), ("w_xo",), ("w_q",), ("w_kv",), ("w_mix_out",),
                 ("w_pool_proj", "w_glu_val", "w_glu_gate"), ("w_in",), ("ffn1_w_gate", "ffn1_w_up", "ffn1_w_down"))
SMALL = ("ffn1_norm", "mix_norm", "pool_w", "pool_scale", "ssm_a_re", "ssm_a_im", "ssm_log_dt", "ssm_b_re",
         "ssm_b_im", "ssm_c_re", "ssm_c_im", "ssm_d", "xattn_norm", "mem_norm", "ffn2_norm", "final_norm")
WEIGHTS = ("ffn1_norm", "ffn1_w_gate", "ffn1_w_up", "ffn1_w_down", "mix_norm", "w_in", "pool_w", "pool_scale",
           "w_pool_proj", "ssm_a_re", "ssm_a_im", "ssm_log_dt", "ssm_b_re", "ssm_b_im", "ssm_c_re", "ssm_c_im",
           "ssm_d", "w_glu_val", "w_glu_gate", "w_mix_out", "xattn_norm", "mem_norm", "w_q", "w_kv", "w_xo",
           "ffn2_norm", "ffn2_w_gate", "ffn2_w_up", "ffn2_w_down", "final_norm")


def _block_diag_in(bb):
    eye = jnp.eye(SSM_GROUPS, dtype=bb.dtype)
    return jnp.einsum("dgph,gk->dghkp", bb, eye).reshape(2, D_SSM, SSM_CH)


def _block_diag_out(cc):
    eye = jnp.eye(SSM_GROUPS, dtype=cc.dtype)
    return jnp.einsum("dghp,gk->dgpkh", cc, eye).reshape(2, SSM_CH, D_SSM)


def _diag_blocks_in(m):
    return jnp.einsum("dghgp->dgph", m.reshape(2, SSM_GROUPS, SSM_GROUP, SSM_GROUPS, SSM_STATE))


def _diag_blocks_out(m):
    return jnp.einsum("dgpgh->dghp", m.reshape(2, SSM_GROUPS, SSM_STATE, SSM_GROUPS, SSM_GROUP))


def _device_step(x, mem, target, wts, sp, reducer=None):
    t = x.shape[0]
    tm = min(TM, t)
    g = {}

    first_gather = wts.start("f1a")
    u1 = _rmsnorm("norm_ffn1", x, sp["ffn1_norm"], tm, after=first_gather)

    ar = sp["ssm_a_re"].reshape(2 * SSM_GROUPS, SSM_STATE)
    ai = sp["ssm_a_im"].reshape(2 * SSM_GROUPS, SSM_STATE)
    ldt = sp["ssm_log_dt"].reshape(2 * SSM_GROUPS, 1)
    abr, abi, qr, qi = _ssm_disc(ar, ai, ldt, after=first_gather)
    b_r = sp["ssm_b_re"].reshape(2 * SSM_CH, SSM_GROUP)
    b_i = sp["ssm_b_im"].reshape(2 * SSM_CH, SSM_GROUP)
    qr_col, qi_col = qr.reshape(2 * SSM_CH, 1), qi.reshape(2 * SSM_CH, 1)
    bbr, bbi = _ssm_bbar(qr_col, qi_col, b_r, b_i)
    shape_b = (2, SSM_GROUPS, SSM_STATE, SSM_GROUP)
    b_mat = jnp.concatenate([_block_diag_in(bbr.reshape(shape_b)), _block_diag_in(bbi.reshape(shape_b))], axis=-1).astype(BF16)
    c_mat = jnp.concatenate([_block_diag_out(sp["ssm_c_re"][0]), -_block_diag_out(sp["ssm_c_im"][0])], axis=1).astype(BF16)
    b_mat_t = jnp.swapaxes(b_mat, 1, 2)
    c_mat_t = jnp.swapaxes(c_mat, 1, 2)
    a_r = abr.reshape(2, 1, SSM_CH)
    a_i = abi.reshape(2, 1, SSM_CH)
    mem_n = _rmsnorm("norm_mem", mem, sp["mem_norm"], mem.shape[0], after=first_gather)

    (w_gu,) = wts.finish("f1a", [u1, b_mat, c_mat, b_mat_t, c_mat_t, mem_n])
    w_f1 = {"gate": (w_gu, FFN_GATE), "up": (w_gu, FFN_UP)}
    down_gather = wts.start("f1b", [w_gu])
    g1, up1, a1 = _ffn_up("ffn1_up", u1, w_f1, tm, after=wts.start("win", down_gather))
    (w_dn,) = wts.finish("f1b", [a1])
    w_f1["down"] = (w_dn, 0)
    h1, u2 = _ffn_down("ffn1_down", a1, w_f1, x, tm, next_gain=sp["mix_norm"])

    (w_in_g,) = wts.finish("win", [u2])
    w_in_t = w_in_g.reshape(D_FF, D_MODEL)
    proj = _mm("mix_in", [(u2, BS((tm, D_MODEL), lambda j, i: (i, 0)), w_in_t, BS((D_FF // 2, D_MODEL), lambda j, i: (j, 0)), NT)],
               grid=(2, t // tm), out_shape=SDS((t, D_FF), F32), out_spec=BS((tm, D_FF // 2), lambda j, i: (i, j)),
               after=wts.start("f2", wts.start("mix", [w_in_g])))
    pooled, mixed, ms = _pool_fwd(proj, sp["pool_w"][0], sp["pool_scale"])

    s_in = proj[:, D_POOL:D_POOL + D_SSM].astype(BF16)
    states, y_dirs = [], []
    for dr in range(2):
        st, yd = _ssm_scan(f"ssm_scan_fwd{dr}", s_in, b_mat[dr], a_r[dr], a_i[dr], c_mat[dr], reverse=(dr == 1))
        states.append(st)
        y_dirs.append(yd)
    w_sq, w_kv, w_e = wts.finish("mix", y_dirs)
    w_mo, w_q, w_xo = (w_sq[:, 256 * k:256 * (k + 1)].reshape(D_MODEL, D_MODEL) for k in range(3))
    w_d = w_kv[:, None]
    y_total, yssm = _ssm_combine(proj, y_dirs[0], y_dirs[1], sp["ssm_d"], tm)

    merged = _mixer_merge(ms, yssm, w_e, proj, tm)
    h2, u3 = _mm_resid_norm("mix_out", merged, w_mo, h1, sp["xattn_norm"], tm)

    q = _plain_mm("attn_q", u3, w_q, NN, BF16, tm)
    n_mem = mem.shape[0]
    kv = _mm("attn_kv", [(mem_n, BS((n_mem, D_MODEL), lambda s: (0, 0)), w_d, BS((None, None, D_MODEL, 512), lambda s: (s, 0, 0, 0)), NN)],
             grid=(N_SHARD,), out_shape=SDS((n_mem, 2 * D_MODEL), BF16), out_spec=BS((n_mem, 512), lambda s: (0, s)))
    o = _attn_fwd(q, kv, tm)
    h3, u4 = _mm_resid_norm("attn_out", o, w_xo, h2, sp["ffn2_norm"], tm)

    (w_2,) = wts.finish("f2", [u4])
    w_f2 = {"gate": (w_2, FFN_GATE), "up": (w_2, FFN_UP), "down": (w_2, FFN_DOWN)}
    g2, up2, a2 = _ffn_up("ffn2_up", u4, w_f2, tm)
    loss, dh4, dh4_b, g["final_norm"] = _ffn_down("ffn2_down", a2, w_f2, h3, tm,
                                                  head=(sp["final_norm"].reshape(1, D_MODEL), target))

    dg2, dup2 = _ffn_bwd_act("ffn2_bwd_act", dh4_b, w_f2, g2, up2, tm)
    dw_f2 = _ffn_dw("ffn2_dw", u4, dg2, dup2, a2, dh4_b, tm)
    dh3, dh3_b, g["ffn2_norm"] = _ffn_dx("ffn2_dx", dg2, dup2, w_f2, h3, sp["ffn2_norm"], dh4, tm)

    d_o = _plain_mm("attn_out_dx", dh3_b, w_xo, NT, BF16, tm)
    dw_xo = _dw_mm("attn_out_dw", o, dh3_b, tm)
    dq, dkv = _attn_bwd(q, kv, d_o, tm)
    dw_q = _dw_mm("attn_q_dw", u3, dq, tm)
    dh2, dh2_b, g["xattn_norm"] = _mm_norm_bwd("attn_q_dx", dq, w_q, NT, h2, sp["xattn_norm"], dh3, tm)
    dw_kv = _mm("attn_kv_dw", [(mem_n, BS((n_mem, D_MODEL), lambda s: (0, 0)), dkv, BS((n_mem, 512), lambda s: (0, s)), TN)],
                grid=(N_SHARD,), out_shape=SDS((N_SHARD, D_MODEL, 512), BF16), out_spec=BS((None, D_MODEL, 512), lambda s: (s, 0, 0)))
    dmem_n = _mm("attn_kv_dx", [(dkv, BS((n_mem, 512), lambda s: (0, s)), w_d, BS((None, None, D_MODEL, 512), lambda s: (s, 0, 0, 0)), NT)],
                 grid=(N_SHARD,), red_axis=0, out_shape=SDS((n_mem, D_MODEL), F32), out_spec=BS((n_mem, D_MODEL), lambda s: (0, 0)))
    _, _, g["mem_norm"] = _rmsnorm_bwd("norm_mem_bwd", mem, sp["mem_norm"], dmem_n, None, n_mem)

    square = (N_SHARD, D_MODEL // N_SHARD, D_MODEL)
    early = [dw_f2.reshape(N_SHARD, 3 * FF_SH, D_MODEL), dw_xo.reshape(square), dw_q.reshape(square), dw_kv]
    swapping = reducer.swap_start("a1", early) if reducer is not None else []
    dmerged = _plain_mm("mix_out_dx", dh2_b, w_mo, NT, BF16, tm, after=swapping)
    dw_mo = _dw_mm("mix_out_dw", merged, dh2_b, tm)
    d_gp, d_gs, dzp, dzv, dzg = _mixer_merge_bwd(ms, yssm, w_e, proj, dmerged, tm)
    dw_e = _mixer_dw(ms, yssm, dzp, dzv, dzg, tm)
    d_ms, d_yt = _mixer_dx(dzp, dzv, dzg, w_e, y_total, tm)
    dp, d_scale, d_pw = _pool_bwd(d_ms, mixed, pooled, sp["pool_w"][0], sp["pool_scale"])
    g["pool_scale"] = d_scale
    g["pool_w"] = d_pw[None]

    d_yt_b = d_yt.astype(BF16)
    du_dirs, lams = [], []
    for dr in range(2):
        lam, du = _ssm_scan(f"ssm_scan_bwd{dr}", d_yt_b, c_mat_t[dr], a_r[dr], -a_i[dr], b_mat_t[dr], reverse=(dr == 0))
        du_dirs.append(du)
        lams.append(lam)
    ds, g["ssm_d"] = _ssm_ds(proj, d_yt, du_dirs[0], du_dirs[1], sp["ssm_d"], tm)

    d_proj = jnp.concatenate([dp, ds, d_gp, d_gs], axis=1)
    dw_in_t = _mm("mix_in_dw", [(d_proj, BS((tm, D_FF // 2), lambda j, i: (i, j)), u2, BS((tm, D_MODEL), lambda j, i: (i, 0)), TN)],
                  grid=(2, t // tm), red_axis=1, out_shape=SDS((D_FF, D_MODEL), BF16), out_spec=BS((D_FF // 2, D_MODEL), lambda j, i: (j, 0)))
    dh1, dh1_b, g["mix_norm"] = _mm_norm_bwd("mix_in_dx", d_proj, w_in_t, NN, h1, sp["mix_norm"], dh2, tm)

    early += [dw_mo.reshape(square), dw_e, dw_in_t.reshape(N_SHARD, FF_SH, D_MODEL)]
    g["final_norm"] = g["final_norm"].reshape(D_MODEL)

    travelling = reducer.start("a", early[4:], swapped=["a1"], after=list(g.values())) if reducer is not None else []
    d_abr, d_abi, d_cm, d_bm = [], [], [], []
    for dr in range(2):
        da_r, da_i = _ssm_da(f"ssm_da{dr}", lams[dr], states[dr], reverse=(dr == 1), after=travelling)
        d_abr.append(da_r)
        d_abi.append(da_i)
        d_cm.append(_dw_mm(f"ssm_dc{dr}", states[dr], d_yt_b, tm, F32, after=travelling))
        d_bm.append(_dw_mm(f"ssm_db{dr}", s_in, lams[dr], tm, F32, after=travelling))
    d_cm = jnp.stack(d_cm)
    d_bm = jnp.stack(d_bm)
    g["ssm_c_re"] = _diag_blocks_out(d_cm[:, :SSM_CH])[None]
    g["ssm_c_im"] = -_diag_blocks_out(d_cm[:, SSM_CH:])[None]
    g_bbr = _diag_blocks_in(d_bm[:, :, :SSM_CH]).reshape(2 * SSM_CH, SSM_GROUP)
    g_bbi = _diag_blocks_in(d_bm[:, :, SSM_CH:]).reshape(2 * SSM_CH, SSM_GROUP)
    d_qr, d_qi, d_br, d_bi = _ssm_bbar_bwd(qr_col, qi_col, b_r, b_i, g_bbr, g_bbi)
    g["ssm_b_re"] = d_br.reshape(sp["ssm_b_re"].shape)
    g["ssm_b_im"] = d_bi.reshape(sp["ssm_b_im"].shape)
    d_ar, d_ai, d_ldt = _ssm_disc_bwd(ar, ai, ldt, jnp.stack(d_abr).reshape(ar.shape), jnp.stack(d_abi).reshape(ar.shape),
                                      d_qr.reshape(ar.shape), d_qi.reshape(ar.shape))
    g["ssm_a_re"] = d_ar.reshape(sp["ssm_a_re"].shape)
    g["ssm_a_im"] = d_ai.reshape(sp["ssm_a_im"].shape)
    g["ssm_log_dt"] = d_ldt.reshape(sp["ssm_log_dt"].shape)
    if reducer is not None:
        travelling = travelling + [d_ar, d_ai, d_ldt, d_br, d_bi, d_cm]
    dg1, dup1 = _ffn_bwd_act("ffn1_bwd_act", dh1_b, w_f1, g1, up1, tm, after=travelling)
    dw_f1 = _ffn_dw("ffn1_dw", u1, dg1, dup1, a1, dh1_b, tm).reshape(N_SHARD, 3 * FF_SH, D_MODEL)
    if reducer is not None:
        travelling = reducer.start("b", [dw_f1], after=reducer.finish("a", [dw_f1]))
        travelling = travelling + reducer.join_start("a", after=travelling)
    grad_x, _, g["ffn1_norm"] = _ffn_dx("ffn1_dx", dg1, dup1, w_f1, x, sp["ffn1_norm"], dh1, tm, after=travelling)
    if reducer is not None:
        reducer.finish("b", [grad_x])
        reducer.join_finish("a", [grad_x])
    return loss, grad_x, early + [dw_f1], g


def _mesh_place():
    x, y, c = lax.axis_index("x"), lax.axis_index("y"), lax.axis_index("c")
    chips = [(1 - x, y), (x, 1 - y), (1 - x, 1 - y)]
    return x, y, c, chips


def _remote(src, dst, send_sems, recv_sems, k, to):
    return pltpu.make_async_remote_copy(src_ref=src, dst_ref=dst, send_sem=send_sems.at[k], recv_sem=recv_sems.at[k],
                                        device_id=to, device_id_type=MESH)


def _sibling_swap_halves(tag, grads, after=()):
    n = len(grads)
    after_ops, after_specs = _after_operands(after)

    def body(*refs):
        ins, outs = refs[:n], refs[n + len(after_ops):2 * n + len(after_ops)]
        send_sems, recv_sems = refs[2 * n + len(after_ops):]
        x, y, c, _ = _mesh_place()
        sibling = (x, y, 1 - c)
        copies = []
        for k in range(n):
            half = grads[k].shape[1] // 2
            theirs = pl.ds(pl.multiple_of((1 - c) * half, 16), half)
            cp = _remote(ins[k].at[:, theirs, :], outs[k], send_sems, recv_sems, k, sibling)
            cp.start()
            copies.append(cp)
        for cp in copies:
            cp.wait_recv()
        for cp in copies:
            cp.wait_send()

    hbm = BS(memory_space=pl.ANY)
    return pl.pallas_call(
        body, out_shape=tuple(SDS((g.shape[0], g.shape[1] // 2, g.shape[2]), g.dtype) for g in grads),
        in_specs=[hbm] * n + after_specs, out_specs=(hbm,) * n,
        scratch_shapes=[pltpu.SemaphoreType.DMA((n,)), pltpu.SemaphoreType.DMA((n,))],
        name="reduce_sibling_send_" + tag, compiler_params=_params())(*grads, *after_ops)


def _row_tile(rows, cap=512):
    return max(r for r in range(16, cap + 1, 16) if rows % r == 0)


def _chip_presum(k, grad, got, c_idx):
    n_sh, rows, cols = grad.shape
    half = rows // 2
    tr = _row_tile(half)
    grad4 = grad.reshape(n_sh, 2, half, cols)

    def body(c_ref, a_ref, b_ref, o_ref):
        o_ref[...] = (a_ref[...].astype(F32) + b_ref[...].astype(F32)).astype(o_ref.dtype)

    return pl.pallas_call(
        body, out_shape=SDS((n_sh, half, cols), BF16),
        grid_spec=pltpu.PrefetchScalarGridSpec(
            num_scalar_prefetch=1, grid=(n_sh, half // tr),
            in_specs=[BS((None, None, tr, cols), lambda s, i, c_ref: (s, c_ref[0], i, 0)),
                      BS((None, tr, cols), lambda s, i, c_ref: (s, i, 0))],
            out_specs=BS((None, tr, cols), lambda s, i, c_ref: (s, i, 0))),
        name=f"reduce_presum{k}", compiler_params=_params("parallel", "parallel"))(c_idx, grad4, got)


HBM_SPEC = BS(memory_space=pltpu.HBM)
SEM_SPEC = BS(memory_space=pltpu.SEMAPHORE)
DATAFLOW = pltpu.SideEffectType.DATAFLOW_SIDE_EFFECTING


def _chip_exchange_copies(parts, lands, send_sems, recv_sems):
    _, _, c, chips = _mesh_place()
    return [_remote(parts[k].at[2 * px + py], lands[k].at[j], send_sems, recv_sems, 3 * k + j, (px, py, c))
            for k in range(len(parts)) for j, (px, py) in enumerate(chips)]


def _gather_copies(shards, lands, send_sems, recv_sems):
    x, y, c, chips = _mesh_place()
    return [_remote(shards[k], lands[k].at[2 * x + y], send_sems, recv_sems, 3 * k + j, (px, py, c))
            for k in range(len(shards)) for j, (px, py) in enumerate(chips)]


def _gather_half_copies(shards, lands, send_sems, recv_sems):
    x, y, c, chips = _mesh_place()
    out = []
    for k in range(len(shards)):
        half = shards[k].shape[0] // 2
        mine = pl.ds(pl.multiple_of(c * half, 16), half)
        for j, (px, py) in enumerate(chips):
            out.append(_remote(shards[k].at[mine, :], lands[k].at[2 * x + y, mine, :], send_sems, recv_sems,
                               3 * k + j, (px, py, c)))
    return out


def _sibling_fill(tag, lands):
    n = len(lands)

    def body(*refs):
        outs = refs[n:2 * n]
        send_sems, recv_sems = refs[2 * n:]
        x, y, c, chips = _mesh_place()
        copies = []
        for k in range(n):
            half = lands[k].shape[1] // 2
            mine = pl.ds(pl.multiple_of(c * half, 16), half)
            for j, (px, py) in enumerate(chips):
                blk = outs[k].at[2 * px + py, mine, :]
                copies.append(_remote(blk, blk, send_sems, recv_sems, 3 * k + j, (x, y, 1 - c)))
        for cp in copies:
            cp.start()
        for cp in copies:
            cp.wait_recv()
        for cp in copies:
            cp.wait_send()

    hbm = BS(memory_space=pl.ANY)
    return list(pl.pallas_call(
        body, out_shape=tuple(SDS(a.shape, a.dtype) for a in lands),
        in_specs=[hbm] * n, out_specs=(hbm,) * n, input_output_aliases={k: k for k in range(n)},
        scratch_shapes=[pltpu.SemaphoreType.DMA((3 * n,)), pltpu.SemaphoreType.DMA((3 * n,))],
        name="gather_fill_" + tag, compiler_params=_params())(*lands))


def _swap_copies(grads, lands, send_sems, recv_sems):
    x, y, c, _ = _mesh_place()
    out = []
    for k in range(len(grads)):
        half = grads[k].shape[1] // 2
        theirs = pl.ds(pl.multiple_of((1 - c) * half, 16), half)
        out.append(_remote(grads[k].at[:, theirs, :], lands[k], send_sems, recv_sems, k, (x, y, 1 - c)))
    return out


def _join_copies(fulls, same, send_sems, recv_sems):
    x, y, c, _ = _mesh_place()
    out = []
    for k in range(len(fulls)):
        half = fulls[k].shape[0] // 2
        mine = fulls[k].at[pl.ds(pl.multiple_of(c * half, 8), half), :]
        out.append(_remote(mine, mine, send_sems, recv_sems, k, (x, y, 1 - c)))
    return out


def _everyone_copies(packs, lands, send_sems, recv_sems):
    x, y, c, _ = _mesh_place()
    out = []
    for k in range(len(packs)):
        for j in range(N_DEV - 1):
            bx, by, bc = (j + 1) >> 2 & 1, (j + 1) >> 1 & 1, (j + 1) & 1
            peer = (x ^ bx, y ^ by, c ^ bc)
            out.append(_remote(packs[k], lands[k].at[4 * x + 2 * y + c], send_sems, recv_sems, (N_DEV - 1) * k + j, peer))
    return out


def _split_start(name, copies, sources, land_shapes, after=(), fanout=3):
    n = len(sources)
    n_land = len(land_shapes)
    m = n + n_land
    n_sems = fanout * n
    after_ops, after_specs = _after_operands(after)

    def body(*refs):
        ins = refs[:n]
        lands = refs[n:m] if n_land else ins
        send_sems, recv_sems = refs[m + len(after_ops)], refs[m + len(after_ops) + 1]
        token = refs[-1]
        for cp in copies(ins, lands, send_sems, recv_sems):
            cp.start()
        token[...] = jnp.zeros_like(token)

    lands = [pltpu.with_memory_space_constraint(lax.empty(s, d), pltpu.HBM) for s, d in land_shapes]
    sources = [pltpu.with_memory_space_constraint(p, pltpu.HBM) for p in sources]
    thru = [pltpu.HBM(a.shape, a.dtype) for a in sources + lands]
    out = pl.pallas_call(
        body, name=name,
        out_shape=(pltpu.SemaphoreType.DMA((n_sems,)), pltpu.SemaphoreType.DMA((n_sems,)), *thru, SDS((8, 128), F32)),
        in_specs=[HBM_SPEC] * m + after_specs,
        out_specs=(SEM_SPEC, SEM_SPEC, *[HBM_SPEC] * m, BS(memory_space=pltpu.VMEM)),
        input_output_aliases={i: 2 + i for i in range(m)},
        compiler_params=pltpu.CompilerParams(has_side_effects=DATAFLOW))(*sources, *lands, *after_ops)
    return out[0], out[1], list(out[2:2 + n]), list(out[2 + n:2 + m]), out[-1]


def _split_wait(name, copies, send_sems, recv_sems, sources, lands, after):
    n = len(sources)
    m = n + len(lands)
    after_ops, after_specs = _after_operands(after)

    def body(*refs):
        ins = refs[:n]
        zones = refs[n:m] if m > n else ins
        for cp in copies(ins, zones, refs[m], refs[m + 1]):
            cp.wait_send()
            cp.wait_recv()

    out = pl.pallas_call(
        body, name=name,
        out_shape=tuple(pltpu.HBM(a.shape, a.dtype) for a in sources + lands),
        in_specs=[HBM_SPEC] * m + [SEM_SPEC, SEM_SPEC] + after_specs, out_specs=(HBM_SPEC,) * m,
        input_output_aliases={i: i for i in range(m)},
        compiler_params=pltpu.CompilerParams(has_side_effects=DATAFLOW))(*sources, *lands, send_sems, recv_sems, *after_ops)
    return list(out[:n]), list(out[n:])


class _WeightGatherer:
    def __init__(self, shards):
        self.shards, self.open = shards, {}
        self.me = 2 * lax.axis_index("x") + lax.axis_index("y")

    HALVED = ("f1a",)

    def start(self, tag, after=()):
        shapes = [((N_SHARD,) + s.shape, s.dtype) for s in self.shards[tag]]
        copies = _gather_half_copies if tag in self.HALVED else _gather_copies
        self.open[tag] = _split_start("gather_start_" + tag, copies, self.shards[tag], shapes, after)
        return [self.open[tag][-1]]

    def finish(self, tag, after):
        send_sems, recv_sems, shards, lands, _ = self.open.pop(tag)
        copies = _gather_half_copies if tag in self.HALVED else _gather_copies
        shards, lands = _split_wait("gather_wait_" + tag, copies, send_sems, recv_sems, shards, lands, after)
        if tag in self.HALVED:
            lands = _sibling_fill(tag, lands)
        return [lax.dynamic_update_slice(zone, s[None], (self.me, 0, 0)) for zone, s in zip(lands, shards)]


class _GradReducer:
    def __init__(self):
        self.c_idx = lax.axis_index("c").astype(jnp.int32).reshape(1)
        self.place = jnp.stack([2 * lax.axis_index("x") + lax.axis_index("y"), lax.axis_index("c")]).astype(jnp.int32)
        self.swaps, self.open, self.landed, self.joins, self.reduced = {}, {}, {}, {}, []

    def swap_start(self, tag, grads, after=()):
        shapes = [((g.shape[0], g.shape[1] // 2, g.shape[2]), g.dtype) for g in grads]
        self.swaps[tag] = _split_start("reduce_swap_start_" + tag, _swap_copies, grads, shapes, after, fanout=1)
        return [self.swaps[tag][-1]]

    def start(self, tag, grads, after=(), swapped=()):
        pairs = []
        for s in swapped:
            send_sems, recv_sems, early, lands, _ = self.swaps.pop(s)
            pairs += zip(*_split_wait("reduce_swap_wait_" + s, _swap_copies, send_sems, recv_sems, early, lands, grads[-1:]))
        pairs += zip(grads, _sibling_swap_halves(tag, grads, after))
        parts = [_chip_presum(f"{tag}{k}", g, s, self.c_idx) for k, (g, s) in enumerate(pairs)]
        shapes = [((3,) + p.shape[1:], p.dtype) for p in parts]
        self.open[tag] = _split_start("reduce_exchange_start_" + tag, _chip_exchange_copies, parts, shapes)
        return [self.open[tag][-1]]

    def finish(self, tag, after):
        send_sems, recv_sems, parts, lands, _ = self.open.pop(tag)
        self.landed[tag] = _split_wait("reduce_exchange_wait_" + tag, _chip_exchange_copies, send_sems, recv_sems, parts, lands, after)
        return self.landed[tag][1][:1]

    def _sums(self, tag, after=()):
        parts, landed = self.landed.pop(tag)
        return [_chip_sum(f"{tag}{k}", p, got, self.place, after) for k, (p, got) in enumerate(zip(parts, landed))]

    def join_start(self, tag, after=()):
        self.joins[tag] = _split_start("reduce_join_start_" + tag, _join_copies, self._sums(tag, after), [], fanout=1)
        return [self.joins[tag][-1]]

    def join_finish(self, tag, after):
        send_sems, recv_sems, fulls, _, _ = self.joins.pop(tag)
        self.reduced += _split_wait("reduce_join_wait_" + tag, _join_copies, send_sems, recv_sems, fulls, [], after)[0]

    def join(self, tag, after=()):
        self.reduced += _sibling_join_halves(self._sums(tag), after)


def _chip_sum(k, part, got, place, after=()):
    _, half, cols = part.shape
    tr = _row_tile(half)
    n_t = half // tr
    after_ops, after_specs = _after_operands(after)

    def body(place_ref, a_ref, b_ref, *rest):
        o_ref = rest[-1]
        acc = a_ref[...].astype(F32)
        for j in range(3):
            acc = acc + b_ref[j].astype(F32)
        o_ref[...] = acc

    return pl.pallas_call(
        body, out_shape=SDS((2 * half, cols), F32),
        grid_spec=pltpu.PrefetchScalarGridSpec(
            num_scalar_prefetch=1, grid=(n_t,),
            in_specs=[BS((None, tr, cols), lambda i, place_ref: (place_ref[0], i, 0)),
                      BS((3, tr, cols), lambda i, place_ref: (0, i, 0))] + after_specs,
            out_specs=BS((tr, cols), lambda i, place_ref: (place_ref[1] * n_t + i, 0))),
        name=f"reduce_sum{k}", compiler_params=_params("parallel"))(place, part, got, *after_ops)


def _sibling_join_halves(fulls, after=()):
    n = len(fulls)
    after_ops, after_specs = _after_operands(after)

    def body(*refs):
        outs = refs[n + len(after_ops):2 * n + len(after_ops)]
        send_sems, recv_sems = refs[2 * n + len(after_ops):]
        copies = _join_copies(outs, outs, send_sems, recv_sems)
        for cp in copies:
            cp.start()
        for cp in copies:
            cp.wait_recv()
        for cp in copies:
            cp.wait_send()

    hbm = BS(memory_space=pl.ANY)
    return list(pl.pallas_call(
        body, out_shape=tuple(SDS(f.shape, f.dtype) for f in fulls),
        in_specs=[hbm] * n + after_specs, out_specs=(hbm,) * n, input_output_aliases={k: k for k in range(n)},
        scratch_shapes=[pltpu.SemaphoreType.DMA((n,)), pltpu.SemaphoreType.DMA((n,))],
        name="reduce_sibling_join", compiler_params=_params())(*fulls, *after_ops))


N_DEV = 8


def _sum_devices(packs):
    _, rows, lanes = packs.shape

    def body(p_ref, o_ref):
        acc = p_ref[0]
        for dev in range(1, N_DEV):
            acc = acc + p_ref[dev]
        o_ref[...] = acc

    vm = BS(memory_space=pltpu.VMEM)
    return pl.pallas_call(body, out_shape=SDS((rows, lanes), F32), in_specs=[vm], out_specs=vm,
                          name="small_sum", compiler_params=_params())(packs)


def _adamw(name, w, grad, row0, m, v, after=()):
    rows, cols = w.shape
    tr = rows if rows < 16 else _row_tile(rows, 352)
    bc1 = 1.0 - ADAM_B1 ** ADAM_STEP
    bc2 = 1.0 - ADAM_B2 ** ADAM_STEP
    after_ops, after_specs = _after_operands(after)

    def body(w_ref, g_ref, m_ref, v_ref, *rest):
        go_ref, d_ref, mo_ref, vo_ref = rest[len(after_ops):]
        g = g_ref[...]
        m_new = ADAM_B1 * m_ref[...] + (1.0 - ADAM_B1) * g
        v_new = ADAM_B2 * v_ref[...] + (1.0 - ADAM_B2) * (g * g)
        go_ref[...] = g
        mo_ref[...] = m_new
        vo_ref[...] = v_new
        d_ref[...] = -ADAM_LR * ((m_new / bc1) / (jnp.sqrt(v_new / bc2) + ADAM_EPS) + ADAM_WD * w_ref[...])

    blk = BS((tr, cols), lambda i: (i, 0))
    shape = SDS((rows, cols), F32)
    return pl.pallas_call(
        body, out_shape=(shape,) * 4, grid=(rows // tr,),
        in_specs=[blk, BS((tr, cols), lambda i: (row0 // tr + i, 0)), blk, blk] + after_specs, out_specs=(blk,) * 4,
        name=name, compiler_params=_params("parallel"))(w, grad, m, v, *after_ops)


SMALL_LANES = 128


def _pack_small(parts):
    flat = jnp.concatenate([jnp.ravel(p) for p in parts])
    rows = -(-flat.shape[0] // (64 * SMALL_LANES)) * 64
    return jnp.pad(flat, (0, rows * SMALL_LANES - flat.shape[0])).reshape(rows, SMALL_LANES)


def _unpack_small(packed, like):
    flat = jnp.ravel(packed)
    out, at = [], 0
    for p in like:
        out.append(flat[at:at + p.size].reshape(p.shape))
        at += p.size
    return out


def kernel(x, mem, ffn1_norm, ffn1_w_gate, ffn1_w_up, ffn1_w_down, mix_norm, w_in, pool_w, pool_scale, w_pool_proj, ssm_a_re, ssm_a_im, ssm_log_dt, ssm_b_re, ssm_b_im, ssm_c_re, ssm_c_im, ssm_d, w_glu_val, w_glu_gate, w_mix_out, xattn_norm, mem_norm, w_q, w_kv, w_xo, ffn2_norm, ffn2_w_gate, ffn2_w_up, ffn2_w_down, final_norm, loss_target, m_ffn1_norm, m_ffn1_w_gate, m_ffn1_w_up, m_ffn1_w_down, m_mix_norm, m_w_in, m_pool_w, m_pool_scale, m_w_pool_proj, m_ssm_a_re, m_ssm_a_im, m_ssm_log_dt, m_ssm_b_re, m_ssm_b_im, m_ssm_c_re, m_ssm_c_im, m_ssm_d, m_w_glu_val, m_w_glu_gate, m_w_mix_out, m_xattn_norm, m_mem_norm, m_w_q, m_w_kv, m_w_xo, m_ffn2_norm, m_ffn2_w_gate, m_ffn2_w_up, m_ffn2_w_down, m_final_norm, v_ffn1_norm, v_ffn1_w_gate, v_ffn1_w_up, v_ffn1_w_down, v_mix_norm, v_w_in, v_pool_w, v_pool_scale, v_w_pool_proj, v_ssm_a_re, v_ssm_a_im, v_ssm_log_dt, v_ssm_b_re, v_ssm_b_im, v_ssm_c_re, v_ssm_c_im, v_ssm_d, v_w_glu_val, v_w_glu_gate, v_w_mix_out, v_xattn_norm, v_mem_norm, v_w_q, v_w_kv, v_w_xo, v_ffn2_norm, v_ffn2_w_gate, v_ffn2_w_up, v_ffn2_w_down, v_final_norm):
    given = dict(locals())
    w = {n: given[n] for n in WEIGHTS}
    m = {n: given["m_" + n] for n in WEIGHTS}
    v = {n: given["v_" + n] for n in WEIGHTS}

    def shard_view(a, n):
        return a[0].T if n in TRANSPOSED else a[0]

    def shard_unview(a, n):
        return (a.T if n in TRANSPOSED else a)[None]

    shards = {tag: [jnp.concatenate([shard_view(w[n], n).astype(BF16) for n in grp], axis=0) for grp in arrays]
              for tag, arrays in GATHER_PHASES.items()}
    reducer = _GradReducer()
    loss_part, grad_x, _, small = _device_step(x[0], mem[0], loss_target[0], _WeightGatherer(shards),
                                               {n: w[n] for n in SMALL}, reducer)

    small_like = [w[n] for n in SMALL] + [loss_part[0, :1]]
    pack = _pack_small([small[n] for n in SMALL] + [loss_part[0, :1]])
    everyone = _split_start("small_start", _everyone_copies, [pack], [((N_DEV,) + pack.shape, F32)], fanout=N_DEV - 1)
    reducer.join("b", after=everyone[-1:])

    grads, delta, new_m, new_v = {}, {}, {}, {}
    big_done = []
    for grp, red in zip(REDUCE_GROUPS, reducer.reduced):
        row0 = 0
        for n in grp:
            w_n = shard_view(w[n], n)
            outs = _adamw("adamw_" + n, w_n, red, row0, shard_view(m[n], n), shard_view(v[n], n), after=everyone[-1:])
            grads[n], delta[n], new_m[n], new_v[n] = (shard_unview(o, n) for o in outs)
            big_done.append(outs[1])
            row0 += w_n.shape[0]

    send_sems, recv_sems, packs, landed, _ = everyone
    packs, landed = _split_wait("small_wait", _everyone_copies, send_sems, recv_sems, packs, landed, big_done)
    mine = 4 * lax.axis_index("x") + 2 * lax.axis_index("y") + lax.axis_index("c")
    summed = _sum_devices(lax.dynamic_update_slice(landed[0], packs[0][None], (mine, 0, 0)))
    g_small = dict(zip(SMALL + ("loss",), _unpack_small(summed, small_like)))
    loss = g_small.pop("loss").reshape(())
    narrow = [n for n in SMALL if w[n].ndim > 3]
    dense = [n for n in SMALL if n not in narrow]
    for n in narrow:
        two_d = (-1, w[n].shape[-1])
        outs = _adamw("adamw_" + n, w[n].reshape(two_d), g_small[n].reshape(two_d), 0, m[n].reshape(two_d), v[n].reshape(two_d))
        grads[n], delta[n], new_m[n], new_v[n] = (o.reshape(w[n].shape) for o in outs)
    dense_like = [w[n] for n in dense]
    packed = _adamw("adamw_small", _pack_small(dense_like), _pack_small([g_small[n] for n in dense]), 0,
                    _pack_small([m[n] for n in dense]), _pack_small([v[n] for n in dense]))
    for out, store in zip(packed, (grads, delta, new_m, new_v)):
        for n, val in zip(dense, _unpack_small(out, dense_like)):
            store[n] = val

    return (loss, grad_x[None], *[grads[n] for n in WEIGHTS], *[delta[n] for n in WEIGHTS],
            *[new_m[n] for n in WEIGHTS], *[new_v[n] for n in WEIGHTS])
```

```python
import functools
import math

import jax
import jax.numpy as jnp
from jax import lax
from jax.experimental import pallas as pl
from jax.experimental.pallas import tpu as pltpu

F32 = jnp.float32
BF16 = jnp.bfloat16
SDS = jax.ShapeDtypeStruct
BS = pl.BlockSpec
MESH = pl.DeviceIdType.MESH

D_MODEL = 1024
D_FF = 2816
N_SHARD = 4
FF_SH = D_FF // N_SHARD
D_POOL = 512
POOL_WINDOWS = (2, 4, 8, 16)
POOL_GROUP = 128
D_SSM = 256
SSM_GROUPS = 16
SSM_GROUP = 16
SSM_STATE = 64
SSM_CH = SSM_GROUPS * SSM_STATE
N_HEADS = 4
HEAD_DIM = 256
EPS = 1e-6
ADAM_LR, ADAM_B1, ADAM_B2, ADAM_EPS, ADAM_WD, ADAM_STEP = 0.001, 0.9, 0.999, 1e-08, 0.01, 10

VMEM_LIMIT_V7X = 52 * 1024 * 1024
TM = 512

NN = (((1,), (0,)), ((), ()))
NT = (((1,), (1,)), ((), ()))
TN = (((0,), (0,)), ((), ()))


def _params(*sem):
    return pltpu.CompilerParams(dimension_semantics=sem if sem else None, vmem_limit_bytes=VMEM_LIMIT_V7X)


def _dot(a, b, dims=NN):
    return lax.dot_general(a.astype(BF16), b.astype(BF16), dims, preferred_element_type=F32)


def _sigmoid(v):
    return pl.reciprocal(1.0 + jnp.exp(-v), approx=True)


def _block_dims(spec):
    return tuple(d for d in spec.block_shape if d is not None)


def _after_operands(after):
    return list(after), [BS(memory_space=pl.ANY)] * len(after)


def _mm(name, pairs, *, grid, out_shape, out_spec, red_axis=None, extras=(), epilogue=None, after=()):
    n_pairs, n_extra = len(pairs), len(extras)
    n_red = grid[red_axis] if red_axis is not None else 1
    dims = [p[4] for p in pairs]

    def body(*refs):
        ab = refs[:2 * n_pairs]
        ex = refs[2 * n_pairs:2 * n_pairs + n_extra]
        o_ref = refs[2 * n_pairs + n_extra + len(after)]

        def partial():
            acc = None
            for p in range(n_pairs):
                t = _dot(ab[2 * p][...], ab[2 * p + 1][...], dims[p])
                acc = t if acc is None else acc + t
            return acc

        def finish(acc):
            res = epilogue(acc, *[e[...] for e in ex]) if epilogue is not None else acc
            o_ref[...] = res.astype(o_ref.dtype)

        if n_red == 1:
            finish(partial())
        else:
            acc_ref = refs[-1]
            k = pl.program_id(red_axis)

            @pl.when(k == 0)
            def _():
                acc_ref[...] = jnp.zeros_like(acc_ref)

            acc_ref[...] += partial()

            @pl.when(k == n_red - 1)
            def _():
                finish(acc_ref[...])

    operands, in_specs = [], []
    for a, a_spec, b, b_spec, _ in pairs:
        operands += [a, b]
        in_specs += [a_spec, b_spec]
    for e, e_spec in extras:
        operands.append(e)
        in_specs.append(e_spec)
    after_ops, after_specs = _after_operands(after)
    operands += after_ops
    in_specs += after_specs
    scratch = [pltpu.VMEM(_block_dims(out_spec), F32)] if n_red > 1 else []
    sem = tuple("arbitrary" if ax == red_axis else "parallel" for ax in range(len(grid)))
    return pl.pallas_call(body, out_shape=out_shape, grid=grid, in_specs=in_specs, out_specs=out_spec,
                          scratch_shapes=scratch, name=name, compiler_params=_params(*sem))(*operands)


def _rmsnorm(name, h, gain, tm, after=()):
    t, d = h.shape
    after_ops, after_specs = _after_operands(after)

    def body(h_ref, g_ref, *rest):
        u_ref = rest[-1]
        hv = h_ref[...]
        r = lax.rsqrt(jnp.mean(hv * hv, axis=-1, keepdims=True) + EPS)
        u_ref[...] = ((hv * r) * g_ref[...]).astype(u_ref.dtype)

    return pl.pallas_call(
        body, out_shape=SDS((t, d), BF16), grid=(t // tm,),
        in_specs=[BS((tm, d), lambda i: (i, 0)), BS((1, d), lambda i: (0, 0))] + after_specs,
        out_specs=BS((tm, d), lambda i: (i, 0)), name=name, compiler_params=_params("parallel"))(h, gain, *after_ops)


def _rmsnorm_bwd(name, h, gain, du, dh_in, tm):
    t, d = h.shape
    has_in = dh_in is not None

    def body(*refs):
        if has_in:
            h_ref, g_ref, du_ref, dhin_ref, dh_ref, dhb_ref, dg_ref = refs
        else:
            h_ref, g_ref, du_ref, dh_ref, dhb_ref, dg_ref = refs
        i = pl.program_id(0)
        hv = h_ref[...]
        r = lax.rsqrt(jnp.mean(hv * hv, axis=-1, keepdims=True) + EPS)
        n = hv * r
        duv = du_ref[...].astype(F32)
        dn = duv * g_ref[...]
        dh = r * (dn - n * jnp.mean(dn * n, axis=-1, keepdims=True))
        if has_in:
            dh = dhin_ref[...] + dh
        dh_ref[...] = dh
        dhb_ref[...] = dh.astype(BF16)

        @pl.when(i == 0)
        def _():
            dg_ref[...] = jnp.zeros_like(dg_ref)

        dg_ref[...] += jnp.sum(duv * n, axis=0, keepdims=True)

    row = BS((tm, d), lambda i: (i, 0))
    vec = BS((1, d), lambda i: (0, 0))
    operands = [h, gain, du] + ([dh_in] if has_in else [])
    in_specs = [row, vec, row] + ([row] if has_in else [])
    return pl.pallas_call(
        body, out_shape=(SDS((t, d), F32), SDS((t, d), BF16), SDS((1, d), F32)), grid=(t // tm,),
        in_specs=in_specs, out_specs=(row, row, vec), name=name, compiler_params=_params("arbitrary"))(*operands)


def _loss_head_tile(i, hv, g_ref, t_ref, loss_ref, dh_ref, dhb_ref, dg_ref):
    g = g_ref[...]
    r = lax.rsqrt(jnp.mean(hv * hv, axis=-1, keepdims=True) + EPS)
    n = hv * r
    err = n * g - t_ref[...]
    dy = err * (1.0 / hv.shape[-1])
    dn = dy * g
    dh = r * (dn - n * jnp.mean(dn * n, axis=-1, keepdims=True))
    dh_ref[...] = dh
    dhb_ref[...] = dh.astype(BF16)

    @pl.when(i == 0)
    def _():
        dg_ref[...] = jnp.zeros_like(dg_ref)
        loss_ref[...] = jnp.zeros_like(loss_ref)

    dg_ref[...] += jnp.sum(dy * n, axis=0, keepdims=True)
    part = 0.5 * jnp.sum(jnp.mean(err * err, axis=-1, keepdims=True), axis=0, keepdims=True)
    loss_ref[...] += jnp.broadcast_to(part, loss_ref.shape)


def _norm_tile(h, g_ref, u_ref):
    r = lax.rsqrt(jnp.mean(h * h, axis=-1, keepdims=True) + EPS)
    u_ref[...] = ((h * r) * g_ref[...]).astype(u_ref.dtype)


FFN_GATE, FFN_UP, FFN_DOWN = 0, 1, 2


def _ffn_up(name, u, w_f, tm, after=()):
    t, d = u.shape
    after_ops, after_specs = _after_operands(after)

    def body(u_ref, wg_ref, wu_ref, *rest):
        pg_ref, pu_ref, a_ref = rest[len(after_ops):]
        uv = u_ref[...]
        for s in range(N_SHARD):
            g = _dot(uv, wg_ref[s], NT)
            up = _dot(uv, wu_ref[s], NT)
            sg = _sigmoid(g)
            silu = g * sg
            a_ref[s] = (silu * up).astype(BF16)
            pu_ref[s] = (0.5 * silu).astype(BF16)
            pg_ref[s] = (0.5 * sg * (1.0 + g * (1.0 - sg)) * up).astype(BF16)

    hid = BS((N_SHARD, tm, FF_SH), lambda i: (0, i, 0))
    shape = SDS((N_SHARD, t, FF_SH), BF16)
    return pl.pallas_call(
        body, out_shape=(shape, shape, shape), grid=(t // tm,),
        in_specs=[BS((tm, d), lambda i: (i, 0)), _ffn_all_shards_spec(w_f["gate"][1]),
                  _ffn_all_shards_spec(w_f["up"][1])] + after_specs,
        out_specs=(hid, hid, hid), name=name,
        compiler_params=_params("parallel"))(u, w_f["gate"][0], w_f["up"][0], *after_ops)


def _ffn_all_shards_spec(block):
    return BS((N_SHARD, FF_SH, D_MODEL), lambda i: (0, block, 0))


def _ffn_down(name, a, w_f, resid, tm, next_gain=None, head=None):
    t, d = resid.shape
    row = BS((tm, d), lambda i: (i, 0))
    vec = BS((1, d), lambda i: (0, 0))

    def body(a_ref, w_ref, res_ref, *rest):
        acc = _dot(a_ref[0], w_ref[0])
        for s in range(1, N_SHARD):
            acc = acc + _dot(a_ref[s], w_ref[s])
        h = res_ref[...] + 0.5 * acc
        if head is not None:
            _loss_head_tile(pl.program_id(0), h, *rest)
        else:
            g_ref, h_ref, u_ref = rest
            h_ref[...] = h
            _norm_tile(h, g_ref, u_ref)

    if head is not None:
        extra, extra_specs = list(head), [vec, row]
        out_shape = (SDS((1, 128), F32), SDS((t, d), F32), SDS((t, d), BF16), SDS((1, d), F32))
        out_specs = (BS((1, 128), lambda i: (0, 0)), row, row, vec)
    else:
        extra, extra_specs = [next_gain], [vec]
        out_shape = (SDS((t, d), F32), SDS((t, d), BF16))
        out_specs = (row, row)
    return pl.pallas_call(
        body, out_shape=out_shape, grid=(t // tm,),
        in_specs=[BS((N_SHARD, tm, FF_SH), lambda i: (0, i, 0)), _ffn_all_shards_spec(w_f["down"][1]), row] + extra_specs,
        out_specs=out_specs, name=name,
        compiler_params=_params("arbitrary" if head is not None else "parallel"))(a, w_f["down"][0], resid, *extra)


def _mm_resid_norm(name, a, b, resid, next_gain, tm):
    t, d = resid.shape

    def body(a_ref, b_ref, res_ref, g_ref, h_ref, u_ref):
        h = res_ref[...] + _dot(a_ref[...], b_ref[...])
        h_ref[...] = h
        _norm_tile(h, g_ref, u_ref)

    row = BS((tm, d), lambda i: (i, 0))
    return pl.pallas_call(
        body, out_shape=(SDS((t, d), F32), SDS((t, d), BF16)), grid=(t // tm,),
        in_specs=[BS((tm, a.shape[1]), lambda i: (i, 0)), BS(b.shape, lambda i: (0, 0)), row, BS((1, d), lambda i: (0, 0))],
        out_specs=(row, row), name=name, compiler_params=_params("parallel"))(a, b, resid, next_gain)


def _ffn_bwd_act(name, dh_b, w_f, pg, pu, tm, after=()):
    t, d = dh_b.shape
    after_ops, after_specs = _after_operands(after)

    def body(dh_ref, wd_ref, pg_ref, pu_ref, *rest):
        dg_ref, dup_ref = rest[len(after_ops):]
        dh = dh_ref[...]
        for s in range(N_SHARD):
            da = _dot(dh, wd_ref[s], NT)
            dg_ref[s] = (da * pg_ref[s].astype(F32)).astype(BF16)
            dup_ref[s] = (da * pu_ref[s].astype(F32)).astype(BF16)

    hid = BS((N_SHARD, tm, FF_SH), lambda i: (0, i, 0))
    shape = SDS((N_SHARD, t, FF_SH), BF16)
    return pl.pallas_call(
        body, out_shape=(shape, shape), grid=(t // tm,),
        in_specs=[BS((tm, d), lambda i: (i, 0)), _ffn_all_shards_spec(w_f["down"][1]), hid, hid] + after_specs,
        out_specs=(hid, hid), name=name,
        compiler_params=_params("parallel"))(dh_b, w_f["down"][0], pg, pu, *after_ops)


def _ffn_dw(name, u, dg, dup, a, dh_b, tm):
    t, d = u.shape
    n_t = t // tm

    def body(u_ref, dg_ref, dup_ref, a_ref, dh_ref, o_ref, acc):
        i = pl.program_id(1)

        @pl.when(i == 0)
        def _():
            acc[...] = jnp.zeros_like(acc)

        uv = u_ref[...]
        acc[FFN_GATE] += _dot(dg_ref[...], uv, TN)
        acc[FFN_UP] += _dot(dup_ref[...], uv, TN)
        acc[FFN_DOWN] += _dot(a_ref[...], dh_ref[...], TN)

        @pl.when(i == n_t - 1)
        def _():
            o_ref[FFN_GATE] = acc[FFN_GATE].astype(BF16)
            o_ref[FFN_UP] = acc[FFN_UP].astype(BF16)
            o_ref[FFN_DOWN] = (0.5 * acc[FFN_DOWN]).astype(BF16)

    hid = BS((None, tm, FF_SH), lambda s, i: (s, i, 0))
    row = BS((tm, d), lambda s, i: (i, 0))
    return pl.pallas_call(
        body, out_shape=SDS((N_SHARD, 3, FF_SH, d), BF16), grid=(N_SHARD, n_t),
        in_specs=[row, hid, hid, hid, row], out_specs=BS((None, 3, FF_SH, d), lambda s, i: (s, 0, 0, 0)),
        scratch_shapes=[pltpu.VMEM((3, FF_SH, d), F32)],
        name=name, compiler_params=_params("parallel", "arbitrary"))(u, dg, dup, a, dh_b)


def _norm_bwd_tile(i, du, h_ref, g_ref, dhin_ref, dh_ref, dhb_ref, dg_ref):
    hv = h_ref[...]
    r = lax.rsqrt(jnp.mean(hv * hv, axis=-1, keepdims=True) + EPS)
    n = hv * r
    dn = du * g_ref[...]
    dh = dhin_ref[...] + r * (dn - n * jnp.mean(dn * n, axis=-1, keepdims=True))
    dh_ref[...] = dh
    dhb_ref[...] = dh.astype(BF16)

    @pl.when(i == 0)
    def _():
        dg_ref[...] = jnp.zeros_like(dg_ref)

    dg_ref[...] += jnp.sum(du * n, axis=0, keepdims=True)


def _norm_bwd_specs(tm):
    row = BS((tm, D_MODEL), lambda i: (i, 0))
    vec = BS((1, D_MODEL), lambda i: (0, 0))
    return [row, vec, row], (row, row, vec)


def _norm_bwd_shapes(t):
    return SDS((t, D_MODEL), F32), SDS((t, D_MODEL), BF16), SDS((1, D_MODEL), F32)


def _ffn_dx(name, dg, dup, w_f, h, gain, dh_in, tm, after=()):
    t = dg.shape[1]
    tm = tm // 2
    after_ops, after_specs = _after_operands(after)

    def body(dg_ref, dup_ref, wg_ref, wu_ref, h_ref, g_ref, dhin_ref, *rest):
        acc = _dot(dg_ref[0], wg_ref[0]) + _dot(dup_ref[0], wu_ref[0])
        for s in range(1, N_SHARD):
            acc = acc + _dot(dg_ref[s], wg_ref[s]) + _dot(dup_ref[s], wu_ref[s])
        _norm_bwd_tile(pl.program_id(0), acc, h_ref, g_ref, dhin_ref, *rest[len(after_ops):])

    hid = BS((N_SHARD, tm, FF_SH), lambda i: (0, i, 0))
    norm_in, norm_out = _norm_bwd_specs(tm)
    return pl.pallas_call(
        body, out_shape=_norm_bwd_shapes(t), grid=(t // tm,),
        in_specs=[hid, hid, _ffn_all_shards_spec(w_f["gate"][1]), _ffn_all_shards_spec(w_f["up"][1])] + norm_in + after_specs,
        out_specs=norm_out, name=name,
        compiler_params=_params("arbitrary"))(dg, dup, w_f["gate"][0], w_f["up"][0], h, gain, dh_in, *after_ops)


def _mm_norm_bwd(name, a, b, dims, h, gain, dh_in, tm):
    t = a.shape[0]

    def body(a_ref, b_ref, h_ref, g_ref, dhin_ref, *outs):
        _norm_bwd_tile(pl.program_id(0), _dot(a_ref[...], b_ref[...], dims), h_ref, g_ref, dhin_ref, *outs)

    norm_in, norm_out = _norm_bwd_specs(tm)
    return pl.pallas_call(
        body, out_shape=_norm_bwd_shapes(t), grid=(t // tm,),
        in_specs=[BS((tm, a.shape[1]), lambda i: (i, 0)), BS(b.shape, lambda i: (0, 0))] + norm_in,
        out_specs=norm_out, name=name, compiler_params=_params("arbitrary"))(a, b, h, gain, dh_in)


def _plain_mm(name, a, b, dims, out_dtype, tm, resid=None, after=()):
    t = a.shape[0]
    n = b.shape[1] if dims == NN else b.shape[0]
    extras = [(resid, BS((tm, n), lambda i: (i, 0)))] if resid is not None else []
    epi = (lambda acc, res: res + acc) if resid is not None else None
    return _mm(name, [(a, BS((tm, a.shape[1]), lambda i: (i, 0)), b, BS(b.shape, lambda i: (0, 0)), dims)],
               grid=(t // tm,), out_shape=SDS((t, n), out_dtype), out_spec=BS((tm, n), lambda i: (i, 0)),
               extras=extras, epilogue=epi, after=after)


def _dw_mm(name, a, b, tm, out_dtype=BF16, after=()):
    t, k = a.shape
    n = b.shape[1]
    return _mm(name, [(a, BS((tm, k), lambda i: (i, 0)), b, BS((tm, n), lambda i: (i, 0)), TN)],
               grid=(t // tm,), red_axis=0, out_shape=SDS((k, n), out_dtype), out_spec=BS((k, n), lambda i: (0, 0)),
               after=after)


POOL_CHUNK = 256
POOL_HALO = 8


def _window_sum(v, width, lead):
    n = v.shape[0]
    s = v
    k = 1
    while k < width:
        s = s + pltpu.roll(s, n - k, 0)
        k *= 2
    return pltpu.roll(s, lead, 0) if lead else s


def _pool_count(base, left, right, t, shape):
    pos = base + lax.broadcasted_iota(jnp.int32, shape, 0)
    lo = jnp.maximum(pos - left, 0)
    hi = jnp.minimum(pos + right + 1, t)
    return (hi - lo).astype(F32)


def _pool_fwd(proj, pool_w, pool_scale):
    t = proj.shape[0]
    c, h = POOL_CHUNK, POOL_HALO
    n_chunks = t // c

    def body(proj_hbm, pw_ref, sc_ref, pooled_ref, mixed_ref, ms_ref, pad_ref, sem):
        cp = pltpu.make_async_copy(proj_hbm.at[:, pl.ds(0, D_POOL)], pad_ref.at[pl.ds(h, t), :], sem)
        cp.start()
        pad_ref[pl.ds(0, h), :] = jnp.zeros((h, D_POOL), F32)
        pad_ref[pl.ds(t + h, h), :] = jnp.zeros((h, D_POOL), F32)
        cp.wait()
        for g, width in enumerate(POOL_WINDOWS):
            left = width // 2
            right = width - 1 - left
            cols = slice(g * POOL_GROUP, (g + 1) * POOL_GROUP)
            wmat = pw_ref[g].astype(BF16)
            scale = sc_ref[:, cols]

            def chunk(ci, carry, left=left, right=right, width=width, cols=cols, wmat=wmat, scale=scale):
                base = pl.multiple_of(ci * c, c)
                v = pad_ref[pl.ds(base, c + 2 * h), cols]
                win = _window_sum(v, width, left)[h:h + c]
                cnt = _pool_count(base, left, right, t, (c, POOL_GROUP))
                pooled = (win / cnt - v[h:h + c]).astype(BF16)
                mixed = _dot(pooled, wmat)
                pooled_ref[pl.ds(base, c), cols] = pooled
                mixed_ref[pl.ds(base, c), cols] = mixed.astype(BF16)
                ms_ref[pl.ds(base, c), cols] = (mixed * scale).astype(BF16)
                return carry

            lax.fori_loop(0, n_chunks, chunk, 0)

    vm = BS(memory_space=pltpu.VMEM)
    shape = SDS((t, D_POOL), BF16)
    return pl.pallas_call(
        body, out_shape=(shape, shape, shape),
        in_specs=[BS(memory_space=pl.ANY), vm, vm], out_specs=(vm, vm, vm),
        scratch_shapes=[pltpu.VMEM((t + 2 * h, D_POOL), F32), pltpu.SemaphoreType.DMA],
        name="pool_fwd", compiler_params=_params())(proj, pool_w, pool_scale)


def _pool_bwd(d_ms, mixed, pooled, pool_w, pool_scale):
    t = d_ms.shape[0]
    c, h = POOL_CHUNK, POOL_HALO
    n_chunks = t // c

    def body(dms_ref, mixed_ref, pooled_ref, pw_ref, sc_ref, dp_ref, dsc_ref, dpw_ref, pad_ref):
        pad_ref[pl.ds(0, h), :] = jnp.zeros((h, D_POOL), F32)
        pad_ref[pl.ds(t + h, h), :] = jnp.zeros((h, D_POOL), F32)
        for g, width in enumerate(POOL_WINDOWS):
            left = width // 2
            right = width - 1 - left
            cols = slice(g * POOL_GROUP, (g + 1) * POOL_GROUP)
            wmat = pw_ref[g].astype(BF16)
            scale = sc_ref[:, cols]

            def first(ci, carry, left=left, right=right, cols=cols, wmat=wmat, scale=scale):
                dsc, dpw = carry
                base = pl.multiple_of(ci * c, c)
                dms = dms_ref[pl.ds(base, c), cols].astype(F32)
                dsc = dsc + jnp.sum(dms * mixed_ref[pl.ds(base, c), cols].astype(F32), axis=0, keepdims=True)
                dmix = (dms * scale).astype(BF16)
                dpw = dpw + _dot(pooled_ref[pl.ds(base, c), cols], dmix, TN)
                dpooled = _dot(dmix, wmat, NT)
                cnt = _pool_count(base, left, right, t, (c, POOL_GROUP))
                pad_ref[pl.ds(base + h, c), cols] = dpooled / cnt
                return dsc, dpw

            dsc, dpw = lax.fori_loop(0, n_chunks, first,
                                     (jnp.zeros((1, POOL_GROUP), F32), jnp.zeros((POOL_GROUP, POOL_GROUP), F32)))
            dsc_ref[:, cols] = dsc
            dpw_ref[g] = dpw

            def second(ci, carry, left=left, right=right, width=width, cols=cols):
                base = pl.multiple_of(ci * c, c)
                v = pad_ref[pl.ds(base, c + 2 * h), cols]
                win = _window_sum(v, width, right)[h:h + c]
                cnt = _pool_count(base, left, right, t, (c, POOL_GROUP))
                dp_ref[pl.ds(base, c), cols] = (win - v[h:h + c] * cnt).astype(BF16)
                return carry

            lax.fori_loop(0, n_chunks, second, 0)

    vm = BS(memory_space=pltpu.VMEM)
    return pl.pallas_call(
        body, out_shape=(SDS((t, D_POOL), BF16), SDS((1, D_POOL), F32), SDS((4, POOL_GROUP, POOL_GROUP), F32)),
        in_specs=[vm] * 5, out_specs=(vm, vm, vm),
        scratch_shapes=[pltpu.VMEM((t + 2 * h, D_POOL), F32)],
        name="pool_bwd", compiler_params=_params())(d_ms, mixed, pooled, pool_w, pool_scale)


SSM_ROWS = 2 * SSM_GROUPS * SSM_GROUP
SSM_HALF = SSM_GROUPS * SSM_GROUP


def _ssm_zoh(a_r, a_i, ldt):
    dt = jnp.exp(ldt)
    mag = jnp.exp(dt * a_r)
    ang = dt * a_i
    cs, sn = jnp.cos(ang), jnp.sin(ang)
    abr, abi = mag * cs, mag * sn
    den = a_r * a_r + a_i * a_i
    nr = abr - 1.0
    qr = (nr * a_r + abi * a_i) / den
    qi = (abi * a_r - nr * a_i) / den
    return dt, mag, cs, sn, abr, abi, den, nr, qr, qi


def _ssm_group_mask():
    row = lax.broadcasted_iota(jnp.int32, (SSM_HALF, SSM_CH), 0)
    col = lax.broadcasted_iota(jnp.int32, (SSM_HALF, SSM_CH), 1)
    return (row // SSM_GROUP) == (col // SSM_STATE)


def _ssm_prep(a_r, a_i, ldt, b_r, b_i, c_r, c_i, after=()):
    after_ops, after_specs = _after_operands(after)

    def body(ar_ref, ai_ref, ldt_ref, br_ref, bi_ref, cr_ref, ci_ref, *rest):
        abr_ref, abi_ref, win_ref, wint_ref, woutt_ref, wout_ref = rest[len(after_ops):]
        *_, abr, abi, _, _, qr, qi = _ssm_zoh(ar_ref[...], ai_ref[...], ldt_ref[...])
        abr_ref[...] = abr
        abi_ref[...] = abi
        b_r, b_i = br_ref[...], bi_ref[...]
        bbr = qr * b_r - qi * b_i
        bbi = qr * b_i + qi * b_r
        mask = _ssm_group_mask()
        state = lax.broadcasted_iota(jnp.int32, (SSM_STATE, SSM_CH), 0)
        col = lax.broadcasted_iota(jnp.int32, (SSM_STATE, SSM_CH), 1)
        every_group = (col % SSM_STATE == state).astype(BF16)

        def spread(x):
            return jnp.where(mask, _dot(x, every_group), 0.0)

        for d in range(2):
            rows = slice(d * SSM_HALF, (d + 1) * SSM_HALF)
            for half, x_in, x_out in ((0, bbr[rows], cr_ref[rows, :]), (1, bbi[rows], -ci_ref[rows, :])):
                cols = slice(half * SSM_CH, (half + 1) * SSM_CH)
                m_in, m_out = spread(x_in), spread(x_out)
                win_ref[d, :, cols] = m_in.astype(BF16)
                wint_ref[d, cols, :] = m_in.T.astype(BF16)
                woutt_ref[d, :, cols] = m_out.astype(BF16)
                wout_ref[d, cols, :] = m_out.T.astype(BF16)

    vm = BS(memory_space=pltpu.VMEM)
    vec = SDS((SSM_ROWS, SSM_STATE), F32)
    wide = SDS((2, SSM_HALF, 2 * SSM_CH), BF16)
    tall = SDS((2, 2 * SSM_CH, SSM_HALF), BF16)
    return pl.pallas_call(body, out_shape=(vec, vec, wide, tall, wide, tall), in_specs=[vm] * 7 + after_specs,
                          out_specs=(vm,) * 6, name="ssm_prep",
                          compiler_params=_params())(a_r, a_i, ldt, b_r, b_i, c_r, c_i, *after_ops)


def _ssm_prep_bwd(a_r, a_i, ldt, b_r, b_i, g_abr, g_abi, d_win, d_woutt):
    def body(ar_ref, ai_ref, ldt_ref, br_ref, bi_ref, gabr_ref, gabi_ref, dwin_ref, dwoutt_ref,
             dar_ref, dai_ref, dldt_ref, dbr_ref, dbi_ref, dcr_ref, dci_ref):
        a_r, a_i = ar_ref[...], ai_ref[...]
        dt, mag, cs, sn, abr, abi, den, nr, qr, qi = _ssm_zoh(a_r, a_i, ldt_ref[...])
        mask = _ssm_group_mask()
        col = lax.broadcasted_iota(jnp.int32, (SSM_CH, SSM_STATE), 0)
        state = lax.broadcasted_iota(jnp.int32, (SSM_CH, SSM_STATE), 1)
        own_state = (col % SSM_STATE == state).astype(BF16)

        def pick(dense):
            m = jnp.where(mask, dense, 0.0)
            hi = m.astype(BF16)
            lo = m - hi.astype(F32)
            return _dot(hi, own_state) + _dot(lo, own_state)

        def picked(ref, half):
            cols = slice(half * SSM_CH, (half + 1) * SSM_CH)
            return jnp.concatenate([pick(ref[d, :, cols]) for d in range(2)], axis=0)

        g_r, g_i = picked(dwin_ref, 0), picked(dwin_ref, 1)
        dcr_ref[...] = picked(dwoutt_ref, 0)
        dci_ref[...] = -picked(dwoutt_ref, 1)
        b_r, b_i = br_ref[...], bi_ref[...]
        dbr_ref[...] = g_r * qr + g_i * qi
        dbi_ref[...] = g_i * qr - g_r * qi
        gqr = g_r * b_r + g_i * b_i
        gqi = g_i * b_r - g_r * b_i
        g_nr_num = gqr / den
        g_ni_num = gqi / den
        g_den = -(gqr * qr + gqi * qi) / den
        g_nr = g_nr_num * a_r - g_ni_num * a_i
        g_abi = g_nr_num * a_i + g_ni_num * a_r
        d_ar = g_nr_num * nr + g_ni_num * abi + 2.0 * a_r * g_den
        d_ai = g_nr_num * abi - g_ni_num * nr + 2.0 * a_i * g_den
        g_abr = gabr_ref[...] + g_nr
        g_abi = gabi_ref[...] + g_abi
        g_mag = g_abr * cs + g_abi * sn
        g_ang = mag * (g_abi * cs - g_abr * sn)
        g_e = g_mag * mag
        d_ar = d_ar + g_e * dt
        d_ai = d_ai + g_ang * dt
        g_dt = g_e * a_r + g_ang * a_i
        dar_ref[...] = d_ar
        dai_ref[...] = d_ai
        dldt_ref[...] = g_dt * dt

    vm = BS(memory_space=pltpu.VMEM)
    vec = SDS((SSM_ROWS, SSM_STATE), F32)
    return pl.pallas_call(body, out_shape=(vec,) * 7, in_specs=[vm] * 9, out_specs=(vm,) * 7, name="ssm_prep_bwd",
                          compiler_params=_params())(a_r, a_i, ldt, b_r, b_i, g_abr, g_abi, d_win, d_woutt)


SCAN_ROWS = 512


def _ssm_scan(name, inp, w1, a_r, a_i, w2, reverse):
    t = inp.shape[0]
    rows = min(SCAN_ROWS, t)
    n = t // rows
    n_groups = rows // 8
    ch = SSM_CH
    at = (lambda i: (n - 1 - i, 0)) if reverse else (lambda i: (i, 0))

    def body(in_ref, w1_ref, ar_ref, ai_ref, w2_ref, sb_ref, out_ref, cr_ref, ci_ref, k_ref, st_ref):
        i = pl.program_id(0)

        @pl.when(i == 0)
        def _():
            ar8 = jnp.broadcast_to(ar_ref[...], (8, ch))
            ai8 = jnp.broadcast_to(ai_ref[...], (8, ch))
            row = lax.broadcasted_iota(jnp.int32, (8, ch), 0)
            rank = (7 - row) if reverse else row
            powers = [(ar8, ai8)]
            for _ in range(7):
                p_r, p_i = powers[-1]
                powers.append((p_r * ar8 - p_i * ai8, p_r * ai8 + p_i * ar8))
            zero = jnp.zeros((8, ch), F32)
            for slot, k in enumerate((1, 2, 4)):
                k_ref[2 * slot] = jnp.where(rank >= k, powers[k - 1][0], zero)
                k_ref[2 * slot + 1] = jnp.where(rank >= k, powers[k - 1][1], zero)
            carry_r, carry_i = zero, zero
            for j in range(8):
                carry_r = jnp.where(rank == j, powers[j][0], carry_r)
                carry_i = jnp.where(rank == j, powers[j][1], carry_i)
            k_ref[6] = carry_r
            k_ref[7] = carry_i
            cr_ref[...] = zero
            ci_ref[...] = zero

        st_ref[...] = _dot(in_ref[...], w1_ref[...])

        def group(gi, carry):
            c_r, c_i = carry
            g = (n_groups - 1 - gi) if reverse else gi
            r0 = pl.multiple_of(g * 8, 8)
            x_r = st_ref[pl.ds(r0, 8), 0:ch]
            x_i = st_ref[pl.ds(r0, 8), ch:2 * ch]
            for slot, k in enumerate((1, 2, 4)):
                shift = (8 - k) if reverse else k
                s_r = pltpu.roll(x_r, shift, 0)
                s_i = pltpu.roll(x_i, shift, 0)
                m_r, m_i = k_ref[2 * slot], k_ref[2 * slot + 1]
                x_r, x_i = x_r + m_r * s_r - m_i * s_i, x_i + m_r * s_i + m_i * s_r
            p_r, p_i = k_ref[6], k_ref[7]
            x_r, x_i = x_r + p_r * c_r - p_i * c_i, x_i + p_r * c_i + p_i * c_r
            st_ref[pl.ds(r0, 8), 0:ch] = x_r
            st_ref[pl.ds(r0, 8), ch:2 * ch] = x_i
            last = 0 if reverse else 7
            return (jnp.broadcast_to(x_r[last:last + 1, :], (8, ch)), jnp.broadcast_to(x_i[last:last + 1, :], (8, ch)))

        c_r, c_i = lax.fori_loop(0, n_groups, group, (cr_ref[...], ci_ref[...]))
        cr_ref[...] = c_r
        ci_ref[...] = c_i
        states = st_ref[...].astype(BF16)
        sb_ref[...] = states
        out_ref[...] = _dot(states, w2_ref[...])

    return pl.pallas_call(
        body, out_shape=(SDS((t, 2 * ch), BF16), SDS((t, D_SSM), F32)), grid=(n,),
        in_specs=[BS((rows, D_SSM), at), BS((D_SSM, 2 * ch), lambda i: (0, 0)), BS((1, ch), lambda i: (0, 0)),
                  BS((1, ch), lambda i: (0, 0)), BS((2 * ch, D_SSM), lambda i: (0, 0))],
        out_specs=(BS((rows, 2 * ch), at), BS((rows, D_SSM), at)),
        scratch_shapes=[pltpu.VMEM((8, ch), F32), pltpu.VMEM((8, ch), F32), pltpu.VMEM((8, 8, ch), F32),
                        pltpu.VMEM((rows, 2 * ch), F32)],
        name=name, compiler_params=_params("arbitrary"))(inp, w1, a_r, a_i, w2)


DA_ROWS = 1024


def _ssm_da(name, lam, states, reverse, after=()):
    t = lam.shape[0]
    rows = min(DA_ROWS, t)
    n = t // rows
    halo_rows = 16
    nb = rows // halo_rows
    ch = SSM_CH
    if reverse:
        halo_at = lambda i: (jnp.minimum((i + 1) * nb, t // halo_rows - 1), 0)
    else:
        halo_at = lambda i: (jnp.maximum(i * nb - 1, 0), 0)

    after_ops, after_specs = _after_operands(after)

    def body(lam_ref, x_ref, halo_ref, *rest):
        dr_ref, di_ref = rest[len(after_ops):]
        i = pl.program_id(0)

        @pl.when(i == 0)
        def _():
            dr_ref[...] = jnp.zeros_like(dr_ref)
            di_ref[...] = jnp.zeros_like(di_ref)

        row = lax.broadcasted_iota(jnp.int32, (rows, ch), 0)
        if reverse:
            edge, shift, h_row, live = rows - 1, rows - 1, 0, i < n - 1
        else:
            edge, shift, h_row, live = 0, 1, halo_rows - 1, i > 0

        def neighbour(lo):
            halo = halo_ref[:, lo:lo + ch].astype(F32)[h_row:h_row + 1]
            halo = jnp.where(live, halo, 0.0)
            x = x_ref[:, lo:lo + ch].astype(F32)
            return jnp.where(row == edge, jnp.broadcast_to(halo, (rows, ch)), pltpu.roll(x, shift, 0))

        xp_r, xp_i = neighbour(0), neighbour(ch)
        l_r, l_i = lam_ref[:, 0:ch].astype(F32), lam_ref[:, ch:2 * ch].astype(F32)
        dr_ref[...] += jnp.sum(l_r * xp_r + l_i * xp_i, axis=0, keepdims=True)
        di_ref[...] += jnp.sum(l_i * xp_r - l_r * xp_i, axis=0, keepdims=True)

    blk = BS((rows, 2 * ch), lambda i: (i, 0))
    vec = BS((1, ch), lambda i: (0, 0))
    return pl.pallas_call(
        body, out_shape=(SDS((1, ch), F32), SDS((1, ch), F32)), grid=(n,),
        in_specs=[blk, blk, BS((halo_rows, 2 * ch), halo_at)] + after_specs, out_specs=(vec, vec),
        name=name, compiler_params=_params("arbitrary"))(lam, states, states, *after_ops)


GELU_C = math.sqrt(2.0 / math.pi)
GELU_K = 0.044715


def _ssm_combine(proj, y_fwd, y_bwd, d_skip, tm, after=()):
    t = proj.shape[0]
    after_ops, after_specs = _after_operands(after)

    def body(s_ref, yf_ref, yb_ref, d_ref, *rest):
        yt_ref, g_ref = rest[len(after_ops):]
        y = s_ref[...] * d_ref[...] + yf_ref[...] + yb_ref[...]
        yt_ref[...] = y
        th = jnp.tanh(GELU_C * (y + GELU_K * y * y * y))
        g_ref[...] = (0.5 * y * (1.0 + th)).astype(BF16)

    blk = BS((tm, D_SSM), lambda i: (i, 0))
    return pl.pallas_call(
        body, out_shape=(SDS((t, D_SSM), F32), SDS((t, D_SSM), BF16)), grid=(t // tm,),
        in_specs=[BS((tm, D_SSM), lambda i: (i, D_POOL // D_SSM)), blk, blk, BS((1, D_SSM), lambda i: (0, 0))] + after_specs,
        out_specs=(blk, blk), name="ssm_combine",
        compiler_params=_params("parallel"))(proj, y_fwd, y_bwd, d_skip, *after_ops)


def _ssm_ds(proj, d_yt, du_fwd, du_bwd, d_skip, tm):
    t = proj.shape[0]

    def body(s_ref, dy_ref, duf_ref, dub_ref, d_ref, ds_ref, dd_ref):
        i = pl.program_id(0)
        dy = dy_ref[...]
        ds_ref[...] = (dy * d_ref[...] + duf_ref[...] + dub_ref[...]).astype(BF16)

        @pl.when(i == 0)
        def _():
            dd_ref[...] = jnp.zeros_like(dd_ref)

        dd_ref[...] += jnp.sum(dy * s_ref[...], axis=0, keepdims=True)

    blk = BS((tm, D_SSM), lambda i: (i, 0))
    vec = BS((1, D_SSM), lambda i: (0, 0))
    return pl.pallas_call(
        body, out_shape=(SDS((t, D_SSM), BF16), SDS((1, D_SSM), F32)), grid=(t // tm,),
        in_specs=[BS((tm, D_SSM), lambda i: (i, D_POOL // D_SSM)), blk, blk, blk, vec],
        out_specs=(blk, vec), name="ssm_ds", compiler_params=_params("arbitrary"))(proj, d_yt, du_fwd, du_bwd, d_skip)


GP_BLOCK = (D_POOL + D_SSM) // 256
GS_BLOCK = GP_BLOCK + D_MODEL // 256


def _merge_specs(tm):
    return [BS((tm, D_POOL), lambda s, i: (i, 0)), BS((tm, D_SSM), lambda s, i: (i, 0)),
            BS((None, D_POOL, 256), lambda s, i: (s, 0, 0)), BS((None, D_SSM, 256), lambda s, i: (s, 2, 0)),
            BS((None, D_SSM, 256), lambda s, i: (s, 3, 0)),
            BS((tm, 256), lambda s, i: (i, GP_BLOCK + s)), BS((tm, 256), lambda s, i: (i, GS_BLOCK + s))]


def _mixer_merge(ms, yssm, w_e, proj, tm):
    t = ms.shape[0]

    def body(ms_ref, y_ref, wpp_ref, wgv_ref, wgg_ref, gp_ref, gs_ref, o_ref):
        zp = _dot(ms_ref[...], wpp_ref[...])
        yv = y_ref[...]
        zv = _dot(yv, wgv_ref[...])
        zg = _dot(yv, wgg_ref[...])
        o_ref[...] = (_sigmoid(gp_ref[...]) * zp + _sigmoid(gs_ref[...]) * zv * _sigmoid(zg)).astype(BF16)

    col = BS((tm, 256), lambda s, i: (i, s))
    return pl.pallas_call(
        body, out_shape=SDS((t, D_MODEL), BF16), grid=(N_SHARD, t // tm), in_specs=_merge_specs(tm), out_specs=col,
        name="mixer_merge", compiler_params=_params("parallel", "parallel"))(ms, yssm, w_e, w_e, w_e, proj, proj)


def _mixer_merge_bwd(ms, yssm, w_e, proj, dmerged, tm):
    t = ms.shape[0]

    def body(ms_ref, y_ref, wpp_ref, wgv_ref, wgg_ref, gp_ref, gs_ref, dm_ref,
             dgp_ref, dgs_ref, dzp_ref, dzv_ref, dzg_ref):
        zp = _dot(ms_ref[...], wpp_ref[...])
        yv = y_ref[...]
        zv = _dot(yv, wgv_ref[...])
        zg = _dot(yv, wgg_ref[...])
        dm = dm_ref[...].astype(F32)
        sp, ss, sg = _sigmoid(gp_ref[...]), _sigmoid(gs_ref[...]), _sigmoid(zg)
        dgp_ref[...] = (dm * zp * sp * (1.0 - sp)).astype(BF16)
        dgs_ref[...] = (dm * zv * sg * ss * (1.0 - ss)).astype(BF16)
        dzp_ref[...] = (dm * sp).astype(BF16)
        dz = dm * ss
        dzv_ref[...] = (dz * sg).astype(BF16)
        dzg_ref[...] = (dz * zv * sg * (1.0 - sg)).astype(BF16)

    col = BS((tm, 256), lambda s, i: (i, s))
    shape = SDS((t, D_MODEL), BF16)
    return pl.pallas_call(
        body, out_shape=(shape,) * 5, grid=(N_SHARD, t // tm), in_specs=_merge_specs(tm) + [col],
        out_specs=(col,) * 5, name="mixer_merge_bwd",
        compiler_params=_params("parallel", "parallel"))(ms, yssm, w_e, w_e, w_e, proj, proj, dmerged)


def _mixer_dw(ms, yssm, dzp, dzv, dzg, tm):
    t = ms.shape[0]
    n_t = t // tm

    def body(ms_ref, y_ref, dzp_ref, dzv_ref, dzg_ref, o_ref, acc):
        i = pl.program_id(1)

        @pl.when(i == 0)
        def _():
            acc[...] = jnp.zeros_like(acc)

        yv = y_ref[...]
        acc[0:D_POOL, :] += _dot(ms_ref[...], dzp_ref[...], TN)
        acc[D_POOL:D_POOL + D_SSM, :] += _dot(yv, dzv_ref[...], TN)
        acc[D_POOL + D_SSM:, :] += _dot(yv, dzg_ref[...], TN)

        @pl.when(i == n_t - 1)
        def _():
            o_ref[...] = acc[...].astype(BF16)

    col = BS((tm, 256), lambda s, i: (i, s))
    return pl.pallas_call(
        body, out_shape=SDS((N_SHARD, 1024, 256), BF16), grid=(N_SHARD, n_t),
        in_specs=[BS((tm, D_POOL), lambda s, i: (i, 0)), BS((tm, D_SSM), lambda s, i: (i, 0)), col, col, col],
        out_specs=BS((None, 1024, 256), lambda s, i: (s, 0, 0)), scratch_shapes=[pltpu.VMEM((1024, 256), F32)],
        name="mixer_dw", compiler_params=_params("parallel", "arbitrary"))(ms, yssm, dzp, dzv, dzg)


def _mixer_dx(dzp, dzv, dzg, w_e, y_total, tm):
    t = dzp.shape[0]

    def body(dzp_ref, dzv_ref, dzg_ref, wpp_ref, wgv_ref, wgg_ref, yt_ref, dms_ref, dy_ref, acc_ms, acc_y):
        s = pl.program_id(1)

        @pl.when(s == 0)
        def _():
            acc_ms[...] = jnp.zeros_like(acc_ms)
            acc_y[...] = jnp.zeros_like(acc_y)

        acc_ms[...] += _dot(dzp_ref[...], wpp_ref[...], NT)
        acc_y[...] += _dot(dzv_ref[...], wgv_ref[...], NT) + _dot(dzg_ref[...], wgg_ref[...], NT)

        @pl.when(s == N_SHARD - 1)
        def _():
            dms_ref[...] = acc_ms[...].astype(BF16)
            y = yt_ref[...]
            inner = GELU_C * (y + GELU_K * y * y * y)
            th = jnp.tanh(inner)
            dgelu = 0.5 * (1.0 + th) + 0.5 * y * (1.0 - th * th) * GELU_C * (1.0 + 3.0 * GELU_K * y * y)
            dy_ref[...] = acc_y[...] * dgelu

    col = BS((tm, 256), lambda i, s: (i, s))
    return pl.pallas_call(
        body, out_shape=(SDS((t, D_POOL), BF16), SDS((t, D_SSM), F32)), grid=(t // tm, N_SHARD),
        in_specs=[col, col, col, BS((None, D_POOL, 256), lambda i, s: (s, 0, 0)),
                  BS((None, D_SSM, 256), lambda i, s: (s, 2, 0)), BS((None, D_SSM, 256), lambda i, s: (s, 3, 0)),
                  BS((tm, D_SSM), lambda i, s: (i, 0))],
        out_specs=(BS((tm, D_POOL), lambda i, s: (i, 0)), BS((tm, D_SSM), lambda i, s: (i, 0))),
        scratch_shapes=[pltpu.VMEM((tm, D_POOL), F32), pltpu.VMEM((tm, D_SSM), F32)],
        name="mixer_dx", compiler_params=_params("parallel", "arbitrary"))(dzp, dzv, dzg, w_e, w_e, w_e, y_total)


def _attn_probs(q_h, k_h):
    s = _dot(q_h, k_h, NT) * (1.0 / math.sqrt(HEAD_DIM))
    e = jnp.exp(s - jnp.max(s, axis=-1, keepdims=True))
    return e / jnp.sum(e, axis=-1, keepdims=True)


def _attn_fwd(q, kv, tm):
    t = q.shape[0]
    m = kv.shape[0]

    def body(q_ref, kv_ref, o_ref):
        for hd in range(N_HEADS):
            lo = hd * HEAD_DIM
            p = _attn_probs(q_ref[:, lo:lo + HEAD_DIM], kv_ref[:, lo:lo + HEAD_DIM])
            o_ref[:, lo:lo + HEAD_DIM] = _dot(p, kv_ref[:, D_MODEL + lo:D_MODEL + lo + HEAD_DIM]).astype(BF16)

    return pl.pallas_call(
        body, out_shape=SDS((t, D_MODEL), BF16), grid=(t // tm,),
        in_specs=[BS((tm, D_MODEL), lambda i: (i, 0)), BS((m, 2 * D_MODEL), lambda i: (0, 0))],
        out_specs=BS((tm, D_MODEL), lambda i: (i, 0)), name="attn_fwd", compiler_params=_params("parallel"))(q, kv)


def _attn_bwd(q, kv, d_o, tm):
    t = q.shape[0]
    m = kv.shape[0]

    def body(q_ref, kv_ref, do_ref, dq_ref, dkv_ref):
        i = pl.program_id(0)

        @pl.when(i == 0)
        def _():
            dkv_ref[...] = jnp.zeros_like(dkv_ref)

        for hd in range(N_HEADS):
            lo = hd * HEAD_DIM
            q_h = q_ref[:, lo:lo + HEAD_DIM]
            k_h = kv_ref[:, lo:lo + HEAD_DIM]
            v_h = kv_ref[:, D_MODEL + lo:D_MODEL + lo + HEAD_DIM]
            do_h = do_ref[:, lo:lo + HEAD_DIM]
            p = _attn_probs(q_h, k_h)
            dkv_ref[:, D_MODEL + lo:D_MODEL + lo + HEAD_DIM] += _dot(p, do_h, TN)
            dp = _dot(do_h, v_h, NT)
            ds = p * (dp - jnp.sum(dp * p, axis=-1, keepdims=True)) * (1.0 / math.sqrt(HEAD_DIM))
            dq_ref[:, lo:lo + HEAD_DIM] = _dot(ds, k_h).astype(BF16)
            dkv_ref[:, lo:lo + HEAD_DIM] += _dot(ds, q_h, TN)

    row = BS((tm, D_MODEL), lambda i: (i, 0))
    full = BS((m, 2 * D_MODEL), lambda i: (0, 0))
    return pl.pallas_call(
        body, out_shape=(SDS((t, D_MODEL), BF16), SDS((m, 2 * D_MODEL), F32)), grid=(t // tm,),
        in_specs=[row, full, row], out_specs=(row, full), name="attn_bwd",
        compiler_params=_params("arbitrary"))(q, kv, d_o)


TRANSPOSED = ("ffn1_w_gate", "ffn1_w_up", "ffn2_w_gate", "ffn2_w_up", "w_in")
GATHER_PHASES = {"f1a": (("ffn1_w_gate", "ffn1_w_up"),),
                 "f1b": (("ffn1_w_down",),),
                 "win": (("w_in",),),
                 "mix": (("w_mix_out", "w_q", "w_xo"), ("w_kv",), ("w_pool_proj", "w_glu_val", "w_glu_gate")),
                 "f2": (("ffn2_w_gate", "ffn2_w_up", "ffn2_w_down"),)}
REDUCE_GROUPS = (("ffn2_w_gate", "ffn2_w_up", "ffn2_w_down"), ("w_xo",), ("w_q",), ("w_kv",), ("w_mix_out",),
                 ("w_pool_proj", "w_glu_val", "w_glu_gate"), ("w_in",), ("ffn1_w_gate", "ffn1_w_up", "ffn1_w_down"))
SMALL = ("ffn1_norm", "mix_norm", "pool_w", "pool_scale", "ssm_a_re", "ssm_a_im", "ssm_log_dt", "ssm_b_re",
         "ssm_b_im", "ssm_c_re", "ssm_c_im", "ssm_d", "xattn_norm", "mem_norm", "ffn2_norm", "final_norm")
WEIGHTS = ("ffn1_norm", "ffn1_w_gate", "ffn1_w_up", "ffn1_w_down", "mix_norm", "w_in", "pool_w", "pool_scale",
           "w_pool_proj", "ssm_a_re", "ssm_a_im", "ssm_log_dt", "ssm_b_re", "ssm_b_im", "ssm_c_re", "ssm_c_im",
           "ssm_d", "w_glu_val", "w_glu_gate", "w_mix_out", "xattn_norm", "mem_norm", "w_q", "w_kv", "w_xo",
           "ffn2_norm", "ffn2_w_gate", "ffn2_w_up", "ffn2_w_down", "final_norm")


def _small_view(a, n):
    return jnp.swapaxes(a, 3, 4) if n in ("ssm_b_re", "ssm_b_im") else a


def _device_step(x, mem, target, wts, sp, reducer=None):
    t = x.shape[0]
    tm = min(TM, t)
    g = {}

    first_gather = wts.start("f1a")
    u1 = _rmsnorm("norm_ffn1", x, sp["ffn1_norm"], tm, after=first_gather)

    def per_channel(a):
        a = a.reshape(2 * SSM_GROUPS, 1, -1)
        return jnp.broadcast_to(a, (2 * SSM_GROUPS, SSM_GROUP, a.shape[-1])).reshape(SSM_ROWS, a.shape[-1])

    ssm_a = per_channel(sp["ssm_a_re"]), per_channel(sp["ssm_a_im"]), per_channel(sp["ssm_log_dt"])
    ssm_b = sp["ssm_b_re"].reshape(SSM_ROWS, SSM_STATE), sp["ssm_b_im"].reshape(SSM_ROWS, SSM_STATE)
    abr, abi, w_in_s, w_in_s_t, w_out_s_t, w_out_s = _ssm_prep(
        *ssm_a, *ssm_b, sp["ssm_c_re"].reshape(SSM_ROWS, SSM_STATE), sp["ssm_c_im"].reshape(SSM_ROWS, SSM_STATE),
        after=first_gather)
    first_rows = (2, SSM_GROUPS, SSM_GROUP, SSM_STATE)
    a_r = abr.reshape(first_rows)[:, :, 0].reshape(2, 1, SSM_CH)
    a_i = abi.reshape(first_rows)[:, :, 0].reshape(2, 1, SSM_CH)
    mem_n = _rmsnorm("norm_mem", mem, sp["mem_norm"], mem.shape[0], after=first_gather)

    (w_gu,) = wts.finish("f1a", [u1, w_in_s, w_in_s_t, w_out_s, w_out_s_t, a_r, a_i, mem_n])
    w_f1 = {"gate": (w_gu, FFN_GATE), "up": (w_gu, FFN_UP)}
    down_gather = wts.start("f1b", [w_gu])
    g1, up1, a1 = _ffn_up("ffn1_up", u1, w_f1, tm, after=wts.start("win", down_gather))
    (w_dn,) = wts.finish("f1b", [a1])
    w_f1["down"] = (w_dn, 0)
    h1, u2 = _ffn_down("ffn1_down", a1, w_f1, x, tm, next_gain=sp["mix_norm"])

    (w_in_g,) = wts.finish("win", [u2])
    w_in_t = w_in_g.reshape(D_FF, D_MODEL)
    proj = _mm("mix_in", [(u2, BS((tm, D_MODEL), lambda j, i: (i, 0)), w_in_t, BS((D_FF // 2, D_MODEL), lambda j, i: (j, 0)), NT)],
               grid=(2, t // tm), out_shape=SDS((t, D_FF), F32), out_spec=BS((tm, D_FF // 2), lambda j, i: (i, j)),
               after=wts.start("f2", wts.start("mix", [w_in_g])))
    pooled, mixed, ms = _pool_fwd(proj, sp["pool_w"][0], sp["pool_scale"])

    s_in = proj[:, D_POOL:D_POOL + D_SSM].astype(BF16)
    states, y_dirs = [], []
    for dr in range(2):
        st, yd = _ssm_scan(f"ssm_scan_fwd{dr}", s_in, w_in_s[dr], a_r[dr], a_i[dr], w_out_s[dr], reverse=(dr == 1))
        states.append(st)
        y_dirs.append(yd)
    w_sq, w_kv, w_e = wts.finish("mix", y_dirs)
    w_mo, w_q, w_xo = (w_sq[:, 256 * k:256 * (k + 1)].reshape(D_MODEL, D_MODEL) for k in range(3))
    w_d = w_kv[:, None]
    y_total, yssm = _ssm_combine(proj, y_dirs[0], y_dirs[1], sp["ssm_d"], tm)

    merged = _mixer_merge(ms, yssm, w_e, proj, tm)
    h2, u3 = _mm_resid_norm("mix_out", merged, w_mo, h1, sp["xattn_norm"], tm)

    q = _plain_mm("attn_q", u3, w_q, NN, BF16, tm)
    n_mem = mem.shape[0]
    kv = _mm("attn_kv", [(mem_n, BS((n_mem, D_MODEL), lambda s: (0, 0)), w_d, BS((None, None, D_MODEL, 512), lambda s: (s, 0, 0, 0)), NN)],
             grid=(N_SHARD,), out_shape=SDS((n_mem, 2 * D_MODEL), BF16), out_spec=BS((n_mem, 512), lambda s: (0, s)))
    o = _attn_fwd(q, kv, tm)
    h3, u4 = _mm_resid_norm("attn_out", o, w_xo, h2, sp["ffn2_norm"], tm)

    (w_2,) = wts.finish("f2", [u4])
    w_f2 = {"gate": (w_2, FFN_GATE), "up": (w_2, FFN_UP), "down": (w_2, FFN_DOWN)}
    g2, up2, a2 = _ffn_up("ffn2_up", u4, w_f2, tm)
    loss, dh4, dh4_b, g["final_norm"] = _ffn_down("ffn2_down", a2, w_f2, h3, tm,
                                                  head=(sp["final_norm"].reshape(1, D_MODEL), target))

    dg2, dup2 = _ffn_bwd_act("ffn2_bwd_act", dh4_b, w_f2, g2, up2, tm)
    dw_f2 = _ffn_dw("ffn2_dw", u4, dg2, dup2, a2, dh4_b, tm)
    dh3, dh3_b, g["ffn2_norm"] = _ffn_dx("ffn2_dx", dg2, dup2, w_f2, h3, sp["ffn2_norm"], dh4, tm)

    d_o = _plain_mm("attn_out_dx", dh3_b, w_xo, NT, BF16, tm)
    dw_xo = _dw_mm("attn_out_dw", o, dh3_b, tm)
    dq, dkv = _attn_bwd(q, kv, d_o, tm)
    dw_q = _dw_mm("attn_q_dw", u3, dq, tm)
    dh2, dh2_b, g["xattn_norm"] = _mm_norm_bwd("attn_q_dx", dq, w_q, NT, h2, sp["xattn_norm"], dh3, tm)
    dw_kv = _mm("attn_kv_dw", [(mem_n, BS((n_mem, D_MODEL), lambda s: (0, 0)), dkv, BS((n_mem, 512), lambda s: (0, s)), TN)],
                grid=(N_SHARD,), out_shape=SDS((N_SHARD, D_MODEL, 512), BF16), out_spec=BS((None, D_MODEL, 512), lambda s: (s, 0, 0)))
    dmem_n = _mm("attn_kv_dx", [(dkv, BS((n_mem, 512), lambda s: (0, s)), w_d, BS((None, None, D_MODEL, 512), lambda s: (s, 0, 0, 0)), NT)],
                 grid=(N_SHARD,), red_axis=0, out_shape=SDS((n_mem, D_MODEL), F32), out_spec=BS((n_mem, D_MODEL), lambda s: (0, 0)))
    _, _, g["mem_norm"] = _rmsnorm_bwd("norm_mem_bwd", mem, sp["mem_norm"], dmem_n, None, n_mem)

    square = (N_SHARD, D_MODEL // N_SHARD, D_MODEL)
    early = [dw_f2.reshape(N_SHARD, 3 * FF_SH, D_MODEL), dw_xo.reshape(square), dw_q.reshape(square), dw_kv]
    swapping = reducer.swap_start("a1", early) if reducer is not None else []
    dmerged = _plain_mm("mix_out_dx", dh2_b, w_mo, NT, BF16, tm, after=swapping)
    dw_mo = _dw_mm("mix_out_dw", merged, dh2_b, tm)
    d_gp, d_gs, dzp, dzv, dzg = _mixer_merge_bwd(ms, yssm, w_e, proj, dmerged, tm)
    dw_e = _mixer_dw(ms, yssm, dzp, dzv, dzg, tm)
    d_ms, d_yt = _mixer_dx(dzp, dzv, dzg, w_e, y_total, tm)
    dp, d_scale, d_pw = _pool_bwd(d_ms, mixed, pooled, sp["pool_w"][0], sp["pool_scale"])
    g["pool_scale"] = d_scale
    g["pool_w"] = d_pw[None]

    d_yt_b = d_yt.astype(BF16)
    du_dirs, lams = [], []
    for dr in range(2):
        lam, du = _ssm_scan(f"ssm_scan_bwd{dr}", d_yt_b, w_out_s_t[dr], a_r[dr], -a_i[dr], w_in_s_t[dr], reverse=(dr == 0))
        du_dirs.append(du)
        lams.append(lam)
    ds, g["ssm_d"] = _ssm_ds(proj, d_yt, du_dirs[0], du_dirs[1], sp["ssm_d"], tm)

    d_proj = jnp.concatenate([dp, ds, d_gp, d_gs], axis=1)
    dw_in_t = _mm("mix_in_dw", [(d_proj, BS((tm, D_FF // 2), lambda j, i: (i, j)), u2, BS((tm, D_MODEL), lambda j, i: (i, 0)), TN)],
                  grid=(2, t // tm), red_axis=1, out_shape=SDS((D_FF, D_MODEL), BF16), out_spec=BS((D_FF // 2, D_MODEL), lambda j, i: (j, 0)))
    dh1, dh1_b, g["mix_norm"] = _mm_norm_bwd("mix_in_dx", d_proj, w_in_t, NN, h1, sp["mix_norm"], dh2, tm)

    early += [dw_mo.reshape(square), dw_e, dw_in_t.reshape(N_SHARD, FF_SH, D_MODEL)]
    g["final_norm"] = g["final_norm"].reshape(D_MODEL)

    travelling = reducer.start("a", early[4:], swapped=["a1"], after=list(g.values())) if reducer is not None else []
    d_abr, d_abi, d_cm, d_bm = [], [], [], []
    for dr in range(2):
        da_r, da_i = _ssm_da(f"ssm_da{dr}", lams[dr], states[dr], reverse=(dr == 1), after=travelling)
        d_abr.append(da_r)
        d_abi.append(da_i)
        d_cm.append(_dw_mm(f"ssm_dc{dr}", d_yt_b, states[dr], tm, F32, after=travelling))
        d_bm.append(_dw_mm(f"ssm_db{dr}", s_in, lams[dr], tm, F32, after=travelling))

    def first_channel(da):
        da = jnp.stack(da).reshape(2, SSM_GROUPS, 1, SSM_STATE)
        return jnp.pad(da, ((0, 0), (0, 0), (0, SSM_GROUP - 1), (0, 0))).reshape(SSM_ROWS, SSM_STATE)

    d_ar, d_ai, d_ldt, d_br, d_bi, d_cr, d_ci = _ssm_prep_bwd(
        *ssm_a, *ssm_b, first_channel(d_abr), first_channel(d_abi), jnp.stack(d_bm), jnp.stack(d_cm))
    per_group = (2 * SSM_GROUPS, SSM_GROUP * SSM_STATE)
    g["ssm_a_re"] = d_ar.reshape(2 * SSM_GROUPS, SSM_GROUP, SSM_STATE).sum(axis=1).reshape(sp["ssm_a_re"].shape)
    g["ssm_a_im"] = d_ai.reshape(2 * SSM_GROUPS, SSM_GROUP, SSM_STATE).sum(axis=1).reshape(sp["ssm_a_im"].shape)
    g["ssm_log_dt"] = d_ldt.reshape(per_group).sum(axis=1).reshape(sp["ssm_log_dt"].shape)
    g["ssm_b_re"] = d_br.reshape(sp["ssm_b_re"].shape)
    g["ssm_b_im"] = d_bi.reshape(sp["ssm_b_im"].shape)
    g["ssm_c_re"] = d_cr.reshape(sp["ssm_c_re"].shape)
    g["ssm_c_im"] = d_ci.reshape(sp["ssm_c_im"].shape)
    if reducer is not None:
        travelling = travelling + [d_ar, d_br, d_cr]
    dg1, dup1 = _ffn_bwd_act("ffn1_bwd_act", dh1_b, w_f1, g1, up1, tm, after=travelling)
    dw_f1 = _ffn_dw("ffn1_dw", u1, dg1, dup1, a1, dh1_b, tm).reshape(N_SHARD, 3 * FF_SH, D_MODEL)
    if reducer is not None:
        travelling = reducer.start("b", [dw_f1], after=reducer.finish("a", [dw_f1]))
        travelling = travelling + reducer.join_start("a", after=travelling)
    grad_x, _, g["ffn1_norm"] = _ffn_dx("ffn1_dx", dg1, dup1, w_f1, x, sp["ffn1_norm"], dh1, tm, after=travelling)
    if reducer is not None:
        reducer.finish("b", [grad_x])
        reducer.join_finish("a", [grad_x])
    return loss, grad_x, early + [dw_f1], g


def _mesh_place():
    x, y, c = lax.axis_index("x"), lax.axis_index("y"), lax.axis_index("c")
    chips = [(1 - x, y), (x, 1 - y), (1 - x, 1 - y)]
    return x, y, c, chips


def _remote(src, dst, send_sems, recv_sems, k, to):
    return pltpu.make_async_remote_copy(src_ref=src, dst_ref=dst, send_sem=send_sems.at[k], recv_sem=recv_sems.at[k],
                                        device_id=to, device_id_type=MESH)


def _sibling_swap_halves(tag, grads, after=()):
    n = len(grads)
    after_ops, after_specs = _after_operands(after)

    def body(*refs):
        ins, outs = refs[:n], refs[n + len(after_ops):2 * n + len(after_ops)]
        send_sems, recv_sems = refs[2 * n + len(after_ops):]
        x, y, c, _ = _mesh_place()
        sibling = (x, y, 1 - c)
        copies = []
        for k in range(n):
            half = grads[k].shape[1] // 2
            theirs = pl.ds(pl.multiple_of((1 - c) * half, 16), half)
            cp = _remote(ins[k].at[:, theirs, :], outs[k], send_sems, recv_sems, k, sibling)
            cp.start()
            copies.append(cp)
        for cp in copies:
            cp.wait_recv()
        for cp in copies:
            cp.wait_send()

    hbm = BS(memory_space=pl.ANY)
    return pl.pallas_call(
        body, out_shape=tuple(SDS((g.shape[0], g.shape[1] // 2, g.shape[2]), g.dtype) for g in grads),
        in_specs=[hbm] * n + after_specs, out_specs=(hbm,) * n,
        scratch_shapes=[pltpu.SemaphoreType.DMA((n,)), pltpu.SemaphoreType.DMA((n,))],
        name="reduce_sibling_send_" + tag, compiler_params=_params())(*grads, *after_ops)


def _row_tile(rows, cap=512):
    return max(r for r in range(16, cap + 1, 16) if rows % r == 0)


def _chip_presum(k, grad, got, c_idx):
    n_sh, rows, cols = grad.shape
    half = rows // 2
    tr = _row_tile(half)
    grad4 = grad.reshape(n_sh, 2, half, cols)

    def body(c_ref, a_ref, b_ref, o_ref):
        o_ref[...] = (a_ref[...].astype(F32) + b_ref[...].astype(F32)).astype(o_ref.dtype)

    return pl.pallas_call(
        body, out_shape=SDS((n_sh, half, cols), BF16),
        grid_spec=pltpu.PrefetchScalarGridSpec(
            num_scalar_prefetch=1, grid=(n_sh, half // tr),
            in_specs=[BS((None, None, tr, cols), lambda s, i, c_ref: (s, c_ref[0], i, 0)),
                      BS((None, tr, cols), lambda s, i, c_ref: (s, i, 0))],
            out_specs=BS((None, tr, cols), lambda s, i, c_ref: (s, i, 0))),
        name=f"reduce_presum{k}", compiler_params=_params("parallel", "parallel"))(c_idx, grad4, got)


HBM_SPEC = BS(memory_space=pltpu.HBM)
SEM_SPEC = BS(memory_space=pltpu.SEMAPHORE)
DATAFLOW = pltpu.SideEffectType.DATAFLOW_SIDE_EFFECTING


def _chip_exchange_copies(parts, lands, send_sems, recv_sems):
    _, _, c, chips = _mesh_place()
    return [_remote(parts[k].at[2 * px + py], lands[k].at[j], send_sems, recv_sems, 3 * k + j, (px, py, c))
            for k in range(len(parts)) for j, (px, py) in enumerate(chips)]


def _gather_copies(shards, lands, send_sems, recv_sems):
    x, y, c, chips = _mesh_place()
    return [_remote(shards[k], lands[k].at[2 * x + y], send_sems, recv_sems, 3 * k + j, (px, py, c))
            for k in range(len(shards)) for j, (px, py) in enumerate(chips)]


def _gather_half_copies(shards, lands, send_sems, recv_sems):
    x, y, c, chips = _mesh_place()
    out = []
    for k in range(len(shards)):
        half = shards[k].shape[0] // 2
        mine = pl.ds(pl.multiple_of(c * half, 16), half)
        for j, (px, py) in enumerate(chips):
            out.append(_remote(shards[k].at[mine, :], lands[k].at[2 * x + y, mine, :], send_sems, recv_sems,
                               3 * k + j, (px, py, c)))
    return out


def _sibling_fill(tag, lands):
    n = len(lands)

    def body(*refs):
        outs = refs[n:2 * n]
        send_sems, recv_sems = refs[2 * n:]
        x, y, c, chips = _mesh_place()
        copies = []
        for k in range(n):
            half = lands[k].shape[1] // 2
            mine = pl.ds(pl.multiple_of(c * half, 16), half)
            for j, (px, py) in enumerate(chips):
                blk = outs[k].at[2 * px + py, mine, :]
                copies.append(_remote(blk, blk, send_sems, recv_sems, 3 * k + j, (x, y, 1 - c)))
        for cp in copies:
            cp.start()
        for cp in copies:
            cp.wait_recv()
        for cp in copies:
            cp.wait_send()

    hbm = BS(memory_space=pl.ANY)
    return list(pl.pallas_call(
        body, out_shape=tuple(SDS(a.shape, a.dtype) for a in lands),
        in_specs=[hbm] * n, out_specs=(hbm,) * n, input_output_aliases={k: k for k in range(n)},
        scratch_shapes=[pltpu.SemaphoreType.DMA((3 * n,)), pltpu.SemaphoreType.DMA((3 * n,))],
        name="gather_fill_" + tag, compiler_params=_params())(*lands))


def _swap_copies(grads, lands, send_sems, recv_sems):
    x, y, c, _ = _mesh_place()
    out = []
    for k in range(len(grads)):
        half = grads[k].shape[1] // 2
        theirs = pl.ds(pl.multiple_of((1 - c) * half, 16), half)
        out.append(_remote(grads[k].at[:, theirs, :], lands[k], send_sems, recv_sems, k, (x, y, 1 - c)))
    return out


def _join_copies(fulls, same, send_sems, recv_sems):
    x, y, c, _ = _mesh_place()
    out = []
    for k in range(len(fulls)):
        half = fulls[k].shape[0] // 2
        mine = fulls[k].at[pl.ds(pl.multiple_of(c * half, 8), half), :]
        out.append(_remote(mine, mine, send_sems, recv_sems, k, (x, y, 1 - c)))
    return out


def _everyone_copies(packs, lands, send_sems, recv_sems):
    x, y, c, _ = _mesh_place()
    out = []
    for k in range(len(packs)):
        for j in range(N_DEV - 1):
            bx, by, bc = (j + 1) >> 2 & 1, (j + 1) >> 1 & 1, (j + 1) & 1
            peer = (x ^ bx, y ^ by, c ^ bc)
            out.append(_remote(packs[k], lands[k].at[4 * x + 2 * y + c], send_sems, recv_sems, (N_DEV - 1) * k + j, peer))
    return out


def _split_start(name, copies, sources, land_shapes, after=(), fanout=3):
    n = len(sources)
    n_land = len(land_shapes)
    m = n + n_land
    n_sems = fanout * n
    after_ops, after_specs = _after_operands(after)

    def body(*refs):
        ins = refs[:n]
        lands = refs[n:m] if n_land else ins
        send_sems, recv_sems = refs[m + len(after_ops)], refs[m + len(after_ops) + 1]
        token = refs[-1]
        for cp in copies(ins, lands, send_sems, recv_sems):
            cp.start()
        token[...] = jnp.zeros_like(token)

    lands = [pltpu.with_memory_space_constraint(lax.empty(s, d), pltpu.HBM) for s, d in land_shapes]
    sources = [pltpu.with_memory_space_constraint(p, pltpu.HBM) for p in sources]
    thru = [pltpu.HBM(a.shape, a.dtype) for a in sources + lands]
    out = pl.pallas_call(
        body, name=name,
        out_shape=(pltpu.SemaphoreType.DMA((n_sems,)), pltpu.SemaphoreType.DMA((n_sems,)), *thru, SDS((8, 128), F32)),
        in_specs=[HBM_SPEC] * m + after_specs,
        out_specs=(SEM_SPEC, SEM_SPEC, *[HBM_SPEC] * m, BS(memory_space=pltpu.VMEM)),
        input_output_aliases={i: 2 + i for i in range(m)},
        compiler_params=pltpu.CompilerParams(has_side_effects=DATAFLOW))(*sources, *lands, *after_ops)
    return out[0], out[1], list(out[2:2 + n]), list(out[2 + n:2 + m]), out[-1]


def _split_wait(name, copies, send_sems, recv_sems, sources, lands, after):
    n = len(sources)
    m = n + len(lands)
    after_ops, after_specs = _after_operands(after)

    def body(*refs):
        ins = refs[:n]
        zones = refs[n:m] if m > n else ins
        for cp in copies(ins, zones, refs[m], refs[m + 1]):
            cp.wait_send()
            cp.wait_recv()

    out = pl.pallas_call(
        body, name=name,
        out_shape=tuple(pltpu.HBM(a.shape, a.dtype) for a in sources + lands),
        in_specs=[HBM_SPEC] * m + [SEM_SPEC, SEM_SPEC] + after_specs, out_specs=(HBM_SPEC,) * m,
        input_output_aliases={i: i for i in range(m)},
        compiler_params=pltpu.CompilerParams(has_side_effects=DATAFLOW))(*sources, *lands, send_sems, recv_sems, *after_ops)
    return list(out[:n]), list(out[n:])


class _WeightGatherer:
    def __init__(self, shards):
        self.shards, self.open = shards, {}
        self.me = 2 * lax.axis_index("x") + lax.axis_index("y")

    HALVED = ("f1a",)

    def start(self, tag, after=()):
        shapes = [((N_SHARD,) + s.shape, s.dtype) for s in self.shards[tag]]
        copies = _gather_half_copies if tag in self.HALVED else _gather_copies
        self.open[tag] = _split_start("gather_start_" + tag, copies, self.shards[tag], shapes, after)
        return [self.open[tag][-1]]

    def finish(self, tag, after):
        send_sems, recv_sems, shards, lands, _ = self.open.pop(tag)
        copies = _gather_half_copies if tag in self.HALVED else _gather_copies
        shards, lands = _split_wait("gather_wait_" + tag, copies, send_sems, recv_sems, shards, lands, after)
        if tag in self.HALVED:
            lands = _sibling_fill(tag, lands)
        return [lax.dynamic_update_slice(zone, s[None], (self.me, 0, 0)) for zone, s in zip(lands, shards)]


class _GradReducer:
    def __init__(self):
        self.c_idx = lax.axis_index("c").astype(jnp.int32).reshape(1)
        self.place = jnp.stack([2 * lax.axis_index("x") + lax.axis_index("y"), lax.axis_index("c")]).astype(jnp.int32)
        self.swaps, self.open, self.landed, self.joins, self.reduced = {}, {}, {}, {}, []

    def swap_start(self, tag, grads, after=()):
        shapes = [((g.shape[0], g.shape[1] // 2, g.shape[2]), g.dtype) for g in grads]
        self.swaps[tag] = _split_start("reduce_swap_start_" + tag, _swap_copies, grads, shapes, after, fanout=1)
        return [self.swaps[tag][-1]]

    def start(self, tag, grads, after=(), swapped=()):
        pairs = []
        for s in swapped:
            send_sems, recv_sems, early, lands, _ = self.swaps.pop(s)
            pairs += zip(*_split_wait("reduce_swap_wait_" + s, _swap_copies, send_sems, recv_sems, early, lands, grads[-1:]))
        pairs += zip(grads, _sibling_swap_halves(tag, grads, after))
        parts = [_chip_presum(f"{tag}{k}", g, s, self.c_idx) for k, (g, s) in enumerate(pairs)]
        shapes = [((3,) + p.shape[1:], p.dtype) for p in parts]
        self.open[tag] = _split_start("reduce_exchange_start_" + tag, _chip_exchange_copies, parts, shapes)
        return [self.open[tag][-1]]

    def finish(self, tag, after):
        send_sems, recv_sems, parts, lands, _ = self.open.pop(tag)
        self.landed[tag] = _split_wait("reduce_exchange_wait_" + tag, _chip_exchange_copies, send_sems, recv_sems, parts, lands, after)
        return self.landed[tag][1][:1]

    def _sums(self, tag, after=()):
        parts, landed = self.landed.pop(tag)
        return [_chip_sum(f"{tag}{k}", p, got, self.place, after) for k, (p, got) in enumerate(zip(parts, landed))]

    def join_start(self, tag, after=()):
        self.joins[tag] = _split_start("reduce_join_start_" + tag, _join_copies, self._sums(tag, after), [], fanout=1)
        return [self.joins[tag][-1]]

    def join_finish(self, tag, after):
        send_sems, recv_sems, fulls, _, _ = self.joins.pop(tag)
        self.reduced += _split_wait("reduce_join_wait_" + tag, _join_copies, send_sems, recv_sems, fulls, [], after)[0]

    def join(self, tag, after=()):
        self.reduced += _sibling_join_halves(self._sums(tag), after)


def _chip_sum(k, part, got, place, after=()):
    _, half, cols = part.shape
    tr = _row_tile(half)
    n_t = half // tr
    after_ops, after_specs = _after_operands(after)

    def body(place_ref, a_ref, b_ref, *rest):
        o_ref = rest[-1]
        acc = a_ref[...].astype(F32)
        for j in range(3):
            acc = acc + b_ref[j].astype(F32)
        o_ref[...] = acc

    return pl.pallas_call(
        body, out_shape=SDS((2 * half, cols), F32),
        grid_spec=pltpu.PrefetchScalarGridSpec(
            num_scalar_prefetch=1, grid=(n_t,),
            in_specs=[BS((None, tr, cols), lambda i, place_ref: (place_ref[0], i, 0)),
                      BS((3, tr, cols), lambda i, place_ref: (0, i, 0))] + after_specs,
            out_specs=BS((tr, cols), lambda i, place_ref: (place_ref[1] * n_t + i, 0))),
        name=f"reduce_sum{k}", compiler_params=_params("parallel"))(place, part, got, *after_ops)


def _sibling_join_halves(fulls, after=()):
    n = len(fulls)
    after_ops, after_specs = _after_operands(after)

    def body(*refs):
        outs = refs[n + len(after_ops):2 * n + len(after_ops)]
        send_sems, recv_sems = refs[2 * n + len(after_ops):]
        copies = _join_copies(outs, outs, send_sems, recv_sems)
        for cp in copies:
            cp.start()
        for cp in copies:
            cp.wait_recv()
        for cp in copies:
            cp.wait_send()

    hbm = BS(memory_space=pl.ANY)
    return list(pl.pallas_call(
        body, out_shape=tuple(SDS(f.shape, f.dtype) for f in fulls),
        in_specs=[hbm] * n + after_specs, out_specs=(hbm,) * n, input_output_aliases={k: k for k in range(n)},
        scratch_shapes=[pltpu.SemaphoreType.DMA((n,)), pltpu.SemaphoreType.DMA((n,))],
        name="reduce_sibling_join", compiler_params=_params())(*fulls, *after_ops))


N_DEV = 8


def _sum_devices(packs):
    _, rows, lanes = packs.shape

    def body(p_ref, o_ref):
        acc = p_ref[0]
        for dev in range(1, N_DEV):
            acc = acc + p_ref[dev]
        o_ref[...] = acc

    vm = BS(memory_space=pltpu.VMEM)
    return pl.pallas_call(body, out_shape=SDS((rows, lanes), F32), in_specs=[vm], out_specs=vm,
                          name="small_sum", compiler_params=_params())(packs)


def _adamw(name, w, grad, row0, m, v, after=()):
    rows, cols = w.shape
    tr = rows if rows < 16 else _row_tile(rows, 352)
    bc1 = 1.0 - ADAM_B1 ** ADAM_STEP
    bc2 = 1.0 - ADAM_B2 ** ADAM_STEP
    after_ops, after_specs = _after_operands(after)

    def body(w_ref, g_ref, m_ref, v_ref, *rest):
        go_ref, d_ref, mo_ref, vo_ref = rest[len(after_ops):]
        g = g_ref[...]
        m_new = ADAM_B1 * m_ref[...] + (1.0 - ADAM_B1) * g
        v_new = ADAM_B2 * v_ref[...] + (1.0 - ADAM_B2) * (g * g)
        go_ref[...] = g
        mo_ref[...] = m_new
        vo_ref[...] = v_new
        d_ref[...] = -ADAM_LR * ((m_new / bc1) / (jnp.sqrt(v_new / bc2) + ADAM_EPS) + ADAM_WD * w_ref[...])

    blk = BS((tr, cols), lambda i: (i, 0))
    shape = SDS((rows, cols), F32)
    return pl.pallas_call(
        body, out_shape=(shape,) * 4, grid=(rows // tr,),
        in_specs=[blk, BS((tr, cols), lambda i: (row0 // tr + i, 0)), blk, blk] + after_specs, out_specs=(blk,) * 4,
        name=name, compiler_params=_params("parallel"))(w, grad, m, v, *after_ops)


SMALL_LANES = 128


def _pack_small(parts):
    flat = jnp.concatenate([jnp.ravel(p) for p in parts])
    rows = -(-flat.shape[0] // (64 * SMALL_LANES)) * 64
    return jnp.pad(flat, (0, rows * SMALL_LANES - flat.shape[0])).reshape(rows, SMALL_LANES)


def _unpack_small(packed, like):
    flat = jnp.ravel(packed)
    out, at = [], 0
    for p in like:
        out.append(flat[at:at + p.size].reshape(p.shape))
        at += p.size
    return out


def kernel(x, mem, ffn1_norm, ffn1_w_gate, ffn1_w_up, ffn1_w_down, mix_norm, w_in, pool_w, pool_scale, w_pool_proj, ssm_a_re, ssm_a_im, ssm_log_dt, ssm_b_re, ssm_b_im, ssm_c_re, ssm_c_im, ssm_d, w_glu_val, w_glu_gate, w_mix_out, xattn_norm, mem_norm, w_q, w_kv, w_xo, ffn2_norm, ffn2_w_gate, ffn2_w_up, ffn2_w_down, final_norm, loss_target, m_ffn1_norm, m_ffn1_w_gate, m_ffn1_w_up, m_ffn1_w_down, m_mix_norm, m_w_in, m_pool_w, m_pool_scale, m_w_pool_proj, m_ssm_a_re, m_ssm_a_im, m_ssm_log_dt, m_ssm_b_re, m_ssm_b_im, m_ssm_c_re, m_ssm_c_im, m_ssm_d, m_w_glu_val, m_w_glu_gate, m_w_mix_out, m_xattn_norm, m_mem_norm, m_w_q, m_w_kv, m_w_xo, m_ffn2_norm, m_ffn2_w_gate, m_ffn2_w_up, m_ffn2_w_down, m_final_norm, v_ffn1_norm, v_ffn1_w_gate, v_ffn1_w_up, v_ffn1_w_down, v_mix_norm, v_w_in, v_pool_w, v_pool_scale, v_w_pool_proj, v_ssm_a_re, v_ssm_a_im, v_ssm_log_dt, v_ssm_b_re, v_ssm_b_im, v_ssm_c_re, v_ssm_c_im, v_ssm_d, v_w_glu_val, v_w_glu_gate, v_w_mix_out, v_xattn_norm, v_mem_norm, v_w_q, v_w_kv, v_w_xo, v_ffn2_norm, v_ffn2_w_gate, v_ffn2_w_up, v_ffn2_w_down, v_final_norm):
    given = dict(locals())
    w = {n: given[n] for n in WEIGHTS}
    m = {n: given["m_" + n] for n in WEIGHTS}
    v = {n: given["v_" + n] for n in WEIGHTS}

    def shard_view(a, n):
        return a[0].T if n in TRANSPOSED else a[0]

    def shard_unview(a, n):
        return (a.T if n in TRANSPOSED else a)[None]

    shards = {tag: [jnp.concatenate([shard_view(w[n], n).astype(BF16) for n in grp], axis=0) for grp in arrays]
              for tag, arrays in GATHER_PHASES.items()}
    reducer = _GradReducer()
    ws, ms, vs = ({n: _small_view(a[n], n) for n in SMALL} for a in (w, m, v))
    loss_part, grad_x, _, small = _device_step(x[0], mem[0], loss_target[0], _WeightGatherer(shards), ws, reducer)

    small_like = [ws[n] for n in SMALL] + [loss_part[0, :1]]
    pack = _pack_small([small[n] for n in SMALL] + [loss_part[0, :1]])
    everyone = _split_start("small_start", _everyone_copies, [pack], [((N_DEV,) + pack.shape, F32)], fanout=N_DEV - 1)
    reducer.join("b", after=everyone[-1:])

    grads, delta, new_m, new_v = {}, {}, {}, {}
    big_done = []
    for grp, red in zip(REDUCE_GROUPS, reducer.reduced):
        row0 = 0
        for n in grp:
            w_n = shard_view(w[n], n)
            outs = _adamw("adamw_" + n, w_n, red, row0, shard_view(m[n], n), shard_view(v[n], n), after=everyone[-1:])
            grads[n], delta[n], new_m[n], new_v[n] = (shard_unview(o, n) for o in outs)
            big_done.append(outs[1])
            row0 += w_n.shape[0]

    send_sems, recv_sems, packs, landed, _ = everyone
    packs, landed = _split_wait("small_wait", _everyone_copies, send_sems, recv_sems, packs, landed, big_done)
    mine = 4 * lax.axis_index("x") + 2 * lax.axis_index("y") + lax.axis_index("c")
    summed = _sum_devices(lax.dynamic_update_slice(landed[0], packs[0][None], (mine, 0, 0)))
    g_small = dict(zip(SMALL + ("loss",), _unpack_small(summed, small_like)))
    loss = g_small.pop("loss").reshape(())
    narrow = [n for n in SMALL if ws[n].ndim > 3]
    dense = [n for n in SMALL if n not in narrow]
    for n in narrow:
        two_d = (-1, ws[n].shape[-1])
        outs = _adamw("adamw_" + n, ws[n].reshape(two_d), g_small[n].reshape(two_d), 0, ms[n].reshape(two_d), vs[n].reshape(two_d))
        grads[n], delta[n], new_m[n], new_v[n] = (_small_view(o.reshape(ws[n].shape), n) for o in outs)
    dense_like = [ws[n] for n in dense]
    packed = _adamw("adamw_small", _pack_small(dense_like), _pack_small([g_small[n] for n in dense]), 0,
                    _pack_small([ms[n] for n in dense]), _pack_small([vs[n] for n in dense]))
    for out, store in zip(packed, (grads, delta, new_m, new_v)):
        for n, val in zip(dense, _unpack_small(out, dense_like)):
            store[n] = val

    return (loss, grad_x[None], *[grads[n] for n in WEIGHTS], *[delta[n] for n in WEIGHTS],
            *[new_m[n] for n in WEIGHTS], *[new_v[n] for n in WEIGHTS])
```

```python
import functools
import math

import jax
import jax.numpy as jnp
from jax import lax
from jax.experimental import pallas as pl
from jax.experimental.pallas import tpu as pltpu

F32 = jnp.float32
BF16 = jnp.bfloat16
SDS = jax.ShapeDtypeStruct
BS = pl.BlockSpec
MESH = pl.DeviceIdType.MESH

D_MODEL = 1024
D_FF = 2816
N_SHARD = 4
FF_SH = D_FF // N_SHARD
D_POOL = 512
POOL_WINDOWS = (2, 4, 8, 16)
POOL_GROUP = 128
D_SSM = 256
SSM_GROUPS = 16
SSM_GROUP = 16
SSM_STATE = 64
SSM_CH = SSM_GROUPS * SSM_STATE
N_HEADS = 4
HEAD_DIM = 256
EPS = 1e-6
ADAM_LR, ADAM_B1, ADAM_B2, ADAM_EPS, ADAM_WD, ADAM_STEP = 0.001, 0.9, 0.999, 1e-08, 0.01, 10

VMEM_LIMIT_V7X = 52 * 1024 * 1024
TM = 512

NN = (((1,), (0,)), ((), ()))
NT = (((1,), (1,)), ((), ()))
TN = (((0,), (0,)), ((), ()))


def _params(*sem):
    return pltpu.CompilerParams(dimension_semantics=sem if sem else None, vmem_limit_bytes=VMEM_LIMIT_V7X)


def _dot(a, b, dims=NN):
    return lax.dot_general(a.astype(BF16), b.astype(BF16), dims, preferred_element_type=F32)


def _sigmoid(v):
    return pl.reciprocal(1.0 + jnp.exp(-v), approx=True)


def _block_dims(spec):
    return tuple(d for d in spec.block_shape if d is not None)


def _after_operands(after):
    return list(after), [BS(memory_space=pl.ANY)] * len(after)


def _mm(name, pairs, *, grid, out_shape, out_spec, red_axis=None, extras=(), epilogue=None, after=()):
    n_pairs, n_extra = len(pairs), len(extras)
    n_red = grid[red_axis] if red_axis is not None else 1
    dims = [p[4] for p in pairs]

    def body(*refs):
        ab = refs[:2 * n_pairs]
        ex = refs[2 * n_pairs:2 * n_pairs + n_extra]
        o_ref = refs[2 * n_pairs + n_extra + len(after)]

        def partial():
            acc = None
            for p in range(n_pairs):
                t = _dot(ab[2 * p][...], ab[2 * p + 1][...], dims[p])
                acc = t if acc is None else acc + t
            return acc

        def finish(acc):
            res = epilogue(acc, *[e[...] for e in ex]) if epilogue is not None else acc
            o_ref[...] = res.astype(o_ref.dtype)

        if n_red == 1:
            finish(partial())
        else:
            acc_ref = refs[-1]
            k = pl.program_id(red_axis)

            @pl.when(k == 0)
            def _():
                acc_ref[...] = jnp.zeros_like(acc_ref)

            acc_ref[...] += partial()

            @pl.when(k == n_red - 1)
            def _():
                finish(acc_ref[...])

    operands, in_specs = [], []
    for a, a_spec, b, b_spec, _ in pairs:
        operands += [a, b]
        in_specs += [a_spec, b_spec]
    for e, e_spec in extras:
        operands.append(e)
        in_specs.append(e_spec)
    after_ops, after_specs = _after_operands(after)
    operands += after_ops
    in_specs += after_specs
    scratch = [pltpu.VMEM(_block_dims(out_spec), F32)] if n_red > 1 else []
    sem = tuple("arbitrary" if ax == red_axis else "parallel" for ax in range(len(grid)))
    return pl.pallas_call(body, out_shape=out_shape, grid=grid, in_specs=in_specs, out_specs=out_spec,
                          scratch_shapes=scratch, name=name, compiler_params=_params(*sem))(*operands)


def _rmsnorm(name, h, gain, tm, after=()):
    t, d = h.shape
    after_ops, after_specs = _after_operands(after)

    def body(h_ref, g_ref, *rest):
        u_ref = rest[-1]
        hv = h_ref[...]
        r = lax.rsqrt(jnp.mean(hv * hv, axis=-1, keepdims=True) + EPS)
        u_ref[...] = ((hv * r) * g_ref[...]).astype(u_ref.dtype)

    return pl.pallas_call(
        body, out_shape=SDS((t, d), BF16), grid=(t // tm,),
        in_specs=[BS((tm, d), lambda i: (i, 0)), BS((1, d), lambda i: (0, 0))] + after_specs,
        out_specs=BS((tm, d), lambda i: (i, 0)), name=name, compiler_params=_params("parallel"))(h, gain, *after_ops)


def _rmsnorm_bwd(name, h, gain, du, dh_in, tm):
    t, d = h.shape
    has_in = dh_in is not None

    def body(*refs):
        if has_in:
            h_ref, g_ref, du_ref, dhin_ref, dh_ref, dhb_ref, dg_ref = refs
        else:
            h_ref, g_ref, du_ref, dh_ref, dhb_ref, dg_ref = refs
        i = pl.program_id(0)
        hv = h_ref[...]
        r = lax.rsqrt(jnp.mean(hv * hv, axis=-1, keepdims=True) + EPS)
        n = hv * r
        duv = du_ref[...].astype(F32)
        dn = duv * g_ref[...]
        dh = r * (dn - n * jnp.mean(dn * n, axis=-1, keepdims=True))
        if has_in:
            dh = dhin_ref[...] + dh
        dh_ref[...] = dh
        dhb_ref[...] = dh.astype(BF16)

        @pl.when(i == 0)
        def _():
            dg_ref[...] = jnp.zeros_like(dg_ref)

        dg_ref[...] += jnp.sum(duv * n, axis=0, keepdims=True)

    row = BS((tm, d), lambda i: (i, 0))
    vec = BS((1, d), lambda i: (0, 0))
    operands = [h, gain, du] + ([dh_in] if has_in else [])
    in_specs = [row, vec, row] + ([row] if has_in else [])
    return pl.pallas_call(
        body, out_shape=(SDS((t, d), F32), SDS((t, d), BF16), SDS((1, d), F32)), grid=(t // tm,),
        in_specs=in_specs, out_specs=(row, row, vec), name=name, compiler_params=_params("arbitrary"))(*operands)


def _loss_head_tile(i, hv, g_ref, t_ref, loss_ref, dh_ref, dhb_ref, dg_ref):
    g = g_ref[...]
    r = lax.rsqrt(jnp.mean(hv * hv, axis=-1, keepdims=True) + EPS)
    n = hv * r
    err = n * g - t_ref[...]
    dy = err * (1.0 / hv.shape[-1])
    dn = dy * g
    dh = r * (dn - n * jnp.mean(dn * n, axis=-1, keepdims=True))
    dh_ref[...] = dh
    dhb_ref[...] = dh.astype(BF16)

    @pl.when(i == 0)
    def _():
        dg_ref[...] = jnp.zeros_like(dg_ref)
        loss_ref[...] = jnp.zeros_like(loss_ref)

    dg_ref[...] += jnp.sum(dy * n, axis=0, keepdims=True)
    part = 0.5 * jnp.sum(jnp.mean(err * err, axis=-1, keepdims=True), axis=0, keepdims=True)
    loss_ref[...] += jnp.broadcast_to(part, loss_ref.shape)


def _norm_tile(h, g_ref, u_ref):
    r = lax.rsqrt(jnp.mean(h * h, axis=-1, keepdims=True) + EPS)
    u_ref[...] = ((h * r) * g_ref[...]).astype(u_ref.dtype)


FFN_GATE, FFN_UP, FFN_DOWN = 0, 1, 2


def _ffn_up(name, u, w_f, tm, after=()):
    t, d = u.shape
    after_ops, after_specs = _after_operands(after)

    def body(u_ref, wg_ref, wu_ref, *rest):
        pg_ref, pu_ref, a_ref = rest[len(after_ops):]
        uv = u_ref[...]
        for s in range(N_SHARD):
            g = _dot(uv, wg_ref[s], NT)
            up = _dot(uv, wu_ref[s], NT)
            sg = _sigmoid(g)
            silu = g * sg
            a_ref[s] = (silu * up).astype(BF16)
            pu_ref[s] = (0.5 * silu).astype(BF16)
            pg_ref[s] = (0.5 * sg * (1.0 + g * (1.0 - sg)) * up).astype(BF16)

    hid = BS((N_SHARD, tm, FF_SH), lambda i: (0, i, 0))
    shape = SDS((N_SHARD, t, FF_SH), BF16)
    return pl.pallas_call(
        body, out_shape=(shape, shape, shape), grid=(t // tm,),
        in_specs=[BS((tm, d), lambda i: (i, 0)), _ffn_all_shards_spec(w_f["gate"][1]),
                  _ffn_all_shards_spec(w_f["up"][1])] + after_specs,
        out_specs=(hid, hid, hid), name=name,
        compiler_params=_params("parallel"))(u, w_f["gate"][0], w_f["up"][0], *after_ops)


def _ffn_all_shards_spec(block):
    return BS((N_SHARD, FF_SH, D_MODEL), lambda i: (0, block, 0))


def _ffn_down(name, a, w_f, resid, tm, next_gain=None, head=None):
    t, d = resid.shape
    row = BS((tm, d), lambda i: (i, 0))
    vec = BS((1, d), lambda i: (0, 0))

    def body(a_ref, w_ref, res_ref, *rest):
        acc = _dot(a_ref[0], w_ref[0])
        for s in range(1, N_SHARD):
            acc = acc + _dot(a_ref[s], w_ref[s])
        h = res_ref[...] + 0.5 * acc
        if head is not None:
            _loss_head_tile(pl.program_id(0), h, *rest)
        else:
            g_ref, h_ref, u_ref = rest
            h_ref[...] = h
            _norm_tile(h, g_ref, u_ref)

    if head is not None:
        extra, extra_specs = list(head), [vec, row]
        out_shape = (SDS((1, 128), F32), SDS((t, d), F32), SDS((t, d), BF16), SDS((1, d), F32))
        out_specs = (BS((1, 128), lambda i: (0, 0)), row, row, vec)
    else:
        extra, extra_specs = [next_gain], [vec]
        out_shape = (SDS((t, d), F32), SDS((t, d), BF16))
        out_specs = (row, row)
    return pl.pallas_call(
        body, out_shape=out_shape, grid=(t // tm,),
        in_specs=[BS((N_SHARD, tm, FF_SH), lambda i: (0, i, 0)), _ffn_all_shards_spec(w_f["down"][1]), row] + extra_specs,
        out_specs=out_specs, name=name,
        compiler_params=_params("arbitrary" if head is not None else "parallel"))(a, w_f["down"][0], resid, *extra)


def _mm_resid_norm(name, a, b, resid, next_gain, tm):
    t, d = resid.shape

    def body(a_ref, b_ref, res_ref, g_ref, h_ref, u_ref):
        h = res_ref[...] + _dot(a_ref[...], b_ref[...])
        h_ref[...] = h
        _norm_tile(h, g_ref, u_ref)

    row = BS((tm, d), lambda i: (i, 0))
    return pl.pallas_call(
        body, out_shape=(SDS((t, d), F32), SDS((t, d), BF16)), grid=(t // tm,),
        in_specs=[BS((tm, a.shape[1]), lambda i: (i, 0)), BS(b.shape, lambda i: (0, 0)), row, BS((1, d), lambda i: (0, 0))],
        out_specs=(row, row), name=name, compiler_params=_params("parallel"))(a, b, resid, next_gain)


def _ffn_bwd_act(name, dh_b, w_f, pg, pu, tm, after=()):
    t, d = dh_b.shape
    after_ops, after_specs = _after_operands(after)

    def body(dh_ref, wd_ref, pg_ref, pu_ref, *rest):
        dg_ref, dup_ref = rest[len(after_ops):]
        dh = dh_ref[...]
        for s in range(N_SHARD):
            da = _dot(dh, wd_ref[s], NT)
            dg_ref[s] = (da * pg_ref[s].astype(F32)).astype(BF16)
            dup_ref[s] = (da * pu_ref[s].astype(F32)).astype(BF16)

    hid = BS((N_SHARD, tm, FF_SH), lambda i: (0, i, 0))
    shape = SDS((N_SHARD, t, FF_SH), BF16)
    return pl.pallas_call(
        body, out_shape=(shape, shape), grid=(t // tm,),
        in_specs=[BS((tm, d), lambda i: (i, 0)), _ffn_all_shards_spec(w_f["down"][1]), hid, hid] + after_specs,
        out_specs=(hid, hid), name=name,
        compiler_params=_params("parallel"))(dh_b, w_f["down"][0], pg, pu, *after_ops)


def _ffn_dw(name, u, dg, dup, a, dh_b, tm):
    t, d = u.shape
    n_t = t // tm

    def body(u_ref, dg_ref, dup_ref, a_ref, dh_ref, o_ref, acc):
        i = pl.program_id(1)

        @pl.when(i == 0)
        def _():
            acc[...] = jnp.zeros_like(acc)

        uv = u_ref[...]
        acc[FFN_GATE] += _dot(dg_ref[...], uv, TN)
        acc[FFN_UP] += _dot(dup_ref[...], uv, TN)
        acc[FFN_DOWN] += _dot(a_ref[...], dh_ref[...], TN)

        @pl.when(i == n_t - 1)
        def _():
            o_ref[FFN_GATE] = acc[FFN_GATE].astype(BF16)
            o_ref[FFN_UP] = acc[FFN_UP].astype(BF16)
            o_ref[FFN_DOWN] = (0.5 * acc[FFN_DOWN]).astype(BF16)

    hid = BS((None, tm, FF_SH), lambda s, i: (s, i, 0))
    row = BS((tm, d), lambda s, i: (i, 0))
    return pl.pallas_call(
        body, out_shape=SDS((N_SHARD, 3, FF_SH, d), BF16), grid=(N_SHARD, n_t),
        in_specs=[row, hid, hid, hid, row], out_specs=BS((None, 3, FF_SH, d), lambda s, i: (s, 0, 0, 0)),
        scratch_shapes=[pltpu.VMEM((3, FF_SH, d), F32)],
        name=name, compiler_params=_params("parallel", "arbitrary"))(u, dg, dup, a, dh_b)


def _norm_bwd_tile(i, du, h_ref, g_ref, dhin_ref, dh_ref, dhb_ref, dg_ref):
    hv = h_ref[...]
    r = lax.rsqrt(jnp.mean(hv * hv, axis=-1, keepdims=True) + EPS)
    n = hv * r
    dn = du * g_ref[...]
    dh = dhin_ref[...] + r * (dn - n * jnp.mean(dn * n, axis=-1, keepdims=True))
    dh_ref[...] = dh
    dhb_ref[...] = dh.astype(BF16)

    @pl.when(i == 0)
    def _():
        dg_ref[...] = jnp.zeros_like(dg_ref)

    dg_ref[...] += jnp.sum(du * n, axis=0, keepdims=True)


def _norm_bwd_specs(tm):
    row = BS((tm, D_MODEL), lambda i: (i, 0))
    vec = BS((1, D_MODEL), lambda i: (0, 0))
    return [row, vec, row], (row, row, vec)


def _norm_bwd_shapes(t):
    return SDS((t, D_MODEL), F32), SDS((t, D_MODEL), BF16), SDS((1, D_MODEL), F32)


def _ffn_dx(name, dg, dup, w_f, h, gain, dh_in, tm, after=()):
    t = dg.shape[1]
    tm = tm // 2
    after_ops, after_specs = _after_operands(after)

    def body(dg_ref, dup_ref, wg_ref, wu_ref, h_ref, g_ref, dhin_ref, *rest):
        acc = _dot(dg_ref[0], wg_ref[0]) + _dot(dup_ref[0], wu_ref[0])
        for s in range(1, N_SHARD):
            acc = acc + _dot(dg_ref[s], wg_ref[s]) + _dot(dup_ref[s], wu_ref[s])
        _norm_bwd_tile(pl.program_id(0), acc, h_ref, g_ref, dhin_ref, *rest[len(after_ops):])

    hid = BS((N_SHARD, tm, FF_SH), lambda i: (0, i, 0))
    norm_in, norm_out = _norm_bwd_specs(tm)
    return pl.pallas_call(
        body, out_shape=_norm_bwd_shapes(t), grid=(t // tm,),
        in_specs=[hid, hid, _ffn_all_shards_spec(w_f["gate"][1]), _ffn_all_shards_spec(w_f["up"][1])] + norm_in + after_specs,
        out_specs=norm_out, name=name,
        compiler_params=_params("arbitrary"))(dg, dup, w_f["gate"][0], w_f["up"][0], h, gain, dh_in, *after_ops)


def _mm_norm_bwd(name, a, b, dims, h, gain, dh_in, tm):
    t = a.shape[0]

    def body(a_ref, b_ref, h_ref, g_ref, dhin_ref, *outs):
        _norm_bwd_tile(pl.program_id(0), _dot(a_ref[...], b_ref[...], dims), h_ref, g_ref, dhin_ref, *outs)

    norm_in, norm_out = _norm_bwd_specs(tm)
    return pl.pallas_call(
        body, out_shape=_norm_bwd_shapes(t), grid=(t // tm,),
        in_specs=[BS((tm, a.shape[1]), lambda i: (i, 0)), BS(b.shape, lambda i: (0, 0))] + norm_in,
        out_specs=norm_out, name=name, compiler_params=_params("arbitrary"))(a, b, h, gain, dh_in)


def _plain_mm(name, a, b, dims, out_dtype, tm, resid=None, after=()):
    t = a.shape[0]
    n = b.shape[1] if dims == NN else b.shape[0]
    extras = [(resid, BS((tm, n), lambda i: (i, 0)))] if resid is not None else []
    epi = (lambda acc, res: res + acc) if resid is not None else None
    return _mm(name, [(a, BS((tm, a.shape[1]), lambda i: (i, 0)), b, BS(b.shape, lambda i: (0, 0)), dims)],
               grid=(t // tm,), out_shape=SDS((t, n), out_dtype), out_spec=BS((tm, n), lambda i: (i, 0)),
               extras=extras, epilogue=epi, after=after)


def _dw_mm(name, a, b, tm, out_dtype=BF16, after=()):
    t, k = a.shape
    n = b.shape[1]
    return _mm(name, [(a, BS((tm, k), lambda i: (i, 0)), b, BS((tm, n), lambda i: (i, 0)), TN)],
               grid=(t // tm,), red_axis=0, out_shape=SDS((k, n), out_dtype), out_spec=BS((k, n), lambda i: (0, 0)),
               after=after)


POOL_CHUNK = 256
POOL_HALO = 8


def _window_sum(v, width, lead):
    n = v.shape[0]
    s = v
    k = 1
    while k < width:
        s = s + pltpu.roll(s, n - k, 0)
        k *= 2
    return pltpu.roll(s, lead, 0) if lead else s


def _pool_count(base, left, right, t, shape):
    pos = base + lax.broadcasted_iota(jnp.int32, shape, 0)
    lo = jnp.maximum(pos - left, 0)
    hi = jnp.minimum(pos + right + 1, t)
    return (hi - lo).astype(F32)


def _pool_fwd(proj, pool_w, pool_scale):
    t = proj.shape[0]
    c, h = POOL_CHUNK, POOL_HALO
    n_chunks = t // c

    def body(proj_hbm, pw_ref, sc_ref, pooled_ref, mixed_ref, ms_ref, pad_ref, sem):
        cp = pltpu.make_async_copy(proj_hbm.at[:, pl.ds(0, D_POOL)], pad_ref.at[pl.ds(h, t), :], sem)
        cp.start()
        pad_ref[pl.ds(0, h), :] = jnp.zeros((h, D_POOL), F32)
        pad_ref[pl.ds(t + h, h), :] = jnp.zeros((h, D_POOL), F32)
        cp.wait()
        for g, width in enumerate(POOL_WINDOWS):
            left = width // 2
            right = width - 1 - left
            cols = slice(g * POOL_GROUP, (g + 1) * POOL_GROUP)
            wmat = pw_ref[g].astype(BF16)
            scale = sc_ref[:, cols]

            def chunk(ci, carry, left=left, right=right, width=width, cols=cols, wmat=wmat, scale=scale):
                base = pl.multiple_of(ci * c, c)
                v = pad_ref[pl.ds(base, c + 2 * h), cols]
                win = _window_sum(v, width, left)[h:h + c]
                cnt = _pool_count(base, left, right, t, (c, POOL_GROUP))
                pooled = (win / cnt - v[h:h + c]).astype(BF16)
                mixed = _dot(pooled, wmat)
                pooled_ref[pl.ds(base, c), cols] = pooled
                mixed_ref[pl.ds(base, c), cols] = mixed.astype(BF16)
                ms_ref[pl.ds(base, c), cols] = (mixed * scale).astype(BF16)
                return carry

            lax.fori_loop(0, n_chunks, chunk, 0)

    vm = BS(memory_space=pltpu.VMEM)
    shape = SDS((t, D_POOL), BF16)
    return pl.pallas_call(
        body, out_shape=(shape, shape, shape),
        in_specs=[BS(memory_space=pl.ANY), vm, vm], out_specs=(vm, vm, vm),
        scratch_shapes=[pltpu.VMEM((t + 2 * h, D_POOL), F32), pltpu.SemaphoreType.DMA],
        name="pool_fwd", compiler_params=_params())(proj, pool_w, pool_scale)


def _pool_bwd(d_ms, mixed, pooled, pool_w, pool_scale):
    t = d_ms.shape[0]
    c, h = POOL_CHUNK, POOL_HALO
    n_chunks = t // c

    def body(dms_ref, mixed_ref, pooled_ref, pw_ref, sc_ref, dp_ref, dsc_ref, dpw_ref, pad_ref):
        pad_ref[pl.ds(0, h), :] = jnp.zeros((h, D_POOL), F32)
        pad_ref[pl.ds(t + h, h), :] = jnp.zeros((h, D_POOL), F32)
        for g, width in enumerate(POOL_WINDOWS):
            left = width // 2
            right = width - 1 - left
            cols = slice(g * POOL_GROUP, (g + 1) * POOL_GROUP)
            wmat = pw_ref[g].astype(BF16)
            scale = sc_ref[:, cols]

            def first(ci, carry, left=left, right=right, cols=cols, wmat=wmat, scale=scale):
                dsc, dpw = carry
                base = pl.multiple_of(ci * c, c)
                dms = dms_ref[pl.ds(base, c), cols].astype(F32)
                dsc = dsc + jnp.sum(dms * mixed_ref[pl.ds(base, c), cols].astype(F32), axis=0, keepdims=True)
                dmix = (dms * scale).astype(BF16)
                dpw = dpw + _dot(pooled_ref[pl.ds(base, c), cols], dmix, TN)
                dpooled = _dot(dmix, wmat, NT)
                cnt = _pool_count(base, left, right, t, (c, POOL_GROUP))
                pad_ref[pl.ds(base + h, c), cols] = dpooled / cnt
                return dsc, dpw

            dsc, dpw = lax.fori_loop(0, n_chunks, first,
                                     (jnp.zeros((1, POOL_GROUP), F32), jnp.zeros((POOL_GROUP, POOL_GROUP), F32)))
            dsc_ref[:, cols] = dsc
            dpw_ref[g] = dpw

            def second(ci, carry, left=left, right=right, width=width, cols=cols):
                base = pl.multiple_of(ci * c, c)
                v = pad_ref[pl.ds(base, c + 2 * h), cols]
                win = _window_sum(v, width, right)[h:h + c]
                cnt = _pool_count(base, left, right, t, (c, POOL_GROUP))
                dp_ref[pl.ds(base, c), cols] = (win - v[h:h + c] * cnt).astype(BF16)
                return carry

            lax.fori_loop(0, n_chunks, second, 0)

    vm = BS(memory_space=pltpu.VMEM)
    return pl.pallas_call(
        body, out_shape=(SDS((t, D_POOL), BF16), SDS((1, D_POOL), F32), SDS((4, POOL_GROUP, POOL_GROUP), F32)),
        in_specs=[vm] * 5, out_specs=(vm, vm, vm),
        scratch_shapes=[pltpu.VMEM((t + 2 * h, D_POOL), F32)],
        name="pool_bwd", compiler_params=_params())(d_ms, mixed, pooled, pool_w, pool_scale)


SSM_ROWS = 2 * SSM_GROUPS * SSM_GROUP
SSM_HALF = SSM_GROUPS * SSM_GROUP


def _ssm_zoh(a_r, a_i, ldt):
    dt = jnp.exp(ldt)
    mag = jnp.exp(dt * a_r)
    ang = dt * a_i
    cs, sn = jnp.cos(ang), jnp.sin(ang)
    abr, abi = mag * cs, mag * sn
    den = a_r * a_r + a_i * a_i
    nr = abr - 1.0
    qr = (nr * a_r + abi * a_i) / den
    qi = (abi * a_r - nr * a_i) / den
    return dt, mag, cs, sn, abr, abi, den, nr, qr, qi


def _ssm_group_mask():
    row = lax.broadcasted_iota(jnp.int32, (SSM_HALF, SSM_CH), 0)
    col = lax.broadcasted_iota(jnp.int32, (SSM_HALF, SSM_CH), 1)
    return (row // SSM_GROUP) == (col // SSM_STATE)


def _ssm_prep(a_r, a_i, ldt, b_r, b_i, c_r, c_i, after=()):
    after_ops, after_specs = _after_operands(after)

    def body(ar_ref, ai_ref, ldt_ref, br_ref, bi_ref, cr_ref, ci_ref, *rest):
        abr_ref, abi_ref, win_ref, wint_ref, woutt_ref, wout_ref = rest[len(after_ops):]
        *_, abr, abi, _, _, qr, qi = _ssm_zoh(ar_ref[...], ai_ref[...], ldt_ref[...])
        abr_ref[...] = abr
        abi_ref[...] = abi
        b_r, b_i = br_ref[...], bi_ref[...]
        bbr = qr * b_r - qi * b_i
        bbi = qr * b_i + qi * b_r
        mask = _ssm_group_mask()
        state = lax.broadcasted_iota(jnp.int32, (SSM_STATE, SSM_CH), 0)
        col = lax.broadcasted_iota(jnp.int32, (SSM_STATE, SSM_CH), 1)
        every_group = (col % SSM_STATE == state).astype(BF16)

        def spread(x):
            return jnp.where(mask, _dot(x, every_group), 0.0)

        for d in range(2):
            rows = slice(d * SSM_HALF, (d + 1) * SSM_HALF)
            for half, x_in, x_out in ((0, bbr[rows], cr_ref[rows, :]), (1, bbi[rows], -ci_ref[rows, :])):
                cols = slice(half * SSM_CH, (half + 1) * SSM_CH)
                m_in, m_out = spread(x_in), spread(x_out)
                win_ref[d, :, cols] = m_in.astype(BF16)
                wint_ref[d, cols, :] = m_in.T.astype(BF16)
                woutt_ref[d, :, cols] = m_out.astype(BF16)
                wout_ref[d, cols, :] = m_out.T.astype(BF16)

    vm = BS(memory_space=pltpu.VMEM)
    vec = SDS((SSM_ROWS, SSM_STATE), F32)
    wide = SDS((2, SSM_HALF, 2 * SSM_CH), BF16)
    tall = SDS((2, 2 * SSM_CH, SSM_HALF), BF16)
    return pl.pallas_call(body, out_shape=(vec, vec, wide, tall, wide, tall), in_specs=[vm] * 7 + after_specs,
                          out_specs=(vm,) * 6, name="ssm_prep",
                          compiler_params=_params())(a_r, a_i, ldt, b_r, b_i, c_r, c_i, *after_ops)


def _ssm_prep_bwd(a_r, a_i, ldt, b_r, b_i, g_abr, g_abi, d_win, d_woutt):
    def body(ar_ref, ai_ref, ldt_ref, br_ref, bi_ref, gabr_ref, gabi_ref, dwin_ref, dwoutt_ref,
             dar_ref, dai_ref, dldt_ref, dbr_ref, dbi_ref, dcr_ref, dci_ref):
        a_r, a_i = ar_ref[...], ai_ref[...]
        dt, mag, cs, sn, abr, abi, den, nr, qr, qi = _ssm_zoh(a_r, a_i, ldt_ref[...])
        mask = _ssm_group_mask()
        col = lax.broadcasted_iota(jnp.int32, (SSM_CH, SSM_STATE), 0)
        state = lax.broadcasted_iota(jnp.int32, (SSM_CH, SSM_STATE), 1)
        own_state = (col % SSM_STATE == state).astype(BF16)

        def pick(dense):
            m = jnp.where(mask, dense, 0.0)
            hi = m.astype(BF16)
            lo = m - hi.astype(F32)
            return _dot(hi, own_state) + _dot(lo, own_state)

        def picked(ref, half):
            cols = slice(half * SSM_CH, (half + 1) * SSM_CH)
            return jnp.concatenate([pick(ref[d, :, cols]) for d in range(2)], axis=0)

        g_r, g_i = picked(dwin_ref, 0), picked(dwin_ref, 1)
        dcr_ref[...] = picked(dwoutt_ref, 0)
        dci_ref[...] = -picked(dwoutt_ref, 1)
        b_r, b_i = br_ref[...], bi_ref[...]
        dbr_ref[...] = g_r * qr + g_i * qi
        dbi_ref[...] = g_i * qr - g_r * qi
        gqr = g_r * b_r + g_i * b_i
        gqi = g_i * b_r - g_r * b_i
        g_nr_num = gqr / den
        g_ni_num = gqi / den
        g_den = -(gqr * qr + gqi * qi) / den
        g_nr = g_nr_num * a_r - g_ni_num * a_i
        g_abi = g_nr_num * a_i + g_ni_num * a_r
        d_ar = g_nr_num * nr + g_ni_num * abi + 2.0 * a_r * g_den
        d_ai = g_nr_num * abi - g_ni_num * nr + 2.0 * a_i * g_den
        g_abr = gabr_ref[...] + g_nr
        g_abi = gabi_ref[...] + g_abi
        g_mag = g_abr * cs + g_abi * sn
        g_ang = mag * (g_abi * cs - g_abr * sn)
        g_e = g_mag * mag
        d_ar = d_ar + g_e * dt
        d_ai = d_ai + g_ang * dt
        g_dt = g_e * a_r + g_ang * a_i
        dar_ref[...] = d_ar
        dai_ref[...] = d_ai
        dldt_ref[...] = g_dt * dt

    vm = BS(memory_space=pltpu.VMEM)
    vec = SDS((SSM_ROWS, SSM_STATE), F32)
    return pl.pallas_call(body, out_shape=(vec,) * 7, in_specs=[vm] * 9, out_specs=(vm,) * 7, name="ssm_prep_bwd",
                          compiler_params=_params())(a_r, a_i, ldt, b_r, b_i, g_abr, g_abi, d_win, d_woutt)


SCAN_ROWS = 512
SCAN_SUB = 128


def _ssm_scan(name, inp, w1, a_r, a_i, w2, reverse):
    t = inp.shape[0]
    rows = min(SCAN_ROWS, t)
    n = t // rows
    n_sub = rows // SCAN_SUB
    ch = SSM_CH
    at = (lambda i: (n - 1 - i, 0)) if reverse else (lambda i: (i, 0))

    def body(in_ref, w1_ref, ar_ref, ai_ref, w2_ref, sb_ref, out_ref, cr_ref, ci_ref, k_ref, st_ref):
        i = pl.program_id(0)

        @pl.when(i == 0)
        def _():
            ar8 = jnp.broadcast_to(ar_ref[...], (8, ch))
            ai8 = jnp.broadcast_to(ai_ref[...], (8, ch))
            row = lax.broadcasted_iota(jnp.int32, (8, ch), 0)
            rank = (7 - row) if reverse else row
            powers = [(ar8, ai8)]
            for _ in range(7):
                p_r, p_i = powers[-1]
                powers.append((p_r * ar8 - p_i * ai8, p_r * ai8 + p_i * ar8))
            zero = jnp.zeros((8, ch), F32)
            for slot, k in enumerate((1, 2, 4)):
                k_ref[2 * slot] = jnp.where(rank >= k, powers[k - 1][0], zero)
                k_ref[2 * slot + 1] = jnp.where(rank >= k, powers[k - 1][1], zero)
            carry_r, carry_i = zero, zero
            for j in range(8):
                carry_r = jnp.where(rank == j, powers[j][0], carry_r)
                carry_i = jnp.where(rank == j, powers[j][1], carry_i)
            k_ref[6] = carry_r
            k_ref[7] = carry_i
            cr_ref[...] = zero
            ci_ref[...] = zero

        def group(r0, carry):
            c_r, c_i = carry
            x_r = st_ref[pl.ds(r0, 8), 0:ch]
            x_i = st_ref[pl.ds(r0, 8), ch:2 * ch]
            for slot, k in enumerate((1, 2, 4)):
                shift = (8 - k) if reverse else k
                s_r = pltpu.roll(x_r, shift, 0)
                s_i = pltpu.roll(x_i, shift, 0)
                m_r, m_i = k_ref[2 * slot], k_ref[2 * slot + 1]
                x_r, x_i = x_r + m_r * s_r - m_i * s_i, x_i + m_r * s_i + m_i * s_r
            p_r, p_i = k_ref[6], k_ref[7]
            x_r, x_i = x_r + p_r * c_r - p_i * c_i, x_i + p_r * c_i + p_i * c_r
            st_ref[pl.ds(r0, 8), 0:ch] = x_r
            st_ref[pl.ds(r0, 8), ch:2 * ch] = x_i
            last = 0 if reverse else 7
            return (jnp.broadcast_to(x_r[last:last + 1, :], (8, ch)), jnp.broadcast_to(x_i[last:last + 1, :], (8, ch)))

        carry = (cr_ref[...], ci_ref[...])
        for sc in (range(n_sub - 1, -1, -1) if reverse else range(n_sub)):
            part = pl.ds(sc * SCAN_SUB, SCAN_SUB)
            st_ref[part, :] = _dot(in_ref[part, :], w1_ref[...])
            for gi in range(SCAN_SUB // 8):
                g = (SCAN_SUB // 8 - 1 - gi) if reverse else gi
                carry = group(sc * SCAN_SUB + g * 8, carry)
            states = st_ref[part, :].astype(BF16)
            sb_ref[part, :] = states
            out_ref[part, :] = _dot(states, w2_ref[...])
        cr_ref[...] = carry[0]
        ci_ref[...] = carry[1]

    return pl.pallas_call(
        body, out_shape=(SDS((t, 2 * ch), BF16), SDS((t, D_SSM), F32)), grid=(n,),
        in_specs=[BS((rows, D_SSM), at), BS((D_SSM, 2 * ch), lambda i: (0, 0)), BS((1, ch), lambda i: (0, 0)),
                  BS((1, ch), lambda i: (0, 0)), BS((2 * ch, D_SSM), lambda i: (0, 0))],
        out_specs=(BS((rows, 2 * ch), at), BS((rows, D_SSM), at)),
        scratch_shapes=[pltpu.VMEM((8, ch), F32), pltpu.VMEM((8, ch), F32), pltpu.VMEM((8, 8, ch), F32),
                        pltpu.VMEM((rows, 2 * ch), F32)],
        name=name, compiler_params=_params("arbitrary"))(inp, w1, a_r, a_i, w2)


DA_ROWS = 1024


def _ssm_da(name, lam, states, reverse, after=()):
    t = lam.shape[0]
    rows = min(DA_ROWS, t)
    n = t // rows
    halo_rows = 16
    nb = rows // halo_rows
    ch = SSM_CH
    if reverse:
        halo_at = lambda i: (jnp.minimum((i + 1) * nb, t // halo_rows - 1), 0)
    else:
        halo_at = lambda i: (jnp.maximum(i * nb - 1, 0), 0)

    after_ops, after_specs = _after_operands(after)

    def body(lam_ref, x_ref, halo_ref, *rest):
        dr_ref, di_ref = rest[len(after_ops):]
        i = pl.program_id(0)

        @pl.when(i == 0)
        def _():
            dr_ref[...] = jnp.zeros_like(dr_ref)
            di_ref[...] = jnp.zeros_like(di_ref)

        row = lax.broadcasted_iota(jnp.int32, (rows, ch), 0)
        if reverse:
            edge, shift, h_row, live = rows - 1, rows - 1, 0, i < n - 1
        else:
            edge, shift, h_row, live = 0, 1, halo_rows - 1, i > 0

        def neighbour(lo):
            halo = halo_ref[:, lo:lo + ch].astype(F32)[h_row:h_row + 1]
            halo = jnp.where(live, halo, 0.0)
            x = x_ref[:, lo:lo + ch].astype(F32)
            return jnp.where(row == edge, jnp.broadcast_to(halo, (rows, ch)), pltpu.roll(x, shift, 0))

        xp_r, xp_i = neighbour(0), neighbour(ch)
        l_r, l_i = lam_ref[:, 0:ch].astype(F32), lam_ref[:, ch:2 * ch].astype(F32)
        dr_ref[...] += jnp.sum(l_r * xp_r + l_i * xp_i, axis=0, keepdims=True)
        di_ref[...] += jnp.sum(l_i * xp_r - l_r * xp_i, axis=0, keepdims=True)

    blk = BS((rows, 2 * ch), lambda i: (i, 0))
    vec = BS((1, ch), lambda i: (0, 0))
    return pl.pallas_call(
        body, out_shape=(SDS((1, ch), F32), SDS((1, ch), F32)), grid=(n,),
        in_specs=[blk, blk, BS((halo_rows, 2 * ch), halo_at)] + after_specs, out_specs=(vec, vec),
        name=name, compiler_params=_params("arbitrary"))(lam, states, states, *after_ops)


GELU_C = math.sqrt(2.0 / math.pi)
GELU_K = 0.044715


def _ssm_combine(proj, y_fwd, y_bwd, d_skip, tm, after=()):
    t = proj.shape[0]
    after_ops, after_specs = _after_operands(after)

    def body(s_ref, yf_ref, yb_ref, d_ref, *rest):
        yt_ref, g_ref = rest[len(after_ops):]
        y = s_ref[...] * d_ref[...] + yf_ref[...] + yb_ref[...]
        yt_ref[...] = y
        th = jnp.tanh(GELU_C * (y + GELU_K * y * y * y))
        g_ref[...] = (0.5 * y * (1.0 + th)).astype(BF16)

    blk = BS((tm, D_SSM), lambda i: (i, 0))
    return pl.pallas_call(
        body, out_shape=(SDS((t, D_SSM), F32), SDS((t, D_SSM), BF16)), grid=(t // tm,),
        in_specs=[BS((tm, D_SSM), lambda i: (i, D_POOL // D_SSM)), blk, blk, BS((1, D_SSM), lambda i: (0, 0))] + after_specs,
        out_specs=(blk, blk), name="ssm_combine",
        compiler_params=_params("parallel"))(proj, y_fwd, y_bwd, d_skip, *after_ops)


def _ssm_ds(proj, d_yt, du_fwd, du_bwd, d_skip, tm):
    t = proj.shape[0]

    def body(s_ref, dy_ref, duf_ref, dub_ref, d_ref, ds_ref, dd_ref):
        i = pl.program_id(0)
        dy = dy_ref[...]
        ds_ref[...] = (dy * d_ref[...] + duf_ref[...] + dub_ref[...]).astype(BF16)

        @pl.when(i == 0)
        def _():
            dd_ref[...] = jnp.zeros_like(dd_ref)

        dd_ref[...] += jnp.sum(dy * s_ref[...], axis=0, keepdims=True)

    blk = BS((tm, D_SSM), lambda i: (i, 0))
    vec = BS((1, D_SSM), lambda i: (0, 0))
    return pl.pallas_call(
        body, out_shape=(SDS((t, D_SSM), BF16), SDS((1, D_SSM), F32)), grid=(t // tm,),
        in_specs=[BS((tm, D_SSM), lambda i: (i, D_POOL // D_SSM)), blk, blk, blk, vec],
        out_specs=(blk, vec), name="ssm_ds", compiler_params=_params("arbitrary"))(proj, d_yt, du_fwd, du_bwd, d_skip)


GP_BLOCK = (D_POOL + D_SSM) // 256
GS_BLOCK = GP_BLOCK + D_MODEL // 256


def _merge_specs(tm):
    return [BS((tm, D_POOL), lambda s, i: (i, 0)), BS((tm, D_SSM), lambda s, i: (i, 0)),
            BS((None, D_POOL, 256), lambda s, i: (s, 0, 0)), BS((None, D_SSM, 256), lambda s, i: (s, 2, 0)),
            BS((None, D_SSM, 256), lambda s, i: (s, 3, 0)),
            BS((tm, 256), lambda s, i: (i, GP_BLOCK + s)), BS((tm, 256), lambda s, i: (i, GS_BLOCK + s))]


def _mixer_merge(ms, yssm, w_e, proj, tm):
    t = ms.shape[0]

    def body(ms_ref, y_ref, wpp_ref, wgv_ref, wgg_ref, gp_ref, gs_ref, o_ref):
        zp = _dot(ms_ref[...], wpp_ref[...])
        yv = y_ref[...]
        zv = _dot(yv, wgv_ref[...])
        zg = _dot(yv, wgg_ref[...])
        o_ref[...] = (_sigmoid(gp_ref[...]) * zp + _sigmoid(gs_ref[...]) * zv * _sigmoid(zg)).astype(BF16)

    col = BS((tm, 256), lambda s, i: (i, s))
    return pl.pallas_call(
        body, out_shape=SDS((t, D_MODEL), BF16), grid=(N_SHARD, t // tm), in_specs=_merge_specs(tm), out_specs=col,
        name="mixer_merge", compiler_params=_params("parallel", "parallel"))(ms, yssm, w_e, w_e, w_e, proj, proj)


def _mixer_merge_bwd(ms, yssm, w_e, proj, dmerged, tm):
    t = ms.shape[0]

    def body(ms_ref, y_ref, wpp_ref, wgv_ref, wgg_ref, gp_ref, gs_ref, dm_ref,
             dgp_ref, dgs_ref, dzp_ref, dzv_ref, dzg_ref):
        zp = _dot(ms_ref[...], wpp_ref[...])
        yv = y_ref[...]
        zv = _dot(yv, wgv_ref[...])
        zg = _dot(yv, wgg_ref[...])
        dm = dm_ref[...].astype(F32)
        sp, ss, sg = _sigmoid(gp_ref[...]), _sigmoid(gs_ref[...]), _sigmoid(zg)
        dgp_ref[...] = (dm * zp * sp * (1.0 - sp)).astype(BF16)
        dgs_ref[...] = (dm * zv * sg * ss * (1.0 - ss)).astype(BF16)
        dzp_ref[...] = (dm * sp).astype(BF16)
        dz = dm * ss
        dzv_ref[...] = (dz * sg).astype(BF16)
        dzg_ref[...] = (dz * zv * sg * (1.0 - sg)).astype(BF16)

    col = BS((tm, 256), lambda s, i: (i, s))
    shape = SDS((t, D_MODEL), BF16)
    return pl.pallas_call(
        body, out_shape=(shape,) * 5, grid=(N_SHARD, t // tm), in_specs=_merge_specs(tm) + [col],
        out_specs=(col,) * 5, name="mixer_merge_bwd",
        compiler_params=_params("parallel", "parallel"))(ms, yssm, w_e, w_e, w_e, proj, proj, dmerged)


def _mixer_dw(ms, yssm, dzp, dzv, dzg, tm):
    t = ms.shape[0]
    n_t = t // tm

    def body(ms_ref, y_ref, dzp_ref, dzv_ref, dzg_ref, o_ref, acc):
        i = pl.program_id(1)

        @pl.when(i == 0)
        def _():
            acc[...] = jnp.zeros_like(acc)

        yv = y_ref[...]
        acc[0:D_POOL, :] += _dot(ms_ref[...], dzp_ref[...], TN)
        acc[D_POOL:D_POOL + D_SSM, :] += _dot(yv, dzv_ref[...], TN)
        acc[D_POOL + D_SSM:, :] += _dot(yv, dzg_ref[...], TN)

        @pl.when(i == n_t - 1)
        def _():
            o_ref[...] = acc[...].astype(BF16)

    col = BS((tm, 256), lambda s, i: (i, s))
    return pl.pallas_call(
        body, out_shape=SDS((N_SHARD, 1024, 256), BF16), grid=(N_SHARD, n_t),
        in_specs=[BS((tm, D_POOL), lambda s, i: (i, 0)), BS((tm, D_SSM), lambda s, i: (i, 0)), col, col, col],
        out_specs=BS((None, 1024, 256), lambda s, i: (s, 0, 0)), scratch_shapes=[pltpu.VMEM((1024, 256), F32)],
        name="mixer_dw", compiler_params=_params("parallel", "arbitrary"))(ms, yssm, dzp, dzv, dzg)


def _mixer_dx(dzp, dzv, dzg, w_e, y_total, tm):
    t = dzp.shape[0]

    def body(dzp_ref, dzv_ref, dzg_ref, wpp_ref, wgv_ref, wgg_ref, yt_ref, dms_ref, dy_ref, acc_ms, acc_y):
        s = pl.program_id(1)

        @pl.when(s == 0)
        def _():
            acc_ms[...] = jnp.zeros_like(acc_ms)
            acc_y[...] = jnp.zeros_like(acc_y)

        acc_ms[...] += _dot(dzp_ref[...], wpp_ref[...], NT)
        acc_y[...] += _dot(dzv_ref[...], wgv_ref[...], NT) + _dot(dzg_ref[...], wgg_ref[...], NT)

        @pl.when(s == N_SHARD - 1)
        def _():
            dms_ref[...] = acc_ms[...].astype(BF16)
            y = yt_ref[...]
            inner = GELU_C * (y + GELU_K * y * y * y)
            th = jnp.tanh(inner)
            dgelu = 0.5 * (1.0 + th) + 0.5 * y * (1.0 - th * th) * GELU_C * (1.0 + 3.0 * GELU_K * y * y)
            dy_ref[...] = acc_y[...] * dgelu

    col = BS((tm, 256), lambda i, s: (i, s))
    return pl.pallas_call(
        body, out_shape=(SDS((t, D_POOL), BF16), SDS((t, D_SSM), F32)), grid=(t // tm, N_SHARD),
        in_specs=[col, col, col, BS((None, D_POOL, 256), lambda i, s: (s, 0, 0)),
                  BS((None, D_SSM, 256), lambda i, s: (s, 2, 0)), BS((None, D_SSM, 256), lambda i, s: (s, 3, 0)),
                  BS((tm, D_SSM), lambda i, s: (i, 0))],
        out_specs=(BS((tm, D_POOL), lambda i, s: (i, 0)), BS((tm, D_SSM), lambda i, s: (i, 0))),
        scratch_shapes=[pltpu.VMEM((tm, D_POOL), F32), pltpu.VMEM((tm, D_SSM), F32)],
        name="mixer_dx", compiler_params=_params("parallel", "arbitrary"))(dzp, dzv, dzg, w_e, w_e, w_e, y_total)


def _attn_probs(q_h, k_h):
    s = _dot(q_h, k_h, NT) * (1.0 / math.sqrt(HEAD_DIM))
    e = jnp.exp(s - jnp.max(s, axis=-1, keepdims=True))
    return e / jnp.sum(e, axis=-1, keepdims=True)


def _attn_fwd(q, kv, tm):
    t = q.shape[0]
    m = kv.shape[0]

    def body(q_ref, kv_ref, o_ref):
        for hd in range(N_HEADS):
            lo = hd * HEAD_DIM
            p = _attn_probs(q_ref[:, lo:lo + HEAD_DIM], kv_ref[:, lo:lo + HEAD_DIM])
            o_ref[:, lo:lo + HEAD_DIM] = _dot(p, kv_ref[:, D_MODEL + lo:D_MODEL + lo + HEAD_DIM]).astype(BF16)

    return pl.pallas_call(
        body, out_shape=SDS((t, D_MODEL), BF16), grid=(t // tm,),
        in_specs=[BS((tm, D_MODEL), lambda i: (i, 0)), BS((m, 2 * D_MODEL), lambda i: (0, 0))],
        out_specs=BS((tm, D_MODEL), lambda i: (i, 0)), name="attn_fwd", compiler_params=_params("parallel"))(q, kv)


def _attn_bwd(q, kv, d_o, tm):
    t = q.shape[0]
    m = kv.shape[0]

    def body(q_ref, kv_ref, do_ref, dq_ref, dkv_ref):
        i = pl.program_id(0)

        @pl.when(i == 0)
        def _():
            dkv_ref[...] = jnp.zeros_like(dkv_ref)

        for hd in range(N_HEADS):
            lo = hd * HEAD_DIM
            q_h = q_ref[:, lo:lo + HEAD_DIM]
            k_h = kv_ref[:, lo:lo + HEAD_DIM]
            v_h = kv_ref[:, D_MODEL + lo:D_MODEL + lo + HEAD_DIM]
            do_h = do_ref[:, lo:lo + HEAD_DIM]
            p = _attn_probs(q_h, k_h)
            dkv_ref[:, D_MODEL + lo:D_MODEL + lo + HEAD_DIM] += _dot(p, do_h, TN)
            dp = _dot(do_h, v_h, NT)
            ds = p * (dp - jnp.sum(dp * p, axis=-1, keepdims=True)) * (1.0 / math.sqrt(HEAD_DIM))
            dq_ref[:, lo:lo + HEAD_DIM] = _dot(ds, k_h).astype(BF16)
            dkv_ref[:, lo:lo + HEAD_DIM] += _dot(ds, q_h, TN)

    row = BS((tm, D_MODEL), lambda i: (i, 0))
    full = BS((m, 2 * D_MODEL), lambda i: (0, 0))
    return pl.pallas_call(
        body, out_shape=(SDS((t, D_MODEL), BF16), SDS((m, 2 * D_MODEL), F32)), grid=(t // tm,),
        in_specs=[row, full, row], out_specs=(row, full), name="attn_bwd",
        compiler_params=_params("arbitrary"))(q, kv, d_o)


TRANSPOSED = ("ffn1_w_gate", "ffn1_w_up", "ffn2_w_gate", "ffn2_w_up", "w_in")
GATHER_PHASES = {"f1a": (("ffn1_w_gate", "ffn1_w_up"),),
                 "f1b": (("ffn1_w_down",),),
                 "win": (("w_in",),),
                 "mix": (("w_mix_out", "w_q", "w_xo"), ("w_kv",), ("w_pool_proj", "w_glu_val", "w_glu_gate")),
                 "f2": (("ffn2_w_gate", "ffn2_w_up", "ffn2_w_down"),)}
REDUCE_GROUPS = (("ffn2_w_gate", "ffn2_w_up", "ffn2_w_down"), ("w_xo",), ("w_q",), ("w_kv",), ("w_mix_out",),
                 ("w_pool_proj", "w_glu_val", "w_glu_gate"), ("w_in",), ("ffn1_w_gate", "ffn1_w_up", "ffn1_w_down"))
SMALL = ("ffn1_norm", "mix_norm", "pool_w", "pool_scale", "ssm_a_re", "ssm_a_im", "ssm_log_dt", "ssm_b_re",
         "ssm_b_im", "ssm_c_re", "ssm_c_im", "ssm_d", "xattn_norm", "mem_norm", "ffn2_norm", "final_norm")
WEIGHTS = ("ffn1_norm", "ffn1_w_gate", "ffn1_w_up", "ffn1_w_down", "mix_norm", "w_in", "pool_w", "pool_scale",
           "w_pool_proj", "ssm_a_re", "ssm_a_im", "ssm_log_dt", "ssm_b_re", "ssm_b_im", "ssm_c_re", "ssm_c_im",
           "ssm_d", "w_glu_val", "w_glu_gate", "w_mix_out", "xattn_norm", "mem_norm", "w_q", "w_kv", "w_xo",
           "ffn2_norm", "ffn2_w_gate", "ffn2_w_up", "ffn2_w_down", "final_norm")


def _small_view(a, n):
    return jnp.swapaxes(a, 3, 4) if n in ("ssm_b_re", "ssm_b_im") else a


def _device_step(x, mem, target, wts, sp, reducer=None):
    t = x.shape[0]
    tm = min(TM, t)
    g = {}

    first_gather = wts.start("f1a")
    u1 = _rmsnorm("norm_ffn1", x, sp["ffn1_norm"], tm, after=first_gather)

    def per_channel(a):
        a = a.reshape(2 * SSM_GROUPS, 1, -1)
        return jnp.broadcast_to(a, (2 * SSM_GROUPS, SSM_GROUP, a.shape[-1])).reshape(SSM_ROWS, a.shape[-1])

    ssm_a = per_channel(sp["ssm_a_re"]), per_channel(sp["ssm_a_im"]), per_channel(sp["ssm_log_dt"])
    ssm_b = sp["ssm_b_re"].reshape(SSM_ROWS, SSM_STATE), sp["ssm_b_im"].reshape(SSM_ROWS, SSM_STATE)
    abr, abi, w_in_s, w_in_s_t, w_out_s_t, w_out_s = _ssm_prep(
        *ssm_a, *ssm_b, sp["ssm_c_re"].reshape(SSM_ROWS, SSM_STATE), sp["ssm_c_im"].reshape(SSM_ROWS, SSM_STATE),
        after=first_gather)
    first_rows = (2, SSM_GROUPS, SSM_GROUP, SSM_STATE)
    a_r = abr.reshape(first_rows)[:, :, 0].reshape(2, 1, SSM_CH)
    a_i = abi.reshape(first_rows)[:, :, 0].reshape(2, 1, SSM_CH)
    mem_n = _rmsnorm("norm_mem", mem, sp["mem_norm"], mem.shape[0], after=first_gather)

    (w_gu,) = wts.finish("f1a", [u1, w_in_s, w_in_s_t, w_out_s, w_out_s_t, a_r, a_i, mem_n])
    w_f1 = {"gate": (w_gu, FFN_GATE), "up": (w_gu, FFN_UP)}
    down_gather = wts.start("f1b", [w_gu])
    g1, up1, a1 = _ffn_up("ffn1_up", u1, w_f1, tm, after=wts.start("win", down_gather))
    (w_dn,) = wts.finish("f1b", [a1])
    w_f1["down"] = (w_dn, 0)
    h1, u2 = _ffn_down("ffn1_down", a1, w_f1, x, tm, next_gain=sp["mix_norm"])

    (w_in_g,) = wts.finish("win", [u2])
    w_in_t = w_in_g.reshape(D_FF, D_MODEL)
    proj = _mm("mix_in", [(u2, BS((tm, D_MODEL), lambda j, i: (i, 0)), w_in_t, BS((D_FF // 2, D_MODEL), lambda j, i: (j, 0)), NT)],
               grid=(2, t // tm), out_shape=SDS((t, D_FF), F32), out_spec=BS((tm, D_FF // 2), lambda j, i: (i, j)),
               after=wts.start("f2", wts.start("mix", [w_in_g])))
    pooled, mixed, ms = _pool_fwd(proj, sp["pool_w"][0], sp["pool_scale"])

    s_in = proj[:, D_POOL:D_POOL + D_SSM].astype(BF16)
    states, y_dirs = [], []
    for dr in range(2):
        st, yd = _ssm_scan(f"ssm_scan_fwd{dr}", s_in, w_in_s[dr], a_r[dr], a_i[dr], w_out_s[dr], reverse=(dr == 1))
        states.append(st)
        y_dirs.append(yd)
    w_sq, w_kv, w_e = wts.finish("mix", y_dirs)
    w_mo, w_q, w_xo = (w_sq[:, 256 * k:256 * (k + 1)].reshape(D_MODEL, D_MODEL) for k in range(3))
    w_d = w_kv[:, None]
    y_total, yssm = _ssm_combine(proj, y_dirs[0], y_dirs[1], sp["ssm_d"], tm)

    merged = _mixer_merge(ms, yssm, w_e, proj, tm)
    h2, u3 = _mm_resid_norm("mix_out", merged, w_mo, h1, sp["xattn_norm"], tm)

    q = _plain_mm("attn_q", u3, w_q, NN, BF16, tm)
    n_mem = mem.shape[0]
    kv = _mm("attn_kv", [(mem_n, BS((n_mem, D_MODEL), lambda s: (0, 0)), w_d, BS((None, None, D_MODEL, 512), lambda s: (s, 0, 0, 0)), NN)],
             grid=(N_SHARD,), out_shape=SDS((n_mem, 2 * D_MODEL), BF16), out_spec=BS((n_mem, 512), lambda s: (0, s)))
    o = _attn_fwd(q, kv, tm)
    h3, u4 = _mm_resid_norm("attn_out", o, w_xo, h2, sp["ffn2_norm"], tm)

    (w_2,) = wts.finish("f2", [u4])
    w_f2 = {"gate": (w_2, FFN_GATE), "up": (w_2, FFN_UP), "down": (w_2, FFN_DOWN)}
    g2, up2, a2 = _ffn_up("ffn2_up", u4, w_f2, tm)
    loss, dh4, dh4_b, g["final_norm"] = _ffn_down("ffn2_down", a2, w_f2, h3, tm,
                                                  head=(sp["final_norm"].reshape(1, D_MODEL), target))

    dg2, dup2 = _ffn_bwd_act("ffn2_bwd_act", dh4_b, w_f2, g2, up2, tm)
    dw_f2 = _ffn_dw("ffn2_dw", u4, dg2, dup2, a2, dh4_b, tm)
    dh3, dh3_b, g["ffn2_norm"] = _ffn_dx("ffn2_dx", dg2, dup2, w_f2, h3, sp["ffn2_norm"], dh4, tm)

    d_o = _plain_mm("attn_out_dx", dh3_b, w_xo, NT, BF16, tm)
    dw_xo = _dw_mm("attn_out_dw", o, dh3_b, tm)
    dq, dkv = _attn_bwd(q, kv, d_o, tm)
    dw_q = _dw_mm("attn_q_dw", u3, dq, tm)
    dh2, dh2_b, g["xattn_norm"] = _mm_norm_bwd("attn_q_dx", dq, w_q, NT, h2, sp["xattn_norm"], dh3, tm)
    dw_kv = _mm("attn_kv_dw", [(mem_n, BS((n_mem, D_MODEL), lambda s: (0, 0)), dkv, BS((n_mem, 512), lambda s: (0, s)), TN)],
                grid=(N_SHARD,), out_shape=SDS((N_SHARD, D_MODEL, 512), BF16), out_spec=BS((None, D_MODEL, 512), lambda s: (s, 0, 0)))
    dmem_n = _mm("attn_kv_dx", [(dkv, BS((n_mem, 512), lambda s: (0, s)), w_d, BS((None, None, D_MODEL, 512), lambda s: (s, 0, 0, 0)), NT)],
                 grid=(N_SHARD,), red_axis=0, out_shape=SDS((n_mem, D_MODEL), F32), out_spec=BS((n_mem, D_MODEL), lambda s: (0, 0)))
    _, _, g["mem_norm"] = _rmsnorm_bwd("norm_mem_bwd", mem, sp["mem_norm"], dmem_n, None, n_mem)

    square = (N_SHARD, D_MODEL // N_SHARD, D_MODEL)
    early = [dw_f2.reshape(N_SHARD, 3 * FF_SH, D_MODEL), dw_xo.reshape(square), dw_q.reshape(square), dw_kv]
    swapping = reducer.swap_start("a1", early) if reducer is not None else []
    dmerged = _plain_mm("mix_out_dx", dh2_b, w_mo, NT, BF16, tm, after=swapping)
    dw_mo = _dw_mm("mix_out_dw", merged, dh2_b, tm)
    d_gp, d_gs, dzp, dzv, dzg = _mixer_merge_bwd(ms, yssm, w_e, proj, dmerged, tm)
    dw_e = _mixer_dw(ms, yssm, dzp, dzv, dzg, tm)
    d_ms, d_yt = _mixer_dx(dzp, dzv, dzg, w_e, y_total, tm)
    dp, d_scale, d_pw = _pool_bwd(d_ms, mixed, pooled, sp["pool_w"][0], sp["pool_scale"])
    g["pool_scale"] = d_scale
    g["pool_w"] = d_pw[None]

    d_yt_b = d_yt.astype(BF16)
    du_dirs, lams = [], []
    for dr in range(2):
        lam, du = _ssm_scan(f"ssm_scan_bwd{dr}", d_yt_b, w_out_s_t[dr], a_r[dr], -a_i[dr], w_in_s_t[dr], reverse=(dr == 0))
        du_dirs.append(du)
        lams.append(lam)
    ds, g["ssm_d"] = _ssm_ds(proj, d_yt, du_dirs[0], du_dirs[1], sp["ssm_d"], tm)

    d_proj = jnp.concatenate([dp, ds, d_gp, d_gs], axis=1)
    dw_in_t = _mm("mix_in_dw", [(d_proj, BS((tm, D_FF // 2), lambda j, i: (i, j)), u2, BS((tm, D_MODEL), lambda j, i: (i, 0)), TN)],
                  grid=(2, t // tm), red_axis=1, out_shape=SDS((D_FF, D_MODEL), BF16), out_spec=BS((D_FF // 2, D_MODEL), lambda j, i: (j, 0)))
    dh1, dh1_b, g["mix_norm"] = _mm_norm_bwd("mix_in_dx", d_proj, w_in_t, NN, h1, sp["mix_norm"], dh2, tm)

    early += [dw_mo.reshape(square), dw_e, dw_in_t.reshape(N_SHARD, FF_SH, D_MODEL)]
    g["final_norm"] = g["final_norm"].reshape(D_MODEL)

    travelling = reducer.start("a", early[4:], swapped=["a1"], after=list(g.values())) if reducer is not None else []
    d_abr, d_abi, d_cm, d_bm = [], [], [], []
    for dr in range(2):
        da_r, da_i = _ssm_da(f"ssm_da{dr}", lams[dr], states[dr], reverse=(dr == 1), after=travelling)
        d_abr.append(da_r)
        d_abi.append(da_i)
        d_cm.append(_dw_mm(f"ssm_dc{dr}", d_yt_b, states[dr], tm, F32, after=travelling))
        d_bm.append(_dw_mm(f"ssm_db{dr}", s_in, lams[dr], tm, F32, after=travelling))

    def first_channel(da):
        da = jnp.stack(da).reshape(2, SSM_GROUPS, 1, SSM_STATE)
        return jnp.pad(da, ((0, 0), (0, 0), (0, SSM_GROUP - 1), (0, 0))).reshape(SSM_ROWS, SSM_STATE)

    d_ar, d_ai, d_ldt, d_br, d_bi, d_cr, d_ci = _ssm_prep_bwd(
        *ssm_a, *ssm_b, first_channel(d_abr), first_channel(d_abi), jnp.stack(d_bm), jnp.stack(d_cm))
    per_group = (2 * SSM_GROUPS, SSM_GROUP * SSM_STATE)
    g["ssm_a_re"] = d_ar.reshape(2 * SSM_GROUPS, SSM_GROUP, SSM_STATE).sum(axis=1).reshape(sp["ssm_a_re"].shape)
    g["ssm_a_im"] = d_ai.reshape(2 * SSM_GROUPS, SSM_GROUP, SSM_STATE).sum(axis=1).reshape(sp["ssm_a_im"].shape)
    g["ssm_log_dt"] = d_ldt.reshape(per_group).sum(axis=1).reshape(sp["ssm_log_dt"].shape)
    g["ssm_b_re"] = d_br.reshape(sp["ssm_b_re"].shape)
    g["ssm_b_im"] = d_bi.reshape(sp["ssm_b_im"].shape)
    g["ssm_c_re"] = d_cr.reshape(sp["ssm_c_re"].shape)
    g["ssm_c_im"] = d_ci.reshape(sp["ssm_c_im"].shape)
    if reducer is not None:
        travelling = travelling + [d_ar, d_br, d_cr]
    dg1, dup1 = _ffn_bwd_act("ffn1_bwd_act", dh1_b, w_f1, g1, up1, tm, after=travelling)
    dw_f1 = _ffn_dw("ffn1_dw", u1, dg1, dup1, a1, dh1_b, tm).reshape(N_SHARD, 3 * FF_SH, D_MODEL)
    if reducer is not None:
        travelling = reducer.start("b", [dw_f1], after=reducer.finish("a", [dw_f1]))
        travelling = travelling + reducer.join_start("a", after=travelling)
    grad_x, _, g["ffn1_norm"] = _ffn_dx("ffn1_dx", dg1, dup1, w_f1, x, sp["ffn1_norm"], dh1, tm, after=travelling)
    if reducer is not None:
        reducer.finish("b", [grad_x])
        reducer.join_finish("a", [grad_x])
    return loss, grad_x, early + [dw_f1], g


def _mesh_place():
    x, y, c = lax.axis_index("x"), lax.axis_index("y"), lax.axis_index("c")
    chips = [(1 - x, y), (x, 1 - y), (1 - x, 1 - y)]
    return x, y, c, chips


def _remote(src, dst, send_sems, recv_sems, k, to):
    return pltpu.make_async_remote_copy(src_ref=src, dst_ref=dst, send_sem=send_sems.at[k], recv_sem=recv_sems.at[k],
                                        device_id=to, device_id_type=MESH)


def _sibling_swap_halves(tag, grads, after=()):
    n = len(grads)
    after_ops, after_specs = _after_operands(after)

    def body(*refs):
        ins, outs = refs[:n], refs[n + len(after_ops):2 * n + len(after_ops)]
        send_sems, recv_sems = refs[2 * n + len(after_ops):]
        x, y, c, _ = _mesh_place()
        sibling = (x, y, 1 - c)
        copies = []
        for k in range(n):
            half = grads[k].shape[1] // 2
            theirs = pl.ds(pl.multiple_of((1 - c) * half, 16), half)
            cp = _remote(ins[k].at[:, theirs, :], outs[k], send_sems, recv_sems, k, sibling)
            cp.start()
            copies.append(cp)
        for cp in copies:
            cp.wait_recv()
        for cp in copies:
            cp.wait_send()

    hbm = BS(memory_space=pl.ANY)
    return pl.pallas_call(
        body, out_shape=tuple(SDS((g.shape[0], g.shape[1] // 2, g.shape[2]), g.dtype) for g in grads),
        in_specs=[hbm] * n + after_specs, out_specs=(hbm,) * n,
        scratch_shapes=[pltpu.SemaphoreType.DMA((n,)), pltpu.SemaphoreType.DMA((n,))],
        name="reduce_sibling_send_" + tag, compiler_params=_params())(*grads, *after_ops)


def _row_tile(rows, cap=512):
    return max(r for r in range(16, cap + 1, 16) if rows % r == 0)


def _chip_presum(k, grad, got, c_idx):
    n_sh, rows, cols = grad.shape
    half = rows // 2
    tr = _row_tile(half)
    grad4 = grad.reshape(n_sh, 2, half, cols)

    def body(c_ref, a_ref, b_ref, o_ref):
        o_ref[...] = (a_ref[...].astype(F32) + b_ref[...].astype(F32)).astype(o_ref.dtype)

    return pl.pallas_call(
        body, out_shape=SDS((n_sh, half, cols), BF16),
        grid_spec=pltpu.PrefetchScalarGridSpec(
            num_scalar_prefetch=1, grid=(n_sh, half // tr),
            in_specs=[BS((None, None, tr, cols), lambda s, i, c_ref: (s, c_ref[0], i, 0)),
                      BS((None, tr, cols), lambda s, i, c_ref: (s, i, 0))],
            out_specs=BS((None, tr, cols), lambda s, i, c_ref: (s, i, 0))),
        name=f"reduce_presum{k}", compiler_params=_params("parallel", "parallel"))(c_idx, grad4, got)


HBM_SPEC = BS(memory_space=pltpu.HBM)
SEM_SPEC = BS(memory_space=pltpu.SEMAPHORE)
DATAFLOW = pltpu.SideEffectType.DATAFLOW_SIDE_EFFECTING


def _chip_exchange_copies(parts, lands, send_sems, recv_sems):
    _, _, c, chips = _mesh_place()
    return [_remote(parts[k].at[2 * px + py], lands[k].at[j], send_sems, recv_sems, 3 * k + j, (px, py, c))
            for k in range(len(parts)) for j, (px, py) in enumerate(chips)]


def _gather_copies(shards, lands, send_sems, recv_sems):
    x, y, c, chips = _mesh_place()
    return [_remote(shards[k], lands[k].at[2 * x + y], send_sems, recv_sems, 3 * k + j, (px, py, c))
            for k in range(len(shards)) for j, (px, py) in enumerate(chips)]


def _gather_half_copies(shards, lands, send_sems, recv_sems):
    x, y, c, chips = _mesh_place()
    out = []
    for k in range(len(shards)):
        half = shards[k].shape[0] // 2
        mine = pl.ds(pl.multiple_of(c * half, 16), half)
        for j, (px, py) in enumerate(chips):
            out.append(_remote(shards[k].at[mine, :], lands[k].at[2 * x + y, mine, :], send_sems, recv_sems,
                               3 * k + j, (px, py, c)))
    return out


def _sibling_fill(tag, lands):
    n = len(lands)

    def body(*refs):
        outs = refs[n:2 * n]
        send_sems, recv_sems = refs[2 * n:]
        x, y, c, chips = _mesh_place()
        copies = []
        for k in range(n):
            half = lands[k].shape[1] // 2
            mine = pl.ds(pl.multiple_of(c * half, 16), half)
            for j, (px, py) in enumerate(chips):
                blk = outs[k].at[2 * px + py, mine, :]
                copies.append(_remote(blk, blk, send_sems, recv_sems, 3 * k + j, (x, y, 1 - c)))
        for cp in copies:
            cp.start()
        for cp in copies:
            cp.wait_recv()
        for cp in copies:
            cp.wait_send()

    hbm = BS(memory_space=pl.ANY)
    return list(pl.pallas_call(
        body, out_shape=tuple(SDS(a.shape, a.dtype) for a in lands),
        in_specs=[hbm] * n, out_specs=(hbm,) * n, input_output_aliases={k: k for k in range(n)},
        scratch_shapes=[pltpu.SemaphoreType.DMA((3 * n,)), pltpu.SemaphoreType.DMA((3 * n,))],
        name="gather_fill_" + tag, compiler_params=_params())(*lands))


def _swap_copies(grads, lands, send_sems, recv_sems):
    x, y, c, _ = _mesh_place()
    out = []
    for k in range(len(grads)):
        half = grads[k].shape[1] // 2
        theirs = pl.ds(pl.multiple_of((1 - c) * half, 16), half)
        out.append(_remote(grads[k].at[:, theirs, :], lands[k], send_sems, recv_sems, k, (x, y, 1 - c)))
    return out


def _join_copies(fulls, same, send_sems, recv_sems):
    x, y, c, _ = _mesh_place()
    out = []
    for k in range(len(fulls)):
        half = fulls[k].shape[0] // 2
        mine = fulls[k].at[pl.ds(pl.multiple_of(c * half, 8), half), :]
        out.append(_remote(mine, mine, send_sems, recv_sems, k, (x, y, 1 - c)))
    return out


def _everyone_copies(packs, lands, send_sems, recv_sems):
    x, y, c, _ = _mesh_place()
    out = []
    for k in range(len(packs)):
        for j in range(N_DEV - 1):
            bx, by, bc = (j + 1) >> 2 & 1, (j + 1) >> 1 & 1, (j + 1) & 1
            peer = (x ^ bx, y ^ by, c ^ bc)
            out.append(_remote(packs[k], lands[k].at[4 * x + 2 * y + c], send_sems, recv_sems, (N_DEV - 1) * k + j, peer))
    return out


def _split_start(name, copies, sources, land_shapes, after=(), fanout=3):
    n = len(sources)
    n_land = len(land_shapes)
    m = n + n_land
    n_sems = fanout * n
    after_ops, after_specs = _after_operands(after)

    def body(*refs):
        ins = refs[:n]
        lands = refs[n:m] if n_land else ins
        send_sems, recv_sems = refs[m + len(after_ops)], refs[m + len(after_ops) + 1]
        token = refs[-1]
        for cp in copies(ins, lands, send_sems, recv_sems):
            cp.start()
        token[...] = jnp.zeros_like(token)

    lands = [pltpu.with_memory_space_constraint(lax.empty(s, d), pltpu.HBM) for s, d in land_shapes]
    sources = [pltpu.with_memory_space_constraint(p, pltpu.HBM) for p in sources]
    thru = [pltpu.HBM(a.shape, a.dtype) for a in sources + lands]
    out = pl.pallas_call(
        body, name=name,
        out_shape=(pltpu.SemaphoreType.DMA((n_sems,)), pltpu.SemaphoreType.DMA((n_sems,)), *thru, SDS((8, 128), F32)),
        in_specs=[HBM_SPEC] * m + after_specs,
        out_specs=(SEM_SPEC, SEM_SPEC, *[HBM_SPEC] * m, BS(memory_space=pltpu.VMEM)),
        input_output_aliases={i: 2 + i for i in range(m)},
        compiler_params=pltpu.CompilerParams(has_side_effects=DATAFLOW))(*sources, *lands, *after_ops)
    return out[0], out[1], list(out[2:2 + n]), list(out[2 + n:2 + m]), out[-1]


def _split_wait(name, copies, send_sems, recv_sems, sources, lands, after):
    n = len(sources)
    m = n + len(lands)
    after_ops, after_specs = _after_operands(after)

    def body(*refs):
        ins = refs[:n]
        zones = refs[n:m] if m > n else ins
        for cp in copies(ins, zones, refs[m], refs[m + 1]):
            cp.wait_send()
            cp.wait_recv()

    out = pl.pallas_call(
        body, name=name,
        out_shape=tuple(pltpu.HBM(a.shape, a.dtype) for a in sources + lands),
        in_specs=[HBM_SPEC] * m + [SEM_SPEC, SEM_SPEC] + after_specs, out_specs=(HBM_SPEC,) * m,
        input_output_aliases={i: i for i in range(m)},
        compiler_params=pltpu.CompilerParams(has_side_effects=DATAFLOW))(*sources, *lands, send_sems, recv_sems, *after_ops)
    return list(out[:n]), list(out[n:])


class _WeightGatherer:
    def __init__(self, shards):
        self.shards, self.open = shards, {}
        self.me = 2 * lax.axis_index("x") + lax.axis_index("y")

    HALVED = ("f1a",)

    def start(self, tag, after=()):
        shapes = [((N_SHARD,) + s.shape, s.dtype) for s in self.shards[tag]]
        copies = _gather_half_copies if tag in self.HALVED else _gather_copies
        self.open[tag] = _split_start("gather_start_" + tag, copies, self.shards[tag], shapes, after)
        return [self.open[tag][-1]]

    def finish(self, tag, after):
        send_sems, recv_sems, shards, lands, _ = self.open.pop(tag)
        copies = _gather_half_copies if tag in self.HALVED else _gather_copies
        shards, lands = _split_wait("gather_wait_" + tag, copies, send_sems, recv_sems, shards, lands, after)
        if tag in self.HALVED:
            lands = _sibling_fill(tag, lands)
        return [lax.dynamic_update_slice(zone, s[None], (self.me, 0, 0)) for zone, s in zip(lands, shards)]


class _GradReducer:
    def __init__(self):
        self.c_idx = lax.axis_index("c").astype(jnp.int32).reshape(1)
        self.place = jnp.stack([2 * lax.axis_index("x") + lax.axis_index("y"), lax.axis_index("c")]).astype(jnp.int32)
        self.swaps, self.open, self.landed, self.joins, self.reduced = {}, {}, {}, {}, []

    def swap_start(self, tag, grads, after=()):
        shapes = [((g.shape[0], g.shape[1] // 2, g.shape[2]), g.dtype) for g in grads]
        self.swaps[tag] = _split_start("reduce_swap_start_" + tag, _swap_copies, grads, shapes, after, fanout=1)
        return [self.swaps[tag][-1]]

    def start(self, tag, grads, after=(), swapped=()):
        pairs = []
        for s in swapped:
            send_sems, recv_sems, early, lands, _ = self.swaps.pop(s)
            pairs += zip(*_split_wait("reduce_swap_wait_" + s, _swap_copies, send_sems, recv_sems, early, lands, grads[-1:]))
        pairs += zip(grads, _sibling_swap_halves(tag, grads, after))
        parts = [_chip_presum(f"{tag}{k}", g, s, self.c_idx) for k, (g, s) in enumerate(pairs)]
        shapes = [((3,) + p.shape[1:], p.dtype) for p in parts]
        self.open[tag] = _split_start("reduce_exchange_start_" + tag, _chip_exchange_copies, parts, shapes)
        return [self.open[tag][-1]]

    def finish(self, tag, after):
        send_sems, recv_sems, parts, lands, _ = self.open.pop(tag)
        self.landed[tag] = _split_wait("reduce_exchange_wait_" + tag, _chip_exchange_copies, send_sems, recv_sems, parts, lands, after)
        return self.landed[tag][1][:1]

    def _sums(self, tag, after=()):
        parts, landed = self.landed.pop(tag)
        return [_chip_sum(f"{tag}{k}", p, got, self.place, after) for k, (p, got) in enumerate(zip(parts, landed))]

    def join_start(self, tag, after=()):
        self.joins[tag] = _split_start("reduce_join_start_" + tag, _join_copies, self._sums(tag, after), [], fanout=1)
        return [self.joins[tag][-1]]

    def join_finish(self, tag, after):
        send_sems, recv_sems, fulls, _, _ = self.joins.pop(tag)
        self.reduced += _split_wait("reduce_join_wait_" + tag, _join_copies, send_sems, recv_sems, fulls, [], after)[0]

    def join(self, tag, after=()):
        self.reduced += _sibling_join_halves(self._sums(tag), after)


def _chip_sum(k, part, got, place, after=()):
    _, half, cols = part.shape
    tr = _row_tile(half)
    n_t = half // tr
    after_ops, after_specs = _after_operands(after)

    def body(place_ref, a_ref, b_ref, *rest):
        o_ref = rest[-1]
        acc = a_ref[...].astype(F32)
        for j in range(3):
            acc = acc + b_ref[j].astype(F32)
        o_ref[...] = acc

    return pl.pallas_call(
        body, out_shape=SDS((2 * half, cols), F32),
        grid_spec=pltpu.PrefetchScalarGridSpec(
            num_scalar_prefetch=1, grid=(n_t,),
            in_specs=[BS((None, tr, cols), lambda i, place_ref: (place_ref[0], i, 0)),
                      BS((3, tr, cols), lambda i, place_ref: (0, i, 0))] + after_specs,
            out_specs=BS((tr, cols), lambda i, place_ref: (place_ref[1] * n_t + i, 0))),
        name=f"reduce_sum{k}", compiler_params=_params("parallel"))(place, part, got, *after_ops)


def _sibling_join_halves(fulls, after=()):
    n = len(fulls)
    after_ops, after_specs = _after_operands(after)

    def body(*refs):
        outs = refs[n + len(after_ops):2 * n + len(after_ops)]
        send_sems, recv_sems = refs[2 * n + len(after_ops):]
        copies = _join_copies(outs, outs, send_sems, recv_sems)
        for cp in copies:
            cp.start()
        for cp in copies:
            cp.wait_recv()
        for cp in copies:
            cp.wait_send()

    hbm = BS(memory_space=pl.ANY)
    return list(pl.pallas_call(
        body, out_shape=tuple(SDS(f.shape, f.dtype) for f in fulls),
        in_specs=[hbm] * n + after_specs, out_specs=(hbm,) * n, input_output_aliases={k: k for k in range(n)},
        scratch_shapes=[pltpu.SemaphoreType.DMA((n,)), pltpu.SemaphoreType.DMA((n,))],
        name="reduce_sibling_join", compiler_params=_params())(*fulls, *after_ops))


N_DEV = 8


def _sum_devices(packs):
    _, rows, lanes = packs.shape

    def body(p_ref, o_ref):
        acc = p_ref[0]
        for dev in range(1, N_DEV):
            acc = acc + p_ref[dev]
        o_ref[...] = acc

    vm = BS(memory_space=pltpu.VMEM)
    return pl.pallas_call(body, out_shape=SDS((rows, lanes), F32), in_specs=[vm], out_specs=vm,
                          name="small_sum", compiler_params=_params())(packs)


def _adamw(name, w, grad, row0, m, v, after=()):
    rows, cols = w.shape
    tr = rows if rows < 16 else _row_tile(rows, 352)
    bc1 = 1.0 - ADAM_B1 ** ADAM_STEP
    bc2 = 1.0 - ADAM_B2 ** ADAM_STEP
    after_ops, after_specs = _after_operands(after)

    def body(w_ref, g_ref, m_ref, v_ref, *rest):
        go_ref, d_ref, mo_ref, vo_ref = rest[len(after_ops):]
        g = g_ref[...]
        m_new = ADAM_B1 * m_ref[...] + (1.0 - ADAM_B1) * g
        v_new = ADAM_B2 * v_ref[...] + (1.0 - ADAM_B2) * (g * g)
        go_ref[...] = g
        mo_ref[...] = m_new
        vo_ref[...] = v_new
        d_ref[...] = -ADAM_LR * ((m_new / bc1) / (jnp.sqrt(v_new / bc2) + ADAM_EPS) + ADAM_WD * w_ref[...])

    blk = BS((tr, cols), lambda i: (i, 0))
    shape = SDS((rows, cols), F32)
    return pl.pallas_call(
        body, out_shape=(shape,) * 4, grid=(rows // tr,),
        in_specs=[blk, BS((tr, cols), lambda i: (row0 // tr + i, 0)), blk, blk] + after_specs, out_specs=(blk,) * 4,
        name=name, compiler_params=_params("parallel"))(w, grad, m, v, *after_ops)


SMALL_LANES = 128


def _pack_small(parts):
    flat = jnp.concatenate([jnp.ravel(p) for p in parts])
    rows = -(-flat.shape[0] // (64 * SMALL_LANES)) * 64
    return jnp.pad(flat, (0, rows * SMALL_LANES - flat.shape[0])).reshape(rows, SMALL_LANES)


def _unpack_small(packed, like):
    flat = jnp.ravel(packed)
    out, at = [], 0
    for p in like:
        out.append(flat[at:at + p.size].reshape(p.shape))
        at += p.size
    return out


def kernel(x, mem, ffn1_norm, ffn1_w_gate, ffn1_w_up, ffn1_w_down, mix_norm, w_in, pool_w, pool_scale, w_pool_proj, ssm_a_re, ssm_a_im, ssm_log_dt, ssm_b_re, ssm_b_im, ssm_c_re, ssm_c_im, ssm_d, w_glu_val, w_glu_gate, w_mix_out, xattn_norm, mem_norm, w_q, w_kv, w_xo, ffn2_norm, ffn2_w_gate, ffn2_w_up, ffn2_w_down, final_norm, loss_target, m_ffn1_norm, m_ffn1_w_gate, m_ffn1_w_up, m_ffn1_w_down, m_mix_norm, m_w_in, m_pool_w, m_pool_scale, m_w_pool_proj, m_ssm_a_re, m_ssm_a_im, m_ssm_log_dt, m_ssm_b_re, m_ssm_b_im, m_ssm_c_re, m_ssm_c_im, m_ssm_d, m_w_glu_val, m_w_glu_gate, m_w_mix_out, m_xattn_norm, m_mem_norm, m_w_q, m_w_kv, m_w_xo, m_ffn2_norm, m_ffn2_w_gate, m_ffn2_w_up, m_ffn2_w_down, m_final_norm, v_ffn1_norm, v_ffn1_w_gate, v_ffn1_w_up, v_ffn1_w_down, v_mix_norm, v_w_in, v_pool_w, v_pool_scale, v_w_pool_proj, v_ssm_a_re, v_ssm_a_im, v_ssm_log_dt, v_ssm_b_re, v_ssm_b_im, v_ssm_c_re, v_ssm_c_im, v_ssm_d, v_w_glu_val, v_w_glu_gate, v_w_mix_out, v_xattn_norm, v_mem_norm, v_w_q, v_w_kv, v_w_xo, v_ffn2_norm, v_ffn2_w_gate, v_ffn2_w_up, v_ffn2_w_down, v_final_norm):
    given = dict(locals())
    w = {n: given[n] for n in WEIGHTS}
    m = {n: given["m_" + n] for n in WEIGHTS}
    v = {n: given["v_" + n] for n in WEIGHTS}

    def shard_view(a, n):
        return a[0].T if n in TRANSPOSED else a[0]

    def shard_unview(a, n):
        return (a.T if n in TRANSPOSED else a)[None]

    shards = {tag: [jnp.concatenate([shard_view(w[n], n).astype(BF16) for n in grp], axis=0) for grp in arrays]
              for tag, arrays in GATHER_PHASES.items()}
    reducer = _GradReducer()
    ws, ms, vs = ({n: _small_view(a[n], n) for n in SMALL} for a in (w, m, v))
    loss_part, grad_x, _, small = _device_step(x[0], mem[0], loss_target[0], _WeightGatherer(shards), ws, reducer)

    small_like = [ws[n] for n in SMALL] + [loss_part[0, :1]]
    pack = _pack_small([small[n] for n in SMALL] + [loss_part[0, :1]])
    everyone = _split_start("small_start", _everyone_copies, [pack], [((N_DEV,) + pack.shape, F32)], fanout=N_DEV - 1)
    reducer.join("b", after=everyone[-1:])

    grads, delta, new_m, new_v = {}, {}, {}, {}
    big_done = []
    for grp, red in zip(REDUCE_GROUPS, reducer.reduced):
        row0 = 0
        for n in grp:
            w_n = shard_view(w[n], n)
            outs = _adamw("adamw_" + n, w_n, red, row0, shard_view(m[n], n), shard_view(v[n], n), after=everyone[-1:])
            grads[n], delta[n], new_m[n], new_v[n] = (shard_unview(o, n) for o in outs)
            big_done.append(outs[1])
            row0 += w_n.shape[0]

    send_sems, recv_sems, packs, landed, _ = everyone
    packs, landed = _split_wait("small_wait", _everyone_copies, send_sems, recv_sems, packs, landed, big_done)
    mine = 4 * lax.axis_index("x") + 2 * lax.axis_index("y") + lax.axis_index("c")
    summed = _sum_devices(lax.dynamic_update_slice(landed[0], packs[0][None], (mine, 0, 0)))
    g_small = dict(zip(SMALL + ("loss",), _unpack_small(summed, small_like)))
    loss = g_small.pop("loss").reshape(())
    narrow = [n for n in SMALL if ws[n].ndim > 3]
    dense = [n for n in SMALL if n not in narrow]
    for n in narrow:
        two_d = (-1, ws[n].shape[-1])
        outs = _adamw("adamw_" + n, ws[n].reshape(two_d), g_small[n].reshape(two_d), 0, ms[n].reshape(two_d), vs[n].reshape(two_d))
        grads[n], delta[n], new_m[n], new_v[n] = (_small_view(o.reshape(ws[n].shape), n) for o in outs)
    dense_like = [ws[n] for n in dense]
    packed = _adamw("adamw_small", _pack_small(dense_like), _pack_small([g_small[n] for n in dense]), 0,
                    _pack_small([ms[n] for n in dense]), _pack_small([vs[n] for n in dense]))
    for out, store in zip(packed, (grads, delta, new_m, new_v)):
        for n, val in zip(dense, _unpack_small(out, dense_like)):
            store[n] = val

    return (loss, grad_x[None], *[grads[n] for n in WEIGHTS], *[delta[n] for n in WEIGHTS],
            *[new_m[n] for n in WEIGHTS], *[new_v[n] for n in WEIGHTS])
```

```python
import functools
import math

import jax
import jax.numpy as jnp
from jax import lax
from jax.experimental import pallas as pl
from jax.experimental.pallas import tpu as pltpu

F32 = jnp.float32
BF16 = jnp.bfloat16
SDS = jax.ShapeDtypeStruct
BS = pl.BlockSpec
MESH = pl.DeviceIdType.MESH

D_MODEL = 1024
D_FF = 2816
N_SHARD = 4
FF_SH = D_FF // N_SHARD
D_POOL = 512
POOL_WINDOWS = (2, 4, 8, 16)
POOL_GROUP = 128
D_SSM = 256
SSM_GROUPS = 16
SSM_GROUP = 16
SSM_STATE = 64
SSM_CH = SSM_GROUPS * SSM_STATE
N_HEADS = 4
HEAD_DIM = 256
EPS = 1e-6
ADAM_LR, ADAM_B1, ADAM_B2, ADAM_EPS, ADAM_WD, ADAM_STEP = 0.001, 0.9, 0.999, 1e-08, 0.01, 10

VMEM_LIMIT_V7X = 52 * 1024 * 1024
TM = 512

NN = (((1,), (0,)), ((), ()))
NT = (((1,), (1,)), ((), ()))
TN = (((0,), (0,)), ((), ()))


def _params(*sem):
    return pltpu.CompilerParams(dimension_semantics=sem if sem else None, vmem_limit_bytes=VMEM_LIMIT_V7X)


def _dot(a, b, dims=NN):
    return lax.dot_general(a.astype(BF16), b.astype(BF16), dims, preferred_element_type=F32)


def _sigmoid(v):
    return pl.reciprocal(1.0 + jnp.exp(-v), approx=True)


def _block_dims(spec):
    return tuple(d for d in spec.block_shape if d is not None)


def _after_operands(after):
    return list(after), [BS(memory_space=pl.ANY)] * len(after)


def _mm(name, pairs, *, grid, out_shape, out_spec, red_axis=None, extras=(), epilogue=None, after=()):
    n_pairs, n_extra = len(pairs), len(extras)
    n_red = grid[red_axis] if red_axis is not None else 1
    dims = [p[4] for p in pairs]

    def body(*refs):
        ab = refs[:2 * n_pairs]
        ex = refs[2 * n_pairs:2 * n_pairs + n_extra]
        o_ref = refs[2 * n_pairs + n_extra + len(after)]

        def partial():
            acc = None
            for p in range(n_pairs):
                t = _dot(ab[2 * p][...], ab[2 * p + 1][...], dims[p])
                acc = t if acc is None else acc + t
            return acc

        def finish(acc):
            res = epilogue(acc, *[e[...] for e in ex]) if epilogue is not None else acc
            o_ref[...] = res.astype(o_ref.dtype)

        if n_red == 1:
            finish(partial())
        else:
            acc_ref = refs[-1]
            k = pl.program_id(red_axis)

            @pl.when(k == 0)
            def _():
                acc_ref[...] = jnp.zeros_like(acc_ref)

            acc_ref[...] += partial()

            @pl.when(k == n_red - 1)
            def _():
                finish(acc_ref[...])

    operands, in_specs = [], []
    for a, a_spec, b, b_spec, _ in pairs:
        operands += [a, b]
        in_specs += [a_spec, b_spec]
    for e, e_spec in extras:
        operands.append(e)
        in_specs.append(e_spec)
    after_ops, after_specs = _after_operands(after)
    operands += after_ops
    in_specs += after_specs
    scratch = [pltpu.VMEM(_block_dims(out_spec), F32)] if n_red > 1 else []
    sem = tuple("arbitrary" if ax == red_axis else "parallel" for ax in range(len(grid)))
    return pl.pallas_call(body, out_shape=out_shape, grid=grid, in_specs=in_specs, out_specs=out_spec,
                          scratch_shapes=scratch, name=name, compiler_params=_params(*sem))(*operands)


def _rmsnorm(name, h, gain, tm, after=()):
    t, d = h.shape
    after_ops, after_specs = _after_operands(after)

    def body(h_ref, g_ref, *rest):
        u_ref = rest[-1]
        hv = h_ref[...]
        r = lax.rsqrt(jnp.mean(hv * hv, axis=-1, keepdims=True) + EPS)
        u_ref[...] = ((hv * r) * g_ref[...]).astype(u_ref.dtype)

    return pl.pallas_call(
        body, out_shape=SDS((t, d), BF16), grid=(t // tm,),
        in_specs=[BS((tm, d), lambda i: (i, 0)), BS((1, d), lambda i: (0, 0))] + after_specs,
        out_specs=BS((tm, d), lambda i: (i, 0)), name=name, compiler_params=_params("parallel"))(h, gain, *after_ops)


def _rmsnorm_bwd(name, h, gain, du, dh_in, tm):
    t, d = h.shape
    has_in = dh_in is not None

    def body(*refs):
        if has_in:
            h_ref, g_ref, du_ref, dhin_ref, dh_ref, dhb_ref, dg_ref = refs
        else:
            h_ref, g_ref, du_ref, dh_ref, dhb_ref, dg_ref = refs
        i = pl.program_id(0)
        hv = h_ref[...]
        r = lax.rsqrt(jnp.mean(hv * hv, axis=-1, keepdims=True) + EPS)
        n = hv * r
        duv = du_ref[...].astype(F32)
        dn = duv * g_ref[...]
        dh = r * (dn - n * jnp.mean(dn * n, axis=-1, keepdims=True))
        if has_in:
            dh = dhin_ref[...] + dh
        dh_ref[...] = dh
        dhb_ref[...] = dh.astype(BF16)

        @pl.when(i == 0)
        def _():
            dg_ref[...] = jnp.zeros_like(dg_ref)

        dg_ref[...] += jnp.sum(duv * n, axis=0, keepdims=True)

    row = BS((tm, d), lambda i: (i, 0))
    vec = BS((1, d), lambda i: (0, 0))
    operands = [h, gain, du] + ([dh_in] if has_in else [])
    in_specs = [row, vec, row] + ([row] if has_in else [])
    return pl.pallas_call(
        body, out_shape=(SDS((t, d), F32), SDS((t, d), BF16), SDS((1, d), F32)), grid=(t // tm,),
        in_specs=in_specs, out_specs=(row, row, vec), name=name, compiler_params=_params("arbitrary"))(*operands)


def _loss_head_tile(i, hv, g_ref, t_ref, loss_ref, dh_ref, dhb_ref, dg_ref):
    g = g_ref[...]
    r = lax.rsqrt(jnp.mean(hv * hv, axis=-1, keepdims=True) + EPS)
    n = hv * r
    err = n * g - t_ref[...]
    dy = err * (1.0 / hv.shape[-1])
    dn = dy * g
    dh = r * (dn - n * jnp.mean(dn * n, axis=-1, keepdims=True))
    dh_ref[...] = dh
    dhb_ref[...] = dh.astype(BF16)

    @pl.when(i == 0)
    def _():
        dg_ref[...] = jnp.zeros_like(dg_ref)
        loss_ref[...] = jnp.zeros_like(loss_ref)

    dg_ref[...] += jnp.sum(dy * n, axis=0, keepdims=True)
    part = 0.5 * jnp.sum(jnp.mean(err * err, axis=-1, keepdims=True), axis=0, keepdims=True)
    loss_ref[...] += jnp.broadcast_to(part, loss_ref.shape)


def _norm_tile(h, g_ref, u_ref):
    r = lax.rsqrt(jnp.mean(h * h, axis=-1, keepdims=True) + EPS)
    u_ref[...] = ((h * r) * g_ref[...]).astype(u_ref.dtype)


FFN_GATE, FFN_UP, FFN_DOWN = 0, 1, 2


def _ffn_up(name, u, w_f, tm, after=()):
    t, d = u.shape
    after_ops, after_specs = _after_operands(after)

    def body(u_ref, wg_ref, wu_ref, *rest):
        pg_ref, pu_ref, a_ref = rest[len(after_ops):]
        uv = u_ref[...]
        for s in range(N_SHARD):
            g = _dot(uv, wg_ref[s], NT)
            up = _dot(uv, wu_ref[s], NT)
            sg = _sigmoid(g)
            silu = g * sg
            a_ref[s] = (silu * up).astype(BF16)
            pu_ref[s] = (0.5 * silu).astype(BF16)
            pg_ref[s] = (0.5 * sg * (1.0 + g * (1.0 - sg)) * up).astype(BF16)

    hid = BS((N_SHARD, tm, FF_SH), lambda i: (0, i, 0))
    shape = SDS((N_SHARD, t, FF_SH), BF16)
    return pl.pallas_call(
        body, out_shape=(shape, shape, shape), grid=(t // tm,),
        in_specs=[BS((tm, d), lambda i: (i, 0)), _ffn_all_shards_spec(w_f["gate"][1]),
                  _ffn_all_shards_spec(w_f["up"][1])] + after_specs,
        out_specs=(hid, hid, hid), name=name,
        compiler_params=_params("parallel"))(u, w_f["gate"][0], w_f["up"][0], *after_ops)


def _ffn_all_shards_spec(block):
    return BS((N_SHARD, FF_SH, D_MODEL), lambda i: (0, block, 0))


def _ffn_down(name, a, w_f, resid, tm, next_gain=None, head=None):
    t, d = resid.shape
    row = BS((tm, d), lambda i: (i, 0))
    vec = BS((1, d), lambda i: (0, 0))

    def body(a_ref, w_ref, res_ref, *rest):
        acc = _dot(a_ref[0], w_ref[0])
        for s in range(1, N_SHARD):
            acc = acc + _dot(a_ref[s], w_ref[s])
        h = res_ref[...] + 0.5 * acc
        if head is not None:
            _loss_head_tile(pl.program_id(0), h, *rest)
        else:
            g_ref, h_ref, u_ref = rest
            h_ref[...] = h
            _norm_tile(h, g_ref, u_ref)

    if head is not None:
        extra, extra_specs = list(head), [vec, row]
        out_shape = (SDS((1, 128), F32), SDS((t, d), F32), SDS((t, d), BF16), SDS((1, d), F32))
        out_specs = (BS((1, 128), lambda i: (0, 0)), row, row, vec)
    else:
        extra, extra_specs = [next_gain], [vec]
        out_shape = (SDS((t, d), F32), SDS((t, d), BF16))
        out_specs = (row, row)
    return pl.pallas_call(
        body, out_shape=out_shape, grid=(t // tm,),
        in_specs=[BS((N_SHARD, tm, FF_SH), lambda i: (0, i, 0)), _ffn_all_shards_spec(w_f["down"][1]), row] + extra_specs,
        out_specs=out_specs, name=name,
        compiler_params=_params("arbitrary" if head is not None else "parallel"))(a, w_f["down"][0], resid, *extra)


def _mm_resid_norm(name, a, b, resid, next_gain, tm):
    t, d = resid.shape

    def body(a_ref, b_ref, res_ref, g_ref, h_ref, u_ref):
        h = res_ref[...] + _dot(a_ref[...], b_ref[...])
        h_ref[...] = h
        _norm_tile(h, g_ref, u_ref)

    row = BS((tm, d), lambda i: (i, 0))
    return pl.pallas_call(
        body, out_shape=(SDS((t, d), F32), SDS((t, d), BF16)), grid=(t // tm,),
        in_specs=[BS((tm, a.shape[1]), lambda i: (i, 0)), BS(b.shape, lambda i: (0, 0)), row, BS((1, d), lambda i: (0, 0))],
        out_specs=(row, row), name=name, compiler_params=_params("parallel"))(a, b, resid, next_gain)


def _ffn_bwd_act(name, dh_b, w_f, pg, pu, tm, after=()):
    t, d = dh_b.shape
    after_ops, after_specs = _after_operands(after)

    def body(dh_ref, wd_ref, pg_ref, pu_ref, *rest):
        dg_ref, dup_ref = rest[len(after_ops):]
        dh = dh_ref[...]
        for s in range(N_SHARD):
            da = _dot(dh, wd_ref[s], NT)
            dg_ref[s] = (da * pg_ref[s].astype(F32)).astype(BF16)
            dup_ref[s] = (da * pu_ref[s].astype(F32)).astype(BF16)

    hid = BS((N_SHARD, tm, FF_SH), lambda i: (0, i, 0))
    shape = SDS((N_SHARD, t, FF_SH), BF16)
    return pl.pallas_call(
        body, out_shape=(shape, shape), grid=(t // tm,),
        in_specs=[BS((tm, d), lambda i: (i, 0)), _ffn_all_shards_spec(w_f["down"][1]), hid, hid] + after_specs,
        out_specs=(hid, hid), name=name,
        compiler_params=_params("parallel"))(dh_b, w_f["down"][0], pg, pu, *after_ops)


def _ffn_dw(name, u, dg, dup, a, dh_b, tm):
    t, d = u.shape
    tm = min(2 * tm, t)
    n_t = t // tm

    def body(u_ref, dg_ref, dup_ref, a_ref, dh_ref, o_ref, acc):
        i = pl.program_id(1)

        @pl.when(i == 0)
        def _():
            acc[...] = jnp.zeros_like(acc)

        uv = u_ref[...]
        acc[FFN_GATE] += _dot(dg_ref[...], uv, TN)
        acc[FFN_UP] += _dot(dup_ref[...], uv, TN)
        acc[FFN_DOWN] += _dot(a_ref[...], dh_ref[...], TN)

        @pl.when(i == n_t - 1)
        def _():
            o_ref[FFN_GATE] = acc[FFN_GATE].astype(BF16)
            o_ref[FFN_UP] = acc[FFN_UP].astype(BF16)
            o_ref[FFN_DOWN] = (0.5 * acc[FFN_DOWN]).astype(BF16)

    hid = BS((None, tm, FF_SH), lambda s, i: (s, i, 0))
    row = BS((tm, d), lambda s, i: (i, 0))
    return pl.pallas_call(
        body, out_shape=SDS((N_SHARD, 3, FF_SH, d), BF16), grid=(N_SHARD, n_t),
        in_specs=[row, hid, hid, hid, row], out_specs=BS((None, 3, FF_SH, d), lambda s, i: (s, 0, 0, 0)),
        scratch_shapes=[pltpu.VMEM((3, FF_SH, d), F32)],
        name=name, compiler_params=_params("parallel", "arbitrary"))(u, dg, dup, a, dh_b)


def _norm_bwd_tile(i, du, h_ref, g_ref, dhin_ref, dh_ref, dhb_ref, dg_ref):
    hv = h_ref[...]
    r = lax.rsqrt(jnp.mean(hv * hv, axis=-1, keepdims=True) + EPS)
    n = hv * r
    dn = du * g_ref[...]
    dh = dhin_ref[...] + r * (dn - n * jnp.mean(dn * n, axis=-1, keepdims=True))
    dh_ref[...] = dh
    dhb_ref[...] = dh.astype(BF16)

    @pl.when(i == 0)
    def _():
        dg_ref[...] = jnp.zeros_like(dg_ref)

    dg_ref[...] += jnp.sum(du * n, axis=0, keepdims=True)


def _norm_bwd_specs(tm):
    row = BS((tm, D_MODEL), lambda i: (i, 0))
    vec = BS((1, D_MODEL), lambda i: (0, 0))
    return [row, vec, row], (row, row, vec)


def _norm_bwd_shapes(t):
    return SDS((t, D_MODEL), F32), SDS((t, D_MODEL), BF16), SDS((1, D_MODEL), F32)


def _ffn_dx(name, dg, dup, w_f, h, gain, dh_in, tm, after=()):
    t = dg.shape[1]
    tm = tm // 2
    after_ops, after_specs = _after_operands(after)

    def body(dg_ref, dup_ref, wg_ref, wu_ref, h_ref, g_ref, dhin_ref, *rest):
        acc = _dot(dg_ref[0], wg_ref[0]) + _dot(dup_ref[0], wu_ref[0])
        for s in range(1, N_SHARD):
            acc = acc + _dot(dg_ref[s], wg_ref[s]) + _dot(dup_ref[s], wu_ref[s])
        _norm_bwd_tile(pl.program_id(0), acc, h_ref, g_ref, dhin_ref, *rest[len(after_ops):])

    hid = BS((N_SHARD, tm, FF_SH), lambda i: (0, i, 0))
    norm_in, norm_out = _norm_bwd_specs(tm)
    return pl.pallas_call(
        body, out_shape=_norm_bwd_shapes(t), grid=(t // tm,),
        in_specs=[hid, hid, _ffn_all_shards_spec(w_f["gate"][1]), _ffn_all_shards_spec(w_f["up"][1])] + norm_in + after_specs,
        out_specs=norm_out, name=name,
        compiler_params=_params("arbitrary"))(dg, dup, w_f["gate"][0], w_f["up"][0], h, gain, dh_in, *after_ops)


def _mm_norm_bwd(name, a, b, dims, h, gain, dh_in, tm):
    t = a.shape[0]

    def body(a_ref, b_ref, h_ref, g_ref, dhin_ref, *outs):
        _norm_bwd_tile(pl.program_id(0), _dot(a_ref[...], b_ref[...], dims), h_ref, g_ref, dhin_ref, *outs)

    norm_in, norm_out = _norm_bwd_specs(tm)
    return pl.pallas_call(
        body, out_shape=_norm_bwd_shapes(t), grid=(t // tm,),
        in_specs=[BS((tm, a.shape[1]), lambda i: (i, 0)), BS(b.shape, lambda i: (0, 0))] + norm_in,
        out_specs=norm_out, name=name, compiler_params=_params("arbitrary"))(a, b, h, gain, dh_in)


def _plain_mm(name, a, b, dims, out_dtype, tm, resid=None, after=()):
    t = a.shape[0]
    n = b.shape[1] if dims == NN else b.shape[0]
    extras = [(resid, BS((tm, n), lambda i: (i, 0)))] if resid is not None else []
    epi = (lambda acc, res: res + acc) if resid is not None else None
    return _mm(name, [(a, BS((tm, a.shape[1]), lambda i: (i, 0)), b, BS(b.shape, lambda i: (0, 0)), dims)],
               grid=(t // tm,), out_shape=SDS((t, n), out_dtype), out_spec=BS((tm, n), lambda i: (i, 0)),
               extras=extras, epilogue=epi, after=after)


def _dw_mm(name, a, b, tm, out_dtype=BF16, after=()):
    t, k = a.shape
    n = b.shape[1]
    tm = min(2 * tm, t)
    return _mm(name, [(a, BS((tm, k), lambda i: (i, 0)), b, BS((tm, n), lambda i: (i, 0)), TN)],
               grid=(t // tm,), red_axis=0, out_shape=SDS((k, n), out_dtype), out_spec=BS((k, n), lambda i: (0, 0)),
               after=after)


POOL_CHUNK = 256
POOL_HALO = 8


def _window_sum(v, width, lead):
    n = v.shape[0]
    s = v
    k = 1
    while k < width:
        s = s + pltpu.roll(s, n - k, 0)
        k *= 2
    return pltpu.roll(s, lead, 0) if lead else s


def _pool_count(base, left, right, t, shape):
    pos = base + lax.broadcasted_iota(jnp.int32, shape, 0)
    lo = jnp.maximum(pos - left, 0)
    hi = jnp.minimum(pos + right + 1, t)
    return (hi - lo).astype(F32)


def _pool_fwd(proj, pool_w, pool_scale):
    t = proj.shape[0]
    c, h = POOL_CHUNK, POOL_HALO
    n_chunks = t // c

    def body(proj_hbm, pw_ref, sc_ref, pooled_ref, mixed_ref, ms_ref, pad_ref, sem):
        cp = pltpu.make_async_copy(proj_hbm.at[:, pl.ds(0, D_POOL)], pad_ref.at[pl.ds(h, t), :], sem)
        cp.start()
        pad_ref[pl.ds(0, h), :] = jnp.zeros((h, D_POOL), F32)
        pad_ref[pl.ds(t + h, h), :] = jnp.zeros((h, D_POOL), F32)
        cp.wait()
        for g, width in enumerate(POOL_WINDOWS):
            left = width // 2
            right = width - 1 - left
            cols = slice(g * POOL_GROUP, (g + 1) * POOL_GROUP)
            wmat = pw_ref[g].astype(BF16)
            scale = sc_ref[:, cols]

            def chunk(ci, carry, left=left, right=right, width=width, cols=cols, wmat=wmat, scale=scale):
                base = pl.multiple_of(ci * c, c)
                v = pad_ref[pl.ds(base, c + 2 * h), cols]
                win = _window_sum(v, width, left)[h:h + c]
                cnt = _pool_count(base, left, right, t, (c, POOL_GROUP))
                pooled = (win / cnt - v[h:h + c]).astype(BF16)
                mixed = _dot(pooled, wmat)
                pooled_ref[pl.ds(base, c), cols] = pooled
                mixed_ref[pl.ds(base, c), cols] = mixed.astype(BF16)
                ms_ref[pl.ds(base, c), cols] = (mixed * scale).astype(BF16)
                return carry

            lax.fori_loop(0, n_chunks, chunk, 0)

    vm = BS(memory_space=pltpu.VMEM)
    shape = SDS((t, D_POOL), BF16)
    return pl.pallas_call(
        body, out_shape=(shape, shape, shape),
        in_specs=[BS(memory_space=pl.ANY), vm, vm], out_specs=(vm, vm, vm),
        scratch_shapes=[pltpu.VMEM((t + 2 * h, D_POOL), F32), pltpu.SemaphoreType.DMA],
        name="pool_fwd", compiler_params=_params())(proj, pool_w, pool_scale)


def _pool_bwd(d_ms, mixed, pooled, pool_w, pool_scale):
    t = d_ms.shape[0]
    c, h = POOL_CHUNK, POOL_HALO
    n_chunks = t // c

    def body(dms_ref, mixed_ref, pooled_ref, pw_ref, sc_ref, dp_ref, dsc_ref, dpw_ref, pad_ref):
        pad_ref[pl.ds(0, h), :] = jnp.zeros((h, D_POOL), F32)
        pad_ref[pl.ds(t + h, h), :] = jnp.zeros((h, D_POOL), F32)
        for g, width in enumerate(POOL_WINDOWS):
            left = width // 2
            right = width - 1 - left
            cols = slice(g * POOL_GROUP, (g + 1) * POOL_GROUP)
            wmat = pw_ref[g].astype(BF16)
            scale = sc_ref[:, cols]

            def first(ci, carry, left=left, right=right, cols=cols, wmat=wmat, scale=scale):
                dsc, dpw = carry
                base = pl.multiple_of(ci * c, c)
                dms = dms_ref[pl.ds(base, c), cols].astype(F32)
                dsc = dsc + jnp.sum(dms * mixed_ref[pl.ds(base, c), cols].astype(F32), axis=0, keepdims=True)
                dmix = (dms * scale).astype(BF16)
                dpw = dpw + _dot(pooled_ref[pl.ds(base, c), cols], dmix, TN)
                dpooled = _dot(dmix, wmat, NT)
                cnt = _pool_count(base, left, right, t, (c, POOL_GROUP))
                pad_ref[pl.ds(base + h, c), cols] = dpooled / cnt
                return dsc, dpw

            dsc, dpw = lax.fori_loop(0, n_chunks, first,
                                     (jnp.zeros((1, POOL_GROUP), F32), jnp.zeros((POOL_GROUP, POOL_GROUP), F32)))
            dsc_ref[:, cols] = dsc
            dpw_ref[g] = dpw

            def second(ci, carry, left=left, right=right, width=width, cols=cols):
                base = pl.multiple_of(ci * c, c)
                v = pad_ref[pl.ds(base, c + 2 * h), cols]
                win = _window_sum(v, width, right)[h:h + c]
                cnt = _pool_count(base, left, right, t, (c, POOL_GROUP))
                dp_ref[pl.ds(base, c), cols] = (win - v[h:h + c] * cnt).astype(BF16)
                return carry

            lax.fori_loop(0, n_chunks, second, 0)

    vm = BS(memory_space=pltpu.VMEM)
    return pl.pallas_call(
        body, out_shape=(SDS((t, D_POOL), BF16), SDS((1, D_POOL), F32), SDS((4, POOL_GROUP, POOL_GROUP), F32)),
        in_specs=[vm] * 5, out_specs=(vm, vm, vm),
        scratch_shapes=[pltpu.VMEM((t + 2 * h, D_POOL), F32)],
        name="pool_bwd", compiler_params=_params())(d_ms, mixed, pooled, pool_w, pool_scale)


SSM_ROWS = 2 * SSM_GROUPS * SSM_GROUP
SSM_HALF = SSM_GROUPS * SSM_GROUP


def _ssm_zoh(a_r, a_i, ldt):
    dt = jnp.exp(ldt)
    mag = jnp.exp(dt * a_r)
    ang = dt * a_i
    cs, sn = jnp.cos(ang), jnp.sin(ang)
    abr, abi = mag * cs, mag * sn
    den = a_r * a_r + a_i * a_i
    nr = abr - 1.0
    qr = (nr * a_r + abi * a_i) / den
    qi = (abi * a_r - nr * a_i) / den
    return dt, mag, cs, sn, abr, abi, den, nr, qr, qi


def _ssm_group_mask():
    row = lax.broadcasted_iota(jnp.int32, (SSM_HALF, SSM_CH), 0)
    col = lax.broadcasted_iota(jnp.int32, (SSM_HALF, SSM_CH), 1)
    return (row // SSM_GROUP) == (col // SSM_STATE)


def _ssm_prep(a_r, a_i, ldt, b_r, b_i, c_r, c_i, after=()):
    after_ops, after_specs = _after_operands(after)

    def body(ar_ref, ai_ref, ldt_ref, br_ref, bi_ref, cr_ref, ci_ref, *rest):
        abr_ref, abi_ref, win_ref, wint_ref, woutt_ref, wout_ref = rest[len(after_ops):]
        *_, abr, abi, _, _, qr, qi = _ssm_zoh(ar_ref[...], ai_ref[...], ldt_ref[...])
        abr_ref[...] = abr
        abi_ref[...] = abi
        b_r, b_i = br_ref[...], bi_ref[...]
        bbr = qr * b_r - qi * b_i
        bbi = qr * b_i + qi * b_r
        mask = _ssm_group_mask()
        state = lax.broadcasted_iota(jnp.int32, (SSM_STATE, SSM_CH), 0)
        col = lax.broadcasted_iota(jnp.int32, (SSM_STATE, SSM_CH), 1)
        every_group = (col % SSM_STATE == state).astype(BF16)

        def spread(x):
            return jnp.where(mask, _dot(x, every_group), 0.0)

        for d in range(2):
            rows = slice(d * SSM_HALF, (d + 1) * SSM_HALF)
            for half, x_in, x_out in ((0, bbr[rows], cr_ref[rows, :]), (1, bbi[rows], -ci_ref[rows, :])):
                cols = slice(half * SSM_CH, (half + 1) * SSM_CH)
                m_in, m_out = spread(x_in), spread(x_out)
                win_ref[d, :, cols] = m_in.astype(BF16)
                wint_ref[d, cols, :] = m_in.T.astype(BF16)
                woutt_ref[d, :, cols] = m_out.astype(BF16)
                wout_ref[d, cols, :] = m_out.T.astype(BF16)

    vm = BS(memory_space=pltpu.VMEM)
    vec = SDS((SSM_ROWS, SSM_STATE), F32)
    wide = SDS((2, SSM_HALF, 2 * SSM_CH), BF16)
    tall = SDS((2, 2 * SSM_CH, SSM_HALF), BF16)
    return pl.pallas_call(body, out_shape=(vec, vec, wide, tall, wide, tall), in_specs=[vm] * 7 + after_specs,
                          out_specs=(vm,) * 6, name="ssm_prep",
                          compiler_params=_params())(a_r, a_i, ldt, b_r, b_i, c_r, c_i, *after_ops)


def _ssm_prep_bwd(a_r, a_i, ldt, b_r, b_i, g_abr, g_abi, d_win, d_woutt):
    def body(ar_ref, ai_ref, ldt_ref, br_ref, bi_ref, gabr_ref, gabi_ref, dwin_ref, dwoutt_ref,
             dar_ref, dai_ref, dldt_ref, dbr_ref, dbi_ref, dcr_ref, dci_ref):
        a_r, a_i = ar_ref[...], ai_ref[...]
        dt, mag, cs, sn, abr, abi, den, nr, qr, qi = _ssm_zoh(a_r, a_i, ldt_ref[...])
        mask = _ssm_group_mask()
        col = lax.broadcasted_iota(jnp.int32, (SSM_CH, SSM_STATE), 0)
        state = lax.broadcasted_iota(jnp.int32, (SSM_CH, SSM_STATE), 1)
        own_state = (col % SSM_STATE == state).astype(BF16)

        def pick(dense):
            m = jnp.where(mask, dense, 0.0)
            hi = m.astype(BF16)
            lo = m - hi.astype(F32)
            return _dot(hi, own_state) + _dot(lo, own_state)

        def picked(ref, half):
            cols = slice(half * SSM_CH, (half + 1) * SSM_CH)
            return jnp.concatenate([pick(ref[d, :, cols]) for d in range(2)], axis=0)

        g_r, g_i = picked(dwin_ref, 0), picked(dwin_ref, 1)
        dcr_ref[...] = picked(dwoutt_ref, 0)
        dci_ref[...] = -picked(dwoutt_ref, 1)
        b_r, b_i = br_ref[...], bi_ref[...]
        dbr_ref[...] = g_r * qr + g_i * qi
        dbi_ref[...] = g_i * qr - g_r * qi
        gqr = g_r * b_r + g_i * b_i
        gqi = g_i * b_r - g_r * b_i
        g_nr_num = gqr / den
        g_ni_num = gqi / den
        g_den = -(gqr * qr + gqi * qi) / den
        g_nr = g_nr_num * a_r - g_ni_num * a_i
        g_abi = g_nr_num * a_i + g_ni_num * a_r
        d_ar = g_nr_num * nr + g_ni_num * abi + 2.0 * a_r * g_den
        d_ai = g_nr_num * abi - g_ni_num * nr + 2.0 * a_i * g_den
        g_abr = gabr_ref[...] + g_nr
        g_abi = gabi_ref[...] + g_abi
        g_mag = g_abr * cs + g_abi * sn
        g_ang = mag * (g_abi * cs - g_abr * sn)
        g_e = g_mag * mag
        d_ar = d_ar + g_e * dt
        d_ai = d_ai + g_ang * dt
        g_dt = g_e * a_r + g_ang * a_i
        dar_ref[...] = d_ar
        dai_ref[...] = d_ai
        dldt_ref[...] = g_dt * dt

    vm = BS(memory_space=pltpu.VMEM)
    vec = SDS((SSM_ROWS, SSM_STATE), F32)
    return pl.pallas_call(body, out_shape=(vec,) * 7, in_specs=[vm] * 9, out_specs=(vm,) * 7, name="ssm_prep_bwd",
                          compiler_params=_params())(a_r, a_i, ldt, b_r, b_i, g_abr, g_abi, d_win, d_woutt)


SCAN_ROWS = 512
SCAN_SUB = 128


def _ssm_scan(name, inp, w1, a_r, a_i, w2, reverse):
    t = inp.shape[0]
    rows = min(SCAN_ROWS, t)
    n = t // rows
    n_sub = rows // SCAN_SUB
    ch = SSM_CH
    at = (lambda i: (n - 1 - i, 0)) if reverse else (lambda i: (i, 0))

    def body(in_ref, w1_ref, ar_ref, ai_ref, w2_ref, sb_ref, out_ref, cr_ref, ci_ref, k_ref, st_ref):
        i = pl.program_id(0)

        @pl.when(i == 0)
        def _():
            ar8 = jnp.broadcast_to(ar_ref[...], (8, ch))
            ai8 = jnp.broadcast_to(ai_ref[...], (8, ch))
            row = lax.broadcasted_iota(jnp.int32, (8, ch), 0)
            rank = (7 - row) if reverse else row
            powers = [(ar8, ai8)]
            for _ in range(7):
                p_r, p_i = powers[-1]
                powers.append((p_r * ar8 - p_i * ai8, p_r * ai8 + p_i * ar8))
            zero = jnp.zeros((8, ch), F32)
            for slot, k in enumerate((1, 2, 4)):
                k_ref[2 * slot] = jnp.where(rank >= k, powers[k - 1][0], zero)
                k_ref[2 * slot + 1] = jnp.where(rank >= k, powers[k - 1][1], zero)
            carry_r, carry_i = zero, zero
            for j in range(8):
                carry_r = jnp.where(rank == j, powers[j][0], carry_r)
                carry_i = jnp.where(rank == j, powers[j][1], carry_i)
            k_ref[6] = carry_r
            k_ref[7] = carry_i
            cr_ref[...] = zero
            ci_ref[...] = zero

        def group(r0, carry):
            c_r, c_i = carry
            x_r = st_ref[pl.ds(r0, 8), 0:ch]
            x_i = st_ref[pl.ds(r0, 8), ch:2 * ch]
            for slot, k in enumerate((1, 2, 4)):
                shift = (8 - k) if reverse else k
                s_r = pltpu.roll(x_r, shift, 0)
                s_i = pltpu.roll(x_i, shift, 0)
                m_r, m_i = k_ref[2 * slot], k_ref[2 * slot + 1]
                x_r, x_i = x_r + m_r * s_r - m_i * s_i, x_i + m_r * s_i + m_i * s_r
            p_r, p_i = k_ref[6], k_ref[7]
            x_r, x_i = x_r + p_r * c_r - p_i * c_i, x_i + p_r * c_i + p_i * c_r
            st_ref[pl.ds(r0, 8), 0:ch] = x_r
            st_ref[pl.ds(r0, 8), ch:2 * ch] = x_i
            last = 0 if reverse else 7
            return (jnp.broadcast_to(x_r[last:last + 1, :], (8, ch)), jnp.broadcast_to(x_i[last:last + 1, :], (8, ch)))

        carry = (cr_ref[...], ci_ref[...])
        for sc in (range(n_sub - 1, -1, -1) if reverse else range(n_sub)):
            part = pl.ds(sc * SCAN_SUB, SCAN_SUB)
            st_ref[part, :] = _dot(in_ref[part, :], w1_ref[...])
            for gi in range(SCAN_SUB // 8):
                g = (SCAN_SUB // 8 - 1 - gi) if reverse else gi
                carry = group(sc * SCAN_SUB + g * 8, carry)
            states = st_ref[part, :].astype(BF16)
            sb_ref[part, :] = states
            out_ref[part, :] = _dot(states, w2_ref[...])
        cr_ref[...] = carry[0]
        ci_ref[...] = carry[1]

    return pl.pallas_call(
        body, out_shape=(SDS((t, 2 * ch), BF16), SDS((t, D_SSM), F32)), grid=(n,),
        in_specs=[BS((rows, D_SSM), at), BS((D_SSM, 2 * ch), lambda i: (0, 0)), BS((1, ch), lambda i: (0, 0)),
                  BS((1, ch), lambda i: (0, 0)), BS((2 * ch, D_SSM), lambda i: (0, 0))],
        out_specs=(BS((rows, 2 * ch), at), BS((rows, D_SSM), at)),
        scratch_shapes=[pltpu.VMEM((8, ch), F32), pltpu.VMEM((8, ch), F32), pltpu.VMEM((8, 8, ch), F32),
                        pltpu.VMEM((rows, 2 * ch), F32)],
        name=name, compiler_params=_params("arbitrary"))(inp, w1, a_r, a_i, w2)


DA_ROWS = 1024


def _ssm_param_grads(name, lam, states, u, dy, reverse, after=()):
    t = lam.shape[0]
    rows = min(DA_ROWS, t)
    n = t // rows
    halo_rows = 16
    nb = rows // halo_rows
    ch = SSM_CH
    if reverse:
        halo_at = lambda i: (jnp.minimum((i + 1) * nb, t // halo_rows - 1), 0)
    else:
        halo_at = lambda i: (jnp.maximum(i * nb - 1, 0), 0)

    after_ops, after_specs = _after_operands(after)

    def body(lam_ref, x_ref, halo_ref, u_ref, dy_ref, *rest):
        dr_ref, di_ref, dwin_ref, dwoutt_ref = rest[len(after_ops):]
        i = pl.program_id(0)

        @pl.when(i == 0)
        def _():
            dr_ref[...] = jnp.zeros_like(dr_ref)
            di_ref[...] = jnp.zeros_like(di_ref)
            dwin_ref[...] = jnp.zeros_like(dwin_ref)
            dwoutt_ref[...] = jnp.zeros_like(dwoutt_ref)

        dwin_ref[...] += _dot(u_ref[...], lam_ref[...], TN)
        dwoutt_ref[...] += _dot(dy_ref[...], x_ref[...], TN)
        row = lax.broadcasted_iota(jnp.int32, (rows, ch), 0)
        if reverse:
            edge, shift, h_row, live = rows - 1, rows - 1, 0, i < n - 1
        else:
            edge, shift, h_row, live = 0, 1, halo_rows - 1, i > 0

        def neighbour(lo):
            halo = halo_ref[:, lo:lo + ch].astype(F32)[h_row:h_row + 1]
            halo = jnp.where(live, halo, 0.0)
            x = x_ref[:, lo:lo + ch].astype(F32)
            return jnp.where(row == edge, jnp.broadcast_to(halo, (rows, ch)), pltpu.roll(x, shift, 0))

        xp_r, xp_i = neighbour(0), neighbour(ch)
        l_r, l_i = lam_ref[:, 0:ch].astype(F32), lam_ref[:, ch:2 * ch].astype(F32)
        dr_ref[...] += jnp.sum(l_r * xp_r + l_i * xp_i, axis=0, keepdims=True)
        di_ref[...] += jnp.sum(l_i * xp_r - l_r * xp_i, axis=0, keepdims=True)

    blk = BS((rows, 2 * ch), lambda i: (i, 0))
    thin = BS((rows, D_SSM), lambda i: (i, 0))
    vec = BS((1, ch), lambda i: (0, 0))
    mat = BS((D_SSM, 2 * ch), lambda i: (0, 0))
    return pl.pallas_call(
        body, out_shape=(SDS((1, ch), F32), SDS((1, ch), F32), SDS((D_SSM, 2 * ch), F32), SDS((D_SSM, 2 * ch), F32)),
        grid=(n,), in_specs=[blk, blk, BS((halo_rows, 2 * ch), halo_at), thin, thin] + after_specs,
        out_specs=(vec, vec, mat, mat),
        name=name, compiler_params=_params("arbitrary"))(lam, states, states, u, dy, *after_ops)


GELU_C = math.sqrt(2.0 / math.pi)
GELU_K = 0.044715


def _ssm_combine(proj, y_fwd, y_bwd, d_skip, tm, after=()):
    t = proj.shape[0]
    after_ops, after_specs = _after_operands(after)

    def body(s_ref, yf_ref, yb_ref, d_ref, *rest):
        yt_ref, g_ref = rest[len(after_ops):]
        y = s_ref[...] * d_ref[...] + yf_ref[...] + yb_ref[...]
        yt_ref[...] = y
        th = jnp.tanh(GELU_C * (y + GELU_K * y * y * y))
        g_ref[...] = (0.5 * y * (1.0 + th)).astype(BF16)

    blk = BS((tm, D_SSM), lambda i: (i, 0))
    return pl.pallas_call(
        body, out_shape=(SDS((t, D_SSM), F32), SDS((t, D_SSM), BF16)), grid=(t // tm,),
        in_specs=[BS((tm, D_SSM), lambda i: (i, D_POOL // D_SSM)), blk, blk, BS((1, D_SSM), lambda i: (0, 0))] + after_specs,
        out_specs=(blk, blk), name="ssm_combine",
        compiler_params=_params("parallel"))(proj, y_fwd, y_bwd, d_skip, *after_ops)


def _ssm_ds(proj, d_yt, du_fwd, du_bwd, d_skip, tm):
    t = proj.shape[0]

    def body(s_ref, dy_ref, duf_ref, dub_ref, d_ref, ds_ref, dd_ref):
        i = pl.program_id(0)
        dy = dy_ref[...]
        ds_ref[...] = (dy * d_ref[...] + duf_ref[...] + dub_ref[...]).astype(BF16)

        @pl.when(i == 0)
        def _():
            dd_ref[...] = jnp.zeros_like(dd_ref)

        dd_ref[...] += jnp.sum(dy * s_ref[...], axis=0, keepdims=True)

    blk = BS((tm, D_SSM), lambda i: (i, 0))
    vec = BS((1, D_SSM), lambda i: (0, 0))
    return pl.pallas_call(
        body, out_shape=(SDS((t, D_SSM), BF16), SDS((1, D_SSM), F32)), grid=(t // tm,),
        in_specs=[BS((tm, D_SSM), lambda i: (i, D_POOL // D_SSM)), blk, blk, blk, vec],
        out_specs=(blk, vec), name="ssm_ds", compiler_params=_params("arbitrary"))(proj, d_yt, du_fwd, du_bwd, d_skip)


GP_BLOCK = (D_POOL + D_SSM) // 256
GS_BLOCK = GP_BLOCK + D_MODEL // 256


def _merge_specs(tm):
    return [BS((tm, D_POOL), lambda s, i: (i, 0)), BS((tm, D_SSM), lambda s, i: (i, 0)),
            BS((None, D_POOL, 256), lambda s, i: (s, 0, 0)), BS((None, D_SSM, 256), lambda s, i: (s, 2, 0)),
            BS((None, D_SSM, 256), lambda s, i: (s, 3, 0)),
            BS((tm, 256), lambda s, i: (i, GP_BLOCK + s)), BS((tm, 256), lambda s, i: (i, GS_BLOCK + s))]


def _mixer_merge(ms, yssm, w_e, proj, tm):
    t = ms.shape[0]

    def body(ms_ref, y_ref, wpp_ref, wgv_ref, wgg_ref, gp_ref, gs_ref, o_ref):
        zp = _dot(ms_ref[...], wpp_ref[...])
        yv = y_ref[...]
        zv = _dot(yv, wgv_ref[...])
        zg = _dot(yv, wgg_ref[...])
        o_ref[...] = (_sigmoid(gp_ref[...]) * zp + _sigmoid(gs_ref[...]) * zv * _sigmoid(zg)).astype(BF16)

    col = BS((tm, 256), lambda s, i: (i, s))
    return pl.pallas_call(
        body, out_shape=SDS((t, D_MODEL), BF16), grid=(N_SHARD, t // tm), in_specs=_merge_specs(tm), out_specs=col,
        name="mixer_merge", compiler_params=_params("parallel", "parallel"))(ms, yssm, w_e, w_e, w_e, proj, proj)


def _mixer_merge_bwd(ms, yssm, w_e, proj, dmerged, tm):
    t = ms.shape[0]

    def body(ms_ref, y_ref, wpp_ref, wgv_ref, wgg_ref, gp_ref, gs_ref, dm_ref,
             dgp_ref, dgs_ref, dzp_ref, dzv_ref, dzg_ref):
        zp = _dot(ms_ref[...], wpp_ref[...])
        yv = y_ref[...]
        zv = _dot(yv, wgv_ref[...])
        zg = _dot(yv, wgg_ref[...])
        dm = dm_ref[...].astype(F32)
        sp, ss, sg = _sigmoid(gp_ref[...]), _sigmoid(gs_ref[...]), _sigmoid(zg)
        dgp_ref[...] = (dm * zp * sp * (1.0 - sp)).astype(BF16)
        dgs_ref[...] = (dm * zv * sg * ss * (1.0 - ss)).astype(BF16)
        dzp_ref[...] = (dm * sp).astype(BF16)
        dz = dm * ss
        dzv_ref[...] = (dz * sg).astype(BF16)
        dzg_ref[...] = (dz * zv * sg * (1.0 - sg)).astype(BF16)

    col = BS((tm, 256), lambda s, i: (i, s))
    shape = SDS((t, D_MODEL), BF16)
    return pl.pallas_call(
        body, out_shape=(shape,) * 5, grid=(N_SHARD, t // tm), in_specs=_merge_specs(tm) + [col],
        out_specs=(col,) * 5, name="mixer_merge_bwd",
        compiler_params=_params("parallel", "parallel"))(ms, yssm, w_e, w_e, w_e, proj, proj, dmerged)


def _mixer_dw(ms, yssm, dzp, dzv, dzg, tm):
    t = ms.shape[0]
    n_t = t // tm

    def body(ms_ref, y_ref, dzp_ref, dzv_ref, dzg_ref, o_ref, acc):
        i = pl.program_id(1)

        @pl.when(i == 0)
        def _():
            acc[...] = jnp.zeros_like(acc)

        yv = y_ref[...]
        acc[0:D_POOL, :] += _dot(ms_ref[...], dzp_ref[...], TN)
        acc[D_POOL:D_POOL + D_SSM, :] += _dot(yv, dzv_ref[...], TN)
        acc[D_POOL + D_SSM:, :] += _dot(yv, dzg_ref[...], TN)

        @pl.when(i == n_t - 1)
        def _():
            o_ref[...] = acc[...].astype(BF16)

    col = BS((tm, 256), lambda s, i: (i, s))
    return pl.pallas_call(
        body, out_shape=SDS((N_SHARD, 1024, 256), BF16), grid=(N_SHARD, n_t),
        in_specs=[BS((tm, D_POOL), lambda s, i: (i, 0)), BS((tm, D_SSM), lambda s, i: (i, 0)), col, col, col],
        out_specs=BS((None, 1024, 256), lambda s, i: (s, 0, 0)), scratch_shapes=[pltpu.VMEM((1024, 256), F32)],
        name="mixer_dw", compiler_params=_params("parallel", "arbitrary"))(ms, yssm, dzp, dzv, dzg)


def _mixer_dx(dzp, dzv, dzg, w_e, y_total, tm):
    t = dzp.shape[0]

    def body(dzp_ref, dzv_ref, dzg_ref, wpp_ref, wgv_ref, wgg_ref, yt_ref, dms_ref, dy_ref, acc_ms, acc_y):
        s = pl.program_id(1)

        @pl.when(s == 0)
        def _():
            acc_ms[...] = jnp.zeros_like(acc_ms)
            acc_y[...] = jnp.zeros_like(acc_y)

        acc_ms[...] += _dot(dzp_ref[...], wpp_ref[...], NT)
        acc_y[...] += _dot(dzv_ref[...], wgv_ref[...], NT) + _dot(dzg_ref[...], wgg_ref[...], NT)

        @pl.when(s == N_SHARD - 1)
        def _():
            dms_ref[...] = acc_ms[...].astype(BF16)
            y = yt_ref[...]
            inner = GELU_C * (y + GELU_K * y * y * y)
            th = jnp.tanh(inner)
            dgelu = 0.5 * (1.0 + th) + 0.5 * y * (1.0 - th * th) * GELU_C * (1.0 + 3.0 * GELU_K * y * y)
            dy_ref[...] = acc_y[...] * dgelu

    col = BS((tm, 256), lambda i, s: (i, s))
    return pl.pallas_call(
        body, out_shape=(SDS((t, D_POOL), BF16), SDS((t, D_SSM), F32)), grid=(t // tm, N_SHARD),
        in_specs=[col, col, col, BS((None, D_POOL, 256), lambda i, s: (s, 0, 0)),
                  BS((None, D_SSM, 256), lambda i, s: (s, 2, 0)), BS((None, D_SSM, 256), lambda i, s: (s, 3, 0)),
                  BS((tm, D_SSM), lambda i, s: (i, 0))],
        out_specs=(BS((tm, D_POOL), lambda i, s: (i, 0)), BS((tm, D_SSM), lambda i, s: (i, 0))),
        scratch_shapes=[pltpu.VMEM((tm, D_POOL), F32), pltpu.VMEM((tm, D_SSM), F32)],
        name="mixer_dx", compiler_params=_params("parallel", "arbitrary"))(dzp, dzv, dzg, w_e, w_e, w_e, y_total)


def _attn_probs(q_h, k_h):
    s = _dot(q_h, k_h, NT) * (1.0 / math.sqrt(HEAD_DIM))
    e = jnp.exp(s - jnp.max(s, axis=-1, keepdims=True))
    return e / jnp.sum(e, axis=-1, keepdims=True)


def _attn_fwd(q, kv, tm):
    t = q.shape[0]
    m = kv.shape[0]

    def body(q_ref, kv_ref, o_ref):
        for hd in range(N_HEADS):
            lo = hd * HEAD_DIM
            p = _attn_probs(q_ref[:, lo:lo + HEAD_DIM], kv_ref[:, lo:lo + HEAD_DIM])
            o_ref[:, lo:lo + HEAD_DIM] = _dot(p, kv_ref[:, D_MODEL + lo:D_MODEL + lo + HEAD_DIM]).astype(BF16)

    return pl.pallas_call(
        body, out_shape=SDS((t, D_MODEL), BF16), grid=(t // tm,),
        in_specs=[BS((tm, D_MODEL), lambda i: (i, 0)), BS((m, 2 * D_MODEL), lambda i: (0, 0))],
        out_specs=BS((tm, D_MODEL), lambda i: (i, 0)), name="attn_fwd", compiler_params=_params("parallel"))(q, kv)


def _attn_bwd(q, kv, d_o, tm):
    t = q.shape[0]
    m = kv.shape[0]

    def body(q_ref, kv_ref, do_ref, dq_ref, dkv_ref):
        i = pl.program_id(0)

        @pl.when(i == 0)
        def _():
            dkv_ref[...] = jnp.zeros_like(dkv_ref)

        for hd in range(N_HEADS):
            lo = hd * HEAD_DIM
            q_h = q_ref[:, lo:lo + HEAD_DIM]
            k_h = kv_ref[:, lo:lo + HEAD_DIM]
            v_h = kv_ref[:, D_MODEL + lo:D_MODEL + lo + HEAD_DIM]
            do_h = do_ref[:, lo:lo + HEAD_DIM]
            p = _attn_probs(q_h, k_h)
            dkv_ref[:, D_MODEL + lo:D_MODEL + lo + HEAD_DIM] += _dot(p, do_h, TN)
            dp = _dot(do_h, v_h, NT)
            ds = p * (dp - jnp.sum(dp * p, axis=-1, keepdims=True)) * (1.0 / math.sqrt(HEAD_DIM))
            dq_ref[:, lo:lo + HEAD_DIM] = _dot(ds, k_h).astype(BF16)
            dkv_ref[:, lo:lo + HEAD_DIM] += _dot(ds, q_h, TN)

    row = BS((tm, D_MODEL), lambda i: (i, 0))
    full = BS((m, 2 * D_MODEL), lambda i: (0, 0))
    return pl.pallas_call(
        body, out_shape=(SDS((t, D_MODEL), BF16), SDS((m, 2 * D_MODEL), F32)), grid=(t // tm,),
        in_specs=[row, full, row], out_specs=(row, full), name="attn_bwd",
        compiler_params=_params("arbitrary"))(q, kv, d_o)


TRANSPOSED = ("ffn1_w_gate", "ffn1_w_up", "ffn2_w_gate", "ffn2_w_up", "w_in")
GATHER_PHASES = {"f1a": (("ffn1_w_gate", "ffn1_w_up"),),
                 "f1b": (("ffn1_w_down",),),
                 "win": (("w_in",),),
                 "mix": (("w_mix_out", "w_q", "w_xo"), ("w_kv",), ("w_pool_proj", "w_glu_val", "w_glu_gate")),
                 "f2": (("ffn2_w_gate", "ffn2_w_up", "ffn2_w_down"),)}
REDUCE_GROUPS = (("ffn2_w_gate", "ffn2_w_up", "ffn2_w_down"), ("w_xo",), ("w_q",), ("w_kv",), ("w_mix_out",),
                 ("w_pool_proj", "w_glu_val", "w_glu_gate"), ("w_in",), ("ffn1_w_gate", "ffn1_w_up", "ffn1_w_down"))
SMALL = ("ffn1_norm", "mix_norm", "pool_w", "pool_scale", "ssm_a_re", "ssm_a_im", "ssm_log_dt", "ssm_b_re",
         "ssm_b_im", "ssm_c_re", "ssm_c_im", "ssm_d", "xattn_norm", "mem_norm", "ffn2_norm", "final_norm")
WEIGHTS = ("ffn1_norm", "ffn1_w_gate", "ffn1_w_up", "ffn1_w_down", "mix_norm", "w_in", "pool_w", "pool_scale",
           "w_pool_proj", "ssm_a_re", "ssm_a_im", "ssm_log_dt", "ssm_b_re", "ssm_b_im", "ssm_c_re", "ssm_c_im",
           "ssm_d", "w_glu_val", "w_glu_gate", "w_mix_out", "xattn_norm", "mem_norm", "w_q", "w_kv", "w_xo",
           "ffn2_norm", "ffn2_w_gate", "ffn2_w_up", "ffn2_w_down", "final_norm")


def _small_view(a, n):
    return jnp.swapaxes(a, 3, 4) if n in ("ssm_b_re", "ssm_b_im") else a


def _device_step(x, mem, target, wts, sp, reducer=None):
    t = x.shape[0]
    tm = min(TM, t)
    g = {}

    first_gather = wts.start("f1a")
    u1 = _rmsnorm("norm_ffn1", x, sp["ffn1_norm"], tm, after=first_gather)

    def per_channel(a):
        a = a.reshape(2 * SSM_GROUPS, 1, -1)
        return jnp.broadcast_to(a, (2 * SSM_GROUPS, SSM_GROUP, a.shape[-1])).reshape(SSM_ROWS, a.shape[-1])

    ssm_a = per_channel(sp["ssm_a_re"]), per_channel(sp["ssm_a_im"]), per_channel(sp["ssm_log_dt"])
    ssm_b = sp["ssm_b_re"].reshape(SSM_ROWS, SSM_STATE), sp["ssm_b_im"].reshape(SSM_ROWS, SSM_STATE)
    abr, abi, w_in_s, w_in_s_t, w_out_s_t, w_out_s = _ssm_prep(
        *ssm_a, *ssm_b, sp["ssm_c_re"].reshape(SSM_ROWS, SSM_STATE), sp["ssm_c_im"].reshape(SSM_ROWS, SSM_STATE),
        after=first_gather)
    first_rows = (2, SSM_GROUPS, SSM_GROUP, SSM_STATE)
    a_r = abr.reshape(first_rows)[:, :, 0].reshape(2, 1, SSM_CH)
    a_i = abi.reshape(first_rows)[:, :, 0].reshape(2, 1, SSM_CH)
    mem_n = _rmsnorm("norm_mem", mem, sp["mem_norm"], mem.shape[0], after=first_gather)

    (w_gu,) = wts.finish("f1a", [u1, w_in_s, w_in_s_t, w_out_s, w_out_s_t, a_r, a_i, mem_n])
    w_f1 = {"gate": (w_gu, FFN_GATE), "up": (w_gu, FFN_UP)}
    down_gather = wts.start("f1b", [w_gu])
    g1, up1, a1 = _ffn_up("ffn1_up", u1, w_f1, tm, after=wts.start("win", down_gather))
    (w_dn,) = wts.finish("f1b", [a1])
    w_f1["down"] = (w_dn, 0)
    h1, u2 = _ffn_down("ffn1_down", a1, w_f1, x, tm, next_gain=sp["mix_norm"])

    (w_in_g,) = wts.finish("win", [u2])
    w_in_t = w_in_g.reshape(D_FF, D_MODEL)
    proj = _mm("mix_in", [(u2, BS((tm, D_MODEL), lambda j, i: (i, 0)), w_in_t, BS((D_FF // 2, D_MODEL), lambda j, i: (j, 0)), NT)],
               grid=(2, t // tm), out_shape=SDS((t, D_FF), F32), out_spec=BS((tm, D_FF // 2), lambda j, i: (i, j)),
               after=wts.start("f2", wts.start("mix", [w_in_g])))
    pooled, mixed, ms = _pool_fwd(proj, sp["pool_w"][0], sp["pool_scale"])

    s_in = proj[:, D_POOL:D_POOL + D_SSM].astype(BF16)
    states, y_dirs = [], []
    for dr in range(2):
        st, yd = _ssm_scan(f"ssm_scan_fwd{dr}", s_in, w_in_s[dr], a_r[dr], a_i[dr], w_out_s[dr], reverse=(dr == 1))
        states.append(st)
        y_dirs.append(yd)
    w_sq, w_kv, w_e = wts.finish("mix", y_dirs)
    w_mo, w_q, w_xo = (w_sq[:, 256 * k:256 * (k + 1)].reshape(D_MODEL, D_MODEL) for k in range(3))
    w_d = w_kv[:, None]
    y_total, yssm = _ssm_combine(proj, y_dirs[0], y_dirs[1], sp["ssm_d"], tm)

    merged = _mixer_merge(ms, yssm, w_e, proj, tm)
    h2, u3 = _mm_resid_norm("mix_out", merged, w_mo, h1, sp["xattn_norm"], tm)

    q = _plain_mm("attn_q", u3, w_q, NN, BF16, tm)
    n_mem = mem.shape[0]
    kv = _mm("attn_kv", [(mem_n, BS((n_mem, D_MODEL), lambda s: (0, 0)), w_d, BS((None, None, D_MODEL, 512), lambda s: (s, 0, 0, 0)), NN)],
             grid=(N_SHARD,), out_shape=SDS((n_mem, 2 * D_MODEL), BF16), out_spec=BS((n_mem, 512), lambda s: (0, s)))
    o = _attn_fwd(q, kv, tm)
    h3, u4 = _mm_resid_norm("attn_out", o, w_xo, h2, sp["ffn2_norm"], tm)

    (w_2,) = wts.finish("f2", [u4])
    w_f2 = {"gate": (w_2, FFN_GATE), "up": (w_2, FFN_UP), "down": (w_2, FFN_DOWN)}
    g2, up2, a2 = _ffn_up("ffn2_up", u4, w_f2, tm)
    loss, dh4, dh4_b, g["final_norm"] = _ffn_down("ffn2_down", a2, w_f2, h3, tm,
                                                  head=(sp["final_norm"].reshape(1, D_MODEL), target))

    dg2, dup2 = _ffn_bwd_act("ffn2_bwd_act", dh4_b, w_f2, g2, up2, tm)
    dw_f2 = _ffn_dw("ffn2_dw", u4, dg2, dup2, a2, dh4_b, tm)
    dh3, dh3_b, g["ffn2_norm"] = _ffn_dx("ffn2_dx", dg2, dup2, w_f2, h3, sp["ffn2_norm"], dh4, tm)

    d_o = _plain_mm("attn_out_dx", dh3_b, w_xo, NT, BF16, tm)
    dw_xo = _dw_mm("attn_out_dw", o, dh3_b, tm)
    dq, dkv = _attn_bwd(q, kv, d_o, tm)
    dw_q = _dw_mm("attn_q_dw", u3, dq, tm)
    dh2, dh2_b, g["xattn_norm"] = _mm_norm_bwd("attn_q_dx", dq, w_q, NT, h2, sp["xattn_norm"], dh3, tm)
    dw_kv = _mm("attn_kv_dw", [(mem_n, BS((n_mem, D_MODEL), lambda s: (0, 0)), dkv, BS((n_mem, 512), lambda s: (0, s)), TN)],
                grid=(N_SHARD,), out_shape=SDS((N_SHARD, D_MODEL, 512), BF16), out_spec=BS((None, D_MODEL, 512), lambda s: (s, 0, 0)))
    dmem_n = _mm("attn_kv_dx", [(dkv, BS((n_mem, 512), lambda s: (0, s)), w_d, BS((None, None, D_MODEL, 512), lambda s: (s, 0, 0, 0)), NT)],
                 grid=(N_SHARD,), red_axis=0, out_shape=SDS((n_mem, D_MODEL), F32), out_spec=BS((n_mem, D_MODEL), lambda s: (0, 0)))
    _, _, g["mem_norm"] = _rmsnorm_bwd("norm_mem_bwd", mem, sp["mem_norm"], dmem_n, None, n_mem)

    square = (N_SHARD, D_MODEL // N_SHARD, D_MODEL)
    early = [dw_f2.reshape(N_SHARD, 3 * FF_SH, D_MODEL), dw_xo.reshape(square), dw_q.reshape(square), dw_kv]
    swapping = reducer.swap_start("a1", early) if reducer is not None else []
    dmerged = _plain_mm("mix_out_dx", dh2_b, w_mo, NT, BF16, tm, after=swapping)
    dw_mo = _dw_mm("mix_out_dw", merged, dh2_b, tm)
    d_gp, d_gs, dzp, dzv, dzg = _mixer_merge_bwd(ms, yssm, w_e, proj, dmerged, tm)
    dw_e = _mixer_dw(ms, yssm, dzp, dzv, dzg, tm)
    d_ms, d_yt = _mixer_dx(dzp, dzv, dzg, w_e, y_total, tm)
    dp, d_scale, d_pw = _pool_bwd(d_ms, mixed, pooled, sp["pool_w"][0], sp["pool_scale"])
    g["pool_scale"] = d_scale
    g["pool_w"] = d_pw[None]

    d_yt_b = d_yt.astype(BF16)
    du_dirs, lams = [], []
    for dr in range(2):
        lam, du = _ssm_scan(f"ssm_scan_bwd{dr}", d_yt_b, w_out_s_t[dr], a_r[dr], -a_i[dr], w_in_s_t[dr], reverse=(dr == 0))
        du_dirs.append(du)
        lams.append(lam)
    ds, g["ssm_d"] = _ssm_ds(proj, d_yt, du_dirs[0], du_dirs[1], sp["ssm_d"], tm)

    d_proj = jnp.concatenate([dp, ds, d_gp, d_gs], axis=1)
    tw = min(2 * tm, t)
    dw_in_t = _mm("mix_in_dw", [(d_proj, BS((tw, D_FF // 2), lambda j, i: (i, j)), u2, BS((tw, D_MODEL), lambda j, i: (i, 0)), TN)],
                  grid=(2, t // tw), red_axis=1, out_shape=SDS((D_FF, D_MODEL), BF16), out_spec=BS((D_FF // 2, D_MODEL), lambda j, i: (j, 0)))
    dh1, dh1_b, g["mix_norm"] = _mm_norm_bwd("mix_in_dx", d_proj, w_in_t, NN, h1, sp["mix_norm"], dh2, tm)

    early += [dw_mo.reshape(square), dw_e, dw_in_t.reshape(N_SHARD, FF_SH, D_MODEL)]
    g["final_norm"] = g["final_norm"].reshape(D_MODEL)

    travelling = reducer.start("a", early[4:], swapped=["a1"], after=list(g.values())) if reducer is not None else []
    d_abr, d_abi, d_cm, d_bm = [], [], [], []
    for dr in range(2):
        da_r, da_i, d_win, d_woutt = _ssm_param_grads(f"ssm_param_grads{dr}", lams[dr], states[dr], s_in, d_yt_b,
                                                      reverse=(dr == 1), after=travelling)
        d_abr.append(da_r)
        d_abi.append(da_i)
        d_bm.append(d_win)
        d_cm.append(d_woutt)

    def first_channel(da):
        da = jnp.stack(da).reshape(2, SSM_GROUPS, 1, SSM_STATE)
        return jnp.pad(da, ((0, 0), (0, 0), (0, SSM_GROUP - 1), (0, 0))).reshape(SSM_ROWS, SSM_STATE)

    d_ar, d_ai, d_ldt, d_br, d_bi, d_cr, d_ci = _ssm_prep_bwd(
        *ssm_a, *ssm_b, first_channel(d_abr), first_channel(d_abi), jnp.stack(d_bm), jnp.stack(d_cm))
    per_group = (2 * SSM_GROUPS, SSM_GROUP * SSM_STATE)
    g["ssm_a_re"] = d_ar.reshape(2 * SSM_GROUPS, SSM_GROUP, SSM_STATE).sum(axis=1).reshape(sp["ssm_a_re"].shape)
    g["ssm_a_im"] = d_ai.reshape(2 * SSM_GROUPS, SSM_GROUP, SSM_STATE).sum(axis=1).reshape(sp["ssm_a_im"].shape)
    g["ssm_log_dt"] = d_ldt.reshape(per_group).sum(axis=1).reshape(sp["ssm_log_dt"].shape)
    g["ssm_b_re"] = d_br.reshape(sp["ssm_b_re"].shape)
    g["ssm_b_im"] = d_bi.reshape(sp["ssm_b_im"].shape)
    g["ssm_c_re"] = d_cr.reshape(sp["ssm_c_re"].shape)
    g["ssm_c_im"] = d_ci.reshape(sp["ssm_c_im"].shape)
    if reducer is not None:
        travelling = travelling + [d_ar, d_br, d_cr]
    dg1, dup1 = _ffn_bwd_act("ffn1_bwd_act", dh1_b, w_f1, g1, up1, tm, after=travelling)
    dw_f1 = _ffn_dw("ffn1_dw", u1, dg1, dup1, a1, dh1_b, tm).reshape(N_SHARD, 3 * FF_SH, D_MODEL)
    if reducer is not None:
        travelling = reducer.start("b", [dw_f1], after=reducer.finish("a", [dw_f1]))
        travelling = travelling + reducer.join_start("a", after=travelling)
    grad_x, _, g["ffn1_norm"] = _ffn_dx("ffn1_dx", dg1, dup1, w_f1, x, sp["ffn1_norm"], dh1, tm, after=travelling)
    if reducer is not None:
        reducer.finish("b", [grad_x])
        reducer.join_finish("a", [grad_x])
    return loss, grad_x, early + [dw_f1], g


def _mesh_place():
    x, y, c = lax.axis_index("x"), lax.axis_index("y"), lax.axis_index("c")
    chips = [(1 - x, y), (x, 1 - y), (1 - x, 1 - y)]
    return x, y, c, chips


def _remote(src, dst, send_sems, recv_sems, k, to):
    return pltpu.make_async_remote_copy(src_ref=src, dst_ref=dst, send_sem=send_sems.at[k], recv_sem=recv_sems.at[k],
                                        device_id=to, device_id_type=MESH)


def _sibling_swap_halves(tag, grads, after=()):
    n = len(grads)
    after_ops, after_specs = _after_operands(after)

    def body(*refs):
        ins, outs = refs[:n], refs[n + len(after_ops):2 * n + len(after_ops)]
        send_sems, recv_sems = refs[2 * n + len(after_ops):]
        x, y, c, _ = _mesh_place()
        sibling = (x, y, 1 - c)
        copies = []
        for k in range(n):
            half = grads[k].shape[1] // 2
            theirs = pl.ds(pl.multiple_of((1 - c) * half, 16), half)
            cp = _remote(ins[k].at[:, theirs, :], outs[k], send_sems, recv_sems, k, sibling)
            cp.start()
            copies.append(cp)
        for cp in copies:
            cp.wait_recv()
        for cp in copies:
            cp.wait_send()

    hbm = BS(memory_space=pl.ANY)
    return pl.pallas_call(
        body, out_shape=tuple(SDS((g.shape[0], g.shape[1] // 2, g.shape[2]), g.dtype) for g in grads),
        in_specs=[hbm] * n + after_specs, out_specs=(hbm,) * n,
        scratch_shapes=[pltpu.SemaphoreType.DMA((n,)), pltpu.SemaphoreType.DMA((n,))],
        name="reduce_sibling_send_" + tag, compiler_params=_params())(*grads, *after_ops)


def _row_tile(rows, cap=512):
    return max(r for r in range(16, cap + 1, 16) if rows % r == 0)


def _chip_presum(k, grad, got, c_idx):
    n_sh, rows, cols = grad.shape
    half = rows // 2
    tr = _row_tile(half)
    grad4 = grad.reshape(n_sh, 2, half, cols)

    def body(c_ref, a_ref, b_ref, o_ref):
        o_ref[...] = (a_ref[...].astype(F32) + b_ref[...].astype(F32)).astype(o_ref.dtype)

    return pl.pallas_call(
        body, out_shape=SDS((n_sh, half, cols), BF16),
        grid_spec=pltpu.PrefetchScalarGridSpec(
            num_scalar_prefetch=1, grid=(n_sh, half // tr),
            in_specs=[BS((None, None, tr, cols), lambda s, i, c_ref: (s, c_ref[0], i, 0)),
                      BS((None, tr, cols), lambda s, i, c_ref: (s, i, 0))],
            out_specs=BS((None, tr, cols), lambda s, i, c_ref: (s, i, 0))),
        name=f"reduce_presum{k}", compiler_params=_params("parallel", "parallel"))(c_idx, grad4, got)


HBM_SPEC = BS(memory_space=pltpu.HBM)
SEM_SPEC = BS(memory_space=pltpu.SEMAPHORE)
DATAFLOW = pltpu.SideEffectType.DATAFLOW_SIDE_EFFECTING


def _chip_exchange_copies(parts, lands, send_sems, recv_sems):
    _, _, c, chips = _mesh_place()
    return [_remote(parts[k].at[2 * px + py], lands[k].at[j], send_sems, recv_sems, 3 * k + j, (px, py, c))
            for k in range(len(parts)) for j, (px, py) in enumerate(chips)]


def _gather_copies(shards, lands, send_sems, recv_sems):
    x, y, c, chips = _mesh_place()
    return [_remote(shards[k], lands[k].at[2 * x + y], send_sems, recv_sems, 3 * k + j, (px, py, c))
            for k in range(len(shards)) for j, (px, py) in enumerate(chips)]


def _gather_half_copies(shards, lands, send_sems, recv_sems):
    x, y, c, chips = _mesh_place()
    out = []
    for k in range(len(shards)):
        half = shards[k].shape[0] // 2
        mine = pl.ds(pl.multiple_of(c * half, 16), half)
        for j, (px, py) in enumerate(chips):
            out.append(_remote(shards[k].at[mine, :], lands[k].at[2 * x + y, mine, :], send_sems, recv_sems,
                               3 * k + j, (px, py, c)))
    return out


def _sibling_fill(tag, lands):
    n = len(lands)

    def body(*refs):
        outs = refs[n:2 * n]
        send_sems, recv_sems = refs[2 * n:]
        x, y, c, chips = _mesh_place()
        copies = []
        for k in range(n):
            half = lands[k].shape[1] // 2
            mine = pl.ds(pl.multiple_of(c * half, 16), half)
            for j, (px, py) in enumerate(chips):
                blk = outs[k].at[2 * px + py, mine, :]
                copies.append(_remote(blk, blk, send_sems, recv_sems, 3 * k + j, (x, y, 1 - c)))
        for cp in copies:
            cp.start()
        for cp in copies:
            cp.wait_recv()
        for cp in copies:
            cp.wait_send()

    hbm = BS(memory_space=pl.ANY)
    return list(pl.pallas_call(
        body, out_shape=tuple(SDS(a.shape, a.dtype) for a in lands),
        in_specs=[hbm] * n, out_specs=(hbm,) * n, input_output_aliases={k: k for k in range(n)},
        scratch_shapes=[pltpu.SemaphoreType.DMA((3 * n,)), pltpu.SemaphoreType.DMA((3 * n,))],
        name="gather_fill_" + tag, compiler_params=_params())(*lands))


def _swap_copies(grads, lands, send_sems, recv_sems):
    x, y, c, _ = _mesh_place()
    out = []
    for k in range(len(grads)):
        half = grads[k].shape[1] // 2
        theirs = pl.ds(pl.multiple_of((1 - c) * half, 16), half)
        out.append(_remote(grads[k].at[:, theirs, :], lands[k], send_sems, recv_sems, k, (x, y, 1 - c)))
    return out


def _join_copies(fulls, same, send_sems, recv_sems):
    x, y, c, _ = _mesh_place()
    out = []
    for k in range(len(fulls)):
        half = fulls[k].shape[0] // 2
        mine = fulls[k].at[pl.ds(pl.multiple_of(c * half, 8), half), :]
        out.append(_remote(mine, mine, send_sems, recv_sems, k, (x, y, 1 - c)))
    return out


def _everyone_copies(packs, lands, send_sems, recv_sems):
    x, y, c, _ = _mesh_place()
    out = []
    for k in range(len(packs)):
        for j in range(N_DEV - 1):
            bx, by, bc = (j + 1) >> 2 & 1, (j + 1) >> 1 & 1, (j + 1) & 1
            peer = (x ^ bx, y ^ by, c ^ bc)
            out.append(_remote(packs[k], lands[k].at[4 * x + 2 * y + c], send_sems, recv_sems, (N_DEV - 1) * k + j, peer))
    return out


def _split_start(name, copies, sources, land_shapes, after=(), fanout=3):
    n = len(sources)
    n_land = len(land_shapes)
    m = n + n_land
    n_sems = fanout * n
    after_ops, after_specs = _after_operands(after)

    def body(*refs):
        ins = refs[:n]
        lands = refs[n:m] if n_land else ins
        send_sems, recv_sems = refs[m + len(after_ops)], refs[m + len(after_ops) + 1]
        token = refs[-1]
        for cp in copies(ins, lands, send_sems, recv_sems):
            cp.start()
        token[...] = jnp.zeros_like(token)

    lands = [pltpu.with_memory_space_constraint(lax.empty(s, d), pltpu.HBM) for s, d in land_shapes]
    sources = [pltpu.with_memory_space_constraint(p, pltpu.HBM) for p in sources]
    thru = [pltpu.HBM(a.shape, a.dtype) for a in sources + lands]
    out = pl.pallas_call(
        body, name=name,
        out_shape=(pltpu.SemaphoreType.DMA((n_sems,)), pltpu.SemaphoreType.DMA((n_sems,)), *thru, SDS((8, 128), F32)),
        in_specs=[HBM_SPEC] * m + after_specs,
        out_specs=(SEM_SPEC, SEM_SPEC, *[HBM_SPEC] * m, BS(memory_space=pltpu.VMEM)),
        input_output_aliases={i: 2 + i for i in range(m)},
        compiler_params=pltpu.CompilerParams(has_side_effects=DATAFLOW))(*sources, *lands, *after_ops)
    return out[0], out[1], list(out[2:2 + n]), list(out[2 + n:2 + m]), out[-1]


def _split_wait(name, copies, send_sems, recv_sems, sources, lands, after):
    n = len(sources)
    m = n + len(lands)
    after_ops, after_specs = _after_operands(after)

    def body(*refs):
        ins = refs[:n]
        zones = refs[n:m] if m > n else ins
        for cp in copies(ins, zones, refs[m], refs[m + 1]):
            cp.wait_send()
            cp.wait_recv()

    out = pl.pallas_call(
        body, name=name,
        out_shape=tuple(pltpu.HBM(a.shape, a.dtype) for a in sources + lands),
        in_specs=[HBM_SPEC] * m + [SEM_SPEC, SEM_SPEC] + after_specs, out_specs=(HBM_SPEC,) * m,
        input_output_aliases={i: i for i in range(m)},
        compiler_params=pltpu.CompilerParams(has_side_effects=DATAFLOW))(*sources, *lands, send_sems, recv_sems, *after_ops)
    return list(out[:n]), list(out[n:])


class _WeightGatherer:
    def __init__(self, shards):
        self.shards, self.open = shards, {}
        self.me = 2 * lax.axis_index("x") + lax.axis_index("y")

    HALVED = ("f1a",)

    def start(self, tag, after=()):
        shapes = [((N_SHARD,) + s.shape, s.dtype) for s in self.shards[tag]]
        copies = _gather_half_copies if tag in self.HALVED else _gather_copies
        self.open[tag] = _split_start("gather_start_" + tag, copies, self.shards[tag], shapes, after)
        return [self.open[tag][-1]]

    def finish(self, tag, after):
        send_sems, recv_sems, shards, lands, _ = self.open.pop(tag)
        copies = _gather_half_copies if tag in self.HALVED else _gather_copies
        shards, lands = _split_wait("gather_wait_" + tag, copies, send_sems, recv_sems, shards, lands, after)
        if tag in self.HALVED:
            lands = _sibling_fill(tag, lands)
        return [lax.dynamic_update_slice(zone, s[None], (self.me, 0, 0)) for zone, s in zip(lands, shards)]


class _GradReducer:
    def __init__(self):
        self.c_idx = lax.axis_index("c").astype(jnp.int32).reshape(1)
        self.place = jnp.stack([2 * lax.axis_index("x") + lax.axis_index("y"), lax.axis_index("c")]).astype(jnp.int32)
        self.swaps, self.open, self.landed, self.joins, self.reduced = {}, {}, {}, {}, []

    def swap_start(self, tag, grads, after=()):
        shapes = [((g.shape[0], g.shape[1] // 2, g.shape[2]), g.dtype) for g in grads]
        self.swaps[tag] = _split_start("reduce_swap_start_" + tag, _swap_copies, grads, shapes, after, fanout=1)
        return [self.swaps[tag][-1]]

    def start(self, tag, grads, after=(), swapped=()):
        pairs = []
        for s in swapped:
            send_sems, recv_sems, early, lands, _ = self.swaps.pop(s)
            pairs += zip(*_split_wait("reduce_swap_wait_" + s, _swap_copies, send_sems, recv_sems, early, lands, grads[-1:]))
        pairs += zip(grads, _sibling_swap_halves(tag, grads, after))
        parts = [_chip_presum(f"{tag}{k}", g, s, self.c_idx) for k, (g, s) in enumerate(pairs)]
        shapes = [((3,) + p.shape[1:], p.dtype) for p in parts]
        self.open[tag] = _split_start("reduce_exchange_start_" + tag, _chip_exchange_copies, parts, shapes)
        return [self.open[tag][-1]]

    def finish(self, tag, after):
        send_sems, recv_sems, parts, lands, _ = self.open.pop(tag)
        self.landed[tag] = _split_wait("reduce_exchange_wait_" + tag, _chip_exchange_copies, send_sems, recv_sems, parts, lands, after)
        return self.landed[tag][1][:1]

    def _sums(self, tag, after=()):
        parts, landed = self.landed.pop(tag)
        return [_chip_sum(f"{tag}{k}", p, got, self.place, after) for k, (p, got) in enumerate(zip(parts, landed))]

    def join_start(self, tag, after=()):
        self.joins[tag] = _split_start("reduce_join_start_" + tag, _join_copies, self._sums(tag, after), [], fanout=1)
        return [self.joins[tag][-1]]

    def join_finish(self, tag, after):
        send_sems, recv_sems, fulls, _, _ = self.joins.pop(tag)
        self.reduced += _split_wait("reduce_join_wait_" + tag, _join_copies, send_sems, recv_sems, fulls, [], after)[0]

    def join(self, tag, after=()):
        self.reduced += _sibling_join_halves(self._sums(tag), after)


def _chip_sum(k, part, got, place, after=()):
    _, half, cols = part.shape
    tr = _row_tile(half)
    n_t = half // tr
    after_ops, after_specs = _after_operands(after)

    def body(place_ref, a_ref, b_ref, *rest):
        o_ref = rest[-1]
        acc = a_ref[...].astype(F32)
        for j in range(3):
            acc = acc + b_ref[j].astype(F32)
        o_ref[...] = acc

    return pl.pallas_call(
        body, out_shape=SDS((2 * half, cols), F32),
        grid_spec=pltpu.PrefetchScalarGridSpec(
            num_scalar_prefetch=1, grid=(n_t,),
            in_specs=[BS((None, tr, cols), lambda i, place_ref: (place_ref[0], i, 0)),
                      BS((3, tr, cols), lambda i, place_ref: (0, i, 0))] + after_specs,
            out_specs=BS((tr, cols), lambda i, place_ref: (place_ref[1] * n_t + i, 0))),
        name=f"reduce_sum{k}", compiler_params=_params("parallel"))(place, part, got, *after_ops)


def _sibling_join_halves(fulls, after=()):
    n = len(fulls)
    after_ops, after_specs = _after_operands(after)

    def body(*refs):
        outs = refs[n + len(after_ops):2 * n + len(after_ops)]
        send_sems, recv_sems = refs[2 * n + len(after_ops):]
        copies = _join_copies(outs, outs, send_sems, recv_sems)
        for cp in copies:
            cp.start()
        for cp in copies:
            cp.wait_recv()
        for cp in copies:
            cp.wait_send()

    hbm = BS(memory_space=pl.ANY)
    return list(pl.pallas_call(
        body, out_shape=tuple(SDS(f.shape, f.dtype) for f in fulls),
        in_specs=[hbm] * n + after_specs, out_specs=(hbm,) * n, input_output_aliases={k: k for k in range(n)},
        scratch_shapes=[pltpu.SemaphoreType.DMA((n,)), pltpu.SemaphoreType.DMA((n,))],
        name="reduce_sibling_join", compiler_params=_params())(*fulls, *after_ops))


N_DEV = 8


def _sum_devices(packs):
    _, rows, lanes = packs.shape

    def body(p_ref, o_ref):
        acc = p_ref[0]
        for dev in range(1, N_DEV):
            acc = acc + p_ref[dev]
        o_ref[...] = acc

    vm = BS(memory_space=pltpu.VMEM)
    return pl.pallas_call(body, out_shape=SDS((rows, lanes), F32), in_specs=[vm], out_specs=vm,
                          name="small_sum", compiler_params=_params())(packs)


def _adamw(name, w, grad, row0, m, v, after=()):
    rows, cols = w.shape
    tr = rows if rows < 16 else _row_tile(rows, 352)
    bc1 = 1.0 - ADAM_B1 ** ADAM_STEP
    bc2 = 1.0 - ADAM_B2 ** ADAM_STEP
    after_ops, after_specs = _after_operands(after)

    def body(w_ref, g_ref, m_ref, v_ref, *rest):
        go_ref, d_ref, mo_ref, vo_ref = rest[len(after_ops):]
        g = g_ref[...]
        m_new = ADAM_B1 * m_ref[...] + (1.0 - ADAM_B1) * g
        v_new = ADAM_B2 * v_ref[...] + (1.0 - ADAM_B2) * (g * g)
        go_ref[...] = g
        mo_ref[...] = m_new
        vo_ref[...] = v_new
        d_ref[...] = -ADAM_LR * ((m_new / bc1) / (jnp.sqrt(v_new / bc2) + ADAM_EPS) + ADAM_WD * w_ref[...])

    blk = BS((tr, cols), lambda i: (i, 0))
    shape = SDS((rows, cols), F32)
    return pl.pallas_call(
        body, out_shape=(shape,) * 4, grid=(rows // tr,),
        in_specs=[blk, BS((tr, cols), lambda i: (row0 // tr + i, 0)), blk, blk] + after_specs, out_specs=(blk,) * 4,
        name=name, compiler_params=_params("parallel"))(w, grad, m, v, *after_ops)


SMALL_LANES = 128


def _pack_small(parts):
    flat = jnp.concatenate([jnp.ravel(p) for p in parts])
    rows = -(-flat.shape[0] // (64 * SMALL_LANES)) * 64
    return jnp.pad(flat, (0, rows * SMALL_LANES - flat.shape[0])).reshape(rows, SMALL_LANES)


def _unpack_small(packed, like):
    flat = jnp.ravel(packed)
    out, at = [], 0
    for p in like:
        out.append(flat[at:at + p.size].reshape(p.shape))
        at += p.size
    return out


def kernel(x, mem, ffn1_norm, ffn1_w_gate, ffn1_w_up, ffn1_w_down, mix_norm, w_in, pool_w, pool_scale, w_pool_proj, ssm_a_re, ssm_a_im, ssm_log_dt, ssm_b_re, ssm_b_im, ssm_c_re, ssm_c_im, ssm_d, w_glu_val, w_glu_gate, w_mix_out, xattn_norm, mem_norm, w_q, w_kv, w_xo, ffn2_norm, ffn2_w_gate, ffn2_w_up, ffn2_w_down, final_norm, loss_target, m_ffn1_norm, m_ffn1_w_gate, m_ffn1_w_up, m_ffn1_w_down, m_mix_norm, m_w_in, m_pool_w, m_pool_scale, m_w_pool_proj, m_ssm_a_re, m_ssm_a_im, m_ssm_log_dt, m_ssm_b_re, m_ssm_b_im, m_ssm_c_re, m_ssm_c_im, m_ssm_d, m_w_glu_val, m_w_glu_gate, m_w_mix_out, m_xattn_norm, m_mem_norm, m_w_q, m_w_kv, m_w_xo, m_ffn2_norm, m_ffn2_w_gate, m_ffn2_w_up, m_ffn2_w_down, m_final_norm, v_ffn1_norm, v_ffn1_w_gate, v_ffn1_w_up, v_ffn1_w_down, v_mix_norm, v_w_in, v_pool_w, v_pool_scale, v_w_pool_proj, v_ssm_a_re, v_ssm_a_im, v_ssm_log_dt, v_ssm_b_re, v_ssm_b_im, v_ssm_c_re, v_ssm_c_im, v_ssm_d, v_w_glu_val, v_w_glu_gate, v_w_mix_out, v_xattn_norm, v_mem_norm, v_w_q, v_w_kv, v_w_xo, v_ffn2_norm, v_ffn2_w_gate, v_ffn2_w_up, v_ffn2_w_down, v_final_norm):
    given = dict(locals())
    w = {n: given[n] for n in WEIGHTS}
    m = {n: given["m_" + n] for n in WEIGHTS}
    v = {n: given["v_" + n] for n in WEIGHTS}

    def shard_view(a, n):
        return a[0].T if n in TRANSPOSED else a[0]

    def shard_unview(a, n):
        return (a.T if n in TRANSPOSED else a)[None]

    shards = {tag: [jnp.concatenate([shard_view(w[n], n).astype(BF16) for n in grp], axis=0) for grp in arrays]
              for tag, arrays in GATHER_PHASES.items()}
    reducer = _GradReducer()
    ws, ms, vs = ({n: _small_view(a[n], n) for n in SMALL} for a in (w, m, v))
    loss_part, grad_x, _, small = _device_step(x[0], mem[0], loss_target[0], _WeightGatherer(shards), ws, reducer)

    small_like = [ws[n] for n in SMALL] + [loss_part[0, :1]]
    pack = _pack_small([small[n] for n in SMALL] + [loss_part[0, :1]])
    everyone = _split_start("small_start", _everyone_copies, [pack], [((N_DEV,) + pack.shape, F32)], fanout=N_DEV - 1)
    reducer.join("b", after=everyone[-1:])

    grads, delta, new_m, new_v = {}, {}, {}, {}
    big_done = []
    for grp, red in zip(REDUCE_GROUPS, reducer.reduced):
        row0 = 0
        for n in grp:
            w_n = shard_view(w[n], n)
            outs = _adamw("adamw_" + n, w_n, red, row0, shard_view(m[n], n), shard_view(v[n], n), after=everyone[-1:])
            grads[n], delta[n], new_m[n], new_v[n] = (shard_unview(o, n) for o in outs)
            big_done.append(outs[1])
            row0 += w_n.shape[0]

    send_sems, recv_sems, packs, landed, _ = everyone
    packs, landed = _split_wait("small_wait", _everyone_copies, send_sems, recv_sems, packs, landed, big_done)
    mine = 4 * lax.axis_index("x") + 2 * lax.axis_index("y") + lax.axis_index("c")
    summed = _sum_devices(lax.dynamic_update_slice(landed[0], packs[0][None], (mine, 0, 0)))
    g_small = dict(zip(SMALL + ("loss",), _unpack_small(summed, small_like)))
    loss = g_small.pop("loss").reshape(())
    narrow = [n for n in SMALL if ws[n].ndim > 3]
    dense = [n for n in SMALL if n not in narrow]
    for n in narrow:
        two_d = (-1, ws[n].shape[-1])
        outs = _adamw("adamw_" + n, ws[n].reshape(two_d), g_small[n].reshape(two_d), 0, ms[n].reshape(two_d), vs[n].reshape(two_d))
        grads[n], delta[n], new_m[n], new_v[n] = (_small_view(o.reshape(ws[n].shape), n) for o in outs)
    dense_like = [ws[n] for n in dense]
    packed = _adamw("adamw_small", _pack_small(dense_like), _pack_small([g_small[n] for n in dense]), 0,
                    _pack_small([ms[n] for n in dense]), _pack_small([vs[n] for n in dense]))
    for out, store in zip(packed, (grads, delta, new_m, new_v)):
        for n, val in zip(dense, _unpack_small(out, dense_like)):
            store[n] = val

    return (loss, grad_x[None], *[grads[n] for n in WEIGHTS], *[delta[n] for n in WEIGHTS],
            *[new_m[n] for n in WEIGHTS], *[new_v[n] for n in WEIGHTS])
```

```python
import functools
import math

import jax
import jax.numpy as jnp
from jax import lax
from jax.experimental import pallas as pl
from jax.experimental.pallas import tpu as pltpu

F32 = jnp.float32
BF16 = jnp.bfloat16
SDS = jax.ShapeDtypeStruct
BS = pl.BlockSpec
MESH = pl.DeviceIdType.MESH

D_MODEL = 1024
D_FF = 2816
N_SHARD = 4
FF_SH = D_FF // N_SHARD
D_POOL = 512
POOL_WINDOWS = (2, 4, 8, 16)
POOL_GROUP = 128
D_SSM = 256
SSM_GROUPS = 16
SSM_GROUP = 16
SSM_STATE = 64
SSM_CH = SSM_GROUPS * SSM_STATE
N_HEADS = 4
HEAD_DIM = 256
EPS = 1e-6
ADAM_LR, ADAM_B1, ADAM_B2, ADAM_EPS, ADAM_WD, ADAM_STEP = 0.001, 0.9, 0.999, 1e-08, 0.01, 10

VMEM_LIMIT_V7X = 52 * 1024 * 1024
TM = 512

NN = (((1,), (0,)), ((), ()))
NT = (((1,), (1,)), ((), ()))
TN = (((0,), (0,)), ((), ()))


def _params(*sem):
    return pltpu.CompilerParams(dimension_semantics=sem if sem else None, vmem_limit_bytes=VMEM_LIMIT_V7X)


def _dot(a, b, dims=NN):
    return lax.dot_general(a.astype(BF16), b.astype(BF16), dims, preferred_element_type=F32)


def _sigmoid(v):
    return pl.reciprocal(1.0 + jnp.exp(-v), approx=True)


def _block_dims(spec):
    return tuple(d for d in spec.block_shape if d is not None)


def _after_operands(after):
    return list(after), [BS(memory_space=pl.ANY)] * len(after)


def _mm(name, pairs, *, grid, out_shape, out_spec, red_axis=None, extras=(), epilogue=None, after=()):
    n_pairs, n_extra = len(pairs), len(extras)
    n_red = grid[red_axis] if red_axis is not None else 1
    dims = [p[4] for p in pairs]

    def body(*refs):
        ab = refs[:2 * n_pairs]
        ex = refs[2 * n_pairs:2 * n_pairs + n_extra]
        o_ref = refs[2 * n_pairs + n_extra + len(after)]

        def partial():
            acc = None
            for p in range(n_pairs):
                t = _dot(ab[2 * p][...], ab[2 * p + 1][...], dims[p])
                acc = t if acc is None else acc + t
            return acc

        def finish(acc):
            res = epilogue(acc, *[e[...] for e in ex]) if epilogue is not None else acc
            o_ref[...] = res.astype(o_ref.dtype)

        if n_red == 1:
            finish(partial())
        else:
            acc_ref = refs[-1]
            k = pl.program_id(red_axis)

            @pl.when(k == 0)
            def _():
                acc_ref[...] = jnp.zeros_like(acc_ref)

            acc_ref[...] += partial()

            @pl.when(k == n_red - 1)
            def _():
                finish(acc_ref[...])

    operands, in_specs = [], []
    for a, a_spec, b, b_spec, _ in pairs:
        operands += [a, b]
        in_specs += [a_spec, b_spec]
    for e, e_spec in extras:
        operands.append(e)
        in_specs.append(e_spec)
    after_ops, after_specs = _after_operands(after)
    operands += after_ops
    in_specs += after_specs
    scratch = [pltpu.VMEM(_block_dims(out_spec), F32)] if n_red > 1 else []
    sem = tuple("arbitrary" if ax == red_axis else "parallel" for ax in range(len(grid)))
    return pl.pallas_call(body, out_shape=out_shape, grid=grid, in_specs=in_specs, out_specs=out_spec,
                          scratch_shapes=scratch, name=name, compiler_params=_params(*sem))(*operands)


def _rmsnorm(name, h, gain, tm, after=()):
    t, d = h.shape
    after_ops, after_specs = _after_operands(after)

    def body(h_ref, g_ref, *rest):
        u_ref = rest[-1]
        hv = h_ref[...]
        r = lax.rsqrt(jnp.mean(hv * hv, axis=-1, keepdims=True) + EPS)
        u_ref[...] = ((hv * r) * g_ref[...]).astype(u_ref.dtype)

    return pl.pallas_call(
        body, out_shape=SDS((t, d), BF16), grid=(t // tm,),
        in_specs=[BS((tm, d), lambda i: (i, 0)), BS((1, d), lambda i: (0, 0))] + after_specs,
        out_specs=BS((tm, d), lambda i: (i, 0)), name=name, compiler_params=_params("parallel"))(h, gain, *after_ops)


def _rmsnorm_bwd(name, h, gain, du, dh_in, tm):
    t, d = h.shape
    has_in = dh_in is not None

    def body(*refs):
        if has_in:
            h_ref, g_ref, du_ref, dhin_ref, dh_ref, dhb_ref, dg_ref = refs
        else:
            h_ref, g_ref, du_ref, dh_ref, dhb_ref, dg_ref = refs
        i = pl.program_id(0)
        hv = h_ref[...]
        r = lax.rsqrt(jnp.mean(hv * hv, axis=-1, keepdims=True) + EPS)
        n = hv * r
        duv = du_ref[...].astype(F32)
        dn = duv * g_ref[...]
        dh = r * (dn - n * jnp.mean(dn * n, axis=-1, keepdims=True))
        if has_in:
            dh = dhin_ref[...] + dh
        dh_ref[...] = dh
        dhb_ref[...] = dh.astype(BF16)

        @pl.when(i == 0)
        def _():
            dg_ref[...] = jnp.zeros_like(dg_ref)

        dg_ref[...] += jnp.sum(duv * n, axis=0, keepdims=True)

    row = BS((tm, d), lambda i: (i, 0))
    vec = BS((1, d), lambda i: (0, 0))
    operands = [h, gain, du] + ([dh_in] if has_in else [])
    in_specs = [row, vec, row] + ([row] if has_in else [])
    return pl.pallas_call(
        body, out_shape=(SDS((t, d), F32), SDS((t, d), BF16), SDS((1, d), F32)), grid=(t // tm,),
        in_specs=in_specs, out_specs=(row, row, vec), name=name, compiler_params=_params("arbitrary"))(*operands)


def _loss_head_tile(i, hv, g_ref, t_ref, loss_ref, dh_ref, dhb_ref, dg_ref):
    g = g_ref[...]
    r = lax.rsqrt(jnp.mean(hv * hv, axis=-1, keepdims=True) + EPS)
    n = hv * r
    err = n * g - t_ref[...]
    dy = err * (1.0 / hv.shape[-1])
    dn = dy * g
    dh = r * (dn - n * jnp.mean(dn * n, axis=-1, keepdims=True))
    dh_ref[...] = dh
    dhb_ref[...] = dh.astype(BF16)

    @pl.when(i == 0)
    def _():
        dg_ref[...] = jnp.zeros_like(dg_ref)
        loss_ref[...] = jnp.zeros_like(loss_ref)

    dg_ref[...] += jnp.sum(dy * n, axis=0, keepdims=True)
    part = 0.5 * jnp.sum(jnp.mean(err * err, axis=-1, keepdims=True), axis=0, keepdims=True)
    loss_ref[...] += jnp.broadcast_to(part, loss_ref.shape)


def _norm_tile(h, g_ref, u_ref):
    r = lax.rsqrt(jnp.mean(h * h, axis=-1, keepdims=True) + EPS)
    u_ref[...] = ((h * r) * g_ref[...]).astype(u_ref.dtype)


FFN_GATE, FFN_UP, FFN_DOWN = 0, 1, 2


def _ffn_up(name, u, w_f, tm, after=()):
    t, d = u.shape
    after_ops, after_specs = _after_operands(after)

    def body(u_ref, wg_ref, wu_ref, *rest):
        pg_ref, pu_ref, a_ref = rest[len(after_ops):]
        uv = u_ref[...]
        for s in range(N_SHARD):
            g = _dot(uv, wg_ref[s], NT)
            up = _dot(uv, wu_ref[s], NT)
            sg = _sigmoid(g)
            silu = g * sg
            a_ref[s] = (silu * up).astype(BF16)
            pu_ref[s] = (0.5 * silu).astype(BF16)
            pg_ref[s] = (0.5 * sg * (1.0 + g * (1.0 - sg)) * up).astype(BF16)

    hid = BS((N_SHARD, tm, FF_SH), lambda i: (0, i, 0))
    shape = SDS((N_SHARD, t, FF_SH), BF16)
    return pl.pallas_call(
        body, out_shape=(shape, shape, shape), grid=(t // tm,),
        in_specs=[BS((tm, d), lambda i: (i, 0)), _ffn_all_shards_spec(w_f["gate"][1]),
                  _ffn_all_shards_spec(w_f["up"][1])] + after_specs,
        out_specs=(hid, hid, hid), name=name,
        compiler_params=_params("parallel"))(u, w_f["gate"][0], w_f["up"][0], *after_ops)


def _ffn_all_shards_spec(block):
    return BS((N_SHARD, FF_SH, D_MODEL), lambda i: (0, block, 0))


def _ffn_down(name, a, w_f, resid, tm, next_gain=None, head=None):
    t, d = resid.shape
    row = BS((tm, d), lambda i: (i, 0))
    vec = BS((1, d), lambda i: (0, 0))

    def body(a_ref, w_ref, res_ref, *rest):
        acc = _dot(a_ref[0], w_ref[0])
        for s in range(1, N_SHARD):
            acc = acc + _dot(a_ref[s], w_ref[s])
        h = res_ref[...] + 0.5 * acc
        if head is not None:
            _loss_head_tile(pl.program_id(0), h, *rest)
        else:
            g_ref, h_ref, u_ref = rest
            h_ref[...] = h
            _norm_tile(h, g_ref, u_ref)

    if head is not None:
        extra, extra_specs = list(head), [vec, row]
        out_shape = (SDS((1, 128), F32), SDS((t, d), F32), SDS((t, d), BF16), SDS((1, d), F32))
        out_specs = (BS((1, 128), lambda i: (0, 0)), row, row, vec)
    else:
        extra, extra_specs = [next_gain], [vec]
        out_shape = (SDS((t, d), F32), SDS((t, d), BF16))
        out_specs = (row, row)
    return pl.pallas_call(
        body, out_shape=out_shape, grid=(t // tm,),
        in_specs=[BS((N_SHARD, tm, FF_SH), lambda i: (0, i, 0)), _ffn_all_shards_spec(w_f["down"][1]), row] + extra_specs,
        out_specs=out_specs, name=name,
        compiler_params=_params("arbitrary" if head is not None else "parallel"))(a, w_f["down"][0], resid, *extra)


def _mm_resid_norm(name, a, b, resid, next_gain, tm):
    t, d = resid.shape

    def body(a_ref, b_ref, res_ref, g_ref, h_ref, u_ref):
        h = res_ref[...] + _dot(a_ref[...], b_ref[...])
        h_ref[...] = h
        _norm_tile(h, g_ref, u_ref)

    row = BS((tm, d), lambda i: (i, 0))
    return pl.pallas_call(
        body, out_shape=(SDS((t, d), F32), SDS((t, d), BF16)), grid=(t // tm,),
        in_specs=[BS((tm, a.shape[1]), lambda i: (i, 0)), BS(b.shape, lambda i: (0, 0)), row, BS((1, d), lambda i: (0, 0))],
        out_specs=(row, row), name=name, compiler_params=_params("parallel"))(a, b, resid, next_gain)


def _ffn_bwd_act(name, dh_b, w_f, pg, pu, tm, after=()):
    t, d = dh_b.shape
    after_ops, after_specs = _after_operands(after)

    def body(dh_ref, wd_ref, pg_ref, pu_ref, *rest):
        dg_ref, dup_ref = rest[len(after_ops):]
        dh = dh_ref[...]
        for s in range(N_SHARD):
            da = _dot(dh, wd_ref[s], NT)
            dg_ref[s] = (da * pg_ref[s].astype(F32)).astype(BF16)
            dup_ref[s] = (da * pu_ref[s].astype(F32)).astype(BF16)

    hid = BS((N_SHARD, tm, FF_SH), lambda i: (0, i, 0))
    shape = SDS((N_SHARD, t, FF_SH), BF16)
    return pl.pallas_call(
        body, out_shape=(shape, shape), grid=(t // tm,),
        in_specs=[BS((tm, d), lambda i: (i, 0)), _ffn_all_shards_spec(w_f["down"][1]), hid, hid] + after_specs,
        out_specs=(hid, hid), name=name,
        compiler_params=_params("parallel"))(dh_b, w_f["down"][0], pg, pu, *after_ops)


def _ffn_dw(name, u, dg, dup, a, dh_b, tm):
    t, d = u.shape
    tm = min(2 * tm, t)
    n_t = t // tm

    def body(u_ref, dg_ref, dup_ref, a_ref, dh_ref, o_ref, acc):
        i = pl.program_id(1)

        @pl.when(i == 0)
        def _():
            acc[...] = jnp.zeros_like(acc)

        uv = u_ref[...]
        acc[FFN_GATE] += _dot(dg_ref[...], uv, TN)
        acc[FFN_UP] += _dot(dup_ref[...], uv, TN)
        acc[FFN_DOWN] += _dot(a_ref[...], dh_ref[...], TN)

        @pl.when(i == n_t - 1)
        def _():
            o_ref[FFN_GATE] = acc[FFN_GATE].astype(BF16)
            o_ref[FFN_UP] = acc[FFN_UP].astype(BF16)
            o_ref[FFN_DOWN] = (0.5 * acc[FFN_DOWN]).astype(BF16)

    hid = BS((None, tm, FF_SH), lambda s, i: (s, i, 0))
    row = BS((tm, d), lambda s, i: (i, 0))
    return pl.pallas_call(
        body, out_shape=SDS((N_SHARD, 3, FF_SH, d), BF16), grid=(N_SHARD, n_t),
        in_specs=[row, hid, hid, hid, row], out_specs=BS((None, 3, FF_SH, d), lambda s, i: (s, 0, 0, 0)),
        scratch_shapes=[pltpu.VMEM((3, FF_SH, d), F32)],
        name=name, compiler_params=_params("parallel", "arbitrary"))(u, dg, dup, a, dh_b)


def _norm_bwd_tile(i, du, h_ref, g_ref, dhin_ref, dh_ref, dhb_ref, dg_ref):
    hv = h_ref[...]
    r = lax.rsqrt(jnp.mean(hv * hv, axis=-1, keepdims=True) + EPS)
    n = hv * r
    dn = du * g_ref[...]
    dh = dhin_ref[...] + r * (dn - n * jnp.mean(dn * n, axis=-1, keepdims=True))
    dh_ref[...] = dh
    dhb_ref[...] = dh.astype(BF16)

    @pl.when(i == 0)
    def _():
        dg_ref[...] = jnp.zeros_like(dg_ref)

    dg_ref[...] += jnp.sum(du * n, axis=0, keepdims=True)


def _norm_bwd_specs(tm):
    row = BS((tm, D_MODEL), lambda i: (i, 0))
    vec = BS((1, D_MODEL), lambda i: (0, 0))
    return [row, vec, row], (row, row, vec)


def _norm_bwd_shapes(t):
    return SDS((t, D_MODEL), F32), SDS((t, D_MODEL), BF16), SDS((1, D_MODEL), F32)


def _ffn_dx(name, dg, dup, w_f, h, gain, dh_in, tm, after=()):
    t = dg.shape[1]
    tm = tm // 2
    after_ops, after_specs = _after_operands(after)

    def body(dg_ref, dup_ref, wg_ref, wu_ref, h_ref, g_ref, dhin_ref, *rest):
        acc = _dot(dg_ref[0], wg_ref[0]) + _dot(dup_ref[0], wu_ref[0])
        for s in range(1, N_SHARD):
            acc = acc + _dot(dg_ref[s], wg_ref[s]) + _dot(dup_ref[s], wu_ref[s])
        _norm_bwd_tile(pl.program_id(0), acc, h_ref, g_ref, dhin_ref, *rest[len(after_ops):])

    hid = BS((N_SHARD, tm, FF_SH), lambda i: (0, i, 0))
    norm_in, norm_out = _norm_bwd_specs(tm)
    return pl.pallas_call(
        body, out_shape=_norm_bwd_shapes(t), grid=(t // tm,),
        in_specs=[hid, hid, _ffn_all_shards_spec(w_f["gate"][1]), _ffn_all_shards_spec(w_f["up"][1])] + norm_in + after_specs,
        out_specs=norm_out, name=name,
        compiler_params=_params("arbitrary"))(dg, dup, w_f["gate"][0], w_f["up"][0], h, gain, dh_in, *after_ops)


def _mm_norm_bwd(name, a, b, dims, h, gain, dh_in, tm):
    t = a.shape[0]

    def body(a_ref, b_ref, h_ref, g_ref, dhin_ref, *outs):
        _norm_bwd_tile(pl.program_id(0), _dot(a_ref[...], b_ref[...], dims), h_ref, g_ref, dhin_ref, *outs)

    norm_in, norm_out = _norm_bwd_specs(tm)
    return pl.pallas_call(
        body, out_shape=_norm_bwd_shapes(t), grid=(t // tm,),
        in_specs=[BS((tm, a.shape[1]), lambda i: (i, 0)), BS(b.shape, lambda i: (0, 0))] + norm_in,
        out_specs=norm_out, name=name, compiler_params=_params("arbitrary"))(a, b, h, gain, dh_in)


def _plain_mm(name, a, b, dims, out_dtype, tm, resid=None, after=()):
    t = a.shape[0]
    n = b.shape[1] if dims == NN else b.shape[0]
    extras = [(resid, BS((tm, n), lambda i: (i, 0)))] if resid is not None else []
    epi = (lambda acc, res: res + acc) if resid is not None else None
    return _mm(name, [(a, BS((tm, a.shape[1]), lambda i: (i, 0)), b, BS(b.shape, lambda i: (0, 0)), dims)],
               grid=(t // tm,), out_shape=SDS((t, n), out_dtype), out_spec=BS((tm, n), lambda i: (i, 0)),
               extras=extras, epilogue=epi, after=after)


def _dw_mm(name, a, b, tm, out_dtype=BF16, after=()):
    t, k = a.shape
    n = b.shape[1]
    tm = min(2 * tm, t)
    return _mm(name, [(a, BS((tm, k), lambda i: (i, 0)), b, BS((tm, n), lambda i: (i, 0)), TN)],
               grid=(t // tm,), red_axis=0, out_shape=SDS((k, n), out_dtype), out_spec=BS((k, n), lambda i: (0, 0)),
               after=after)


POOL_CHUNK = 256
POOL_HALO = 8


def _window_sum(v, width, lead):
    n = v.shape[0]
    s = v
    k = 1
    while k < width:
        s = s + pltpu.roll(s, n - k, 0)
        k *= 2
    return pltpu.roll(s, lead, 0) if lead else s


def _pool_count(base, left, right, t, shape):
    pos = base + lax.broadcasted_iota(jnp.int32, shape, 0)
    lo = jnp.maximum(pos - left, 0)
    hi = jnp.minimum(pos + right + 1, t)
    return (hi - lo).astype(F32)


def _pool_fwd(proj, pool_w, pool_scale):
    t = proj.shape[0]
    c, h = POOL_CHUNK, POOL_HALO
    n_chunks = t // c

    def body(proj_hbm, pw_ref, sc_ref, pooled_ref, mixed_ref, ms_ref, pad_ref, sem):
        cp = pltpu.make_async_copy(proj_hbm.at[:, pl.ds(0, D_POOL)], pad_ref.at[pl.ds(h, t), :], sem)
        cp.start()
        pad_ref[pl.ds(0, h), :] = jnp.zeros((h, D_POOL), F32)
        pad_ref[pl.ds(t + h, h), :] = jnp.zeros((h, D_POOL), F32)
        cp.wait()
        for g, width in enumerate(POOL_WINDOWS):
            left = width // 2
            right = width - 1 - left
            cols = slice(g * POOL_GROUP, (g + 1) * POOL_GROUP)
            wmat = pw_ref[g].astype(BF16)
            scale = sc_ref[:, cols]

            def chunk(ci, carry, left=left, right=right, width=width, cols=cols, wmat=wmat, scale=scale):
                base = pl.multiple_of(ci * c, c)
                v = pad_ref[pl.ds(base, c + 2 * h), cols]
                win = _window_sum(v, width, left)[h:h + c]
                cnt = _pool_count(base, left, right, t, (c, POOL_GROUP))
                pooled = (win / cnt - v[h:h + c]).astype(BF16)
                mixed = _dot(pooled, wmat)
                pooled_ref[pl.ds(base, c), cols] = pooled
                mixed_ref[pl.ds(base, c), cols] = mixed.astype(BF16)
                ms_ref[pl.ds(base, c), cols] = (mixed * scale).astype(BF16)
                return carry

            lax.fori_loop(0, n_chunks, chunk, 0)

    vm = BS(memory_space=pltpu.VMEM)
    shape = SDS((t, D_POOL), BF16)
    return pl.pallas_call(
        body, out_shape=(shape, shape, shape),
        in_specs=[BS(memory_space=pl.ANY), vm, vm], out_specs=(vm, vm, vm),
        scratch_shapes=[pltpu.VMEM((t + 2 * h, D_POOL), F32), pltpu.SemaphoreType.DMA],
        name="pool_fwd", compiler_params=_params())(proj, pool_w, pool_scale)


def _pool_bwd(d_ms, mixed, pooled, pool_w, pool_scale):
    t = d_ms.shape[0]
    c, h = POOL_CHUNK, POOL_HALO
    n_chunks = t // c

    def body(dms_ref, mixed_ref, pooled_ref, pw_ref, sc_ref, dp_ref, dsc_ref, dpw_ref, pad_ref):
        pad_ref[pl.ds(0, h), :] = jnp.zeros((h, D_POOL), F32)
        pad_ref[pl.ds(t + h, h), :] = jnp.zeros((h, D_POOL), F32)
        for g, width in enumerate(POOL_WINDOWS):
            left = width // 2
            right = width - 1 - left
            cols = slice(g * POOL_GROUP, (g + 1) * POOL_GROUP)
            wmat = pw_ref[g].astype(BF16)
            scale = sc_ref[:, cols]

            def first(ci, carry, left=left, right=right, cols=cols, wmat=wmat, scale=scale):
                dsc, dpw = carry
                base = pl.multiple_of(ci * c, c)
                dms = dms_ref[pl.ds(base, c), cols].astype(F32)
                dsc = dsc + jnp.sum(dms * mixed_ref[pl.ds(base, c), cols].astype(F32), axis=0, keepdims=True)
                dmix = (dms * scale).astype(BF16)
                dpw = dpw + _dot(pooled_ref[pl.ds(base, c), cols], dmix, TN)
                dpooled = _dot(dmix, wmat, NT)
                cnt = _pool_count(base, left, right, t, (c, POOL_GROUP))
                pad_ref[pl.ds(base + h, c), cols] = dpooled / cnt
                return dsc, dpw

            dsc, dpw = lax.fori_loop(0, n_chunks, first,
                                     (jnp.zeros((1, POOL_GROUP), F32), jnp.zeros((POOL_GROUP, POOL_GROUP), F32)))
            dsc_ref[:, cols] = dsc
            dpw_ref[g] = dpw

            def second(ci, carry, left=left, right=right, width=width, cols=cols):
                base = pl.multiple_of(ci * c, c)
                v = pad_ref[pl.ds(base, c + 2 * h), cols]
                win = _window_sum(v, width, right)[h:h + c]
                cnt = _pool_count(base, left, right, t, (c, POOL_GROUP))
                dp_ref[pl.ds(base, c), cols] = (win - v[h:h + c] * cnt).astype(BF16)
                return carry

            lax.fori_loop(0, n_chunks, second, 0)

    vm = BS(memory_space=pltpu.VMEM)
    return pl.pallas_call(
        body, out_shape=(SDS((t, D_POOL), BF16), SDS((1, D_POOL), F32), SDS((4, POOL_GROUP, POOL_GROUP), F32)),
        in_specs=[vm] * 5, out_specs=(vm, vm, vm),
        scratch_shapes=[pltpu.VMEM((t + 2 * h, D_POOL), F32)],
        name="pool_bwd", compiler_params=_params())(d_ms, mixed, pooled, pool_w, pool_scale)


SSM_ROWS = 2 * SSM_GROUPS * SSM_GROUP
SSM_HALF = SSM_GROUPS * SSM_GROUP


def _ssm_zoh(a_r, a_i, ldt):
    dt = jnp.exp(ldt)
    mag = jnp.exp(dt * a_r)
    ang = dt * a_i
    cs, sn = jnp.cos(ang), jnp.sin(ang)
    abr, abi = mag * cs, mag * sn
    den = a_r * a_r + a_i * a_i
    nr = abr - 1.0
    qr = (nr * a_r + abi * a_i) / den
    qi = (abi * a_r - nr * a_i) / den
    return dt, mag, cs, sn, abr, abi, den, nr, qr, qi


def _ssm_group_mask():
    row = lax.broadcasted_iota(jnp.int32, (SSM_HALF, SSM_CH), 0)
    col = lax.broadcasted_iota(jnp.int32, (SSM_HALF, SSM_CH), 1)
    return (row // SSM_GROUP) == (col // SSM_STATE)


def _ssm_prep(a_r, a_i, ldt, b_r, b_i, c_r, c_i, after=()):
    after_ops, after_specs = _after_operands(after)

    def body(ar_ref, ai_ref, ldt_ref, br_ref, bi_ref, cr_ref, ci_ref, *rest):
        abr_ref, abi_ref, win_ref, wint_ref, woutt_ref, wout_ref = rest[len(after_ops):]
        *_, abr, abi, _, _, qr, qi = _ssm_zoh(ar_ref[...], ai_ref[...], ldt_ref[...])
        abr_ref[...] = abr
        abi_ref[...] = abi
        b_r, b_i = br_ref[...], bi_ref[...]
        bbr = qr * b_r - qi * b_i
        bbi = qr * b_i + qi * b_r
        mask = _ssm_group_mask()
        state = lax.broadcasted_iota(jnp.int32, (SSM_STATE, SSM_CH), 0)
        col = lax.broadcasted_iota(jnp.int32, (SSM_STATE, SSM_CH), 1)
        every_group = (col % SSM_STATE == state).astype(BF16)

        def spread(x):
            return jnp.where(mask, _dot(x, every_group), 0.0)

        for d in range(2):
            rows = slice(d * SSM_HALF, (d + 1) * SSM_HALF)
            for half, x_in, x_out in ((0, bbr[rows], cr_ref[rows, :]), (1, bbi[rows], -ci_ref[rows, :])):
                cols = slice(half * SSM_CH, (half + 1) * SSM_CH)
                m_in, m_out = spread(x_in), spread(x_out)
                win_ref[d, :, cols] = m_in.astype(BF16)
                wint_ref[d, cols, :] = m_in.T.astype(BF16)
                woutt_ref[d, :, cols] = m_out.astype(BF16)
                wout_ref[d, cols, :] = m_out.T.astype(BF16)

    vm = BS(memory_space=pltpu.VMEM)
    vec = SDS((SSM_ROWS, SSM_STATE), F32)
    wide = SDS((2, SSM_HALF, 2 * SSM_CH), BF16)
    tall = SDS((2, 2 * SSM_CH, SSM_HALF), BF16)
    return pl.pallas_call(body, out_shape=(vec, vec, wide, tall, wide, tall), in_specs=[vm] * 7 + after_specs,
                          out_specs=(vm,) * 6, name="ssm_prep",
                          compiler_params=_params())(a_r, a_i, ldt, b_r, b_i, c_r, c_i, *after_ops)


def _ssm_prep_bwd(a_r, a_i, ldt, b_r, b_i, g_abr, g_abi, d_win, d_woutt):
    def body(ar_ref, ai_ref, ldt_ref, br_ref, bi_ref, gabr_ref, gabi_ref, dwin_ref, dwoutt_ref,
             dar_ref, dai_ref, dldt_ref, dbr_ref, dbi_ref, dcr_ref, dci_ref):
        a_r, a_i = ar_ref[...], ai_ref[...]
        dt, mag, cs, sn, abr, abi, den, nr, qr, qi = _ssm_zoh(a_r, a_i, ldt_ref[...])
        mask = _ssm_group_mask()
        col = lax.broadcasted_iota(jnp.int32, (SSM_CH, SSM_STATE), 0)
        state = lax.broadcasted_iota(jnp.int32, (SSM_CH, SSM_STATE), 1)
        own_state = (col % SSM_STATE == state).astype(BF16)

        def pick(dense):
            m = jnp.where(mask, dense, 0.0)
            hi = m.astype(BF16)
            lo = m - hi.astype(F32)
            return _dot(hi, own_state) + _dot(lo, own_state)

        def picked(ref, half):
            cols = slice(half * SSM_CH, (half + 1) * SSM_CH)
            return jnp.concatenate([pick(ref[d, :, cols]) for d in range(2)], axis=0)

        g_r, g_i = picked(dwin_ref, 0), picked(dwin_ref, 1)
        dcr_ref[...] = picked(dwoutt_ref, 0)
        dci_ref[...] = -picked(dwoutt_ref, 1)
        b_r, b_i = br_ref[...], bi_ref[...]
        dbr_ref[...] = g_r * qr + g_i * qi
        dbi_ref[...] = g_i * qr - g_r * qi
        gqr = g_r * b_r + g_i * b_i
        gqi = g_i * b_r - g_r * b_i
        g_nr_num = gqr / den
        g_ni_num = gqi / den
        g_den = -(gqr * qr + gqi * qi) / den
        g_nr = g_nr_num * a_r - g_ni_num * a_i
        g_abi = g_nr_num * a_i + g_ni_num * a_r
        d_ar = g_nr_num * nr + g_ni_num * abi + 2.0 * a_r * g_den
        d_ai = g_nr_num * abi - g_ni_num * nr + 2.0 * a_i * g_den
        g_abr = gabr_ref[...] + g_nr
        g_abi = gabi_ref[...] + g_abi
        g_mag = g_abr * cs + g_abi * sn
        g_ang = mag * (g_abi * cs - g_abr * sn)
        g_e = g_mag * mag
        d_ar = d_ar + g_e * dt
        d_ai = d_ai + g_ang * dt
        g_dt = g_e * a_r + g_ang * a_i
        dar_ref[...] = d_ar
        dai_ref[...] = d_ai
        dldt_ref[...] = g_dt * dt

    vm = BS(memory_space=pltpu.VMEM)
    vec = SDS((SSM_ROWS, SSM_STATE), F32)
    return pl.pallas_call(body, out_shape=(vec,) * 7, in_specs=[vm] * 9, out_specs=(vm,) * 7, name="ssm_prep_bwd",
                          compiler_params=_params())(a_r, a_i, ldt, b_r, b_i, g_abr, g_abi, d_win, d_woutt)


SCAN_ROWS = 512
SCAN_SUB = 128


def _ssm_scan(name, inp, w1, a_r, a_i, w2, reverse):
    t = inp.shape[0]
    rows = min(SCAN_ROWS, t)
    n = t // rows
    n_sub = rows // SCAN_SUB
    ch = SSM_CH
    at = (lambda i: (n - 1 - i, 0)) if reverse else (lambda i: (i, 0))

    def body(in_ref, w1_ref, ar_ref, ai_ref, w2_ref, sb_ref, out_ref, cr_ref, ci_ref, k_ref, st_ref):
        i = pl.program_id(0)

        @pl.when(i == 0)
        def _():
            ar8 = jnp.broadcast_to(ar_ref[...], (8, ch))
            ai8 = jnp.broadcast_to(ai_ref[...], (8, ch))
            row = lax.broadcasted_iota(jnp.int32, (8, ch), 0)
            rank = (7 - row) if reverse else row
            powers = [(ar8, ai8)]
            for _ in range(7):
                p_r, p_i = powers[-1]
                powers.append((p_r * ar8 - p_i * ai8, p_r * ai8 + p_i * ar8))
            zero = jnp.zeros((8, ch), F32)
            for slot, k in enumerate((1, 2, 4)):
                k_ref[2 * slot] = jnp.where(rank >= k, powers[k - 1][0], zero)
                k_ref[2 * slot + 1] = jnp.where(rank >= k, powers[k - 1][1], zero)
            carry_r, carry_i = zero, zero
            for j in range(8):
                carry_r = jnp.where(rank == j, powers[j][0], carry_r)
                carry_i = jnp.where(rank == j, powers[j][1], carry_i)
            k_ref[6] = carry_r
            k_ref[7] = carry_i
            cr_ref[...] = zero
            ci_ref[...] = zero

        def group(r0, carry):
            c_r, c_i = carry
            x_r = st_ref[pl.ds(r0, 8), 0:ch]
            x_i = st_ref[pl.ds(r0, 8), ch:2 * ch]
            for slot, k in enumerate((1, 2, 4)):
                shift = (8 - k) if reverse else k
                s_r = pltpu.roll(x_r, shift, 0)
                s_i = pltpu.roll(x_i, shift, 0)
                m_r, m_i = k_ref[2 * slot], k_ref[2 * slot + 1]
                x_r, x_i = x_r + m_r * s_r - m_i * s_i, x_i + m_r * s_i + m_i * s_r
            p_r, p_i = k_ref[6], k_ref[7]
            x_r, x_i = x_r + p_r * c_r - p_i * c_i, x_i + p_r * c_i + p_i * c_r
            st_ref[pl.ds(r0, 8), 0:ch] = x_r
            st_ref[pl.ds(r0, 8), ch:2 * ch] = x_i
            last = 0 if reverse else 7
            return (jnp.broadcast_to(x_r[last:last + 1, :], (8, ch)), jnp.broadcast_to(x_i[last:last + 1, :], (8, ch)))

        carry = (cr_ref[...], ci_ref[...])
        for sc in (range(n_sub - 1, -1, -1) if reverse else range(n_sub)):
            part = pl.ds(sc * SCAN_SUB, SCAN_SUB)
            st_ref[part, :] = _dot(in_ref[part, :], w1_ref[...])
            for gi in range(SCAN_SUB // 8):
                g = (SCAN_SUB // 8 - 1 - gi) if reverse else gi
                carry = group(sc * SCAN_SUB + g * 8, carry)
            states = st_ref[part, :].astype(BF16)
            sb_ref[part, :] = states
            out_ref[part, :] = _dot(states, w2_ref[...])
        cr_ref[...] = carry[0]
        ci_ref[...] = carry[1]

    return pl.pallas_call(
        body, out_shape=(SDS((t, 2 * ch), BF16), SDS((t, D_SSM), F32)), grid=(n,),
        in_specs=[BS((rows, D_SSM), at), BS((D_SSM, 2 * ch), lambda i: (0, 0)), BS((1, ch), lambda i: (0, 0)),
                  BS((1, ch), lambda i: (0, 0)), BS((2 * ch, D_SSM), lambda i: (0, 0))],
        out_specs=(BS((rows, 2 * ch), at), BS((rows, D_SSM), at)),
        scratch_shapes=[pltpu.VMEM((8, ch), F32), pltpu.VMEM((8, ch), F32), pltpu.VMEM((8, 8, ch), F32),
                        pltpu.VMEM((rows, 2 * ch), F32)],
        name=name, compiler_params=_params("arbitrary"))(inp, w1, a_r, a_i, w2)


DA_ROWS = 1024


def _ssm_param_grads(name, lam, states, u, dy, reverse, after=()):
    t = lam.shape[0]
    rows = min(DA_ROWS, t)
    n = t // rows
    halo_rows = 16
    nb = rows // halo_rows
    ch = SSM_CH
    if reverse:
        halo_at = lambda i: (jnp.minimum((i + 1) * nb, t // halo_rows - 1), 0)
    else:
        halo_at = lambda i: (jnp.maximum(i * nb - 1, 0), 0)

    after_ops, after_specs = _after_operands(after)

    def body(lam_ref, x_ref, halo_ref, u_ref, dy_ref, *rest):
        dr_ref, di_ref, dwin_ref, dwoutt_ref = rest[len(after_ops):]
        i = pl.program_id(0)

        @pl.when(i == 0)
        def _():
            dr_ref[...] = jnp.zeros_like(dr_ref)
            di_ref[...] = jnp.zeros_like(di_ref)
            dwin_ref[...] = jnp.zeros_like(dwin_ref)
            dwoutt_ref[...] = jnp.zeros_like(dwoutt_ref)

        dwin_ref[...] += _dot(u_ref[...], lam_ref[...], TN)
        dwoutt_ref[...] += _dot(dy_ref[...], x_ref[...], TN)
        row = lax.broadcasted_iota(jnp.int32, (rows, ch), 0)
        if reverse:
            edge, shift, h_row, live = rows - 1, rows - 1, 0, i < n - 1
        else:
            edge, shift, h_row, live = 0, 1, halo_rows - 1, i > 0

        def neighbour(lo):
            halo = halo_ref[:, lo:lo + ch].astype(F32)[h_row:h_row + 1]
            halo = jnp.where(live, halo, 0.0)
            x = x_ref[:, lo:lo + ch].astype(F32)
            return jnp.where(row == edge, jnp.broadcast_to(halo, (rows, ch)), pltpu.roll(x, shift, 0))

        xp_r, xp_i = neighbour(0), neighbour(ch)
        l_r, l_i = lam_ref[:, 0:ch].astype(F32), lam_ref[:, ch:2 * ch].astype(F32)
        dr_ref[...] += jnp.sum(l_r * xp_r + l_i * xp_i, axis=0, keepdims=True)
        di_ref[...] += jnp.sum(l_i * xp_r - l_r * xp_i, axis=0, keepdims=True)

    blk = BS((rows, 2 * ch), lambda i: (i, 0))
    thin = BS((rows, D_SSM), lambda i: (i, 0))
    vec = BS((1, ch), lambda i: (0, 0))
    mat = BS((D_SSM, 2 * ch), lambda i: (0, 0))
    return pl.pallas_call(
        body, out_shape=(SDS((1, ch), F32), SDS((1, ch), F32), SDS((D_SSM, 2 * ch), F32), SDS((D_SSM, 2 * ch), F32)),
        grid=(n,), in_specs=[blk, blk, BS((halo_rows, 2 * ch), halo_at), thin, thin] + after_specs,
        out_specs=(vec, vec, mat, mat),
        name=name, compiler_params=_params("arbitrary"))(lam, states, states, u, dy, *after_ops)


GELU_C = math.sqrt(2.0 / math.pi)
GELU_K = 0.044715


def _ssm_combine(proj, y_fwd, y_bwd, d_skip, tm, after=()):
    t = proj.shape[0]
    after_ops, after_specs = _after_operands(after)

    def body(s_ref, yf_ref, yb_ref, d_ref, *rest):
        yt_ref, g_ref = rest[len(after_ops):]
        y = s_ref[...] * d_ref[...] + yf_ref[...] + yb_ref[...]
        yt_ref[...] = y
        th = jnp.tanh(GELU_C * (y + GELU_K * y * y * y))
        g_ref[...] = (0.5 * y * (1.0 + th)).astype(BF16)

    blk = BS((tm, D_SSM), lambda i: (i, 0))
    return pl.pallas_call(
        body, out_shape=(SDS((t, D_SSM), F32), SDS((t, D_SSM), BF16)), grid=(t // tm,),
        in_specs=[BS((tm, D_SSM), lambda i: (i, D_POOL // D_SSM)), blk, blk, BS((1, D_SSM), lambda i: (0, 0))] + after_specs,
        out_specs=(blk, blk), name="ssm_combine",
        compiler_params=_params("parallel"))(proj, y_fwd, y_bwd, d_skip, *after_ops)


def _ssm_ds(proj, d_yt, du_fwd, du_bwd, d_skip, tm):
    t = proj.shape[0]

    def body(s_ref, dy_ref, duf_ref, dub_ref, d_ref, ds_ref, dd_ref):
        i = pl.program_id(0)
        dy = dy_ref[...]
        ds_ref[...] = (dy * d_ref[...] + duf_ref[...] + dub_ref[...]).astype(BF16)

        @pl.when(i == 0)
        def _():
            dd_ref[...] = jnp.zeros_like(dd_ref)

        dd_ref[...] += jnp.sum(dy * s_ref[...], axis=0, keepdims=True)

    blk = BS((tm, D_SSM), lambda i: (i, 0))
    vec = BS((1, D_SSM), lambda i: (0, 0))
    return pl.pallas_call(
        body, out_shape=(SDS((t, D_SSM), BF16), SDS((1, D_SSM), F32)), grid=(t // tm,),
        in_specs=[BS((tm, D_SSM), lambda i: (i, D_POOL // D_SSM)), blk, blk, blk, vec],
        out_specs=(blk, vec), name="ssm_ds", compiler_params=_params("arbitrary"))(proj, d_yt, du_fwd, du_bwd, d_skip)


GP_BLOCK = (D_POOL + D_SSM) // 256
GS_BLOCK = GP_BLOCK + D_MODEL // 256


def _merge_specs(tm):
    return [BS((tm, D_POOL), lambda s, i: (i, 0)), BS((tm, D_SSM), lambda s, i: (i, 0)),
            BS((None, D_POOL, 256), lambda s, i: (s, 0, 0)), BS((None, D_SSM, 256), lambda s, i: (s, 2, 0)),
            BS((None, D_SSM, 256), lambda s, i: (s, 3, 0)),
            BS((tm, 256), lambda s, i: (i, GP_BLOCK + s)), BS((tm, 256), lambda s, i: (i, GS_BLOCK + s))]


def _mixer_merge(ms, yssm, w_e, proj, tm):
    t = ms.shape[0]

    def body(ms_ref, y_ref, wpp_ref, wgv_ref, wgg_ref, gp_ref, gs_ref, o_ref):
        zp = _dot(ms_ref[...], wpp_ref[...])
        yv = y_ref[...]
        zv = _dot(yv, wgv_ref[...])
        zg = _dot(yv, wgg_ref[...])
        o_ref[...] = (_sigmoid(gp_ref[...]) * zp + _sigmoid(gs_ref[...]) * zv * _sigmoid(zg)).astype(BF16)

    col = BS((tm, 256), lambda s, i: (i, s))
    return pl.pallas_call(
        body, out_shape=SDS((t, D_MODEL), BF16), grid=(N_SHARD, t // tm), in_specs=_merge_specs(tm), out_specs=col,
        name="mixer_merge", compiler_params=_params("parallel", "parallel"))(ms, yssm, w_e, w_e, w_e, proj, proj)


def _mixer_merge_bwd(ms, yssm, w_e, proj, dmerged, tm):
    t = ms.shape[0]

    def body(ms_ref, y_ref, wpp_ref, wgv_ref, wgg_ref, gp_ref, gs_ref, dm_ref,
             dgp_ref, dgs_ref, dzp_ref, dzv_ref, dzg_ref):
        zp = _dot(ms_ref[...], wpp_ref[...])
        yv = y_ref[...]
        zv = _dot(yv, wgv_ref[...])
        zg = _dot(yv, wgg_ref[...])
        dm = dm_ref[...].astype(F32)
        sp, ss, sg = _sigmoid(gp_ref[...]), _sigmoid(gs_ref[...]), _sigmoid(zg)
        dgp_ref[...] = (dm * zp * sp * (1.0 - sp)).astype(BF16)
        dgs_ref[...] = (dm * zv * sg * ss * (1.0 - ss)).astype(BF16)
        dzp_ref[...] = (dm * sp).astype(BF16)
        dz = dm * ss
        dzv_ref[...] = (dz * sg).astype(BF16)
        dzg_ref[...] = (dz * zv * sg * (1.0 - sg)).astype(BF16)

    col = BS((tm, 256), lambda s, i: (i, s))
    shape = SDS((t, D_MODEL), BF16)
    return pl.pallas_call(
        body, out_shape=(shape,) * 5, grid=(N_SHARD, t // tm), in_specs=_merge_specs(tm) + [col],
        out_specs=(col,) * 5, name="mixer_merge_bwd",
        compiler_params=_params("parallel", "parallel"))(ms, yssm, w_e, w_e, w_e, proj, proj, dmerged)


def _mixer_dw(ms, yssm, dzp, dzv, dzg, tm):
    t = ms.shape[0]
    n_t = t // tm

    def body(ms_ref, y_ref, dzp_ref, dzv_ref, dzg_ref, o_ref, acc):
        i = pl.program_id(1)

        @pl.when(i == 0)
        def _():
            acc[...] = jnp.zeros_like(acc)

        yv = y_ref[...]
        acc[0:D_POOL, :] += _dot(ms_ref[...], dzp_ref[...], TN)
        acc[D_POOL:D_POOL + D_SSM, :] += _dot(yv, dzv_ref[...], TN)
        acc[D_POOL + D_SSM:, :] += _dot(yv, dzg_ref[...], TN)

        @pl.when(i == n_t - 1)
        def _():
            o_ref[...] = acc[...].astype(BF16)

    col = BS((tm, 256), lambda s, i: (i, s))
    return pl.pallas_call(
        body, out_shape=SDS((N_SHARD, 1024, 256), BF16), grid=(N_SHARD, n_t),
        in_specs=[BS((tm, D_POOL), lambda s, i: (i, 0)), BS((tm, D_SSM), lambda s, i: (i, 0)), col, col, col],
        out_specs=BS((None, 1024, 256), lambda s, i: (s, 0, 0)), scratch_shapes=[pltpu.VMEM((1024, 256), F32)],
        name="mixer_dw", compiler_params=_params("parallel", "arbitrary"))(ms, yssm, dzp, dzv, dzg)


def _mixer_dx(dzp, dzv, dzg, w_e, y_total, tm):
    t = dzp.shape[0]

    def body(dzp_ref, dzv_ref, dzg_ref, wpp_ref, wgv_ref, wgg_ref, yt_ref, dms_ref, dy_ref, acc_ms, acc_y):
        s = pl.program_id(1)

        @pl.when(s == 0)
        def _():
            acc_ms[...] = jnp.zeros_like(acc_ms)
            acc_y[...] = jnp.zeros_like(acc_y)

        acc_ms[...] += _dot(dzp_ref[...], wpp_ref[...], NT)
        acc_y[...] += _dot(dzv_ref[...], wgv_ref[...], NT) + _dot(dzg_ref[...], wgg_ref[...], NT)

        @pl.when(s == N_SHARD - 1)
        def _():
            dms_ref[...] = acc_ms[...].astype(BF16)
            y = yt_ref[...]
            inner = GELU_C * (y + GELU_K * y * y * y)
            th = jnp.tanh(inner)
            dgelu = 0.5 * (1.0 + th) + 0.5 * y * (1.0 - th * th) * GELU_C * (1.0 + 3.0 * GELU_K * y * y)
            dy_ref[...] = acc_y[...] * dgelu

    col = BS((tm, 256), lambda i, s: (i, s))
    return pl.pallas_call(
        body, out_shape=(SDS((t, D_POOL), BF16), SDS((t, D_SSM), F32)), grid=(t // tm, N_SHARD),
        in_specs=[col, col, col, BS((None, D_POOL, 256), lambda i, s: (s, 0, 0)),
                  BS((None, D_SSM, 256), lambda i, s: (s, 2, 0)), BS((None, D_SSM, 256), lambda i, s: (s, 3, 0)),
                  BS((tm, D_SSM), lambda i, s: (i, 0))],
        out_specs=(BS((tm, D_POOL), lambda i, s: (i, 0)), BS((tm, D_SSM), lambda i, s: (i, 0))),
        scratch_shapes=[pltpu.VMEM((tm, D_POOL), F32), pltpu.VMEM((tm, D_SSM), F32)],
        name="mixer_dx", compiler_params=_params("parallel", "arbitrary"))(dzp, dzv, dzg, w_e, w_e, w_e, y_total)


def _attn_probs(q_h, k_h):
    s = _dot(q_h, k_h, NT) * (1.0 / math.sqrt(HEAD_DIM))
    e = jnp.exp(s - jnp.max(s, axis=-1, keepdims=True))
    return e / jnp.sum(e, axis=-1, keepdims=True)


def _attn_fwd(q, kv, tm):
    t = q.shape[0]
    m = kv.shape[0]

    def body(q_ref, kv_ref, o_ref):
        for hd in range(N_HEADS):
            lo = hd * HEAD_DIM
            p = _attn_probs(q_ref[:, lo:lo + HEAD_DIM], kv_ref[:, lo:lo + HEAD_DIM])
            o_ref[:, lo:lo + HEAD_DIM] = _dot(p, kv_ref[:, D_MODEL + lo:D_MODEL + lo + HEAD_DIM]).astype(BF16)

    return pl.pallas_call(
        body, out_shape=SDS((t, D_MODEL), BF16), grid=(t // tm,),
        in_specs=[BS((tm, D_MODEL), lambda i: (i, 0)), BS((m, 2 * D_MODEL), lambda i: (0, 0))],
        out_specs=BS((tm, D_MODEL), lambda i: (i, 0)), name="attn_fwd", compiler_params=_params("parallel"))(q, kv)


def _attn_bwd(q, kv, d_o, tm):
    t = q.shape[0]
    m = kv.shape[0]

    def body(q_ref, kv_ref, do_ref, dq_ref, dkv_ref):
        i = pl.program_id(0)

        @pl.when(i == 0)
        def _():
            dkv_ref[...] = jnp.zeros_like(dkv_ref)

        for hd in range(N_HEADS):
            lo = hd * HEAD_DIM
            q_h = q_ref[:, lo:lo + HEAD_DIM]
            k_h = kv_ref[:, lo:lo + HEAD_DIM]
            v_h = kv_ref[:, D_MODEL + lo:D_MODEL + lo + HEAD_DIM]
            do_h = do_ref[:, lo:lo + HEAD_DIM]
            p = _attn_probs(q_h, k_h)
            dkv_ref[:, D_MODEL + lo:D_MODEL + lo + HEAD_DIM] += _dot(p, do_h, TN)
            dp = _dot(do_h, v_h, NT)
            ds = p * (dp - jnp.sum(dp * p, axis=-1, keepdims=True)) * (1.0 / math.sqrt(HEAD_DIM))
            dq_ref[:, lo:lo + HEAD_DIM] = _dot(ds, k_h).astype(BF16)
            dkv_ref[:, lo:lo + HEAD_DIM] += _dot(ds, q_h, TN)

    row = BS((tm, D_MODEL), lambda i: (i, 0))
    full = BS((m, 2 * D_MODEL), lambda i: (0, 0))
    return pl.pallas_call(
        body, out_shape=(SDS((t, D_MODEL), BF16), SDS((m, 2 * D_MODEL), F32)), grid=(t // tm,),
        in_specs=[row, full, row], out_specs=(row, full), name="attn_bwd",
        compiler_params=_params("arbitrary"))(q, kv, d_o)


TRANSPOSED = ("ffn1_w_gate", "ffn1_w_up", "ffn2_w_gate", "ffn2_w_up", "w_in")
GATHER_PHASES = {"f1a": (("ffn1_w_gate", "ffn1_w_up"),),
                 "f1b": (("ffn1_w_down",),),
                 "win": (("w_in",),),
                 "mix": (("w_mix_out", "w_q", "w_xo"), ("w_kv",), ("w_pool_proj", "w_glu_val", "w_glu_gate")),
                 "f2": (("ffn2_w_gate", "ffn2_w_up", "ffn2_w_down"),)}
REDUCE_GROUPS = (("ffn2_w_gate", "ffn2_w_up", "ffn2_w_down"), ("w_xo",), ("w_q",), ("w_kv",), ("w_mix_out",),
                 ("w_pool_proj", "w_glu_val", "w_glu_gate"), ("w_in",), ("ffn1_w_gate", "ffn1_w_up", "ffn1_w_down"))
SMALL = ("ffn1_norm", "mix_norm", "pool_w", "pool_scale", "ssm_a_re", "ssm_a_im", "ssm_log_dt", "ssm_b_re",
         "ssm_b_im", "ssm_c_re", "ssm_c_im", "ssm_d", "xattn_norm", "mem_norm", "ffn2_norm", "final_norm")
WEIGHTS = ("ffn1_norm", "ffn1_w_gate", "ffn1_w_up", "ffn1_w_down", "mix_norm", "w_in", "pool_w", "pool_scale",
           "w_pool_proj", "ssm_a_re", "ssm_a_im", "ssm_log_dt", "ssm_b_re", "ssm_b_im", "ssm_c_re", "ssm_c_im",
           "ssm_d", "w_glu_val", "w_glu_gate", "w_mix_out", "xattn_norm", "mem_norm", "w_q", "w_kv", "w_xo",
           "ffn2_norm", "ffn2_w_gate", "ffn2_w_up", "ffn2_w_down", "final_norm")


def _small_view(a, n):
    return jnp.swapaxes(a, 3, 4) if n in ("ssm_b_re", "ssm_b_im") else a


def _device_step(x, mem, target, wts, sp, reducer=None):
    t = x.shape[0]
    tm = min(TM, t)
    g = {}

    first_gather = wts.start("f1a")
    u1 = _rmsnorm("norm_ffn1", x, sp["ffn1_norm"], tm, after=first_gather)

    def per_channel(a):
        a = a.reshape(2 * SSM_GROUPS, 1, -1)
        return jnp.broadcast_to(a, (2 * SSM_GROUPS, SSM_GROUP, a.shape[-1])).reshape(SSM_ROWS, a.shape[-1])

    ssm_a = per_channel(sp["ssm_a_re"]), per_channel(sp["ssm_a_im"]), per_channel(sp["ssm_log_dt"])
    ssm_b = sp["ssm_b_re"].reshape(SSM_ROWS, SSM_STATE), sp["ssm_b_im"].reshape(SSM_ROWS, SSM_STATE)
    abr, abi, w_in_s, w_in_s_t, w_out_s_t, w_out_s = _ssm_prep(
        *ssm_a, *ssm_b, sp["ssm_c_re"].reshape(SSM_ROWS, SSM_STATE), sp["ssm_c_im"].reshape(SSM_ROWS, SSM_STATE),
        after=first_gather)
    first_rows = (2, SSM_GROUPS, SSM_GROUP, SSM_STATE)
    a_r = abr.reshape(first_rows)[:, :, 0].reshape(2, 1, SSM_CH)
    a_i = abi.reshape(first_rows)[:, :, 0].reshape(2, 1, SSM_CH)
    mem_n = _rmsnorm("norm_mem", mem, sp["mem_norm"], mem.shape[0], after=first_gather)

    (w_gu,) = wts.finish("f1a", [u1, w_in_s, w_in_s_t, w_out_s, w_out_s_t, a_r, a_i, mem_n])
    w_f1 = {"gate": (w_gu, FFN_GATE), "up": (w_gu, FFN_UP)}
    down_gather = wts.start("f1b", [w_gu])
    g1, up1, a1 = _ffn_up("ffn1_up", u1, w_f1, tm, after=wts.start("win", down_gather))
    (w_dn,) = wts.finish("f1b", [a1])
    w_f1["down"] = (w_dn, 0)
    h1, u2 = _ffn_down("ffn1_down", a1, w_f1, x, tm, next_gain=sp["mix_norm"])

    (w_in_g,) = wts.finish("win", [u2])
    w_in_t = w_in_g.reshape(D_FF, D_MODEL)
    proj = _mm("mix_in", [(u2, BS((tm, D_MODEL), lambda j, i: (i, 0)), w_in_t, BS((D_FF // 2, D_MODEL), lambda j, i: (j, 0)), NT)],
               grid=(2, t // tm), out_shape=SDS((t, D_FF), F32), out_spec=BS((tm, D_FF // 2), lambda j, i: (i, j)),
               after=wts.start("f2", wts.start("mix", [w_in_g])))
    pooled, mixed, ms = _pool_fwd(proj, sp["pool_w"][0], sp["pool_scale"])

    s_in = proj[:, D_POOL:D_POOL + D_SSM].astype(BF16)
    states, y_dirs = [], []
    for dr in range(2):
        st, yd = _ssm_scan(f"ssm_scan_fwd{dr}", s_in, w_in_s[dr], a_r[dr], a_i[dr], w_out_s[dr], reverse=(dr == 1))
        states.append(st)
        y_dirs.append(yd)
    w_sq, w_kv, w_e = wts.finish("mix", y_dirs)
    w_mo, w_q, w_xo = (w_sq[:, 256 * k:256 * (k + 1)].reshape(D_MODEL, D_MODEL) for k in range(3))
    w_d = w_kv[:, None]
    y_total, yssm = _ssm_combine(proj, y_dirs[0], y_dirs[1], sp["ssm_d"], tm)

    merged = _mixer_merge(ms, yssm, w_e, proj, tm)
    h2, u3 = _mm_resid_norm("mix_out", merged, w_mo, h1, sp["xattn_norm"], tm)

    q = _plain_mm("attn_q", u3, w_q, NN, BF16, tm)
    n_mem = mem.shape[0]
    kv = _mm("attn_kv", [(mem_n, BS((n_mem, D_MODEL), lambda s: (0, 0)), w_d, BS((None, None, D_MODEL, 512), lambda s: (s, 0, 0, 0)), NN)],
             grid=(N_SHARD,), out_shape=SDS((n_mem, 2 * D_MODEL), BF16), out_spec=BS((n_mem, 512), lambda s: (0, s)))
    o = _attn_fwd(q, kv, tm)
    h3, u4 = _mm_resid_norm("attn_out", o, w_xo, h2, sp["ffn2_norm"], tm)

    (w_2,) = wts.finish("f2", [u4])
    w_f2 = {"gate": (w_2, FFN_GATE), "up": (w_2, FFN_UP), "down": (w_2, FFN_DOWN)}
    g2, up2, a2 = _ffn_up("ffn2_up", u4, w_f2, tm)
    loss, dh4, dh4_b, g["final_norm"] = _ffn_down("ffn2_down", a2, w_f2, h3, tm,
                                                  head=(sp["final_norm"].reshape(1, D_MODEL), target))

    dg2, dup2 = _ffn_bwd_act("ffn2_bwd_act", dh4_b, w_f2, g2, up2, tm)
    dw_f2 = _ffn_dw("ffn2_dw", u4, dg2, dup2, a2, dh4_b, tm)
    dh3, dh3_b, g["ffn2_norm"] = _ffn_dx("ffn2_dx", dg2, dup2, w_f2, h3, sp["ffn2_norm"], dh4, tm)

    d_o = _plain_mm("attn_out_dx", dh3_b, w_xo, NT, BF16, tm)
    dw_xo = _dw_mm("attn_out_dw", o, dh3_b, tm)
    dq, dkv = _attn_bwd(q, kv, d_o, tm)
    dw_q = _dw_mm("attn_q_dw", u3, dq, tm)
    dh2, dh2_b, g["xattn_norm"] = _mm_norm_bwd("attn_q_dx", dq, w_q, NT, h2, sp["xattn_norm"], dh3, tm)
    dw_kv = _mm("attn_kv_dw", [(mem_n, BS((n_mem, D_MODEL), lambda s: (0, 0)), dkv, BS((n_mem, 512), lambda s: (0, s)), TN)],
                grid=(N_SHARD,), out_shape=SDS((N_SHARD, D_MODEL, 512), BF16), out_spec=BS((None, D_MODEL, 512), lambda s: (s, 0, 0)))
    dmem_n = _mm("attn_kv_dx", [(dkv, BS((n_mem, 512), lambda s: (0, s)), w_d, BS((None, None, D_MODEL, 512), lambda s: (s, 0, 0, 0)), NT)],
                 grid=(N_SHARD,), red_axis=0, out_shape=SDS((n_mem, D_MODEL), F32), out_spec=BS((n_mem, D_MODEL), lambda s: (0, 0)))
    _, _, g["mem_norm"] = _rmsnorm_bwd("norm_mem_bwd", mem, sp["mem_norm"], dmem_n, None, n_mem)

    square = (N_SHARD, D_MODEL // N_SHARD, D_MODEL)
    early = [dw_f2.reshape(N_SHARD, 3 * FF_SH, D_MODEL), dw_xo.reshape(square), dw_q.reshape(square), dw_kv]
    swapping = reducer.swap_start("a1", early) if reducer is not None else []
    dmerged = _plain_mm("mix_out_dx", dh2_b, w_mo, NT, BF16, tm, after=swapping)
    dw_mo = _dw_mm("mix_out_dw", merged, dh2_b, tm)
    d_gp, d_gs, dzp, dzv, dzg = _mixer_merge_bwd(ms, yssm, w_e, proj, dmerged, tm)
    dw_e = _mixer_dw(ms, yssm, dzp, dzv, dzg, tm)
    d_ms, d_yt = _mixer_dx(dzp, dzv, dzg, w_e, y_total, tm)
    dp, d_scale, d_pw = _pool_bwd(d_ms, mixed, pooled, sp["pool_w"][0], sp["pool_scale"])
    g["pool_scale"] = d_scale
    g["pool_w"] = d_pw[None]

    d_yt_b = d_yt.astype(BF16)
    du_dirs, lams = [], []
    for dr in range(2):
        lam, du = _ssm_scan(f"ssm_scan_bwd{dr}", d_yt_b, w_out_s_t[dr], a_r[dr], -a_i[dr], w_in_s_t[dr], reverse=(dr == 0))
        du_dirs.append(du)
        lams.append(lam)
    ds, g["ssm_d"] = _ssm_ds(proj, d_yt, du_dirs[0], du_dirs[1], sp["ssm_d"], tm)

    d_proj = jnp.concatenate([dp, ds, d_gp, d_gs], axis=1)
    tw = min(2 * tm, t)
    dw_in_t = _mm("mix_in_dw", [(d_proj, BS((tw, D_FF // 2), lambda j, i: (i, j)), u2, BS((tw, D_MODEL), lambda j, i: (i, 0)), TN)],
                  grid=(2, t // tw), red_axis=1, out_shape=SDS((D_FF, D_MODEL), BF16), out_spec=BS((D_FF // 2, D_MODEL), lambda j, i: (j, 0)))
    dh1, dh1_b, g["mix_norm"] = _mm_norm_bwd("mix_in_dx", d_proj, w_in_t, NN, h1, sp["mix_norm"], dh2, tm)

    early += [dw_mo.reshape(square), dw_e, dw_in_t.reshape(N_SHARD, FF_SH, D_MODEL)]
    g["final_norm"] = g["final_norm"].reshape(D_MODEL)

    travelling = reducer.start("a", early[4:], swapped=["a1"], after=list(g.values())) if reducer is not None else []
    d_abr, d_abi, d_cm, d_bm = [], [], [], []
    for dr in range(2):
        da_r, da_i, d_win, d_woutt = _ssm_param_grads(f"ssm_param_grads{dr}", lams[dr], states[dr], s_in, d_yt_b,
                                                      reverse=(dr == 1), after=travelling)
        d_abr.append(da_r)
        d_abi.append(da_i)
        d_bm.append(d_win)
        d_cm.append(d_woutt)

    def first_channel(da):
        da = jnp.stack(da).reshape(2, SSM_GROUPS, 1, SSM_STATE)
        return jnp.pad(da, ((0, 0), (0, 0), (0, SSM_GROUP - 1), (0, 0))).reshape(SSM_ROWS, SSM_STATE)

    d_ar, d_ai, d_ldt, d_br, d_bi, d_cr, d_ci = _ssm_prep_bwd(
        *ssm_a, *ssm_b, first_channel(d_abr), first_channel(d_abi), jnp.stack(d_bm), jnp.stack(d_cm))
    per_group = (2 * SSM_GROUPS, SSM_GROUP * SSM_STATE)
    g["ssm_a_re"] = d_ar.reshape(2 * SSM_GROUPS, SSM_GROUP, SSM_STATE).sum(axis=1).reshape(sp["ssm_a_re"].shape)
    g["ssm_a_im"] = d_ai.reshape(2 * SSM_GROUPS, SSM_GROUP, SSM_STATE).sum(axis=1).reshape(sp["ssm_a_im"].shape)
    g["ssm_log_dt"] = d_ldt.reshape(per_group).sum(axis=1).reshape(sp["ssm_log_dt"].shape)
    g["ssm_b_re"] = d_br.reshape(sp["ssm_b_re"].shape)
    g["ssm_b_im"] = d_bi.reshape(sp["ssm_b_im"].shape)
    g["ssm_c_re"] = d_cr.reshape(sp["ssm_c_re"].shape)
    g["ssm_c_im"] = d_ci.reshape(sp["ssm_c_im"].shape)
    if reducer is not None:
        travelling = travelling + [d_ar, d_br, d_cr]
    dg1, dup1 = _ffn_bwd_act("ffn1_bwd_act", dh1_b, w_f1, g1, up1, tm, after=travelling)
    dw_f1 = _ffn_dw("ffn1_dw", u1, dg1, dup1, a1, dh1_b, tm).reshape(N_SHARD, 3 * FF_SH, D_MODEL)
    if reducer is not None:
        travelling = reducer.start("b", [dw_f1], after=reducer.finish("a", [dw_f1]))
        travelling = travelling + reducer.join_start("a", after=travelling)
    grad_x, _, g["ffn1_norm"] = _ffn_dx("ffn1_dx", dg1, dup1, w_f1, x, sp["ffn1_norm"], dh1, tm, after=travelling)
    if reducer is not None:
        reducer.finish("b", [grad_x])
        reducer.join_finish("a", [grad_x])
    return loss, grad_x, early + [dw_f1], g


def _mesh_place():
    x, y, c = lax.axis_index("x"), lax.axis_index("y"), lax.axis_index("c")
    chips = [(1 - x, y), (x, 1 - y), (1 - x, 1 - y)]
    return x, y, c, chips


def _remote(src, dst, send_sems, recv_sems, k, to):
    return pltpu.make_async_remote_copy(src_ref=src, dst_ref=dst, send_sem=send_sems.at[k], recv_sem=recv_sems.at[k],
                                        device_id=to, device_id_type=MESH)


def _sibling_swap_halves(tag, grads, after=()):
    n = len(grads)
    after_ops, after_specs = _after_operands(after)

    def body(*refs):
        ins, outs = refs[:n], refs[n + len(after_ops):2 * n + len(after_ops)]
        send_sems, recv_sems = refs[2 * n + len(after_ops):]
        x, y, c, _ = _mesh_place()
        sibling = (x, y, 1 - c)
        copies = []
        for k in range(n):
            half = grads[k].shape[1] // 2
            theirs = pl.ds(pl.multiple_of((1 - c) * half, 16), half)
            cp = _remote(ins[k].at[:, theirs, :], outs[k], send_sems, recv_sems, k, sibling)
            cp.start()
            copies.append(cp)
        for cp in copies:
            cp.wait_recv()
        for cp in copies:
            cp.wait_send()

    hbm = BS(memory_space=pl.ANY)
    return pl.pallas_call(
        body, out_shape=tuple(SDS((g.shape[0], g.shape[1] // 2, g.shape[2]), g.dtype) for g in grads),
        in_specs=[hbm] * n + after_specs, out_specs=(hbm,) * n,
        scratch_shapes=[pltpu.SemaphoreType.DMA((n,)), pltpu.SemaphoreType.DMA((n,))],
        name="reduce_sibling_send_" + tag, compiler_params=_params())(*grads, *after_ops)


def _row_tile(rows, cap=512):
    return max(r for r in range(16, cap + 1, 16) if rows % r == 0)


REDUCE_STEPS = 2


def _chip_presum(tag, grads, gots, c_idx):
    n = len(grads)
    halves = [g.shape[1] // 2 for g in grads]
    tiles = [(h // REDUCE_STEPS, g.shape[2]) for h, g in zip(halves, grads)]

    def body(c_ref, *refs):
        for k in range(n):
            refs[2 * n + k][...] = (refs[k][...].astype(F32) + refs[n + k][...].astype(F32)).astype(BF16)

    mine = [BS((None, None) + tile, lambda s, i, c_ref: (s, c_ref[0], i, 0)) for tile in tiles]
    plain = [BS((None,) + tile, lambda s, i, c_ref: (s, i, 0)) for tile in tiles]
    return list(pl.pallas_call(
        body, out_shape=tuple(SDS((g.shape[0], h, g.shape[2]), BF16) for g, h in zip(grads, halves)),
        grid_spec=pltpu.PrefetchScalarGridSpec(num_scalar_prefetch=1, grid=(N_SHARD, REDUCE_STEPS),
                                               in_specs=mine + plain, out_specs=plain),
        name="reduce_presum_" + tag, compiler_params=_params("parallel", "parallel"))(
            c_idx, *[g.reshape(g.shape[0], 2, h, g.shape[2]) for g, h in zip(grads, halves)], *gots))


HBM_SPEC = BS(memory_space=pltpu.HBM)
SEM_SPEC = BS(memory_space=pltpu.SEMAPHORE)
DATAFLOW = pltpu.SideEffectType.DATAFLOW_SIDE_EFFECTING


def _chip_exchange_copies(parts, lands, send_sems, recv_sems):
    _, _, c, chips = _mesh_place()
    return [_remote(parts[k].at[2 * px + py], lands[k].at[j], send_sems, recv_sems, 3 * k + j, (px, py, c))
            for k in range(len(parts)) for j, (px, py) in enumerate(chips)]


def _gather_copies(shards, lands, send_sems, recv_sems):
    x, y, c, chips = _mesh_place()
    return [_remote(shards[k], lands[k].at[2 * x + y], send_sems, recv_sems, 3 * k + j, (px, py, c))
            for k in range(len(shards)) for j, (px, py) in enumerate(chips)]


def _gather_half_copies(shards, lands, send_sems, recv_sems):
    x, y, c, chips = _mesh_place()
    out = []
    for k in range(len(shards)):
        half = shards[k].shape[0] // 2
        mine = pl.ds(pl.multiple_of(c * half, 16), half)
        for j, (px, py) in enumerate(chips):
            out.append(_remote(shards[k].at[mine, :], lands[k].at[2 * x + y, mine, :], send_sems, recv_sems,
                               3 * k + j, (px, py, c)))
    return out


def _sibling_fill(tag, lands):
    n = len(lands)

    def body(*refs):
        outs = refs[n:2 * n]
        send_sems, recv_sems = refs[2 * n:]
        x, y, c, chips = _mesh_place()
        copies = []
        for k in range(n):
            half = lands[k].shape[1] // 2
            mine = pl.ds(pl.multiple_of(c * half, 16), half)
            for j, (px, py) in enumerate(chips):
                blk = outs[k].at[2 * px + py, mine, :]
                copies.append(_remote(blk, blk, send_sems, recv_sems, 3 * k + j, (x, y, 1 - c)))
        for cp in copies:
            cp.start()
        for cp in copies:
            cp.wait_recv()
        for cp in copies:
            cp.wait_send()

    hbm = BS(memory_space=pl.ANY)
    return list(pl.pallas_call(
        body, out_shape=tuple(SDS(a.shape, a.dtype) for a in lands),
        in_specs=[hbm] * n, out_specs=(hbm,) * n, input_output_aliases={k: k for k in range(n)},
        scratch_shapes=[pltpu.SemaphoreType.DMA((3 * n,)), pltpu.SemaphoreType.DMA((3 * n,))],
        name="gather_fill_" + tag, compiler_params=_params())(*lands))


def _swap_copies(grads, lands, send_sems, recv_sems):
    x, y, c, _ = _mesh_place()
    out = []
    for k in range(len(grads)):
        half = grads[k].shape[1] // 2
        theirs = pl.ds(pl.multiple_of((1 - c) * half, 16), half)
        out.append(_remote(grads[k].at[:, theirs, :], lands[k], send_sems, recv_sems, k, (x, y, 1 - c)))
    return out


def _join_copies(fulls, same, send_sems, recv_sems):
    x, y, c, _ = _mesh_place()
    out = []
    for k in range(len(fulls)):
        half = fulls[k].shape[0] // 2
        mine = fulls[k].at[pl.ds(pl.multiple_of(c * half, 8), half), :]
        out.append(_remote(mine, mine, send_sems, recv_sems, k, (x, y, 1 - c)))
    return out


def _everyone_copies(packs, lands, send_sems, recv_sems):
    x, y, c, _ = _mesh_place()
    out = []
    for k in range(len(packs)):
        for j in range(N_DEV - 1):
            bx, by, bc = (j + 1) >> 2 & 1, (j + 1) >> 1 & 1, (j + 1) & 1
            peer = (x ^ bx, y ^ by, c ^ bc)
            out.append(_remote(packs[k], lands[k].at[4 * x + 2 * y + c], send_sems, recv_sems, (N_DEV - 1) * k + j, peer))
    return out


def _split_start(name, copies, sources, land_shapes, after=(), fanout=3):
    n = len(sources)
    n_land = len(land_shapes)
    m = n + n_land
    n_sems = fanout * n
    after_ops, after_specs = _after_operands(after)

    def body(*refs):
        ins = refs[:n]
        lands = refs[n:m] if n_land else ins
        send_sems, recv_sems = refs[m + len(after_ops)], refs[m + len(after_ops) + 1]
        token = refs[-1]
        for cp in copies(ins, lands, send_sems, recv_sems):
            cp.start()
        token[...] = jnp.zeros_like(token)

    lands = [pltpu.with_memory_space_constraint(lax.empty(s, d), pltpu.HBM) for s, d in land_shapes]
    sources = [pltpu.with_memory_space_constraint(p, pltpu.HBM) for p in sources]
    thru = [pltpu.HBM(a.shape, a.dtype) for a in sources + lands]
    out = pl.pallas_call(
        body, name=name,
        out_shape=(pltpu.SemaphoreType.DMA((n_sems,)), pltpu.SemaphoreType.DMA((n_sems,)), *thru, SDS((8, 128), F32)),
        in_specs=[HBM_SPEC] * m + after_specs,
        out_specs=(SEM_SPEC, SEM_SPEC, *[HBM_SPEC] * m, BS(memory_space=pltpu.VMEM)),
        input_output_aliases={i: 2 + i for i in range(m)},
        compiler_params=pltpu.CompilerParams(has_side_effects=DATAFLOW))(*sources, *lands, *after_ops)
    return out[0], out[1], list(out[2:2 + n]), list(out[2 + n:2 + m]), out[-1]


def _split_wait(name, copies, send_sems, recv_sems, sources, lands, after):
    n = len(sources)
    m = n + len(lands)
    after_ops, after_specs = _after_operands(after)

    def body(*refs):
        ins = refs[:n]
        zones = refs[n:m] if m > n else ins
        for cp in copies(ins, zones, refs[m], refs[m + 1]):
            cp.wait_send()
            cp.wait_recv()

    out = pl.pallas_call(
        body, name=name,
        out_shape=tuple(pltpu.HBM(a.shape, a.dtype) for a in sources + lands),
        in_specs=[HBM_SPEC] * m + [SEM_SPEC, SEM_SPEC] + after_specs, out_specs=(HBM_SPEC,) * m,
        input_output_aliases={i: i for i in range(m)},
        compiler_params=pltpu.CompilerParams(has_side_effects=DATAFLOW))(*sources, *lands, send_sems, recv_sems, *after_ops)
    return list(out[:n]), list(out[n:])


class _WeightGatherer:
    def __init__(self, shards):
        self.shards, self.open = shards, {}
        self.me = 2 * lax.axis_index("x") + lax.axis_index("y")

    HALVED = ("f1a", "mix")

    def start(self, tag, after=()):
        shapes = [((N_SHARD,) + s.shape, s.dtype) for s in self.shards[tag]]
        copies = _gather_half_copies if tag in self.HALVED else _gather_copies
        self.open[tag] = _split_start("gather_start_" + tag, copies, self.shards[tag], shapes, after)
        return [self.open[tag][-1]]

    def finish(self, tag, after):
        send_sems, recv_sems, shards, lands, _ = self.open.pop(tag)
        copies = _gather_half_copies if tag in self.HALVED else _gather_copies
        shards, lands = _split_wait("gather_wait_" + tag, copies, send_sems, recv_sems, shards, lands, after)
        if tag in self.HALVED:
            lands = _sibling_fill(tag, lands)
        return [lax.dynamic_update_slice(zone, s[None], (self.me, 0, 0)) for zone, s in zip(lands, shards)]


class _GradReducer:
    def __init__(self):
        self.c_idx = lax.axis_index("c").astype(jnp.int32).reshape(1)
        self.place = jnp.stack([2 * lax.axis_index("x") + lax.axis_index("y"), lax.axis_index("c")]).astype(jnp.int32)
        self.swaps, self.open, self.landed, self.joins, self.reduced = {}, {}, {}, {}, []

    def swap_start(self, tag, grads, after=()):
        shapes = [((g.shape[0], g.shape[1] // 2, g.shape[2]), g.dtype) for g in grads]
        self.swaps[tag] = _split_start("reduce_swap_start_" + tag, _swap_copies, grads, shapes, after, fanout=1)
        return [self.swaps[tag][-1]]

    def start(self, tag, grads, after=(), swapped=()):
        pairs = []
        for s in swapped:
            send_sems, recv_sems, early, lands, _ = self.swaps.pop(s)
            pairs += zip(*_split_wait("reduce_swap_wait_" + s, _swap_copies, send_sems, recv_sems, early, lands, grads[-1:]))
        pairs += zip(grads, _sibling_swap_halves(tag, grads, after))
        parts = _chip_presum(tag, [g for g, _ in pairs], [s for _, s in pairs], self.c_idx)
        shapes = [((3,) + p.shape[1:], p.dtype) for p in parts]
        self.open[tag] = _split_start("reduce_exchange_start_" + tag, _chip_exchange_copies, parts, shapes)
        return [self.open[tag][-1]]

    def finish(self, tag, after):
        send_sems, recv_sems, parts, lands, _ = self.open.pop(tag)
        self.landed[tag] = _split_wait("reduce_exchange_wait_" + tag, _chip_exchange_copies, send_sems, recv_sems, parts, lands, after)
        return self.landed[tag][1][:1]

    def _sums(self, tag, after=()):
        parts, landed = self.landed.pop(tag)
        return _chip_sum(tag, parts, landed, self.place, after)

    def join_start(self, tag, after=()):
        self.joins[tag] = _split_start("reduce_join_start_" + tag, _join_copies, self._sums(tag, after), [], fanout=1)
        return [self.joins[tag][-1]]

    def join_finish(self, tag, after):
        send_sems, recv_sems, fulls, _, _ = self.joins.pop(tag)
        self.reduced += _split_wait("reduce_join_wait_" + tag, _join_copies, send_sems, recv_sems, fulls, [], after)[0]

    def join(self, tag, after=()):
        self.reduced += _sibling_join_halves(self._sums(tag), after)


def _chip_sum(tag, parts, gots, place, after=()):
    n = len(parts)
    tiles = [(p.shape[1] // REDUCE_STEPS, p.shape[2]) for p in parts]
    after_ops, after_specs = _after_operands(after)

    def body(place_ref, *refs):
        outs = refs[2 * n + len(after_ops):]
        for k in range(n):
            acc = refs[k][...].astype(F32)
            for j in range(3):
                acc = acc + refs[n + k][j].astype(F32)
            outs[k][...] = acc

    return list(pl.pallas_call(
        body, out_shape=tuple(SDS((2 * p.shape[1], p.shape[2]), F32) for p in parts),
        grid_spec=pltpu.PrefetchScalarGridSpec(
            num_scalar_prefetch=1, grid=(REDUCE_STEPS,),
            in_specs=[BS((None,) + tile, lambda i, place_ref: (place_ref[0], i, 0)) for tile in tiles]
            + [BS((3,) + tile, lambda i, place_ref: (0, i, 0)) for tile in tiles] + after_specs,
            out_specs=[BS(tile, lambda i, place_ref: (place_ref[1] * REDUCE_STEPS + i, 0)) for tile in tiles]),
        name="reduce_sum_" + tag, compiler_params=_params("parallel"))(place, *parts, *gots, *after_ops))


def _sibling_join_halves(fulls, after=()):
    n = len(fulls)
    after_ops, after_specs = _after_operands(after)

    def body(*refs):
        outs = refs[n + len(after_ops):2 * n + len(after_ops)]
        send_sems, recv_sems = refs[2 * n + len(after_ops):]
        copies = _join_copies(outs, outs, send_sems, recv_sems)
        for cp in copies:
            cp.start()
        for cp in copies:
            cp.wait_recv()
        for cp in copies:
            cp.wait_send()

    hbm = BS(memory_space=pl.ANY)
    return list(pl.pallas_call(
        body, out_shape=tuple(SDS(f.shape, f.dtype) for f in fulls),
        in_specs=[hbm] * n + after_specs, out_specs=(hbm,) * n, input_output_aliases={k: k for k in range(n)},
        scratch_shapes=[pltpu.SemaphoreType.DMA((n,)), pltpu.SemaphoreType.DMA((n,))],
        name="reduce_sibling_join", compiler_params=_params())(*fulls, *after_ops))


N_DEV = 8


def _sum_devices(packs):
    _, rows, lanes = packs.shape

    def body(p_ref, o_ref):
        acc = p_ref[0]
        for dev in range(1, N_DEV):
            acc = acc + p_ref[dev]
        o_ref[...] = acc

    vm = BS(memory_space=pltpu.VMEM)
    return pl.pallas_call(body, out_shape=SDS((rows, lanes), F32), in_specs=[vm], out_specs=vm,
                          name="small_sum", compiler_params=_params())(packs)


def _adamw(name, w, grad, row0, m, v, after=()):
    rows, cols = w.shape
    tr = rows if rows < 16 else _row_tile(rows, 352)
    bc1 = 1.0 - ADAM_B1 ** ADAM_STEP
    bc2 = 1.0 - ADAM_B2 ** ADAM_STEP
    after_ops, after_specs = _after_operands(after)

    def body(w_ref, g_ref, m_ref, v_ref, *rest):
        go_ref, d_ref, mo_ref, vo_ref = rest[len(after_ops):]
        g = g_ref[...]
        m_new = ADAM_B1 * m_ref[...] + (1.0 - ADAM_B1) * g
        v_new = ADAM_B2 * v_ref[...] + (1.0 - ADAM_B2) * (g * g)
        go_ref[...] = g
        mo_ref[...] = m_new
        vo_ref[...] = v_new
        d_ref[...] = -ADAM_LR * ((m_new / bc1) / (jnp.sqrt(v_new / bc2) + ADAM_EPS) + ADAM_WD * w_ref[...])

    blk = BS((tr, cols), lambda i: (i, 0))
    shape = SDS((rows, cols), F32)
    return pl.pallas_call(
        body, out_shape=(shape,) * 4, grid=(rows // tr,),
        in_specs=[blk, BS((tr, cols), lambda i: (row0 // tr + i, 0)), blk, blk] + after_specs, out_specs=(blk,) * 4,
        name=name, compiler_params=_params("parallel"))(w, grad, m, v, *after_ops)


SMALL_LANES = 128


def _pack_small(parts):
    flat = jnp.concatenate([jnp.ravel(p) for p in parts])
    rows = -(-flat.shape[0] // (64 * SMALL_LANES)) * 64
    return jnp.pad(flat, (0, rows * SMALL_LANES - flat.shape[0])).reshape(rows, SMALL_LANES)


def _unpack_small(packed, like):
    flat = jnp.ravel(packed)
    out, at = [], 0
    for p in like:
        out.append(flat[at:at + p.size].reshape(p.shape))
        at += p.size
    return out


def kernel(x, mem, ffn1_norm, ffn1_w_gate, ffn1_w_up, ffn1_w_down, mix_norm, w_in, pool_w, pool_scale, w_pool_proj, ssm_a_re, ssm_a_im, ssm_log_dt, ssm_b_re, ssm_b_im, ssm_c_re, ssm_c_im, ssm_d, w_glu_val, w_glu_gate, w_mix_out, xattn_norm, mem_norm, w_q, w_kv, w_xo, ffn2_norm, ffn2_w_gate, ffn2_w_up, ffn2_w_down, final_norm, loss_target, m_ffn1_norm, m_ffn1_w_gate, m_ffn1_w_up, m_ffn1_w_down, m_mix_norm, m_w_in, m_pool_w, m_pool_scale, m_w_pool_proj, m_ssm_a_re, m_ssm_a_im, m_ssm_log_dt, m_ssm_b_re, m_ssm_b_im, m_ssm_c_re, m_ssm_c_im, m_ssm_d, m_w_glu_val, m_w_glu_gate, m_w_mix_out, m_xattn_norm, m_mem_norm, m_w_q, m_w_kv, m_w_xo, m_ffn2_norm, m_ffn2_w_gate, m_ffn2_w_up, m_ffn2_w_down, m_final_norm, v_ffn1_norm, v_ffn1_w_gate, v_ffn1_w_up, v_ffn1_w_down, v_mix_norm, v_w_in, v_pool_w, v_pool_scale, v_w_pool_proj, v_ssm_a_re, v_ssm_a_im, v_ssm_log_dt, v_ssm_b_re, v_ssm_b_im, v_ssm_c_re, v_ssm_c_im, v_ssm_d, v_w_glu_val, v_w_glu_gate, v_w_mix_out, v_xattn_norm, v_mem_norm, v_w_q, v_w_kv, v_w_xo, v_ffn2_norm, v_ffn2_w_gate, v_ffn2_w_up, v_ffn2_w_down, v_final_norm):
    given = dict(locals())
    w = {n: given[n] for n in WEIGHTS}
    m = {n: given["m_" + n] for n in WEIGHTS}
    v = {n: given["v_" + n] for n in WEIGHTS}

    def shard_view(a, n):
        return a[0].T if n in TRANSPOSED else a[0]

    def shard_unview(a, n):
        return (a.T if n in TRANSPOSED else a)[None]

    shards = {tag: [jnp.concatenate([shard_view(w[n], n).astype(BF16) for n in grp], axis=0) for grp in arrays]
              for tag, arrays in GATHER_PHASES.items()}
    reducer = _GradReducer()
    ws, ms, vs = ({n: _small_view(a[n], n) for n in SMALL} for a in (w, m, v))
    loss_part, grad_x, _, small = _device_step(x[0], mem[0], loss_target[0], _WeightGatherer(shards), ws, reducer)

    small_like = [ws[n] for n in SMALL] + [loss_part[0, :1]]
    pack = _pack_small([small[n] for n in SMALL] + [loss_part[0, :1]])
    everyone = _split_start("small_start", _everyone_copies, [pack], [((N_DEV,) + pack.shape, F32)], fanout=N_DEV - 1)
    reducer.join("b", after=everyone[-1:])

    grads, delta, new_m, new_v = {}, {}, {}, {}
    big_done = []
    for grp, red in zip(REDUCE_GROUPS, reducer.reduced):
        row0 = 0
        for n in grp:
            w_n = shard_view(w[n], n)
            outs = _adamw("adamw_" + n, w_n, red, row0, shard_view(m[n], n), shard_view(v[n], n), after=everyone[-1:])
            grads[n], delta[n], new_m[n], new_v[n] = (shard_unview(o, n) for o in outs)
            big_done.append(outs[1])
            row0 += w_n.shape[0]

    send_sems, recv_sems, packs, landed, _ = everyone
    packs, landed = _split_wait("small_wait", _everyone_copies, send_sems, recv_sems, packs, landed, big_done)
    mine = 4 * lax.axis_index("x") + 2 * lax.axis_index("y") + lax.axis_index("c")
    summed = _sum_devices(lax.dynamic_update_slice(landed[0], packs[0][None], (mine, 0, 0)))
    g_small = dict(zip(SMALL + ("loss",), _unpack_small(summed, small_like)))
    loss = g_small.pop("loss").reshape(())
    narrow = [n for n in SMALL if ws[n].ndim > 3]
    dense = [n for n in SMALL if n not in narrow]
    for n in narrow:
        two_d = (-1, ws[n].shape[-1])
        outs = _adamw("adamw_" + n, ws[n].reshape(two_d), g_small[n].reshape(two_d), 0, ms[n].reshape(two_d), vs[n].reshape(two_d))
        grads[n], delta[n], new_m[n], new_v[n] = (_small_view(o.reshape(ws[n].shape), n) for o in outs)
    dense_like = [ws[n] for n in dense]
    packed = _adamw("adamw_small", _pack_small(dense_like), _pack_small([g_small[n] for n in dense]), 0,
                    _pack_small([ms[n] for n in dense]), _pack_small([vs[n] for n in dense]))
    for out, store in zip(packed, (grads, delta, new_m, new_v)):
        for n, val in zip(dense, _unpack_small(out, dense_like)):
            store[n] = val

    return (loss, grad_x[None], *[grads[n] for n in WEIGHTS], *[delta[n] for n in WEIGHTS],
            *[new_m[n] for n in WEIGHTS], *[new_v[n] for n in WEIGHTS])
```

```python
import functools
import math

import jax
import jax.numpy as jnp
from jax import lax
from jax.experimental import pallas as pl
from jax.experimental.pallas import tpu as pltpu

F32 = jnp.float32
BF16 = jnp.bfloat16
SDS = jax.ShapeDtypeStruct
BS = pl.BlockSpec
MESH = pl.DeviceIdType.MESH

D_MODEL = 1024
D_FF = 2816
N_SHARD = 4
FF_SH = D_FF // N_SHARD
D_POOL = 512
POOL_WINDOWS = (2, 4, 8, 16)
POOL_GROUP = 128
D_SSM = 256
SSM_GROUPS = 16
SSM_GROUP = 16
SSM_STATE = 64
SSM_CH = SSM_GROUPS * SSM_STATE
N_HEADS = 4
HEAD_DIM = 256
EPS = 1e-6
ADAM_LR, ADAM_B1, ADAM_B2, ADAM_EPS, ADAM_WD, ADAM_STEP = 0.001, 0.9, 0.999, 1e-08, 0.01, 10

VMEM_LIMIT_V7X = 52 * 1024 * 1024
TM = 512

NN = (((1,), (0,)), ((), ()))
NT = (((1,), (1,)), ((), ()))
TN = (((0,), (0,)), ((), ()))


def _params(*sem):
    return pltpu.CompilerParams(dimension_semantics=sem if sem else None, vmem_limit_bytes=VMEM_LIMIT_V7X)


def _dot(a, b, dims=NN):
    return lax.dot_general(a.astype(BF16), b.astype(BF16), dims, preferred_element_type=F32)


def _sigmoid(v):
    return pl.reciprocal(1.0 + jnp.exp(-v), approx=True)


def _block_dims(spec):
    return tuple(d for d in spec.block_shape if d is not None)


def _after_operands(after):
    return list(after), [BS(memory_space=pl.ANY)] * len(after)


def _mm(name, pairs, *, grid, out_shape, out_spec, red_axis=None, extras=(), epilogue=None, after=()):
    n_pairs, n_extra = len(pairs), len(extras)
    n_red = grid[red_axis] if red_axis is not None else 1
    dims = [p[4] for p in pairs]

    def body(*refs):
        ab = refs[:2 * n_pairs]
        ex = refs[2 * n_pairs:2 * n_pairs + n_extra]
        o_ref = refs[2 * n_pairs + n_extra + len(after)]

        def partial():
            acc = None
            for p in range(n_pairs):
                t = _dot(ab[2 * p][...], ab[2 * p + 1][...], dims[p])
                acc = t if acc is None else acc + t
            return acc

        def finish(acc):
            res = epilogue(acc, *[e[...] for e in ex]) if epilogue is not None else acc
            o_ref[...] = res.astype(o_ref.dtype)

        if n_red == 1:
            finish(partial())
        else:
            acc_ref = refs[-1]
            k = pl.program_id(red_axis)

            @pl.when(k == 0)
            def _():
                acc_ref[...] = jnp.zeros_like(acc_ref)

            acc_ref[...] += partial()

            @pl.when(k == n_red - 1)
            def _():
                finish(acc_ref[...])

    operands, in_specs = [], []
    for a, a_spec, b, b_spec, _ in pairs:
        operands += [a, b]
        in_specs += [a_spec, b_spec]
    for e, e_spec in extras:
        operands.append(e)
        in_specs.append(e_spec)
    after_ops, after_specs = _after_operands(after)
    operands += after_ops
    in_specs += after_specs
    scratch = [pltpu.VMEM(_block_dims(out_spec), F32)] if n_red > 1 else []
    sem = tuple("arbitrary" if ax == red_axis else "parallel" for ax in range(len(grid)))
    return pl.pallas_call(body, out_shape=out_shape, grid=grid, in_specs=in_specs, out_specs=out_spec,
                          scratch_shapes=scratch, name=name, compiler_params=_params(*sem))(*operands)


def _rmsnorm(name, h, gain, tm, after=()):
    t, d = h.shape
    after_ops, after_specs = _after_operands(after)

    def body(h_ref, g_ref, *rest):
        u_ref = rest[-1]
        hv = h_ref[...]
        r = lax.rsqrt(jnp.mean(hv * hv, axis=-1, keepdims=True) + EPS)
        u_ref[...] = ((hv * r) * g_ref[...]).astype(u_ref.dtype)

    return pl.pallas_call(
        body, out_shape=SDS((t, d), BF16), grid=(t // tm,),
        in_specs=[BS((tm, d), lambda i: (i, 0)), BS((1, d), lambda i: (0, 0))] + after_specs,
        out_specs=BS((tm, d), lambda i: (i, 0)), name=name, compiler_params=_params("parallel"))(h, gain, *after_ops)


def _rmsnorm_bwd(name, h, gain, du, dh_in, tm):
    t, d = h.shape
    has_in = dh_in is not None

    def body(*refs):
        if has_in:
            h_ref, g_ref, du_ref, dhin_ref, dh_ref, dhb_ref, dg_ref = refs
        else:
            h_ref, g_ref, du_ref, dh_ref, dhb_ref, dg_ref = refs
        i = pl.program_id(0)
        hv = h_ref[...]
        r = lax.rsqrt(jnp.mean(hv * hv, axis=-1, keepdims=True) + EPS)
        n = hv * r
        duv = du_ref[...].astype(F32)
        dn = duv * g_ref[...]
        dh = r * (dn - n * jnp.mean(dn * n, axis=-1, keepdims=True))
        if has_in:
            dh = dhin_ref[...] + dh
        dh_ref[...] = dh
        dhb_ref[...] = dh.astype(BF16)

        @pl.when(i == 0)
        def _():
            dg_ref[...] = jnp.zeros_like(dg_ref)

        dg_ref[...] += jnp.sum(duv * n, axis=0, keepdims=True)

    row = BS((tm, d), lambda i: (i, 0))
    vec = BS((1, d), lambda i: (0, 0))
    operands = [h, gain, du] + ([dh_in] if has_in else [])
    in_specs = [row, vec, row] + ([row] if has_in else [])
    return pl.pallas_call(
        body, out_shape=(SDS((t, d), F32), SDS((t, d), BF16), SDS((1, d), F32)), grid=(t // tm,),
        in_specs=in_specs, out_specs=(row, row, vec), name=name, compiler_params=_params("arbitrary"))(*operands)


def _loss_head_tile(i, hv, g_ref, t_ref, loss_ref, dh_ref, dhb_ref, dg_ref):
    g = g_ref[...]
    r = lax.rsqrt(jnp.mean(hv * hv, axis=-1, keepdims=True) + EPS)
    n = hv * r
    err = n * g - t_ref[...]
    dy = err * (1.0 / hv.shape[-1])
    dn = dy * g
    dh = r * (dn - n * jnp.mean(dn * n, axis=-1, keepdims=True))
    dh_ref[...] = dh
    dhb_ref[...] = dh.astype(BF16)

    @pl.when(i == 0)
    def _():
        dg_ref[...] = jnp.zeros_like(dg_ref)
        loss_ref[...] = jnp.zeros_like(loss_ref)

    dg_ref[...] += jnp.sum(dy * n, axis=0, keepdims=True)
    part = 0.5 * jnp.sum(jnp.mean(err * err, axis=-1, keepdims=True), axis=0, keepdims=True)
    loss_ref[...] += jnp.broadcast_to(part, loss_ref.shape)


def _norm_tile(h, g_ref, u_ref):
    r = lax.rsqrt(jnp.mean(h * h, axis=-1, keepdims=True) + EPS)
    u_ref[...] = ((h * r) * g_ref[...]).astype(u_ref.dtype)


FFN_GATE, FFN_UP, FFN_DOWN = 0, 1, 2


def _ffn_up(name, u, w_f, tm, after=()):
    t, d = u.shape
    after_ops, after_specs = _after_operands(after)

    def body(u_ref, wg_ref, wu_ref, *rest):
        pg_ref, pu_ref, a_ref = rest[len(after_ops):]
        uv = u_ref[...]
        for s in range(N_SHARD):
            g = _dot(uv, wg_ref[s], NT)
            up = _dot(uv, wu_ref[s], NT)
            sg = _sigmoid(g)
            silu = g * sg
            a_ref[s] = (silu * up).astype(BF16)
            pu_ref[s] = (0.5 * silu).astype(BF16)
            pg_ref[s] = (0.5 * sg * (1.0 + g * (1.0 - sg)) * up).astype(BF16)

    hid = BS((N_SHARD, tm, FF_SH), lambda i: (0, i, 0))
    shape = SDS((N_SHARD, t, FF_SH), BF16)
    return pl.pallas_call(
        body, out_shape=(shape, shape, shape), grid=(t // tm,),
        in_specs=[BS((tm, d), lambda i: (i, 0)), _ffn_all_shards_spec(w_f["gate"][1]),
                  _ffn_all_shards_spec(w_f["up"][1])] + after_specs,
        out_specs=(hid, hid, hid), name=name,
        compiler_params=_params("parallel"))(u, w_f["gate"][0], w_f["up"][0], *after_ops)


def _ffn_all_shards_spec(block):
    return BS((N_SHARD, FF_SH, D_MODEL), lambda i: (0, block, 0))


def _ffn_down(name, a, w_f, resid, tm, next_gain=None, head=None):
    t, d = resid.shape
    row = BS((tm, d), lambda i: (i, 0))
    vec = BS((1, d), lambda i: (0, 0))

    def body(a_ref, w_ref, res_ref, *rest):
        acc = _dot(a_ref[0], w_ref[0])
        for s in range(1, N_SHARD):
            acc = acc + _dot(a_ref[s], w_ref[s])
        h = res_ref[...] + 0.5 * acc
        if head is not None:
            _loss_head_tile(pl.program_id(0), h, *rest)
        else:
            g_ref, h_ref, u_ref = rest
            h_ref[...] = h
            _norm_tile(h, g_ref, u_ref)

    if head is not None:
        extra, extra_specs = list(head), [vec, row]
        out_shape = (SDS((1, 128), F32), SDS((t, d), F32), SDS((t, d), BF16), SDS((1, d), F32))
        out_specs = (BS((1, 128), lambda i: (0, 0)), row, row, vec)
    else:
        extra, extra_specs = [next_gain], [vec]
        out_shape = (SDS((t, d), F32), SDS((t, d), BF16))
        out_specs = (row, row)
    return pl.pallas_call(
        body, out_shape=out_shape, grid=(t // tm,),
        in_specs=[BS((N_SHARD, tm, FF_SH), lambda i: (0, i, 0)), _ffn_all_shards_spec(w_f["down"][1]), row] + extra_specs,
        out_specs=out_specs, name=name,
        compiler_params=_params("arbitrary" if head is not None else "parallel"))(a, w_f["down"][0], resid, *extra)


def _mm_resid_norm(name, a, b, resid, next_gain, tm):
    t, d = resid.shape

    def body(a_ref, b_ref, res_ref, g_ref, h_ref, u_ref):
        h = res_ref[...] + _dot(a_ref[...], b_ref[...])
        h_ref[...] = h
        _norm_tile(h, g_ref, u_ref)

    row = BS((tm, d), lambda i: (i, 0))
    return pl.pallas_call(
        body, out_shape=(SDS((t, d), F32), SDS((t, d), BF16)), grid=(t // tm,),
        in_specs=[BS((tm, a.shape[1]), lambda i: (i, 0)), BS(b.shape, lambda i: (0, 0)), row, BS((1, d), lambda i: (0, 0))],
        out_specs=(row, row), name=name, compiler_params=_params("parallel"))(a, b, resid, next_gain)


def _ffn_bwd_act(name, dh_b, w_f, pg, pu, tm, after=()):
    t, d = dh_b.shape
    after_ops, after_specs = _after_operands(after)

    def body(dh_ref, wd_ref, pg_ref, pu_ref, *rest):
        dg_ref, dup_ref = rest[len(after_ops):]
        dh = dh_ref[...]
        for s in range(N_SHARD):
            da = _dot(dh, wd_ref[s], NT)
            dg_ref[s] = (da * pg_ref[s].astype(F32)).astype(BF16)
            dup_ref[s] = (da * pu_ref[s].astype(F32)).astype(BF16)

    hid = BS((N_SHARD, tm, FF_SH), lambda i: (0, i, 0))
    shape = SDS((N_SHARD, t, FF_SH), BF16)
    return pl.pallas_call(
        body, out_shape=(shape, shape), grid=(t // tm,),
        in_specs=[BS((tm, d), lambda i: (i, 0)), _ffn_all_shards_spec(w_f["down"][1]), hid, hid] + after_specs,
        out_specs=(hid, hid), name=name,
        compiler_params=_params("parallel"))(dh_b, w_f["down"][0], pg, pu, *after_ops)


def _ffn_dw(name, u, dg, dup, a, dh_b, tm):
    t, d = u.shape
    tm = min(2 * tm, t)
    n_t = t // tm

    def body(u_ref, dg_ref, dup_ref, a_ref, dh_ref, o_ref, acc):
        i = pl.program_id(1)

        @pl.when(i == 0)
        def _():
            acc[...] = jnp.zeros_like(acc)

        uv = u_ref[...]
        acc[FFN_GATE] += _dot(dg_ref[...], uv, TN)
        acc[FFN_UP] += _dot(dup_ref[...], uv, TN)
        acc[FFN_DOWN] += _dot(a_ref[...], dh_ref[...], TN)

        @pl.when(i == n_t - 1)
        def _():
            o_ref[FFN_GATE] = acc[FFN_GATE].astype(BF16)
            o_ref[FFN_UP] = acc[FFN_UP].astype(BF16)
            o_ref[FFN_DOWN] = (0.5 * acc[FFN_DOWN]).astype(BF16)

    hid = BS((None, tm, FF_SH), lambda s, i: (s, i, 0))
    row = BS((tm, d), lambda s, i: (i, 0))
    return pl.pallas_call(
        body, out_shape=SDS((N_SHARD, 3, FF_SH, d), BF16), grid=(N_SHARD, n_t),
        in_specs=[row, hid, hid, hid, row], out_specs=BS((None, 3, FF_SH, d), lambda s, i: (s, 0, 0, 0)),
        scratch_shapes=[pltpu.VMEM((3, FF_SH, d), F32)],
        name=name, compiler_params=_params("parallel", "arbitrary"))(u, dg, dup, a, dh_b)


def _norm_bwd_tile(i, du, h_ref, g_ref, dhin_ref, dh_ref, dhb_ref, dg_ref):
    hv = h_ref[...]
    r = lax.rsqrt(jnp.mean(hv * hv, axis=-1, keepdims=True) + EPS)
    n = hv * r
    dn = du * g_ref[...]
    dh = dhin_ref[...] + r * (dn - n * jnp.mean(dn * n, axis=-1, keepdims=True))
    dh_ref[...] = dh
    dhb_ref[...] = dh.astype(BF16)

    @pl.when(i == 0)
    def _():
        dg_ref[...] = jnp.zeros_like(dg_ref)

    dg_ref[...] += jnp.sum(du * n, axis=0, keepdims=True)


def _norm_bwd_specs(tm):
    row = BS((tm, D_MODEL), lambda i: (i, 0))
    vec = BS((1, D_MODEL), lambda i: (0, 0))
    return [row, vec, row], (row, row, vec)


def _norm_bwd_shapes(t):
    return SDS((t, D_MODEL), F32), SDS((t, D_MODEL), BF16), SDS((1, D_MODEL), F32)


def _ffn_dx(name, dg, dup, w_f, h, gain, dh_in, tm, after=()):
    t = dg.shape[1]
    tm = tm // 2
    after_ops, after_specs = _after_operands(after)

    def body(dg_ref, dup_ref, wg_ref, wu_ref, h_ref, g_ref, dhin_ref, *rest):
        acc = _dot(dg_ref[0], wg_ref[0]) + _dot(dup_ref[0], wu_ref[0])
        for s in range(1, N_SHARD):
            acc = acc + _dot(dg_ref[s], wg_ref[s]) + _dot(dup_ref[s], wu_ref[s])
        _norm_bwd_tile(pl.program_id(0), acc, h_ref, g_ref, dhin_ref, *rest[len(after_ops):])

    hid = BS((N_SHARD, tm, FF_SH), lambda i: (0, i, 0))
    norm_in, norm_out = _norm_bwd_specs(tm)
    return pl.pallas_call(
        body, out_shape=_norm_bwd_shapes(t), grid=(t // tm,),
        in_specs=[hid, hid, _ffn_all_shards_spec(w_f["gate"][1]), _ffn_all_shards_spec(w_f["up"][1])] + norm_in + after_specs,
        out_specs=norm_out, name=name,
        compiler_params=_params("arbitrary"))(dg, dup, w_f["gate"][0], w_f["up"][0], h, gain, dh_in, *after_ops)


def _mm_norm_bwd(name, a, b, dims, h, gain, dh_in, tm):
    t = a.shape[0]

    def body(a_ref, b_ref, h_ref, g_ref, dhin_ref, *outs):
        _norm_bwd_tile(pl.program_id(0), _dot(a_ref[...], b_ref[...], dims), h_ref, g_ref, dhin_ref, *outs)

    norm_in, norm_out = _norm_bwd_specs(tm)
    return pl.pallas_call(
        body, out_shape=_norm_bwd_shapes(t), grid=(t // tm,),
        in_specs=[BS((tm, a.shape[1]), lambda i: (i, 0)), BS(b.shape, lambda i: (0, 0))] + norm_in,
        out_specs=norm_out, name=name, compiler_params=_params("arbitrary"))(a, b, h, gain, dh_in)


def _plain_mm(name, a, b, dims, out_dtype, tm, resid=None, after=()):
    t = a.shape[0]
    n = b.shape[1] if dims == NN else b.shape[0]
    extras = [(resid, BS((tm, n), lambda i: (i, 0)))] if resid is not None else []
    epi = (lambda acc, res: res + acc) if resid is not None else None
    return _mm(name, [(a, BS((tm, a.shape[1]), lambda i: (i, 0)), b, BS(b.shape, lambda i: (0, 0)), dims)],
               grid=(t // tm,), out_shape=SDS((t, n), out_dtype), out_spec=BS((tm, n), lambda i: (i, 0)),
               extras=extras, epilogue=epi, after=after)


def _dw_mm(name, a, b, tm, out_dtype=BF16, after=()):
    t, k = a.shape
    n = b.shape[1]
    tm = min(2 * tm, t)
    return _mm(name, [(a, BS((tm, k), lambda i: (i, 0)), b, BS((tm, n), lambda i: (i, 0)), TN)],
               grid=(t // tm,), red_axis=0, out_shape=SDS((k, n), out_dtype), out_spec=BS((k, n), lambda i: (0, 0)),
               after=after)


POOL_CHUNK = 256
POOL_HALO = 8


def _window_sum(v, width, lead):
    n = v.shape[0]
    s = v
    k = 1
    while k < width:
        s = s + pltpu.roll(s, n - k, 0)
        k *= 2
    return pltpu.roll(s, lead, 0) if lead else s


def _pool_count(base, left, right, t, shape):
    pos = base + lax.broadcasted_iota(jnp.int32, shape, 0)
    lo = jnp.maximum(pos - left, 0)
    hi = jnp.minimum(pos + right + 1, t)
    return (hi - lo).astype(F32)


def _pool_fwd(proj, pool_w, pool_scale):
    t = proj.shape[0]
    c, h = POOL_CHUNK, POOL_HALO
    n_chunks = t // c

    def body(proj_hbm, pw_ref, sc_ref, pooled_ref, mixed_ref, ms_ref, pad_ref, sem):
        cp = pltpu.make_async_copy(proj_hbm.at[:, pl.ds(0, D_POOL)], pad_ref.at[pl.ds(h, t), :], sem)
        cp.start()
        pad_ref[pl.ds(0, h), :] = jnp.zeros((h, D_POOL), F32)
        pad_ref[pl.ds(t + h, h), :] = jnp.zeros((h, D_POOL), F32)
        cp.wait()
        for g, width in enumerate(POOL_WINDOWS):
            left = width // 2
            right = width - 1 - left
            cols = slice(g * POOL_GROUP, (g + 1) * POOL_GROUP)
            wmat = pw_ref[g].astype(BF16)
            scale = sc_ref[:, cols]

            def chunk(ci, carry, left=left, right=right, width=width, cols=cols, wmat=wmat, scale=scale):
                base = pl.multiple_of(ci * c, c)
                v = pad_ref[pl.ds(base, c + 2 * h), cols]
                win = _window_sum(v, width, left)[h:h + c]
                cnt = _pool_count(base, left, right, t, (c, POOL_GROUP))
                pooled = (win / cnt - v[h:h + c]).astype(BF16)
                mixed = _dot(pooled, wmat)
                pooled_ref[pl.ds(base, c), cols] = pooled
                mixed_ref[pl.ds(base, c), cols] = mixed.astype(BF16)
                ms_ref[pl.ds(base, c), cols] = (mixed * scale).astype(BF16)
                return carry

            lax.fori_loop(0, n_chunks, chunk, 0)

    vm = BS(memory_space=pltpu.VMEM)
    shape = SDS((t, D_POOL), BF16)
    return pl.pallas_call(
        body, out_shape=(shape, shape, shape),
        in_specs=[BS(memory_space=pl.ANY), vm, vm], out_specs=(vm, vm, vm),
        scratch_shapes=[pltpu.VMEM((t + 2 * h, D_POOL), F32), pltpu.SemaphoreType.DMA],
        name="pool_fwd", compiler_params=_params())(proj, pool_w, pool_scale)


def _pool_bwd(d_ms, mixed, pooled, pool_w, pool_scale):
    t = d_ms.shape[0]
    c, h = POOL_CHUNK, POOL_HALO
    n_chunks = t // c

    def body(dms_ref, mixed_ref, pooled_ref, pw_ref, sc_ref, dp_ref, dsc_ref, dpw_ref, pad_ref):
        pad_ref[pl.ds(0, h), :] = jnp.zeros((h, D_POOL), F32)
        pad_ref[pl.ds(t + h, h), :] = jnp.zeros((h, D_POOL), F32)
        for g, width in enumerate(POOL_WINDOWS):
            left = width // 2
            right = width - 1 - left
            cols = slice(g * POOL_GROUP, (g + 1) * POOL_GROUP)
            wmat = pw_ref[g].astype(BF16)
            scale = sc_ref[:, cols]

            def first(ci, carry, left=left, right=right, cols=cols, wmat=wmat, scale=scale):
                dsc, dpw = carry
                base = pl.multiple_of(ci * c, c)
                dms = dms_ref[pl.ds(base, c), cols].astype(F32)
                dsc = dsc + jnp.sum(dms * mixed_ref[pl.ds(base, c), cols].astype(F32), axis=0, keepdims=True)
                dmix = (dms * scale).astype(BF16)
                dpw = dpw + _dot(pooled_ref[pl.ds(base, c), cols], dmix, TN)
                dpooled = _dot(dmix, wmat, NT)
                cnt = _pool_count(base, left, right, t, (c, POOL_GROUP))
                pad_ref[pl.ds(base + h, c), cols] = dpooled / cnt
                return dsc, dpw

            dsc, dpw = lax.fori_loop(0, n_chunks, first,
                                     (jnp.zeros((1, POOL_GROUP), F32), jnp.zeros((POOL_GROUP, POOL_GROUP), F32)))
            dsc_ref[:, cols] = dsc
            dpw_ref[g] = dpw

            def second(ci, carry, left=left, right=right, width=width, cols=cols):
                base = pl.multiple_of(ci * c, c)
                v = pad_ref[pl.ds(base, c + 2 * h), cols]
                win = _window_sum(v, width, right)[h:h + c]
                cnt = _pool_count(base, left, right, t, (c, POOL_GROUP))
                dp_ref[pl.ds(base, c), cols] = (win - v[h:h + c] * cnt).astype(BF16)
                return carry

            lax.fori_loop(0, n_chunks, second, 0)

    vm = BS(memory_space=pltpu.VMEM)
    return pl.pallas_call(
        body, out_shape=(SDS((t, D_POOL), BF16), SDS((1, D_POOL), F32), SDS((4, POOL_GROUP, POOL_GROUP), F32)),
        in_specs=[vm] * 5, out_specs=(vm, vm, vm),
        scratch_shapes=[pltpu.VMEM((t + 2 * h, D_POOL), F32)],
        name="pool_bwd", compiler_params=_params())(d_ms, mixed, pooled, pool_w, pool_scale)


SSM_ROWS = 2 * SSM_GROUPS * SSM_GROUP
SSM_HALF = SSM_GROUPS * SSM_GROUP


def _ssm_zoh(a_r, a_i, ldt):
    dt = jnp.exp(ldt)
    mag = jnp.exp(dt * a_r)
    ang = dt * a_i
    cs, sn = jnp.cos(ang), jnp.sin(ang)
    abr, abi = mag * cs, mag * sn
    den = a_r * a_r + a_i * a_i
    nr = abr - 1.0
    qr = (nr * a_r + abi * a_i) / den
    qi = (abi * a_r - nr * a_i) / den
    return dt, mag, cs, sn, abr, abi, den, nr, qr, qi


def _ssm_group_mask():
    row = lax.broadcasted_iota(jnp.int32, (SSM_HALF, SSM_CH), 0)
    col = lax.broadcasted_iota(jnp.int32, (SSM_HALF, SSM_CH), 1)
    return (row // SSM_GROUP) == (col // SSM_STATE)


def _ssm_prep(a_r, a_i, ldt, b_r, b_i, c_r, c_i, after=()):
    after_ops, after_specs = _after_operands(after)

    def body(ar_ref, ai_ref, ldt_ref, br_ref, bi_ref, cr_ref, ci_ref, *rest):
        abr_ref, abi_ref, win_ref, wint_ref, woutt_ref, wout_ref = rest[len(after_ops):]
        *_, abr, abi, _, _, qr, qi = _ssm_zoh(ar_ref[...], ai_ref[...], ldt_ref[...])
        abr_ref[...] = abr
        abi_ref[...] = abi
        b_r, b_i = br_ref[...], bi_ref[...]
        bbr = qr * b_r - qi * b_i
        bbi = qr * b_i + qi * b_r
        mask = _ssm_group_mask()
        state = lax.broadcasted_iota(jnp.int32, (SSM_STATE, SSM_CH), 0)
        col = lax.broadcasted_iota(jnp.int32, (SSM_STATE, SSM_CH), 1)
        every_group = (col % SSM_STATE == state).astype(BF16)

        def spread(x):
            return jnp.where(mask, _dot(x, every_group), 0.0)

        for d in range(2):
            rows = slice(d * SSM_HALF, (d + 1) * SSM_HALF)
            for half, x_in, x_out in ((0, bbr[rows], cr_ref[rows, :]), (1, bbi[rows], -ci_ref[rows, :])):
                cols = slice(half * SSM_CH, (half + 1) * SSM_CH)
                m_in, m_out = spread(x_in), spread(x_out)
                win_ref[d, :, cols] = m_in.astype(BF16)
                wint_ref[d, cols, :] = m_in.T.astype(BF16)
                woutt_ref[d, :, cols] = m_out.astype(BF16)
                wout_ref[d, cols, :] = m_out.T.astype(BF16)

    vm = BS(memory_space=pltpu.VMEM)
    vec = SDS((SSM_ROWS, SSM_STATE), F32)
    wide = SDS((2, SSM_HALF, 2 * SSM_CH), BF16)
    tall = SDS((2, 2 * SSM_CH, SSM_HALF), BF16)
    return pl.pallas_call(body, out_shape=(vec, vec, wide, tall, wide, tall), in_specs=[vm] * 7 + after_specs,
                          out_specs=(vm,) * 6, name="ssm_prep",
                          compiler_params=_params())(a_r, a_i, ldt, b_r, b_i, c_r, c_i, *after_ops)


def _ssm_prep_bwd(a_r, a_i, ldt, b_r, b_i, g_abr, g_abi, d_win, d_woutt):
    def body(ar_ref, ai_ref, ldt_ref, br_ref, bi_ref, gabr_ref, gabi_ref, dwin_ref, dwoutt_ref,
             dar_ref, dai_ref, dldt_ref, dbr_ref, dbi_ref, dcr_ref, dci_ref):
        a_r, a_i = ar_ref[...], ai_ref[...]
        dt, mag, cs, sn, abr, abi, den, nr, qr, qi = _ssm_zoh(a_r, a_i, ldt_ref[...])
        mask = _ssm_group_mask()
        col = lax.broadcasted_iota(jnp.int32, (SSM_CH, SSM_STATE), 0)
        state = lax.broadcasted_iota(jnp.int32, (SSM_CH, SSM_STATE), 1)
        own_state = (col % SSM_STATE == state).astype(BF16)

        def pick(dense):
            m = jnp.where(mask, dense, 0.0)
            hi = m.astype(BF16)
            lo = m - hi.astype(F32)
            return _dot(hi, own_state) + _dot(lo, own_state)

        def picked(ref, half):
            cols = slice(half * SSM_CH, (half + 1) * SSM_CH)
            return jnp.concatenate([pick(ref[d, :, cols]) for d in range(2)], axis=0)

        g_r, g_i = picked(dwin_ref, 0), picked(dwin_ref, 1)
        dcr_ref[...] = picked(dwoutt_ref, 0)
        dci_ref[...] = -picked(dwoutt_ref, 1)
        b_r, b_i = br_ref[...], bi_ref[...]
        dbr_ref[...] = g_r * qr + g_i * qi
        dbi_ref[...] = g_i * qr - g_r * qi
        gqr = g_r * b_r + g_i * b_i
        gqi = g_i * b_r - g_r * b_i
        g_nr_num = gqr / den
        g_ni_num = gqi / den
        g_den = -(gqr * qr + gqi * qi) / den
        g_nr = g_nr_num * a_r - g_ni_num * a_i
        g_abi = g_nr_num * a_i + g_ni_num * a_r
        d_ar = g_nr_num * nr + g_ni_num * abi + 2.0 * a_r * g_den
        d_ai = g_nr_num * abi - g_ni_num * nr + 2.0 * a_i * g_den
        g_abr = gabr_ref[...] + g_nr
        g_abi = gabi_ref[...] + g_abi
        g_mag = g_abr * cs + g_abi * sn
        g_ang = mag * (g_abi * cs - g_abr * sn)
        g_e = g_mag * mag
        d_ar = d_ar + g_e * dt
        d_ai = d_ai + g_ang * dt
        g_dt = g_e * a_r + g_ang * a_i
        dar_ref[...] = d_ar
        dai_ref[...] = d_ai
        dldt_ref[...] = g_dt * dt

    vm = BS(memory_space=pltpu.VMEM)
    vec = SDS((SSM_ROWS, SSM_STATE), F32)
    return pl.pallas_call(body, out_shape=(vec,) * 7, in_specs=[vm] * 9, out_specs=(vm,) * 7, name="ssm_prep_bwd",
                          compiler_params=_params())(a_r, a_i, ldt, b_r, b_i, g_abr, g_abi, d_win, d_woutt)


SCAN_ROWS = 512
SCAN_SUB = 128


def _ssm_scan(name, inp, w1, a_r, a_i, w2, reverse):
    t = inp.shape[0]
    rows = min(SCAN_ROWS, t)
    n = t // rows
    n_sub = rows // SCAN_SUB
    ch = SSM_CH
    at = (lambda i: (n - 1 - i, 0)) if reverse else (lambda i: (i, 0))

    def body(in_ref, w1_ref, ar_ref, ai_ref, w2_ref, sb_ref, out_ref, cr_ref, ci_ref, k_ref, st_ref):
        i = pl.program_id(0)

        @pl.when(i == 0)
        def _():
            ar8 = jnp.broadcast_to(ar_ref[...], (8, ch))
            ai8 = jnp.broadcast_to(ai_ref[...], (8, ch))
            row = lax.broadcasted_iota(jnp.int32, (8, ch), 0)
            rank = (7 - row) if reverse else row
            powers = [(ar8, ai8)]
            for _ in range(7):
                p_r, p_i = powers[-1]
                powers.append((p_r * ar8 - p_i * ai8, p_r * ai8 + p_i * ar8))
            zero = jnp.zeros((8, ch), F32)
            for slot, k in enumerate((1, 2, 4)):
                k_ref[2 * slot] = jnp.where(rank >= k, powers[k - 1][0], zero)
                k_ref[2 * slot + 1] = jnp.where(rank >= k, powers[k - 1][1], zero)
            carry_r, carry_i = zero, zero
            for j in range(8):
                carry_r = jnp.where(rank == j, powers[j][0], carry_r)
                carry_i = jnp.where(rank == j, powers[j][1], carry_i)
            k_ref[6] = carry_r
            k_ref[7] = carry_i
            cr_ref[...] = zero
            ci_ref[...] = zero

        def group(r0, carry):
            c_r, c_i = carry
            x_r = st_ref[pl.ds(r0, 8), 0:ch]
            x_i = st_ref[pl.ds(r0, 8), ch:2 * ch]
            for slot, k in enumerate((1, 2, 4)):
                shift = (8 - k) if reverse else k
                s_r = pltpu.roll(x_r, shift, 0)
                s_i = pltpu.roll(x_i, shift, 0)
                m_r, m_i = k_ref[2 * slot], k_ref[2 * slot + 1]
                x_r, x_i = x_r + m_r * s_r - m_i * s_i, x_i + m_r * s_i + m_i * s_r
            p_r, p_i = k_ref[6], k_ref[7]
            x_r, x_i = x_r + p_r * c_r - p_i * c_i, x_i + p_r * c_i + p_i * c_r
            st_ref[pl.ds(r0, 8), 0:ch] = x_r
            st_ref[pl.ds(r0, 8), ch:2 * ch] = x_i
            last = 0 if reverse else 7
            return (jnp.broadcast_to(x_r[last:last + 1, :], (8, ch)), jnp.broadcast_to(x_i[last:last + 1, :], (8, ch)))

        carry = (cr_ref[...], ci_ref[...])
        for sc in (range(n_sub - 1, -1, -1) if reverse else range(n_sub)):
            part = pl.ds(sc * SCAN_SUB, SCAN_SUB)
            st_ref[part, :] = _dot(in_ref[part, :], w1_ref[...])
            for gi in range(SCAN_SUB // 8):
                g = (SCAN_SUB // 8 - 1 - gi) if reverse else gi
                carry = group(sc * SCAN_SUB + g * 8, carry)
            states = st_ref[part, :].astype(BF16)
            sb_ref[part, :] = states
            out_ref[part, :] = _dot(states, w2_ref[...])
        cr_ref[...] = carry[0]
        ci_ref[...] = carry[1]

    return pl.pallas_call(
        body, out_shape=(SDS((t, 2 * ch), BF16), SDS((t, D_SSM), F32)), grid=(n,),
        in_specs=[BS((rows, D_SSM), at), BS((D_SSM, 2 * ch), lambda i: (0, 0)), BS((1, ch), lambda i: (0, 0)),
                  BS((1, ch), lambda i: (0, 0)), BS((2 * ch, D_SSM), lambda i: (0, 0))],
        out_specs=(BS((rows, 2 * ch), at), BS((rows, D_SSM), at)),
        scratch_shapes=[pltpu.VMEM((8, ch), F32), pltpu.VMEM((8, ch), F32), pltpu.VMEM((8, 8, ch), F32),
                        pltpu.VMEM((rows, 2 * ch), F32)],
        name=name, compiler_params=_params("arbitrary"))(inp, w1, a_r, a_i, w2)


DA_ROWS = 1024


def _ssm_param_grads(name, lam, states, u, dy, reverse, after=()):
    t = lam.shape[0]
    rows = min(DA_ROWS, t)
    n = t // rows
    halo_rows = 16
    nb = rows // halo_rows
    ch = SSM_CH
    if reverse:
        halo_at = lambda i: (jnp.minimum((i + 1) * nb, t // halo_rows - 1), 0)
    else:
        halo_at = lambda i: (jnp.maximum(i * nb - 1, 0), 0)

    after_ops, after_specs = _after_operands(after)

    def body(lam_ref, x_ref, halo_ref, u_ref, dy_ref, *rest):
        dr_ref, di_ref, dwin_ref, dwoutt_ref = rest[len(after_ops):]
        i = pl.program_id(0)

        @pl.when(i == 0)
        def _():
            dr_ref[...] = jnp.zeros_like(dr_ref)
            di_ref[...] = jnp.zeros_like(di_ref)
            dwin_ref[...] = jnp.zeros_like(dwin_ref)
            dwoutt_ref[...] = jnp.zeros_like(dwoutt_ref)

        dwin_ref[...] += _dot(u_ref[...], lam_ref[...], TN)
        dwoutt_ref[...] += _dot(dy_ref[...], x_ref[...], TN)
        row = lax.broadcasted_iota(jnp.int32, (rows, ch), 0)
        if reverse:
            edge, shift, h_row, live = rows - 1, rows - 1, 0, i < n - 1
        else:
            edge, shift, h_row, live = 0, 1, halo_rows - 1, i > 0

        def neighbour(lo):
            halo = halo_ref[:, lo:lo + ch].astype(F32)[h_row:h_row + 1]
            halo = jnp.where(live, halo, 0.0)
            x = x_ref[:, lo:lo + ch].astype(F32)
            return jnp.where(row == edge, jnp.broadcast_to(halo, (rows, ch)), pltpu.roll(x, shift, 0))

        xp_r, xp_i = neighbour(0), neighbour(ch)
        l_r, l_i = lam_ref[:, 0:ch].astype(F32), lam_ref[:, ch:2 * ch].astype(F32)
        dr_ref[...] += jnp.sum(l_r * xp_r + l_i * xp_i, axis=0, keepdims=True)
        di_ref[...] += jnp.sum(l_i * xp_r - l_r * xp_i, axis=0, keepdims=True)

    blk = BS((rows, 2 * ch), lambda i: (i, 0))
    thin = BS((rows, D_SSM), lambda i: (i, 0))
    vec = BS((1, ch), lambda i: (0, 0))
    mat = BS((D_SSM, 2 * ch), lambda i: (0, 0))
    return pl.pallas_call(
        body, out_shape=(SDS((1, ch), F32), SDS((1, ch), F32), SDS((D_SSM, 2 * ch), F32), SDS((D_SSM, 2 * ch), F32)),
        grid=(n,), in_specs=[blk, blk, BS((halo_rows, 2 * ch), halo_at), thin, thin] + after_specs,
        out_specs=(vec, vec, mat, mat),
        name=name, compiler_params=_params("arbitrary"))(lam, states, states, u, dy, *after_ops)


GELU_C = math.sqrt(2.0 / math.pi)
GELU_K = 0.044715


def _ssm_combine(proj, y_fwd, y_bwd, d_skip, tm, after=()):
    t = proj.shape[0]
    after_ops, after_specs = _after_operands(after)

    def body(s_ref, yf_ref, yb_ref, d_ref, *rest):
        yt_ref, g_ref = rest[len(after_ops):]
        y = s_ref[...] * d_ref[...] + yf_ref[...] + yb_ref[...]
        yt_ref[...] = y
        th = jnp.tanh(GELU_C * (y + GELU_K * y * y * y))
        g_ref[...] = (0.5 * y * (1.0 + th)).astype(BF16)

    blk = BS((tm, D_SSM), lambda i: (i, 0))
    return pl.pallas_call(
        body, out_shape=(SDS((t, D_SSM), F32), SDS((t, D_SSM), BF16)), grid=(t // tm,),
        in_specs=[BS((tm, D_SSM), lambda i: (i, D_POOL // D_SSM)), blk, blk, BS((1, D_SSM), lambda i: (0, 0))] + after_specs,
        out_specs=(blk, blk), name="ssm_combine",
        compiler_params=_params("parallel"))(proj, y_fwd, y_bwd, d_skip, *after_ops)


def _ssm_ds(proj, d_yt, du_fwd, du_bwd, d_skip, tm):
    t = proj.shape[0]

    def body(s_ref, dy_ref, duf_ref, dub_ref, d_ref, ds_ref, dd_ref):
        i = pl.program_id(0)
        dy = dy_ref[...]
        ds_ref[...] = (dy * d_ref[...] + duf_ref[...] + dub_ref[...]).astype(BF16)

        @pl.when(i == 0)
        def _():
            dd_ref[...] = jnp.zeros_like(dd_ref)

        dd_ref[...] += jnp.sum(dy * s_ref[...], axis=0, keepdims=True)

    blk = BS((tm, D_SSM), lambda i: (i, 0))
    vec = BS((1, D_SSM), lambda i: (0, 0))
    return pl.pallas_call(
        body, out_shape=(SDS((t, D_SSM), BF16), SDS((1, D_SSM), F32)), grid=(t // tm,),
        in_specs=[BS((tm, D_SSM), lambda i: (i, D_POOL // D_SSM)), blk, blk, blk, vec],
        out_specs=(blk, vec), name="ssm_ds", compiler_params=_params("arbitrary"))(proj, d_yt, du_fwd, du_bwd, d_skip)


GP_BLOCK = (D_POOL + D_SSM) // 256
GS_BLOCK = GP_BLOCK + D_MODEL // 256


def _merge_specs(tm):
    return [BS((tm, D_POOL), lambda s, i: (i, 0)), BS((tm, D_SSM), lambda s, i: (i, 0)),
            BS((None, D_POOL, 256), lambda s, i: (s, 0, 0)), BS((None, D_SSM, 256), lambda s, i: (s, 2, 0)),
            BS((None, D_SSM, 256), lambda s, i: (s, 3, 0)),
            BS((tm, 256), lambda s, i: (i, GP_BLOCK + s)), BS((tm, 256), lambda s, i: (i, GS_BLOCK + s))]


def _mixer_merge(ms, yssm, w_e, proj, tm):
    t = ms.shape[0]

    def body(ms_ref, y_ref, wpp_ref, wgv_ref, wgg_ref, gp_ref, gs_ref, o_ref):
        zp = _dot(ms_ref[...], wpp_ref[...])
        yv = y_ref[...]
        zv = _dot(yv, wgv_ref[...])
        zg = _dot(yv, wgg_ref[...])
        o_ref[...] = (_sigmoid(gp_ref[...]) * zp + _sigmoid(gs_ref[...]) * zv * _sigmoid(zg)).astype(BF16)

    col = BS((tm, 256), lambda s, i: (i, s))
    return pl.pallas_call(
        body, out_shape=SDS((t, D_MODEL), BF16), grid=(N_SHARD, t // tm), in_specs=_merge_specs(tm), out_specs=col,
        name="mixer_merge", compiler_params=_params("parallel", "parallel"))(ms, yssm, w_e, w_e, w_e, proj, proj)


def _mixer_merge_bwd(ms, yssm, w_e, proj, dmerged, tm):
    t = ms.shape[0]

    def body(ms_ref, y_ref, wpp_ref, wgv_ref, wgg_ref, gp_ref, gs_ref, dm_ref,
             dgp_ref, dgs_ref, dzp_ref, dzv_ref, dzg_ref):
        zp = _dot(ms_ref[...], wpp_ref[...])
        yv = y_ref[...]
        zv = _dot(yv, wgv_ref[...])
        zg = _dot(yv, wgg_ref[...])
        dm = dm_ref[...].astype(F32)
        sp, ss, sg = _sigmoid(gp_ref[...]), _sigmoid(gs_ref[...]), _sigmoid(zg)
        dgp_ref[...] = (dm * zp * sp * (1.0 - sp)).astype(BF16)
        dgs_ref[...] = (dm * zv * sg * ss * (1.0 - ss)).astype(BF16)
        dzp_ref[...] = (dm * sp).astype(BF16)
        dz = dm * ss
        dzv_ref[...] = (dz * sg).astype(BF16)
        dzg_ref[...] = (dz * zv * sg * (1.0 - sg)).astype(BF16)

    col = BS((tm, 256), lambda s, i: (i, s))
    shape = SDS((t, D_MODEL), BF16)
    return pl.pallas_call(
        body, out_shape=(shape,) * 5, grid=(N_SHARD, t // tm), in_specs=_merge_specs(tm) + [col],
        out_specs=(col,) * 5, name="mixer_merge_bwd",
        compiler_params=_params("parallel", "parallel"))(ms, yssm, w_e, w_e, w_e, proj, proj, dmerged)


def _mixer_dw(ms, yssm, dzp, dzv, dzg, tm):
    t = ms.shape[0]
    n_t = t // tm

    def body(ms_ref, y_ref, dzp_ref, dzv_ref, dzg_ref, o_ref, acc):
        i = pl.program_id(1)

        @pl.when(i == 0)
        def _():
            acc[...] = jnp.zeros_like(acc)

        yv = y_ref[...]
        acc[0:D_POOL, :] += _dot(ms_ref[...], dzp_ref[...], TN)
        acc[D_POOL:D_POOL + D_SSM, :] += _dot(yv, dzv_ref[...], TN)
        acc[D_POOL + D_SSM:, :] += _dot(yv, dzg_ref[...], TN)

        @pl.when(i == n_t - 1)
        def _():
            o_ref[...] = acc[...].astype(BF16)

    col = BS((tm, 256), lambda s, i: (i, s))
    return pl.pallas_call(
        body, out_shape=SDS((N_SHARD, 1024, 256), BF16), grid=(N_SHARD, n_t),
        in_specs=[BS((tm, D_POOL), lambda s, i: (i, 0)), BS((tm, D_SSM), lambda s, i: (i, 0)), col, col, col],
        out_specs=BS((None, 1024, 256), lambda s, i: (s, 0, 0)), scratch_shapes=[pltpu.VMEM((1024, 256), F32)],
        name="mixer_dw", compiler_params=_params("parallel", "arbitrary"))(ms, yssm, dzp, dzv, dzg)


def _mixer_dx(dzp, dzv, dzg, w_e, y_total, tm):
    t = dzp.shape[0]

    def body(dzp_ref, dzv_ref, dzg_ref, wpp_ref, wgv_ref, wgg_ref, yt_ref, dms_ref, dy_ref, acc_ms, acc_y):
        s = pl.program_id(1)

        @pl.when(s == 0)
        def _():
            acc_ms[...] = jnp.zeros_like(acc_ms)
            acc_y[...] = jnp.zeros_like(acc_y)

        acc_ms[...] += _dot(dzp_ref[...], wpp_ref[...], NT)
        acc_y[...] += _dot(dzv_ref[...], wgv_ref[...], NT) + _dot(dzg_ref[...], wgg_ref[...], NT)

        @pl.when(s == N_SHARD - 1)
        def _():
            dms_ref[...] = acc_ms[...].astype(BF16)
            y = yt_ref[...]
            inner = GELU_C * (y + GELU_K * y * y * y)
            th = jnp.tanh(inner)
            dgelu = 0.5 * (1.0 + th) + 0.5 * y * (1.0 - th * th) * GELU_C * (1.0 + 3.0 * GELU_K * y * y)
            dy_ref[...] = acc_y[...] * dgelu

    col = BS((tm, 256), lambda i, s: (i, s))
    return pl.pallas_call(
        body, out_shape=(SDS((t, D_POOL), BF16), SDS((t, D_SSM), F32)), grid=(t // tm, N_SHARD),
        in_specs=[col, col, col, BS((None, D_POOL, 256), lambda i, s: (s, 0, 0)),
                  BS((None, D_SSM, 256), lambda i, s: (s, 2, 0)), BS((None, D_SSM, 256), lambda i, s: (s, 3, 0)),
                  BS((tm, D_SSM), lambda i, s: (i, 0))],
        out_specs=(BS((tm, D_POOL), lambda i, s: (i, 0)), BS((tm, D_SSM), lambda i, s: (i, 0))),
        scratch_shapes=[pltpu.VMEM((tm, D_POOL), F32), pltpu.VMEM((tm, D_SSM), F32)],
        name="mixer_dx", compiler_params=_params("parallel", "arbitrary"))(dzp, dzv, dzg, w_e, w_e, w_e, y_total)


def _attn_probs(q_h, k_h):
    s = _dot(q_h, k_h, NT) * (1.0 / math.sqrt(HEAD_DIM))
    e = jnp.exp(s - jnp.max(s, axis=-1, keepdims=True))
    return e / jnp.sum(e, axis=-1, keepdims=True)


def _attn_fwd(q, kv, tm):
    t = q.shape[0]
    m = kv.shape[0]

    def body(q_ref, kv_ref, o_ref):
        for hd in range(N_HEADS):
            lo = hd * HEAD_DIM
            p = _attn_probs(q_ref[:, lo:lo + HEAD_DIM], kv_ref[:, lo:lo + HEAD_DIM])
            o_ref[:, lo:lo + HEAD_DIM] = _dot(p, kv_ref[:, D_MODEL + lo:D_MODEL + lo + HEAD_DIM]).astype(BF16)

    return pl.pallas_call(
        body, out_shape=SDS((t, D_MODEL), BF16), grid=(t // tm,),
        in_specs=[BS((tm, D_MODEL), lambda i: (i, 0)), BS((m, 2 * D_MODEL), lambda i: (0, 0))],
        out_specs=BS((tm, D_MODEL), lambda i: (i, 0)), name="attn_fwd", compiler_params=_params("parallel"))(q, kv)


def _attn_bwd(q, kv, d_o, tm):
    t = q.shape[0]
    m = kv.shape[0]

    def body(q_ref, kv_ref, do_ref, dq_ref, dkv_ref):
        i = pl.program_id(0)

        @pl.when(i == 0)
        def _():
            dkv_ref[...] = jnp.zeros_like(dkv_ref)

        for hd in range(N_HEADS):
            lo = hd * HEAD_DIM
            q_h = q_ref[:, lo:lo + HEAD_DIM]
            k_h = kv_ref[:, lo:lo + HEAD_DIM]
            v_h = kv_ref[:, D_MODEL + lo:D_MODEL + lo + HEAD_DIM]
            do_h = do_ref[:, lo:lo + HEAD_DIM]
            p = _attn_probs(q_h, k_h)
            dkv_ref[:, D_MODEL + lo:D_MODEL + lo + HEAD_DIM] += _dot(p, do_h, TN)
            dp = _dot(do_h, v_h, NT)
            ds = p * (dp - jnp.sum(dp * p, axis=-1, keepdims=True)) * (1.0 / math.sqrt(HEAD_DIM))
            dq_ref[:, lo:lo + HEAD_DIM] = _dot(ds, k_h).astype(BF16)
            dkv_ref[:, lo:lo + HEAD_DIM] += _dot(ds, q_h, TN)

    row = BS((tm, D_MODEL), lambda i: (i, 0))
    full = BS((m, 2 * D_MODEL), lambda i: (0, 0))
    return pl.pallas_call(
        body, out_shape=(SDS((t, D_MODEL), BF16), SDS((m, 2 * D_MODEL), F32)), grid=(t // tm,),
        in_specs=[row, full, row], out_specs=(row, full), name="attn_bwd",
        compiler_params=_params("arbitrary"))(q, kv, d_o)


TRANSPOSED = ("ffn1_w_gate", "ffn1_w_up", "ffn2_w_gate", "ffn2_w_up", "w_in")
GATHER_PHASES = {"f1a": (("ffn1_w_gate", "ffn1_w_up"),),
                 "f1b": (("ffn1_w_down",),),
                 "win": (("w_in",),),
                 "mix": (("w_mix_out", "w_q", "w_xo"), ("w_kv",), ("w_pool_proj", "w_glu_val", "w_glu_gate")),
                 "f2": (("ffn2_w_gate", "ffn2_w_up", "ffn2_w_down"),)}
REDUCE_GROUPS = (("ffn2_w_gate", "ffn2_w_up", "ffn2_w_down"), ("w_xo",), ("w_q",), ("w_kv",), ("w_mix_out",),
                 ("w_pool_proj", "w_glu_val", "w_glu_gate"), ("w_in",), ("ffn1_w_gate", "ffn1_w_up", "ffn1_w_down"))
SMALL = ("ffn1_norm", "mix_norm", "pool_w", "pool_scale", "ssm_a_re", "ssm_a_im", "ssm_log_dt", "ssm_b_re",
         "ssm_b_im", "ssm_c_re", "ssm_c_im", "ssm_d", "xattn_norm", "mem_norm", "ffn2_norm", "final_norm")
WEIGHTS = ("ffn1_norm", "ffn1_w_gate", "ffn1_w_up", "ffn1_w_down", "mix_norm", "w_in", "pool_w", "pool_scale",
           "w_pool_proj", "ssm_a_re", "ssm_a_im", "ssm_log_dt", "ssm_b_re", "ssm_b_im", "ssm_c_re", "ssm_c_im",
           "ssm_d", "w_glu_val", "w_glu_gate", "w_mix_out", "xattn_norm", "mem_norm", "w_q", "w_kv", "w_xo",
           "ffn2_norm", "ffn2_w_gate", "ffn2_w_up", "ffn2_w_down", "final_norm")


def _small_view(a, n):
    return jnp.swapaxes(a, 3, 4) if n in ("ssm_b_re", "ssm_b_im") else a


def _device_step(x, mem, target, wts, sp, reducer=None):
    t = x.shape[0]
    tm = min(TM, t)
    g = {}

    first_gather = wts.start("f1a")
    u1 = _rmsnorm("norm_ffn1", x, sp["ffn1_norm"], tm, after=first_gather)

    def per_channel(a):
        a = a.reshape(2 * SSM_GROUPS, 1, -1)
        return jnp.broadcast_to(a, (2 * SSM_GROUPS, SSM_GROUP, a.shape[-1])).reshape(SSM_ROWS, a.shape[-1])

    ssm_a = per_channel(sp["ssm_a_re"]), per_channel(sp["ssm_a_im"]), per_channel(sp["ssm_log_dt"])
    ssm_b = sp["ssm_b_re"].reshape(SSM_ROWS, SSM_STATE), sp["ssm_b_im"].reshape(SSM_ROWS, SSM_STATE)
    abr, abi, w_in_s, w_in_s_t, w_out_s_t, w_out_s = _ssm_prep(
        *ssm_a, *ssm_b, sp["ssm_c_re"].reshape(SSM_ROWS, SSM_STATE), sp["ssm_c_im"].reshape(SSM_ROWS, SSM_STATE),
        after=first_gather)
    first_rows = (2, SSM_GROUPS, SSM_GROUP, SSM_STATE)
    a_r = abr.reshape(first_rows)[:, :, 0].reshape(2, 1, SSM_CH)
    a_i = abi.reshape(first_rows)[:, :, 0].reshape(2, 1, SSM_CH)
    mem_n = _rmsnorm("norm_mem", mem, sp["mem_norm"], mem.shape[0], after=first_gather)

    (w_gu,) = wts.finish("f1a", [u1, w_in_s, w_in_s_t, w_out_s, w_out_s_t, a_r, a_i, mem_n])
    w_f1 = {"gate": (w_gu, FFN_GATE), "up": (w_gu, FFN_UP)}
    down_gather = wts.start("f1b", [w_gu])
    g1, up1, a1 = _ffn_up("ffn1_up", u1, w_f1, tm, after=wts.start("win", down_gather))
    (w_dn,) = wts.finish("f1b", [a1])
    w_f1["down"] = (w_dn, 0)
    h1, u2 = _ffn_down("ffn1_down", a1, w_f1, x, tm, next_gain=sp["mix_norm"])

    (w_in_g,) = wts.finish("win", [u2])
    w_in_t = w_in_g.reshape(D_FF, D_MODEL)
    proj = _mm("mix_in", [(u2, BS((tm, D_MODEL), lambda j, i: (i, 0)), w_in_t, BS((D_FF // 2, D_MODEL), lambda j, i: (j, 0)), NT)],
               grid=(2, t // tm), out_shape=SDS((t, D_FF), F32), out_spec=BS((tm, D_FF // 2), lambda j, i: (i, j)),
               after=wts.start("f2", wts.start("mix", [w_in_g])))
    pooled, mixed, ms = _pool_fwd(proj, sp["pool_w"][0], sp["pool_scale"])

    s_in = proj[:, D_POOL:D_POOL + D_SSM].astype(BF16)
    states, y_dirs = [], []
    for dr in range(2):
        st, yd = _ssm_scan(f"ssm_scan_fwd{dr}", s_in, w_in_s[dr], a_r[dr], a_i[dr], w_out_s[dr], reverse=(dr == 1))
        states.append(st)
        y_dirs.append(yd)
    w_sq, w_kv, w_e = wts.finish("mix", y_dirs)
    w_mo, w_q, w_xo = (w_sq[:, 256 * k:256 * (k + 1)].reshape(D_MODEL, D_MODEL) for k in range(3))
    w_d = w_kv[:, None]
    y_total, yssm = _ssm_combine(proj, y_dirs[0], y_dirs[1], sp["ssm_d"], tm)

    merged = _mixer_merge(ms, yssm, w_e, proj, tm)
    h2, u3 = _mm_resid_norm("mix_out", merged, w_mo, h1, sp["xattn_norm"], tm)

    q = _plain_mm("attn_q", u3, w_q, NN, BF16, tm)
    n_mem = mem.shape[0]
    kv = _mm("attn_kv", [(mem_n, BS((n_mem, D_MODEL), lambda s: (0, 0)), w_d, BS((None, None, D_MODEL, 512), lambda s: (s, 0, 0, 0)), NN)],
             grid=(N_SHARD,), out_shape=SDS((n_mem, 2 * D_MODEL), BF16), out_spec=BS((n_mem, 512), lambda s: (0, s)))
    o = _attn_fwd(q, kv, tm)
    h3, u4 = _mm_resid_norm("attn_out", o, w_xo, h2, sp["ffn2_norm"], tm)

    (w_2,) = wts.finish("f2", [u4])
    w_f2 = {"gate": (w_2, FFN_GATE), "up": (w_2, FFN_UP), "down": (w_2, FFN_DOWN)}
    g2, up2, a2 = _ffn_up("ffn2_up", u4, w_f2, tm)
    loss, dh4, dh4_b, g["final_norm"] = _ffn_down("ffn2_down", a2, w_f2, h3, tm,
                                                  head=(sp["final_norm"].reshape(1, D_MODEL), target))

    dg2, dup2 = _ffn_bwd_act("ffn2_bwd_act", dh4_b, w_f2, g2, up2, tm)
    dw_f2 = _ffn_dw("ffn2_dw", u4, dg2, dup2, a2, dh4_b, tm)
    dh3, dh3_b, g["ffn2_norm"] = _ffn_dx("ffn2_dx", dg2, dup2, w_f2, h3, sp["ffn2_norm"], dh4, tm)

    d_o = _plain_mm("attn_out_dx", dh3_b, w_xo, NT, BF16, tm)
    dw_xo = _dw_mm("attn_out_dw", o, dh3_b, tm)
    dq, dkv = _attn_bwd(q, kv, d_o, tm)
    dw_q = _dw_mm("attn_q_dw", u3, dq, tm)
    dh2, dh2_b, g["xattn_norm"] = _mm_norm_bwd("attn_q_dx", dq, w_q, NT, h2, sp["xattn_norm"], dh3, tm)
    dw_kv = _mm("attn_kv_dw", [(mem_n, BS((n_mem, D_MODEL), lambda s: (0, 0)), dkv, BS((n_mem, 512), lambda s: (0, s)), TN)],
                grid=(N_SHARD,), out_shape=SDS((N_SHARD, D_MODEL, 512), BF16), out_spec=BS((None, D_MODEL, 512), lambda s: (s, 0, 0)))
    dmem_n = _mm("attn_kv_dx", [(dkv, BS((n_mem, 512), lambda s: (0, s)), w_d, BS((None, None, D_MODEL, 512), lambda s: (s, 0, 0, 0)), NT)],
                 grid=(N_SHARD,), red_axis=0, out_shape=SDS((n_mem, D_MODEL), F32), out_spec=BS((n_mem, D_MODEL), lambda s: (0, 0)))
    _, _, g["mem_norm"] = _rmsnorm_bwd("norm_mem_bwd", mem, sp["mem_norm"], dmem_n, None, n_mem)

    square = (N_SHARD, D_MODEL // N_SHARD, D_MODEL)
    early = [dw_f2.reshape(N_SHARD, 3 * FF_SH, D_MODEL), dw_xo.reshape(square), dw_q.reshape(square), dw_kv]
    swapping = reducer.swap_start("a1", early) if reducer is not None else []
    dmerged = _plain_mm("mix_out_dx", dh2_b, w_mo, NT, BF16, tm, after=swapping)
    dw_mo = _dw_mm("mix_out_dw", merged, dh2_b, tm)
    d_gp, d_gs, dzp, dzv, dzg = _mixer_merge_bwd(ms, yssm, w_e, proj, dmerged, tm)
    dw_e = _mixer_dw(ms, yssm, dzp, dzv, dzg, tm)
    d_ms, d_yt = _mixer_dx(dzp, dzv, dzg, w_e, y_total, tm)
    dp, d_scale, d_pw = _pool_bwd(d_ms, mixed, pooled, sp["pool_w"][0], sp["pool_scale"])
    g["pool_scale"] = d_scale
    g["pool_w"] = d_pw[None]

    d_yt_b = d_yt.astype(BF16)
    du_dirs, lams = [], []
    for dr in range(2):
        lam, du = _ssm_scan(f"ssm_scan_bwd{dr}", d_yt_b, w_out_s_t[dr], a_r[dr], -a_i[dr], w_in_s_t[dr], reverse=(dr == 0))
        du_dirs.append(du)
        lams.append(lam)
    ds, g["ssm_d"] = _ssm_ds(proj, d_yt, du_dirs[0], du_dirs[1], sp["ssm_d"], tm)

    d_proj = jnp.concatenate([dp, ds, d_gp, d_gs], axis=1)
    tw = min(2 * tm, t)
    dw_in_t = _mm("mix_in_dw", [(d_proj, BS((tw, D_FF // 2), lambda j, i: (i, j)), u2, BS((tw, D_MODEL), lambda j, i: (i, 0)), TN)],
                  grid=(2, t // tw), red_axis=1, out_shape=SDS((D_FF, D_MODEL), BF16), out_spec=BS((D_FF // 2, D_MODEL), lambda j, i: (j, 0)))
    dh1, dh1_b, g["mix_norm"] = _mm_norm_bwd("mix_in_dx", d_proj, w_in_t, NN, h1, sp["mix_norm"], dh2, tm)

    early += [dw_mo.reshape(square), dw_e, dw_in_t.reshape(N_SHARD, FF_SH, D_MODEL)]
    g["final_norm"] = g["final_norm"].reshape(D_MODEL)

    travelling = reducer.start("a", early[4:], swapped=["a1"], after=list(g.values())) if reducer is not None else []
    d_abr, d_abi, d_cm, d_bm = [], [], [], []
    for dr in range(2):
        da_r, da_i, d_win, d_woutt = _ssm_param_grads(f"ssm_param_grads{dr}", lams[dr], states[dr], s_in, d_yt_b,
                                                      reverse=(dr == 1), after=travelling)
        d_abr.append(da_r)
        d_abi.append(da_i)
        d_bm.append(d_win)
        d_cm.append(d_woutt)

    def first_channel(da):
        da = jnp.stack(da).reshape(2, SSM_GROUPS, 1, SSM_STATE)
        return jnp.pad(da, ((0, 0), (0, 0), (0, SSM_GROUP - 1), (0, 0))).reshape(SSM_ROWS, SSM_STATE)

    d_ar, d_ai, d_ldt, d_br, d_bi, d_cr, d_ci = _ssm_prep_bwd(
        *ssm_a, *ssm_b, first_channel(d_abr), first_channel(d_abi), jnp.stack(d_bm), jnp.stack(d_cm))
    per_group = (2 * SSM_GROUPS, SSM_GROUP * SSM_STATE)
    g["ssm_a_re"] = d_ar.reshape(2 * SSM_GROUPS, SSM_GROUP, SSM_STATE).sum(axis=1).reshape(sp["ssm_a_re"].shape)
    g["ssm_a_im"] = d_ai.reshape(2 * SSM_GROUPS, SSM_GROUP, SSM_STATE).sum(axis=1).reshape(sp["ssm_a_im"].shape)
    g["ssm_log_dt"] = d_ldt.reshape(per_group).sum(axis=1).reshape(sp["ssm_log_dt"].shape)
    g["ssm_b_re"] = d_br.reshape(sp["ssm_b_re"].shape)
    g["ssm_b_im"] = d_bi.reshape(sp["ssm_b_im"].shape)
    g["ssm_c_re"] = d_cr.reshape(sp["ssm_c_re"].shape)
    g["ssm_c_im"] = d_ci.reshape(sp["ssm_c_im"].shape)
    if reducer is not None:
        travelling = travelling + [d_ar, d_br, d_cr]
    dg1, dup1 = _ffn_bwd_act("ffn1_bwd_act", dh1_b, w_f1, g1, up1, tm, after=travelling)
    dw_f1 = _ffn_dw("ffn1_dw", u1, dg1, dup1, a1, dh1_b, tm).reshape(N_SHARD, 3 * FF_SH, D_MODEL)
    if reducer is not None:
        travelling = reducer.start("b", [dw_f1], after=reducer.finish("a", [dw_f1]))
        travelling = travelling + reducer.join_start("a", after=travelling)
    grad_x, _, g["ffn1_norm"] = _ffn_dx("ffn1_dx", dg1, dup1, w_f1, x, sp["ffn1_norm"], dh1, tm, after=travelling)
    if reducer is not None:
        reducer.finish("b", [grad_x])
        reducer.join_finish("a", [grad_x])
    return loss, grad_x, early + [dw_f1], g


def _mesh_place():
    x, y, c = lax.axis_index("x"), lax.axis_index("y"), lax.axis_index("c")
    chips = [(1 - x, y), (x, 1 - y), (1 - x, 1 - y)]
    return x, y, c, chips


def _remote(src, dst, send_sems, recv_sems, k, to):
    return pltpu.make_async_remote_copy(src_ref=src, dst_ref=dst, send_sem=send_sems.at[k], recv_sem=recv_sems.at[k],
                                        device_id=to, device_id_type=MESH)


def _sibling_swap_halves(tag, grads, after=()):
    n = len(grads)
    after_ops, after_specs = _after_operands(after)

    def body(*refs):
        ins, outs = refs[:n], refs[n + len(after_ops):2 * n + len(after_ops)]
        send_sems, recv_sems = refs[2 * n + len(after_ops):]
        x, y, c, _ = _mesh_place()
        sibling = (x, y, 1 - c)
        copies = []
        for k in range(n):
            half = grads[k].shape[1] // 2
            theirs = pl.ds(pl.multiple_of((1 - c) * half, 16), half)
            cp = _remote(ins[k].at[:, theirs, :], outs[k], send_sems, recv_sems, k, sibling)
            cp.start()
            copies.append(cp)
        for cp in copies:
            cp.wait_recv()
        for cp in copies:
            cp.wait_send()

    hbm = BS(memory_space=pl.ANY)
    return pl.pallas_call(
        body, out_shape=tuple(SDS((g.shape[0], g.shape[1] // 2, g.shape[2]), g.dtype) for g in grads),
        in_specs=[hbm] * n + after_specs, out_specs=(hbm,) * n,
        scratch_shapes=[pltpu.SemaphoreType.DMA((n,)), pltpu.SemaphoreType.DMA((n,))],
        name="reduce_sibling_send_" + tag, compiler_params=_params())(*grads, *after_ops)


def _row_tile(rows, cap=512):
    return max(r for r in range(16, cap + 1, 16) if rows % r == 0)


REDUCE_STEPS = 2


def _chip_presum(tag, grads, gots, c_idx):
    n = len(grads)
    halves = [g.shape[1] // 2 for g in grads]
    tiles = [(h // REDUCE_STEPS, g.shape[2]) for h, g in zip(halves, grads)]

    def body(c_ref, *refs):
        for k in range(n):
            refs[2 * n + k][...] = (refs[k][...].astype(F32) + refs[n + k][...].astype(F32)).astype(BF16)

    mine = [BS((None, None) + tile, lambda s, i, c_ref: (s, c_ref[0], i, 0)) for tile in tiles]
    plain = [BS((None,) + tile, lambda s, i, c_ref: (s, i, 0)) for tile in tiles]
    return list(pl.pallas_call(
        body, out_shape=tuple(SDS((g.shape[0], h, g.shape[2]), BF16) for g, h in zip(grads, halves)),
        grid_spec=pltpu.PrefetchScalarGridSpec(num_scalar_prefetch=1, grid=(N_SHARD, REDUCE_STEPS),
                                               in_specs=mine + plain, out_specs=plain),
        name="reduce_presum_" + tag, compiler_params=_params("parallel", "parallel"))(
            c_idx, *[g.reshape(g.shape[0], 2, h, g.shape[2]) for g, h in zip(grads, halves)], *gots))


HBM_SPEC = BS(memory_space=pltpu.HBM)
SEM_SPEC = BS(memory_space=pltpu.SEMAPHORE)
DATAFLOW = pltpu.SideEffectType.DATAFLOW_SIDE_EFFECTING


def _chip_exchange_copies(parts, lands, send_sems, recv_sems):
    _, _, c, chips = _mesh_place()
    return [_remote(parts[k].at[2 * px + py], lands[k].at[j], send_sems, recv_sems, 3 * k + j, (px, py, c))
            for k in range(len(parts)) for j, (px, py) in enumerate(chips)]


def _gather_copies(shards, lands, send_sems, recv_sems):
    x, y, c, chips = _mesh_place()
    return [_remote(shards[k], lands[k].at[2 * x + y], send_sems, recv_sems, 3 * k + j, (px, py, c))
            for k in range(len(shards)) for j, (px, py) in enumerate(chips)]


def _gather_half_copies(shards, lands, send_sems, recv_sems):
    x, y, c, chips = _mesh_place()
    out = []
    for k in range(len(shards)):
        half = shards[k].shape[0] // 2
        mine = pl.ds(pl.multiple_of(c * half, 16), half)
        for j, (px, py) in enumerate(chips):
            out.append(_remote(shards[k].at[mine, :], lands[k].at[2 * x + y, mine, :], send_sems, recv_sems,
                               3 * k + j, (px, py, c)))
    return out


def _sibling_fill(tag, lands):
    n = len(lands)

    def body(*refs):
        outs = refs[n:2 * n]
        send_sems, recv_sems = refs[2 * n:]
        x, y, c, chips = _mesh_place()
        copies = []
        for k in range(n):
            half = lands[k].shape[1] // 2
            mine = pl.ds(pl.multiple_of(c * half, 16), half)
            for j, (px, py) in enumerate(chips):
                blk = outs[k].at[2 * px + py, mine, :]
                copies.append(_remote(blk, blk, send_sems, recv_sems, 3 * k + j, (x, y, 1 - c)))
        for cp in copies:
            cp.start()
        for cp in copies:
            cp.wait_recv()
        for cp in copies:
            cp.wait_send()

    hbm = BS(memory_space=pl.ANY)
    return list(pl.pallas_call(
        body, out_shape=tuple(SDS(a.shape, a.dtype) for a in lands),
        in_specs=[hbm] * n, out_specs=(hbm,) * n, input_output_aliases={k: k for k in range(n)},
        scratch_shapes=[pltpu.SemaphoreType.DMA((3 * n,)), pltpu.SemaphoreType.DMA((3 * n,))],
        name="gather_fill_" + tag, compiler_params=_params())(*lands))


def _swap_copies(grads, lands, send_sems, recv_sems):
    x, y, c, _ = _mesh_place()
    out = []
    for k in range(len(grads)):
        half = grads[k].shape[1] // 2
        theirs = pl.ds(pl.multiple_of((1 - c) * half, 16), half)
        out.append(_remote(grads[k].at[:, theirs, :], lands[k], send_sems, recv_sems, k, (x, y, 1 - c)))
    return out


def _join_copies(fulls, same, send_sems, recv_sems):
    x, y, c, _ = _mesh_place()
    out = []
    for k in range(len(fulls)):
        half = fulls[k].shape[0] // 2
        mine = fulls[k].at[pl.ds(pl.multiple_of(c * half, 8), half), :]
        out.append(_remote(mine, mine, send_sems, recv_sems, k, (x, y, 1 - c)))
    return out


def _everyone_copies(packs, lands, send_sems, recv_sems):
    x, y, c, _ = _mesh_place()
    out = []
    for k in range(len(packs)):
        for j in range(N_DEV - 1):
            bx, by, bc = (j + 1) >> 2 & 1, (j + 1) >> 1 & 1, (j + 1) & 1
            peer = (x ^ bx, y ^ by, c ^ bc)
            out.append(_remote(packs[k], lands[k].at[4 * x + 2 * y + c], send_sems, recv_sems, (N_DEV - 1) * k + j, peer))
    return out


def _split_start(name, copies, sources, land_shapes, after=(), fanout=3):
    n = len(sources)
    n_land = len(land_shapes)
    m = n + n_land
    n_sems = fanout * n
    after_ops, after_specs = _after_operands(after)

    def body(*refs):
        ins = refs[:n]
        lands = refs[n:m] if n_land else ins
        send_sems, recv_sems = refs[m + len(after_ops)], refs[m + len(after_ops) + 1]
        token = refs[-1]
        for cp in copies(ins, lands, send_sems, recv_sems):
            cp.start()
        token[...] = jnp.zeros_like(token)

    lands = [pltpu.with_memory_space_constraint(lax.empty(s, d), pltpu.HBM) for s, d in land_shapes]
    sources = [pltpu.with_memory_space_constraint(p, pltpu.HBM) for p in sources]
    thru = [pltpu.HBM(a.shape, a.dtype) for a in sources + lands]
    out = pl.pallas_call(
        body, name=name,
        out_shape=(pltpu.SemaphoreType.DMA((n_sems,)), pltpu.SemaphoreType.DMA((n_sems,)), *thru, SDS((8, 128), F32)),
        in_specs=[HBM_SPEC] * m + after_specs,
        out_specs=(SEM_SPEC, SEM_SPEC, *[HBM_SPEC] * m, BS(memory_space=pltpu.VMEM)),
        input_output_aliases={i: 2 + i for i in range(m)},
        compiler_params=pltpu.CompilerParams(has_side_effects=DATAFLOW))(*sources, *lands, *after_ops)
    return out[0], out[1], list(out[2:2 + n]), list(out[2 + n:2 + m]), out[-1]


def _split_wait(name, copies, send_sems, recv_sems, sources, lands, after):
    n = len(sources)
    m = n + len(lands)
    after_ops, after_specs = _after_operands(after)

    def body(*refs):
        ins = refs[:n]
        zones = refs[n:m] if m > n else ins
        for cp in copies(ins, zones, refs[m], refs[m + 1]):
            cp.wait_send()
            cp.wait_recv()

    out = pl.pallas_call(
        body, name=name,
        out_shape=tuple(pltpu.HBM(a.shape, a.dtype) for a in sources + lands),
        in_specs=[HBM_SPEC] * m + [SEM_SPEC, SEM_SPEC] + after_specs, out_specs=(HBM_SPEC,) * m,
        input_output_aliases={i: i for i in range(m)},
        compiler_params=pltpu.CompilerParams(has_side_effects=DATAFLOW))(*sources, *lands, send_sems, recv_sems, *after_ops)
    return list(out[:n]), list(out[n:])


class _WeightGatherer:
    def __init__(self, shards):
        self.shards, self.open = shards, {}
        self.me = 2 * lax.axis_index("x") + lax.axis_index("y")

    HALVED = ("f1a", "mix")

    def start(self, tag, after=()):
        shapes = [((N_SHARD,) + s.shape, s.dtype) for s in self.shards[tag]]
        copies = _gather_half_copies if tag in self.HALVED else _gather_copies
        self.open[tag] = _split_start("gather_start_" + tag, copies, self.shards[tag], shapes, after)
        return [self.open[tag][-1]]

    def finish(self, tag, after):
        send_sems, recv_sems, shards, lands, _ = self.open.pop(tag)
        copies = _gather_half_copies if tag in self.HALVED else _gather_copies
        shards, lands = _split_wait("gather_wait_" + tag, copies, send_sems, recv_sems, shards, lands, after)
        if tag in self.HALVED:
            lands = _sibling_fill(tag, lands)
        return [lax.dynamic_update_slice(zone, s[None], (self.me, 0, 0)) for zone, s in zip(lands, shards)]


class _GradReducer:
    def __init__(self):
        self.c_idx = lax.axis_index("c").astype(jnp.int32).reshape(1)
        self.place = jnp.stack([2 * lax.axis_index("x") + lax.axis_index("y"), lax.axis_index("c")]).astype(jnp.int32)
        self.swaps, self.open, self.landed, self.joins, self.reduced = {}, {}, {}, {}, []

    def swap_start(self, tag, grads, after=()):
        shapes = [((g.shape[0], g.shape[1] // 2, g.shape[2]), g.dtype) for g in grads]
        self.swaps[tag] = _split_start("reduce_swap_start_" + tag, _swap_copies, grads, shapes, after, fanout=1)
        return [self.swaps[tag][-1]]

    def start(self, tag, grads, after=(), swapped=()):
        pairs = []
        for s in swapped:
            send_sems, recv_sems, early, lands, _ = self.swaps.pop(s)
            pairs += zip(*_split_wait("reduce_swap_wait_" + s, _swap_copies, send_sems, recv_sems, early, lands, grads[-1:]))
        pairs += zip(grads, _sibling_swap_halves(tag, grads, after))
        parts = _chip_presum(tag, [g for g, _ in pairs], [s for _, s in pairs], self.c_idx)
        shapes = [((3,) + p.shape[1:], p.dtype) for p in parts]
        self.open[tag] = _split_start("reduce_exchange_start_" + tag, _chip_exchange_copies, parts, shapes)
        return [self.open[tag][-1]]

    def finish(self, tag, after):
        send_sems, recv_sems, parts, lands, _ = self.open.pop(tag)
        self.landed[tag] = _split_wait("reduce_exchange_wait_" + tag, _chip_exchange_copies, send_sems, recv_sems, parts, lands, after)
        return self.landed[tag][1][:1]

    def _sums(self, tag, after=()):
        parts, landed = self.landed.pop(tag)
        return _chip_sum(tag, parts, landed, self.place, after)

    def join_start(self, tag, after=()):
        self.joins[tag] = _split_start("reduce_join_start_" + tag, _join_copies, self._sums(tag, after), [], fanout=1)
        return [self.joins[tag][-1]]

    def join_finish(self, tag, after):
        send_sems, recv_sems, fulls, _, _ = self.joins.pop(tag)
        self.reduced += _split_wait("reduce_join_wait_" + tag, _join_copies, send_sems, recv_sems, fulls, [], after)[0]

    def join(self, tag, after=()):
        self.reduced += _sibling_join_halves(self._sums(tag), after)


def _chip_sum(tag, parts, gots, place, after=()):
    n = len(parts)
    tiles = [(p.shape[1] // REDUCE_STEPS, p.shape[2]) for p in parts]
    after_ops, after_specs = _after_operands(after)

    def body(place_ref, *refs):
        outs = refs[2 * n + len(after_ops):]
        for k in range(n):
            acc = refs[k][...].astype(F32)
            for j in range(3):
                acc = acc + refs[n + k][j].astype(F32)
            outs[k][...] = acc

    return list(pl.pallas_call(
        body, out_shape=tuple(SDS((2 * p.shape[1], p.shape[2]), F32) for p in parts),
        grid_spec=pltpu.PrefetchScalarGridSpec(
            num_scalar_prefetch=1, grid=(REDUCE_STEPS,),
            in_specs=[BS((None,) + tile, lambda i, place_ref: (place_ref[0], i, 0)) for tile in tiles]
            + [BS((3,) + tile, lambda i, place_ref: (0, i, 0)) for tile in tiles] + after_specs,
            out_specs=[BS(tile, lambda i, place_ref: (place_ref[1] * REDUCE_STEPS + i, 0)) for tile in tiles]),
        name="reduce_sum_" + tag, compiler_params=_params("parallel"))(place, *parts, *gots, *after_ops))


def _sibling_join_halves(fulls, after=()):
    n = len(fulls)
    after_ops, after_specs = _after_operands(after)

    def body(*refs):
        outs = refs[n + len(after_ops):2 * n + len(after_ops)]
        send_sems, recv_sems = refs[2 * n + len(after_ops):]
        copies = _join_copies(outs, outs, send_sems, recv_sems)
        for cp in copies:
            cp.start()
        for cp in copies:
            cp.wait_recv()
        for cp in copies:
            cp.wait_send()

    hbm = BS(memory_space=pl.ANY)
    return list(pl.pallas_call(
        body, out_shape=tuple(SDS(f.shape, f.dtype) for f in fulls),
        in_specs=[hbm] * n + after_specs, out_specs=(hbm,) * n, input_output_aliases={k: k for k in range(n)},
        scratch_shapes=[pltpu.SemaphoreType.DMA((n,)), pltpu.SemaphoreType.DMA((n,))],
        name="reduce_sibling_join", compiler_params=_params())(*fulls, *after_ops))


N_DEV = 8


def _sum_devices(packs):
    _, rows, lanes = packs.shape

    def body(p_ref, o_ref):
        acc = p_ref[0]
        for dev in range(1, N_DEV):
            acc = acc + p_ref[dev]
        o_ref[...] = acc

    vm = BS(memory_space=pltpu.VMEM)
    return pl.pallas_call(body, out_shape=SDS((rows, lanes), F32), in_specs=[vm], out_specs=vm,
                          name="small_sum", compiler_params=_params())(packs)


def _adamw_refs(w_ref, g_ref, m_ref, v_ref, go_ref, d_ref, mo_ref, vo_ref):
    bc1 = 1.0 - ADAM_B1 ** ADAM_STEP
    bc2 = 1.0 - ADAM_B2 ** ADAM_STEP
    g = g_ref[...]
    m_new = ADAM_B1 * m_ref[...] + (1.0 - ADAM_B1) * g
    v_new = ADAM_B2 * v_ref[...] + (1.0 - ADAM_B2) * (g * g)
    go_ref[...] = g
    mo_ref[...] = m_new
    vo_ref[...] = v_new
    d_ref[...] = -ADAM_LR * ((m_new / bc1) / (jnp.sqrt(v_new / bc2) + ADAM_EPS) + ADAM_WD * w_ref[...])


def _adamw_small(ws, gs, ms, vs):
    n = len(ws)

    def body(*refs):
        for k in range(n):
            _adamw_refs(*[refs[j * n + k] for j in range(4)], *refs[4 * n + 4 * k:4 * n + 4 * k + 4])

    vm = BS(memory_space=pltpu.VMEM)
    outs = pl.pallas_call(
        body, out_shape=tuple(SDS(a.shape, F32) for a in ws for _ in range(4)), in_specs=[vm] * (4 * n),
        out_specs=(vm,) * (4 * n), name="adamw_small", compiler_params=_params())(*ws, *gs, *ms, *vs)
    return [outs[4 * k:4 * k + 4] for k in range(n)]


def _adamw(name, w, grad, row0, m, v, after=()):
    rows, cols = w.shape
    tr = rows if rows < 16 else _row_tile(rows, 352)
    after_ops, after_specs = _after_operands(after)

    def body(w_ref, g_ref, m_ref, v_ref, *rest):
        _adamw_refs(w_ref, g_ref, m_ref, v_ref, *rest[len(after_ops):])

    blk = BS((tr, cols), lambda i: (i, 0))
    shape = SDS((rows, cols), F32)
    return pl.pallas_call(
        body, out_shape=(shape,) * 4, grid=(rows // tr,),
        in_specs=[blk, BS((tr, cols), lambda i: (row0 // tr + i, 0)), blk, blk] + after_specs, out_specs=(blk,) * 4,
        name=name, compiler_params=_params("parallel"))(w, grad, m, v, *after_ops)


SMALL_LANES = 128


def _pack_small(parts):
    flat = jnp.concatenate([jnp.ravel(p) for p in parts])
    rows = -(-flat.shape[0] // (64 * SMALL_LANES)) * 64
    return jnp.pad(flat, (0, rows * SMALL_LANES - flat.shape[0])).reshape(rows, SMALL_LANES)


def _unpack_small(packed, like):
    flat = jnp.ravel(packed)
    out, at = [], 0
    for p in like:
        out.append(flat[at:at + p.size].reshape(p.shape))
        at += p.size
    return out


def kernel(x, mem, ffn1_norm, ffn1_w_gate, ffn1_w_up, ffn1_w_down, mix_norm, w_in, pool_w, pool_scale, w_pool_proj, ssm_a_re, ssm_a_im, ssm_log_dt, ssm_b_re, ssm_b_im, ssm_c_re, ssm_c_im, ssm_d, w_glu_val, w_glu_gate, w_mix_out, xattn_norm, mem_norm, w_q, w_kv, w_xo, ffn2_norm, ffn2_w_gate, ffn2_w_up, ffn2_w_down, final_norm, loss_target, m_ffn1_norm, m_ffn1_w_gate, m_ffn1_w_up, m_ffn1_w_down, m_mix_norm, m_w_in, m_pool_w, m_pool_scale, m_w_pool_proj, m_ssm_a_re, m_ssm_a_im, m_ssm_log_dt, m_ssm_b_re, m_ssm_b_im, m_ssm_c_re, m_ssm_c_im, m_ssm_d, m_w_glu_val, m_w_glu_gate, m_w_mix_out, m_xattn_norm, m_mem_norm, m_w_q, m_w_kv, m_w_xo, m_ffn2_norm, m_ffn2_w_gate, m_ffn2_w_up, m_ffn2_w_down, m_final_norm, v_ffn1_norm, v_ffn1_w_gate, v_ffn1_w_up, v_ffn1_w_down, v_mix_norm, v_w_in, v_pool_w, v_pool_scale, v_w_pool_proj, v_ssm_a_re, v_ssm_a_im, v_ssm_log_dt, v_ssm_b_re, v_ssm_b_im, v_ssm_c_re, v_ssm_c_im, v_ssm_d, v_w_glu_val, v_w_glu_gate, v_w_mix_out, v_xattn_norm, v_mem_norm, v_w_q, v_w_kv, v_w_xo, v_ffn2_norm, v_ffn2_w_gate, v_ffn2_w_up, v_ffn2_w_down, v_final_norm):
    given = dict(locals())
    w = {n: given[n] for n in WEIGHTS}
    m = {n: given["m_" + n] for n in WEIGHTS}
    v = {n: given["v_" + n] for n in WEIGHTS}

    def shard_view(a, n):
        return a[0].T if n in TRANSPOSED else a[0]

    def shard_unview(a, n):
        return (a.T if n in TRANSPOSED else a)[None]

    shards = {tag: [jnp.concatenate([shard_view(w[n], n).astype(BF16) for n in grp], axis=0) for grp in arrays]
              for tag, arrays in GATHER_PHASES.items()}
    reducer = _GradReducer()
    ws, ms, vs = ({n: _small_view(a[n], n) for n in SMALL} for a in (w, m, v))
    loss_part, grad_x, _, small = _device_step(x[0], mem[0], loss_target[0], _WeightGatherer(shards), ws, reducer)

    small_like = [ws[n] for n in SMALL] + [loss_part[0, :1]]
    pack = _pack_small([small[n] for n in SMALL] + [loss_part[0, :1]])
    everyone = _split_start("small_start", _everyone_copies, [pack], [((N_DEV,) + pack.shape, F32)], fanout=N_DEV - 1)
    reducer.join("b", after=everyone[-1:])

    grads, delta, new_m, new_v = {}, {}, {}, {}
    big_done = []
    for grp, red in zip(REDUCE_GROUPS, reducer.reduced):
        row0 = 0
        for n in grp:
            w_n = shard_view(w[n], n)
            outs = _adamw("adamw_" + n, w_n, red, row0, shard_view(m[n], n), shard_view(v[n], n), after=everyone[-1:])
            grads[n], delta[n], new_m[n], new_v[n] = (shard_unview(o, n) for o in outs)
            big_done.append(outs[1])
            row0 += w_n.shape[0]

    send_sems, recv_sems, packs, landed, _ = everyone
    packs, landed = _split_wait("small_wait", _everyone_copies, send_sems, recv_sems, packs, landed, big_done)
    mine = 4 * lax.axis_index("x") + 2 * lax.axis_index("y") + lax.axis_index("c")
    summed = _sum_devices(lax.dynamic_update_slice(landed[0], packs[0][None], (mine, 0, 0)))
    g_small = dict(zip(SMALL + ("loss",), _unpack_small(summed, small_like)))
    loss = g_small.pop("loss").reshape(())
    def two_d(a):
        return a.reshape(-1, a.shape[-1])

    updated = _adamw_small(*([two_d(a[n]) for n in SMALL] for a in (ws, g_small, ms, vs)))
    for n, outs in zip(SMALL, updated):
        grads[n], delta[n], new_m[n], new_v[n] = (_small_view(o.reshape(ws[n].shape), n) for o in outs)

    return (loss, grad_x[None], *[grads[n] for n in WEIGHTS], *[delta[n] for n in WEIGHTS],
            *[new_m[n] for n in WEIGHTS], *[new_v[n] for n in WEIGHTS])
```

```python
import functools
import math

import jax
import jax.numpy as jnp
from jax import lax
from jax.experimental import pallas as pl
from jax.experimental.pallas import tpu as pltpu

F32 = jnp.float32
BF16 = jnp.bfloat16
SDS = jax.ShapeDtypeStruct
BS = pl.BlockSpec
MESH = pl.DeviceIdType.MESH

D_MODEL = 1024
D_FF = 2816
N_SHARD = 4
FF_SH = D_FF // N_SHARD
D_POOL = 512
POOL_WINDOWS = (2, 4, 8, 16)
POOL_GROUP = 128
D_SSM = 256
SSM_GROUPS = 16
SSM_GROUP = 16
SSM_STATE = 64
SSM_CH = SSM_GROUPS * SSM_STATE
N_HEADS = 4
HEAD_DIM = 256
EPS = 1e-6
ADAM_LR, ADAM_B1, ADAM_B2, ADAM_EPS, ADAM_WD, ADAM_STEP = 0.001, 0.9, 0.999, 1e-08, 0.01, 10

VMEM_LIMIT_V7X = 52 * 1024 * 1024
TM = 512

NN = (((1,), (0,)), ((), ()))
NT = (((1,), (1,)), ((), ()))
TN = (((0,), (0,)), ((), ()))


def _params(*sem):
    return pltpu.CompilerParams(dimension_semantics=sem if sem else None, vmem_limit_bytes=VMEM_LIMIT_V7X)


def _dot(a, b, dims=NN):
    return lax.dot_general(a.astype(BF16), b.astype(BF16), dims, preferred_element_type=F32)


def _sigmoid(v):
    return pl.reciprocal(1.0 + jnp.exp(-v), approx=True)


def _block_dims(spec):
    return tuple(d for d in spec.block_shape if d is not None)


def _after_operands(after):
    return list(after), [BS(memory_space=pl.ANY)] * len(after)


def _mm(name, pairs, *, grid, out_shape, out_spec, red_axis=None, extras=(), epilogue=None, after=()):
    n_pairs, n_extra = len(pairs), len(extras)
    n_red = grid[red_axis] if red_axis is not None else 1
    dims = [p[4] for p in pairs]

    def body(*refs):
        ab = refs[:2 * n_pairs]
        ex = refs[2 * n_pairs:2 * n_pairs + n_extra]
        o_ref = refs[2 * n_pairs + n_extra + len(after)]

        def partial():
            acc = None
            for p in range(n_pairs):
                t = _dot(ab[2 * p][...], ab[2 * p + 1][...], dims[p])
                acc = t if acc is None else acc + t
            return acc

        def finish(acc):
            res = epilogue(acc, *[e[...] for e in ex]) if epilogue is not None else acc
            o_ref[...] = res.astype(o_ref.dtype)

        if n_red == 1:
            finish(partial())
        else:
            acc_ref = refs[-1]
            k = pl.program_id(red_axis)

            @pl.when(k == 0)
            def _():
                acc_ref[...] = jnp.zeros_like(acc_ref)

            acc_ref[...] += partial()

            @pl.when(k == n_red - 1)
            def _():
                finish(acc_ref[...])

    operands, in_specs = [], []
    for a, a_spec, b, b_spec, _ in pairs:
        operands += [a, b]
        in_specs += [a_spec, b_spec]
    for e, e_spec in extras:
        operands.append(e)
        in_specs.append(e_spec)
    after_ops, after_specs = _after_operands(after)
    operands += after_ops
    in_specs += after_specs
    scratch = [pltpu.VMEM(_block_dims(out_spec), F32)] if n_red > 1 else []
    sem = tuple("arbitrary" if ax == red_axis else "parallel" for ax in range(len(grid)))
    return pl.pallas_call(body, out_shape=out_shape, grid=grid, in_specs=in_specs, out_specs=out_spec,
                          scratch_shapes=scratch, name=name, compiler_params=_params(*sem))(*operands)


def _rmsnorm(name, h, gain, tm, after=()):
    t, d = h.shape
    after_ops, after_specs = _after_operands(after)

    def body(h_ref, g_ref, *rest):
        u_ref = rest[-1]
        hv = h_ref[...]
        r = lax.rsqrt(jnp.mean(hv * hv, axis=-1, keepdims=True) + EPS)
        u_ref[...] = ((hv * r) * g_ref[...]).astype(u_ref.dtype)

    return pl.pallas_call(
        body, out_shape=SDS((t, d), BF16), grid=(t // tm,),
        in_specs=[BS((tm, d), lambda i: (i, 0)), BS((1, d), lambda i: (0, 0))] + after_specs,
        out_specs=BS((tm, d), lambda i: (i, 0)), name=name, compiler_params=_params("parallel"))(h, gain, *after_ops)


def _rmsnorm_bwd(name, h, gain, du, dh_in, tm):
    t, d = h.shape
    has_in = dh_in is not None

    def body(*refs):
        if has_in:
            h_ref, g_ref, du_ref, dhin_ref, dh_ref, dhb_ref, dg_ref = refs
        else:
            h_ref, g_ref, du_ref, dh_ref, dhb_ref, dg_ref = refs
        i = pl.program_id(0)
        hv = h_ref[...]
        r = lax.rsqrt(jnp.mean(hv * hv, axis=-1, keepdims=True) + EPS)
        n = hv * r
        duv = du_ref[...].astype(F32)
        dn = duv * g_ref[...]
        dh = r * (dn - n * jnp.mean(dn * n, axis=-1, keepdims=True))
        if has_in:
            dh = dhin_ref[...] + dh
        dh_ref[...] = dh
        dhb_ref[...] = dh.astype(BF16)

        @pl.when(i == 0)
        def _():
            dg_ref[...] = jnp.zeros_like(dg_ref)

        dg_ref[...] += jnp.sum(duv * n, axis=0, keepdims=True)

    row = BS((tm, d), lambda i: (i, 0))
    vec = BS((1, d), lambda i: (0, 0))
    operands = [h, gain, du] + ([dh_in] if has_in else [])
    in_specs = [row, vec, row] + ([row] if has_in else [])
    return pl.pallas_call(
        body, out_shape=(SDS((t, d), F32), SDS((t, d), BF16), SDS((1, d), F32)), grid=(t // tm,),
        in_specs=in_specs, out_specs=(row, row, vec), name=name, compiler_params=_params("arbitrary"))(*operands)


def _loss_head_tile(i, hv, g_ref, t_ref, loss_ref, dh_ref, dhb_ref, dg_ref):
    g = g_ref[...]
    r = lax.rsqrt(jnp.mean(hv * hv, axis=-1, keepdims=True) + EPS)
    n = hv * r
    err = n * g - t_ref[...]
    dy = err * (1.0 / hv.shape[-1])
    dn = dy * g
    dh = r * (dn - n * jnp.mean(dn * n, axis=-1, keepdims=True))
    dh_ref[...] = dh
    dhb_ref[...] = dh.astype(BF16)

    @pl.when(i == 0)
    def _():
        dg_ref[...] = jnp.zeros_like(dg_ref)
        loss_ref[...] = jnp.zeros_like(loss_ref)

    dg_ref[...] += jnp.sum(dy * n, axis=0, keepdims=True)
    part = 0.5 * jnp.sum(jnp.mean(err * err, axis=-1, keepdims=True), axis=0, keepdims=True)
    loss_ref[...] += jnp.broadcast_to(part, loss_ref.shape)


def _norm_tile(h, g_ref, u_ref):
    r = lax.rsqrt(jnp.mean(h * h, axis=-1, keepdims=True) + EPS)
    u_ref[...] = ((h * r) * g_ref[...]).astype(u_ref.dtype)


FFN_GATE, FFN_UP, FFN_DOWN = 0, 1, 2


def _ffn_up(name, u, w_f, tm, after=()):
    t, d = u.shape
    after_ops, after_specs = _after_operands(after)

    def body(u_ref, wg_ref, wu_ref, *rest):
        pg_ref, pu_ref, a_ref = rest[len(after_ops):]
        uv = u_ref[...]
        for s in range(N_SHARD):
            g = _dot(uv, wg_ref[s], NT)
            up = _dot(uv, wu_ref[s], NT)
            sg = _sigmoid(g)
            silu = g * sg
            a_ref[s] = (silu * up).astype(BF16)
            pu_ref[s] = (0.5 * silu).astype(BF16)
            pg_ref[s] = (0.5 * sg * (1.0 + g * (1.0 - sg)) * up).astype(BF16)

    hid = BS((N_SHARD, tm, FF_SH), lambda i: (0, i, 0))
    shape = SDS((N_SHARD, t, FF_SH), BF16)
    return pl.pallas_call(
        body, out_shape=(shape, shape, shape), grid=(t // tm,),
        in_specs=[BS((tm, d), lambda i: (i, 0)), _ffn_all_shards_spec(w_f["gate"][1]),
                  _ffn_all_shards_spec(w_f["up"][1])] + after_specs,
        out_specs=(hid, hid, hid), name=name,
        compiler_params=_params("parallel"))(u, w_f["gate"][0], w_f["up"][0], *after_ops)


def _ffn_all_shards_spec(block):
    return BS((N_SHARD, FF_SH, D_MODEL), lambda i: (0, block, 0))


def _ffn_down(name, a, w_f, resid, tm, next_gain=None, head=None):
    t, d = resid.shape
    row = BS((tm, d), lambda i: (i, 0))
    vec = BS((1, d), lambda i: (0, 0))

    def body(a_ref, w_ref, res_ref, *rest):
        acc = _dot(a_ref[0], w_ref[0])
        for s in range(1, N_SHARD):
            acc = acc + _dot(a_ref[s], w_ref[s])
        h = res_ref[...] + 0.5 * acc
        if head is not None:
            _loss_head_tile(pl.program_id(0), h, *rest)
        else:
            g_ref, h_ref, u_ref = rest
            h_ref[...] = h
            _norm_tile(h, g_ref, u_ref)

    if head is not None:
        extra, extra_specs = list(head), [vec, row]
        out_shape = (SDS((1, 128), F32), SDS((t, d), F32), SDS((t, d), BF16), SDS((1, d), F32))
        out_specs = (BS((1, 128), lambda i: (0, 0)), row, row, vec)
    else:
        extra, extra_specs = [next_gain], [vec]
        out_shape = (SDS((t, d), F32), SDS((t, d), BF16))
        out_specs = (row, row)
    return pl.pallas_call(
        body, out_shape=out_shape, grid=(t // tm,),
        in_specs=[BS((N_SHARD, tm, FF_SH), lambda i: (0, i, 0)), _ffn_all_shards_spec(w_f["down"][1]), row] + extra_specs,
        out_specs=out_specs, name=name,
        compiler_params=_params("arbitrary" if head is not None else "parallel"))(a, w_f["down"][0], resid, *extra)


def _mm_resid_norm(name, a, b, resid, next_gain, tm):
    t, d = resid.shape

    def body(a_ref, b_ref, res_ref, g_ref, h_ref, u_ref):
        h = res_ref[...] + _dot(a_ref[...], b_ref[...])
        h_ref[...] = h
        _norm_tile(h, g_ref, u_ref)

    row = BS((tm, d), lambda i: (i, 0))
    return pl.pallas_call(
        body, out_shape=(SDS((t, d), F32), SDS((t, d), BF16)), grid=(t // tm,),
        in_specs=[BS((tm, a.shape[1]), lambda i: (i, 0)), BS(b.shape, lambda i: (0, 0)), row, BS((1, d), lambda i: (0, 0))],
        out_specs=(row, row), name=name, compiler_params=_params("parallel"))(a, b, resid, next_gain)


def _ffn_bwd_act(name, dh_b, w_f, pg, pu, tm, after=()):
    t, d = dh_b.shape
    after_ops, after_specs = _after_operands(after)

    def body(dh_ref, wd_ref, pg_ref, pu_ref, *rest):
        dg_ref, dup_ref = rest[len(after_ops):]
        dh = dh_ref[...]
        for s in range(N_SHARD):
            da = _dot(dh, wd_ref[s], NT)
            dg_ref[s] = (da * pg_ref[s].astype(F32)).astype(BF16)
            dup_ref[s] = (da * pu_ref[s].astype(F32)).astype(BF16)

    hid = BS((N_SHARD, tm, FF_SH), lambda i: (0, i, 0))
    shape = SDS((N_SHARD, t, FF_SH), BF16)
    return pl.pallas_call(
        body, out_shape=(shape, shape), grid=(t // tm,),
        in_specs=[BS((tm, d), lambda i: (i, 0)), _ffn_all_shards_spec(w_f["down"][1]), hid, hid] + after_specs,
        out_specs=(hid, hid), name=name,
        compiler_params=_params("parallel"))(dh_b, w_f["down"][0], pg, pu, *after_ops)


def _ffn_dw(name, u, dg, dup, a, dh_b, tm):
    t, d = u.shape
    tm = min(2 * tm, t)
    n_t = t // tm

    def body(u_ref, dg_ref, dup_ref, a_ref, dh_ref, o_ref, acc):
        i = pl.program_id(1)

        @pl.when(i == 0)
        def _():
            acc[...] = jnp.zeros_like(acc)

        uv = u_ref[...]
        acc[FFN_GATE] += _dot(dg_ref[...], uv, TN)
        acc[FFN_UP] += _dot(dup_ref[...], uv, TN)
        acc[FFN_DOWN] += _dot(a_ref[...], dh_ref[...], TN)

        @pl.when(i == n_t - 1)
        def _():
            o_ref[FFN_GATE] = acc[FFN_GATE].astype(BF16)
            o_ref[FFN_UP] = acc[FFN_UP].astype(BF16)
            o_ref[FFN_DOWN] = (0.5 * acc[FFN_DOWN]).astype(BF16)

    hid = BS((None, tm, FF_SH), lambda s, i: (s, i, 0))
    row = BS((tm, d), lambda s, i: (i, 0))
    return pl.pallas_call(
        body, out_shape=SDS((N_SHARD, 3, FF_SH, d), BF16), grid=(N_SHARD, n_t),
        in_specs=[row, hid, hid, hid, row], out_specs=BS((None, 3, FF_SH, d), lambda s, i: (s, 0, 0, 0)),
        scratch_shapes=[pltpu.VMEM((3, FF_SH, d), F32)],
        name=name, compiler_params=_params("parallel", "arbitrary"))(u, dg, dup, a, dh_b)


def _norm_bwd_tile(i, du, h_ref, g_ref, dhin_ref, dh_ref, dhb_ref, dg_ref):
    hv = h_ref[...]
    r = lax.rsqrt(jnp.mean(hv * hv, axis=-1, keepdims=True) + EPS)
    n = hv * r
    dn = du * g_ref[...]
    dh = dhin_ref[...] + r * (dn - n * jnp.mean(dn * n, axis=-1, keepdims=True))
    dh_ref[...] = dh
    dhb_ref[...] = dh.astype(BF16)

    @pl.when(i == 0)
    def _():
        dg_ref[...] = jnp.zeros_like(dg_ref)

    dg_ref[...] += jnp.sum(du * n, axis=0, keepdims=True)


def _norm_bwd_specs(tm):
    row = BS((tm, D_MODEL), lambda i: (i, 0))
    vec = BS((1, D_MODEL), lambda i: (0, 0))
    return [row, vec, row], (row, row, vec)


def _norm_bwd_shapes(t):
    return SDS((t, D_MODEL), F32), SDS((t, D_MODEL), BF16), SDS((1, D_MODEL), F32)


def _ffn_dx(name, dg, dup, w_f, h, gain, dh_in, tm, after=()):
    t = dg.shape[1]
    tm = tm // 2
    after_ops, after_specs = _after_operands(after)

    def body(dg_ref, dup_ref, wg_ref, wu_ref, h_ref, g_ref, dhin_ref, *rest):
        acc = _dot(dg_ref[0], wg_ref[0]) + _dot(dup_ref[0], wu_ref[0])
        for s in range(1, N_SHARD):
            acc = acc + _dot(dg_ref[s], wg_ref[s]) + _dot(dup_ref[s], wu_ref[s])
        _norm_bwd_tile(pl.program_id(0), acc, h_ref, g_ref, dhin_ref, *rest[len(after_ops):])

    hid = BS((N_SHARD, tm, FF_SH), lambda i: (0, i, 0))
    norm_in, norm_out = _norm_bwd_specs(tm)
    return pl.pallas_call(
        body, out_shape=_norm_bwd_shapes(t), grid=(t // tm,),
        in_specs=[hid, hid, _ffn_all_shards_spec(w_f["gate"][1]), _ffn_all_shards_spec(w_f["up"][1])] + norm_in + after_specs,
        out_specs=norm_out, name=name,
        compiler_params=_params("arbitrary"))(dg, dup, w_f["gate"][0], w_f["up"][0], h, gain, dh_in, *after_ops)


def _mm_norm_bwd(name, a, b, dims, h, gain, dh_in, tm):
    t = a.shape[0]

    def body(a_ref, b_ref, h_ref, g_ref, dhin_ref, *outs):
        _norm_bwd_tile(pl.program_id(0), _dot(a_ref[...], b_ref[...], dims), h_ref, g_ref, dhin_ref, *outs)

    norm_in, norm_out = _norm_bwd_specs(tm)
    return pl.pallas_call(
        body, out_shape=_norm_bwd_shapes(t), grid=(t // tm,),
        in_specs=[BS((tm, a.shape[1]), lambda i: (i, 0)), BS(b.shape, lambda i: (0, 0))] + norm_in,
        out_specs=norm_out, name=name, compiler_params=_params("arbitrary"))(a, b, h, gain, dh_in)


def _plain_mm(name, a, b, dims, out_dtype, tm, resid=None, after=()):
    t = a.shape[0]
    n = b.shape[1] if dims == NN else b.shape[0]
    extras = [(resid, BS((tm, n), lambda i: (i, 0)))] if resid is not None else []
    epi = (lambda acc, res: res + acc) if resid is not None else None
    return _mm(name, [(a, BS((tm, a.shape[1]), lambda i: (i, 0)), b, BS(b.shape, lambda i: (0, 0)), dims)],
               grid=(t // tm,), out_shape=SDS((t, n), out_dtype), out_spec=BS((tm, n), lambda i: (i, 0)),
               extras=extras, epilogue=epi, after=after)


def _dw_mm(name, a, b, tm, out_dtype=BF16, after=()):
    t, k = a.shape
    n = b.shape[1]
    tm = min(2 * tm, t)
    return _mm(name, [(a, BS((tm, k), lambda i: (i, 0)), b, BS((tm, n), lambda i: (i, 0)), TN)],
               grid=(t // tm,), red_axis=0, out_shape=SDS((k, n), out_dtype), out_spec=BS((k, n), lambda i: (0, 0)),
               after=after)


POOL_CHUNK = 256
POOL_HALO = 8


def _window_sum(v, width, lead):
    n = v.shape[0]
    s = v
    k = 1
    while k < width:
        s = s + pltpu.roll(s, n - k, 0)
        k *= 2
    return pltpu.roll(s, lead, 0) if lead else s


def _pool_count(base, left, right, t, shape):
    pos = base + lax.broadcasted_iota(jnp.int32, shape, 0)
    lo = jnp.maximum(pos - left, 0)
    hi = jnp.minimum(pos + right + 1, t)
    return (hi - lo).astype(F32)


def _pool_fwd(proj, pool_w, pool_scale):
    t = proj.shape[0]
    c, h = POOL_CHUNK, POOL_HALO
    n_chunks = t // c

    def body(proj_hbm, pw_ref, sc_ref, pooled_ref, mixed_ref, ms_ref, pad_ref, sem):
        cp = pltpu.make_async_copy(proj_hbm.at[:, pl.ds(0, D_POOL)], pad_ref.at[pl.ds(h, t), :], sem)
        cp.start()
        pad_ref[pl.ds(0, h), :] = jnp.zeros((h, D_POOL), F32)
        pad_ref[pl.ds(t + h, h), :] = jnp.zeros((h, D_POOL), F32)
        cp.wait()
        for g, width in enumerate(POOL_WINDOWS):
            left = width // 2
            right = width - 1 - left
            cols = slice(g * POOL_GROUP, (g + 1) * POOL_GROUP)
            wmat = pw_ref[g].astype(BF16)
            scale = sc_ref[:, cols]

            def chunk(ci, carry, left=left, right=right, width=width, cols=cols, wmat=wmat, scale=scale):
                base = pl.multiple_of(ci * c, c)
                v = pad_ref[pl.ds(base, c + 2 * h), cols]
                win = _window_sum(v, width, left)[h:h + c]
                cnt = _pool_count(base, left, right, t, (c, POOL_GROUP))
                pooled = (win / cnt - v[h:h + c]).astype(BF16)
                mixed = _dot(pooled, wmat)
                pooled_ref[pl.ds(base, c), cols] = pooled
                mixed_ref[pl.ds(base, c), cols] = mixed.astype(BF16)
                ms_ref[pl.ds(base, c), cols] = (mixed * scale).astype(BF16)
                return carry

            lax.fori_loop(0, n_chunks, chunk, 0)

    vm = BS(memory_space=pltpu.VMEM)
    shape = SDS((t, D_POOL), BF16)
    return pl.pallas_call(
        body, out_shape=(shape, shape, shape),
        in_specs=[BS(memory_space=pl.ANY), vm, vm], out_specs=(vm, vm, vm),
        scratch_shapes=[pltpu.VMEM((t + 2 * h, D_POOL), F32), pltpu.SemaphoreType.DMA],
        name="pool_fwd", compiler_params=_params())(proj, pool_w, pool_scale)


def _pool_bwd(d_ms, mixed, pooled, pool_w, pool_scale):
    t = d_ms.shape[0]
    c, h = POOL_CHUNK, POOL_HALO
    n_chunks = t // c

    def body(dms_ref, mixed_ref, pooled_ref, pw_ref, sc_ref, dp_ref, dsc_ref, dpw_ref, pad_ref):
        pad_ref[pl.ds(0, h), :] = jnp.zeros((h, D_POOL), F32)
        pad_ref[pl.ds(t + h, h), :] = jnp.zeros((h, D_POOL), F32)
        for g, width in enumerate(POOL_WINDOWS):
            left = width // 2
            right = width - 1 - left
            cols = slice(g * POOL_GROUP, (g + 1) * POOL_GROUP)
            wmat = pw_ref[g].astype(BF16)
            scale = sc_ref[:, cols]

            def first(ci, carry, left=left, right=right, cols=cols, wmat=wmat, scale=scale):
                dsc, dpw = carry
                base = pl.multiple_of(ci * c, c)
                dms = dms_ref[pl.ds(base, c), cols].astype(F32)
                dsc = dsc + jnp.sum(dms * mixed_ref[pl.ds(base, c), cols].astype(F32), axis=0, keepdims=True)
                dmix = (dms * scale).astype(BF16)
                dpw = dpw + _dot(pooled_ref[pl.ds(base, c), cols], dmix, TN)
                dpooled = _dot(dmix, wmat, NT)
                cnt = _pool_count(base, left, right, t, (c, POOL_GROUP))
                pad_ref[pl.ds(base + h, c), cols] = dpooled / cnt
                return dsc, dpw

            dsc, dpw = lax.fori_loop(0, n_chunks, first,
                                     (jnp.zeros((1, POOL_GROUP), F32), jnp.zeros((POOL_GROUP, POOL_GROUP), F32)))
            dsc_ref[:, cols] = dsc
            dpw_ref[g] = dpw

            def second(ci, carry, left=left, right=right, width=width, cols=cols):
                base = pl.multiple_of(ci * c, c)
                v = pad_ref[pl.ds(base, c + 2 * h), cols]
                win = _window_sum(v, width, right)[h:h + c]
                cnt = _pool_count(base, left, right, t, (c, POOL_GROUP))
                dp_ref[pl.ds(base, c), cols] = (win - v[h:h + c] * cnt).astype(BF16)
                return carry

            lax.fori_loop(0, n_chunks, second, 0)

    vm = BS(memory_space=pltpu.VMEM)
    return pl.pallas_call(
        body, out_shape=(SDS((t, D_POOL), BF16), SDS((1, D_POOL), F32), SDS((4, POOL_GROUP, POOL_GROUP), F32)),
        in_specs=[vm] * 5, out_specs=(vm, vm, vm),
        scratch_shapes=[pltpu.VMEM((t + 2 * h, D_POOL), F32)],
        name="pool_bwd", compiler_params=_params())(d_ms, mixed, pooled, pool_w, pool_scale)


SSM_ROWS = 2 * SSM_GROUPS * SSM_GROUP
SSM_HALF = SSM_GROUPS * SSM_GROUP


def _ssm_zoh(a_r, a_i, ldt):
    dt = jnp.exp(ldt)
    mag = jnp.exp(dt * a_r)
    ang = dt * a_i
    cs, sn = jnp.cos(ang), jnp.sin(ang)
    abr, abi = mag * cs, mag * sn
    den = a_r * a_r + a_i * a_i
    nr = abr - 1.0
    qr = (nr * a_r + abi * a_i) / den
    qi = (abi * a_r - nr * a_i) / den
    return dt, mag, cs, sn, abr, abi, den, nr, qr, qi


def _ssm_group_mask():
    row = lax.broadcasted_iota(jnp.int32, (SSM_HALF, SSM_CH), 0)
    col = lax.broadcasted_iota(jnp.int32, (SSM_HALF, SSM_CH), 1)
    return (row // SSM_GROUP) == (col // SSM_STATE)


def _ssm_prep(a_r, a_i, ldt, b_r, b_i, c_r, c_i, after=()):
    after_ops, after_specs = _after_operands(after)

    def body(ar_ref, ai_ref, ldt_ref, br_ref, bi_ref, cr_ref, ci_ref, *rest):
        abr_ref, abi_ref, win_ref, wint_ref, woutt_ref, wout_ref = rest[len(after_ops):]
        *_, abr, abi, _, _, qr, qi = _ssm_zoh(ar_ref[...], ai_ref[...], ldt_ref[...])
        abr_ref[...] = abr
        abi_ref[...] = abi
        b_r, b_i = br_ref[...], bi_ref[...]
        bbr = qr * b_r - qi * b_i
        bbi = qr * b_i + qi * b_r
        mask = _ssm_group_mask()
        state = lax.broadcasted_iota(jnp.int32, (SSM_STATE, SSM_CH), 0)
        col = lax.broadcasted_iota(jnp.int32, (SSM_STATE, SSM_CH), 1)
        every_group = (col % SSM_STATE == state).astype(BF16)

        def spread(x):
            return jnp.where(mask, _dot(x, every_group), 0.0)

        for d in range(2):
            rows = slice(d * SSM_HALF, (d + 1) * SSM_HALF)
            for half, x_in, x_out in ((0, bbr[rows], cr_ref[rows, :]), (1, bbi[rows], -ci_ref[rows, :])):
                cols = slice(half * SSM_CH, (half + 1) * SSM_CH)
                m_in, m_out = spread(x_in), spread(x_out)
                win_ref[d, :, cols] = m_in.astype(BF16)
                wint_ref[d, cols, :] = m_in.T.astype(BF16)
                woutt_ref[d, :, cols] = m_out.astype(BF16)
                wout_ref[d, cols, :] = m_out.T.astype(BF16)

    vm = BS(memory_space=pltpu.VMEM)
    vec = SDS((SSM_ROWS, SSM_STATE), F32)
    wide = SDS((2, SSM_HALF, 2 * SSM_CH), BF16)
    tall = SDS((2, 2 * SSM_CH, SSM_HALF), BF16)
    return pl.pallas_call(body, out_shape=(vec, vec, wide, tall, wide, tall), in_specs=[vm] * 7 + after_specs,
                          out_specs=(vm,) * 6, name="ssm_prep",
                          compiler_params=_params())(a_r, a_i, ldt, b_r, b_i, c_r, c_i, *after_ops)


def _ssm_prep_bwd(a_r, a_i, ldt, b_r, b_i, g_abr, g_abi, d_win, d_woutt):
    def body(ar_ref, ai_ref, ldt_ref, br_ref, bi_ref, gabr_ref, gabi_ref, dwin_ref, dwoutt_ref,
             dar_ref, dai_ref, dldt_ref, dbr_ref, dbi_ref, dcr_ref, dci_ref):
        a_r, a_i = ar_ref[...], ai_ref[...]
        dt, mag, cs, sn, abr, abi, den, nr, qr, qi = _ssm_zoh(a_r, a_i, ldt_ref[...])
        mask = _ssm_group_mask()
        col = lax.broadcasted_iota(jnp.int32, (SSM_CH, SSM_STATE), 0)
        state = lax.broadcasted_iota(jnp.int32, (SSM_CH, SSM_STATE), 1)
        own_state = (col % SSM_STATE == state).astype(BF16)

        def pick(dense):
            m = jnp.where(mask, dense, 0.0)
            hi = m.astype(BF16)
            lo = m - hi.astype(F32)
            return _dot(hi, own_state) + _dot(lo, own_state)

        def picked(ref, half):
            cols = slice(half * SSM_CH, (half + 1) * SSM_CH)
            return jnp.concatenate([pick(ref[d, :, cols]) for d in range(2)], axis=0)

        g_r, g_i = picked(dwin_ref, 0), picked(dwin_ref, 1)
        dcr_ref[...] = picked(dwoutt_ref, 0)
        dci_ref[...] = -picked(dwoutt_ref, 1)
        b_r, b_i = br_ref[...], bi_ref[...]
        dbr_ref[...] = g_r * qr + g_i * qi
        dbi_ref[...] = g_i * qr - g_r * qi
        gqr = g_r * b_r + g_i * b_i
        gqi = g_i * b_r - g_r * b_i
        g_nr_num = gqr / den
        g_ni_num = gqi / den
        g_den = -(gqr * qr + gqi * qi) / den
        g_nr = g_nr_num * a_r - g_ni_num * a_i
        g_abi = g_nr_num * a_i + g_ni_num * a_r
        d_ar = g_nr_num * nr + g_ni_num * abi + 2.0 * a_r * g_den
        d_ai = g_nr_num * abi - g_ni_num * nr + 2.0 * a_i * g_den
        g_abr = gabr_ref[...] + g_nr
        g_abi = gabi_ref[...] + g_abi
        g_mag = g_abr * cs + g_abi * sn
        g_ang = mag * (g_abi * cs - g_abr * sn)
        g_e = g_mag * mag
        d_ar = d_ar + g_e * dt
        d_ai = d_ai + g_ang * dt
        g_dt = g_e * a_r + g_ang * a_i
        dar_ref[...] = d_ar
        dai_ref[...] = d_ai
        dldt_ref[...] = g_dt * dt

    vm = BS(memory_space=pltpu.VMEM)
    vec = SDS((SSM_ROWS, SSM_STATE), F32)
    return pl.pallas_call(body, out_shape=(vec,) * 7, in_specs=[vm] * 9, out_specs=(vm,) * 7, name="ssm_prep_bwd",
                          compiler_params=_params())(a_r, a_i, ldt, b_r, b_i, g_abr, g_abi, d_win, d_woutt)


SCAN_ROWS = 512
SCAN_SUB = 128


def _ssm_scan(name, inp, w1, a_r, a_i, w2, reverse):
    t = inp.shape[0]
    rows = min(SCAN_ROWS, t)
    n = t // rows
    n_sub = rows // SCAN_SUB
    ch = SSM_CH
    at = (lambda i: (n - 1 - i, 0)) if reverse else (lambda i: (i, 0))

    def body(in_ref, w1_ref, ar_ref, ai_ref, w2_ref, sb_ref, out_ref, cr_ref, ci_ref, k_ref, st_ref):
        i = pl.program_id(0)

        @pl.when(i == 0)
        def _():
            ar8 = jnp.broadcast_to(ar_ref[...], (8, ch))
            ai8 = jnp.broadcast_to(ai_ref[...], (8, ch))
            row = lax.broadcasted_iota(jnp.int32, (8, ch), 0)
            rank = (7 - row) if reverse else row
            powers = [(ar8, ai8)]
            for _ in range(7):
                p_r, p_i = powers[-1]
                powers.append((p_r * ar8 - p_i * ai8, p_r * ai8 + p_i * ar8))
            zero = jnp.zeros((8, ch), F32)
            for slot, k in enumerate((1, 2, 4)):
                k_ref[2 * slot] = jnp.where(rank >= k, powers[k - 1][0], zero)
                k_ref[2 * slot + 1] = jnp.where(rank >= k, powers[k - 1][1], zero)
            carry_r, carry_i = zero, zero
            for j in range(8):
                carry_r = jnp.where(rank == j, powers[j][0], carry_r)
                carry_i = jnp.where(rank == j, powers[j][1], carry_i)
            k_ref[6] = carry_r
            k_ref[7] = carry_i
            cr_ref[...] = zero
            ci_ref[...] = zero

        def group(r0, carry):
            c_r, c_i = carry
            x_r = st_ref[pl.ds(r0, 8), 0:ch]
            x_i = st_ref[pl.ds(r0, 8), ch:2 * ch]
            for slot, k in enumerate((1, 2, 4)):
                shift = (8 - k) if reverse else k
                s_r = pltpu.roll(x_r, shift, 0)
                s_i = pltpu.roll(x_i, shift, 0)
                m_r, m_i = k_ref[2 * slot], k_ref[2 * slot + 1]
                x_r, x_i = x_r + m_r * s_r - m_i * s_i, x_i + m_r * s_i + m_i * s_r
            p_r, p_i = k_ref[6], k_ref[7]
            x_r, x_i = x_r + p_r * c_r - p_i * c_i, x_i + p_r * c_i + p_i * c_r
            st_ref[pl.ds(r0, 8), 0:ch] = x_r
            st_ref[pl.ds(r0, 8), ch:2 * ch] = x_i
            last = 0 if reverse else 7
            return (jnp.broadcast_to(x_r[last:last + 1, :], (8, ch)), jnp.broadcast_to(x_i[last:last + 1, :], (8, ch)))

        carry = (cr_ref[...], ci_ref[...])
        for sc in (range(n_sub - 1, -1, -1) if reverse else range(n_sub)):
            part = pl.ds(sc * SCAN_SUB, SCAN_SUB)
            st_ref[part, :] = _dot(in_ref[part, :], w1_ref[...])
            for gi in range(SCAN_SUB // 8):
                g = (SCAN_SUB // 8 - 1 - gi) if reverse else gi
                carry = group(sc * SCAN_SUB + g * 8, carry)
            states = st_ref[part, :].astype(BF16)
            sb_ref[part, :] = states
            out_ref[part, :] = _dot(states, w2_ref[...])
        cr_ref[...] = carry[0]
        ci_ref[...] = carry[1]

    return pl.pallas_call(
        body, out_shape=(SDS((t, 2 * ch), BF16), SDS((t, D_SSM), F32)), grid=(n,),
        in_specs=[BS((rows, D_SSM), at), BS((D_SSM, 2 * ch), lambda i: (0, 0)), BS((1, ch), lambda i: (0, 0)),
                  BS((1, ch), lambda i: (0, 0)), BS((2 * ch, D_SSM), lambda i: (0, 0))],
        out_specs=(BS((rows, 2 * ch), at), BS((rows, D_SSM), at)),
        scratch_shapes=[pltpu.VMEM((8, ch), F32), pltpu.VMEM((8, ch), F32), pltpu.VMEM((8, 8, ch), F32),
                        pltpu.VMEM((rows, 2 * ch), F32)],
        name=name, compiler_params=_params("arbitrary"))(inp, w1, a_r, a_i, w2)


DA_ROWS = 1024


def _ssm_param_grads(name, lam, states, u, dy, reverse, after=()):
    t = lam.shape[0]
    rows = min(DA_ROWS, t)
    n = t // rows
    halo_rows = 16
    nb = rows // halo_rows
    ch = SSM_CH
    if reverse:
        halo_at = lambda i: (jnp.minimum((i + 1) * nb, t // halo_rows - 1), 0)
    else:
        halo_at = lambda i: (jnp.maximum(i * nb - 1, 0), 0)

    after_ops, after_specs = _after_operands(after)

    def body(lam_ref, x_ref, halo_ref, u_ref, dy_ref, *rest):
        dr_ref, di_ref, dwin_ref, dwoutt_ref = rest[len(after_ops):]
        i = pl.program_id(0)

        @pl.when(i == 0)
        def _():
            dr_ref[...] = jnp.zeros_like(dr_ref)
            di_ref[...] = jnp.zeros_like(di_ref)
            dwin_ref[...] = jnp.zeros_like(dwin_ref)
            dwoutt_ref[...] = jnp.zeros_like(dwoutt_ref)

        dwin_ref[...] += _dot(u_ref[...], lam_ref[...], TN)
        dwoutt_ref[...] += _dot(dy_ref[...], x_ref[...], TN)
        row = lax.broadcasted_iota(jnp.int32, (rows, ch), 0)
        if reverse:
            edge, shift, h_row, live = rows - 1, rows - 1, 0, i < n - 1
        else:
            edge, shift, h_row, live = 0, 1, halo_rows - 1, i > 0

        def neighbour(lo):
            halo = halo_ref[:, lo:lo + ch].astype(F32)[h_row:h_row + 1]
            halo = jnp.where(live, halo, 0.0)
            x = x_ref[:, lo:lo + ch].astype(F32)
            return jnp.where(row == edge, jnp.broadcast_to(halo, (rows, ch)), pltpu.roll(x, shift, 0))

        xp_r, xp_i = neighbour(0), neighbour(ch)
        l_r, l_i = lam_ref[:, 0:ch].astype(F32), lam_ref[:, ch:2 * ch].astype(F32)
        dr_ref[...] += jnp.sum(l_r * xp_r + l_i * xp_i, axis=0, keepdims=True)
        di_ref[...] += jnp.sum(l_i * xp_r - l_r * xp_i, axis=0, keepdims=True)

    blk = BS((rows, 2 * ch), lambda i: (i, 0))
    thin = BS((rows, D_SSM), lambda i: (i, 0))
    vec = BS((1, ch), lambda i: (0, 0))
    mat = BS((D_SSM, 2 * ch), lambda i: (0, 0))
    return pl.pallas_call(
        body, out_shape=(SDS((1, ch), F32), SDS((1, ch), F32), SDS((D_SSM, 2 * ch), F32), SDS((D_SSM, 2 * ch), F32)),
        grid=(n,), in_specs=[blk, blk, BS((halo_rows, 2 * ch), halo_at), thin, thin] + after_specs,
        out_specs=(vec, vec, mat, mat),
        name=name, compiler_params=_params("arbitrary"))(lam, states, states, u, dy, *after_ops)


GELU_C = math.sqrt(2.0 / math.pi)
GELU_K = 0.044715


def _ssm_combine(proj, y_fwd, y_bwd, d_skip, tm, after=()):
    t = proj.shape[0]
    after_ops, after_specs = _after_operands(after)

    def body(s_ref, yf_ref, yb_ref, d_ref, *rest):
        yt_ref, g_ref = rest[len(after_ops):]
        y = s_ref[...] * d_ref[...] + yf_ref[...] + yb_ref[...]
        yt_ref[...] = y
        th = jnp.tanh(GELU_C * (y + GELU_K * y * y * y))
        g_ref[...] = (0.5 * y * (1.0 + th)).astype(BF16)

    blk = BS((tm, D_SSM), lambda i: (i, 0))
    return pl.pallas_call(
        body, out_shape=(SDS((t, D_SSM), F32), SDS((t, D_SSM), BF16)), grid=(t // tm,),
        in_specs=[BS((tm, D_SSM), lambda i: (i, D_POOL // D_SSM)), blk, blk, BS((1, D_SSM), lambda i: (0, 0))] + after_specs,
        out_specs=(blk, blk), name="ssm_combine",
        compiler_params=_params("parallel"))(proj, y_fwd, y_bwd, d_skip, *after_ops)


def _ssm_ds(proj, d_yt, du_fwd, du_bwd, d_skip, tm):
    t = proj.shape[0]

    def body(s_ref, dy_ref, duf_ref, dub_ref, d_ref, ds_ref, dd_ref):
        i = pl.program_id(0)
        dy = dy_ref[...]
        ds_ref[...] = (dy * d_ref[...] + duf_ref[...] + dub_ref[...]).astype(BF16)

        @pl.when(i == 0)
        def _():
            dd_ref[...] = jnp.zeros_like(dd_ref)

        dd_ref[...] += jnp.sum(dy * s_ref[...], axis=0, keepdims=True)

    blk = BS((tm, D_SSM), lambda i: (i, 0))
    vec = BS((1, D_SSM), lambda i: (0, 0))
    return pl.pallas_call(
        body, out_shape=(SDS((t, D_SSM), BF16), SDS((1, D_SSM), F32)), grid=(t // tm,),
        in_specs=[BS((tm, D_SSM), lambda i: (i, D_POOL // D_SSM)), blk, blk, blk, vec],
        out_specs=(blk, vec), name="ssm_ds", compiler_params=_params("arbitrary"))(proj, d_yt, du_fwd, du_bwd, d_skip)


G_POOL_AT = D_POOL + D_SSM
G_SSM_AT = G_POOL_AT + D_MODEL
E_VAL, E_GATE = D_POOL, D_POOL + D_SSM


def _merge_specs(tm):
    return [BS((tm, D_POOL), lambda i: (i, 0)), BS((tm, D_SSM), lambda i: (i, 0)),
            BS((N_SHARD, 1024, 256), lambda i: (0, 0, 0)), BS((tm, D_FF), lambda i: (i, 0))]


def _merge_parts(s, ms, yv, w_ref, proj_ref):
    lo = 256 * s
    zp = _dot(ms, w_ref[s, 0:E_VAL, :])
    zv = _dot(yv, w_ref[s, E_VAL:E_GATE, :])
    zg = _dot(yv, w_ref[s, E_GATE:, :])
    return zp, zv, zg, proj_ref[:, G_POOL_AT + lo:G_POOL_AT + lo + 256], proj_ref[:, G_SSM_AT + lo:G_SSM_AT + lo + 256]


def _mixer_merge(ms, yssm, w_e, proj, tm):
    t = ms.shape[0]

    def body(ms_ref, y_ref, w_ref, proj_ref, o_ref):
        msv, yv = ms_ref[...], y_ref[...]
        for s in range(N_SHARD):
            zp, zv, zg, gp, gs = _merge_parts(s, msv, yv, w_ref, proj_ref)
            o_ref[:, 256 * s:256 * (s + 1)] = (_sigmoid(gp) * zp + _sigmoid(gs) * zv * _sigmoid(zg)).astype(BF16)

    row = BS((tm, D_MODEL), lambda i: (i, 0))
    return pl.pallas_call(
        body, out_shape=SDS((t, D_MODEL), BF16), grid=(t // tm,), in_specs=_merge_specs(tm), out_specs=row,
        name="mixer_merge", compiler_params=_params("parallel"))(ms, yssm, w_e, proj)


def _mixer_merge_bwd(ms, yssm, w_e, proj, dmerged, tm):
    t = ms.shape[0]

    def body(ms_ref, y_ref, w_ref, proj_ref, dm_ref, dgp_ref, dgs_ref, dzp_ref, dzv_ref, dzg_ref):
        msv, yv = ms_ref[...], y_ref[...]
        for s in range(N_SHARD):
            cols = slice(256 * s, 256 * (s + 1))
            zp, zv, zg, gp, gs = _merge_parts(s, msv, yv, w_ref, proj_ref)
            dm = dm_ref[:, cols].astype(F32)
            sp, ss, sg = _sigmoid(gp), _sigmoid(gs), _sigmoid(zg)
            dgp_ref[:, cols] = (dm * zp * sp * (1.0 - sp)).astype(BF16)
            dgs_ref[:, cols] = (dm * zv * sg * ss * (1.0 - ss)).astype(BF16)
            dzp_ref[:, cols] = (dm * sp).astype(BF16)
            dz = dm * ss
            dzv_ref[:, cols] = (dz * sg).astype(BF16)
            dzg_ref[:, cols] = (dz * zv * sg * (1.0 - sg)).astype(BF16)

    row = BS((tm, D_MODEL), lambda i: (i, 0))
    shape = SDS((t, D_MODEL), BF16)
    return pl.pallas_call(
        body, out_shape=(shape,) * 5, grid=(t // tm,), in_specs=_merge_specs(tm) + [row],
        out_specs=(row,) * 5, name="mixer_merge_bwd",
        compiler_params=_params("parallel"))(ms, yssm, w_e, proj, dmerged)


def _mixer_dw(ms, yssm, dzp, dzv, dzg, tm):
    t = ms.shape[0]
    tm = min(2 * tm, t)
    n_t = t // tm

    def body(ms_ref, y_ref, dzp_ref, dzv_ref, dzg_ref, o_ref, acc):
        i = pl.program_id(0)

        @pl.when(i == 0)
        def _():
            acc[...] = jnp.zeros_like(acc)

        msv, yv = ms_ref[...], y_ref[...]
        for s in range(N_SHARD):
            cols = slice(256 * s, 256 * (s + 1))
            acc[s, 0:E_VAL, :] += _dot(msv, dzp_ref[:, cols], TN)
            acc[s, E_VAL:E_GATE, :] += _dot(yv, dzv_ref[:, cols], TN)
            acc[s, E_GATE:, :] += _dot(yv, dzg_ref[:, cols], TN)

        @pl.when(i == n_t - 1)
        def _():
            o_ref[...] = acc[...].astype(BF16)

    row = BS((tm, D_MODEL), lambda i: (i, 0))
    full = BS((N_SHARD, 1024, 256), lambda i: (0, 0, 0))
    return pl.pallas_call(
        body, out_shape=SDS((N_SHARD, 1024, 256), BF16), grid=(n_t,),
        in_specs=[BS((tm, D_POOL), lambda i: (i, 0)), BS((tm, D_SSM), lambda i: (i, 0)), row, row, row],
        out_specs=full, scratch_shapes=[pltpu.VMEM((N_SHARD, 1024, 256), F32)],
        name="mixer_dw", compiler_params=_params("arbitrary"))(ms, yssm, dzp, dzv, dzg)


def _mixer_dx(dzp, dzv, dzg, w_e, y_total, tm):
    t = dzp.shape[0]

    def body(dzp_ref, dzv_ref, dzg_ref, w_ref, yt_ref, dms_ref, dy_ref):
        acc_ms, acc_y = None, None
        for s in range(N_SHARD):
            cols = slice(256 * s, 256 * (s + 1))
            part_ms = _dot(dzp_ref[:, cols], w_ref[s, 0:E_VAL, :], NT)
            part_y = _dot(dzv_ref[:, cols], w_ref[s, E_VAL:E_GATE, :], NT) + _dot(dzg_ref[:, cols], w_ref[s, E_GATE:, :], NT)
            acc_ms = part_ms if s == 0 else acc_ms + part_ms
            acc_y = part_y if s == 0 else acc_y + part_y
        dms_ref[...] = acc_ms.astype(BF16)
        y = yt_ref[...]
        th = jnp.tanh(GELU_C * (y + GELU_K * y * y * y))
        dgelu = 0.5 * (1.0 + th) + 0.5 * y * (1.0 - th * th) * GELU_C * (1.0 + 3.0 * GELU_K * y * y)
        dy_ref[...] = acc_y * dgelu

    row = BS((tm, D_MODEL), lambda i: (i, 0))
    return pl.pallas_call(
        body, out_shape=(SDS((t, D_POOL), BF16), SDS((t, D_SSM), F32)), grid=(t // tm,),
        in_specs=[row, row, row, BS((N_SHARD, 1024, 256), lambda i: (0, 0, 0)), BS((tm, D_SSM), lambda i: (i, 0))],
        out_specs=(BS((tm, D_POOL), lambda i: (i, 0)), BS((tm, D_SSM), lambda i: (i, 0))),
        name="mixer_dx", compiler_params=_params("parallel"))(dzp, dzv, dzg, w_e, y_total)


def _attn_probs(q_h, k_h):
    s = _dot(q_h, k_h, NT) * (1.0 / math.sqrt(HEAD_DIM))
    e = jnp.exp(s - jnp.max(s, axis=-1, keepdims=True))
    return e / jnp.sum(e, axis=-1, keepdims=True)


def _attn_fwd(q, kv, tm):
    t = q.shape[0]
    m = kv.shape[0]

    def body(q_ref, kv_ref, o_ref):
        for hd in range(N_HEADS):
            lo = hd * HEAD_DIM
            p = _attn_probs(q_ref[:, lo:lo + HEAD_DIM], kv_ref[:, lo:lo + HEAD_DIM])
            o_ref[:, lo:lo + HEAD_DIM] = _dot(p, kv_ref[:, D_MODEL + lo:D_MODEL + lo + HEAD_DIM]).astype(BF16)

    return pl.pallas_call(
        body, out_shape=SDS((t, D_MODEL), BF16), grid=(t // tm,),
        in_specs=[BS((tm, D_MODEL), lambda i: (i, 0)), BS((m, 2 * D_MODEL), lambda i: (0, 0))],
        out_specs=BS((tm, D_MODEL), lambda i: (i, 0)), name="attn_fwd", compiler_params=_params("parallel"))(q, kv)


def _attn_bwd(q, kv, d_o, tm):
    t = q.shape[0]
    m = kv.shape[0]

    def body(q_ref, kv_ref, do_ref, dq_ref, dkv_ref):
        i = pl.program_id(0)

        @pl.when(i == 0)
        def _():
            dkv_ref[...] = jnp.zeros_like(dkv_ref)

        for hd in range(N_HEADS):
            lo = hd * HEAD_DIM
            q_h = q_ref[:, lo:lo + HEAD_DIM]
            k_h = kv_ref[:, lo:lo + HEAD_DIM]
            v_h = kv_ref[:, D_MODEL + lo:D_MODEL + lo + HEAD_DIM]
            do_h = do_ref[:, lo:lo + HEAD_DIM]
            p = _attn_probs(q_h, k_h)
            dkv_ref[:, D_MODEL + lo:D_MODEL + lo + HEAD_DIM] += _dot(p, do_h, TN)
            dp = _dot(do_h, v_h, NT)
            ds = p * (dp - jnp.sum(dp * p, axis=-1, keepdims=True)) * (1.0 / math.sqrt(HEAD_DIM))
            dq_ref[:, lo:lo + HEAD_DIM] = _dot(ds, k_h).astype(BF16)
            dkv_ref[:, lo:lo + HEAD_DIM] += _dot(ds, q_h, TN)

    row = BS((tm, D_MODEL), lambda i: (i, 0))
    full = BS((m, 2 * D_MODEL), lambda i: (0, 0))
    return pl.pallas_call(
        body, out_shape=(SDS((t, D_MODEL), BF16), SDS((m, 2 * D_MODEL), F32)), grid=(t // tm,),
        in_specs=[row, full, row], out_specs=(row, full), name="attn_bwd",
        compiler_params=_params("arbitrary"))(q, kv, d_o)


TRANSPOSED = ("ffn1_w_gate", "ffn1_w_up", "ffn2_w_gate", "ffn2_w_up", "w_in")
GATHER_PHASES = {"f1a": (("ffn1_w_gate", "ffn1_w_up"),),
                 "f1b": (("ffn1_w_down",),),
                 "win": (("w_in",),),
                 "mix": (("w_mix_out", "w_q", "w_xo"), ("w_kv",), ("w_pool_proj", "w_glu_val", "w_glu_gate")),
                 "f2": (("ffn2_w_gate", "ffn2_w_up", "ffn2_w_down"),)}
REDUCE_GROUPS = (("ffn2_w_gate", "ffn2_w_up", "ffn2_w_down"), ("w_xo",), ("w_q",), ("w_kv",), ("w_mix_out",),
                 ("w_pool_proj", "w_glu_val", "w_glu_gate"), ("w_in",), ("ffn1_w_gate", "ffn1_w_up", "ffn1_w_down"))
SMALL = ("ffn1_norm", "mix_norm", "pool_w", "pool_scale", "ssm_a_re", "ssm_a_im", "ssm_log_dt", "ssm_b_re",
         "ssm_b_im", "ssm_c_re", "ssm_c_im", "ssm_d", "xattn_norm", "mem_norm", "ffn2_norm", "final_norm")
WEIGHTS = ("ffn1_norm", "ffn1_w_gate", "ffn1_w_up", "ffn1_w_down", "mix_norm", "w_in", "pool_w", "pool_scale",
           "w_pool_proj", "ssm_a_re", "ssm_a_im", "ssm_log_dt", "ssm_b_re", "ssm_b_im", "ssm_c_re", "ssm_c_im",
           "ssm_d", "w_glu_val", "w_glu_gate", "w_mix_out", "xattn_norm", "mem_norm", "w_q", "w_kv", "w_xo",
           "ffn2_norm", "ffn2_w_gate", "ffn2_w_up", "ffn2_w_down", "final_norm")


def _small_view(a, n):
    return jnp.swapaxes(a, 3, 4) if n in ("ssm_b_re", "ssm_b_im") else a


def _device_step(x, mem, target, wts, sp, reducer=None):
    t = x.shape[0]
    tm = min(TM, t)
    g = {}

    first_gather = wts.start("f1a")
    u1 = _rmsnorm("norm_ffn1", x, sp["ffn1_norm"], tm, after=first_gather)

    def per_channel(a):
        a = a.reshape(2 * SSM_GROUPS, 1, -1)
        return jnp.broadcast_to(a, (2 * SSM_GROUPS, SSM_GROUP, a.shape[-1])).reshape(SSM_ROWS, a.shape[-1])

    ssm_a = per_channel(sp["ssm_a_re"]), per_channel(sp["ssm_a_im"]), per_channel(sp["ssm_log_dt"])
    ssm_b = sp["ssm_b_re"].reshape(SSM_ROWS, SSM_STATE), sp["ssm_b_im"].reshape(SSM_ROWS, SSM_STATE)
    abr, abi, w_in_s, w_in_s_t, w_out_s_t, w_out_s = _ssm_prep(
        *ssm_a, *ssm_b, sp["ssm_c_re"].reshape(SSM_ROWS, SSM_STATE), sp["ssm_c_im"].reshape(SSM_ROWS, SSM_STATE),
        after=first_gather)
    first_rows = (2, SSM_GROUPS, SSM_GROUP, SSM_STATE)
    a_r = abr.reshape(first_rows)[:, :, 0].reshape(2, 1, SSM_CH)
    a_i = abi.reshape(first_rows)[:, :, 0].reshape(2, 1, SSM_CH)
    mem_n = _rmsnorm("norm_mem", mem, sp["mem_norm"], mem.shape[0], after=first_gather)

    (w_gu,) = wts.finish("f1a", [u1, w_in_s, w_in_s_t, w_out_s, w_out_s_t, a_r, a_i, mem_n])
    w_f1 = {"gate": (w_gu, FFN_GATE), "up": (w_gu, FFN_UP)}
    down_gather = wts.start("f1b", [w_gu])
    g1, up1, a1 = _ffn_up("ffn1_up", u1, w_f1, tm, after=wts.start("win", down_gather))
    (w_dn,) = wts.finish("f1b", [a1])
    w_f1["down"] = (w_dn, 0)
    h1, u2 = _ffn_down("ffn1_down", a1, w_f1, x, tm, next_gain=sp["mix_norm"])

    (w_in_g,) = wts.finish("win", [u2])
    w_in_t = w_in_g.reshape(D_FF, D_MODEL)
    proj = _mm("mix_in", [(u2, BS((tm, D_MODEL), lambda j, i: (i, 0)), w_in_t, BS((D_FF // 2, D_MODEL), lambda j, i: (j, 0)), NT)],
               grid=(2, t // tm), out_shape=SDS((t, D_FF), F32), out_spec=BS((tm, D_FF // 2), lambda j, i: (i, j)),
               after=wts.start("f2", wts.start("mix", [w_in_g])))
    pooled, mixed, ms = _pool_fwd(proj, sp["pool_w"][0], sp["pool_scale"])

    s_in = proj[:, D_POOL:D_POOL + D_SSM].astype(BF16)
    states, y_dirs = [], []
    for dr in range(2):
        st, yd = _ssm_scan(f"ssm_scan_fwd{dr}", s_in, w_in_s[dr], a_r[dr], a_i[dr], w_out_s[dr], reverse=(dr == 1))
        states.append(st)
        y_dirs.append(yd)
    w_sq, w_kv, w_e = wts.finish("mix", y_dirs)
    w_mo, w_q, w_xo = (w_sq[:, 256 * k:256 * (k + 1)].reshape(D_MODEL, D_MODEL) for k in range(3))
    w_d = w_kv[:, None]
    y_total, yssm = _ssm_combine(proj, y_dirs[0], y_dirs[1], sp["ssm_d"], tm)

    merged = _mixer_merge(ms, yssm, w_e, proj, tm)
    h2, u3 = _mm_resid_norm("mix_out", merged, w_mo, h1, sp["xattn_norm"], tm)

    q = _plain_mm("attn_q", u3, w_q, NN, BF16, tm)
    n_mem = mem.shape[0]
    kv = _mm("attn_kv", [(mem_n, BS((n_mem, D_MODEL), lambda s: (0, 0)), w_d, BS((None, None, D_MODEL, 512), lambda s: (s, 0, 0, 0)), NN)],
             grid=(N_SHARD,), out_shape=SDS((n_mem, 2 * D_MODEL), BF16), out_spec=BS((n_mem, 512), lambda s: (0, s)))
    o = _attn_fwd(q, kv, tm)
    h3, u4 = _mm_resid_norm("attn_out", o, w_xo, h2, sp["ffn2_norm"], tm)

    (w_2,) = wts.finish("f2", [u4])
    w_f2 = {"gate": (w_2, FFN_GATE), "up": (w_2, FFN_UP), "down": (w_2, FFN_DOWN)}
    g2, up2, a2 = _ffn_up("ffn2_up", u4, w_f2, tm)
    loss, dh4, dh4_b, g["final_norm"] = _ffn_down("ffn2_down", a2, w_f2, h3, tm,
                                                  head=(sp["final_norm"].reshape(1, D_MODEL), target))

    dg2, dup2 = _ffn_bwd_act("ffn2_bwd_act", dh4_b, w_f2, g2, up2, tm)
    dw_f2 = _ffn_dw("ffn2_dw", u4, dg2, dup2, a2, dh4_b, tm)
    dh3, dh3_b, g["ffn2_norm"] = _ffn_dx("ffn2_dx", dg2, dup2, w_f2, h3, sp["ffn2_norm"], dh4, tm)

    d_o = _plain_mm("attn_out_dx", dh3_b, w_xo, NT, BF16, tm)
    dw_xo = _dw_mm("attn_out_dw", o, dh3_b, tm)
    dq, dkv = _attn_bwd(q, kv, d_o, tm)
    dw_q = _dw_mm("attn_q_dw", u3, dq, tm)
    dh2, dh2_b, g["xattn_norm"] = _mm_norm_bwd("attn_q_dx", dq, w_q, NT, h2, sp["xattn_norm"], dh3, tm)
    dw_kv = _mm("attn_kv_dw", [(mem_n, BS((n_mem, D_MODEL), lambda s: (0, 0)), dkv, BS((n_mem, 512), lambda s: (0, s)), TN)],
                grid=(N_SHARD,), out_shape=SDS((N_SHARD, D_MODEL, 512), BF16), out_spec=BS((None, D_MODEL, 512), lambda s: (s, 0, 0)))
    dmem_n = _mm("attn_kv_dx", [(dkv, BS((n_mem, 512), lambda s: (0, s)), w_d, BS((None, None, D_MODEL, 512), lambda s: (s, 0, 0, 0)), NT)],
                 grid=(N_SHARD,), red_axis=0, out_shape=SDS((n_mem, D_MODEL), F32), out_spec=BS((n_mem, D_MODEL), lambda s: (0, 0)))
    _, _, g["mem_norm"] = _rmsnorm_bwd("norm_mem_bwd", mem, sp["mem_norm"], dmem_n, None, n_mem)

    square = (N_SHARD, D_MODEL // N_SHARD, D_MODEL)
    early = [dw_f2.reshape(N_SHARD, 3 * FF_SH, D_MODEL), dw_xo.reshape(square), dw_q.reshape(square), dw_kv]
    swapping = reducer.swap_start("a1", early) if reducer is not None else []
    dmerged = _plain_mm("mix_out_dx", dh2_b, w_mo, NT, BF16, tm, after=swapping)
    dw_mo = _dw_mm("mix_out_dw", merged, dh2_b, tm)
    d_gp, d_gs, dzp, dzv, dzg = _mixer_merge_bwd(ms, yssm, w_e, proj, dmerged, tm)
    dw_e = _mixer_dw(ms, yssm, dzp, dzv, dzg, tm)
    d_ms, d_yt = _mixer_dx(dzp, dzv, dzg, w_e, y_total, tm)
    dp, d_scale, d_pw = _pool_bwd(d_ms, mixed, pooled, sp["pool_w"][0], sp["pool_scale"])
    g["pool_scale"] = d_scale
    g["pool_w"] = d_pw[None]

    d_yt_b = d_yt.astype(BF16)
    du_dirs, lams = [], []
    for dr in range(2):
        lam, du = _ssm_scan(f"ssm_scan_bwd{dr}", d_yt_b, w_out_s_t[dr], a_r[dr], -a_i[dr], w_in_s_t[dr], reverse=(dr == 0))
        du_dirs.append(du)
        lams.append(lam)
    ds, g["ssm_d"] = _ssm_ds(proj, d_yt, du_dirs[0], du_dirs[1], sp["ssm_d"], tm)

    d_proj = jnp.concatenate([dp, ds, d_gp, d_gs], axis=1)
    tw = min(2 * tm, t)
    dw_in_t = _mm("mix_in_dw", [(d_proj, BS((tw, D_FF // 2), lambda j, i: (i, j)), u2, BS((tw, D_MODEL), lambda j, i: (i, 0)), TN)],
                  grid=(2, t // tw), red_axis=1, out_shape=SDS((D_FF, D_MODEL), BF16), out_spec=BS((D_FF // 2, D_MODEL), lambda j, i: (j, 0)))
    dh1, dh1_b, g["mix_norm"] = _mm_norm_bwd("mix_in_dx", d_proj, w_in_t, NN, h1, sp["mix_norm"], dh2, tm)

    early += [dw_mo.reshape(square), dw_e, dw_in_t.reshape(N_SHARD, FF_SH, D_MODEL)]
    g["final_norm"] = g["final_norm"].reshape(D_MODEL)

    travelling = reducer.start("a", early[4:], swapped=["a1"], after=list(g.values())) if reducer is not None else []
    d_abr, d_abi, d_cm, d_bm = [], [], [], []
    for dr in range(2):
        da_r, da_i, d_win, d_woutt = _ssm_param_grads(f"ssm_param_grads{dr}", lams[dr], states[dr], s_in, d_yt_b,
                                                      reverse=(dr == 1), after=travelling)
        d_abr.append(da_r)
        d_abi.append(da_i)
        d_bm.append(d_win)
        d_cm.append(d_woutt)

    def first_channel(da):
        da = jnp.stack(da).reshape(2, SSM_GROUPS, 1, SSM_STATE)
        return jnp.pad(da, ((0, 0), (0, 0), (0, SSM_GROUP - 1), (0, 0))).reshape(SSM_ROWS, SSM_STATE)

    d_ar, d_ai, d_ldt, d_br, d_bi, d_cr, d_ci = _ssm_prep_bwd(
        *ssm_a, *ssm_b, first_channel(d_abr), first_channel(d_abi), jnp.stack(d_bm), jnp.stack(d_cm))
    per_group = (2 * SSM_GROUPS, SSM_GROUP * SSM_STATE)
    g["ssm_a_re"] = d_ar.reshape(2 * SSM_GROUPS, SSM_GROUP, SSM_STATE).sum(axis=1).reshape(sp["ssm_a_re"].shape)
    g["ssm_a_im"] = d_ai.reshape(2 * SSM_GROUPS, SSM_GROUP, SSM_STATE).sum(axis=1).reshape(sp["ssm_a_im"].shape)
    g["ssm_log_dt"] = d_ldt.reshape(per_group).sum(axis=1).reshape(sp["ssm_log_dt"].shape)
    g["ssm_b_re"] = d_br.reshape(sp["ssm_b_re"].shape)
    g["ssm_b_im"] = d_bi.reshape(sp["ssm_b_im"].shape)
    g["ssm_c_re"] = d_cr.reshape(sp["ssm_c_re"].shape)
    g["ssm_c_im"] = d_ci.reshape(sp["ssm_c_im"].shape)
    if reducer is not None:
        travelling = travelling + [d_ar, d_br, d_cr]
    dg1, dup1 = _ffn_bwd_act("ffn1_bwd_act", dh1_b, w_f1, g1, up1, tm, after=travelling)
    dw_f1 = _ffn_dw("ffn1_dw", u1, dg1, dup1, a1, dh1_b, tm).reshape(N_SHARD, 3 * FF_SH, D_MODEL)
    if reducer is not None:
        travelling = reducer.start("b", [dw_f1], after=reducer.finish("a", [dw_f1]))
        travelling = travelling + reducer.join_start("a", after=travelling)
    grad_x, _, g["ffn1_norm"] = _ffn_dx("ffn1_dx", dg1, dup1, w_f1, x, sp["ffn1_norm"], dh1, tm, after=travelling)
    if reducer is not None:
        reducer.finish("b", [grad_x])
        reducer.join_finish("a", [grad_x])
    return loss, grad_x, early + [dw_f1], g


def _mesh_place():
    x, y, c = lax.axis_index("x"), lax.axis_index("y"), lax.axis_index("c")
    chips = [(1 - x, y), (x, 1 - y), (1 - x, 1 - y)]
    return x, y, c, chips


def _remote(src, dst, send_sems, recv_sems, k, to):
    return pltpu.make_async_remote_copy(src_ref=src, dst_ref=dst, send_sem=send_sems.at[k], recv_sem=recv_sems.at[k],
                                        device_id=to, device_id_type=MESH)


def _sibling_swap_halves(tag, grads, after=()):
    n = len(grads)
    after_ops, after_specs = _after_operands(after)

    def body(*refs):
        ins, outs = refs[:n], refs[n + len(after_ops):2 * n + len(after_ops)]
        send_sems, recv_sems = refs[2 * n + len(after_ops):]
        x, y, c, _ = _mesh_place()
        sibling = (x, y, 1 - c)
        copies = []
        for k in range(n):
            half = grads[k].shape[1] // 2
            theirs = pl.ds(pl.multiple_of((1 - c) * half, 16), half)
            cp = _remote(ins[k].at[:, theirs, :], outs[k], send_sems, recv_sems, k, sibling)
            cp.start()
            copies.append(cp)
        for cp in copies:
            cp.wait_recv()
        for cp in copies:
            cp.wait_send()

    hbm = BS(memory_space=pl.ANY)
    return pl.pallas_call(
        body, out_shape=tuple(SDS((g.shape[0], g.shape[1] // 2, g.shape[2]), g.dtype) for g in grads),
        in_specs=[hbm] * n + after_specs, out_specs=(hbm,) * n,
        scratch_shapes=[pltpu.SemaphoreType.DMA((n,)), pltpu.SemaphoreType.DMA((n,))],
        name="reduce_sibling_send_" + tag, compiler_params=_params())(*grads, *after_ops)


def _row_tile(rows, cap=512):
    return max(r for r in range(16, cap + 1, 16) if rows % r == 0)


REDUCE_STEPS = 2


def _chip_presum(tag, grads, gots, c_idx):
    n = len(grads)
    halves = [g.shape[1] // 2 for g in grads]
    tiles = [(h // REDUCE_STEPS, g.shape[2]) for h, g in zip(halves, grads)]

    def body(c_ref, *refs):
        for k in range(n):
            refs[2 * n + k][...] = (refs[k][...].astype(F32) + refs[n + k][...].astype(F32)).astype(BF16)

    mine = [BS((None, None) + tile, lambda s, i, c_ref: (s, c_ref[0], i, 0)) for tile in tiles]
    plain = [BS((None,) + tile, lambda s, i, c_ref: (s, i, 0)) for tile in tiles]
    return list(pl.pallas_call(
        body, out_shape=tuple(SDS((g.shape[0], h, g.shape[2]), BF16) for g, h in zip(grads, halves)),
        grid_spec=pltpu.PrefetchScalarGridSpec(num_scalar_prefetch=1, grid=(N_SHARD, REDUCE_STEPS),
                                               in_specs=mine + plain, out_specs=plain),
        name="reduce_presum_" + tag, compiler_params=_params("parallel", "parallel"))(
            c_idx, *[g.reshape(g.shape[0], 2, h, g.shape[2]) for g, h in zip(grads, halves)], *gots))


HBM_SPEC = BS(memory_space=pltpu.HBM)
SEM_SPEC = BS(memory_space=pltpu.SEMAPHORE)
DATAFLOW = pltpu.SideEffectType.DATAFLOW_SIDE_EFFECTING


def _chip_exchange_copies(parts, lands, send_sems, recv_sems):
    _, _, c, chips = _mesh_place()
    return [_remote(parts[k].at[2 * px + py], lands[k].at[j], send_sems, recv_sems, 3 * k + j, (px, py, c))
            for k in range(len(parts)) for j, (px, py) in enumerate(chips)]


def _gather_copies(shards, lands, send_sems, recv_sems):
    x, y, c, chips = _mesh_place()
    return [_remote(shards[k], lands[k].at[2 * x + y], send_sems, recv_sems, 3 * k + j, (px, py, c))
            for k in range(len(shards)) for j, (px, py) in enumerate(chips)]


def _gather_half_copies(shards, lands, send_sems, recv_sems):
    x, y, c, chips = _mesh_place()
    out = []
    for k in range(len(shards)):
        half = shards[k].shape[0] // 2
        mine = pl.ds(pl.multiple_of(c * half, 16), half)
        for j, (px, py) in enumerate(chips):
            out.append(_remote(shards[k].at[mine, :], lands[k].at[2 * x + y, mine, :], send_sems, recv_sems,
                               3 * k + j, (px, py, c)))
    return out


def _sibling_fill(tag, lands):
    n = len(lands)

    def body(*refs):
        outs = refs[n:2 * n]
        send_sems, recv_sems = refs[2 * n:]
        x, y, c, chips = _mesh_place()
        copies = []
        for k in range(n):
            half = lands[k].shape[1] // 2
            mine = pl.ds(pl.multiple_of(c * half, 16), half)
            for j, (px, py) in enumerate(chips):
                blk = outs[k].at[2 * px + py, mine, :]
                copies.append(_remote(blk, blk, send_sems, recv_sems, 3 * k + j, (x, y, 1 - c)))
        for cp in copies:
            cp.start()
        for cp in copies:
            cp.wait_recv()
        for cp in copies:
            cp.wait_send()

    hbm = BS(memory_space=pl.ANY)
    return list(pl.pallas_call(
        body, out_shape=tuple(SDS(a.shape, a.dtype) for a in lands),
        in_specs=[hbm] * n, out_specs=(hbm,) * n, input_output_aliases={k: k for k in range(n)},
        scratch_shapes=[pltpu.SemaphoreType.DMA((3 * n,)), pltpu.SemaphoreType.DMA((3 * n,))],
        name="gather_fill_" + tag, compiler_params=_params())(*lands))


def _swap_copies(grads, lands, send_sems, recv_sems):
    x, y, c, _ = _mesh_place()
    out = []
    for k in range(len(grads)):
        half = grads[k].shape[1] // 2
        theirs = pl.ds(pl.multiple_of((1 - c) * half, 16), half)
        out.append(_remote(grads[k].at[:, theirs, :], lands[k], send_sems, recv_sems, k, (x, y, 1 - c)))
    return out


def _join_copies(fulls, same, send_sems, recv_sems):
    x, y, c, _ = _mesh_place()
    out = []
    for k in range(len(fulls)):
        half = fulls[k].shape[0] // 2
        mine = fulls[k].at[pl.ds(pl.multiple_of(c * half, 8), half), :]
        out.append(_remote(mine, mine, send_sems, recv_sems, k, (x, y, 1 - c)))
    return out


def _everyone_copies(packs, lands, send_sems, recv_sems):
    x, y, c, _ = _mesh_place()
    out = []
    for k in range(len(packs)):
        for j in range(N_DEV - 1):
            bx, by, bc = (j + 1) >> 2 & 1, (j + 1) >> 1 & 1, (j + 1) & 1
            peer = (x ^ bx, y ^ by, c ^ bc)
            out.append(_remote(packs[k], lands[k].at[4 * x + 2 * y + c], send_sems, recv_sems, (N_DEV - 1) * k + j, peer))
    return out


def _split_start(name, copies, sources, land_shapes, after=(), fanout=3):
    n = len(sources)
    n_land = len(land_shapes)
    m = n + n_land
    n_sems = fanout * n
    after_ops, after_specs = _after_operands(after)

    def body(*refs):
        ins = refs[:n]
        lands = refs[n:m] if n_land else ins
        send_sems, recv_sems = refs[m + len(after_ops)], refs[m + len(after_ops) + 1]
        token = refs[-1]
        for cp in copies(ins, lands, send_sems, recv_sems):
            cp.start()
        token[...] = jnp.zeros_like(token)

    lands = [pltpu.with_memory_space_constraint(lax.empty(s, d), pltpu.HBM) for s, d in land_shapes]
    sources = [pltpu.with_memory_space_constraint(p, pltpu.HBM) for p in sources]
    thru = [pltpu.HBM(a.shape, a.dtype) for a in sources + lands]
    out = pl.pallas_call(
        body, name=name,
        out_shape=(pltpu.SemaphoreType.DMA((n_sems,)), pltpu.SemaphoreType.DMA((n_sems,)), *thru, SDS((8, 128), F32)),
        in_specs=[HBM_SPEC] * m + after_specs,
        out_specs=(SEM_SPEC, SEM_SPEC, *[HBM_SPEC] * m, BS(memory_space=pltpu.VMEM)),
        input_output_aliases={i: 2 + i for i in range(m)},
        compiler_params=pltpu.CompilerParams(has_side_effects=DATAFLOW))(*sources, *lands, *after_ops)
    return out[0], out[1], list(out[2:2 + n]), list(out[2 + n:2 + m]), out[-1]


def _split_wait(name, copies, send_sems, recv_sems, sources, lands, after):
    n = len(sources)
    m = n + len(lands)
    after_ops, after_specs = _after_operands(after)

    def body(*refs):
        ins = refs[:n]
        zones = refs[n:m] if m > n else ins
        for cp in copies(ins, zones, refs[m], refs[m + 1]):
            cp.wait_send()
            cp.wait_recv()

    out = pl.pallas_call(
        body, name=name,
        out_shape=tuple(pltpu.HBM(a.shape, a.dtype) for a in sources + lands),
        in_specs=[HBM_SPEC] * m + [SEM_SPEC, SEM_SPEC] + after_specs, out_specs=(HBM_SPEC,) * m,
        input_output_aliases={i: i for i in range(m)},
        compiler_params=pltpu.CompilerParams(has_side_effects=DATAFLOW))(*sources, *lands, send_sems, recv_sems, *after_ops)
    return list(out[:n]), list(out[n:])


class _WeightGatherer:
    def __init__(self, shards):
        self.shards, self.open = shards, {}
        self.me = 2 * lax.axis_index("x") + lax.axis_index("y")

    HALVED = ("f1a", "mix")

    def start(self, tag, after=()):
        shapes = [((N_SHARD,) + s.shape, s.dtype) for s in self.shards[tag]]
        copies = _gather_half_copies if tag in self.HALVED else _gather_copies
        self.open[tag] = _split_start("gather_start_" + tag, copies, self.shards[tag], shapes, after)
        return [self.open[tag][-1]]

    def finish(self, tag, after):
        send_sems, recv_sems, shards, lands, _ = self.open.pop(tag)
        copies = _gather_half_copies if tag in self.HALVED else _gather_copies
        shards, lands = _split_wait("gather_wait_" + tag, copies, send_sems, recv_sems, shards, lands, after)
        if tag in self.HALVED:
            lands = _sibling_fill(tag, lands)
        return [lax.dynamic_update_slice(zone, s[None], (self.me, 0, 0)) for zone, s in zip(lands, shards)]


class _GradReducer:
    def __init__(self):
        self.c_idx = lax.axis_index("c").astype(jnp.int32).reshape(1)
        self.place = jnp.stack([2 * lax.axis_index("x") + lax.axis_index("y"), lax.axis_index("c")]).astype(jnp.int32)
        self.swaps, self.open, self.landed, self.joins, self.reduced = {}, {}, {}, {}, []

    def swap_start(self, tag, grads, after=()):
        shapes = [((g.shape[0], g.shape[1] // 2, g.shape[2]), g.dtype) for g in grads]
        self.swaps[tag] = _split_start("reduce_swap_start_" + tag, _swap_copies, grads, shapes, after, fanout=1)
        return [self.swaps[tag][-1]]

    def start(self, tag, grads, after=(), swapped=()):
        pairs = []
        for s in swapped:
            send_sems, recv_sems, early, lands, _ = self.swaps.pop(s)
            pairs += zip(*_split_wait("reduce_swap_wait_" + s, _swap_copies, send_sems, recv_sems, early, lands, grads[-1:]))
        pairs += zip(grads, _sibling_swap_halves(tag, grads, after))
        parts = _chip_presum(tag, [g for g, _ in pairs], [s for _, s in pairs], self.c_idx)
        shapes = [((3,) + p.shape[1:], p.dtype) for p in parts]
        self.open[tag] = _split_start("reduce_exchange_start_" + tag, _chip_exchange_copies, parts, shapes)
        return [self.open[tag][-1]]

    def finish(self, tag, after):
        send_sems, recv_sems, parts, lands, _ = self.open.pop(tag)
        self.landed[tag] = _split_wait("reduce_exchange_wait_" + tag, _chip_exchange_copies, send_sems, recv_sems, parts, lands, after)
        return self.landed[tag][1][:1]

    def _sums(self, tag, after=()):
        parts, landed = self.landed.pop(tag)
        return _chip_sum(tag, parts, landed, self.place, after)

    def join_start(self, tag, after=()):
        self.joins[tag] = _split_start("reduce_join_start_" + tag, _join_copies, self._sums(tag, after), [], fanout=1)
        return [self.joins[tag][-1]]

    def join_finish(self, tag, after):
        send_sems, recv_sems, fulls, _, _ = self.joins.pop(tag)
        self.reduced += _split_wait("reduce_join_wait_" + tag, _join_copies, send_sems, recv_sems, fulls, [], after)[0]

    def join(self, tag, after=()):
        self.reduced += _sibling_join_halves(self._sums(tag), after)


def _chip_sum(tag, parts, gots, place, after=()):
    n = len(parts)
    tiles = [(p.shape[1] // REDUCE_STEPS, p.shape[2]) for p in parts]
    after_ops, after_specs = _after_operands(after)

    def body(place_ref, *refs):
        outs = refs[2 * n + len(after_ops):]
        for k in range(n):
            acc = refs[k][...].astype(F32)
            for j in range(3):
                acc = acc + refs[n + k][j].astype(F32)
            outs[k][...] = acc

    return list(pl.pallas_call(
        body, out_shape=tuple(SDS((2 * p.shape[1], p.shape[2]), F32) for p in parts),
        grid_spec=pltpu.PrefetchScalarGridSpec(
            num_scalar_prefetch=1, grid=(REDUCE_STEPS,),
            in_specs=[BS((None,) + tile, lambda i, place_ref: (place_ref[0], i, 0)) for tile in tiles]
            + [BS((3,) + tile, lambda i, place_ref: (0, i, 0)) for tile in tiles] + after_specs,
            out_specs=[BS(tile, lambda i, place_ref: (place_ref[1] * REDUCE_STEPS + i, 0)) for tile in tiles]),
        name="reduce_sum_" + tag, compiler_params=_params("parallel"))(place, *parts, *gots, *after_ops))


def _sibling_join_halves(fulls, after=()):
    n = len(fulls)
    after_ops, after_specs = _after_operands(after)

    def body(*refs):
        outs = refs[n + len(after_ops):2 * n + len(after_ops)]
        send_sems, recv_sems = refs[2 * n + len(after_ops):]
        copies = _join_copies(outs, outs, send_sems, recv_sems)
        for cp in copies:
            cp.start()
        for cp in copies:
            cp.wait_recv()
        for cp in copies:
            cp.wait_send()

    hbm = BS(memory_space=pl.ANY)
    return list(pl.pallas_call(
        body, out_shape=tuple(SDS(f.shape, f.dtype) for f in fulls),
        in_specs=[hbm] * n + after_specs, out_specs=(hbm,) * n, input_output_aliases={k: k for k in range(n)},
        scratch_shapes=[pltpu.SemaphoreType.DMA((n,)), pltpu.SemaphoreType.DMA((n,))],
        name="reduce_sibling_join", compiler_params=_params())(*fulls, *after_ops))


N_DEV = 8


def _sum_devices(packs):
    _, rows, lanes = packs.shape

    def body(p_ref, o_ref):
        acc = p_ref[0]
        for dev in range(1, N_DEV):
            acc = acc + p_ref[dev]
        o_ref[...] = acc

    vm = BS(memory_space=pltpu.VMEM)
    return pl.pallas_call(body, out_shape=SDS((rows, lanes), F32), in_specs=[vm], out_specs=vm,
                          name="small_sum", compiler_params=_params())(packs)


def _adamw_refs(w_ref, g_ref, m_ref, v_ref, go_ref, d_ref, mo_ref, vo_ref):
    bc1 = 1.0 - ADAM_B1 ** ADAM_STEP
    bc2 = 1.0 - ADAM_B2 ** ADAM_STEP
    g = g_ref[...]
    m_new = ADAM_B1 * m_ref[...] + (1.0 - ADAM_B1) * g
    v_new = ADAM_B2 * v_ref[...] + (1.0 - ADAM_B2) * (g * g)
    go_ref[...] = g
    mo_ref[...] = m_new
    vo_ref[...] = v_new
    d_ref[...] = -ADAM_LR * ((m_new / bc1) / (jnp.sqrt(v_new / bc2) + ADAM_EPS) + ADAM_WD * w_ref[...])


def _adamw_small(ws, gs, ms, vs):
    n = len(ws)

    def body(*refs):
        for k in range(n):
            _adamw_refs(*[refs[j * n + k] for j in range(4)], *refs[4 * n + 4 * k:4 * n + 4 * k + 4])

    vm = BS(memory_space=pltpu.VMEM)
    outs = pl.pallas_call(
        body, out_shape=tuple(SDS(a.shape, F32) for a in ws for _ in range(4)), in_specs=[vm] * (4 * n),
        out_specs=(vm,) * (4 * n), name="adamw_small", compiler_params=_params())(*ws, *gs, *ms, *vs)
    return [outs[4 * k:4 * k + 4] for k in range(n)]


def _adamw(name, w, grad, row0, m, v, after=()):
    rows, cols = w.shape
    tr = rows if rows < 16 else _row_tile(rows, 352)
    after_ops, after_specs = _after_operands(after)

    def body(w_ref, g_ref, m_ref, v_ref, *rest):
        _adamw_refs(w_ref, g_ref, m_ref, v_ref, *rest[len(after_ops):])

    blk = BS((tr, cols), lambda i: (i, 0))
    shape = SDS((rows, cols), F32)
    return pl.pallas_call(
        body, out_shape=(shape,) * 4, grid=(rows // tr,),
        in_specs=[blk, BS((tr, cols), lambda i: (row0 // tr + i, 0)), blk, blk] + after_specs, out_specs=(blk,) * 4,
        name=name, compiler_params=_params("parallel"))(w, grad, m, v, *after_ops)


SMALL_LANES = 128


def _pack_small(parts):
    flat = jnp.concatenate([jnp.ravel(p) for p in parts])
    rows = -(-flat.shape[0] // (64 * SMALL_LANES)) * 64
    return jnp.pad(flat, (0, rows * SMALL_LANES - flat.shape[0])).reshape(rows, SMALL_LANES)


def _unpack_small(packed, like):
    flat = jnp.ravel(packed)
    out, at = [], 0
    for p in like:
        out.append(flat[at:at + p.size].reshape(p.shape))
        at += p.size
    return out


def kernel(x, mem, ffn1_norm, ffn1_w_gate, ffn1_w_up, ffn1_w_down, mix_norm, w_in, pool_w, pool_scale, w_pool_proj, ssm_a_re, ssm_a_im, ssm_log_dt, ssm_b_re, ssm_b_im, ssm_c_re, ssm_c_im, ssm_d, w_glu_val, w_glu_gate, w_mix_out, xattn_norm, mem_norm, w_q, w_kv, w_xo, ffn2_norm, ffn2_w_gate, ffn2_w_up, ffn2_w_down, final_norm, loss_target, m_ffn1_norm, m_ffn1_w_gate, m_ffn1_w_up, m_ffn1_w_down, m_mix_norm, m_w_in, m_pool_w, m_pool_scale, m_w_pool_proj, m_ssm_a_re, m_ssm_a_im, m_ssm_log_dt, m_ssm_b_re, m_ssm_b_im, m_ssm_c_re, m_ssm_c_im, m_ssm_d, m_w_glu_val, m_w_glu_gate, m_w_mix_out, m_xattn_norm, m_mem_norm, m_w_q, m_w_kv, m_w_xo, m_ffn2_norm, m_ffn2_w_gate, m_ffn2_w_up, m_ffn2_w_down, m_final_norm, v_ffn1_norm, v_ffn1_w_gate, v_ffn1_w_up, v_ffn1_w_down, v_mix_norm, v_w_in, v_pool_w, v_pool_scale, v_w_pool_proj, v_ssm_a_re, v_ssm_a_im, v_ssm_log_dt, v_ssm_b_re, v_ssm_b_im, v_ssm_c_re, v_ssm_c_im, v_ssm_d, v_w_glu_val, v_w_glu_gate, v_w_mix_out, v_xattn_norm, v_mem_norm, v_w_q, v_w_kv, v_w_xo, v_ffn2_norm, v_ffn2_w_gate, v_ffn2_w_up, v_ffn2_w_down, v_final_norm):
    given = dict(locals())
    w = {n: given[n] for n in WEIGHTS}
    m = {n: given["m_" + n] for n in WEIGHTS}
    v = {n: given["v_" + n] for n in WEIGHTS}

    def shard_view(a, n):
        return a[0].T if n in TRANSPOSED else a[0]

    def shard_unview(a, n):
        return (a.T if n in TRANSPOSED else a)[None]

    shards = {tag: [jnp.concatenate([shard_view(w[n], n).astype(BF16) for n in grp], axis=0) for grp in arrays]
              for tag, arrays in GATHER_PHASES.items()}
    reducer = _GradReducer()
    ws, ms, vs = ({n: _small_view(a[n], n) for n in SMALL} for a in (w, m, v))
    loss_part, grad_x, _, small = _device_step(x[0], mem[0], loss_target[0], _WeightGatherer(shards), ws, reducer)

    small_like = [ws[n] for n in SMALL] + [loss_part[0, :1]]
    pack = _pack_small([small[n] for n in SMALL] + [loss_part[0, :1]])
    everyone = _split_start("small_start", _everyone_copies, [pack], [((N_DEV,) + pack.shape, F32)], fanout=N_DEV - 1)
    reducer.join("b", after=everyone[-1:])

    grads, delta, new_m, new_v = {}, {}, {}, {}
    big_done = []
    for grp, red in zip(REDUCE_GROUPS, reducer.reduced):
        row0 = 0
        for n in grp:
            w_n = shard_view(w[n], n)
            outs = _adamw("adamw_" + n, w_n, red, row0, shard_view(m[n], n), shard_view(v[n], n), after=everyone[-1:])
            grads[n], delta[n], new_m[n], new_v[n] = (shard_unview(o, n) for o in outs)
            big_done.append(outs[1])
            row0 += w_n.shape[0]

    send_sems, recv_sems, packs, landed, _ = everyone
    packs, landed = _split_wait("small_wait", _everyone_copies, send_sems, recv_sems, packs, landed, big_done)
    mine = 4 * lax.axis_index("x") + 2 * lax.axis_index("y") + lax.axis_index("c")
    summed = _sum_devices(lax.dynamic_update_slice(landed[0], packs[0][None], (mine, 0, 0)))
    g_small = dict(zip(SMALL + ("loss",), _unpack_small(summed, small_like)))
    loss = g_small.pop("loss").reshape(())
    def two_d(a):
        return a.reshape(-1, a.shape[-1])

    updated = _adamw_small(*([two_d(a[n]) for n in SMALL] for a in (ws, g_small, ms, vs)))
    for n, outs in zip(SMALL, updated):
        grads[n], delta[n], new_m[n], new_v[n] = (_small_view(o.reshape(ws[n].shape), n) for o in outs)

    return (loss, grad_x[None], *[grads[n] for n in WEIGHTS], *[delta[n] for n in WEIGHTS],
            *[new_m[n] for n in WEIGHTS], *[new_v[n] for n in WEIGHTS])
```

```python
import functools
import math

import jax
import jax.numpy as jnp
from jax import lax
from jax.experimental import pallas as pl
from jax.experimental.pallas import tpu as pltpu

F32 = jnp.float32
BF16 = jnp.bfloat16
SDS = jax.ShapeDtypeStruct
BS = pl.BlockSpec
MESH = pl.DeviceIdType.MESH

D_MODEL = 1024
D_FF = 2816
N_SHARD = 4
FF_SH = D_FF // N_SHARD
D_POOL = 512
POOL_WINDOWS = (2, 4, 8, 16)
POOL_GROUP = 128
D_SSM = 256
SSM_GROUPS = 16
SSM_GROUP = 16
SSM_STATE = 64
SSM_CH = SSM_GROUPS * SSM_STATE
N_HEADS = 4
HEAD_DIM = 256
EPS = 1e-6
ADAM_LR, ADAM_B1, ADAM_B2, ADAM_EPS, ADAM_WD, ADAM_STEP = 0.001, 0.9, 0.999, 1e-08, 0.01, 10

VMEM_LIMIT_V7X = 58 * 1024 * 1024
TM = 512

NN = (((1,), (0,)), ((), ()))
NT = (((1,), (1,)), ((), ()))
TN = (((0,), (0,)), ((), ()))


def _params(*sem):
    return pltpu.CompilerParams(dimension_semantics=sem if sem else None, vmem_limit_bytes=VMEM_LIMIT_V7X)


def _dot(a, b, dims=NN):
    return lax.dot_general(a.astype(BF16), b.astype(BF16), dims, preferred_element_type=F32)


def _sigmoid(v):
    return pl.reciprocal(1.0 + jnp.exp(-v), approx=True)


def _block_dims(spec):
    return tuple(d for d in spec.block_shape if d is not None)


def _after_operands(after):
    return list(after), [BS(memory_space=pl.ANY)] * len(after)


def _mm(name, pairs, *, grid, out_shape, out_spec, red_axis=None, extras=(), epilogue=None, after=()):
    n_pairs, n_extra = len(pairs), len(extras)
    n_red = grid[red_axis] if red_axis is not None else 1
    dims = [p[4] for p in pairs]

    def body(*refs):
        ab = refs[:2 * n_pairs]
        ex = refs[2 * n_pairs:2 * n_pairs + n_extra]
        o_ref = refs[2 * n_pairs + n_extra + len(after)]

        def partial():
            acc = None
            for p in range(n_pairs):
                t = _dot(ab[2 * p][...], ab[2 * p + 1][...], dims[p])
                acc = t if acc is None else acc + t
            return acc

        def finish(acc):
            res = epilogue(acc, *[e[...] for e in ex]) if epilogue is not None else acc
            o_ref[...] = res.astype(o_ref.dtype)

        if n_red == 1:
            finish(partial())
        else:
            acc_ref = refs[-1]
            k = pl.program_id(red_axis)

            @pl.when(k == 0)
            def _():
                acc_ref[...] = jnp.zeros_like(acc_ref)

            acc_ref[...] += partial()

            @pl.when(k == n_red - 1)
            def _():
                finish(acc_ref[...])

    operands, in_specs = [], []
    for a, a_spec, b, b_spec, _ in pairs:
        operands += [a, b]
        in_specs += [a_spec, b_spec]
    for e, e_spec in extras:
        operands.append(e)
        in_specs.append(e_spec)
    after_ops, after_specs = _after_operands(after)
    operands += after_ops
    in_specs += after_specs
    scratch = [pltpu.VMEM(_block_dims(out_spec), F32)] if n_red > 1 else []
    sem = tuple("arbitrary" if ax == red_axis else "parallel" for ax in range(len(grid)))
    return pl.pallas_call(body, out_shape=out_shape, grid=grid, in_specs=in_specs, out_specs=out_spec,
                          scratch_shapes=scratch, name=name, compiler_params=_params(*sem))(*operands)


def _rmsnorm(name, h, gain, tm, after=()):
    t, d = h.shape
    after_ops, after_specs = _after_operands(after)

    def body(h_ref, g_ref, *rest):
        u_ref = rest[-1]
        hv = h_ref[...]
        r = lax.rsqrt(jnp.mean(hv * hv, axis=-1, keepdims=True) + EPS)
        u_ref[...] = ((hv * r) * g_ref[...]).astype(u_ref.dtype)

    return pl.pallas_call(
        body, out_shape=SDS((t, d), BF16), grid=(t // tm,),
        in_specs=[BS((tm, d), lambda i: (i, 0)), BS((1, d), lambda i: (0, 0))] + after_specs,
        out_specs=BS((tm, d), lambda i: (i, 0)), name=name, compiler_params=_params("parallel"))(h, gain, *after_ops)


def _rmsnorm_bwd(name, h, gain, du, dh_in, tm):
    t, d = h.shape
    has_in = dh_in is not None

    def body(*refs):
        if has_in:
            h_ref, g_ref, du_ref, dhin_ref, dh_ref, dhb_ref, dg_ref = refs
        else:
            h_ref, g_ref, du_ref, dh_ref, dhb_ref, dg_ref = refs
        i = pl.program_id(0)
        hv = h_ref[...]
        r = lax.rsqrt(jnp.mean(hv * hv, axis=-1, keepdims=True) + EPS)
        n = hv * r
        duv = du_ref[...].astype(F32)
        dn = duv * g_ref[...]
        dh = r * (dn - n * jnp.mean(dn * n, axis=-1, keepdims=True))
        if has_in:
            dh = dhin_ref[...] + dh
        dh_ref[...] = dh
        dhb_ref[...] = dh.astype(BF16)

        @pl.when(i == 0)
        def _():
            dg_ref[...] = jnp.zeros_like(dg_ref)

        dg_ref[...] += jnp.sum(duv * n, axis=0, keepdims=True)

    row = BS((tm, d), lambda i: (i, 0))
    vec = BS((1, d), lambda i: (0, 0))
    operands = [h, gain, du] + ([dh_in] if has_in else [])
    in_specs = [row, vec, row] + ([row] if has_in else [])
    return pl.pallas_call(
        body, out_shape=(SDS((t, d), F32), SDS((t, d), BF16), SDS((1, d), F32)), grid=(t // tm,),
        in_specs=in_specs, out_specs=(row, row, vec), name=name, compiler_params=_params("arbitrary"))(*operands)


def _loss_head_tile(i, hv, g_ref, t_ref, loss_ref, dh_ref, dhb_ref, dg_ref):
    g = g_ref[...]
    r = lax.rsqrt(jnp.mean(hv * hv, axis=-1, keepdims=True) + EPS)
    n = hv * r
    err = n * g - t_ref[...]
    dy = err * (1.0 / hv.shape[-1])
    dn = dy * g
    dh = r * (dn - n * jnp.mean(dn * n, axis=-1, keepdims=True))
    dh_ref[...] = dh
    dhb_ref[...] = dh.astype(BF16)

    @pl.when(i == 0)
    def _():
        dg_ref[...] = jnp.zeros_like(dg_ref)
        loss_ref[...] = jnp.zeros_like(loss_ref)

    dg_ref[...] += jnp.sum(dy * n, axis=0, keepdims=True)
    part = 0.5 * jnp.sum(jnp.mean(err * err, axis=-1, keepdims=True), axis=0, keepdims=True)
    loss_ref[...] += jnp.broadcast_to(part, loss_ref.shape)


def _norm_tile(h, g_ref, u_ref):
    r = lax.rsqrt(jnp.mean(h * h, axis=-1, keepdims=True) + EPS)
    u_ref[...] = ((h * r) * g_ref[...]).astype(u_ref.dtype)


FFN_GATE, FFN_UP, FFN_DOWN = 0, 1, 2


def _ffn_up(name, u, w_f, tm, after=()):
    t, d = u.shape
    after_ops, after_specs = _after_operands(after)

    def body(u_ref, wg_ref, wu_ref, *rest):
        pg_ref, pu_ref, a_ref = rest[len(after_ops):]
        uv = u_ref[...]
        for s in range(N_SHARD):
            g = _dot(uv, wg_ref[s], NT)
            up = _dot(uv, wu_ref[s], NT)
            sg = _sigmoid(g)
            silu = g * sg
            a_ref[s] = (silu * up).astype(BF16)
            pu_ref[s] = (0.5 * silu).astype(BF16)
            pg_ref[s] = (0.5 * sg * (1.0 + g * (1.0 - sg)) * up).astype(BF16)

    hid = BS((N_SHARD, tm, FF_SH), lambda i: (0, i, 0))
    shape = SDS((N_SHARD, t, FF_SH), BF16)
    return pl.pallas_call(
        body, out_shape=(shape, shape, shape), grid=(t // tm,),
        in_specs=[BS((tm, d), lambda i: (i, 0)), _ffn_all_shards_spec(w_f["gate"][1]),
                  _ffn_all_shards_spec(w_f["up"][1])] + after_specs,
        out_specs=(hid, hid, hid), name=name,
        compiler_params=_params("parallel"))(u, w_f["gate"][0], w_f["up"][0], *after_ops)


def _ffn_all_shards_spec(block):
    return BS((N_SHARD, FF_SH, D_MODEL), lambda i: (0, block, 0))


def _ffn_down(name, a, w_f, resid, tm, next_gain=None, head=None):
    t, d = resid.shape
    row = BS((tm, d), lambda i: (i, 0))
    vec = BS((1, d), lambda i: (0, 0))

    def body(a_ref, w_ref, res_ref, *rest):
        acc = _dot(a_ref[0], w_ref[0])
        for s in range(1, N_SHARD):
            acc = acc + _dot(a_ref[s], w_ref[s])
        h = res_ref[...] + 0.5 * acc
        if head is not None:
            _loss_head_tile(pl.program_id(0), h, *rest)
        else:
            g_ref, h_ref, u_ref = rest
            h_ref[...] = h
            _norm_tile(h, g_ref, u_ref)

    if head is not None:
        extra, extra_specs = list(head), [vec, row]
        out_shape = (SDS((1, 128), F32), SDS((t, d), F32), SDS((t, d), BF16), SDS((1, d), F32))
        out_specs = (BS((1, 128), lambda i: (0, 0)), row, row, vec)
    else:
        extra, extra_specs = [next_gain], [vec]
        out_shape = (SDS((t, d), F32), SDS((t, d), BF16))
        out_specs = (row, row)
    return pl.pallas_call(
        body, out_shape=out_shape, grid=(t // tm,),
        in_specs=[BS((N_SHARD, tm, FF_SH), lambda i: (0, i, 0)), _ffn_all_shards_spec(w_f["down"][1]), row] + extra_specs,
        out_specs=out_specs, name=name,
        compiler_params=_params("arbitrary" if head is not None else "parallel"))(a, w_f["down"][0], resid, *extra)


def _mm_resid_norm(name, a, b, resid, next_gain, tm):
    t, d = resid.shape

    def body(a_ref, b_ref, res_ref, g_ref, h_ref, u_ref):
        h = res_ref[...] + _dot(a_ref[...], b_ref[...])
        h_ref[...] = h
        _norm_tile(h, g_ref, u_ref)

    row = BS((tm, d), lambda i: (i, 0))
    return pl.pallas_call(
        body, out_shape=(SDS((t, d), F32), SDS((t, d), BF16)), grid=(t // tm,),
        in_specs=[BS((tm, a.shape[1]), lambda i: (i, 0)), BS(b.shape, lambda i: (0, 0)), row, BS((1, d), lambda i: (0, 0))],
        out_specs=(row, row), name=name, compiler_params=_params("parallel"))(a, b, resid, next_gain)


def _ffn_bwd_act(name, dh_b, w_f, pg, pu, tm, after=()):
    t, d = dh_b.shape
    after_ops, after_specs = _after_operands(after)

    def body(dh_ref, wd_ref, pg_ref, pu_ref, *rest):
        dg_ref, dup_ref = rest[len(after_ops):]
        dh = dh_ref[...]
        for s in range(N_SHARD):
            da = _dot(dh, wd_ref[s], NT)
            dg_ref[s] = (da * pg_ref[s].astype(F32)).astype(BF16)
            dup_ref[s] = (da * pu_ref[s].astype(F32)).astype(BF16)

    hid = BS((N_SHARD, tm, FF_SH), lambda i: (0, i, 0))
    shape = SDS((N_SHARD, t, FF_SH), BF16)
    return pl.pallas_call(
        body, out_shape=(shape, shape), grid=(t // tm,),
        in_specs=[BS((tm, d), lambda i: (i, 0)), _ffn_all_shards_spec(w_f["down"][1]), hid, hid] + after_specs,
        out_specs=(hid, hid), name=name,
        compiler_params=_params("parallel"))(dh_b, w_f["down"][0], pg, pu, *after_ops)


def _ffn_dw(name, u, dg, dup, a, dh_b, tm):
    t, d = u.shape
    tm = min(4 * tm, t)
    n_t = t // tm

    def body(u_ref, dg_ref, dup_ref, a_ref, dh_ref, o_ref, acc):
        i = pl.program_id(1)

        @pl.when(i == 0)
        def _():
            acc[...] = jnp.zeros_like(acc)

        uv = u_ref[...]
        acc[FFN_GATE] += _dot(dg_ref[...], uv, TN)
        acc[FFN_UP] += _dot(dup_ref[...], uv, TN)
        acc[FFN_DOWN] += _dot(a_ref[...], dh_ref[...], TN)

        @pl.when(i == n_t - 1)
        def _():
            o_ref[FFN_GATE] = acc[FFN_GATE].astype(BF16)
            o_ref[FFN_UP] = acc[FFN_UP].astype(BF16)
            o_ref[FFN_DOWN] = (0.5 * acc[FFN_DOWN]).astype(BF16)

    hid = BS((None, tm, FF_SH), lambda s, i: (s, i, 0))
    row = BS((tm, d), lambda s, i: (i, 0))
    return pl.pallas_call(
        body, out_shape=SDS((N_SHARD, 3, FF_SH, d), BF16), grid=(N_SHARD, n_t),
        in_specs=[row, hid, hid, hid, row], out_specs=BS((None, 3, FF_SH, d), lambda s, i: (s, 0, 0, 0)),
        scratch_shapes=[pltpu.VMEM((3, FF_SH, d), F32)],
        name=name, compiler_params=_params("parallel", "arbitrary"))(u, dg, dup, a, dh_b)


def _norm_bwd_tile(i, du, h_ref, g_ref, dhin_ref, dh_ref, dhb_ref, dg_ref):
    hv = h_ref[...]
    r = lax.rsqrt(jnp.mean(hv * hv, axis=-1, keepdims=True) + EPS)
    n = hv * r
    dn = du * g_ref[...]
    dh = dhin_ref[...] + r * (dn - n * jnp.mean(dn * n, axis=-1, keepdims=True))
    dh_ref[...] = dh
    dhb_ref[...] = dh.astype(BF16)

    @pl.when(i == 0)
    def _():
        dg_ref[...] = jnp.zeros_like(dg_ref)

    dg_ref[...] += jnp.sum(du * n, axis=0, keepdims=True)


def _norm_bwd_specs(tm):
    row = BS((tm, D_MODEL), lambda i: (i, 0))
    vec = BS((1, D_MODEL), lambda i: (0, 0))
    return [row, vec, row], (row, row, vec)


def _norm_bwd_shapes(t):
    return SDS((t, D_MODEL), F32), SDS((t, D_MODEL), BF16), SDS((1, D_MODEL), F32)


def _ffn_dx(name, dg, dup, w_f, h, gain, dh_in, tm, after=()):
    t = dg.shape[1]
    tm = tm // 2
    after_ops, after_specs = _after_operands(after)

    def body(dg_ref, dup_ref, wg_ref, wu_ref, h_ref, g_ref, dhin_ref, *rest):
        acc = _dot(dg_ref[0], wg_ref[0]) + _dot(dup_ref[0], wu_ref[0])
        for s in range(1, N_SHARD):
            acc = acc + _dot(dg_ref[s], wg_ref[s]) + _dot(dup_ref[s], wu_ref[s])
        _norm_bwd_tile(pl.program_id(0), acc, h_ref, g_ref, dhin_ref, *rest[len(after_ops):])

    hid = BS((N_SHARD, tm, FF_SH), lambda i: (0, i, 0))
    norm_in, norm_out = _norm_bwd_specs(tm)
    return pl.pallas_call(
        body, out_shape=_norm_bwd_shapes(t), grid=(t // tm,),
        in_specs=[hid, hid, _ffn_all_shards_spec(w_f["gate"][1]), _ffn_all_shards_spec(w_f["up"][1])] + norm_in + after_specs,
        out_specs=norm_out, name=name,
        compiler_params=_params("arbitrary"))(dg, dup, w_f["gate"][0], w_f["up"][0], h, gain, dh_in, *after_ops)


def _mm_norm_bwd(name, a, b, dims, h, gain, dh_in, tm):
    t = a.shape[0]

    def body(a_ref, b_ref, h_ref, g_ref, dhin_ref, *outs):
        _norm_bwd_tile(pl.program_id(0), _dot(a_ref[...], b_ref[...], dims), h_ref, g_ref, dhin_ref, *outs)

    norm_in, norm_out = _norm_bwd_specs(tm)
    return pl.pallas_call(
        body, out_shape=_norm_bwd_shapes(t), grid=(t // tm,),
        in_specs=[BS((tm, a.shape[1]), lambda i: (i, 0)), BS(b.shape, lambda i: (0, 0))] + norm_in,
        out_specs=norm_out, name=name, compiler_params=_params("arbitrary"))(a, b, h, gain, dh_in)


def _plain_mm(name, a, b, dims, out_dtype, tm, resid=None, after=()):
    t = a.shape[0]
    tm = min(2 * tm, t)
    n = b.shape[1] if dims == NN else b.shape[0]
    extras = [(resid, BS((tm, n), lambda i: (i, 0)))] if resid is not None else []
    epi = (lambda acc, res: res + acc) if resid is not None else None
    return _mm(name, [(a, BS((tm, a.shape[1]), lambda i: (i, 0)), b, BS(b.shape, lambda i: (0, 0)), dims)],
               grid=(t // tm,), out_shape=SDS((t, n), out_dtype), out_spec=BS((tm, n), lambda i: (i, 0)),
               extras=extras, epilogue=epi, after=after)


def _dw_mm(name, a, b, tm, out_dtype=BF16, after=()):
    t, k = a.shape
    n = b.shape[1]
    tm = min(2 * tm, t)
    return _mm(name, [(a, BS((tm, k), lambda i: (i, 0)), b, BS((tm, n), lambda i: (i, 0)), TN)],
               grid=(t // tm,), red_axis=0, out_shape=SDS((k, n), out_dtype), out_spec=BS((k, n), lambda i: (0, 0)),
               after=after)


POOL_CHUNK = 256
POOL_HALO = 8


def _window_sum(v, width, lead):
    n = v.shape[0]
    s = v
    k = 1
    while k < width:
        s = s + pltpu.roll(s, n - k, 0)
        k *= 2
    return pltpu.roll(s, lead, 0) if lead else s


def _pool_count(base, left, right, t, shape):
    pos = base + lax.broadcasted_iota(jnp.int32, shape, 0)
    lo = jnp.maximum(pos - left, 0)
    hi = jnp.minimum(pos + right + 1, t)
    return (hi - lo).astype(F32)


def _pool_fwd(proj, pool_w, pool_scale):
    t = proj.shape[0]
    c, h = POOL_CHUNK, POOL_HALO
    n_chunks = t // c

    def body(proj_hbm, pw_ref, sc_ref, pooled_ref, mixed_ref, ms_ref, pad_ref, sem):
        cp = pltpu.make_async_copy(proj_hbm.at[:, pl.ds(0, D_POOL)], pad_ref.at[pl.ds(h, t), :], sem)
        cp.start()
        pad_ref[pl.ds(0, h), :] = jnp.zeros((h, D_POOL), F32)
        pad_ref[pl.ds(t + h, h), :] = jnp.zeros((h, D_POOL), F32)
        cp.wait()
        for g, width in enumerate(POOL_WINDOWS):
            left = width // 2
            right = width - 1 - left
            cols = slice(g * POOL_GROUP, (g + 1) * POOL_GROUP)
            wmat = pw_ref[g].astype(BF16)
            scale = sc_ref[:, cols]

            def chunk(ci, carry, left=left, right=right, width=width, cols=cols, wmat=wmat, scale=scale):
                base = pl.multiple_of(ci * c, c)
                v = pad_ref[pl.ds(base, c + 2 * h), cols]
                win = _window_sum(v, width, left)[h:h + c]
                cnt = _pool_count(base, left, right, t, (c, POOL_GROUP))
                pooled = (win / cnt - v[h:h + c]).astype(BF16)
                mixed = _dot(pooled, wmat)
                pooled_ref[pl.ds(base, c), cols] = pooled
                mixed_ref[pl.ds(base, c), cols] = mixed.astype(BF16)
                ms_ref[pl.ds(base, c), cols] = (mixed * scale).astype(BF16)
                return carry

            lax.fori_loop(0, n_chunks, chunk, 0)

    vm = BS(memory_space=pltpu.VMEM)
    shape = SDS((t, D_POOL), BF16)
    return pl.pallas_call(
        body, out_shape=(shape, shape, shape),
        in_specs=[BS(memory_space=pl.ANY), vm, vm], out_specs=(vm, vm, vm),
        scratch_shapes=[pltpu.VMEM((t + 2 * h, D_POOL), F32), pltpu.SemaphoreType.DMA],
        name="pool_fwd", compiler_params=_params())(proj, pool_w, pool_scale)


def _pool_bwd(d_ms, mixed, pooled, pool_w, pool_scale):
    t = d_ms.shape[0]
    c, h = POOL_CHUNK, POOL_HALO
    n_chunks = t // c

    def body(dms_ref, mixed_ref, pooled_ref, pw_ref, sc_ref, dp_ref, dsc_ref, dpw_ref, pad_ref):
        pad_ref[pl.ds(0, h), :] = jnp.zeros((h, D_POOL), F32)
        pad_ref[pl.ds(t + h, h), :] = jnp.zeros((h, D_POOL), F32)
        for g, width in enumerate(POOL_WINDOWS):
            left = width // 2
            right = width - 1 - left
            cols = slice(g * POOL_GROUP, (g + 1) * POOL_GROUP)
            wmat = pw_ref[g].astype(BF16)
            scale = sc_ref[:, cols]

            def first(ci, carry, left=left, right=right, cols=cols, wmat=wmat, scale=scale):
                dsc, dpw = carry
                base = pl.multiple_of(ci * c, c)
                dms = dms_ref[pl.ds(base, c), cols].astype(F32)
                dsc = dsc + jnp.sum(dms * mixed_ref[pl.ds(base, c), cols].astype(F32), axis=0, keepdims=True)
                dmix = (dms * scale).astype(BF16)
                dpw = dpw + _dot(pooled_ref[pl.ds(base, c), cols], dmix, TN)
                dpooled = _dot(dmix, wmat, NT)
                cnt = _pool_count(base, left, right, t, (c, POOL_GROUP))
                pad_ref[pl.ds(base + h, c), cols] = dpooled / cnt
                return dsc, dpw

            dsc, dpw = lax.fori_loop(0, n_chunks, first,
                                     (jnp.zeros((1, POOL_GROUP), F32), jnp.zeros((POOL_GROUP, POOL_GROUP), F32)))
            dsc_ref[:, cols] = dsc
            dpw_ref[g] = dpw

            def second(ci, carry, left=left, right=right, width=width, cols=cols):
                base = pl.multiple_of(ci * c, c)
                v = pad_ref[pl.ds(base, c + 2 * h), cols]
                win = _window_sum(v, width, right)[h:h + c]
                cnt = _pool_count(base, left, right, t, (c, POOL_GROUP))
                dp_ref[pl.ds(base, c), cols] = (win - v[h:h + c] * cnt).astype(BF16)
                return carry

            lax.fori_loop(0, n_chunks, second, 0)

    vm = BS(memory_space=pltpu.VMEM)
    return pl.pallas_call(
        body, out_shape=(SDS((t, D_POOL), BF16), SDS((1, D_POOL), F32), SDS((4, POOL_GROUP, POOL_GROUP), F32)),
        in_specs=[vm] * 5, out_specs=(vm, vm, vm),
        scratch_shapes=[pltpu.VMEM((t + 2 * h, D_POOL), F32)],
        name="pool_bwd", compiler_params=_params())(d_ms, mixed, pooled, pool_w, pool_scale)


SSM_ROWS = 2 * SSM_GROUPS * SSM_GROUP
SSM_HALF = SSM_GROUPS * SSM_GROUP


def _ssm_zoh(a_r, a_i, ldt):
    dt = jnp.exp(ldt)
    mag = jnp.exp(dt * a_r)
    ang = dt * a_i
    cs, sn = jnp.cos(ang), jnp.sin(ang)
    abr, abi = mag * cs, mag * sn
    den = a_r * a_r + a_i * a_i
    nr = abr - 1.0
    qr = (nr * a_r + abi * a_i) / den
    qi = (abi * a_r - nr * a_i) / den
    return dt, mag, cs, sn, abr, abi, den, nr, qr, qi


def _ssm_group_mask():
    row = lax.broadcasted_iota(jnp.int32, (SSM_HALF, SSM_CH), 0)
    col = lax.broadcasted_iota(jnp.int32, (SSM_HALF, SSM_CH), 1)
    return (row // SSM_GROUP) == (col // SSM_STATE)


def _ssm_prep(a_r, a_i, ldt, b_r, b_i, c_r, c_i, after=()):
    after_ops, after_specs = _after_operands(after)

    def body(ar_ref, ai_ref, ldt_ref, br_ref, bi_ref, cr_ref, ci_ref, *rest):
        abr_ref, abi_ref, win_ref, wint_ref, woutt_ref, wout_ref = rest[len(after_ops):]
        *_, abr, abi, _, _, qr, qi = _ssm_zoh(ar_ref[...], ai_ref[...], ldt_ref[...])
        abr_ref[...] = abr
        abi_ref[...] = abi
        b_r, b_i = br_ref[...], bi_ref[...]
        bbr = qr * b_r - qi * b_i
        bbi = qr * b_i + qi * b_r
        mask = _ssm_group_mask()
        state = lax.broadcasted_iota(jnp.int32, (SSM_STATE, SSM_CH), 0)
        col = lax.broadcasted_iota(jnp.int32, (SSM_STATE, SSM_CH), 1)
        every_group = (col % SSM_STATE == state).astype(BF16)

        def spread(x):
            return jnp.where(mask, _dot(x, every_group), 0.0)

        for d in range(2):
            rows = slice(d * SSM_HALF, (d + 1) * SSM_HALF)
            for half, x_in, x_out in ((0, bbr[rows], cr_ref[rows, :]), (1, bbi[rows], -ci_ref[rows, :])):
                cols = slice(half * SSM_CH, (half + 1) * SSM_CH)
                m_in, m_out = spread(x_in), spread(x_out)
                win_ref[d, :, cols] = m_in.astype(BF16)
                wint_ref[d, cols, :] = m_in.T.astype(BF16)
                woutt_ref[d, :, cols] = m_out.astype(BF16)
                wout_ref[d, cols, :] = m_out.T.astype(BF16)

    vm = BS(memory_space=pltpu.VMEM)
    vec = SDS((SSM_ROWS, SSM_STATE), F32)
    wide = SDS((2, SSM_HALF, 2 * SSM_CH), BF16)
    tall = SDS((2, 2 * SSM_CH, SSM_HALF), BF16)
    return pl.pallas_call(body, out_shape=(vec, vec, wide, tall, wide, tall), in_specs=[vm] * 7 + after_specs,
                          out_specs=(vm,) * 6, name="ssm_prep",
                          compiler_params=_params())(a_r, a_i, ldt, b_r, b_i, c_r, c_i, *after_ops)


def _ssm_prep_bwd(a_r, a_i, ldt, b_r, b_i, g_abr, g_abi, d_win, d_woutt):
    def body(ar_ref, ai_ref, ldt_ref, br_ref, bi_ref, gabr_ref, gabi_ref, dwin_ref, dwoutt_ref,
             dar_ref, dai_ref, dldt_ref, dbr_ref, dbi_ref, dcr_ref, dci_ref):
        a_r, a_i = ar_ref[...], ai_ref[...]
        dt, mag, cs, sn, abr, abi, den, nr, qr, qi = _ssm_zoh(a_r, a_i, ldt_ref[...])
        mask = _ssm_group_mask()
        col = lax.broadcasted_iota(jnp.int32, (SSM_CH, SSM_STATE), 0)
        state = lax.broadcasted_iota(jnp.int32, (SSM_CH, SSM_STATE), 1)
        own_state = (col % SSM_STATE == state).astype(BF16)

        def pick(dense):
            m = jnp.where(mask, dense, 0.0)
            hi = m.astype(BF16)
            lo = m - hi.astype(F32)
            return _dot(hi, own_state) + _dot(lo, own_state)

        def picked(ref, half):
            cols = slice(half * SSM_CH, (half + 1) * SSM_CH)
            return jnp.concatenate([pick(ref[d, :, cols]) for d in range(2)], axis=0)

        g_r, g_i = picked(dwin_ref, 0), picked(dwin_ref, 1)
        dcr_ref[...] = picked(dwoutt_ref, 0)
        dci_ref[...] = -picked(dwoutt_ref, 1)
        b_r, b_i = br_ref[...], bi_ref[...]
        dbr_ref[...] = g_r * qr + g_i * qi
        dbi_ref[...] = g_i * qr - g_r * qi
        gqr = g_r * b_r + g_i * b_i
        gqi = g_i * b_r - g_r * b_i
        g_nr_num = gqr / den
        g_ni_num = gqi / den
        g_den = -(gqr * qr + gqi * qi) / den
        g_nr = g_nr_num * a_r - g_ni_num * a_i
        g_abi = g_nr_num * a_i + g_ni_num * a_r
        d_ar = g_nr_num * nr + g_ni_num * abi + 2.0 * a_r * g_den
        d_ai = g_nr_num * abi - g_ni_num * nr + 2.0 * a_i * g_den
        g_abr = gabr_ref[...] + g_nr
        g_abi = gabi_ref[...] + g_abi
        g_mag = g_abr * cs + g_abi * sn
        g_ang = mag * (g_abi * cs - g_abr * sn)
        g_e = g_mag * mag
        d_ar = d_ar + g_e * dt
        d_ai = d_ai + g_ang * dt
        g_dt = g_e * a_r + g_ang * a_i
        dar_ref[...] = d_ar
        dai_ref[...] = d_ai
        dldt_ref[...] = g_dt * dt

    vm = BS(memory_space=pltpu.VMEM)
    vec = SDS((SSM_ROWS, SSM_STATE), F32)
    return pl.pallas_call(body, out_shape=(vec,) * 7, in_specs=[vm] * 9, out_specs=(vm,) * 7, name="ssm_prep_bwd",
                          compiler_params=_params())(a_r, a_i, ldt, b_r, b_i, g_abr, g_abi, d_win, d_woutt)


SCAN_ROWS = 512
SCAN_SUB = 128


def _ssm_scan(name, inp, w1, a_r, a_i, w2, reverse):
    t = inp.shape[0]
    rows = min(SCAN_ROWS, t)
    n = t // rows
    n_sub = rows // SCAN_SUB
    ch = SSM_CH
    at = (lambda i: (n - 1 - i, 0)) if reverse else (lambda i: (i, 0))

    def body(in_ref, w1_ref, ar_ref, ai_ref, w2_ref, sb_ref, out_ref, cr_ref, ci_ref, k_ref, st_ref):
        i = pl.program_id(0)

        @pl.when(i == 0)
        def _():
            ar8 = jnp.broadcast_to(ar_ref[...], (8, ch))
            ai8 = jnp.broadcast_to(ai_ref[...], (8, ch))
            row = lax.broadcasted_iota(jnp.int32, (8, ch), 0)
            rank = (7 - row) if reverse else row
            powers = [(ar8, ai8)]
            for _ in range(7):
                p_r, p_i = powers[-1]
                powers.append((p_r * ar8 - p_i * ai8, p_r * ai8 + p_i * ar8))
            zero = jnp.zeros((8, ch), F32)
            for slot, k in enumerate((1, 2, 4)):
                k_ref[2 * slot] = jnp.where(rank >= k, powers[k - 1][0], zero)
                k_ref[2 * slot + 1] = jnp.where(rank >= k, powers[k - 1][1], zero)
            carry_r, carry_i = zero, zero
            for j in range(8):
                carry_r = jnp.where(rank == j, powers[j][0], carry_r)
                carry_i = jnp.where(rank == j, powers[j][1], carry_i)
            k_ref[6] = carry_r
            k_ref[7] = carry_i
            cr_ref[...] = zero
            ci_ref[...] = zero

        def group(r0, carry):
            c_r, c_i = carry
            x_r = st_ref[pl.ds(r0, 8), 0:ch]
            x_i = st_ref[pl.ds(r0, 8), ch:2 * ch]
            for slot, k in enumerate((1, 2, 4)):
                shift = (8 - k) if reverse else k
                s_r = pltpu.roll(x_r, shift, 0)
                s_i = pltpu.roll(x_i, shift, 0)
                m_r, m_i = k_ref[2 * slot], k_ref[2 * slot + 1]
                x_r, x_i = x_r + m_r * s_r - m_i * s_i, x_i + m_r * s_i + m_i * s_r
            p_r, p_i = k_ref[6], k_ref[7]
            x_r, x_i = x_r + p_r * c_r - p_i * c_i, x_i + p_r * c_i + p_i * c_r
            st_ref[pl.ds(r0, 8), 0:ch] = x_r
            st_ref[pl.ds(r0, 8), ch:2 * ch] = x_i
            last = 0 if reverse else 7
            return (jnp.broadcast_to(x_r[last:last + 1, :], (8, ch)), jnp.broadcast_to(x_i[last:last + 1, :], (8, ch)))

        carry = (cr_ref[...], ci_ref[...])
        for sc in (range(n_sub - 1, -1, -1) if reverse else range(n_sub)):
            part = pl.ds(sc * SCAN_SUB, SCAN_SUB)
            st_ref[part, :] = _dot(in_ref[part, :], w1_ref[...])
            for gi in range(SCAN_SUB // 8):
                g = (SCAN_SUB // 8 - 1 - gi) if reverse else gi
                carry = group(sc * SCAN_SUB + g * 8, carry)
            states = st_ref[part, :].astype(BF16)
            sb_ref[part, :] = states
            out_ref[part, :] = _dot(states, w2_ref[...])
        cr_ref[...] = carry[0]
        ci_ref[...] = carry[1]

    return pl.pallas_call(
        body, out_shape=(SDS((t, 2 * ch), BF16), SDS((t, D_SSM), F32)), grid=(n,),
        in_specs=[BS((rows, D_SSM), at), BS((D_SSM, 2 * ch), lambda i: (0, 0)), BS((1, ch), lambda i: (0, 0)),
                  BS((1, ch), lambda i: (0, 0)), BS((2 * ch, D_SSM), lambda i: (0, 0))],
        out_specs=(BS((rows, 2 * ch), at), BS((rows, D_SSM), at)),
        scratch_shapes=[pltpu.VMEM((8, ch), F32), pltpu.VMEM((8, ch), F32), pltpu.VMEM((8, 8, ch), F32),
                        pltpu.VMEM((rows, 2 * ch), F32)],
        name=name, compiler_params=_params("arbitrary"))(inp, w1, a_r, a_i, w2)


DA_ROWS = 1024


def _ssm_param_grads(name, lam, states, u, dy, reverse, after=()):
    t = lam.shape[0]
    rows = min(DA_ROWS, t)
    n = t // rows
    halo_rows = 16
    nb = rows // halo_rows
    ch = SSM_CH
    if reverse:
        halo_at = lambda i: (jnp.minimum((i + 1) * nb, t // halo_rows - 1), 0)
    else:
        halo_at = lambda i: (jnp.maximum(i * nb - 1, 0), 0)

    after_ops, after_specs = _after_operands(after)

    def body(lam_ref, x_ref, halo_ref, u_ref, dy_ref, *rest):
        dr_ref, di_ref, dwin_ref, dwoutt_ref = rest[len(after_ops):]
        i = pl.program_id(0)

        @pl.when(i == 0)
        def _():
            dr_ref[...] = jnp.zeros_like(dr_ref)
            di_ref[...] = jnp.zeros_like(di_ref)
            dwin_ref[...] = jnp.zeros_like(dwin_ref)
            dwoutt_ref[...] = jnp.zeros_like(dwoutt_ref)

        dwin_ref[...] += _dot(u_ref[...], lam_ref[...], TN)
        dwoutt_ref[...] += _dot(dy_ref[...], x_ref[...], TN)
        row = lax.broadcasted_iota(jnp.int32, (rows, ch), 0)
        if reverse:
            edge, shift, h_row, live = rows - 1, rows - 1, 0, i < n - 1
        else:
            edge, shift, h_row, live = 0, 1, halo_rows - 1, i > 0

        def neighbour(lo):
            halo = halo_ref[:, lo:lo + ch].astype(F32)[h_row:h_row + 1]
            halo = jnp.where(live, halo, 0.0)
            x = x_ref[:, lo:lo + ch].astype(F32)
            return jnp.where(row == edge, jnp.broadcast_to(halo, (rows, ch)), pltpu.roll(x, shift, 0))

        xp_r, xp_i = neighbour(0), neighbour(ch)
        l_r, l_i = lam_ref[:, 0:ch].astype(F32), lam_ref[:, ch:2 * ch].astype(F32)
        dr_ref[...] += jnp.sum(l_r * xp_r + l_i * xp_i, axis=0, keepdims=True)
        di_ref[...] += jnp.sum(l_i * xp_r - l_r * xp_i, axis=0, keepdims=True)

    blk = BS((rows, 2 * ch), lambda i: (i, 0))
    thin = BS((rows, D_SSM), lambda i: (i, 0))
    vec = BS((1, ch), lambda i: (0, 0))
    mat = BS((D_SSM, 2 * ch), lambda i: (0, 0))
    return pl.pallas_call(
        body, out_shape=(SDS((1, ch), F32), SDS((1, ch), F32), SDS((D_SSM, 2 * ch), F32), SDS((D_SSM, 2 * ch), F32)),
        grid=(n,), in_specs=[blk, blk, BS((halo_rows, 2 * ch), halo_at), thin, thin] + after_specs,
        out_specs=(vec, vec, mat, mat),
        name=name, compiler_params=_params("arbitrary"))(lam, states, states, u, dy, *after_ops)


GELU_C = math.sqrt(2.0 / math.pi)
GELU_K = 0.044715


def _ssm_combine(proj, y_fwd, y_bwd, d_skip, tm, after=()):
    t = proj.shape[0]
    after_ops, after_specs = _after_operands(after)

    def body(s_ref, yf_ref, yb_ref, d_ref, *rest):
        yt_ref, g_ref = rest[len(after_ops):]
        y = s_ref[...] * d_ref[...] + yf_ref[...] + yb_ref[...]
        yt_ref[...] = y
        th = jnp.tanh(GELU_C * (y + GELU_K * y * y * y))
        g_ref[...] = (0.5 * y * (1.0 + th)).astype(BF16)

    blk = BS((tm, D_SSM), lambda i: (i, 0))
    return pl.pallas_call(
        body, out_shape=(SDS((t, D_SSM), F32), SDS((t, D_SSM), BF16)), grid=(t // tm,),
        in_specs=[BS((tm, D_SSM), lambda i: (i, D_POOL // D_SSM)), blk, blk, BS((1, D_SSM), lambda i: (0, 0))] + after_specs,
        out_specs=(blk, blk), name="ssm_combine",
        compiler_params=_params("parallel"))(proj, y_fwd, y_bwd, d_skip, *after_ops)


def _ssm_ds(proj, d_yt, du_fwd, du_bwd, d_skip, tm):
    t = proj.shape[0]

    def body(s_ref, dy_ref, duf_ref, dub_ref, d_ref, ds_ref, dd_ref):
        i = pl.program_id(0)
        dy = dy_ref[...]
        ds_ref[...] = (dy * d_ref[...] + duf_ref[...] + dub_ref[...]).astype(BF16)

        @pl.when(i == 0)
        def _():
            dd_ref[...] = jnp.zeros_like(dd_ref)

        dd_ref[...] += jnp.sum(dy * s_ref[...], axis=0, keepdims=True)

    blk = BS((tm, D_SSM), lambda i: (i, 0))
    vec = BS((1, D_SSM), lambda i: (0, 0))
    return pl.pallas_call(
        body, out_shape=(SDS((t, D_SSM), BF16), SDS((1, D_SSM), F32)), grid=(t // tm,),
        in_specs=[BS((tm, D_SSM), lambda i: (i, D_POOL // D_SSM)), blk, blk, blk, vec],
        out_specs=(blk, vec), name="ssm_ds", compiler_params=_params("arbitrary"))(proj, d_yt, du_fwd, du_bwd, d_skip)


G_POOL_AT = D_POOL + D_SSM
G_SSM_AT = G_POOL_AT + D_MODEL
E_VAL, E_GATE = D_POOL, D_POOL + D_SSM


def _merge_specs(tm):
    return [BS((tm, D_POOL), lambda i: (i, 0)), BS((tm, D_SSM), lambda i: (i, 0)),
            BS((N_SHARD, 1024, 256), lambda i: (0, 0, 0)), BS((tm, D_FF), lambda i: (i, 0))]


def _merge_parts(s, ms, yv, w_ref, proj_ref):
    lo = 256 * s
    zp = _dot(ms, w_ref[s, 0:E_VAL, :])
    zv = _dot(yv, w_ref[s, E_VAL:E_GATE, :])
    zg = _dot(yv, w_ref[s, E_GATE:, :])
    return zp, zv, zg, proj_ref[:, G_POOL_AT + lo:G_POOL_AT + lo + 256], proj_ref[:, G_SSM_AT + lo:G_SSM_AT + lo + 256]


def _mixer_merge(ms, yssm, w_e, proj, tm):
    t = ms.shape[0]

    def body(ms_ref, y_ref, w_ref, proj_ref, o_ref):
        msv, yv = ms_ref[...], y_ref[...]
        for s in range(N_SHARD):
            zp, zv, zg, gp, gs = _merge_parts(s, msv, yv, w_ref, proj_ref)
            o_ref[:, 256 * s:256 * (s + 1)] = (_sigmoid(gp) * zp + _sigmoid(gs) * zv * _sigmoid(zg)).astype(BF16)

    row = BS((tm, D_MODEL), lambda i: (i, 0))
    return pl.pallas_call(
        body, out_shape=SDS((t, D_MODEL), BF16), grid=(t // tm,), in_specs=_merge_specs(tm), out_specs=row,
        name="mixer_merge", compiler_params=_params("parallel"))(ms, yssm, w_e, proj)


def _mixer_merge_bwd(ms, yssm, w_e, proj, dmerged, tm):
    t = ms.shape[0]

    def body(ms_ref, y_ref, w_ref, proj_ref, dm_ref, dgp_ref, dgs_ref, dzp_ref, dzv_ref, dzg_ref):
        msv, yv = ms_ref[...], y_ref[...]
        for s in range(N_SHARD):
            cols = slice(256 * s, 256 * (s + 1))
            zp, zv, zg, gp, gs = _merge_parts(s, msv, yv, w_ref, proj_ref)
            dm = dm_ref[:, cols].astype(F32)
            sp, ss, sg = _sigmoid(gp), _sigmoid(gs), _sigmoid(zg)
            dgp_ref[:, cols] = (dm * zp * sp * (1.0 - sp)).astype(BF16)
            dgs_ref[:, cols] = (dm * zv * sg * ss * (1.0 - ss)).astype(BF16)
            dzp_ref[:, cols] = (dm * sp).astype(BF16)
            dz = dm * ss
            dzv_ref[:, cols] = (dz * sg).astype(BF16)
            dzg_ref[:, cols] = (dz * zv * sg * (1.0 - sg)).astype(BF16)

    row = BS((tm, D_MODEL), lambda i: (i, 0))
    shape = SDS((t, D_MODEL), BF16)
    return pl.pallas_call(
        body, out_shape=(shape,) * 5, grid=(t // tm,), in_specs=_merge_specs(tm) + [row],
        out_specs=(row,) * 5, name="mixer_merge_bwd",
        compiler_params=_params("parallel"))(ms, yssm, w_e, proj, dmerged)


def _mixer_dw(ms, yssm, dzp, dzv, dzg, tm):
    t = ms.shape[0]
    tm = min(2 * tm, t)
    n_t = t // tm

    def body(ms_ref, y_ref, dzp_ref, dzv_ref, dzg_ref, o_ref, acc):
        i = pl.program_id(0)

        @pl.when(i == 0)
        def _():
            acc[...] = jnp.zeros_like(acc)

        msv, yv = ms_ref[...], y_ref[...]
        for s in range(N_SHARD):
            cols = slice(256 * s, 256 * (s + 1))
            acc[s, 0:E_VAL, :] += _dot(msv, dzp_ref[:, cols], TN)
            acc[s, E_VAL:E_GATE, :] += _dot(yv, dzv_ref[:, cols], TN)
            acc[s, E_GATE:, :] += _dot(yv, dzg_ref[:, cols], TN)

        @pl.when(i == n_t - 1)
        def _():
            o_ref[...] = acc[...].astype(BF16)

    row = BS((tm, D_MODEL), lambda i: (i, 0))
    full = BS((N_SHARD, 1024, 256), lambda i: (0, 0, 0))
    return pl.pallas_call(
        body, out_shape=SDS((N_SHARD, 1024, 256), BF16), grid=(n_t,),
        in_specs=[BS((tm, D_POOL), lambda i: (i, 0)), BS((tm, D_SSM), lambda i: (i, 0)), row, row, row],
        out_specs=full, scratch_shapes=[pltpu.VMEM((N_SHARD, 1024, 256), F32)],
        name="mixer_dw", compiler_params=_params("arbitrary"))(ms, yssm, dzp, dzv, dzg)


def _mixer_dx(dzp, dzv, dzg, w_e, y_total, tm):
    t = dzp.shape[0]

    def body(dzp_ref, dzv_ref, dzg_ref, w_ref, yt_ref, dms_ref, dy_ref):
        acc_ms, acc_y = None, None
        for s in range(N_SHARD):
            cols = slice(256 * s, 256 * (s + 1))
            part_ms = _dot(dzp_ref[:, cols], w_ref[s, 0:E_VAL, :], NT)
            part_y = _dot(dzv_ref[:, cols], w_ref[s, E_VAL:E_GATE, :], NT) + _dot(dzg_ref[:, cols], w_ref[s, E_GATE:, :], NT)
            acc_ms = part_ms if s == 0 else acc_ms + part_ms
            acc_y = part_y if s == 0 else acc_y + part_y
        dms_ref[...] = acc_ms.astype(BF16)
        y = yt_ref[...]
        th = jnp.tanh(GELU_C * (y + GELU_K * y * y * y))
        dgelu = 0.5 * (1.0 + th) + 0.5 * y * (1.0 - th * th) * GELU_C * (1.0 + 3.0 * GELU_K * y * y)
        dy_ref[...] = acc_y * dgelu

    row = BS((tm, D_MODEL), lambda i: (i, 0))
    return pl.pallas_call(
        body, out_shape=(SDS((t, D_POOL), BF16), SDS((t, D_SSM), F32)), grid=(t // tm,),
        in_specs=[row, row, row, BS((N_SHARD, 1024, 256), lambda i: (0, 0, 0)), BS((tm, D_SSM), lambda i: (i, 0))],
        out_specs=(BS((tm, D_POOL), lambda i: (i, 0)), BS((tm, D_SSM), lambda i: (i, 0))),
        name="mixer_dx", compiler_params=_params("parallel"))(dzp, dzv, dzg, w_e, y_total)


def _attn_probs(q_h, k_h):
    s = _dot(q_h, k_h, NT) * (1.0 / math.sqrt(HEAD_DIM))
    e = jnp.exp(s - jnp.max(s, axis=-1, keepdims=True))
    return e / jnp.sum(e, axis=-1, keepdims=True)


def _attn_fwd(q, kv, tm):
    t = q.shape[0]
    m = kv.shape[0]

    def body(q_ref, kv_ref, o_ref):
        for hd in range(N_HEADS):
            lo = hd * HEAD_DIM
            p = _attn_probs(q_ref[:, lo:lo + HEAD_DIM], kv_ref[:, lo:lo + HEAD_DIM])
            o_ref[:, lo:lo + HEAD_DIM] = _dot(p, kv_ref[:, D_MODEL + lo:D_MODEL + lo + HEAD_DIM]).astype(BF16)

    return pl.pallas_call(
        body, out_shape=SDS((t, D_MODEL), BF16), grid=(t // tm,),
        in_specs=[BS((tm, D_MODEL), lambda i: (i, 0)), BS((m, 2 * D_MODEL), lambda i: (0, 0))],
        out_specs=BS((tm, D_MODEL), lambda i: (i, 0)), name="attn_fwd", compiler_params=_params("parallel"))(q, kv)


def _attn_bwd(q, kv, d_o, tm):
    t = q.shape[0]
    m = kv.shape[0]

    def body(q_ref, kv_ref, do_ref, dq_ref, dkv_ref):
        i = pl.program_id(0)

        @pl.when(i == 0)
        def _():
            dkv_ref[...] = jnp.zeros_like(dkv_ref)

        for hd in range(N_HEADS):
            lo = hd * HEAD_DIM
            q_h = q_ref[:, lo:lo + HEAD_DIM]
            k_h = kv_ref[:, lo:lo + HEAD_DIM]
            v_h = kv_ref[:, D_MODEL + lo:D_MODEL + lo + HEAD_DIM]
            do_h = do_ref[:, lo:lo + HEAD_DIM]
            p = _attn_probs(q_h, k_h)
            dkv_ref[:, D_MODEL + lo:D_MODEL + lo + HEAD_DIM] += _dot(p, do_h, TN)
            dp = _dot(do_h, v_h, NT)
            ds = p * (dp - jnp.sum(dp * p, axis=-1, keepdims=True)) * (1.0 / math.sqrt(HEAD_DIM))
            dq_ref[:, lo:lo + HEAD_DIM] = _dot(ds, k_h).astype(BF16)
            dkv_ref[:, lo:lo + HEAD_DIM] += _dot(ds, q_h, TN)

    row = BS((tm, D_MODEL), lambda i: (i, 0))
    full = BS((m, 2 * D_MODEL), lambda i: (0, 0))
    return pl.pallas_call(
        body, out_shape=(SDS((t, D_MODEL), BF16), SDS((m, 2 * D_MODEL), F32)), grid=(t // tm,),
        in_specs=[row, full, row], out_specs=(row, full), name="attn_bwd",
        compiler_params=_params("arbitrary"))(q, kv, d_o)


TRANSPOSED = ("ffn1_w_gate", "ffn1_w_up", "ffn2_w_gate", "ffn2_w_up", "w_in")
GATHER_PHASES = {"f1a": (("ffn1_w_gate", "ffn1_w_up"),),
                 "f1b": (("ffn1_w_down",),),
                 "win": (("w_in",),),
                 "mix": (("w_mix_out", "w_q", "w_xo"), ("w_kv",), ("w_pool_proj", "w_glu_val", "w_glu_gate")),
                 "f2": (("ffn2_w_gate", "ffn2_w_up", "ffn2_w_down"),)}
REDUCE_GROUPS = (("ffn2_w_gate", "ffn2_w_up", "ffn2_w_down"), ("w_xo",), ("w_q",), ("w_kv",), ("w_mix_out",),
                 ("w_pool_proj", "w_glu_val", "w_glu_gate"), ("w_in",), ("ffn1_w_gate", "ffn1_w_up", "ffn1_w_down"))
SMALL = ("ffn1_norm", "mix_norm", "pool_w", "pool_scale", "ssm_a_re", "ssm_a_im", "ssm_log_dt", "ssm_b_re",
         "ssm_b_im", "ssm_c_re", "ssm_c_im", "ssm_d", "xattn_norm", "mem_norm", "ffn2_norm", "final_norm")
WEIGHTS = ("ffn1_norm", "ffn1_w_gate", "ffn1_w_up", "ffn1_w_down", "mix_norm", "w_in", "pool_w", "pool_scale",
           "w_pool_proj", "ssm_a_re", "ssm_a_im", "ssm_log_dt", "ssm_b_re", "ssm_b_im", "ssm_c_re", "ssm_c_im",
           "ssm_d", "w_glu_val", "w_glu_gate", "w_mix_out", "xattn_norm", "mem_norm", "w_q", "w_kv", "w_xo",
           "ffn2_norm", "ffn2_w_gate", "ffn2_w_up", "ffn2_w_down", "final_norm")


def _small_view(a, n):
    return jnp.swapaxes(a, 3, 4) if n in ("ssm_b_re", "ssm_b_im") else a


def _device_step(x, mem, target, wts, sp, reducer=None):
    t = x.shape[0]
    tm = min(TM, t)
    g = {}

    first_gather = wts.start("f1a")
    u1 = _rmsnorm("norm_ffn1", x, sp["ffn1_norm"], tm, after=first_gather)

    def per_channel(a):
        a = a.reshape(2 * SSM_GROUPS, 1, -1)
        return jnp.broadcast_to(a, (2 * SSM_GROUPS, SSM_GROUP, a.shape[-1])).reshape(SSM_ROWS, a.shape[-1])

    ssm_a = per_channel(sp["ssm_a_re"]), per_channel(sp["ssm_a_im"]), per_channel(sp["ssm_log_dt"])
    ssm_b = sp["ssm_b_re"].reshape(SSM_ROWS, SSM_STATE), sp["ssm_b_im"].reshape(SSM_ROWS, SSM_STATE)
    abr, abi, w_in_s, w_in_s_t, w_out_s_t, w_out_s = _ssm_prep(
        *ssm_a, *ssm_b, sp["ssm_c_re"].reshape(SSM_ROWS, SSM_STATE), sp["ssm_c_im"].reshape(SSM_ROWS, SSM_STATE),
        after=first_gather)
    first_rows = (2, SSM_GROUPS, SSM_GROUP, SSM_STATE)
    a_r = abr.reshape(first_rows)[:, :, 0].reshape(2, 1, SSM_CH)
    a_i = abi.reshape(first_rows)[:, :, 0].reshape(2, 1, SSM_CH)
    mem_n = _rmsnorm("norm_mem", mem, sp["mem_norm"], mem.shape[0], after=first_gather)

    (w_gu,) = wts.finish("f1a", [u1, w_in_s, w_in_s_t, w_out_s, w_out_s_t, a_r, a_i, mem_n])
    w_f1 = {"gate": (w_gu, FFN_GATE), "up": (w_gu, FFN_UP)}
    down_gather = wts.start("f1b", [w_gu])
    g1, up1, a1 = _ffn_up("ffn1_up", u1, w_f1, tm, after=wts.start("win", down_gather))
    (w_dn,) = wts.finish("f1b", [a1])
    w_f1["down"] = (w_dn, 0)
    h1, u2 = _ffn_down("ffn1_down", a1, w_f1, x, tm, next_gain=sp["mix_norm"])

    (w_in_g,) = wts.finish("win", [u2])
    w_in_t = w_in_g.reshape(D_FF, D_MODEL)
    proj = _mm("mix_in", [(u2, BS((tm, D_MODEL), lambda j, i: (i, 0)), w_in_t, BS((D_FF // 2, D_MODEL), lambda j, i: (j, 0)), NT)],
               grid=(2, t // tm), out_shape=SDS((t, D_FF), F32), out_spec=BS((tm, D_FF // 2), lambda j, i: (i, j)),
               after=wts.start("f2", wts.start("mix", [w_in_g])))
    pooled, mixed, ms = _pool_fwd(proj, sp["pool_w"][0], sp["pool_scale"])

    s_in = proj[:, D_POOL:D_POOL + D_SSM].astype(BF16)
    states, y_dirs = [], []
    for dr in range(2):
        st, yd = _ssm_scan(f"ssm_scan_fwd{dr}", s_in, w_in_s[dr], a_r[dr], a_i[dr], w_out_s[dr], reverse=(dr == 1))
        states.append(st)
        y_dirs.append(yd)
    w_sq, w_kv, w_e = wts.finish("mix", y_dirs)
    w_mo, w_q, w_xo = (w_sq[:, 256 * k:256 * (k + 1)].reshape(D_MODEL, D_MODEL) for k in range(3))
    w_d = w_kv[:, None]
    y_total, yssm = _ssm_combine(proj, y_dirs[0], y_dirs[1], sp["ssm_d"], tm)

    merged = _mixer_merge(ms, yssm, w_e, proj, tm)
    h2, u3 = _mm_resid_norm("mix_out", merged, w_mo, h1, sp["xattn_norm"], tm)

    q = _plain_mm("attn_q", u3, w_q, NN, BF16, tm)
    n_mem = mem.shape[0]
    kv = _mm("attn_kv", [(mem_n, BS((n_mem, D_MODEL), lambda s: (0, 0)), w_d, BS((None, None, D_MODEL, 512), lambda s: (s, 0, 0, 0)), NN)],
             grid=(N_SHARD,), out_shape=SDS((n_mem, 2 * D_MODEL), BF16), out_spec=BS((n_mem, 512), lambda s: (0, s)))
    o = _attn_fwd(q, kv, tm)
    h3, u4 = _mm_resid_norm("attn_out", o, w_xo, h2, sp["ffn2_norm"], tm)

    (w_2,) = wts.finish("f2", [u4])
    w_f2 = {"gate": (w_2, FFN_GATE), "up": (w_2, FFN_UP), "down": (w_2, FFN_DOWN)}
    g2, up2, a2 = _ffn_up("ffn2_up", u4, w_f2, tm)
    loss, dh4, dh4_b, g["final_norm"] = _ffn_down("ffn2_down", a2, w_f2, h3, tm,
                                                  head=(sp["final_norm"].reshape(1, D_MODEL), target))

    dg2, dup2 = _ffn_bwd_act("ffn2_bwd_act", dh4_b, w_f2, g2, up2, tm)
    dw_f2 = _ffn_dw("ffn2_dw", u4, dg2, dup2, a2, dh4_b, tm)
    dh3, dh3_b, g["ffn2_norm"] = _ffn_dx("ffn2_dx", dg2, dup2, w_f2, h3, sp["ffn2_norm"], dh4, tm)

    d_o = _plain_mm("attn_out_dx", dh3_b, w_xo, NT, BF16, tm)
    dw_xo = _dw_mm("attn_out_dw", o, dh3_b, tm)
    dq, dkv = _attn_bwd(q, kv, d_o, tm)
    dw_q = _dw_mm("attn_q_dw", u3, dq, tm)
    dh2, dh2_b, g["xattn_norm"] = _mm_norm_bwd("attn_q_dx", dq, w_q, NT, h2, sp["xattn_norm"], dh3, tm)
    dw_kv = _mm("attn_kv_dw", [(mem_n, BS((n_mem, D_MODEL), lambda s: (0, 0)), dkv, BS((n_mem, 512), lambda s: (0, s)), TN)],
                grid=(N_SHARD,), out_shape=SDS((N_SHARD, D_MODEL, 512), BF16), out_spec=BS((None, D_MODEL, 512), lambda s: (s, 0, 0)))
    dmem_n = _mm("attn_kv_dx", [(dkv, BS((n_mem, 512), lambda s: (0, s)), w_d, BS((None, None, D_MODEL, 512), lambda s: (s, 0, 0, 0)), NT)],
                 grid=(N_SHARD,), red_axis=0, out_shape=SDS((n_mem, D_MODEL), F32), out_spec=BS((n_mem, D_MODEL), lambda s: (0, 0)))
    _, _, g["mem_norm"] = _rmsnorm_bwd("norm_mem_bwd", mem, sp["mem_norm"], dmem_n, None, n_mem)

    square = (N_SHARD, D_MODEL // N_SHARD, D_MODEL)
    early = [dw_f2.reshape(N_SHARD, 3 * FF_SH, D_MODEL), dw_xo.reshape(square), dw_q.reshape(square), dw_kv]
    swapping = reducer.swap_start("a1", early) if reducer is not None else []
    dmerged = _plain_mm("mix_out_dx", dh2_b, w_mo, NT, BF16, tm, after=swapping)
    dw_mo = _dw_mm("mix_out_dw", merged, dh2_b, tm)
    d_gp, d_gs, dzp, dzv, dzg = _mixer_merge_bwd(ms, yssm, w_e, proj, dmerged, tm)
    dw_e = _mixer_dw(ms, yssm, dzp, dzv, dzg, tm)
    d_ms, d_yt = _mixer_dx(dzp, dzv, dzg, w_e, y_total, tm)
    dp, d_scale, d_pw = _pool_bwd(d_ms, mixed, pooled, sp["pool_w"][0], sp["pool_scale"])
    g["pool_scale"] = d_scale
    g["pool_w"] = d_pw[None]

    d_yt_b = d_yt.astype(BF16)
    du_dirs, lams = [], []
    for dr in range(2):
        lam, du = _ssm_scan(f"ssm_scan_bwd{dr}", d_yt_b, w_out_s_t[dr], a_r[dr], -a_i[dr], w_in_s_t[dr], reverse=(dr == 0))
        du_dirs.append(du)
        lams.append(lam)
    ds, g["ssm_d"] = _ssm_ds(proj, d_yt, du_dirs[0], du_dirs[1], sp["ssm_d"], tm)

    d_proj = jnp.concatenate([dp, ds, d_gp, d_gs], axis=1)
    tw = min(2 * tm, t)
    dw_in_t = _mm("mix_in_dw", [(d_proj, BS((tw, D_FF // 2), lambda j, i: (i, j)), u2, BS((tw, D_MODEL), lambda j, i: (i, 0)), TN)],
                  grid=(2, t // tw), red_axis=1, out_shape=SDS((D_FF, D_MODEL), BF16), out_spec=BS((D_FF // 2, D_MODEL), lambda j, i: (j, 0)))
    dh1, dh1_b, g["mix_norm"] = _mm_norm_bwd("mix_in_dx", d_proj, w_in_t, NN, h1, sp["mix_norm"], dh2, tm)

    early += [dw_mo.reshape(square), dw_e, dw_in_t.reshape(N_SHARD, FF_SH, D_MODEL)]
    g["final_norm"] = g["final_norm"].reshape(D_MODEL)

    travelling = reducer.start("a", early[4:], swapped=["a1"], after=list(g.values())) if reducer is not None else []
    d_abr, d_abi, d_cm, d_bm = [], [], [], []
    for dr in range(2):
        da_r, da_i, d_win, d_woutt = _ssm_param_grads(f"ssm_param_grads{dr}", lams[dr], states[dr], s_in, d_yt_b,
                                                      reverse=(dr == 1), after=travelling)
        d_abr.append(da_r)
        d_abi.append(da_i)
        d_bm.append(d_win)
        d_cm.append(d_woutt)

    def first_channel(da):
        da = jnp.stack(da).reshape(2, SSM_GROUPS, 1, SSM_STATE)
        return jnp.pad(da, ((0, 0), (0, 0), (0, SSM_GROUP - 1), (0, 0))).reshape(SSM_ROWS, SSM_STATE)

    d_ar, d_ai, d_ldt, d_br, d_bi, d_cr, d_ci = _ssm_prep_bwd(
        *ssm_a, *ssm_b, first_channel(d_abr), first_channel(d_abi), jnp.stack(d_bm), jnp.stack(d_cm))
    per_group = (2 * SSM_GROUPS, SSM_GROUP * SSM_STATE)
    g["ssm_a_re"] = d_ar.reshape(2 * SSM_GROUPS, SSM_GROUP, SSM_STATE).sum(axis=1).reshape(sp["ssm_a_re"].shape)
    g["ssm_a_im"] = d_ai.reshape(2 * SSM_GROUPS, SSM_GROUP, SSM_STATE).sum(axis=1).reshape(sp["ssm_a_im"].shape)
    g["ssm_log_dt"] = d_ldt.reshape(per_group).sum(axis=1).reshape(sp["ssm_log_dt"].shape)
    g["ssm_b_re"] = d_br.reshape(sp["ssm_b_re"].shape)
    g["ssm_b_im"] = d_bi.reshape(sp["ssm_b_im"].shape)
    g["ssm_c_re"] = d_cr.reshape(sp["ssm_c_re"].shape)
    g["ssm_c_im"] = d_ci.reshape(sp["ssm_c_im"].shape)
    if reducer is not None:
        travelling = travelling + [d_ar, d_br, d_cr]
    dg1, dup1 = _ffn_bwd_act("ffn1_bwd_act", dh1_b, w_f1, g1, up1, tm, after=travelling)
    dw_f1 = _ffn_dw("ffn1_dw", u1, dg1, dup1, a1, dh1_b, tm).reshape(N_SHARD, 3 * FF_SH, D_MODEL)
    if reducer is not None:
        travelling = reducer.start("b", [dw_f1], after=reducer.finish("a", [dw_f1]))
        travelling = travelling + reducer.join_start("a", after=travelling)
    grad_x, _, g["ffn1_norm"] = _ffn_dx("ffn1_dx", dg1, dup1, w_f1, x, sp["ffn1_norm"], dh1, tm, after=travelling)
    if reducer is not None:
        reducer.finish("b", [grad_x])
        reducer.join_finish("a", [grad_x])
    return loss, grad_x, early + [dw_f1], g


def _mesh_place():
    x, y, c = lax.axis_index("x"), lax.axis_index("y"), lax.axis_index("c")
    chips = [(1 - x, y), (x, 1 - y), (1 - x, 1 - y)]
    return x, y, c, chips


def _remote(src, dst, send_sems, recv_sems, k, to):
    return pltpu.make_async_remote_copy(src_ref=src, dst_ref=dst, send_sem=send_sems.at[k], recv_sem=recv_sems.at[k],
                                        device_id=to, device_id_type=MESH)


def _sibling_swap_halves(tag, grads, after=()):
    n = len(grads)
    after_ops, after_specs = _after_operands(after)

    def body(*refs):
        ins, outs = refs[:n], refs[n + len(after_ops):2 * n + len(after_ops)]
        send_sems, recv_sems = refs[2 * n + len(after_ops):]
        x, y, c, _ = _mesh_place()
        sibling = (x, y, 1 - c)
        copies = []
        for k in range(n):
            half = grads[k].shape[1] // 2
            theirs = pl.ds(pl.multiple_of((1 - c) * half, 16), half)
            cp = _remote(ins[k].at[:, theirs, :], outs[k], send_sems, recv_sems, k, sibling)
            cp.start()
            copies.append(cp)
        for cp in copies:
            cp.wait_recv()
        for cp in copies:
            cp.wait_send()

    hbm = BS(memory_space=pl.ANY)
    return pl.pallas_call(
        body, out_shape=tuple(SDS((g.shape[0], g.shape[1] // 2, g.shape[2]), g.dtype) for g in grads),
        in_specs=[hbm] * n + after_specs, out_specs=(hbm,) * n,
        scratch_shapes=[pltpu.SemaphoreType.DMA((n,)), pltpu.SemaphoreType.DMA((n,))],
        name="reduce_sibling_send_" + tag, compiler_params=_params())(*grads, *after_ops)


def _row_tile(rows, cap=512):
    return max(r for r in range(16, cap + 1, 16) if rows % r == 0)


REDUCE_STEPS = 2


def _chip_presum(tag, grads, gots, c_idx):
    n = len(grads)
    halves = [g.shape[1] // 2 for g in grads]
    tiles = [(h // REDUCE_STEPS, g.shape[2]) for h, g in zip(halves, grads)]

    def body(c_ref, *refs):
        for k in range(n):
            refs[2 * n + k][...] = (refs[k][...].astype(F32) + refs[n + k][...].astype(F32)).astype(BF16)

    mine = [BS((None, None) + tile, lambda s, i, c_ref: (s, c_ref[0], i, 0)) for tile in tiles]
    plain = [BS((None,) + tile, lambda s, i, c_ref: (s, i, 0)) for tile in tiles]
    return list(pl.pallas_call(
        body, out_shape=tuple(SDS((g.shape[0], h, g.shape[2]), BF16) for g, h in zip(grads, halves)),
        grid_spec=pltpu.PrefetchScalarGridSpec(num_scalar_prefetch=1, grid=(N_SHARD, REDUCE_STEPS),
                                               in_specs=mine + plain, out_specs=plain),
        name="reduce_presum_" + tag, compiler_params=_params("parallel", "parallel"))(
            c_idx, *[g.reshape(g.shape[0], 2, h, g.shape[2]) for g, h in zip(grads, halves)], *gots))


HBM_SPEC = BS(memory_space=pltpu.HBM)
SEM_SPEC = BS(memory_space=pltpu.SEMAPHORE)
DATAFLOW = pltpu.SideEffectType.DATAFLOW_SIDE_EFFECTING


def _chip_exchange_copies(parts, lands, send_sems, recv_sems):
    _, _, c, chips = _mesh_place()
    return [_remote(parts[k].at[2 * px + py], lands[k].at[j], send_sems, recv_sems, 3 * k + j, (px, py, c))
            for k in range(len(parts)) for j, (px, py) in enumerate(chips)]


def _gather_copies(shards, lands, send_sems, recv_sems):
    x, y, c, chips = _mesh_place()
    return [_remote(shards[k], lands[k].at[2 * x + y], send_sems, recv_sems, 3 * k + j, (px, py, c))
            for k in range(len(shards)) for j, (px, py) in enumerate(chips)]


def _gather_half_copies(shards, lands, send_sems, recv_sems):
    x, y, c, chips = _mesh_place()
    out = []
    for k in range(len(shards)):
        half = shards[k].shape[0] // 2
        mine = pl.ds(pl.multiple_of(c * half, 16), half)
        for j, (px, py) in enumerate(chips):
            out.append(_remote(shards[k].at[mine, :], lands[k].at[2 * x + y, mine, :], send_sems, recv_sems,
                               3 * k + j, (px, py, c)))
    return out


def _sibling_fill(tag, lands):
    n = len(lands)

    def body(*refs):
        outs = refs[n:2 * n]
        send_sems, recv_sems = refs[2 * n:]
        x, y, c, chips = _mesh_place()
        copies = []
        for k in range(n):
            half = lands[k].shape[1] // 2
            mine = pl.ds(pl.multiple_of(c * half, 16), half)
            for j, (px, py) in enumerate(chips):
                blk = outs[k].at[2 * px + py, mine, :]
                copies.append(_remote(blk, blk, send_sems, recv_sems, 3 * k + j, (x, y, 1 - c)))
        for cp in copies:
            cp.start()
        for cp in copies:
            cp.wait_recv()
        for cp in copies:
            cp.wait_send()

    hbm = BS(memory_space=pl.ANY)
    return list(pl.pallas_call(
        body, out_shape=tuple(SDS(a.shape, a.dtype) for a in lands),
        in_specs=[hbm] * n, out_specs=(hbm,) * n, input_output_aliases={k: k for k in range(n)},
        scratch_shapes=[pltpu.SemaphoreType.DMA((3 * n,)), pltpu.SemaphoreType.DMA((3 * n,))],
        name="gather_fill_" + tag, compiler_params=_params())(*lands))


def _swap_copies(grads, lands, send_sems, recv_sems):
    x, y, c, _ = _mesh_place()
    out = []
    for k in range(len(grads)):
        half = grads[k].shape[1] // 2
        theirs = pl.ds(pl.multiple_of((1 - c) * half, 16), half)
        out.append(_remote(grads[k].at[:, theirs, :], lands[k], send_sems, recv_sems, k, (x, y, 1 - c)))
    return out


def _join_copies(fulls, same, send_sems, recv_sems):
    x, y, c, _ = _mesh_place()
    out = []
    for k in range(len(fulls)):
        half = fulls[k].shape[0] // 2
        mine = fulls[k].at[pl.ds(pl.multiple_of(c * half, 8), half), :]
        out.append(_remote(mine, mine, send_sems, recv_sems, k, (x, y, 1 - c)))
    return out


def _everyone_copies(packs, lands, send_sems, recv_sems):
    x, y, c, _ = _mesh_place()
    out = []
    for k in range(len(packs)):
        for j in range(N_DEV - 1):
            bx, by, bc = (j + 1) >> 2 & 1, (j + 1) >> 1 & 1, (j + 1) & 1
            peer = (x ^ bx, y ^ by, c ^ bc)
            out.append(_remote(packs[k], lands[k].at[4 * x + 2 * y + c], send_sems, recv_sems, (N_DEV - 1) * k + j, peer))
    return out


def _split_start(name, copies, sources, land_shapes, after=(), fanout=3):
    n = len(sources)
    n_land = len(land_shapes)
    m = n + n_land
    n_sems = fanout * n
    after_ops, after_specs = _after_operands(after)

    def body(*refs):
        ins = refs[:n]
        lands = refs[n:m] if n_land else ins
        send_sems, recv_sems = refs[m + len(after_ops)], refs[m + len(after_ops) + 1]
        token = refs[-1]
        for cp in copies(ins, lands, send_sems, recv_sems):
            cp.start()
        token[...] = jnp.zeros_like(token)

    lands = [pltpu.with_memory_space_constraint(lax.empty(s, d), pltpu.HBM) for s, d in land_shapes]
    sources = [pltpu.with_memory_space_constraint(p, pltpu.HBM) for p in sources]
    thru = [pltpu.HBM(a.shape, a.dtype) for a in sources + lands]
    out = pl.pallas_call(
        body, name=name,
        out_shape=(pltpu.SemaphoreType.DMA((n_sems,)), pltpu.SemaphoreType.DMA((n_sems,)), *thru, SDS((8, 128), F32)),
        in_specs=[HBM_SPEC] * m + after_specs,
        out_specs=(SEM_SPEC, SEM_SPEC, *[HBM_SPEC] * m, BS(memory_space=pltpu.VMEM)),
        input_output_aliases={i: 2 + i for i in range(m)},
        compiler_params=pltpu.CompilerParams(has_side_effects=DATAFLOW))(*sources, *lands, *after_ops)
    return out[0], out[1], list(out[2:2 + n]), list(out[2 + n:2 + m]), out[-1]


def _split_wait(name, copies, send_sems, recv_sems, sources, lands, after):
    n = len(sources)
    m = n + len(lands)
    after_ops, after_specs = _after_operands(after)

    def body(*refs):
        ins = refs[:n]
        zones = refs[n:m] if m > n else ins
        for cp in copies(ins, zones, refs[m], refs[m + 1]):
            cp.wait_send()
            cp.wait_recv()

    out = pl.pallas_call(
        body, name=name,
        out_shape=tuple(pltpu.HBM(a.shape, a.dtype) for a in sources + lands),
        in_specs=[HBM_SPEC] * m + [SEM_SPEC, SEM_SPEC] + after_specs, out_specs=(HBM_SPEC,) * m,
        input_output_aliases={i: i for i in range(m)},
        compiler_params=pltpu.CompilerParams(has_side_effects=DATAFLOW))(*sources, *lands, send_sems, recv_sems, *after_ops)
    return list(out[:n]), list(out[n:])


class _WeightGatherer:
    def __init__(self, shards):
        self.shards, self.open = shards, {}
        self.me = 2 * lax.axis_index("x") + lax.axis_index("y")

    HALVED = ("f1a", "mix")

    def start(self, tag, after=()):
        shapes = [((N_SHARD,) + s.shape, s.dtype) for s in self.shards[tag]]
        copies = _gather_half_copies if tag in self.HALVED else _gather_copies
        self.open[tag] = _split_start("gather_start_" + tag, copies, self.shards[tag], shapes, after)
        return [self.open[tag][-1]]

    def finish(self, tag, after):
        send_sems, recv_sems, shards, lands, _ = self.open.pop(tag)
        copies = _gather_half_copies if tag in self.HALVED else _gather_copies
        shards, lands = _split_wait("gather_wait_" + tag, copies, send_sems, recv_sems, shards, lands, after)
        if tag in self.HALVED:
            lands = _sibling_fill(tag, lands)
        return [lax.dynamic_update_slice(zone, s[None], (self.me, 0, 0)) for zone, s in zip(lands, shards)]


class _GradReducer:
    def __init__(self):
        self.c_idx = lax.axis_index("c").astype(jnp.int32).reshape(1)
        self.place = jnp.stack([2 * lax.axis_index("x") + lax.axis_index("y"), lax.axis_index("c")]).astype(jnp.int32)
        self.swaps, self.open, self.landed, self.joins, self.reduced = {}, {}, {}, {}, []

    def swap_start(self, tag, grads, after=()):
        shapes = [((g.shape[0], g.shape[1] // 2, g.shape[2]), g.dtype) for g in grads]
        self.swaps[tag] = _split_start("reduce_swap_start_" + tag, _swap_copies, grads, shapes, after, fanout=1)
        return [self.swaps[tag][-1]]

    def start(self, tag, grads, after=(), swapped=()):
        pairs = []
        for s in swapped:
            send_sems, recv_sems, early, lands, _ = self.swaps.pop(s)
            pairs += zip(*_split_wait("reduce_swap_wait_" + s, _swap_copies, send_sems, recv_sems, early, lands, grads[-1:]))
        pairs += zip(grads, _sibling_swap_halves(tag, grads, after))
        parts = _chip_presum(tag, [g for g, _ in pairs], [s for _, s in pairs], self.c_idx)
        shapes = [((3,) + p.shape[1:], p.dtype) for p in parts]
        self.open[tag] = _split_start("reduce_exchange_start_" + tag, _chip_exchange_copies, parts, shapes)
        return [self.open[tag][-1]]

    def finish(self, tag, after):
        send_sems, recv_sems, parts, lands, _ = self.open.pop(tag)
        self.landed[tag] = _split_wait("reduce_exchange_wait_" + tag, _chip_exchange_copies, send_sems, recv_sems, parts, lands, after)
        return self.landed[tag][1][:1]

    def _sums(self, tag, after=()):
        parts, landed = self.landed.pop(tag)
        return _chip_sum(tag, parts, landed, self.place, after)

    def join_start(self, tag, after=()):
        self.joins[tag] = _split_start("reduce_join_start_" + tag, _join_copies, self._sums(tag, after), [], fanout=1)
        return [self.joins[tag][-1]]

    def join_finish(self, tag, after):
        send_sems, recv_sems, fulls, _, _ = self.joins.pop(tag)
        self.reduced += _split_wait("reduce_join_wait_" + tag, _join_copies, send_sems, recv_sems, fulls, [], after)[0]

    def join(self, tag, after=()):
        self.reduced += _sibling_join_halves(self._sums(tag), after)


def _chip_sum(tag, parts, gots, place, after=()):
    n = len(parts)
    tiles = [(p.shape[1] // REDUCE_STEPS, p.shape[2]) for p in parts]
    after_ops, after_specs = _after_operands(after)

    def body(place_ref, *refs):
        outs = refs[2 * n + len(after_ops):]
        for k in range(n):
            acc = refs[k][...].astype(F32)
            for j in range(3):
                acc = acc + refs[n + k][j].astype(F32)
            outs[k][...] = acc

    return list(pl.pallas_call(
        body, out_shape=tuple(SDS((2 * p.shape[1], p.shape[2]), F32) for p in parts),
        grid_spec=pltpu.PrefetchScalarGridSpec(
            num_scalar_prefetch=1, grid=(REDUCE_STEPS,),
            in_specs=[BS((None,) + tile, lambda i, place_ref: (place_ref[0], i, 0)) for tile in tiles]
            + [BS((3,) + tile, lambda i, place_ref: (0, i, 0)) for tile in tiles] + after_specs,
            out_specs=[BS(tile, lambda i, place_ref: (place_ref[1] * REDUCE_STEPS + i, 0)) for tile in tiles]),
        name="reduce_sum_" + tag, compiler_params=_params("parallel"))(place, *parts, *gots, *after_ops))


def _sibling_join_halves(fulls, after=()):
    n = len(fulls)
    after_ops, after_specs = _after_operands(after)

    def body(*refs):
        outs = refs[n + len(after_ops):2 * n + len(after_ops)]
        send_sems, recv_sems = refs[2 * n + len(after_ops):]
        copies = _join_copies(outs, outs, send_sems, recv_sems)
        for cp in copies:
            cp.start()
        for cp in copies:
            cp.wait_recv()
        for cp in copies:
            cp.wait_send()

    hbm = BS(memory_space=pl.ANY)
    return list(pl.pallas_call(
        body, out_shape=tuple(SDS(f.shape, f.dtype) for f in fulls),
        in_specs=[hbm] * n + after_specs, out_specs=(hbm,) * n, input_output_aliases={k: k for k in range(n)},
        scratch_shapes=[pltpu.SemaphoreType.DMA((n,)), pltpu.SemaphoreType.DMA((n,))],
        name="reduce_sibling_join", compiler_params=_params())(*fulls, *after_ops))


N_DEV = 8


def _sum_devices(packs):
    _, rows, lanes = packs.shape

    def body(p_ref, o_ref):
        acc = p_ref[0]
        for dev in range(1, N_DEV):
            acc = acc + p_ref[dev]
        o_ref[...] = acc

    vm = BS(memory_space=pltpu.VMEM)
    return pl.pallas_call(body, out_shape=SDS((rows, lanes), F32), in_specs=[vm], out_specs=vm,
                          name="small_sum", compiler_params=_params())(packs)


def _adamw_refs(w_ref, g_ref, m_ref, v_ref, go_ref, d_ref, mo_ref, vo_ref):
    bc1 = 1.0 - ADAM_B1 ** ADAM_STEP
    bc2 = 1.0 - ADAM_B2 ** ADAM_STEP
    g = g_ref[...]
    m_new = ADAM_B1 * m_ref[...] + (1.0 - ADAM_B1) * g
    v_new = ADAM_B2 * v_ref[...] + (1.0 - ADAM_B2) * (g * g)
    go_ref[...] = g
    mo_ref[...] = m_new
    vo_ref[...] = v_new
    d_ref[...] = -ADAM_LR * ((m_new / bc1) / (jnp.sqrt(v_new / bc2) + ADAM_EPS) + ADAM_WD * w_ref[...])


def _adamw_small(ws, gs, ms, vs):
    n = len(ws)

    def body(*refs):
        for k in range(n):
            _adamw_refs(*[refs[j * n + k] for j in range(4)], *refs[4 * n + 4 * k:4 * n + 4 * k + 4])

    vm = BS(memory_space=pltpu.VMEM)
    outs = pl.pallas_call(
        body, out_shape=tuple(SDS(a.shape, F32) for a in ws for _ in range(4)), in_specs=[vm] * (4 * n),
        out_specs=(vm,) * (4 * n), name="adamw_small", compiler_params=_params())(*ws, *gs, *ms, *vs)
    return [outs[4 * k:4 * k + 4] for k in range(n)]


def _adamw(name, w, grad, row0, m, v, after=()):
    rows, cols = w.shape
    tr = rows if rows < 16 else _row_tile(rows, 352)
    after_ops, after_specs = _after_operands(after)

    def body(w_ref, g_ref, m_ref, v_ref, *rest):
        _adamw_refs(w_ref, g_ref, m_ref, v_ref, *rest[len(after_ops):])

    blk = BS((tr, cols), lambda i: (i, 0))
    shape = SDS((rows, cols), F32)
    return pl.pallas_call(
        body, out_shape=(shape,) * 4, grid=(rows // tr,),
        in_specs=[blk, BS((tr, cols), lambda i: (row0 // tr + i, 0)), blk, blk] + after_specs, out_specs=(blk,) * 4,
        name=name, compiler_params=_params("parallel"))(w, grad, m, v, *after_ops)


SMALL_LANES = 128


def _pack_small(parts):
    flat = jnp.concatenate([jnp.ravel(p) for p in parts])
    rows = -(-flat.shape[0] // (64 * SMALL_LANES)) * 64
    return jnp.pad(flat, (0, rows * SMALL_LANES - flat.shape[0])).reshape(rows, SMALL_LANES)


def _unpack_small(packed, like):
    flat = jnp.ravel(packed)
    out, at = [], 0
    for p in like:
        out.append(flat[at:at + p.size].reshape(p.shape))
        at += p.size
    return out


def kernel(x, mem, ffn1_norm, ffn1_w_gate, ffn1_w_up, ffn1_w_down, mix_norm, w_in, pool_w, pool_scale, w_pool_proj, ssm_a_re, ssm_a_im, ssm_log_dt, ssm_b_re, ssm_b_im, ssm_c_re, ssm_c_im, ssm_d, w_glu_val, w_glu_gate, w_mix_out, xattn_norm, mem_norm, w_q, w_kv, w_xo, ffn2_norm, ffn2_w_gate, ffn2_w_up, ffn2_w_down, final_norm, loss_target, m_ffn1_norm, m_ffn1_w_gate, m_ffn1_w_up, m_ffn1_w_down, m_mix_norm, m_w_in, m_pool_w, m_pool_scale, m_w_pool_proj, m_ssm_a_re, m_ssm_a_im, m_ssm_log_dt, m_ssm_b_re, m_ssm_b_im, m_ssm_c_re, m_ssm_c_im, m_ssm_d, m_w_glu_val, m_w_glu_gate, m_w_mix_out, m_xattn_norm, m_mem_norm, m_w_q, m_w_kv, m_w_xo, m_ffn2_norm, m_ffn2_w_gate, m_ffn2_w_up, m_ffn2_w_down, m_final_norm, v_ffn1_norm, v_ffn1_w_gate, v_ffn1_w_up, v_ffn1_w_down, v_mix_norm, v_w_in, v_pool_w, v_pool_scale, v_w_pool_proj, v_ssm_a_re, v_ssm_a_im, v_ssm_log_dt, v_ssm_b_re, v_ssm_b_im, v_ssm_c_re, v_ssm_c_im, v_ssm_d, v_w_glu_val, v_w_glu_gate, v_w_mix_out, v_xattn_norm, v_mem_norm, v_w_q, v_w_kv, v_w_xo, v_ffn2_norm, v_ffn2_w_gate, v_ffn2_w_up, v_ffn2_w_down, v_final_norm):
    given = dict(locals())
    w = {n: given[n] for n in WEIGHTS}
    m = {n: given["m_" + n] for n in WEIGHTS}
    v = {n: given["v_" + n] for n in WEIGHTS}

    def shard_view(a, n):
        return a[0].T if n in TRANSPOSED else a[0]

    def shard_unview(a, n):
        return (a.T if n in TRANSPOSED else a)[None]

    shards = {tag: [jnp.concatenate([shard_view(w[n], n).astype(BF16) for n in grp], axis=0) for grp in arrays]
              for tag, arrays in GATHER_PHASES.items()}
    reducer = _GradReducer()
    ws, ms, vs = ({n: _small_view(a[n], n) for n in SMALL} for a in (w, m, v))
    loss_part, grad_x, _, small = _device_step(x[0], mem[0], loss_target[0], _WeightGatherer(shards), ws, reducer)

    small_like = [ws[n] for n in SMALL] + [loss_part[0, :1]]
    pack = _pack_small([small[n] for n in SMALL] + [loss_part[0, :1]])
    everyone = _split_start("small_start", _everyone_copies, [pack], [((N_DEV,) + pack.shape, F32)], fanout=N_DEV - 1)
    reducer.join("b", after=everyone[-1:])

    grads, delta, new_m, new_v = {}, {}, {}, {}
    big_done = []
    for grp, red in zip(REDUCE_GROUPS, reducer.reduced):
        row0 = 0
        for n in grp:
            w_n = shard_view(w[n], n)
            outs = _adamw("adamw_" + n, w_n, red, row0, shard_view(m[n], n), shard_view(v[n], n), after=everyone[-1:])
            grads[n], delta[n], new_m[n], new_v[n] = (shard_unview(o, n) for o in outs)
            big_done.append(outs[1])
            row0 += w_n.shape[0]

    send_sems, recv_sems, packs, landed, _ = everyone
    packs, landed = _split_wait("small_wait", _everyone_copies, send_sems, recv_sems, packs, landed, big_done)
    mine = 4 * lax.axis_index("x") + 2 * lax.axis_index("y") + lax.axis_index("c")
    summed = _sum_devices(lax.dynamic_update_slice(landed[0], packs[0][None], (mine, 0, 0)))
    g_small = dict(zip(SMALL + ("loss",), _unpack_small(summed, small_like)))
    loss = g_small.pop("loss").reshape(())
    def two_d(a):
        return a.reshape(-1, a.shape[-1])

    updated = _adamw_small(*([two_d(a[n]) for n in SMALL] for a in (ws, g_small, ms, vs)))
    for n, outs in zip(SMALL, updated):
        grads[n], delta[n], new_m[n], new_v[n] = (_small_view(o.reshape(ws[n].shape), n) for o in outs)

    return (loss, grad_x[None], *[grads[n] for n in WEIGHTS], *[delta[n] for n in WEIGHTS],
            *[new_m[n] for n in WEIGHTS], *[new_v[n] for n in WEIGHTS])
```

```python
import functools
import math

import jax
import jax.numpy as jnp
from jax import lax
from jax.experimental import pallas as pl
from jax.experimental.pallas import tpu as pltpu

F32 = jnp.float32
BF16 = jnp.bfloat16
SDS = jax.ShapeDtypeStruct
BS = pl.BlockSpec
MESH = pl.DeviceIdType.MESH

D_MODEL = 1024
D_FF = 2816
N_SHARD = 4
FF_SH = D_FF // N_SHARD
D_POOL = 512
POOL_WINDOWS = (2, 4, 8, 16)
POOL_GROUP = 128
D_SSM = 256
SSM_GROUPS = 16
SSM_GROUP = 16
SSM_STATE = 64
SSM_CH = SSM_GROUPS * SSM_STATE
N_HEADS = 4
HEAD_DIM = 256
EPS = 1e-6
ADAM_LR, ADAM_B1, ADAM_B2, ADAM_EPS, ADAM_WD, ADAM_STEP = 0.001, 0.9, 0.999, 1e-08, 0.01, 10

VMEM_LIMIT_V7X = 58 * 1024 * 1024
TM = 512

NN = (((1,), (0,)), ((), ()))
NT = (((1,), (1,)), ((), ()))
TN = (((0,), (0,)), ((), ()))


def _params(*sem):
    return pltpu.CompilerParams(dimension_semantics=sem if sem else None, vmem_limit_bytes=VMEM_LIMIT_V7X)


def _dot(a, b, dims=NN):
    return lax.dot_general(a.astype(BF16), b.astype(BF16), dims, preferred_element_type=F32)


def _sigmoid(v):
    return pl.reciprocal(1.0 + jnp.exp(-v), approx=True)


def _block_dims(spec):
    return tuple(d for d in spec.block_shape if d is not None)


def _after_operands(after):
    return list(after), [BS(memory_space=pl.ANY)] * len(after)


def _mm(name, pairs, *, grid, out_shape, out_spec, red_axis=None, extras=(), epilogue=None, after=()):
    n_pairs, n_extra = len(pairs), len(extras)
    n_red = grid[red_axis] if red_axis is not None else 1
    dims = [p[4] for p in pairs]

    def body(*refs):
        ab = refs[:2 * n_pairs]
        ex = refs[2 * n_pairs:2 * n_pairs + n_extra]
        o_ref = refs[2 * n_pairs + n_extra + len(after)]

        def partial():
            acc = None
            for p in range(n_pairs):
                t = _dot(ab[2 * p][...], ab[2 * p + 1][...], dims[p])
                acc = t if acc is None else acc + t
            return acc

        def finish(acc):
            res = epilogue(acc, *[e[...] for e in ex]) if epilogue is not None else acc
            o_ref[...] = res.astype(o_ref.dtype)

        if n_red == 1:
            finish(partial())
        else:
            acc_ref = refs[-1]
            k = pl.program_id(red_axis)

            @pl.when(k == 0)
            def _():
                acc_ref[...] = jnp.zeros_like(acc_ref)

            acc_ref[...] += partial()

            @pl.when(k == n_red - 1)
            def _():
                finish(acc_ref[...])

    operands, in_specs = [], []
    for a, a_spec, b, b_spec, _ in pairs:
        operands += [a, b]
        in_specs += [a_spec, b_spec]
    for e, e_spec in extras:
        operands.append(e)
        in_specs.append(e_spec)
    after_ops, after_specs = _after_operands(after)
    operands += after_ops
    in_specs += after_specs
    scratch = [pltpu.VMEM(_block_dims(out_spec), F32)] if n_red > 1 else []
    sem = tuple("arbitrary" if ax == red_axis else "parallel" for ax in range(len(grid)))
    return pl.pallas_call(body, out_shape=out_shape, grid=grid, in_specs=in_specs, out_specs=out_spec,
                          scratch_shapes=scratch, name=name, compiler_params=_params(*sem))(*operands)


def _rmsnorm(name, h, gain, tm, after=()):
    t, d = h.shape
    after_ops, after_specs = _after_operands(after)

    def body(h_ref, g_ref, *rest):
        u_ref = rest[-1]
        hv = h_ref[...]
        r = lax.rsqrt(jnp.mean(hv * hv, axis=-1, keepdims=True) + EPS)
        u_ref[...] = ((hv * r) * g_ref[...]).astype(u_ref.dtype)

    return pl.pallas_call(
        body, out_shape=SDS((t, d), BF16), grid=(t // tm,),
        in_specs=[BS((tm, d), lambda i: (i, 0)), BS((1, d), lambda i: (0, 0))] + after_specs,
        out_specs=BS((tm, d), lambda i: (i, 0)), name=name, compiler_params=_params("parallel"))(h, gain, *after_ops)


def _rmsnorm_bwd(name, h, gain, du, dh_in, tm):
    t, d = h.shape
    has_in = dh_in is not None

    def body(*refs):
        if has_in:
            h_ref, g_ref, du_ref, dhin_ref, dh_ref, dhb_ref, dg_ref = refs
        else:
            h_ref, g_ref, du_ref, dh_ref, dhb_ref, dg_ref = refs
        i = pl.program_id(0)
        hv = h_ref[...]
        r = lax.rsqrt(jnp.mean(hv * hv, axis=-1, keepdims=True) + EPS)
        n = hv * r
        duv = du_ref[...].astype(F32)
        dn = duv * g_ref[...]
        dh = r * (dn - n * jnp.mean(dn * n, axis=-1, keepdims=True))
        if has_in:
            dh = dhin_ref[...] + dh
        dh_ref[...] = dh
        dhb_ref[...] = dh.astype(BF16)

        @pl.when(i == 0)
        def _():
            dg_ref[...] = jnp.zeros_like(dg_ref)

        dg_ref[...] += jnp.sum(duv * n, axis=0, keepdims=True)

    row = BS((tm, d), lambda i: (i, 0))
    vec = BS((1, d), lambda i: (0, 0))
    operands = [h, gain, du] + ([dh_in] if has_in else [])
    in_specs = [row, vec, row] + ([row] if has_in else [])
    return pl.pallas_call(
        body, out_shape=(SDS((t, d), F32), SDS((t, d), BF16), SDS((1, d), F32)), grid=(t // tm,),
        in_specs=in_specs, out_specs=(row, row, vec), name=name, compiler_params=_params("arbitrary"))(*operands)


def _loss_head_tile(i, hv, g_ref, t_ref, loss_ref, dh_ref, dhb_ref, dg_ref):
    g = g_ref[...]
    r = lax.rsqrt(jnp.mean(hv * hv, axis=-1, keepdims=True) + EPS)
    n = hv * r
    err = n * g - t_ref[...]
    dy = err * (1.0 / hv.shape[-1])
    dn = dy * g
    dh = r * (dn - n * jnp.mean(dn * n, axis=-1, keepdims=True))
    dh_ref[...] = dh
    dhb_ref[...] = dh.astype(BF16)

    @pl.when(i == 0)
    def _():
        dg_ref[...] = jnp.zeros_like(dg_ref)
        loss_ref[...] = jnp.zeros_like(loss_ref)

    dg_ref[...] += jnp.sum(dy * n, axis=0, keepdims=True)
    part = 0.5 * jnp.sum(jnp.mean(err * err, axis=-1, keepdims=True), axis=0, keepdims=True)
    loss_ref[...] += jnp.broadcast_to(part, loss_ref.shape)


def _norm_tile(h, g_ref, u_ref):
    r = lax.rsqrt(jnp.mean(h * h, axis=-1, keepdims=True) + EPS)
    u_ref[...] = ((h * r) * g_ref[...]).astype(u_ref.dtype)


FFN_BLOCK = D_FF // 2


def _ffn_up(name, u, w_f, tm, after=()):
    t, d = u.shape
    after_ops, after_specs = _after_operands(after)

    def body(u_ref, wg_ref, wu_ref, *rest):
        pg_ref, pu_ref, a_ref = rest[len(after_ops):]
        uv = u_ref[...]
        for lo in range(0, D_FF, FFN_BLOCK):
            cols = slice(lo, lo + FFN_BLOCK)
            g = _dot(uv, wg_ref[cols, :], NT)
            up = _dot(uv, wu_ref[cols, :], NT)
            sg = _sigmoid(g)
            silu = g * sg
            a_ref[:, cols] = (silu * up).astype(BF16)
            pu_ref[:, cols] = (0.5 * silu).astype(BF16)
            pg_ref[:, cols] = (0.5 * sg * (1.0 + g * (1.0 - sg)) * up).astype(BF16)

    hid = BS((tm, D_FF), lambda i: (i, 0))
    shape = SDS((t, D_FF), BF16)
    whole = BS((D_FF, d), lambda i: (0, 0))
    return pl.pallas_call(
        body, out_shape=(shape, shape, shape), grid=(t // tm,),
        in_specs=[BS((tm, d), lambda i: (i, 0)), whole, whole] + after_specs,
        out_specs=(hid, hid, hid), name=name,
        compiler_params=_params("parallel"))(u, w_f["gate"], w_f["up"], *after_ops)


def _ffn_down(name, a, w_f, resid, tm, next_gain=None, head=None):
    t, d = resid.shape
    row = BS((tm, d), lambda i: (i, 0))
    vec = BS((1, d), lambda i: (0, 0))

    def body(a_ref, w_ref, res_ref, *rest):
        h = res_ref[...] + 0.5 * _dot(a_ref[...], w_ref[...])
        if head is not None:
            _loss_head_tile(pl.program_id(0), h, *rest)
        else:
            g_ref, h_ref, u_ref = rest
            h_ref[...] = h
            _norm_tile(h, g_ref, u_ref)

    if head is not None:
        extra, extra_specs = list(head), [vec, row]
        out_shape = (SDS((1, 128), F32), SDS((t, d), F32), SDS((t, d), BF16), SDS((1, d), F32))
        out_specs = (BS((1, 128), lambda i: (0, 0)), row, row, vec)
    else:
        extra, extra_specs = [next_gain], [vec]
        out_shape = (SDS((t, d), F32), SDS((t, d), BF16))
        out_specs = (row, row)
    return pl.pallas_call(
        body, out_shape=out_shape, grid=(t // tm,),
        in_specs=[BS((tm, D_FF), lambda i: (i, 0)), BS((D_FF, d), lambda i: (0, 0)), row] + extra_specs,
        out_specs=out_specs, name=name,
        compiler_params=_params("arbitrary" if head is not None else "parallel"))(a, w_f["down"], resid, *extra)


def _mm_resid_norm(name, a, b, resid, next_gain, tm):
    t, d = resid.shape

    def body(a_ref, b_ref, res_ref, g_ref, h_ref, u_ref):
        h = res_ref[...] + _dot(a_ref[...], b_ref[...])
        h_ref[...] = h
        _norm_tile(h, g_ref, u_ref)

    row = BS((tm, d), lambda i: (i, 0))
    return pl.pallas_call(
        body, out_shape=(SDS((t, d), F32), SDS((t, d), BF16)), grid=(t // tm,),
        in_specs=[BS((tm, a.shape[1]), lambda i: (i, 0)), BS(b.shape, lambda i: (0, 0)), row, BS((1, d), lambda i: (0, 0))],
        out_specs=(row, row), name=name, compiler_params=_params("parallel"))(a, b, resid, next_gain)


def _ffn_bwd_act(name, dh_b, w_f, pg, pu, tm, after=()):
    t, d = dh_b.shape
    after_ops, after_specs = _after_operands(after)

    def body(dh_ref, wd_ref, pg_ref, pu_ref, *rest):
        dg_ref, dup_ref = rest[len(after_ops):]
        dh = dh_ref[...]
        for lo in range(0, D_FF, FFN_BLOCK):
            cols = slice(lo, lo + FFN_BLOCK)
            da = _dot(dh, wd_ref[cols, :], NT)
            dg_ref[:, cols] = (da * pg_ref[:, cols].astype(F32)).astype(BF16)
            dup_ref[:, cols] = (da * pu_ref[:, cols].astype(F32)).astype(BF16)

    hid = BS((tm, D_FF), lambda i: (i, 0))
    shape = SDS((t, D_FF), BF16)
    return pl.pallas_call(
        body, out_shape=(shape, shape), grid=(t // tm,),
        in_specs=[BS((tm, d), lambda i: (i, 0)), BS((D_FF, d), lambda i: (0, 0)), hid, hid] + after_specs,
        out_specs=(hid, hid), name=name,
        compiler_params=_params("parallel"))(dh_b, w_f["down"], pg, pu, *after_ops)


def _ffn_dw(name, u, dg, dup, a, dh_b, tm):
    t, d = u.shape
    n_t = t // tm

    def body(u_ref, dg_ref, dup_ref, a_ref, dh_ref, og_ref, ou_ref, od_ref, acc):
        i = pl.program_id(1)

        @pl.when(i == 0)
        def _():
            acc[...] = jnp.zeros_like(acc)

        uv = u_ref[...]
        acc[0] += _dot(dg_ref[...], uv, TN)
        acc[1] += _dot(dup_ref[...], uv, TN)
        acc[2] += _dot(a_ref[...], dh_ref[...], TN)

        @pl.when(i == n_t - 1)
        def _():
            og_ref[...] = acc[0].astype(BF16)
            ou_ref[...] = acc[1].astype(BF16)
            od_ref[...] = (0.5 * acc[2]).astype(BF16)

    hid = BS((tm, FFN_BLOCK), lambda j, i: (i, j))
    row = BS((tm, d), lambda j, i: (i, 0))
    out = BS((FFN_BLOCK, d), lambda j, i: (j, 0))
    shape = SDS((D_FF, d), BF16)
    return pl.pallas_call(
        body, out_shape=(shape, shape, shape), grid=(D_FF // FFN_BLOCK, n_t),
        in_specs=[row, hid, hid, hid, row], out_specs=(out, out, out),
        scratch_shapes=[pltpu.VMEM((3, FFN_BLOCK, d), F32)],
        name=name, compiler_params=_params("parallel", "arbitrary"))(u, dg, dup, a, dh_b)


def _norm_bwd_tile(i, du, h_ref, g_ref, dhin_ref, dh_ref, dhb_ref, dg_ref):
    hv = h_ref[...]
    r = lax.rsqrt(jnp.mean(hv * hv, axis=-1, keepdims=True) + EPS)
    n = hv * r
    dn = du * g_ref[...]
    dh = dhin_ref[...] + r * (dn - n * jnp.mean(dn * n, axis=-1, keepdims=True))
    dh_ref[...] = dh
    dhb_ref[...] = dh.astype(BF16)

    @pl.when(i == 0)
    def _():
        dg_ref[...] = jnp.zeros_like(dg_ref)

    dg_ref[...] += jnp.sum(du * n, axis=0, keepdims=True)


def _norm_bwd_specs(tm):
    row = BS((tm, D_MODEL), lambda i: (i, 0))
    vec = BS((1, D_MODEL), lambda i: (0, 0))
    return [row, vec, row], (row, row, vec)


def _norm_bwd_shapes(t):
    return SDS((t, D_MODEL), F32), SDS((t, D_MODEL), BF16), SDS((1, D_MODEL), F32)


def _ffn_dx(name, dg, dup, w_f, h, gain, dh_in, tm, after=()):
    t = dg.shape[0]
    tm = tm // 2
    after_ops, after_specs = _after_operands(after)

    def body(dg_ref, dup_ref, wg_ref, wu_ref, h_ref, g_ref, dhin_ref, *rest):
        du = _dot(dg_ref[...], wg_ref[...]) + _dot(dup_ref[...], wu_ref[...])
        _norm_bwd_tile(pl.program_id(0), du, h_ref, g_ref, dhin_ref, *rest[len(after_ops):])

    hid = BS((tm, D_FF), lambda i: (i, 0))
    whole = BS((D_FF, D_MODEL), lambda i: (0, 0))
    norm_in, norm_out = _norm_bwd_specs(tm)
    return pl.pallas_call(
        body, out_shape=_norm_bwd_shapes(t), grid=(t // tm,),
        in_specs=[hid, hid, whole, whole] + norm_in + after_specs, out_specs=norm_out, name=name,
        compiler_params=_params("arbitrary"))(dg, dup, w_f["gate"], w_f["up"], h, gain, dh_in, *after_ops)


def _mm_norm_bwd(name, a, b, dims, h, gain, dh_in, tm):
    t = a.shape[0]

    def body(a_ref, b_ref, h_ref, g_ref, dhin_ref, *outs):
        _norm_bwd_tile(pl.program_id(0), _dot(a_ref[...], b_ref[...], dims), h_ref, g_ref, dhin_ref, *outs)

    norm_in, norm_out = _norm_bwd_specs(tm)
    return pl.pallas_call(
        body, out_shape=_norm_bwd_shapes(t), grid=(t // tm,),
        in_specs=[BS((tm, a.shape[1]), lambda i: (i, 0)), BS(b.shape, lambda i: (0, 0))] + norm_in,
        out_specs=norm_out, name=name, compiler_params=_params("arbitrary"))(a, b, h, gain, dh_in)


def _plain_mm(name, a, b, dims, out_dtype, tm, resid=None, after=()):
    t = a.shape[0]
    tm = min(2 * tm, t)
    n = b.shape[1] if dims == NN else b.shape[0]
    extras = [(resid, BS((tm, n), lambda i: (i, 0)))] if resid is not None else []
    epi = (lambda acc, res: res + acc) if resid is not None else None
    return _mm(name, [(a, BS((tm, a.shape[1]), lambda i: (i, 0)), b, BS(b.shape, lambda i: (0, 0)), dims)],
               grid=(t // tm,), out_shape=SDS((t, n), out_dtype), out_spec=BS((tm, n), lambda i: (i, 0)),
               extras=extras, epilogue=epi, after=after)


def _dw_mm(name, a, b, tm, out_dtype=BF16, after=()):
    t, k = a.shape
    n = b.shape[1]
    tm = min(2 * tm, t)
    return _mm(name, [(a, BS((tm, k), lambda i: (i, 0)), b, BS((tm, n), lambda i: (i, 0)), TN)],
               grid=(t // tm,), red_axis=0, out_shape=SDS((k, n), out_dtype), out_spec=BS((k, n), lambda i: (0, 0)),
               after=after)


POOL_CHUNK = 256
POOL_HALO = 8


def _window_sum(v, width, lead):
    n = v.shape[0]
    s = v
    k = 1
    while k < width:
        s = s + pltpu.roll(s, n - k, 0)
        k *= 2
    return pltpu.roll(s, lead, 0) if lead else s


def _pool_count(base, left, right, t, shape):
    pos = base + lax.broadcasted_iota(jnp.int32, shape, 0)
    lo = jnp.maximum(pos - left, 0)
    hi = jnp.minimum(pos + right + 1, t)
    return (hi - lo).astype(F32)


def _pool_fwd(proj, pool_w, pool_scale):
    t = proj.shape[0]
    c, h = POOL_CHUNK, POOL_HALO
    n_chunks = t // c

    def body(proj_hbm, pw_ref, sc_ref, pooled_ref, mixed_ref, ms_ref, pad_ref, sem):
        cp = pltpu.make_async_copy(proj_hbm.at[:, pl.ds(0, D_POOL)], pad_ref.at[pl.ds(h, t), :], sem)
        cp.start()
        pad_ref[pl.ds(0, h), :] = jnp.zeros((h, D_POOL), F32)
        pad_ref[pl.ds(t + h, h), :] = jnp.zeros((h, D_POOL), F32)
        cp.wait()
        for g, width in enumerate(POOL_WINDOWS):
            left = width // 2
            right = width - 1 - left
            cols = slice(g * POOL_GROUP, (g + 1) * POOL_GROUP)
            wmat = pw_ref[g].astype(BF16)
            scale = sc_ref[:, cols]

            def chunk(ci, carry, left=left, right=right, width=width, cols=cols, wmat=wmat, scale=scale):
                base = pl.multiple_of(ci * c, c)
                v = pad_ref[pl.ds(base, c + 2 * h), cols]
                win = _window_sum(v, width, left)[h:h + c]
                cnt = _pool_count(base, left, right, t, (c, POOL_GROUP))
                pooled = (win / cnt - v[h:h + c]).astype(BF16)
                mixed = _dot(pooled, wmat)
                pooled_ref[pl.ds(base, c), cols] = pooled
                mixed_ref[pl.ds(base, c), cols] = mixed.astype(BF16)
                ms_ref[pl.ds(base, c), cols] = (mixed * scale).astype(BF16)
                return carry

            lax.fori_loop(0, n_chunks, chunk, 0)

    vm = BS(memory_space=pltpu.VMEM)
    shape = SDS((t, D_POOL), BF16)
    return pl.pallas_call(
        body, out_shape=(shape, shape, shape),
        in_specs=[BS(memory_space=pl.ANY), vm, vm], out_specs=(vm, vm, vm),
        scratch_shapes=[pltpu.VMEM((t + 2 * h, D_POOL), F32), pltpu.SemaphoreType.DMA],
        name="pool_fwd", compiler_params=_params())(proj, pool_w, pool_scale)


def _pool_bwd(d_ms, mixed, pooled, pool_w, pool_scale):
    t = d_ms.shape[0]
    c, h = POOL_CHUNK, POOL_HALO
    n_chunks = t // c

    def body(dms_ref, mixed_ref, pooled_ref, pw_ref, sc_ref, dp_ref, dsc_ref, dpw_ref, pad_ref):
        pad_ref[pl.ds(0, h), :] = jnp.zeros((h, D_POOL), F32)
        pad_ref[pl.ds(t + h, h), :] = jnp.zeros((h, D_POOL), F32)
        for g, width in enumerate(POOL_WINDOWS):
            left = width // 2
            right = width - 1 - left
            cols = slice(g * POOL_GROUP, (g + 1) * POOL_GROUP)
            wmat = pw_ref[g].astype(BF16)
            scale = sc_ref[:, cols]

            def first(ci, carry, left=left, right=right, cols=cols, wmat=wmat, scale=scale):
                dsc, dpw = carry
                base = pl.multiple_of(ci * c, c)
                dms = dms_ref[pl.ds(base, c), cols].astype(F32)
                dsc = dsc + jnp.sum(dms * mixed_ref[pl.ds(base, c), cols].astype(F32), axis=0, keepdims=True)
                dmix = (dms * scale).astype(BF16)
                dpw = dpw + _dot(pooled_ref[pl.ds(base, c), cols], dmix, TN)
                dpooled = _dot(dmix, wmat, NT)
                cnt = _pool_count(base, left, right, t, (c, POOL_GROUP))
                pad_ref[pl.ds(base + h, c), cols] = dpooled / cnt
                return dsc, dpw

            dsc, dpw = lax.fori_loop(0, n_chunks, first,
                                     (jnp.zeros((1, POOL_GROUP), F32), jnp.zeros((POOL_GROUP, POOL_GROUP), F32)))
            dsc_ref[:, cols] = dsc
            dpw_ref[g] = dpw

            def second(ci, carry, left=left, right=right, width=width, cols=cols):
                base = pl.multiple_of(ci * c, c)
                v = pad_ref[pl.ds(base, c + 2 * h), cols]
                win = _window_sum(v, width, right)[h:h + c]
                cnt = _pool_count(base, left, right, t, (c, POOL_GROUP))
                dp_ref[pl.ds(base, c), cols] = (win - v[h:h + c] * cnt).astype(BF16)
                return carry

            lax.fori_loop(0, n_chunks, second, 0)

    vm = BS(memory_space=pltpu.VMEM)
    return pl.pallas_call(
        body, out_shape=(SDS((t, D_POOL), BF16), SDS((1, D_POOL), F32), SDS((4, POOL_GROUP, POOL_GROUP), F32)),
        in_specs=[vm] * 5, out_specs=(vm, vm, vm),
        scratch_shapes=[pltpu.VMEM((t + 2 * h, D_POOL), F32)],
        name="pool_bwd", compiler_params=_params())(d_ms, mixed, pooled, pool_w, pool_scale)


SSM_ROWS = 2 * SSM_GROUPS * SSM_GROUP
SSM_HALF = SSM_GROUPS * SSM_GROUP


def _ssm_zoh(a_r, a_i, ldt):
    dt = jnp.exp(ldt)
    mag = jnp.exp(dt * a_r)
    ang = dt * a_i
    cs, sn = jnp.cos(ang), jnp.sin(ang)
    abr, abi = mag * cs, mag * sn
    den = a_r * a_r + a_i * a_i
    nr = abr - 1.0
    qr = (nr * a_r + abi * a_i) / den
    qi = (abi * a_r - nr * a_i) / den
    return dt, mag, cs, sn, abr, abi, den, nr, qr, qi


def _ssm_group_mask():
    row = lax.broadcasted_iota(jnp.int32, (SSM_HALF, SSM_CH), 0)
    col = lax.broadcasted_iota(jnp.int32, (SSM_HALF, SSM_CH), 1)
    return (row // SSM_GROUP) == (col // SSM_STATE)


def _ssm_prep(a_r, a_i, ldt, b_r, b_i, c_r, c_i, after=()):
    after_ops, after_specs = _after_operands(after)

    def body(ar_ref, ai_ref, ldt_ref, br_ref, bi_ref, cr_ref, ci_ref, *rest):
        abr_ref, abi_ref, win_ref, wint_ref, woutt_ref, wout_ref = rest[len(after_ops):]
        *_, abr, abi, _, _, qr, qi = _ssm_zoh(ar_ref[...], ai_ref[...], ldt_ref[...])
        abr_ref[...] = abr
        abi_ref[...] = abi
        b_r, b_i = br_ref[...], bi_ref[...]
        bbr = qr * b_r - qi * b_i
        bbi = qr * b_i + qi * b_r
        mask = _ssm_group_mask()
        state = lax.broadcasted_iota(jnp.int32, (SSM_STATE, SSM_CH), 0)
        col = lax.broadcasted_iota(jnp.int32, (SSM_STATE, SSM_CH), 1)
        every_group = (col % SSM_STATE == state).astype(BF16)

        def spread(x):
            return jnp.where(mask, _dot(x, every_group), 0.0)

        for d in range(2):
            rows = slice(d * SSM_HALF, (d + 1) * SSM_HALF)
            for half, x_in, x_out in ((0, bbr[rows], cr_ref[rows, :]), (1, bbi[rows], -ci_ref[rows, :])):
                cols = slice(half * SSM_CH, (half + 1) * SSM_CH)
                m_in, m_out = spread(x_in), spread(x_out)
                win_ref[d, :, cols] = m_in.astype(BF16)
                wint_ref[d, cols, :] = m_in.T.astype(BF16)
                woutt_ref[d, :, cols] = m_out.astype(BF16)
                wout_ref[d, cols, :] = m_out.T.astype(BF16)

    vm = BS(memory_space=pltpu.VMEM)
    vec = SDS((SSM_ROWS, SSM_STATE), F32)
    wide = SDS((2, SSM_HALF, 2 * SSM_CH), BF16)
    tall = SDS((2, 2 * SSM_CH, SSM_HALF), BF16)
    return pl.pallas_call(body, out_shape=(vec, vec, wide, tall, wide, tall), in_specs=[vm] * 7 + after_specs,
                          out_specs=(vm,) * 6, name="ssm_prep",
                          compiler_params=_params())(a_r, a_i, ldt, b_r, b_i, c_r, c_i, *after_ops)


def _ssm_prep_bwd(a_r, a_i, ldt, b_r, b_i, g_abr, g_abi, d_win, d_woutt):
    def body(ar_ref, ai_ref, ldt_ref, br_ref, bi_ref, gabr_ref, gabi_ref, dwin_ref, dwoutt_ref,
             dar_ref, dai_ref, dldt_ref, dbr_ref, dbi_ref, dcr_ref, dci_ref):
        a_r, a_i = ar_ref[...], ai_ref[...]
        dt, mag, cs, sn, abr, abi, den, nr, qr, qi = _ssm_zoh(a_r, a_i, ldt_ref[...])
        mask = _ssm_group_mask()
        col = lax.broadcasted_iota(jnp.int32, (SSM_CH, SSM_STATE), 0)
        state = lax.broadcasted_iota(jnp.int32, (SSM_CH, SSM_STATE), 1)
        own_state = (col % SSM_STATE == state).astype(BF16)

        def pick(dense):
            m = jnp.where(mask, dense, 0.0)
            hi = m.astype(BF16)
            lo = m - hi.astype(F32)
            return _dot(hi, own_state) + _dot(lo, own_state)

        def picked(ref, half):
            cols = slice(half * SSM_CH, (half + 1) * SSM_CH)
            return jnp.concatenate([pick(ref[d, :, cols]) for d in range(2)], axis=0)

        g_r, g_i = picked(dwin_ref, 0), picked(dwin_ref, 1)
        dcr_ref[...] = picked(dwoutt_ref, 0)
        dci_ref[...] = -picked(dwoutt_ref, 1)
        b_r, b_i = br_ref[...], bi_ref[...]
        dbr_ref[...] = g_r * qr + g_i * qi
        dbi_ref[...] = g_i * qr - g_r * qi
        gqr = g_r * b_r + g_i * b_i
        gqi = g_i * b_r - g_r * b_i
        g_nr_num = gqr / den
        g_ni_num = gqi / den
        g_den = -(gqr * qr + gqi * qi) / den
        g_nr = g_nr_num * a_r - g_ni_num * a_i
        g_abi = g_nr_num * a_i + g_ni_num * a_r
        d_ar = g_nr_num * nr + g_ni_num * abi + 2.0 * a_r * g_den
        d_ai = g_nr_num * abi - g_ni_num * nr + 2.0 * a_i * g_den
        g_abr = gabr_ref[...] + g_nr
        g_abi = gabi_ref[...] + g_abi
        g_mag = g_abr * cs + g_abi * sn
        g_ang = mag * (g_abi * cs - g_abr * sn)
        g_e = g_mag * mag
        d_ar = d_ar + g_e * dt
        d_ai = d_ai + g_ang * dt
        g_dt = g_e * a_r + g_ang * a_i
        dar_ref[...] = d_ar
        dai_ref[...] = d_ai
        dldt_ref[...] = g_dt * dt

    vm = BS(memory_space=pltpu.VMEM)
    vec = SDS((SSM_ROWS, SSM_STATE), F32)
    return pl.pallas_call(body, out_shape=(vec,) * 7, in_specs=[vm] * 9, out_specs=(vm,) * 7, name="ssm_prep_bwd",
                          compiler_params=_params())(a_r, a_i, ldt, b_r, b_i, g_abr, g_abi, d_win, d_woutt)


SCAN_ROWS = 512
SCAN_SUB = 128


def _ssm_scan(name, inp, w1, a_r, a_i, w2, reverse):
    t = inp.shape[0]
    rows = min(SCAN_ROWS, t)
    n = t // rows
    n_sub = rows // SCAN_SUB
    ch = SSM_CH
    at = (lambda i: (n - 1 - i, 0)) if reverse else (lambda i: (i, 0))

    def body(in_ref, w1_ref, ar_ref, ai_ref, w2_ref, sb_ref, out_ref, cr_ref, ci_ref, k_ref, st_ref):
        i = pl.program_id(0)

        @pl.when(i == 0)
        def _():
            ar8 = jnp.broadcast_to(ar_ref[...], (8, ch))
            ai8 = jnp.broadcast_to(ai_ref[...], (8, ch))
            row = lax.broadcasted_iota(jnp.int32, (8, ch), 0)
            rank = (7 - row) if reverse else row
            powers = [(ar8, ai8)]
            for _ in range(7):
                p_r, p_i = powers[-1]
                powers.append((p_r * ar8 - p_i * ai8, p_r * ai8 + p_i * ar8))
            zero = jnp.zeros((8, ch), F32)
            for slot, k in enumerate((1, 2, 4)):
                k_ref[2 * slot] = jnp.where(rank >= k, powers[k - 1][0], zero)
                k_ref[2 * slot + 1] = jnp.where(rank >= k, powers[k - 1][1], zero)
            carry_r, carry_i = zero, zero
            for j in range(8):
                carry_r = jnp.where(rank == j, powers[j][0], carry_r)
                carry_i = jnp.where(rank == j, powers[j][1], carry_i)
            k_ref[6] = carry_r
            k_ref[7] = carry_i
            cr_ref[...] = zero
            ci_ref[...] = zero

        def group(r0, carry):
            c_r, c_i = carry
            x_r = st_ref[pl.ds(r0, 8), 0:ch]
            x_i = st_ref[pl.ds(r0, 8), ch:2 * ch]
            for slot, k in enumerate((1, 2, 4)):
                shift = (8 - k) if reverse else k
                s_r = pltpu.roll(x_r, shift, 0)
                s_i = pltpu.roll(x_i, shift, 0)
                m_r, m_i = k_ref[2 * slot], k_ref[2 * slot + 1]
                x_r, x_i = x_r + m_r * s_r - m_i * s_i, x_i + m_r * s_i + m_i * s_r
            p_r, p_i = k_ref[6], k_ref[7]
            x_r, x_i = x_r + p_r * c_r - p_i * c_i, x_i + p_r * c_i + p_i * c_r
            st_ref[pl.ds(r0, 8), 0:ch] = x_r
            st_ref[pl.ds(r0, 8), ch:2 * ch] = x_i
            last = 0 if reverse else 7
            return (jnp.broadcast_to(x_r[last:last + 1, :], (8, ch)), jnp.broadcast_to(x_i[last:last + 1, :], (8, ch)))

        carry = (cr_ref[...], ci_ref[...])
        for sc in (range(n_sub - 1, -1, -1) if reverse else range(n_sub)):
            part = pl.ds(sc * SCAN_SUB, SCAN_SUB)
            st_ref[part, :] = _dot(in_ref[part, :], w1_ref[...])
            for gi in range(SCAN_SUB // 8):
                g = (SCAN_SUB // 8 - 1 - gi) if reverse else gi
                carry = group(sc * SCAN_SUB + g * 8, carry)
            states = st_ref[part, :].astype(BF16)
            sb_ref[part, :] = states
            out_ref[part, :] = _dot(states, w2_ref[...])
        cr_ref[...] = carry[0]
        ci_ref[...] = carry[1]

    return pl.pallas_call(
        body, out_shape=(SDS((t, 2 * ch), BF16), SDS((t, D_SSM), F32)), grid=(n,),
        in_specs=[BS((rows, D_SSM), at), BS((D_SSM, 2 * ch), lambda i: (0, 0)), BS((1, ch), lambda i: (0, 0)),
                  BS((1, ch), lambda i: (0, 0)), BS((2 * ch, D_SSM), lambda i: (0, 0))],
        out_specs=(BS((rows, 2 * ch), at), BS((rows, D_SSM), at)),
        scratch_shapes=[pltpu.VMEM((8, ch), F32), pltpu.VMEM((8, ch), F32), pltpu.VMEM((8, 8, ch), F32),
                        pltpu.VMEM((rows, 2 * ch), F32)],
        name=name, compiler_params=_params("arbitrary"))(inp, w1, a_r, a_i, w2)


DA_ROWS = 1024


def _ssm_param_grads(name, lam, states, u, dy, reverse, after=()):
    t = lam.shape[0]
    rows = min(DA_ROWS, t)
    n = t // rows
    halo_rows = 16
    nb = rows // halo_rows
    ch = SSM_CH
    if reverse:
        halo_at = lambda i: (jnp.minimum((i + 1) * nb, t // halo_rows - 1), 0)
    else:
        halo_at = lambda i: (jnp.maximum(i * nb - 1, 0), 0)

    after_ops, after_specs = _after_operands(after)

    def body(lam_ref, x_ref, halo_ref, u_ref, dy_ref, *rest):
        dr_ref, di_ref, dwin_ref, dwoutt_ref = rest[len(after_ops):]
        i = pl.program_id(0)

        @pl.when(i == 0)
        def _():
            dr_ref[...] = jnp.zeros_like(dr_ref)
            di_ref[...] = jnp.zeros_like(di_ref)
            dwin_ref[...] = jnp.zeros_like(dwin_ref)
            dwoutt_ref[...] = jnp.zeros_like(dwoutt_ref)

        dwin_ref[...] += _dot(u_ref[...], lam_ref[...], TN)
        dwoutt_ref[...] += _dot(dy_ref[...], x_ref[...], TN)
        row = lax.broadcasted_iota(jnp.int32, (rows, ch), 0)
        if reverse:
            edge, shift, h_row, live = rows - 1, rows - 1, 0, i < n - 1
        else:
            edge, shift, h_row, live = 0, 1, halo_rows - 1, i > 0

        def neighbour(lo):
            halo = halo_ref[:, lo:lo + ch].astype(F32)[h_row:h_row + 1]
            halo = jnp.where(live, halo, 0.0)
            x = x_ref[:, lo:lo + ch].astype(F32)
            return jnp.where(row == edge, jnp.broadcast_to(halo, (rows, ch)), pltpu.roll(x, shift, 0))

        xp_r, xp_i = neighbour(0), neighbour(ch)
        l_r, l_i = lam_ref[:, 0:ch].astype(F32), lam_ref[:, ch:2 * ch].astype(F32)
        dr_ref[...] += jnp.sum(l_r * xp_r + l_i * xp_i, axis=0, keepdims=True)
        di_ref[...] += jnp.sum(l_i * xp_r - l_r * xp_i, axis=0, keepdims=True)

    blk = BS((rows, 2 * ch), lambda i: (i, 0))
    thin = BS((rows, D_SSM), lambda i: (i, 0))
    vec = BS((1, ch), lambda i: (0, 0))
    mat = BS((D_SSM, 2 * ch), lambda i: (0, 0))
    return pl.pallas_call(
        body, out_shape=(SDS((1, ch), F32), SDS((1, ch), F32), SDS((D_SSM, 2 * ch), F32), SDS((D_SSM, 2 * ch), F32)),
        grid=(n,), in_specs=[blk, blk, BS((halo_rows, 2 * ch), halo_at), thin, thin] + after_specs,
        out_specs=(vec, vec, mat, mat),
        name=name, compiler_params=_params("arbitrary"))(lam, states, states, u, dy, *after_ops)


GELU_C = math.sqrt(2.0 / math.pi)
GELU_K = 0.044715


def _ssm_combine(proj, y_fwd, y_bwd, d_skip, tm, after=()):
    t = proj.shape[0]
    after_ops, after_specs = _after_operands(after)

    def body(s_ref, yf_ref, yb_ref, d_ref, *rest):
        yt_ref, g_ref = rest[len(after_ops):]
        y = s_ref[...] * d_ref[...] + yf_ref[...] + yb_ref[...]
        yt_ref[...] = y
        th = jnp.tanh(GELU_C * (y + GELU_K * y * y * y))
        g_ref[...] = (0.5 * y * (1.0 + th)).astype(BF16)

    blk = BS((tm, D_SSM), lambda i: (i, 0))
    return pl.pallas_call(
        body, out_shape=(SDS((t, D_SSM), F32), SDS((t, D_SSM), BF16)), grid=(t // tm,),
        in_specs=[BS((tm, D_SSM), lambda i: (i, D_POOL // D_SSM)), blk, blk, BS((1, D_SSM), lambda i: (0, 0))] + after_specs,
        out_specs=(blk, blk), name="ssm_combine",
        compiler_params=_params("parallel"))(proj, y_fwd, y_bwd, d_skip, *after_ops)


def _ssm_ds(proj, d_yt, du_fwd, du_bwd, d_skip, tm):
    t = proj.shape[0]

    def body(s_ref, dy_ref, duf_ref, dub_ref, d_ref, ds_ref, dd_ref):
        i = pl.program_id(0)
        dy = dy_ref[...]
        ds_ref[...] = (dy * d_ref[...] + duf_ref[...] + dub_ref[...]).astype(BF16)

        @pl.when(i == 0)
        def _():
            dd_ref[...] = jnp.zeros_like(dd_ref)

        dd_ref[...] += jnp.sum(dy * s_ref[...], axis=0, keepdims=True)

    blk = BS((tm, D_SSM), lambda i: (i, 0))
    vec = BS((1, D_SSM), lambda i: (0, 0))
    return pl.pallas_call(
        body, out_shape=(SDS((t, D_SSM), BF16), SDS((1, D_SSM), F32)), grid=(t // tm,),
        in_specs=[BS((tm, D_SSM), lambda i: (i, D_POOL // D_SSM)), blk, blk, blk, vec],
        out_specs=(blk, vec), name="ssm_ds", compiler_params=_params("arbitrary"))(proj, d_yt, du_fwd, du_bwd, d_skip)


G_POOL_AT = D_POOL + D_SSM
G_SSM_AT = G_POOL_AT + D_MODEL
E_VAL, E_GATE = D_POOL, D_POOL + D_SSM


def _merge_specs(tm):
    return [BS((tm, D_POOL), lambda i: (i, 0)), BS((tm, D_SSM), lambda i: (i, 0)),
            BS((N_SHARD, 1024, 256), lambda i: (0, 0, 0)), BS((tm, D_FF), lambda i: (i, 0))]


def _merge_parts(s, ms, yv, w_ref, proj_ref):
    lo = 256 * s
    zp = _dot(ms, w_ref[s, 0:E_VAL, :])
    zv = _dot(yv, w_ref[s, E_VAL:E_GATE, :])
    zg = _dot(yv, w_ref[s, E_GATE:, :])
    return zp, zv, zg, proj_ref[:, G_POOL_AT + lo:G_POOL_AT + lo + 256], proj_ref[:, G_SSM_AT + lo:G_SSM_AT + lo + 256]


def _mixer_merge(ms, yssm, w_e, proj, tm):
    t = ms.shape[0]

    def body(ms_ref, y_ref, w_ref, proj_ref, o_ref):
        msv, yv = ms_ref[...], y_ref[...]
        for s in range(N_SHARD):
            zp, zv, zg, gp, gs = _merge_parts(s, msv, yv, w_ref, proj_ref)
            o_ref[:, 256 * s:256 * (s + 1)] = (_sigmoid(gp) * zp + _sigmoid(gs) * zv * _sigmoid(zg)).astype(BF16)

    row = BS((tm, D_MODEL), lambda i: (i, 0))
    return pl.pallas_call(
        body, out_shape=SDS((t, D_MODEL), BF16), grid=(t // tm,), in_specs=_merge_specs(tm), out_specs=row,
        name="mixer_merge", compiler_params=_params("parallel"))(ms, yssm, w_e, proj)


def _mixer_merge_bwd(ms, yssm, w_e, proj, dmerged, tm):
    t = ms.shape[0]

    def body(ms_ref, y_ref, w_ref, proj_ref, dm_ref, dgp_ref, dgs_ref, dzp_ref, dzv_ref, dzg_ref):
        msv, yv = ms_ref[...], y_ref[...]
        for s in range(N_SHARD):
            cols = slice(256 * s, 256 * (s + 1))
            zp, zv, zg, gp, gs = _merge_parts(s, msv, yv, w_ref, proj_ref)
            dm = dm_ref[:, cols].astype(F32)
            sp, ss, sg = _sigmoid(gp), _sigmoid(gs), _sigmoid(zg)
            dgp_ref[:, cols] = (dm * zp * sp * (1.0 - sp)).astype(BF16)
            dgs_ref[:, cols] = (dm * zv * sg * ss * (1.0 - ss)).astype(BF16)
            dzp_ref[:, cols] = (dm * sp).astype(BF16)
            dz = dm * ss
            dzv_ref[:, cols] = (dz * sg).astype(BF16)
            dzg_ref[:, cols] = (dz * zv * sg * (1.0 - sg)).astype(BF16)

    row = BS((tm, D_MODEL), lambda i: (i, 0))
    shape = SDS((t, D_MODEL), BF16)
    return pl.pallas_call(
        body, out_shape=(shape,) * 5, grid=(t // tm,), in_specs=_merge_specs(tm) + [row],
        out_specs=(row,) * 5, name="mixer_merge_bwd",
        compiler_params=_params("parallel"))(ms, yssm, w_e, proj, dmerged)


def _mixer_dw(ms, yssm, dzp, dzv, dzg, tm):
    t = ms.shape[0]
    tm = min(2 * tm, t)
    n_t = t // tm

    def body(ms_ref, y_ref, dzp_ref, dzv_ref, dzg_ref, o_ref, acc):
        i = pl.program_id(0)

        @pl.when(i == 0)
        def _():
            acc[...] = jnp.zeros_like(acc)

        msv, yv = ms_ref[...], y_ref[...]
        for s in range(N_SHARD):
            cols = slice(256 * s, 256 * (s + 1))
            acc[s, 0:E_VAL, :] += _dot(msv, dzp_ref[:, cols], TN)
            acc[s, E_VAL:E_GATE, :] += _dot(yv, dzv_ref[:, cols], TN)
            acc[s, E_GATE:, :] += _dot(yv, dzg_ref[:, cols], TN)

        @pl.when(i == n_t - 1)
        def _():
            o_ref[...] = acc[...].astype(BF16)

    row = BS((tm, D_MODEL), lambda i: (i, 0))
    full = BS((N_SHARD, 1024, 256), lambda i: (0, 0, 0))
    return pl.pallas_call(
        body, out_shape=SDS((N_SHARD, 1024, 256), BF16), grid=(n_t,),
        in_specs=[BS((tm, D_POOL), lambda i: (i, 0)), BS((tm, D_SSM), lambda i: (i, 0)), row, row, row],
        out_specs=full, scratch_shapes=[pltpu.VMEM((N_SHARD, 1024, 256), F32)],
        name="mixer_dw", compiler_params=_params("arbitrary"))(ms, yssm, dzp, dzv, dzg)


def _mixer_dx(dzp, dzv, dzg, w_e, y_total, tm):
    t = dzp.shape[0]

    def body(dzp_ref, dzv_ref, dzg_ref, w_ref, yt_ref, dms_ref, dy_ref):
        acc_ms, acc_y = None, None
        for s in range(N_SHARD):
            cols = slice(256 * s, 256 * (s + 1))
            part_ms = _dot(dzp_ref[:, cols], w_ref[s, 0:E_VAL, :], NT)
            part_y = _dot(dzv_ref[:, cols], w_ref[s, E_VAL:E_GATE, :], NT) + _dot(dzg_ref[:, cols], w_ref[s, E_GATE:, :], NT)
            acc_ms = part_ms if s == 0 else acc_ms + part_ms
            acc_y = part_y if s == 0 else acc_y + part_y
        dms_ref[...] = acc_ms.astype(BF16)
        y = yt_ref[...]
        th = jnp.tanh(GELU_C * (y + GELU_K * y * y * y))
        dgelu = 0.5 * (1.0 + th) + 0.5 * y * (1.0 - th * th) * GELU_C * (1.0 + 3.0 * GELU_K * y * y)
        dy_ref[...] = acc_y * dgelu

    row = BS((tm, D_MODEL), lambda i: (i, 0))
    return pl.pallas_call(
        body, out_shape=(SDS((t, D_POOL), BF16), SDS((t, D_SSM), F32)), grid=(t // tm,),
        in_specs=[row, row, row, BS((N_SHARD, 1024, 256), lambda i: (0, 0, 0)), BS((tm, D_SSM), lambda i: (i, 0))],
        out_specs=(BS((tm, D_POOL), lambda i: (i, 0)), BS((tm, D_SSM), lambda i: (i, 0))),
        name="mixer_dx", compiler_params=_params("parallel"))(dzp, dzv, dzg, w_e, y_total)


def _attn_probs(q_h, k_h):
    s = _dot(q_h, k_h, NT) * (1.0 / math.sqrt(HEAD_DIM))
    e = jnp.exp(s - jnp.max(s, axis=-1, keepdims=True))
    return e / jnp.sum(e, axis=-1, keepdims=True)


def _attn_fwd(q, kv, tm):
    t = q.shape[0]
    m = kv.shape[0]

    def body(q_ref, kv_ref, o_ref):
        for hd in range(N_HEADS):
            lo = hd * HEAD_DIM
            p = _attn_probs(q_ref[:, lo:lo + HEAD_DIM], kv_ref[:, lo:lo + HEAD_DIM])
            o_ref[:, lo:lo + HEAD_DIM] = _dot(p, kv_ref[:, D_MODEL + lo:D_MODEL + lo + HEAD_DIM]).astype(BF16)

    return pl.pallas_call(
        body, out_shape=SDS((t, D_MODEL), BF16), grid=(t // tm,),
        in_specs=[BS((tm, D_MODEL), lambda i: (i, 0)), BS((m, 2 * D_MODEL), lambda i: (0, 0))],
        out_specs=BS((tm, D_MODEL), lambda i: (i, 0)), name="attn_fwd", compiler_params=_params("parallel"))(q, kv)


def _attn_bwd(q, kv, d_o, tm):
    t = q.shape[0]
    m = kv.shape[0]

    def body(q_ref, kv_ref, do_ref, dq_ref, dkv_ref):
        i = pl.program_id(0)

        @pl.when(i == 0)
        def _():
            dkv_ref[...] = jnp.zeros_like(dkv_ref)

        for hd in range(N_HEADS):
            lo = hd * HEAD_DIM
            q_h = q_ref[:, lo:lo + HEAD_DIM]
            k_h = kv_ref[:, lo:lo + HEAD_DIM]
            v_h = kv_ref[:, D_MODEL + lo:D_MODEL + lo + HEAD_DIM]
            do_h = do_ref[:, lo:lo + HEAD_DIM]
            p = _attn_probs(q_h, k_h)
            dkv_ref[:, D_MODEL + lo:D_MODEL + lo + HEAD_DIM] += _dot(p, do_h, TN)
            dp = _dot(do_h, v_h, NT)
            ds = p * (dp - jnp.sum(dp * p, axis=-1, keepdims=True)) * (1.0 / math.sqrt(HEAD_DIM))
            dq_ref[:, lo:lo + HEAD_DIM] = _dot(ds, k_h).astype(BF16)
            dkv_ref[:, lo:lo + HEAD_DIM] += _dot(ds, q_h, TN)

    row = BS((tm, D_MODEL), lambda i: (i, 0))
    full = BS((m, 2 * D_MODEL), lambda i: (0, 0))
    return pl.pallas_call(
        body, out_shape=(SDS((t, D_MODEL), BF16), SDS((m, 2 * D_MODEL), F32)), grid=(t // tm,),
        in_specs=[row, full, row], out_specs=(row, full), name="attn_bwd",
        compiler_params=_params("arbitrary"))(q, kv, d_o)


TRANSPOSED = ("ffn1_w_gate", "ffn1_w_up", "ffn2_w_gate", "ffn2_w_up", "w_in")
GATHER_PHASES = {"f1a": (("ffn1_w_gate",), ("ffn1_w_up",)),
                 "f1b": (("ffn1_w_down",),),
                 "win": (("w_in",),),
                 "mix": (("w_mix_out", "w_q", "w_xo"), ("w_kv",), ("w_pool_proj", "w_glu_val", "w_glu_gate")),
                 "f2": (("ffn2_w_gate",), ("ffn2_w_up",), ("ffn2_w_down",))}
REDUCE_GROUPS = (("ffn2_w_gate",), ("ffn2_w_up",), ("ffn2_w_down",), ("w_xo",), ("w_q",), ("w_kv",), ("w_mix_out",),
                 ("w_pool_proj", "w_glu_val", "w_glu_gate"), ("w_in",), ("ffn1_w_gate",), ("ffn1_w_up",), ("ffn1_w_down",))
SMALL = ("ffn1_norm", "mix_norm", "pool_w", "pool_scale", "ssm_a_re", "ssm_a_im", "ssm_log_dt", "ssm_b_re",
         "ssm_b_im", "ssm_c_re", "ssm_c_im", "ssm_d", "xattn_norm", "mem_norm", "ffn2_norm", "final_norm")
WEIGHTS = ("ffn1_norm", "ffn1_w_gate", "ffn1_w_up", "ffn1_w_down", "mix_norm", "w_in", "pool_w", "pool_scale",
           "w_pool_proj", "ssm_a_re", "ssm_a_im", "ssm_log_dt", "ssm_b_re", "ssm_b_im", "ssm_c_re", "ssm_c_im",
           "ssm_d", "w_glu_val", "w_glu_gate", "w_mix_out", "xattn_norm", "mem_norm", "w_q", "w_kv", "w_xo",
           "ffn2_norm", "ffn2_w_gate", "ffn2_w_up", "ffn2_w_down", "final_norm")


def _small_view(a, n):
    return jnp.swapaxes(a, 3, 4) if n in ("ssm_b_re", "ssm_b_im") else a


def _device_step(x, mem, target, wts, sp, reducer=None):
    t = x.shape[0]
    tm = min(TM, t)
    g = {}

    first_gather = wts.start("f1a")
    u1 = _rmsnorm("norm_ffn1", x, sp["ffn1_norm"], tm, after=first_gather)

    def per_channel(a):
        a = a.reshape(2 * SSM_GROUPS, 1, -1)
        return jnp.broadcast_to(a, (2 * SSM_GROUPS, SSM_GROUP, a.shape[-1])).reshape(SSM_ROWS, a.shape[-1])

    ssm_a = per_channel(sp["ssm_a_re"]), per_channel(sp["ssm_a_im"]), per_channel(sp["ssm_log_dt"])
    ssm_b = sp["ssm_b_re"].reshape(SSM_ROWS, SSM_STATE), sp["ssm_b_im"].reshape(SSM_ROWS, SSM_STATE)
    abr, abi, w_in_s, w_in_s_t, w_out_s_t, w_out_s = _ssm_prep(
        *ssm_a, *ssm_b, sp["ssm_c_re"].reshape(SSM_ROWS, SSM_STATE), sp["ssm_c_im"].reshape(SSM_ROWS, SSM_STATE),
        after=first_gather)
    first_rows = (2, SSM_GROUPS, SSM_GROUP, SSM_STATE)
    a_r = abr.reshape(first_rows)[:, :, 0].reshape(2, 1, SSM_CH)
    a_i = abi.reshape(first_rows)[:, :, 0].reshape(2, 1, SSM_CH)
    mem_n = _rmsnorm("norm_mem", mem, sp["mem_norm"], mem.shape[0], after=first_gather)

    whole = (D_FF, D_MODEL)
    w_g1, w_u1 = wts.finish("f1a", [u1, w_in_s, w_in_s_t, w_out_s, w_out_s_t, a_r, a_i, mem_n])
    w_f1 = {"gate": w_g1.reshape(whole), "up": w_u1.reshape(whole)}
    down_gather = wts.start("f1b", [w_g1])
    g1, up1, a1 = _ffn_up("ffn1_up", u1, w_f1, tm, after=wts.start("win", down_gather))
    (w_dn,) = wts.finish("f1b", [a1])
    w_f1["down"] = w_dn.reshape(whole)
    h1, u2 = _ffn_down("ffn1_down", a1, w_f1, x, tm, next_gain=sp["mix_norm"])

    (w_in_g,) = wts.finish("win", [u2])
    w_in_t = w_in_g.reshape(D_FF, D_MODEL)
    proj = _mm("mix_in", [(u2, BS((tm, D_MODEL), lambda j, i: (i, 0)), w_in_t, BS((D_FF // 2, D_MODEL), lambda j, i: (j, 0)), NT)],
               grid=(2, t // tm), out_shape=SDS((t, D_FF), F32), out_spec=BS((tm, D_FF // 2), lambda j, i: (i, j)),
               after=wts.start("f2", wts.start("mix", [w_in_g])))
    pooled, mixed, ms = _pool_fwd(proj, sp["pool_w"][0], sp["pool_scale"])

    s_in = proj[:, D_POOL:D_POOL + D_SSM].astype(BF16)
    states, y_dirs = [], []
    for dr in range(2):
        st, yd = _ssm_scan(f"ssm_scan_fwd{dr}", s_in, w_in_s[dr], a_r[dr], a_i[dr], w_out_s[dr], reverse=(dr == 1))
        states.append(st)
        y_dirs.append(yd)
    w_sq, w_kv, w_e = wts.finish("mix", y_dirs)
    w_mo, w_q, w_xo = (w_sq[:, 256 * k:256 * (k + 1)].reshape(D_MODEL, D_MODEL) for k in range(3))
    w_d = w_kv[:, None]
    y_total, yssm = _ssm_combine(proj, y_dirs[0], y_dirs[1], sp["ssm_d"], tm)

    merged = _mixer_merge(ms, yssm, w_e, proj, tm)
    h2, u3 = _mm_resid_norm("mix_out", merged, w_mo, h1, sp["xattn_norm"], tm)

    q = _plain_mm("attn_q", u3, w_q, NN, BF16, tm)
    n_mem = mem.shape[0]
    kv = _mm("attn_kv", [(mem_n, BS((n_mem, D_MODEL), lambda s: (0, 0)), w_d, BS((None, None, D_MODEL, 512), lambda s: (s, 0, 0, 0)), NN)],
             grid=(N_SHARD,), out_shape=SDS((n_mem, 2 * D_MODEL), BF16), out_spec=BS((n_mem, 512), lambda s: (0, s)))
    o = _attn_fwd(q, kv, tm)
    h3, u4 = _mm_resid_norm("attn_out", o, w_xo, h2, sp["ffn2_norm"], tm)

    w_f2 = dict(zip(("gate", "up", "down"), (a.reshape(whole) for a in wts.finish("f2", [u4]))))
    g2, up2, a2 = _ffn_up("ffn2_up", u4, w_f2, tm)
    loss, dh4, dh4_b, g["final_norm"] = _ffn_down("ffn2_down", a2, w_f2, h3, tm,
                                                  head=(sp["final_norm"].reshape(1, D_MODEL), target))

    dg2, dup2 = _ffn_bwd_act("ffn2_bwd_act", dh4_b, w_f2, g2, up2, tm)
    dw_f2 = _ffn_dw("ffn2_dw", u4, dg2, dup2, a2, dh4_b, tm)
    dh3, dh3_b, g["ffn2_norm"] = _ffn_dx("ffn2_dx", dg2, dup2, w_f2, h3, sp["ffn2_norm"], dh4, tm)

    d_o = _plain_mm("attn_out_dx", dh3_b, w_xo, NT, BF16, tm)
    dw_xo = _dw_mm("attn_out_dw", o, dh3_b, tm)
    dq, dkv = _attn_bwd(q, kv, d_o, tm)
    dw_q = _dw_mm("attn_q_dw", u3, dq, tm)
    dh2, dh2_b, g["xattn_norm"] = _mm_norm_bwd("attn_q_dx", dq, w_q, NT, h2, sp["xattn_norm"], dh3, tm)
    dw_kv = _mm("attn_kv_dw", [(mem_n, BS((n_mem, D_MODEL), lambda s: (0, 0)), dkv, BS((n_mem, 512), lambda s: (0, s)), TN)],
                grid=(N_SHARD,), out_shape=SDS((N_SHARD, D_MODEL, 512), BF16), out_spec=BS((None, D_MODEL, 512), lambda s: (s, 0, 0)))
    dmem_n = _mm("attn_kv_dx", [(dkv, BS((n_mem, 512), lambda s: (0, s)), w_d, BS((None, None, D_MODEL, 512), lambda s: (s, 0, 0, 0)), NT)],
                 grid=(N_SHARD,), red_axis=0, out_shape=SDS((n_mem, D_MODEL), F32), out_spec=BS((n_mem, D_MODEL), lambda s: (0, 0)))
    _, _, g["mem_norm"] = _rmsnorm_bwd("norm_mem_bwd", mem, sp["mem_norm"], dmem_n, None, n_mem)

    square = (N_SHARD, D_MODEL // N_SHARD, D_MODEL)
    sharded = (N_SHARD, FF_SH, D_MODEL)
    early = [a.reshape(sharded) for a in dw_f2] + [dw_xo.reshape(square), dw_q.reshape(square), dw_kv]
    swapping = reducer.swap_start("a1", early) if reducer is not None else []
    dmerged = _plain_mm("mix_out_dx", dh2_b, w_mo, NT, BF16, tm, after=swapping)
    dw_mo = _dw_mm("mix_out_dw", merged, dh2_b, tm)
    d_gp, d_gs, dzp, dzv, dzg = _mixer_merge_bwd(ms, yssm, w_e, proj, dmerged, tm)
    dw_e = _mixer_dw(ms, yssm, dzp, dzv, dzg, tm)
    d_ms, d_yt = _mixer_dx(dzp, dzv, dzg, w_e, y_total, tm)
    dp, d_scale, d_pw = _pool_bwd(d_ms, mixed, pooled, sp["pool_w"][0], sp["pool_scale"])
    g["pool_scale"] = d_scale
    g["pool_w"] = d_pw[None]

    d_yt_b = d_yt.astype(BF16)
    du_dirs, lams = [], []
    for dr in range(2):
        lam, du = _ssm_scan(f"ssm_scan_bwd{dr}", d_yt_b, w_out_s_t[dr], a_r[dr], -a_i[dr], w_in_s_t[dr], reverse=(dr == 0))
        du_dirs.append(du)
        lams.append(lam)
    ds, g["ssm_d"] = _ssm_ds(proj, d_yt, du_dirs[0], du_dirs[1], sp["ssm_d"], tm)

    d_proj = jnp.concatenate([dp, ds, d_gp, d_gs], axis=1)
    tw = min(2 * tm, t)
    dw_in_t = _mm("mix_in_dw", [(d_proj, BS((tw, D_FF // 2), lambda j, i: (i, j)), u2, BS((tw, D_MODEL), lambda j, i: (i, 0)), TN)],
                  grid=(2, t // tw), red_axis=1, out_shape=SDS((D_FF, D_MODEL), BF16), out_spec=BS((D_FF // 2, D_MODEL), lambda j, i: (j, 0)))
    dh1, dh1_b, g["mix_norm"] = _mm_norm_bwd("mix_in_dx", d_proj, w_in_t, NN, h1, sp["mix_norm"], dh2, tm)

    early += [dw_mo.reshape(square), dw_e, dw_in_t.reshape(sharded)]
    g["final_norm"] = g["final_norm"].reshape(D_MODEL)

    travelling = reducer.start("a", early[6:], swapped=["a1"], after=list(g.values())) if reducer is not None else []
    d_abr, d_abi, d_cm, d_bm = [], [], [], []
    for dr in range(2):
        da_r, da_i, d_win, d_woutt = _ssm_param_grads(f"ssm_param_grads{dr}", lams[dr], states[dr], s_in, d_yt_b,
                                                      reverse=(dr == 1), after=travelling)
        d_abr.append(da_r)
        d_abi.append(da_i)
        d_bm.append(d_win)
        d_cm.append(d_woutt)

    def first_channel(da):
        da = jnp.stack(da).reshape(2, SSM_GROUPS, 1, SSM_STATE)
        return jnp.pad(da, ((0, 0), (0, 0), (0, SSM_GROUP - 1), (0, 0))).reshape(SSM_ROWS, SSM_STATE)

    d_ar, d_ai, d_ldt, d_br, d_bi, d_cr, d_ci = _ssm_prep_bwd(
        *ssm_a, *ssm_b, first_channel(d_abr), first_channel(d_abi), jnp.stack(d_bm), jnp.stack(d_cm))
    per_group = (2 * SSM_GROUPS, SSM_GROUP * SSM_STATE)
    g["ssm_a_re"] = d_ar.reshape(2 * SSM_GROUPS, SSM_GROUP, SSM_STATE).sum(axis=1).reshape(sp["ssm_a_re"].shape)
    g["ssm_a_im"] = d_ai.reshape(2 * SSM_GROUPS, SSM_GROUP, SSM_STATE).sum(axis=1).reshape(sp["ssm_a_im"].shape)
    g["ssm_log_dt"] = d_ldt.reshape(per_group).sum(axis=1).reshape(sp["ssm_log_dt"].shape)
    g["ssm_b_re"] = d_br.reshape(sp["ssm_b_re"].shape)
    g["ssm_b_im"] = d_bi.reshape(sp["ssm_b_im"].shape)
    g["ssm_c_re"] = d_cr.reshape(sp["ssm_c_re"].shape)
    g["ssm_c_im"] = d_ci.reshape(sp["ssm_c_im"].shape)
    if reducer is not None:
        travelling = travelling + [d_ar, d_br, d_cr]
    dg1, dup1 = _ffn_bwd_act("ffn1_bwd_act", dh1_b, w_f1, g1, up1, tm, after=travelling)
    dw_f1 = [a.reshape(sharded) for a in _ffn_dw("ffn1_dw", u1, dg1, dup1, a1, dh1_b, tm)]
    if reducer is not None:
        travelling = reducer.start("b", dw_f1, after=reducer.finish("a", dw_f1[:1]))
        travelling = travelling + reducer.join_start("a", after=travelling)
    grad_x, _, g["ffn1_norm"] = _ffn_dx("ffn1_dx", dg1, dup1, w_f1, x, sp["ffn1_norm"], dh1, tm, after=travelling)
    if reducer is not None:
        reducer.finish("b", [grad_x])
        reducer.join_finish("a", [grad_x])
    return loss, grad_x, early + dw_f1, g


def _mesh_place():
    x, y, c = lax.axis_index("x"), lax.axis_index("y"), lax.axis_index("c")
    chips = [(1 - x, y), (x, 1 - y), (1 - x, 1 - y)]
    return x, y, c, chips


def _remote(src, dst, send_sems, recv_sems, k, to):
    return pltpu.make_async_remote_copy(src_ref=src, dst_ref=dst, send_sem=send_sems.at[k], recv_sem=recv_sems.at[k],
                                        device_id=to, device_id_type=MESH)


def _sibling_swap_halves(tag, grads, after=()):
    n = len(grads)
    after_ops, after_specs = _after_operands(after)

    def body(*refs):
        ins, outs = refs[:n], refs[n + len(after_ops):2 * n + len(after_ops)]
        send_sems, recv_sems = refs[2 * n + len(after_ops):]
        x, y, c, _ = _mesh_place()
        sibling = (x, y, 1 - c)
        copies = []
        for k in range(n):
            half = grads[k].shape[1] // 2
            theirs = pl.ds(pl.multiple_of((1 - c) * half, 16), half)
            cp = _remote(ins[k].at[:, theirs, :], outs[k], send_sems, recv_sems, k, sibling)
            cp.start()
            copies.append(cp)
        for cp in copies:
            cp.wait_recv()
        for cp in copies:
            cp.wait_send()

    hbm = BS(memory_space=pl.ANY)
    return pl.pallas_call(
        body, out_shape=tuple(SDS((g.shape[0], g.shape[1] // 2, g.shape[2]), g.dtype) for g in grads),
        in_specs=[hbm] * n + after_specs, out_specs=(hbm,) * n,
        scratch_shapes=[pltpu.SemaphoreType.DMA((n,)), pltpu.SemaphoreType.DMA((n,))],
        name="reduce_sibling_send_" + tag, compiler_params=_params())(*grads, *after_ops)


def _row_tile(rows, cap=512):
    return max(r for r in range(16, cap + 1, 16) if rows % r == 0)


REDUCE_STEPS = 2


def _chip_presum(tag, grads, gots, c_idx):
    n = len(grads)
    halves = [g.shape[1] // 2 for g in grads]
    tiles = [(h // REDUCE_STEPS, g.shape[2]) for h, g in zip(halves, grads)]

    def body(c_ref, *refs):
        for k in range(n):
            refs[2 * n + k][...] = (refs[k][...].astype(F32) + refs[n + k][...].astype(F32)).astype(BF16)

    mine = [BS((None, None) + tile, lambda s, i, c_ref: (s, c_ref[0], i, 0)) for tile in tiles]
    plain = [BS((None,) + tile, lambda s, i, c_ref: (s, i, 0)) for tile in tiles]
    return list(pl.pallas_call(
        body, out_shape=tuple(SDS((g.shape[0], h, g.shape[2]), BF16) for g, h in zip(grads, halves)),
        grid_spec=pltpu.PrefetchScalarGridSpec(num_scalar_prefetch=1, grid=(N_SHARD, REDUCE_STEPS),
                                               in_specs=mine + plain, out_specs=plain),
        name="reduce_presum_" + tag, compiler_params=_params("parallel", "parallel"))(
            c_idx, *[g.reshape(g.shape[0], 2, h, g.shape[2]) for g, h in zip(grads, halves)], *gots))


HBM_SPEC = BS(memory_space=pltpu.HBM)
SEM_SPEC = BS(memory_space=pltpu.SEMAPHORE)
DATAFLOW = pltpu.SideEffectType.DATAFLOW_SIDE_EFFECTING


def _chip_exchange_copies(parts, lands, send_sems, recv_sems):
    _, _, c, chips = _mesh_place()
    return [_remote(parts[k].at[2 * px + py], lands[k].at[j], send_sems, recv_sems, 3 * k + j, (px, py, c))
            for k in range(len(parts)) for j, (px, py) in enumerate(chips)]


def _gather_copies(shards, lands, send_sems, recv_sems):
    x, y, c, chips = _mesh_place()
    return [_remote(shards[k], lands[k].at[2 * x + y], send_sems, recv_sems, 3 * k + j, (px, py, c))
            for k in range(len(shards)) for j, (px, py) in enumerate(chips)]


def _gather_half_copies(shards, lands, send_sems, recv_sems):
    x, y, c, chips = _mesh_place()
    out = []
    for k in range(len(shards)):
        half = shards[k].shape[0] // 2
        mine = pl.ds(pl.multiple_of(c * half, 16), half)
        for j, (px, py) in enumerate(chips):
            out.append(_remote(shards[k].at[mine, :], lands[k].at[2 * x + y, mine, :], send_sems, recv_sems,
                               3 * k + j, (px, py, c)))
    return out


def _sibling_fill(tag, lands):
    n = len(lands)

    def body(*refs):
        outs = refs[n:2 * n]
        send_sems, recv_sems = refs[2 * n:]
        x, y, c, chips = _mesh_place()
        copies = []
        for k in range(n):
            half = lands[k].shape[1] // 2
            mine = pl.ds(pl.multiple_of(c * half, 16), half)
            for j, (px, py) in enumerate(chips):
                blk = outs[k].at[2 * px + py, mine, :]
                copies.append(_remote(blk, blk, send_sems, recv_sems, 3 * k + j, (x, y, 1 - c)))
        for cp in copies:
            cp.start()
        for cp in copies:
            cp.wait_recv()
        for cp in copies:
            cp.wait_send()

    hbm = BS(memory_space=pl.ANY)
    return list(pl.pallas_call(
        body, out_shape=tuple(SDS(a.shape, a.dtype) for a in lands),
        in_specs=[hbm] * n, out_specs=(hbm,) * n, input_output_aliases={k: k for k in range(n)},
        scratch_shapes=[pltpu.SemaphoreType.DMA((3 * n,)), pltpu.SemaphoreType.DMA((3 * n,))],
        name="gather_fill_" + tag, compiler_params=_params())(*lands))


def _swap_copies(grads, lands, send_sems, recv_sems):
    x, y, c, _ = _mesh_place()
    out = []
    for k in range(len(grads)):
        half = grads[k].shape[1] // 2
        theirs = pl.ds(pl.multiple_of((1 - c) * half, 16), half)
        out.append(_remote(grads[k].at[:, theirs, :], lands[k], send_sems, recv_sems, k, (x, y, 1 - c)))
    return out


def _join_copies(fulls, same, send_sems, recv_sems):
    x, y, c, _ = _mesh_place()
    out = []
    for k in range(len(fulls)):
        half = fulls[k].shape[0] // 2
        mine = fulls[k].at[pl.ds(pl.multiple_of(c * half, 8), half), :]
        out.append(_remote(mine, mine, send_sems, recv_sems, k, (x, y, 1 - c)))
    return out


def _everyone_copies(packs, lands, send_sems, recv_sems):
    x, y, c, _ = _mesh_place()
    out = []
    for k in range(len(packs)):
        for j in range(N_DEV - 1):
            bx, by, bc = (j + 1) >> 2 & 1, (j + 1) >> 1 & 1, (j + 1) & 1
            peer = (x ^ bx, y ^ by, c ^ bc)
            out.append(_remote(packs[k], lands[k].at[4 * x + 2 * y + c], send_sems, recv_sems, (N_DEV - 1) * k + j, peer))
    return out


def _split_start(name, copies, sources, land_shapes, after=(), fanout=3):
    n = len(sources)
    n_land = len(land_shapes)
    m = n + n_land
    n_sems = fanout * n
    after_ops, after_specs = _after_operands(after)

    def body(*refs):
        ins = refs[:n]
        lands = refs[n:m] if n_land else ins
        send_sems, recv_sems = refs[m + len(after_ops)], refs[m + len(after_ops) + 1]
        token = refs[-1]
        for cp in copies(ins, lands, send_sems, recv_sems):
            cp.start()
        token[...] = jnp.zeros_like(token)

    lands = [pltpu.with_memory_space_constraint(lax.empty(s, d), pltpu.HBM) for s, d in land_shapes]
    sources = [pltpu.with_memory_space_constraint(p, pltpu.HBM) for p in sources]
    thru = [pltpu.HBM(a.shape, a.dtype) for a in sources + lands]
    out = pl.pallas_call(
        body, name=name,
        out_shape=(pltpu.SemaphoreType.DMA((n_sems,)), pltpu.SemaphoreType.DMA((n_sems,)), *thru, SDS((8, 128), F32)),
        in_specs=[HBM_SPEC] * m + after_specs,
        out_specs=(SEM_SPEC, SEM_SPEC, *[HBM_SPEC] * m, BS(memory_space=pltpu.VMEM)),
        input_output_aliases={i: 2 + i for i in range(m)},
        compiler_params=pltpu.CompilerParams(has_side_effects=DATAFLOW))(*sources, *lands, *after_ops)
    return out[0], out[1], list(out[2:2 + n]), list(out[2 + n:2 + m]), out[-1]


def _split_wait(name, copies, send_sems, recv_sems, sources, lands, after):
    n = len(sources)
    m = n + len(lands)
    after_ops, after_specs = _after_operands(after)

    def body(*refs):
        ins = refs[:n]
        zones = refs[n:m] if m > n else ins
        for cp in copies(ins, zones, refs[m], refs[m + 1]):
            cp.wait_send()
            cp.wait_recv()

    out = pl.pallas_call(
        body, name=name,
        out_shape=tuple(pltpu.HBM(a.shape, a.dtype) for a in sources + lands),
        in_specs=[HBM_SPEC] * m + [SEM_SPEC, SEM_SPEC] + after_specs, out_specs=(HBM_SPEC,) * m,
        input_output_aliases={i: i for i in range(m)},
        compiler_params=pltpu.CompilerParams(has_side_effects=DATAFLOW))(*sources, *lands, send_sems, recv_sems, *after_ops)
    return list(out[:n]), list(out[n:])


class _WeightGatherer:
    def __init__(self, shards):
        self.shards, self.open = shards, {}
        self.me = 2 * lax.axis_index("x") + lax.axis_index("y")

    HALVED = ("f1a", "mix")

    def start(self, tag, after=()):
        shapes = [((N_SHARD,) + s.shape, s.dtype) for s in self.shards[tag]]
        copies = _gather_half_copies if tag in self.HALVED else _gather_copies
        self.open[tag] = _split_start("gather_start_" + tag, copies, self.shards[tag], shapes, after)
        return [self.open[tag][-1]]

    def finish(self, tag, after):
        send_sems, recv_sems, shards, lands, _ = self.open.pop(tag)
        copies = _gather_half_copies if tag in self.HALVED else _gather_copies
        shards, lands = _split_wait("gather_wait_" + tag, copies, send_sems, recv_sems, shards, lands, after)
        if tag in self.HALVED:
            lands = _sibling_fill(tag, lands)
        return [lax.dynamic_update_slice(zone, s[None], (self.me, 0, 0)) for zone, s in zip(lands, shards)]


class _GradReducer:
    def __init__(self):
        self.c_idx = lax.axis_index("c").astype(jnp.int32).reshape(1)
        self.place = jnp.stack([2 * lax.axis_index("x") + lax.axis_index("y"), lax.axis_index("c")]).astype(jnp.int32)
        self.swaps, self.open, self.landed, self.joins, self.reduced = {}, {}, {}, {}, []

    def swap_start(self, tag, grads, after=()):
        shapes = [((g.shape[0], g.shape[1] // 2, g.shape[2]), g.dtype) for g in grads]
        self.swaps[tag] = _split_start("reduce_swap_start_" + tag, _swap_copies, grads, shapes, after, fanout=1)
        return [self.swaps[tag][-1]]

    def start(self, tag, grads, after=(), swapped=()):
        pairs = []
        for s in swapped:
            send_sems, recv_sems, early, lands, _ = self.swaps.pop(s)
            pairs += zip(*_split_wait("reduce_swap_wait_" + s, _swap_copies, send_sems, recv_sems, early, lands, grads[-1:]))
        pairs += zip(grads, _sibling_swap_halves(tag, grads, after))
        parts = _chip_presum(tag, [g for g, _ in pairs], [s for _, s in pairs], self.c_idx)
        shapes = [((3,) + p.shape[1:], p.dtype) for p in parts]
        self.open[tag] = _split_start("reduce_exchange_start_" + tag, _chip_exchange_copies, parts, shapes)
        return [self.open[tag][-1]]

    def finish(self, tag, after):
        send_sems, recv_sems, parts, lands, _ = self.open.pop(tag)
        self.landed[tag] = _split_wait("reduce_exchange_wait_" + tag, _chip_exchange_copies, send_sems, recv_sems, parts, lands, after)
        return self.landed[tag][1][:1]

    def _sums(self, tag, after=()):
        parts, landed = self.landed.pop(tag)
        return _chip_sum(tag, parts, landed, self.place, after)

    def join_start(self, tag, after=()):
        self.joins[tag] = _split_start("reduce_join_start_" + tag, _join_copies, self._sums(tag, after), [], fanout=1)
        return [self.joins[tag][-1]]

    def join_finish(self, tag, after):
        send_sems, recv_sems, fulls, _, _ = self.joins.pop(tag)
        self.reduced += _split_wait("reduce_join_wait_" + tag, _join_copies, send_sems, recv_sems, fulls, [], after)[0]

    def join(self, tag, after=()):
        self.reduced += _sibling_join_halves(self._sums(tag), after)


def _chip_sum(tag, parts, gots, place, after=()):
    n = len(parts)
    tiles = [(p.shape[1] // REDUCE_STEPS, p.shape[2]) for p in parts]
    after_ops, after_specs = _after_operands(after)

    def body(place_ref, *refs):
        outs = refs[2 * n + len(after_ops):]
        for k in range(n):
            acc = refs[k][...].astype(F32)
            for j in range(3):
                acc = acc + refs[n + k][j].astype(F32)
            outs[k][...] = acc

    return list(pl.pallas_call(
        body, out_shape=tuple(SDS((2 * p.shape[1], p.shape[2]), F32) for p in parts),
        grid_spec=pltpu.PrefetchScalarGridSpec(
            num_scalar_prefetch=1, grid=(REDUCE_STEPS,),
            in_specs=[BS((None,) + tile, lambda i, place_ref: (place_ref[0], i, 0)) for tile in tiles]
            + [BS((3,) + tile, lambda i, place_ref: (0, i, 0)) for tile in tiles] + after_specs,
            out_specs=[BS(tile, lambda i, place_ref: (place_ref[1] * REDUCE_STEPS + i, 0)) for tile in tiles]),
        name="reduce_sum_" + tag, compiler_params=_params("parallel"))(place, *parts, *gots, *after_ops))


def _sibling_join_halves(fulls, after=()):
    n = len(fulls)
    after_ops, after_specs = _after_operands(after)

    def body(*refs):
        outs = refs[n + len(after_ops):2 * n + len(after_ops)]
        send_sems, recv_sems = refs[2 * n + len(after_ops):]
        copies = _join_copies(outs, outs, send_sems, recv_sems)
        for cp in copies:
            cp.start()
        for cp in copies:
            cp.wait_recv()
        for cp in copies:
            cp.wait_send()

    hbm = BS(memory_space=pl.ANY)
    return list(pl.pallas_call(
        body, out_shape=tuple(SDS(f.shape, f.dtype) for f in fulls),
        in_specs=[hbm] * n + after_specs, out_specs=(hbm,) * n, input_output_aliases={k: k for k in range(n)},
        scratch_shapes=[pltpu.SemaphoreType.DMA((n,)), pltpu.SemaphoreType.DMA((n,))],
        name="reduce_sibling_join", compiler_params=_params())(*fulls, *after_ops))


N_DEV = 8


def _sum_devices(packs):
    _, rows, lanes = packs.shape

    def body(p_ref, o_ref):
        acc = p_ref[0]
        for dev in range(1, N_DEV):
            acc = acc + p_ref[dev]
        o_ref[...] = acc

    vm = BS(memory_space=pltpu.VMEM)
    return pl.pallas_call(body, out_shape=SDS((rows, lanes), F32), in_specs=[vm], out_specs=vm,
                          name="small_sum", compiler_params=_params())(packs)


def _adamw_refs(w_ref, g_ref, m_ref, v_ref, go_ref, d_ref, mo_ref, vo_ref):
    bc1 = 1.0 - ADAM_B1 ** ADAM_STEP
    bc2 = 1.0 - ADAM_B2 ** ADAM_STEP
    g = g_ref[...]
    m_new = ADAM_B1 * m_ref[...] + (1.0 - ADAM_B1) * g
    v_new = ADAM_B2 * v_ref[...] + (1.0 - ADAM_B2) * (g * g)
    go_ref[...] = g
    mo_ref[...] = m_new
    vo_ref[...] = v_new
    d_ref[...] = -ADAM_LR * ((m_new / bc1) / (jnp.sqrt(v_new / bc2) + ADAM_EPS) + ADAM_WD * w_ref[...])


def _adamw_small(ws, gs, ms, vs):
    n = len(ws)

    def body(*refs):
        for k in range(n):
            _adamw_refs(*[refs[j * n + k] for j in range(4)], *refs[4 * n + 4 * k:4 * n + 4 * k + 4])

    vm = BS(memory_space=pltpu.VMEM)
    outs = pl.pallas_call(
        body, out_shape=tuple(SDS(a.shape, F32) for a in ws for _ in range(4)), in_specs=[vm] * (4 * n),
        out_specs=(vm,) * (4 * n), name="adamw_small", compiler_params=_params())(*ws, *gs, *ms, *vs)
    return [outs[4 * k:4 * k + 4] for k in range(n)]


def _adamw(name, w, grad, row0, m, v, after=()):
    rows, cols = w.shape
    tr = rows if rows < 16 else _row_tile(rows, 352)
    after_ops, after_specs = _after_operands(after)

    def body(w_ref, g_ref, m_ref, v_ref, *rest):
        _adamw_refs(w_ref, g_ref, m_ref, v_ref, *rest[len(after_ops):])

    blk = BS((tr, cols), lambda i: (i, 0))
    shape = SDS((rows, cols), F32)
    return pl.pallas_call(
        body, out_shape=(shape,) * 4, grid=(rows // tr,),
        in_specs=[blk, BS((tr, cols), lambda i: (row0 // tr + i, 0)), blk, blk] + after_specs, out_specs=(blk,) * 4,
        name=name, compiler_params=_params("parallel"))(w, grad, m, v, *after_ops)


SMALL_LANES = 128


def _pack_small(parts):
    flat = jnp.concatenate([jnp.ravel(p) for p in parts])
    rows = -(-flat.shape[0] // (64 * SMALL_LANES)) * 64
    return jnp.pad(flat, (0, rows * SMALL_LANES - flat.shape[0])).reshape(rows, SMALL_LANES)


def _unpack_small(packed, like):
    flat = jnp.ravel(packed)
    out, at = [], 0
    for p in like:
        out.append(flat[at:at + p.size].reshape(p.shape))
        at += p.size
    return out


def kernel(x, mem, ffn1_norm, ffn1_w_gate, ffn1_w_up, ffn1_w_down, mix_norm, w_in, pool_w, pool_scale, w_pool_proj, ssm_a_re, ssm_a_im, ssm_log_dt, ssm_b_re, ssm_b_im, ssm_c_re, ssm_c_im, ssm_d, w_glu_val, w_glu_gate, w_mix_out, xattn_norm, mem_norm, w_q, w_kv, w_xo, ffn2_norm, ffn2_w_gate, ffn2_w_up, ffn2_w_down, final_norm, loss_target, m_ffn1_norm, m_ffn1_w_gate, m_ffn1_w_up, m_ffn1_w_down, m_mix_norm, m_w_in, m_pool_w, m_pool_scale, m_w_pool_proj, m_ssm_a_re, m_ssm_a_im, m_ssm_log_dt, m_ssm_b_re, m_ssm_b_im, m_ssm_c_re, m_ssm_c_im, m_ssm_d, m_w_glu_val, m_w_glu_gate, m_w_mix_out, m_xattn_norm, m_mem_norm, m_w_q, m_w_kv, m_w_xo, m_ffn2_norm, m_ffn2_w_gate, m_ffn2_w_up, m_ffn2_w_down, m_final_norm, v_ffn1_norm, v_ffn1_w_gate, v_ffn1_w_up, v_ffn1_w_down, v_mix_norm, v_w_in, v_pool_w, v_pool_scale, v_w_pool_proj, v_ssm_a_re, v_ssm_a_im, v_ssm_log_dt, v_ssm_b_re, v_ssm_b_im, v_ssm_c_re, v_ssm_c_im, v_ssm_d, v_w_glu_val, v_w_glu_gate, v_w_mix_out, v_xattn_norm, v_mem_norm, v_w_q, v_w_kv, v_w_xo, v_ffn2_norm, v_ffn2_w_gate, v_ffn2_w_up, v_ffn2_w_down, v_final_norm):
    given = dict(locals())
    w = {n: given[n] for n in WEIGHTS}
    m = {n: given["m_" + n] for n in WEIGHTS}
    v = {n: given["v_" + n] for n in WEIGHTS}

    def shard_view(a, n):
        return a[0].T if n in TRANSPOSED else a[0]

    def shard_unview(a, n):
        return (a.T if n in TRANSPOSED else a)[None]

    shards = {tag: [jnp.concatenate([shard_view(w[n], n).astype(BF16) for n in grp], axis=0) for grp in arrays]
              for tag, arrays in GATHER_PHASES.items()}
    reducer = _GradReducer()
    ws, ms, vs = ({n: _small_view(a[n], n) for n in SMALL} for a in (w, m, v))
    loss_part, grad_x, _, small = _device_step(x[0], mem[0], loss_target[0], _WeightGatherer(shards), ws, reducer)

    small_like = [ws[n] for n in SMALL] + [loss_part[0, :1]]
    pack = _pack_small([small[n] for n in SMALL] + [loss_part[0, :1]])
    everyone = _split_start("small_start", _everyone_copies, [pack], [((N_DEV,) + pack.shape, F32)], fanout=N_DEV - 1)
    reducer.join("b", after=everyone[-1:])

    grads, delta, new_m, new_v = {}, {}, {}, {}
    big_done = []
    for grp, red in zip(REDUCE_GROUPS, reducer.reduced):
        row0 = 0
        for n in grp:
            w_n = shard_view(w[n], n)
            outs = _adamw("adamw_" + n, w_n, red, row0, shard_view(m[n], n), shard_view(v[n], n), after=everyone[-1:])
            grads[n], delta[n], new_m[n], new_v[n] = (shard_unview(o, n) for o in outs)
            big_done.append(outs[1])
            row0 += w_n.shape[0]

    send_sems, recv_sems, packs, landed, _ = everyone
    packs, landed = _split_wait("small_wait", _everyone_copies, send_sems, recv_sems, packs, landed, big_done)
    mine = 4 * lax.axis_index("x") + 2 * lax.axis_index("y") + lax.axis_index("c")
    summed = _sum_devices(lax.dynamic_update_slice(landed[0], packs[0][None], (mine, 0, 0)))
    g_small = dict(zip(SMALL + ("loss",), _unpack_small(summed, small_like)))
    loss = g_small.pop("loss").reshape(())
    def two_d(a):
        return a.reshape(-1, a.shape[-1])

    updated = _adamw_small(*([two_d(a[n]) for n in SMALL] for a in (ws, g_small, ms, vs)))
    for n, outs in zip(SMALL, updated):
        grads[n], delta[n], new_m[n], new_v[n] = (_small_view(o.reshape(ws[n].shape), n) for o in outs)

    return (loss, grad_x[None], *[grads[n] for n in WEIGHTS], *[delta[n] for n in WEIGHTS],
            *[new_m[n] for n in WEIGHTS], *[new_v[n] for n in WEIGHTS])
```

```python
import functools
import math

import jax
import jax.numpy as jnp
from jax import lax
from jax.experimental import pallas as pl
from jax.experimental.pallas import tpu as pltpu

F32 = jnp.float32
BF16 = jnp.bfloat16
SDS = jax.ShapeDtypeStruct
BS = pl.BlockSpec
MESH = pl.DeviceIdType.MESH

D_MODEL = 1024
D_FF = 2816
N_SHARD = 4
FF_SH = D_FF // N_SHARD
D_POOL = 512
POOL_WINDOWS = (2, 4, 8, 16)
POOL_GROUP = 128
D_SSM = 256
SSM_GROUPS = 16
SSM_GROUP = 16
SSM_STATE = 64
SSM_CH = SSM_GROUPS * SSM_STATE
N_HEADS = 4
HEAD_DIM = 256
EPS = 1e-6
ADAM_LR, ADAM_B1, ADAM_B2, ADAM_EPS, ADAM_WD, ADAM_STEP = 0.001, 0.9, 0.999, 1e-08, 0.01, 10

VMEM_LIMIT_V7X = 58 * 1024 * 1024
TM = 512

NN = (((1,), (0,)), ((), ()))
NT = (((1,), (1,)), ((), ()))
TN = (((0,), (0,)), ((), ()))


def _params(*sem):
    return pltpu.CompilerParams(dimension_semantics=sem if sem else None, vmem_limit_bytes=VMEM_LIMIT_V7X)


def _dot(a, b, dims=NN):
    return lax.dot_general(a.astype(BF16), b.astype(BF16), dims, preferred_element_type=F32)


def _sigmoid(v):
    return pl.reciprocal(1.0 + jnp.exp(-v), approx=True)


def _block_dims(spec):
    return tuple(d for d in spec.block_shape if d is not None)


def _after_operands(after):
    return list(after), [BS(memory_space=pl.ANY)] * len(after)


def _mm(name, pairs, *, grid, out_shape, out_spec, red_axis=None, extras=(), epilogue=None, after=()):
    n_pairs, n_extra = len(pairs), len(extras)
    n_red = grid[red_axis] if red_axis is not None else 1
    dims = [p[4] for p in pairs]

    def body(*refs):
        ab = refs[:2 * n_pairs]
        ex = refs[2 * n_pairs:2 * n_pairs + n_extra]
        o_ref = refs[2 * n_pairs + n_extra + len(after)]

        def partial():
            acc = None
            for p in range(n_pairs):
                t = _dot(ab[2 * p][...], ab[2 * p + 1][...], dims[p])
                acc = t if acc is None else acc + t
            return acc

        def finish(acc):
            res = epilogue(acc, *[e[...] for e in ex]) if epilogue is not None else acc
            o_ref[...] = res.astype(o_ref.dtype)

        if n_red == 1:
            finish(partial())
        else:
            acc_ref = refs[-1]
            k = pl.program_id(red_axis)

            @pl.when(k == 0)
            def _():
                acc_ref[...] = jnp.zeros_like(acc_ref)

            acc_ref[...] += partial()

            @pl.when(k == n_red - 1)
            def _():
                finish(acc_ref[...])

    operands, in_specs = [], []
    for a, a_spec, b, b_spec, _ in pairs:
        operands += [a, b]
        in_specs += [a_spec, b_spec]
    for e, e_spec in extras:
        operands.append(e)
        in_specs.append(e_spec)
    after_ops, after_specs = _after_operands(after)
    operands += after_ops
    in_specs += after_specs
    scratch = [pltpu.VMEM(_block_dims(out_spec), F32)] if n_red > 1 else []
    sem = tuple("arbitrary" if ax == red_axis else "parallel" for ax in range(len(grid)))
    return pl.pallas_call(body, out_shape=out_shape, grid=grid, in_specs=in_specs, out_specs=out_spec,
                          scratch_shapes=scratch, name=name, compiler_params=_params(*sem))(*operands)


def _rmsnorm(name, h, gain, tm, after=()):
    t, d = h.shape
    after_ops, after_specs = _after_operands(after)

    def body(h_ref, g_ref, *rest):
        u_ref = rest[-1]
        hv = h_ref[...]
        r = lax.rsqrt(jnp.mean(hv * hv, axis=-1, keepdims=True) + EPS)
        u_ref[...] = ((hv * r) * g_ref[...]).astype(u_ref.dtype)

    return pl.pallas_call(
        body, out_shape=SDS((t, d), BF16), grid=(t // tm,),
        in_specs=[BS((tm, d), lambda i: (i, 0)), BS((1, d), lambda i: (0, 0))] + after_specs,
        out_specs=BS((tm, d), lambda i: (i, 0)), name=name, compiler_params=_params("parallel"))(h, gain, *after_ops)


def _rmsnorm_bwd(name, h, gain, du, dh_in, tm):
    t, d = h.shape
    has_in = dh_in is not None

    def body(*refs):
        if has_in:
            h_ref, g_ref, du_ref, dhin_ref, dh_ref, dhb_ref, dg_ref = refs
        else:
            h_ref, g_ref, du_ref, dh_ref, dhb_ref, dg_ref = refs
        i = pl.program_id(0)
        hv = h_ref[...]
        r = lax.rsqrt(jnp.mean(hv * hv, axis=-1, keepdims=True) + EPS)
        n = hv * r
        duv = du_ref[...].astype(F32)
        dn = duv * g_ref[...]
        dh = r * (dn - n * jnp.mean(dn * n, axis=-1, keepdims=True))
        if has_in:
            dh = dhin_ref[...] + dh
        dh_ref[...] = dh
        dhb_ref[...] = dh.astype(BF16)

        @pl.when(i == 0)
        def _():
            dg_ref[...] = jnp.zeros_like(dg_ref)

        dg_ref[...] += jnp.sum(duv * n, axis=0, keepdims=True)

    row = BS((tm, d), lambda i: (i, 0))
    vec = BS((1, d), lambda i: (0, 0))
    operands = [h, gain, du] + ([dh_in] if has_in else [])
    in_specs = [row, vec, row] + ([row] if has_in else [])
    return pl.pallas_call(
        body, out_shape=(SDS((t, d), F32), SDS((t, d), BF16), SDS((1, d), F32)), grid=(t // tm,),
        in_specs=in_specs, out_specs=(row, row, vec), name=name, compiler_params=_params("arbitrary"))(*operands)


def _loss_head_tile(i, hv, g_ref, t_ref, loss_ref, dh_ref, dhb_ref, dg_ref):
    g = g_ref[...]
    r = lax.rsqrt(jnp.mean(hv * hv, axis=-1, keepdims=True) + EPS)
    n = hv * r
    err = n * g - t_ref[...]
    dy = err * (1.0 / hv.shape[-1])
    dn = dy * g
    dh = r * (dn - n * jnp.mean(dn * n, axis=-1, keepdims=True))
    dh_ref[...] = dh
    dhb_ref[...] = dh.astype(BF16)

    @pl.when(i == 0)
    def _():
        dg_ref[...] = jnp.zeros_like(dg_ref)
        loss_ref[...] = jnp.zeros_like(loss_ref)

    dg_ref[...] += jnp.sum(dy * n, axis=0, keepdims=True)
    part = 0.5 * jnp.sum(jnp.mean(err * err, axis=-1, keepdims=True), axis=0, keepdims=True)
    loss_ref[...] += jnp.broadcast_to(part, loss_ref.shape)


def _norm_tile(h, g_ref, u_ref):
    r = lax.rsqrt(jnp.mean(h * h, axis=-1, keepdims=True) + EPS)
    u_ref[...] = ((h * r) * g_ref[...]).astype(u_ref.dtype)


FFN_BLOCK = D_FF // 2


def _ffn_up(name, u, w_f, tm, after=()):
    t, d = u.shape
    after_ops, after_specs = _after_operands(after)

    def body(u_ref, wg_ref, wu_ref, *rest):
        pg_ref, pu_ref, a_ref = rest[len(after_ops):]
        uv = u_ref[...]
        for lo in range(0, D_FF, FFN_BLOCK):
            cols = slice(lo, lo + FFN_BLOCK)
            g = _dot(uv, wg_ref[cols, :], NT)
            up = _dot(uv, wu_ref[cols, :], NT)
            sg = _sigmoid(g)
            silu = g * sg
            a_ref[:, cols] = (silu * up).astype(BF16)
            pu_ref[:, cols] = (0.5 * silu).astype(BF16)
            pg_ref[:, cols] = (0.5 * sg * (1.0 + g * (1.0 - sg)) * up).astype(BF16)

    hid = BS((tm, D_FF), lambda i: (i, 0))
    shape = SDS((t, D_FF), BF16)
    whole = BS((D_FF, d), lambda i: (0, 0))
    return pl.pallas_call(
        body, out_shape=(shape, shape, shape), grid=(t // tm,),
        in_specs=[BS((tm, d), lambda i: (i, 0)), whole, whole] + after_specs,
        out_specs=(hid, hid, hid), name=name,
        compiler_params=_params("parallel"))(u, w_f["gate"], w_f["up"], *after_ops)


def _ffn_down(name, a, w_f, resid, tm, next_gain=None, head=None):
    t, d = resid.shape
    row = BS((tm, d), lambda i: (i, 0))
    vec = BS((1, d), lambda i: (0, 0))

    def body(a_ref, w_ref, res_ref, *rest):
        h = res_ref[...] + 0.5 * _dot(a_ref[...], w_ref[...])
        if head is not None:
            _loss_head_tile(pl.program_id(0), h, *rest)
        else:
            g_ref, h_ref, u_ref = rest
            h_ref[...] = h
            _norm_tile(h, g_ref, u_ref)

    if head is not None:
        extra, extra_specs = list(head), [vec, row]
        out_shape = (SDS((1, 128), F32), SDS((t, d), F32), SDS((t, d), BF16), SDS((1, d), F32))
        out_specs = (BS((1, 128), lambda i: (0, 0)), row, row, vec)
    else:
        extra, extra_specs = [next_gain], [vec]
        out_shape = (SDS((t, d), F32), SDS((t, d), BF16))
        out_specs = (row, row)
    return pl.pallas_call(
        body, out_shape=out_shape, grid=(t // tm,),
        in_specs=[BS((tm, D_FF), lambda i: (i, 0)), BS((D_FF, d), lambda i: (0, 0)), row] + extra_specs,
        out_specs=out_specs, name=name,
        compiler_params=_params("arbitrary" if head is not None else "parallel"))(a, w_f["down"], resid, *extra)


def _mm_resid_norm(name, a, b, resid, next_gain, tm):
    t, d = resid.shape

    def body(a_ref, b_ref, res_ref, g_ref, h_ref, u_ref):
        h = res_ref[...] + _dot(a_ref[...], b_ref[...])
        h_ref[...] = h
        _norm_tile(h, g_ref, u_ref)

    row = BS((tm, d), lambda i: (i, 0))
    return pl.pallas_call(
        body, out_shape=(SDS((t, d), F32), SDS((t, d), BF16)), grid=(t // tm,),
        in_specs=[BS((tm, a.shape[1]), lambda i: (i, 0)), BS(b.shape, lambda i: (0, 0)), row, BS((1, d), lambda i: (0, 0))],
        out_specs=(row, row), name=name, compiler_params=_params("parallel"))(a, b, resid, next_gain)


def _ffn_bwd_act(name, dh_b, w_f, pg, pu, tm, after=()):
    t, d = dh_b.shape
    after_ops, after_specs = _after_operands(after)

    def body(dh_ref, wd_ref, pg_ref, pu_ref, *rest):
        dg_ref, dup_ref = rest[len(after_ops):]
        dh = dh_ref[...]
        for lo in range(0, D_FF, FFN_BLOCK):
            cols = slice(lo, lo + FFN_BLOCK)
            da = _dot(dh, wd_ref[cols, :], NT)
            dg_ref[:, cols] = (da * pg_ref[:, cols].astype(F32)).astype(BF16)
            dup_ref[:, cols] = (da * pu_ref[:, cols].astype(F32)).astype(BF16)

    hid = BS((tm, D_FF), lambda i: (i, 0))
    shape = SDS((t, D_FF), BF16)
    return pl.pallas_call(
        body, out_shape=(shape, shape), grid=(t // tm,),
        in_specs=[BS((tm, d), lambda i: (i, 0)), BS((D_FF, d), lambda i: (0, 0)), hid, hid] + after_specs,
        out_specs=(hid, hid), name=name,
        compiler_params=_params("parallel"))(dh_b, w_f["down"], pg, pu, *after_ops)


def _ffn_dw(name, u, dg, dup, a, dh_b, tm):
    t, d = u.shape
    n_t = t // tm

    def body(u_ref, dg_ref, dup_ref, a_ref, dh_ref, og_ref, ou_ref, od_ref, acc):
        i = pl.program_id(1)

        @pl.when(i == 0)
        def _():
            acc[...] = jnp.zeros_like(acc)

        uv = u_ref[...]
        acc[0] += _dot(dg_ref[...], uv, TN)
        acc[1] += _dot(dup_ref[...], uv, TN)
        acc[2] += _dot(a_ref[...], dh_ref[...], TN)

        @pl.when(i == n_t - 1)
        def _():
            og_ref[...] = acc[0].astype(BF16)
            ou_ref[...] = acc[1].astype(BF16)
            od_ref[...] = (0.5 * acc[2]).astype(BF16)

    hid = BS((tm, FFN_BLOCK), lambda j, i: (i, j))
    row = BS((tm, d), lambda j, i: (i, 0))
    out = BS((FFN_BLOCK, d), lambda j, i: (j, 0))
    shape = SDS((D_FF, d), BF16)
    return pl.pallas_call(
        body, out_shape=(shape, shape, shape), grid=(D_FF // FFN_BLOCK, n_t),
        in_specs=[row, hid, hid, hid, row], out_specs=(out, out, out),
        scratch_shapes=[pltpu.VMEM((3, FFN_BLOCK, d), F32)],
        name=name, compiler_params=_params("parallel", "arbitrary"))(u, dg, dup, a, dh_b)


def _norm_bwd_tile(i, du, h_ref, g_ref, dhin_ref, dh_ref, dhb_ref, dg_ref):
    hv = h_ref[...]
    r = lax.rsqrt(jnp.mean(hv * hv, axis=-1, keepdims=True) + EPS)
    n = hv * r
    dn = du * g_ref[...]
    dh = dhin_ref[...] + r * (dn - n * jnp.mean(dn * n, axis=-1, keepdims=True))
    dh_ref[...] = dh
    dhb_ref[...] = dh.astype(BF16)

    @pl.when(i == 0)
    def _():
        dg_ref[...] = jnp.zeros_like(dg_ref)

    dg_ref[...] += jnp.sum(du * n, axis=0, keepdims=True)


def _norm_bwd_specs(tm):
    row = BS((tm, D_MODEL), lambda i: (i, 0))
    vec = BS((1, D_MODEL), lambda i: (0, 0))
    return [row, vec, row], (row, row, vec)


def _norm_bwd_shapes(t):
    return SDS((t, D_MODEL), F32), SDS((t, D_MODEL), BF16), SDS((1, D_MODEL), F32)


def _ffn_dx(name, dg, dup, w_f, h, gain, dh_in, tm, after=()):
    t = dg.shape[0]
    tm = tm // 2
    after_ops, after_specs = _after_operands(after)

    def body(dg_ref, dup_ref, wg_ref, wu_ref, h_ref, g_ref, dhin_ref, *rest):
        du = _dot(dg_ref[...], wg_ref[...]) + _dot(dup_ref[...], wu_ref[...])
        _norm_bwd_tile(pl.program_id(0), du, h_ref, g_ref, dhin_ref, *rest[len(after_ops):])

    hid = BS((tm, D_FF), lambda i: (i, 0))
    whole = BS((D_FF, D_MODEL), lambda i: (0, 0))
    norm_in, norm_out = _norm_bwd_specs(tm)
    return pl.pallas_call(
        body, out_shape=_norm_bwd_shapes(t), grid=(t // tm,),
        in_specs=[hid, hid, whole, whole] + norm_in + after_specs, out_specs=norm_out, name=name,
        compiler_params=_params("arbitrary"))(dg, dup, w_f["gate"], w_f["up"], h, gain, dh_in, *after_ops)


def _mm_norm_bwd(name, a, b, dims, h, gain, dh_in, tm):
    t = a.shape[0]

    def body(a_ref, b_ref, h_ref, g_ref, dhin_ref, *outs):
        _norm_bwd_tile(pl.program_id(0), _dot(a_ref[...], b_ref[...], dims), h_ref, g_ref, dhin_ref, *outs)

    norm_in, norm_out = _norm_bwd_specs(tm)
    return pl.pallas_call(
        body, out_shape=_norm_bwd_shapes(t), grid=(t // tm,),
        in_specs=[BS((tm, a.shape[1]), lambda i: (i, 0)), BS(b.shape, lambda i: (0, 0))] + norm_in,
        out_specs=norm_out, name=name, compiler_params=_params("arbitrary"))(a, b, h, gain, dh_in)


def _plain_mm(name, a, b, dims, out_dtype, tm, resid=None, after=()):
    t = a.shape[0]
    tm = min(2 * tm, t)
    n = b.shape[1] if dims == NN else b.shape[0]
    extras = [(resid, BS((tm, n), lambda i: (i, 0)))] if resid is not None else []
    epi = (lambda acc, res: res + acc) if resid is not None else None
    return _mm(name, [(a, BS((tm, a.shape[1]), lambda i: (i, 0)), b, BS(b.shape, lambda i: (0, 0)), dims)],
               grid=(t // tm,), out_shape=SDS((t, n), out_dtype), out_spec=BS((tm, n), lambda i: (i, 0)),
               extras=extras, epilogue=epi, after=after)


def _dw_mm(name, a, b, tm, out_dtype=BF16, after=()):
    t, k = a.shape
    n = b.shape[1]
    tm = min(2 * tm, t)
    return _mm(name, [(a, BS((tm, k), lambda i: (i, 0)), b, BS((tm, n), lambda i: (i, 0)), TN)],
               grid=(t // tm,), red_axis=0, out_shape=SDS((k, n), out_dtype), out_spec=BS((k, n), lambda i: (0, 0)),
               after=after)


POOL_CHUNK = 256
POOL_HALO = 8


def _window_sum(v, width, lead):
    n = v.shape[0]
    s = v
    k = 1
    while k < width:
        s = s + pltpu.roll(s, n - k, 0)
        k *= 2
    return pltpu.roll(s, lead, 0) if lead else s


def _pool_count(base, left, right, t, shape):
    pos = base + lax.broadcasted_iota(jnp.int32, shape, 0)
    lo = jnp.maximum(pos - left, 0)
    hi = jnp.minimum(pos + right + 1, t)
    return (hi - lo).astype(F32)


def _pool_fwd(proj, pool_w, pool_scale):
    t = proj.shape[0]
    c, h = POOL_CHUNK, POOL_HALO
    n_chunks = t // c

    def body(proj_hbm, pw_ref, sc_ref, pooled_ref, mixed_ref, ms_ref, pad_ref, sem):
        cp = pltpu.make_async_copy(proj_hbm.at[:, pl.ds(0, D_POOL)], pad_ref.at[pl.ds(h, t), :], sem)
        cp.start()
        pad_ref[pl.ds(0, h), :] = jnp.zeros((h, D_POOL), F32)
        pad_ref[pl.ds(t + h, h), :] = jnp.zeros((h, D_POOL), F32)
        cp.wait()
        for g, width in enumerate(POOL_WINDOWS):
            left = width // 2
            right = width - 1 - left
            cols = slice(g * POOL_GROUP, (g + 1) * POOL_GROUP)
            wmat = pw_ref[g].astype(BF16)
            scale = sc_ref[:, cols]

            def chunk(ci, carry, left=left, right=right, width=width, cols=cols, wmat=wmat, scale=scale):
                base = pl.multiple_of(ci * c, c)
                v = pad_ref[pl.ds(base, c + 2 * h), cols]
                win = _window_sum(v, width, left)[h:h + c]
                cnt = _pool_count(base, left, right, t, (c, POOL_GROUP))
                pooled = (win / cnt - v[h:h + c]).astype(BF16)
                mixed = _dot(pooled, wmat)
                pooled_ref[pl.ds(base, c), cols] = pooled
                mixed_ref[pl.ds(base, c), cols] = mixed.astype(BF16)
                ms_ref[pl.ds(base, c), cols] = (mixed * scale).astype(BF16)
                return carry

            lax.fori_loop(0, n_chunks, chunk, 0)

    vm = BS(memory_space=pltpu.VMEM)
    shape = SDS((t, D_POOL), BF16)
    return pl.pallas_call(
        body, out_shape=(shape, shape, shape),
        in_specs=[BS(memory_space=pl.ANY), vm, vm], out_specs=(vm, vm, vm),
        scratch_shapes=[pltpu.VMEM((t + 2 * h, D_POOL), F32), pltpu.SemaphoreType.DMA],
        name="pool_fwd", compiler_params=_params())(proj, pool_w, pool_scale)


def _pool_bwd(d_ms, mixed, pooled, pool_w, pool_scale):
    t = d_ms.shape[0]
    c, h = POOL_CHUNK, POOL_HALO
    n_chunks = t // c

    def body(dms_ref, mixed_ref, pooled_ref, pw_ref, sc_ref, dp_ref, dsc_ref, dpw_ref, pad_ref):
        pad_ref[pl.ds(0, h), :] = jnp.zeros((h, D_POOL), F32)
        pad_ref[pl.ds(t + h, h), :] = jnp.zeros((h, D_POOL), F32)
        for g, width in enumerate(POOL_WINDOWS):
            left = width // 2
            right = width - 1 - left
            cols = slice(g * POOL_GROUP, (g + 1) * POOL_GROUP)
            wmat = pw_ref[g].astype(BF16)
            scale = sc_ref[:, cols]

            def first(ci, carry, left=left, right=right, cols=cols, wmat=wmat, scale=scale):
                dsc, dpw = carry
                base = pl.multiple_of(ci * c, c)
                dms = dms_ref[pl.ds(base, c), cols].astype(F32)
                dsc = dsc + jnp.sum(dms * mixed_ref[pl.ds(base, c), cols].astype(F32), axis=0, keepdims=True)
                dmix = (dms * scale).astype(BF16)
                dpw = dpw + _dot(pooled_ref[pl.ds(base, c), cols], dmix, TN)
                dpooled = _dot(dmix, wmat, NT)
                cnt = _pool_count(base, left, right, t, (c, POOL_GROUP))
                pad_ref[pl.ds(base + h, c), cols] = dpooled / cnt
                return dsc, dpw

            dsc, dpw = lax.fori_loop(0, n_chunks, first,
                                     (jnp.zeros((1, POOL_GROUP), F32), jnp.zeros((POOL_GROUP, POOL_GROUP), F32)))
            dsc_ref[:, cols] = dsc
            dpw_ref[g] = dpw

            def second(ci, carry, left=left, right=right, width=width, cols=cols):
                base = pl.multiple_of(ci * c, c)
                v = pad_ref[pl.ds(base, c + 2 * h), cols]
                win = _window_sum(v, width, right)[h:h + c]
                cnt = _pool_count(base, left, right, t, (c, POOL_GROUP))
                dp_ref[pl.ds(base, c), cols] = (win - v[h:h + c] * cnt).astype(BF16)
                return carry

            lax.fori_loop(0, n_chunks, second, 0)

    vm = BS(memory_space=pltpu.VMEM)
    return pl.pallas_call(
        body, out_shape=(SDS((t, D_POOL), BF16), SDS((1, D_POOL), F32), SDS((4, POOL_GROUP, POOL_GROUP), F32)),
        in_specs=[vm] * 5, out_specs=(vm, vm, vm),
        scratch_shapes=[pltpu.VMEM((t + 2 * h, D_POOL), F32)],
        name="pool_bwd", compiler_params=_params())(d_ms, mixed, pooled, pool_w, pool_scale)


SSM_ROWS = 2 * SSM_GROUPS * SSM_GROUP
SSM_HALF = SSM_GROUPS * SSM_GROUP


def _ssm_zoh(a_r, a_i, ldt):
    dt = jnp.exp(ldt)
    mag = jnp.exp(dt * a_r)
    ang = dt * a_i
    cs, sn = jnp.cos(ang), jnp.sin(ang)
    abr, abi = mag * cs, mag * sn
    den = a_r * a_r + a_i * a_i
    nr = abr - 1.0
    qr = (nr * a_r + abi * a_i) / den
    qi = (abi * a_r - nr * a_i) / den
    return dt, mag, cs, sn, abr, abi, den, nr, qr, qi


def _ssm_group_mask():
    row = lax.broadcasted_iota(jnp.int32, (SSM_HALF, SSM_CH), 0)
    col = lax.broadcasted_iota(jnp.int32, (SSM_HALF, SSM_CH), 1)
    return (row // SSM_GROUP) == (col // SSM_STATE)


def _ssm_prep(a_r, a_i, ldt, b_r, b_i, c_r, c_i, after=()):
    after_ops, after_specs = _after_operands(after)

    def body(ar_ref, ai_ref, ldt_ref, br_ref, bi_ref, cr_ref, ci_ref, *rest):
        abr_ref, abi_ref, win_ref, wint_ref, woutt_ref, wout_ref = rest[len(after_ops):]
        *_, abr, abi, _, _, qr, qi = _ssm_zoh(ar_ref[...], ai_ref[...], ldt_ref[...])
        abr_ref[...] = abr
        abi_ref[...] = abi
        b_r, b_i = br_ref[...], bi_ref[...]
        bbr = qr * b_r - qi * b_i
        bbi = qr * b_i + qi * b_r
        mask = _ssm_group_mask()
        state = lax.broadcasted_iota(jnp.int32, (SSM_STATE, SSM_CH), 0)
        col = lax.broadcasted_iota(jnp.int32, (SSM_STATE, SSM_CH), 1)
        every_group = (col % SSM_STATE == state).astype(BF16)

        def spread(x):
            return jnp.where(mask, _dot(x, every_group), 0.0)

        for d in range(2):
            rows = slice(d * SSM_HALF, (d + 1) * SSM_HALF)
            for half, x_in, x_out in ((0, bbr[rows], cr_ref[rows, :]), (1, bbi[rows], -ci_ref[rows, :])):
                cols = slice(half * SSM_CH, (half + 1) * SSM_CH)
                m_in, m_out = spread(x_in), spread(x_out)
                win_ref[d, :, cols] = m_in.astype(BF16)
                wint_ref[d, cols, :] = m_in.T.astype(BF16)
                woutt_ref[d, :, cols] = m_out.astype(BF16)
                wout_ref[d, cols, :] = m_out.T.astype(BF16)

    vm = BS(memory_space=pltpu.VMEM)
    vec = SDS((SSM_ROWS, SSM_STATE), F32)
    wide = SDS((2, SSM_HALF, 2 * SSM_CH), BF16)
    tall = SDS((2, 2 * SSM_CH, SSM_HALF), BF16)
    return pl.pallas_call(body, out_shape=(vec, vec, wide, tall, wide, tall), in_specs=[vm] * 7 + after_specs,
                          out_specs=(vm,) * 6, name="ssm_prep",
                          compiler_params=_params())(a_r, a_i, ldt, b_r, b_i, c_r, c_i, *after_ops)


def _ssm_prep_bwd(a_r, a_i, ldt, b_r, b_i, d_abr, d_abi, d_win, d_woutt):
    def body(ar_ref, ai_ref, ldt_ref, br_ref, bi_ref, *rest):
        (dabr_refs, dabi_refs, dwin_refs, dwoutt_refs), outs = [rest[2 * k:2 * k + 2] for k in range(4)], rest[8:]
        dar_ref, dai_ref, dldt_ref, dbr_ref, dbi_ref, dcr_ref, dci_ref = outs
        a_r, a_i = ar_ref[...], ai_ref[...]
        dt, mag, cs, sn, abr, abi, den, nr, qr, qi = _ssm_zoh(a_r, a_i, ldt_ref[...])
        mask = _ssm_group_mask()
        col = lax.broadcasted_iota(jnp.int32, (SSM_CH, SSM_STATE), 0)
        state = lax.broadcasted_iota(jnp.int32, (SSM_CH, SSM_STATE), 1)
        own_state = (col % SSM_STATE == state).astype(BF16)

        def pick(dense):
            m = jnp.where(mask, dense, 0.0)
            hi = m.astype(BF16)
            lo = m - hi.astype(F32)
            return _dot(hi, own_state) + _dot(lo, own_state)

        def picked(refs, half):
            cols = slice(half * SSM_CH, (half + 1) * SSM_CH)
            return jnp.concatenate([pick(ref[:, cols]) for ref in refs], axis=0)

        first_channel = lax.broadcasted_iota(jnp.int32, (SSM_HALF, SSM_CH), 0) % SSM_GROUP == 0

        def first_rows(refs):
            return jnp.concatenate(
                [pick(jnp.where(first_channel, jnp.broadcast_to(ref[...], (SSM_HALF, SSM_CH)), 0.0)) for ref in refs], axis=0)

        g_r, g_i = picked(dwin_refs, 0), picked(dwin_refs, 1)
        dcr_ref[...] = picked(dwoutt_refs, 0)
        dci_ref[...] = -picked(dwoutt_refs, 1)
        b_r, b_i = br_ref[...], bi_ref[...]
        dbr_ref[...] = g_r * qr + g_i * qi
        dbi_ref[...] = g_i * qr - g_r * qi
        gqr = g_r * b_r + g_i * b_i
        gqi = g_i * b_r - g_r * b_i
        g_nr_num = gqr / den
        g_ni_num = gqi / den
        g_den = -(gqr * qr + gqi * qi) / den
        g_nr = g_nr_num * a_r - g_ni_num * a_i
        g_abi = g_nr_num * a_i + g_ni_num * a_r
        d_ar = g_nr_num * nr + g_ni_num * abi + 2.0 * a_r * g_den
        d_ai = g_nr_num * abi - g_ni_num * nr + 2.0 * a_i * g_den
        g_abr = first_rows(dabr_refs) + g_nr
        g_abi = first_rows(dabi_refs) + g_abi
        g_mag = g_abr * cs + g_abi * sn
        g_ang = mag * (g_abi * cs - g_abr * sn)
        g_e = g_mag * mag
        d_ar = d_ar + g_e * dt
        d_ai = d_ai + g_ang * dt
        g_dt = g_e * a_r + g_ang * a_i
        dar_ref[...] = d_ar
        dai_ref[...] = d_ai
        dldt_ref[...] = g_dt * dt

    vm = BS(memory_space=pltpu.VMEM)
    vec = SDS((SSM_ROWS, SSM_STATE), F32)
    return pl.pallas_call(body, out_shape=(vec,) * 7, in_specs=[vm] * 13, out_specs=(vm,) * 7, name="ssm_prep_bwd",
                          compiler_params=_params())(a_r, a_i, ldt, b_r, b_i, *d_abr, *d_abi, *d_win, *d_woutt)


SCAN_ROWS = 512
SCAN_SUB = 128


def _ssm_scan(name, inp, w1, a_r, a_i, w2, reverse):
    t = inp.shape[0]
    rows = min(SCAN_ROWS, t)
    n = t // rows
    n_sub = rows // SCAN_SUB
    ch = SSM_CH
    at = (lambda i: (n - 1 - i, 0)) if reverse else (lambda i: (i, 0))

    def body(in_ref, w1_ref, ar_ref, ai_ref, w2_ref, sb_ref, out_ref, cr_ref, ci_ref, k_ref, st_ref):
        i = pl.program_id(0)

        @pl.when(i == 0)
        def _():
            ar8 = jnp.broadcast_to(ar_ref[...], (8, ch))
            ai8 = jnp.broadcast_to(ai_ref[...], (8, ch))
            row = lax.broadcasted_iota(jnp.int32, (8, ch), 0)
            rank = (7 - row) if reverse else row
            powers = [(ar8, ai8)]
            for _ in range(7):
                p_r, p_i = powers[-1]
                powers.append((p_r * ar8 - p_i * ai8, p_r * ai8 + p_i * ar8))
            zero = jnp.zeros((8, ch), F32)
            for slot, k in enumerate((1, 2, 4)):
                k_ref[2 * slot] = jnp.where(rank >= k, powers[k - 1][0], zero)
                k_ref[2 * slot + 1] = jnp.where(rank >= k, powers[k - 1][1], zero)
            carry_r, carry_i = zero, zero
            for j in range(8):
                carry_r = jnp.where(rank == j, powers[j][0], carry_r)
                carry_i = jnp.where(rank == j, powers[j][1], carry_i)
            k_ref[6] = carry_r
            k_ref[7] = carry_i
            cr_ref[...] = zero
            ci_ref[...] = zero

        def group(r0, carry):
            c_r, c_i = carry
            x_r = st_ref[pl.ds(r0, 8), 0:ch]
            x_i = st_ref[pl.ds(r0, 8), ch:2 * ch]
            for slot, k in enumerate((1, 2, 4)):
                shift = (8 - k) if reverse else k
                s_r = pltpu.roll(x_r, shift, 0)
                s_i = pltpu.roll(x_i, shift, 0)
                m_r, m_i = k_ref[2 * slot], k_ref[2 * slot + 1]
                x_r, x_i = x_r + m_r * s_r - m_i * s_i, x_i + m_r * s_i + m_i * s_r
            p_r, p_i = k_ref[6], k_ref[7]
            x_r, x_i = x_r + p_r * c_r - p_i * c_i, x_i + p_r * c_i + p_i * c_r
            st_ref[pl.ds(r0, 8), 0:ch] = x_r
            st_ref[pl.ds(r0, 8), ch:2 * ch] = x_i
            last = 0 if reverse else 7
            return (jnp.broadcast_to(x_r[last:last + 1, :], (8, ch)), jnp.broadcast_to(x_i[last:last + 1, :], (8, ch)))

        carry = (cr_ref[...], ci_ref[...])
        for sc in (range(n_sub - 1, -1, -1) if reverse else range(n_sub)):
            part = pl.ds(sc * SCAN_SUB, SCAN_SUB)
            st_ref[part, :] = _dot(in_ref[part, :], w1_ref[...])
            for gi in range(SCAN_SUB // 8):
                g = (SCAN_SUB // 8 - 1 - gi) if reverse else gi
                carry = group(sc * SCAN_SUB + g * 8, carry)
            states = st_ref[part, :].astype(BF16)
            sb_ref[part, :] = states
            out_ref[part, :] = _dot(states, w2_ref[...])
        cr_ref[...] = carry[0]
        ci_ref[...] = carry[1]

    return pl.pallas_call(
        body, out_shape=(SDS((t, 2 * ch), BF16), SDS((t, D_SSM), F32)), grid=(n,),
        in_specs=[BS((rows, D_SSM), at), BS((D_SSM, 2 * ch), lambda i: (0, 0)), BS((1, ch), lambda i: (0, 0)),
                  BS((1, ch), lambda i: (0, 0)), BS((2 * ch, D_SSM), lambda i: (0, 0))],
        out_specs=(BS((rows, 2 * ch), at), BS((rows, D_SSM), at)),
        scratch_shapes=[pltpu.VMEM((8, ch), F32), pltpu.VMEM((8, ch), F32), pltpu.VMEM((8, 8, ch), F32),
                        pltpu.VMEM((rows, 2 * ch), F32)],
        name=name, compiler_params=_params("arbitrary"))(inp, w1, a_r, a_i, w2)


DA_ROWS = 1024


def _ssm_param_grads(name, lam, states, u, dy, reverse, after=()):
    t = lam.shape[0]
    rows = min(DA_ROWS, t)
    n = t // rows
    halo_rows = 16
    nb = rows // halo_rows
    ch = SSM_CH
    if reverse:
        halo_at = lambda i: (jnp.minimum((i + 1) * nb, t // halo_rows - 1), 0)
    else:
        halo_at = lambda i: (jnp.maximum(i * nb - 1, 0), 0)

    after_ops, after_specs = _after_operands(after)

    def body(lam_ref, x_ref, halo_ref, u_ref, dy_ref, *rest):
        dr_ref, di_ref, dwin_ref, dwoutt_ref = rest[len(after_ops):]
        i = pl.program_id(0)

        @pl.when(i == 0)
        def _():
            dr_ref[...] = jnp.zeros_like(dr_ref)
            di_ref[...] = jnp.zeros_like(di_ref)
            dwin_ref[...] = jnp.zeros_like(dwin_ref)
            dwoutt_ref[...] = jnp.zeros_like(dwoutt_ref)

        dwin_ref[...] += _dot(u_ref[...], lam_ref[...], TN)
        dwoutt_ref[...] += _dot(dy_ref[...], x_ref[...], TN)
        row = lax.broadcasted_iota(jnp.int32, (rows, ch), 0)
        if reverse:
            edge, shift, h_row, live = rows - 1, rows - 1, 0, i < n - 1
        else:
            edge, shift, h_row, live = 0, 1, halo_rows - 1, i > 0

        def neighbour(lo):
            halo = halo_ref[:, lo:lo + ch].astype(F32)[h_row:h_row + 1]
            halo = jnp.where(live, halo, 0.0)
            x = x_ref[:, lo:lo + ch].astype(F32)
            return jnp.where(row == edge, jnp.broadcast_to(halo, (rows, ch)), pltpu.roll(x, shift, 0))

        xp_r, xp_i = neighbour(0), neighbour(ch)
        l_r, l_i = lam_ref[:, 0:ch].astype(F32), lam_ref[:, ch:2 * ch].astype(F32)
        dr_ref[...] += jnp.sum(l_r * xp_r + l_i * xp_i, axis=0, keepdims=True)
        di_ref[...] += jnp.sum(l_i * xp_r - l_r * xp_i, axis=0, keepdims=True)

    blk = BS((rows, 2 * ch), lambda i: (i, 0))
    thin = BS((rows, D_SSM), lambda i: (i, 0))
    vec = BS((1, ch), lambda i: (0, 0))
    mat = BS((D_SSM, 2 * ch), lambda i: (0, 0))
    return pl.pallas_call(
        body, out_shape=(SDS((1, ch), F32), SDS((1, ch), F32), SDS((D_SSM, 2 * ch), F32), SDS((D_SSM, 2 * ch), F32)),
        grid=(n,), in_specs=[blk, blk, BS((halo_rows, 2 * ch), halo_at), thin, thin] + after_specs,
        out_specs=(vec, vec, mat, mat),
        name=name, compiler_params=_params("arbitrary"))(lam, states, states, u, dy, *after_ops)


GELU_C = math.sqrt(2.0 / math.pi)
GELU_K = 0.044715


def _ssm_combine(proj, y_fwd, y_bwd, d_skip, tm, after=()):
    t = proj.shape[0]
    after_ops, after_specs = _after_operands(after)

    def body(s_ref, yf_ref, yb_ref, d_ref, *rest):
        yt_ref, g_ref = rest[len(after_ops):]
        y = s_ref[...] * d_ref[...] + yf_ref[...] + yb_ref[...]
        yt_ref[...] = y
        th = jnp.tanh(GELU_C * (y + GELU_K * y * y * y))
        g_ref[...] = (0.5 * y * (1.0 + th)).astype(BF16)

    blk = BS((tm, D_SSM), lambda i: (i, 0))
    return pl.pallas_call(
        body, out_shape=(SDS((t, D_SSM), F32), SDS((t, D_SSM), BF16)), grid=(t // tm,),
        in_specs=[BS((tm, D_SSM), lambda i: (i, D_POOL // D_SSM)), blk, blk, BS((1, D_SSM), lambda i: (0, 0))] + after_specs,
        out_specs=(blk, blk), name="ssm_combine",
        compiler_params=_params("parallel"))(proj, y_fwd, y_bwd, d_skip, *after_ops)


def _ssm_ds(proj, d_yt, du_fwd, du_bwd, d_skip, tm):
    t = proj.shape[0]

    def body(s_ref, dy_ref, duf_ref, dub_ref, d_ref, ds_ref, dd_ref):
        i = pl.program_id(0)
        dy = dy_ref[...]
        ds_ref[...] = (dy * d_ref[...] + duf_ref[...] + dub_ref[...]).astype(BF16)

        @pl.when(i == 0)
        def _():
            dd_ref[...] = jnp.zeros_like(dd_ref)

        dd_ref[...] += jnp.sum(dy * s_ref[...], axis=0, keepdims=True)

    blk = BS((tm, D_SSM), lambda i: (i, 0))
    vec = BS((1, D_SSM), lambda i: (0, 0))
    return pl.pallas_call(
        body, out_shape=(SDS((t, D_SSM), BF16), SDS((1, D_SSM), F32)), grid=(t // tm,),
        in_specs=[BS((tm, D_SSM), lambda i: (i, D_POOL // D_SSM)), blk, blk, blk, vec],
        out_specs=(blk, vec), name="ssm_ds", compiler_params=_params("arbitrary"))(proj, d_yt, du_fwd, du_bwd, d_skip)


G_POOL_AT = D_POOL + D_SSM
G_SSM_AT = G_POOL_AT + D_MODEL
E_VAL, E_GATE = D_POOL, D_POOL + D_SSM


def _merge_specs(tm):
    return [BS((tm, D_POOL), lambda i: (i, 0)), BS((tm, D_SSM), lambda i: (i, 0)),
            BS((N_SHARD, 1024, 256), lambda i: (0, 0, 0)), BS((tm, D_FF), lambda i: (i, 0))]


def _merge_parts(s, ms, yv, w_ref, proj_ref):
    lo = 256 * s
    zp = _dot(ms, w_ref[s, 0:E_VAL, :])
    zv = _dot(yv, w_ref[s, E_VAL:E_GATE, :])
    zg = _dot(yv, w_ref[s, E_GATE:, :])
    return zp, zv, zg, proj_ref[:, G_POOL_AT + lo:G_POOL_AT + lo + 256], proj_ref[:, G_SSM_AT + lo:G_SSM_AT + lo + 256]


def _mixer_merge(ms, yssm, w_e, proj, tm):
    t = ms.shape[0]

    def body(ms_ref, y_ref, w_ref, proj_ref, o_ref):
        msv, yv = ms_ref[...], y_ref[...]
        for s in range(N_SHARD):
            zp, zv, zg, gp, gs = _merge_parts(s, msv, yv, w_ref, proj_ref)
            o_ref[:, 256 * s:256 * (s + 1)] = (_sigmoid(gp) * zp + _sigmoid(gs) * zv * _sigmoid(zg)).astype(BF16)

    row = BS((tm, D_MODEL), lambda i: (i, 0))
    return pl.pallas_call(
        body, out_shape=SDS((t, D_MODEL), BF16), grid=(t // tm,), in_specs=_merge_specs(tm), out_specs=row,
        name="mixer_merge", compiler_params=_params("parallel"))(ms, yssm, w_e, proj)


def _mixer_merge_bwd(ms, yssm, w_e, proj, dmerged, tm):
    t = ms.shape[0]

    def body(ms_ref, y_ref, w_ref, proj_ref, dm_ref, dgp_ref, dgs_ref, dzp_ref, dzv_ref, dzg_ref):
        msv, yv = ms_ref[...], y_ref[...]
        for s in range(N_SHARD):
            cols = slice(256 * s, 256 * (s + 1))
            zp, zv, zg, gp, gs = _merge_parts(s, msv, yv, w_ref, proj_ref)
            dm = dm_ref[:, cols].astype(F32)
            sp, ss, sg = _sigmoid(gp), _sigmoid(gs), _sigmoid(zg)
            dgp_ref[:, cols] = (dm * zp * sp * (1.0 - sp)).astype(BF16)
            dgs_ref[:, cols] = (dm * zv * sg * ss * (1.0 - ss)).astype(BF16)
            dzp_ref[:, cols] = (dm * sp).astype(BF16)
            dz = dm * ss
            dzv_ref[:, cols] = (dz * sg).astype(BF16)
            dzg_ref[:, cols] = (dz * zv * sg * (1.0 - sg)).astype(BF16)

    row = BS((tm, D_MODEL), lambda i: (i, 0))
    shape = SDS((t, D_MODEL), BF16)
    return pl.pallas_call(
        body, out_shape=(shape,) * 5, grid=(t // tm,), in_specs=_merge_specs(tm) + [row],
        out_specs=(row,) * 5, name="mixer_merge_bwd",
        compiler_params=_params("parallel"))(ms, yssm, w_e, proj, dmerged)


def _mixer_dw(ms, yssm, dzp, dzv, dzg, tm):
    t = ms.shape[0]
    tm = min(2 * tm, t)
    n_t = t // tm

    def body(ms_ref, y_ref, dzp_ref, dzv_ref, dzg_ref, o_ref, acc):
        i = pl.program_id(0)

        @pl.when(i == 0)
        def _():
            acc[...] = jnp.zeros_like(acc)

        msv, yv = ms_ref[...], y_ref[...]
        for s in range(N_SHARD):
            cols = slice(256 * s, 256 * (s + 1))
            acc[s, 0:E_VAL, :] += _dot(msv, dzp_ref[:, cols], TN)
            acc[s, E_VAL:E_GATE, :] += _dot(yv, dzv_ref[:, cols], TN)
            acc[s, E_GATE:, :] += _dot(yv, dzg_ref[:, cols], TN)

        @pl.when(i == n_t - 1)
        def _():
            o_ref[...] = acc[...].astype(BF16)

    row = BS((tm, D_MODEL), lambda i: (i, 0))
    full = BS((N_SHARD, 1024, 256), lambda i: (0, 0, 0))
    return pl.pallas_call(
        body, out_shape=SDS((N_SHARD, 1024, 256), BF16), grid=(n_t,),
        in_specs=[BS((tm, D_POOL), lambda i: (i, 0)), BS((tm, D_SSM), lambda i: (i, 0)), row, row, row],
        out_specs=full, scratch_shapes=[pltpu.VMEM((N_SHARD, 1024, 256), F32)],
        name="mixer_dw", compiler_params=_params("arbitrary"))(ms, yssm, dzp, dzv, dzg)


def _mixer_dx(dzp, dzv, dzg, w_e, y_total, tm):
    t = dzp.shape[0]

    def body(dzp_ref, dzv_ref, dzg_ref, w_ref, yt_ref, dms_ref, dy_ref):
        acc_ms, acc_y = None, None
        for s in range(N_SHARD):
            cols = slice(256 * s, 256 * (s + 1))
            part_ms = _dot(dzp_ref[:, cols], w_ref[s, 0:E_VAL, :], NT)
            part_y = _dot(dzv_ref[:, cols], w_ref[s, E_VAL:E_GATE, :], NT) + _dot(dzg_ref[:, cols], w_ref[s, E_GATE:, :], NT)
            acc_ms = part_ms if s == 0 else acc_ms + part_ms
            acc_y = part_y if s == 0 else acc_y + part_y
        dms_ref[...] = acc_ms.astype(BF16)
        y = yt_ref[...]
        th = jnp.tanh(GELU_C * (y + GELU_K * y * y * y))
        dgelu = 0.5 * (1.0 + th) + 0.5 * y * (1.0 - th * th) * GELU_C * (1.0 + 3.0 * GELU_K * y * y)
        dy_ref[...] = acc_y * dgelu

    row = BS((tm, D_MODEL), lambda i: (i, 0))
    return pl.pallas_call(
        body, out_shape=(SDS((t, D_POOL), BF16), SDS((t, D_SSM), F32)), grid=(t // tm,),
        in_specs=[row, row, row, BS((N_SHARD, 1024, 256), lambda i: (0, 0, 0)), BS((tm, D_SSM), lambda i: (i, 0))],
        out_specs=(BS((tm, D_POOL), lambda i: (i, 0)), BS((tm, D_SSM), lambda i: (i, 0))),
        name="mixer_dx", compiler_params=_params("parallel"))(dzp, dzv, dzg, w_e, y_total)


def _attn_probs(q_h, k_h):
    s = _dot(q_h, k_h, NT) * (1.0 / math.sqrt(HEAD_DIM))
    e = jnp.exp(s - jnp.max(s, axis=-1, keepdims=True))
    return e / jnp.sum(e, axis=-1, keepdims=True)


def _attn_fwd(q, kv, tm):
    t = q.shape[0]
    m = kv.shape[0]

    def body(q_ref, kv_ref, o_ref):
        for hd in range(N_HEADS):
            lo = hd * HEAD_DIM
            p = _attn_probs(q_ref[:, lo:lo + HEAD_DIM], kv_ref[:, lo:lo + HEAD_DIM])
            o_ref[:, lo:lo + HEAD_DIM] = _dot(p, kv_ref[:, D_MODEL + lo:D_MODEL + lo + HEAD_DIM]).astype(BF16)

    return pl.pallas_call(
        body, out_shape=SDS((t, D_MODEL), BF16), grid=(t // tm,),
        in_specs=[BS((tm, D_MODEL), lambda i: (i, 0)), BS((m, 2 * D_MODEL), lambda i: (0, 0))],
        out_specs=BS((tm, D_MODEL), lambda i: (i, 0)), name="attn_fwd", compiler_params=_params("parallel"))(q, kv)


def _attn_bwd(q, kv, d_o, tm):
    t = q.shape[0]
    m = kv.shape[0]

    def body(q_ref, kv_ref, do_ref, dq_ref, dkv_ref):
        i = pl.program_id(0)

        @pl.when(i == 0)
        def _():
            dkv_ref[...] = jnp.zeros_like(dkv_ref)

        for hd in range(N_HEADS):
            lo = hd * HEAD_DIM
            q_h = q_ref[:, lo:lo + HEAD_DIM]
            k_h = kv_ref[:, lo:lo + HEAD_DIM]
            v_h = kv_ref[:, D_MODEL + lo:D_MODEL + lo + HEAD_DIM]
            do_h = do_ref[:, lo:lo + HEAD_DIM]
            p = _attn_probs(q_h, k_h)
            dkv_ref[:, D_MODEL + lo:D_MODEL + lo + HEAD_DIM] += _dot(p, do_h, TN)
            dp = _dot(do_h, v_h, NT)
            ds = p * (dp - jnp.sum(dp * p, axis=-1, keepdims=True)) * (1.0 / math.sqrt(HEAD_DIM))
            dq_ref[:, lo:lo + HEAD_DIM] = _dot(ds, k_h).astype(BF16)
            dkv_ref[:, lo:lo + HEAD_DIM] += _dot(ds, q_h, TN)

    row = BS((tm, D_MODEL), lambda i: (i, 0))
    full = BS((m, 2 * D_MODEL), lambda i: (0, 0))
    return pl.pallas_call(
        body, out_shape=(SDS((t, D_MODEL), BF16), SDS((m, 2 * D_MODEL), F32)), grid=(t // tm,),
        in_specs=[row, full, row], out_specs=(row, full), name="attn_bwd",
        compiler_params=_params("arbitrary"))(q, kv, d_o)


TRANSPOSED = ("ffn1_w_gate", "ffn1_w_up", "ffn2_w_gate", "ffn2_w_up", "w_in")
GATHER_PHASES = {"f1a": (("ffn1_w_gate",), ("ffn1_w_up",)),
                 "f1b": (("ffn1_w_down",),),
                 "win": (("w_in",),),
                 "mix": (("w_mix_out", "w_q", "w_xo"), ("w_kv",), ("w_pool_proj", "w_glu_val", "w_glu_gate")),
                 "f2": (("ffn2_w_gate",), ("ffn2_w_up",), ("ffn2_w_down",))}
REDUCE_GROUPS = (("ffn2_w_gate",), ("ffn2_w_up",), ("ffn2_w_down",), ("w_xo",), ("w_q",), ("w_kv",), ("w_mix_out",),
                 ("w_pool_proj", "w_glu_val", "w_glu_gate"), ("w_in",), ("ffn1_w_gate",), ("ffn1_w_up",), ("ffn1_w_down",))
SMALL = ("ffn1_norm", "mix_norm", "pool_w", "pool_scale", "ssm_a_re", "ssm_a_im", "ssm_log_dt", "ssm_b_re",
         "ssm_b_im", "ssm_c_re", "ssm_c_im", "ssm_d", "xattn_norm", "mem_norm", "ffn2_norm", "final_norm")
WEIGHTS = ("ffn1_norm", "ffn1_w_gate", "ffn1_w_up", "ffn1_w_down", "mix_norm", "w_in", "pool_w", "pool_scale",
           "w_pool_proj", "ssm_a_re", "ssm_a_im", "ssm_log_dt", "ssm_b_re", "ssm_b_im", "ssm_c_re", "ssm_c_im",
           "ssm_d", "w_glu_val", "w_glu_gate", "w_mix_out", "xattn_norm", "mem_norm", "w_q", "w_kv", "w_xo",
           "ffn2_norm", "ffn2_w_gate", "ffn2_w_up", "ffn2_w_down", "final_norm")


def _small_view(a, n):
    return jnp.swapaxes(a, 3, 4) if n in ("ssm_b_re", "ssm_b_im") else a


def _device_step(x, mem, target, wts, sp, reducer=None):
    t = x.shape[0]
    tm = min(TM, t)
    g = {}

    first_gather = wts.start("f1a")
    u1 = _rmsnorm("norm_ffn1", x, sp["ffn1_norm"], tm, after=first_gather)

    def per_channel(a):
        a = a.reshape(2 * SSM_GROUPS, 1, -1)
        return jnp.broadcast_to(a, (2 * SSM_GROUPS, SSM_GROUP, a.shape[-1])).reshape(SSM_ROWS, a.shape[-1])

    ssm_a = per_channel(sp["ssm_a_re"]), per_channel(sp["ssm_a_im"]), per_channel(sp["ssm_log_dt"])
    ssm_b = sp["ssm_b_re"].reshape(SSM_ROWS, SSM_STATE), sp["ssm_b_im"].reshape(SSM_ROWS, SSM_STATE)
    abr, abi, w_in_s, w_in_s_t, w_out_s_t, w_out_s = _ssm_prep(
        *ssm_a, *ssm_b, sp["ssm_c_re"].reshape(SSM_ROWS, SSM_STATE), sp["ssm_c_im"].reshape(SSM_ROWS, SSM_STATE),
        after=first_gather)
    first_rows = (2, SSM_GROUPS, SSM_GROUP, SSM_STATE)
    a_r = abr.reshape(first_rows)[:, :, 0].reshape(2, 1, SSM_CH)
    a_i = abi.reshape(first_rows)[:, :, 0].reshape(2, 1, SSM_CH)
    mem_n = _rmsnorm("norm_mem", mem, sp["mem_norm"], mem.shape[0], after=first_gather)

    whole = (D_FF, D_MODEL)
    w_g1, w_u1 = wts.finish("f1a", [u1, w_in_s, w_in_s_t, w_out_s, w_out_s_t, a_r, a_i, mem_n])
    w_f1 = {"gate": w_g1.reshape(whole), "up": w_u1.reshape(whole)}
    down_gather = wts.start("f1b", [w_g1])
    g1, up1, a1 = _ffn_up("ffn1_up", u1, w_f1, tm, after=wts.start("win", down_gather))
    (w_dn,) = wts.finish("f1b", [a1])
    w_f1["down"] = w_dn.reshape(whole)
    h1, u2 = _ffn_down("ffn1_down", a1, w_f1, x, tm, next_gain=sp["mix_norm"])

    (w_in_g,) = wts.finish("win", [u2])
    w_in_t = w_in_g.reshape(D_FF, D_MODEL)
    proj = _mm("mix_in", [(u2, BS((tm, D_MODEL), lambda j, i: (i, 0)), w_in_t, BS((D_FF // 2, D_MODEL), lambda j, i: (j, 0)), NT)],
               grid=(2, t // tm), out_shape=SDS((t, D_FF), F32), out_spec=BS((tm, D_FF // 2), lambda j, i: (i, j)),
               after=wts.start("f2", wts.start("mix", [w_in_g])))
    pooled, mixed, ms = _pool_fwd(proj, sp["pool_w"][0], sp["pool_scale"])

    s_in = proj[:, D_POOL:D_POOL + D_SSM].astype(BF16)
    states, y_dirs = [], []
    for dr in range(2):
        st, yd = _ssm_scan(f"ssm_scan_fwd{dr}", s_in, w_in_s[dr], a_r[dr], a_i[dr], w_out_s[dr], reverse=(dr == 1))
        states.append(st)
        y_dirs.append(yd)
    w_sq, w_kv, w_e = wts.finish("mix", y_dirs)
    w_mo, w_q, w_xo = (w_sq[:, 256 * k:256 * (k + 1)].reshape(D_MODEL, D_MODEL) for k in range(3))
    w_d = w_kv[:, None]
    y_total, yssm = _ssm_combine(proj, y_dirs[0], y_dirs[1], sp["ssm_d"], tm)

    merged = _mixer_merge(ms, yssm, w_e, proj, tm)
    h2, u3 = _mm_resid_norm("mix_out", merged, w_mo, h1, sp["xattn_norm"], tm)

    q = _plain_mm("attn_q", u3, w_q, NN, BF16, tm)
    n_mem = mem.shape[0]
    kv = _mm("attn_kv", [(mem_n, BS((n_mem, D_MODEL), lambda s: (0, 0)), w_d, BS((None, None, D_MODEL, 512), lambda s: (s, 0, 0, 0)), NN)],
             grid=(N_SHARD,), out_shape=SDS((n_mem, 2 * D_MODEL), BF16), out_spec=BS((n_mem, 512), lambda s: (0, s)))
    o = _attn_fwd(q, kv, tm)
    h3, u4 = _mm_resid_norm("attn_out", o, w_xo, h2, sp["ffn2_norm"], tm)

    w_f2 = dict(zip(("gate", "up", "down"), (a.reshape(whole) for a in wts.finish("f2", [u4]))))
    g2, up2, a2 = _ffn_up("ffn2_up", u4, w_f2, tm)
    loss, dh4, dh4_b, g["final_norm"] = _ffn_down("ffn2_down", a2, w_f2, h3, tm,
                                                  head=(sp["final_norm"].reshape(1, D_MODEL), target))

    dg2, dup2 = _ffn_bwd_act("ffn2_bwd_act", dh4_b, w_f2, g2, up2, tm)
    dw_f2 = _ffn_dw("ffn2_dw", u4, dg2, dup2, a2, dh4_b, tm)
    dh3, dh3_b, g["ffn2_norm"] = _ffn_dx("ffn2_dx", dg2, dup2, w_f2, h3, sp["ffn2_norm"], dh4, tm)

    d_o = _plain_mm("attn_out_dx", dh3_b, w_xo, NT, BF16, tm)
    dw_xo = _dw_mm("attn_out_dw", o, dh3_b, tm)
    dq, dkv = _attn_bwd(q, kv, d_o, tm)
    dw_q = _dw_mm("attn_q_dw", u3, dq, tm)
    dh2, dh2_b, g["xattn_norm"] = _mm_norm_bwd("attn_q_dx", dq, w_q, NT, h2, sp["xattn_norm"], dh3, tm)
    dw_kv = _mm("attn_kv_dw", [(mem_n, BS((n_mem, D_MODEL), lambda s: (0, 0)), dkv, BS((n_mem, 512), lambda s: (0, s)), TN)],
                grid=(N_SHARD,), out_shape=SDS((N_SHARD, D_MODEL, 512), BF16), out_spec=BS((None, D_MODEL, 512), lambda s: (s, 0, 0)))
    dmem_n = _mm("attn_kv_dx", [(dkv, BS((n_mem, 512), lambda s: (0, s)), w_d, BS((None, None, D_MODEL, 512), lambda s: (s, 0, 0, 0)), NT)],
                 grid=(N_SHARD,), red_axis=0, out_shape=SDS((n_mem, D_MODEL), F32), out_spec=BS((n_mem, D_MODEL), lambda s: (0, 0)))
    _, _, g["mem_norm"] = _rmsnorm_bwd("norm_mem_bwd", mem, sp["mem_norm"], dmem_n, None, n_mem)

    square = (N_SHARD, D_MODEL // N_SHARD, D_MODEL)
    sharded = (N_SHARD, FF_SH, D_MODEL)
    early = [a.reshape(sharded) for a in dw_f2] + [dw_xo.reshape(square), dw_q.reshape(square), dw_kv]
    swapping = reducer.swap_start("a1", early) if reducer is not None else []
    dmerged = _plain_mm("mix_out_dx", dh2_b, w_mo, NT, BF16, tm, after=swapping)
    dw_mo = _dw_mm("mix_out_dw", merged, dh2_b, tm)
    d_gp, d_gs, dzp, dzv, dzg = _mixer_merge_bwd(ms, yssm, w_e, proj, dmerged, tm)
    dw_e = _mixer_dw(ms, yssm, dzp, dzv, dzg, tm)
    d_ms, d_yt = _mixer_dx(dzp, dzv, dzg, w_e, y_total, tm)
    dp, d_scale, d_pw = _pool_bwd(d_ms, mixed, pooled, sp["pool_w"][0], sp["pool_scale"])
    g["pool_scale"] = d_scale
    g["pool_w"] = d_pw[None]

    d_yt_b = d_yt.astype(BF16)
    du_dirs, lams = [], []
    for dr in range(2):
        lam, du = _ssm_scan(f"ssm_scan_bwd{dr}", d_yt_b, w_out_s_t[dr], a_r[dr], -a_i[dr], w_in_s_t[dr], reverse=(dr == 0))
        du_dirs.append(du)
        lams.append(lam)
    ds, g["ssm_d"] = _ssm_ds(proj, d_yt, du_dirs[0], du_dirs[1], sp["ssm_d"], tm)

    d_proj = jnp.concatenate([dp, ds, d_gp, d_gs], axis=1)
    tw = min(2 * tm, t)
    dw_in_t = _mm("mix_in_dw", [(d_proj, BS((tw, D_FF // 2), lambda j, i: (i, j)), u2, BS((tw, D_MODEL), lambda j, i: (i, 0)), TN)],
                  grid=(2, t // tw), red_axis=1, out_shape=SDS((D_FF, D_MODEL), BF16), out_spec=BS((D_FF // 2, D_MODEL), lambda j, i: (j, 0)))
    dh1, dh1_b, g["mix_norm"] = _mm_norm_bwd("mix_in_dx", d_proj, w_in_t, NN, h1, sp["mix_norm"], dh2, tm)

    early += [dw_mo.reshape(square), dw_e, dw_in_t.reshape(sharded)]
    g["final_norm"] = g["final_norm"].reshape(D_MODEL)

    travelling = reducer.start("a", early[6:], swapped=["a1"], after=list(g.values())) if reducer is not None else []
    d_abr, d_abi, d_cm, d_bm = [], [], [], []
    for dr in range(2):
        da_r, da_i, d_win, d_woutt = _ssm_param_grads(f"ssm_param_grads{dr}", lams[dr], states[dr], s_in, d_yt_b,
                                                      reverse=(dr == 1), after=travelling)
        d_abr.append(da_r)
        d_abi.append(da_i)
        d_bm.append(d_win)
        d_cm.append(d_woutt)

    d_ar, d_ai, d_ldt, d_br, d_bi, d_cr, d_ci = _ssm_prep_bwd(*ssm_a, *ssm_b, d_abr, d_abi, d_bm, d_cm)
    per_group = (2 * SSM_GROUPS, SSM_GROUP * SSM_STATE)
    g["ssm_a_re"] = d_ar.reshape(2 * SSM_GROUPS, SSM_GROUP, SSM_STATE).sum(axis=1).reshape(sp["ssm_a_re"].shape)
    g["ssm_a_im"] = d_ai.reshape(2 * SSM_GROUPS, SSM_GROUP, SSM_STATE).sum(axis=1).reshape(sp["ssm_a_im"].shape)
    g["ssm_log_dt"] = d_ldt.reshape(per_group).sum(axis=1).reshape(sp["ssm_log_dt"].shape)
    g["ssm_b_re"] = d_br.reshape(sp["ssm_b_re"].shape)
    g["ssm_b_im"] = d_bi.reshape(sp["ssm_b_im"].shape)
    g["ssm_c_re"] = d_cr.reshape(sp["ssm_c_re"].shape)
    g["ssm_c_im"] = d_ci.reshape(sp["ssm_c_im"].shape)
    if reducer is not None:
        travelling = travelling + [d_ar, d_br, d_cr]
    dg1, dup1 = _ffn_bwd_act("ffn1_bwd_act", dh1_b, w_f1, g1, up1, tm, after=travelling)
    dw_f1 = [a.reshape(sharded) for a in _ffn_dw("ffn1_dw", u1, dg1, dup1, a1, dh1_b, tm)]
    if reducer is not None:
        travelling = reducer.start("b", dw_f1, after=reducer.finish("a", dw_f1[:1]))
        travelling = travelling + reducer.join_start("a", after=travelling)
    grad_x, _, g["ffn1_norm"] = _ffn_dx("ffn1_dx", dg1, dup1, w_f1, x, sp["ffn1_norm"], dh1, tm, after=travelling)
    if reducer is not None:
        reducer.finish("b", [grad_x])
        reducer.join_finish("a", [grad_x])
    return loss, grad_x, early + dw_f1, g


def _mesh_place():
    x, y, c = lax.axis_index("x"), lax.axis_index("y"), lax.axis_index("c")
    chips = [(1 - x, y), (x, 1 - y), (1 - x, 1 - y)]
    return x, y, c, chips


def _remote(src, dst, send_sems, recv_sems, k, to):
    return pltpu.make_async_remote_copy(src_ref=src, dst_ref=dst, send_sem=send_sems.at[k], recv_sem=recv_sems.at[k],
                                        device_id=to, device_id_type=MESH)


def _sibling_swap_halves(tag, grads, after=()):
    n = len(grads)
    after_ops, after_specs = _after_operands(after)

    def body(*refs):
        ins, outs = refs[:n], refs[n + len(after_ops):2 * n + len(after_ops)]
        send_sems, recv_sems = refs[2 * n + len(after_ops):]
        x, y, c, _ = _mesh_place()
        sibling = (x, y, 1 - c)
        copies = []
        for k in range(n):
            half = grads[k].shape[1] // 2
            theirs = pl.ds(pl.multiple_of((1 - c) * half, 16), half)
            cp = _remote(ins[k].at[:, theirs, :], outs[k], send_sems, recv_sems, k, sibling)
            cp.start()
            copies.append(cp)
        for cp in copies:
            cp.wait_recv()
        for cp in copies:
            cp.wait_send()

    hbm = BS(memory_space=pl.ANY)
    return pl.pallas_call(
        body, out_shape=tuple(SDS((g.shape[0], g.shape[1] // 2, g.shape[2]), g.dtype) for g in grads),
        in_specs=[hbm] * n + after_specs, out_specs=(hbm,) * n,
        scratch_shapes=[pltpu.SemaphoreType.DMA((n,)), pltpu.SemaphoreType.DMA((n,))],
        name="reduce_sibling_send_" + tag, compiler_params=_params())(*grads, *after_ops)


def _row_tile(rows, cap=512):
    return max(r for r in range(16, cap + 1, 16) if rows % r == 0)


REDUCE_STEPS = 2


def _chip_presum(tag, grads, gots, c_idx):
    n = len(grads)
    halves = [g.shape[1] // 2 for g in grads]
    tiles = [(h // REDUCE_STEPS, g.shape[2]) for h, g in zip(halves, grads)]

    def body(c_ref, *refs):
        for k in range(n):
            refs[2 * n + k][...] = (refs[k][...].astype(F32) + refs[n + k][...].astype(F32)).astype(BF16)

    mine = [BS((None, None) + tile, lambda s, i, c_ref: (s, c_ref[0], i, 0)) for tile in tiles]
    plain = [BS((None,) + tile, lambda s, i, c_ref: (s, i, 0)) for tile in tiles]
    return list(pl.pallas_call(
        body, out_shape=tuple(SDS((g.shape[0], h, g.shape[2]), BF16) for g, h in zip(grads, halves)),
        grid_spec=pltpu.PrefetchScalarGridSpec(num_scalar_prefetch=1, grid=(N_SHARD, REDUCE_STEPS),
                                               in_specs=mine + plain, out_specs=plain),
        name="reduce_presum_" + tag, compiler_params=_params("parallel", "parallel"))(
            c_idx, *[g.reshape(g.shape[0], 2, h, g.shape[2]) for g, h in zip(grads, halves)], *gots))


HBM_SPEC = BS(memory_space=pltpu.HBM)
SEM_SPEC = BS(memory_space=pltpu.SEMAPHORE)
DATAFLOW = pltpu.SideEffectType.DATAFLOW_SIDE_EFFECTING


def _chip_exchange_copies(parts, lands, send_sems, recv_sems):
    _, _, c, chips = _mesh_place()
    return [_remote(parts[k].at[2 * px + py], lands[k].at[j], send_sems, recv_sems, 3 * k + j, (px, py, c))
            for k in range(len(parts)) for j, (px, py) in enumerate(chips)]


def _gather_copies(shards, lands, send_sems, recv_sems):
    x, y, c, chips = _mesh_place()
    return [_remote(shards[k], lands[k].at[2 * x + y], send_sems, recv_sems, 3 * k + j, (px, py, c))
            for k in range(len(shards)) for j, (px, py) in enumerate(chips)]


def _gather_half_copies(shards, lands, send_sems, recv_sems):
    x, y, c, chips = _mesh_place()
    out = []
    for k in range(len(shards)):
        half = shards[k].shape[0] // 2
        mine = pl.ds(pl.multiple_of(c * half, 16), half)
        for j, (px, py) in enumerate(chips):
            out.append(_remote(shards[k].at[mine, :], lands[k].at[2 * x + y, mine, :], send_sems, recv_sems,
                               3 * k + j, (px, py, c)))
    return out


def _sibling_fill(tag, lands):
    n = len(lands)

    def body(*refs):
        outs = refs[n:2 * n]
        send_sems, recv_sems = refs[2 * n:]
        x, y, c, chips = _mesh_place()
        copies = []
        for k in range(n):
            half = lands[k].shape[1] // 2
            mine = pl.ds(pl.multiple_of(c * half, 16), half)
            for j, (px, py) in enumerate(chips):
                blk = outs[k].at[2 * px + py, mine, :]
                copies.append(_remote(blk, blk, send_sems, recv_sems, 3 * k + j, (x, y, 1 - c)))
        for cp in copies:
            cp.start()
        for cp in copies:
            cp.wait_recv()
        for cp in copies:
            cp.wait_send()

    hbm = BS(memory_space=pl.ANY)
    return list(pl.pallas_call(
        body, out_shape=tuple(SDS(a.shape, a.dtype) for a in lands),
        in_specs=[hbm] * n, out_specs=(hbm,) * n, input_output_aliases={k: k for k in range(n)},
        scratch_shapes=[pltpu.SemaphoreType.DMA((3 * n,)), pltpu.SemaphoreType.DMA((3 * n,))],
        name="gather_fill_" + tag, compiler_params=_params())(*lands))


def _swap_copies(grads, lands, send_sems, recv_sems):
    x, y, c, _ = _mesh_place()
    out = []
    for k in range(len(grads)):
        half = grads[k].shape[1] // 2
        theirs = pl.ds(pl.multiple_of((1 - c) * half, 16), half)
        out.append(_remote(grads[k].at[:, theirs, :], lands[k], send_sems, recv_sems, k, (x, y, 1 - c)))
    return out


def _join_copies(fulls, same, send_sems, recv_sems):
    x, y, c, _ = _mesh_place()
    out = []
    for k in range(len(fulls)):
        half = fulls[k].shape[0] // 2
        mine = fulls[k].at[pl.ds(pl.multiple_of(c * half, 8), half), :]
        out.append(_remote(mine, mine, send_sems, recv_sems, k, (x, y, 1 - c)))
    return out


def _everyone_copies(packs, lands, send_sems, recv_sems):
    x, y, c, _ = _mesh_place()
    out = []
    for k in range(len(packs)):
        for j in range(N_DEV - 1):
            bx, by, bc = (j + 1) >> 2 & 1, (j + 1) >> 1 & 1, (j + 1) & 1
            peer = (x ^ bx, y ^ by, c ^ bc)
            out.append(_remote(packs[k], lands[k].at[4 * x + 2 * y + c], send_sems, recv_sems, (N_DEV - 1) * k + j, peer))
    return out


def _split_start(name, copies, sources, land_shapes, after=(), fanout=3):
    n = len(sources)
    n_land = len(land_shapes)
    m = n + n_land
    n_sems = fanout * n
    after_ops, after_specs = _after_operands(after)

    def body(*refs):
        ins = refs[:n]
        lands = refs[n:m] if n_land else ins
        send_sems, recv_sems = refs[m + len(after_ops)], refs[m + len(after_ops) + 1]
        token = refs[-1]
        for cp in copies(ins, lands, send_sems, recv_sems):
            cp.start()
        token[...] = jnp.zeros_like(token)

    lands = [pltpu.with_memory_space_constraint(lax.empty(s, d), pltpu.HBM) for s, d in land_shapes]
    sources = [pltpu.with_memory_space_constraint(p, pltpu.HBM) for p in sources]
    thru = [pltpu.HBM(a.shape, a.dtype) for a in sources + lands]
    out = pl.pallas_call(
        body, name=name,
        out_shape=(pltpu.SemaphoreType.DMA((n_sems,)), pltpu.SemaphoreType.DMA((n_sems,)), *thru, SDS((8, 128), F32)),
        in_specs=[HBM_SPEC] * m + after_specs,
        out_specs=(SEM_SPEC, SEM_SPEC, *[HBM_SPEC] * m, BS(memory_space=pltpu.VMEM)),
        input_output_aliases={i: 2 + i for i in range(m)},
        compiler_params=pltpu.CompilerParams(has_side_effects=DATAFLOW))(*sources, *lands, *after_ops)
    return out[0], out[1], list(out[2:2 + n]), list(out[2 + n:2 + m]), out[-1]


def _split_wait(name, copies, send_sems, recv_sems, sources, lands, after):
    n = len(sources)
    m = n + len(lands)
    after_ops, after_specs = _after_operands(after)

    def body(*refs):
        ins = refs[:n]
        zones = refs[n:m] if m > n else ins
        for cp in copies(ins, zones, refs[m], refs[m + 1]):
            cp.wait_send()
            cp.wait_recv()

    out = pl.pallas_call(
        body, name=name,
        out_shape=tuple(pltpu.HBM(a.shape, a.dtype) for a in sources + lands),
        in_specs=[HBM_SPEC] * m + [SEM_SPEC, SEM_SPEC] + after_specs, out_specs=(HBM_SPEC,) * m,
        input_output_aliases={i: i for i in range(m)},
        compiler_params=pltpu.CompilerParams(has_side_effects=DATAFLOW))(*sources, *lands, send_sems, recv_sems, *after_ops)
    return list(out[:n]), list(out[n:])


class _WeightGatherer:
    def __init__(self, shards):
        self.shards, self.open = shards, {}
        self.me = 2 * lax.axis_index("x") + lax.axis_index("y")

    HALVED = ("f1a", "mix")

    def start(self, tag, after=()):
        shapes = [((N_SHARD,) + s.shape, s.dtype) for s in self.shards[tag]]
        copies = _gather_half_copies if tag in self.HALVED else _gather_copies
        self.open[tag] = _split_start("gather_start_" + tag, copies, self.shards[tag], shapes, after)
        return [self.open[tag][-1]]

    def finish(self, tag, after):
        send_sems, recv_sems, shards, lands, _ = self.open.pop(tag)
        copies = _gather_half_copies if tag in self.HALVED else _gather_copies
        shards, lands = _split_wait("gather_wait_" + tag, copies, send_sems, recv_sems, shards, lands, after)
        if tag in self.HALVED:
            lands = _sibling_fill(tag, lands)
        return [lax.dynamic_update_slice(zone, s[None], (self.me, 0, 0)) for zone, s in zip(lands, shards)]


class _GradReducer:
    def __init__(self):
        self.c_idx = lax.axis_index("c").astype(jnp.int32).reshape(1)
        self.place = jnp.stack([2 * lax.axis_index("x") + lax.axis_index("y"), lax.axis_index("c")]).astype(jnp.int32)
        self.swaps, self.open, self.landed, self.joins, self.reduced = {}, {}, {}, {}, []

    def swap_start(self, tag, grads, after=()):
        shapes = [((g.shape[0], g.shape[1] // 2, g.shape[2]), g.dtype) for g in grads]
        self.swaps[tag] = _split_start("reduce_swap_start_" + tag, _swap_copies, grads, shapes, after, fanout=1)
        return [self.swaps[tag][-1]]

    def start(self, tag, grads, after=(), swapped=()):
        pairs = []
        for s in swapped:
            send_sems, recv_sems, early, lands, _ = self.swaps.pop(s)
            pairs += zip(*_split_wait("reduce_swap_wait_" + s, _swap_copies, send_sems, recv_sems, early, lands, grads[-1:]))
        pairs += zip(grads, _sibling_swap_halves(tag, grads, after))
        parts = _chip_presum(tag, [g for g, _ in pairs], [s for _, s in pairs], self.c_idx)
        shapes = [((3,) + p.shape[1:], p.dtype) for p in parts]
        self.open[tag] = _split_start("reduce_exchange_start_" + tag, _chip_exchange_copies, parts, shapes)
        return [self.open[tag][-1]]

    def finish(self, tag, after):
        send_sems, recv_sems, parts, lands, _ = self.open.pop(tag)
        self.landed[tag] = _split_wait("reduce_exchange_wait_" + tag, _chip_exchange_copies, send_sems, recv_sems, parts, lands, after)
        return self.landed[tag][1][:1]

    def _sums(self, tag, after=()):
        parts, landed = self.landed.pop(tag)
        return _chip_sum(tag, parts, landed, self.place, after)

    def join_start(self, tag, after=()):
        self.joins[tag] = _split_start("reduce_join_start_" + tag, _join_copies, self._sums(tag, after), [], fanout=1)
        return [self.joins[tag][-1]]

    def join_finish(self, tag, after):
        send_sems, recv_sems, fulls, _, _ = self.joins.pop(tag)
        self.reduced += _split_wait("reduce_join_wait_" + tag, _join_copies, send_sems, recv_sems, fulls, [], after)[0]

    def join(self, tag, after=()):
        self.reduced += _sibling_join_halves(self._sums(tag), after)


def _chip_sum(tag, parts, gots, place, after=()):
    n = len(parts)
    tiles = [(p.shape[1] // REDUCE_STEPS, p.shape[2]) for p in parts]
    after_ops, after_specs = _after_operands(after)

    def body(place_ref, *refs):
        outs = refs[2 * n + len(after_ops):]
        for k in range(n):
            acc = refs[k][...].astype(F32)
            for j in range(3):
                acc = acc + refs[n + k][j].astype(F32)
            outs[k][...] = acc

    return list(pl.pallas_call(
        body, out_shape=tuple(SDS((2 * p.shape[1], p.shape[2]), F32) for p in parts),
        grid_spec=pltpu.PrefetchScalarGridSpec(
            num_scalar_prefetch=1, grid=(REDUCE_STEPS,),
            in_specs=[BS((None,) + tile, lambda i, place_ref: (place_ref[0], i, 0)) for tile in tiles]
            + [BS((3,) + tile, lambda i, place_ref: (0, i, 0)) for tile in tiles] + after_specs,
            out_specs=[BS(tile, lambda i, place_ref: (place_ref[1] * REDUCE_STEPS + i, 0)) for tile in tiles]),
        name="reduce_sum_" + tag, compiler_params=_params("parallel"))(place, *parts, *gots, *after_ops))


def _sibling_join_halves(fulls, after=()):
    n = len(fulls)
    after_ops, after_specs = _after_operands(after)

    def body(*refs):
        outs = refs[n + len(after_ops):2 * n + len(after_ops)]
        send_sems, recv_sems = refs[2 * n + len(after_ops):]
        copies = _join_copies(outs, outs, send_sems, recv_sems)
        for cp in copies:
            cp.start()
        for cp in copies:
            cp.wait_recv()
        for cp in copies:
            cp.wait_send()

    hbm = BS(memory_space=pl.ANY)
    return list(pl.pallas_call(
        body, out_shape=tuple(SDS(f.shape, f.dtype) for f in fulls),
        in_specs=[hbm] * n + after_specs, out_specs=(hbm,) * n, input_output_aliases={k: k for k in range(n)},
        scratch_shapes=[pltpu.SemaphoreType.DMA((n,)), pltpu.SemaphoreType.DMA((n,))],
        name="reduce_sibling_join", compiler_params=_params())(*fulls, *after_ops))


N_DEV = 8


def _sum_devices(packs):
    _, rows, lanes = packs.shape

    def body(p_ref, o_ref):
        acc = p_ref[0]
        for dev in range(1, N_DEV):
            acc = acc + p_ref[dev]
        o_ref[...] = acc

    vm = BS(memory_space=pltpu.VMEM)
    return pl.pallas_call(body, out_shape=SDS((rows, lanes), F32), in_specs=[vm], out_specs=vm,
                          name="small_sum", compiler_params=_params())(packs)


def _adamw_refs(w_ref, g_ref, m_ref, v_ref, go_ref, d_ref, mo_ref, vo_ref):
    bc1 = 1.0 - ADAM_B1 ** ADAM_STEP
    bc2 = 1.0 - ADAM_B2 ** ADAM_STEP
    g = g_ref[...]
    m_new = ADAM_B1 * m_ref[...] + (1.0 - ADAM_B1) * g
    v_new = ADAM_B2 * v_ref[...] + (1.0 - ADAM_B2) * (g * g)
    go_ref[...] = g
    mo_ref[...] = m_new
    vo_ref[...] = v_new
    d_ref[...] = -ADAM_LR * ((m_new / bc1) / (jnp.sqrt(v_new / bc2) + ADAM_EPS) + ADAM_WD * w_ref[...])


def _adamw_small(ws, gs, ms, vs):
    n = len(ws)

    def body(*refs):
        for k in range(n):
            _adamw_refs(*[refs[j * n + k] for j in range(4)], *refs[4 * n + 4 * k:4 * n + 4 * k + 4])

    vm = BS(memory_space=pltpu.VMEM)
    outs = pl.pallas_call(
        body, out_shape=tuple(SDS(a.shape, F32) for a in ws for _ in range(4)), in_specs=[vm] * (4 * n),
        out_specs=(vm,) * (4 * n), name="adamw_small", compiler_params=_params())(*ws, *gs, *ms, *vs)
    return [outs[4 * k:4 * k + 4] for k in range(n)]


def _adamw(name, w, grad, row0, m, v, after=()):
    rows, cols = w.shape
    tr = rows if rows < 16 else _row_tile(rows, 352)
    after_ops, after_specs = _after_operands(after)

    def body(w_ref, g_ref, m_ref, v_ref, *rest):
        _adamw_refs(w_ref, g_ref, m_ref, v_ref, *rest[len(after_ops):])

    blk = BS((tr, cols), lambda i: (i, 0))
    shape = SDS((rows, cols), F32)
    return pl.pallas_call(
        body, out_shape=(shape,) * 4, grid=(rows // tr,),
        in_specs=[blk, BS((tr, cols), lambda i: (row0 // tr + i, 0)), blk, blk] + after_specs, out_specs=(blk,) * 4,
        name=name, compiler_params=_params("parallel"))(w, grad, m, v, *after_ops)


SMALL_LANES = 128


def _pack_small(parts):
    flat = jnp.concatenate([jnp.ravel(p) for p in parts])
    rows = -(-flat.shape[0] // (64 * SMALL_LANES)) * 64
    return jnp.pad(flat, (0, rows * SMALL_LANES - flat.shape[0])).reshape(rows, SMALL_LANES)


def _unpack_small(packed, like):
    flat = jnp.ravel(packed)
    out, at = [], 0
    for p in like:
        out.append(flat[at:at + p.size].reshape(p.shape))
        at += p.size
    return out


def kernel(x, mem, ffn1_norm, ffn1_w_gate, ffn1_w_up, ffn1_w_down, mix_norm, w_in, pool_w, pool_scale, w_pool_proj, ssm_a_re, ssm_a_im, ssm_log_dt, ssm_b_re, ssm_b_im, ssm_c_re, ssm_c_im, ssm_d, w_glu_val, w_glu_gate, w_mix_out, xattn_norm, mem_norm, w_q, w_kv, w_xo, ffn2_norm, ffn2_w_gate, ffn2_w_up, ffn2_w_down, final_norm, loss_target, m_ffn1_norm, m_ffn1_w_gate, m_ffn1_w_up, m_ffn1_w_down, m_mix_norm, m_w_in, m_pool_w, m_pool_scale, m_w_pool_proj, m_ssm_a_re, m_ssm_a_im, m_ssm_log_dt, m_ssm_b_re, m_ssm_b_im, m_ssm_c_re, m_ssm_c_im, m_ssm_d, m_w_glu_val, m_w_glu_gate, m_w_mix_out, m_xattn_norm, m_mem_norm, m_w_q, m_w_kv, m_w_xo, m_ffn2_norm, m_ffn2_w_gate, m_ffn2_w_up, m_ffn2_w_down, m_final_norm, v_ffn1_norm, v_ffn1_w_gate, v_ffn1_w_up, v_ffn1_w_down, v_mix_norm, v_w_in, v_pool_w, v_pool_scale, v_w_pool_proj, v_ssm_a_re, v_ssm_a_im, v_ssm_log_dt, v_ssm_b_re, v_ssm_b_im, v_ssm_c_re, v_ssm_c_im, v_ssm_d, v_w_glu_val, v_w_glu_gate, v_w_mix_out, v_xattn_norm, v_mem_norm, v_w_q, v_w_kv, v_w_xo, v_ffn2_norm, v_ffn2_w_gate, v_ffn2_w_up, v_ffn2_w_down, v_final_norm):
    given = dict(locals())
    w = {n: given[n] for n in WEIGHTS}
    m = {n: given["m_" + n] for n in WEIGHTS}
    v = {n: given["v_" + n] for n in WEIGHTS}

    def shard_view(a, n):
        return a[0].T if n in TRANSPOSED else a[0]

    def shard_unview(a, n):
        return (a.T if n in TRANSPOSED else a)[None]

    shards = {tag: [jnp.concatenate([shard_view(w[n], n).astype(BF16) for n in grp], axis=0) for grp in arrays]
              for tag, arrays in GATHER_PHASES.items()}
    reducer = _GradReducer()
    ws, ms, vs = ({n: _small_view(a[n], n) for n in SMALL} for a in (w, m, v))
    loss_part, grad_x, _, small = _device_step(x[0], mem[0], loss_target[0], _WeightGatherer(shards), ws, reducer)

    small_like = [ws[n] for n in SMALL] + [loss_part[0, :1]]
    pack = _pack_small([small[n] for n in SMALL] + [loss_part[0, :1]])
    everyone = _split_start("small_start", _everyone_copies, [pack], [((N_DEV,) + pack.shape, F32)], fanout=N_DEV - 1)
    reducer.join("b", after=everyone[-1:])

    grads, delta, new_m, new_v = {}, {}, {}, {}
    big_done = []
    for grp, red in zip(REDUCE_GROUPS, reducer.reduced):
        row0 = 0
        for n in grp:
            w_n = shard_view(w[n], n)
            outs = _adamw("adamw_" + n, w_n, red, row0, shard_view(m[n], n), shard_view(v[n], n), after=everyone[-1:])
            grads[n], delta[n], new_m[n], new_v[n] = (shard_unview(o, n) for o in outs)
            big_done.append(outs[1])
            row0 += w_n.shape[0]

    send_sems, recv_sems, packs, landed, _ = everyone
    packs, landed = _split_wait("small_wait", _everyone_copies, send_sems, recv_sems, packs, landed, big_done)
    mine = 4 * lax.axis_index("x") + 2 * lax.axis_index("y") + lax.axis_index("c")
    summed = _sum_devices(lax.dynamic_update_slice(landed[0], packs[0][None], (mine, 0, 0)))
    g_small = dict(zip(SMALL + ("loss",), _unpack_small(summed, small_like)))
    loss = g_small.pop("loss").reshape(())
    def two_d(a):
        return a.reshape(-1, a.shape[-1])

    updated = _adamw_small(*([two_d(a[n]) for n in SMALL] for a in (ws, g_small, ms, vs)))
    for n, outs in zip(SMALL, updated):
        grads[n], delta[n], new_m[n], new_v[n] = (_small_view(o.reshape(ws[n].shape), n) for o in outs)

    return (loss, grad_x[None], *[grads[n] for n in WEIGHTS], *[delta[n] for n in WEIGHTS],
            *[new_m[n] for n in WEIGHTS], *[new_v[n] for n in WEIGHTS])
```

```python
import functools
import math

import jax
import jax.numpy as jnp
from jax import lax
from jax.experimental import pallas as pl
from jax.experimental.pallas import tpu as pltpu

F32 = jnp.float32
BF16 = jnp.bfloat16
SDS = jax.ShapeDtypeStruct
BS = pl.BlockSpec
MESH = pl.DeviceIdType.MESH

D_MODEL = 1024
D_FF = 2816
N_SHARD = 4
FF_SH = D_FF // N_SHARD
D_POOL = 512
POOL_WINDOWS = (2, 4, 8, 16)
POOL_GROUP = 128
D_SSM = 256
SSM_GROUPS = 16
SSM_GROUP = 16
SSM_STATE = 64
SSM_CH = SSM_GROUPS * SSM_STATE
N_HEADS = 4
HEAD_DIM = 256
EPS = 1e-6
ADAM_LR, ADAM_B1, ADAM_B2, ADAM_EPS, ADAM_WD, ADAM_STEP = 0.001, 0.9, 0.999, 1e-08, 0.01, 10

VMEM_LIMIT_V7X = 58 * 1024 * 1024
TM = 512

NN = (((1,), (0,)), ((), ()))
NT = (((1,), (1,)), ((), ()))
TN = (((0,), (0,)), ((), ()))


def _params(*sem):
    return pltpu.CompilerParams(dimension_semantics=sem if sem else None, vmem_limit_bytes=VMEM_LIMIT_V7X)


def _dot(a, b, dims=NN):
    return lax.dot_general(a.astype(BF16), b.astype(BF16), dims, preferred_element_type=F32)


def _sigmoid(v):
    return pl.reciprocal(1.0 + jnp.exp(-v), approx=True)


def _block_dims(spec):
    return tuple(d for d in spec.block_shape if d is not None)


def _after_operands(after):
    return list(after), [BS(memory_space=pl.ANY)] * len(after)


def _mm(name, pairs, *, grid, out_shape, out_spec, red_axis=None, extras=(), epilogue=None, after=()):
    n_pairs, n_extra = len(pairs), len(extras)
    n_red = grid[red_axis] if red_axis is not None else 1
    dims = [p[4] for p in pairs]

    def body(*refs):
        ab = refs[:2 * n_pairs]
        ex = refs[2 * n_pairs:2 * n_pairs + n_extra]
        o_ref = refs[2 * n_pairs + n_extra + len(after)]

        def partial():
            acc = None
            for p in range(n_pairs):
                t = _dot(ab[2 * p][...], ab[2 * p + 1][...], dims[p])
                acc = t if acc is None else acc + t
            return acc

        def finish(acc):
            res = epilogue(acc, *[e[...] for e in ex]) if epilogue is not None else acc
            o_ref[...] = res.astype(o_ref.dtype)

        if n_red == 1:
            finish(partial())
        else:
            acc_ref = refs[-1]
            k = pl.program_id(red_axis)

            @pl.when(k == 0)
            def _():
                acc_ref[...] = jnp.zeros_like(acc_ref)

            acc_ref[...] += partial()

            @pl.when(k == n_red - 1)
            def _():
                finish(acc_ref[...])

    operands, in_specs = [], []
    for a, a_spec, b, b_spec, _ in pairs:
        operands += [a, b]
        in_specs += [a_spec, b_spec]
    for e, e_spec in extras:
        operands.append(e)
        in_specs.append(e_spec)
    after_ops, after_specs = _after_operands(after)
    operands += after_ops
    in_specs += after_specs
    scratch = [pltpu.VMEM(_block_dims(out_spec), F32)] if n_red > 1 else []
    sem = tuple("arbitrary" if ax == red_axis else "parallel" for ax in range(len(grid)))
    return pl.pallas_call(body, out_shape=out_shape, grid=grid, in_specs=in_specs, out_specs=out_spec,
                          scratch_shapes=scratch, name=name, compiler_params=_params(*sem))(*operands)


def _rmsnorm(name, h, gain, tm, after=()):
    t, d = h.shape
    after_ops, after_specs = _after_operands(after)

    def body(h_ref, g_ref, *rest):
        u_ref = rest[-1]
        hv = h_ref[...]
        r = lax.rsqrt(jnp.mean(hv * hv, axis=-1, keepdims=True) + EPS)
        u_ref[...] = ((hv * r) * g_ref[...]).astype(u_ref.dtype)

    return pl.pallas_call(
        body, out_shape=SDS((t, d), BF16), grid=(t // tm,),
        in_specs=[BS((tm, d), lambda i: (i, 0)), BS((1, d), lambda i: (0, 0))] + after_specs,
        out_specs=BS((tm, d), lambda i: (i, 0)), name=name, compiler_params=_params("parallel"))(h, gain, *after_ops)


def _rmsnorm_bwd(name, h, gain, du, dh_in, tm):
    t, d = h.shape
    has_in = dh_in is not None

    def body(*refs):
        if has_in:
            h_ref, g_ref, du_ref, dhin_ref, dh_ref, dhb_ref, dg_ref = refs
        else:
            h_ref, g_ref, du_ref, dh_ref, dhb_ref, dg_ref = refs
        i = pl.program_id(0)
        hv = h_ref[...]
        r = lax.rsqrt(jnp.mean(hv * hv, axis=-1, keepdims=True) + EPS)
        n = hv * r
        duv = du_ref[...].astype(F32)
        dn = duv * g_ref[...]
        dh = r * (dn - n * jnp.mean(dn * n, axis=-1, keepdims=True))
        if has_in:
            dh = dhin_ref[...] + dh
        dh_ref[...] = dh
        dhb_ref[...] = dh.astype(BF16)

        @pl.when(i == 0)
        def _():
            dg_ref[...] = jnp.zeros_like(dg_ref)

        dg_ref[...] += jnp.sum(duv * n, axis=0, keepdims=True)

    row = BS((tm, d), lambda i: (i, 0))
    vec = BS((1, d), lambda i: (0, 0))
    operands = [h, gain, du] + ([dh_in] if has_in else [])
    in_specs = [row, vec, row] + ([row] if has_in else [])
    return pl.pallas_call(
        body, out_shape=(SDS((t, d), F32), SDS((t, d), BF16), SDS((1, d), F32)), grid=(t // tm,),
        in_specs=in_specs, out_specs=(row, row, vec), name=name, compiler_params=_params("arbitrary"))(*operands)


def _loss_head_tile(i, hv, g_ref, t_ref, loss_ref, dh_ref, dhb_ref, dg_ref):
    g = g_ref[...]
    r = lax.rsqrt(jnp.mean(hv * hv, axis=-1, keepdims=True) + EPS)
    n = hv * r
    err = n * g - t_ref[...]
    dy = err * (1.0 / hv.shape[-1])
    dn = dy * g
    dh = r * (dn - n * jnp.mean(dn * n, axis=-1, keepdims=True))
    dh_ref[...] = dh
    dhb_ref[...] = dh.astype(BF16)

    @pl.when(i == 0)
    def _():
        dg_ref[...] = jnp.zeros_like(dg_ref)
        loss_ref[...] = jnp.zeros_like(loss_ref)

    dg_ref[...] += jnp.sum(dy * n, axis=0, keepdims=True)
    part = 0.5 * jnp.sum(jnp.mean(err * err, axis=-1, keepdims=True), axis=0, keepdims=True)
    loss_ref[...] += jnp.broadcast_to(part, loss_ref.shape)


def _norm_tile(h, g_ref, u_ref):
    r = lax.rsqrt(jnp.mean(h * h, axis=-1, keepdims=True) + EPS)
    u_ref[...] = ((h * r) * g_ref[...]).astype(u_ref.dtype)


FFN_BLOCK = D_FF // 2


def _ffn_up(name, u, w_f, tm, after=()):
    t, d = u.shape
    after_ops, after_specs = _after_operands(after)

    def body(u_ref, wg_ref, wu_ref, *rest):
        pg_ref, pu_ref, a_ref = rest[len(after_ops):]
        uv = u_ref[...]
        for lo in range(0, D_FF, FFN_BLOCK):
            cols = slice(lo, lo + FFN_BLOCK)
            g = _dot(uv, wg_ref[cols, :], NT)
            up = _dot(uv, wu_ref[cols, :], NT)
            sg = _sigmoid(g)
            silu = g * sg
            a_ref[:, cols] = (silu * up).astype(BF16)
            pu_ref[:, cols] = (0.5 * silu).astype(BF16)
            pg_ref[:, cols] = (0.5 * sg * (1.0 + g * (1.0 - sg)) * up).astype(BF16)

    hid = BS((tm, D_FF), lambda i: (i, 0))
    shape = SDS((t, D_FF), BF16)
    whole = BS((D_FF, d), lambda i: (0, 0))
    return pl.pallas_call(
        body, out_shape=(shape, shape, shape), grid=(t // tm,),
        in_specs=[BS((tm, d), lambda i: (i, 0)), whole, whole] + after_specs,
        out_specs=(hid, hid, hid), name=name,
        compiler_params=_params("parallel"))(u, w_f["gate"], w_f["up"], *after_ops)


def _ffn_down(name, a, w_f, resid, tm, next_gain=None, head=None):
    t, d = resid.shape
    row = BS((tm, d), lambda i: (i, 0))
    vec = BS((1, d), lambda i: (0, 0))

    def body(a_ref, w_ref, res_ref, *rest):
        h = res_ref[...] + 0.5 * _dot(a_ref[...], w_ref[...])
        if head is not None:
            _loss_head_tile(pl.program_id(0), h, *rest)
        else:
            g_ref, h_ref, u_ref = rest
            h_ref[...] = h
            _norm_tile(h, g_ref, u_ref)

    if head is not None:
        extra, extra_specs = list(head), [vec, row]
        out_shape = (SDS((1, 128), F32), SDS((t, d), F32), SDS((t, d), BF16), SDS((1, d), F32))
        out_specs = (BS((1, 128), lambda i: (0, 0)), row, row, vec)
    else:
        extra, extra_specs = [next_gain], [vec]
        out_shape = (SDS((t, d), F32), SDS((t, d), BF16))
        out_specs = (row, row)
    return pl.pallas_call(
        body, out_shape=out_shape, grid=(t // tm,),
        in_specs=[BS((tm, D_FF), lambda i: (i, 0)), BS((D_FF, d), lambda i: (0, 0)), row] + extra_specs,
        out_specs=out_specs, name=name,
        compiler_params=_params("arbitrary" if head is not None else "parallel"))(a, w_f["down"], resid, *extra)


def _mix_in(u, w_t, tm, after=()):
    t, d = u.shape
    after_ops, after_specs = _after_operands(after)

    def body(u_ref, w_ref, *rest):
        o_ref = rest[-1]
        uv = u_ref[...]
        for lo in range(0, D_FF, FFN_BLOCK):
            o_ref[:, lo:lo + FFN_BLOCK] = _dot(uv, w_ref[lo:lo + FFN_BLOCK, :], NT)

    return pl.pallas_call(
        body, out_shape=SDS((t, D_FF), F32), grid=(t // tm,),
        in_specs=[BS((tm, d), lambda i: (i, 0)), BS((D_FF, d), lambda i: (0, 0))] + after_specs,
        out_specs=BS((tm, D_FF), lambda i: (i, 0)), name="mix_in",
        compiler_params=_params("parallel"))(u, w_t, *after_ops)


def _mm_resid_norm(name, a, b, resid, next_gain, tm):
    t, d = resid.shape
    tm = min(2 * tm, t)

    def body(a_ref, b_ref, res_ref, g_ref, h_ref, u_ref):
        h = res_ref[...] + _dot(a_ref[...], b_ref[...])
        h_ref[...] = h
        _norm_tile(h, g_ref, u_ref)

    row = BS((tm, d), lambda i: (i, 0))
    return pl.pallas_call(
        body, out_shape=(SDS((t, d), F32), SDS((t, d), BF16)), grid=(t // tm,),
        in_specs=[BS((tm, a.shape[1]), lambda i: (i, 0)), BS(b.shape, lambda i: (0, 0)), row, BS((1, d), lambda i: (0, 0))],
        out_specs=(row, row), name=name, compiler_params=_params("parallel"))(a, b, resid, next_gain)


def _ffn_bwd_act(name, dh_b, w_f, pg, pu, tm, after=()):
    t, d = dh_b.shape
    after_ops, after_specs = _after_operands(after)

    def body(dh_ref, wd_ref, pg_ref, pu_ref, *rest):
        dg_ref, dup_ref = rest[len(after_ops):]
        dh = dh_ref[...]
        for lo in range(0, D_FF, FFN_BLOCK):
            cols = slice(lo, lo + FFN_BLOCK)
            da = _dot(dh, wd_ref[cols, :], NT)
            dg_ref[:, cols] = (da * pg_ref[:, cols].astype(F32)).astype(BF16)
            dup_ref[:, cols] = (da * pu_ref[:, cols].astype(F32)).astype(BF16)

    hid = BS((tm, D_FF), lambda i: (i, 0))
    shape = SDS((t, D_FF), BF16)
    return pl.pallas_call(
        body, out_shape=(shape, shape), grid=(t // tm,),
        in_specs=[BS((tm, d), lambda i: (i, 0)), BS((D_FF, d), lambda i: (0, 0)), hid, hid] + after_specs,
        out_specs=(hid, hid), name=name,
        compiler_params=_params("parallel"))(dh_b, w_f["down"], pg, pu, *after_ops)


def _ffn_dw(name, u, dg, dup, a, dh_b, tm):
    t, d = u.shape
    n_t = t // tm

    def body(u_ref, dg_ref, dup_ref, a_ref, dh_ref, og_ref, ou_ref, od_ref, acc):
        i = pl.program_id(1)

        @pl.when(i == 0)
        def _():
            acc[...] = jnp.zeros_like(acc)

        uv = u_ref[...]
        acc[0] += _dot(dg_ref[...], uv, TN)
        acc[1] += _dot(dup_ref[...], uv, TN)
        acc[2] += _dot(a_ref[...], dh_ref[...], TN)

        @pl.when(i == n_t - 1)
        def _():
            og_ref[...] = acc[0].astype(BF16)
            ou_ref[...] = acc[1].astype(BF16)
            od_ref[...] = (0.5 * acc[2]).astype(BF16)

    hid = BS((tm, FFN_BLOCK), lambda j, i: (i, j))
    row = BS((tm, d), lambda j, i: (i, 0))
    out = BS((FFN_BLOCK, d), lambda j, i: (j, 0))
    shape = SDS((D_FF, d), BF16)
    return pl.pallas_call(
        body, out_shape=(shape, shape, shape), grid=(D_FF // FFN_BLOCK, n_t),
        in_specs=[row, hid, hid, hid, row], out_specs=(out, out, out),
        scratch_shapes=[pltpu.VMEM((3, FFN_BLOCK, d), F32)],
        name=name, compiler_params=_params("parallel", "arbitrary"))(u, dg, dup, a, dh_b)


def _norm_bwd_tile(i, du, h_ref, g_ref, dhin_ref, dh_ref, dhb_ref, dg_ref):
    hv = h_ref[...]
    r = lax.rsqrt(jnp.mean(hv * hv, axis=-1, keepdims=True) + EPS)
    n = hv * r
    dn = du * g_ref[...]
    dh = dhin_ref[...] + r * (dn - n * jnp.mean(dn * n, axis=-1, keepdims=True))
    dh_ref[...] = dh
    dhb_ref[...] = dh.astype(BF16)

    @pl.when(i == 0)
    def _():
        dg_ref[...] = jnp.zeros_like(dg_ref)

    dg_ref[...] += jnp.sum(du * n, axis=0, keepdims=True)


def _norm_bwd_specs(tm):
    row = BS((tm, D_MODEL), lambda i: (i, 0))
    vec = BS((1, D_MODEL), lambda i: (0, 0))
    return [row, vec, row], (row, row, vec)


def _norm_bwd_shapes(t):
    return SDS((t, D_MODEL), F32), SDS((t, D_MODEL), BF16), SDS((1, D_MODEL), F32)


def _ffn_dx(name, dg, dup, w_f, h, gain, dh_in, tm, after=()):
    t = dg.shape[0]
    tm = tm // 2
    after_ops, after_specs = _after_operands(after)

    def body(dg_ref, dup_ref, wg_ref, wu_ref, h_ref, g_ref, dhin_ref, *rest):
        du = _dot(dg_ref[...], wg_ref[...]) + _dot(dup_ref[...], wu_ref[...])
        _norm_bwd_tile(pl.program_id(0), du, h_ref, g_ref, dhin_ref, *rest[len(after_ops):])

    hid = BS((tm, D_FF), lambda i: (i, 0))
    whole = BS((D_FF, D_MODEL), lambda i: (0, 0))
    norm_in, norm_out = _norm_bwd_specs(tm)
    return pl.pallas_call(
        body, out_shape=_norm_bwd_shapes(t), grid=(t // tm,),
        in_specs=[hid, hid, whole, whole] + norm_in + after_specs, out_specs=norm_out, name=name,
        compiler_params=_params("arbitrary"))(dg, dup, w_f["gate"], w_f["up"], h, gain, dh_in, *after_ops)


def _mm_norm_bwd(name, a, b, dims, h, gain, dh_in, tm):
    t = a.shape[0]

    def body(a_ref, b_ref, h_ref, g_ref, dhin_ref, *outs):
        _norm_bwd_tile(pl.program_id(0), _dot(a_ref[...], b_ref[...], dims), h_ref, g_ref, dhin_ref, *outs)

    norm_in, norm_out = _norm_bwd_specs(tm)
    return pl.pallas_call(
        body, out_shape=_norm_bwd_shapes(t), grid=(t // tm,),
        in_specs=[BS((tm, a.shape[1]), lambda i: (i, 0)), BS(b.shape, lambda i: (0, 0))] + norm_in,
        out_specs=norm_out, name=name, compiler_params=_params("arbitrary"))(a, b, h, gain, dh_in)


def _plain_mm(name, a, b, dims, out_dtype, tm, resid=None, after=()):
    t = a.shape[0]
    tm = min(2 * tm, t)
    n = b.shape[1] if dims == NN else b.shape[0]
    extras = [(resid, BS((tm, n), lambda i: (i, 0)))] if resid is not None else []
    epi = (lambda acc, res: res + acc) if resid is not None else None
    return _mm(name, [(a, BS((tm, a.shape[1]), lambda i: (i, 0)), b, BS(b.shape, lambda i: (0, 0)), dims)],
               grid=(t // tm,), out_shape=SDS((t, n), out_dtype), out_spec=BS((tm, n), lambda i: (i, 0)),
               extras=extras, epilogue=epi, after=after)


def _dw_mm(name, a, b, tm, out_dtype=BF16, after=()):
    t, k = a.shape
    n = b.shape[1]
    tm = min(2 * tm, t)
    return _mm(name, [(a, BS((tm, k), lambda i: (i, 0)), b, BS((tm, n), lambda i: (i, 0)), TN)],
               grid=(t // tm,), red_axis=0, out_shape=SDS((k, n), out_dtype), out_spec=BS((k, n), lambda i: (0, 0)),
               after=after)


POOL_CHUNK = 256
POOL_HALO = 8


def _window_sum(v, width, lead):
    n = v.shape[0]
    s = v
    k = 1
    while k < width:
        s = s + pltpu.roll(s, n - k, 0)
        k *= 2
    return pltpu.roll(s, lead, 0) if lead else s


def _pool_count(base, left, right, t, shape):
    pos = base + lax.broadcasted_iota(jnp.int32, shape, 0)
    lo = jnp.maximum(pos - left, 0)
    hi = jnp.minimum(pos + right + 1, t)
    return (hi - lo).astype(F32)


def _pool_fwd(proj, pool_w, pool_scale):
    t = proj.shape[0]
    c, h = POOL_CHUNK, POOL_HALO
    n_chunks = t // c

    def body(proj_hbm, pw_ref, sc_ref, pooled_ref, mixed_ref, ms_ref, pad_ref, sem):
        cp = pltpu.make_async_copy(proj_hbm.at[:, pl.ds(0, D_POOL)], pad_ref.at[pl.ds(h, t), :], sem)
        cp.start()
        pad_ref[pl.ds(0, h), :] = jnp.zeros((h, D_POOL), F32)
        pad_ref[pl.ds(t + h, h), :] = jnp.zeros((h, D_POOL), F32)
        cp.wait()
        for g, width in enumerate(POOL_WINDOWS):
            left = width // 2
            right = width - 1 - left
            cols = slice(g * POOL_GROUP, (g + 1) * POOL_GROUP)
            wmat = pw_ref[g].astype(BF16)
            scale = sc_ref[:, cols]

            def chunk(ci, carry, left=left, right=right, width=width, cols=cols, wmat=wmat, scale=scale):
                base = pl.multiple_of(ci * c, c)
                v = pad_ref[pl.ds(base, c + 2 * h), cols]
                win = _window_sum(v, width, left)[h:h + c]
                cnt = _pool_count(base, left, right, t, (c, POOL_GROUP))
                pooled = (win / cnt - v[h:h + c]).astype(BF16)
                mixed = _dot(pooled, wmat)
                pooled_ref[pl.ds(base, c), cols] = pooled
                mixed_ref[pl.ds(base, c), cols] = mixed.astype(BF16)
                ms_ref[pl.ds(base, c), cols] = (mixed * scale).astype(BF16)
                return carry

            lax.fori_loop(0, n_chunks, chunk, 0)

    vm = BS(memory_space=pltpu.VMEM)
    shape = SDS((t, D_POOL), BF16)
    return pl.pallas_call(
        body, out_shape=(shape, shape, shape),
        in_specs=[BS(memory_space=pl.ANY), vm, vm], out_specs=(vm, vm, vm),
        scratch_shapes=[pltpu.VMEM((t + 2 * h, D_POOL), F32), pltpu.SemaphoreType.DMA],
        name="pool_fwd", compiler_params=_params())(proj, pool_w, pool_scale)


def _pool_bwd(d_ms, mixed, pooled, pool_w, pool_scale):
    t = d_ms.shape[0]
    c, h = POOL_CHUNK, POOL_HALO
    n_chunks = t // c

    def body(dms_ref, mixed_ref, pooled_ref, pw_ref, sc_ref, dp_ref, dsc_ref, dpw_ref, pad_ref):
        pad_ref[pl.ds(0, h), :] = jnp.zeros((h, D_POOL), F32)
        pad_ref[pl.ds(t + h, h), :] = jnp.zeros((h, D_POOL), F32)
        for g, width in enumerate(POOL_WINDOWS):
            left = width // 2
            right = width - 1 - left
            cols = slice(g * POOL_GROUP, (g + 1) * POOL_GROUP)
            wmat = pw_ref[g].astype(BF16)
            scale = sc_ref[:, cols]

            def first(ci, carry, left=left, right=right, cols=cols, wmat=wmat, scale=scale):
                dsc, dpw = carry
                base = pl.multiple_of(ci * c, c)
                dms = dms_ref[pl.ds(base, c), cols].astype(F32)
                dsc = dsc + jnp.sum(dms * mixed_ref[pl.ds(base, c), cols].astype(F32), axis=0, keepdims=True)
                dmix = (dms * scale).astype(BF16)
                dpw = dpw + _dot(pooled_ref[pl.ds(base, c), cols], dmix, TN)
                dpooled = _dot(dmix, wmat, NT)
                cnt = _pool_count(base, left, right, t, (c, POOL_GROUP))
                pad_ref[pl.ds(base + h, c), cols] = dpooled / cnt
                return dsc, dpw

            dsc, dpw = lax.fori_loop(0, n_chunks, first,
                                     (jnp.zeros((1, POOL_GROUP), F32), jnp.zeros((POOL_GROUP, POOL_GROUP), F32)))
            dsc_ref[:, cols] = dsc
            dpw_ref[g] = dpw

            def second(ci, carry, left=left, right=right, width=width, cols=cols):
                base = pl.multiple_of(ci * c, c)
                v = pad_ref[pl.ds(base, c + 2 * h), cols]
                win = _window_sum(v, width, right)[h:h + c]
                cnt = _pool_count(base, left, right, t, (c, POOL_GROUP))
                dp_ref[pl.ds(base, c), cols] = (win - v[h:h + c] * cnt).astype(BF16)
                return carry

            lax.fori_loop(0, n_chunks, second, 0)

    vm = BS(memory_space=pltpu.VMEM)
    return pl.pallas_call(
        body, out_shape=(SDS((t, D_POOL), BF16), SDS((1, D_POOL), F32), SDS((4, POOL_GROUP, POOL_GROUP), F32)),
        in_specs=[vm] * 5, out_specs=(vm, vm, vm),
        scratch_shapes=[pltpu.VMEM((t + 2 * h, D_POOL), F32)],
        name="pool_bwd", compiler_params=_params())(d_ms, mixed, pooled, pool_w, pool_scale)


SSM_ROWS = 2 * SSM_GROUPS * SSM_GROUP
SSM_HALF = SSM_GROUPS * SSM_GROUP


def _ssm_zoh(a_r, a_i, ldt):
    dt = jnp.exp(ldt)
    mag = jnp.exp(dt * a_r)
    ang = dt * a_i
    cs, sn = jnp.cos(ang), jnp.sin(ang)
    abr, abi = mag * cs, mag * sn
    den = a_r * a_r + a_i * a_i
    nr = abr - 1.0
    qr = (nr * a_r + abi * a_i) / den
    qi = (abi * a_r - nr * a_i) / den
    return dt, mag, cs, sn, abr, abi, den, nr, qr, qi


def _ssm_group_mask():
    row = lax.broadcasted_iota(jnp.int32, (SSM_HALF, SSM_CH), 0)
    col = lax.broadcasted_iota(jnp.int32, (SSM_HALF, SSM_CH), 1)
    return (row // SSM_GROUP) == (col // SSM_STATE)


def _ssm_prep(a_r, a_i, ldt, b_r, b_i, c_r, c_i, after=()):
    after_ops, after_specs = _after_operands(after)

    def body(ar_ref, ai_ref, ldt_ref, br_ref, bi_ref, cr_ref, ci_ref, *rest):
        abr_ref, abi_ref, win_ref, wint_ref, woutt_ref, wout_ref = rest[len(after_ops):]
        *_, abr, abi, _, _, qr, qi = _ssm_zoh(ar_ref[...], ai_ref[...], ldt_ref[...])
        abr_ref[...] = abr
        abi_ref[...] = abi
        b_r, b_i = br_ref[...], bi_ref[...]
        bbr = qr * b_r - qi * b_i
        bbi = qr * b_i + qi * b_r
        mask = _ssm_group_mask()
        state = lax.broadcasted_iota(jnp.int32, (SSM_STATE, SSM_CH), 0)
        col = lax.broadcasted_iota(jnp.int32, (SSM_STATE, SSM_CH), 1)
        every_group = (col % SSM_STATE == state).astype(BF16)

        def spread(x):
            return jnp.where(mask, _dot(x, every_group), 0.0)

        for d in range(2):
            rows = slice(d * SSM_HALF, (d + 1) * SSM_HALF)
            for half, x_in, x_out in ((0, bbr[rows], cr_ref[rows, :]), (1, bbi[rows], -ci_ref[rows, :])):
                cols = slice(half * SSM_CH, (half + 1) * SSM_CH)
                m_in, m_out = spread(x_in), spread(x_out)
                win_ref[d, :, cols] = m_in.astype(BF16)
                wint_ref[d, cols, :] = m_in.T.astype(BF16)
                woutt_ref[d, :, cols] = m_out.astype(BF16)
                wout_ref[d, cols, :] = m_out.T.astype(BF16)

    vm = BS(memory_space=pltpu.VMEM)
    vec = SDS((SSM_ROWS, SSM_STATE), F32)
    wide = SDS((2, SSM_HALF, 2 * SSM_CH), BF16)
    tall = SDS((2, 2 * SSM_CH, SSM_HALF), BF16)
    return pl.pallas_call(body, out_shape=(vec, vec, wide, tall, wide, tall), in_specs=[vm] * 7 + after_specs,
                          out_specs=(vm,) * 6, name="ssm_prep",
                          compiler_params=_params())(a_r, a_i, ldt, b_r, b_i, c_r, c_i, *after_ops)


def _ssm_prep_bwd(a_r, a_i, ldt, b_r, b_i, d_abr, d_abi, d_win, d_woutt):
    def body(ar_ref, ai_ref, ldt_ref, br_ref, bi_ref, *rest):
        (dabr_refs, dabi_refs, dwin_refs, dwoutt_refs), outs = [rest[2 * k:2 * k + 2] for k in range(4)], rest[8:]
        dar_ref, dai_ref, dldt_ref, dbr_ref, dbi_ref, dcr_ref, dci_ref = outs
        a_r, a_i = ar_ref[...], ai_ref[...]
        dt, mag, cs, sn, abr, abi, den, nr, qr, qi = _ssm_zoh(a_r, a_i, ldt_ref[...])
        mask = _ssm_group_mask()
        col = lax.broadcasted_iota(jnp.int32, (SSM_CH, SSM_STATE), 0)
        state = lax.broadcasted_iota(jnp.int32, (SSM_CH, SSM_STATE), 1)
        own_state = (col % SSM_STATE == state).astype(BF16)

        def pick(dense):
            m = jnp.where(mask, dense, 0.0)
            hi = m.astype(BF16)
            lo = m - hi.astype(F32)
            return _dot(hi, own_state) + _dot(lo, own_state)

        def picked(refs, half):
            cols = slice(half * SSM_CH, (half + 1) * SSM_CH)
            return jnp.concatenate([pick(ref[:, cols]) for ref in refs], axis=0)

        first_channel = lax.broadcasted_iota(jnp.int32, (SSM_HALF, SSM_CH), 0) % SSM_GROUP == 0

        def first_rows(refs):
            return jnp.concatenate(
                [pick(jnp.where(first_channel, jnp.broadcast_to(ref[...], (SSM_HALF, SSM_CH)), 0.0)) for ref in refs], axis=0)

        g_r, g_i = picked(dwin_refs, 0), picked(dwin_refs, 1)
        dcr_ref[...] = picked(dwoutt_refs, 0)
        dci_ref[...] = -picked(dwoutt_refs, 1)
        b_r, b_i = br_ref[...], bi_ref[...]
        dbr_ref[...] = g_r * qr + g_i * qi
        dbi_ref[...] = g_i * qr - g_r * qi
        gqr = g_r * b_r + g_i * b_i
        gqi = g_i * b_r - g_r * b_i
        g_nr_num = gqr / den
        g_ni_num = gqi / den
        g_den = -(gqr * qr + gqi * qi) / den
        g_nr = g_nr_num * a_r - g_ni_num * a_i
        g_abi = g_nr_num * a_i + g_ni_num * a_r
        d_ar = g_nr_num * nr + g_ni_num * abi + 2.0 * a_r * g_den
        d_ai = g_nr_num * abi - g_ni_num * nr + 2.0 * a_i * g_den
        g_abr = first_rows(dabr_refs) + g_nr
        g_abi = first_rows(dabi_refs) + g_abi
        g_mag = g_abr * cs + g_abi * sn
        g_ang = mag * (g_abi * cs - g_abr * sn)
        g_e = g_mag * mag
        d_ar = d_ar + g_e * dt
        d_ai = d_ai + g_ang * dt
        g_dt = g_e * a_r + g_ang * a_i
        dar_ref[...] = d_ar
        dai_ref[...] = d_ai
        dldt_ref[...] = g_dt * dt

    vm = BS(memory_space=pltpu.VMEM)
    vec = SDS((SSM_ROWS, SSM_STATE), F32)
    return pl.pallas_call(body, out_shape=(vec,) * 7, in_specs=[vm] * 13, out_specs=(vm,) * 7, name="ssm_prep_bwd",
                          compiler_params=_params())(a_r, a_i, ldt, b_r, b_i, *d_abr, *d_abi, *d_win, *d_woutt)


SCAN_ROWS = 512
SCAN_SUB = 128


def _ssm_scan(name, inp, w1, a_r, a_i, w2, reverse):
    t = inp.shape[0]
    rows = min(SCAN_ROWS, t)
    n = t // rows
    n_sub = rows // SCAN_SUB
    ch = SSM_CH
    at = (lambda i: (n - 1 - i, 0)) if reverse else (lambda i: (i, 0))

    def body(in_ref, w1_ref, ar_ref, ai_ref, w2_ref, sb_ref, out_ref, cr_ref, ci_ref, k_ref, st_ref):
        i = pl.program_id(0)

        @pl.when(i == 0)
        def _():
            ar8 = jnp.broadcast_to(ar_ref[...], (8, ch))
            ai8 = jnp.broadcast_to(ai_ref[...], (8, ch))
            row = lax.broadcasted_iota(jnp.int32, (8, ch), 0)
            rank = (7 - row) if reverse else row
            powers = [(ar8, ai8)]
            for _ in range(7):
                p_r, p_i = powers[-1]
                powers.append((p_r * ar8 - p_i * ai8, p_r * ai8 + p_i * ar8))
            zero = jnp.zeros((8, ch), F32)
            for slot, k in enumerate((1, 2, 4)):
                k_ref[2 * slot] = jnp.where(rank >= k, powers[k - 1][0], zero)
                k_ref[2 * slot + 1] = jnp.where(rank >= k, powers[k - 1][1], zero)
            carry_r, carry_i = zero, zero
            for j in range(8):
                carry_r = jnp.where(rank == j, powers[j][0], carry_r)
                carry_i = jnp.where(rank == j, powers[j][1], carry_i)
            k_ref[6] = carry_r
            k_ref[7] = carry_i
            cr_ref[...] = zero
            ci_ref[...] = zero

        def group(r0, carry):
            c_r, c_i = carry
            x_r = st_ref[pl.ds(r0, 8), 0:ch]
            x_i = st_ref[pl.ds(r0, 8), ch:2 * ch]
            for slot, k in enumerate((1, 2, 4)):
                shift = (8 - k) if reverse else k
                s_r = pltpu.roll(x_r, shift, 0)
                s_i = pltpu.roll(x_i, shift, 0)
                m_r, m_i = k_ref[2 * slot], k_ref[2 * slot + 1]
                x_r, x_i = x_r + m_r * s_r - m_i * s_i, x_i + m_r * s_i + m_i * s_r
            p_r, p_i = k_ref[6], k_ref[7]
            x_r, x_i = x_r + p_r * c_r - p_i * c_i, x_i + p_r * c_i + p_i * c_r
            st_ref[pl.ds(r0, 8), 0:ch] = x_r
            st_ref[pl.ds(r0, 8), ch:2 * ch] = x_i
            last = 0 if reverse else 7
            return (jnp.broadcast_to(x_r[last:last + 1, :], (8, ch)), jnp.broadcast_to(x_i[last:last + 1, :], (8, ch)))

        carry = (cr_ref[...], ci_ref[...])
        for sc in (range(n_sub - 1, -1, -1) if reverse else range(n_sub)):
            part = pl.ds(sc * SCAN_SUB, SCAN_SUB)
            st_ref[part, :] = _dot(in_ref[part, :], w1_ref[...])
            for gi in range(SCAN_SUB // 8):
                g = (SCAN_SUB // 8 - 1 - gi) if reverse else gi
                carry = group(sc * SCAN_SUB + g * 8, carry)
            states = st_ref[part, :].astype(BF16)
            sb_ref[part, :] = states
            out_ref[part, :] = _dot(states, w2_ref[...])
        cr_ref[...] = carry[0]
        ci_ref[...] = carry[1]

    return pl.pallas_call(
        body, out_shape=(SDS((t, 2 * ch), BF16), SDS((t, D_SSM), F32)), grid=(n,),
        in_specs=[BS((rows, D_SSM), at), BS((D_SSM, 2 * ch), lambda i: (0, 0)), BS((1, ch), lambda i: (0, 0)),
                  BS((1, ch), lambda i: (0, 0)), BS((2 * ch, D_SSM), lambda i: (0, 0))],
        out_specs=(BS((rows, 2 * ch), at), BS((rows, D_SSM), at)),
        scratch_shapes=[pltpu.VMEM((8, ch), F32), pltpu.VMEM((8, ch), F32), pltpu.VMEM((8, 8, ch), F32),
                        pltpu.VMEM((rows, 2 * ch), F32)],
        name=name, compiler_params=_params("arbitrary"))(inp, w1, a_r, a_i, w2)


DA_ROWS = 1024


def _ssm_param_grads(name, lam, states, u, dy, reverse, after=()):
    t = lam.shape[0]
    rows = min(DA_ROWS, t)
    n = t // rows
    halo_rows = 16
    nb = rows // halo_rows
    ch = SSM_CH
    if reverse:
        halo_at = lambda i: (jnp.minimum((i + 1) * nb, t // halo_rows - 1), 0)
    else:
        halo_at = lambda i: (jnp.maximum(i * nb - 1, 0), 0)

    after_ops, after_specs = _after_operands(after)

    def body(lam_ref, x_ref, halo_ref, u_ref, dy_ref, *rest):
        dr_ref, di_ref, dwin_ref, dwoutt_ref = rest[len(after_ops):]
        i = pl.program_id(0)

        @pl.when(i == 0)
        def _():
            dr_ref[...] = jnp.zeros_like(dr_ref)
            di_ref[...] = jnp.zeros_like(di_ref)
            dwin_ref[...] = jnp.zeros_like(dwin_ref)
            dwoutt_ref[...] = jnp.zeros_like(dwoutt_ref)

        dwin_ref[...] += _dot(u_ref[...], lam_ref[...], TN)
        dwoutt_ref[...] += _dot(dy_ref[...], x_ref[...], TN)
        row = lax.broadcasted_iota(jnp.int32, (rows, ch), 0)
        if reverse:
            edge, shift, h_row, live = rows - 1, rows - 1, 0, i < n - 1
        else:
            edge, shift, h_row, live = 0, 1, halo_rows - 1, i > 0

        def neighbour(lo):
            halo = halo_ref[:, lo:lo + ch].astype(F32)[h_row:h_row + 1]
            halo = jnp.where(live, halo, 0.0)
            x = x_ref[:, lo:lo + ch].astype(F32)
            return jnp.where(row == edge, jnp.broadcast_to(halo, (rows, ch)), pltpu.roll(x, shift, 0))

        xp_r, xp_i = neighbour(0), neighbour(ch)
        l_r, l_i = lam_ref[:, 0:ch].astype(F32), lam_ref[:, ch:2 * ch].astype(F32)
        dr_ref[...] += jnp.sum(l_r * xp_r + l_i * xp_i, axis=0, keepdims=True)
        di_ref[...] += jnp.sum(l_i * xp_r - l_r * xp_i, axis=0, keepdims=True)

    blk = BS((rows, 2 * ch), lambda i: (i, 0))
    thin = BS((rows, D_SSM), lambda i: (i, 0))
    vec = BS((1, ch), lambda i: (0, 0))
    mat = BS((D_SSM, 2 * ch), lambda i: (0, 0))
    return pl.pallas_call(
        body, out_shape=(SDS((1, ch), F32), SDS((1, ch), F32), SDS((D_SSM, 2 * ch), F32), SDS((D_SSM, 2 * ch), F32)),
        grid=(n,), in_specs=[blk, blk, BS((halo_rows, 2 * ch), halo_at), thin, thin] + after_specs,
        out_specs=(vec, vec, mat, mat),
        name=name, compiler_params=_params("arbitrary"))(lam, states, states, u, dy, *after_ops)


GELU_C = math.sqrt(2.0 / math.pi)
GELU_K = 0.044715


def _ssm_combine(proj, y_fwd, y_bwd, d_skip, tm, after=()):
    t = proj.shape[0]
    after_ops, after_specs = _after_operands(after)

    def body(s_ref, yf_ref, yb_ref, d_ref, *rest):
        yt_ref, g_ref = rest[len(after_ops):]
        y = s_ref[...] * d_ref[...] + yf_ref[...] + yb_ref[...]
        yt_ref[...] = y
        th = jnp.tanh(GELU_C * (y + GELU_K * y * y * y))
        g_ref[...] = (0.5 * y * (1.0 + th)).astype(BF16)

    blk = BS((tm, D_SSM), lambda i: (i, 0))
    return pl.pallas_call(
        body, out_shape=(SDS((t, D_SSM), F32), SDS((t, D_SSM), BF16)), grid=(t // tm,),
        in_specs=[BS((tm, D_SSM), lambda i: (i, D_POOL // D_SSM)), blk, blk, BS((1, D_SSM), lambda i: (0, 0))] + after_specs,
        out_specs=(blk, blk), name="ssm_combine",
        compiler_params=_params("parallel"))(proj, y_fwd, y_bwd, d_skip, *after_ops)


def _ssm_ds(proj, d_yt, du_fwd, du_bwd, d_skip, tm):
    t = proj.shape[0]

    def body(s_ref, dy_ref, duf_ref, dub_ref, d_ref, ds_ref, dd_ref):
        i = pl.program_id(0)
        dy = dy_ref[...]
        ds_ref[...] = (dy * d_ref[...] + duf_ref[...] + dub_ref[...]).astype(BF16)

        @pl.when(i == 0)
        def _():
            dd_ref[...] = jnp.zeros_like(dd_ref)

        dd_ref[...] += jnp.sum(dy * s_ref[...], axis=0, keepdims=True)

    blk = BS((tm, D_SSM), lambda i: (i, 0))
    vec = BS((1, D_SSM), lambda i: (0, 0))
    return pl.pallas_call(
        body, out_shape=(SDS((t, D_SSM), BF16), SDS((1, D_SSM), F32)), grid=(t // tm,),
        in_specs=[BS((tm, D_SSM), lambda i: (i, D_POOL // D_SSM)), blk, blk, blk, vec],
        out_specs=(blk, vec), name="ssm_ds", compiler_params=_params("arbitrary"))(proj, d_yt, du_fwd, du_bwd, d_skip)


G_POOL_AT = D_POOL + D_SSM
G_SSM_AT = G_POOL_AT + D_MODEL
E_VAL, E_GATE = D_POOL, D_POOL + D_SSM


def _merge_specs(tm):
    return [BS((tm, D_POOL), lambda i: (i, 0)), BS((tm, D_SSM), lambda i: (i, 0)),
            BS((N_SHARD, 1024, 256), lambda i: (0, 0, 0)), BS((tm, D_FF), lambda i: (i, 0))]


def _merge_parts(s, ms, yv, w_ref, proj_ref):
    lo = 256 * s
    zp = _dot(ms, w_ref[s, 0:E_VAL, :])
    zv = _dot(yv, w_ref[s, E_VAL:E_GATE, :])
    zg = _dot(yv, w_ref[s, E_GATE:, :])
    return zp, zv, zg, proj_ref[:, G_POOL_AT + lo:G_POOL_AT + lo + 256], proj_ref[:, G_SSM_AT + lo:G_SSM_AT + lo + 256]


def _mixer_merge(ms, yssm, w_e, proj, tm):
    t = ms.shape[0]

    def body(ms_ref, y_ref, w_ref, proj_ref, o_ref):
        msv, yv = ms_ref[...], y_ref[...]
        for s in range(N_SHARD):
            zp, zv, zg, gp, gs = _merge_parts(s, msv, yv, w_ref, proj_ref)
            o_ref[:, 256 * s:256 * (s + 1)] = (_sigmoid(gp) * zp + _sigmoid(gs) * zv * _sigmoid(zg)).astype(BF16)

    row = BS((tm, D_MODEL), lambda i: (i, 0))
    return pl.pallas_call(
        body, out_shape=SDS((t, D_MODEL), BF16), grid=(t // tm,), in_specs=_merge_specs(tm), out_specs=row,
        name="mixer_merge", compiler_params=_params("parallel"))(ms, yssm, w_e, proj)


def _mixer_merge_bwd(ms, yssm, w_e, proj, dmerged, tm):
    t = ms.shape[0]

    def body(ms_ref, y_ref, w_ref, proj_ref, dm_ref, dgp_ref, dgs_ref, dzp_ref, dzv_ref, dzg_ref):
        msv, yv = ms_ref[...], y_ref[...]
        for s in range(N_SHARD):
            cols = slice(256 * s, 256 * (s + 1))
            zp, zv, zg, gp, gs = _merge_parts(s, msv, yv, w_ref, proj_ref)
            dm = dm_ref[:, cols].astype(F32)
            sp, ss, sg = _sigmoid(gp), _sigmoid(gs), _sigmoid(zg)
            dgp_ref[:, cols] = (dm * zp * sp * (1.0 - sp)).astype(BF16)
            dgs_ref[:, cols] = (dm * zv * sg * ss * (1.0 - ss)).astype(BF16)
            dzp_ref[:, cols] = (dm * sp).astype(BF16)
            dz = dm * ss
            dzv_ref[:, cols] = (dz * sg).astype(BF16)
            dzg_ref[:, cols] = (dz * zv * sg * (1.0 - sg)).astype(BF16)

    row = BS((tm, D_MODEL), lambda i: (i, 0))
    shape = SDS((t, D_MODEL), BF16)
    return pl.pallas_call(
        body, out_shape=(shape,) * 5, grid=(t // tm,), in_specs=_merge_specs(tm) + [row],
        out_specs=(row,) * 5, name="mixer_merge_bwd",
        compiler_params=_params("parallel"))(ms, yssm, w_e, proj, dmerged)


def _mixer_dw(ms, yssm, dzp, dzv, dzg, tm):
    t = ms.shape[0]
    tm = min(2 * tm, t)
    n_t = t // tm

    def body(ms_ref, y_ref, dzp_ref, dzv_ref, dzg_ref, o_ref, acc):
        i = pl.program_id(0)

        @pl.when(i == 0)
        def _():
            acc[...] = jnp.zeros_like(acc)

        msv, yv = ms_ref[...], y_ref[...]
        for s in range(N_SHARD):
            cols = slice(256 * s, 256 * (s + 1))
            acc[s, 0:E_VAL, :] += _dot(msv, dzp_ref[:, cols], TN)
            acc[s, E_VAL:E_GATE, :] += _dot(yv, dzv_ref[:, cols], TN)
            acc[s, E_GATE:, :] += _dot(yv, dzg_ref[:, cols], TN)

        @pl.when(i == n_t - 1)
        def _():
            o_ref[...] = acc[...].astype(BF16)

    row = BS((tm, D_MODEL), lambda i: (i, 0))
    full = BS((N_SHARD, 1024, 256), lambda i: (0, 0, 0))
    return pl.pallas_call(
        body, out_shape=SDS((N_SHARD, 1024, 256), BF16), grid=(n_t,),
        in_specs=[BS((tm, D_POOL), lambda i: (i, 0)), BS((tm, D_SSM), lambda i: (i, 0)), row, row, row],
        out_specs=full, scratch_shapes=[pltpu.VMEM((N_SHARD, 1024, 256), F32)],
        name="mixer_dw", compiler_params=_params("arbitrary"))(ms, yssm, dzp, dzv, dzg)


def _mixer_dx(dzp, dzv, dzg, w_e, y_total, tm):
    t = dzp.shape[0]

    def body(dzp_ref, dzv_ref, dzg_ref, w_ref, yt_ref, dms_ref, dy_ref):
        acc_ms, acc_y = None, None
        for s in range(N_SHARD):
            cols = slice(256 * s, 256 * (s + 1))
            part_ms = _dot(dzp_ref[:, cols], w_ref[s, 0:E_VAL, :], NT)
            part_y = _dot(dzv_ref[:, cols], w_ref[s, E_VAL:E_GATE, :], NT) + _dot(dzg_ref[:, cols], w_ref[s, E_GATE:, :], NT)
            acc_ms = part_ms if s == 0 else acc_ms + part_ms
            acc_y = part_y if s == 0 else acc_y + part_y
        dms_ref[...] = acc_ms.astype(BF16)
        y = yt_ref[...]
        th = jnp.tanh(GELU_C * (y + GELU_K * y * y * y))
        dgelu = 0.5 * (1.0 + th) + 0.5 * y * (1.0 - th * th) * GELU_C * (1.0 + 3.0 * GELU_K * y * y)
        dy_ref[...] = acc_y * dgelu

    row = BS((tm, D_MODEL), lambda i: (i, 0))
    return pl.pallas_call(
        body, out_shape=(SDS((t, D_POOL), BF16), SDS((t, D_SSM), F32)), grid=(t // tm,),
        in_specs=[row, row, row, BS((N_SHARD, 1024, 256), lambda i: (0, 0, 0)), BS((tm, D_SSM), lambda i: (i, 0))],
        out_specs=(BS((tm, D_POOL), lambda i: (i, 0)), BS((tm, D_SSM), lambda i: (i, 0))),
        name="mixer_dx", compiler_params=_params("parallel"))(dzp, dzv, dzg, w_e, y_total)


def _attn_probs(q_h, k_h):
    s = _dot(q_h, k_h, NT) * (1.0 / math.sqrt(HEAD_DIM))
    e = jnp.exp(s - jnp.max(s, axis=-1, keepdims=True))
    return e / jnp.sum(e, axis=-1, keepdims=True)


def _attn_fwd(q, kv, tm):
    t = q.shape[0]
    tm = min(2 * tm, t)
    m = kv.shape[0]

    def body(q_ref, kv_ref, o_ref):
        for hd in range(N_HEADS):
            lo = hd * HEAD_DIM
            p = _attn_probs(q_ref[:, lo:lo + HEAD_DIM], kv_ref[:, lo:lo + HEAD_DIM])
            o_ref[:, lo:lo + HEAD_DIM] = _dot(p, kv_ref[:, D_MODEL + lo:D_MODEL + lo + HEAD_DIM]).astype(BF16)

    return pl.pallas_call(
        body, out_shape=SDS((t, D_MODEL), BF16), grid=(t // tm,),
        in_specs=[BS((tm, D_MODEL), lambda i: (i, 0)), BS((m, 2 * D_MODEL), lambda i: (0, 0))],
        out_specs=BS((tm, D_MODEL), lambda i: (i, 0)), name="attn_fwd", compiler_params=_params("parallel"))(q, kv)


def _attn_bwd(q, kv, d_o, tm):
    t = q.shape[0]
    m = kv.shape[0]

    def body(q_ref, kv_ref, do_ref, dq_ref, dkv_ref):
        i = pl.program_id(0)

        @pl.when(i == 0)
        def _():
            dkv_ref[...] = jnp.zeros_like(dkv_ref)

        for hd in range(N_HEADS):
            lo = hd * HEAD_DIM
            q_h = q_ref[:, lo:lo + HEAD_DIM]
            k_h = kv_ref[:, lo:lo + HEAD_DIM]
            v_h = kv_ref[:, D_MODEL + lo:D_MODEL + lo + HEAD_DIM]
            do_h = do_ref[:, lo:lo + HEAD_DIM]
            p = _attn_probs(q_h, k_h)
            dkv_ref[:, D_MODEL + lo:D_MODEL + lo + HEAD_DIM] += _dot(p, do_h, TN)
            dp = _dot(do_h, v_h, NT)
            ds = p * (dp - jnp.sum(dp * p, axis=-1, keepdims=True)) * (1.0 / math.sqrt(HEAD_DIM))
            dq_ref[:, lo:lo + HEAD_DIM] = _dot(ds, k_h).astype(BF16)
            dkv_ref[:, lo:lo + HEAD_DIM] += _dot(ds, q_h, TN)

    row = BS((tm, D_MODEL), lambda i: (i, 0))
    full = BS((m, 2 * D_MODEL), lambda i: (0, 0))
    return pl.pallas_call(
        body, out_shape=(SDS((t, D_MODEL), BF16), SDS((m, 2 * D_MODEL), F32)), grid=(t // tm,),
        in_specs=[row, full, row], out_specs=(row, full), name="attn_bwd",
        compiler_params=_params("arbitrary"))(q, kv, d_o)


TRANSPOSED = ("ffn1_w_gate", "ffn1_w_up", "ffn2_w_gate", "ffn2_w_up", "w_in")
GATHER_PHASES = {"f1a": (("ffn1_w_gate",), ("ffn1_w_up",)),
                 "f1b": (("ffn1_w_down",),),
                 "win": (("w_in",),),
                 "mix": (("w_mix_out", "w_q", "w_xo"), ("w_kv",), ("w_pool_proj", "w_glu_val", "w_glu_gate")),
                 "f2": (("ffn2_w_gate",), ("ffn2_w_up",), ("ffn2_w_down",))}
REDUCE_GROUPS = (("ffn2_w_gate",), ("ffn2_w_up",), ("ffn2_w_down",), ("w_xo",), ("w_q",), ("w_kv",), ("w_mix_out",),
                 ("w_pool_proj", "w_glu_val", "w_glu_gate"), ("w_in",), ("ffn1_w_gate",), ("ffn1_w_up",), ("ffn1_w_down",))
SMALL = ("ffn1_norm", "mix_norm", "pool_w", "pool_scale", "ssm_a_re", "ssm_a_im", "ssm_log_dt", "ssm_b_re",
         "ssm_b_im", "ssm_c_re", "ssm_c_im", "ssm_d", "xattn_norm", "mem_norm", "ffn2_norm", "final_norm")
WEIGHTS = ("ffn1_norm", "ffn1_w_gate", "ffn1_w_up", "ffn1_w_down", "mix_norm", "w_in", "pool_w", "pool_scale",
           "w_pool_proj", "ssm_a_re", "ssm_a_im", "ssm_log_dt", "ssm_b_re", "ssm_b_im", "ssm_c_re", "ssm_c_im",
           "ssm_d", "w_glu_val", "w_glu_gate", "w_mix_out", "xattn_norm", "mem_norm", "w_q", "w_kv", "w_xo",
           "ffn2_norm", "ffn2_w_gate", "ffn2_w_up", "ffn2_w_down", "final_norm")


def _small_view(a, n):
    return jnp.swapaxes(a, 3, 4) if n in ("ssm_b_re", "ssm_b_im") else a


def _device_step(x, mem, target, wts, sp, reducer=None):
    t = x.shape[0]
    tm = min(TM, t)
    g = {}

    first_gather = wts.start("f1a")
    u1 = _rmsnorm("norm_ffn1", x, sp["ffn1_norm"], tm, after=first_gather)

    def per_channel(a):
        a = a.reshape(2 * SSM_GROUPS, 1, -1)
        return jnp.broadcast_to(a, (2 * SSM_GROUPS, SSM_GROUP, a.shape[-1])).reshape(SSM_ROWS, a.shape[-1])

    ssm_a = per_channel(sp["ssm_a_re"]), per_channel(sp["ssm_a_im"]), per_channel(sp["ssm_log_dt"])
    ssm_b = sp["ssm_b_re"].reshape(SSM_ROWS, SSM_STATE), sp["ssm_b_im"].reshape(SSM_ROWS, SSM_STATE)
    abr, abi, w_in_s, w_in_s_t, w_out_s_t, w_out_s = _ssm_prep(
        *ssm_a, *ssm_b, sp["ssm_c_re"].reshape(SSM_ROWS, SSM_STATE), sp["ssm_c_im"].reshape(SSM_ROWS, SSM_STATE),
        after=first_gather)
    first_rows = (2, SSM_GROUPS, SSM_GROUP, SSM_STATE)
    a_r = abr.reshape(first_rows)[:, :, 0].reshape(2, 1, SSM_CH)
    a_i = abi.reshape(first_rows)[:, :, 0].reshape(2, 1, SSM_CH)
    mem_n = _rmsnorm("norm_mem", mem, sp["mem_norm"], mem.shape[0], after=first_gather)

    whole = (D_FF, D_MODEL)
    w_g1, w_u1 = wts.finish("f1a", [u1, w_in_s, w_in_s_t, w_out_s, w_out_s_t, a_r, a_i, mem_n])
    w_f1 = {"gate": w_g1.reshape(whole), "up": w_u1.reshape(whole)}
    down_gather = wts.start("f1b", [w_g1])
    g1, up1, a1 = _ffn_up("ffn1_up", u1, w_f1, tm, after=wts.start("win", down_gather))
    (w_dn,) = wts.finish("f1b", [a1])
    w_f1["down"] = w_dn.reshape(whole)
    h1, u2 = _ffn_down("ffn1_down", a1, w_f1, x, tm, next_gain=sp["mix_norm"])

    (w_in_g,) = wts.finish("win", [u2])
    w_in_t = w_in_g.reshape(D_FF, D_MODEL)
    proj = _mix_in(u2, w_in_t, tm, after=wts.start("f2", wts.start("mix", [w_in_g])))
    pooled, mixed, ms = _pool_fwd(proj, sp["pool_w"][0], sp["pool_scale"])

    s_in = proj[:, D_POOL:D_POOL + D_SSM].astype(BF16)
    states, y_dirs = [], []
    for dr in range(2):
        st, yd = _ssm_scan(f"ssm_scan_fwd{dr}", s_in, w_in_s[dr], a_r[dr], a_i[dr], w_out_s[dr], reverse=(dr == 1))
        states.append(st)
        y_dirs.append(yd)
    w_sq, w_kv, w_e = wts.finish("mix", y_dirs)
    w_mo, w_q, w_xo = (w_sq[:, 256 * k:256 * (k + 1)].reshape(D_MODEL, D_MODEL) for k in range(3))
    w_d = w_kv[:, None]
    y_total, yssm = _ssm_combine(proj, y_dirs[0], y_dirs[1], sp["ssm_d"], tm)

    merged = _mixer_merge(ms, yssm, w_e, proj, tm)
    h2, u3 = _mm_resid_norm("mix_out", merged, w_mo, h1, sp["xattn_norm"], tm)

    q = _plain_mm("attn_q", u3, w_q, NN, BF16, tm)
    n_mem = mem.shape[0]
    kv = _mm("attn_kv", [(mem_n, BS((n_mem, D_MODEL), lambda s: (0, 0)), w_d, BS((None, None, D_MODEL, 512), lambda s: (s, 0, 0, 0)), NN)],
             grid=(N_SHARD,), out_shape=SDS((n_mem, 2 * D_MODEL), BF16), out_spec=BS((n_mem, 512), lambda s: (0, s)))
    o = _attn_fwd(q, kv, tm)
    h3, u4 = _mm_resid_norm("attn_out", o, w_xo, h2, sp["ffn2_norm"], tm)

    w_f2 = dict(zip(("gate", "up", "down"), (a.reshape(whole) for a in wts.finish("f2", [u4]))))
    g2, up2, a2 = _ffn_up("ffn2_up", u4, w_f2, tm)
    loss, dh4, dh4_b, g["final_norm"] = _ffn_down("ffn2_down", a2, w_f2, h3, tm,
                                                  head=(sp["final_norm"].reshape(1, D_MODEL), target))

    dg2, dup2 = _ffn_bwd_act("ffn2_bwd_act", dh4_b, w_f2, g2, up2, tm)
    dw_f2 = _ffn_dw("ffn2_dw", u4, dg2, dup2, a2, dh4_b, tm)
    dh3, dh3_b, g["ffn2_norm"] = _ffn_dx("ffn2_dx", dg2, dup2, w_f2, h3, sp["ffn2_norm"], dh4, tm)

    d_o = _plain_mm("attn_out_dx", dh3_b, w_xo, NT, BF16, tm)
    dw_xo = _dw_mm("attn_out_dw", o, dh3_b, tm)
    dq, dkv = _attn_bwd(q, kv, d_o, tm)
    dw_q = _dw_mm("attn_q_dw", u3, dq, tm)
    dh2, dh2_b, g["xattn_norm"] = _mm_norm_bwd("attn_q_dx", dq, w_q, NT, h2, sp["xattn_norm"], dh3, tm)
    dw_kv = _mm("attn_kv_dw", [(mem_n, BS((n_mem, D_MODEL), lambda s: (0, 0)), dkv, BS((n_mem, 512), lambda s: (0, s)), TN)],
                grid=(N_SHARD,), out_shape=SDS((N_SHARD, D_MODEL, 512), BF16), out_spec=BS((None, D_MODEL, 512), lambda s: (s, 0, 0)))
    dmem_n = _mm("attn_kv_dx", [(dkv, BS((n_mem, 512), lambda s: (0, s)), w_d, BS((None, None, D_MODEL, 512), lambda s: (s, 0, 0, 0)), NT)],
                 grid=(N_SHARD,), red_axis=0, out_shape=SDS((n_mem, D_MODEL), F32), out_spec=BS((n_mem, D_MODEL), lambda s: (0, 0)))
    _, _, g["mem_norm"] = _rmsnorm_bwd("norm_mem_bwd", mem, sp["mem_norm"], dmem_n, None, n_mem)

    square = (N_SHARD, D_MODEL // N_SHARD, D_MODEL)
    sharded = (N_SHARD, FF_SH, D_MODEL)
    early = [a.reshape(sharded) for a in dw_f2] + [dw_xo.reshape(square), dw_q.reshape(square), dw_kv]
    swapping = reducer.swap_start("a1", early) if reducer is not None else []
    dmerged = _plain_mm("mix_out_dx", dh2_b, w_mo, NT, BF16, tm, after=swapping)
    dw_mo = _dw_mm("mix_out_dw", merged, dh2_b, tm)
    d_gp, d_gs, dzp, dzv, dzg = _mixer_merge_bwd(ms, yssm, w_e, proj, dmerged, tm)
    dw_e = _mixer_dw(ms, yssm, dzp, dzv, dzg, tm)
    d_ms, d_yt = _mixer_dx(dzp, dzv, dzg, w_e, y_total, tm)
    dp, d_scale, d_pw = _pool_bwd(d_ms, mixed, pooled, sp["pool_w"][0], sp["pool_scale"])
    g["pool_scale"] = d_scale
    g["pool_w"] = d_pw[None]

    d_yt_b = d_yt.astype(BF16)
    du_dirs, lams = [], []
    for dr in range(2):
        lam, du = _ssm_scan(f"ssm_scan_bwd{dr}", d_yt_b, w_out_s_t[dr], a_r[dr], -a_i[dr], w_in_s_t[dr], reverse=(dr == 0))
        du_dirs.append(du)
        lams.append(lam)
    ds, g["ssm_d"] = _ssm_ds(proj, d_yt, du_dirs[0], du_dirs[1], sp["ssm_d"], tm)

    d_proj = jnp.concatenate([dp, ds, d_gp, d_gs], axis=1)
    tw = min(2 * tm, t)
    dw_in_t = _mm("mix_in_dw", [(d_proj, BS((tw, D_FF // 2), lambda j, i: (i, j)), u2, BS((tw, D_MODEL), lambda j, i: (i, 0)), TN)],
                  grid=(2, t // tw), red_axis=1, out_shape=SDS((D_FF, D_MODEL), BF16), out_spec=BS((D_FF // 2, D_MODEL), lambda j, i: (j, 0)))
    dh1, dh1_b, g["mix_norm"] = _mm_norm_bwd("mix_in_dx", d_proj, w_in_t, NN, h1, sp["mix_norm"], dh2, tm)

    early += [dw_mo.reshape(square), dw_e, dw_in_t.reshape(sharded)]
    g["final_norm"] = g["final_norm"].reshape(D_MODEL)

    travelling = reducer.start("a", early[6:], swapped=["a1"], after=list(g.values())) if reducer is not None else []
    d_abr, d_abi, d_cm, d_bm = [], [], [], []
    for dr in range(2):
        da_r, da_i, d_win, d_woutt = _ssm_param_grads(f"ssm_param_grads{dr}", lams[dr], states[dr], s_in, d_yt_b,
                                                      reverse=(dr == 1), after=travelling)
        d_abr.append(da_r)
        d_abi.append(da_i)
        d_bm.append(d_win)
        d_cm.append(d_woutt)

    d_ar, d_ai, d_ldt, d_br, d_bi, d_cr, d_ci = _ssm_prep_bwd(*ssm_a, *ssm_b, d_abr, d_abi, d_bm, d_cm)
    per_group = (2 * SSM_GROUPS, SSM_GROUP * SSM_STATE)
    g["ssm_a_re"] = d_ar.reshape(2 * SSM_GROUPS, SSM_GROUP, SSM_STATE).sum(axis=1).reshape(sp["ssm_a_re"].shape)
    g["ssm_a_im"] = d_ai.reshape(2 * SSM_GROUPS, SSM_GROUP, SSM_STATE).sum(axis=1).reshape(sp["ssm_a_im"].shape)
    g["ssm_log_dt"] = d_ldt.reshape(per_group).sum(axis=1).reshape(sp["ssm_log_dt"].shape)
    g["ssm_b_re"] = d_br.reshape(sp["ssm_b_re"].shape)
    g["ssm_b_im"] = d_bi.reshape(sp["ssm_b_im"].shape)
    g["ssm_c_re"] = d_cr.reshape(sp["ssm_c_re"].shape)
    g["ssm_c_im"] = d_ci.reshape(sp["ssm_c_im"].shape)
    if reducer is not None:
        travelling = travelling + [d_ar, d_br, d_cr]
    dg1, dup1 = _ffn_bwd_act("ffn1_bwd_act", dh1_b, w_f1, g1, up1, tm, after=travelling)
    dw_f1 = [a.reshape(sharded) for a in _ffn_dw("ffn1_dw", u1, dg1, dup1, a1, dh1_b, tm)]
    if reducer is not None:
        travelling = reducer.start("b", dw_f1, after=reducer.finish("a", dw_f1[:1]))
        travelling = travelling + reducer.join_start("a", after=travelling)
    grad_x, _, g["ffn1_norm"] = _ffn_dx("ffn1_dx", dg1, dup1, w_f1, x, sp["ffn1_norm"], dh1, tm, after=travelling)
    if reducer is not None:
        reducer.finish("b", [grad_x])
        reducer.join_finish("a", [grad_x])
    return loss, grad_x, early + dw_f1, g


def _mesh_place():
    x, y, c = lax.axis_index("x"), lax.axis_index("y"), lax.axis_index("c")
    chips = [(1 - x, y), (x, 1 - y), (1 - x, 1 - y)]
    return x, y, c, chips


def _remote(src, dst, send_sems, recv_sems, k, to):
    return pltpu.make_async_remote_copy(src_ref=src, dst_ref=dst, send_sem=send_sems.at[k], recv_sem=recv_sems.at[k],
                                        device_id=to, device_id_type=MESH)


def _sibling_swap_halves(tag, grads, after=()):
    n = len(grads)
    after_ops, after_specs = _after_operands(after)

    def body(*refs):
        ins, outs = refs[:n], refs[n + len(after_ops):2 * n + len(after_ops)]
        send_sems, recv_sems = refs[2 * n + len(after_ops):]
        x, y, c, _ = _mesh_place()
        sibling = (x, y, 1 - c)
        copies = []
        for k in range(n):
            half = grads[k].shape[1] // 2
            theirs = pl.ds(pl.multiple_of((1 - c) * half, 16), half)
            cp = _remote(ins[k].at[:, theirs, :], outs[k], send_sems, recv_sems, k, sibling)
            cp.start()
            copies.append(cp)
        for cp in copies:
            cp.wait_recv()
        for cp in copies:
            cp.wait_send()

    hbm = BS(memory_space=pl.ANY)
    return pl.pallas_call(
        body, out_shape=tuple(SDS((g.shape[0], g.shape[1] // 2, g.shape[2]), g.dtype) for g in grads),
        in_specs=[hbm] * n + after_specs, out_specs=(hbm,) * n,
        scratch_shapes=[pltpu.SemaphoreType.DMA((n,)), pltpu.SemaphoreType.DMA((n,))],
        name="reduce_sibling_send_" + tag, compiler_params=_params())(*grads, *after_ops)


def _row_tile(rows, cap=512):
    return max(r for r in range(16, cap + 1, 16) if rows % r == 0)


REDUCE_STEPS = 2


def _chip_presum(tag, grads, gots, c_idx):
    n = len(grads)
    halves = [g.shape[1] // 2 for g in grads]
    tiles = [(h // REDUCE_STEPS, g.shape[2]) for h, g in zip(halves, grads)]

    def body(c_ref, *refs):
        for k in range(n):
            refs[2 * n + k][...] = (refs[k][...].astype(F32) + refs[n + k][...].astype(F32)).astype(BF16)

    mine = [BS((None, None) + tile, lambda s, i, c_ref: (s, c_ref[0], i, 0)) for tile in tiles]
    plain = [BS((None,) + tile, lambda s, i, c_ref: (s, i, 0)) for tile in tiles]
    return list(pl.pallas_call(
        body, out_shape=tuple(SDS((g.shape[0], h, g.shape[2]), BF16) for g, h in zip(grads, halves)),
        grid_spec=pltpu.PrefetchScalarGridSpec(num_scalar_prefetch=1, grid=(N_SHARD, REDUCE_STEPS),
                                               in_specs=mine + plain, out_specs=plain),
        name="reduce_presum_" + tag, compiler_params=_params("parallel", "parallel"))(
            c_idx, *[g.reshape(g.shape[0], 2, h, g.shape[2]) for g, h in zip(grads, halves)], *gots))


HBM_SPEC = BS(memory_space=pltpu.HBM)
SEM_SPEC = BS(memory_space=pltpu.SEMAPHORE)
DATAFLOW = pltpu.SideEffectType.DATAFLOW_SIDE_EFFECTING


def _chip_exchange_copies(parts, lands, send_sems, recv_sems):
    _, _, c, chips = _mesh_place()
    return [_remote(parts[k].at[2 * px + py], lands[k].at[j], send_sems, recv_sems, 3 * k + j, (px, py, c))
            for k in range(len(parts)) for j, (px, py) in enumerate(chips)]


def _gather_copies(shards, lands, send_sems, recv_sems):
    x, y, c, chips = _mesh_place()
    return [_remote(shards[k], lands[k].at[2 * x + y], send_sems, recv_sems, 3 * k + j, (px, py, c))
            for k in range(len(shards)) for j, (px, py) in enumerate(chips)]


def _gather_half_copies(shards, lands, send_sems, recv_sems):
    x, y, c, chips = _mesh_place()
    out = []
    for k in range(len(shards)):
        half = shards[k].shape[0] // 2
        mine = pl.ds(pl.multiple_of(c * half, 16), half)
        for j, (px, py) in enumerate(chips):
            out.append(_remote(shards[k].at[mine, :], lands[k].at[2 * x + y, mine, :], send_sems, recv_sems,
                               3 * k + j, (px, py, c)))
    return out


def _sibling_fill(tag, lands):
    n = len(lands)

    def body(*refs):
        outs = refs[n:2 * n]
        send_sems, recv_sems = refs[2 * n:]
        x, y, c, chips = _mesh_place()
        copies = []
        for k in range(n):
            half = lands[k].shape[1] // 2
            mine = pl.ds(pl.multiple_of(c * half, 16), half)
            for j, (px, py) in enumerate(chips):
                blk = outs[k].at[2 * px + py, mine, :]
                copies.append(_remote(blk, blk, send_sems, recv_sems, 3 * k + j, (x, y, 1 - c)))
        for cp in copies:
            cp.start()
        for cp in copies:
            cp.wait_recv()
        for cp in copies:
            cp.wait_send()

    hbm = BS(memory_space=pl.ANY)
    return list(pl.pallas_call(
        body, out_shape=tuple(SDS(a.shape, a.dtype) for a in lands),
        in_specs=[hbm] * n, out_specs=(hbm,) * n, input_output_aliases={k: k for k in range(n)},
        scratch_shapes=[pltpu.SemaphoreType.DMA((3 * n,)), pltpu.SemaphoreType.DMA((3 * n,))],
        name="gather_fill_" + tag, compiler_params=_params())(*lands))


def _swap_copies(grads, lands, send_sems, recv_sems):
    x, y, c, _ = _mesh_place()
    out = []
    for k in range(len(grads)):
        half = grads[k].shape[1] // 2
        theirs = pl.ds(pl.multiple_of((1 - c) * half, 16), half)
        out.append(_remote(grads[k].at[:, theirs, :], lands[k], send_sems, recv_sems, k, (x, y, 1 - c)))
    return out


def _join_copies(fulls, same, send_sems, recv_sems):
    x, y, c, _ = _mesh_place()
    out = []
    for k in range(len(fulls)):
        half = fulls[k].shape[0] // 2
        mine = fulls[k].at[pl.ds(pl.multiple_of(c * half, 8), half), :]
        out.append(_remote(mine, mine, send_sems, recv_sems, k, (x, y, 1 - c)))
    return out


def _everyone_copies(packs, lands, send_sems, recv_sems):
    x, y, c, _ = _mesh_place()
    out = []
    for k in range(len(packs)):
        for j in range(N_DEV - 1):
            bx, by, bc = (j + 1) >> 2 & 1, (j + 1) >> 1 & 1, (j + 1) & 1
            peer = (x ^ bx, y ^ by, c ^ bc)
            out.append(_remote(packs[k], lands[k].at[4 * x + 2 * y + c], send_sems, recv_sems, (N_DEV - 1) * k + j, peer))
    return out


def _split_start(name, copies, sources, land_shapes, after=(), fanout=3):
    n = len(sources)
    n_land = len(land_shapes)
    m = n + n_land
    n_sems = fanout * n
    after_ops, after_specs = _after_operands(after)

    def body(*refs):
        ins = refs[:n]
        lands = refs[n:m] if n_land else ins
        send_sems, recv_sems = refs[m + len(after_ops)], refs[m + len(after_ops) + 1]
        token = refs[-1]
        for cp in copies(ins, lands, send_sems, recv_sems):
            cp.start()
        token[...] = jnp.zeros_like(token)

    lands = [pltpu.with_memory_space_constraint(lax.empty(s, d), pltpu.HBM) for s, d in land_shapes]
    sources = [pltpu.with_memory_space_constraint(p, pltpu.HBM) for p in sources]
    thru = [pltpu.HBM(a.shape, a.dtype) for a in sources + lands]
    out = pl.pallas_call(
        body, name=name,
        out_shape=(pltpu.SemaphoreType.DMA((n_sems,)), pltpu.SemaphoreType.DMA((n_sems,)), *thru, SDS((8, 128), F32)),
        in_specs=[HBM_SPEC] * m + after_specs,
        out_specs=(SEM_SPEC, SEM_SPEC, *[HBM_SPEC] * m, BS(memory_space=pltpu.VMEM)),
        input_output_aliases={i: 2 + i for i in range(m)},
        compiler_params=pltpu.CompilerParams(has_side_effects=DATAFLOW))(*sources, *lands, *after_ops)
    return out[0], out[1], list(out[2:2 + n]), list(out[2 + n:2 + m]), out[-1]


def _split_wait(name, copies, send_sems, recv_sems, sources, lands, after):
    n = len(sources)
    m = n + len(lands)
    after_ops, after_specs = _after_operands(after)

    def body(*refs):
        ins = refs[:n]
        zones = refs[n:m] if m > n else ins
        for cp in copies(ins, zones, refs[m], refs[m + 1]):
            cp.wait_send()
            cp.wait_recv()

    out = pl.pallas_call(
        body, name=name,
        out_shape=tuple(pltpu.HBM(a.shape, a.dtype) for a in sources + lands),
        in_specs=[HBM_SPEC] * m + [SEM_SPEC, SEM_SPEC] + after_specs, out_specs=(HBM_SPEC,) * m,
        input_output_aliases={i: i for i in range(m)},
        compiler_params=pltpu.CompilerParams(has_side_effects=DATAFLOW))(*sources, *lands, send_sems, recv_sems, *after_ops)
    return list(out[:n]), list(out[n:])


class _WeightGatherer:
    def __init__(self, shards):
        self.shards, self.open = shards, {}
        self.me = 2 * lax.axis_index("x") + lax.axis_index("y")

    HALVED = ("f1a", "mix")

    def start(self, tag, after=()):
        shapes = [((N_SHARD,) + s.shape, s.dtype) for s in self.shards[tag]]
        copies = _gather_half_copies if tag in self.HALVED else _gather_copies
        self.open[tag] = _split_start("gather_start_" + tag, copies, self.shards[tag], shapes, after)
        return [self.open[tag][-1]]

    def finish(self, tag, after):
        send_sems, recv_sems, shards, lands, _ = self.open.pop(tag)
        copies = _gather_half_copies if tag in self.HALVED else _gather_copies
        shards, lands = _split_wait("gather_wait_" + tag, copies, send_sems, recv_sems, shards, lands, after)
        if tag in self.HALVED:
            lands = _sibling_fill(tag, lands)
        return [lax.dynamic_update_slice(zone, s[None], (self.me, 0, 0)) for zone, s in zip(lands, shards)]


class _GradReducer:
    def __init__(self):
        self.c_idx = lax.axis_index("c").astype(jnp.int32).reshape(1)
        self.place = jnp.stack([2 * lax.axis_index("x") + lax.axis_index("y"), lax.axis_index("c")]).astype(jnp.int32)
        self.swaps, self.open, self.landed, self.joins, self.reduced = {}, {}, {}, {}, []

    def swap_start(self, tag, grads, after=()):
        shapes = [((g.shape[0], g.shape[1] // 2, g.shape[2]), g.dtype) for g in grads]
        self.swaps[tag] = _split_start("reduce_swap_start_" + tag, _swap_copies, grads, shapes, after, fanout=1)
        return [self.swaps[tag][-1]]

    def start(self, tag, grads, after=(), swapped=()):
        pairs = []
        for s in swapped:
            send_sems, recv_sems, early, lands, _ = self.swaps.pop(s)
            pairs += zip(*_split_wait("reduce_swap_wait_" + s, _swap_copies, send_sems, recv_sems, early, lands, grads[-1:]))
        pairs += zip(grads, _sibling_swap_halves(tag, grads, after))
        parts = _chip_presum(tag, [g for g, _ in pairs], [s for _, s in pairs], self.c_idx)
        shapes = [((3,) + p.shape[1:], p.dtype) for p in parts]
        self.open[tag] = _split_start("reduce_exchange_start_" + tag, _chip_exchange_copies, parts, shapes)
        return [self.open[tag][-1]]

    def finish(self, tag, after):
        send_sems, recv_sems, parts, lands, _ = self.open.pop(tag)
        self.landed[tag] = _split_wait("reduce_exchange_wait_" + tag, _chip_exchange_copies, send_sems, recv_sems, parts, lands, after)
        return self.landed[tag][1][:1]

    def _sums(self, tag, after=()):
        parts, landed = self.landed.pop(tag)
        return _chip_sum(tag, parts, landed, self.place, after)

    def join_start(self, tag, after=()):
        self.joins[tag] = _split_start("reduce_join_start_" + tag, _join_copies, self._sums(tag, after), [], fanout=1)
        return [self.joins[tag][-1]]

    def join_finish(self, tag, after):
        send_sems, recv_sems, fulls, _, _ = self.joins.pop(tag)
        self.reduced += _split_wait("reduce_join_wait_" + tag, _join_copies, send_sems, recv_sems, fulls, [], after)[0]

    def join(self, tag, after=()):
        self.reduced += _sibling_join_halves(self._sums(tag), after)


def _chip_sum(tag, parts, gots, place, after=()):
    n = len(parts)
    tiles = [(p.shape[1] // REDUCE_STEPS, p.shape[2]) for p in parts]
    after_ops, after_specs = _after_operands(after)

    def body(place_ref, *refs):
        outs = refs[2 * n + len(after_ops):]
        for k in range(n):
            acc = refs[k][...].astype(F32)
            for j in range(3):
                acc = acc + refs[n + k][j].astype(F32)
            outs[k][...] = acc

    return list(pl.pallas_call(
        body, out_shape=tuple(SDS((2 * p.shape[1], p.shape[2]), F32) for p in parts),
        grid_spec=pltpu.PrefetchScalarGridSpec(
            num_scalar_prefetch=1, grid=(REDUCE_STEPS,),
            in_specs=[BS((None,) + tile, lambda i, place_ref: (place_ref[0], i, 0)) for tile in tiles]
            + [BS((3,) + tile, lambda i, place_ref: (0, i, 0)) for tile in tiles] + after_specs,
            out_specs=[BS(tile, lambda i, place_ref: (place_ref[1] * REDUCE_STEPS + i, 0)) for tile in tiles]),
        name="reduce_sum_" + tag, compiler_params=_params("parallel"))(place, *parts, *gots, *after_ops))


def _sibling_join_halves(fulls, after=()):
    n = len(fulls)
    after_ops, after_specs = _after_operands(after)

    def body(*refs):
        outs = refs[n + len(after_ops):2 * n + len(after_ops)]
        send_sems, recv_sems = refs[2 * n + len(after_ops):]
        copies = _join_copies(outs, outs, send_sems, recv_sems)
        for cp in copies:
            cp.start()
        for cp in copies:
            cp.wait_recv()
        for cp in copies:
            cp.wait_send()

    hbm = BS(memory_space=pl.ANY)
    return list(pl.pallas_call(
        body, out_shape=tuple(SDS(f.shape, f.dtype) for f in fulls),
        in_specs=[hbm] * n + after_specs, out_specs=(hbm,) * n, input_output_aliases={k: k for k in range(n)},
        scratch_shapes=[pltpu.SemaphoreType.DMA((n,)), pltpu.SemaphoreType.DMA((n,))],
        name="reduce_sibling_join", compiler_params=_params())(*fulls, *after_ops))


N_DEV = 8


def _sum_devices(packs):
    _, rows, lanes = packs.shape

    def body(p_ref, o_ref):
        acc = p_ref[0]
        for dev in range(1, N_DEV):
            acc = acc + p_ref[dev]
        o_ref[...] = acc

    vm = BS(memory_space=pltpu.VMEM)
    return pl.pallas_call(body, out_shape=SDS((rows, lanes), F32), in_specs=[vm], out_specs=vm,
                          name="small_sum", compiler_params=_params())(packs)


def _adamw_refs(w_ref, g_ref, m_ref, v_ref, go_ref, d_ref, mo_ref, vo_ref):
    bc1 = 1.0 - ADAM_B1 ** ADAM_STEP
    bc2 = 1.0 - ADAM_B2 ** ADAM_STEP
    g = g_ref[...]
    m_new = ADAM_B1 * m_ref[...] + (1.0 - ADAM_B1) * g
    v_new = ADAM_B2 * v_ref[...] + (1.0 - ADAM_B2) * (g * g)
    go_ref[...] = g
    mo_ref[...] = m_new
    vo_ref[...] = v_new
    d_ref[...] = -ADAM_LR * ((m_new / bc1) / (jnp.sqrt(v_new / bc2) + ADAM_EPS) + ADAM_WD * w_ref[...])


def _adamw_small(ws, gs, ms, vs):
    n = len(ws)

    def body(*refs):
        for k in range(n):
            _adamw_refs(*[refs[j * n + k] for j in range(4)], *refs[4 * n + 4 * k:4 * n + 4 * k + 4])

    vm = BS(memory_space=pltpu.VMEM)
    outs = pl.pallas_call(
        body, out_shape=tuple(SDS(a.shape, F32) for a in ws for _ in range(4)), in_specs=[vm] * (4 * n),
        out_specs=(vm,) * (4 * n), name="adamw_small", compiler_params=_params())(*ws, *gs, *ms, *vs)
    return [outs[4 * k:4 * k + 4] for k in range(n)]


def _adamw(name, w, grad, row0, m, v, after=()):
    rows, cols = w.shape
    tr = rows if rows < 16 else _row_tile(rows, 352)
    after_ops, after_specs = _after_operands(after)

    def body(w_ref, g_ref, m_ref, v_ref, *rest):
        _adamw_refs(w_ref, g_ref, m_ref, v_ref, *rest[len(after_ops):])

    blk = BS((tr, cols), lambda i: (i, 0))
    shape = SDS((rows, cols), F32)
    return pl.pallas_call(
        body, out_shape=(shape,) * 4, grid=(rows // tr,),
        in_specs=[blk, BS((tr, cols), lambda i: (row0 // tr + i, 0)), blk, blk] + after_specs, out_specs=(blk,) * 4,
        name=name, compiler_params=_params("parallel"))(w, grad, m, v, *after_ops)


SMALL_LANES = 128


def _pack_small(parts):
    flat = jnp.concatenate([jnp.ravel(p) for p in parts])
    rows = -(-flat.shape[0] // (64 * SMALL_LANES)) * 64
    return jnp.pad(flat, (0, rows * SMALL_LANES - flat.shape[0])).reshape(rows, SMALL_LANES)


def _unpack_small(packed, like):
    flat = jnp.ravel(packed)
    out, at = [], 0
    for p in like:
        out.append(flat[at:at + p.size].reshape(p.shape))
        at += p.size
    return out


def kernel(x, mem, ffn1_norm, ffn1_w_gate, ffn1_w_up, ffn1_w_down, mix_norm, w_in, pool_w, pool_scale, w_pool_proj, ssm_a_re, ssm_a_im, ssm_log_dt, ssm_b_re, ssm_b_im, ssm_c_re, ssm_c_im, ssm_d, w_glu_val, w_glu_gate, w_mix_out, xattn_norm, mem_norm, w_q, w_kv, w_xo, ffn2_norm, ffn2_w_gate, ffn2_w_up, ffn2_w_down, final_norm, loss_target, m_ffn1_norm, m_ffn1_w_gate, m_ffn1_w_up, m_ffn1_w_down, m_mix_norm, m_w_in, m_pool_w, m_pool_scale, m_w_pool_proj, m_ssm_a_re, m_ssm_a_im, m_ssm_log_dt, m_ssm_b_re, m_ssm_b_im, m_ssm_c_re, m_ssm_c_im, m_ssm_d, m_w_glu_val, m_w_glu_gate, m_w_mix_out, m_xattn_norm, m_mem_norm, m_w_q, m_w_kv, m_w_xo, m_ffn2_norm, m_ffn2_w_gate, m_ffn2_w_up, m_ffn2_w_down, m_final_norm, v_ffn1_norm, v_ffn1_w_gate, v_ffn1_w_up, v_ffn1_w_down, v_mix_norm, v_w_in, v_pool_w, v_pool_scale, v_w_pool_proj, v_ssm_a_re, v_ssm_a_im, v_ssm_log_dt, v_ssm_b_re, v_ssm_b_im, v_ssm_c_re, v_ssm_c_im, v_ssm_d, v_w_glu_val, v_w_glu_gate, v_w_mix_out, v_xattn_norm, v_mem_norm, v_w_q, v_w_kv, v_w_xo, v_ffn2_norm, v_ffn2_w_gate, v_ffn2_w_up, v_ffn2_w_down, v_final_norm):
    given = dict(locals())
    w = {n: given[n] for n in WEIGHTS}
    m = {n: given["m_" + n] for n in WEIGHTS}
    v = {n: given["v_" + n] for n in WEIGHTS}

    def shard_view(a, n):
        return a[0].T if n in TRANSPOSED else a[0]

    def shard_unview(a, n):
        return (a.T if n in TRANSPOSED else a)[None]

    shards = {tag: [jnp.concatenate([shard_view(w[n], n).astype(BF16) for n in grp], axis=0) for grp in arrays]
              for tag, arrays in GATHER_PHASES.items()}
    reducer = _GradReducer()
    ws, ms, vs = ({n: _small_view(a[n], n) for n in SMALL} for a in (w, m, v))
    loss_part, grad_x, _, small = _device_step(x[0], mem[0], loss_target[0], _WeightGatherer(shards), ws, reducer)

    small_like = [ws[n] for n in SMALL] + [loss_part[0, :1]]
    pack = _pack_small([small[n] for n in SMALL] + [loss_part[0, :1]])
    everyone = _split_start("small_start", _everyone_copies, [pack], [((N_DEV,) + pack.shape, F32)], fanout=N_DEV - 1)
    reducer.join("b", after=everyone[-1:])

    grads, delta, new_m, new_v = {}, {}, {}, {}
    big_done = []
    for grp, red in zip(REDUCE_GROUPS, reducer.reduced):
        row0 = 0
        for n in grp:
            w_n = shard_view(w[n], n)
            outs = _adamw("adamw_" + n, w_n, red, row0, shard_view(m[n], n), shard_view(v[n], n), after=everyone[-1:])
            grads[n], delta[n], new_m[n], new_v[n] = (shard_unview(o, n) for o in outs)
            big_done.append(outs[1])
            row0 += w_n.shape[0]

    send_sems, recv_sems, packs, landed, _ = everyone
    packs, landed = _split_wait("small_wait", _everyone_copies, send_sems, recv_sems, packs, landed, big_done)
    mine = 4 * lax.axis_index("x") + 2 * lax.axis_index("y") + lax.axis_index("c")
    summed = _sum_devices(lax.dynamic_update_slice(landed[0], packs[0][None], (mine, 0, 0)))
    g_small = dict(zip(SMALL + ("loss",), _unpack_small(summed, small_like)))
    loss = g_small.pop("loss").reshape(())
    def two_d(a):
        return a.reshape(-1, a.shape[-1])

    updated = _adamw_small(*([two_d(a[n]) for n in SMALL] for a in (ws, g_small, ms, vs)))
    for n, outs in zip(SMALL, updated):
        grads[n], delta[n], new_m[n], new_v[n] = (_small_view(o.reshape(ws[n].shape), n) for o in outs)

    return (loss, grad_x[None], *[grads[n] for n in WEIGHTS], *[delta[n] for n in WEIGHTS],
            *[new_m[n] for n in WEIGHTS], *[new_v[n] for n in WEIGHTS])
```

```python
import functools
import math

import jax
import jax.numpy as jnp
from jax import lax
from jax.experimental import pallas as pl
from jax.experimental.pallas import tpu as pltpu

F32 = jnp.float32
BF16 = jnp.bfloat16
SDS = jax.ShapeDtypeStruct
BS = pl.BlockSpec
MESH = pl.DeviceIdType.MESH

D_MODEL = 1024
D_FF = 2816
N_SHARD = 4
FF_SH = D_FF // N_SHARD
D_POOL = 512
POOL_WINDOWS = (2, 4, 8, 16)
POOL_GROUP = 128
D_SSM = 256
SSM_GROUPS = 16
SSM_GROUP = 16
SSM_STATE = 64
SSM_CH = SSM_GROUPS * SSM_STATE
N_HEADS = 4
HEAD_DIM = 256
EPS = 1e-6
ADAM_LR, ADAM_B1, ADAM_B2, ADAM_EPS, ADAM_WD, ADAM_STEP = 0.001, 0.9, 0.999, 1e-08, 0.01, 10

VMEM_LIMIT_V7X = 58 * 1024 * 1024
TM = 512

NN = (((1,), (0,)), ((), ()))
NT = (((1,), (1,)), ((), ()))
TN = (((0,), (0,)), ((), ()))


def _params(*sem):
    return pltpu.CompilerParams(dimension_semantics=sem if sem else None, vmem_limit_bytes=VMEM_LIMIT_V7X)


def _dot(a, b, dims=NN):
    return lax.dot_general(a.astype(BF16), b.astype(BF16), dims, preferred_element_type=F32)


def _sigmoid(v):
    return pl.reciprocal(1.0 + jnp.exp(-v), approx=True)


def _block_dims(spec):
    return tuple(d for d in spec.block_shape if d is not None)


def _after_operands(after):
    return list(after), [BS(memory_space=pl.ANY)] * len(after)


def _mm(name, pairs, *, grid, out_shape, out_spec, red_axis=None, extras=(), epilogue=None, after=()):
    n_pairs, n_extra = len(pairs), len(extras)
    n_red = grid[red_axis] if red_axis is not None else 1
    dims = [p[4] for p in pairs]

    def body(*refs):
        ab = refs[:2 * n_pairs]
        ex = refs[2 * n_pairs:2 * n_pairs + n_extra]
        o_ref = refs[2 * n_pairs + n_extra + len(after)]

        def partial():
            acc = None
            for p in range(n_pairs):
                t = _dot(ab[2 * p][...], ab[2 * p + 1][...], dims[p])
                acc = t if acc is None else acc + t
            return acc

        def finish(acc):
            res = epilogue(acc, *[e[...] for e in ex]) if epilogue is not None else acc
            o_ref[...] = res.astype(o_ref.dtype)

        if n_red == 1:
            finish(partial())
        else:
            acc_ref = refs[-1]
            k = pl.program_id(red_axis)

            @pl.when(k == 0)
            def _():
                acc_ref[...] = jnp.zeros_like(acc_ref)

            acc_ref[...] += partial()

            @pl.when(k == n_red - 1)
            def _():
                finish(acc_ref[...])

    operands, in_specs = [], []
    for a, a_spec, b, b_spec, _ in pairs:
        operands += [a, b]
        in_specs += [a_spec, b_spec]
    for e, e_spec in extras:
        operands.append(e)
        in_specs.append(e_spec)
    after_ops, after_specs = _after_operands(after)
    operands += after_ops
    in_specs += after_specs
    scratch = [pltpu.VMEM(_block_dims(out_spec), F32)] if n_red > 1 else []
    sem = tuple("arbitrary" if ax == red_axis else "parallel" for ax in range(len(grid)))
    return pl.pallas_call(body, out_shape=out_shape, grid=grid, in_specs=in_specs, out_specs=out_spec,
                          scratch_shapes=scratch, name=name, compiler_params=_params(*sem))(*operands)


def _rmsnorm(name, h, gain, tm, after=()):
    t, d = h.shape
    after_ops, after_specs = _after_operands(after)

    def body(h_ref, g_ref, *rest):
        u_ref = rest[-1]
        hv = h_ref[...]
        r = lax.rsqrt(jnp.mean(hv * hv, axis=-1, keepdims=True) + EPS)
        u_ref[...] = ((hv * r) * g_ref[...]).astype(u_ref.dtype)

    return pl.pallas_call(
        body, out_shape=SDS((t, d), BF16), grid=(t // tm,),
        in_specs=[BS((tm, d), lambda i: (i, 0)), BS((1, d), lambda i: (0, 0))] + after_specs,
        out_specs=BS((tm, d), lambda i: (i, 0)), name=name, compiler_params=_params("parallel"))(h, gain, *after_ops)


def _rmsnorm_bwd(name, h, gain, du, dh_in, tm):
    t, d = h.shape
    has_in = dh_in is not None

    def body(*refs):
        if has_in:
            h_ref, g_ref, du_ref, dhin_ref, dh_ref, dhb_ref, dg_ref = refs
        else:
            h_ref, g_ref, du_ref, dh_ref, dhb_ref, dg_ref = refs
        i = pl.program_id(0)
        hv = h_ref[...]
        r = lax.rsqrt(jnp.mean(hv * hv, axis=-1, keepdims=True) + EPS)
        n = hv * r
        duv = du_ref[...].astype(F32)
        dn = duv * g_ref[...]
        dh = r * (dn - n * jnp.mean(dn * n, axis=-1, keepdims=True))
        if has_in:
            dh = dhin_ref[...] + dh
        dh_ref[...] = dh
        dhb_ref[...] = dh.astype(BF16)

        @pl.when(i == 0)
        def _():
            dg_ref[...] = jnp.zeros_like(dg_ref)

        dg_ref[...] += jnp.sum(duv * n, axis=0, keepdims=True)

    row = BS((tm, d), lambda i: (i, 0))
    vec = BS((1, d), lambda i: (0, 0))
    operands = [h, gain, du] + ([dh_in] if has_in else [])
    in_specs = [row, vec, row] + ([row] if has_in else [])
    return pl.pallas_call(
        body, out_shape=(SDS((t, d), F32), SDS((t, d), BF16), SDS((1, d), F32)), grid=(t // tm,),
        in_specs=in_specs, out_specs=(row, row, vec), name=name, compiler_params=_params("arbitrary"))(*operands)


def _loss_head_tile(i, hv, g_ref, t_ref, loss_ref, dh_ref, dhb_ref, dg_ref):
    g = g_ref[...]
    r = lax.rsqrt(jnp.mean(hv * hv, axis=-1, keepdims=True) + EPS)
    n = hv * r
    err = n * g - t_ref[...]
    dy = err * (1.0 / hv.shape[-1])
    dn = dy * g
    dh = r * (dn - n * jnp.mean(dn * n, axis=-1, keepdims=True))
    dh_ref[...] = dh
    dhb_ref[...] = dh.astype(BF16)

    @pl.when(i == 0)
    def _():
        dg_ref[...] = jnp.zeros_like(dg_ref)
        loss_ref[...] = jnp.zeros_like(loss_ref)

    dg_ref[...] += jnp.sum(dy * n, axis=0, keepdims=True)
    part = 0.5 * jnp.sum(jnp.mean(err * err, axis=-1, keepdims=True), axis=0, keepdims=True)
    loss_ref[...] += jnp.broadcast_to(part, loss_ref.shape)


def _norm_tile(h, g_ref, u_ref):
    r = lax.rsqrt(jnp.mean(h * h, axis=-1, keepdims=True) + EPS)
    u_ref[...] = ((h * r) * g_ref[...]).astype(u_ref.dtype)


FFN_BLOCK = D_FF // 2


def _ffn_up(name, u, w_f, tm, after=()):
    t, d = u.shape
    after_ops, after_specs = _after_operands(after)

    def body(u_ref, wg_ref, wu_ref, *rest):
        pg_ref, pu_ref, a_ref = rest[len(after_ops):]
        uv = u_ref[...]
        for lo in range(0, D_FF, FFN_BLOCK):
            cols = slice(lo, lo + FFN_BLOCK)
            g = _dot(uv, wg_ref[cols, :], NT)
            up = _dot(uv, wu_ref[cols, :], NT)
            sg = _sigmoid(g)
            silu = g * sg
            a_ref[:, cols] = (silu * up).astype(BF16)
            pu_ref[:, cols] = (0.5 * silu).astype(BF16)
            pg_ref[:, cols] = (0.5 * sg * (1.0 + g * (1.0 - sg)) * up).astype(BF16)

    hid = BS((tm, D_FF), lambda i: (i, 0))
    shape = SDS((t, D_FF), BF16)
    whole = BS((D_FF, d), lambda i: (0, 0))
    return pl.pallas_call(
        body, out_shape=(shape, shape, shape), grid=(t // tm,),
        in_specs=[BS((tm, d), lambda i: (i, 0)), whole, whole] + after_specs,
        out_specs=(hid, hid, hid), name=name,
        compiler_params=_params("parallel"))(u, w_f["gate"], w_f["up"], *after_ops)


def _ffn_down(name, a, w_f, resid, tm, next_gain=None, head=None):
    t, d = resid.shape
    row = BS((tm, d), lambda i: (i, 0))
    vec = BS((1, d), lambda i: (0, 0))

    def body(a_ref, w_ref, res_ref, *rest):
        h = res_ref[...] + 0.5 * _dot(a_ref[...], w_ref[...])
        if head is not None:
            _loss_head_tile(pl.program_id(0), h, *rest)
        else:
            g_ref, h_ref, u_ref = rest
            h_ref[...] = h
            _norm_tile(h, g_ref, u_ref)

    if head is not None:
        extra, extra_specs = list(head), [vec, row]
        out_shape = (SDS((1, 128), F32), SDS((t, d), F32), SDS((t, d), BF16), SDS((1, d), F32))
        out_specs = (BS((1, 128), lambda i: (0, 0)), row, row, vec)
    else:
        extra, extra_specs = [next_gain], [vec]
        out_shape = (SDS((t, d), F32), SDS((t, d), BF16))
        out_specs = (row, row)
    return pl.pallas_call(
        body, out_shape=out_shape, grid=(t // tm,),
        in_specs=[BS((tm, D_FF), lambda i: (i, 0)), BS((D_FF, d), lambda i: (0, 0)), row] + extra_specs,
        out_specs=out_specs, name=name,
        compiler_params=_params("arbitrary" if head is not None else "parallel"))(a, w_f["down"], resid, *extra)


def _mix_in(u, w_t, tm, after=()):
    t, d = u.shape
    after_ops, after_specs = _after_operands(after)

    def body(u_ref, w_ref, *rest):
        o_ref = rest[-1]
        uv = u_ref[...]
        for lo in range(0, D_FF, FFN_BLOCK):
            o_ref[:, lo:lo + FFN_BLOCK] = _dot(uv, w_ref[lo:lo + FFN_BLOCK, :], NT)

    return pl.pallas_call(
        body, out_shape=SDS((t, D_FF), F32), grid=(t // tm,),
        in_specs=[BS((tm, d), lambda i: (i, 0)), BS((D_FF, d), lambda i: (0, 0))] + after_specs,
        out_specs=BS((tm, D_FF), lambda i: (i, 0)), name="mix_in",
        compiler_params=_params("parallel"))(u, w_t, *after_ops)


def _mm_resid_norm(name, a, b, resid, next_gain, tm):
    t, d = resid.shape
    tm = min(2 * tm, t)

    def body(a_ref, b_ref, res_ref, g_ref, h_ref, u_ref):
        h = res_ref[...] + _dot(a_ref[...], b_ref[...])
        h_ref[...] = h
        _norm_tile(h, g_ref, u_ref)

    row = BS((tm, d), lambda i: (i, 0))
    return pl.pallas_call(
        body, out_shape=(SDS((t, d), F32), SDS((t, d), BF16)), grid=(t // tm,),
        in_specs=[BS((tm, a.shape[1]), lambda i: (i, 0)), BS(b.shape, lambda i: (0, 0)), row, BS((1, d), lambda i: (0, 0))],
        out_specs=(row, row), name=name, compiler_params=_params("parallel"))(a, b, resid, next_gain)


def _ffn_bwd_act(name, dh_b, w_f, pg, pu, tm, after=()):
    t, d = dh_b.shape
    after_ops, after_specs = _after_operands(after)

    def body(dh_ref, wd_ref, pg_ref, pu_ref, *rest):
        dg_ref, dup_ref = rest[len(after_ops):]
        dh = dh_ref[...]
        for lo in range(0, D_FF, FFN_BLOCK):
            cols = slice(lo, lo + FFN_BLOCK)
            da = _dot(dh, wd_ref[cols, :], NT)
            dg_ref[:, cols] = (da * pg_ref[:, cols].astype(F32)).astype(BF16)
            dup_ref[:, cols] = (da * pu_ref[:, cols].astype(F32)).astype(BF16)

    hid = BS((tm, D_FF), lambda i: (i, 0))
    shape = SDS((t, D_FF), BF16)
    return pl.pallas_call(
        body, out_shape=(shape, shape), grid=(t // tm,),
        in_specs=[BS((tm, d), lambda i: (i, 0)), BS((D_FF, d), lambda i: (0, 0)), hid, hid] + after_specs,
        out_specs=(hid, hid), name=name,
        compiler_params=_params("parallel"))(dh_b, w_f["down"], pg, pu, *after_ops)


def _ffn_dw(name, u, dg, dup, a, dh_b, tm):
    t, d = u.shape
    n_t = t // tm

    def body(u_ref, dg_ref, dup_ref, a_ref, dh_ref, og_ref, ou_ref, od_ref, acc):
        i = pl.program_id(1)

        @pl.when(i == 0)
        def _():
            acc[...] = jnp.zeros_like(acc)

        uv = u_ref[...]
        acc[0] += _dot(dg_ref[...], uv, TN)
        acc[1] += _dot(dup_ref[...], uv, TN)
        acc[2] += _dot(a_ref[...], dh_ref[...], TN)

        @pl.when(i == n_t - 1)
        def _():
            og_ref[...] = acc[0].astype(BF16)
            ou_ref[...] = acc[1].astype(BF16)
            od_ref[...] = (0.5 * acc[2]).astype(BF16)

    hid = BS((tm, FFN_BLOCK), lambda j, i: (i, j))
    row = BS((tm, d), lambda j, i: (i, 0))
    out = BS((FFN_BLOCK, d), lambda j, i: (j, 0))
    shape = SDS((D_FF, d), BF16)
    return pl.pallas_call(
        body, out_shape=(shape, shape, shape), grid=(D_FF // FFN_BLOCK, n_t),
        in_specs=[row, hid, hid, hid, row], out_specs=(out, out, out),
        scratch_shapes=[pltpu.VMEM((3, FFN_BLOCK, d), F32)],
        name=name, compiler_params=_params("parallel", "arbitrary"))(u, dg, dup, a, dh_b)


def _norm_bwd_tile(i, du, h_ref, g_ref, dhin_ref, dh_ref, dhb_ref, dg_ref):
    hv = h_ref[...]
    r = lax.rsqrt(jnp.mean(hv * hv, axis=-1, keepdims=True) + EPS)
    n = hv * r
    dn = du * g_ref[...]
    dh = dhin_ref[...] + r * (dn - n * jnp.mean(dn * n, axis=-1, keepdims=True))
    dh_ref[...] = dh
    dhb_ref[...] = dh.astype(BF16)

    @pl.when(i == 0)
    def _():
        dg_ref[...] = jnp.zeros_like(dg_ref)

    dg_ref[...] += jnp.sum(du * n, axis=0, keepdims=True)


def _norm_bwd_specs(tm):
    row = BS((tm, D_MODEL), lambda i: (i, 0))
    vec = BS((1, D_MODEL), lambda i: (0, 0))
    return [row, vec, row], (row, row, vec)


def _norm_bwd_shapes(t):
    return SDS((t, D_MODEL), F32), SDS((t, D_MODEL), BF16), SDS((1, D_MODEL), F32)


def _ffn_dx(name, dg, dup, w_f, h, gain, dh_in, tm, after=()):
    t = dg.shape[0]
    tm = tm // 2
    after_ops, after_specs = _after_operands(after)

    def body(dg_ref, dup_ref, wg_ref, wu_ref, h_ref, g_ref, dhin_ref, *rest):
        du = _dot(dg_ref[...], wg_ref[...]) + _dot(dup_ref[...], wu_ref[...])
        _norm_bwd_tile(pl.program_id(0), du, h_ref, g_ref, dhin_ref, *rest[len(after_ops):])

    hid = BS((tm, D_FF), lambda i: (i, 0))
    whole = BS((D_FF, D_MODEL), lambda i: (0, 0))
    norm_in, norm_out = _norm_bwd_specs(tm)
    return pl.pallas_call(
        body, out_shape=_norm_bwd_shapes(t), grid=(t // tm,),
        in_specs=[hid, hid, whole, whole] + norm_in + after_specs, out_specs=norm_out, name=name,
        compiler_params=_params("arbitrary"))(dg, dup, w_f["gate"], w_f["up"], h, gain, dh_in, *after_ops)


def _mm_norm_bwd(name, a, b, dims, h, gain, dh_in, tm):
    t = a.shape[0]

    def body(a_ref, b_ref, h_ref, g_ref, dhin_ref, *outs):
        _norm_bwd_tile(pl.program_id(0), _dot(a_ref[...], b_ref[...], dims), h_ref, g_ref, dhin_ref, *outs)

    norm_in, norm_out = _norm_bwd_specs(tm)
    return pl.pallas_call(
        body, out_shape=_norm_bwd_shapes(t), grid=(t // tm,),
        in_specs=[BS((tm, a.shape[1]), lambda i: (i, 0)), BS(b.shape, lambda i: (0, 0))] + norm_in,
        out_specs=norm_out, name=name, compiler_params=_params("arbitrary"))(a, b, h, gain, dh_in)


def _plain_mm(name, a, b, dims, out_dtype, tm, resid=None, after=()):
    t = a.shape[0]
    tm = min(2 * tm, t)
    n = b.shape[1] if dims == NN else b.shape[0]
    extras = [(resid, BS((tm, n), lambda i: (i, 0)))] if resid is not None else []
    epi = (lambda acc, res: res + acc) if resid is not None else None
    return _mm(name, [(a, BS((tm, a.shape[1]), lambda i: (i, 0)), b, BS(b.shape, lambda i: (0, 0)), dims)],
               grid=(t // tm,), out_shape=SDS((t, n), out_dtype), out_spec=BS((tm, n), lambda i: (i, 0)),
               extras=extras, epilogue=epi, after=after)


def _dw_mm(name, a, b, tm, out_dtype=BF16, after=()):
    t, k = a.shape
    n = b.shape[1]
    tm = min(2 * tm, t)
    return _mm(name, [(a, BS((tm, k), lambda i: (i, 0)), b, BS((tm, n), lambda i: (i, 0)), TN)],
               grid=(t // tm,), red_axis=0, out_shape=SDS((k, n), out_dtype), out_spec=BS((k, n), lambda i: (0, 0)),
               after=after)


POOL_CHUNK = 256
POOL_HALO = 8


def _window_sum(v, width, lead):
    n = v.shape[0]
    s = v
    k = 1
    while k < width:
        s = s + pltpu.roll(s, n - k, 0)
        k *= 2
    return pltpu.roll(s, lead, 0) if lead else s


def _pool_count(base, left, right, t, shape):
    pos = base + lax.broadcasted_iota(jnp.int32, shape, 0)
    lo = jnp.maximum(pos - left, 0)
    hi = jnp.minimum(pos + right + 1, t)
    return (hi - lo).astype(F32)


def _pool_fwd(proj, pool_w, pool_scale):
    t = proj.shape[0]
    c, h = POOL_CHUNK, POOL_HALO
    n_chunks = t // c

    def body(proj_hbm, pw_ref, sc_ref, pooled_ref, mixed_ref, ms_ref, pad_ref, sem):
        cp = pltpu.make_async_copy(proj_hbm.at[:, pl.ds(0, D_POOL)], pad_ref.at[pl.ds(h, t), :], sem)
        cp.start()
        pad_ref[pl.ds(0, h), :] = jnp.zeros((h, D_POOL), F32)
        pad_ref[pl.ds(t + h, h), :] = jnp.zeros((h, D_POOL), F32)
        cp.wait()
        for g, width in enumerate(POOL_WINDOWS):
            left = width // 2
            right = width - 1 - left
            cols = slice(g * POOL_GROUP, (g + 1) * POOL_GROUP)
            wmat = pw_ref[g].astype(BF16)
            scale = sc_ref[:, cols]

            def chunk(ci, carry, left=left, right=right, width=width, cols=cols, wmat=wmat, scale=scale):
                base = pl.multiple_of(ci * c, c)
                v = pad_ref[pl.ds(base, c + 2 * h), cols]
                win = _window_sum(v, width, left)[h:h + c]
                cnt = _pool_count(base, left, right, t, (c, POOL_GROUP))
                pooled = (win / cnt - v[h:h + c]).astype(BF16)
                mixed = _dot(pooled, wmat)
                pooled_ref[pl.ds(base, c), cols] = pooled
                mixed_ref[pl.ds(base, c), cols] = mixed.astype(BF16)
                ms_ref[pl.ds(base, c), cols] = (mixed * scale).astype(BF16)
                return carry

            lax.fori_loop(0, n_chunks, chunk, 0)

    vm = BS(memory_space=pltpu.VMEM)
    shape = SDS((t, D_POOL), BF16)
    return pl.pallas_call(
        body, out_shape=(shape, shape, shape),
        in_specs=[BS(memory_space=pl.ANY), vm, vm], out_specs=(vm, vm, vm),
        scratch_shapes=[pltpu.VMEM((t + 2 * h, D_POOL), F32), pltpu.SemaphoreType.DMA],
        name="pool_fwd", compiler_params=_params())(proj, pool_w, pool_scale)


def _pool_bwd(d_ms, mixed, pooled, pool_w, pool_scale):
    t = d_ms.shape[0]
    c, h = POOL_CHUNK, POOL_HALO
    n_chunks = t // c

    def body(dms_ref, mixed_ref, pooled_ref, pw_ref, sc_ref, dp_ref, dsc_ref, dpw_ref, pad_ref):
        pad_ref[pl.ds(0, h), :] = jnp.zeros((h, D_POOL), F32)
        pad_ref[pl.ds(t + h, h), :] = jnp.zeros((h, D_POOL), F32)
        for g, width in enumerate(POOL_WINDOWS):
            left = width // 2
            right = width - 1 - left
            cols = slice(g * POOL_GROUP, (g + 1) * POOL_GROUP)
            wmat = pw_ref[g].astype(BF16)
            scale = sc_ref[:, cols]

            def first(ci, carry, left=left, right=right, cols=cols, wmat=wmat, scale=scale):
                dsc, dpw = carry
                base = pl.multiple_of(ci * c, c)
                dms = dms_ref[pl.ds(base, c), cols].astype(F32)
                dsc = dsc + jnp.sum(dms * mixed_ref[pl.ds(base, c), cols].astype(F32), axis=0, keepdims=True)
                dmix = (dms * scale).astype(BF16)
                dpw = dpw + _dot(pooled_ref[pl.ds(base, c), cols], dmix, TN)
                dpooled = _dot(dmix, wmat, NT)
                cnt = _pool_count(base, left, right, t, (c, POOL_GROUP))
                pad_ref[pl.ds(base + h, c), cols] = dpooled / cnt
                return dsc, dpw

            dsc, dpw = lax.fori_loop(0, n_chunks, first,
                                     (jnp.zeros((1, POOL_GROUP), F32), jnp.zeros((POOL_GROUP, POOL_GROUP), F32)))
            dsc_ref[:, cols] = dsc
            dpw_ref[g] = dpw

            def second(ci, carry, left=left, right=right, width=width, cols=cols):
                base = pl.multiple_of(ci * c, c)
                v = pad_ref[pl.ds(base, c + 2 * h), cols]
                win = _window_sum(v, width, right)[h:h + c]
                cnt = _pool_count(base, left, right, t, (c, POOL_GROUP))
                dp_ref[pl.ds(base, c), cols] = (win - v[h:h + c] * cnt).astype(BF16)
                return carry

            lax.fori_loop(0, n_chunks, second, 0)

    vm = BS(memory_space=pltpu.VMEM)
    return pl.pallas_call(
        body, out_shape=(SDS((t, D_POOL), BF16), SDS((1, D_POOL), F32), SDS((4, POOL_GROUP, POOL_GROUP), F32)),
        in_specs=[vm] * 5, out_specs=(vm, vm, vm),
        scratch_shapes=[pltpu.VMEM((t + 2 * h, D_POOL), F32)],
        name="pool_bwd", compiler_params=_params())(d_ms, mixed, pooled, pool_w, pool_scale)


SSM_ROWS = 2 * SSM_GROUPS * SSM_GROUP
SSM_HALF = SSM_GROUPS * SSM_GROUP


def _ssm_zoh(a_r, a_i, ldt):
    dt = jnp.exp(ldt)
    mag = jnp.exp(dt * a_r)
    ang = dt * a_i
    cs, sn = jnp.cos(ang), jnp.sin(ang)
    abr, abi = mag * cs, mag * sn
    den = a_r * a_r + a_i * a_i
    nr = abr - 1.0
    qr = (nr * a_r + abi * a_i) / den
    qi = (abi * a_r - nr * a_i) / den
    return dt, mag, cs, sn, abr, abi, den, nr, qr, qi


def _ssm_group_mask():
    row = lax.broadcasted_iota(jnp.int32, (SSM_HALF, SSM_CH), 0)
    col = lax.broadcasted_iota(jnp.int32, (SSM_HALF, SSM_CH), 1)
    return (row // SSM_GROUP) == (col // SSM_STATE)


def _ssm_prep(a_r, a_i, ldt, b_r, b_i, c_r, c_i, after=()):
    after_ops, after_specs = _after_operands(after)

    def body(ar_ref, ai_ref, ldt_ref, br_ref, bi_ref, cr_ref, ci_ref, *rest):
        abr_ref, abi_ref, win_ref, wint_ref, woutt_ref, wout_ref = rest[len(after_ops):]
        *_, abr, abi, _, _, qr, qi = _ssm_zoh(ar_ref[...], ai_ref[...], ldt_ref[...])
        abr_ref[...] = abr
        abi_ref[...] = abi
        b_r, b_i = br_ref[...], bi_ref[...]
        bbr = qr * b_r - qi * b_i
        bbi = qr * b_i + qi * b_r
        mask = _ssm_group_mask()
        state = lax.broadcasted_iota(jnp.int32, (SSM_STATE, SSM_CH), 0)
        col = lax.broadcasted_iota(jnp.int32, (SSM_STATE, SSM_CH), 1)
        every_group = (col % SSM_STATE == state).astype(BF16)

        def spread(x):
            return jnp.where(mask, _dot(x, every_group), 0.0)

        for d in range(2):
            rows = slice(d * SSM_HALF, (d + 1) * SSM_HALF)
            for half, x_in, x_out in ((0, bbr[rows], cr_ref[rows, :]), (1, bbi[rows], -ci_ref[rows, :])):
                cols = slice(half * SSM_CH, (half + 1) * SSM_CH)
                m_in, m_out = spread(x_in), spread(x_out)
                win_ref[d, :, cols] = m_in.astype(BF16)
                wint_ref[d, cols, :] = m_in.T.astype(BF16)
                woutt_ref[d, :, cols] = m_out.astype(BF16)
                wout_ref[d, cols, :] = m_out.T.astype(BF16)

    vm = BS(memory_space=pltpu.VMEM)
    vec = SDS((SSM_ROWS, SSM_STATE), F32)
    wide = SDS((2, SSM_HALF, 2 * SSM_CH), BF16)
    tall = SDS((2, 2 * SSM_CH, SSM_HALF), BF16)
    return pl.pallas_call(body, out_shape=(vec, vec, wide, tall, wide, tall), in_specs=[vm] * 7 + after_specs,
                          out_specs=(vm,) * 6, name="ssm_prep",
                          compiler_params=_params())(a_r, a_i, ldt, b_r, b_i, c_r, c_i, *after_ops)


def _ssm_prep_bwd(a_r, a_i, ldt, b_r, b_i, d_abr, d_abi, d_win, d_woutt):
    def body(ar_ref, ai_ref, ldt_ref, br_ref, bi_ref, *rest):
        (dabr_refs, dabi_refs, dwin_refs, dwoutt_refs), outs = [rest[2 * k:2 * k + 2] for k in range(4)], rest[8:]
        dar_ref, dai_ref, dldt_ref, dbr_ref, dbi_ref, dcr_ref, dci_ref = outs
        a_r, a_i = ar_ref[...], ai_ref[...]
        dt, mag, cs, sn, abr, abi, den, nr, qr, qi = _ssm_zoh(a_r, a_i, ldt_ref[...])
        mask = _ssm_group_mask()
        col = lax.broadcasted_iota(jnp.int32, (SSM_CH, SSM_STATE), 0)
        state = lax.broadcasted_iota(jnp.int32, (SSM_CH, SSM_STATE), 1)
        own_state = (col % SSM_STATE == state).astype(BF16)

        def pick(dense):
            m = jnp.where(mask, dense, 0.0)
            hi = m.astype(BF16)
            lo = m - hi.astype(F32)
            return _dot(hi, own_state) + _dot(lo, own_state)

        def picked(refs, half):
            cols = slice(half * SSM_CH, (half + 1) * SSM_CH)
            return jnp.concatenate([pick(ref[:, cols]) for ref in refs], axis=0)

        first_channel = lax.broadcasted_iota(jnp.int32, (SSM_HALF, SSM_CH), 0) % SSM_GROUP == 0

        def first_rows(refs):
            return jnp.concatenate(
                [pick(jnp.where(first_channel, jnp.broadcast_to(ref[...], (SSM_HALF, SSM_CH)), 0.0)) for ref in refs], axis=0)

        g_r, g_i = picked(dwin_refs, 0), picked(dwin_refs, 1)
        dcr_ref[...] = picked(dwoutt_refs, 0)
        dci_ref[...] = -picked(dwoutt_refs, 1)
        b_r, b_i = br_ref[...], bi_ref[...]
        dbr_ref[...] = g_r * qr + g_i * qi
        dbi_ref[...] = g_i * qr - g_r * qi
        gqr = g_r * b_r + g_i * b_i
        gqi = g_i * b_r - g_r * b_i
        g_nr_num = gqr / den
        g_ni_num = gqi / den
        g_den = -(gqr * qr + gqi * qi) / den
        g_nr = g_nr_num * a_r - g_ni_num * a_i
        g_abi = g_nr_num * a_i + g_ni_num * a_r
        d_ar = g_nr_num * nr + g_ni_num * abi + 2.0 * a_r * g_den
        d_ai = g_nr_num * abi - g_ni_num * nr + 2.0 * a_i * g_den
        g_abr = first_rows(dabr_refs) + g_nr
        g_abi = first_rows(dabi_refs) + g_abi
        g_mag = g_abr * cs + g_abi * sn
        g_ang = mag * (g_abi * cs - g_abr * sn)
        g_e = g_mag * mag
        d_ar = d_ar + g_e * dt
        d_ai = d_ai + g_ang * dt
        g_dt = g_e * a_r + g_ang * a_i
        dar_ref[...] = d_ar
        dai_ref[...] = d_ai
        dldt_ref[...] = g_dt * dt

    vm = BS(memory_space=pltpu.VMEM)
    vec = SDS((SSM_ROWS, SSM_STATE), F32)
    return pl.pallas_call(body, out_shape=(vec,) * 7, in_specs=[vm] * 13, out_specs=(vm,) * 7, name="ssm_prep_bwd",
                          compiler_params=_params())(a_r, a_i, ldt, b_r, b_i, *d_abr, *d_abi, *d_win, *d_woutt)


SCAN_ROWS = 512
SCAN_SUB = 128


def _ssm_scan(name, inp, w1, a_r, a_i, w2, reverse):
    t = inp.shape[0]
    rows = min(SCAN_ROWS, t)
    n = t // rows
    n_sub = rows // SCAN_SUB
    ch = SSM_CH
    at = (lambda i: (n - 1 - i, 0)) if reverse else (lambda i: (i, 0))

    def body(in_ref, w1_ref, ar_ref, ai_ref, w2_ref, sb_ref, out_ref, cr_ref, ci_ref, k_ref, st_ref):
        i = pl.program_id(0)

        @pl.when(i == 0)
        def _():
            ar8 = jnp.broadcast_to(ar_ref[...], (8, ch))
            ai8 = jnp.broadcast_to(ai_ref[...], (8, ch))
            row = lax.broadcasted_iota(jnp.int32, (8, ch), 0)
            rank = (7 - row) if reverse else row
            powers = [(ar8, ai8)]
            for _ in range(7):
                p_r, p_i = powers[-1]
                powers.append((p_r * ar8 - p_i * ai8, p_r * ai8 + p_i * ar8))
            zero = jnp.zeros((8, ch), F32)
            for slot, k in enumerate((1, 2, 4)):
                k_ref[2 * slot] = jnp.where(rank >= k, powers[k - 1][0], zero)
                k_ref[2 * slot + 1] = jnp.where(rank >= k, powers[k - 1][1], zero)
            carry_r, carry_i = zero, zero
            for j in range(8):
                carry_r = jnp.where(rank == j, powers[j][0], carry_r)
                carry_i = jnp.where(rank == j, powers[j][1], carry_i)
            k_ref[6] = carry_r
            k_ref[7] = carry_i
            cr_ref[...] = zero
            ci_ref[...] = zero

        def group(r0, carry):
            c_r, c_i = carry
            x_r = st_ref[pl.ds(r0, 8), 0:ch]
            x_i = st_ref[pl.ds(r0, 8), ch:2 * ch]
            for slot, k in enumerate((1, 2, 4)):
                shift = (8 - k) if reverse else k
                s_r = pltpu.roll(x_r, shift, 0)
                s_i = pltpu.roll(x_i, shift, 0)
                m_r, m_i = k_ref[2 * slot], k_ref[2 * slot + 1]
                x_r, x_i = x_r + m_r * s_r - m_i * s_i, x_i + m_r * s_i + m_i * s_r
            p_r, p_i = k_ref[6], k_ref[7]
            x_r, x_i = x_r + p_r * c_r - p_i * c_i, x_i + p_r * c_i + p_i * c_r
            st_ref[pl.ds(r0, 8), 0:ch] = x_r
            st_ref[pl.ds(r0, 8), ch:2 * ch] = x_i
            last = 0 if reverse else 7
            return (jnp.broadcast_to(x_r[last:last + 1, :], (8, ch)), jnp.broadcast_to(x_i[last:last + 1, :], (8, ch)))

        carry = (cr_ref[...], ci_ref[...])
        for sc in (range(n_sub - 1, -1, -1) if reverse else range(n_sub)):
            part = pl.ds(sc * SCAN_SUB, SCAN_SUB)
            st_ref[part, :] = _dot(in_ref[part, :], w1_ref[...])
            for gi in range(SCAN_SUB // 8):
                g = (SCAN_SUB // 8 - 1 - gi) if reverse else gi
                carry = group(sc * SCAN_SUB + g * 8, carry)
            states = st_ref[part, :].astype(BF16)
            sb_ref[part, :] = states
            out_ref[part, :] = _dot(states, w2_ref[...])
        cr_ref[...] = carry[0]
        ci_ref[...] = carry[1]

    return pl.pallas_call(
        body, out_shape=(SDS((t, 2 * ch), BF16), SDS((t, D_SSM), F32)), grid=(n,),
        in_specs=[BS((rows, D_SSM), at), BS((D_SSM, 2 * ch), lambda i: (0, 0)), BS((1, ch), lambda i: (0, 0)),
                  BS((1, ch), lambda i: (0, 0)), BS((2 * ch, D_SSM), lambda i: (0, 0))],
        out_specs=(BS((rows, 2 * ch), at), BS((rows, D_SSM), at)),
        scratch_shapes=[pltpu.VMEM((8, ch), F32), pltpu.VMEM((8, ch), F32), pltpu.VMEM((8, 8, ch), F32),
                        pltpu.VMEM((rows, 2 * ch), F32)],
        name=name, compiler_params=_params("arbitrary"))(inp, w1, a_r, a_i, w2)


DA_ROWS = 1024


def _ssm_param_grads(name, lam, states, u, dy, reverse, after=()):
    t = lam.shape[0]
    rows = min(DA_ROWS, t)
    n = t // rows
    halo_rows = 16
    nb = rows // halo_rows
    ch = SSM_CH
    if reverse:
        halo_at = lambda i: (jnp.minimum((i + 1) * nb, t // halo_rows - 1), 0)
    else:
        halo_at = lambda i: (jnp.maximum(i * nb - 1, 0), 0)

    after_ops, after_specs = _after_operands(after)

    def body(lam_ref, x_ref, halo_ref, u_ref, dy_ref, *rest):
        dr_ref, di_ref, dwin_ref, dwoutt_ref = rest[len(after_ops):]
        i = pl.program_id(0)

        @pl.when(i == 0)
        def _():
            dr_ref[...] = jnp.zeros_like(dr_ref)
            di_ref[...] = jnp.zeros_like(di_ref)
            dwin_ref[...] = jnp.zeros_like(dwin_ref)
            dwoutt_ref[...] = jnp.zeros_like(dwoutt_ref)

        dwin_ref[...] += _dot(u_ref[...], lam_ref[...], TN)
        dwoutt_ref[...] += _dot(dy_ref[...], x_ref[...], TN)
        row = lax.broadcasted_iota(jnp.int32, (rows, ch), 0)
        if reverse:
            edge, shift, h_row, live = rows - 1, rows - 1, 0, i < n - 1
        else:
            edge, shift, h_row, live = 0, 1, halo_rows - 1, i > 0

        def neighbour(lo):
            halo = halo_ref[:, lo:lo + ch].astype(F32)[h_row:h_row + 1]
            halo = jnp.where(live, halo, 0.0)
            x = x_ref[:, lo:lo + ch].astype(F32)
            return jnp.where(row == edge, jnp.broadcast_to(halo, (rows, ch)), pltpu.roll(x, shift, 0))

        xp_r, xp_i = neighbour(0), neighbour(ch)
        l_r, l_i = lam_ref[:, 0:ch].astype(F32), lam_ref[:, ch:2 * ch].astype(F32)
        dr_ref[...] += jnp.sum(l_r * xp_r + l_i * xp_i, axis=0, keepdims=True)
        di_ref[...] += jnp.sum(l_i * xp_r - l_r * xp_i, axis=0, keepdims=True)

    blk = BS((rows, 2 * ch), lambda i: (i, 0))
    thin = BS((rows, D_SSM), lambda i: (i, 0))
    vec = BS((1, ch), lambda i: (0, 0))
    mat = BS((D_SSM, 2 * ch), lambda i: (0, 0))
    return pl.pallas_call(
        body, out_shape=(SDS((1, ch), F32), SDS((1, ch), F32), SDS((D_SSM, 2 * ch), F32), SDS((D_SSM, 2 * ch), F32)),
        grid=(n,), in_specs=[blk, blk, BS((halo_rows, 2 * ch), halo_at), thin, thin] + after_specs,
        out_specs=(vec, vec, mat, mat),
        name=name, compiler_params=_params("arbitrary"))(lam, states, states, u, dy, *after_ops)


GELU_C = math.sqrt(2.0 / math.pi)
GELU_K = 0.044715


def _ssm_combine(proj, y_fwd, y_bwd, d_skip, tm, after=()):
    t = proj.shape[0]
    after_ops, after_specs = _after_operands(after)

    def body(s_ref, yf_ref, yb_ref, d_ref, *rest):
        yt_ref, g_ref = rest[len(after_ops):]
        y = s_ref[...] * d_ref[...] + yf_ref[...] + yb_ref[...]
        yt_ref[...] = y
        th = jnp.tanh(GELU_C * (y + GELU_K * y * y * y))
        g_ref[...] = (0.5 * y * (1.0 + th)).astype(BF16)

    blk = BS((tm, D_SSM), lambda i: (i, 0))
    return pl.pallas_call(
        body, out_shape=(SDS((t, D_SSM), F32), SDS((t, D_SSM), BF16)), grid=(t // tm,),
        in_specs=[BS((tm, D_SSM), lambda i: (i, D_POOL // D_SSM)), blk, blk, BS((1, D_SSM), lambda i: (0, 0))] + after_specs,
        out_specs=(blk, blk), name="ssm_combine",
        compiler_params=_params("parallel"))(proj, y_fwd, y_bwd, d_skip, *after_ops)


def _ssm_ds(proj, d_yt, du_fwd, du_bwd, d_skip, tm):
    t = proj.shape[0]

    def body(s_ref, dy_ref, duf_ref, dub_ref, d_ref, ds_ref, dd_ref):
        i = pl.program_id(0)
        dy = dy_ref[...]
        ds_ref[...] = (dy * d_ref[...] + duf_ref[...] + dub_ref[...]).astype(BF16)

        @pl.when(i == 0)
        def _():
            dd_ref[...] = jnp.zeros_like(dd_ref)

        dd_ref[...] += jnp.sum(dy * s_ref[...], axis=0, keepdims=True)

    blk = BS((tm, D_SSM), lambda i: (i, 0))
    vec = BS((1, D_SSM), lambda i: (0, 0))
    return pl.pallas_call(
        body, out_shape=(SDS((t, D_SSM), BF16), SDS((1, D_SSM), F32)), grid=(t // tm,),
        in_specs=[BS((tm, D_SSM), lambda i: (i, D_POOL // D_SSM)), blk, blk, blk, vec],
        out_specs=(blk, vec), name="ssm_ds", compiler_params=_params("arbitrary"))(proj, d_yt, du_fwd, du_bwd, d_skip)


G_POOL_AT = D_POOL + D_SSM
G_SSM_AT = G_POOL_AT + D_MODEL
E_VAL, E_GATE = D_POOL, D_POOL + D_SSM


def _merge_specs(tm):
    return [BS((tm, D_POOL), lambda i: (i, 0)), BS((tm, D_SSM), lambda i: (i, 0)),
            BS((N_SHARD, 1024, 256), lambda i: (0, 0, 0)), BS((tm, D_FF), lambda i: (i, 0))]


def _merge_parts(s, ms, yv, w_ref, proj_ref):
    lo = 256 * s
    zp = _dot(ms, w_ref[s, 0:E_VAL, :])
    zv = _dot(yv, w_ref[s, E_VAL:E_GATE, :])
    zg = _dot(yv, w_ref[s, E_GATE:, :])
    return zp, zv, zg, proj_ref[:, G_POOL_AT + lo:G_POOL_AT + lo + 256], proj_ref[:, G_SSM_AT + lo:G_SSM_AT + lo + 256]


def _mixer_merge(ms, yssm, w_e, proj, tm):
    t = ms.shape[0]

    def body(ms_ref, y_ref, w_ref, proj_ref, o_ref):
        msv, yv = ms_ref[...], y_ref[...]
        for s in range(N_SHARD):
            zp, zv, zg, gp, gs = _merge_parts(s, msv, yv, w_ref, proj_ref)
            o_ref[:, 256 * s:256 * (s + 1)] = (_sigmoid(gp) * zp + _sigmoid(gs) * zv * _sigmoid(zg)).astype(BF16)

    row = BS((tm, D_MODEL), lambda i: (i, 0))
    return pl.pallas_call(
        body, out_shape=SDS((t, D_MODEL), BF16), grid=(t // tm,), in_specs=_merge_specs(tm), out_specs=row,
        name="mixer_merge", compiler_params=_params("parallel"))(ms, yssm, w_e, proj)


def _mixer_merge_bwd(ms, yssm, w_e, proj, dmerged, tm):
    t = ms.shape[0]

    def body(ms_ref, y_ref, w_ref, proj_ref, dm_ref, dgp_ref, dgs_ref, dzp_ref, dzv_ref, dzg_ref):
        msv, yv = ms_ref[...], y_ref[...]
        for s in range(N_SHARD):
            cols = slice(256 * s, 256 * (s + 1))
            zp, zv, zg, gp, gs = _merge_parts(s, msv, yv, w_ref, proj_ref)
            dm = dm_ref[:, cols].astype(F32)
            sp, ss, sg = _sigmoid(gp), _sigmoid(gs), _sigmoid(zg)
            dgp_ref[:, cols] = (dm * zp * sp * (1.0 - sp)).astype(BF16)
            dgs_ref[:, cols] = (dm * zv * sg * ss * (1.0 - ss)).astype(BF16)
            dzp_ref[:, cols] = (dm * sp).astype(BF16)
            dz = dm * ss
            dzv_ref[:, cols] = (dz * sg).astype(BF16)
            dzg_ref[:, cols] = (dz * zv * sg * (1.0 - sg)).astype(BF16)

    row = BS((tm, D_MODEL), lambda i: (i, 0))
    shape = SDS((t, D_MODEL), BF16)
    return pl.pallas_call(
        body, out_shape=(shape,) * 5, grid=(t // tm,), in_specs=_merge_specs(tm) + [row],
        out_specs=(row,) * 5, name="mixer_merge_bwd",
        compiler_params=_params("parallel"))(ms, yssm, w_e, proj, dmerged)


def _mixer_dw(ms, yssm, dzp, dzv, dzg, tm):
    t = ms.shape[0]
    tm = min(2 * tm, t)
    n_t = t // tm

    def body(ms_ref, y_ref, dzp_ref, dzv_ref, dzg_ref, o_ref, acc):
        i = pl.program_id(0)

        @pl.when(i == 0)
        def _():
            acc[...] = jnp.zeros_like(acc)

        msv, yv = ms_ref[...], y_ref[...]
        for s in range(N_SHARD):
            cols = slice(256 * s, 256 * (s + 1))
            acc[s, 0:E_VAL, :] += _dot(msv, dzp_ref[:, cols], TN)
            acc[s, E_VAL:E_GATE, :] += _dot(yv, dzv_ref[:, cols], TN)
            acc[s, E_GATE:, :] += _dot(yv, dzg_ref[:, cols], TN)

        @pl.when(i == n_t - 1)
        def _():
            o_ref[...] = acc[...].astype(BF16)

    row = BS((tm, D_MODEL), lambda i: (i, 0))
    full = BS((N_SHARD, 1024, 256), lambda i: (0, 0, 0))
    return pl.pallas_call(
        body, out_shape=SDS((N_SHARD, 1024, 256), BF16), grid=(n_t,),
        in_specs=[BS((tm, D_POOL), lambda i: (i, 0)), BS((tm, D_SSM), lambda i: (i, 0)), row, row, row],
        out_specs=full, scratch_shapes=[pltpu.VMEM((N_SHARD, 1024, 256), F32)],
        name="mixer_dw", compiler_params=_params("arbitrary"))(ms, yssm, dzp, dzv, dzg)


def _mixer_dx(dzp, dzv, dzg, w_e, y_total, tm):
    t = dzp.shape[0]

    def body(dzp_ref, dzv_ref, dzg_ref, w_ref, yt_ref, dms_ref, dy_ref):
        acc_ms, acc_y = None, None
        for s in range(N_SHARD):
            cols = slice(256 * s, 256 * (s + 1))
            part_ms = _dot(dzp_ref[:, cols], w_ref[s, 0:E_VAL, :], NT)
            part_y = _dot(dzv_ref[:, cols], w_ref[s, E_VAL:E_GATE, :], NT) + _dot(dzg_ref[:, cols], w_ref[s, E_GATE:, :], NT)
            acc_ms = part_ms if s == 0 else acc_ms + part_ms
            acc_y = part_y if s == 0 else acc_y + part_y
        dms_ref[...] = acc_ms.astype(BF16)
        y = yt_ref[...]
        th = jnp.tanh(GELU_C * (y + GELU_K * y * y * y))
        dgelu = 0.5 * (1.0 + th) + 0.5 * y * (1.0 - th * th) * GELU_C * (1.0 + 3.0 * GELU_K * y * y)
        dy_ref[...] = acc_y * dgelu

    row = BS((tm, D_MODEL), lambda i: (i, 0))
    return pl.pallas_call(
        body, out_shape=(SDS((t, D_POOL), BF16), SDS((t, D_SSM), F32)), grid=(t // tm,),
        in_specs=[row, row, row, BS((N_SHARD, 1024, 256), lambda i: (0, 0, 0)), BS((tm, D_SSM), lambda i: (i, 0))],
        out_specs=(BS((tm, D_POOL), lambda i: (i, 0)), BS((tm, D_SSM), lambda i: (i, 0))),
        name="mixer_dx", compiler_params=_params("parallel"))(dzp, dzv, dzg, w_e, y_total)


def _attn_probs(q_h, k_h):
    s = _dot(q_h, k_h, NT) * (1.0 / math.sqrt(HEAD_DIM))
    e = jnp.exp(s - jnp.max(s, axis=-1, keepdims=True))
    return e / jnp.sum(e, axis=-1, keepdims=True)


def _attn_fwd(q, kv, tm):
    t = q.shape[0]
    tm = min(2 * tm, t)
    m = kv.shape[0]

    def body(q_ref, kv_ref, o_ref):
        for hd in range(N_HEADS):
            lo = hd * HEAD_DIM
            p = _attn_probs(q_ref[:, lo:lo + HEAD_DIM], kv_ref[:, lo:lo + HEAD_DIM])
            o_ref[:, lo:lo + HEAD_DIM] = _dot(p, kv_ref[:, D_MODEL + lo:D_MODEL + lo + HEAD_DIM]).astype(BF16)

    return pl.pallas_call(
        body, out_shape=SDS((t, D_MODEL), BF16), grid=(t // tm,),
        in_specs=[BS((tm, D_MODEL), lambda i: (i, 0)), BS((m, 2 * D_MODEL), lambda i: (0, 0))],
        out_specs=BS((tm, D_MODEL), lambda i: (i, 0)), name="attn_fwd", compiler_params=_params("parallel"))(q, kv)


def _attn_bwd(q, kv, d_o, tm):
    t = q.shape[0]
    m = kv.shape[0]

    def body(q_ref, kv_ref, do_ref, dq_ref, dkv_ref):
        i = pl.program_id(0)

        @pl.when(i == 0)
        def _():
            dkv_ref[...] = jnp.zeros_like(dkv_ref)

        for hd in range(N_HEADS):
            lo = hd * HEAD_DIM
            q_h = q_ref[:, lo:lo + HEAD_DIM]
            k_h = kv_ref[:, lo:lo + HEAD_DIM]
            v_h = kv_ref[:, D_MODEL + lo:D_MODEL + lo + HEAD_DIM]
            do_h = do_ref[:, lo:lo + HEAD_DIM]
            p = _attn_probs(q_h, k_h)
            dkv_ref[:, D_MODEL + lo:D_MODEL + lo + HEAD_DIM] += _dot(p, do_h, TN)
            dp = _dot(do_h, v_h, NT)
            ds = p * (dp - jnp.sum(dp * p, axis=-1, keepdims=True)) * (1.0 / math.sqrt(HEAD_DIM))
            dq_ref[:, lo:lo + HEAD_DIM] = _dot(ds, k_h).astype(BF16)
            dkv_ref[:, lo:lo + HEAD_DIM] += _dot(ds, q_h, TN)

    row = BS((tm, D_MODEL), lambda i: (i, 0))
    full = BS((m, 2 * D_MODEL), lambda i: (0, 0))
    return pl.pallas_call(
        body, out_shape=(SDS((t, D_MODEL), BF16), SDS((m, 2 * D_MODEL), F32)), grid=(t // tm,),
        in_specs=[row, full, row], out_specs=(row, full), name="attn_bwd",
        compiler_params=_params("arbitrary"))(q, kv, d_o)


TRANSPOSED = ("ffn1_w_gate", "ffn1_w_up", "ffn2_w_gate", "ffn2_w_up", "w_in")
GATHER_PHASES = {"f1a": (("ffn1_w_gate",), ("ffn1_w_up",)),
                 "f1b": (("ffn1_w_down",),),
                 "win": (("w_in",),),
                 "mix": (("w_mix_out", "w_q", "w_xo"), ("w_kv",), ("w_pool_proj", "w_glu_val", "w_glu_gate")),
                 "f2": (("ffn2_w_gate",), ("ffn2_w_up",), ("ffn2_w_down",))}
REDUCE_GROUPS = (("ffn2_w_gate",), ("ffn2_w_up",), ("ffn2_w_down",), ("w_xo",), ("w_q",), ("w_kv",), ("w_mix_out",),
                 ("w_pool_proj", "w_glu_val", "w_glu_gate"), ("w_in",), ("ffn1_w_gate",), ("ffn1_w_up",), ("ffn1_w_down",))
SMALL = ("ffn1_norm", "mix_norm", "pool_w", "pool_scale", "ssm_a_re", "ssm_a_im", "ssm_log_dt", "ssm_b_re",
         "ssm_b_im", "ssm_c_re", "ssm_c_im", "ssm_d", "xattn_norm", "mem_norm", "ffn2_norm", "final_norm")
WEIGHTS = ("ffn1_norm", "ffn1_w_gate", "ffn1_w_up", "ffn1_w_down", "mix_norm", "w_in", "pool_w", "pool_scale",
           "w_pool_proj", "ssm_a_re", "ssm_a_im", "ssm_log_dt", "ssm_b_re", "ssm_b_im", "ssm_c_re", "ssm_c_im",
           "ssm_d", "w_glu_val", "w_glu_gate", "w_mix_out", "xattn_norm", "mem_norm", "w_q", "w_kv", "w_xo",
           "ffn2_norm", "ffn2_w_gate", "ffn2_w_up", "ffn2_w_down", "final_norm")


def _small_view(a, n):
    return jnp.swapaxes(a, 3, 4) if n in ("ssm_b_re", "ssm_b_im") else a


def _device_step(x, mem, target, wts, sp, reducer=None):
    t = x.shape[0]
    tm = min(TM, t)
    g = {}

    first_gather = wts.start("f1a")
    u1 = _rmsnorm("norm_ffn1", x, sp["ffn1_norm"], tm, after=first_gather)

    def per_channel(a):
        a = a.reshape(2 * SSM_GROUPS, 1, -1)
        return jnp.broadcast_to(a, (2 * SSM_GROUPS, SSM_GROUP, a.shape[-1])).reshape(SSM_ROWS, a.shape[-1])

    ssm_a = per_channel(sp["ssm_a_re"]), per_channel(sp["ssm_a_im"]), per_channel(sp["ssm_log_dt"])
    ssm_b = sp["ssm_b_re"].reshape(SSM_ROWS, SSM_STATE), sp["ssm_b_im"].reshape(SSM_ROWS, SSM_STATE)
    abr, abi, w_in_s, w_in_s_t, w_out_s_t, w_out_s = _ssm_prep(
        *ssm_a, *ssm_b, sp["ssm_c_re"].reshape(SSM_ROWS, SSM_STATE), sp["ssm_c_im"].reshape(SSM_ROWS, SSM_STATE),
        after=first_gather)
    first_rows = (2, SSM_GROUPS, SSM_GROUP, SSM_STATE)
    a_r = abr.reshape(first_rows)[:, :, 0].reshape(2, 1, SSM_CH)
    a_i = abi.reshape(first_rows)[:, :, 0].reshape(2, 1, SSM_CH)
    mem_n = _rmsnorm("norm_mem", mem, sp["mem_norm"], mem.shape[0], after=first_gather)

    whole = (D_FF, D_MODEL)
    w_g1, w_u1 = wts.finish("f1a", [u1, w_in_s, w_in_s_t, w_out_s, w_out_s_t, a_r, a_i, mem_n])
    w_f1 = {"gate": w_g1.reshape(whole), "up": w_u1.reshape(whole)}
    down_gather = wts.start("f1b", [w_g1])
    g1, up1, a1 = _ffn_up("ffn1_up", u1, w_f1, tm, after=wts.start("win", down_gather))
    (w_dn,) = wts.finish("f1b", [a1])
    w_f1["down"] = w_dn.reshape(whole)
    h1, u2 = _ffn_down("ffn1_down", a1, w_f1, x, tm, next_gain=sp["mix_norm"])

    (w_in_g,) = wts.finish("win", [u2])
    w_in_t = w_in_g.reshape(D_FF, D_MODEL)
    proj = _mix_in(u2, w_in_t, tm, after=wts.start("f2", wts.start("mix", [w_in_g])))
    pooled, mixed, ms = _pool_fwd(proj, sp["pool_w"][0], sp["pool_scale"])

    s_in = proj[:, D_POOL:D_POOL + D_SSM].astype(BF16)
    states, y_dirs = [], []
    for dr in range(2):
        st, yd = _ssm_scan(f"ssm_scan_fwd{dr}", s_in, w_in_s[dr], a_r[dr], a_i[dr], w_out_s[dr], reverse=(dr == 1))
        states.append(st)
        y_dirs.append(yd)
    w_sq, w_kv, w_e = wts.finish("mix", y_dirs)
    w_mo, w_q, w_xo = (w_sq[:, 256 * k:256 * (k + 1)].reshape(D_MODEL, D_MODEL) for k in range(3))
    w_d = w_kv[:, None]
    y_total, yssm = _ssm_combine(proj, y_dirs[0], y_dirs[1], sp["ssm_d"], tm)

    merged = _mixer_merge(ms, yssm, w_e, proj, tm)
    h2, u3 = _mm_resid_norm("mix_out", merged, w_mo, h1, sp["xattn_norm"], tm)

    q = _plain_mm("attn_q", u3, w_q, NN, BF16, tm)
    n_mem = mem.shape[0]
    kv = _mm("attn_kv", [(mem_n, BS((n_mem, D_MODEL), lambda s: (0, 0)), w_d, BS((None, None, D_MODEL, 512), lambda s: (s, 0, 0, 0)), NN)],
             grid=(N_SHARD,), out_shape=SDS((n_mem, 2 * D_MODEL), BF16), out_spec=BS((n_mem, 512), lambda s: (0, s)))
    o = _attn_fwd(q, kv, tm)
    h3, u4 = _mm_resid_norm("attn_out", o, w_xo, h2, sp["ffn2_norm"], tm)

    w_f2 = dict(zip(("gate", "up", "down"), (a.reshape(whole) for a in wts.finish("f2", [u4]))))
    g2, up2, a2 = _ffn_up("ffn2_up", u4, w_f2, tm)
    loss, dh4, dh4_b, g["final_norm"] = _ffn_down("ffn2_down", a2, w_f2, h3, tm,
                                                  head=(sp["final_norm"].reshape(1, D_MODEL), target))

    dg2, dup2 = _ffn_bwd_act("ffn2_bwd_act", dh4_b, w_f2, g2, up2, tm)
    dw_f2 = _ffn_dw("ffn2_dw", u4, dg2, dup2, a2, dh4_b, tm)
    dh3, dh3_b, g["ffn2_norm"] = _ffn_dx("ffn2_dx", dg2, dup2, w_f2, h3, sp["ffn2_norm"], dh4, tm)

    d_o = _plain_mm("attn_out_dx", dh3_b, w_xo, NT, BF16, tm)
    dw_xo = _dw_mm("attn_out_dw", o, dh3_b, tm)
    dq, dkv = _attn_bwd(q, kv, d_o, tm)
    dw_q = _dw_mm("attn_q_dw", u3, dq, tm)
    dh2, dh2_b, g["xattn_norm"] = _mm_norm_bwd("attn_q_dx", dq, w_q, NT, h2, sp["xattn_norm"], dh3, tm)
    dw_kv = _mm("attn_kv_dw", [(mem_n, BS((n_mem, D_MODEL), lambda s: (0, 0)), dkv, BS((n_mem, 512), lambda s: (0, s)), TN)],
                grid=(N_SHARD,), out_shape=SDS((N_SHARD, D_MODEL, 512), BF16), out_spec=BS((None, D_MODEL, 512), lambda s: (s, 0, 0)))
    dmem_n = _mm("attn_kv_dx", [(dkv, BS((n_mem, 512), lambda s: (0, s)), w_d, BS((None, None, D_MODEL, 512), lambda s: (s, 0, 0, 0)), NT)],
                 grid=(N_SHARD,), red_axis=0, out_shape=SDS((n_mem, D_MODEL), F32), out_spec=BS((n_mem, D_MODEL), lambda s: (0, 0)))
    _, _, g["mem_norm"] = _rmsnorm_bwd("norm_mem_bwd", mem, sp["mem_norm"], dmem_n, None, n_mem)

    square = (N_SHARD, D_MODEL // N_SHARD, D_MODEL)
    sharded = (N_SHARD, FF_SH, D_MODEL)
    early = [a.reshape(sharded) for a in dw_f2] + [dw_xo.reshape(square), dw_q.reshape(square), dw_kv]
    swapping = reducer.swap_start("a1", early) if reducer is not None else []
    dmerged = _plain_mm("mix_out_dx", dh2_b, w_mo, NT, BF16, tm, after=swapping)
    dw_mo = _dw_mm("mix_out_dw", merged, dh2_b, tm)
    d_gp, d_gs, dzp, dzv, dzg = _mixer_merge_bwd(ms, yssm, w_e, proj, dmerged, tm)
    dw_e = _mixer_dw(ms, yssm, dzp, dzv, dzg, tm)
    d_ms, d_yt = _mixer_dx(dzp, dzv, dzg, w_e, y_total, tm)
    dp, d_scale, d_pw = _pool_bwd(d_ms, mixed, pooled, sp["pool_w"][0], sp["pool_scale"])
    g["pool_scale"] = d_scale
    g["pool_w"] = d_pw[None]

    d_yt_b = d_yt.astype(BF16)
    du_dirs, lams = [], []
    for dr in range(2):
        lam, du = _ssm_scan(f"ssm_scan_bwd{dr}", d_yt_b, w_out_s_t[dr], a_r[dr], -a_i[dr], w_in_s_t[dr], reverse=(dr == 0))
        du_dirs.append(du)
        lams.append(lam)
    ds, g["ssm_d"] = _ssm_ds(proj, d_yt, du_dirs[0], du_dirs[1], sp["ssm_d"], tm)

    d_proj = jnp.concatenate([dp, ds, d_gp, d_gs], axis=1)
    tw = min(2 * tm, t)
    dw_in_t = _mm("mix_in_dw", [(d_proj, BS((tw, D_FF // 2), lambda j, i: (i, j)), u2, BS((tw, D_MODEL), lambda j, i: (i, 0)), TN)],
                  grid=(2, t // tw), red_axis=1, out_shape=SDS((D_FF, D_MODEL), BF16), out_spec=BS((D_FF // 2, D_MODEL), lambda j, i: (j, 0)))
    dh1, dh1_b, g["mix_norm"] = _mm_norm_bwd("mix_in_dx", d_proj, w_in_t, NN, h1, sp["mix_norm"], dh2, tm)

    early += [dw_mo.reshape(square), dw_e, dw_in_t.reshape(sharded)]
    g["final_norm"] = g["final_norm"].reshape(D_MODEL)

    travelling = reducer.start("a", early[6:], swapped=["a1"], after=list(g.values())) if reducer is not None else []
    d_abr, d_abi, d_cm, d_bm = [], [], [], []
    for dr in range(2):
        da_r, da_i, d_win, d_woutt = _ssm_param_grads(f"ssm_param_grads{dr}", lams[dr], states[dr], s_in, d_yt_b,
                                                      reverse=(dr == 1), after=travelling)
        d_abr.append(da_r)
        d_abi.append(da_i)
        d_bm.append(d_win)
        d_cm.append(d_woutt)

    d_ar, d_ai, d_ldt, d_br, d_bi, d_cr, d_ci = _ssm_prep_bwd(*ssm_a, *ssm_b, d_abr, d_abi, d_bm, d_cm)
    per_group = (2 * SSM_GROUPS, SSM_GROUP * SSM_STATE)
    g["ssm_a_re"] = d_ar.reshape(2 * SSM_GROUPS, SSM_GROUP, SSM_STATE).sum(axis=1).reshape(sp["ssm_a_re"].shape)
    g["ssm_a_im"] = d_ai.reshape(2 * SSM_GROUPS, SSM_GROUP, SSM_STATE).sum(axis=1).reshape(sp["ssm_a_im"].shape)
    g["ssm_log_dt"] = d_ldt.reshape(per_group).sum(axis=1).reshape(sp["ssm_log_dt"].shape)
    g["ssm_b_re"] = d_br.reshape(sp["ssm_b_re"].shape)
    g["ssm_b_im"] = d_bi.reshape(sp["ssm_b_im"].shape)
    g["ssm_c_re"] = d_cr.reshape(sp["ssm_c_re"].shape)
    g["ssm_c_im"] = d_ci.reshape(sp["ssm_c_im"].shape)
    if reducer is not None:
        travelling = travelling + [d_ar, d_br, d_cr]
    dg1, dup1 = _ffn_bwd_act("ffn1_bwd_act", dh1_b, w_f1, g1, up1, tm, after=travelling)
    dw_f1 = [a.reshape(sharded) for a in _ffn_dw("ffn1_dw", u1, dg1, dup1, a1, dh1_b, tm)]
    if reducer is not None:
        travelling = reducer.start("b", dw_f1, after=reducer.finish("a", dw_f1[:1]))
        travelling = travelling + reducer.join_start("a", after=travelling)
    grad_x, _, g["ffn1_norm"] = _ffn_dx("ffn1_dx", dg1, dup1, w_f1, x, sp["ffn1_norm"], dh1, tm, after=travelling)
    if reducer is not None:
        reducer.join_finish("a", [grad_x])
    return loss, grad_x, early + dw_f1, g


def _mesh_place():
    x, y, c = lax.axis_index("x"), lax.axis_index("y"), lax.axis_index("c")
    chips = [(1 - x, y), (x, 1 - y), (1 - x, 1 - y)]
    return x, y, c, chips


def _remote(src, dst, send_sems, recv_sems, k, to):
    return pltpu.make_async_remote_copy(src_ref=src, dst_ref=dst, send_sem=send_sems.at[k], recv_sem=recv_sems.at[k],
                                        device_id=to, device_id_type=MESH)


def _sibling_swap_halves(tag, grads, after=()):
    n = len(grads)
    after_ops, after_specs = _after_operands(after)

    def body(*refs):
        ins, outs = refs[:n], refs[n + len(after_ops):2 * n + len(after_ops)]
        send_sems, recv_sems = refs[2 * n + len(after_ops):]
        x, y, c, _ = _mesh_place()
        sibling = (x, y, 1 - c)
        copies = []
        for k in range(n):
            half = grads[k].shape[1] // 2
            theirs = pl.ds(pl.multiple_of((1 - c) * half, 16), half)
            cp = _remote(ins[k].at[:, theirs, :], outs[k], send_sems, recv_sems, k, sibling)
            cp.start()
            copies.append(cp)
        for cp in copies:
            cp.wait_recv()
        for cp in copies:
            cp.wait_send()

    hbm = BS(memory_space=pl.ANY)
    return pl.pallas_call(
        body, out_shape=tuple(SDS((g.shape[0], g.shape[1] // 2, g.shape[2]), g.dtype) for g in grads),
        in_specs=[hbm] * n + after_specs, out_specs=(hbm,) * n,
        scratch_shapes=[pltpu.SemaphoreType.DMA((n,)), pltpu.SemaphoreType.DMA((n,))],
        name="reduce_sibling_send_" + tag, compiler_params=_params())(*grads, *after_ops)


def _row_tile(rows, cap=512):
    return max(r for r in range(16, cap + 1, 16) if rows % r == 0)


REDUCE_STEPS = 2


def _chip_presum(tag, grads, gots, c_idx):
    n = len(grads)
    halves = [g.shape[1] // 2 for g in grads]
    tiles = [(h // REDUCE_STEPS, g.shape[2]) for h, g in zip(halves, grads)]

    def body(c_ref, *refs):
        for k in range(n):
            refs[2 * n + k][...] = (refs[k][...].astype(F32) + refs[n + k][...].astype(F32)).astype(BF16)

    mine = [BS((None, None) + tile, lambda s, i, c_ref: (s, c_ref[0], i, 0)) for tile in tiles]
    plain = [BS((None,) + tile, lambda s, i, c_ref: (s, i, 0)) for tile in tiles]
    return list(pl.pallas_call(
        body, out_shape=tuple(SDS((g.shape[0], h, g.shape[2]), BF16) for g, h in zip(grads, halves)),
        grid_spec=pltpu.PrefetchScalarGridSpec(num_scalar_prefetch=1, grid=(N_SHARD, REDUCE_STEPS),
                                               in_specs=mine + plain, out_specs=plain),
        name="reduce_presum_" + tag, compiler_params=_params("parallel", "parallel"))(
            c_idx, *[g.reshape(g.shape[0], 2, h, g.shape[2]) for g, h in zip(grads, halves)], *gots))


HBM_SPEC = BS(memory_space=pltpu.HBM)
SEM_SPEC = BS(memory_space=pltpu.SEMAPHORE)
DATAFLOW = pltpu.SideEffectType.DATAFLOW_SIDE_EFFECTING


def _chip_exchange_copies(parts, lands, send_sems, recv_sems):
    _, _, c, chips = _mesh_place()
    return [_remote(parts[k].at[2 * px + py], lands[k].at[j], send_sems, recv_sems, 3 * k + j, (px, py, c))
            for k in range(len(parts)) for j, (px, py) in enumerate(chips)]


def _gather_copies(shards, lands, send_sems, recv_sems):
    x, y, c, chips = _mesh_place()
    return [_remote(shards[k], lands[k].at[2 * x + y], send_sems, recv_sems, 3 * k + j, (px, py, c))
            for k in range(len(shards)) for j, (px, py) in enumerate(chips)]


def _gather_half_copies(shards, lands, send_sems, recv_sems):
    x, y, c, chips = _mesh_place()
    out = []
    for k in range(len(shards)):
        half = shards[k].shape[0] // 2
        mine = pl.ds(pl.multiple_of(c * half, 16), half)
        for j, (px, py) in enumerate(chips):
            out.append(_remote(shards[k].at[mine, :], lands[k].at[2 * x + y, mine, :], send_sems, recv_sems,
                               3 * k + j, (px, py, c)))
    return out


def _sibling_fill(tag, lands):
    n = len(lands)

    def body(*refs):
        outs = refs[n:2 * n]
        send_sems, recv_sems = refs[2 * n:]
        x, y, c, chips = _mesh_place()
        copies = []
        for k in range(n):
            half = lands[k].shape[1] // 2
            mine = pl.ds(pl.multiple_of(c * half, 16), half)
            for j, (px, py) in enumerate(chips):
                blk = outs[k].at[2 * px + py, mine, :]
                copies.append(_remote(blk, blk, send_sems, recv_sems, 3 * k + j, (x, y, 1 - c)))
        for cp in copies:
            cp.start()
        for cp in copies:
            cp.wait_recv()
        for cp in copies:
            cp.wait_send()

    hbm = BS(memory_space=pl.ANY)
    return list(pl.pallas_call(
        body, out_shape=tuple(SDS(a.shape, a.dtype) for a in lands),
        in_specs=[hbm] * n, out_specs=(hbm,) * n, input_output_aliases={k: k for k in range(n)},
        scratch_shapes=[pltpu.SemaphoreType.DMA((3 * n,)), pltpu.SemaphoreType.DMA((3 * n,))],
        name="gather_fill_" + tag, compiler_params=_params())(*lands))


def _swap_copies(grads, lands, send_sems, recv_sems):
    x, y, c, _ = _mesh_place()
    out = []
    for k in range(len(grads)):
        half = grads[k].shape[1] // 2
        theirs = pl.ds(pl.multiple_of((1 - c) * half, 16), half)
        out.append(_remote(grads[k].at[:, theirs, :], lands[k], send_sems, recv_sems, k, (x, y, 1 - c)))
    return out


def _join_copies(fulls, same, send_sems, recv_sems):
    x, y, c, _ = _mesh_place()
    out = []
    for k in range(len(fulls)):
        half = fulls[k].shape[0] // 2
        mine = fulls[k].at[pl.ds(pl.multiple_of(c * half, 8), half), :]
        out.append(_remote(mine, mine, send_sems, recv_sems, k, (x, y, 1 - c)))
    return out


def _everyone_copies(packs, lands, send_sems, recv_sems):
    x, y, c, _ = _mesh_place()
    out = []
    for k in range(len(packs)):
        for j in range(N_DEV - 1):
            bx, by, bc = (j + 1) >> 2 & 1, (j + 1) >> 1 & 1, (j + 1) & 1
            peer = (x ^ bx, y ^ by, c ^ bc)
            out.append(_remote(packs[k], lands[k].at[4 * x + 2 * y + c], send_sems, recv_sems, (N_DEV - 1) * k + j, peer))
    return out


def _split_start(name, copies, sources, land_shapes, after=(), fanout=3):
    n = len(sources)
    n_land = len(land_shapes)
    m = n + n_land
    n_sems = fanout * n
    after_ops, after_specs = _after_operands(after)

    def body(*refs):
        ins = refs[:n]
        lands = refs[n:m] if n_land else ins
        send_sems, recv_sems = refs[m + len(after_ops)], refs[m + len(after_ops) + 1]
        token = refs[-1]
        for cp in copies(ins, lands, send_sems, recv_sems):
            cp.start()
        token[...] = jnp.zeros_like(token)

    lands = [pltpu.with_memory_space_constraint(lax.empty(s, d), pltpu.HBM) for s, d in land_shapes]
    sources = [pltpu.with_memory_space_constraint(p, pltpu.HBM) for p in sources]
    thru = [pltpu.HBM(a.shape, a.dtype) for a in sources + lands]
    out = pl.pallas_call(
        body, name=name,
        out_shape=(pltpu.SemaphoreType.DMA((n_sems,)), pltpu.SemaphoreType.DMA((n_sems,)), *thru, SDS((8, 128), F32)),
        in_specs=[HBM_SPEC] * m + after_specs,
        out_specs=(SEM_SPEC, SEM_SPEC, *[HBM_SPEC] * m, BS(memory_space=pltpu.VMEM)),
        input_output_aliases={i: 2 + i for i in range(m)},
        compiler_params=pltpu.CompilerParams(has_side_effects=DATAFLOW))(*sources, *lands, *after_ops)
    return out[0], out[1], list(out[2:2 + n]), list(out[2 + n:2 + m]), out[-1]


def _split_wait(name, copies, send_sems, recv_sems, sources, lands, after):
    n = len(sources)
    m = n + len(lands)
    after_ops, after_specs = _after_operands(after)

    def body(*refs):
        ins = refs[:n]
        zones = refs[n:m] if m > n else ins
        for cp in copies(ins, zones, refs[m], refs[m + 1]):
            cp.wait_send()
            cp.wait_recv()

    out = pl.pallas_call(
        body, name=name,
        out_shape=tuple(pltpu.HBM(a.shape, a.dtype) for a in sources + lands),
        in_specs=[HBM_SPEC] * m + [SEM_SPEC, SEM_SPEC] + after_specs, out_specs=(HBM_SPEC,) * m,
        input_output_aliases={i: i for i in range(m)},
        compiler_params=pltpu.CompilerParams(has_side_effects=DATAFLOW))(*sources, *lands, send_sems, recv_sems, *after_ops)
    return list(out[:n]), list(out[n:])


class _WeightGatherer:
    def __init__(self, shards):
        self.shards, self.open = shards, {}
        self.me = 2 * lax.axis_index("x") + lax.axis_index("y")

    HALVED = ("f1a", "win", "mix")

    def start(self, tag, after=()):
        shapes = [((N_SHARD,) + s.shape, s.dtype) for s in self.shards[tag]]
        copies = _gather_half_copies if tag in self.HALVED else _gather_copies
        self.open[tag] = _split_start("gather_start_" + tag, copies, self.shards[tag], shapes, after)
        return [self.open[tag][-1]]

    def finish(self, tag, after):
        send_sems, recv_sems, shards, lands, _ = self.open.pop(tag)
        copies = _gather_half_copies if tag in self.HALVED else _gather_copies
        shards, lands = _split_wait("gather_wait_" + tag, copies, send_sems, recv_sems, shards, lands, after)
        if tag in self.HALVED:
            lands = _sibling_fill(tag, lands)
        return [lax.dynamic_update_slice(zone, s[None], (self.me, 0, 0)) for zone, s in zip(lands, shards)]


class _GradReducer:
    def __init__(self):
        self.c_idx = lax.axis_index("c").astype(jnp.int32).reshape(1)
        self.place = jnp.stack([2 * lax.axis_index("x") + lax.axis_index("y"), lax.axis_index("c")]).astype(jnp.int32)
        self.swaps, self.open, self.landed, self.joins, self.reduced = {}, {}, {}, {}, []

    def swap_start(self, tag, grads, after=()):
        shapes = [((g.shape[0], g.shape[1] // 2, g.shape[2]), g.dtype) for g in grads]
        self.swaps[tag] = _split_start("reduce_swap_start_" + tag, _swap_copies, grads, shapes, after, fanout=1)
        return [self.swaps[tag][-1]]

    def start(self, tag, grads, after=(), swapped=()):
        pairs = []
        for s in swapped:
            send_sems, recv_sems, early, lands, _ = self.swaps.pop(s)
            pairs += zip(*_split_wait("reduce_swap_wait_" + s, _swap_copies, send_sems, recv_sems, early, lands, grads[-1:]))
        pairs += zip(grads, _sibling_swap_halves(tag, grads, after))
        parts = _chip_presum(tag, [g for g, _ in pairs], [s for _, s in pairs], self.c_idx)
        shapes = [((3,) + p.shape[1:], p.dtype) for p in parts]
        self.open[tag] = _split_start("reduce_exchange_start_" + tag, _chip_exchange_copies, parts, shapes)
        return [self.open[tag][-1]]

    def finish(self, tag, after):
        send_sems, recv_sems, parts, lands, _ = self.open.pop(tag)
        self.landed[tag] = _split_wait("reduce_exchange_wait_" + tag, _chip_exchange_copies, send_sems, recv_sems, parts, lands, after)
        return self.landed[tag][1][:1]

    def _sums(self, tag, after=()):
        parts, landed = self.landed.pop(tag)
        return _chip_sum(tag, parts, landed, self.place, after)

    def join_start(self, tag, after=()):
        self.joins[tag] = _split_start("reduce_join_start_" + tag, _join_copies, self._sums(tag, after), [], fanout=1)
        return [self.joins[tag][-1]]

    def join_finish(self, tag, after):
        send_sems, recv_sems, fulls, _, _ = self.joins.pop(tag)
        self.reduced += _split_wait("reduce_join_wait_" + tag, _join_copies, send_sems, recv_sems, fulls, [], after)[0]

    def join(self, tag, after=()):
        self.reduced += _sibling_join_halves(self._sums(tag), after)


def _chip_sum(tag, parts, gots, place, after=()):
    n = len(parts)
    tiles = [(p.shape[1] // REDUCE_STEPS, p.shape[2]) for p in parts]
    after_ops, after_specs = _after_operands(after)

    def body(place_ref, *refs):
        outs = refs[2 * n + len(after_ops):]
        for k in range(n):
            acc = refs[k][...].astype(F32)
            for j in range(3):
                acc = acc + refs[n + k][j].astype(F32)
            outs[k][...] = acc

    return list(pl.pallas_call(
        body, out_shape=tuple(SDS((2 * p.shape[1], p.shape[2]), F32) for p in parts),
        grid_spec=pltpu.PrefetchScalarGridSpec(
            num_scalar_prefetch=1, grid=(REDUCE_STEPS,),
            in_specs=[BS((None,) + tile, lambda i, place_ref: (place_ref[0], i, 0)) for tile in tiles]
            + [BS((3,) + tile, lambda i, place_ref: (0, i, 0)) for tile in tiles] + after_specs,
            out_specs=[BS(tile, lambda i, place_ref: (place_ref[1] * REDUCE_STEPS + i, 0)) for tile in tiles]),
        name="reduce_sum_" + tag, compiler_params=_params("parallel"))(place, *parts, *gots, *after_ops))


def _sibling_join_halves(fulls, after=()):
    n = len(fulls)
    after_ops, after_specs = _after_operands(after)

    def body(*refs):
        outs = refs[n + len(after_ops):2 * n + len(after_ops)]
        send_sems, recv_sems = refs[2 * n + len(after_ops):]
        copies = _join_copies(outs, outs, send_sems, recv_sems)
        for cp in copies:
            cp.start()
        for cp in copies:
            cp.wait_recv()
        for cp in copies:
            cp.wait_send()

    hbm = BS(memory_space=pl.ANY)
    return list(pl.pallas_call(
        body, out_shape=tuple(SDS(f.shape, f.dtype) for f in fulls),
        in_specs=[hbm] * n + after_specs, out_specs=(hbm,) * n, input_output_aliases={k: k for k in range(n)},
        scratch_shapes=[pltpu.SemaphoreType.DMA((n,)), pltpu.SemaphoreType.DMA((n,))],
        name="reduce_sibling_join", compiler_params=_params())(*fulls, *after_ops))


N_DEV = 8


def _sum_devices(packs):
    _, rows, lanes = packs.shape

    def body(p_ref, o_ref):
        acc = p_ref[0]
        for dev in range(1, N_DEV):
            acc = acc + p_ref[dev]
        o_ref[...] = acc

    vm = BS(memory_space=pltpu.VMEM)
    return pl.pallas_call(body, out_shape=SDS((rows, lanes), F32), in_specs=[vm], out_specs=vm,
                          name="small_sum", compiler_params=_params())(packs)


def _adamw_refs(w_ref, g_ref, m_ref, v_ref, go_ref, d_ref, mo_ref, vo_ref):
    bc1 = 1.0 - ADAM_B1 ** ADAM_STEP
    bc2 = 1.0 - ADAM_B2 ** ADAM_STEP
    g = g_ref[...]
    m_new = ADAM_B1 * m_ref[...] + (1.0 - ADAM_B1) * g
    v_new = ADAM_B2 * v_ref[...] + (1.0 - ADAM_B2) * (g * g)
    go_ref[...] = g
    mo_ref[...] = m_new
    vo_ref[...] = v_new
    d_ref[...] = -ADAM_LR * ((m_new / bc1) / (jnp.sqrt(v_new / bc2) + ADAM_EPS) + ADAM_WD * w_ref[...])


def _adamw_small(ws, gs, ms, vs):
    n = len(ws)

    def body(*refs):
        for k in range(n):
            _adamw_refs(*[refs[j * n + k] for j in range(4)], *refs[4 * n + 4 * k:4 * n + 4 * k + 4])

    vm = BS(memory_space=pltpu.VMEM)
    outs = pl.pallas_call(
        body, out_shape=tuple(SDS(a.shape, F32) for a in ws for _ in range(4)), in_specs=[vm] * (4 * n),
        out_specs=(vm,) * (4 * n), name="adamw_small", compiler_params=_params())(*ws, *gs, *ms, *vs)
    return [outs[4 * k:4 * k + 4] for k in range(n)]


def _adamw(name, w, grad, row0, m, v, after=()):
    rows, cols = w.shape
    tr = rows if rows < 16 else _row_tile(rows, 352)
    after_ops, after_specs = _after_operands(after)

    def body(w_ref, g_ref, m_ref, v_ref, *rest):
        _adamw_refs(w_ref, g_ref, m_ref, v_ref, *rest[len(after_ops):])

    blk = BS((tr, cols), lambda i: (i, 0))
    shape = SDS((rows, cols), F32)
    return pl.pallas_call(
        body, out_shape=(shape,) * 4, grid=(rows // tr,),
        in_specs=[blk, BS((tr, cols), lambda i: (row0 // tr + i, 0)), blk, blk] + after_specs, out_specs=(blk,) * 4,
        name=name, compiler_params=_params("parallel"))(w, grad, m, v, *after_ops)


SMALL_LANES = 128


def _pack_small(parts):
    flat = jnp.concatenate([jnp.ravel(p) for p in parts])
    rows = -(-flat.shape[0] // (64 * SMALL_LANES)) * 64
    return jnp.pad(flat, (0, rows * SMALL_LANES - flat.shape[0])).reshape(rows, SMALL_LANES)


def _unpack_small(packed, like):
    flat = jnp.ravel(packed)
    out, at = [], 0
    for p in like:
        out.append(flat[at:at + p.size].reshape(p.shape))
        at += p.size
    return out


def kernel(x, mem, ffn1_norm, ffn1_w_gate, ffn1_w_up, ffn1_w_down, mix_norm, w_in, pool_w, pool_scale, w_pool_proj, ssm_a_re, ssm_a_im, ssm_log_dt, ssm_b_re, ssm_b_im, ssm_c_re, ssm_c_im, ssm_d, w_glu_val, w_glu_gate, w_mix_out, xattn_norm, mem_norm, w_q, w_kv, w_xo, ffn2_norm, ffn2_w_gate, ffn2_w_up, ffn2_w_down, final_norm, loss_target, m_ffn1_norm, m_ffn1_w_gate, m_ffn1_w_up, m_ffn1_w_down, m_mix_norm, m_w_in, m_pool_w, m_pool_scale, m_w_pool_proj, m_ssm_a_re, m_ssm_a_im, m_ssm_log_dt, m_ssm_b_re, m_ssm_b_im, m_ssm_c_re, m_ssm_c_im, m_ssm_d, m_w_glu_val, m_w_glu_gate, m_w_mix_out, m_xattn_norm, m_mem_norm, m_w_q, m_w_kv, m_w_xo, m_ffn2_norm, m_ffn2_w_gate, m_ffn2_w_up, m_ffn2_w_down, m_final_norm, v_ffn1_norm, v_ffn1_w_gate, v_ffn1_w_up, v_ffn1_w_down, v_mix_norm, v_w_in, v_pool_w, v_pool_scale, v_w_pool_proj, v_ssm_a_re, v_ssm_a_im, v_ssm_log_dt, v_ssm_b_re, v_ssm_b_im, v_ssm_c_re, v_ssm_c_im, v_ssm_d, v_w_glu_val, v_w_glu_gate, v_w_mix_out, v_xattn_norm, v_mem_norm, v_w_q, v_w_kv, v_w_xo, v_ffn2_norm, v_ffn2_w_gate, v_ffn2_w_up, v_ffn2_w_down, v_final_norm):
    given = dict(locals())
    w = {n: given[n] for n in WEIGHTS}
    m = {n: given["m_" + n] for n in WEIGHTS}
    v = {n: given["v_" + n] for n in WEIGHTS}

    def shard_view(a, n):
        return a[0].T if n in TRANSPOSED else a[0]

    def shard_unview(a, n):
        return (a.T if n in TRANSPOSED else a)[None]

    shards = {tag: [jnp.concatenate([shard_view(w[n], n).astype(BF16) for n in grp], axis=0) for grp in arrays]
              for tag, arrays in GATHER_PHASES.items()}
    reducer = _GradReducer()
    ws, ms, vs = ({n: _small_view(a[n], n) for n in SMALL} for a in (w, m, v))
    loss_part, grad_x, _, small = _device_step(x[0], mem[0], loss_target[0], _WeightGatherer(shards), ws, reducer)

    small_like = [ws[n] for n in SMALL] + [loss_part[0, :1]]
    pack = _pack_small([small[n] for n in SMALL] + [loss_part[0, :1]])
    everyone = _split_start("small_start", _everyone_copies, [pack], [((N_DEV,) + pack.shape, F32)], fanout=N_DEV - 1)

    grads, delta, new_m, new_v = {}, {}, {}, {}
    big_done = []

    def update(groups, reduced):
        for grp, red in zip(groups, reduced):
            row0 = 0
            for n in grp:
                w_n = shard_view(w[n], n)
                outs = _adamw("adamw_" + n, w_n, red, row0, shard_view(m[n], n), shard_view(v[n], n), after=everyone[-1:])
                grads[n], delta[n], new_m[n], new_v[n] = (shard_unview(o, n) for o in outs)
                big_done.append(outs[1])
                row0 += w_n.shape[0]

    n_a = len(reducer.reduced)
    update(REDUCE_GROUPS[:n_a], reducer.reduced)
    reducer.finish("b", list(big_done))
    reducer.join("b")
    update(REDUCE_GROUPS[n_a:], reducer.reduced[n_a:])

    send_sems, recv_sems, packs, landed, _ = everyone
    packs, landed = _split_wait("small_wait", _everyone_copies, send_sems, recv_sems, packs, landed, big_done)
    mine = 4 * lax.axis_index("x") + 2 * lax.axis_index("y") + lax.axis_index("c")
    summed = _sum_devices(lax.dynamic_update_slice(landed[0], packs[0][None], (mine, 0, 0)))
    g_small = dict(zip(SMALL + ("loss",), _unpack_small(summed, small_like)))
    loss = g_small.pop("loss").reshape(())
    def two_d(a):
        return a.reshape(-1, a.shape[-1])

    updated = _adamw_small(*([two_d(a[n]) for n in SMALL] for a in (ws, g_small, ms, vs)))
    for n, outs in zip(SMALL, updated):
        grads[n], delta[n], new_m[n], new_v[n] = (_small_view(o.reshape(ws[n].shape), n) for o in outs)

    return (loss, grad_x[None], *[grads[n] for n in WEIGHTS], *[delta[n] for n in WEIGHTS],
            *[new_m[n] for n in WEIGHTS], *[new_v[n] for n in WEIGHTS])
```

```python
import functools
import math

import jax
import jax.numpy as jnp
from jax import lax
from jax.experimental import pallas as pl
from jax.experimental.pallas import tpu as pltpu

F32 = jnp.float32
BF16 = jnp.bfloat16
SDS = jax.ShapeDtypeStruct
BS = pl.BlockSpec
MESH = pl.DeviceIdType.MESH

D_MODEL = 1024
D_FF = 2816
N_SHARD = 4
FF_SH = D_FF // N_SHARD
D_POOL = 512
POOL_WINDOWS = (2, 4, 8, 16)
POOL_GROUP = 128
D_SSM = 256
SSM_GROUPS = 16
SSM_GROUP = 16
SSM_STATE = 64
SSM_CH = SSM_GROUPS * SSM_STATE
N_HEADS = 4
HEAD_DIM = 256
EPS = 1e-6
ADAM_LR, ADAM_B1, ADAM_B2, ADAM_EPS, ADAM_WD, ADAM_STEP = 0.001, 0.9, 0.999, 1e-08, 0.01, 10

VMEM_LIMIT_V7X = 58 * 1024 * 1024
TM = 512

NN = (((1,), (0,)), ((), ()))
NT = (((1,), (1,)), ((), ()))
TN = (((0,), (0,)), ((), ()))


def _params(*sem):
    return pltpu.CompilerParams(dimension_semantics=sem if sem else None, vmem_limit_bytes=VMEM_LIMIT_V7X)


def _dot(a, b, dims=NN):
    return lax.dot_general(a.astype(BF16), b.astype(BF16), dims, preferred_element_type=F32)


def _sigmoid(v):
    return pl.reciprocal(1.0 + jnp.exp(-v), approx=True)


def _block_dims(spec):
    return tuple(d for d in spec.block_shape if d is not None)


def _after_operands(after):
    return list(after), [BS(memory_space=pl.ANY)] * len(after)


def _mm(name, pairs, *, grid, out_shape, out_spec, red_axis=None, extras=(), epilogue=None, after=()):
    n_pairs, n_extra = len(pairs), len(extras)
    n_red = grid[red_axis] if red_axis is not None else 1
    dims = [p[4] for p in pairs]

    def body(*refs):
        ab = refs[:2 * n_pairs]
        ex = refs[2 * n_pairs:2 * n_pairs + n_extra]
        o_ref = refs[2 * n_pairs + n_extra + len(after)]

        def partial():
            acc = None
            for p in range(n_pairs):
                t = _dot(ab[2 * p][...], ab[2 * p + 1][...], dims[p])
                acc = t if acc is None else acc + t
            return acc

        def finish(acc):
            res = epilogue(acc, *[e[...] for e in ex]) if epilogue is not None else acc
            o_ref[...] = res.astype(o_ref.dtype)

        if n_red == 1:
            finish(partial())
        else:
            acc_ref = refs[-1]
            k = pl.program_id(red_axis)

            @pl.when(k == 0)
            def _():
                acc_ref[...] = jnp.zeros_like(acc_ref)

            acc_ref[...] += partial()

            @pl.when(k == n_red - 1)
            def _():
                finish(acc_ref[...])

    operands, in_specs = [], []
    for a, a_spec, b, b_spec, _ in pairs:
        operands += [a, b]
        in_specs += [a_spec, b_spec]
    for e, e_spec in extras:
        operands.append(e)
        in_specs.append(e_spec)
    after_ops, after_specs = _after_operands(after)
    operands += after_ops
    in_specs += after_specs
    scratch = [pltpu.VMEM(_block_dims(out_spec), F32)] if n_red > 1 else []
    sem = tuple("arbitrary" if ax == red_axis else "parallel" for ax in range(len(grid)))
    return pl.pallas_call(body, out_shape=out_shape, grid=grid, in_specs=in_specs, out_specs=out_spec,
                          scratch_shapes=scratch, name=name, compiler_params=_params(*sem))(*operands)


def _rmsnorm(name, h, gain, tm, after=()):
    t, d = h.shape
    after_ops, after_specs = _after_operands(after)

    def body(h_ref, g_ref, *rest):
        u_ref = rest[-1]
        hv = h_ref[...]
        r = lax.rsqrt(jnp.mean(hv * hv, axis=-1, keepdims=True) + EPS)
        u_ref[...] = ((hv * r) * g_ref[...]).astype(u_ref.dtype)

    return pl.pallas_call(
        body, out_shape=SDS((t, d), BF16), grid=(t // tm,),
        in_specs=[BS((tm, d), lambda i: (i, 0)), BS((1, d), lambda i: (0, 0))] + after_specs,
        out_specs=BS((tm, d), lambda i: (i, 0)), name=name, compiler_params=_params("parallel"))(h, gain, *after_ops)


def _rmsnorm_bwd(name, h, gain, du, dh_in, tm):
    t, d = h.shape
    has_in = dh_in is not None

    def body(*refs):
        if has_in:
            h_ref, g_ref, du_ref, dhin_ref, dh_ref, dhb_ref, dg_ref = refs
        else:
            h_ref, g_ref, du_ref, dh_ref, dhb_ref, dg_ref = refs
        i = pl.program_id(0)
        hv = h_ref[...]
        r = lax.rsqrt(jnp.mean(hv * hv, axis=-1, keepdims=True) + EPS)
        n = hv * r
        duv = du_ref[...].astype(F32)
        dn = duv * g_ref[...]
        dh = r * (dn - n * jnp.mean(dn * n, axis=-1, keepdims=True))
        if has_in:
            dh = dhin_ref[...] + dh
        dh_ref[...] = dh
        dhb_ref[...] = dh.astype(BF16)

        @pl.when(i == 0)
        def _():
            dg_ref[...] = jnp.zeros_like(dg_ref)

        dg_ref[...] += jnp.sum(duv * n, axis=0, keepdims=True)

    row = BS((tm, d), lambda i: (i, 0))
    vec = BS((1, d), lambda i: (0, 0))
    operands = [h, gain, du] + ([dh_in] if has_in else [])
    in_specs = [row, vec, row] + ([row] if has_in else [])
    return pl.pallas_call(
        body, out_shape=(SDS((t, d), F32), SDS((t, d), BF16), SDS((1, d), F32)), grid=(t // tm,),
        in_specs=in_specs, out_specs=(row, row, vec), name=name, compiler_params=_params("arbitrary"))(*operands)


def _loss_head_tile(i, hv, g_ref, t_ref, loss_ref, dh_ref, dhb_ref, dg_ref):
    g = g_ref[...]
    r = lax.rsqrt(jnp.mean(hv * hv, axis=-1, keepdims=True) + EPS)
    n = hv * r
    err = n * g - t_ref[...]
    dy = err * (1.0 / hv.shape[-1])
    dn = dy * g
    dh = r * (dn - n * jnp.mean(dn * n, axis=-1, keepdims=True))
    dh_ref[...] = dh
    dhb_ref[...] = dh.astype(BF16)

    @pl.when(i == 0)
    def _():
        dg_ref[...] = jnp.zeros_like(dg_ref)
        loss_ref[...] = jnp.zeros_like(loss_ref)

    dg_ref[...] += jnp.sum(dy * n, axis=0, keepdims=True)
    part = 0.5 * jnp.sum(jnp.mean(err * err, axis=-1, keepdims=True), axis=0, keepdims=True)
    loss_ref[...] += jnp.broadcast_to(part, loss_ref.shape)


def _norm_tile(h, g_ref, u_ref):
    r = lax.rsqrt(jnp.mean(h * h, axis=-1, keepdims=True) + EPS)
    u_ref[...] = ((h * r) * g_ref[...]).astype(u_ref.dtype)


FFN_BLOCK = D_FF // 2


def _ffn_up(name, u, w_f, tm, after=()):
    t, d = u.shape
    after_ops, after_specs = _after_operands(after)

    def body(u_ref, wg_ref, wu_ref, *rest):
        pg_ref, pu_ref, a_ref = rest[len(after_ops):]
        uv = u_ref[...]
        for lo in range(0, D_FF, FFN_BLOCK):
            cols = slice(lo, lo + FFN_BLOCK)
            g = _dot(uv, wg_ref[cols, :], NT)
            up = _dot(uv, wu_ref[cols, :], NT)
            sg = _sigmoid(g)
            silu = g * sg
            a_ref[:, cols] = (silu * up).astype(BF16)
            pu_ref[:, cols] = (0.5 * silu).astype(BF16)
            pg_ref[:, cols] = (0.5 * sg * (1.0 + g * (1.0 - sg)) * up).astype(BF16)

    hid = BS((tm, D_FF), lambda i: (i, 0))
    shape = SDS((t, D_FF), BF16)
    whole = BS((D_FF, d), lambda i: (0, 0))
    return pl.pallas_call(
        body, out_shape=(shape, shape, shape), grid=(t // tm,),
        in_specs=[BS((tm, d), lambda i: (i, 0)), whole, whole] + after_specs,
        out_specs=(hid, hid, hid), name=name,
        compiler_params=_params("parallel"))(u, w_f["gate"], w_f["up"], *after_ops)


def _ffn_down(name, a, w_f, resid, tm, next_gain=None, head=None):
    t, d = resid.shape
    row = BS((tm, d), lambda i: (i, 0))
    vec = BS((1, d), lambda i: (0, 0))

    def body(a_ref, w_ref, res_ref, *rest):
        h = res_ref[...] + 0.5 * _dot(a_ref[...], w_ref[...])
        if head is not None:
            _loss_head_tile(pl.program_id(0), h, *rest)
        else:
            g_ref, h_ref, u_ref = rest
            h_ref[...] = h
            _norm_tile(h, g_ref, u_ref)

    if head is not None:
        extra, extra_specs = list(head), [vec, row]
        out_shape = (SDS((1, 128), F32), SDS((t, d), F32), SDS((t, d), BF16), SDS((1, d), F32))
        out_specs = (BS((1, 128), lambda i: (0, 0)), row, row, vec)
    else:
        extra, extra_specs = [next_gain], [vec]
        out_shape = (SDS((t, d), F32), SDS((t, d), BF16))
        out_specs = (row, row)
    return pl.pallas_call(
        body, out_shape=out_shape, grid=(t // tm,),
        in_specs=[BS((tm, D_FF), lambda i: (i, 0)), BS((D_FF, d), lambda i: (0, 0)), row] + extra_specs,
        out_specs=out_specs, name=name,
        compiler_params=_params("arbitrary" if head is not None else "parallel"))(a, w_f["down"], resid, *extra)


def _mix_in(u, w_t, tm, after=()):
    t, d = u.shape
    after_ops, after_specs = _after_operands(after)

    def body(u_ref, w_ref, *rest):
        o_ref = rest[-1]
        uv = u_ref[...]
        for lo in range(0, D_FF, FFN_BLOCK):
            o_ref[:, lo:lo + FFN_BLOCK] = _dot(uv, w_ref[lo:lo + FFN_BLOCK, :], NT)

    return pl.pallas_call(
        body, out_shape=SDS((t, D_FF), F32), grid=(t // tm,),
        in_specs=[BS((tm, d), lambda i: (i, 0)), BS((D_FF, d), lambda i: (0, 0))] + after_specs,
        out_specs=BS((tm, D_FF), lambda i: (i, 0)), name="mix_in",
        compiler_params=_params("parallel"))(u, w_t, *after_ops)


def _mm_resid_norm(name, a, b, resid, next_gain, tm):
    t, d = resid.shape
    tm = min(2 * tm, t)

    def body(a_ref, b_ref, res_ref, g_ref, h_ref, u_ref):
        h = res_ref[...] + _dot(a_ref[...], b_ref[...])
        h_ref[...] = h
        _norm_tile(h, g_ref, u_ref)

    row = BS((tm, d), lambda i: (i, 0))
    return pl.pallas_call(
        body, out_shape=(SDS((t, d), F32), SDS((t, d), BF16)), grid=(t // tm,),
        in_specs=[BS((tm, a.shape[1]), lambda i: (i, 0)), BS(b.shape, lambda i: (0, 0)), row, BS((1, d), lambda i: (0, 0))],
        out_specs=(row, row), name=name, compiler_params=_params("parallel"))(a, b, resid, next_gain)


def _ffn_bwd_act(name, dh_b, w_f, pg, pu, tm, after=()):
    t, d = dh_b.shape
    after_ops, after_specs = _after_operands(after)

    def body(dh_ref, wd_ref, pg_ref, pu_ref, *rest):
        dg_ref, dup_ref = rest[len(after_ops):]
        dh = dh_ref[...]
        for lo in range(0, D_FF, FFN_BLOCK):
            cols = slice(lo, lo + FFN_BLOCK)
            da = _dot(dh, wd_ref[cols, :], NT)
            dg_ref[:, cols] = (da * pg_ref[:, cols].astype(F32)).astype(BF16)
            dup_ref[:, cols] = (da * pu_ref[:, cols].astype(F32)).astype(BF16)

    hid = BS((tm, D_FF), lambda i: (i, 0))
    shape = SDS((t, D_FF), BF16)
    return pl.pallas_call(
        body, out_shape=(shape, shape), grid=(t // tm,),
        in_specs=[BS((tm, d), lambda i: (i, 0)), BS((D_FF, d), lambda i: (0, 0)), hid, hid] + after_specs,
        out_specs=(hid, hid), name=name,
        compiler_params=_params("parallel"))(dh_b, w_f["down"], pg, pu, *after_ops)


def _ffn_dw(name, u, dg, dup, a, dh_b, tm):
    t, d = u.shape
    n_t = t // tm

    def body(u_ref, dg_ref, dup_ref, a_ref, dh_ref, og_ref, ou_ref, od_ref, acc):
        i = pl.program_id(1)

        @pl.when(i == 0)
        def _():
            acc[...] = jnp.zeros_like(acc)

        uv = u_ref[...]
        acc[0] += _dot(dg_ref[...], uv, TN)
        acc[1] += _dot(dup_ref[...], uv, TN)
        acc[2] += _dot(a_ref[...], dh_ref[...], TN)

        @pl.when(i == n_t - 1)
        def _():
            og_ref[...] = acc[0].astype(BF16)
            ou_ref[...] = acc[1].astype(BF16)
            od_ref[...] = (0.5 * acc[2]).astype(BF16)

    hid = BS((tm, FFN_BLOCK), lambda j, i: (i, j))
    row = BS((tm, d), lambda j, i: (i, 0))
    out = BS((FFN_BLOCK, d), lambda j, i: (j, 0))
    shape = SDS((D_FF, d), BF16)
    return pl.pallas_call(
        body, out_shape=(shape, shape, shape), grid=(D_FF // FFN_BLOCK, n_t),
        in_specs=[row, hid, hid, hid, row], out_specs=(out, out, out),
        scratch_shapes=[pltpu.VMEM((3, FFN_BLOCK, d), F32)],
        name=name, compiler_params=_params("parallel", "arbitrary"))(u, dg, dup, a, dh_b)


def _norm_bwd_tile(i, du, h_ref, g_ref, dhin_ref, dh_ref, dhb_ref, dg_ref):
    hv = h_ref[...]
    r = lax.rsqrt(jnp.mean(hv * hv, axis=-1, keepdims=True) + EPS)
    n = hv * r
    dn = du * g_ref[...]
    dh = dhin_ref[...] + r * (dn - n * jnp.mean(dn * n, axis=-1, keepdims=True))
    dh_ref[...] = dh
    dhb_ref[...] = dh.astype(BF16)

    @pl.when(i == 0)
    def _():
        dg_ref[...] = jnp.zeros_like(dg_ref)

    dg_ref[...] += jnp.sum(du * n, axis=0, keepdims=True)


def _norm_bwd_specs(tm):
    row = BS((tm, D_MODEL), lambda i: (i, 0))
    vec = BS((1, D_MODEL), lambda i: (0, 0))
    return [row, vec, row], (row, row, vec)


def _norm_bwd_shapes(t):
    return SDS((t, D_MODEL), F32), SDS((t, D_MODEL), BF16), SDS((1, D_MODEL), F32)


def _ffn_dx(name, dg, dup, w_f, h, gain, dh_in, tm, after=()):
    t = dg.shape[0]
    tm = tm // 2
    after_ops, after_specs = _after_operands(after)

    def body(dg_ref, dup_ref, wg_ref, wu_ref, h_ref, g_ref, dhin_ref, *rest):
        du = _dot(dg_ref[...], wg_ref[...]) + _dot(dup_ref[...], wu_ref[...])
        _norm_bwd_tile(pl.program_id(0), du, h_ref, g_ref, dhin_ref, *rest[len(after_ops):])

    hid = BS((tm, D_FF), lambda i: (i, 0))
    whole = BS((D_FF, D_MODEL), lambda i: (0, 0))
    norm_in, norm_out = _norm_bwd_specs(tm)
    return pl.pallas_call(
        body, out_shape=_norm_bwd_shapes(t), grid=(t // tm,),
        in_specs=[hid, hid, whole, whole] + norm_in + after_specs, out_specs=norm_out, name=name,
        compiler_params=_params("arbitrary"))(dg, dup, w_f["gate"], w_f["up"], h, gain, dh_in, *after_ops)


def _mm_norm_bwd(name, a, b, dims, h, gain, dh_in, tm):
    t = a.shape[0]

    def body(a_ref, b_ref, h_ref, g_ref, dhin_ref, *outs):
        _norm_bwd_tile(pl.program_id(0), _dot(a_ref[...], b_ref[...], dims), h_ref, g_ref, dhin_ref, *outs)

    norm_in, norm_out = _norm_bwd_specs(tm)
    return pl.pallas_call(
        body, out_shape=_norm_bwd_shapes(t), grid=(t // tm,),
        in_specs=[BS((tm, a.shape[1]), lambda i: (i, 0)), BS(b.shape, lambda i: (0, 0))] + norm_in,
        out_specs=norm_out, name=name, compiler_params=_params("arbitrary"))(a, b, h, gain, dh_in)


def _plain_mm(name, a, b, dims, out_dtype, tm, resid=None, after=()):
    t = a.shape[0]
    tm = min(2 * tm, t)
    n = b.shape[1] if dims == NN else b.shape[0]
    extras = [(resid, BS((tm, n), lambda i: (i, 0)))] if resid is not None else []
    epi = (lambda acc, res: res + acc) if resid is not None else None
    return _mm(name, [(a, BS((tm, a.shape[1]), lambda i: (i, 0)), b, BS(b.shape, lambda i: (0, 0)), dims)],
               grid=(t // tm,), out_shape=SDS((t, n), out_dtype), out_spec=BS((tm, n), lambda i: (i, 0)),
               extras=extras, epilogue=epi, after=after)


def _dw_mm(name, a, b, tm, out_dtype=BF16, after=()):
    t, k = a.shape
    n = b.shape[1]
    tm = min(2 * tm, t)
    return _mm(name, [(a, BS((tm, k), lambda i: (i, 0)), b, BS((tm, n), lambda i: (i, 0)), TN)],
               grid=(t // tm,), red_axis=0, out_shape=SDS((k, n), out_dtype), out_spec=BS((k, n), lambda i: (0, 0)),
               after=after)


POOL_CHUNK = 256
POOL_HALO = 8


def _window_sum(v, width, lead):
    n = v.shape[0]
    s = v
    k = 1
    while k < width:
        s = s + pltpu.roll(s, n - k, 0)
        k *= 2
    return pltpu.roll(s, lead, 0) if lead else s


def _pool_count(base, left, right, t, shape):
    pos = base + lax.broadcasted_iota(jnp.int32, shape, 0)
    lo = jnp.maximum(pos - left, 0)
    hi = jnp.minimum(pos + right + 1, t)
    return (hi - lo).astype(F32)


def _pool_fwd(proj, pool_w, pool_scale):
    t = proj.shape[0]
    c, h = POOL_CHUNK, POOL_HALO
    n_chunks = t // c

    def body(proj_hbm, pw_ref, sc_ref, pooled_ref, mixed_ref, ms_ref, pad_ref, sem):
        cp = pltpu.make_async_copy(proj_hbm.at[:, pl.ds(0, D_POOL)], pad_ref.at[pl.ds(h, t), :], sem)
        cp.start()
        pad_ref[pl.ds(0, h), :] = jnp.zeros((h, D_POOL), F32)
        pad_ref[pl.ds(t + h, h), :] = jnp.zeros((h, D_POOL), F32)
        cp.wait()
        for g, width in enumerate(POOL_WINDOWS):
            left = width // 2
            right = width - 1 - left
            cols = slice(g * POOL_GROUP, (g + 1) * POOL_GROUP)
            wmat = pw_ref[g].astype(BF16)
            scale = sc_ref[:, cols]

            def chunk(ci, carry, left=left, right=right, width=width, cols=cols, wmat=wmat, scale=scale):
                base = pl.multiple_of(ci * c, c)
                v = pad_ref[pl.ds(base, c + 2 * h), cols]
                win = _window_sum(v, width, left)[h:h + c]
                cnt = _pool_count(base, left, right, t, (c, POOL_GROUP))
                pooled = (win / cnt - v[h:h + c]).astype(BF16)
                mixed = _dot(pooled, wmat)
                pooled_ref[pl.ds(base, c), cols] = pooled
                mixed_ref[pl.ds(base, c), cols] = mixed.astype(BF16)
                ms_ref[pl.ds(base, c), cols] = (mixed * scale).astype(BF16)
                return carry

            lax.fori_loop(0, n_chunks, chunk, 0)

    vm = BS(memory_space=pltpu.VMEM)
    shape = SDS((t, D_POOL), BF16)
    return pl.pallas_call(
        body, out_shape=(shape, shape, shape),
        in_specs=[BS(memory_space=pl.ANY), vm, vm], out_specs=(vm, vm, vm),
        scratch_shapes=[pltpu.VMEM((t + 2 * h, D_POOL), F32), pltpu.SemaphoreType.DMA],
        name="pool_fwd", compiler_params=_params())(proj, pool_w, pool_scale)


def _pool_bwd(d_ms, mixed, pooled, pool_w, pool_scale):
    t = d_ms.shape[0]
    c, h = POOL_CHUNK, POOL_HALO
    n_chunks = t // c

    def body(dms_ref, mixed_ref, pooled_ref, pw_ref, sc_ref, dp_ref, dsc_ref, dpw_ref, pad_ref):
        pad_ref[pl.ds(0, h), :] = jnp.zeros((h, D_POOL), F32)
        pad_ref[pl.ds(t + h, h), :] = jnp.zeros((h, D_POOL), F32)
        for g, width in enumerate(POOL_WINDOWS):
            left = width // 2
            right = width - 1 - left
            cols = slice(g * POOL_GROUP, (g + 1) * POOL_GROUP)
            wmat = pw_ref[g].astype(BF16)
            scale = sc_ref[:, cols]

            def first(ci, carry, left=left, right=right, cols=cols, wmat=wmat, scale=scale):
                dsc, dpw = carry
                base = pl.multiple_of(ci * c, c)
                dms = dms_ref[pl.ds(base, c), cols].astype(F32)
                dsc = dsc + jnp.sum(dms * mixed_ref[pl.ds(base, c), cols].astype(F32), axis=0, keepdims=True)
                dmix = (dms * scale).astype(BF16)
                dpw = dpw + _dot(pooled_ref[pl.ds(base, c), cols], dmix, TN)
                dpooled = _dot(dmix, wmat, NT)
                cnt = _pool_count(base, left, right, t, (c, POOL_GROUP))
                pad_ref[pl.ds(base + h, c), cols] = dpooled / cnt
                return dsc, dpw

            dsc, dpw = lax.fori_loop(0, n_chunks, first,
                                     (jnp.zeros((1, POOL_GROUP), F32), jnp.zeros((POOL_GROUP, POOL_GROUP), F32)))
            dsc_ref[:, cols] = dsc
            dpw_ref[g] = dpw

            def second(ci, carry, left=left, right=right, width=width, cols=cols):
                base = pl.multiple_of(ci * c, c)
                v = pad_ref[pl.ds(base, c + 2 * h), cols]
                win = _window_sum(v, width, right)[h:h + c]
                cnt = _pool_count(base, left, right, t, (c, POOL_GROUP))
                dp_ref[pl.ds(base, c), cols] = (win - v[h:h + c] * cnt).astype(BF16)
                return carry

            lax.fori_loop(0, n_chunks, second, 0)

    vm = BS(memory_space=pltpu.VMEM)
    return pl.pallas_call(
        body, out_shape=(SDS((t, D_POOL), BF16), SDS((1, D_POOL), F32), SDS((4, POOL_GROUP, POOL_GROUP), F32)),
        in_specs=[vm] * 5, out_specs=(vm, vm, vm),
        scratch_shapes=[pltpu.VMEM((t + 2 * h, D_POOL), F32)],
        name="pool_bwd", compiler_params=_params())(d_ms, mixed, pooled, pool_w, pool_scale)


SSM_ROWS = 2 * SSM_GROUPS * SSM_GROUP
SSM_HALF = SSM_GROUPS * SSM_GROUP


def _ssm_zoh(a_r, a_i, ldt):
    dt = jnp.exp(ldt)
    mag = jnp.exp(dt * a_r)
    ang = dt * a_i
    cs, sn = jnp.cos(ang), jnp.sin(ang)
    abr, abi = mag * cs, mag * sn
    den = a_r * a_r + a_i * a_i
    nr = abr - 1.0
    qr = (nr * a_r + abi * a_i) / den
    qi = (abi * a_r - nr * a_i) / den
    return dt, mag, cs, sn, abr, abi, den, nr, qr, qi


def _ssm_group_mask():
    row = lax.broadcasted_iota(jnp.int32, (SSM_HALF, SSM_CH), 0)
    col = lax.broadcasted_iota(jnp.int32, (SSM_HALF, SSM_CH), 1)
    return (row // SSM_GROUP) == (col // SSM_STATE)


def _ssm_prep(a_r, a_i, ldt, b_r, b_i, c_r, c_i, after=()):
    after_ops, after_specs = _after_operands(after)

    def body(ar_ref, ai_ref, ldt_ref, br_ref, bi_ref, cr_ref, ci_ref, *rest):
        abr_ref, abi_ref, win_ref, wint_ref, woutt_ref, wout_ref = rest[len(after_ops):]
        *_, abr, abi, _, _, qr, qi = _ssm_zoh(ar_ref[...], ai_ref[...], ldt_ref[...])
        abr_ref[...] = abr
        abi_ref[...] = abi
        b_r, b_i = br_ref[...], bi_ref[...]
        bbr = qr * b_r - qi * b_i
        bbi = qr * b_i + qi * b_r
        mask = _ssm_group_mask()
        state = lax.broadcasted_iota(jnp.int32, (SSM_STATE, SSM_CH), 0)
        col = lax.broadcasted_iota(jnp.int32, (SSM_STATE, SSM_CH), 1)
        every_group = (col % SSM_STATE == state).astype(BF16)

        def spread(x):
            return jnp.where(mask, _dot(x, every_group), 0.0)

        for d in range(2):
            rows = slice(d * SSM_HALF, (d + 1) * SSM_HALF)
            for half, x_in, x_out in ((0, bbr[rows], cr_ref[rows, :]), (1, bbi[rows], -ci_ref[rows, :])):
                cols = slice(half * SSM_CH, (half + 1) * SSM_CH)
                m_in, m_out = spread(x_in), spread(x_out)
                win_ref[d, :, cols] = m_in.astype(BF16)
                wint_ref[d, cols, :] = m_in.T.astype(BF16)
                woutt_ref[d, :, cols] = m_out.astype(BF16)
                wout_ref[d, cols, :] = m_out.T.astype(BF16)

    vm = BS(memory_space=pltpu.VMEM)
    vec = SDS((SSM_ROWS, SSM_STATE), F32)
    wide = SDS((2, SSM_HALF, 2 * SSM_CH), BF16)
    tall = SDS((2, 2 * SSM_CH, SSM_HALF), BF16)
    return pl.pallas_call(body, out_shape=(vec, vec, wide, tall, wide, tall), in_specs=[vm] * 7 + after_specs,
                          out_specs=(vm,) * 6, name="ssm_prep",
                          compiler_params=_params())(a_r, a_i, ldt, b_r, b_i, c_r, c_i, *after_ops)


def _ssm_prep_bwd(a_r, a_i, ldt, b_r, b_i, d_abr, d_abi, d_win, d_woutt):
    def body(ar_ref, ai_ref, ldt_ref, br_ref, bi_ref, *rest):
        (dabr_refs, dabi_refs, dwin_refs, dwoutt_refs), outs = [rest[2 * k:2 * k + 2] for k in range(4)], rest[8:]
        dar_ref, dai_ref, dldt_ref, dbr_ref, dbi_ref, dcr_ref, dci_ref = outs
        a_r, a_i = ar_ref[...], ai_ref[...]
        dt, mag, cs, sn, abr, abi, den, nr, qr, qi = _ssm_zoh(a_r, a_i, ldt_ref[...])
        mask = _ssm_group_mask()
        col = lax.broadcasted_iota(jnp.int32, (SSM_CH, SSM_STATE), 0)
        state = lax.broadcasted_iota(jnp.int32, (SSM_CH, SSM_STATE), 1)
        own_state = (col % SSM_STATE == state).astype(BF16)

        def pick(dense):
            m = jnp.where(mask, dense, 0.0)
            hi = m.astype(BF16)
            lo = m - hi.astype(F32)
            return _dot(hi, own_state) + _dot(lo, own_state)

        def picked(refs, half):
            cols = slice(half * SSM_CH, (half + 1) * SSM_CH)
            return jnp.concatenate([pick(ref[:, cols]) for ref in refs], axis=0)

        first_channel = lax.broadcasted_iota(jnp.int32, (SSM_HALF, SSM_CH), 0) % SSM_GROUP == 0

        def first_rows(refs):
            return jnp.concatenate(
                [pick(jnp.where(first_channel, jnp.broadcast_to(ref[...], (SSM_HALF, SSM_CH)), 0.0)) for ref in refs], axis=0)

        g_r, g_i = picked(dwin_refs, 0), picked(dwin_refs, 1)
        dcr_ref[...] = picked(dwoutt_refs, 0)
        dci_ref[...] = -picked(dwoutt_refs, 1)
        b_r, b_i = br_ref[...], bi_ref[...]
        dbr_ref[...] = g_r * qr + g_i * qi
        dbi_ref[...] = g_i * qr - g_r * qi
        gqr = g_r * b_r + g_i * b_i
        gqi = g_i * b_r - g_r * b_i
        g_nr_num = gqr / den
        g_ni_num = gqi / den
        g_den = -(gqr * qr + gqi * qi) / den
        g_nr = g_nr_num * a_r - g_ni_num * a_i
        g_abi = g_nr_num * a_i + g_ni_num * a_r
        d_ar = g_nr_num * nr + g_ni_num * abi + 2.0 * a_r * g_den
        d_ai = g_nr_num * abi - g_ni_num * nr + 2.0 * a_i * g_den
        g_abr = first_rows(dabr_refs) + g_nr
        g_abi = first_rows(dabi_refs) + g_abi
        g_mag = g_abr * cs + g_abi * sn
        g_ang = mag * (g_abi * cs - g_abr * sn)
        g_e = g_mag * mag
        d_ar = d_ar + g_e * dt
        d_ai = d_ai + g_ang * dt
        g_dt = g_e * a_r + g_ang * a_i
        dar_ref[...] = d_ar
        dai_ref[...] = d_ai
        dldt_ref[...] = g_dt * dt

    vm = BS(memory_space=pltpu.VMEM)
    vec = SDS((SSM_ROWS, SSM_STATE), F32)
    return pl.pallas_call(body, out_shape=(vec,) * 7, in_specs=[vm] * 13, out_specs=(vm,) * 7, name="ssm_prep_bwd",
                          compiler_params=_params())(a_r, a_i, ldt, b_r, b_i, *d_abr, *d_abi, *d_win, *d_woutt)


SCAN_ROWS = 512
SCAN_SUB = 128


def _ssm_scan(name, inp, w1, a_r, a_i, w2, reverse):
    t = inp.shape[0]
    rows = min(SCAN_ROWS, t)
    n = t // rows
    n_sub = rows // SCAN_SUB
    ch = SSM_CH
    at = (lambda i: (n - 1 - i, 0)) if reverse else (lambda i: (i, 0))

    def body(in_ref, w1_ref, ar_ref, ai_ref, w2_ref, sb_ref, out_ref, cr_ref, ci_ref, k_ref, st_ref):
        i = pl.program_id(0)

        @pl.when(i == 0)
        def _():
            ar8 = jnp.broadcast_to(ar_ref[...], (8, ch))
            ai8 = jnp.broadcast_to(ai_ref[...], (8, ch))
            row = lax.broadcasted_iota(jnp.int32, (8, ch), 0)
            rank = (7 - row) if reverse else row
            powers = [(ar8, ai8)]
            for _ in range(7):
                p_r, p_i = powers[-1]
                powers.append((p_r * ar8 - p_i * ai8, p_r * ai8 + p_i * ar8))
            zero = jnp.zeros((8, ch), F32)
            for slot, k in enumerate((1, 2, 4)):
                k_ref[2 * slot] = jnp.where(rank >= k, powers[k - 1][0], zero)
                k_ref[2 * slot + 1] = jnp.where(rank >= k, powers[k - 1][1], zero)
            carry_r, carry_i = zero, zero
            for j in range(8):
                carry_r = jnp.where(rank == j, powers[j][0], carry_r)
                carry_i = jnp.where(rank == j, powers[j][1], carry_i)
            k_ref[6] = carry_r
            k_ref[7] = carry_i
            cr_ref[...] = zero
            ci_ref[...] = zero

        def group(r0, carry):
            c_r, c_i = carry
            x_r = st_ref[pl.ds(r0, 8), 0:ch]
            x_i = st_ref[pl.ds(r0, 8), ch:2 * ch]
            for slot, k in enumerate((1, 2, 4)):
                shift = (8 - k) if reverse else k
                s_r = pltpu.roll(x_r, shift, 0)
                s_i = pltpu.roll(x_i, shift, 0)
                m_r, m_i = k_ref[2 * slot], k_ref[2 * slot + 1]
                x_r, x_i = x_r + m_r * s_r - m_i * s_i, x_i + m_r * s_i + m_i * s_r
            p_r, p_i = k_ref[6], k_ref[7]
            x_r, x_i = x_r + p_r * c_r - p_i * c_i, x_i + p_r * c_i + p_i * c_r
            st_ref[pl.ds(r0, 8), 0:ch] = x_r
            st_ref[pl.ds(r0, 8), ch:2 * ch] = x_i
            last = 0 if reverse else 7
            return (jnp.broadcast_to(x_r[last:last + 1, :], (8, ch)), jnp.broadcast_to(x_i[last:last + 1, :], (8, ch)))

        carry = (cr_ref[...], ci_ref[...])
        for sc in (range(n_sub - 1, -1, -1) if reverse else range(n_sub)):
            part = pl.ds(sc * SCAN_SUB, SCAN_SUB)
            st_ref[part, :] = _dot(in_ref[part, :], w1_ref[...])
            for gi in range(SCAN_SUB // 8):
                g = (SCAN_SUB // 8 - 1 - gi) if reverse else gi
                carry = group(sc * SCAN_SUB + g * 8, carry)
            states = st_ref[part, :].astype(BF16)
            sb_ref[part, :] = states
            out_ref[part, :] = _dot(states, w2_ref[...])
        cr_ref[...] = carry[0]
        ci_ref[...] = carry[1]

    return pl.pallas_call(
        body, out_shape=(SDS((t, 2 * ch), BF16), SDS((t, D_SSM), F32)), grid=(n,),
        in_specs=[BS((rows, D_SSM), at), BS((D_SSM, 2 * ch), lambda i: (0, 0)), BS((1, ch), lambda i: (0, 0)),
                  BS((1, ch), lambda i: (0, 0)), BS((2 * ch, D_SSM), lambda i: (0, 0))],
        out_specs=(BS((rows, 2 * ch), at), BS((rows, D_SSM), at)),
        scratch_shapes=[pltpu.VMEM((8, ch), F32), pltpu.VMEM((8, ch), F32), pltpu.VMEM((8, 8, ch), F32),
                        pltpu.VMEM((rows, 2 * ch), F32)],
        name=name, compiler_params=_params("arbitrary"))(inp, w1, a_r, a_i, w2)


DA_ROWS = 1024


def _ssm_param_grads(name, lam, states, u, dy, reverse, after=()):
    t = lam.shape[0]
    rows = min(DA_ROWS, t)
    n = t // rows
    halo_rows = 16
    nb = rows // halo_rows
    ch = SSM_CH
    if reverse:
        halo_at = lambda i: (jnp.minimum((i + 1) * nb, t // halo_rows - 1), 0)
    else:
        halo_at = lambda i: (jnp.maximum(i * nb - 1, 0), 0)

    after_ops, after_specs = _after_operands(after)

    def body(lam_ref, x_ref, halo_ref, u_ref, dy_ref, *rest):
        dr_ref, di_ref, dwin_ref, dwoutt_ref = rest[len(after_ops):]
        i = pl.program_id(0)

        @pl.when(i == 0)
        def _():
            dr_ref[...] = jnp.zeros_like(dr_ref)
            di_ref[...] = jnp.zeros_like(di_ref)
            dwin_ref[...] = jnp.zeros_like(dwin_ref)
            dwoutt_ref[...] = jnp.zeros_like(dwoutt_ref)

        dwin_ref[...] += _dot(u_ref[...], lam_ref[...], TN)
        dwoutt_ref[...] += _dot(dy_ref[...], x_ref[...], TN)
        row = lax.broadcasted_iota(jnp.int32, (rows, ch), 0)
        if reverse:
            edge, shift, h_row, live = rows - 1, rows - 1, 0, i < n - 1
        else:
            edge, shift, h_row, live = 0, 1, halo_rows - 1, i > 0

        def neighbour(lo):
            halo = halo_ref[:, lo:lo + ch].astype(F32)[h_row:h_row + 1]
            halo = jnp.where(live, halo, 0.0)
            x = x_ref[:, lo:lo + ch].astype(F32)
            return jnp.where(row == edge, jnp.broadcast_to(halo, (rows, ch)), pltpu.roll(x, shift, 0))

        xp_r, xp_i = neighbour(0), neighbour(ch)
        l_r, l_i = lam_ref[:, 0:ch].astype(F32), lam_ref[:, ch:2 * ch].astype(F32)
        dr_ref[...] += jnp.sum(l_r * xp_r + l_i * xp_i, axis=0, keepdims=True)
        di_ref[...] += jnp.sum(l_i * xp_r - l_r * xp_i, axis=0, keepdims=True)

    blk = BS((rows, 2 * ch), lambda i: (i, 0))
    thin = BS((rows, D_SSM), lambda i: (i, 0))
    vec = BS((1, ch), lambda i: (0, 0))
    mat = BS((D_SSM, 2 * ch), lambda i: (0, 0))
    return pl.pallas_call(
        body, out_shape=(SDS((1, ch), F32), SDS((1, ch), F32), SDS((D_SSM, 2 * ch), F32), SDS((D_SSM, 2 * ch), F32)),
        grid=(n,), in_specs=[blk, blk, BS((halo_rows, 2 * ch), halo_at), thin, thin] + after_specs,
        out_specs=(vec, vec, mat, mat),
        name=name, compiler_params=_params("arbitrary"))(lam, states, states, u, dy, *after_ops)


GELU_C = math.sqrt(2.0 / math.pi)
GELU_K = 0.044715


def _ssm_combine(proj, y_fwd, y_bwd, d_skip, tm, after=()):
    t = proj.shape[0]
    after_ops, after_specs = _after_operands(after)

    def body(s_ref, yf_ref, yb_ref, d_ref, *rest):
        yt_ref, g_ref = rest[len(after_ops):]
        y = s_ref[...] * d_ref[...] + yf_ref[...] + yb_ref[...]
        yt_ref[...] = y
        th = jnp.tanh(GELU_C * (y + GELU_K * y * y * y))
        g_ref[...] = (0.5 * y * (1.0 + th)).astype(BF16)

    blk = BS((tm, D_SSM), lambda i: (i, 0))
    return pl.pallas_call(
        body, out_shape=(SDS((t, D_SSM), F32), SDS((t, D_SSM), BF16)), grid=(t // tm,),
        in_specs=[BS((tm, D_SSM), lambda i: (i, D_POOL // D_SSM)), blk, blk, BS((1, D_SSM), lambda i: (0, 0))] + after_specs,
        out_specs=(blk, blk), name="ssm_combine",
        compiler_params=_params("parallel"))(proj, y_fwd, y_bwd, d_skip, *after_ops)


def _ssm_ds(proj, d_yt, du_fwd, du_bwd, d_skip, tm):
    t = proj.shape[0]

    def body(s_ref, dy_ref, duf_ref, dub_ref, d_ref, ds_ref, dd_ref):
        i = pl.program_id(0)
        dy = dy_ref[...]
        ds_ref[...] = (dy * d_ref[...] + duf_ref[...] + dub_ref[...]).astype(BF16)

        @pl.when(i == 0)
        def _():
            dd_ref[...] = jnp.zeros_like(dd_ref)

        dd_ref[...] += jnp.sum(dy * s_ref[...], axis=0, keepdims=True)

    blk = BS((tm, D_SSM), lambda i: (i, 0))
    vec = BS((1, D_SSM), lambda i: (0, 0))
    return pl.pallas_call(
        body, out_shape=(SDS((t, D_SSM), BF16), SDS((1, D_SSM), F32)), grid=(t // tm,),
        in_specs=[BS((tm, D_SSM), lambda i: (i, D_POOL // D_SSM)), blk, blk, blk, vec],
        out_specs=(blk, vec), name="ssm_ds", compiler_params=_params("arbitrary"))(proj, d_yt, du_fwd, du_bwd, d_skip)


G_POOL_AT = D_POOL + D_SSM
G_SSM_AT = G_POOL_AT + D_MODEL
E_VAL, E_GATE = D_POOL, D_POOL + D_SSM


def _merge_specs(tm):
    return [BS((tm, D_POOL), lambda i: (i, 0)), BS((tm, D_SSM), lambda i: (i, 0)),
            BS((N_SHARD, 1024, 256), lambda i: (0, 0, 0)), BS((tm, D_FF), lambda i: (i, 0))]


def _merge_parts(s, ms, yv, w_ref, proj_ref):
    lo = 256 * s
    zp = _dot(ms, w_ref[s, 0:E_VAL, :])
    zv = _dot(yv, w_ref[s, E_VAL:E_GATE, :])
    zg = _dot(yv, w_ref[s, E_GATE:, :])
    return zp, zv, zg, proj_ref[:, G_POOL_AT + lo:G_POOL_AT + lo + 256], proj_ref[:, G_SSM_AT + lo:G_SSM_AT + lo + 256]


def _mixer_merge(ms, yssm, w_e, proj, tm):
    t = ms.shape[0]

    def body(ms_ref, y_ref, w_ref, proj_ref, o_ref):
        msv, yv = ms_ref[...], y_ref[...]
        for s in range(N_SHARD):
            zp, zv, zg, gp, gs = _merge_parts(s, msv, yv, w_ref, proj_ref)
            o_ref[:, 256 * s:256 * (s + 1)] = (_sigmoid(gp) * zp + _sigmoid(gs) * zv * _sigmoid(zg)).astype(BF16)

    row = BS((tm, D_MODEL), lambda i: (i, 0))
    return pl.pallas_call(
        body, out_shape=SDS((t, D_MODEL), BF16), grid=(t // tm,), in_specs=_merge_specs(tm), out_specs=row,
        name="mixer_merge", compiler_params=_params("parallel"))(ms, yssm, w_e, proj)


def _mixer_merge_bwd(ms, yssm, w_e, proj, dmerged, tm):
    t = ms.shape[0]

    def body(ms_ref, y_ref, w_ref, proj_ref, dm_ref, dgp_ref, dgs_ref, dzp_ref, dzv_ref, dzg_ref):
        msv, yv = ms_ref[...], y_ref[...]
        for s in range(N_SHARD):
            cols = slice(256 * s, 256 * (s + 1))
            zp, zv, zg, gp, gs = _merge_parts(s, msv, yv, w_ref, proj_ref)
            dm = dm_ref[:, cols].astype(F32)
            sp, ss, sg = _sigmoid(gp), _sigmoid(gs), _sigmoid(zg)
            dgp_ref[:, cols] = (dm * zp * sp * (1.0 - sp)).astype(BF16)
            dgs_ref[:, cols] = (dm * zv * sg * ss * (1.0 - ss)).astype(BF16)
            dzp_ref[:, cols] = (dm * sp).astype(BF16)
            dz = dm * ss
            dzv_ref[:, cols] = (dz * sg).astype(BF16)
            dzg_ref[:, cols] = (dz * zv * sg * (1.0 - sg)).astype(BF16)

    row = BS((tm, D_MODEL), lambda i: (i, 0))
    shape = SDS((t, D_MODEL), BF16)
    return pl.pallas_call(
        body, out_shape=(shape,) * 5, grid=(t // tm,), in_specs=_merge_specs(tm) + [row],
        out_specs=(row,) * 5, name="mixer_merge_bwd",
        compiler_params=_params("parallel"))(ms, yssm, w_e, proj, dmerged)


def _mixer_dw(ms, yssm, dzp, dzv, dzg, tm):
    t = ms.shape[0]
    tm = min(2 * tm, t)
    n_t = t // tm

    def body(ms_ref, y_ref, dzp_ref, dzv_ref, dzg_ref, o_ref, acc):
        i = pl.program_id(0)

        @pl.when(i == 0)
        def _():
            acc[...] = jnp.zeros_like(acc)

        msv, yv = ms_ref[...], y_ref[...]
        for s in range(N_SHARD):
            cols = slice(256 * s, 256 * (s + 1))
            acc[s, 0:E_VAL, :] += _dot(msv, dzp_ref[:, cols], TN)
            acc[s, E_VAL:E_GATE, :] += _dot(yv, dzv_ref[:, cols], TN)
            acc[s, E_GATE:, :] += _dot(yv, dzg_ref[:, cols], TN)

        @pl.when(i == n_t - 1)
        def _():
            o_ref[...] = acc[...].astype(BF16)

    row = BS((tm, D_MODEL), lambda i: (i, 0))
    full = BS((N_SHARD, 1024, 256), lambda i: (0, 0, 0))
    return pl.pallas_call(
        body, out_shape=SDS((N_SHARD, 1024, 256), BF16), grid=(n_t,),
        in_specs=[BS((tm, D_POOL), lambda i: (i, 0)), BS((tm, D_SSM), lambda i: (i, 0)), row, row, row],
        out_specs=full, scratch_shapes=[pltpu.VMEM((N_SHARD, 1024, 256), F32)],
        name="mixer_dw", compiler_params=_params("arbitrary"))(ms, yssm, dzp, dzv, dzg)


def _mixer_dx(dzp, dzv, dzg, w_e, y_total, tm):
    t = dzp.shape[0]

    def body(dzp_ref, dzv_ref, dzg_ref, w_ref, yt_ref, dms_ref, dy_ref):
        acc_ms, acc_y = None, None
        for s in range(N_SHARD):
            cols = slice(256 * s, 256 * (s + 1))
            part_ms = _dot(dzp_ref[:, cols], w_ref[s, 0:E_VAL, :], NT)
            part_y = _dot(dzv_ref[:, cols], w_ref[s, E_VAL:E_GATE, :], NT) + _dot(dzg_ref[:, cols], w_ref[s, E_GATE:, :], NT)
            acc_ms = part_ms if s == 0 else acc_ms + part_ms
            acc_y = part_y if s == 0 else acc_y + part_y
        dms_ref[...] = acc_ms.astype(BF16)
        y = yt_ref[...]
        th = jnp.tanh(GELU_C * (y + GELU_K * y * y * y))
        dgelu = 0.5 * (1.0 + th) + 0.5 * y * (1.0 - th * th) * GELU_C * (1.0 + 3.0 * GELU_K * y * y)
        dy_ref[...] = acc_y * dgelu

    row = BS((tm, D_MODEL), lambda i: (i, 0))
    return pl.pallas_call(
        body, out_shape=(SDS((t, D_POOL), BF16), SDS((t, D_SSM), F32)), grid=(t // tm,),
        in_specs=[row, row, row, BS((N_SHARD, 1024, 256), lambda i: (0, 0, 0)), BS((tm, D_SSM), lambda i: (i, 0))],
        out_specs=(BS((tm, D_POOL), lambda i: (i, 0)), BS((tm, D_SSM), lambda i: (i, 0))),
        name="mixer_dx", compiler_params=_params("parallel"))(dzp, dzv, dzg, w_e, y_total)


def _attn_probs(q_h, k_h):
    s = _dot(q_h, k_h, NT) * (1.0 / math.sqrt(HEAD_DIM))
    e = jnp.exp(s - jnp.max(s, axis=-1, keepdims=True))
    return e / jnp.sum(e, axis=-1, keepdims=True)


def _attn_fwd(q, kv, tm):
    t = q.shape[0]
    tm = min(2 * tm, t)
    m = kv.shape[0]

    def body(q_ref, kv_ref, o_ref):
        for hd in range(N_HEADS):
            lo = hd * HEAD_DIM
            p = _attn_probs(q_ref[:, lo:lo + HEAD_DIM], kv_ref[:, lo:lo + HEAD_DIM])
            o_ref[:, lo:lo + HEAD_DIM] = _dot(p, kv_ref[:, D_MODEL + lo:D_MODEL + lo + HEAD_DIM]).astype(BF16)

    return pl.pallas_call(
        body, out_shape=SDS((t, D_MODEL), BF16), grid=(t // tm,),
        in_specs=[BS((tm, D_MODEL), lambda i: (i, 0)), BS((m, 2 * D_MODEL), lambda i: (0, 0))],
        out_specs=BS((tm, D_MODEL), lambda i: (i, 0)), name="attn_fwd", compiler_params=_params("parallel"))(q, kv)


def _attn_bwd(q, kv, d_o, tm):
    t = q.shape[0]
    m = kv.shape[0]

    def body(q_ref, kv_ref, do_ref, dq_ref, dkv_ref):
        i = pl.program_id(0)

        @pl.when(i == 0)
        def _():
            dkv_ref[...] = jnp.zeros_like(dkv_ref)

        for hd in range(N_HEADS):
            lo = hd * HEAD_DIM
            q_h = q_ref[:, lo:lo + HEAD_DIM]
            k_h = kv_ref[:, lo:lo + HEAD_DIM]
            v_h = kv_ref[:, D_MODEL + lo:D_MODEL + lo + HEAD_DIM]
            do_h = do_ref[:, lo:lo + HEAD_DIM]
            p = _attn_probs(q_h, k_h)
            dkv_ref[:, D_MODEL + lo:D_MODEL + lo + HEAD_DIM] += _dot(p, do_h, TN)
            dp = _dot(do_h, v_h, NT)
            ds = p * (dp - jnp.sum(dp * p, axis=-1, keepdims=True)) * (1.0 / math.sqrt(HEAD_DIM))
            dq_ref[:, lo:lo + HEAD_DIM] = _dot(ds, k_h).astype(BF16)
            dkv_ref[:, lo:lo + HEAD_DIM] += _dot(ds, q_h, TN)

    row = BS((tm, D_MODEL), lambda i: (i, 0))
    full = BS((m, 2 * D_MODEL), lambda i: (0, 0))
    return pl.pallas_call(
        body, out_shape=(SDS((t, D_MODEL), BF16), SDS((m, 2 * D_MODEL), F32)), grid=(t // tm,),
        in_specs=[row, full, row], out_specs=(row, full), name="attn_bwd",
        compiler_params=_params("arbitrary"))(q, kv, d_o)


TRANSPOSED = ("ffn1_w_gate", "ffn1_w_up", "ffn2_w_gate", "ffn2_w_up", "w_in")
GATHER_PHASES = {"f1a": (("ffn1_w_gate",), ("ffn1_w_up",)),
                 "f1b": (("ffn1_w_down",),),
                 "win": (("w_in",),),
                 "mix": (("w_mix_out", "w_q", "w_xo"), ("w_kv",), ("w_pool_proj", "w_glu_val", "w_glu_gate")),
                 "f2": (("ffn2_w_gate",), ("ffn2_w_up",), ("ffn2_w_down",))}
REDUCE_GROUPS = (("ffn2_w_gate",), ("ffn2_w_up",), ("ffn2_w_down",), ("w_xo",), ("w_q",), ("w_kv",), ("w_mix_out",),
                 ("w_pool_proj", "w_glu_val", "w_glu_gate"), ("w_in",), ("ffn1_w_gate",), ("ffn1_w_up",), ("ffn1_w_down",))
SMALL = ("ffn1_norm", "mix_norm", "pool_w", "pool_scale", "ssm_a_re", "ssm_a_im", "ssm_log_dt", "ssm_b_re",
         "ssm_b_im", "ssm_c_re", "ssm_c_im", "ssm_d", "xattn_norm", "mem_norm", "ffn2_norm", "final_norm")
WEIGHTS = ("ffn1_norm", "ffn1_w_gate", "ffn1_w_up", "ffn1_w_down", "mix_norm", "w_in", "pool_w", "pool_scale",
           "w_pool_proj", "ssm_a_re", "ssm_a_im", "ssm_log_dt", "ssm_b_re", "ssm_b_im", "ssm_c_re", "ssm_c_im",
           "ssm_d", "w_glu_val", "w_glu_gate", "w_mix_out", "xattn_norm", "mem_norm", "w_q", "w_kv", "w_xo",
           "ffn2_norm", "ffn2_w_gate", "ffn2_w_up", "ffn2_w_down", "final_norm")


def _small_view(a, n):
    return jnp.swapaxes(a, 3, 4) if n in ("ssm_b_re", "ssm_b_im") else a


def _device_step(x, mem, target, wts, sp, reducer=None):
    t = x.shape[0]
    tm = min(TM, t)
    g = {}

    first_gather = wts.start("win", wts.start("f1b", wts.start("f1a")))
    u1 = _rmsnorm("norm_ffn1", x, sp["ffn1_norm"], tm, after=first_gather)

    def per_channel(a):
        a = a.reshape(2 * SSM_GROUPS, 1, -1)
        return jnp.broadcast_to(a, (2 * SSM_GROUPS, SSM_GROUP, a.shape[-1])).reshape(SSM_ROWS, a.shape[-1])

    ssm_a = per_channel(sp["ssm_a_re"]), per_channel(sp["ssm_a_im"]), per_channel(sp["ssm_log_dt"])
    ssm_b = sp["ssm_b_re"].reshape(SSM_ROWS, SSM_STATE), sp["ssm_b_im"].reshape(SSM_ROWS, SSM_STATE)
    abr, abi, w_in_s, w_in_s_t, w_out_s_t, w_out_s = _ssm_prep(
        *ssm_a, *ssm_b, sp["ssm_c_re"].reshape(SSM_ROWS, SSM_STATE), sp["ssm_c_im"].reshape(SSM_ROWS, SSM_STATE),
        after=first_gather)
    first_rows = (2, SSM_GROUPS, SSM_GROUP, SSM_STATE)
    a_r = abr.reshape(first_rows)[:, :, 0].reshape(2, 1, SSM_CH)
    a_i = abi.reshape(first_rows)[:, :, 0].reshape(2, 1, SSM_CH)
    mem_n = _rmsnorm("norm_mem", mem, sp["mem_norm"], mem.shape[0], after=first_gather)

    whole = (D_FF, D_MODEL)
    w_g1, w_u1 = wts.finish("f1a", [u1, w_in_s, w_in_s_t, w_out_s, w_out_s_t, a_r, a_i, mem_n])
    w_f1 = {"gate": w_g1.reshape(whole), "up": w_u1.reshape(whole)}
    g1, up1, a1 = _ffn_up("ffn1_up", u1, w_f1, tm)
    (w_dn,) = wts.finish("f1b", [a1])
    w_f1["down"] = w_dn.reshape(whole)
    h1, u2 = _ffn_down("ffn1_down", a1, w_f1, x, tm, next_gain=sp["mix_norm"])

    (w_in_g,) = wts.finish("win", [u2])
    w_in_t = w_in_g.reshape(D_FF, D_MODEL)
    proj = _mix_in(u2, w_in_t, tm, after=wts.start("f2", wts.start("mix", [w_in_g])))
    pooled, mixed, ms = _pool_fwd(proj, sp["pool_w"][0], sp["pool_scale"])

    s_in = proj[:, D_POOL:D_POOL + D_SSM].astype(BF16)
    states, y_dirs = [], []
    for dr in range(2):
        st, yd = _ssm_scan(f"ssm_scan_fwd{dr}", s_in, w_in_s[dr], a_r[dr], a_i[dr], w_out_s[dr], reverse=(dr == 1))
        states.append(st)
        y_dirs.append(yd)
    w_sq, w_kv, w_e = wts.finish("mix", y_dirs)
    w_mo, w_q, w_xo = (w_sq[:, 256 * k:256 * (k + 1)].reshape(D_MODEL, D_MODEL) for k in range(3))
    w_d = w_kv[:, None]
    y_total, yssm = _ssm_combine(proj, y_dirs[0], y_dirs[1], sp["ssm_d"], tm)

    merged = _mixer_merge(ms, yssm, w_e, proj, tm)
    h2, u3 = _mm_resid_norm("mix_out", merged, w_mo, h1, sp["xattn_norm"], tm)

    q = _plain_mm("attn_q", u3, w_q, NN, BF16, tm)
    n_mem = mem.shape[0]
    kv = _mm("attn_kv", [(mem_n, BS((n_mem, D_MODEL), lambda s: (0, 0)), w_d, BS((None, None, D_MODEL, 512), lambda s: (s, 0, 0, 0)), NN)],
             grid=(N_SHARD,), out_shape=SDS((n_mem, 2 * D_MODEL), BF16), out_spec=BS((n_mem, 512), lambda s: (0, s)))
    o = _attn_fwd(q, kv, tm)
    h3, u4 = _mm_resid_norm("attn_out", o, w_xo, h2, sp["ffn2_norm"], tm)

    w_f2 = dict(zip(("gate", "up", "down"), (a.reshape(whole) for a in wts.finish("f2", [u4]))))
    g2, up2, a2 = _ffn_up("ffn2_up", u4, w_f2, tm)
    loss, dh4, dh4_b, g["final_norm"] = _ffn_down("ffn2_down", a2, w_f2, h3, tm,
                                                  head=(sp["final_norm"].reshape(1, D_MODEL), target))

    dg2, dup2 = _ffn_bwd_act("ffn2_bwd_act", dh4_b, w_f2, g2, up2, tm)
    dw_f2 = _ffn_dw("ffn2_dw", u4, dg2, dup2, a2, dh4_b, tm)
    dh3, dh3_b, g["ffn2_norm"] = _ffn_dx("ffn2_dx", dg2, dup2, w_f2, h3, sp["ffn2_norm"], dh4, tm)

    d_o = _plain_mm("attn_out_dx", dh3_b, w_xo, NT, BF16, tm)
    dw_xo = _dw_mm("attn_out_dw", o, dh3_b, tm)
    dq, dkv = _attn_bwd(q, kv, d_o, tm)
    dw_q = _dw_mm("attn_q_dw", u3, dq, tm)
    dh2, dh2_b, g["xattn_norm"] = _mm_norm_bwd("attn_q_dx", dq, w_q, NT, h2, sp["xattn_norm"], dh3, tm)
    dw_kv = _mm("attn_kv_dw", [(mem_n, BS((n_mem, D_MODEL), lambda s: (0, 0)), dkv, BS((n_mem, 512), lambda s: (0, s)), TN)],
                grid=(N_SHARD,), out_shape=SDS((N_SHARD, D_MODEL, 512), BF16), out_spec=BS((None, D_MODEL, 512), lambda s: (s, 0, 0)))
    dmem_n = _mm("attn_kv_dx", [(dkv, BS((n_mem, 512), lambda s: (0, s)), w_d, BS((None, None, D_MODEL, 512), lambda s: (s, 0, 0, 0)), NT)],
                 grid=(N_SHARD,), red_axis=0, out_shape=SDS((n_mem, D_MODEL), F32), out_spec=BS((n_mem, D_MODEL), lambda s: (0, 0)))
    _, _, g["mem_norm"] = _rmsnorm_bwd("norm_mem_bwd", mem, sp["mem_norm"], dmem_n, None, n_mem)

    square = (N_SHARD, D_MODEL // N_SHARD, D_MODEL)
    sharded = (N_SHARD, FF_SH, D_MODEL)
    early = [a.reshape(sharded) for a in dw_f2] + [dw_xo.reshape(square), dw_q.reshape(square), dw_kv]
    swapping = reducer.swap_start("a1", early) if reducer is not None else []
    dmerged = _plain_mm("mix_out_dx", dh2_b, w_mo, NT, BF16, tm, after=swapping)
    dw_mo = _dw_mm("mix_out_dw", merged, dh2_b, tm)
    d_gp, d_gs, dzp, dzv, dzg = _mixer_merge_bwd(ms, yssm, w_e, proj, dmerged, tm)
    dw_e = _mixer_dw(ms, yssm, dzp, dzv, dzg, tm)
    d_ms, d_yt = _mixer_dx(dzp, dzv, dzg, w_e, y_total, tm)
    dp, d_scale, d_pw = _pool_bwd(d_ms, mixed, pooled, sp["pool_w"][0], sp["pool_scale"])
    g["pool_scale"] = d_scale
    g["pool_w"] = d_pw[None]

    d_yt_b = d_yt.astype(BF16)
    du_dirs, lams = [], []
    for dr in range(2):
        lam, du = _ssm_scan(f"ssm_scan_bwd{dr}", d_yt_b, w_out_s_t[dr], a_r[dr], -a_i[dr], w_in_s_t[dr], reverse=(dr == 0))
        du_dirs.append(du)
        lams.append(lam)
    ds, g["ssm_d"] = _ssm_ds(proj, d_yt, du_dirs[0], du_dirs[1], sp["ssm_d"], tm)

    d_proj = jnp.concatenate([dp, ds, d_gp, d_gs], axis=1)
    tw = min(2 * tm, t)
    dw_in_t = _mm("mix_in_dw", [(d_proj, BS((tw, D_FF // 2), lambda j, i: (i, j)), u2, BS((tw, D_MODEL), lambda j, i: (i, 0)), TN)],
                  grid=(2, t // tw), red_axis=1, out_shape=SDS((D_FF, D_MODEL), BF16), out_spec=BS((D_FF // 2, D_MODEL), lambda j, i: (j, 0)))
    dh1, dh1_b, g["mix_norm"] = _mm_norm_bwd("mix_in_dx", d_proj, w_in_t, NN, h1, sp["mix_norm"], dh2, tm)

    early += [dw_mo.reshape(square), dw_e, dw_in_t.reshape(sharded)]
    g["final_norm"] = g["final_norm"].reshape(D_MODEL)

    travelling = reducer.start("a", early[6:], swapped=["a1"], after=list(g.values())) if reducer is not None else []
    d_abr, d_abi, d_cm, d_bm = [], [], [], []
    for dr in range(2):
        da_r, da_i, d_win, d_woutt = _ssm_param_grads(f"ssm_param_grads{dr}", lams[dr], states[dr], s_in, d_yt_b,
                                                      reverse=(dr == 1), after=travelling)
        d_abr.append(da_r)
        d_abi.append(da_i)
        d_bm.append(d_win)
        d_cm.append(d_woutt)

    d_ar, d_ai, d_ldt, d_br, d_bi, d_cr, d_ci = _ssm_prep_bwd(*ssm_a, *ssm_b, d_abr, d_abi, d_bm, d_cm)
    per_group = (2 * SSM_GROUPS, SSM_GROUP * SSM_STATE)
    g["ssm_a_re"] = d_ar.reshape(2 * SSM_GROUPS, SSM_GROUP, SSM_STATE).sum(axis=1).reshape(sp["ssm_a_re"].shape)
    g["ssm_a_im"] = d_ai.reshape(2 * SSM_GROUPS, SSM_GROUP, SSM_STATE).sum(axis=1).reshape(sp["ssm_a_im"].shape)
    g["ssm_log_dt"] = d_ldt.reshape(per_group).sum(axis=1).reshape(sp["ssm_log_dt"].shape)
    g["ssm_b_re"] = d_br.reshape(sp["ssm_b_re"].shape)
    g["ssm_b_im"] = d_bi.reshape(sp["ssm_b_im"].shape)
    g["ssm_c_re"] = d_cr.reshape(sp["ssm_c_re"].shape)
    g["ssm_c_im"] = d_ci.reshape(sp["ssm_c_im"].shape)
    if reducer is not None:
        travelling = travelling + [d_ar, d_br, d_cr]
    dg1, dup1 = _ffn_bwd_act("ffn1_bwd_act", dh1_b, w_f1, g1, up1, tm, after=travelling)
    dw_f1 = [a.reshape(sharded) for a in _ffn_dw("ffn1_dw", u1, dg1, dup1, a1, dh1_b, tm)]
    if reducer is not None:
        travelling = reducer.start("b", dw_f1, after=reducer.finish("a", dw_f1[:1]))
        travelling = travelling + reducer.join_start("a", after=travelling)
    grad_x, _, g["ffn1_norm"] = _ffn_dx("ffn1_dx", dg1, dup1, w_f1, x, sp["ffn1_norm"], dh1, tm, after=travelling)
    if reducer is not None:
        reducer.join_finish("a", [grad_x])
    return loss, grad_x, early + dw_f1, g


def _mesh_place():
    x, y, c = lax.axis_index("x"), lax.axis_index("y"), lax.axis_index("c")
    chips = [(1 - x, y), (x, 1 - y), (1 - x, 1 - y)]
    return x, y, c, chips


def _remote(src, dst, send_sems, recv_sems, k, to):
    return pltpu.make_async_remote_copy(src_ref=src, dst_ref=dst, send_sem=send_sems.at[k], recv_sem=recv_sems.at[k],
                                        device_id=to, device_id_type=MESH)


def _sibling_swap_halves(tag, grads, after=()):
    n = len(grads)
    after_ops, after_specs = _after_operands(after)

    def body(*refs):
        ins, outs = refs[:n], refs[n + len(after_ops):2 * n + len(after_ops)]
        send_sems, recv_sems = refs[2 * n + len(after_ops):]
        x, y, c, _ = _mesh_place()
        sibling = (x, y, 1 - c)
        copies = []
        for k in range(n):
            half = grads[k].shape[1] // 2
            theirs = pl.ds(pl.multiple_of((1 - c) * half, 16), half)
            cp = _remote(ins[k].at[:, theirs, :], outs[k], send_sems, recv_sems, k, sibling)
            cp.start()
            copies.append(cp)
        for cp in copies:
            cp.wait_recv()
        for cp in copies:
            cp.wait_send()

    hbm = BS(memory_space=pl.ANY)
    return pl.pallas_call(
        body, out_shape=tuple(SDS((g.shape[0], g.shape[1] // 2, g.shape[2]), g.dtype) for g in grads),
        in_specs=[hbm] * n + after_specs, out_specs=(hbm,) * n,
        scratch_shapes=[pltpu.SemaphoreType.DMA((n,)), pltpu.SemaphoreType.DMA((n,))],
        name="reduce_sibling_send_" + tag, compiler_params=_params())(*grads, *after_ops)


def _row_tile(rows, cap=512):
    return max(r for r in range(16, cap + 1, 16) if rows % r == 0)


REDUCE_STEPS = 2


def _chip_presum(tag, grads, gots, c_idx):
    n = len(grads)
    halves = [g.shape[1] // 2 for g in grads]
    tiles = [(h // REDUCE_STEPS, g.shape[2]) for h, g in zip(halves, grads)]

    def body(c_ref, *refs):
        for k in range(n):
            refs[2 * n + k][...] = (refs[k][...].astype(F32) + refs[n + k][...].astype(F32)).astype(BF16)

    mine = [BS((None, None) + tile, lambda s, i, c_ref: (s, c_ref[0], i, 0)) for tile in tiles]
    plain = [BS((None,) + tile, lambda s, i, c_ref: (s, i, 0)) for tile in tiles]
    return list(pl.pallas_call(
        body, out_shape=tuple(SDS((g.shape[0], h, g.shape[2]), BF16) for g, h in zip(grads, halves)),
        grid_spec=pltpu.PrefetchScalarGridSpec(num_scalar_prefetch=1, grid=(N_SHARD, REDUCE_STEPS),
                                               in_specs=mine + plain, out_specs=plain),
        name="reduce_presum_" + tag, compiler_params=_params("parallel", "parallel"))(
            c_idx, *[g.reshape(g.shape[0], 2, h, g.shape[2]) for g, h in zip(grads, halves)], *gots))


HBM_SPEC = BS(memory_space=pltpu.HBM)
SEM_SPEC = BS(memory_space=pltpu.SEMAPHORE)
DATAFLOW = pltpu.SideEffectType.DATAFLOW_SIDE_EFFECTING


def _chip_exchange_copies(parts, lands, send_sems, recv_sems):
    _, _, c, chips = _mesh_place()
    return [_remote(parts[k].at[2 * px + py], lands[k].at[j], send_sems, recv_sems, 3 * k + j, (px, py, c))
            for k in range(len(parts)) for j, (px, py) in enumerate(chips)]


def _gather_copies(shards, lands, send_sems, recv_sems):
    x, y, c, chips = _mesh_place()
    return [_remote(shards[k], lands[k].at[2 * x + y], send_sems, recv_sems, 3 * k + j, (px, py, c))
            for k in range(len(shards)) for j, (px, py) in enumerate(chips)]


def _gather_half_copies(shards, lands, send_sems, recv_sems):
    x, y, c, chips = _mesh_place()
    out = []
    for k in range(len(shards)):
        half = shards[k].shape[0] // 2
        mine = pl.ds(pl.multiple_of(c * half, 16), half)
        for j, (px, py) in enumerate(chips):
            out.append(_remote(shards[k].at[mine, :], lands[k].at[2 * x + y, mine, :], send_sems, recv_sems,
                               3 * k + j, (px, py, c)))
    return out


def _sibling_fill(tag, lands):
    n = len(lands)

    def body(*refs):
        outs = refs[n:2 * n]
        send_sems, recv_sems = refs[2 * n:]
        x, y, c, chips = _mesh_place()
        copies = []
        for k in range(n):
            half = lands[k].shape[1] // 2
            mine = pl.ds(pl.multiple_of(c * half, 16), half)
            for j, (px, py) in enumerate(chips):
                blk = outs[k].at[2 * px + py, mine, :]
                copies.append(_remote(blk, blk, send_sems, recv_sems, 3 * k + j, (x, y, 1 - c)))
        for cp in copies:
            cp.start()
        for cp in copies:
            cp.wait_recv()
        for cp in copies:
            cp.wait_send()

    hbm = BS(memory_space=pl.ANY)
    return list(pl.pallas_call(
        body, out_shape=tuple(SDS(a.shape, a.dtype) for a in lands),
        in_specs=[hbm] * n, out_specs=(hbm,) * n, input_output_aliases={k: k for k in range(n)},
        scratch_shapes=[pltpu.SemaphoreType.DMA((3 * n,)), pltpu.SemaphoreType.DMA((3 * n,))],
        name="gather_fill_" + tag, compiler_params=_params())(*lands))


def _swap_copies(grads, lands, send_sems, recv_sems):
    x, y, c, _ = _mesh_place()
    out = []
    for k in range(len(grads)):
        half = grads[k].shape[1] // 2
        theirs = pl.ds(pl.multiple_of((1 - c) * half, 16), half)
        out.append(_remote(grads[k].at[:, theirs, :], lands[k], send_sems, recv_sems, k, (x, y, 1 - c)))
    return out


def _join_copies(fulls, same, send_sems, recv_sems):
    x, y, c, _ = _mesh_place()
    out = []
    for k in range(len(fulls)):
        half = fulls[k].shape[0] // 2
        mine = fulls[k].at[pl.ds(pl.multiple_of(c * half, 8), half), :]
        out.append(_remote(mine, mine, send_sems, recv_sems, k, (x, y, 1 - c)))
    return out


def _everyone_copies(packs, lands, send_sems, recv_sems):
    x, y, c, _ = _mesh_place()
    out = []
    for k in range(len(packs)):
        for j in range(N_DEV - 1):
            bx, by, bc = (j + 1) >> 2 & 1, (j + 1) >> 1 & 1, (j + 1) & 1
            peer = (x ^ bx, y ^ by, c ^ bc)
            out.append(_remote(packs[k], lands[k].at[4 * x + 2 * y + c], send_sems, recv_sems, (N_DEV - 1) * k + j, peer))
    return out


def _split_start(name, copies, sources, land_shapes, after=(), fanout=3):
    n = len(sources)
    n_land = len(land_shapes)
    m = n + n_land
    n_sems = fanout * n
    after_ops, after_specs = _after_operands(after)

    def body(*refs):
        ins = refs[:n]
        lands = refs[n:m] if n_land else ins
        send_sems, recv_sems = refs[m + len(after_ops)], refs[m + len(after_ops) + 1]
        token = refs[-1]
        for cp in copies(ins, lands, send_sems, recv_sems):
            cp.start()
        token[...] = jnp.zeros_like(token)

    lands = [pltpu.with_memory_space_constraint(lax.empty(s, d), pltpu.HBM) for s, d in land_shapes]
    sources = [pltpu.with_memory_space_constraint(p, pltpu.HBM) for p in sources]
    thru = [pltpu.HBM(a.shape, a.dtype) for a in sources + lands]
    out = pl.pallas_call(
        body, name=name,
        out_shape=(pltpu.SemaphoreType.DMA((n_sems,)), pltpu.SemaphoreType.DMA((n_sems,)), *thru, SDS((8, 128), F32)),
        in_specs=[HBM_SPEC] * m + after_specs,
        out_specs=(SEM_SPEC, SEM_SPEC, *[HBM_SPEC] * m, BS(memory_space=pltpu.VMEM)),
        input_output_aliases={i: 2 + i for i in range(m)},
        compiler_params=pltpu.CompilerParams(has_side_effects=DATAFLOW))(*sources, *lands, *after_ops)
    return out[0], out[1], list(out[2:2 + n]), list(out[2 + n:2 + m]), out[-1]


def _split_wait(name, copies, send_sems, recv_sems, sources, lands, after):
    n = len(sources)
    m = n + len(lands)
    after_ops, after_specs = _after_operands(after)

    def body(*refs):
        ins = refs[:n]
        zones = refs[n:m] if m > n else ins
        for cp in copies(ins, zones, refs[m], refs[m + 1]):
            cp.wait_send()
            cp.wait_recv()

    out = pl.pallas_call(
        body, name=name,
        out_shape=tuple(pltpu.HBM(a.shape, a.dtype) for a in sources + lands),
        in_specs=[HBM_SPEC] * m + [SEM_SPEC, SEM_SPEC] + after_specs, out_specs=(HBM_SPEC,) * m,
        input_output_aliases={i: i for i in range(m)},
        compiler_params=pltpu.CompilerParams(has_side_effects=DATAFLOW))(*sources, *lands, send_sems, recv_sems, *after_ops)
    return list(out[:n]), list(out[n:])


class _WeightGatherer:
    def __init__(self, shards):
        self.shards, self.open = shards, {}
        self.me = 2 * lax.axis_index("x") + lax.axis_index("y")

    HALVED = ("f1a", "win", "mix")

    def start(self, tag, after=()):
        shapes = [((N_SHARD,) + s.shape, s.dtype) for s in self.shards[tag]]
        copies = _gather_half_copies if tag in self.HALVED else _gather_copies
        self.open[tag] = _split_start("gather_start_" + tag, copies, self.shards[tag], shapes, after)
        return [self.open[tag][-1]]

    def finish(self, tag, after):
        send_sems, recv_sems, shards, lands, _ = self.open.pop(tag)
        copies = _gather_half_copies if tag in self.HALVED else _gather_copies
        shards, lands = _split_wait("gather_wait_" + tag, copies, send_sems, recv_sems, shards, lands, after)
        if tag in self.HALVED:
            lands = _sibling_fill(tag, lands)
        return [lax.dynamic_update_slice(zone, s[None], (self.me, 0, 0)) for zone, s in zip(lands, shards)]


class _GradReducer:
    def __init__(self):
        self.c_idx = lax.axis_index("c").astype(jnp.int32).reshape(1)
        self.place = jnp.stack([2 * lax.axis_index("x") + lax.axis_index("y"), lax.axis_index("c")]).astype(jnp.int32)
        self.swaps, self.open, self.landed, self.joins, self.reduced = {}, {}, {}, {}, []

    def swap_start(self, tag, grads, after=()):
        shapes = [((g.shape[0], g.shape[1] // 2, g.shape[2]), g.dtype) for g in grads]
        self.swaps[tag] = _split_start("reduce_swap_start_" + tag, _swap_copies, grads, shapes, after, fanout=1)
        return [self.swaps[tag][-1]]

    def start(self, tag, grads, after=(), swapped=()):
        pairs = []
        for s in swapped:
            send_sems, recv_sems, early, lands, _ = self.swaps.pop(s)
            pairs += zip(*_split_wait("reduce_swap_wait_" + s, _swap_copies, send_sems, recv_sems, early, lands, grads[-1:]))
        pairs += zip(grads, _sibling_swap_halves(tag, grads, after))
        parts = _chip_presum(tag, [g for g, _ in pairs], [s for _, s in pairs], self.c_idx)
        shapes = [((3,) + p.shape[1:], p.dtype) for p in parts]
        self.open[tag] = _split_start("reduce_exchange_start_" + tag, _chip_exchange_copies, parts, shapes)
        return [self.open[tag][-1]]

    def finish(self, tag, after):
        send_sems, recv_sems, parts, lands, _ = self.open.pop(tag)
        self.landed[tag] = _split_wait("reduce_exchange_wait_" + tag, _chip_exchange_copies, send_sems, recv_sems, parts, lands, after)
        return self.landed[tag][1][:1]

    def _sums(self, tag, after=()):
        parts, landed = self.landed.pop(tag)
        return _chip_sum(tag, parts, landed, self.place, after)

    def join_start(self, tag, after=()):
        self.joins[tag] = _split_start("reduce_join_start_" + tag, _join_copies, self._sums(tag, after), [], fanout=1)
        return [self.joins[tag][-1]]

    def join_finish(self, tag, after):
        send_sems, recv_sems, fulls, _, _ = self.joins.pop(tag)
        self.reduced += _split_wait("reduce_join_wait_" + tag, _join_copies, send_sems, recv_sems, fulls, [], after)[0]

    def join(self, tag, after=()):
        self.reduced += _sibling_join_halves(self._sums(tag), after)


def _chip_sum(tag, parts, gots, place, after=()):
    n = len(parts)
    tiles = [(p.shape[1] // REDUCE_STEPS, p.shape[2]) for p in parts]
    after_ops, after_specs = _after_operands(after)

    def body(place_ref, *refs):
        outs = refs[2 * n + len(after_ops):]
        for k in range(n):
            acc = refs[k][...].astype(F32)
            for j in range(3):
                acc = acc + refs[n + k][j].astype(F32)
            outs[k][...] = acc

    return list(pl.pallas_call(
        body, out_shape=tuple(SDS((2 * p.shape[1], p.shape[2]), F32) for p in parts),
        grid_spec=pltpu.PrefetchScalarGridSpec(
            num_scalar_prefetch=1, grid=(REDUCE_STEPS,),
            in_specs=[BS((None,) + tile, lambda i, place_ref: (place_ref[0], i, 0)) for tile in tiles]
            + [BS((3,) + tile, lambda i, place_ref: (0, i, 0)) for tile in tiles] + after_specs,
            out_specs=[BS(tile, lambda i, place_ref: (place_ref[1] * REDUCE_STEPS + i, 0)) for tile in tiles]),
        name="reduce_sum_" + tag, compiler_params=_params("parallel"))(place, *parts, *gots, *after_ops))


def _sibling_join_halves(fulls, after=()):
    n = len(fulls)
    after_ops, after_specs = _after_operands(after)

    def body(*refs):
        outs = refs[n + len(after_ops):2 * n + len(after_ops)]
        send_sems, recv_sems = refs[2 * n + len(after_ops):]
        copies = _join_copies(outs, outs, send_sems, recv_sems)
        for cp in copies:
            cp.start()
        for cp in copies:
            cp.wait_recv()
        for cp in copies:
            cp.wait_send()

    hbm = BS(memory_space=pl.ANY)
    return list(pl.pallas_call(
        body, out_shape=tuple(SDS(f.shape, f.dtype) for f in fulls),
        in_specs=[hbm] * n + after_specs, out_specs=(hbm,) * n, input_output_aliases={k: k for k in range(n)},
        scratch_shapes=[pltpu.SemaphoreType.DMA((n,)), pltpu.SemaphoreType.DMA((n,))],
        name="reduce_sibling_join", compiler_params=_params())(*fulls, *after_ops))


N_DEV = 8


def _sum_devices(packs):
    _, rows, lanes = packs.shape

    def body(p_ref, o_ref):
        acc = p_ref[0]
        for dev in range(1, N_DEV):
            acc = acc + p_ref[dev]
        o_ref[...] = acc

    vm = BS(memory_space=pltpu.VMEM)
    return pl.pallas_call(body, out_shape=SDS((rows, lanes), F32), in_specs=[vm], out_specs=vm,
                          name="small_sum", compiler_params=_params())(packs)


def _adamw_refs(w_ref, g_ref, m_ref, v_ref, go_ref, d_ref, mo_ref, vo_ref):
    bc1 = 1.0 - ADAM_B1 ** ADAM_STEP
    bc2 = 1.0 - ADAM_B2 ** ADAM_STEP
    g = g_ref[...]
    m_new = ADAM_B1 * m_ref[...] + (1.0 - ADAM_B1) * g
    v_new = ADAM_B2 * v_ref[...] + (1.0 - ADAM_B2) * (g * g)
    go_ref[...] = g
    mo_ref[...] = m_new
    vo_ref[...] = v_new
    d_ref[...] = -ADAM_LR * ((m_new / bc1) / (jnp.sqrt(v_new / bc2) + ADAM_EPS) + ADAM_WD * w_ref[...])


def _adamw_small(ws, gs, ms, vs):
    n = len(ws)

    def body(*refs):
        for k in range(n):
            _adamw_refs(*[refs[j * n + k] for j in range(4)], *refs[4 * n + 4 * k:4 * n + 4 * k + 4])

    vm = BS(memory_space=pltpu.VMEM)
    outs = pl.pallas_call(
        body, out_shape=tuple(SDS(a.shape, F32) for a in ws for _ in range(4)), in_specs=[vm] * (4 * n),
        out_specs=(vm,) * (4 * n), name="adamw_small", compiler_params=_params())(*ws, *gs, *ms, *vs)
    return [outs[4 * k:4 * k + 4] for k in range(n)]


def _adamw(name, w, grad, row0, m, v, after=()):
    rows, cols = w.shape
    tr = rows if rows < 16 else _row_tile(rows, 352)
    after_ops, after_specs = _after_operands(after)

    def body(w_ref, g_ref, m_ref, v_ref, *rest):
        _adamw_refs(w_ref, g_ref, m_ref, v_ref, *rest[len(after_ops):])

    blk = BS((tr, cols), lambda i: (i, 0))
    shape = SDS((rows, cols), F32)
    return pl.pallas_call(
        body, out_shape=(shape,) * 4, grid=(rows // tr,),
        in_specs=[blk, BS((tr, cols), lambda i: (row0 // tr + i, 0)), blk, blk] + after_specs, out_specs=(blk,) * 4,
        name=name, compiler_params=_params("parallel"))(w, grad, m, v, *after_ops)


SMALL_LANES = 128


def _pack_small(parts):
    flat = jnp.concatenate([jnp.ravel(p) for p in parts])
    rows = -(-flat.shape[0] // (64 * SMALL_LANES)) * 64
    return jnp.pad(flat, (0, rows * SMALL_LANES - flat.shape[0])).reshape(rows, SMALL_LANES)


def _unpack_small(packed, like):
    flat = jnp.ravel(packed)
    out, at = [], 0
    for p in like:
        out.append(flat[at:at + p.size].reshape(p.shape))
        at += p.size
    return out


def kernel(x, mem, ffn1_norm, ffn1_w_gate, ffn1_w_up, ffn1_w_down, mix_norm, w_in, pool_w, pool_scale, w_pool_proj, ssm_a_re, ssm_a_im, ssm_log_dt, ssm_b_re, ssm_b_im, ssm_c_re, ssm_c_im, ssm_d, w_glu_val, w_glu_gate, w_mix_out, xattn_norm, mem_norm, w_q, w_kv, w_xo, ffn2_norm, ffn2_w_gate, ffn2_w_up, ffn2_w_down, final_norm, loss_target, m_ffn1_norm, m_ffn1_w_gate, m_ffn1_w_up, m_ffn1_w_down, m_mix_norm, m_w_in, m_pool_w, m_pool_scale, m_w_pool_proj, m_ssm_a_re, m_ssm_a_im, m_ssm_log_dt, m_ssm_b_re, m_ssm_b_im, m_ssm_c_re, m_ssm_c_im, m_ssm_d, m_w_glu_val, m_w_glu_gate, m_w_mix_out, m_xattn_norm, m_mem_norm, m_w_q, m_w_kv, m_w_xo, m_ffn2_norm, m_ffn2_w_gate, m_ffn2_w_up, m_ffn2_w_down, m_final_norm, v_ffn1_norm, v_ffn1_w_gate, v_ffn1_w_up, v_ffn1_w_down, v_mix_norm, v_w_in, v_pool_w, v_pool_scale, v_w_pool_proj, v_ssm_a_re, v_ssm_a_im, v_ssm_log_dt, v_ssm_b_re, v_ssm_b_im, v_ssm_c_re, v_ssm_c_im, v_ssm_d, v_w_glu_val, v_w_glu_gate, v_w_mix_out, v_xattn_norm, v_mem_norm, v_w_q, v_w_kv, v_w_xo, v_ffn2_norm, v_ffn2_w_gate, v_ffn2_w_up, v_ffn2_w_down, v_final_norm):
    given = dict(locals())
    w = {n: given[n] for n in WEIGHTS}
    m = {n: given["m_" + n] for n in WEIGHTS}
    v = {n: given["v_" + n] for n in WEIGHTS}

    def shard_view(a, n):
        return a[0].T if n in TRANSPOSED else a[0]

    def shard_unview(a, n):
        return (a.T if n in TRANSPOSED else a)[None]

    shards = {tag: [jnp.concatenate([shard_view(w[n], n).astype(BF16) for n in grp], axis=0) for grp in arrays]
              for tag, arrays in GATHER_PHASES.items()}
    reducer = _GradReducer()
    ws, ms, vs = ({n: _small_view(a[n], n) for n in SMALL} for a in (w, m, v))
    loss_part, grad_x, _, small = _device_step(x[0], mem[0], loss_target[0], _WeightGatherer(shards), ws, reducer)

    small_like = [ws[n] for n in SMALL] + [loss_part[0, :1]]
    pack = _pack_small([small[n] for n in SMALL] + [loss_part[0, :1]])
    everyone = _split_start("small_start", _everyone_copies, [pack], [((N_DEV,) + pack.shape, F32)], fanout=N_DEV - 1)

    grads, delta, new_m, new_v = {}, {}, {}, {}
    big_done = []

    def update(groups, reduced):
        for grp, red in zip(groups, reduced):
            row0 = 0
            for n in grp:
                w_n = shard_view(w[n], n)
                outs = _adamw("adamw_" + n, w_n, red, row0, shard_view(m[n], n), shard_view(v[n], n), after=everyone[-1:])
                grads[n], delta[n], new_m[n], new_v[n] = (shard_unview(o, n) for o in outs)
                big_done.append(outs[1])
                row0 += w_n.shape[0]

    n_a = len(reducer.reduced)
    update(REDUCE_GROUPS[:n_a], reducer.reduced)
    reducer.finish("b", list(big_done))
    reducer.join("b")
    update(REDUCE_GROUPS[n_a:], reducer.reduced[n_a:])

    send_sems, recv_sems, packs, landed, _ = everyone
    packs, landed = _split_wait("small_wait", _everyone_copies, send_sems, recv_sems, packs, landed, big_done)
    mine = 4 * lax.axis_index("x") + 2 * lax.axis_index("y") + lax.axis_index("c")
    summed = _sum_devices(lax.dynamic_update_slice(landed[0], packs[0][None], (mine, 0, 0)))
    g_small = dict(zip(SMALL + ("loss",), _unpack_small(summed, small_like)))
    loss = g_small.pop("loss").reshape(())
    def two_d(a):
        return a.reshape(-1, a.shape[-1])

    updated = _adamw_small(*([two_d(a[n]) for n in SMALL] for a in (ws, g_small, ms, vs)))
    for n, outs in zip(SMALL, updated):
        grads[n], delta[n], new_m[n], new_v[n] = (_small_view(o.reshape(ws[n].shape), n) for o in outs)

    return (loss, grad_x[None], *[grads[n] for n in WEIGHTS], *[delta[n] for n in WEIGHTS],
            *[new_m[n] for n in WEIGHTS], *[new_v[n] for n in WEIGHTS])
```

```python
import functools
import math

import jax
import jax.numpy as jnp
from jax import lax
from jax.experimental import pallas as pl
from jax.experimental.pallas import tpu as pltpu

F32 = jnp.float32
BF16 = jnp.bfloat16
SDS = jax.ShapeDtypeStruct
BS = pl.BlockSpec
MESH = pl.DeviceIdType.MESH

D_MODEL = 1024
D_FF = 2816
N_SHARD = 4
FF_SH = D_FF // N_SHARD
D_POOL = 512
POOL_WINDOWS = (2, 4, 8, 16)
POOL_GROUP = 128
D_SSM = 256
SSM_GROUPS = 16
SSM_GROUP = 16
SSM_STATE = 64
SSM_CH = SSM_GROUPS * SSM_STATE
N_HEADS = 4
HEAD_DIM = 256
EPS = 1e-6
ADAM_LR, ADAM_B1, ADAM_B2, ADAM_EPS, ADAM_WD, ADAM_STEP = 0.001, 0.9, 0.999, 1e-08, 0.01, 10

VMEM_LIMIT_V7X = 58 * 1024 * 1024
TM = 512

NN = (((1,), (0,)), ((), ()))
NT = (((1,), (1,)), ((), ()))
TN = (((0,), (0,)), ((), ()))


def _params(*sem):
    return pltpu.CompilerParams(dimension_semantics=sem if sem else None, vmem_limit_bytes=VMEM_LIMIT_V7X)


def _dot(a, b, dims=NN):
    return lax.dot_general(a.astype(BF16), b.astype(BF16), dims, preferred_element_type=F32)


def _sigmoid(v):
    return pl.reciprocal(1.0 + jnp.exp(-v), approx=True)


def _block_dims(spec):
    return tuple(d for d in spec.block_shape if d is not None)


def _after_operands(after):
    return list(after), [BS(memory_space=pl.ANY)] * len(after)


def _mm(name, pairs, *, grid, out_shape, out_spec, red_axis=None, extras=(), epilogue=None, after=()):
    n_pairs, n_extra = len(pairs), len(extras)
    n_red = grid[red_axis] if red_axis is not None else 1
    dims = [p[4] for p in pairs]

    def body(*refs):
        ab = refs[:2 * n_pairs]
        ex = refs[2 * n_pairs:2 * n_pairs + n_extra]
        o_ref = refs[2 * n_pairs + n_extra + len(after)]

        def partial():
            acc = None
            for p in range(n_pairs):
                t = _dot(ab[2 * p][...], ab[2 * p + 1][...], dims[p])
                acc = t if acc is None else acc + t
            return acc

        def finish(acc):
            res = epilogue(acc, *[e[...] for e in ex]) if epilogue is not None else acc
            o_ref[...] = res.astype(o_ref.dtype)

        if n_red == 1:
            finish(partial())
        else:
            acc_ref = refs[-1]
            k = pl.program_id(red_axis)

            @pl.when(k == 0)
            def _():
                acc_ref[...] = jnp.zeros_like(acc_ref)

            acc_ref[...] += partial()

            @pl.when(k == n_red - 1)
            def _():
                finish(acc_ref[...])

    operands, in_specs = [], []
    for a, a_spec, b, b_spec, _ in pairs:
        operands += [a, b]
        in_specs += [a_spec, b_spec]
    for e, e_spec in extras:
        operands.append(e)
        in_specs.append(e_spec)
    after_ops, after_specs = _after_operands(after)
    operands += after_ops
    in_specs += after_specs
    scratch = [pltpu.VMEM(_block_dims(out_spec), F32)] if n_red > 1 else []
    sem = tuple("arbitrary" if ax == red_axis else "parallel" for ax in range(len(grid)))
    return pl.pallas_call(body, out_shape=out_shape, grid=grid, in_specs=in_specs, out_specs=out_spec,
                          scratch_shapes=scratch, name=name, compiler_params=_params(*sem))(*operands)


def _rmsnorm(name, h, gain, tm, after=()):
    t, d = h.shape
    after_ops, after_specs = _after_operands(after)

    def body(h_ref, g_ref, *rest):
        u_ref = rest[-1]
        hv = h_ref[...]
        r = lax.rsqrt(jnp.mean(hv * hv, axis=-1, keepdims=True) + EPS)
        u_ref[...] = ((hv * r) * g_ref[...]).astype(u_ref.dtype)

    return pl.pallas_call(
        body, out_shape=SDS((t, d), BF16), grid=(t // tm,),
        in_specs=[BS((tm, d), lambda i: (i, 0)), BS((1, d), lambda i: (0, 0))] + after_specs,
        out_specs=BS((tm, d), lambda i: (i, 0)), name=name, compiler_params=_params("parallel"))(h, gain, *after_ops)


def _rmsnorm_bwd(name, h, gain, du, dh_in, tm):
    t, d = h.shape
    has_in = dh_in is not None

    def body(*refs):
        if has_in:
            h_ref, g_ref, du_ref, dhin_ref, dh_ref, dhb_ref, dg_ref = refs
        else:
            h_ref, g_ref, du_ref, dh_ref, dhb_ref, dg_ref = refs
        i = pl.program_id(0)
        hv = h_ref[...]
        r = lax.rsqrt(jnp.mean(hv * hv, axis=-1, keepdims=True) + EPS)
        n = hv * r
        duv = du_ref[...].astype(F32)
        dn = duv * g_ref[...]
        dh = r * (dn - n * jnp.mean(dn * n, axis=-1, keepdims=True))
        if has_in:
            dh = dhin_ref[...] + dh
        dh_ref[...] = dh
        dhb_ref[...] = dh.astype(BF16)

        @pl.when(i == 0)
        def _():
            dg_ref[...] = jnp.zeros_like(dg_ref)

        dg_ref[...] += jnp.sum(duv * n, axis=0, keepdims=True)

    row = BS((tm, d), lambda i: (i, 0))
    vec = BS((1, d), lambda i: (0, 0))
    operands = [h, gain, du] + ([dh_in] if has_in else [])
    in_specs = [row, vec, row] + ([row] if has_in else [])
    return pl.pallas_call(
        body, out_shape=(SDS((t, d), F32), SDS((t, d), BF16), SDS((1, d), F32)), grid=(t // tm,),
        in_specs=in_specs, out_specs=(row, row, vec), name=name, compiler_params=_params("arbitrary"))(*operands)


def _loss_head_tile(i, hv, g_ref, t_ref, loss_ref, dh_ref, dhb_ref, dg_ref):
    g = g_ref[...]
    r = lax.rsqrt(jnp.mean(hv * hv, axis=-1, keepdims=True) + EPS)
    n = hv * r
    err = n * g - t_ref[...]
    dy = err * (1.0 / hv.shape[-1])
    dn = dy * g
    dh = r * (dn - n * jnp.mean(dn * n, axis=-1, keepdims=True))
    dh_ref[...] = dh
    dhb_ref[...] = dh.astype(BF16)

    @pl.when(i == 0)
    def _():
        dg_ref[...] = jnp.zeros_like(dg_ref)
        loss_ref[...] = jnp.zeros_like(loss_ref)

    dg_ref[...] += jnp.sum(dy * n, axis=0, keepdims=True)
    part = 0.5 * jnp.sum(jnp.mean(err * err, axis=-1, keepdims=True), axis=0, keepdims=True)
    loss_ref[...] += jnp.broadcast_to(part, loss_ref.shape)


def _norm_tile(h, g_ref, u_ref):
    r = lax.rsqrt(jnp.mean(h * h, axis=-1, keepdims=True) + EPS)
    u_ref[...] = ((h * r) * g_ref[...]).astype(u_ref.dtype)


FFN_BLOCK = D_FF // 2


def _ffn_up(name, u, w_f, tm, after=()):
    t, d = u.shape
    after_ops, after_specs = _after_operands(after)

    def body(u_ref, wg_ref, wu_ref, *rest):
        pg_ref, pu_ref, a_ref = rest[len(after_ops):]
        uv = u_ref[...]
        for lo in range(0, D_FF, FFN_BLOCK):
            cols = slice(lo, lo + FFN_BLOCK)
            g = _dot(uv, wg_ref[cols, :], NT)
            up = _dot(uv, wu_ref[cols, :], NT)
            sg = _sigmoid(g)
            silu = g * sg
            a_ref[:, cols] = (silu * up).astype(BF16)
            pu_ref[:, cols] = (0.5 * silu).astype(BF16)
            pg_ref[:, cols] = (0.5 * sg * (1.0 + g * (1.0 - sg)) * up).astype(BF16)

    hid = BS((tm, D_FF), lambda i: (i, 0))
    shape = SDS((t, D_FF), BF16)
    whole = BS((D_FF, d), lambda i: (0, 0))
    return pl.pallas_call(
        body, out_shape=(shape, shape, shape), grid=(t // tm,),
        in_specs=[BS((tm, d), lambda i: (i, 0)), whole, whole] + after_specs,
        out_specs=(hid, hid, hid), name=name,
        compiler_params=_params("parallel"))(u, w_f["gate"], w_f["up"], *after_ops)


def _ffn_down(name, a, w_f, resid, tm, next_gain=None, head=None):
    t, d = resid.shape
    row = BS((tm, d), lambda i: (i, 0))
    vec = BS((1, d), lambda i: (0, 0))

    def body(a_ref, w_ref, res_ref, *rest):
        h = res_ref[...] + 0.5 * _dot(a_ref[...], w_ref[...])
        if head is not None:
            _loss_head_tile(pl.program_id(0), h, *rest)
        else:
            g_ref, h_ref, u_ref = rest
            h_ref[...] = h
            _norm_tile(h, g_ref, u_ref)

    if head is not None:
        extra, extra_specs = list(head), [vec, row]
        out_shape = (SDS((1, 128), F32), SDS((t, d), F32), SDS((t, d), BF16), SDS((1, d), F32))
        out_specs = (BS((1, 128), lambda i: (0, 0)), row, row, vec)
    else:
        extra, extra_specs = [next_gain], [vec]
        out_shape = (SDS((t, d), F32), SDS((t, d), BF16))
        out_specs = (row, row)
    return pl.pallas_call(
        body, out_shape=out_shape, grid=(t // tm,),
        in_specs=[BS((tm, D_FF), lambda i: (i, 0)), BS((D_FF, d), lambda i: (0, 0)), row] + extra_specs,
        out_specs=out_specs, name=name,
        compiler_params=_params("arbitrary" if head is not None else "parallel"))(a, w_f["down"], resid, *extra)


def _mix_in(u, w_t, tm, after=()):
    t, d = u.shape
    after_ops, after_specs = _after_operands(after)

    def body(u_ref, w_ref, *rest):
        o_ref, s_ref = rest[-2:]
        uv = u_ref[...]
        for lo in range(0, D_FF, FFN_BLOCK):
            o_ref[:, lo:lo + FFN_BLOCK] = _dot(uv, w_ref[lo:lo + FFN_BLOCK, :], NT)
        s_ref[...] = o_ref[:, D_POOL:D_POOL + D_SSM].astype(BF16)

    return pl.pallas_call(
        body, out_shape=(SDS((t, D_FF), F32), SDS((t, D_SSM), BF16)), grid=(t // tm,),
        in_specs=[BS((tm, d), lambda i: (i, 0)), BS((D_FF, d), lambda i: (0, 0))] + after_specs,
        out_specs=(BS((tm, D_FF), lambda i: (i, 0)), BS((tm, D_SSM), lambda i: (i, 0))), name="mix_in",
        compiler_params=_params("parallel"))(u, w_t, *after_ops)


def _mm_resid_norm(name, a, b, resid, next_gain, tm):
    t, d = resid.shape
    tm = min(2 * tm, t)

    def body(a_ref, b_ref, res_ref, g_ref, h_ref, u_ref):
        h = res_ref[...] + _dot(a_ref[...], b_ref[...])
        h_ref[...] = h
        _norm_tile(h, g_ref, u_ref)

    row = BS((tm, d), lambda i: (i, 0))
    return pl.pallas_call(
        body, out_shape=(SDS((t, d), F32), SDS((t, d), BF16)), grid=(t // tm,),
        in_specs=[BS((tm, a.shape[1]), lambda i: (i, 0)), BS(b.shape, lambda i: (0, 0)), row, BS((1, d), lambda i: (0, 0))],
        out_specs=(row, row), name=name, compiler_params=_params("parallel"))(a, b, resid, next_gain)


def _ffn_bwd_act(name, dh_b, w_f, pg, pu, tm, after=()):
    t, d = dh_b.shape
    after_ops, after_specs = _after_operands(after)

    def body(dh_ref, wd_ref, pg_ref, pu_ref, *rest):
        dg_ref, dup_ref = rest[len(after_ops):]
        dh = dh_ref[...]
        for lo in range(0, D_FF, FFN_BLOCK):
            cols = slice(lo, lo + FFN_BLOCK)
            da = _dot(dh, wd_ref[cols, :], NT)
            dg_ref[:, cols] = (da * pg_ref[:, cols].astype(F32)).astype(BF16)
            dup_ref[:, cols] = (da * pu_ref[:, cols].astype(F32)).astype(BF16)

    hid = BS((tm, D_FF), lambda i: (i, 0))
    shape = SDS((t, D_FF), BF16)
    return pl.pallas_call(
        body, out_shape=(shape, shape), grid=(t // tm,),
        in_specs=[BS((tm, d), lambda i: (i, 0)), BS((D_FF, d), lambda i: (0, 0)), hid, hid] + after_specs,
        out_specs=(hid, hid), name=name,
        compiler_params=_params("parallel"))(dh_b, w_f["down"], pg, pu, *after_ops)


def _ffn_dw(name, u, dg, dup, a, dh_b, tm):
    t, d = u.shape
    n_t = t // tm

    def body(u_ref, dg_ref, dup_ref, a_ref, dh_ref, og_ref, ou_ref, od_ref, acc):
        i = pl.program_id(1)

        @pl.when(i == 0)
        def _():
            acc[...] = jnp.zeros_like(acc)

        uv = u_ref[...]
        acc[0] += _dot(dg_ref[...], uv, TN)
        acc[1] += _dot(dup_ref[...], uv, TN)
        acc[2] += _dot(a_ref[...], dh_ref[...], TN)

        @pl.when(i == n_t - 1)
        def _():
            og_ref[...] = acc[0].astype(BF16)
            ou_ref[...] = acc[1].astype(BF16)
            od_ref[...] = (0.5 * acc[2]).astype(BF16)

    hid = BS((tm, FFN_BLOCK), lambda j, i: (i, j))
    row = BS((tm, d), lambda j, i: (i, 0))
    out = BS((FFN_BLOCK, d), lambda j, i: (j, 0))
    shape = SDS((D_FF, d), BF16)
    return pl.pallas_call(
        body, out_shape=(shape, shape, shape), grid=(D_FF // FFN_BLOCK, n_t),
        in_specs=[row, hid, hid, hid, row], out_specs=(out, out, out),
        scratch_shapes=[pltpu.VMEM((3, FFN_BLOCK, d), F32)],
        name=name, compiler_params=_params("parallel", "arbitrary"))(u, dg, dup, a, dh_b)


def _norm_bwd_tile(i, du, h_ref, g_ref, dhin_ref, dh_ref, dhb_ref, dg_ref):
    hv = h_ref[...]
    r = lax.rsqrt(jnp.mean(hv * hv, axis=-1, keepdims=True) + EPS)
    n = hv * r
    dn = du * g_ref[...]
    dh = dhin_ref[...] + r * (dn - n * jnp.mean(dn * n, axis=-1, keepdims=True))
    dh_ref[...] = dh
    dhb_ref[...] = dh.astype(BF16)

    @pl.when(i == 0)
    def _():
        dg_ref[...] = jnp.zeros_like(dg_ref)

    dg_ref[...] += jnp.sum(du * n, axis=0, keepdims=True)


def _norm_bwd_specs(tm):
    row = BS((tm, D_MODEL), lambda i: (i, 0))
    vec = BS((1, D_MODEL), lambda i: (0, 0))
    return [row, vec, row], (row, row, vec)


def _norm_bwd_shapes(t):
    return SDS((t, D_MODEL), F32), SDS((t, D_MODEL), BF16), SDS((1, D_MODEL), F32)


def _ffn_dx(name, dg, dup, w_f, h, gain, dh_in, tm, after=()):
    t = dg.shape[0]
    tm = tm // 2
    after_ops, after_specs = _after_operands(after)

    def body(dg_ref, dup_ref, wg_ref, wu_ref, h_ref, g_ref, dhin_ref, *rest):
        du = _dot(dg_ref[...], wg_ref[...]) + _dot(dup_ref[...], wu_ref[...])
        _norm_bwd_tile(pl.program_id(0), du, h_ref, g_ref, dhin_ref, *rest[len(after_ops):])

    hid = BS((tm, D_FF), lambda i: (i, 0))
    whole = BS((D_FF, D_MODEL), lambda i: (0, 0))
    norm_in, norm_out = _norm_bwd_specs(tm)
    return pl.pallas_call(
        body, out_shape=_norm_bwd_shapes(t), grid=(t // tm,),
        in_specs=[hid, hid, whole, whole] + norm_in + after_specs, out_specs=norm_out, name=name,
        compiler_params=_params("arbitrary"))(dg, dup, w_f["gate"], w_f["up"], h, gain, dh_in, *after_ops)


def _mm_norm_bwd(name, a, b, dims, h, gain, dh_in, tm):
    pieces = list(a) if isinstance(a, (list, tuple)) else [a]
    assert len(pieces) == 1 or dims == NN
    t = pieces[0].shape[0]
    widths = [p.shape[1] for p in pieces]
    row0 = [sum(widths[:j]) for j in range(len(pieces))]

    def body(*refs):
        a_refs, (b_ref, h_ref, g_ref, dhin_ref), outs = refs[:len(pieces)], refs[len(pieces):len(pieces) + 4], refs[len(pieces) + 4:]
        if len(pieces) == 1:
            du = _dot(a_refs[0][...], b_ref[...], dims)
        else:
            du = _dot(a_refs[0][...], b_ref[0:widths[0], :])
            for a_ref, r0, w in zip(a_refs[1:], row0[1:], widths[1:]):
                du = du + _dot(a_ref[...], b_ref[r0:r0 + w, :])
        _norm_bwd_tile(pl.program_id(0), du, h_ref, g_ref, dhin_ref, *outs)

    norm_in, norm_out = _norm_bwd_specs(tm)
    return pl.pallas_call(
        body, out_shape=_norm_bwd_shapes(t), grid=(t // tm,),
        in_specs=[BS((tm, w), lambda i: (i, 0)) for w in widths] + [BS(b.shape, lambda i: (0, 0))] + norm_in,
        out_specs=norm_out, name=name, compiler_params=_params("arbitrary"))(*pieces, b, h, gain, dh_in)


def _plain_mm(name, a, b, dims, out_dtype, tm, resid=None, after=()):
    t = a.shape[0]
    tm = min(2 * tm, t)
    n = b.shape[1] if dims == NN else b.shape[0]
    extras = [(resid, BS((tm, n), lambda i: (i, 0)))] if resid is not None else []
    epi = (lambda acc, res: res + acc) if resid is not None else None
    return _mm(name, [(a, BS((tm, a.shape[1]), lambda i: (i, 0)), b, BS(b.shape, lambda i: (0, 0)), dims)],
               grid=(t // tm,), out_shape=SDS((t, n), out_dtype), out_spec=BS((tm, n), lambda i: (i, 0)),
               extras=extras, epilogue=epi, after=after)


def _dw_mm(name, a, b, tm, out_dtype=BF16, after=()):
    t, k = a.shape
    n = b.shape[1]
    tm = min(2 * tm, t)
    return _mm(name, [(a, BS((tm, k), lambda i: (i, 0)), b, BS((tm, n), lambda i: (i, 0)), TN)],
               grid=(t // tm,), red_axis=0, out_shape=SDS((k, n), out_dtype), out_spec=BS((k, n), lambda i: (0, 0)),
               after=after)


def _mix_in_dw(pieces, u, tm):
    t, d = u.shape
    tm = min(2 * tm, t)
    n_steps = t // tm
    widths = [p.shape[1] for p in pieces]
    row0 = [sum(widths[:j]) for j in range(len(pieces))]
    assert sum(widths) == D_FF

    def body(*refs):
        p_refs, u_ref, o_ref, acc_ref = refs[:len(pieces)], refs[len(pieces)], refs[-2], refs[-1]
        k = pl.program_id(0)

        @pl.when(k == 0)
        def _():
            acc_ref[...] = jnp.zeros_like(acc_ref)

        uv = u_ref[...]
        for p_ref, r0, w in zip(p_refs, row0, widths):
            acc_ref[r0:r0 + w, :] += _dot(p_ref[...], uv, TN)

        @pl.when(k == n_steps - 1)
        def _():
            o_ref[...] = acc_ref[...].astype(BF16)

    return pl.pallas_call(
        body, out_shape=SDS((D_FF, d), BF16), grid=(n_steps,),
        in_specs=[BS((tm, w), lambda i: (i, 0)) for w in widths] + [BS((tm, d), lambda i: (i, 0))],
        out_specs=BS((D_FF, d), lambda i: (0, 0)), scratch_shapes=[pltpu.VMEM((D_FF, d), F32)],
        name="mix_in_dw", compiler_params=_params("arbitrary"))(*pieces, u)


POOL_CHUNK = 256
POOL_HALO = 8


def _window_sum(v, width, lead):
    n = v.shape[0]
    s = v
    k = 1
    while k < width:
        s = s + pltpu.roll(s, n - k, 0)
        k *= 2
    return pltpu.roll(s, lead, 0) if lead else s


def _pool_count(base, left, right, t, shape):
    pos = base + lax.broadcasted_iota(jnp.int32, shape, 0)
    lo = jnp.maximum(pos - left, 0)
    hi = jnp.minimum(pos + right + 1, t)
    return (hi - lo).astype(F32)


def _pool_fwd(proj, pool_w, pool_scale):
    t = proj.shape[0]
    c, h = POOL_CHUNK, POOL_HALO
    n_chunks = t // c

    def body(proj_hbm, pw_ref, sc_ref, pooled_ref, mixed_ref, ms_ref, pad_ref, sem):
        cp = pltpu.make_async_copy(proj_hbm.at[:, pl.ds(0, D_POOL)], pad_ref.at[pl.ds(h, t), :], sem)
        cp.start()
        pad_ref[pl.ds(0, h), :] = jnp.zeros((h, D_POOL), F32)
        pad_ref[pl.ds(t + h, h), :] = jnp.zeros((h, D_POOL), F32)
        cp.wait()
        for g, width in enumerate(POOL_WINDOWS):
            left = width // 2
            right = width - 1 - left
            cols = slice(g * POOL_GROUP, (g + 1) * POOL_GROUP)
            wmat = pw_ref[g].astype(BF16)
            scale = sc_ref[:, cols]

            def chunk(ci, carry, left=left, right=right, width=width, cols=cols, wmat=wmat, scale=scale):
                base = pl.multiple_of(ci * c, c)
                v = pad_ref[pl.ds(base, c + 2 * h), cols]
                win = _window_sum(v, width, left)[h:h + c]
                cnt = _pool_count(base, left, right, t, (c, POOL_GROUP))
                pooled = (win / cnt - v[h:h + c]).astype(BF16)
                mixed = _dot(pooled, wmat)
                pooled_ref[pl.ds(base, c), cols] = pooled
                mixed_ref[pl.ds(base, c), cols] = mixed.astype(BF16)
                ms_ref[pl.ds(base, c), cols] = (mixed * scale).astype(BF16)
                return carry

            lax.fori_loop(0, n_chunks, chunk, 0)

    vm = BS(memory_space=pltpu.VMEM)
    shape = SDS((t, D_POOL), BF16)
    return pl.pallas_call(
        body, out_shape=(shape, shape, shape),
        in_specs=[BS(memory_space=pl.ANY), vm, vm], out_specs=(vm, vm, vm),
        scratch_shapes=[pltpu.VMEM((t + 2 * h, D_POOL), F32), pltpu.SemaphoreType.DMA],
        name="pool_fwd", compiler_params=_params())(proj, pool_w, pool_scale)


def _pool_bwd(d_ms, mixed, pooled, pool_w, pool_scale):
    t = d_ms.shape[0]
    c, h = POOL_CHUNK, POOL_HALO
    n_chunks = t // c

    def body(dms_ref, mixed_ref, pooled_ref, pw_ref, sc_ref, dp_ref, dsc_ref, dpw_ref, pad_ref):
        pad_ref[pl.ds(0, h), :] = jnp.zeros((h, D_POOL), F32)
        pad_ref[pl.ds(t + h, h), :] = jnp.zeros((h, D_POOL), F32)
        for g, width in enumerate(POOL_WINDOWS):
            left = width // 2
            right = width - 1 - left
            cols = slice(g * POOL_GROUP, (g + 1) * POOL_GROUP)
            wmat = pw_ref[g].astype(BF16)
            scale = sc_ref[:, cols]

            def first(ci, carry, left=left, right=right, cols=cols, wmat=wmat, scale=scale):
                dsc, dpw = carry
                base = pl.multiple_of(ci * c, c)
                dms = dms_ref[pl.ds(base, c), cols].astype(F32)
                dsc = dsc + jnp.sum(dms * mixed_ref[pl.ds(base, c), cols].astype(F32), axis=0, keepdims=True)
                dmix = (dms * scale).astype(BF16)
                dpw = dpw + _dot(pooled_ref[pl.ds(base, c), cols], dmix, TN)
                dpooled = _dot(dmix, wmat, NT)
                cnt = _pool_count(base, left, right, t, (c, POOL_GROUP))
                pad_ref[pl.ds(base + h, c), cols] = dpooled / cnt
                return dsc, dpw

            dsc, dpw = lax.fori_loop(0, n_chunks, first,
                                     (jnp.zeros((1, POOL_GROUP), F32), jnp.zeros((POOL_GROUP, POOL_GROUP), F32)))
            dsc_ref[:, cols] = dsc
            dpw_ref[g] = dpw

            def second(ci, carry, left=left, right=right, width=width, cols=cols):
                base = pl.multiple_of(ci * c, c)
                v = pad_ref[pl.ds(base, c + 2 * h), cols]
                win = _window_sum(v, width, right)[h:h + c]
                cnt = _pool_count(base, left, right, t, (c, POOL_GROUP))
                dp_ref[pl.ds(base, c), cols] = (win - v[h:h + c] * cnt).astype(BF16)
                return carry

            lax.fori_loop(0, n_chunks, second, 0)

    vm = BS(memory_space=pltpu.VMEM)
    return pl.pallas_call(
        body, out_shape=(SDS((t, D_POOL), BF16), SDS((1, D_POOL), F32), SDS((4, POOL_GROUP, POOL_GROUP), F32)),
        in_specs=[vm] * 5, out_specs=(vm, vm, vm),
        scratch_shapes=[pltpu.VMEM((t + 2 * h, D_POOL), F32)],
        name="pool_bwd", compiler_params=_params())(d_ms, mixed, pooled, pool_w, pool_scale)


SSM_ROWS = 2 * SSM_GROUPS * SSM_GROUP
SSM_HALF = SSM_GROUPS * SSM_GROUP


def _ssm_zoh(a_r, a_i, ldt):
    dt = jnp.exp(ldt)
    mag = jnp.exp(dt * a_r)
    ang = dt * a_i
    cs, sn = jnp.cos(ang), jnp.sin(ang)
    abr, abi = mag * cs, mag * sn
    den = a_r * a_r + a_i * a_i
    nr = abr - 1.0
    qr = (nr * a_r + abi * a_i) / den
    qi = (abi * a_r - nr * a_i) / den
    return dt, mag, cs, sn, abr, abi, den, nr, qr, qi


def _ssm_group_mask():
    row = lax.broadcasted_iota(jnp.int32, (SSM_HALF, SSM_CH), 0)
    col = lax.broadcasted_iota(jnp.int32, (SSM_HALF, SSM_CH), 1)
    return (row // SSM_GROUP) == (col // SSM_STATE)


def _ssm_prep(a_r, a_i, ldt, b_r, b_i, c_r, c_i, after=()):
    after_ops, after_specs = _after_operands(after)

    def body(ar_ref, ai_ref, ldt_ref, br_ref, bi_ref, cr_ref, ci_ref, *rest):
        abr_ref, abi_ref, win_ref, wint_ref, woutt_ref, wout_ref = rest[len(after_ops):]
        *_, abr, abi, _, _, qr, qi = _ssm_zoh(ar_ref[...], ai_ref[...], ldt_ref[...])
        abr_ref[...] = abr
        abi_ref[...] = abi
        b_r, b_i = br_ref[...], bi_ref[...]
        bbr = qr * b_r - qi * b_i
        bbi = qr * b_i + qi * b_r
        mask = _ssm_group_mask()
        state = lax.broadcasted_iota(jnp.int32, (SSM_STATE, SSM_CH), 0)
        col = lax.broadcasted_iota(jnp.int32, (SSM_STATE, SSM_CH), 1)
        every_group = (col % SSM_STATE == state).astype(BF16)

        def spread(x):
            return jnp.where(mask, _dot(x, every_group), 0.0)

        for d in range(2):
            rows = slice(d * SSM_HALF, (d + 1) * SSM_HALF)
            for half, x_in, x_out in ((0, bbr[rows], cr_ref[rows, :]), (1, bbi[rows], -ci_ref[rows, :])):
                cols = slice(half * SSM_CH, (half + 1) * SSM_CH)
                m_in, m_out = spread(x_in), spread(x_out)
                win_ref[d, :, cols] = m_in.astype(BF16)
                wint_ref[d, cols, :] = m_in.T.astype(BF16)
                woutt_ref[d, :, cols] = m_out.astype(BF16)
                wout_ref[d, cols, :] = m_out.T.astype(BF16)

    vm = BS(memory_space=pltpu.VMEM)
    vec = SDS((SSM_ROWS, SSM_STATE), F32)
    wide = SDS((2, SSM_HALF, 2 * SSM_CH), BF16)
    tall = SDS((2, 2 * SSM_CH, SSM_HALF), BF16)
    return pl.pallas_call(body, out_shape=(vec, vec, wide, tall, wide, tall), in_specs=[vm] * 7 + after_specs,
                          out_specs=(vm,) * 6, name="ssm_prep",
                          compiler_params=_params())(a_r, a_i, ldt, b_r, b_i, c_r, c_i, *after_ops)


def _ssm_prep_bwd(a_r, a_i, ldt, b_r, b_i, d_abr, d_abi, d_win, d_woutt):
    def body(ar_ref, ai_ref, ldt_ref, br_ref, bi_ref, *rest):
        (dabr_refs, dabi_refs, dwin_refs, dwoutt_refs), outs = [rest[2 * k:2 * k + 2] for k in range(4)], rest[8:]
        dar_ref, dai_ref, dldt_ref, dbr_ref, dbi_ref, dcr_ref, dci_ref = outs
        a_r, a_i = ar_ref[...], ai_ref[...]
        dt, mag, cs, sn, abr, abi, den, nr, qr, qi = _ssm_zoh(a_r, a_i, ldt_ref[...])
        mask = _ssm_group_mask()
        col = lax.broadcasted_iota(jnp.int32, (SSM_CH, SSM_STATE), 0)
        state = lax.broadcasted_iota(jnp.int32, (SSM_CH, SSM_STATE), 1)
        own_state = (col % SSM_STATE == state).astype(BF16)

        def pick(dense):
            m = jnp.where(mask, dense, 0.0)
            hi = m.astype(BF16)
            lo = m - hi.astype(F32)
            return _dot(hi, own_state) + _dot(lo, own_state)

        def picked(refs, half):
            cols = slice(half * SSM_CH, (half + 1) * SSM_CH)
            return jnp.concatenate([pick(ref[:, cols]) for ref in refs], axis=0)

        first_channel = lax.broadcasted_iota(jnp.int32, (SSM_HALF, SSM_CH), 0) % SSM_GROUP == 0

        def first_rows(refs):
            return jnp.concatenate(
                [pick(jnp.where(first_channel, jnp.broadcast_to(ref[...], (SSM_HALF, SSM_CH)), 0.0)) for ref in refs], axis=0)

        g_r, g_i = picked(dwin_refs, 0), picked(dwin_refs, 1)
        dcr_ref[...] = picked(dwoutt_refs, 0)
        dci_ref[...] = -picked(dwoutt_refs, 1)
        b_r, b_i = br_ref[...], bi_ref[...]
        dbr_ref[...] = g_r * qr + g_i * qi
        dbi_ref[...] = g_i * qr - g_r * qi
        gqr = g_r * b_r + g_i * b_i
        gqi = g_i * b_r - g_r * b_i
        g_nr_num = gqr / den
        g_ni_num = gqi / den
        g_den = -(gqr * qr + gqi * qi) / den
        g_nr = g_nr_num * a_r - g_ni_num * a_i
        g_abi = g_nr_num * a_i + g_ni_num * a_r
        d_ar = g_nr_num * nr + g_ni_num * abi + 2.0 * a_r * g_den
        d_ai = g_nr_num * abi - g_ni_num * nr + 2.0 * a_i * g_den
        g_abr = first_rows(dabr_refs) + g_nr
        g_abi = first_rows(dabi_refs) + g_abi
        g_mag = g_abr * cs + g_abi * sn
        g_ang = mag * (g_abi * cs - g_abr * sn)
        g_e = g_mag * mag
        d_ar = d_ar + g_e * dt
        d_ai = d_ai + g_ang * dt
        g_dt = g_e * a_r + g_ang * a_i
        dar_ref[...] = d_ar
        dai_ref[...] = d_ai
        dldt_ref[...] = g_dt * dt

    vm = BS(memory_space=pltpu.VMEM)
    vec = SDS((SSM_ROWS, SSM_STATE), F32)
    return pl.pallas_call(body, out_shape=(vec,) * 7, in_specs=[vm] * 13, out_specs=(vm,) * 7, name="ssm_prep_bwd",
                          compiler_params=_params())(a_r, a_i, ldt, b_r, b_i, *d_abr, *d_abi, *d_win, *d_woutt)


SCAN_ROWS = 512
SCAN_SUB = 128


def _ssm_scan(name, inp, w1, a_r, a_i, w2, reverse):
    t = inp.shape[0]
    rows = min(SCAN_ROWS, t)
    n = t // rows
    n_sub = rows // SCAN_SUB
    ch = SSM_CH
    at = (lambda i: (n - 1 - i, 0)) if reverse else (lambda i: (i, 0))

    def body(in_ref, w1_ref, ar_ref, ai_ref, w2_ref, sb_ref, out_ref, cr_ref, ci_ref, k_ref, st_ref):
        i = pl.program_id(0)

        @pl.when(i == 0)
        def _():
            ar8 = jnp.broadcast_to(ar_ref[...], (8, ch))
            ai8 = jnp.broadcast_to(ai_ref[...], (8, ch))
            row = lax.broadcasted_iota(jnp.int32, (8, ch), 0)
            rank = (7 - row) if reverse else row
            powers = [(ar8, ai8)]
            for _ in range(7):
                p_r, p_i = powers[-1]
                powers.append((p_r * ar8 - p_i * ai8, p_r * ai8 + p_i * ar8))
            zero = jnp.zeros((8, ch), F32)
            for slot, k in enumerate((1, 2, 4)):
                k_ref[2 * slot] = jnp.where(rank >= k, powers[k - 1][0], zero)
                k_ref[2 * slot + 1] = jnp.where(rank >= k, powers[k - 1][1], zero)
            carry_r, carry_i = zero, zero
            for j in range(8):
                carry_r = jnp.where(rank == j, powers[j][0], carry_r)
                carry_i = jnp.where(rank == j, powers[j][1], carry_i)
            k_ref[6] = carry_r
            k_ref[7] = carry_i
            cr_ref[...] = zero
            ci_ref[...] = zero

        def group(r0, carry):
            c_r, c_i = carry
            x_r = st_ref[pl.ds(r0, 8), 0:ch]
            x_i = st_ref[pl.ds(r0, 8), ch:2 * ch]
            for slot, k in enumerate((1, 2, 4)):
                shift = (8 - k) if reverse else k
                s_r = pltpu.roll(x_r, shift, 0)
                s_i = pltpu.roll(x_i, shift, 0)
                m_r, m_i = k_ref[2 * slot], k_ref[2 * slot + 1]
                x_r, x_i = x_r + m_r * s_r - m_i * s_i, x_i + m_r * s_i + m_i * s_r
            p_r, p_i = k_ref[6], k_ref[7]
            x_r, x_i = x_r + p_r * c_r - p_i * c_i, x_i + p_r * c_i + p_i * c_r
            st_ref[pl.ds(r0, 8), 0:ch] = x_r
            st_ref[pl.ds(r0, 8), ch:2 * ch] = x_i
            last = 0 if reverse else 7
            return (jnp.broadcast_to(x_r[last:last + 1, :], (8, ch)), jnp.broadcast_to(x_i[last:last + 1, :], (8, ch)))

        carry = (cr_ref[...], ci_ref[...])
        for sc in (range(n_sub - 1, -1, -1) if reverse else range(n_sub)):
            part = pl.ds(sc * SCAN_SUB, SCAN_SUB)
            st_ref[part, :] = _dot(in_ref[part, :], w1_ref[...])
            for gi in range(SCAN_SUB // 8):
                g = (SCAN_SUB // 8 - 1 - gi) if reverse else gi
                carry = group(sc * SCAN_SUB + g * 8, carry)
            states = st_ref[part, :].astype(BF16)
            sb_ref[part, :] = states
            out_ref[part, :] = _dot(states, w2_ref[...])
        cr_ref[...] = carry[0]
        ci_ref[...] = carry[1]

    return pl.pallas_call(
        body, out_shape=(SDS((t, 2 * ch), BF16), SDS((t, D_SSM), F32)), grid=(n,),
        in_specs=[BS((rows, D_SSM), at), BS((D_SSM, 2 * ch), lambda i: (0, 0)), BS((1, ch), lambda i: (0, 0)),
                  BS((1, ch), lambda i: (0, 0)), BS((2 * ch, D_SSM), lambda i: (0, 0))],
        out_specs=(BS((rows, 2 * ch), at), BS((rows, D_SSM), at)),
        scratch_shapes=[pltpu.VMEM((8, ch), F32), pltpu.VMEM((8, ch), F32), pltpu.VMEM((8, 8, ch), F32),
                        pltpu.VMEM((rows, 2 * ch), F32)],
        name=name, compiler_params=_params("arbitrary"))(inp, w1, a_r, a_i, w2)


DA_ROWS = 1024


def _ssm_param_grads(name, lam, states, u, dy, reverse, after=()):
    t = lam.shape[0]
    rows = min(DA_ROWS, t)
    n = t // rows
    halo_rows = 16
    nb = rows // halo_rows
    ch = SSM_CH
    if reverse:
        halo_at = lambda i: (jnp.minimum((i + 1) * nb, t // halo_rows - 1), 0)
    else:
        halo_at = lambda i: (jnp.maximum(i * nb - 1, 0), 0)

    after_ops, after_specs = _after_operands(after)

    def body(lam_ref, x_ref, halo_ref, u_ref, dy_ref, *rest):
        dr_ref, di_ref, dwin_ref, dwoutt_ref = rest[len(after_ops):]
        i = pl.program_id(0)

        @pl.when(i == 0)
        def _():
            dr_ref[...] = jnp.zeros_like(dr_ref)
            di_ref[...] = jnp.zeros_like(di_ref)
            dwin_ref[...] = jnp.zeros_like(dwin_ref)
            dwoutt_ref[...] = jnp.zeros_like(dwoutt_ref)

        dwin_ref[...] += _dot(u_ref[...], lam_ref[...], TN)
        dwoutt_ref[...] += _dot(dy_ref[...], x_ref[...], TN)
        row = lax.broadcasted_iota(jnp.int32, (rows, ch), 0)
        if reverse:
            edge, shift, h_row, live = rows - 1, rows - 1, 0, i < n - 1
        else:
            edge, shift, h_row, live = 0, 1, halo_rows - 1, i > 0

        def neighbour(lo):
            halo = halo_ref[:, lo:lo + ch].astype(F32)[h_row:h_row + 1]
            halo = jnp.where(live, halo, 0.0)
            x = x_ref[:, lo:lo + ch].astype(F32)
            return jnp.where(row == edge, jnp.broadcast_to(halo, (rows, ch)), pltpu.roll(x, shift, 0))

        xp_r, xp_i = neighbour(0), neighbour(ch)
        l_r, l_i = lam_ref[:, 0:ch].astype(F32), lam_ref[:, ch:2 * ch].astype(F32)
        dr_ref[...] += jnp.sum(l_r * xp_r + l_i * xp_i, axis=0, keepdims=True)
        di_ref[...] += jnp.sum(l_i * xp_r - l_r * xp_i, axis=0, keepdims=True)

    blk = BS((rows, 2 * ch), lambda i: (i, 0))
    thin = BS((rows, D_SSM), lambda i: (i, 0))
    vec = BS((1, ch), lambda i: (0, 0))
    mat = BS((D_SSM, 2 * ch), lambda i: (0, 0))
    return pl.pallas_call(
        body, out_shape=(SDS((1, ch), F32), SDS((1, ch), F32), SDS((D_SSM, 2 * ch), F32), SDS((D_SSM, 2 * ch), F32)),
        grid=(n,), in_specs=[blk, blk, BS((halo_rows, 2 * ch), halo_at), thin, thin] + after_specs,
        out_specs=(vec, vec, mat, mat),
        name=name, compiler_params=_params("arbitrary"))(lam, states, states, u, dy, *after_ops)


GELU_C = math.sqrt(2.0 / math.pi)
GELU_K = 0.044715


def _ssm_combine(proj, y_fwd, y_bwd, d_skip, tm, after=()):
    t = proj.shape[0]
    after_ops, after_specs = _after_operands(after)

    def body(s_ref, yf_ref, yb_ref, d_ref, *rest):
        yt_ref, g_ref = rest[len(after_ops):]
        y = s_ref[...] * d_ref[...] + yf_ref[...] + yb_ref[...]
        yt_ref[...] = y
        th = jnp.tanh(GELU_C * (y + GELU_K * y * y * y))
        g_ref[...] = (0.5 * y * (1.0 + th)).astype(BF16)

    blk = BS((tm, D_SSM), lambda i: (i, 0))
    return pl.pallas_call(
        body, out_shape=(SDS((t, D_SSM), F32), SDS((t, D_SSM), BF16)), grid=(t // tm,),
        in_specs=[BS((tm, D_SSM), lambda i: (i, D_POOL // D_SSM)), blk, blk, BS((1, D_SSM), lambda i: (0, 0))] + after_specs,
        out_specs=(blk, blk), name="ssm_combine",
        compiler_params=_params("parallel"))(proj, y_fwd, y_bwd, d_skip, *after_ops)


def _ssm_ds(proj, d_yt, du_fwd, du_bwd, d_skip, tm):
    t = proj.shape[0]

    def body(s_ref, dy_ref, duf_ref, dub_ref, d_ref, ds_ref, dd_ref):
        i = pl.program_id(0)
        dy = dy_ref[...]
        ds_ref[...] = (dy * d_ref[...] + duf_ref[...] + dub_ref[...]).astype(BF16)

        @pl.when(i == 0)
        def _():
            dd_ref[...] = jnp.zeros_like(dd_ref)

        dd_ref[...] += jnp.sum(dy * s_ref[...], axis=0, keepdims=True)

    blk = BS((tm, D_SSM), lambda i: (i, 0))
    vec = BS((1, D_SSM), lambda i: (0, 0))
    return pl.pallas_call(
        body, out_shape=(SDS((t, D_SSM), BF16), SDS((1, D_SSM), F32)), grid=(t // tm,),
        in_specs=[BS((tm, D_SSM), lambda i: (i, D_POOL // D_SSM)), blk, blk, blk, vec],
        out_specs=(blk, vec), name="ssm_ds", compiler_params=_params("arbitrary"))(proj, d_yt, du_fwd, du_bwd, d_skip)


G_POOL_AT = D_POOL + D_SSM
G_SSM_AT = G_POOL_AT + D_MODEL
E_VAL, E_GATE = D_POOL, D_POOL + D_SSM


def _merge_specs(tm):
    return [BS((tm, D_POOL), lambda i: (i, 0)), BS((tm, D_SSM), lambda i: (i, 0)),
            BS((N_SHARD, 1024, 256), lambda i: (0, 0, 0)), BS((tm, D_FF), lambda i: (i, 0))]


def _merge_parts(s, ms, yv, w_ref, proj_ref):
    lo = 256 * s
    zp = _dot(ms, w_ref[s, 0:E_VAL, :])
    zv = _dot(yv, w_ref[s, E_VAL:E_GATE, :])
    zg = _dot(yv, w_ref[s, E_GATE:, :])
    return zp, zv, zg, proj_ref[:, G_POOL_AT + lo:G_POOL_AT + lo + 256], proj_ref[:, G_SSM_AT + lo:G_SSM_AT + lo + 256]


def _mixer_merge(ms, yssm, w_e, proj, tm):
    t = ms.shape[0]

    def body(ms_ref, y_ref, w_ref, proj_ref, o_ref):
        msv, yv = ms_ref[...], y_ref[...]
        for s in range(N_SHARD):
            zp, zv, zg, gp, gs = _merge_parts(s, msv, yv, w_ref, proj_ref)
            o_ref[:, 256 * s:256 * (s + 1)] = (_sigmoid(gp) * zp + _sigmoid(gs) * zv * _sigmoid(zg)).astype(BF16)

    row = BS((tm, D_MODEL), lambda i: (i, 0))
    return pl.pallas_call(
        body, out_shape=SDS((t, D_MODEL), BF16), grid=(t // tm,), in_specs=_merge_specs(tm), out_specs=row,
        name="mixer_merge", compiler_params=_params("parallel"))(ms, yssm, w_e, proj)


def _mixer_merge_bwd(ms, yssm, w_e, proj, dmerged, tm):
    t = ms.shape[0]

    def body(ms_ref, y_ref, w_ref, proj_ref, dm_ref, dgp_ref, dgs_ref, dzp_ref, dzv_ref, dzg_ref):
        msv, yv = ms_ref[...], y_ref[...]
        for s in range(N_SHARD):
            cols = slice(256 * s, 256 * (s + 1))
            zp, zv, zg, gp, gs = _merge_parts(s, msv, yv, w_ref, proj_ref)
            dm = dm_ref[:, cols].astype(F32)
            sp, ss, sg = _sigmoid(gp), _sigmoid(gs), _sigmoid(zg)
            dgp_ref[:, cols] = (dm * zp * sp * (1.0 - sp)).astype(BF16)
            dgs_ref[:, cols] = (dm * zv * sg * ss * (1.0 - ss)).astype(BF16)
            dzp_ref[:, cols] = (dm * sp).astype(BF16)
            dz = dm * ss
            dzv_ref[:, cols] = (dz * sg).astype(BF16)
            dzg_ref[:, cols] = (dz * zv * sg * (1.0 - sg)).astype(BF16)

    row = BS((tm, D_MODEL), lambda i: (i, 0))
    shape = SDS((t, D_MODEL), BF16)
    return pl.pallas_call(
        body, out_shape=(shape,) * 5, grid=(t // tm,), in_specs=_merge_specs(tm) + [row],
        out_specs=(row,) * 5, name="mixer_merge_bwd",
        compiler_params=_params("parallel"))(ms, yssm, w_e, proj, dmerged)


def _mixer_dw(ms, yssm, dzp, dzv, dzg, tm):
    t = ms.shape[0]
    tm = min(2 * tm, t)
    n_t = t // tm

    def body(ms_ref, y_ref, dzp_ref, dzv_ref, dzg_ref, o_ref, acc):
        i = pl.program_id(0)

        @pl.when(i == 0)
        def _():
            acc[...] = jnp.zeros_like(acc)

        msv, yv = ms_ref[...], y_ref[...]
        for s in range(N_SHARD):
            cols = slice(256 * s, 256 * (s + 1))
            acc[s, 0:E_VAL, :] += _dot(msv, dzp_ref[:, cols], TN)
            acc[s, E_VAL:E_GATE, :] += _dot(yv, dzv_ref[:, cols], TN)
            acc[s, E_GATE:, :] += _dot(yv, dzg_ref[:, cols], TN)

        @pl.when(i == n_t - 1)
        def _():
            o_ref[...] = acc[...].astype(BF16)

    row = BS((tm, D_MODEL), lambda i: (i, 0))
    full = BS((N_SHARD, 1024, 256), lambda i: (0, 0, 0))
    return pl.pallas_call(
        body, out_shape=SDS((N_SHARD, 1024, 256), BF16), grid=(n_t,),
        in_specs=[BS((tm, D_POOL), lambda i: (i, 0)), BS((tm, D_SSM), lambda i: (i, 0)), row, row, row],
        out_specs=full, scratch_shapes=[pltpu.VMEM((N_SHARD, 1024, 256), F32)],
        name="mixer_dw", compiler_params=_params("arbitrary"))(ms, yssm, dzp, dzv, dzg)


def _mixer_dx(dzp, dzv, dzg, w_e, y_total, tm):
    t = dzp.shape[0]

    def body(dzp_ref, dzv_ref, dzg_ref, w_ref, yt_ref, dms_ref, dy_ref, dyb_ref):
        acc_ms, acc_y = None, None
        for s in range(N_SHARD):
            cols = slice(256 * s, 256 * (s + 1))
            part_ms = _dot(dzp_ref[:, cols], w_ref[s, 0:E_VAL, :], NT)
            part_y = _dot(dzv_ref[:, cols], w_ref[s, E_VAL:E_GATE, :], NT) + _dot(dzg_ref[:, cols], w_ref[s, E_GATE:, :], NT)
            acc_ms = part_ms if s == 0 else acc_ms + part_ms
            acc_y = part_y if s == 0 else acc_y + part_y
        dms_ref[...] = acc_ms.astype(BF16)
        y = yt_ref[...]
        th = jnp.tanh(GELU_C * (y + GELU_K * y * y * y))
        dgelu = 0.5 * (1.0 + th) + 0.5 * y * (1.0 - th * th) * GELU_C * (1.0 + 3.0 * GELU_K * y * y)
        dy = acc_y * dgelu
        dy_ref[...] = dy
        dyb_ref[...] = dy.astype(BF16)

    row = BS((tm, D_MODEL), lambda i: (i, 0))
    narrow = BS((tm, D_SSM), lambda i: (i, 0))
    return pl.pallas_call(
        body, out_shape=(SDS((t, D_POOL), BF16), SDS((t, D_SSM), F32), SDS((t, D_SSM), BF16)), grid=(t // tm,),
        in_specs=[row, row, row, BS((N_SHARD, 1024, 256), lambda i: (0, 0, 0)), narrow],
        out_specs=(BS((tm, D_POOL), lambda i: (i, 0)), narrow, narrow),
        name="mixer_dx", compiler_params=_params("parallel"))(dzp, dzv, dzg, w_e, y_total)


def _attn_probs(q_h, k_h):
    s = _dot(q_h, k_h, NT) * (1.0 / math.sqrt(HEAD_DIM))
    e = jnp.exp(s - jnp.max(s, axis=-1, keepdims=True))
    return e / jnp.sum(e, axis=-1, keepdims=True)


def _attn_fwd(q, kv, tm):
    t = q.shape[0]
    tm = min(2 * tm, t)
    m = kv.shape[0]

    def body(q_ref, kv_ref, o_ref):
        for hd in range(N_HEADS):
            lo = hd * HEAD_DIM
            p = _attn_probs(q_ref[:, lo:lo + HEAD_DIM], kv_ref[:, lo:lo + HEAD_DIM])
            o_ref[:, lo:lo + HEAD_DIM] = _dot(p, kv_ref[:, D_MODEL + lo:D_MODEL + lo + HEAD_DIM]).astype(BF16)

    return pl.pallas_call(
        body, out_shape=SDS((t, D_MODEL), BF16), grid=(t // tm,),
        in_specs=[BS((tm, D_MODEL), lambda i: (i, 0)), BS((m, 2 * D_MODEL), lambda i: (0, 0))],
        out_specs=BS((tm, D_MODEL), lambda i: (i, 0)), name="attn_fwd", compiler_params=_params("parallel"))(q, kv)


def _attn_bwd(q, kv, d_o, tm):
    t = q.shape[0]
    m = kv.shape[0]

    def body(q_ref, kv_ref, do_ref, dq_ref, dkv_ref):
        i = pl.program_id(0)

        @pl.when(i == 0)
        def _():
            dkv_ref[...] = jnp.zeros_like(dkv_ref)

        for hd in range(N_HEADS):
            lo = hd * HEAD_DIM
            q_h = q_ref[:, lo:lo + HEAD_DIM]
            k_h = kv_ref[:, lo:lo + HEAD_DIM]
            v_h = kv_ref[:, D_MODEL + lo:D_MODEL + lo + HEAD_DIM]
            do_h = do_ref[:, lo:lo + HEAD_DIM]
            p = _attn_probs(q_h, k_h)
            dkv_ref[:, D_MODEL + lo:D_MODEL + lo + HEAD_DIM] += _dot(p, do_h, TN)
            dp = _dot(do_h, v_h, NT)
            ds = p * (dp - jnp.sum(dp * p, axis=-1, keepdims=True)) * (1.0 / math.sqrt(HEAD_DIM))
            dq_ref[:, lo:lo + HEAD_DIM] = _dot(ds, k_h).astype(BF16)
            dkv_ref[:, lo:lo + HEAD_DIM] += _dot(ds, q_h, TN)

    row = BS((tm, D_MODEL), lambda i: (i, 0))
    full = BS((m, 2 * D_MODEL), lambda i: (0, 0))
    return pl.pallas_call(
        body, out_shape=(SDS((t, D_MODEL), BF16), SDS((m, 2 * D_MODEL), F32)), grid=(t // tm,),
        in_specs=[row, full, row], out_specs=(row, full), name="attn_bwd",
        compiler_params=_params("arbitrary"))(q, kv, d_o)


TRANSPOSED = ("ffn1_w_gate", "ffn1_w_up", "ffn2_w_gate", "ffn2_w_up", "w_in")
GATHER_PHASES = {"f1a": (("ffn1_w_gate",), ("ffn1_w_up",)),
                 "f1b": (("ffn1_w_down",),),
                 "win": (("w_in",),),
                 "mix": (("w_mix_out", "w_q", "w_xo"), ("w_kv",), ("w_pool_proj", "w_glu_val", "w_glu_gate")),
                 "f2": (("ffn2_w_gate",), ("ffn2_w_up",), ("ffn2_w_down",))}
REDUCE_GROUPS = (("ffn2_w_gate",), ("ffn2_w_up",), ("ffn2_w_down",), ("w_xo",), ("w_q",), ("w_kv",), ("w_mix_out",),
                 ("w_pool_proj", "w_glu_val", "w_glu_gate"), ("w_in",), ("ffn1_w_gate",), ("ffn1_w_up",), ("ffn1_w_down",))
SMALL = ("ffn1_norm", "mix_norm", "pool_w", "pool_scale", "ssm_a_re", "ssm_a_im", "ssm_log_dt", "ssm_b_re",
         "ssm_b_im", "ssm_c_re", "ssm_c_im", "ssm_d", "xattn_norm", "mem_norm", "ffn2_norm", "final_norm")
WEIGHTS = ("ffn1_norm", "ffn1_w_gate", "ffn1_w_up", "ffn1_w_down", "mix_norm", "w_in", "pool_w", "pool_scale",
           "w_pool_proj", "ssm_a_re", "ssm_a_im", "ssm_log_dt", "ssm_b_re", "ssm_b_im", "ssm_c_re", "ssm_c_im",
           "ssm_d", "w_glu_val", "w_glu_gate", "w_mix_out", "xattn_norm", "mem_norm", "w_q", "w_kv", "w_xo",
           "ffn2_norm", "ffn2_w_gate", "ffn2_w_up", "ffn2_w_down", "final_norm")


def _small_view(a, n):
    return jnp.swapaxes(a, 3, 4) if n in ("ssm_b_re", "ssm_b_im") else a


def _device_step(x, mem, target, wts, sp, reducer=None):
    t = x.shape[0]
    tm = min(TM, t)
    g = {}

    first_gather = wts.start("win", wts.start("f1b", wts.start("f1a")))
    u1 = _rmsnorm("norm_ffn1", x, sp["ffn1_norm"], tm, after=first_gather)

    def per_channel(a):
        a = a.reshape(2 * SSM_GROUPS, 1, -1)
        return jnp.broadcast_to(a, (2 * SSM_GROUPS, SSM_GROUP, a.shape[-1])).reshape(SSM_ROWS, a.shape[-1])

    ssm_a = per_channel(sp["ssm_a_re"]), per_channel(sp["ssm_a_im"]), per_channel(sp["ssm_log_dt"])
    ssm_b = sp["ssm_b_re"].reshape(SSM_ROWS, SSM_STATE), sp["ssm_b_im"].reshape(SSM_ROWS, SSM_STATE)
    abr, abi, w_in_s, w_in_s_t, w_out_s_t, w_out_s = _ssm_prep(
        *ssm_a, *ssm_b, sp["ssm_c_re"].reshape(SSM_ROWS, SSM_STATE), sp["ssm_c_im"].reshape(SSM_ROWS, SSM_STATE),
        after=first_gather)
    first_rows = (2, SSM_GROUPS, SSM_GROUP, SSM_STATE)
    a_r = abr.reshape(first_rows)[:, :, 0].reshape(2, 1, SSM_CH)
    a_i = abi.reshape(first_rows)[:, :, 0].reshape(2, 1, SSM_CH)
    mem_n = _rmsnorm("norm_mem", mem, sp["mem_norm"], mem.shape[0], after=first_gather)

    whole = (D_FF, D_MODEL)
    w_g1, w_u1 = wts.finish("f1a", [u1, w_in_s, w_in_s_t, w_out_s, w_out_s_t, a_r, a_i, mem_n])
    w_f1 = {"gate": w_g1.reshape(whole), "up": w_u1.reshape(whole)}
    g1, up1, a1 = _ffn_up("ffn1_up", u1, w_f1, tm)
    (w_dn,) = wts.finish("f1b", [a1])
    w_f1["down"] = w_dn.reshape(whole)
    h1, u2 = _ffn_down("ffn1_down", a1, w_f1, x, tm, next_gain=sp["mix_norm"])

    (w_in_g,) = wts.finish("win", [u2])
    w_in_t = w_in_g.reshape(D_FF, D_MODEL)
    proj, s_in = _mix_in(u2, w_in_t, tm, after=wts.start("f2", wts.start("mix", [w_in_g])))
    pooled, mixed, ms = _pool_fwd(proj, sp["pool_w"][0], sp["pool_scale"])

    states, y_dirs = [], []
    for dr in range(2):
        st, yd = _ssm_scan(f"ssm_scan_fwd{dr}", s_in, w_in_s[dr], a_r[dr], a_i[dr], w_out_s[dr], reverse=(dr == 1))
        states.append(st)
        y_dirs.append(yd)
    w_sq, w_kv, w_e = wts.finish("mix", y_dirs)
    w_mo, w_q, w_xo = (w_sq[:, 256 * k:256 * (k + 1)].reshape(D_MODEL, D_MODEL) for k in range(3))
    w_d = w_kv[:, None]
    y_total, yssm = _ssm_combine(proj, y_dirs[0], y_dirs[1], sp["ssm_d"], tm)

    merged = _mixer_merge(ms, yssm, w_e, proj, tm)
    h2, u3 = _mm_resid_norm("mix_out", merged, w_mo, h1, sp["xattn_norm"], tm)

    q = _plain_mm("attn_q", u3, w_q, NN, BF16, tm)
    n_mem = mem.shape[0]
    kv = _mm("attn_kv", [(mem_n, BS((n_mem, D_MODEL), lambda s: (0, 0)), w_d, BS((None, None, D_MODEL, 512), lambda s: (s, 0, 0, 0)), NN)],
             grid=(N_SHARD,), out_shape=SDS((n_mem, 2 * D_MODEL), BF16), out_spec=BS((n_mem, 512), lambda s: (0, s)))
    o = _attn_fwd(q, kv, tm)
    h3, u4 = _mm_resid_norm("attn_out", o, w_xo, h2, sp["ffn2_norm"], tm)

    w_f2 = dict(zip(("gate", "up", "down"), (a.reshape(whole) for a in wts.finish("f2", [u4]))))
    g2, up2, a2 = _ffn_up("ffn2_up", u4, w_f2, tm)
    loss, dh4, dh4_b, g["final_norm"] = _ffn_down("ffn2_down", a2, w_f2, h3, tm,
                                                  head=(sp["final_norm"].reshape(1, D_MODEL), target))

    dg2, dup2 = _ffn_bwd_act("ffn2_bwd_act", dh4_b, w_f2, g2, up2, tm)
    dw_f2 = _ffn_dw("ffn2_dw", u4, dg2, dup2, a2, dh4_b, tm)
    dh3, dh3_b, g["ffn2_norm"] = _ffn_dx("ffn2_dx", dg2, dup2, w_f2, h3, sp["ffn2_norm"], dh4, tm)

    d_o = _plain_mm("attn_out_dx", dh3_b, w_xo, NT, BF16, tm)
    dw_xo = _dw_mm("attn_out_dw", o, dh3_b, tm)
    dq, dkv = _attn_bwd(q, kv, d_o, tm)
    dw_q = _dw_mm("attn_q_dw", u3, dq, tm)
    dh2, dh2_b, g["xattn_norm"] = _mm_norm_bwd("attn_q_dx", dq, w_q, NT, h2, sp["xattn_norm"], dh3, tm)
    dw_kv = _mm("attn_kv_dw", [(mem_n, BS((n_mem, D_MODEL), lambda s: (0, 0)), dkv, BS((n_mem, 512), lambda s: (0, s)), TN)],
                grid=(N_SHARD,), out_shape=SDS((N_SHARD, D_MODEL, 512), BF16), out_spec=BS((None, D_MODEL, 512), lambda s: (s, 0, 0)))
    dmem_n = _mm("attn_kv_dx", [(dkv, BS((n_mem, 512), lambda s: (0, s)), w_d, BS((None, None, D_MODEL, 512), lambda s: (s, 0, 0, 0)), NT)],
                 grid=(N_SHARD,), red_axis=0, out_shape=SDS((n_mem, D_MODEL), F32), out_spec=BS((n_mem, D_MODEL), lambda s: (0, 0)))
    _, _, g["mem_norm"] = _rmsnorm_bwd("norm_mem_bwd", mem, sp["mem_norm"], dmem_n, None, n_mem)

    square = (N_SHARD, D_MODEL // N_SHARD, D_MODEL)
    sharded = (N_SHARD, FF_SH, D_MODEL)
    early = [a.reshape(sharded) for a in dw_f2] + [dw_xo.reshape(square), dw_q.reshape(square), dw_kv]
    swapping = reducer.swap_start("a1", early) if reducer is not None else []
    dmerged = _plain_mm("mix_out_dx", dh2_b, w_mo, NT, BF16, tm, after=swapping)
    dw_mo = _dw_mm("mix_out_dw", merged, dh2_b, tm)
    d_gp, d_gs, dzp, dzv, dzg = _mixer_merge_bwd(ms, yssm, w_e, proj, dmerged, tm)
    dw_e = _mixer_dw(ms, yssm, dzp, dzv, dzg, tm)
    d_ms, d_yt, d_yt_b = _mixer_dx(dzp, dzv, dzg, w_e, y_total, tm)
    dp, d_scale, d_pw = _pool_bwd(d_ms, mixed, pooled, sp["pool_w"][0], sp["pool_scale"])
    g["pool_scale"] = d_scale
    g["pool_w"] = d_pw[None]

    du_dirs, lams = [], []
    for dr in range(2):
        lam, du = _ssm_scan(f"ssm_scan_bwd{dr}", d_yt_b, w_out_s_t[dr], a_r[dr], -a_i[dr], w_in_s_t[dr], reverse=(dr == 0))
        du_dirs.append(du)
        lams.append(lam)
    ds, g["ssm_d"] = _ssm_ds(proj, d_yt, du_dirs[0], du_dirs[1], sp["ssm_d"], tm)

    d_proj = [dp, ds, d_gp, d_gs]
    dw_in_t = _mix_in_dw(d_proj, u2, tm)
    dh1, dh1_b, g["mix_norm"] = _mm_norm_bwd("mix_in_dx", d_proj, w_in_t, NN, h1, sp["mix_norm"], dh2, tm)

    early += [dw_mo.reshape(square), dw_e, dw_in_t.reshape(sharded)]
    g["final_norm"] = g["final_norm"].reshape(D_MODEL)

    travelling = reducer.start("a", early[6:], swapped=["a1"], after=list(g.values())) if reducer is not None else []
    d_abr, d_abi, d_cm, d_bm = [], [], [], []
    for dr in range(2):
        da_r, da_i, d_win, d_woutt = _ssm_param_grads(f"ssm_param_grads{dr}", lams[dr], states[dr], s_in, d_yt_b,
                                                      reverse=(dr == 1), after=travelling)
        d_abr.append(da_r)
        d_abi.append(da_i)
        d_bm.append(d_win)
        d_cm.append(d_woutt)

    d_ar, d_ai, d_ldt, d_br, d_bi, d_cr, d_ci = _ssm_prep_bwd(*ssm_a, *ssm_b, d_abr, d_abi, d_bm, d_cm)
    per_group = (2 * SSM_GROUPS, SSM_GROUP * SSM_STATE)
    g["ssm_a_re"] = d_ar.reshape(2 * SSM_GROUPS, SSM_GROUP, SSM_STATE).sum(axis=1).reshape(sp["ssm_a_re"].shape)
    g["ssm_a_im"] = d_ai.reshape(2 * SSM_GROUPS, SSM_GROUP, SSM_STATE).sum(axis=1).reshape(sp["ssm_a_im"].shape)
    g["ssm_log_dt"] = d_ldt.reshape(per_group).sum(axis=1).reshape(sp["ssm_log_dt"].shape)
    g["ssm_b_re"] = d_br.reshape(sp["ssm_b_re"].shape)
    g["ssm_b_im"] = d_bi.reshape(sp["ssm_b_im"].shape)
    g["ssm_c_re"] = d_cr.reshape(sp["ssm_c_re"].shape)
    g["ssm_c_im"] = d_ci.reshape(sp["ssm_c_im"].shape)
    if reducer is not None:
        travelling = travelling + [d_ar, d_br, d_cr]
    dg1, dup1 = _ffn_bwd_act("ffn1_bwd_act", dh1_b, w_f1, g1, up1, tm, after=travelling)
    dw_f1 = [a.reshape(sharded) for a in _ffn_dw("ffn1_dw", u1, dg1, dup1, a1, dh1_b, tm)]
    if reducer is not None:
        travelling = reducer.start("b", dw_f1, after=reducer.finish("a", dw_f1[:1]))
        travelling = travelling + reducer.join_start("a", after=travelling)
    grad_x, _, g["ffn1_norm"] = _ffn_dx("ffn1_dx", dg1, dup1, w_f1, x, sp["ffn1_norm"], dh1, tm, after=travelling)
    if reducer is not None:
        reducer.join_finish("a", [grad_x])
    return loss, grad_x, early + dw_f1, g


def _mesh_place():
    x, y, c = lax.axis_index("x"), lax.axis_index("y"), lax.axis_index("c")
    chips = [(1 - x, y), (x, 1 - y), (1 - x, 1 - y)]
    return x, y, c, chips


def _remote(src, dst, send_sems, recv_sems, k, to):
    return pltpu.make_async_remote_copy(src_ref=src, dst_ref=dst, send_sem=send_sems.at[k], recv_sem=recv_sems.at[k],
                                        device_id=to, device_id_type=MESH)


def _sibling_swap_halves(tag, grads, after=()):
    n = len(grads)
    after_ops, after_specs = _after_operands(after)

    def body(*refs):
        ins, outs = refs[:n], refs[n + len(after_ops):2 * n + len(after_ops)]
        send_sems, recv_sems = refs[2 * n + len(after_ops):]
        x, y, c, _ = _mesh_place()
        sibling = (x, y, 1 - c)
        copies = []
        for k in range(n):
            half = grads[k].shape[1] // 2
            theirs = pl.ds(pl.multiple_of((1 - c) * half, 16), half)
            cp = _remote(ins[k].at[:, theirs, :], outs[k], send_sems, recv_sems, k, sibling)
            cp.start()
            copies.append(cp)
        for cp in copies:
            cp.wait_recv()
        for cp in copies:
            cp.wait_send()

    hbm = BS(memory_space=pl.ANY)
    return pl.pallas_call(
        body, out_shape=tuple(SDS((g.shape[0], g.shape[1] // 2, g.shape[2]), g.dtype) for g in grads),
        in_specs=[hbm] * n + after_specs, out_specs=(hbm,) * n,
        scratch_shapes=[pltpu.SemaphoreType.DMA((n,)), pltpu.SemaphoreType.DMA((n,))],
        name="reduce_sibling_send_" + tag, compiler_params=_params())(*grads, *after_ops)


def _row_tile(rows, cap=512):
    return max(r for r in range(16, cap + 1, 16) if rows % r == 0)


REDUCE_STEPS = 2


def _chip_presum(tag, grads, gots, c_idx):
    n = len(grads)
    halves = [g.shape[1] // 2 for g in grads]
    tiles = [(h // REDUCE_STEPS, g.shape[2]) for h, g in zip(halves, grads)]

    def body(c_ref, *refs):
        for k in range(n):
            refs[2 * n + k][...] = (refs[k][...].astype(F32) + refs[n + k][...].astype(F32)).astype(BF16)

    mine = [BS((None, None) + tile, lambda s, i, c_ref: (s, c_ref[0], i, 0)) for tile in tiles]
    plain = [BS((None,) + tile, lambda s, i, c_ref: (s, i, 0)) for tile in tiles]
    return list(pl.pallas_call(
        body, out_shape=tuple(SDS((g.shape[0], h, g.shape[2]), BF16) for g, h in zip(grads, halves)),
        grid_spec=pltpu.PrefetchScalarGridSpec(num_scalar_prefetch=1, grid=(N_SHARD, REDUCE_STEPS),
                                               in_specs=mine + plain, out_specs=plain),
        name="reduce_presum_" + tag, compiler_params=_params("parallel", "parallel"))(
            c_idx, *[g.reshape(g.shape[0], 2, h, g.shape[2]) for g, h in zip(grads, halves)], *gots))


HBM_SPEC = BS(memory_space=pltpu.HBM)
SEM_SPEC = BS(memory_space=pltpu.SEMAPHORE)
DATAFLOW = pltpu.SideEffectType.DATAFLOW_SIDE_EFFECTING


def _chip_exchange_copies(parts, lands, send_sems, recv_sems):
    _, _, c, chips = _mesh_place()
    return [_remote(parts[k].at[2 * px + py], lands[k].at[j], send_sems, recv_sems, 3 * k + j, (px, py, c))
            for k in range(len(parts)) for j, (px, py) in enumerate(chips)]


def _gather_copies(shards, lands, send_sems, recv_sems):
    x, y, c, chips = _mesh_place()
    return [_remote(shards[k], lands[k].at[2 * x + y], send_sems, recv_sems, 3 * k + j, (px, py, c))
            for k in range(len(shards)) for j, (px, py) in enumerate(chips)]


def _gather_half_copies(shards, lands, send_sems, recv_sems):
    x, y, c, chips = _mesh_place()
    out = []
    for k in range(len(shards)):
        half = shards[k].shape[0] // 2
        mine = pl.ds(pl.multiple_of(c * half, 16), half)
        for j, (px, py) in enumerate(chips):
            out.append(_remote(shards[k].at[mine, :], lands[k].at[2 * x + y, mine, :], send_sems, recv_sems,
                               3 * k + j, (px, py, c)))
    return out


def _sibling_fill(tag, lands):
    n = len(lands)

    def body(*refs):
        outs = refs[n:2 * n]
        send_sems, recv_sems = refs[2 * n:]
        x, y, c, chips = _mesh_place()
        copies = []
        for k in range(n):
            half = lands[k].shape[1] // 2
            mine = pl.ds(pl.multiple_of(c * half, 16), half)
            for j, (px, py) in enumerate(chips):
                blk = outs[k].at[2 * px + py, mine, :]
                copies.append(_remote(blk, blk, send_sems, recv_sems, 3 * k + j, (x, y, 1 - c)))
        for cp in copies:
            cp.start()
        for cp in copies:
            cp.wait_recv()
        for cp in copies:
            cp.wait_send()

    hbm = BS(memory_space=pl.ANY)
    return list(pl.pallas_call(
        body, out_shape=tuple(SDS(a.shape, a.dtype) for a in lands),
        in_specs=[hbm] * n, out_specs=(hbm,) * n, input_output_aliases={k: k for k in range(n)},
        scratch_shapes=[pltpu.SemaphoreType.DMA((3 * n,)), pltpu.SemaphoreType.DMA((3 * n,))],
        name="gather_fill_" + tag, compiler_params=_params())(*lands))


def _swap_copies(grads, lands, send_sems, recv_sems):
    x, y, c, _ = _mesh_place()
    out = []
    for k in range(len(grads)):
        half = grads[k].shape[1] // 2
        theirs = pl.ds(pl.multiple_of((1 - c) * half, 16), half)
        out.append(_remote(grads[k].at[:, theirs, :], lands[k], send_sems, recv_sems, k, (x, y, 1 - c)))
    return out


def _join_copies(fulls, same, send_sems, recv_sems):
    x, y, c, _ = _mesh_place()
    out = []
    for k in range(len(fulls)):
        half = fulls[k].shape[0] // 2
        mine = fulls[k].at[pl.ds(pl.multiple_of(c * half, 8), half), :]
        out.append(_remote(mine, mine, send_sems, recv_sems, k, (x, y, 1 - c)))
    return out


def _everyone_copies(packs, lands, send_sems, recv_sems):
    x, y, c, _ = _mesh_place()
    out = []
    for k in range(len(packs)):
        for j in range(N_DEV - 1):
            bx, by, bc = (j + 1) >> 2 & 1, (j + 1) >> 1 & 1, (j + 1) & 1
            peer = (x ^ bx, y ^ by, c ^ bc)
            out.append(_remote(packs[k], lands[k].at[4 * x + 2 * y + c], send_sems, recv_sems, (N_DEV - 1) * k + j, peer))
    return out


def _split_start(name, copies, sources, land_shapes, after=(), fanout=3):
    n = len(sources)
    n_land = len(land_shapes)
    m = n + n_land
    n_sems = fanout * n
    after_ops, after_specs = _after_operands(after)

    def body(*refs):
        ins = refs[:n]
        lands = refs[n:m] if n_land else ins
        send_sems, recv_sems = refs[m + len(after_ops)], refs[m + len(after_ops) + 1]
        token = refs[-1]
        for cp in copies(ins, lands, send_sems, recv_sems):
            cp.start()
        token[...] = jnp.zeros_like(token)

    lands = [pltpu.with_memory_space_constraint(lax.empty(s, d), pltpu.HBM) for s, d in land_shapes]
    sources = [pltpu.with_memory_space_constraint(p, pltpu.HBM) for p in sources]
    thru = [pltpu.HBM(a.shape, a.dtype) for a in sources + lands]
    out = pl.pallas_call(
        body, name=name,
        out_shape=(pltpu.SemaphoreType.DMA((n_sems,)), pltpu.SemaphoreType.DMA((n_sems,)), *thru, SDS((8, 128), F32)),
        in_specs=[HBM_SPEC] * m + after_specs,
        out_specs=(SEM_SPEC, SEM_SPEC, *[HBM_SPEC] * m, BS(memory_space=pltpu.VMEM)),
        input_output_aliases={i: 2 + i for i in range(m)},
        compiler_params=pltpu.CompilerParams(has_side_effects=DATAFLOW))(*sources, *lands, *after_ops)
    return out[0], out[1], list(out[2:2 + n]), list(out[2 + n:2 + m]), out[-1]


def _split_wait(name, copies, send_sems, recv_sems, sources, lands, after):
    n = len(sources)
    m = n + len(lands)
    after_ops, after_specs = _after_operands(after)

    def body(*refs):
        ins = refs[:n]
        zones = refs[n:m] if m > n else ins
        for cp in copies(ins, zones, refs[m], refs[m + 1]):
            cp.wait_send()
            cp.wait_recv()

    out = pl.pallas_call(
        body, name=name,
        out_shape=tuple(pltpu.HBM(a.shape, a.dtype) for a in sources + lands),
        in_specs=[HBM_SPEC] * m + [SEM_SPEC, SEM_SPEC] + after_specs, out_specs=(HBM_SPEC,) * m,
        input_output_aliases={i: i for i in range(m)},
        compiler_params=pltpu.CompilerParams(has_side_effects=DATAFLOW))(*sources, *lands, send_sems, recv_sems, *after_ops)
    return list(out[:n]), list(out[n:])


class _WeightGatherer:
    def __init__(self, shards):
        self.shards, self.open = shards, {}
        self.me = 2 * lax.axis_index("x") + lax.axis_index("y")

    HALVED = ("f1a", "win", "mix")

    def start(self, tag, after=()):
        shapes = [((N_SHARD,) + s.shape, s.dtype) for s in self.shards[tag]]
        copies = _gather_half_copies if tag in self.HALVED else _gather_copies
        self.open[tag] = _split_start("gather_start_" + tag, copies, self.shards[tag], shapes, after)
        return [self.open[tag][-1]]

    def finish(self, tag, after):
        send_sems, recv_sems, shards, lands, _ = self.open.pop(tag)
        copies = _gather_half_copies if tag in self.HALVED else _gather_copies
        shards, lands = _split_wait("gather_wait_" + tag, copies, send_sems, recv_sems, shards, lands, after)
        if tag in self.HALVED:
            lands = _sibling_fill(tag, lands)
        return [lax.dynamic_update_slice(zone, s[None], (self.me, 0, 0)) for zone, s in zip(lands, shards)]


class _GradReducer:
    def __init__(self):
        self.c_idx = lax.axis_index("c").astype(jnp.int32).reshape(1)
        self.place = jnp.stack([2 * lax.axis_index("x") + lax.axis_index("y"), lax.axis_index("c")]).astype(jnp.int32)
        self.swaps, self.open, self.landed, self.joins, self.reduced = {}, {}, {}, {}, []

    def swap_start(self, tag, grads, after=()):
        shapes = [((g.shape[0], g.shape[1] // 2, g.shape[2]), g.dtype) for g in grads]
        self.swaps[tag] = _split_start("reduce_swap_start_" + tag, _swap_copies, grads, shapes, after, fanout=1)
        return [self.swaps[tag][-1]]

    def start(self, tag, grads, after=(), swapped=()):
        pairs = []
        for s in swapped:
            send_sems, recv_sems, early, lands, _ = self.swaps.pop(s)
            pairs += zip(*_split_wait("reduce_swap_wait_" + s, _swap_copies, send_sems, recv_sems, early, lands, grads[-1:]))
        pairs += zip(grads, _sibling_swap_halves(tag, grads, after))
        parts = _chip_presum(tag, [g for g, _ in pairs], [s for _, s in pairs], self.c_idx)
        shapes = [((3,) + p.shape[1:], p.dtype) for p in parts]
        self.open[tag] = _split_start("reduce_exchange_start_" + tag, _chip_exchange_copies, parts, shapes)
        return [self.open[tag][-1]]

    def finish(self, tag, after):
        send_sems, recv_sems, parts, lands, _ = self.open.pop(tag)
        self.landed[tag] = _split_wait("reduce_exchange_wait_" + tag, _chip_exchange_copies, send_sems, recv_sems, parts, lands, after)
        return self.landed[tag][1][:1]

    def _sums(self, tag, after=()):
        parts, landed = self.landed.pop(tag)
        return _chip_sum(tag, parts, landed, self.place, after)

    def join_start(self, tag, after=()):
        self.joins[tag] = _split_start("reduce_join_start_" + tag, _join_copies, self._sums(tag, after), [], fanout=1)
        return [self.joins[tag][-1]]

    def join_finish(self, tag, after):
        send_sems, recv_sems, fulls, _, _ = self.joins.pop(tag)
        self.reduced += _split_wait("reduce_join_wait_" + tag, _join_copies, send_sems, recv_sems, fulls, [], after)[0]

    def join(self, tag, after=()):
        self.reduced += _sibling_join_halves(self._sums(tag), after)


def _chip_sum(tag, parts, gots, place, after=()):
    n = len(parts)
    tiles = [(p.shape[1] // REDUCE_STEPS, p.shape[2]) for p in parts]
    after_ops, after_specs = _after_operands(after)

    def body(place_ref, *refs):
        outs = refs[2 * n + len(after_ops):]
        for k in range(n):
            acc = refs[k][...].astype(F32)
            for j in range(3):
                acc = acc + refs[n + k][j].astype(F32)
            outs[k][...] = acc

    return list(pl.pallas_call(
        body, out_shape=tuple(SDS((2 * p.shape[1], p.shape[2]), F32) for p in parts),
        grid_spec=pltpu.PrefetchScalarGridSpec(
            num_scalar_prefetch=1, grid=(REDUCE_STEPS,),
            in_specs=[BS((None,) + tile, lambda i, place_ref: (place_ref[0], i, 0)) for tile in tiles]
            + [BS((3,) + tile, lambda i, place_ref: (0, i, 0)) for tile in tiles] + after_specs,
            out_specs=[BS(tile, lambda i, place_ref: (place_ref[1] * REDUCE_STEPS + i, 0)) for tile in tiles]),
        name="reduce_sum_" + tag, compiler_params=_params("parallel"))(place, *parts, *gots, *after_ops))


def _sibling_join_halves(fulls, after=()):
    n = len(fulls)
    after_ops, after_specs = _after_operands(after)

    def body(*refs):
        outs = refs[n + len(after_ops):2 * n + len(after_ops)]
        send_sems, recv_sems = refs[2 * n + len(after_ops):]
        copies = _join_copies(outs, outs, send_sems, recv_sems)
        for cp in copies:
            cp.start()
        for cp in copies:
            cp.wait_recv()
        for cp in copies:
            cp.wait_send()

    hbm = BS(memory_space=pl.ANY)
    return list(pl.pallas_call(
        body, out_shape=tuple(SDS(f.shape, f.dtype) for f in fulls),
        in_specs=[hbm] * n + after_specs, out_specs=(hbm,) * n, input_output_aliases={k: k for k in range(n)},
        scratch_shapes=[pltpu.SemaphoreType.DMA((n,)), pltpu.SemaphoreType.DMA((n,))],
        name="reduce_sibling_join", compiler_params=_params())(*fulls, *after_ops))


N_DEV = 8


def _sum_devices(packs):
    _, rows, lanes = packs.shape

    def body(p_ref, o_ref):
        acc = p_ref[0]
        for dev in range(1, N_DEV):
            acc = acc + p_ref[dev]
        o_ref[...] = acc

    vm = BS(memory_space=pltpu.VMEM)
    return pl.pallas_call(body, out_shape=SDS((rows, lanes), F32), in_specs=[vm], out_specs=vm,
                          name="small_sum", compiler_params=_params())(packs)


def _adamw_refs(w_ref, g_ref, m_ref, v_ref, go_ref, d_ref, mo_ref, vo_ref):
    bc1 = 1.0 - ADAM_B1 ** ADAM_STEP
    bc2 = 1.0 - ADAM_B2 ** ADAM_STEP
    g = g_ref[...]
    m_new = ADAM_B1 * m_ref[...] + (1.0 - ADAM_B1) * g
    v_new = ADAM_B2 * v_ref[...] + (1.0 - ADAM_B2) * (g * g)
    go_ref[...] = g
    mo_ref[...] = m_new
    vo_ref[...] = v_new
    d_ref[...] = -ADAM_LR * ((m_new / bc1) / (jnp.sqrt(v_new / bc2) + ADAM_EPS) + ADAM_WD * w_ref[...])


def _adamw_small(ws, gs, ms, vs):
    n = len(ws)

    def body(*refs):
        for k in range(n):
            _adamw_refs(*[refs[j * n + k] for j in range(4)], *refs[4 * n + 4 * k:4 * n + 4 * k + 4])

    vm = BS(memory_space=pltpu.VMEM)
    outs = pl.pallas_call(
        body, out_shape=tuple(SDS(a.shape, F32) for a in ws for _ in range(4)), in_specs=[vm] * (4 * n),
        out_specs=(vm,) * (4 * n), name="adamw_small", compiler_params=_params())(*ws, *gs, *ms, *vs)
    return [outs[4 * k:4 * k + 4] for k in range(n)]


def _adamw(name, w, grad, row0, m, v, after=()):
    rows, cols = w.shape
    tr = rows if rows < 16 else _row_tile(rows, 352)
    after_ops, after_specs = _after_operands(after)

    def body(w_ref, g_ref, m_ref, v_ref, *rest):
        _adamw_refs(w_ref, g_ref, m_ref, v_ref, *rest[len(after_ops):])

    blk = BS((tr, cols), lambda i: (i, 0))
    shape = SDS((rows, cols), F32)
    return pl.pallas_call(
        body, out_shape=(shape,) * 4, grid=(rows // tr,),
        in_specs=[blk, BS((tr, cols), lambda i: (row0 // tr + i, 0)), blk, blk] + after_specs, out_specs=(blk,) * 4,
        name=name, compiler_params=_params("parallel"))(w, grad, m, v, *after_ops)


SMALL_LANES = 128


def _pack_small(parts):
    flat = jnp.concatenate([jnp.ravel(p) for p in parts])
    rows = -(-flat.shape[0] // (64 * SMALL_LANES)) * 64
    return jnp.pad(flat, (0, rows * SMALL_LANES - flat.shape[0])).reshape(rows, SMALL_LANES)


def _unpack_small(packed, like):
    flat = jnp.ravel(packed)
    out, at = [], 0
    for p in like:
        out.append(flat[at:at + p.size].reshape(p.shape))
        at += p.size
    return out


def kernel(x, mem, ffn1_norm, ffn1_w_gate, ffn1_w_up, ffn1_w_down, mix_norm, w_in, pool_w, pool_scale, w_pool_proj, ssm_a_re, ssm_a_im, ssm_log_dt, ssm_b_re, ssm_b_im, ssm_c_re, ssm_c_im, ssm_d, w_glu_val, w_glu_gate, w_mix_out, xattn_norm, mem_norm, w_q, w_kv, w_xo, ffn2_norm, ffn2_w_gate, ffn2_w_up, ffn2_w_down, final_norm, loss_target, m_ffn1_norm, m_ffn1_w_gate, m_ffn1_w_up, m_ffn1_w_down, m_mix_norm, m_w_in, m_pool_w, m_pool_scale, m_w_pool_proj, m_ssm_a_re, m_ssm_a_im, m_ssm_log_dt, m_ssm_b_re, m_ssm_b_im, m_ssm_c_re, m_ssm_c_im, m_ssm_d, m_w_glu_val, m_w_glu_gate, m_w_mix_out, m_xattn_norm, m_mem_norm, m_w_q, m_w_kv, m_w_xo, m_ffn2_norm, m_ffn2_w_gate, m_ffn2_w_up, m_ffn2_w_down, m_final_norm, v_ffn1_norm, v_ffn1_w_gate, v_ffn1_w_up, v_ffn1_w_down, v_mix_norm, v_w_in, v_pool_w, v_pool_scale, v_w_pool_proj, v_ssm_a_re, v_ssm_a_im, v_ssm_log_dt, v_ssm_b_re, v_ssm_b_im, v_ssm_c_re, v_ssm_c_im, v_ssm_d, v_w_glu_val, v_w_glu_gate, v_w_mix_out, v_xattn_norm, v_mem_norm, v_w_q, v_w_kv, v_w_xo, v_ffn2_norm, v_ffn2_w_gate, v_ffn2_w_up, v_ffn2_w_down, v_final_norm):
    given = dict(locals())
    w = {n: given[n] for n in WEIGHTS}
    m = {n: given["m_" + n] for n in WEIGHTS}
    v = {n: given["v_" + n] for n in WEIGHTS}

    def shard_view(a, n):
        return a[0].T if n in TRANSPOSED else a[0]

    def shard_unview(a, n):
        return (a.T if n in TRANSPOSED else a)[None]

    shards = {tag: [jnp.concatenate([shard_view(w[n], n).astype(BF16) for n in grp], axis=0) for grp in arrays]
              for tag, arrays in GATHER_PHASES.items()}
    reducer = _GradReducer()
    ws, ms, vs = ({n: _small_view(a[n], n) for n in SMALL} for a in (w, m, v))
    loss_part, grad_x, _, small = _device_step(x[0], mem[0], loss_target[0], _WeightGatherer(shards), ws, reducer)

    small_like = [ws[n] for n in SMALL] + [loss_part[0, :1]]
    pack = _pack_small([small[n] for n in SMALL] + [loss_part[0, :1]])
    everyone = _split_start("small_start", _everyone_copies, [pack], [((N_DEV,) + pack.shape, F32)], fanout=N_DEV - 1)

    grads, delta, new_m, new_v = {}, {}, {}, {}
    big_done = []

    def update(groups, reduced):
        for grp, red in zip(groups, reduced):
            row0 = 0
            for n in grp:
                w_n = shard_view(w[n], n)
                outs = _adamw("adamw_" + n, w_n, red, row0, shard_view(m[n], n), shard_view(v[n], n), after=everyone[-1:])
                grads[n], delta[n], new_m[n], new_v[n] = (shard_unview(o, n) for o in outs)
                big_done.append(outs[1])
                row0 += w_n.shape[0]

    n_a = len(reducer.reduced)
    update(REDUCE_GROUPS[:n_a], reducer.reduced)
    reducer.finish("b", list(big_done))
    reducer.join("b")
    update(REDUCE_GROUPS[n_a:], reducer.reduced[n_a:])

    send_sems, recv_sems, packs, landed, _ = everyone
    packs, landed = _split_wait("small_wait", _everyone_copies, send_sems, recv_sems, packs, landed, big_done)
    mine = 4 * lax.axis_index("x") + 2 * lax.axis_index("y") + lax.axis_index("c")
    summed = _sum_devices(lax.dynamic_update_slice(landed[0], packs[0][None], (mine, 0, 0)))
    g_small = dict(zip(SMALL + ("loss",), _unpack_small(summed, small_like)))
    loss = g_small.pop("loss").reshape(())
    def two_d(a):
        return a.reshape(-1, a.shape[-1])

    updated = _adamw_small(*([two_d(a[n]) for n in SMALL] for a in (ws, g_small, ms, vs)))
    for n, outs in zip(SMALL, updated):
        grads[n], delta[n], new_m[n], new_v[n] = (_small_view(o.reshape(ws[n].shape), n) for o in outs)

    return (loss, grad_x[None], *[grads[n] for n in WEIGHTS], *[delta[n] for n in WEIGHTS],
            *[new_m[n] for n in WEIGHTS], *[new_v[n] for n in WEIGHTS])
```

```python
import functools
import math

import jax
import jax.numpy as jnp
from jax import lax
from jax.experimental import pallas as pl
from jax.experimental.pallas import tpu as pltpu

F32 = jnp.float32
BF16 = jnp.bfloat16
SDS = jax.ShapeDtypeStruct
BS = pl.BlockSpec
MESH = pl.DeviceIdType.MESH

D_MODEL = 1024
D_FF = 2816
N_SHARD = 4
FF_SH = D_FF // N_SHARD
D_POOL = 512
POOL_WINDOWS = (2, 4, 8, 16)
POOL_GROUP = 128
D_SSM = 256
SSM_GROUPS = 16
SSM_GROUP = 16
SSM_STATE = 64
SSM_CH = SSM_GROUPS * SSM_STATE
N_HEADS = 4
HEAD_DIM = 256
EPS = 1e-6
ADAM_LR, ADAM_B1, ADAM_B2, ADAM_EPS, ADAM_WD, ADAM_STEP = 0.001, 0.9, 0.999, 1e-08, 0.01, 10

VMEM_LIMIT_V7X = 58 * 1024 * 1024
TM = 512

NN = (((1,), (0,)), ((), ()))
NT = (((1,), (1,)), ((), ()))
TN = (((0,), (0,)), ((), ()))


def _params(*sem):
    return pltpu.CompilerParams(dimension_semantics=sem if sem else None, vmem_limit_bytes=VMEM_LIMIT_V7X)


def _dot(a, b, dims=NN):
    return lax.dot_general(a.astype(BF16), b.astype(BF16), dims, preferred_element_type=F32)


def _sigmoid(v):
    return pl.reciprocal(1.0 + jnp.exp(-v), approx=True)


def _block_dims(spec):
    return tuple(d for d in spec.block_shape if d is not None)


def _after_operands(after):
    return list(after), [BS(memory_space=pl.ANY)] * len(after)


def _mm(name, pairs, *, grid, out_shape, out_spec, red_axis=None, extras=(), epilogue=None, after=()):
    n_pairs, n_extra = len(pairs), len(extras)
    n_red = grid[red_axis] if red_axis is not None else 1
    dims = [p[4] for p in pairs]

    def body(*refs):
        ab = refs[:2 * n_pairs]
        ex = refs[2 * n_pairs:2 * n_pairs + n_extra]
        o_ref = refs[2 * n_pairs + n_extra + len(after)]

        def partial():
            acc = None
            for p in range(n_pairs):
                t = _dot(ab[2 * p][...], ab[2 * p + 1][...], dims[p])
                acc = t if acc is None else acc + t
            return acc

        def finish(acc):
            res = epilogue(acc, *[e[...] for e in ex]) if epilogue is not None else acc
            o_ref[...] = res.astype(o_ref.dtype)

        if n_red == 1:
            finish(partial())
        else:
            acc_ref = refs[-1]
            k = pl.program_id(red_axis)

            @pl.when(k == 0)
            def _():
                acc_ref[...] = jnp.zeros_like(acc_ref)

            acc_ref[...] += partial()

            @pl.when(k == n_red - 1)
            def _():
                finish(acc_ref[...])

    operands, in_specs = [], []
    for a, a_spec, b, b_spec, _ in pairs:
        operands += [a, b]
        in_specs += [a_spec, b_spec]
    for e, e_spec in extras:
        operands.append(e)
        in_specs.append(e_spec)
    after_ops, after_specs = _after_operands(after)
    operands += after_ops
    in_specs += after_specs
    scratch = [pltpu.VMEM(_block_dims(out_spec), F32)] if n_red > 1 else []
    sem = tuple("arbitrary" if ax == red_axis else "parallel" for ax in range(len(grid)))
    return pl.pallas_call(body, out_shape=out_shape, grid=grid, in_specs=in_specs, out_specs=out_spec,
                          scratch_shapes=scratch, name=name, compiler_params=_params(*sem))(*operands)


def _rmsnorm(name, h, gain, tm, after=()):
    t, d = h.shape
    after_ops, after_specs = _after_operands(after)

    def body(h_ref, g_ref, *rest):
        u_ref = rest[-1]
        hv = h_ref[...]
        r = lax.rsqrt(jnp.mean(hv * hv, axis=-1, keepdims=True) + EPS)
        u_ref[...] = ((hv * r) * g_ref[...]).astype(u_ref.dtype)

    return pl.pallas_call(
        body, out_shape=SDS((t, d), BF16), grid=(t // tm,),
        in_specs=[BS((tm, d), lambda i: (i, 0)), BS((1, d), lambda i: (0, 0))] + after_specs,
        out_specs=BS((tm, d), lambda i: (i, 0)), name=name, compiler_params=_params("parallel"))(h, gain, *after_ops)


def _rmsnorm_bwd(name, h, gain, du, dh_in, tm):
    t, d = h.shape
    has_in = dh_in is not None

    def body(*refs):
        if has_in:
            h_ref, g_ref, du_ref, dhin_ref, dh_ref, dhb_ref, dg_ref = refs
        else:
            h_ref, g_ref, du_ref, dh_ref, dhb_ref, dg_ref = refs
        i = pl.program_id(0)
        hv = h_ref[...]
        r = lax.rsqrt(jnp.mean(hv * hv, axis=-1, keepdims=True) + EPS)
        n = hv * r
        duv = du_ref[...].astype(F32)
        dn = duv * g_ref[...]
        dh = r * (dn - n * jnp.mean(dn * n, axis=-1, keepdims=True))
        if has_in:
            dh = dhin_ref[...] + dh
        dh_ref[...] = dh
        dhb_ref[...] = dh.astype(BF16)

        @pl.when(i == 0)
        def _():
            dg_ref[...] = jnp.zeros_like(dg_ref)

        dg_ref[...] += jnp.sum(duv * n, axis=0, keepdims=True)

    row = BS((tm, d), lambda i: (i, 0))
    vec = BS((1, d), lambda i: (0, 0))
    operands = [h, gain, du] + ([dh_in] if has_in else [])
    in_specs = [row, vec, row] + ([row] if has_in else [])
    return pl.pallas_call(
        body, out_shape=(SDS((t, d), F32), SDS((t, d), BF16), SDS((1, d), F32)), grid=(t // tm,),
        in_specs=in_specs, out_specs=(row, row, vec), name=name, compiler_params=_params("arbitrary"))(*operands)


def _loss_head_tile(i, hv, g_ref, t_ref, loss_ref, dh_ref, dhb_ref, dg_ref):
    g = g_ref[...]
    r = lax.rsqrt(jnp.mean(hv * hv, axis=-1, keepdims=True) + EPS)
    n = hv * r
    err = n * g - t_ref[...]
    dy = err * (1.0 / hv.shape[-1])
    dn = dy * g
    dh = r * (dn - n * jnp.mean(dn * n, axis=-1, keepdims=True))
    dh_ref[...] = dh
    dhb_ref[...] = dh.astype(BF16)

    @pl.when(i == 0)
    def _():
        dg_ref[...] = jnp.zeros_like(dg_ref)
        loss_ref[...] = jnp.zeros_like(loss_ref)

    dg_ref[...] += jnp.sum(dy * n, axis=0, keepdims=True)
    part = 0.5 * jnp.sum(jnp.mean(err * err, axis=-1, keepdims=True), axis=0, keepdims=True)
    loss_ref[...] += jnp.broadcast_to(part, loss_ref.shape)


def _norm_tile(h, g_ref, u_ref):
    r = lax.rsqrt(jnp.mean(h * h, axis=-1, keepdims=True) + EPS)
    u_ref[...] = ((h * r) * g_ref[...]).astype(u_ref.dtype)


FFN_BLOCK = D_FF // 2


def _ffn_up(name, u, w_f, tm, after=()):
    t, d = u.shape
    after_ops, after_specs = _after_operands(after)

    def body(u_ref, wg_ref, wu_ref, *rest):
        pg_ref, pu_ref, a_ref = rest[len(after_ops):]
        uv = u_ref[...]
        for lo in range(0, D_FF, FFN_BLOCK):
            cols = slice(lo, lo + FFN_BLOCK)
            g = _dot(uv, wg_ref[cols, :], NT)
            up = _dot(uv, wu_ref[cols, :], NT)
            sg = _sigmoid(g)
            silu = g * sg
            a_ref[:, cols] = (silu * up).astype(BF16)
            pu_ref[:, cols] = (0.5 * silu).astype(BF16)
            pg_ref[:, cols] = (0.5 * sg * (1.0 + g * (1.0 - sg)) * up).astype(BF16)

    hid = BS((tm, D_FF), lambda i: (i, 0))
    shape = SDS((t, D_FF), BF16)
    whole = BS((D_FF, d), lambda i: (0, 0))
    return pl.pallas_call(
        body, out_shape=(shape, shape, shape), grid=(t // tm,),
        in_specs=[BS((tm, d), lambda i: (i, 0)), whole, whole] + after_specs,
        out_specs=(hid, hid, hid), name=name,
        compiler_params=_params("parallel"))(u, w_f["gate"], w_f["up"], *after_ops)


def _ffn_down(name, a, w_f, resid, tm, next_gain=None, head=None):
    t, d = resid.shape
    row = BS((tm, d), lambda i: (i, 0))
    vec = BS((1, d), lambda i: (0, 0))

    def body(a_ref, w_ref, res_ref, *rest):
        h = res_ref[...] + 0.5 * _dot(a_ref[...], w_ref[...])
        if head is not None:
            _loss_head_tile(pl.program_id(0), h, *rest)
        else:
            g_ref, h_ref, u_ref = rest
            h_ref[...] = h
            _norm_tile(h, g_ref, u_ref)

    if head is not None:
        extra, extra_specs = list(head), [vec, row]
        out_shape = (SDS((1, 128), F32), SDS((t, d), F32), SDS((t, d), BF16), SDS((1, d), F32))
        out_specs = (BS((1, 128), lambda i: (0, 0)), row, row, vec)
    else:
        extra, extra_specs = [next_gain], [vec]
        out_shape = (SDS((t, d), F32), SDS((t, d), BF16))
        out_specs = (row, row)
    return pl.pallas_call(
        body, out_shape=out_shape, grid=(t // tm,),
        in_specs=[BS((tm, D_FF), lambda i: (i, 0)), BS((D_FF, d), lambda i: (0, 0)), row] + extra_specs,
        out_specs=out_specs, name=name,
        compiler_params=_params("arbitrary" if head is not None else "parallel"))(a, w_f["down"], resid, *extra)


def _mix_in(u, w_t, tm, after=()):
    t, d = u.shape
    after_ops, after_specs = _after_operands(after)

    def body(u_ref, w_ref, *rest):
        o_ref, s_ref = rest[-2:]
        uv = u_ref[...]
        for lo in range(0, D_FF, FFN_BLOCK):
            o_ref[:, lo:lo + FFN_BLOCK] = _dot(uv, w_ref[lo:lo + FFN_BLOCK, :], NT)
        s_ref[...] = o_ref[:, D_POOL:D_POOL + D_SSM].astype(BF16)

    return pl.pallas_call(
        body, out_shape=(SDS((t, D_FF), F32), SDS((t, D_SSM), BF16)), grid=(t // tm,),
        in_specs=[BS((tm, d), lambda i: (i, 0)), BS((D_FF, d), lambda i: (0, 0))] + after_specs,
        out_specs=(BS((tm, D_FF), lambda i: (i, 0)), BS((tm, D_SSM), lambda i: (i, 0))), name="mix_in",
        compiler_params=_params("parallel"))(u, w_t, *after_ops)


def _mm_resid_norm(name, a, b, resid, next_gain, tm):
    t, d = resid.shape
    tm = min(2 * tm, t)

    def body(a_ref, b_ref, res_ref, g_ref, h_ref, u_ref):
        h = res_ref[...] + _dot(a_ref[...], b_ref[...])
        h_ref[...] = h
        _norm_tile(h, g_ref, u_ref)

    row = BS((tm, d), lambda i: (i, 0))
    return pl.pallas_call(
        body, out_shape=(SDS((t, d), F32), SDS((t, d), BF16)), grid=(t // tm,),
        in_specs=[BS((tm, a.shape[1]), lambda i: (i, 0)), BS(b.shape, lambda i: (0, 0)), row, BS((1, d), lambda i: (0, 0))],
        out_specs=(row, row), name=name, compiler_params=_params("parallel"))(a, b, resid, next_gain)


def _ffn_bwd_act(name, dh_b, w_f, pg, pu, tm, after=()):
    t, d = dh_b.shape
    after_ops, after_specs = _after_operands(after)

    def body(dh_ref, wd_ref, pg_ref, pu_ref, *rest):
        dg_ref, dup_ref = rest[len(after_ops):]
        dh = dh_ref[...]
        for lo in range(0, D_FF, FFN_BLOCK):
            cols = slice(lo, lo + FFN_BLOCK)
            da = _dot(dh, wd_ref[cols, :], NT)
            dg_ref[:, cols] = (da * pg_ref[:, cols].astype(F32)).astype(BF16)
            dup_ref[:, cols] = (da * pu_ref[:, cols].astype(F32)).astype(BF16)

    hid = BS((tm, D_FF), lambda i: (i, 0))
    shape = SDS((t, D_FF), BF16)
    return pl.pallas_call(
        body, out_shape=(shape, shape), grid=(t // tm,),
        in_specs=[BS((tm, d), lambda i: (i, 0)), BS((D_FF, d), lambda i: (0, 0)), hid, hid] + after_specs,
        out_specs=(hid, hid), name=name,
        compiler_params=_params("parallel"))(dh_b, w_f["down"], pg, pu, *after_ops)


def _ffn_dw(name, u, dg, dup, a, dh_b, tm):
    t, d = u.shape
    n_t = t // tm

    def body(u_ref, dg_ref, dup_ref, a_ref, dh_ref, og_ref, ou_ref, od_ref, acc):
        i = pl.program_id(1)

        @pl.when(i == 0)
        def _():
            acc[...] = jnp.zeros_like(acc)

        uv = u_ref[...]
        acc[0] += _dot(dg_ref[...], uv, TN)
        acc[1] += _dot(dup_ref[...], uv, TN)
        acc[2] += _dot(a_ref[...], dh_ref[...], TN)

        @pl.when(i == n_t - 1)
        def _():
            og_ref[...] = acc[0].astype(BF16)
            ou_ref[...] = acc[1].astype(BF16)
            od_ref[...] = (0.5 * acc[2]).astype(BF16)

    hid = BS((tm, FFN_BLOCK), lambda j, i: (i, j))
    row = BS((tm, d), lambda j, i: (i, 0))
    out = BS((FFN_BLOCK, d), lambda j, i: (j, 0))
    shape = SDS((D_FF, d), BF16)
    return pl.pallas_call(
        body, out_shape=(shape, shape, shape), grid=(D_FF // FFN_BLOCK, n_t),
        in_specs=[row, hid, hid, hid, row], out_specs=(out, out, out),
        scratch_shapes=[pltpu.VMEM((3, FFN_BLOCK, d), F32)],
        name=name, compiler_params=_params("parallel", "arbitrary"))(u, dg, dup, a, dh_b)


def _norm_bwd_tile(i, du, h_ref, g_ref, dhin_ref, dh_ref, dhb_ref, dg_ref):
    hv = h_ref[...]
    r = lax.rsqrt(jnp.mean(hv * hv, axis=-1, keepdims=True) + EPS)
    n = hv * r
    dn = du * g_ref[...]
    dh = dhin_ref[...] + r * (dn - n * jnp.mean(dn * n, axis=-1, keepdims=True))
    dh_ref[...] = dh
    dhb_ref[...] = dh.astype(BF16)

    @pl.when(i == 0)
    def _():
        dg_ref[...] = jnp.zeros_like(dg_ref)

    dg_ref[...] += jnp.sum(du * n, axis=0, keepdims=True)


def _norm_bwd_specs(tm):
    row = BS((tm, D_MODEL), lambda i: (i, 0))
    vec = BS((1, D_MODEL), lambda i: (0, 0))
    return [row, vec, row], (row, row, vec)


def _norm_bwd_shapes(t):
    return SDS((t, D_MODEL), F32), SDS((t, D_MODEL), BF16), SDS((1, D_MODEL), F32)


def _ffn_dx(name, dg, dup, w_f, h, gain, dh_in, tm, after=()):
    t = dg.shape[0]
    tm = tm // 2
    after_ops, after_specs = _after_operands(after)

    def body(dg_ref, dup_ref, wg_ref, wu_ref, h_ref, g_ref, dhin_ref, *rest):
        du = _dot(dg_ref[...], wg_ref[...]) + _dot(dup_ref[...], wu_ref[...])
        _norm_bwd_tile(pl.program_id(0), du, h_ref, g_ref, dhin_ref, *rest[len(after_ops):])

    hid = BS((tm, D_FF), lambda i: (i, 0))
    whole = BS((D_FF, D_MODEL), lambda i: (0, 0))
    norm_in, norm_out = _norm_bwd_specs(tm)
    return pl.pallas_call(
        body, out_shape=_norm_bwd_shapes(t), grid=(t // tm,),
        in_specs=[hid, hid, whole, whole] + norm_in + after_specs, out_specs=norm_out, name=name,
        compiler_params=_params("arbitrary"))(dg, dup, w_f["gate"], w_f["up"], h, gain, dh_in, *after_ops)


def _mm_norm_bwd(name, a, b, dims, h, gain, dh_in, tm):
    pieces = list(a) if isinstance(a, (list, tuple)) else [a]
    assert len(pieces) == 1 or dims == NN
    t = pieces[0].shape[0]
    widths = [p.shape[1] for p in pieces]
    row0 = [sum(widths[:j]) for j in range(len(pieces))]

    def body(*refs):
        a_refs, (b_ref, h_ref, g_ref, dhin_ref), outs = refs[:len(pieces)], refs[len(pieces):len(pieces) + 4], refs[len(pieces) + 4:]
        if len(pieces) == 1:
            du = _dot(a_refs[0][...], b_ref[...], dims)
        else:
            du = _dot(a_refs[0][...], b_ref[0:widths[0], :])
            for a_ref, r0, w in zip(a_refs[1:], row0[1:], widths[1:]):
                du = du + _dot(a_ref[...], b_ref[r0:r0 + w, :])
        _norm_bwd_tile(pl.program_id(0), du, h_ref, g_ref, dhin_ref, *outs)

    norm_in, norm_out = _norm_bwd_specs(tm)
    return pl.pallas_call(
        body, out_shape=_norm_bwd_shapes(t), grid=(t // tm,),
        in_specs=[BS((tm, w), lambda i: (i, 0)) for w in widths] + [BS(b.shape, lambda i: (0, 0))] + norm_in,
        out_specs=norm_out, name=name, compiler_params=_params("arbitrary"))(*pieces, b, h, gain, dh_in)


def _plain_mm(name, a, b, dims, out_dtype, tm, resid=None, after=()):
    t = a.shape[0]
    tm = min(2 * tm, t)
    n = b.shape[1] if dims == NN else b.shape[0]
    extras = [(resid, BS((tm, n), lambda i: (i, 0)))] if resid is not None else []
    epi = (lambda acc, res: res + acc) if resid is not None else None
    return _mm(name, [(a, BS((tm, a.shape[1]), lambda i: (i, 0)), b, BS(b.shape, lambda i: (0, 0)), dims)],
               grid=(t // tm,), out_shape=SDS((t, n), out_dtype), out_spec=BS((tm, n), lambda i: (i, 0)),
               extras=extras, epilogue=epi, after=after)


def _dw_mm(name, a, b, tm, out_dtype=BF16, after=()):
    t, k = a.shape
    n = b.shape[1]
    tm = min(2 * tm, t)
    return _mm(name, [(a, BS((tm, k), lambda i: (i, 0)), b, BS((tm, n), lambda i: (i, 0)), TN)],
               grid=(t // tm,), red_axis=0, out_shape=SDS((k, n), out_dtype), out_spec=BS((k, n), lambda i: (0, 0)),
               after=after)


def _mix_in_dw(pieces, u, tm):
    t, d = u.shape
    tm = min(2 * tm, t)
    n_steps = t // tm
    widths = [p.shape[1] for p in pieces]
    row0 = [sum(widths[:j]) for j in range(len(pieces))]
    assert sum(widths) == D_FF

    def body(*refs):
        p_refs, u_ref, o_ref, acc_ref = refs[:len(pieces)], refs[len(pieces)], refs[-2], refs[-1]
        k = pl.program_id(0)

        @pl.when(k == 0)
        def _():
            acc_ref[...] = jnp.zeros_like(acc_ref)

        uv = u_ref[...]
        for p_ref, r0, w in zip(p_refs, row0, widths):
            acc_ref[r0:r0 + w, :] += _dot(p_ref[...], uv, TN)

        @pl.when(k == n_steps - 1)
        def _():
            o_ref[...] = acc_ref[...].astype(BF16)

    return pl.pallas_call(
        body, out_shape=SDS((D_FF, d), BF16), grid=(n_steps,),
        in_specs=[BS((tm, w), lambda i: (i, 0)) for w in widths] + [BS((tm, d), lambda i: (i, 0))],
        out_specs=BS((D_FF, d), lambda i: (0, 0)), scratch_shapes=[pltpu.VMEM((D_FF, d), F32)],
        name="mix_in_dw", compiler_params=_params("arbitrary"))(*pieces, u)


POOL_CHUNK = 256
POOL_HALO = 8


def _window_sum(v, width, lead):
    n = v.shape[0]
    s = v
    k = 1
    while k < width:
        s = s + pltpu.roll(s, n - k, 0)
        k *= 2
    return pltpu.roll(s, lead, 0) if lead else s


def _pool_count(base, left, right, t, shape):
    pos = base + lax.broadcasted_iota(jnp.int32, shape, 0)
    lo = jnp.maximum(pos - left, 0)
    hi = jnp.minimum(pos + right + 1, t)
    return (hi - lo).astype(F32)


def _pool_fwd(proj, pool_w, pool_scale):
    t = proj.shape[0]
    c, h = POOL_CHUNK, POOL_HALO
    n_chunks = t // c

    def body(proj_hbm, pw_ref, sc_ref, pooled_ref, mixed_ref, ms_ref, pad_ref, sem):
        cp = pltpu.make_async_copy(proj_hbm.at[:, pl.ds(0, D_POOL)], pad_ref.at[pl.ds(h, t), :], sem)
        cp.start()
        pad_ref[pl.ds(0, h), :] = jnp.zeros((h, D_POOL), F32)
        pad_ref[pl.ds(t + h, h), :] = jnp.zeros((h, D_POOL), F32)
        cp.wait()
        for g, width in enumerate(POOL_WINDOWS):
            left = width // 2
            right = width - 1 - left
            cols = slice(g * POOL_GROUP, (g + 1) * POOL_GROUP)
            wmat = pw_ref[g].astype(BF16)
            scale = sc_ref[:, cols]

            def chunk(ci, carry, left=left, right=right, width=width, cols=cols, wmat=wmat, scale=scale):
                base = pl.multiple_of(ci * c, c)
                v = pad_ref[pl.ds(base, c + 2 * h), cols]
                win = _window_sum(v, width, left)[h:h + c]
                cnt = _pool_count(base, left, right, t, (c, POOL_GROUP))
                pooled = (win / cnt - v[h:h + c]).astype(BF16)
                mixed = _dot(pooled, wmat)
                pooled_ref[pl.ds(base, c), cols] = pooled
                mixed_ref[pl.ds(base, c), cols] = mixed.astype(BF16)
                ms_ref[pl.ds(base, c), cols] = (mixed * scale).astype(BF16)
                return carry

            lax.fori_loop(0, n_chunks, chunk, 0)

    vm = BS(memory_space=pltpu.VMEM)
    shape = SDS((t, D_POOL), BF16)
    return pl.pallas_call(
        body, out_shape=(shape, shape, shape),
        in_specs=[BS(memory_space=pl.ANY), vm, vm], out_specs=(vm, vm, vm),
        scratch_shapes=[pltpu.VMEM((t + 2 * h, D_POOL), F32), pltpu.SemaphoreType.DMA],
        name="pool_fwd", compiler_params=_params())(proj, pool_w, pool_scale)


def _pool_bwd(d_ms, mixed, pooled, pool_w, pool_scale):
    t = d_ms.shape[0]
    c, h = POOL_CHUNK, POOL_HALO
    n_chunks = t // c

    def body(dms_ref, mixed_ref, pooled_ref, pw_ref, sc_ref, dp_ref, dsc_ref, dpw_ref, pad_ref):
        pad_ref[pl.ds(0, h), :] = jnp.zeros((h, D_POOL), F32)
        pad_ref[pl.ds(t + h, h), :] = jnp.zeros((h, D_POOL), F32)
        for g, width in enumerate(POOL_WINDOWS):
            left = width // 2
            right = width - 1 - left
            cols = slice(g * POOL_GROUP, (g + 1) * POOL_GROUP)
            wmat = pw_ref[g].astype(BF16)
            scale = sc_ref[:, cols]

            def first(ci, carry, left=left, right=right, cols=cols, wmat=wmat, scale=scale):
                dsc, dpw = carry
                base = pl.multiple_of(ci * c, c)
                dms = dms_ref[pl.ds(base, c), cols].astype(F32)
                dsc = dsc + jnp.sum(dms * mixed_ref[pl.ds(base, c), cols].astype(F32), axis=0, keepdims=True)
                dmix = (dms * scale).astype(BF16)
                dpw = dpw + _dot(pooled_ref[pl.ds(base, c), cols], dmix, TN)
                dpooled = _dot(dmix, wmat, NT)
                cnt = _pool_count(base, left, right, t, (c, POOL_GROUP))
                pad_ref[pl.ds(base + h, c), cols] = dpooled / cnt
                return dsc, dpw

            dsc, dpw = lax.fori_loop(0, n_chunks, first,
                                     (jnp.zeros((1, POOL_GROUP), F32), jnp.zeros((POOL_GROUP, POOL_GROUP), F32)))
            dsc_ref[:, cols] = dsc
            dpw_ref[g] = dpw

            def second(ci, carry, left=left, right=right, width=width, cols=cols):
                base = pl.multiple_of(ci * c, c)
                v = pad_ref[pl.ds(base, c + 2 * h), cols]
                win = _window_sum(v, width, right)[h:h + c]
                cnt = _pool_count(base, left, right, t, (c, POOL_GROUP))
                dp_ref[pl.ds(base, c), cols] = (win - v[h:h + c] * cnt).astype(BF16)
                return carry

            lax.fori_loop(0, n_chunks, second, 0)

    vm = BS(memory_space=pltpu.VMEM)
    return pl.pallas_call(
        body, out_shape=(SDS((t, D_POOL), BF16), SDS((1, D_POOL), F32), SDS((4, POOL_GROUP, POOL_GROUP), F32)),
        in_specs=[vm] * 5, out_specs=(vm, vm, vm),
        scratch_shapes=[pltpu.VMEM((t + 2 * h, D_POOL), F32)],
        name="pool_bwd", compiler_params=_params())(d_ms, mixed, pooled, pool_w, pool_scale)


SSM_ROWS = 2 * SSM_GROUPS * SSM_GROUP
SSM_HALF = SSM_GROUPS * SSM_GROUP


def _ssm_zoh(a_r, a_i, ldt):
    dt = jnp.exp(ldt)
    mag = jnp.exp(dt * a_r)
    ang = dt * a_i
    cs, sn = jnp.cos(ang), jnp.sin(ang)
    abr, abi = mag * cs, mag * sn
    den = a_r * a_r + a_i * a_i
    nr = abr - 1.0
    qr = (nr * a_r + abi * a_i) / den
    qi = (abi * a_r - nr * a_i) / den
    return dt, mag, cs, sn, abr, abi, den, nr, qr, qi


def _ssm_group_mask():
    row = lax.broadcasted_iota(jnp.int32, (SSM_HALF, SSM_CH), 0)
    col = lax.broadcasted_iota(jnp.int32, (SSM_HALF, SSM_CH), 1)
    return (row // SSM_GROUP) == (col // SSM_STATE)


def _ssm_prep(a_r, a_i, ldt, b_r, b_i, c_r, c_i, after=()):
    after_ops, after_specs = _after_operands(after)

    def body(ar_ref, ai_ref, ldt_ref, br_ref, bi_ref, cr_ref, ci_ref, *rest):
        abr_ref, abi_ref, win_ref, wint_ref, woutt_ref, wout_ref = rest[len(after_ops):]
        *_, abr, abi, _, _, qr, qi = _ssm_zoh(ar_ref[...], ai_ref[...], ldt_ref[...])
        abr_ref[...] = abr
        abi_ref[...] = abi
        b_r, b_i = br_ref[...], bi_ref[...]
        bbr = qr * b_r - qi * b_i
        bbi = qr * b_i + qi * b_r
        mask = _ssm_group_mask()
        state = lax.broadcasted_iota(jnp.int32, (SSM_STATE, SSM_CH), 0)
        col = lax.broadcasted_iota(jnp.int32, (SSM_STATE, SSM_CH), 1)
        every_group = (col % SSM_STATE == state).astype(BF16)

        def spread(x):
            return jnp.where(mask, _dot(x, every_group), 0.0)

        for d in range(2):
            rows = slice(d * SSM_HALF, (d + 1) * SSM_HALF)
            for half, x_in, x_out in ((0, bbr[rows], cr_ref[rows, :]), (1, bbi[rows], -ci_ref[rows, :])):
                cols = slice(half * SSM_CH, (half + 1) * SSM_CH)
                m_in, m_out = spread(x_in), spread(x_out)
                win_ref[d, :, cols] = m_in.astype(BF16)
                wint_ref[d, cols, :] = m_in.T.astype(BF16)
                woutt_ref[d, :, cols] = m_out.astype(BF16)
                wout_ref[d, cols, :] = m_out.T.astype(BF16)

    vm = BS(memory_space=pltpu.VMEM)
    vec = SDS((SSM_ROWS, SSM_STATE), F32)
    wide = SDS((2, SSM_HALF, 2 * SSM_CH), BF16)
    tall = SDS((2, 2 * SSM_CH, SSM_HALF), BF16)
    return pl.pallas_call(body, out_shape=(vec, vec, wide, tall, wide, tall), in_specs=[vm] * 7 + after_specs,
                          out_specs=(vm,) * 6, name="ssm_prep",
                          compiler_params=_params())(a_r, a_i, ldt, b_r, b_i, c_r, c_i, *after_ops)


def _ssm_prep_bwd(a_r, a_i, ldt, b_r, b_i, d_abr, d_abi, d_win, d_woutt):
    def body(ar_ref, ai_ref, ldt_ref, br_ref, bi_ref, *rest):
        (dabr_refs, dabi_refs, dwin_refs, dwoutt_refs), outs = [rest[2 * k:2 * k + 2] for k in range(4)], rest[8:]
        dar_ref, dai_ref, dldt_ref, dbr_ref, dbi_ref, dcr_ref, dci_ref = outs
        a_r, a_i = ar_ref[...], ai_ref[...]
        dt, mag, cs, sn, abr, abi, den, nr, qr, qi = _ssm_zoh(a_r, a_i, ldt_ref[...])
        mask = _ssm_group_mask()
        col = lax.broadcasted_iota(jnp.int32, (SSM_CH, SSM_STATE), 0)
        state = lax.broadcasted_iota(jnp.int32, (SSM_CH, SSM_STATE), 1)
        own_state = (col % SSM_STATE == state).astype(BF16)

        def pick(dense):
            m = jnp.where(mask, dense, 0.0)
            hi = m.astype(BF16)
            lo = m - hi.astype(F32)
            return _dot(hi, own_state) + _dot(lo, own_state)

        def picked(refs, half):
            cols = slice(half * SSM_CH, (half + 1) * SSM_CH)
            return jnp.concatenate([pick(ref[:, cols]) for ref in refs], axis=0)

        first_channel = lax.broadcasted_iota(jnp.int32, (SSM_HALF, SSM_CH), 0) % SSM_GROUP == 0

        def first_rows(refs):
            return jnp.concatenate(
                [pick(jnp.where(first_channel, jnp.broadcast_to(ref[...], (SSM_HALF, SSM_CH)), 0.0)) for ref in refs], axis=0)

        g_r, g_i = picked(dwin_refs, 0), picked(dwin_refs, 1)
        dcr_ref[...] = picked(dwoutt_refs, 0)
        dci_ref[...] = -picked(dwoutt_refs, 1)
        b_r, b_i = br_ref[...], bi_ref[...]
        dbr_ref[...] = g_r * qr + g_i * qi
        dbi_ref[...] = g_i * qr - g_r * qi
        gqr = g_r * b_r + g_i * b_i
        gqi = g_i * b_r - g_r * b_i
        g_nr_num = gqr / den
        g_ni_num = gqi / den
        g_den = -(gqr * qr + gqi * qi) / den
        g_nr = g_nr_num * a_r - g_ni_num * a_i
        g_abi = g_nr_num * a_i + g_ni_num * a_r
        d_ar = g_nr_num * nr + g_ni_num * abi + 2.0 * a_r * g_den
        d_ai = g_nr_num * abi - g_ni_num * nr + 2.0 * a_i * g_den
        g_abr = first_rows(dabr_refs) + g_nr
        g_abi = first_rows(dabi_refs) + g_abi
        g_mag = g_abr * cs + g_abi * sn
        g_ang = mag * (g_abi * cs - g_abr * sn)
        g_e = g_mag * mag
        d_ar = d_ar + g_e * dt
        d_ai = d_ai + g_ang * dt
        g_dt = g_e * a_r + g_ang * a_i
        dar_ref[...] = d_ar
        dai_ref[...] = d_ai
        dldt_ref[...] = g_dt * dt

    vm = BS(memory_space=pltpu.VMEM)
    vec = SDS((SSM_ROWS, SSM_STATE), F32)
    return pl.pallas_call(body, out_shape=(vec,) * 7, in_specs=[vm] * 13, out_specs=(vm,) * 7, name="ssm_prep_bwd",
                          compiler_params=_params())(a_r, a_i, ldt, b_r, b_i, *d_abr, *d_abi, *d_win, *d_woutt)


SCAN_ROWS = 512
SCAN_SUB = 128


def _ssm_scan(name, inp, w1, a_r, a_i, w2, reverse):
    t = inp.shape[0]
    rows = min(SCAN_ROWS, t)
    n = t // rows
    n_sub = rows // SCAN_SUB
    ch = SSM_CH
    at = (lambda i: (n - 1 - i, 0)) if reverse else (lambda i: (i, 0))

    def body(in_ref, w1_ref, ar_ref, ai_ref, w2_ref, sb_ref, out_ref, cr_ref, ci_ref, k_ref, st_ref):
        i = pl.program_id(0)

        @pl.when(i == 0)
        def _():
            ar8 = jnp.broadcast_to(ar_ref[...], (8, ch))
            ai8 = jnp.broadcast_to(ai_ref[...], (8, ch))
            row = lax.broadcasted_iota(jnp.int32, (8, ch), 0)
            rank = (7 - row) if reverse else row
            powers = [(ar8, ai8)]
            for _ in range(7):
                p_r, p_i = powers[-1]
                powers.append((p_r * ar8 - p_i * ai8, p_r * ai8 + p_i * ar8))
            zero = jnp.zeros((8, ch), F32)
            for slot, k in enumerate((1, 2, 4)):
                k_ref[2 * slot] = jnp.where(rank >= k, powers[k - 1][0], zero)
                k_ref[2 * slot + 1] = jnp.where(rank >= k, powers[k - 1][1], zero)
            carry_r, carry_i = zero, zero
            for j in range(8):
                carry_r = jnp.where(rank == j, powers[j][0], carry_r)
                carry_i = jnp.where(rank == j, powers[j][1], carry_i)
            k_ref[6] = carry_r
            k_ref[7] = carry_i
            cr_ref[...] = zero
            ci_ref[...] = zero

        def group(r0, carry):
            c_r, c_i = carry
            x_r = st_ref[pl.ds(r0, 8), 0:ch]
            x_i = st_ref[pl.ds(r0, 8), ch:2 * ch]
            for slot, k in enumerate((1, 2, 4)):
                shift = (8 - k) if reverse else k
                s_r = pltpu.roll(x_r, shift, 0)
                s_i = pltpu.roll(x_i, shift, 0)
                m_r, m_i = k_ref[2 * slot], k_ref[2 * slot + 1]
                x_r, x_i = x_r + m_r * s_r - m_i * s_i, x_i + m_r * s_i + m_i * s_r
            p_r, p_i = k_ref[6], k_ref[7]
            x_r, x_i = x_r + p_r * c_r - p_i * c_i, x_i + p_r * c_i + p_i * c_r
            st_ref[pl.ds(r0, 8), 0:ch] = x_r
            st_ref[pl.ds(r0, 8), ch:2 * ch] = x_i
            last = 0 if reverse else 7
            return (jnp.broadcast_to(x_r[last:last + 1, :], (8, ch)), jnp.broadcast_to(x_i[last:last + 1, :], (8, ch)))

        carry = (cr_ref[...], ci_ref[...])
        for sc in (range(n_sub - 1, -1, -1) if reverse else range(n_sub)):
            part = pl.ds(sc * SCAN_SUB, SCAN_SUB)
            st_ref[part, :] = _dot(in_ref[part, :], w1_ref[...])
            for gi in range(SCAN_SUB // 8):
                g = (SCAN_SUB // 8 - 1 - gi) if reverse else gi
                carry = group(sc * SCAN_SUB + g * 8, carry)
            states = st_ref[part, :].astype(BF16)
            sb_ref[part, :] = states
            out_ref[part, :] = _dot(states, w2_ref[...])
        cr_ref[...] = carry[0]
        ci_ref[...] = carry[1]

    return pl.pallas_call(
        body, out_shape=(SDS((t, 2 * ch), BF16), SDS((t, D_SSM), F32)), grid=(n,),
        in_specs=[BS((rows, D_SSM), at), BS((D_SSM, 2 * ch), lambda i: (0, 0)), BS((1, ch), lambda i: (0, 0)),
                  BS((1, ch), lambda i: (0, 0)), BS((2 * ch, D_SSM), lambda i: (0, 0))],
        out_specs=(BS((rows, 2 * ch), at), BS((rows, D_SSM), at)),
        scratch_shapes=[pltpu.VMEM((8, ch), F32), pltpu.VMEM((8, ch), F32), pltpu.VMEM((8, 8, ch), F32),
                        pltpu.VMEM((rows, 2 * ch), F32)],
        name=name, compiler_params=_params("arbitrary"))(inp, w1, a_r, a_i, w2)


DA_ROWS = 1024


def _ssm_param_grads(name, lam, states, u, dy, reverse, after=()):
    t = lam.shape[0]
    rows = min(DA_ROWS, t)
    n = t // rows
    halo_rows = 16
    nb = rows // halo_rows
    ch = SSM_CH
    if reverse:
        halo_at = lambda i: (jnp.minimum((i + 1) * nb, t // halo_rows - 1), 0)
    else:
        halo_at = lambda i: (jnp.maximum(i * nb - 1, 0), 0)

    after_ops, after_specs = _after_operands(after)

    def body(lam_ref, x_ref, halo_ref, u_ref, dy_ref, *rest):
        dr_ref, di_ref, dwin_ref, dwoutt_ref = rest[len(after_ops):]
        i = pl.program_id(0)

        @pl.when(i == 0)
        def _():
            dr_ref[...] = jnp.zeros_like(dr_ref)
            di_ref[...] = jnp.zeros_like(di_ref)
            dwin_ref[...] = jnp.zeros_like(dwin_ref)
            dwoutt_ref[...] = jnp.zeros_like(dwoutt_ref)

        dwin_ref[...] += _dot(u_ref[...], lam_ref[...], TN)
        dwoutt_ref[...] += _dot(dy_ref[...], x_ref[...], TN)
        row = lax.broadcasted_iota(jnp.int32, (rows, ch), 0)
        if reverse:
            edge, shift, h_row, live = rows - 1, rows - 1, 0, i < n - 1
        else:
            edge, shift, h_row, live = 0, 1, halo_rows - 1, i > 0

        def neighbour(lo):
            halo = halo_ref[:, lo:lo + ch].astype(F32)[h_row:h_row + 1]
            halo = jnp.where(live, halo, 0.0)
            x = x_ref[:, lo:lo + ch].astype(F32)
            return jnp.where(row == edge, jnp.broadcast_to(halo, (rows, ch)), pltpu.roll(x, shift, 0))

        xp_r, xp_i = neighbour(0), neighbour(ch)
        l_r, l_i = lam_ref[:, 0:ch].astype(F32), lam_ref[:, ch:2 * ch].astype(F32)
        dr_ref[...] += jnp.sum(l_r * xp_r + l_i * xp_i, axis=0, keepdims=True)
        di_ref[...] += jnp.sum(l_i * xp_r - l_r * xp_i, axis=0, keepdims=True)

    blk = BS((rows, 2 * ch), lambda i: (i, 0))
    thin = BS((rows, D_SSM), lambda i: (i, 0))
    vec = BS((1, ch), lambda i: (0, 0))
    mat = BS((D_SSM, 2 * ch), lambda i: (0, 0))
    return pl.pallas_call(
        body, out_shape=(SDS((1, ch), F32), SDS((1, ch), F32), SDS((D_SSM, 2 * ch), F32), SDS((D_SSM, 2 * ch), F32)),
        grid=(n,), in_specs=[blk, blk, BS((halo_rows, 2 * ch), halo_at), thin, thin] + after_specs,
        out_specs=(vec, vec, mat, mat),
        name=name, compiler_params=_params("arbitrary"))(lam, states, states, u, dy, *after_ops)


GELU_C = math.sqrt(2.0 / math.pi)
GELU_K = 0.044715


def _ssm_combine(proj, y_fwd, y_bwd, d_skip, tm, after=()):
    t = proj.shape[0]
    after_ops, after_specs = _after_operands(after)

    def body(s_ref, yf_ref, yb_ref, d_ref, *rest):
        yt_ref, g_ref = rest[len(after_ops):]
        y = s_ref[...] * d_ref[...] + yf_ref[...] + yb_ref[...]
        yt_ref[...] = y
        th = jnp.tanh(GELU_C * (y + GELU_K * y * y * y))
        g_ref[...] = (0.5 * y * (1.0 + th)).astype(BF16)

    blk = BS((tm, D_SSM), lambda i: (i, 0))
    return pl.pallas_call(
        body, out_shape=(SDS((t, D_SSM), F32), SDS((t, D_SSM), BF16)), grid=(t // tm,),
        in_specs=[BS((tm, D_SSM), lambda i: (i, D_POOL // D_SSM)), blk, blk, BS((1, D_SSM), lambda i: (0, 0))] + after_specs,
        out_specs=(blk, blk), name="ssm_combine",
        compiler_params=_params("parallel"))(proj, y_fwd, y_bwd, d_skip, *after_ops)


def _ssm_ds(proj, d_yt, du_fwd, du_bwd, d_skip, tm):
    t = proj.shape[0]

    def body(s_ref, dy_ref, duf_ref, dub_ref, d_ref, ds_ref, dd_ref):
        i = pl.program_id(0)
        dy = dy_ref[...]
        ds_ref[...] = (dy * d_ref[...] + duf_ref[...] + dub_ref[...]).astype(BF16)

        @pl.when(i == 0)
        def _():
            dd_ref[...] = jnp.zeros_like(dd_ref)

        dd_ref[...] += jnp.sum(dy * s_ref[...], axis=0, keepdims=True)

    blk = BS((tm, D_SSM), lambda i: (i, 0))
    vec = BS((1, D_SSM), lambda i: (0, 0))
    return pl.pallas_call(
        body, out_shape=(SDS((t, D_SSM), BF16), SDS((1, D_SSM), F32)), grid=(t // tm,),
        in_specs=[BS((tm, D_SSM), lambda i: (i, D_POOL // D_SSM)), blk, blk, blk, vec],
        out_specs=(blk, vec), name="ssm_ds", compiler_params=_params("arbitrary"))(proj, d_yt, du_fwd, du_bwd, d_skip)


G_POOL_AT = D_POOL + D_SSM
G_SSM_AT = G_POOL_AT + D_MODEL
E_VAL, E_GATE = D_POOL, D_POOL + D_SSM


def _merge_specs(tm):
    return [BS((tm, D_POOL), lambda i: (i, 0)), BS((tm, D_SSM), lambda i: (i, 0)),
            BS((N_SHARD, 1024, 256), lambda i: (0, 0, 0)), BS((tm, D_FF), lambda i: (i, 0))]


def _merge_parts(s, ms, yv, w_ref, proj_ref):
    lo = 256 * s
    zp = _dot(ms, w_ref[s, 0:E_VAL, :])
    zv = _dot(yv, w_ref[s, E_VAL:E_GATE, :])
    zg = _dot(yv, w_ref[s, E_GATE:, :])
    return zp, zv, zg, proj_ref[:, G_POOL_AT + lo:G_POOL_AT + lo + 256], proj_ref[:, G_SSM_AT + lo:G_SSM_AT + lo + 256]


def _mixer_merge(ms, yssm, w_e, proj, tm):
    t = ms.shape[0]

    def body(ms_ref, y_ref, w_ref, proj_ref, o_ref):
        msv, yv = ms_ref[...], y_ref[...]
        for s in range(N_SHARD):
            zp, zv, zg, gp, gs = _merge_parts(s, msv, yv, w_ref, proj_ref)
            o_ref[:, 256 * s:256 * (s + 1)] = (_sigmoid(gp) * zp + _sigmoid(gs) * zv * _sigmoid(zg)).astype(BF16)

    row = BS((tm, D_MODEL), lambda i: (i, 0))
    return pl.pallas_call(
        body, out_shape=SDS((t, D_MODEL), BF16), grid=(t // tm,), in_specs=_merge_specs(tm), out_specs=row,
        name="mixer_merge", compiler_params=_params("parallel"))(ms, yssm, w_e, proj)


def _mixer_merge_bwd(ms, yssm, w_e, proj, dmerged, tm):
    t = ms.shape[0]

    def body(ms_ref, y_ref, w_ref, proj_ref, dm_ref, dgp_ref, dgs_ref, dzp_ref, dzv_ref, dzg_ref):
        msv, yv = ms_ref[...], y_ref[...]
        for s in range(N_SHARD):
            cols = slice(256 * s, 256 * (s + 1))
            zp, zv, zg, gp, gs = _merge_parts(s, msv, yv, w_ref, proj_ref)
            dm = dm_ref[:, cols].astype(F32)
            sp, ss, sg = _sigmoid(gp), _sigmoid(gs), _sigmoid(zg)
            dgp_ref[:, cols] = (dm * zp * sp * (1.0 - sp)).astype(BF16)
            dgs_ref[:, cols] = (dm * zv * sg * ss * (1.0 - ss)).astype(BF16)
            dzp_ref[:, cols] = (dm * sp).astype(BF16)
            dz = dm * ss
            dzv_ref[:, cols] = (dz * sg).astype(BF16)
            dzg_ref[:, cols] = (dz * zv * sg * (1.0 - sg)).astype(BF16)

    row = BS((tm, D_MODEL), lambda i: (i, 0))
    shape = SDS((t, D_MODEL), BF16)
    return pl.pallas_call(
        body, out_shape=(shape,) * 5, grid=(t // tm,), in_specs=_merge_specs(tm) + [row],
        out_specs=(row,) * 5, name="mixer_merge_bwd",
        compiler_params=_params("parallel"))(ms, yssm, w_e, proj, dmerged)


def _mixer_dw(ms, yssm, dzp, dzv, dzg, tm):
    t = ms.shape[0]
    tm = min(2 * tm, t)
    n_t = t // tm

    def body(ms_ref, y_ref, dzp_ref, dzv_ref, dzg_ref, o_ref, acc):
        i = pl.program_id(0)

        @pl.when(i == 0)
        def _():
            acc[...] = jnp.zeros_like(acc)

        msv, yv = ms_ref[...], y_ref[...]
        for s in range(N_SHARD):
            cols = slice(256 * s, 256 * (s + 1))
            acc[s, 0:E_VAL, :] += _dot(msv, dzp_ref[:, cols], TN)
            acc[s, E_VAL:E_GATE, :] += _dot(yv, dzv_ref[:, cols], TN)
            acc[s, E_GATE:, :] += _dot(yv, dzg_ref[:, cols], TN)

        @pl.when(i == n_t - 1)
        def _():
            o_ref[...] = acc[...].astype(BF16)

    row = BS((tm, D_MODEL), lambda i: (i, 0))
    full = BS((N_SHARD, 1024, 256), lambda i: (0, 0, 0))
    return pl.pallas_call(
        body, out_shape=SDS((N_SHARD, 1024, 256), BF16), grid=(n_t,),
        in_specs=[BS((tm, D_POOL), lambda i: (i, 0)), BS((tm, D_SSM), lambda i: (i, 0)), row, row, row],
        out_specs=full, scratch_shapes=[pltpu.VMEM((N_SHARD, 1024, 256), F32)],
        name="mixer_dw", compiler_params=_params("arbitrary"))(ms, yssm, dzp, dzv, dzg)


def _mixer_dx(dzp, dzv, dzg, w_e, y_total, tm):
    t = dzp.shape[0]

    def body(dzp_ref, dzv_ref, dzg_ref, w_ref, yt_ref, dms_ref, dy_ref, dyb_ref):
        acc_ms, acc_y = None, None
        for s in range(N_SHARD):
            cols = slice(256 * s, 256 * (s + 1))
            part_ms = _dot(dzp_ref[:, cols], w_ref[s, 0:E_VAL, :], NT)
            part_y = _dot(dzv_ref[:, cols], w_ref[s, E_VAL:E_GATE, :], NT) + _dot(dzg_ref[:, cols], w_ref[s, E_GATE:, :], NT)
            acc_ms = part_ms if s == 0 else acc_ms + part_ms
            acc_y = part_y if s == 0 else acc_y + part_y
        dms_ref[...] = acc_ms.astype(BF16)
        y = yt_ref[...]
        th = jnp.tanh(GELU_C * (y + GELU_K * y * y * y))
        dgelu = 0.5 * (1.0 + th) + 0.5 * y * (1.0 - th * th) * GELU_C * (1.0 + 3.0 * GELU_K * y * y)
        dy = acc_y * dgelu
        dy_ref[...] = dy
        dyb_ref[...] = dy.astype(BF16)

    row = BS((tm, D_MODEL), lambda i: (i, 0))
    narrow = BS((tm, D_SSM), lambda i: (i, 0))
    return pl.pallas_call(
        body, out_shape=(SDS((t, D_POOL), BF16), SDS((t, D_SSM), F32), SDS((t, D_SSM), BF16)), grid=(t // tm,),
        in_specs=[row, row, row, BS((N_SHARD, 1024, 256), lambda i: (0, 0, 0)), narrow],
        out_specs=(BS((tm, D_POOL), lambda i: (i, 0)), narrow, narrow),
        name="mixer_dx", compiler_params=_params("parallel"))(dzp, dzv, dzg, w_e, y_total)


def _attn_probs(q_h, k_h):
    s = _dot(q_h, k_h, NT) * (1.0 / math.sqrt(HEAD_DIM))
    e = jnp.exp(s - jnp.max(s, axis=-1, keepdims=True))
    return e / jnp.sum(e, axis=-1, keepdims=True)


def _attn_fwd(q, kv, tm):
    t = q.shape[0]
    tm = min(2 * tm, t)
    m = kv.shape[0]

    def body(q_ref, kv_ref, o_ref):
        for hd in range(N_HEADS):
            lo = hd * HEAD_DIM
            p = _attn_probs(q_ref[:, lo:lo + HEAD_DIM], kv_ref[:, lo:lo + HEAD_DIM])
            o_ref[:, lo:lo + HEAD_DIM] = _dot(p, kv_ref[:, D_MODEL + lo:D_MODEL + lo + HEAD_DIM]).astype(BF16)

    return pl.pallas_call(
        body, out_shape=SDS((t, D_MODEL), BF16), grid=(t // tm,),
        in_specs=[BS((tm, D_MODEL), lambda i: (i, 0)), BS((m, 2 * D_MODEL), lambda i: (0, 0))],
        out_specs=BS((tm, D_MODEL), lambda i: (i, 0)), name="attn_fwd", compiler_params=_params("parallel"))(q, kv)


def _attn_bwd(q, kv, d_o, tm):
    t = q.shape[0]
    m = kv.shape[0]

    def body(q_ref, kv_ref, do_ref, dq_ref, dkv_ref):
        i = pl.program_id(0)

        @pl.when(i == 0)
        def _():
            dkv_ref[...] = jnp.zeros_like(dkv_ref)

        for hd in range(N_HEADS):
            lo = hd * HEAD_DIM
            q_h = q_ref[:, lo:lo + HEAD_DIM]
            k_h = kv_ref[:, lo:lo + HEAD_DIM]
            v_h = kv_ref[:, D_MODEL + lo:D_MODEL + lo + HEAD_DIM]
            do_h = do_ref[:, lo:lo + HEAD_DIM]
            p = _attn_probs(q_h, k_h)
            dkv_ref[:, D_MODEL + lo:D_MODEL + lo + HEAD_DIM] += _dot(p, do_h, TN)
            dp = _dot(do_h, v_h, NT)
            ds = p * (dp - jnp.sum(dp * p, axis=-1, keepdims=True)) * (1.0 / math.sqrt(HEAD_DIM))
            dq_ref[:, lo:lo + HEAD_DIM] = _dot(ds, k_h).astype(BF16)
            dkv_ref[:, lo:lo + HEAD_DIM] += _dot(ds, q_h, TN)

    row = BS((tm, D_MODEL), lambda i: (i, 0))
    full = BS((m, 2 * D_MODEL), lambda i: (0, 0))
    return pl.pallas_call(
        body, out_shape=(SDS((t, D_MODEL), BF16), SDS((m, 2 * D_MODEL), F32)), grid=(t // tm,),
        in_specs=[row, full, row], out_specs=(row, full), name="attn_bwd",
        compiler_params=_params("arbitrary"))(q, kv, d_o)


TRANSPOSED = ("ffn1_w_gate", "ffn1_w_up", "ffn2_w_gate", "ffn2_w_up", "w_in")
GATHER_PHASES = {"f1a": (("ffn1_w_gate",), ("ffn1_w_up",)),
                 "f1b": (("ffn1_w_down",),),
                 "win": (("w_in",),),
                 "mix": (("w_mix_out", "w_q", "w_xo"), ("w_kv",), ("w_pool_proj", "w_glu_val", "w_glu_gate")),
                 "f2": (("ffn2_w_gate",), ("ffn2_w_up",), ("ffn2_w_down",))}
REDUCE_GROUPS = (("ffn2_w_gate",), ("ffn2_w_up",), ("ffn2_w_down",), ("w_xo",), ("w_q",), ("w_kv",), ("w_mix_out",),
                 ("w_pool_proj", "w_glu_val", "w_glu_gate"), ("w_in",), ("ffn1_w_gate",), ("ffn1_w_up",), ("ffn1_w_down",))
SMALL = ("ffn1_norm", "mix_norm", "pool_w", "pool_scale", "ssm_a_re", "ssm_a_im", "ssm_log_dt", "ssm_b_re",
         "ssm_b_im", "ssm_c_re", "ssm_c_im", "ssm_d", "xattn_norm", "mem_norm", "ffn2_norm", "final_norm")
WEIGHTS = ("ffn1_norm", "ffn1_w_gate", "ffn1_w_up", "ffn1_w_down", "mix_norm", "w_in", "pool_w", "pool_scale",
           "w_pool_proj", "ssm_a_re", "ssm_a_im", "ssm_log_dt", "ssm_b_re", "ssm_b_im", "ssm_c_re", "ssm_c_im",
           "ssm_d", "w_glu_val", "w_glu_gate", "w_mix_out", "xattn_norm", "mem_norm", "w_q", "w_kv", "w_xo",
           "ffn2_norm", "ffn2_w_gate", "ffn2_w_up", "ffn2_w_down", "final_norm")


def _small_view(a, n):
    return jnp.swapaxes(a, 3, 4) if n in ("ssm_b_re", "ssm_b_im") else a


def _device_step(x, mem, target, wts, sp, reducer=None):
    t = x.shape[0]
    tm = min(TM, t)
    g = {}

    first_gather = wts.start("win", wts.start("f1b", wts.start("f1a")))
    u1 = _rmsnorm("norm_ffn1", x, sp["ffn1_norm"], tm, after=first_gather)

    def per_channel(a):
        a = a.reshape(2 * SSM_GROUPS, 1, -1)
        return jnp.broadcast_to(a, (2 * SSM_GROUPS, SSM_GROUP, a.shape[-1])).reshape(SSM_ROWS, a.shape[-1])

    ssm_a = per_channel(sp["ssm_a_re"]), per_channel(sp["ssm_a_im"]), per_channel(sp["ssm_log_dt"])
    ssm_b = sp["ssm_b_re"].reshape(SSM_ROWS, SSM_STATE), sp["ssm_b_im"].reshape(SSM_ROWS, SSM_STATE)
    abr, abi, w_in_s, w_in_s_t, w_out_s_t, w_out_s = _ssm_prep(
        *ssm_a, *ssm_b, sp["ssm_c_re"].reshape(SSM_ROWS, SSM_STATE), sp["ssm_c_im"].reshape(SSM_ROWS, SSM_STATE),
        after=first_gather)
    first_rows = (2, SSM_GROUPS, SSM_GROUP, SSM_STATE)
    a_r = abr.reshape(first_rows)[:, :, 0].reshape(2, 1, SSM_CH)
    a_i = abi.reshape(first_rows)[:, :, 0].reshape(2, 1, SSM_CH)
    mem_n = _rmsnorm("norm_mem", mem, sp["mem_norm"], mem.shape[0], after=first_gather + wts.sources(("mix", "f2")))

    whole = (D_FF, D_MODEL)
    w_g1, w_u1 = wts.finish("f1a", [u1, w_in_s, w_in_s_t, w_out_s, w_out_s_t, a_r, a_i, mem_n])
    w_f1 = {"gate": w_g1.reshape(whole), "up": w_u1.reshape(whole)}
    g1, up1, a1 = _ffn_up("ffn1_up", u1, w_f1, tm)
    (w_dn,) = wts.finish("f1b", [a1])
    w_f1["down"] = w_dn.reshape(whole)
    h1, u2 = _ffn_down("ffn1_down", a1, w_f1, x, tm, next_gain=sp["mix_norm"])

    (w_in_g,) = wts.finish("win", [u2])
    w_in_t = w_in_g.reshape(D_FF, D_MODEL)
    proj, s_in = _mix_in(u2, w_in_t, tm, after=wts.start("f2", wts.start("mix", [w_in_g])))
    pooled, mixed, ms = _pool_fwd(proj, sp["pool_w"][0], sp["pool_scale"])

    states, y_dirs = [], []
    for dr in range(2):
        st, yd = _ssm_scan(f"ssm_scan_fwd{dr}", s_in, w_in_s[dr], a_r[dr], a_i[dr], w_out_s[dr], reverse=(dr == 1))
        states.append(st)
        y_dirs.append(yd)
    w_sq, w_kv, w_e = wts.finish("mix", y_dirs)
    w_mo, w_q, w_xo = (w_sq[:, 256 * k:256 * (k + 1)].reshape(D_MODEL, D_MODEL) for k in range(3))
    w_d = w_kv[:, None]
    y_total, yssm = _ssm_combine(proj, y_dirs[0], y_dirs[1], sp["ssm_d"], tm)

    merged = _mixer_merge(ms, yssm, w_e, proj, tm)
    h2, u3 = _mm_resid_norm("mix_out", merged, w_mo, h1, sp["xattn_norm"], tm)

    q = _plain_mm("attn_q", u3, w_q, NN, BF16, tm)
    n_mem = mem.shape[0]
    kv = _mm("attn_kv", [(mem_n, BS((n_mem, D_MODEL), lambda s: (0, 0)), w_d, BS((None, None, D_MODEL, 512), lambda s: (s, 0, 0, 0)), NN)],
             grid=(N_SHARD,), out_shape=SDS((n_mem, 2 * D_MODEL), BF16), out_spec=BS((n_mem, 512), lambda s: (0, s)))
    o = _attn_fwd(q, kv, tm)
    h3, u4 = _mm_resid_norm("attn_out", o, w_xo, h2, sp["ffn2_norm"], tm)

    w_f2 = dict(zip(("gate", "up", "down"), (a.reshape(whole) for a in wts.finish("f2", [u4]))))
    g2, up2, a2 = _ffn_up("ffn2_up", u4, w_f2, tm)
    loss, dh4, dh4_b, g["final_norm"] = _ffn_down("ffn2_down", a2, w_f2, h3, tm,
                                                  head=(sp["final_norm"].reshape(1, D_MODEL), target))

    dg2, dup2 = _ffn_bwd_act("ffn2_bwd_act", dh4_b, w_f2, g2, up2, tm)
    dw_f2 = _ffn_dw("ffn2_dw", u4, dg2, dup2, a2, dh4_b, tm)
    dh3, dh3_b, g["ffn2_norm"] = _ffn_dx("ffn2_dx", dg2, dup2, w_f2, h3, sp["ffn2_norm"], dh4, tm)

    d_o = _plain_mm("attn_out_dx", dh3_b, w_xo, NT, BF16, tm)
    dw_xo = _dw_mm("attn_out_dw", o, dh3_b, tm)
    dq, dkv = _attn_bwd(q, kv, d_o, tm)
    dw_q = _dw_mm("attn_q_dw", u3, dq, tm)
    dh2, dh2_b, g["xattn_norm"] = _mm_norm_bwd("attn_q_dx", dq, w_q, NT, h2, sp["xattn_norm"], dh3, tm)
    dw_kv = _mm("attn_kv_dw", [(mem_n, BS((n_mem, D_MODEL), lambda s: (0, 0)), dkv, BS((n_mem, 512), lambda s: (0, s)), TN)],
                grid=(N_SHARD,), out_shape=SDS((N_SHARD, D_MODEL, 512), BF16), out_spec=BS((None, D_MODEL, 512), lambda s: (s, 0, 0)))
    dmem_n = _mm("attn_kv_dx", [(dkv, BS((n_mem, 512), lambda s: (0, s)), w_d, BS((None, None, D_MODEL, 512), lambda s: (s, 0, 0, 0)), NT)],
                 grid=(N_SHARD,), red_axis=0, out_shape=SDS((n_mem, D_MODEL), F32), out_spec=BS((n_mem, D_MODEL), lambda s: (0, 0)))
    _, _, g["mem_norm"] = _rmsnorm_bwd("norm_mem_bwd", mem, sp["mem_norm"], dmem_n, None, n_mem)

    square = (N_SHARD, D_MODEL // N_SHARD, D_MODEL)
    sharded = (N_SHARD, FF_SH, D_MODEL)
    early = [a.reshape(sharded) for a in dw_f2] + [dw_xo.reshape(square), dw_q.reshape(square), dw_kv]
    swapping = reducer.swap_start("a1", early) if reducer is not None else []
    dmerged = _plain_mm("mix_out_dx", dh2_b, w_mo, NT, BF16, tm, after=swapping)
    dw_mo = _dw_mm("mix_out_dw", merged, dh2_b, tm)
    d_gp, d_gs, dzp, dzv, dzg = _mixer_merge_bwd(ms, yssm, w_e, proj, dmerged, tm)
    dw_e = _mixer_dw(ms, yssm, dzp, dzv, dzg, tm)
    d_ms, d_yt, d_yt_b = _mixer_dx(dzp, dzv, dzg, w_e, y_total, tm)
    dp, d_scale, d_pw = _pool_bwd(d_ms, mixed, pooled, sp["pool_w"][0], sp["pool_scale"])
    g["pool_scale"] = d_scale
    g["pool_w"] = d_pw[None]

    du_dirs, lams = [], []
    for dr in range(2):
        lam, du = _ssm_scan(f"ssm_scan_bwd{dr}", d_yt_b, w_out_s_t[dr], a_r[dr], -a_i[dr], w_in_s_t[dr], reverse=(dr == 0))
        du_dirs.append(du)
        lams.append(lam)
    ds, g["ssm_d"] = _ssm_ds(proj, d_yt, du_dirs[0], du_dirs[1], sp["ssm_d"], tm)

    d_proj = [dp, ds, d_gp, d_gs]
    dw_in_t = _mix_in_dw(d_proj, u2, tm)
    dh1, dh1_b, g["mix_norm"] = _mm_norm_bwd("mix_in_dx", d_proj, w_in_t, NN, h1, sp["mix_norm"], dh2, tm)

    early += [dw_mo.reshape(square), dw_e, dw_in_t.reshape(sharded)]
    g["final_norm"] = g["final_norm"].reshape(D_MODEL)

    travelling = reducer.start("a", early[6:], swapped=["a1"], after=list(g.values())) if reducer is not None else []
    d_abr, d_abi, d_cm, d_bm = [], [], [], []
    for dr in range(2):
        da_r, da_i, d_win, d_woutt = _ssm_param_grads(f"ssm_param_grads{dr}", lams[dr], states[dr], s_in, d_yt_b,
                                                      reverse=(dr == 1), after=travelling)
        d_abr.append(da_r)
        d_abi.append(da_i)
        d_bm.append(d_win)
        d_cm.append(d_woutt)

    d_ar, d_ai, d_ldt, d_br, d_bi, d_cr, d_ci = _ssm_prep_bwd(*ssm_a, *ssm_b, d_abr, d_abi, d_bm, d_cm)
    per_group = (2 * SSM_GROUPS, SSM_GROUP * SSM_STATE)
    g["ssm_a_re"] = d_ar.reshape(2 * SSM_GROUPS, SSM_GROUP, SSM_STATE).sum(axis=1).reshape(sp["ssm_a_re"].shape)
    g["ssm_a_im"] = d_ai.reshape(2 * SSM_GROUPS, SSM_GROUP, SSM_STATE).sum(axis=1).reshape(sp["ssm_a_im"].shape)
    g["ssm_log_dt"] = d_ldt.reshape(per_group).sum(axis=1).reshape(sp["ssm_log_dt"].shape)
    g["ssm_b_re"] = d_br.reshape(sp["ssm_b_re"].shape)
    g["ssm_b_im"] = d_bi.reshape(sp["ssm_b_im"].shape)
    g["ssm_c_re"] = d_cr.reshape(sp["ssm_c_re"].shape)
    g["ssm_c_im"] = d_ci.reshape(sp["ssm_c_im"].shape)
    if reducer is not None:
        travelling = travelling + [d_ar, d_br, d_cr]
    dg1, dup1 = _ffn_bwd_act("ffn1_bwd_act", dh1_b, w_f1, g1, up1, tm, after=travelling)
    dw_f1 = [a.reshape(sharded) for a in _ffn_dw("ffn1_dw", u1, dg1, dup1, a1, dh1_b, tm)]
    if reducer is not None:
        travelling = reducer.start("b", dw_f1, after=reducer.finish("a", dw_f1[:1]))
        travelling = travelling + reducer.join_start("a", after=travelling)
    grad_x, _, g["ffn1_norm"] = _ffn_dx("ffn1_dx", dg1, dup1, w_f1, x, sp["ffn1_norm"], dh1, tm, after=travelling)
    if reducer is not None:
        reducer.join_finish("a", [grad_x])
    return loss, grad_x, early + dw_f1, g


def _mesh_place():
    x, y, c = lax.axis_index("x"), lax.axis_index("y"), lax.axis_index("c")
    chips = [(1 - x, y), (x, 1 - y), (1 - x, 1 - y)]
    return x, y, c, chips


def _remote(src, dst, send_sems, recv_sems, k, to):
    return pltpu.make_async_remote_copy(src_ref=src, dst_ref=dst, send_sem=send_sems.at[k], recv_sem=recv_sems.at[k],
                                        device_id=to, device_id_type=MESH)


def _sibling_swap_halves(tag, grads, after=()):
    n = len(grads)
    after_ops, after_specs = _after_operands(after)

    def body(*refs):
        ins, outs = refs[:n], refs[n + len(after_ops):2 * n + len(after_ops)]
        send_sems, recv_sems = refs[2 * n + len(after_ops):]
        x, y, c, _ = _mesh_place()
        sibling = (x, y, 1 - c)
        copies = []
        for k in range(n):
            half = grads[k].shape[1] // 2
            theirs = pl.ds(pl.multiple_of((1 - c) * half, 16), half)
            cp = _remote(ins[k].at[:, theirs, :], outs[k], send_sems, recv_sems, k, sibling)
            cp.start()
            copies.append(cp)
        for cp in copies:
            cp.wait_recv()
        for cp in copies:
            cp.wait_send()

    hbm = BS(memory_space=pl.ANY)
    return pl.pallas_call(
        body, out_shape=tuple(SDS((g.shape[0], g.shape[1] // 2, g.shape[2]), g.dtype) for g in grads),
        in_specs=[hbm] * n + after_specs, out_specs=(hbm,) * n,
        scratch_shapes=[pltpu.SemaphoreType.DMA((n,)), pltpu.SemaphoreType.DMA((n,))],
        name="reduce_sibling_send_" + tag, compiler_params=_params())(*grads, *after_ops)


def _row_tile(rows, cap=512):
    return max(r for r in range(16, cap + 1, 16) if rows % r == 0)


REDUCE_STEPS = 2


def _chip_presum(tag, grads, gots, c_idx):
    n = len(grads)
    halves = [g.shape[1] // 2 for g in grads]
    tiles = [(h // REDUCE_STEPS, g.shape[2]) for h, g in zip(halves, grads)]

    def body(c_ref, *refs):
        for k in range(n):
            refs[2 * n + k][...] = (refs[k][...].astype(F32) + refs[n + k][...].astype(F32)).astype(BF16)

    mine = [BS((None, None) + tile, lambda s, i, c_ref: (s, c_ref[0], i, 0)) for tile in tiles]
    plain = [BS((None,) + tile, lambda s, i, c_ref: (s, i, 0)) for tile in tiles]
    return list(pl.pallas_call(
        body, out_shape=tuple(SDS((g.shape[0], h, g.shape[2]), BF16) for g, h in zip(grads, halves)),
        grid_spec=pltpu.PrefetchScalarGridSpec(num_scalar_prefetch=1, grid=(N_SHARD, REDUCE_STEPS),
                                               in_specs=mine + plain, out_specs=plain),
        name="reduce_presum_" + tag, compiler_params=_params("parallel", "parallel"))(
            c_idx, *[g.reshape(g.shape[0], 2, h, g.shape[2]) for g, h in zip(grads, halves)], *gots))


HBM_SPEC = BS(memory_space=pltpu.HBM)
SEM_SPEC = BS(memory_space=pltpu.SEMAPHORE)
DATAFLOW = pltpu.SideEffectType.DATAFLOW_SIDE_EFFECTING


def _chip_exchange_copies(parts, lands, send_sems, recv_sems):
    _, _, c, chips = _mesh_place()
    return [_remote(parts[k].at[2 * px + py], lands[k].at[j], send_sems, recv_sems, 3 * k + j, (px, py, c))
            for k in range(len(parts)) for j, (px, py) in enumerate(chips)]


def _gather_copies(shards, lands, send_sems, recv_sems):
    x, y, c, chips = _mesh_place()
    return [_remote(shards[k], lands[k].at[2 * x + y], send_sems, recv_sems, 3 * k + j, (px, py, c))
            for k in range(len(shards)) for j, (px, py) in enumerate(chips)]


def _gather_half_copies(shards, lands, send_sems, recv_sems):
    x, y, c, chips = _mesh_place()
    out = []
    for k in range(len(shards)):
        half = shards[k].shape[0] // 2
        mine = pl.ds(pl.multiple_of(c * half, 16), half)
        for j, (px, py) in enumerate(chips):
            out.append(_remote(shards[k].at[mine, :], lands[k].at[2 * x + y, mine, :], send_sems, recv_sems,
                               3 * k + j, (px, py, c)))
    return out


def _sibling_fill(tag, lands):
    n = len(lands)

    def body(*refs):
        outs = refs[n:2 * n]
        send_sems, recv_sems = refs[2 * n:]
        x, y, c, chips = _mesh_place()
        copies = []
        for k in range(n):
            half = lands[k].shape[1] // 2
            mine = pl.ds(pl.multiple_of(c * half, 16), half)
            for j, (px, py) in enumerate(chips):
                blk = outs[k].at[2 * px + py, mine, :]
                copies.append(_remote(blk, blk, send_sems, recv_sems, 3 * k + j, (x, y, 1 - c)))
        for cp in copies:
            cp.start()
        for cp in copies:
            cp.wait_recv()
        for cp in copies:
            cp.wait_send()

    hbm = BS(memory_space=pl.ANY)
    return list(pl.pallas_call(
        body, out_shape=tuple(SDS(a.shape, a.dtype) for a in lands),
        in_specs=[hbm] * n, out_specs=(hbm,) * n, input_output_aliases={k: k for k in range(n)},
        scratch_shapes=[pltpu.SemaphoreType.DMA((3 * n,)), pltpu.SemaphoreType.DMA((3 * n,))],
        name="gather_fill_" + tag, compiler_params=_params())(*lands))


def _swap_copies(grads, lands, send_sems, recv_sems):
    x, y, c, _ = _mesh_place()
    out = []
    for k in range(len(grads)):
        half = grads[k].shape[1] // 2
        theirs = pl.ds(pl.multiple_of((1 - c) * half, 16), half)
        out.append(_remote(grads[k].at[:, theirs, :], lands[k], send_sems, recv_sems, k, (x, y, 1 - c)))
    return out


def _join_copies(fulls, same, send_sems, recv_sems):
    x, y, c, _ = _mesh_place()
    out = []
    for k in range(len(fulls)):
        half = fulls[k].shape[0] // 2
        mine = fulls[k].at[pl.ds(pl.multiple_of(c * half, 8), half), :]
        out.append(_remote(mine, mine, send_sems, recv_sems, k, (x, y, 1 - c)))
    return out


def _everyone_copies(packs, lands, send_sems, recv_sems):
    x, y, c, _ = _mesh_place()
    out = []
    for k in range(len(packs)):
        for j in range(N_DEV - 1):
            bx, by, bc = (j + 1) >> 2 & 1, (j + 1) >> 1 & 1, (j + 1) & 1
            peer = (x ^ bx, y ^ by, c ^ bc)
            out.append(_remote(packs[k], lands[k].at[4 * x + 2 * y + c], send_sems, recv_sems, (N_DEV - 1) * k + j, peer))
    return out


def _split_start(name, copies, sources, land_shapes, after=(), fanout=3):
    n = len(sources)
    n_land = len(land_shapes)
    m = n + n_land
    n_sems = fanout * n
    after_ops, after_specs = _after_operands(after)

    def body(*refs):
        ins = refs[:n]
        lands = refs[n:m] if n_land else ins
        send_sems, recv_sems = refs[m + len(after_ops)], refs[m + len(after_ops) + 1]
        token = refs[-1]
        for cp in copies(ins, lands, send_sems, recv_sems):
            cp.start()
        token[...] = jnp.zeros_like(token)

    lands = [pltpu.with_memory_space_constraint(lax.empty(s, d), pltpu.HBM) for s, d in land_shapes]
    sources = [pltpu.with_memory_space_constraint(p, pltpu.HBM) for p in sources]
    thru = [pltpu.HBM(a.shape, a.dtype) for a in sources + lands]
    out = pl.pallas_call(
        body, name=name,
        out_shape=(pltpu.SemaphoreType.DMA((n_sems,)), pltpu.SemaphoreType.DMA((n_sems,)), *thru, SDS((8, 128), F32)),
        in_specs=[HBM_SPEC] * m + after_specs,
        out_specs=(SEM_SPEC, SEM_SPEC, *[HBM_SPEC] * m, BS(memory_space=pltpu.VMEM)),
        input_output_aliases={i: 2 + i for i in range(m)},
        compiler_params=pltpu.CompilerParams(has_side_effects=DATAFLOW))(*sources, *lands, *after_ops)
    return out[0], out[1], list(out[2:2 + n]), list(out[2 + n:2 + m]), out[-1]


def _split_wait(name, copies, send_sems, recv_sems, sources, lands, after):
    n = len(sources)
    m = n + len(lands)
    after_ops, after_specs = _after_operands(after)

    def body(*refs):
        ins = refs[:n]
        zones = refs[n:m] if m > n else ins
        for cp in copies(ins, zones, refs[m], refs[m + 1]):
            cp.wait_send()
            cp.wait_recv()

    out = pl.pallas_call(
        body, name=name,
        out_shape=tuple(pltpu.HBM(a.shape, a.dtype) for a in sources + lands),
        in_specs=[HBM_SPEC] * m + [SEM_SPEC, SEM_SPEC] + after_specs, out_specs=(HBM_SPEC,) * m,
        input_output_aliases={i: i for i in range(m)},
        compiler_params=pltpu.CompilerParams(has_side_effects=DATAFLOW))(*sources, *lands, send_sems, recv_sems, *after_ops)
    return list(out[:n]), list(out[n:])


class _WeightGatherer:
    def __init__(self, shards):
        self.shards, self.open = shards, {}
        self.me = 2 * lax.axis_index("x") + lax.axis_index("y")

    HALVED = ("f1a", "win", "mix")

    def start(self, tag, after=()):
        shapes = [((N_SHARD,) + s.shape, s.dtype) for s in self.shards[tag]]
        copies = _gather_half_copies if tag in self.HALVED else _gather_copies
        self.open[tag] = _split_start("gather_start_" + tag, copies, self.shards[tag], shapes, after)
        return [self.open[tag][-1]]

    def sources(self, tags):
        return [s for tag in tags for s in self.shards[tag]]

    def finish(self, tag, after):
        send_sems, recv_sems, shards, lands, _ = self.open.pop(tag)
        copies = _gather_half_copies if tag in self.HALVED else _gather_copies
        shards, lands = _split_wait("gather_wait_" + tag, copies, send_sems, recv_sems, shards, lands, after)
        if tag in self.HALVED:
            lands = _sibling_fill(tag, lands)
        return [lax.dynamic_update_slice(zone, s[None], (self.me, 0, 0)) for zone, s in zip(lands, shards)]


class _GradReducer:
    def __init__(self):
        self.c_idx = lax.axis_index("c").astype(jnp.int32).reshape(1)
        self.place = jnp.stack([2 * lax.axis_index("x") + lax.axis_index("y"), lax.axis_index("c")]).astype(jnp.int32)
        self.swaps, self.open, self.landed, self.joins, self.reduced = {}, {}, {}, {}, []

    def swap_start(self, tag, grads, after=()):
        shapes = [((g.shape[0], g.shape[1] // 2, g.shape[2]), g.dtype) for g in grads]
        self.swaps[tag] = _split_start("reduce_swap_start_" + tag, _swap_copies, grads, shapes, after, fanout=1)
        return [self.swaps[tag][-1]]

    def start(self, tag, grads, after=(), swapped=()):
        pairs = []
        for s in swapped:
            send_sems, recv_sems, early, lands, _ = self.swaps.pop(s)
            pairs += zip(*_split_wait("reduce_swap_wait_" + s, _swap_copies, send_sems, recv_sems, early, lands, grads[-1:]))
        pairs += zip(grads, _sibling_swap_halves(tag, grads, after))
        parts = _chip_presum(tag, [g for g, _ in pairs], [s for _, s in pairs], self.c_idx)
        shapes = [((3,) + p.shape[1:], p.dtype) for p in parts]
        self.open[tag] = _split_start("reduce_exchange_start_" + tag, _chip_exchange_copies, parts, shapes)
        return [self.open[tag][-1]]

    def finish(self, tag, after):
        send_sems, recv_sems, parts, lands, _ = self.open.pop(tag)
        self.landed[tag] = _split_wait("reduce_exchange_wait_" + tag, _chip_exchange_copies, send_sems, recv_sems, parts, lands, after)
        return self.landed[tag][1][:1]

    def _sums(self, tag, after=()):
        parts, landed = self.landed.pop(tag)
        return _chip_sum(tag, parts, landed, self.place, after)

    def join_start(self, tag, after=()):
        self.joins[tag] = _split_start("reduce_join_start_" + tag, _join_copies, self._sums(tag, after), [], fanout=1)
        return [self.joins[tag][-1]]

    def join_finish(self, tag, after):
        send_sems, recv_sems, fulls, _, _ = self.joins.pop(tag)
        self.reduced += _split_wait("reduce_join_wait_" + tag, _join_copies, send_sems, recv_sems, fulls, [], after)[0]

    def join(self, tag, after=()):
        self.reduced += _sibling_join_halves(self._sums(tag), after)


def _chip_sum(tag, parts, gots, place, after=()):
    n = len(parts)
    tiles = [(p.shape[1] // REDUCE_STEPS, p.shape[2]) for p in parts]
    after_ops, after_specs = _after_operands(after)

    def body(place_ref, *refs):
        outs = refs[2 * n + len(after_ops):]
        for k in range(n):
            acc = refs[k][...].astype(F32)
            for j in range(3):
                acc = acc + refs[n + k][j].astype(F32)
            outs[k][...] = acc

    return list(pl.pallas_call(
        body, out_shape=tuple(SDS((2 * p.shape[1], p.shape[2]), F32) for p in parts),
        grid_spec=pltpu.PrefetchScalarGridSpec(
            num_scalar_prefetch=1, grid=(REDUCE_STEPS,),
            in_specs=[BS((None,) + tile, lambda i, place_ref: (place_ref[0], i, 0)) for tile in tiles]
            + [BS((3,) + tile, lambda i, place_ref: (0, i, 0)) for tile in tiles] + after_specs,
            out_specs=[BS(tile, lambda i, place_ref: (place_ref[1] * REDUCE_STEPS + i, 0)) for tile in tiles]),
        name="reduce_sum_" + tag, compiler_params=_params("parallel"))(place, *parts, *gots, *after_ops))


def _sibling_join_halves(fulls, after=()):
    n = len(fulls)
    after_ops, after_specs = _after_operands(after)

    def body(*refs):
        outs = refs[n + len(after_ops):2 * n + len(after_ops)]
        send_sems, recv_sems = refs[2 * n + len(after_ops):]
        copies = _join_copies(outs, outs, send_sems, recv_sems)
        for cp in copies:
            cp.start()
        for cp in copies:
            cp.wait_recv()
        for cp in copies:
            cp.wait_send()

    hbm = BS(memory_space=pl.ANY)
    return list(pl.pallas_call(
        body, out_shape=tuple(SDS(f.shape, f.dtype) for f in fulls),
        in_specs=[hbm] * n + after_specs, out_specs=(hbm,) * n, input_output_aliases={k: k for k in range(n)},
        scratch_shapes=[pltpu.SemaphoreType.DMA((n,)), pltpu.SemaphoreType.DMA((n,))],
        name="reduce_sibling_join", compiler_params=_params())(*fulls, *after_ops))


N_DEV = 8


def _sum_devices(packs):
    _, rows, lanes = packs.shape

    def body(p_ref, o_ref):
        acc = p_ref[0]
        for dev in range(1, N_DEV):
            acc = acc + p_ref[dev]
        o_ref[...] = acc

    vm = BS(memory_space=pltpu.VMEM)
    return pl.pallas_call(body, out_shape=SDS((rows, lanes), F32), in_specs=[vm], out_specs=vm,
                          name="small_sum", compiler_params=_params())(packs)


def _adamw_refs(w_ref, g_ref, m_ref, v_ref, go_ref, d_ref, mo_ref, vo_ref):
    bc1 = 1.0 - ADAM_B1 ** ADAM_STEP
    bc2 = 1.0 - ADAM_B2 ** ADAM_STEP
    g = g_ref[...]
    m_new = ADAM_B1 * m_ref[...] + (1.0 - ADAM_B1) * g
    v_new = ADAM_B2 * v_ref[...] + (1.0 - ADAM_B2) * (g * g)
    go_ref[...] = g
    mo_ref[...] = m_new
    vo_ref[...] = v_new
    d_ref[...] = -ADAM_LR * ((m_new / bc1) / (jnp.sqrt(v_new / bc2) + ADAM_EPS) + ADAM_WD * w_ref[...])


def _adamw_small(ws, gs, ms, vs):
    n = len(ws)

    def body(*refs):
        for k in range(n):
            _adamw_refs(*[refs[j * n + k] for j in range(4)], *refs[4 * n + 4 * k:4 * n + 4 * k + 4])

    vm = BS(memory_space=pltpu.VMEM)
    outs = pl.pallas_call(
        body, out_shape=tuple(SDS(a.shape, F32) for a in ws for _ in range(4)), in_specs=[vm] * (4 * n),
        out_specs=(vm,) * (4 * n), name="adamw_small", compiler_params=_params())(*ws, *gs, *ms, *vs)
    return [outs[4 * k:4 * k + 4] for k in range(n)]


def _adamw(name, w, grad, row0, m, v, after=()):
    rows, cols = w.shape
    tr = rows if rows < 16 else _row_tile(rows, 352)
    after_ops, after_specs = _after_operands(after)

    def body(w_ref, g_ref, m_ref, v_ref, *rest):
        _adamw_refs(w_ref, g_ref, m_ref, v_ref, *rest[len(after_ops):])

    blk = BS((tr, cols), lambda i: (i, 0))
    shape = SDS((rows, cols), F32)
    return pl.pallas_call(
        body, out_shape=(shape,) * 4, grid=(rows // tr,),
        in_specs=[blk, BS((tr, cols), lambda i: (row0 // tr + i, 0)), blk, blk] + after_specs, out_specs=(blk,) * 4,
        name=name, compiler_params=_params("parallel"))(w, grad, m, v, *after_ops)


SMALL_LANES = 128


SMALL_TILE = 8 * SMALL_LANES


def _packed_rows(p):
    return -(-p.size // SMALL_TILE) * 8


def _pack_small(parts):
    tiles = [jnp.pad(jnp.ravel(p), (0, _packed_rows(p) * SMALL_LANES - p.size)).reshape(-1, SMALL_LANES) for p in parts]
    rows = sum(t.shape[0] for t in tiles)
    return jnp.concatenate(tiles + [jnp.zeros((-rows % 64, SMALL_LANES), F32)], axis=0)


def _unpack_small(packed, like):
    out, at = [], 0
    for p in like:
        rows = _packed_rows(p)
        out.append(jnp.ravel(packed[at:at + rows])[:p.size].reshape(p.shape))
        at += rows
    return out


def kernel(x, mem, ffn1_norm, ffn1_w_gate, ffn1_w_up, ffn1_w_down, mix_norm, w_in, pool_w, pool_scale, w_pool_proj, ssm_a_re, ssm_a_im, ssm_log_dt, ssm_b_re, ssm_b_im, ssm_c_re, ssm_c_im, ssm_d, w_glu_val, w_glu_gate, w_mix_out, xattn_norm, mem_norm, w_q, w_kv, w_xo, ffn2_norm, ffn2_w_gate, ffn2_w_up, ffn2_w_down, final_norm, loss_target, m_ffn1_norm, m_ffn1_w_gate, m_ffn1_w_up, m_ffn1_w_down, m_mix_norm, m_w_in, m_pool_w, m_pool_scale, m_w_pool_proj, m_ssm_a_re, m_ssm_a_im, m_ssm_log_dt, m_ssm_b_re, m_ssm_b_im, m_ssm_c_re, m_ssm_c_im, m_ssm_d, m_w_glu_val, m_w_glu_gate, m_w_mix_out, m_xattn_norm, m_mem_norm, m_w_q, m_w_kv, m_w_xo, m_ffn2_norm, m_ffn2_w_gate, m_ffn2_w_up, m_ffn2_w_down, m_final_norm, v_ffn1_norm, v_ffn1_w_gate, v_ffn1_w_up, v_ffn1_w_down, v_mix_norm, v_w_in, v_pool_w, v_pool_scale, v_w_pool_proj, v_ssm_a_re, v_ssm_a_im, v_ssm_log_dt, v_ssm_b_re, v_ssm_b_im, v_ssm_c_re, v_ssm_c_im, v_ssm_d, v_w_glu_val, v_w_glu_gate, v_w_mix_out, v_xattn_norm, v_mem_norm, v_w_q, v_w_kv, v_w_xo, v_ffn2_norm, v_ffn2_w_gate, v_ffn2_w_up, v_ffn2_w_down, v_final_norm):
    given = dict(locals())
    w = {n: given[n] for n in WEIGHTS}
    m = {n: given["m_" + n] for n in WEIGHTS}
    v = {n: given["v_" + n] for n in WEIGHTS}

    def shard_view(a, n):
        return a[0].T if n in TRANSPOSED else a[0]

    def shard_unview(a, n):
        return (a.T if n in TRANSPOSED else a)[None]

    shards = {tag: [jnp.concatenate([shard_view(w[n], n).astype(BF16) for n in grp], axis=0) for grp in arrays]
              for tag, arrays in GATHER_PHASES.items()}
    reducer = _GradReducer()
    ws, ms, vs = ({n: _small_view(a[n], n) for n in SMALL} for a in (w, m, v))
    loss_part, grad_x, _, small = _device_step(x[0], mem[0], loss_target[0], _WeightGatherer(shards), ws, reducer)

    small_like = [ws[n] for n in SMALL] + [loss_part[0, :1]]
    pack = _pack_small([small[n] for n in SMALL] + [loss_part[0, :1]])
    everyone = _split_start("small_start", _everyone_copies, [pack], [((N_DEV,) + pack.shape, F32)], fanout=N_DEV - 1)

    grads, delta, new_m, new_v = {}, {}, {}, {}
    big_done = []

    def update(groups, reduced):
        for grp, red in zip(groups, reduced):
            row0 = 0
            for n in grp:
                w_n = shard_view(w[n], n)
                outs = _adamw("adamw_" + n, w_n, red, row0, shard_view(m[n], n), shard_view(v[n], n), after=everyone[-1:])
                grads[n], delta[n], new_m[n], new_v[n] = (shard_unview(o, n) for o in outs)
                big_done.append(outs[1])
                row0 += w_n.shape[0]

    n_a = len(reducer.reduced)
    update(REDUCE_GROUPS[:n_a], reducer.reduced)
    reducer.finish("b", list(big_done))
    reducer.join("b")
    update(REDUCE_GROUPS[n_a:], reducer.reduced[n_a:])

    send_sems, recv_sems, packs, landed, _ = everyone
    packs, landed = _split_wait("small_wait", _everyone_copies, send_sems, recv_sems, packs, landed, big_done)
    mine = 4 * lax.axis_index("x") + 2 * lax.axis_index("y") + lax.axis_index("c")
    summed = _sum_devices(lax.dynamic_update_slice(landed[0], packs[0][None], (mine, 0, 0)))
    g_small = dict(zip(SMALL + ("loss",), _unpack_small(summed, small_like)))
    loss = g_small.pop("loss").reshape(())
    def two_d(a):
        return a.reshape(-1, a.shape[-1])

    updated = _adamw_small(*([two_d(a[n]) for n in SMALL] for a in (ws, g_small, ms, vs)))
    for n, outs in zip(SMALL, updated):
        grads[n], delta[n], new_m[n], new_v[n] = (_small_view(o.reshape(ws[n].shape), n) for o in outs)

    return (loss, grad_x[None], *[grads[n] for n in WEIGHTS], *[delta[n] for n in WEIGHTS],
            *[new_m[n] for n in WEIGHTS], *[new_v[n] for n in WEIGHTS])
```

```python
import functools
import math

import jax
import jax.numpy as jnp
from jax import lax
from jax.experimental import pallas as pl
from jax.experimental.pallas import tpu as pltpu

F32 = jnp.float32
BF16 = jnp.bfloat16
SDS = jax.ShapeDtypeStruct
BS = pl.BlockSpec
MESH = pl.DeviceIdType.MESH

D_MODEL = 1024
D_FF = 2816
N_SHARD = 4
FF_SH = D_FF // N_SHARD
D_POOL = 512
POOL_WINDOWS = (2, 4, 8, 16)
POOL_GROUP = 128
D_SSM = 256
SSM_GROUPS = 16
SSM_GROUP = 16
SSM_STATE = 64
SSM_CH = SSM_GROUPS * SSM_STATE
N_HEADS = 4
HEAD_DIM = 256
EPS = 1e-6
ADAM_LR, ADAM_B1, ADAM_B2, ADAM_EPS, ADAM_WD, ADAM_STEP = 0.001, 0.9, 0.999, 1e-08, 0.01, 10

VMEM_LIMIT_V7X = 58 * 1024 * 1024
TM = 512

NN = (((1,), (0,)), ((), ()))
NT = (((1,), (1,)), ((), ()))
TN = (((0,), (0,)), ((), ()))


def _params(*sem):
    return pltpu.CompilerParams(dimension_semantics=sem if sem else None, vmem_limit_bytes=VMEM_LIMIT_V7X)


def _dot(a, b, dims=NN):
    return lax.dot_general(a.astype(BF16), b.astype(BF16), dims, preferred_element_type=F32)


def _sigmoid(v):
    return pl.reciprocal(1.0 + jnp.exp(-v), approx=True)


def _block_dims(spec):
    return tuple(d for d in spec.block_shape if d is not None)


def _after_operands(after):
    return list(after), [BS(memory_space=pl.ANY)] * len(after)


def _mm(name, pairs, *, grid, out_shape, out_spec, red_axis=None, extras=(), epilogue=None, after=()):
    n_pairs, n_extra = len(pairs), len(extras)
    n_red = grid[red_axis] if red_axis is not None else 1
    dims = [p[4] for p in pairs]

    def body(*refs):
        ab = refs[:2 * n_pairs]
        ex = refs[2 * n_pairs:2 * n_pairs + n_extra]
        o_ref = refs[2 * n_pairs + n_extra + len(after)]

        def partial():
            acc = None
            for p in range(n_pairs):
                t = _dot(ab[2 * p][...], ab[2 * p + 1][...], dims[p])
                acc = t if acc is None else acc + t
            return acc

        def finish(acc):
            res = epilogue(acc, *[e[...] for e in ex]) if epilogue is not None else acc
            o_ref[...] = res.astype(o_ref.dtype)

        if n_red == 1:
            finish(partial())
        else:
            acc_ref = refs[-1]
            k = pl.program_id(red_axis)

            @pl.when(k == 0)
            def _():
                acc_ref[...] = jnp.zeros_like(acc_ref)

            acc_ref[...] += partial()

            @pl.when(k == n_red - 1)
            def _():
                finish(acc_ref[...])

    operands, in_specs = [], []
    for a, a_spec, b, b_spec, _ in pairs:
        operands += [a, b]
        in_specs += [a_spec, b_spec]
    for e, e_spec in extras:
        operands.append(e)
        in_specs.append(e_spec)
    after_ops, after_specs = _after_operands(after)
    operands += after_ops
    in_specs += after_specs
    scratch = [pltpu.VMEM(_block_dims(out_spec), F32)] if n_red > 1 else []
    sem = tuple("arbitrary" if ax == red_axis else "parallel" for ax in range(len(grid)))
    return pl.pallas_call(body, out_shape=out_shape, grid=grid, in_specs=in_specs, out_specs=out_spec,
                          scratch_shapes=scratch, name=name, compiler_params=_params(*sem))(*operands)


def _rmsnorm(name, h, gain, tm, after=()):
    t, d = h.shape
    after_ops, after_specs = _after_operands(after)

    def body(h_ref, g_ref, *rest):
        u_ref = rest[-1]
        hv = h_ref[...]
        r = lax.rsqrt(jnp.mean(hv * hv, axis=-1, keepdims=True) + EPS)
        u_ref[...] = ((hv * r) * g_ref[...]).astype(u_ref.dtype)

    return pl.pallas_call(
        body, out_shape=SDS((t, d), BF16), grid=(t // tm,),
        in_specs=[BS((tm, d), lambda i: (i, 0)), BS((1, d), lambda i: (0, 0))] + after_specs,
        out_specs=BS((tm, d), lambda i: (i, 0)), name=name, compiler_params=_params("parallel"))(h, gain, *after_ops)


def _rmsnorm_bwd(name, h, gain, du, dh_in, tm):
    t, d = h.shape
    has_in = dh_in is not None

    def body(*refs):
        if has_in:
            h_ref, g_ref, du_ref, dhin_ref, dh_ref, dhb_ref, dg_ref = refs
        else:
            h_ref, g_ref, du_ref, dh_ref, dhb_ref, dg_ref = refs
        i = pl.program_id(0)
        hv = h_ref[...]
        r = lax.rsqrt(jnp.mean(hv * hv, axis=-1, keepdims=True) + EPS)
        n = hv * r
        duv = du_ref[...].astype(F32)
        dn = duv * g_ref[...]
        dh = r * (dn - n * jnp.mean(dn * n, axis=-1, keepdims=True))
        if has_in:
            dh = dhin_ref[...] + dh
        dh_ref[...] = dh
        dhb_ref[...] = dh.astype(BF16)

        @pl.when(i == 0)
        def _():
            dg_ref[...] = jnp.zeros_like(dg_ref)

        dg_ref[...] += jnp.sum(duv * n, axis=0, keepdims=True)

    row = BS((tm, d), lambda i: (i, 0))
    vec = BS((1, d), lambda i: (0, 0))
    operands = [h, gain, du] + ([dh_in] if has_in else [])
    in_specs = [row, vec, row] + ([row] if has_in else [])
    return pl.pallas_call(
        body, out_shape=(SDS((t, d), F32), SDS((t, d), BF16), SDS((1, d), F32)), grid=(t // tm,),
        in_specs=in_specs, out_specs=(row, row, vec), name=name, compiler_params=_params("arbitrary"))(*operands)


def _loss_head_tile(i, hv, g_ref, t_ref, loss_ref, dh_ref, dhb_ref, dg_ref):
    g = g_ref[...]
    r = lax.rsqrt(jnp.mean(hv * hv, axis=-1, keepdims=True) + EPS)
    n = hv * r
    err = n * g - t_ref[...]
    dy = err * (1.0 / hv.shape[-1])
    dn = dy * g
    dh = r * (dn - n * jnp.mean(dn * n, axis=-1, keepdims=True))
    dh_ref[...] = dh
    dhb_ref[...] = dh.astype(BF16)

    @pl.when(i == 0)
    def _():
        dg_ref[...] = jnp.zeros_like(dg_ref)
        loss_ref[...] = jnp.zeros_like(loss_ref)

    dg_ref[...] += jnp.sum(dy * n, axis=0, keepdims=True)
    part = 0.5 * jnp.sum(jnp.mean(err * err, axis=-1, keepdims=True), axis=0, keepdims=True)
    loss_ref[...] += jnp.broadcast_to(part, loss_ref.shape)


def _norm_tile(h, g_ref, u_ref):
    r = lax.rsqrt(jnp.mean(h * h, axis=-1, keepdims=True) + EPS)
    u_ref[...] = ((h * r) * g_ref[...]).astype(u_ref.dtype)


FFN_BLOCK = D_FF // 2


def _ffn_up(name, u, w_f, tm, after=()):
    t, d = u.shape
    after_ops, after_specs = _after_operands(after)

    def body(u_ref, wg_ref, wu_ref, *rest):
        pg_ref, pu_ref, a_ref = rest[len(after_ops):]
        uv = u_ref[...]
        for lo in range(0, D_FF, FFN_BLOCK):
            cols = slice(lo, lo + FFN_BLOCK)
            g = _dot(uv, wg_ref[cols, :], NT)
            up = _dot(uv, wu_ref[cols, :], NT)
            sg = _sigmoid(g)
            silu = g * sg
            a_ref[:, cols] = (silu * up).astype(BF16)
            pu_ref[:, cols] = (0.5 * silu).astype(BF16)
            pg_ref[:, cols] = (0.5 * sg * (1.0 + g * (1.0 - sg)) * up).astype(BF16)

    hid = BS((tm, D_FF), lambda i: (i, 0))
    shape = SDS((t, D_FF), BF16)
    whole = BS((D_FF, d), lambda i: (0, 0))
    return pl.pallas_call(
        body, out_shape=(shape, shape, shape), grid=(t // tm,),
        in_specs=[BS((tm, d), lambda i: (i, 0)), whole, whole] + after_specs,
        out_specs=(hid, hid, hid), name=name,
        compiler_params=_params("parallel"))(u, w_f["gate"], w_f["up"], *after_ops)


def _ffn_down(name, a, w_f, resid, tm, next_gain=None, head=None):
    t, d = resid.shape
    row = BS((tm, d), lambda i: (i, 0))
    vec = BS((1, d), lambda i: (0, 0))

    def body(a_ref, w_ref, res_ref, *rest):
        h = res_ref[...] + 0.5 * _dot(a_ref[...], w_ref[...])
        if head is not None:
            _loss_head_tile(pl.program_id(0), h, *rest)
        else:
            g_ref, h_ref, u_ref = rest
            h_ref[...] = h
            _norm_tile(h, g_ref, u_ref)

    if head is not None:
        extra, extra_specs = list(head), [vec, row]
        out_shape = (SDS((1, 128), F32), SDS((t, d), F32), SDS((t, d), BF16), SDS((1, d), F32))
        out_specs = (BS((1, 128), lambda i: (0, 0)), row, row, vec)
    else:
        extra, extra_specs = [next_gain], [vec]
        out_shape = (SDS((t, d), F32), SDS((t, d), BF16))
        out_specs = (row, row)
    return pl.pallas_call(
        body, out_shape=out_shape, grid=(t // tm,),
        in_specs=[BS((tm, D_FF), lambda i: (i, 0)), BS((D_FF, d), lambda i: (0, 0)), row] + extra_specs,
        out_specs=out_specs, name=name,
        compiler_params=_params("arbitrary" if head is not None else "parallel"))(a, w_f["down"], resid, *extra)


def _mix_in(u, w_t, tm, after=()):
    t, d = u.shape
    after_ops, after_specs = _after_operands(after)

    def body(u_ref, w_ref, *rest):
        o_ref, s_ref = rest[-2:]
        uv = u_ref[...]
        for lo in range(0, D_FF, FFN_BLOCK):
            o_ref[:, lo:lo + FFN_BLOCK] = _dot(uv, w_ref[lo:lo + FFN_BLOCK, :], NT)
        s_ref[...] = o_ref[:, D_POOL:D_POOL + D_SSM].astype(BF16)

    return pl.pallas_call(
        body, out_shape=(SDS((t, D_FF), F32), SDS((t, D_SSM), BF16)), grid=(t // tm,),
        in_specs=[BS((tm, d), lambda i: (i, 0)), BS((D_FF, d), lambda i: (0, 0))] + after_specs,
        out_specs=(BS((tm, D_FF), lambda i: (i, 0)), BS((tm, D_SSM), lambda i: (i, 0))), name="mix_in",
        compiler_params=_params("parallel"))(u, w_t, *after_ops)


def _mm_resid_norm(name, a, b, resid, next_gain, tm):
    t, d = resid.shape
    tm = min(2 * tm, t)

    def body(a_ref, b_ref, res_ref, g_ref, h_ref, u_ref):
        h = res_ref[...] + _dot(a_ref[...], b_ref[...])
        h_ref[...] = h
        _norm_tile(h, g_ref, u_ref)

    row = BS((tm, d), lambda i: (i, 0))
    return pl.pallas_call(
        body, out_shape=(SDS((t, d), F32), SDS((t, d), BF16)), grid=(t // tm,),
        in_specs=[BS((tm, a.shape[1]), lambda i: (i, 0)), BS(b.shape, lambda i: (0, 0)), row, BS((1, d), lambda i: (0, 0))],
        out_specs=(row, row), name=name, compiler_params=_params("parallel"))(a, b, resid, next_gain)


def _ffn_bwd_act(name, dh_b, w_f, pg, pu, tm, after=()):
    t, d = dh_b.shape
    after_ops, after_specs = _after_operands(after)

    def body(dh_ref, wd_ref, pg_ref, pu_ref, *rest):
        dg_ref, dup_ref = rest[len(after_ops):]
        dh = dh_ref[...]
        for lo in range(0, D_FF, FFN_BLOCK):
            cols = slice(lo, lo + FFN_BLOCK)
            da = _dot(dh, wd_ref[cols, :], NT)
            dg_ref[:, cols] = (da * pg_ref[:, cols].astype(F32)).astype(BF16)
            dup_ref[:, cols] = (da * pu_ref[:, cols].astype(F32)).astype(BF16)

    hid = BS((tm, D_FF), lambda i: (i, 0))
    shape = SDS((t, D_FF), BF16)
    return pl.pallas_call(
        body, out_shape=(shape, shape), grid=(t // tm,),
        in_specs=[BS((tm, d), lambda i: (i, 0)), BS((D_FF, d), lambda i: (0, 0)), hid, hid] + after_specs,
        out_specs=(hid, hid), name=name,
        compiler_params=_params("parallel"))(dh_b, w_f["down"], pg, pu, *after_ops)


def _ffn_dw(name, u, dg, dup, a, dh_b, tm):
    t, d = u.shape
    n_t = t // tm

    def body(u_ref, dg_ref, dup_ref, a_ref, dh_ref, og_ref, ou_ref, od_ref, acc):
        i = pl.program_id(1)

        @pl.when(i == 0)
        def _():
            acc[...] = jnp.zeros_like(acc)

        uv = u_ref[...]
        acc[0] += _dot(dg_ref[...], uv, TN)
        acc[1] += _dot(dup_ref[...], uv, TN)
        acc[2] += _dot(a_ref[...], dh_ref[...], TN)

        @pl.when(i == n_t - 1)
        def _():
            og_ref[...] = acc[0].astype(BF16)
            ou_ref[...] = acc[1].astype(BF16)
            od_ref[...] = (0.5 * acc[2]).astype(BF16)

    hid = BS((tm, FFN_BLOCK), lambda j, i: (i, j))
    row = BS((tm, d), lambda j, i: (i, 0))
    out = BS((FFN_BLOCK, d), lambda j, i: (j, 0))
    shape = SDS((D_FF, d), BF16)
    return pl.pallas_call(
        body, out_shape=(shape, shape, shape), grid=(D_FF // FFN_BLOCK, n_t),
        in_specs=[row, hid, hid, hid, row], out_specs=(out, out, out),
        scratch_shapes=[pltpu.VMEM((3, FFN_BLOCK, d), F32)],
        name=name, compiler_params=_params("parallel", "arbitrary"))(u, dg, dup, a, dh_b)


def _norm_bwd_tile(i, du, h_ref, g_ref, dhin_ref, dh_ref, dhb_ref, dg_ref):
    hv = h_ref[...]
    r = lax.rsqrt(jnp.mean(hv * hv, axis=-1, keepdims=True) + EPS)
    n = hv * r
    dn = du * g_ref[...]
    dh = dhin_ref[...] + r * (dn - n * jnp.mean(dn * n, axis=-1, keepdims=True))
    dh_ref[...] = dh
    dhb_ref[...] = dh.astype(BF16)

    @pl.when(i == 0)
    def _():
        dg_ref[...] = jnp.zeros_like(dg_ref)

    dg_ref[...] += jnp.sum(du * n, axis=0, keepdims=True)


def _norm_bwd_specs(tm):
    row = BS((tm, D_MODEL), lambda i: (i, 0))
    vec = BS((1, D_MODEL), lambda i: (0, 0))
    return [row, vec, row], (row, row, vec)


def _norm_bwd_shapes(t):
    return SDS((t, D_MODEL), F32), SDS((t, D_MODEL), BF16), SDS((1, D_MODEL), F32)


def _ffn_dx(name, dg, dup, w_f, h, gain, dh_in, tm, after=()):
    t = dg.shape[0]
    tm = tm // 2
    after_ops, after_specs = _after_operands(after)

    def body(dg_ref, dup_ref, wg_ref, wu_ref, h_ref, g_ref, dhin_ref, *rest):
        du = _dot(dg_ref[...], wg_ref[...]) + _dot(dup_ref[...], wu_ref[...])
        _norm_bwd_tile(pl.program_id(0), du, h_ref, g_ref, dhin_ref, *rest[len(after_ops):])

    hid = BS((tm, D_FF), lambda i: (i, 0))
    whole = BS((D_FF, D_MODEL), lambda i: (0, 0))
    norm_in, norm_out = _norm_bwd_specs(tm)
    return pl.pallas_call(
        body, out_shape=_norm_bwd_shapes(t), grid=(t // tm,),
        in_specs=[hid, hid, whole, whole] + norm_in + after_specs, out_specs=norm_out, name=name,
        compiler_params=_params("arbitrary"))(dg, dup, w_f["gate"], w_f["up"], h, gain, dh_in, *after_ops)


def _mm_norm_bwd(name, a, b, dims, h, gain, dh_in, tm):
    pieces = list(a) if isinstance(a, (list, tuple)) else [a]
    assert len(pieces) == 1 or dims == NN
    t = pieces[0].shape[0]
    widths = [p.shape[1] for p in pieces]
    row0 = [sum(widths[:j]) for j in range(len(pieces))]

    def body(*refs):
        a_refs, (b_ref, h_ref, g_ref, dhin_ref), outs = refs[:len(pieces)], refs[len(pieces):len(pieces) + 4], refs[len(pieces) + 4:]
        if len(pieces) == 1:
            du = _dot(a_refs[0][...], b_ref[...], dims)
        else:
            du = _dot(a_refs[0][...], b_ref[0:widths[0], :])
            for a_ref, r0, w in zip(a_refs[1:], row0[1:], widths[1:]):
                du = du + _dot(a_ref[...], b_ref[r0:r0 + w, :])
        _norm_bwd_tile(pl.program_id(0), du, h_ref, g_ref, dhin_ref, *outs)

    norm_in, norm_out = _norm_bwd_specs(tm)
    return pl.pallas_call(
        body, out_shape=_norm_bwd_shapes(t), grid=(t // tm,),
        in_specs=[BS((tm, w), lambda i: (i, 0)) for w in widths] + [BS(b.shape, lambda i: (0, 0))] + norm_in,
        out_specs=norm_out, name=name, compiler_params=_params("arbitrary"))(*pieces, b, h, gain, dh_in)


def _plain_mm(name, a, b, dims, out_dtype, tm, resid=None, after=()):
    t = a.shape[0]
    tm = min(2 * tm, t)
    n = b.shape[1] if dims == NN else b.shape[0]
    extras = [(resid, BS((tm, n), lambda i: (i, 0)))] if resid is not None else []
    epi = (lambda acc, res: res + acc) if resid is not None else None
    return _mm(name, [(a, BS((tm, a.shape[1]), lambda i: (i, 0)), b, BS(b.shape, lambda i: (0, 0)), dims)],
               grid=(t // tm,), out_shape=SDS((t, n), out_dtype), out_spec=BS((tm, n), lambda i: (i, 0)),
               extras=extras, epilogue=epi, after=after)


def _dw_mm(name, a, b, tm, out_dtype=BF16, after=()):
    t, k = a.shape
    n = b.shape[1]
    tm = min(2 * tm, t)
    return _mm(name, [(a, BS((tm, k), lambda i: (i, 0)), b, BS((tm, n), lambda i: (i, 0)), TN)],
               grid=(t // tm,), red_axis=0, out_shape=SDS((k, n), out_dtype), out_spec=BS((k, n), lambda i: (0, 0)),
               after=after)


def _mix_in_dw(pieces, u, tm):
    t, d = u.shape
    tm = min(2 * tm, t)
    n_steps = t // tm
    widths = [p.shape[1] for p in pieces]
    row0 = [sum(widths[:j]) for j in range(len(pieces))]
    assert sum(widths) == D_FF

    def body(*refs):
        p_refs, u_ref, o_ref, acc_ref = refs[:len(pieces)], refs[len(pieces)], refs[-2], refs[-1]
        k = pl.program_id(0)

        @pl.when(k == 0)
        def _():
            acc_ref[...] = jnp.zeros_like(acc_ref)

        uv = u_ref[...]
        for p_ref, r0, w in zip(p_refs, row0, widths):
            acc_ref[r0:r0 + w, :] += _dot(p_ref[...], uv, TN)

        @pl.when(k == n_steps - 1)
        def _():
            o_ref[...] = acc_ref[...].astype(BF16)

    return pl.pallas_call(
        body, out_shape=SDS((D_FF, d), BF16), grid=(n_steps,),
        in_specs=[BS((tm, w), lambda i: (i, 0)) for w in widths] + [BS((tm, d), lambda i: (i, 0))],
        out_specs=BS((D_FF, d), lambda i: (0, 0)), scratch_shapes=[pltpu.VMEM((D_FF, d), F32)],
        name="mix_in_dw", compiler_params=_params("arbitrary"))(*pieces, u)


POOL_CHUNK = 256
POOL_HALO = 8


def _window_sum(v, width, lead):
    n = v.shape[0]
    s = v
    k = 1
    while k < width:
        s = s + pltpu.roll(s, n - k, 0)
        k *= 2
    return pltpu.roll(s, lead, 0) if lead else s


def _pool_count(base, left, right, t, shape):
    pos = base + lax.broadcasted_iota(jnp.int32, shape, 0)
    lo = jnp.maximum(pos - left, 0)
    hi = jnp.minimum(pos + right + 1, t)
    return (hi - lo).astype(F32)


def _pool_fwd(proj, pool_w, pool_scale):
    t = proj.shape[0]
    c, h = POOL_CHUNK, POOL_HALO
    n_chunks = t // c

    def body(proj_hbm, pw_ref, sc_ref, pooled_ref, mixed_ref, ms_ref, pad_ref, sem):
        cp = pltpu.make_async_copy(proj_hbm.at[:, pl.ds(0, D_POOL)], pad_ref.at[pl.ds(h, t), :], sem)
        cp.start()
        pad_ref[pl.ds(0, h), :] = jnp.zeros((h, D_POOL), F32)
        pad_ref[pl.ds(t + h, h), :] = jnp.zeros((h, D_POOL), F32)
        cp.wait()
        for g, width in enumerate(POOL_WINDOWS):
            left = width // 2
            right = width - 1 - left
            cols = slice(g * POOL_GROUP, (g + 1) * POOL_GROUP)
            wmat = pw_ref[g].astype(BF16)
            scale = sc_ref[:, cols]

            def chunk(ci, carry, left=left, right=right, width=width, cols=cols, wmat=wmat, scale=scale):
                base = pl.multiple_of(ci * c, c)
                v = pad_ref[pl.ds(base, c + 2 * h), cols]
                win = _window_sum(v, width, left)[h:h + c]
                cnt = _pool_count(base, left, right, t, (c, POOL_GROUP))
                pooled = (win / cnt - v[h:h + c]).astype(BF16)
                mixed = _dot(pooled, wmat)
                pooled_ref[pl.ds(base, c), cols] = pooled
                mixed_ref[pl.ds(base, c), cols] = mixed.astype(BF16)
                ms_ref[pl.ds(base, c), cols] = (mixed * scale).astype(BF16)
                return carry

            lax.fori_loop(0, n_chunks, chunk, 0)

    vm = BS(memory_space=pltpu.VMEM)
    shape = SDS((t, D_POOL), BF16)
    return pl.pallas_call(
        body, out_shape=(shape, shape, shape),
        in_specs=[BS(memory_space=pl.ANY), vm, vm], out_specs=(vm, vm, vm),
        scratch_shapes=[pltpu.VMEM((t + 2 * h, D_POOL), F32), pltpu.SemaphoreType.DMA],
        name="pool_fwd", compiler_params=_params())(proj, pool_w, pool_scale)


def _pool_bwd(d_ms, mixed, pooled, pool_w, pool_scale):
    t = d_ms.shape[0]
    c, h = POOL_CHUNK, POOL_HALO
    n_chunks = t // c

    def body(dms_ref, mixed_ref, pooled_ref, pw_ref, sc_ref, dp_ref, dsc_ref, dpw_ref, pad_ref):
        pad_ref[pl.ds(0, h), :] = jnp.zeros((h, D_POOL), F32)
        pad_ref[pl.ds(t + h, h), :] = jnp.zeros((h, D_POOL), F32)
        for g, width in enumerate(POOL_WINDOWS):
            left = width // 2
            right = width - 1 - left
            cols = slice(g * POOL_GROUP, (g + 1) * POOL_GROUP)
            wmat = pw_ref[g].astype(BF16)
            scale = sc_ref[:, cols]

            def first(ci, carry, left=left, right=right, cols=cols, wmat=wmat, scale=scale):
                dsc, dpw = carry
                base = pl.multiple_of(ci * c, c)
                dms = dms_ref[pl.ds(base, c), cols].astype(F32)
                dsc = dsc + jnp.sum(dms * mixed_ref[pl.ds(base, c), cols].astype(F32), axis=0, keepdims=True)
                dmix = (dms * scale).astype(BF16)
                dpw = dpw + _dot(pooled_ref[pl.ds(base, c), cols], dmix, TN)
                dpooled = _dot(dmix, wmat, NT)
                cnt = _pool_count(base, left, right, t, (c, POOL_GROUP))
                pad_ref[pl.ds(base + h, c), cols] = dpooled / cnt
                return dsc, dpw

            dsc, dpw = lax.fori_loop(0, n_chunks, first,
                                     (jnp.zeros((1, POOL_GROUP), F32), jnp.zeros((POOL_GROUP, POOL_GROUP), F32)))
            dsc_ref[:, cols] = dsc
            dpw_ref[g] = dpw

            def second(ci, carry, left=left, right=right, width=width, cols=cols):
                base = pl.multiple_of(ci * c, c)
                v = pad_ref[pl.ds(base, c + 2 * h), cols]
                win = _window_sum(v, width, right)[h:h + c]
                cnt = _pool_count(base, left, right, t, (c, POOL_GROUP))
                dp_ref[pl.ds(base, c), cols] = (win - v[h:h + c] * cnt).astype(BF16)
                return carry

            lax.fori_loop(0, n_chunks, second, 0)

    vm = BS(memory_space=pltpu.VMEM)
    return pl.pallas_call(
        body, out_shape=(SDS((t, D_POOL), BF16), SDS((1, D_POOL), F32), SDS((4, POOL_GROUP, POOL_GROUP), F32)),
        in_specs=[vm] * 5, out_specs=(vm, vm, vm),
        scratch_shapes=[pltpu.VMEM((t + 2 * h, D_POOL), F32)],
        name="pool_bwd", compiler_params=_params())(d_ms, mixed, pooled, pool_w, pool_scale)


SSM_ROWS = 2 * SSM_GROUPS * SSM_GROUP
SSM_HALF = SSM_GROUPS * SSM_GROUP


def _ssm_zoh(a_r, a_i, ldt):
    dt = jnp.exp(ldt)
    mag = jnp.exp(dt * a_r)
    ang = dt * a_i
    cs, sn = jnp.cos(ang), jnp.sin(ang)
    abr, abi = mag * cs, mag * sn
    den = a_r * a_r + a_i * a_i
    nr = abr - 1.0
    qr = (nr * a_r + abi * a_i) / den
    qi = (abi * a_r - nr * a_i) / den
    return dt, mag, cs, sn, abr, abi, den, nr, qr, qi


def _ssm_group_mask():
    row = lax.broadcasted_iota(jnp.int32, (SSM_HALF, SSM_CH), 0)
    col = lax.broadcasted_iota(jnp.int32, (SSM_HALF, SSM_CH), 1)
    return (row // SSM_GROUP) == (col // SSM_STATE)


def _ssm_prep(a_r, a_i, ldt, b_r, b_i, c_r, c_i, after=()):
    after_ops, after_specs = _after_operands(after)

    def body(ar_ref, ai_ref, ldt_ref, br_ref, bi_ref, cr_ref, ci_ref, *rest):
        abr_ref, abi_ref, win_ref, wint_ref, woutt_ref, wout_ref = rest[len(after_ops):]
        *_, abr, abi, _, _, qr, qi = _ssm_zoh(ar_ref[...], ai_ref[...], ldt_ref[...])
        abr_ref[...] = abr
        abi_ref[...] = abi
        b_r, b_i = br_ref[...], bi_ref[...]
        bbr = qr * b_r - qi * b_i
        bbi = qr * b_i + qi * b_r
        mask = _ssm_group_mask()
        state = lax.broadcasted_iota(jnp.int32, (SSM_STATE, SSM_CH), 0)
        col = lax.broadcasted_iota(jnp.int32, (SSM_STATE, SSM_CH), 1)
        every_group = (col % SSM_STATE == state).astype(BF16)

        def spread(x):
            return jnp.where(mask, _dot(x, every_group), 0.0)

        for d in range(2):
            rows = slice(d * SSM_HALF, (d + 1) * SSM_HALF)
            for half, x_in, x_out in ((0, bbr[rows], cr_ref[rows, :]), (1, bbi[rows], -ci_ref[rows, :])):
                cols = slice(half * SSM_CH, (half + 1) * SSM_CH)
                m_in, m_out = spread(x_in), spread(x_out)
                win_ref[d, :, cols] = m_in.astype(BF16)
                wint_ref[d, cols, :] = m_in.T.astype(BF16)
                woutt_ref[d, :, cols] = m_out.astype(BF16)
                wout_ref[d, cols, :] = m_out.T.astype(BF16)

    vm = BS(memory_space=pltpu.VMEM)
    vec = SDS((SSM_ROWS, SSM_STATE), F32)
    wide = SDS((2, SSM_HALF, 2 * SSM_CH), BF16)
    tall = SDS((2, 2 * SSM_CH, SSM_HALF), BF16)
    return pl.pallas_call(body, out_shape=(vec, vec, wide, tall, wide, tall), in_specs=[vm] * 7 + after_specs,
                          out_specs=(vm,) * 6, name="ssm_prep",
                          compiler_params=_params())(a_r, a_i, ldt, b_r, b_i, c_r, c_i, *after_ops)


def _ssm_prep_bwd(a_r, a_i, ldt, b_r, b_i, d_abr, d_abi, d_win, d_woutt):
    def body(ar_ref, ai_ref, ldt_ref, br_ref, bi_ref, *rest):
        (dabr_refs, dabi_refs, dwin_refs, dwoutt_refs), outs = [rest[2 * k:2 * k + 2] for k in range(4)], rest[8:]
        dar_ref, dai_ref, dldt_ref, dbr_ref, dbi_ref, dcr_ref, dci_ref = outs
        a_r, a_i = ar_ref[...], ai_ref[...]
        dt, mag, cs, sn, abr, abi, den, nr, qr, qi = _ssm_zoh(a_r, a_i, ldt_ref[...])
        mask = _ssm_group_mask()
        col = lax.broadcasted_iota(jnp.int32, (SSM_CH, SSM_STATE), 0)
        state = lax.broadcasted_iota(jnp.int32, (SSM_CH, SSM_STATE), 1)
        own_state = (col % SSM_STATE == state).astype(BF16)

        def pick(dense):
            m = jnp.where(mask, dense, 0.0)
            hi = m.astype(BF16)
            lo = m - hi.astype(F32)
            return _dot(hi, own_state) + _dot(lo, own_state)

        def picked(refs, half):
            cols = slice(half * SSM_CH, (half + 1) * SSM_CH)
            return jnp.concatenate([pick(ref[:, cols]) for ref in refs], axis=0)

        first_channel = lax.broadcasted_iota(jnp.int32, (SSM_HALF, SSM_CH), 0) % SSM_GROUP == 0

        def first_rows(refs):
            return jnp.concatenate(
                [pick(jnp.where(first_channel, jnp.broadcast_to(ref[...], (SSM_HALF, SSM_CH)), 0.0)) for ref in refs], axis=0)

        g_r, g_i = picked(dwin_refs, 0), picked(dwin_refs, 1)
        dcr_ref[...] = picked(dwoutt_refs, 0)
        dci_ref[...] = -picked(dwoutt_refs, 1)
        b_r, b_i = br_ref[...], bi_ref[...]
        dbr_ref[...] = g_r * qr + g_i * qi
        dbi_ref[...] = g_i * qr - g_r * qi
        gqr = g_r * b_r + g_i * b_i
        gqi = g_i * b_r - g_r * b_i
        g_nr_num = gqr / den
        g_ni_num = gqi / den
        g_den = -(gqr * qr + gqi * qi) / den
        g_nr = g_nr_num * a_r - g_ni_num * a_i
        g_abi = g_nr_num * a_i + g_ni_num * a_r
        d_ar = g_nr_num * nr + g_ni_num * abi + 2.0 * a_r * g_den
        d_ai = g_nr_num * abi - g_ni_num * nr + 2.0 * a_i * g_den
        g_abr = first_rows(dabr_refs) + g_nr
        g_abi = first_rows(dabi_refs) + g_abi
        g_mag = g_abr * cs + g_abi * sn
        g_ang = mag * (g_abi * cs - g_abr * sn)
        g_e = g_mag * mag
        d_ar = d_ar + g_e * dt
        d_ai = d_ai + g_ang * dt
        g_dt = g_e * a_r + g_ang * a_i
        dar_ref[...] = d_ar
        dai_ref[...] = d_ai
        dldt_ref[...] = g_dt * dt

    vm = BS(memory_space=pltpu.VMEM)
    vec = SDS((SSM_ROWS, SSM_STATE), F32)
    return pl.pallas_call(body, out_shape=(vec,) * 7, in_specs=[vm] * 13, out_specs=(vm,) * 7, name="ssm_prep_bwd",
                          compiler_params=_params())(a_r, a_i, ldt, b_r, b_i, *d_abr, *d_abi, *d_win, *d_woutt)


SCAN_ROWS = 512
SCAN_SUB = 128


def _ssm_scan(name, inp, w1, a_r, a_i, w2, reverse, dr, conj=False):
    t = inp.shape[0]
    rows = min(SCAN_ROWS, t)
    n = t // rows
    n_sub = rows // SCAN_SUB
    ch = SSM_CH
    at = (lambda i: (n - 1 - i, 0)) if reverse else (lambda i: (i, 0))

    def body(in_ref, w1_ref, ar_ref, ai_ref, w2_ref, sb_ref, out_ref, cr_ref, ci_ref, k_ref, st_ref):
        i = pl.program_id(0)

        @pl.when(i == 0)
        def _():
            ar8 = jnp.broadcast_to(ar_ref[...], (8, ch))
            ai8 = jnp.broadcast_to(-ai_ref[...] if conj else ai_ref[...], (8, ch))
            row = lax.broadcasted_iota(jnp.int32, (8, ch), 0)
            rank = (7 - row) if reverse else row
            powers = [(ar8, ai8)]
            for _ in range(7):
                p_r, p_i = powers[-1]
                powers.append((p_r * ar8 - p_i * ai8, p_r * ai8 + p_i * ar8))
            zero = jnp.zeros((8, ch), F32)
            for slot, k in enumerate((1, 2, 4)):
                k_ref[2 * slot] = jnp.where(rank >= k, powers[k - 1][0], zero)
                k_ref[2 * slot + 1] = jnp.where(rank >= k, powers[k - 1][1], zero)
            carry_r, carry_i = zero, zero
            for j in range(8):
                carry_r = jnp.where(rank == j, powers[j][0], carry_r)
                carry_i = jnp.where(rank == j, powers[j][1], carry_i)
            k_ref[6] = carry_r
            k_ref[7] = carry_i
            cr_ref[...] = zero
            ci_ref[...] = zero

        def group(r0, carry):
            c_r, c_i = carry
            x_r = st_ref[pl.ds(r0, 8), 0:ch]
            x_i = st_ref[pl.ds(r0, 8), ch:2 * ch]
            for slot, k in enumerate((1, 2, 4)):
                shift = (8 - k) if reverse else k
                s_r = pltpu.roll(x_r, shift, 0)
                s_i = pltpu.roll(x_i, shift, 0)
                m_r, m_i = k_ref[2 * slot], k_ref[2 * slot + 1]
                x_r, x_i = x_r + m_r * s_r - m_i * s_i, x_i + m_r * s_i + m_i * s_r
            p_r, p_i = k_ref[6], k_ref[7]
            x_r, x_i = x_r + p_r * c_r - p_i * c_i, x_i + p_r * c_i + p_i * c_r
            st_ref[pl.ds(r0, 8), 0:ch] = x_r
            st_ref[pl.ds(r0, 8), ch:2 * ch] = x_i
            last = 0 if reverse else 7
            return (jnp.broadcast_to(x_r[last:last + 1, :], (8, ch)), jnp.broadcast_to(x_i[last:last + 1, :], (8, ch)))

        carry = (cr_ref[...], ci_ref[...])
        for sc in (range(n_sub - 1, -1, -1) if reverse else range(n_sub)):
            part = pl.ds(sc * SCAN_SUB, SCAN_SUB)
            st_ref[part, :] = _dot(in_ref[part, :], w1_ref[...])
            for gi in range(SCAN_SUB // 8):
                g = (SCAN_SUB // 8 - 1 - gi) if reverse else gi
                carry = group(sc * SCAN_SUB + g * 8, carry)
            states = st_ref[part, :].astype(BF16)
            sb_ref[part, :] = states
            out_ref[part, :] = _dot(states, w2_ref[...])
        cr_ref[...] = carry[0]
        ci_ref[...] = carry[1]

    return pl.pallas_call(
        body, out_shape=(SDS((t, 2 * ch), BF16), SDS((t, D_SSM), F32)), grid=(n,),
        in_specs=[BS((rows, D_SSM), at), BS((None, D_SSM, 2 * ch), lambda i: (dr, 0, 0)), BS((None, 1, ch), lambda i: (dr, 0, 0)),
                  BS((None, 1, ch), lambda i: (dr, 0, 0)), BS((None, 2 * ch, D_SSM), lambda i: (dr, 0, 0))],
        out_specs=(BS((rows, 2 * ch), at), BS((rows, D_SSM), at)),
        scratch_shapes=[pltpu.VMEM((8, ch), F32), pltpu.VMEM((8, ch), F32), pltpu.VMEM((8, 8, ch), F32),
                        pltpu.VMEM((rows, 2 * ch), F32)],
        name=name, compiler_params=_params("arbitrary"))(inp, w1, a_r, a_i, w2)


DA_ROWS = 1024


def _ssm_param_grads(name, lam, states, u, dy, reverse, after=()):
    t = lam.shape[0]
    rows = min(DA_ROWS, t)
    n = t // rows
    halo_rows = 16
    nb = rows // halo_rows
    ch = SSM_CH
    if reverse:
        halo_at = lambda i: (jnp.minimum((i + 1) * nb, t // halo_rows - 1), 0)
    else:
        halo_at = lambda i: (jnp.maximum(i * nb - 1, 0), 0)

    after_ops, after_specs = _after_operands(after)

    def body(lam_ref, x_ref, halo_ref, u_ref, dy_ref, *rest):
        dr_ref, di_ref, dwin_ref, dwoutt_ref = rest[len(after_ops):]
        i = pl.program_id(0)

        @pl.when(i == 0)
        def _():
            dr_ref[...] = jnp.zeros_like(dr_ref)
            di_ref[...] = jnp.zeros_like(di_ref)
            dwin_ref[...] = jnp.zeros_like(dwin_ref)
            dwoutt_ref[...] = jnp.zeros_like(dwoutt_ref)

        dwin_ref[...] += _dot(u_ref[...], lam_ref[...], TN)
        dwoutt_ref[...] += _dot(dy_ref[...], x_ref[...], TN)
        row = lax.broadcasted_iota(jnp.int32, (rows, ch), 0)
        if reverse:
            edge, shift, h_row, live = rows - 1, rows - 1, 0, i < n - 1
        else:
            edge, shift, h_row, live = 0, 1, halo_rows - 1, i > 0

        def neighbour(lo):
            halo = halo_ref[:, lo:lo + ch].astype(F32)[h_row:h_row + 1]
            halo = jnp.where(live, halo, 0.0)
            x = x_ref[:, lo:lo + ch].astype(F32)
            return jnp.where(row == edge, jnp.broadcast_to(halo, (rows, ch)), pltpu.roll(x, shift, 0))

        xp_r, xp_i = neighbour(0), neighbour(ch)
        l_r, l_i = lam_ref[:, 0:ch].astype(F32), lam_ref[:, ch:2 * ch].astype(F32)
        dr_ref[...] += jnp.sum(l_r * xp_r + l_i * xp_i, axis=0, keepdims=True)
        di_ref[...] += jnp.sum(l_i * xp_r - l_r * xp_i, axis=0, keepdims=True)

    blk = BS((rows, 2 * ch), lambda i: (i, 0))
    thin = BS((rows, D_SSM), lambda i: (i, 0))
    vec = BS((1, ch), lambda i: (0, 0))
    mat = BS((D_SSM, 2 * ch), lambda i: (0, 0))
    return pl.pallas_call(
        body, out_shape=(SDS((1, ch), F32), SDS((1, ch), F32), SDS((D_SSM, 2 * ch), F32), SDS((D_SSM, 2 * ch), F32)),
        grid=(n,), in_specs=[blk, blk, BS((halo_rows, 2 * ch), halo_at), thin, thin] + after_specs,
        out_specs=(vec, vec, mat, mat),
        name=name, compiler_params=_params("arbitrary"))(lam, states, states, u, dy, *after_ops)


GELU_C = math.sqrt(2.0 / math.pi)
GELU_K = 0.044715


def _ssm_combine(proj, y_fwd, y_bwd, d_skip, tm, after=()):
    t = proj.shape[0]
    after_ops, after_specs = _after_operands(after)

    def body(s_ref, yf_ref, yb_ref, d_ref, *rest):
        yt_ref, g_ref = rest[len(after_ops):]
        y = s_ref[...] * d_ref[...] + yf_ref[...] + yb_ref[...]
        yt_ref[...] = y
        th = jnp.tanh(GELU_C * (y + GELU_K * y * y * y))
        g_ref[...] = (0.5 * y * (1.0 + th)).astype(BF16)

    blk = BS((tm, D_SSM), lambda i: (i, 0))
    return pl.pallas_call(
        body, out_shape=(SDS((t, D_SSM), F32), SDS((t, D_SSM), BF16)), grid=(t // tm,),
        in_specs=[BS((tm, D_SSM), lambda i: (i, D_POOL // D_SSM)), blk, blk, BS((1, D_SSM), lambda i: (0, 0))] + after_specs,
        out_specs=(blk, blk), name="ssm_combine",
        compiler_params=_params("parallel"))(proj, y_fwd, y_bwd, d_skip, *after_ops)


def _ssm_ds(proj, d_yt, du_fwd, du_bwd, d_skip, tm):
    t = proj.shape[0]

    def body(s_ref, dy_ref, duf_ref, dub_ref, d_ref, ds_ref, dd_ref):
        i = pl.program_id(0)
        dy = dy_ref[...]
        ds_ref[...] = (dy * d_ref[...] + duf_ref[...] + dub_ref[...]).astype(BF16)

        @pl.when(i == 0)
        def _():
            dd_ref[...] = jnp.zeros_like(dd_ref)

        dd_ref[...] += jnp.sum(dy * s_ref[...], axis=0, keepdims=True)

    blk = BS((tm, D_SSM), lambda i: (i, 0))
    vec = BS((1, D_SSM), lambda i: (0, 0))
    return pl.pallas_call(
        body, out_shape=(SDS((t, D_SSM), BF16), SDS((1, D_SSM), F32)), grid=(t // tm,),
        in_specs=[BS((tm, D_SSM), lambda i: (i, D_POOL // D_SSM)), blk, blk, blk, vec],
        out_specs=(blk, vec), name="ssm_ds", compiler_params=_params("arbitrary"))(proj, d_yt, du_fwd, du_bwd, d_skip)


G_POOL_AT = D_POOL + D_SSM
G_SSM_AT = G_POOL_AT + D_MODEL
E_VAL, E_GATE = D_POOL, D_POOL + D_SSM


def _merge_specs(tm):
    return [BS((tm, D_POOL), lambda i: (i, 0)), BS((tm, D_SSM), lambda i: (i, 0)),
            BS((N_SHARD, 1024, 256), lambda i: (0, 0, 0)), BS((tm, D_FF), lambda i: (i, 0))]


def _merge_parts(s, ms, yv, w_ref, proj_ref):
    lo = 256 * s
    zp = _dot(ms, w_ref[s, 0:E_VAL, :])
    zv = _dot(yv, w_ref[s, E_VAL:E_GATE, :])
    zg = _dot(yv, w_ref[s, E_GATE:, :])
    return zp, zv, zg, proj_ref[:, G_POOL_AT + lo:G_POOL_AT + lo + 256], proj_ref[:, G_SSM_AT + lo:G_SSM_AT + lo + 256]


def _mixer_merge(ms, yssm, w_e, proj, tm):
    t = ms.shape[0]

    def body(ms_ref, y_ref, w_ref, proj_ref, o_ref):
        msv, yv = ms_ref[...], y_ref[...]
        for s in range(N_SHARD):
            zp, zv, zg, gp, gs = _merge_parts(s, msv, yv, w_ref, proj_ref)
            o_ref[:, 256 * s:256 * (s + 1)] = (_sigmoid(gp) * zp + _sigmoid(gs) * zv * _sigmoid(zg)).astype(BF16)

    row = BS((tm, D_MODEL), lambda i: (i, 0))
    return pl.pallas_call(
        body, out_shape=SDS((t, D_MODEL), BF16), grid=(t // tm,), in_specs=_merge_specs(tm), out_specs=row,
        name="mixer_merge", compiler_params=_params("parallel"))(ms, yssm, w_e, proj)


def _mixer_merge_bwd(ms, yssm, w_e, proj, dmerged, tm):
    t = ms.shape[0]

    def body(ms_ref, y_ref, w_ref, proj_ref, dm_ref, dgp_ref, dgs_ref, dzp_ref, dzv_ref, dzg_ref):
        msv, yv = ms_ref[...], y_ref[...]
        for s in range(N_SHARD):
            cols = slice(256 * s, 256 * (s + 1))
            zp, zv, zg, gp, gs = _merge_parts(s, msv, yv, w_ref, proj_ref)
            dm = dm_ref[:, cols].astype(F32)
            sp, ss, sg = _sigmoid(gp), _sigmoid(gs), _sigmoid(zg)
            dgp_ref[:, cols] = (dm * zp * sp * (1.0 - sp)).astype(BF16)
            dgs_ref[:, cols] = (dm * zv * sg * ss * (1.0 - ss)).astype(BF16)
            dzp_ref[:, cols] = (dm * sp).astype(BF16)
            dz = dm * ss
            dzv_ref[:, cols] = (dz * sg).astype(BF16)
            dzg_ref[:, cols] = (dz * zv * sg * (1.0 - sg)).astype(BF16)

    row = BS((tm, D_MODEL), lambda i: (i, 0))
    shape = SDS((t, D_MODEL), BF16)
    return pl.pallas_call(
        body, out_shape=(shape,) * 5, grid=(t // tm,), in_specs=_merge_specs(tm) + [row],
        out_specs=(row,) * 5, name="mixer_merge_bwd",
        compiler_params=_params("parallel"))(ms, yssm, w_e, proj, dmerged)


def _mixer_dw(ms, yssm, dzp, dzv, dzg, tm):
    t = ms.shape[0]
    tm = min(2 * tm, t)
    n_t = t // tm

    def body(ms_ref, y_ref, dzp_ref, dzv_ref, dzg_ref, o_ref, acc):
        i = pl.program_id(0)

        @pl.when(i == 0)
        def _():
            acc[...] = jnp.zeros_like(acc)

        msv, yv = ms_ref[...], y_ref[...]
        for s in range(N_SHARD):
            cols = slice(256 * s, 256 * (s + 1))
            acc[s, 0:E_VAL, :] += _dot(msv, dzp_ref[:, cols], TN)
            acc[s, E_VAL:E_GATE, :] += _dot(yv, dzv_ref[:, cols], TN)
            acc[s, E_GATE:, :] += _dot(yv, dzg_ref[:, cols], TN)

        @pl.when(i == n_t - 1)
        def _():
            o_ref[...] = acc[...].astype(BF16)

    row = BS((tm, D_MODEL), lambda i: (i, 0))
    full = BS((N_SHARD, 1024, 256), lambda i: (0, 0, 0))
    return pl.pallas_call(
        body, out_shape=SDS((N_SHARD, 1024, 256), BF16), grid=(n_t,),
        in_specs=[BS((tm, D_POOL), lambda i: (i, 0)), BS((tm, D_SSM), lambda i: (i, 0)), row, row, row],
        out_specs=full, scratch_shapes=[pltpu.VMEM((N_SHARD, 1024, 256), F32)],
        name="mixer_dw", compiler_params=_params("arbitrary"))(ms, yssm, dzp, dzv, dzg)


def _mixer_dx(dzp, dzv, dzg, w_e, y_total, tm):
    t = dzp.shape[0]

    def body(dzp_ref, dzv_ref, dzg_ref, w_ref, yt_ref, dms_ref, dy_ref, dyb_ref):
        acc_ms, acc_y = None, None
        for s in range(N_SHARD):
            cols = slice(256 * s, 256 * (s + 1))
            part_ms = _dot(dzp_ref[:, cols], w_ref[s, 0:E_VAL, :], NT)
            part_y = _dot(dzv_ref[:, cols], w_ref[s, E_VAL:E_GATE, :], NT) + _dot(dzg_ref[:, cols], w_ref[s, E_GATE:, :], NT)
            acc_ms = part_ms if s == 0 else acc_ms + part_ms
            acc_y = part_y if s == 0 else acc_y + part_y
        dms_ref[...] = acc_ms.astype(BF16)
        y = yt_ref[...]
        th = jnp.tanh(GELU_C * (y + GELU_K * y * y * y))
        dgelu = 0.5 * (1.0 + th) + 0.5 * y * (1.0 - th * th) * GELU_C * (1.0 + 3.0 * GELU_K * y * y)
        dy = acc_y * dgelu
        dy_ref[...] = dy
        dyb_ref[...] = dy.astype(BF16)

    row = BS((tm, D_MODEL), lambda i: (i, 0))
    narrow = BS((tm, D_SSM), lambda i: (i, 0))
    return pl.pallas_call(
        body, out_shape=(SDS((t, D_POOL), BF16), SDS((t, D_SSM), F32), SDS((t, D_SSM), BF16)), grid=(t // tm,),
        in_specs=[row, row, row, BS((N_SHARD, 1024, 256), lambda i: (0, 0, 0)), narrow],
        out_specs=(BS((tm, D_POOL), lambda i: (i, 0)), narrow, narrow),
        name="mixer_dx", compiler_params=_params("parallel"))(dzp, dzv, dzg, w_e, y_total)


def _attn_probs(q_h, k_h):
    s = _dot(q_h, k_h, NT) * (1.0 / math.sqrt(HEAD_DIM))
    e = jnp.exp(s - jnp.max(s, axis=-1, keepdims=True))
    return e / jnp.sum(e, axis=-1, keepdims=True)


def _attn_fwd(q, kv, tm):
    t = q.shape[0]
    tm = min(2 * tm, t)
    m = kv.shape[0]

    def body(q_ref, kv_ref, o_ref):
        for hd in range(N_HEADS):
            lo = hd * HEAD_DIM
            p = _attn_probs(q_ref[:, lo:lo + HEAD_DIM], kv_ref[:, lo:lo + HEAD_DIM])
            o_ref[:, lo:lo + HEAD_DIM] = _dot(p, kv_ref[:, D_MODEL + lo:D_MODEL + lo + HEAD_DIM]).astype(BF16)

    return pl.pallas_call(
        body, out_shape=SDS((t, D_MODEL), BF16), grid=(t // tm,),
        in_specs=[BS((tm, D_MODEL), lambda i: (i, 0)), BS((m, 2 * D_MODEL), lambda i: (0, 0))],
        out_specs=BS((tm, D_MODEL), lambda i: (i, 0)), name="attn_fwd", compiler_params=_params("parallel"))(q, kv)


def _attn_bwd(q, kv, d_o, tm):
    t = q.shape[0]
    m = kv.shape[0]

    def body(q_ref, kv_ref, do_ref, dq_ref, dkv_ref):
        i = pl.program_id(0)

        @pl.when(i == 0)
        def _():
            dkv_ref[...] = jnp.zeros_like(dkv_ref)

        for hd in range(N_HEADS):
            lo = hd * HEAD_DIM
            q_h = q_ref[:, lo:lo + HEAD_DIM]
            k_h = kv_ref[:, lo:lo + HEAD_DIM]
            v_h = kv_ref[:, D_MODEL + lo:D_MODEL + lo + HEAD_DIM]
            do_h = do_ref[:, lo:lo + HEAD_DIM]
            p = _attn_probs(q_h, k_h)
            dkv_ref[:, D_MODEL + lo:D_MODEL + lo + HEAD_DIM] += _dot(p, do_h, TN)
            dp = _dot(do_h, v_h, NT)
            ds = p * (dp - jnp.sum(dp * p, axis=-1, keepdims=True)) * (1.0 / math.sqrt(HEAD_DIM))
            dq_ref[:, lo:lo + HEAD_DIM] = _dot(ds, k_h).astype(BF16)
            dkv_ref[:, lo:lo + HEAD_DIM] += _dot(ds, q_h, TN)

    row = BS((tm, D_MODEL), lambda i: (i, 0))
    full = BS((m, 2 * D_MODEL), lambda i: (0, 0))
    return pl.pallas_call(
        body, out_shape=(SDS((t, D_MODEL), BF16), SDS((m, 2 * D_MODEL), F32)), grid=(t // tm,),
        in_specs=[row, full, row], out_specs=(row, full), name="attn_bwd",
        compiler_params=_params("arbitrary"))(q, kv, d_o)


TRANSPOSED = ("ffn1_w_gate", "ffn1_w_up", "ffn2_w_gate", "ffn2_w_up", "w_in")
GATHER_PHASES = {"f1a": (("ffn1_w_gate",), ("ffn1_w_up",)),
                 "f1b": (("ffn1_w_down",),),
                 "win": (("w_in",),),
                 "mix": (("w_mix_out",), ("w_q",), ("w_xo",), ("w_kv",), ("w_pool_proj", "w_glu_val", "w_glu_gate")),
                 "f2": (("ffn2_w_gate",), ("ffn2_w_up",), ("ffn2_w_down",))}
REDUCE_GROUPS = (("ffn2_w_gate",), ("ffn2_w_up",), ("ffn2_w_down",), ("w_xo",), ("w_q",), ("w_kv",), ("w_mix_out",),
                 ("w_pool_proj", "w_glu_val", "w_glu_gate"), ("w_in",), ("ffn1_w_gate",), ("ffn1_w_up",), ("ffn1_w_down",))
SMALL = ("ffn1_norm", "mix_norm", "pool_w", "pool_scale", "ssm_a_re", "ssm_a_im", "ssm_log_dt", "ssm_b_re",
         "ssm_b_im", "ssm_c_re", "ssm_c_im", "ssm_d", "xattn_norm", "mem_norm", "ffn2_norm", "final_norm")
WEIGHTS = ("ffn1_norm", "ffn1_w_gate", "ffn1_w_up", "ffn1_w_down", "mix_norm", "w_in", "pool_w", "pool_scale",
           "w_pool_proj", "ssm_a_re", "ssm_a_im", "ssm_log_dt", "ssm_b_re", "ssm_b_im", "ssm_c_re", "ssm_c_im",
           "ssm_d", "w_glu_val", "w_glu_gate", "w_mix_out", "xattn_norm", "mem_norm", "w_q", "w_kv", "w_xo",
           "ffn2_norm", "ffn2_w_gate", "ffn2_w_up", "ffn2_w_down", "final_norm")


def _small_view(a, n):
    return jnp.swapaxes(a, 3, 4) if n in ("ssm_b_re", "ssm_b_im") else a


def _device_step(x, mem, target, wts, sp, reducer=None):
    t = x.shape[0]
    tm = min(TM, t)
    g = {}

    first_gather = wts.start("win", wts.start("f1b", wts.start("f1a")))
    u1 = _rmsnorm("norm_ffn1", x, sp["ffn1_norm"], tm, after=first_gather)

    def per_channel(a):
        a = a.reshape(2 * SSM_GROUPS, 1, -1)
        return jnp.broadcast_to(a, (2 * SSM_GROUPS, SSM_GROUP, a.shape[-1])).reshape(SSM_ROWS, a.shape[-1])

    ssm_a = per_channel(sp["ssm_a_re"]), per_channel(sp["ssm_a_im"]), per_channel(sp["ssm_log_dt"])
    ssm_b = sp["ssm_b_re"].reshape(SSM_ROWS, SSM_STATE), sp["ssm_b_im"].reshape(SSM_ROWS, SSM_STATE)
    abr, abi, w_in_s, w_in_s_t, w_out_s_t, w_out_s = _ssm_prep(
        *ssm_a, *ssm_b, sp["ssm_c_re"].reshape(SSM_ROWS, SSM_STATE), sp["ssm_c_im"].reshape(SSM_ROWS, SSM_STATE),
        after=first_gather)
    first_rows = (2, SSM_GROUPS, SSM_GROUP, SSM_STATE)
    a_r = abr.reshape(first_rows)[:, :, 0].reshape(2, 1, SSM_CH)
    a_i = abi.reshape(first_rows)[:, :, 0].reshape(2, 1, SSM_CH)
    mem_n = _rmsnorm("norm_mem", mem, sp["mem_norm"], mem.shape[0], after=first_gather + wts.sources(("mix", "f2")))

    whole = (D_FF, D_MODEL)
    w_g1, w_u1 = wts.finish("f1a", [u1, w_in_s, w_in_s_t, w_out_s, w_out_s_t, a_r, a_i, mem_n])
    w_f1 = {"gate": w_g1.reshape(whole), "up": w_u1.reshape(whole)}
    g1, up1, a1 = _ffn_up("ffn1_up", u1, w_f1, tm)
    (w_dn,) = wts.finish("f1b", [a1])
    w_f1["down"] = w_dn.reshape(whole)
    h1, u2 = _ffn_down("ffn1_down", a1, w_f1, x, tm, next_gain=sp["mix_norm"])

    (w_in_g,) = wts.finish("win", [u2])
    w_in_t = w_in_g.reshape(D_FF, D_MODEL)
    proj, s_in = _mix_in(u2, w_in_t, tm, after=wts.start("f2", wts.start("mix", [w_in_g])))
    pooled, mixed, ms = _pool_fwd(proj, sp["pool_w"][0], sp["pool_scale"])

    states, y_dirs = [], []
    for dr in range(2):
        st, yd = _ssm_scan(f"ssm_scan_fwd{dr}", s_in, w_in_s, a_r, a_i, w_out_s, reverse=(dr == 1), dr=dr)
        states.append(st)
        y_dirs.append(yd)
    *w_sq, w_kv, w_e = wts.finish("mix", y_dirs)
    w_mo, w_q, w_xo = (a.reshape(D_MODEL, D_MODEL) for a in w_sq)
    w_d = w_kv[:, None]
    y_total, yssm = _ssm_combine(proj, y_dirs[0], y_dirs[1], sp["ssm_d"], tm)

    merged = _mixer_merge(ms, yssm, w_e, proj, tm)
    h2, u3 = _mm_resid_norm("mix_out", merged, w_mo, h1, sp["xattn_norm"], tm)

    q = _plain_mm("attn_q", u3, w_q, NN, BF16, tm)
    n_mem = mem.shape[0]
    kv = _mm("attn_kv", [(mem_n, BS((n_mem, D_MODEL), lambda s: (0, 0)), w_d, BS((None, None, D_MODEL, 512), lambda s: (s, 0, 0, 0)), NN)],
             grid=(N_SHARD,), out_shape=SDS((n_mem, 2 * D_MODEL), BF16), out_spec=BS((n_mem, 512), lambda s: (0, s)))
    o = _attn_fwd(q, kv, tm)
    h3, u4 = _mm_resid_norm("attn_out", o, w_xo, h2, sp["ffn2_norm"], tm)

    w_f2 = dict(zip(("gate", "up", "down"), (a.reshape(whole) for a in wts.finish("f2", [u4]))))
    g2, up2, a2 = _ffn_up("ffn2_up", u4, w_f2, tm)
    loss, dh4, dh4_b, g["final_norm"] = _ffn_down("ffn2_down", a2, w_f2, h3, tm,
                                                  head=(sp["final_norm"].reshape(1, D_MODEL), target))

    dg2, dup2 = _ffn_bwd_act("ffn2_bwd_act", dh4_b, w_f2, g2, up2, tm)
    dw_f2 = _ffn_dw("ffn2_dw", u4, dg2, dup2, a2, dh4_b, tm)
    dh3, dh3_b, g["ffn2_norm"] = _ffn_dx("ffn2_dx", dg2, dup2, w_f2, h3, sp["ffn2_norm"], dh4, tm)

    d_o = _plain_mm("attn_out_dx", dh3_b, w_xo, NT, BF16, tm)
    dw_xo = _dw_mm("attn_out_dw", o, dh3_b, tm)
    dq, dkv = _attn_bwd(q, kv, d_o, tm)
    dw_q = _dw_mm("attn_q_dw", u3, dq, tm)
    dh2, dh2_b, g["xattn_norm"] = _mm_norm_bwd("attn_q_dx", dq, w_q, NT, h2, sp["xattn_norm"], dh3, tm)
    dw_kv = _mm("attn_kv_dw", [(mem_n, BS((n_mem, D_MODEL), lambda s: (0, 0)), dkv, BS((n_mem, 512), lambda s: (0, s)), TN)],
                grid=(N_SHARD,), out_shape=SDS((N_SHARD, D_MODEL, 512), BF16), out_spec=BS((None, D_MODEL, 512), lambda s: (s, 0, 0)))
    dmem_n = _mm("attn_kv_dx", [(dkv, BS((n_mem, 512), lambda s: (0, s)), w_d, BS((None, None, D_MODEL, 512), lambda s: (s, 0, 0, 0)), NT)],
                 grid=(N_SHARD,), red_axis=0, out_shape=SDS((n_mem, D_MODEL), F32), out_spec=BS((n_mem, D_MODEL), lambda s: (0, 0)))
    _, _, g["mem_norm"] = _rmsnorm_bwd("norm_mem_bwd", mem, sp["mem_norm"], dmem_n, None, n_mem)

    square = (N_SHARD, D_MODEL // N_SHARD, D_MODEL)
    sharded = (N_SHARD, FF_SH, D_MODEL)
    early = [a.reshape(sharded) for a in dw_f2] + [dw_xo.reshape(square), dw_q.reshape(square), dw_kv]
    swapping = reducer.swap_start("a1", early) if reducer is not None else []
    dmerged = _plain_mm("mix_out_dx", dh2_b, w_mo, NT, BF16, tm, after=swapping)
    dw_mo = _dw_mm("mix_out_dw", merged, dh2_b, tm)
    d_gp, d_gs, dzp, dzv, dzg = _mixer_merge_bwd(ms, yssm, w_e, proj, dmerged, tm)
    dw_e = _mixer_dw(ms, yssm, dzp, dzv, dzg, tm)
    d_ms, d_yt, d_yt_b = _mixer_dx(dzp, dzv, dzg, w_e, y_total, tm)
    dp, d_scale, d_pw = _pool_bwd(d_ms, mixed, pooled, sp["pool_w"][0], sp["pool_scale"])
    g["pool_scale"] = d_scale
    g["pool_w"] = d_pw[None]

    du_dirs, lams = [], []
    for dr in range(2):
        lam, du = _ssm_scan(f"ssm_scan_bwd{dr}", d_yt_b, w_out_s_t, a_r, a_i, w_in_s_t, reverse=(dr == 0), dr=dr, conj=True)
        du_dirs.append(du)
        lams.append(lam)
    ds, g["ssm_d"] = _ssm_ds(proj, d_yt, du_dirs[0], du_dirs[1], sp["ssm_d"], tm)

    d_proj = [dp, ds, d_gp, d_gs]
    dw_in_t = _mix_in_dw(d_proj, u2, tm)
    dh1, dh1_b, g["mix_norm"] = _mm_norm_bwd("mix_in_dx", d_proj, w_in_t, NN, h1, sp["mix_norm"], dh2, tm)

    early += [dw_mo.reshape(square), dw_e, dw_in_t.reshape(sharded)]
    g["final_norm"] = g["final_norm"].reshape(D_MODEL)

    travelling = reducer.start("a", early[6:], swapped=["a1"], after=list(g.values())) if reducer is not None else []
    d_abr, d_abi, d_cm, d_bm = [], [], [], []
    for dr in range(2):
        da_r, da_i, d_win, d_woutt = _ssm_param_grads(f"ssm_param_grads{dr}", lams[dr], states[dr], s_in, d_yt_b,
                                                      reverse=(dr == 1), after=travelling)
        d_abr.append(da_r)
        d_abi.append(da_i)
        d_bm.append(d_win)
        d_cm.append(d_woutt)

    d_ar, d_ai, d_ldt, d_br, d_bi, d_cr, d_ci = _ssm_prep_bwd(*ssm_a, *ssm_b, d_abr, d_abi, d_bm, d_cm)
    per_group = (2 * SSM_GROUPS, SSM_GROUP * SSM_STATE)
    g["ssm_a_re"] = d_ar.reshape(2 * SSM_GROUPS, SSM_GROUP, SSM_STATE).sum(axis=1).reshape(sp["ssm_a_re"].shape)
    g["ssm_a_im"] = d_ai.reshape(2 * SSM_GROUPS, SSM_GROUP, SSM_STATE).sum(axis=1).reshape(sp["ssm_a_im"].shape)
    g["ssm_log_dt"] = d_ldt.reshape(per_group).sum(axis=1).reshape(sp["ssm_log_dt"].shape)
    g["ssm_b_re"] = d_br.reshape(sp["ssm_b_re"].shape)
    g["ssm_b_im"] = d_bi.reshape(sp["ssm_b_im"].shape)
    g["ssm_c_re"] = d_cr.reshape(sp["ssm_c_re"].shape)
    g["ssm_c_im"] = d_ci.reshape(sp["ssm_c_im"].shape)
    if reducer is not None:
        travelling = travelling + [d_ar, d_br, d_cr]
    dg1, dup1 = _ffn_bwd_act("ffn1_bwd_act", dh1_b, w_f1, g1, up1, tm, after=travelling)
    dw_f1 = [a.reshape(sharded) for a in _ffn_dw("ffn1_dw", u1, dg1, dup1, a1, dh1_b, tm)]
    if reducer is not None:
        travelling = reducer.start("b", dw_f1, after=reducer.finish("a", dw_f1[:1]))
        travelling = travelling + reducer.join_start("a", after=travelling)
    grad_x, _, g["ffn1_norm"] = _ffn_dx("ffn1_dx", dg1, dup1, w_f1, x, sp["ffn1_norm"], dh1, tm, after=travelling)
    if reducer is not None:
        reducer.join_finish("a", [grad_x])
    return loss, grad_x, early + dw_f1, g


def _mesh_place():
    x, y, c = lax.axis_index("x"), lax.axis_index("y"), lax.axis_index("c")
    chips = [(1 - x, y), (x, 1 - y), (1 - x, 1 - y)]
    return x, y, c, chips


def _remote(src, dst, send_sems, recv_sems, k, to):
    return pltpu.make_async_remote_copy(src_ref=src, dst_ref=dst, send_sem=send_sems.at[k], recv_sem=recv_sems.at[k],
                                        device_id=to, device_id_type=MESH)


def _sibling_swap_halves(tag, grads, after=()):
    n = len(grads)
    after_ops, after_specs = _after_operands(after)

    def body(*refs):
        ins, outs = refs[:n], refs[n + len(after_ops):2 * n + len(after_ops)]
        send_sems, recv_sems = refs[2 * n + len(after_ops):]
        x, y, c, _ = _mesh_place()
        sibling = (x, y, 1 - c)
        copies = []
        for k in range(n):
            half = grads[k].shape[1] // 2
            theirs = pl.ds(pl.multiple_of((1 - c) * half, 16), half)
            cp = _remote(ins[k].at[:, theirs, :], outs[k], send_sems, recv_sems, k, sibling)
            cp.start()
            copies.append(cp)
        for cp in copies:
            cp.wait_recv()
        for cp in copies:
            cp.wait_send()

    hbm = BS(memory_space=pl.ANY)
    return pl.pallas_call(
        body, out_shape=tuple(SDS((g.shape[0], g.shape[1] // 2, g.shape[2]), g.dtype) for g in grads),
        in_specs=[hbm] * n + after_specs, out_specs=(hbm,) * n,
        scratch_shapes=[pltpu.SemaphoreType.DMA((n,)), pltpu.SemaphoreType.DMA((n,))],
        name="reduce_sibling_send_" + tag, compiler_params=_params())(*grads, *after_ops)


def _row_tile(rows, cap=512):
    return max(r for r in range(16, cap + 1, 16) if rows % r == 0)


REDUCE_STEPS = 2


def _chip_presum(tag, grads, gots, c_idx):
    n = len(grads)
    halves = [g.shape[1] // 2 for g in grads]
    tiles = [(h // REDUCE_STEPS, g.shape[2]) for h, g in zip(halves, grads)]

    def body(c_ref, *refs):
        for k in range(n):
            refs[2 * n + k][...] = (refs[k][...].astype(F32) + refs[n + k][...].astype(F32)).astype(BF16)

    mine = [BS((None, None) + tile, lambda s, i, c_ref: (s, c_ref[0], i, 0)) for tile in tiles]
    plain = [BS((None,) + tile, lambda s, i, c_ref: (s, i, 0)) for tile in tiles]
    return list(pl.pallas_call(
        body, out_shape=tuple(SDS((g.shape[0], h, g.shape[2]), BF16) for g, h in zip(grads, halves)),
        grid_spec=pltpu.PrefetchScalarGridSpec(num_scalar_prefetch=1, grid=(N_SHARD, REDUCE_STEPS),
                                               in_specs=mine + plain, out_specs=plain),
        name="reduce_presum_" + tag, compiler_params=_params("parallel", "parallel"))(
            c_idx, *[g.reshape(g.shape[0], 2, h, g.shape[2]) for g, h in zip(grads, halves)], *gots))


HBM_SPEC = BS(memory_space=pltpu.HBM)
SEM_SPEC = BS(memory_space=pltpu.SEMAPHORE)
DATAFLOW = pltpu.SideEffectType.DATAFLOW_SIDE_EFFECTING


def _chip_exchange_copies(parts, lands, send_sems, recv_sems):
    _, _, c, chips = _mesh_place()
    return [_remote(parts[k].at[2 * px + py], lands[k].at[j], send_sems, recv_sems, 3 * k + j, (px, py, c))
            for k in range(len(parts)) for j, (px, py) in enumerate(chips)]


def _gather_copies(shards, lands, send_sems, recv_sems):
    x, y, c, chips = _mesh_place()
    return [_remote(shards[k], lands[k].at[2 * x + y], send_sems, recv_sems, 3 * k + j, (px, py, c))
            for k in range(len(shards)) for j, (px, py) in enumerate(chips)]


def _gather_half_copies(shards, lands, send_sems, recv_sems):
    x, y, c, chips = _mesh_place()
    out = []
    for k in range(len(shards)):
        half = shards[k].shape[0] // 2
        mine = pl.ds(pl.multiple_of(c * half, 16), half)
        for j, (px, py) in enumerate(chips):
            out.append(_remote(shards[k].at[mine, :], lands[k].at[2 * x + y, mine, :], send_sems, recv_sems,
                               3 * k + j, (px, py, c)))
    return out


def _sibling_fill(tag, lands):
    n = len(lands)

    def body(*refs):
        outs = refs[n:2 * n]
        send_sems, recv_sems = refs[2 * n:]
        x, y, c, chips = _mesh_place()
        copies = []
        for k in range(n):
            half = lands[k].shape[1] // 2
            mine = pl.ds(pl.multiple_of(c * half, 16), half)
            for j, (px, py) in enumerate(chips):
                blk = outs[k].at[2 * px + py, mine, :]
                copies.append(_remote(blk, blk, send_sems, recv_sems, 3 * k + j, (x, y, 1 - c)))
        for cp in copies:
            cp.start()
        for cp in copies:
            cp.wait_recv()
        for cp in copies:
            cp.wait_send()

    hbm = BS(memory_space=pl.ANY)
    return list(pl.pallas_call(
        body, out_shape=tuple(SDS(a.shape, a.dtype) for a in lands),
        in_specs=[hbm] * n, out_specs=(hbm,) * n, input_output_aliases={k: k for k in range(n)},
        scratch_shapes=[pltpu.SemaphoreType.DMA((3 * n,)), pltpu.SemaphoreType.DMA((3 * n,))],
        name="gather_fill_" + tag, compiler_params=_params())(*lands))


def _swap_copies(grads, lands, send_sems, recv_sems):
    x, y, c, _ = _mesh_place()
    out = []
    for k in range(len(grads)):
        half = grads[k].shape[1] // 2
        theirs = pl.ds(pl.multiple_of((1 - c) * half, 16), half)
        out.append(_remote(grads[k].at[:, theirs, :], lands[k], send_sems, recv_sems, k, (x, y, 1 - c)))
    return out


def _join_copies(fulls, same, send_sems, recv_sems):
    x, y, c, _ = _mesh_place()
    out = []
    for k in range(len(fulls)):
        half = fulls[k].shape[0] // 2
        mine = fulls[k].at[pl.ds(pl.multiple_of(c * half, 8), half), :]
        out.append(_remote(mine, mine, send_sems, recv_sems, k, (x, y, 1 - c)))
    return out


def _everyone_copies(packs, lands, send_sems, recv_sems):
    x, y, c, _ = _mesh_place()
    out = []
    for k in range(len(packs)):
        for j in range(N_DEV - 1):
            bx, by, bc = (j + 1) >> 2 & 1, (j + 1) >> 1 & 1, (j + 1) & 1
            peer = (x ^ bx, y ^ by, c ^ bc)
            out.append(_remote(packs[k], lands[k].at[4 * x + 2 * y + c], send_sems, recv_sems, (N_DEV - 1) * k + j, peer))
    return out


def _split_start(name, copies, sources, land_shapes, after=(), fanout=3):
    n = len(sources)
    n_land = len(land_shapes)
    m = n + n_land
    n_sems = fanout * n
    after_ops, after_specs = _after_operands(after)

    def body(*refs):
        ins = refs[:n]
        lands = refs[n:m] if n_land else ins
        send_sems, recv_sems = refs[m + len(after_ops)], refs[m + len(after_ops) + 1]
        token = refs[-1]
        for cp in copies(ins, lands, send_sems, recv_sems):
            cp.start()
        token[...] = jnp.zeros_like(token)

    lands = [pltpu.with_memory_space_constraint(lax.empty(s, d), pltpu.HBM) for s, d in land_shapes]
    sources = [pltpu.with_memory_space_constraint(p, pltpu.HBM) for p in sources]
    thru = [pltpu.HBM(a.shape, a.dtype) for a in sources + lands]
    out = pl.pallas_call(
        body, name=name,
        out_shape=(pltpu.SemaphoreType.DMA((n_sems,)), pltpu.SemaphoreType.DMA((n_sems,)), *thru, SDS((8, 128), F32)),
        in_specs=[HBM_SPEC] * m + after_specs,
        out_specs=(SEM_SPEC, SEM_SPEC, *[HBM_SPEC] * m, BS(memory_space=pltpu.VMEM)),
        input_output_aliases={i: 2 + i for i in range(m)},
        compiler_params=pltpu.CompilerParams(has_side_effects=DATAFLOW))(*sources, *lands, *after_ops)
    return out[0], out[1], list(out[2:2 + n]), list(out[2 + n:2 + m]), out[-1]


def _split_wait(name, copies, send_sems, recv_sems, sources, lands, after):
    n = len(sources)
    m = n + len(lands)
    after_ops, after_specs = _after_operands(after)

    def body(*refs):
        ins = refs[:n]
        zones = refs[n:m] if m > n else ins
        for cp in copies(ins, zones, refs[m], refs[m + 1]):
            cp.wait_send()
            cp.wait_recv()

    out = pl.pallas_call(
        body, name=name,
        out_shape=tuple(pltpu.HBM(a.shape, a.dtype) for a in sources + lands),
        in_specs=[HBM_SPEC] * m + [SEM_SPEC, SEM_SPEC] + after_specs, out_specs=(HBM_SPEC,) * m,
        input_output_aliases={i: i for i in range(m)},
        compiler_params=pltpu.CompilerParams(has_side_effects=DATAFLOW))(*sources, *lands, send_sems, recv_sems, *after_ops)
    return list(out[:n]), list(out[n:])


class _WeightGatherer:
    def __init__(self, shards):
        self.shards, self.open = shards, {}
        self.me = 2 * lax.axis_index("x") + lax.axis_index("y")

    HALVED = ("f1a", "win", "mix")

    def start(self, tag, after=()):
        shapes = [((N_SHARD,) + s.shape, s.dtype) for s in self.shards[tag]]
        copies = _gather_half_copies if tag in self.HALVED else _gather_copies
        self.open[tag] = _split_start("gather_start_" + tag, copies, self.shards[tag], shapes, after)
        return [self.open[tag][-1]]

    def sources(self, tags):
        return [s for tag in tags for s in self.shards[tag]]

    def finish(self, tag, after):
        send_sems, recv_sems, shards, lands, _ = self.open.pop(tag)
        copies = _gather_half_copies if tag in self.HALVED else _gather_copies
        shards, lands = _split_wait("gather_wait_" + tag, copies, send_sems, recv_sems, shards, lands, after)
        if tag in self.HALVED:
            lands = _sibling_fill(tag, lands)
        return [lax.dynamic_update_slice(zone, s[None], (self.me, 0, 0)) for zone, s in zip(lands, shards)]


class _GradReducer:
    def __init__(self):
        self.c_idx = lax.axis_index("c").astype(jnp.int32).reshape(1)
        self.place = jnp.stack([2 * lax.axis_index("x") + lax.axis_index("y"), lax.axis_index("c")]).astype(jnp.int32)
        self.swaps, self.open, self.landed, self.joins, self.reduced = {}, {}, {}, {}, []

    def swap_start(self, tag, grads, after=()):
        shapes = [((g.shape[0], g.shape[1] // 2, g.shape[2]), g.dtype) for g in grads]
        self.swaps[tag] = _split_start("reduce_swap_start_" + tag, _swap_copies, grads, shapes, after, fanout=1)
        return [self.swaps[tag][-1]]

    def start(self, tag, grads, after=(), swapped=()):
        pairs = []
        for s in swapped:
            send_sems, recv_sems, early, lands, _ = self.swaps.pop(s)
            pairs += zip(*_split_wait("reduce_swap_wait_" + s, _swap_copies, send_sems, recv_sems, early, lands, grads[-1:]))
        pairs += zip(grads, _sibling_swap_halves(tag, grads, after))
        parts = _chip_presum(tag, [g for g, _ in pairs], [s for _, s in pairs], self.c_idx)
        shapes = [((3,) + p.shape[1:], p.dtype) for p in parts]
        self.open[tag] = _split_start("reduce_exchange_start_" + tag, _chip_exchange_copies, parts, shapes)
        return [self.open[tag][-1]]

    def finish(self, tag, after):
        send_sems, recv_sems, parts, lands, _ = self.open.pop(tag)
        self.landed[tag] = _split_wait("reduce_exchange_wait_" + tag, _chip_exchange_copies, send_sems, recv_sems, parts, lands, after)
        return self.landed[tag][1][:1]

    def _sums(self, tag, after=()):
        parts, landed = self.landed.pop(tag)
        return _chip_sum(tag, parts, landed, self.place, after)

    def join_start(self, tag, after=()):
        self.joins[tag] = _split_start("reduce_join_start_" + tag, _join_copies, self._sums(tag, after), [], fanout=1)
        return [self.joins[tag][-1]]

    def join_finish(self, tag, after):
        send_sems, recv_sems, fulls, _, _ = self.joins.pop(tag)
        self.reduced += _split_wait("reduce_join_wait_" + tag, _join_copies, send_sems, recv_sems, fulls, [], after)[0]

    def join(self, tag, after=()):
        self.reduced += _sibling_join_halves(self._sums(tag), after)


def _chip_sum(tag, parts, gots, place, after=()):
    n = len(parts)
    tiles = [(p.shape[1] // REDUCE_STEPS, p.shape[2]) for p in parts]
    after_ops, after_specs = _after_operands(after)

    def body(place_ref, *refs):
        outs = refs[2 * n + len(after_ops):]
        for k in range(n):
            acc = refs[k][...].astype(F32)
            for j in range(3):
                acc = acc + refs[n + k][j].astype(F32)
            outs[k][...] = acc

    return list(pl.pallas_call(
        body, out_shape=tuple(SDS((2 * p.shape[1], p.shape[2]), F32) for p in parts),
        grid_spec=pltpu.PrefetchScalarGridSpec(
            num_scalar_prefetch=1, grid=(REDUCE_STEPS,),
            in_specs=[BS((None,) + tile, lambda i, place_ref: (place_ref[0], i, 0)) for tile in tiles]
            + [BS((3,) + tile, lambda i, place_ref: (0, i, 0)) for tile in tiles] + after_specs,
            out_specs=[BS(tile, lambda i, place_ref: (place_ref[1] * REDUCE_STEPS + i, 0)) for tile in tiles]),
        name="reduce_sum_" + tag, compiler_params=_params("parallel"))(place, *parts, *gots, *after_ops))


def _sibling_join_halves(fulls, after=()):
    n = len(fulls)
    after_ops, after_specs = _after_operands(after)

    def body(*refs):
        outs = refs[n + len(after_ops):2 * n + len(after_ops)]
        send_sems, recv_sems = refs[2 * n + len(after_ops):]
        copies = _join_copies(outs, outs, send_sems, recv_sems)
        for cp in copies:
            cp.start()
        for cp in copies:
            cp.wait_recv()
        for cp in copies:
            cp.wait_send()

    hbm = BS(memory_space=pl.ANY)
    return list(pl.pallas_call(
        body, out_shape=tuple(SDS(f.shape, f.dtype) for f in fulls),
        in_specs=[hbm] * n + after_specs, out_specs=(hbm,) * n, input_output_aliases={k: k for k in range(n)},
        scratch_shapes=[pltpu.SemaphoreType.DMA((n,)), pltpu.SemaphoreType.DMA((n,))],
        name="reduce_sibling_join", compiler_params=_params())(*fulls, *after_ops))


N_DEV = 8


def _sum_devices(packs):
    _, rows, lanes = packs.shape

    def body(p_ref, o_ref):
        acc = p_ref[0]
        for dev in range(1, N_DEV):
            acc = acc + p_ref[dev]
        o_ref[...] = acc

    vm = BS(memory_space=pltpu.VMEM)
    return pl.pallas_call(body, out_shape=SDS((rows, lanes), F32), in_specs=[vm], out_specs=vm,
                          name="small_sum", compiler_params=_params())(packs)


def _adamw_refs(w_ref, g_ref, m_ref, v_ref, go_ref, d_ref, mo_ref, vo_ref):
    bc1 = 1.0 - ADAM_B1 ** ADAM_STEP
    bc2 = 1.0 - ADAM_B2 ** ADAM_STEP
    g = g_ref[...]
    m_new = ADAM_B1 * m_ref[...] + (1.0 - ADAM_B1) * g
    v_new = ADAM_B2 * v_ref[...] + (1.0 - ADAM_B2) * (g * g)
    go_ref[...] = g
    mo_ref[...] = m_new
    vo_ref[...] = v_new
    d_ref[...] = -ADAM_LR * ((m_new / bc1) / (jnp.sqrt(v_new / bc2) + ADAM_EPS) + ADAM_WD * w_ref[...])


def _adamw_small(ws, gs, ms, vs):
    n = len(ws)

    def body(*refs):
        for k in range(n):
            _adamw_refs(*[refs[j * n + k] for j in range(4)], *refs[4 * n + 4 * k:4 * n + 4 * k + 4])

    vm = BS(memory_space=pltpu.VMEM)
    outs = pl.pallas_call(
        body, out_shape=tuple(SDS(a.shape, F32) for a in ws for _ in range(4)), in_specs=[vm] * (4 * n),
        out_specs=(vm,) * (4 * n), name="adamw_small", compiler_params=_params())(*ws, *gs, *ms, *vs)
    return [outs[4 * k:4 * k + 4] for k in range(n)]


def _adamw(name, w, grad, row0, m, v, after=()):
    rows, cols = w.shape
    tr = rows if rows < 16 else _row_tile(rows, 352)
    after_ops, after_specs = _after_operands(after)

    def body(w_ref, g_ref, m_ref, v_ref, *rest):
        _adamw_refs(w_ref, g_ref, m_ref, v_ref, *rest[len(after_ops):])

    blk = BS((tr, cols), lambda i: (i, 0))
    shape = SDS((rows, cols), F32)
    return pl.pallas_call(
        body, out_shape=(shape,) * 4, grid=(rows // tr,),
        in_specs=[blk, BS((tr, cols), lambda i: (row0 // tr + i, 0)), blk, blk] + after_specs, out_specs=(blk,) * 4,
        name=name, compiler_params=_params("parallel"))(w, grad, m, v, *after_ops)


SMALL_LANES = 128


SMALL_TILE = 8 * SMALL_LANES


def _packed_rows(p):
    return -(-p.size // SMALL_TILE) * 8


def _pack_small(parts):
    tiles = [jnp.pad(jnp.ravel(p), (0, _packed_rows(p) * SMALL_LANES - p.size)).reshape(-1, SMALL_LANES) for p in parts]
    rows = sum(t.shape[0] for t in tiles)
    return jnp.concatenate(tiles + [jnp.zeros((-rows % 64, SMALL_LANES), F32)], axis=0)


def _unpack_small(packed, like):
    out, at = [], 0
    for p in like:
        rows = _packed_rows(p)
        out.append(jnp.ravel(packed[at:at + rows])[:p.size].reshape(p.shape))
        at += rows
    return out


def kernel(x, mem, ffn1_norm, ffn1_w_gate, ffn1_w_up, ffn1_w_down, mix_norm, w_in, pool_w, pool_scale, w_pool_proj, ssm_a_re, ssm_a_im, ssm_log_dt, ssm_b_re, ssm_b_im, ssm_c_re, ssm_c_im, ssm_d, w_glu_val, w_glu_gate, w_mix_out, xattn_norm, mem_norm, w_q, w_kv, w_xo, ffn2_norm, ffn2_w_gate, ffn2_w_up, ffn2_w_down, final_norm, loss_target, m_ffn1_norm, m_ffn1_w_gate, m_ffn1_w_up, m_ffn1_w_down, m_mix_norm, m_w_in, m_pool_w, m_pool_scale, m_w_pool_proj, m_ssm_a_re, m_ssm_a_im, m_ssm_log_dt, m_ssm_b_re, m_ssm_b_im, m_ssm_c_re, m_ssm_c_im, m_ssm_d, m_w_glu_val, m_w_glu_gate, m_w_mix_out, m_xattn_norm, m_mem_norm, m_w_q, m_w_kv, m_w_xo, m_ffn2_norm, m_ffn2_w_gate, m_ffn2_w_up, m_ffn2_w_down, m_final_norm, v_ffn1_norm, v_ffn1_w_gate, v_ffn1_w_up, v_ffn1_w_down, v_mix_norm, v_w_in, v_pool_w, v_pool_scale, v_w_pool_proj, v_ssm_a_re, v_ssm_a_im, v_ssm_log_dt, v_ssm_b_re, v_ssm_b_im, v_ssm_c_re, v_ssm_c_im, v_ssm_d, v_w_glu_val, v_w_glu_gate, v_w_mix_out, v_xattn_norm, v_mem_norm, v_w_q, v_w_kv, v_w_xo, v_ffn2_norm, v_ffn2_w_gate, v_ffn2_w_up, v_ffn2_w_down, v_final_norm):
    given = dict(locals())
    w = {n: given[n] for n in WEIGHTS}
    m = {n: given["m_" + n] for n in WEIGHTS}
    v = {n: given["v_" + n] for n in WEIGHTS}

    def shard_view(a, n):
        return a[0].T if n in TRANSPOSED else a[0]

    def shard_unview(a, n):
        return (a.T if n in TRANSPOSED else a)[None]

    shards = {tag: [jnp.concatenate([shard_view(w[n], n).astype(BF16) for n in grp], axis=0) for grp in arrays]
              for tag, arrays in GATHER_PHASES.items()}
    reducer = _GradReducer()
    ws, ms, vs = ({n: _small_view(a[n], n) for n in SMALL} for a in (w, m, v))
    loss_part, grad_x, _, small = _device_step(x[0], mem[0], loss_target[0], _WeightGatherer(shards), ws, reducer)

    small_like = [ws[n] for n in SMALL] + [loss_part[0, :1]]
    pack = _pack_small([small[n] for n in SMALL] + [loss_part[0, :1]])
    everyone = _split_start("small_start", _everyone_copies, [pack], [((N_DEV,) + pack.shape, F32)], fanout=N_DEV - 1)

    grads, delta, new_m, new_v = {}, {}, {}, {}
    big_done = []

    def update(groups, reduced):
        for grp, red in zip(groups, reduced):
            row0 = 0
            for n in grp:
                w_n = shard_view(w[n], n)
                outs = _adamw("adamw_" + n, w_n, red, row0, shard_view(m[n], n), shard_view(v[n], n), after=everyone[-1:])
                grads[n], delta[n], new_m[n], new_v[n] = (shard_unview(o, n) for o in outs)
                big_done.append(outs[1])
                row0 += w_n.shape[0]

    n_a = len(reducer.reduced)
    update(REDUCE_GROUPS[:n_a], reducer.reduced)
    reducer.finish("b", list(big_done))
    reducer.join("b")
    update(REDUCE_GROUPS[n_a:], reducer.reduced[n_a:])

    send_sems, recv_sems, packs, landed, _ = everyone
    packs, landed = _split_wait("small_wait", _everyone_copies, send_sems, recv_sems, packs, landed, big_done)
    mine = 4 * lax.axis_index("x") + 2 * lax.axis_index("y") + lax.axis_index("c")
    summed = _sum_devices(lax.dynamic_update_slice(landed[0], packs[0][None], (mine, 0, 0)))
    g_small = dict(zip(SMALL + ("loss",), _unpack_small(summed, small_like)))
    loss = g_small.pop("loss").reshape(())
    def two_d(a):
        return a.reshape(-1, a.shape[-1])

    updated = _adamw_small(*([two_d(a[n]) for n in SMALL] for a in (ws, g_small, ms, vs)))
    for n, outs in zip(SMALL, updated):
        grads[n], delta[n], new_m[n], new_v[n] = (_small_view(o.reshape(ws[n].shape), n) for o in outs)

    return (loss, grad_x[None], *[grads[n] for n in WEIGHTS], *[delta[n] for n in WEIGHTS],
            *[new_m[n] for n in WEIGHTS], *[new_v[n] for n in WEIGHTS])
```

```python
import functools
import math

import jax
import jax.numpy as jnp
from jax import lax
from jax.experimental import pallas as pl
from jax.experimental.pallas import tpu as pltpu

F32 = jnp.float32
BF16 = jnp.bfloat16
SDS = jax.ShapeDtypeStruct
BS = pl.BlockSpec
MESH = pl.DeviceIdType.MESH

D_MODEL = 1024
D_FF = 2816
N_SHARD = 4
FF_SH = D_FF // N_SHARD
D_POOL = 512
POOL_WINDOWS = (2, 4, 8, 16)
POOL_GROUP = 128
D_SSM = 256
SSM_GROUPS = 16
SSM_GROUP = 16
SSM_STATE = 64
SSM_CH = SSM_GROUPS * SSM_STATE
N_HEADS = 4
HEAD_DIM = 256
EPS = 1e-6
ADAM_LR, ADAM_B1, ADAM_B2, ADAM_EPS, ADAM_WD, ADAM_STEP = 0.001, 0.9, 0.999, 1e-08, 0.01, 10

VMEM_LIMIT_V7X = 58 * 1024 * 1024
TM = 512

NN = (((1,), (0,)), ((), ()))
NT = (((1,), (1,)), ((), ()))
TN = (((0,), (0,)), ((), ()))


def _params(*sem):
    return pltpu.CompilerParams(dimension_semantics=sem if sem else None, vmem_limit_bytes=VMEM_LIMIT_V7X)


def _dot(a, b, dims=NN):
    return lax.dot_general(a.astype(BF16), b.astype(BF16), dims, preferred_element_type=F32)


def _sigmoid(v):
    return pl.reciprocal(1.0 + jnp.exp(-v), approx=True)


def _block_dims(spec):
    return tuple(d for d in spec.block_shape if d is not None)


def _after_operands(after):
    return list(after), [BS(memory_space=pl.ANY)] * len(after)


def _mm(name, pairs, *, grid, out_shape, out_spec, red_axis=None, extras=(), epilogue=None, after=()):
    n_pairs, n_extra = len(pairs), len(extras)
    n_red = grid[red_axis] if red_axis is not None else 1
    dims = [p[4] for p in pairs]

    def body(*refs):
        ab = refs[:2 * n_pairs]
        ex = refs[2 * n_pairs:2 * n_pairs + n_extra]
        o_ref = refs[2 * n_pairs + n_extra + len(after)]

        def partial():
            acc = None
            for p in range(n_pairs):
                t = _dot(ab[2 * p][...], ab[2 * p + 1][...], dims[p])
                acc = t if acc is None else acc + t
            return acc

        def finish(acc):
            res = epilogue(acc, *[e[...] for e in ex]) if epilogue is not None else acc
            o_ref[...] = res.astype(o_ref.dtype)

        if n_red == 1:
            finish(partial())
        else:
            acc_ref = refs[-1]
            k = pl.program_id(red_axis)

            @pl.when(k == 0)
            def _():
                acc_ref[...] = jnp.zeros_like(acc_ref)

            acc_ref[...] += partial()

            @pl.when(k == n_red - 1)
            def _():
                finish(acc_ref[...])

    operands, in_specs = [], []
    for a, a_spec, b, b_spec, _ in pairs:
        operands += [a, b]
        in_specs += [a_spec, b_spec]
    for e, e_spec in extras:
        operands.append(e)
        in_specs.append(e_spec)
    after_ops, after_specs = _after_operands(after)
    operands += after_ops
    in_specs += after_specs
    scratch = [pltpu.VMEM(_block_dims(out_spec), F32)] if n_red > 1 else []
    sem = tuple("arbitrary" if ax == red_axis else "parallel" for ax in range(len(grid)))
    return pl.pallas_call(body, out_shape=out_shape, grid=grid, in_specs=in_specs, out_specs=out_spec,
                          scratch_shapes=scratch, name=name, compiler_params=_params(*sem))(*operands)


def _rmsnorm(name, h, gain, tm, after=()):
    t, d = h.shape
    after_ops, after_specs = _after_operands(after)

    def body(h_ref, g_ref, *rest):
        u_ref = rest[-1]
        hv = h_ref[...]
        r = lax.rsqrt(jnp.mean(hv * hv, axis=-1, keepdims=True) + EPS)
        u_ref[...] = ((hv * r) * g_ref[...]).astype(u_ref.dtype)

    return pl.pallas_call(
        body, out_shape=SDS((t, d), BF16), grid=(t // tm,),
        in_specs=[BS((tm, d), lambda i: (i, 0)), BS((1, d), lambda i: (0, 0))] + after_specs,
        out_specs=BS((tm, d), lambda i: (i, 0)), name=name, compiler_params=_params("parallel"))(h, gain, *after_ops)


def _rmsnorm_bwd(name, h, gain, du, dh_in, tm):
    t, d = h.shape
    has_in = dh_in is not None

    def body(*refs):
        if has_in:
            h_ref, g_ref, du_ref, dhin_ref, dh_ref, dhb_ref, dg_ref = refs
        else:
            h_ref, g_ref, du_ref, dh_ref, dhb_ref, dg_ref = refs
        i = pl.program_id(0)
        hv = h_ref[...]
        r = lax.rsqrt(jnp.mean(hv * hv, axis=-1, keepdims=True) + EPS)
        n = hv * r
        duv = du_ref[...].astype(F32)
        dn = duv * g_ref[...]
        dh = r * (dn - n * jnp.mean(dn * n, axis=-1, keepdims=True))
        if has_in:
            dh = dhin_ref[...] + dh
        dh_ref[...] = dh
        dhb_ref[...] = dh.astype(BF16)

        @pl.when(i == 0)
        def _():
            dg_ref[...] = jnp.zeros_like(dg_ref)

        dg_ref[...] += jnp.sum(duv * n, axis=0, keepdims=True)

    row = BS((tm, d), lambda i: (i, 0))
    vec = BS((1, d), lambda i: (0, 0))
    operands = [h, gain, du] + ([dh_in] if has_in else [])
    in_specs = [row, vec, row] + ([row] if has_in else [])
    return pl.pallas_call(
        body, out_shape=(SDS((t, d), F32), SDS((t, d), BF16), SDS((1, d), F32)), grid=(t // tm,),
        in_specs=in_specs, out_specs=(row, row, vec), name=name, compiler_params=_params("arbitrary"))(*operands)


def _loss_head_tile(i, hv, g_ref, t_ref, loss_ref, dh_ref, dhb_ref, dg_ref):
    g = g_ref[...]
    r = lax.rsqrt(jnp.mean(hv * hv, axis=-1, keepdims=True) + EPS)
    n = hv * r
    err = n * g - t_ref[...]
    dy = err * (1.0 / hv.shape[-1])
    dn = dy * g
    dh = r * (dn - n * jnp.mean(dn * n, axis=-1, keepdims=True))
    dh_ref[...] = dh
    dhb_ref[...] = dh.astype(BF16)

    @pl.when(i == 0)
    def _():
        dg_ref[...] = jnp.zeros_like(dg_ref)
        loss_ref[...] = jnp.zeros_like(loss_ref)

    dg_ref[...] += jnp.sum(dy * n, axis=0, keepdims=True)
    part = 0.5 * jnp.sum(jnp.mean(err * err, axis=-1, keepdims=True), axis=0, keepdims=True)
    loss_ref[...] += jnp.broadcast_to(part, loss_ref.shape)


def _norm_tile(h, g_ref, u_ref):
    r = lax.rsqrt(jnp.mean(h * h, axis=-1, keepdims=True) + EPS)
    u_ref[...] = ((h * r) * g_ref[...]).astype(u_ref.dtype)


FFN_BLOCK = D_FF // 2


def _ffn_up(name, u, w_f, tm, after=()):
    t, d = u.shape
    after_ops, after_specs = _after_operands(after)

    def body(u_ref, wg_ref, wu_ref, *rest):
        pg_ref, pu_ref, a_ref = rest[len(after_ops):]
        uv = u_ref[...]
        for lo in range(0, D_FF, FFN_BLOCK):
            cols = slice(lo, lo + FFN_BLOCK)
            g = _dot(uv, wg_ref[cols, :], NT)
            up = _dot(uv, wu_ref[cols, :], NT)
            sg = _sigmoid(g)
            silu = g * sg
            a_ref[:, cols] = (silu * up).astype(BF16)
            pu_ref[:, cols] = (0.5 * silu).astype(BF16)
            pg_ref[:, cols] = (0.5 * sg * (1.0 + g * (1.0 - sg)) * up).astype(BF16)

    hid = BS((tm, D_FF), lambda i: (i, 0))
    shape = SDS((t, D_FF), BF16)
    whole = BS((D_FF, d), lambda i: (0, 0))
    return pl.pallas_call(
        body, out_shape=(shape, shape, shape), grid=(t // tm,),
        in_specs=[BS((tm, d), lambda i: (i, 0)), whole, whole] + after_specs,
        out_specs=(hid, hid, hid), name=name,
        compiler_params=_params("parallel"))(u, w_f["gate"], w_f["up"], *after_ops)


def _ffn_down(name, a, w_f, resid, tm, next_gain=None, head=None):
    t, d = resid.shape
    row = BS((tm, d), lambda i: (i, 0))
    vec = BS((1, d), lambda i: (0, 0))

    def body(a_ref, w_ref, res_ref, *rest):
        h = res_ref[...] + 0.5 * _dot(a_ref[...], w_ref[...])
        if head is not None:
            _loss_head_tile(pl.program_id(0), h, *rest)
        else:
            g_ref, h_ref, u_ref = rest
            h_ref[...] = h
            _norm_tile(h, g_ref, u_ref)

    if head is not None:
        extra, extra_specs = list(head), [vec, row]
        out_shape = (SDS((1, 128), F32), SDS((t, d), F32), SDS((t, d), BF16), SDS((1, d), F32))
        out_specs = (BS((1, 128), lambda i: (0, 0)), row, row, vec)
    else:
        extra, extra_specs = [next_gain], [vec]
        out_shape = (SDS((t, d), F32), SDS((t, d), BF16))
        out_specs = (row, row)
    return pl.pallas_call(
        body, out_shape=out_shape, grid=(t // tm,),
        in_specs=[BS((tm, D_FF), lambda i: (i, 0)), BS((D_FF, d), lambda i: (0, 0)), row] + extra_specs,
        out_specs=out_specs, name=name,
        compiler_params=_params("arbitrary" if head is not None else "parallel"))(a, w_f["down"], resid, *extra)


def _mix_in(u, w_t, tm, after=()):
    t, d = u.shape
    after_ops, after_specs = _after_operands(after)

    def body(u_ref, w_ref, *rest):
        o_ref, s_ref = rest[-2:]
        uv = u_ref[...]
        for lo in range(0, D_FF, FFN_BLOCK):
            o_ref[:, lo:lo + FFN_BLOCK] = _dot(uv, w_ref[lo:lo + FFN_BLOCK, :], NT)
        s_ref[...] = o_ref[:, D_POOL:D_POOL + D_SSM].astype(BF16)

    return pl.pallas_call(
        body, out_shape=(SDS((t, D_FF), F32), SDS((t, D_SSM), BF16)), grid=(t // tm,),
        in_specs=[BS((tm, d), lambda i: (i, 0)), BS((D_FF, d), lambda i: (0, 0))] + after_specs,
        out_specs=(BS((tm, D_FF), lambda i: (i, 0)), BS((tm, D_SSM), lambda i: (i, 0))), name="mix_in",
        compiler_params=_params("parallel"))(u, w_t, *after_ops)


def _mm_resid_norm(name, a, b, resid, next_gain, tm):
    t, d = resid.shape
    tm = min(2 * tm, t)

    def body(a_ref, b_ref, res_ref, g_ref, h_ref, u_ref):
        h = res_ref[...] + _dot(a_ref[...], b_ref[...])
        h_ref[...] = h
        _norm_tile(h, g_ref, u_ref)

    row = BS((tm, d), lambda i: (i, 0))
    return pl.pallas_call(
        body, out_shape=(SDS((t, d), F32), SDS((t, d), BF16)), grid=(t // tm,),
        in_specs=[BS((tm, a.shape[1]), lambda i: (i, 0)), BS(b.shape, lambda i: (0, 0)), row, BS((1, d), lambda i: (0, 0))],
        out_specs=(row, row), name=name, compiler_params=_params("parallel"))(a, b, resid, next_gain)


def _ffn_bwd_act(name, dh_b, w_f, pg, pu, tm, after=()):
    t, d = dh_b.shape
    after_ops, after_specs = _after_operands(after)

    def body(dh_ref, wd_ref, pg_ref, pu_ref, *rest):
        dg_ref, dup_ref = rest[len(after_ops):]
        dh = dh_ref[...]
        for lo in range(0, D_FF, FFN_BLOCK):
            cols = slice(lo, lo + FFN_BLOCK)
            da = _dot(dh, wd_ref[cols, :], NT)
            dg_ref[:, cols] = (da * pg_ref[:, cols].astype(F32)).astype(BF16)
            dup_ref[:, cols] = (da * pu_ref[:, cols].astype(F32)).astype(BF16)

    hid = BS((tm, D_FF), lambda i: (i, 0))
    shape = SDS((t, D_FF), BF16)
    return pl.pallas_call(
        body, out_shape=(shape, shape), grid=(t // tm,),
        in_specs=[BS((tm, d), lambda i: (i, 0)), BS((D_FF, d), lambda i: (0, 0)), hid, hid] + after_specs,
        out_specs=(hid, hid), name=name,
        compiler_params=_params("parallel"))(dh_b, w_f["down"], pg, pu, *after_ops)


def _ffn_dw(name, u, dg, dup, a, dh_b, tm):
    t, d = u.shape
    n_t = t // tm

    def body(u_ref, dg_ref, dup_ref, a_ref, dh_ref, og_ref, ou_ref, od_ref, acc):
        i = pl.program_id(1)

        @pl.when(i == 0)
        def _():
            acc[...] = jnp.zeros_like(acc)

        uv = u_ref[...]
        acc[0] += _dot(dg_ref[...], uv, TN)
        acc[1] += _dot(dup_ref[...], uv, TN)
        acc[2] += _dot(a_ref[...], dh_ref[...], TN)

        @pl.when(i == n_t - 1)
        def _():
            og_ref[...] = acc[0].astype(BF16)
            ou_ref[...] = acc[1].astype(BF16)
            od_ref[...] = (0.5 * acc[2]).astype(BF16)

    hid = BS((tm, FFN_BLOCK), lambda j, i: (i, j))
    row = BS((tm, d), lambda j, i: (i, 0))
    out = BS((FFN_BLOCK, d), lambda j, i: (j, 0))
    shape = SDS((D_FF, d), BF16)
    return pl.pallas_call(
        body, out_shape=(shape, shape, shape), grid=(D_FF // FFN_BLOCK, n_t),
        in_specs=[row, hid, hid, hid, row], out_specs=(out, out, out),
        scratch_shapes=[pltpu.VMEM((3, FFN_BLOCK, d), F32)],
        name=name, compiler_params=_params("parallel", "arbitrary"))(u, dg, dup, a, dh_b)


def _norm_bwd_tile(i, du, h_ref, g_ref, dhin_ref, dh_ref, dhb_ref, dg_ref):
    hv = h_ref[...]
    r = lax.rsqrt(jnp.mean(hv * hv, axis=-1, keepdims=True) + EPS)
    n = hv * r
    dn = du * g_ref[...]
    dh = dhin_ref[...] + r * (dn - n * jnp.mean(dn * n, axis=-1, keepdims=True))
    dh_ref[...] = dh
    dhb_ref[...] = dh.astype(BF16)

    @pl.when(i == 0)
    def _():
        dg_ref[...] = jnp.zeros_like(dg_ref)

    dg_ref[...] += jnp.sum(du * n, axis=0, keepdims=True)


def _norm_bwd_specs(tm):
    row = BS((tm, D_MODEL), lambda i: (i, 0))
    vec = BS((1, D_MODEL), lambda i: (0, 0))
    return [row, vec, row], (row, row, vec)


def _norm_bwd_shapes(t):
    return SDS((t, D_MODEL), F32), SDS((t, D_MODEL), BF16), SDS((1, D_MODEL), F32)


def _ffn_dx(name, dg, dup, w_f, h, gain, dh_in, tm, after=()):
    t = dg.shape[0]
    tm = tm // 2
    after_ops, after_specs = _after_operands(after)

    def body(dg_ref, dup_ref, wg_ref, wu_ref, h_ref, g_ref, dhin_ref, *rest):
        du = _dot(dg_ref[...], wg_ref[...]) + _dot(dup_ref[...], wu_ref[...])
        _norm_bwd_tile(pl.program_id(0), du, h_ref, g_ref, dhin_ref, *rest[len(after_ops):])

    hid = BS((tm, D_FF), lambda i: (i, 0))
    whole = BS((D_FF, D_MODEL), lambda i: (0, 0))
    norm_in, norm_out = _norm_bwd_specs(tm)
    return pl.pallas_call(
        body, out_shape=_norm_bwd_shapes(t), grid=(t // tm,),
        in_specs=[hid, hid, whole, whole] + norm_in + after_specs, out_specs=norm_out, name=name,
        compiler_params=_params("arbitrary"))(dg, dup, w_f["gate"], w_f["up"], h, gain, dh_in, *after_ops)


def _mm_norm_bwd(name, a, b, dims, h, gain, dh_in, tm):
    pieces = list(a) if isinstance(a, (list, tuple)) else [a]
    assert len(pieces) == 1 or dims == NN
    t = pieces[0].shape[0]
    widths = [p.shape[1] for p in pieces]
    row0 = [sum(widths[:j]) for j in range(len(pieces))]

    def body(*refs):
        a_refs, (b_ref, h_ref, g_ref, dhin_ref), outs = refs[:len(pieces)], refs[len(pieces):len(pieces) + 4], refs[len(pieces) + 4:]
        if len(pieces) == 1:
            du = _dot(a_refs[0][...], b_ref[...], dims)
        else:
            du = _dot(a_refs[0][...], b_ref[0:widths[0], :])
            for a_ref, r0, w in zip(a_refs[1:], row0[1:], widths[1:]):
                du = du + _dot(a_ref[...], b_ref[r0:r0 + w, :])
        _norm_bwd_tile(pl.program_id(0), du, h_ref, g_ref, dhin_ref, *outs)

    norm_in, norm_out = _norm_bwd_specs(tm)
    return pl.pallas_call(
        body, out_shape=_norm_bwd_shapes(t), grid=(t // tm,),
        in_specs=[BS((tm, w), lambda i: (i, 0)) for w in widths] + [BS(b.shape, lambda i: (0, 0))] + norm_in,
        out_specs=norm_out, name=name, compiler_params=_params("arbitrary"))(*pieces, b, h, gain, dh_in)


def _plain_mm(name, a, b, dims, out_dtype, tm, resid=None, after=()):
    t = a.shape[0]
    tm = min(2 * tm, t)
    n = b.shape[1] if dims == NN else b.shape[0]
    extras = [(resid, BS((tm, n), lambda i: (i, 0)))] if resid is not None else []
    epi = (lambda acc, res: res + acc) if resid is not None else None
    return _mm(name, [(a, BS((tm, a.shape[1]), lambda i: (i, 0)), b, BS(b.shape, lambda i: (0, 0)), dims)],
               grid=(t // tm,), out_shape=SDS((t, n), out_dtype), out_spec=BS((tm, n), lambda i: (i, 0)),
               extras=extras, epilogue=epi, after=after)


def _dw_mm(name, a, b, tm, out_dtype=BF16, after=()):
    t, k = a.shape
    n = b.shape[1]
    tm = min(2 * tm, t)
    return _mm(name, [(a, BS((tm, k), lambda i: (i, 0)), b, BS((tm, n), lambda i: (i, 0)), TN)],
               grid=(t // tm,), red_axis=0, out_shape=SDS((k, n), out_dtype), out_spec=BS((k, n), lambda i: (0, 0)),
               after=after)


def _mix_in_dw(pieces, u, tm):
    t, d = u.shape
    tm = min(2 * tm, t)
    n_steps = t // tm
    widths = [p.shape[1] for p in pieces]
    row0 = [sum(widths[:j]) for j in range(len(pieces))]
    assert sum(widths) == D_FF

    def body(*refs):
        p_refs, u_ref, o_ref, acc_ref = refs[:len(pieces)], refs[len(pieces)], refs[-2], refs[-1]
        k = pl.program_id(0)

        @pl.when(k == 0)
        def _():
            acc_ref[...] = jnp.zeros_like(acc_ref)

        uv = u_ref[...]
        for p_ref, r0, w in zip(p_refs, row0, widths):
            acc_ref[r0:r0 + w, :] += _dot(p_ref[...], uv, TN)

        @pl.when(k == n_steps - 1)
        def _():
            o_ref[...] = acc_ref[...].astype(BF16)

    return pl.pallas_call(
        body, out_shape=SDS((D_FF, d), BF16), grid=(n_steps,),
        in_specs=[BS((tm, w), lambda i: (i, 0)) for w in widths] + [BS((tm, d), lambda i: (i, 0))],
        out_specs=BS((D_FF, d), lambda i: (0, 0)), scratch_shapes=[pltpu.VMEM((D_FF, d), F32)],
        name="mix_in_dw", compiler_params=_params("arbitrary"))(*pieces, u)


POOL_CHUNK = 256
POOL_HALO = 8


def _window_sum(v, width, lead):
    n = v.shape[0]
    s = v
    k = 1
    while k < width:
        s = s + pltpu.roll(s, n - k, 0)
        k *= 2
    return pltpu.roll(s, lead, 0) if lead else s


def _pool_count(base, left, right, t, shape):
    pos = base + lax.broadcasted_iota(jnp.int32, shape, 0)
    lo = jnp.maximum(pos - left, 0)
    hi = jnp.minimum(pos + right + 1, t)
    return (hi - lo).astype(F32)


def _pool_fwd(proj, pool_w, pool_scale):
    t = proj.shape[0]
    c, h = POOL_CHUNK, POOL_HALO
    n_chunks = t // c

    def body(proj_hbm, pw_ref, sc_ref, pooled_ref, mixed_ref, ms_ref, pad_ref, sem):
        cp = pltpu.make_async_copy(proj_hbm.at[:, pl.ds(0, D_POOL)], pad_ref.at[pl.ds(h, t), :], sem)
        cp.start()
        pad_ref[pl.ds(0, h), :] = jnp.zeros((h, D_POOL), F32)
        pad_ref[pl.ds(t + h, h), :] = jnp.zeros((h, D_POOL), F32)
        cp.wait()
        for g, width in enumerate(POOL_WINDOWS):
            left = width // 2
            right = width - 1 - left
            cols = slice(g * POOL_GROUP, (g + 1) * POOL_GROUP)
            wmat = pw_ref[g].astype(BF16)
            scale = sc_ref[:, cols]

            def chunk(ci, carry, left=left, right=right, width=width, cols=cols, wmat=wmat, scale=scale):
                base = pl.multiple_of(ci * c, c)
                v = pad_ref[pl.ds(base, c + 2 * h), cols]
                win = _window_sum(v, width, left)[h:h + c]
                cnt = _pool_count(base, left, right, t, (c, POOL_GROUP))
                pooled = (win / cnt - v[h:h + c]).astype(BF16)
                mixed = _dot(pooled, wmat)
                pooled_ref[pl.ds(base, c), cols] = pooled
                mixed_ref[pl.ds(base, c), cols] = mixed.astype(BF16)
                ms_ref[pl.ds(base, c), cols] = (mixed * scale).astype(BF16)
                return carry

            lax.fori_loop(0, n_chunks, chunk, 0)

    vm = BS(memory_space=pltpu.VMEM)
    shape = SDS((t, D_POOL), BF16)
    return pl.pallas_call(
        body, out_shape=(shape, shape, shape),
        in_specs=[BS(memory_space=pl.ANY), vm, vm], out_specs=(vm, vm, vm),
        scratch_shapes=[pltpu.VMEM((t + 2 * h, D_POOL), F32), pltpu.SemaphoreType.DMA],
        name="pool_fwd", compiler_params=_params())(proj, pool_w, pool_scale)


def _pool_bwd(d_ms, mixed, pooled, pool_w, pool_scale):
    t = d_ms.shape[0]
    c, h = POOL_CHUNK, POOL_HALO
    n_chunks = t // c

    def body(dms_ref, mixed_ref, pooled_ref, pw_ref, sc_ref, dp_ref, dsc_ref, dpw_ref, pad_ref):
        pad_ref[pl.ds(0, h), :] = jnp.zeros((h, D_POOL), F32)
        pad_ref[pl.ds(t + h, h), :] = jnp.zeros((h, D_POOL), F32)
        for g, width in enumerate(POOL_WINDOWS):
            left = width // 2
            right = width - 1 - left
            cols = slice(g * POOL_GROUP, (g + 1) * POOL_GROUP)
            wmat = pw_ref[g].astype(BF16)
            scale = sc_ref[:, cols]

            def first(ci, carry, left=left, right=right, cols=cols, wmat=wmat, scale=scale):
                dsc, dpw = carry
                base = pl.multiple_of(ci * c, c)
                dms = dms_ref[pl.ds(base, c), cols].astype(F32)
                dsc = dsc + jnp.sum(dms * mixed_ref[pl.ds(base, c), cols].astype(F32), axis=0, keepdims=True)
                dmix = (dms * scale).astype(BF16)
                dpw = dpw + _dot(pooled_ref[pl.ds(base, c), cols], dmix, TN)
                dpooled = _dot(dmix, wmat, NT)
                cnt = _pool_count(base, left, right, t, (c, POOL_GROUP))
                pad_ref[pl.ds(base + h, c), cols] = dpooled / cnt
                return dsc, dpw

            dsc, dpw = lax.fori_loop(0, n_chunks, first,
                                     (jnp.zeros((1, POOL_GROUP), F32), jnp.zeros((POOL_GROUP, POOL_GROUP), F32)))
            dsc_ref[:, cols] = dsc
            dpw_ref[g] = dpw

            def second(ci, carry, left=left, right=right, width=width, cols=cols):
                base = pl.multiple_of(ci * c, c)
                v = pad_ref[pl.ds(base, c + 2 * h), cols]
                win = _window_sum(v, width, right)[h:h + c]
                cnt = _pool_count(base, left, right, t, (c, POOL_GROUP))
                dp_ref[pl.ds(base, c), cols] = (win - v[h:h + c] * cnt).astype(BF16)
                return carry

            lax.fori_loop(0, n_chunks, second, 0)

    vm = BS(memory_space=pltpu.VMEM)
    return pl.pallas_call(
        body, out_shape=(SDS((t, D_POOL), BF16), SDS((1, D_POOL), F32), SDS((4, POOL_GROUP, POOL_GROUP), F32)),
        in_specs=[vm] * 5, out_specs=(vm, vm, vm),
        scratch_shapes=[pltpu.VMEM((t + 2 * h, D_POOL), F32)],
        name="pool_bwd", compiler_params=_params())(d_ms, mixed, pooled, pool_w, pool_scale)


SSM_ROWS = 2 * SSM_GROUPS * SSM_GROUP
SSM_HALF = SSM_GROUPS * SSM_GROUP


def _ssm_zoh(a_r, a_i, ldt):
    dt = jnp.exp(ldt)
    mag = jnp.exp(dt * a_r)
    ang = dt * a_i
    cs, sn = jnp.cos(ang), jnp.sin(ang)
    abr, abi = mag * cs, mag * sn
    den = a_r * a_r + a_i * a_i
    nr = abr - 1.0
    qr = (nr * a_r + abi * a_i) / den
    qi = (abi * a_r - nr * a_i) / den
    return dt, mag, cs, sn, abr, abi, den, nr, qr, qi


def _ssm_group_mask():
    row = lax.broadcasted_iota(jnp.int32, (SSM_HALF, SSM_CH), 0)
    col = lax.broadcasted_iota(jnp.int32, (SSM_HALF, SSM_CH), 1)
    return (row // SSM_GROUP) == (col // SSM_STATE)


def _ssm_prep(a_r, a_i, ldt, b_r, b_i, c_r, c_i, after=()):
    after_ops, after_specs = _after_operands(after)

    def body(ar_ref, ai_ref, ldt_ref, br_ref, bi_ref, cr_ref, ci_ref, *rest):
        abr_ref, abi_ref, win_ref, wint_ref, woutt_ref, wout_ref = rest[len(after_ops):]
        *_, abr, abi, _, _, qr, qi = _ssm_zoh(ar_ref[...], ai_ref[...], ldt_ref[...])
        abr_ref[...] = abr
        abi_ref[...] = abi
        b_r, b_i = br_ref[...], bi_ref[...]
        bbr = qr * b_r - qi * b_i
        bbi = qr * b_i + qi * b_r
        mask = _ssm_group_mask()
        state = lax.broadcasted_iota(jnp.int32, (SSM_STATE, SSM_CH), 0)
        col = lax.broadcasted_iota(jnp.int32, (SSM_STATE, SSM_CH), 1)
        every_group = (col % SSM_STATE == state).astype(BF16)

        def spread(x):
            return jnp.where(mask, _dot(x, every_group), 0.0)

        for d in range(2):
            rows = slice(d * SSM_HALF, (d + 1) * SSM_HALF)
            for half, x_in, x_out in ((0, bbr[rows], cr_ref[rows, :]), (1, bbi[rows], -ci_ref[rows, :])):
                cols = slice(half * SSM_CH, (half + 1) * SSM_CH)
                m_in, m_out = spread(x_in), spread(x_out)
                win_ref[d, :, cols] = m_in.astype(BF16)
                wint_ref[d, cols, :] = m_in.T.astype(BF16)
                woutt_ref[d, :, cols] = m_out.astype(BF16)
                wout_ref[d, cols, :] = m_out.T.astype(BF16)

    vm = BS(memory_space=pltpu.VMEM)
    vec = SDS((SSM_ROWS, SSM_STATE), F32)
    wide = SDS((2, SSM_HALF, 2 * SSM_CH), BF16)
    tall = SDS((2, 2 * SSM_CH, SSM_HALF), BF16)
    return pl.pallas_call(body, out_shape=(vec, vec, wide, tall, wide, tall), in_specs=[vm] * 7 + after_specs,
                          out_specs=(vm,) * 6, name="ssm_prep",
                          compiler_params=_params())(a_r, a_i, ldt, b_r, b_i, c_r, c_i, *after_ops)


def _ssm_prep_bwd(a_r, a_i, ldt, b_r, b_i, d_abr, d_abi, d_win, d_woutt):
    def body(ar_ref, ai_ref, ldt_ref, br_ref, bi_ref, *rest):
        (dabr_refs, dabi_refs, dwin_refs, dwoutt_refs), outs = [rest[2 * k:2 * k + 2] for k in range(4)], rest[8:]
        dar_ref, dai_ref, dldt_ref, dbr_ref, dbi_ref, dcr_ref, dci_ref = outs
        a_r, a_i = ar_ref[...], ai_ref[...]
        dt, mag, cs, sn, abr, abi, den, nr, qr, qi = _ssm_zoh(a_r, a_i, ldt_ref[...])
        mask = _ssm_group_mask()
        col = lax.broadcasted_iota(jnp.int32, (SSM_CH, SSM_STATE), 0)
        state = lax.broadcasted_iota(jnp.int32, (SSM_CH, SSM_STATE), 1)
        own_state = (col % SSM_STATE == state).astype(BF16)

        def pick(dense):
            m = jnp.where(mask, dense, 0.0)
            hi = m.astype(BF16)
            lo = m - hi.astype(F32)
            return _dot(hi, own_state) + _dot(lo, own_state)

        def picked(refs, half):
            cols = slice(half * SSM_CH, (half + 1) * SSM_CH)
            return jnp.concatenate([pick(ref[:, cols]) for ref in refs], axis=0)

        first_channel = lax.broadcasted_iota(jnp.int32, (SSM_HALF, SSM_CH), 0) % SSM_GROUP == 0

        def first_rows(refs):
            return jnp.concatenate(
                [pick(jnp.where(first_channel, jnp.broadcast_to(ref[...], (SSM_HALF, SSM_CH)), 0.0)) for ref in refs], axis=0)

        g_r, g_i = picked(dwin_refs, 0), picked(dwin_refs, 1)
        dcr_ref[...] = picked(dwoutt_refs, 0)
        dci_ref[...] = -picked(dwoutt_refs, 1)
        b_r, b_i = br_ref[...], bi_ref[...]
        dbr_ref[...] = g_r * qr + g_i * qi
        dbi_ref[...] = g_i * qr - g_r * qi
        gqr = g_r * b_r + g_i * b_i
        gqi = g_i * b_r - g_r * b_i
        g_nr_num = gqr / den
        g_ni_num = gqi / den
        g_den = -(gqr * qr + gqi * qi) / den
        g_nr = g_nr_num * a_r - g_ni_num * a_i
        g_abi = g_nr_num * a_i + g_ni_num * a_r
        d_ar = g_nr_num * nr + g_ni_num * abi + 2.0 * a_r * g_den
        d_ai = g_nr_num * abi - g_ni_num * nr + 2.0 * a_i * g_den
        g_abr = first_rows(dabr_refs) + g_nr
        g_abi = first_rows(dabi_refs) + g_abi
        g_mag = g_abr * cs + g_abi * sn
        g_ang = mag * (g_abi * cs - g_abr * sn)
        g_e = g_mag * mag
        d_ar = d_ar + g_e * dt
        d_ai = d_ai + g_ang * dt
        g_dt = g_e * a_r + g_ang * a_i
        dar_ref[...] = d_ar
        dai_ref[...] = d_ai
        dldt_ref[...] = g_dt * dt

    vm = BS(memory_space=pltpu.VMEM)
    vec = SDS((SSM_ROWS, SSM_STATE), F32)
    return pl.pallas_call(body, out_shape=(vec,) * 7, in_specs=[vm] * 13, out_specs=(vm,) * 7, name="ssm_prep_bwd",
                          compiler_params=_params())(a_r, a_i, ldt, b_r, b_i, *d_abr, *d_abi, *d_win, *d_woutt)


SCAN_ROWS = 512
SCAN_SUB = 128


def _ssm_scan(name, inp, w1, a_r, a_i, w2, reverse, dr, conj=False):
    t = inp.shape[0]
    rows = min(SCAN_ROWS, t)
    n = t // rows
    n_sub = rows // SCAN_SUB
    ch = SSM_CH
    at = (lambda i: (n - 1 - i, 0)) if reverse else (lambda i: (i, 0))

    def body(in_ref, w1_ref, ar_ref, ai_ref, w2_ref, sb_ref, out_ref, cr_ref, ci_ref, k_ref, st_ref):
        i = pl.program_id(0)

        @pl.when(i == 0)
        def _():
            ar8 = jnp.broadcast_to(ar_ref[...], (8, ch))
            ai8 = jnp.broadcast_to(-ai_ref[...] if conj else ai_ref[...], (8, ch))
            row = lax.broadcasted_iota(jnp.int32, (8, ch), 0)
            rank = (7 - row) if reverse else row
            powers = [(ar8, ai8)]
            for _ in range(7):
                p_r, p_i = powers[-1]
                powers.append((p_r * ar8 - p_i * ai8, p_r * ai8 + p_i * ar8))
            zero = jnp.zeros((8, ch), F32)
            for slot, k in enumerate((1, 2, 4)):
                k_ref[2 * slot] = jnp.where(rank >= k, powers[k - 1][0], zero)
                k_ref[2 * slot + 1] = jnp.where(rank >= k, powers[k - 1][1], zero)
            carry_r, carry_i = zero, zero
            for j in range(8):
                carry_r = jnp.where(rank == j, powers[j][0], carry_r)
                carry_i = jnp.where(rank == j, powers[j][1], carry_i)
            k_ref[6] = carry_r
            k_ref[7] = carry_i
            cr_ref[...] = zero
            ci_ref[...] = zero

        def group(r0, carry):
            c_r, c_i = carry
            x_r = st_ref[pl.ds(r0, 8), 0:ch]
            x_i = st_ref[pl.ds(r0, 8), ch:2 * ch]
            for slot, k in enumerate((1, 2, 4)):
                shift = (8 - k) if reverse else k
                s_r = pltpu.roll(x_r, shift, 0)
                s_i = pltpu.roll(x_i, shift, 0)
                m_r, m_i = k_ref[2 * slot], k_ref[2 * slot + 1]
                x_r, x_i = x_r + m_r * s_r - m_i * s_i, x_i + m_r * s_i + m_i * s_r
            p_r, p_i = k_ref[6], k_ref[7]
            x_r, x_i = x_r + p_r * c_r - p_i * c_i, x_i + p_r * c_i + p_i * c_r
            st_ref[pl.ds(r0, 8), 0:ch] = x_r
            st_ref[pl.ds(r0, 8), ch:2 * ch] = x_i
            last = 0 if reverse else 7
            return (jnp.broadcast_to(x_r[last:last + 1, :], (8, ch)), jnp.broadcast_to(x_i[last:last + 1, :], (8, ch)))

        carry = (cr_ref[...], ci_ref[...])
        for sc in (range(n_sub - 1, -1, -1) if reverse else range(n_sub)):
            part = pl.ds(sc * SCAN_SUB, SCAN_SUB)
            st_ref[part, :] = _dot(in_ref[part, :], w1_ref[...])
            for gi in range(SCAN_SUB // 8):
                g = (SCAN_SUB // 8 - 1 - gi) if reverse else gi
                carry = group(sc * SCAN_SUB + g * 8, carry)
            states = st_ref[part, :].astype(BF16)
            sb_ref[part, :] = states
            out_ref[part, :] = _dot(states, w2_ref[...])
        cr_ref[...] = carry[0]
        ci_ref[...] = carry[1]

    return pl.pallas_call(
        body, out_shape=(SDS((t, 2 * ch), BF16), SDS((t, D_SSM), F32)), grid=(n,),
        in_specs=[BS((rows, D_SSM), at), BS((None, D_SSM, 2 * ch), lambda i: (dr, 0, 0)), BS((None, 1, ch), lambda i: (dr, 0, 0)),
                  BS((None, 1, ch), lambda i: (dr, 0, 0)), BS((None, 2 * ch, D_SSM), lambda i: (dr, 0, 0))],
        out_specs=(BS((rows, 2 * ch), at), BS((rows, D_SSM), at)),
        scratch_shapes=[pltpu.VMEM((8, ch), F32), pltpu.VMEM((8, ch), F32), pltpu.VMEM((8, 8, ch), F32),
                        pltpu.VMEM((rows, 2 * ch), F32)],
        name=name, compiler_params=_params("arbitrary"))(inp, w1, a_r, a_i, w2)


DA_ROWS = 1024


def _ssm_param_grads(name, lam, states, u, dy, reverse, after=()):
    t = lam.shape[0]
    rows = min(DA_ROWS, t)
    n = t // rows
    halo_rows = 16
    nb = rows // halo_rows
    ch = SSM_CH
    if reverse:
        halo_at = lambda i: (jnp.minimum((i + 1) * nb, t // halo_rows - 1), 0)
    else:
        halo_at = lambda i: (jnp.maximum(i * nb - 1, 0), 0)

    after_ops, after_specs = _after_operands(after)

    def body(lam_ref, x_ref, halo_ref, u_ref, dy_ref, *rest):
        dr_ref, di_ref, dwin_ref, dwoutt_ref = rest[len(after_ops):]
        i = pl.program_id(0)

        @pl.when(i == 0)
        def _():
            dr_ref[...] = jnp.zeros_like(dr_ref)
            di_ref[...] = jnp.zeros_like(di_ref)
            dwin_ref[...] = jnp.zeros_like(dwin_ref)
            dwoutt_ref[...] = jnp.zeros_like(dwoutt_ref)

        dwin_ref[...] += _dot(u_ref[...], lam_ref[...], TN)
        dwoutt_ref[...] += _dot(dy_ref[...], x_ref[...], TN)
        row = lax.broadcasted_iota(jnp.int32, (rows, ch), 0)
        if reverse:
            edge, shift, h_row, live = rows - 1, rows - 1, 0, i < n - 1
        else:
            edge, shift, h_row, live = 0, 1, halo_rows - 1, i > 0

        def neighbour(lo):
            halo = halo_ref[:, lo:lo + ch].astype(F32)[h_row:h_row + 1]
            halo = jnp.where(live, halo, 0.0)
            x = x_ref[:, lo:lo + ch].astype(F32)
            return jnp.where(row == edge, jnp.broadcast_to(halo, (rows, ch)), pltpu.roll(x, shift, 0))

        xp_r, xp_i = neighbour(0), neighbour(ch)
        l_r, l_i = lam_ref[:, 0:ch].astype(F32), lam_ref[:, ch:2 * ch].astype(F32)
        dr_ref[...] += jnp.sum(l_r * xp_r + l_i * xp_i, axis=0, keepdims=True)
        di_ref[...] += jnp.sum(l_i * xp_r - l_r * xp_i, axis=0, keepdims=True)

    blk = BS((rows, 2 * ch), lambda i: (i, 0))
    thin = BS((rows, D_SSM), lambda i: (i, 0))
    vec = BS((1, ch), lambda i: (0, 0))
    mat = BS((D_SSM, 2 * ch), lambda i: (0, 0))
    return pl.pallas_call(
        body, out_shape=(SDS((1, ch), F32), SDS((1, ch), F32), SDS((D_SSM, 2 * ch), F32), SDS((D_SSM, 2 * ch), F32)),
        grid=(n,), in_specs=[blk, blk, BS((halo_rows, 2 * ch), halo_at), thin, thin] + after_specs,
        out_specs=(vec, vec, mat, mat),
        name=name, compiler_params=_params("arbitrary"))(lam, states, states, u, dy, *after_ops)


GELU_C = math.sqrt(2.0 / math.pi)
GELU_K = 0.044715


def _ssm_combine(proj, y_fwd, y_bwd, d_skip, tm, after=()):
    t = proj.shape[0]
    after_ops, after_specs = _after_operands(after)

    def body(s_ref, yf_ref, yb_ref, d_ref, *rest):
        yt_ref, g_ref = rest[len(after_ops):]
        y = s_ref[...] * d_ref[...] + yf_ref[...] + yb_ref[...]
        yt_ref[...] = y
        th = jnp.tanh(GELU_C * (y + GELU_K * y * y * y))
        g_ref[...] = (0.5 * y * (1.0 + th)).astype(BF16)

    blk = BS((tm, D_SSM), lambda i: (i, 0))
    return pl.pallas_call(
        body, out_shape=(SDS((t, D_SSM), F32), SDS((t, D_SSM), BF16)), grid=(t // tm,),
        in_specs=[BS((tm, D_SSM), lambda i: (i, D_POOL // D_SSM)), blk, blk, BS((1, D_SSM), lambda i: (0, 0))] + after_specs,
        out_specs=(blk, blk), name="ssm_combine",
        compiler_params=_params("parallel"))(proj, y_fwd, y_bwd, d_skip, *after_ops)


def _ssm_ds(proj, d_yt, du_fwd, du_bwd, d_skip, tm):
    t = proj.shape[0]

    def body(s_ref, dy_ref, duf_ref, dub_ref, d_ref, ds_ref, dd_ref):
        i = pl.program_id(0)
        dy = dy_ref[...]
        ds_ref[...] = (dy * d_ref[...] + duf_ref[...] + dub_ref[...]).astype(BF16)

        @pl.when(i == 0)
        def _():
            dd_ref[...] = jnp.zeros_like(dd_ref)

        dd_ref[...] += jnp.sum(dy * s_ref[...], axis=0, keepdims=True)

    blk = BS((tm, D_SSM), lambda i: (i, 0))
    vec = BS((1, D_SSM), lambda i: (0, 0))
    return pl.pallas_call(
        body, out_shape=(SDS((t, D_SSM), BF16), SDS((1, D_SSM), F32)), grid=(t // tm,),
        in_specs=[BS((tm, D_SSM), lambda i: (i, D_POOL // D_SSM)), blk, blk, blk, vec],
        out_specs=(blk, vec), name="ssm_ds", compiler_params=_params("arbitrary"))(proj, d_yt, du_fwd, du_bwd, d_skip)


G_POOL_AT = D_POOL + D_SSM
G_SSM_AT = G_POOL_AT + D_MODEL
E_VAL, E_GATE = D_POOL, D_POOL + D_SSM


def _merge_specs(tm):
    return [BS((tm, D_POOL), lambda i: (i, 0)), BS((tm, D_SSM), lambda i: (i, 0)),
            BS((N_SHARD, 1024, 256), lambda i: (0, 0, 0)), BS((tm, D_FF), lambda i: (i, 0))]


def _merge_parts(s, ms, yv, w_ref, proj_ref):
    lo = 256 * s
    zp = _dot(ms, w_ref[s, 0:E_VAL, :])
    zv = _dot(yv, w_ref[s, E_VAL:E_GATE, :])
    zg = _dot(yv, w_ref[s, E_GATE:, :])
    return zp, zv, zg, proj_ref[:, G_POOL_AT + lo:G_POOL_AT + lo + 256], proj_ref[:, G_SSM_AT + lo:G_SSM_AT + lo + 256]


def _mixer_merge(ms, yssm, w_e, proj, tm):
    t = ms.shape[0]

    def body(ms_ref, y_ref, w_ref, proj_ref, o_ref):
        msv, yv = ms_ref[...], y_ref[...]
        for s in range(N_SHARD):
            zp, zv, zg, gp, gs = _merge_parts(s, msv, yv, w_ref, proj_ref)
            o_ref[:, 256 * s:256 * (s + 1)] = (_sigmoid(gp) * zp + _sigmoid(gs) * zv * _sigmoid(zg)).astype(BF16)

    row = BS((tm, D_MODEL), lambda i: (i, 0))
    return pl.pallas_call(
        body, out_shape=SDS((t, D_MODEL), BF16), grid=(t // tm,), in_specs=_merge_specs(tm), out_specs=row,
        name="mixer_merge", compiler_params=_params("parallel"))(ms, yssm, w_e, proj)


def _mixer_merge_bwd(ms, yssm, w_e, proj, dmerged, tm):
    t = ms.shape[0]

    def body(ms_ref, y_ref, w_ref, proj_ref, dm_ref, dgp_ref, dgs_ref, dzp_ref, dzv_ref, dzg_ref):
        msv, yv = ms_ref[...], y_ref[...]
        for s in range(N_SHARD):
            cols = slice(256 * s, 256 * (s + 1))
            zp, zv, zg, gp, gs = _merge_parts(s, msv, yv, w_ref, proj_ref)
            dm = dm_ref[:, cols].astype(F32)
            sp, ss, sg = _sigmoid(gp), _sigmoid(gs), _sigmoid(zg)
            dgp_ref[:, cols] = (dm * zp * sp * (1.0 - sp)).astype(BF16)
            dgs_ref[:, cols] = (dm * zv * sg * ss * (1.0 - ss)).astype(BF16)
            dzp_ref[:, cols] = (dm * sp).astype(BF16)
            dz = dm * ss
            dzv_ref[:, cols] = (dz * sg).astype(BF16)
            dzg_ref[:, cols] = (dz * zv * sg * (1.0 - sg)).astype(BF16)

    row = BS((tm, D_MODEL), lambda i: (i, 0))
    shape = SDS((t, D_MODEL), BF16)
    return pl.pallas_call(
        body, out_shape=(shape,) * 5, grid=(t // tm,), in_specs=_merge_specs(tm) + [row],
        out_specs=(row,) * 5, name="mixer_merge_bwd",
        compiler_params=_params("parallel"))(ms, yssm, w_e, proj, dmerged)


def _mixer_dw(ms, yssm, dzp, dzv, dzg, tm):
    t = ms.shape[0]
    tm = min(2 * tm, t)
    n_t = t // tm

    def body(ms_ref, y_ref, dzp_ref, dzv_ref, dzg_ref, o_ref, acc):
        i = pl.program_id(0)

        @pl.when(i == 0)
        def _():
            acc[...] = jnp.zeros_like(acc)

        msv, yv = ms_ref[...], y_ref[...]
        for s in range(N_SHARD):
            cols = slice(256 * s, 256 * (s + 1))
            acc[s, 0:E_VAL, :] += _dot(msv, dzp_ref[:, cols], TN)
            acc[s, E_VAL:E_GATE, :] += _dot(yv, dzv_ref[:, cols], TN)
            acc[s, E_GATE:, :] += _dot(yv, dzg_ref[:, cols], TN)

        @pl.when(i == n_t - 1)
        def _():
            o_ref[...] = acc[...].astype(BF16)

    row = BS((tm, D_MODEL), lambda i: (i, 0))
    full = BS((N_SHARD, 1024, 256), lambda i: (0, 0, 0))
    return pl.pallas_call(
        body, out_shape=SDS((N_SHARD, 1024, 256), BF16), grid=(n_t,),
        in_specs=[BS((tm, D_POOL), lambda i: (i, 0)), BS((tm, D_SSM), lambda i: (i, 0)), row, row, row],
        out_specs=full, scratch_shapes=[pltpu.VMEM((N_SHARD, 1024, 256), F32)],
        name="mixer_dw", compiler_params=_params("arbitrary"))(ms, yssm, dzp, dzv, dzg)


def _mixer_dx(dzp, dzv, dzg, w_e, y_total, tm):
    t = dzp.shape[0]

    def body(dzp_ref, dzv_ref, dzg_ref, w_ref, yt_ref, dms_ref, dy_ref, dyb_ref):
        acc_ms, acc_y = None, None
        for s in range(N_SHARD):
            cols = slice(256 * s, 256 * (s + 1))
            part_ms = _dot(dzp_ref[:, cols], w_ref[s, 0:E_VAL, :], NT)
            part_y = _dot(dzv_ref[:, cols], w_ref[s, E_VAL:E_GATE, :], NT) + _dot(dzg_ref[:, cols], w_ref[s, E_GATE:, :], NT)
            acc_ms = part_ms if s == 0 else acc_ms + part_ms
            acc_y = part_y if s == 0 else acc_y + part_y
        dms_ref[...] = acc_ms.astype(BF16)
        y = yt_ref[...]
        th = jnp.tanh(GELU_C * (y + GELU_K * y * y * y))
        dgelu = 0.5 * (1.0 + th) + 0.5 * y * (1.0 - th * th) * GELU_C * (1.0 + 3.0 * GELU_K * y * y)
        dy = acc_y * dgelu
        dy_ref[...] = dy
        dyb_ref[...] = dy.astype(BF16)

    row = BS((tm, D_MODEL), lambda i: (i, 0))
    narrow = BS((tm, D_SSM), lambda i: (i, 0))
    return pl.pallas_call(
        body, out_shape=(SDS((t, D_POOL), BF16), SDS((t, D_SSM), F32), SDS((t, D_SSM), BF16)), grid=(t // tm,),
        in_specs=[row, row, row, BS((N_SHARD, 1024, 256), lambda i: (0, 0, 0)), narrow],
        out_specs=(BS((tm, D_POOL), lambda i: (i, 0)), narrow, narrow),
        name="mixer_dx", compiler_params=_params("parallel"))(dzp, dzv, dzg, w_e, y_total)


def _attn_probs(q_h, k_h):
    s = _dot(q_h, k_h, NT) * (1.0 / math.sqrt(HEAD_DIM))
    e = jnp.exp(s - jnp.max(s, axis=-1, keepdims=True))
    return e / jnp.sum(e, axis=-1, keepdims=True)


def _attn_fwd(q, kv, tm):
    t = q.shape[0]
    tm = min(2 * tm, t)
    m = kv.shape[0]

    def body(q_ref, kv_ref, o_ref):
        for hd in range(N_HEADS):
            lo = hd * HEAD_DIM
            p = _attn_probs(q_ref[:, lo:lo + HEAD_DIM], kv_ref[:, lo:lo + HEAD_DIM])
            o_ref[:, lo:lo + HEAD_DIM] = _dot(p, kv_ref[:, D_MODEL + lo:D_MODEL + lo + HEAD_DIM]).astype(BF16)

    return pl.pallas_call(
        body, out_shape=SDS((t, D_MODEL), BF16), grid=(t // tm,),
        in_specs=[BS((tm, D_MODEL), lambda i: (i, 0)), BS((m, 2 * D_MODEL), lambda i: (0, 0))],
        out_specs=BS((tm, D_MODEL), lambda i: (i, 0)), name="attn_fwd", compiler_params=_params("parallel"))(q, kv)


def _attn_bwd(q, kv, d_o, tm):
    t = q.shape[0]
    m = kv.shape[0]

    def body(q_ref, kv_ref, do_ref, dq_ref, dkv_ref):
        i = pl.program_id(0)

        @pl.when(i == 0)
        def _():
            dkv_ref[...] = jnp.zeros_like(dkv_ref)

        for hd in range(N_HEADS):
            lo = hd * HEAD_DIM
            q_h = q_ref[:, lo:lo + HEAD_DIM]
            k_h = kv_ref[:, lo:lo + HEAD_DIM]
            v_h = kv_ref[:, D_MODEL + lo:D_MODEL + lo + HEAD_DIM]
            do_h = do_ref[:, lo:lo + HEAD_DIM]
            p = _attn_probs(q_h, k_h)
            dkv_ref[:, D_MODEL + lo:D_MODEL + lo + HEAD_DIM] += _dot(p, do_h, TN)
            dp = _dot(do_h, v_h, NT)
            ds = p * (dp - jnp.sum(dp * p, axis=-1, keepdims=True)) * (1.0 / math.sqrt(HEAD_DIM))
            dq_ref[:, lo:lo + HEAD_DIM] = _dot(ds, k_h).astype(BF16)
            dkv_ref[:, lo:lo + HEAD_DIM] += _dot(ds, q_h, TN)

    row = BS((tm, D_MODEL), lambda i: (i, 0))
    full = BS((m, 2 * D_MODEL), lambda i: (0, 0))
    return pl.pallas_call(
        body, out_shape=(SDS((t, D_MODEL), BF16), SDS((m, 2 * D_MODEL), F32)), grid=(t // tm,),
        in_specs=[row, full, row], out_specs=(row, full), name="attn_bwd",
        compiler_params=_params("arbitrary"))(q, kv, d_o)


TRANSPOSED = ("ffn1_w_gate", "ffn1_w_up", "ffn2_w_gate", "ffn2_w_up", "w_in")
GATHER_PHASES = {"f1a": (("ffn1_w_gate",), ("ffn1_w_up",)),
                 "f1b": (("ffn1_w_down",),),
                 "win": (("w_in",),),
                 "mix": (("w_mix_out",), ("w_q",), ("w_xo",), ("w_kv",), ("w_pool_proj", "w_glu_val", "w_glu_gate")),
                 "f2": (("ffn2_w_gate",), ("ffn2_w_up",), ("ffn2_w_down",))}
REDUCE_GROUPS = (("ffn2_w_gate",), ("ffn2_w_up",), ("ffn2_w_down",), ("w_xo",), ("w_q",), ("w_kv",), ("w_mix_out",),
                 ("w_pool_proj", "w_glu_val", "w_glu_gate"), ("w_in",), ("ffn1_w_gate",), ("ffn1_w_up",), ("ffn1_w_down",))
SMALL = ("ffn1_norm", "mix_norm", "pool_w", "pool_scale", "ssm_a_re", "ssm_a_im", "ssm_log_dt", "ssm_b_re",
         "ssm_b_im", "ssm_c_re", "ssm_c_im", "ssm_d", "xattn_norm", "mem_norm", "ffn2_norm", "final_norm")
WEIGHTS = ("ffn1_norm", "ffn1_w_gate", "ffn1_w_up", "ffn1_w_down", "mix_norm", "w_in", "pool_w", "pool_scale",
           "w_pool_proj", "ssm_a_re", "ssm_a_im", "ssm_log_dt", "ssm_b_re", "ssm_b_im", "ssm_c_re", "ssm_c_im",
           "ssm_d", "w_glu_val", "w_glu_gate", "w_mix_out", "xattn_norm", "mem_norm", "w_q", "w_kv", "w_xo",
           "ffn2_norm", "ffn2_w_gate", "ffn2_w_up", "ffn2_w_down", "final_norm")


def _small_view(a, n):
    return jnp.swapaxes(a, 3, 4) if n in ("ssm_b_re", "ssm_b_im") else a


def _device_step(x, mem, target, wts, sp, reducer=None):
    t = x.shape[0]
    tm = min(TM, t)
    g = {}

    first_gather = wts.start("win", wts.start("f1b", wts.start("f1a")))
    u1 = _rmsnorm("norm_ffn1", x, sp["ffn1_norm"], tm, after=first_gather)

    def per_channel(a):
        a = a.reshape(2 * SSM_GROUPS, 1, -1)
        return jnp.broadcast_to(a, (2 * SSM_GROUPS, SSM_GROUP, a.shape[-1])).reshape(SSM_ROWS, a.shape[-1])

    ssm_a = per_channel(sp["ssm_a_re"]), per_channel(sp["ssm_a_im"]), per_channel(sp["ssm_log_dt"])
    ssm_b = sp["ssm_b_re"].reshape(SSM_ROWS, SSM_STATE), sp["ssm_b_im"].reshape(SSM_ROWS, SSM_STATE)
    abr, abi, w_in_s, w_in_s_t, w_out_s_t, w_out_s = _ssm_prep(
        *ssm_a, *ssm_b, sp["ssm_c_re"].reshape(SSM_ROWS, SSM_STATE), sp["ssm_c_im"].reshape(SSM_ROWS, SSM_STATE),
        after=first_gather)
    first_rows = (2, SSM_GROUPS, SSM_GROUP, SSM_STATE)
    a_r = abr.reshape(first_rows)[:, :, 0].reshape(2, 1, SSM_CH)
    a_i = abi.reshape(first_rows)[:, :, 0].reshape(2, 1, SSM_CH)
    mem_n = _rmsnorm("norm_mem", mem, sp["mem_norm"], mem.shape[0], after=first_gather + wts.sources(("mix", "f2")))

    whole = (D_FF, D_MODEL)
    w_g1, w_u1 = wts.finish("f1a", [u1, w_in_s, w_in_s_t, w_out_s, w_out_s_t, a_r, a_i, mem_n])
    w_f1 = {"gate": w_g1.reshape(whole), "up": w_u1.reshape(whole)}
    g1, up1, a1 = _ffn_up("ffn1_up", u1, w_f1, tm)
    (w_dn,) = wts.finish("f1b", [a1])
    w_f1["down"] = w_dn.reshape(whole)
    h1, u2 = _ffn_down("ffn1_down", a1, w_f1, x, tm, next_gain=sp["mix_norm"])

    (w_in_g,) = wts.finish("win", [u2])
    w_in_t = w_in_g.reshape(D_FF, D_MODEL)
    proj, s_in = _mix_in(u2, w_in_t, tm, after=wts.start("f2", wts.start("mix", [w_in_g])))
    pooled, mixed, ms = _pool_fwd(proj, sp["pool_w"][0], sp["pool_scale"])

    states, y_dirs = [], []
    for dr in range(2):
        st, yd = _ssm_scan(f"ssm_scan_fwd{dr}", s_in, w_in_s, a_r, a_i, w_out_s, reverse=(dr == 1), dr=dr)
        states.append(st)
        y_dirs.append(yd)
    *w_sq, w_kv, w_e = wts.finish("mix", y_dirs)
    w_mo, w_q, w_xo = (a.reshape(D_MODEL, D_MODEL) for a in w_sq)
    w_d = w_kv[:, None]
    y_total, yssm = _ssm_combine(proj, y_dirs[0], y_dirs[1], sp["ssm_d"], tm)

    merged = _mixer_merge(ms, yssm, w_e, proj, tm)
    h2, u3 = _mm_resid_norm("mix_out", merged, w_mo, h1, sp["xattn_norm"], tm)

    q = _plain_mm("attn_q", u3, w_q, NN, BF16, tm)
    n_mem = mem.shape[0]
    kv = _mm("attn_kv", [(mem_n, BS((n_mem, D_MODEL), lambda s: (0, 0)), w_d, BS((None, None, D_MODEL, 512), lambda s: (s, 0, 0, 0)), NN)],
             grid=(N_SHARD,), out_shape=SDS((n_mem, 2 * D_MODEL), BF16), out_spec=BS((n_mem, 512), lambda s: (0, s)))
    o = _attn_fwd(q, kv, tm)
    h3, u4 = _mm_resid_norm("attn_out", o, w_xo, h2, sp["ffn2_norm"], tm)

    w_f2 = dict(zip(("gate", "up", "down"), (a.reshape(whole) for a in wts.finish("f2", [u4]))))
    g2, up2, a2 = _ffn_up("ffn2_up", u4, w_f2, tm)
    loss, dh4, dh4_b, g["final_norm"] = _ffn_down("ffn2_down", a2, w_f2, h3, tm,
                                                  head=(sp["final_norm"].reshape(1, D_MODEL), target))

    dg2, dup2 = _ffn_bwd_act("ffn2_bwd_act", dh4_b, w_f2, g2, up2, tm)
    dw_f2 = _ffn_dw("ffn2_dw", u4, dg2, dup2, a2, dh4_b, tm)
    dh3, dh3_b, g["ffn2_norm"] = _ffn_dx("ffn2_dx", dg2, dup2, w_f2, h3, sp["ffn2_norm"], dh4, tm)

    d_o = _plain_mm("attn_out_dx", dh3_b, w_xo, NT, BF16, tm)
    dw_xo = _dw_mm("attn_out_dw", o, dh3_b, tm)
    dq, dkv = _attn_bwd(q, kv, d_o, tm)
    dw_q = _dw_mm("attn_q_dw", u3, dq, tm)
    dh2, dh2_b, g["xattn_norm"] = _mm_norm_bwd("attn_q_dx", dq, w_q, NT, h2, sp["xattn_norm"], dh3, tm)
    dw_kv = _mm("attn_kv_dw", [(mem_n, BS((n_mem, D_MODEL), lambda s: (0, 0)), dkv, BS((n_mem, 512), lambda s: (0, s)), TN)],
                grid=(N_SHARD,), out_shape=SDS((N_SHARD, D_MODEL, 512), BF16), out_spec=BS((None, D_MODEL, 512), lambda s: (s, 0, 0)))
    dmem_n = _mm("attn_kv_dx", [(dkv, BS((n_mem, 512), lambda s: (0, s)), w_d, BS((None, None, D_MODEL, 512), lambda s: (s, 0, 0, 0)), NT)],
                 grid=(N_SHARD,), red_axis=0, out_shape=SDS((n_mem, D_MODEL), F32), out_spec=BS((n_mem, D_MODEL), lambda s: (0, 0)))
    _, _, g["mem_norm"] = _rmsnorm_bwd("norm_mem_bwd", mem, sp["mem_norm"], dmem_n, None, n_mem)

    square = (N_SHARD, D_MODEL // N_SHARD, D_MODEL)
    sharded = (N_SHARD, FF_SH, D_MODEL)
    early = [a.reshape(sharded) for a in dw_f2] + [dw_xo.reshape(square), dw_q.reshape(square), dw_kv]
    swapping = reducer.swap_start("a1", early) if reducer is not None else []
    dmerged = _plain_mm("mix_out_dx", dh2_b, w_mo, NT, BF16, tm, after=swapping)
    dw_mo = _dw_mm("mix_out_dw", merged, dh2_b, tm)
    d_gp, d_gs, dzp, dzv, dzg = _mixer_merge_bwd(ms, yssm, w_e, proj, dmerged, tm)
    dw_e = _mixer_dw(ms, yssm, dzp, dzv, dzg, tm)
    d_ms, d_yt, d_yt_b = _mixer_dx(dzp, dzv, dzg, w_e, y_total, tm)
    dp, d_scale, d_pw = _pool_bwd(d_ms, mixed, pooled, sp["pool_w"][0], sp["pool_scale"])
    g["pool_scale"] = d_scale
    g["pool_w"] = d_pw[None]

    du_dirs, lams = [], []
    for dr in range(2):
        lam, du = _ssm_scan(f"ssm_scan_bwd{dr}", d_yt_b, w_out_s_t, a_r, a_i, w_in_s_t, reverse=(dr == 0), dr=dr, conj=True)
        du_dirs.append(du)
        lams.append(lam)
    ds, g["ssm_d"] = _ssm_ds(proj, d_yt, du_dirs[0], du_dirs[1], sp["ssm_d"], tm)

    d_proj = [dp, ds, d_gp, d_gs]
    dw_in_t = _mix_in_dw(d_proj, u2, tm)
    dh1, dh1_b, g["mix_norm"] = _mm_norm_bwd("mix_in_dx", d_proj, w_in_t, NN, h1, sp["mix_norm"], dh2, tm)

    early += [dw_mo.reshape(square), dw_e, dw_in_t.reshape(sharded)]
    g["final_norm"] = g["final_norm"].reshape(D_MODEL)

    travelling = reducer.start("a", early[6:], swapped=["a1"], after=list(g.values())) if reducer is not None else []
    d_abr, d_abi, d_cm, d_bm = [], [], [], []
    for dr in range(2):
        da_r, da_i, d_win, d_woutt = _ssm_param_grads(f"ssm_param_grads{dr}", lams[dr], states[dr], s_in, d_yt_b,
                                                      reverse=(dr == 1), after=travelling)
        d_abr.append(da_r)
        d_abi.append(da_i)
        d_bm.append(d_win)
        d_cm.append(d_woutt)

    d_ar, d_ai, d_ldt, d_br, d_bi, d_cr, d_ci = _ssm_prep_bwd(*ssm_a, *ssm_b, d_abr, d_abi, d_bm, d_cm)
    per_group = (2 * SSM_GROUPS, SSM_GROUP * SSM_STATE)
    g["ssm_a_re"] = d_ar.reshape(2 * SSM_GROUPS, SSM_GROUP, SSM_STATE).sum(axis=1).reshape(sp["ssm_a_re"].shape)
    g["ssm_a_im"] = d_ai.reshape(2 * SSM_GROUPS, SSM_GROUP, SSM_STATE).sum(axis=1).reshape(sp["ssm_a_im"].shape)
    g["ssm_log_dt"] = d_ldt.reshape(per_group).sum(axis=1).reshape(sp["ssm_log_dt"].shape)
    g["ssm_b_re"] = d_br.reshape(sp["ssm_b_re"].shape)
    g["ssm_b_im"] = d_bi.reshape(sp["ssm_b_im"].shape)
    g["ssm_c_re"] = d_cr.reshape(sp["ssm_c_re"].shape)
    g["ssm_c_im"] = d_ci.reshape(sp["ssm_c_im"].shape)
    if reducer is not None:
        travelling = travelling + [d_ar, d_br, d_cr]
    dg1, dup1 = _ffn_bwd_act("ffn1_bwd_act", dh1_b, w_f1, g1, up1, tm, after=travelling)
    dw_f1 = [a.reshape(sharded) for a in _ffn_dw("ffn1_dw", u1, dg1, dup1, a1, dh1_b, tm)]
    if reducer is not None:
        joining = reducer.join_start("a", after=reducer.finish("a", reducer.swap_start("b1", dw_f1)))
        travelling = joining + reducer.start("b", [], swapped=["b1"], after=joining)
    grad_x, _, g["ffn1_norm"] = _ffn_dx("ffn1_dx", dg1, dup1, w_f1, x, sp["ffn1_norm"], dh1, tm, after=travelling)
    if reducer is not None:
        reducer.join_finish("a", [grad_x])
    return loss, grad_x, early + dw_f1, g


def _mesh_place():
    x, y, c = lax.axis_index("x"), lax.axis_index("y"), lax.axis_index("c")
    chips = [(1 - x, y), (x, 1 - y), (1 - x, 1 - y)]
    return x, y, c, chips


def _remote(src, dst, send_sems, recv_sems, k, to):
    return pltpu.make_async_remote_copy(src_ref=src, dst_ref=dst, send_sem=send_sems.at[k], recv_sem=recv_sems.at[k],
                                        device_id=to, device_id_type=MESH)


def _sibling_swap_halves(tag, grads, after=()):
    n = len(grads)
    after_ops, after_specs = _after_operands(after)

    def body(*refs):
        ins, outs = refs[:n], refs[n + len(after_ops):2 * n + len(after_ops)]
        send_sems, recv_sems = refs[2 * n + len(after_ops):]
        x, y, c, _ = _mesh_place()
        sibling = (x, y, 1 - c)
        copies = []
        for k in range(n):
            half = grads[k].shape[1] // 2
            theirs = pl.ds(pl.multiple_of((1 - c) * half, 16), half)
            cp = _remote(ins[k].at[:, theirs, :], outs[k], send_sems, recv_sems, k, sibling)
            cp.start()
            copies.append(cp)
        for cp in copies:
            cp.wait_recv()
        for cp in copies:
            cp.wait_send()

    hbm = BS(memory_space=pl.ANY)
    return pl.pallas_call(
        body, out_shape=tuple(SDS((g.shape[0], g.shape[1] // 2, g.shape[2]), g.dtype) for g in grads),
        in_specs=[hbm] * n + after_specs, out_specs=(hbm,) * n,
        scratch_shapes=[pltpu.SemaphoreType.DMA((n,)), pltpu.SemaphoreType.DMA((n,))],
        name="reduce_sibling_send_" + tag, compiler_params=_params())(*grads, *after_ops)


def _row_tile(rows, cap=512):
    return max(r for r in range(16, cap + 1, 16) if rows % r == 0)


REDUCE_STEPS = 2


def _chip_presum(tag, grads, gots, c_idx):
    n = len(grads)
    halves = [g.shape[1] // 2 for g in grads]
    tiles = [(h // REDUCE_STEPS, g.shape[2]) for h, g in zip(halves, grads)]

    def body(c_ref, *refs):
        for k in range(n):
            refs[2 * n + k][...] = (refs[k][...].astype(F32) + refs[n + k][...].astype(F32)).astype(BF16)

    mine = [BS((None, None) + tile, lambda s, i, c_ref: (s, c_ref[0], i, 0)) for tile in tiles]
    plain = [BS((None,) + tile, lambda s, i, c_ref: (s, i, 0)) for tile in tiles]
    return list(pl.pallas_call(
        body, out_shape=tuple(SDS((g.shape[0], h, g.shape[2]), BF16) for g, h in zip(grads, halves)),
        grid_spec=pltpu.PrefetchScalarGridSpec(num_scalar_prefetch=1, grid=(N_SHARD, REDUCE_STEPS),
                                               in_specs=mine + plain, out_specs=plain),
        name="reduce_presum_" + tag, compiler_params=_params("parallel", "parallel"))(
            c_idx, *[g.reshape(g.shape[0], 2, h, g.shape[2]) for g, h in zip(grads, halves)], *gots))


HBM_SPEC = BS(memory_space=pltpu.HBM)
SEM_SPEC = BS(memory_space=pltpu.SEMAPHORE)
DATAFLOW = pltpu.SideEffectType.DATAFLOW_SIDE_EFFECTING


def _chip_exchange_copies(parts, lands, send_sems, recv_sems):
    _, _, c, chips = _mesh_place()
    return [_remote(parts[k].at[2 * px + py], lands[k].at[j], send_sems, recv_sems, 3 * k + j, (px, py, c))
            for k in range(len(parts)) for j, (px, py) in enumerate(chips)]


def _gather_copies(shards, lands, send_sems, recv_sems):
    x, y, c, chips = _mesh_place()
    return [_remote(shards[k], lands[k].at[2 * x + y], send_sems, recv_sems, 3 * k + j, (px, py, c))
            for k in range(len(shards)) for j, (px, py) in enumerate(chips)]


def _gather_half_copies(shards, lands, send_sems, recv_sems):
    x, y, c, chips = _mesh_place()
    out = []
    for k in range(len(shards)):
        half = shards[k].shape[0] // 2
        mine = pl.ds(pl.multiple_of(c * half, 16), half)
        for j, (px, py) in enumerate(chips):
            out.append(_remote(shards[k].at[mine, :], lands[k].at[2 * x + y, mine, :], send_sems, recv_sems,
                               3 * k + j, (px, py, c)))
    return out


def _sibling_fill(tag, lands):
    n = len(lands)

    def body(*refs):
        outs = refs[n:2 * n]
        send_sems, recv_sems = refs[2 * n:]
        x, y, c, chips = _mesh_place()
        copies = []
        for k in range(n):
            half = lands[k].shape[1] // 2
            mine = pl.ds(pl.multiple_of(c * half, 16), half)
            for j, (px, py) in enumerate(chips):
                blk = outs[k].at[2 * px + py, mine, :]
                copies.append(_remote(blk, blk, send_sems, recv_sems, 3 * k + j, (x, y, 1 - c)))
        for cp in copies:
            cp.start()
        for cp in copies:
            cp.wait_recv()
        for cp in copies:
            cp.wait_send()

    hbm = BS(memory_space=pl.ANY)
    return list(pl.pallas_call(
        body, out_shape=tuple(SDS(a.shape, a.dtype) for a in lands),
        in_specs=[hbm] * n, out_specs=(hbm,) * n, input_output_aliases={k: k for k in range(n)},
        scratch_shapes=[pltpu.SemaphoreType.DMA((3 * n,)), pltpu.SemaphoreType.DMA((3 * n,))],
        name="gather_fill_" + tag, compiler_params=_params())(*lands))


def _swap_copies(grads, lands, send_sems, recv_sems):
    x, y, c, _ = _mesh_place()
    out = []
    for k in range(len(grads)):
        half = grads[k].shape[1] // 2
        theirs = pl.ds(pl.multiple_of((1 - c) * half, 16), half)
        out.append(_remote(grads[k].at[:, theirs, :], lands[k], send_sems, recv_sems, k, (x, y, 1 - c)))
    return out


def _join_copies(fulls, same, send_sems, recv_sems):
    x, y, c, _ = _mesh_place()
    out = []
    for k in range(len(fulls)):
        half = fulls[k].shape[0] // 2
        mine = fulls[k].at[pl.ds(pl.multiple_of(c * half, 8), half), :]
        out.append(_remote(mine, mine, send_sems, recv_sems, k, (x, y, 1 - c)))
    return out


def _everyone_copies(packs, lands, send_sems, recv_sems):
    x, y, c, _ = _mesh_place()
    out = []
    for k in range(len(packs)):
        for j in range(N_DEV - 1):
            bx, by, bc = (j + 1) >> 2 & 1, (j + 1) >> 1 & 1, (j + 1) & 1
            peer = (x ^ bx, y ^ by, c ^ bc)
            out.append(_remote(packs[k], lands[k].at[4 * x + 2 * y + c], send_sems, recv_sems, (N_DEV - 1) * k + j, peer))
    return out


def _split_start(name, copies, sources, land_shapes, after=(), fanout=3):
    n = len(sources)
    n_land = len(land_shapes)
    m = n + n_land
    n_sems = fanout * n
    after_ops, after_specs = _after_operands(after)

    def body(*refs):
        ins = refs[:n]
        lands = refs[n:m] if n_land else ins
        send_sems, recv_sems = refs[m + len(after_ops)], refs[m + len(after_ops) + 1]
        token = refs[-1]
        for cp in copies(ins, lands, send_sems, recv_sems):
            cp.start()
        token[...] = jnp.zeros_like(token)

    lands = [pltpu.with_memory_space_constraint(lax.empty(s, d), pltpu.HBM) for s, d in land_shapes]
    sources = [pltpu.with_memory_space_constraint(p, pltpu.HBM) for p in sources]
    thru = [pltpu.HBM(a.shape, a.dtype) for a in sources + lands]
    out = pl.pallas_call(
        body, name=name,
        out_shape=(pltpu.SemaphoreType.DMA((n_sems,)), pltpu.SemaphoreType.DMA((n_sems,)), *thru, SDS((8, 128), F32)),
        in_specs=[HBM_SPEC] * m + after_specs,
        out_specs=(SEM_SPEC, SEM_SPEC, *[HBM_SPEC] * m, BS(memory_space=pltpu.VMEM)),
        input_output_aliases={i: 2 + i for i in range(m)},
        compiler_params=pltpu.CompilerParams(has_side_effects=DATAFLOW))(*sources, *lands, *after_ops)
    return out[0], out[1], list(out[2:2 + n]), list(out[2 + n:2 + m]), out[-1]


def _split_wait(name, copies, send_sems, recv_sems, sources, lands, after):
    n = len(sources)
    m = n + len(lands)
    after_ops, after_specs = _after_operands(after)

    def body(*refs):
        ins = refs[:n]
        zones = refs[n:m] if m > n else ins
        for cp in copies(ins, zones, refs[m], refs[m + 1]):
            cp.wait_send()
            cp.wait_recv()

    out = pl.pallas_call(
        body, name=name,
        out_shape=tuple(pltpu.HBM(a.shape, a.dtype) for a in sources + lands),
        in_specs=[HBM_SPEC] * m + [SEM_SPEC, SEM_SPEC] + after_specs, out_specs=(HBM_SPEC,) * m,
        input_output_aliases={i: i for i in range(m)},
        compiler_params=pltpu.CompilerParams(has_side_effects=DATAFLOW))(*sources, *lands, send_sems, recv_sems, *after_ops)
    return list(out[:n]), list(out[n:])


class _WeightGatherer:
    def __init__(self, shards):
        self.shards, self.open = shards, {}
        self.me = 2 * lax.axis_index("x") + lax.axis_index("y")

    HALVED = ("f1a", "win", "mix")

    def start(self, tag, after=()):
        shapes = [((N_SHARD,) + s.shape, s.dtype) for s in self.shards[tag]]
        copies = _gather_half_copies if tag in self.HALVED else _gather_copies
        self.open[tag] = _split_start("gather_start_" + tag, copies, self.shards[tag], shapes, after)
        return [self.open[tag][-1]]

    def sources(self, tags):
        return [s for tag in tags for s in self.shards[tag]]

    def finish(self, tag, after):
        send_sems, recv_sems, shards, lands, _ = self.open.pop(tag)
        copies = _gather_half_copies if tag in self.HALVED else _gather_copies
        shards, lands = _split_wait("gather_wait_" + tag, copies, send_sems, recv_sems, shards, lands, after)
        if tag in self.HALVED:
            lands = _sibling_fill(tag, lands)
        return [lax.dynamic_update_slice(zone, s[None], (self.me, 0, 0)) for zone, s in zip(lands, shards)]


class _GradReducer:
    def __init__(self):
        self.c_idx = lax.axis_index("c").astype(jnp.int32).reshape(1)
        self.place = jnp.stack([2 * lax.axis_index("x") + lax.axis_index("y"), lax.axis_index("c")]).astype(jnp.int32)
        self.swaps, self.open, self.landed, self.joins, self.reduced = {}, {}, {}, {}, []

    def swap_start(self, tag, grads, after=()):
        shapes = [((g.shape[0], g.shape[1] // 2, g.shape[2]), g.dtype) for g in grads]
        self.swaps[tag] = _split_start("reduce_swap_start_" + tag, _swap_copies, grads, shapes, after, fanout=1)
        return [self.swaps[tag][-1]]

    def start(self, tag, grads, after=(), swapped=()):
        pairs = []
        for s in swapped:
            send_sems, recv_sems, early, lands, _ = self.swaps.pop(s)
            behind = grads[-1:] or list(after)
            pairs += zip(*_split_wait("reduce_swap_wait_" + s, _swap_copies, send_sems, recv_sems, early, lands, behind))
        if grads:
            pairs += zip(grads, _sibling_swap_halves(tag, grads, after))
        parts = _chip_presum(tag, [g for g, _ in pairs], [s for _, s in pairs], self.c_idx)
        shapes = [((3,) + p.shape[1:], p.dtype) for p in parts]
        self.open[tag] = _split_start("reduce_exchange_start_" + tag, _chip_exchange_copies, parts, shapes)
        return [self.open[tag][-1]]

    def finish(self, tag, after):
        send_sems, recv_sems, parts, lands, _ = self.open.pop(tag)
        self.landed[tag] = _split_wait("reduce_exchange_wait_" + tag, _chip_exchange_copies, send_sems, recv_sems, parts, lands, after)
        return self.landed[tag][1][:1]

    def _sums(self, tag, after=()):
        parts, landed = self.landed.pop(tag)
        return _chip_sum(tag, parts, landed, self.place, after)

    def join_start(self, tag, after=()):
        self.joins[tag] = _split_start("reduce_join_start_" + tag, _join_copies, self._sums(tag, after), [], fanout=1)
        return [self.joins[tag][-1]]

    def join_finish(self, tag, after):
        send_sems, recv_sems, fulls, _, _ = self.joins.pop(tag)
        self.reduced += _split_wait("reduce_join_wait_" + tag, _join_copies, send_sems, recv_sems, fulls, [], after)[0]

    def join(self, tag, after=()):
        self.reduced += _sibling_join_halves(self._sums(tag), after)


def _chip_sum(tag, parts, gots, place, after=()):
    n = len(parts)
    tiles = [(p.shape[1] // REDUCE_STEPS, p.shape[2]) for p in parts]
    after_ops, after_specs = _after_operands(after)

    def body(place_ref, *refs):
        outs = refs[2 * n + len(after_ops):]
        for k in range(n):
            acc = refs[k][...].astype(F32)
            for j in range(3):
                acc = acc + refs[n + k][j].astype(F32)
            outs[k][...] = acc

    return list(pl.pallas_call(
        body, out_shape=tuple(SDS((2 * p.shape[1], p.shape[2]), F32) for p in parts),
        grid_spec=pltpu.PrefetchScalarGridSpec(
            num_scalar_prefetch=1, grid=(REDUCE_STEPS,),
            in_specs=[BS((None,) + tile, lambda i, place_ref: (place_ref[0], i, 0)) for tile in tiles]
            + [BS((3,) + tile, lambda i, place_ref: (0, i, 0)) for tile in tiles] + after_specs,
            out_specs=[BS(tile, lambda i, place_ref: (place_ref[1] * REDUCE_STEPS + i, 0)) for tile in tiles]),
        name="reduce_sum_" + tag, compiler_params=_params("parallel"))(place, *parts, *gots, *after_ops))


def _sibling_join_halves(fulls, after=()):
    n = len(fulls)
    after_ops, after_specs = _after_operands(after)

    def body(*refs):
        outs = refs[n + len(after_ops):2 * n + len(after_ops)]
        send_sems, recv_sems = refs[2 * n + len(after_ops):]
        copies = _join_copies(outs, outs, send_sems, recv_sems)
        for cp in copies:
            cp.start()
        for cp in copies:
            cp.wait_recv()
        for cp in copies:
            cp.wait_send()

    hbm = BS(memory_space=pl.ANY)
    return list(pl.pallas_call(
        body, out_shape=tuple(SDS(f.shape, f.dtype) for f in fulls),
        in_specs=[hbm] * n + after_specs, out_specs=(hbm,) * n, input_output_aliases={k: k for k in range(n)},
        scratch_shapes=[pltpu.SemaphoreType.DMA((n,)), pltpu.SemaphoreType.DMA((n,))],
        name="reduce_sibling_join", compiler_params=_params())(*fulls, *after_ops))


N_DEV = 8


def _sum_devices(packs):
    _, rows, lanes = packs.shape

    def body(p_ref, o_ref):
        acc = p_ref[0]
        for dev in range(1, N_DEV):
            acc = acc + p_ref[dev]
        o_ref[...] = acc

    vm = BS(memory_space=pltpu.VMEM)
    return pl.pallas_call(body, out_shape=SDS((rows, lanes), F32), in_specs=[vm], out_specs=vm,
                          name="small_sum", compiler_params=_params())(packs)


def _adamw_refs(w_ref, g_ref, m_ref, v_ref, go_ref, d_ref, mo_ref, vo_ref):
    bc1 = 1.0 - ADAM_B1 ** ADAM_STEP
    bc2 = 1.0 - ADAM_B2 ** ADAM_STEP
    g = g_ref[...]
    m_new = ADAM_B1 * m_ref[...] + (1.0 - ADAM_B1) * g
    v_new = ADAM_B2 * v_ref[...] + (1.0 - ADAM_B2) * (g * g)
    go_ref[...] = g
    mo_ref[...] = m_new
    vo_ref[...] = v_new
    d_ref[...] = -ADAM_LR * ((m_new / bc1) / (jnp.sqrt(v_new / bc2) + ADAM_EPS) + ADAM_WD * w_ref[...])


def _adamw_small(ws, gs, ms, vs):
    n = len(ws)

    def body(*refs):
        for k in range(n):
            _adamw_refs(*[refs[j * n + k] for j in range(4)], *refs[4 * n + 4 * k:4 * n + 4 * k + 4])

    vm = BS(memory_space=pltpu.VMEM)
    outs = pl.pallas_call(
        body, out_shape=tuple(SDS(a.shape, F32) for a in ws for _ in range(4)), in_specs=[vm] * (4 * n),
        out_specs=(vm,) * (4 * n), name="adamw_small", compiler_params=_params())(*ws, *gs, *ms, *vs)
    return [outs[4 * k:4 * k + 4] for k in range(n)]


def _adamw(name, w, grad, row0, m, v, after=()):
    rows, cols = w.shape
    tr = rows if rows < 16 else _row_tile(rows, 352)
    after_ops, after_specs = _after_operands(after)

    def body(w_ref, g_ref, m_ref, v_ref, *rest):
        _adamw_refs(w_ref, g_ref, m_ref, v_ref, *rest[len(after_ops):])

    blk = BS((tr, cols), lambda i: (i, 0))
    shape = SDS((rows, cols), F32)
    return pl.pallas_call(
        body, out_shape=(shape,) * 4, grid=(rows // tr,),
        in_specs=[blk, BS((tr, cols), lambda i: (row0 // tr + i, 0)), blk, blk] + after_specs, out_specs=(blk,) * 4,
        name=name, compiler_params=_params("parallel"))(w, grad, m, v, *after_ops)


SMALL_LANES = 128


SMALL_TILE = 8 * SMALL_LANES


def _packed_rows(p):
    return -(-p.size // SMALL_TILE) * 8


def _pack_small(parts):
    tiles = [jnp.pad(jnp.ravel(p), (0, _packed_rows(p) * SMALL_LANES - p.size)).reshape(-1, SMALL_LANES) for p in parts]
    rows = sum(t.shape[0] for t in tiles)
    return jnp.concatenate(tiles + [jnp.zeros((-rows % 64, SMALL_LANES), F32)], axis=0)


def _unpack_small(packed, like):
    out, at = [], 0
    for p in like:
        rows = _packed_rows(p)
        out.append(jnp.ravel(packed[at:at + rows])[:p.size].reshape(p.shape))
        at += rows
    return out


def kernel(x, mem, ffn1_norm, ffn1_w_gate, ffn1_w_up, ffn1_w_down, mix_norm, w_in, pool_w, pool_scale, w_pool_proj, ssm_a_re, ssm_a_im, ssm_log_dt, ssm_b_re, ssm_b_im, ssm_c_re, ssm_c_im, ssm_d, w_glu_val, w_glu_gate, w_mix_out, xattn_norm, mem_norm, w_q, w_kv, w_xo, ffn2_norm, ffn2_w_gate, ffn2_w_up, ffn2_w_down, final_norm, loss_target, m_ffn1_norm, m_ffn1_w_gate, m_ffn1_w_up, m_ffn1_w_down, m_mix_norm, m_w_in, m_pool_w, m_pool_scale, m_w_pool_proj, m_ssm_a_re, m_ssm_a_im, m_ssm_log_dt, m_ssm_b_re, m_ssm_b_im, m_ssm_c_re, m_ssm_c_im, m_ssm_d, m_w_glu_val, m_w_glu_gate, m_w_mix_out, m_xattn_norm, m_mem_norm, m_w_q, m_w_kv, m_w_xo, m_ffn2_norm, m_ffn2_w_gate, m_ffn2_w_up, m_ffn2_w_down, m_final_norm, v_ffn1_norm, v_ffn1_w_gate, v_ffn1_w_up, v_ffn1_w_down, v_mix_norm, v_w_in, v_pool_w, v_pool_scale, v_w_pool_proj, v_ssm_a_re, v_ssm_a_im, v_ssm_log_dt, v_ssm_b_re, v_ssm_b_im, v_ssm_c_re, v_ssm_c_im, v_ssm_d, v_w_glu_val, v_w_glu_gate, v_w_mix_out, v_xattn_norm, v_mem_norm, v_w_q, v_w_kv, v_w_xo, v_ffn2_norm, v_ffn2_w_gate, v_ffn2_w_up, v_ffn2_w_down, v_final_norm):
    given = dict(locals())
    w = {n: given[n] for n in WEIGHTS}
    m = {n: given["m_" + n] for n in WEIGHTS}
    v = {n: given["v_" + n] for n in WEIGHTS}

    def shard_view(a, n):
        return a[0].T if n in TRANSPOSED else a[0]

    def shard_unview(a, n):
        return (a.T if n in TRANSPOSED else a)[None]

    shards = {tag: [jnp.concatenate([shard_view(w[n], n).astype(BF16) for n in grp], axis=0) for grp in arrays]
              for tag, arrays in GATHER_PHASES.items()}
    reducer = _GradReducer()
    ws, ms, vs = ({n: _small_view(a[n], n) for n in SMALL} for a in (w, m, v))
    loss_part, grad_x, _, small = _device_step(x[0], mem[0], loss_target[0], _WeightGatherer(shards), ws, reducer)

    small_like = [ws[n] for n in SMALL] + [loss_part[0, :1]]
    pack = _pack_small([small[n] for n in SMALL] + [loss_part[0, :1]])
    everyone = _split_start("small_start", _everyone_copies, [pack], [((N_DEV,) + pack.shape, F32)], fanout=N_DEV - 1)

    grads, delta, new_m, new_v = {}, {}, {}, {}
    big_done = []

    def update(groups, reduced):
        for grp, red in zip(groups, reduced):
            row0 = 0
            for n in grp:
                w_n = shard_view(w[n], n)
                outs = _adamw("adamw_" + n, w_n, red, row0, shard_view(m[n], n), shard_view(v[n], n), after=everyone[-1:])
                grads[n], delta[n], new_m[n], new_v[n] = (shard_unview(o, n) for o in outs)
                big_done.append(outs[1])
                row0 += w_n.shape[0]

    n_a = len(reducer.reduced)
    update(REDUCE_GROUPS[:n_a], reducer.reduced)
    reducer.finish("b", list(big_done))
    reducer.join("b")
    update(REDUCE_GROUPS[n_a:], reducer.reduced[n_a:])

    send_sems, recv_sems, packs, landed, _ = everyone
    packs, landed = _split_wait("small_wait", _everyone_copies, send_sems, recv_sems, packs, landed, big_done)
    mine = 4 * lax.axis_index("x") + 2 * lax.axis_index("y") + lax.axis_index("c")
    summed = _sum_devices(lax.dynamic_update_slice(landed[0], packs[0][None], (mine, 0, 0)))
    g_small = dict(zip(SMALL + ("loss",), _unpack_small(summed, small_like)))
    loss = g_small.pop("loss").reshape(())
    def two_d(a):
        return a.reshape(-1, a.shape[-1])

    updated = _adamw_small(*([two_d(a[n]) for n in SMALL] for a in (ws, g_small, ms, vs)))
    for n, outs in zip(SMALL, updated):
        grads[n], delta[n], new_m[n], new_v[n] = (_small_view(o.reshape(ws[n].shape), n) for o in outs)

    return (loss, grad_x[None], *[grads[n] for n in WEIGHTS], *[delta[n] for n in WEIGHTS],
            *[new_m[n] for n in WEIGHTS], *[new_v[n] for n in WEIGHTS])
```

```python
import functools
import math

import jax
import jax.numpy as jnp
from jax import lax
from jax.experimental import pallas as pl
from jax.experimental.pallas import tpu as pltpu

F32 = jnp.float32
BF16 = jnp.bfloat16
SDS = jax.ShapeDtypeStruct
BS = pl.BlockSpec
MESH = pl.DeviceIdType.MESH

D_MODEL = 1024
D_FF = 2816
N_SHARD = 4
FF_SH = D_FF // N_SHARD
D_POOL = 512
POOL_WINDOWS = (2, 4, 8, 16)
POOL_GROUP = 128
D_SSM = 256
SSM_GROUPS = 16
SSM_GROUP = 16
SSM_STATE = 64
SSM_CH = SSM_GROUPS * SSM_STATE
N_HEADS = 4
HEAD_DIM = 256
EPS = 1e-6
ADAM_LR, ADAM_B1, ADAM_B2, ADAM_EPS, ADAM_WD, ADAM_STEP = 0.001, 0.9, 0.999, 1e-08, 0.01, 10

VMEM_LIMIT_V7X = 58 * 1024 * 1024
TM = 512

NN = (((1,), (0,)), ((), ()))
NT = (((1,), (1,)), ((), ()))
TN = (((0,), (0,)), ((), ()))


def _params(*sem):
    return pltpu.CompilerParams(dimension_semantics=sem if sem else None, vmem_limit_bytes=VMEM_LIMIT_V7X)


def _dot(a, b, dims=NN):
    return lax.dot_general(a.astype(BF16), b.astype(BF16), dims, preferred_element_type=F32)


def _sigmoid(v):
    return pl.reciprocal(1.0 + jnp.exp(-v), approx=True)


def _block_dims(spec):
    return tuple(d for d in spec.block_shape if d is not None)


def _after_operands(after):
    return list(after), [BS(memory_space=pl.ANY)] * len(after)


def _mm(name, pairs, *, grid, out_shape, out_spec, red_axis=None, extras=(), epilogue=None, after=()):
    n_pairs, n_extra = len(pairs), len(extras)
    n_red = grid[red_axis] if red_axis is not None else 1
    dims = [p[4] for p in pairs]

    def body(*refs):
        ab = refs[:2 * n_pairs]
        ex = refs[2 * n_pairs:2 * n_pairs + n_extra]
        o_ref = refs[2 * n_pairs + n_extra + len(after)]

        def partial():
            acc = None
            for p in range(n_pairs):
                t = _dot(ab[2 * p][...], ab[2 * p + 1][...], dims[p])
                acc = t if acc is None else acc + t
            return acc

        def finish(acc):
            res = epilogue(acc, *[e[...] for e in ex]) if epilogue is not None else acc
            o_ref[...] = res.astype(o_ref.dtype)

        if n_red == 1:
            finish(partial())
        else:
            acc_ref = refs[-1]
            k = pl.program_id(red_axis)

            @pl.when(k == 0)
            def _():
                acc_ref[...] = jnp.zeros_like(acc_ref)

            acc_ref[...] += partial()

            @pl.when(k == n_red - 1)
            def _():
                finish(acc_ref[...])

    operands, in_specs = [], []
    for a, a_spec, b, b_spec, _ in pairs:
        operands += [a, b]
        in_specs += [a_spec, b_spec]
    for e, e_spec in extras:
        operands.append(e)
        in_specs.append(e_spec)
    after_ops, after_specs = _after_operands(after)
    operands += after_ops
    in_specs += after_specs
    scratch = [pltpu.VMEM(_block_dims(out_spec), F32)] if n_red > 1 else []
    sem = tuple("arbitrary" if ax == red_axis else "parallel" for ax in range(len(grid)))
    return pl.pallas_call(body, out_shape=out_shape, grid=grid, in_specs=in_specs, out_specs=out_spec,
                          scratch_shapes=scratch, name=name, compiler_params=_params(*sem))(*operands)


def _rmsnorm(name, h, gain, tm, after=()):
    t, d = h.shape
    after_ops, after_specs = _after_operands(after)

    def body(h_ref, g_ref, *rest):
        u_ref = rest[-1]
        hv = h_ref[...]
        r = lax.rsqrt(jnp.mean(hv * hv, axis=-1, keepdims=True) + EPS)
        u_ref[...] = ((hv * r) * g_ref[...]).astype(u_ref.dtype)

    return pl.pallas_call(
        body, out_shape=SDS((t, d), BF16), grid=(t // tm,),
        in_specs=[BS((tm, d), lambda i: (i, 0)), BS((1, d), lambda i: (0, 0))] + after_specs,
        out_specs=BS((tm, d), lambda i: (i, 0)), name=name, compiler_params=_params("parallel"))(h, gain, *after_ops)


def _rmsnorm_bwd(name, h, gain, du, dh_in, tm):
    t, d = h.shape
    has_in = dh_in is not None

    def body(*refs):
        if has_in:
            h_ref, g_ref, du_ref, dhin_ref, dh_ref, dhb_ref, dg_ref = refs
        else:
            h_ref, g_ref, du_ref, dh_ref, dhb_ref, dg_ref = refs
        i = pl.program_id(0)
        hv = h_ref[...]
        r = lax.rsqrt(jnp.mean(hv * hv, axis=-1, keepdims=True) + EPS)
        n = hv * r
        duv = du_ref[...].astype(F32)
        dn = duv * g_ref[...]
        dh = r * (dn - n * jnp.mean(dn * n, axis=-1, keepdims=True))
        if has_in:
            dh = dhin_ref[...] + dh
        dh_ref[...] = dh
        dhb_ref[...] = dh.astype(BF16)

        @pl.when(i == 0)
        def _():
            dg_ref[...] = jnp.zeros_like(dg_ref)

        dg_ref[...] += jnp.sum(duv * n, axis=0, keepdims=True)

    row = BS((tm, d), lambda i: (i, 0))
    vec = BS((1, d), lambda i: (0, 0))
    operands = [h, gain, du] + ([dh_in] if has_in else [])
    in_specs = [row, vec, row] + ([row] if has_in else [])
    return pl.pallas_call(
        body, out_shape=(SDS((t, d), F32), SDS((t, d), BF16), SDS((1, d), F32)), grid=(t // tm,),
        in_specs=in_specs, out_specs=(row, row, vec), name=name, compiler_params=_params("arbitrary"))(*operands)


def _loss_head_tile(i, hv, g_ref, t_ref, loss_ref, dh_ref, dhb_ref, dg_ref):
    g = g_ref[...]
    r = lax.rsqrt(jnp.mean(hv * hv, axis=-1, keepdims=True) + EPS)
    n = hv * r
    err = n * g - t_ref[...]
    dy = err * (1.0 / hv.shape[-1])
    dn = dy * g
    dh = r * (dn - n * jnp.mean(dn * n, axis=-1, keepdims=True))
    dh_ref[...] = dh
    dhb_ref[...] = dh.astype(BF16)

    @pl.when(i == 0)
    def _():
        dg_ref[...] = jnp.zeros_like(dg_ref)
        loss_ref[...] = jnp.zeros_like(loss_ref)

    dg_ref[...] += jnp.sum(dy * n, axis=0, keepdims=True)
    part = 0.5 * jnp.sum(jnp.mean(err * err, axis=-1, keepdims=True), axis=0, keepdims=True)
    loss_ref[...] += jnp.broadcast_to(part, loss_ref.shape)


def _norm_tile(h, g_ref, u_ref):
    r = lax.rsqrt(jnp.mean(h * h, axis=-1, keepdims=True) + EPS)
    u_ref[...] = ((h * r) * g_ref[...]).astype(u_ref.dtype)


FFN_BLOCK = D_FF // 2


def _ffn_up(name, u, w_f, tm, after=(), gate=None):
    t, d = u.shape
    after_ops, after_specs = _after_operands(after)
    given = gate is not None

    def body(u_ref, g_ref, wu_ref, *rest):
        pg_ref, pu_ref, a_ref = rest[len(after_ops):]
        uv = u_ref[...]
        for lo in range(0, D_FF, FFN_BLOCK):
            cols = slice(lo, lo + FFN_BLOCK)
            g = g_ref[:, cols] if given else _dot(uv, g_ref[cols, :], NT)
            up = _dot(uv, wu_ref[cols, :], NT)
            sg = _sigmoid(g)
            silu = g * sg
            a_ref[:, cols] = (silu * up).astype(BF16)
            pu_ref[:, cols] = (0.5 * silu).astype(BF16)
            pg_ref[:, cols] = (0.5 * sg * (1.0 + g * (1.0 - sg)) * up).astype(BF16)

    hid = BS((tm, D_FF), lambda i: (i, 0))
    shape = SDS((t, D_FF), BF16)
    whole = BS((D_FF, d), lambda i: (0, 0))
    return pl.pallas_call(
        body, out_shape=(shape, shape, shape), grid=(t // tm,),
        in_specs=[BS((tm, d), lambda i: (i, 0)), hid if given else whole, whole] + after_specs,
        out_specs=(hid, hid, hid), name=name,
        compiler_params=_params("parallel"))(u, gate if given else w_f["gate"], w_f["up"], *after_ops)


def _ffn_down(name, a, w_f, resid, tm, next_gain=None, head=None):
    t, d = resid.shape
    row = BS((tm, d), lambda i: (i, 0))
    vec = BS((1, d), lambda i: (0, 0))

    def body(a_ref, w_ref, res_ref, *rest):
        h = res_ref[...] + 0.5 * _dot(a_ref[...], w_ref[...])
        if head is not None:
            _loss_head_tile(pl.program_id(0), h, *rest)
        else:
            g_ref, h_ref, u_ref = rest
            h_ref[...] = h
            _norm_tile(h, g_ref, u_ref)

    if head is not None:
        extra, extra_specs = list(head), [vec, row]
        out_shape = (SDS((1, 128), F32), SDS((t, d), F32), SDS((t, d), BF16), SDS((1, d), F32))
        out_specs = (BS((1, 128), lambda i: (0, 0)), row, row, vec)
    else:
        extra, extra_specs = [next_gain], [vec]
        out_shape = (SDS((t, d), F32), SDS((t, d), BF16))
        out_specs = (row, row)
    return pl.pallas_call(
        body, out_shape=out_shape, grid=(t // tm,),
        in_specs=[BS((tm, D_FF), lambda i: (i, 0)), BS((D_FF, d), lambda i: (0, 0)), row] + extra_specs,
        out_specs=out_specs, name=name,
        compiler_params=_params("arbitrary" if head is not None else "parallel"))(a, w_f["down"], resid, *extra)


def _mix_in(u, w_t, tm, after=()):
    t, d = u.shape
    after_ops, after_specs = _after_operands(after)

    def body(u_ref, w_ref, *rest):
        o_ref, s_ref = rest[-2:]
        uv = u_ref[...]
        for lo in range(0, D_FF, FFN_BLOCK):
            o_ref[:, lo:lo + FFN_BLOCK] = _dot(uv, w_ref[lo:lo + FFN_BLOCK, :], NT)
        s_ref[...] = o_ref[:, D_POOL:D_POOL + D_SSM].astype(BF16)

    return pl.pallas_call(
        body, out_shape=(SDS((t, D_FF), F32), SDS((t, D_SSM), BF16)), grid=(t // tm,),
        in_specs=[BS((tm, d), lambda i: (i, 0)), BS((D_FF, d), lambda i: (0, 0))] + after_specs,
        out_specs=(BS((tm, D_FF), lambda i: (i, 0)), BS((tm, D_SSM), lambda i: (i, 0))), name="mix_in",
        compiler_params=_params("parallel"))(u, w_t, *after_ops)


def _mm_resid_norm(name, a, b, resid, next_gain, tm):
    t, d = resid.shape
    tm = min(2 * tm, t)

    def body(a_ref, b_ref, res_ref, g_ref, h_ref, u_ref):
        h = res_ref[...] + _dot(a_ref[...], b_ref[...])
        h_ref[...] = h
        _norm_tile(h, g_ref, u_ref)

    row = BS((tm, d), lambda i: (i, 0))
    return pl.pallas_call(
        body, out_shape=(SDS((t, d), F32), SDS((t, d), BF16)), grid=(t // tm,),
        in_specs=[BS((tm, a.shape[1]), lambda i: (i, 0)), BS(b.shape, lambda i: (0, 0)), row, BS((1, d), lambda i: (0, 0))],
        out_specs=(row, row), name=name, compiler_params=_params("parallel"))(a, b, resid, next_gain)


def _ffn_bwd_act(name, dh_b, w_f, pg, pu, tm, after=()):
    t, d = dh_b.shape
    after_ops, after_specs = _after_operands(after)

    def body(dh_ref, wd_ref, pg_ref, pu_ref, *rest):
        dg_ref, dup_ref = rest[len(after_ops):]
        dh = dh_ref[...]
        for lo in range(0, D_FF, FFN_BLOCK):
            cols = slice(lo, lo + FFN_BLOCK)
            da = _dot(dh, wd_ref[cols, :], NT)
            dg_ref[:, cols] = (da * pg_ref[:, cols].astype(F32)).astype(BF16)
            dup_ref[:, cols] = (da * pu_ref[:, cols].astype(F32)).astype(BF16)

    hid = BS((tm, D_FF), lambda i: (i, 0))
    shape = SDS((t, D_FF), BF16)
    return pl.pallas_call(
        body, out_shape=(shape, shape), grid=(t // tm,),
        in_specs=[BS((tm, d), lambda i: (i, 0)), BS((D_FF, d), lambda i: (0, 0)), hid, hid] + after_specs,
        out_specs=(hid, hid), name=name,
        compiler_params=_params("parallel"))(dh_b, w_f["down"], pg, pu, *after_ops)


def _ffn_dw(name, u, dg, dup, a, dh_b, tm):
    t, d = u.shape
    n_t = t // tm

    def body(u_ref, dg_ref, dup_ref, a_ref, dh_ref, og_ref, ou_ref, od_ref, acc):
        i = pl.program_id(1)

        @pl.when(i == 0)
        def _():
            acc[...] = jnp.zeros_like(acc)

        uv = u_ref[...]
        acc[0] += _dot(dg_ref[...], uv, TN)
        acc[1] += _dot(dup_ref[...], uv, TN)
        acc[2] += _dot(a_ref[...], dh_ref[...], TN)

        @pl.when(i == n_t - 1)
        def _():
            og_ref[...] = acc[0].astype(BF16)
            ou_ref[...] = acc[1].astype(BF16)
            od_ref[...] = (0.5 * acc[2]).astype(BF16)

    hid = BS((tm, FFN_BLOCK), lambda j, i: (i, j))
    row = BS((tm, d), lambda j, i: (i, 0))
    out = BS((FFN_BLOCK, d), lambda j, i: (j, 0))
    shape = SDS((D_FF, d), BF16)
    return pl.pallas_call(
        body, out_shape=(shape, shape, shape), grid=(D_FF // FFN_BLOCK, n_t),
        in_specs=[row, hid, hid, hid, row], out_specs=(out, out, out),
        scratch_shapes=[pltpu.VMEM((3, FFN_BLOCK, d), F32)],
        name=name, compiler_params=_params("parallel", "arbitrary"))(u, dg, dup, a, dh_b)


def _norm_bwd_tile(i, du, h_ref, g_ref, dhin_ref, dh_ref, dhb_ref, dg_ref):
    hv = h_ref[...]
    r = lax.rsqrt(jnp.mean(hv * hv, axis=-1, keepdims=True) + EPS)
    n = hv * r
    dn = du * g_ref[...]
    dh = dhin_ref[...] + r * (dn - n * jnp.mean(dn * n, axis=-1, keepdims=True))
    dh_ref[...] = dh
    dhb_ref[...] = dh.astype(BF16)

    @pl.when(i == 0)
    def _():
        dg_ref[...] = jnp.zeros_like(dg_ref)

    dg_ref[...] += jnp.sum(du * n, axis=0, keepdims=True)


def _norm_bwd_specs(tm):
    row = BS((tm, D_MODEL), lambda i: (i, 0))
    vec = BS((1, D_MODEL), lambda i: (0, 0))
    return [row, vec, row], (row, row, vec)


def _norm_bwd_shapes(t):
    return SDS((t, D_MODEL), F32), SDS((t, D_MODEL), BF16), SDS((1, D_MODEL), F32)


def _ffn_dx(name, dg, dup, w_f, h, gain, dh_in, tm, after=()):
    t = dg.shape[0]
    tm = tm // 2
    after_ops, after_specs = _after_operands(after)

    def body(dg_ref, dup_ref, wg_ref, wu_ref, h_ref, g_ref, dhin_ref, *rest):
        du = _dot(dg_ref[...], wg_ref[...]) + _dot(dup_ref[...], wu_ref[...])
        _norm_bwd_tile(pl.program_id(0), du, h_ref, g_ref, dhin_ref, *rest[len(after_ops):])

    hid = BS((tm, D_FF), lambda i: (i, 0))
    whole = BS((D_FF, D_MODEL), lambda i: (0, 0))
    norm_in, norm_out = _norm_bwd_specs(tm)
    return pl.pallas_call(
        body, out_shape=_norm_bwd_shapes(t), grid=(t // tm,),
        in_specs=[hid, hid, whole, whole] + norm_in + after_specs, out_specs=norm_out, name=name,
        compiler_params=_params("arbitrary"))(dg, dup, w_f["gate"], w_f["up"], h, gain, dh_in, *after_ops)


def _mm_norm_bwd(name, a, b, dims, h, gain, dh_in, tm):
    pieces = list(a) if isinstance(a, (list, tuple)) else [a]
    assert len(pieces) == 1 or dims == NN
    t = pieces[0].shape[0]
    widths = [p.shape[1] for p in pieces]
    row0 = [sum(widths[:j]) for j in range(len(pieces))]

    def body(*refs):
        a_refs, (b_ref, h_ref, g_ref, dhin_ref), outs = refs[:len(pieces)], refs[len(pieces):len(pieces) + 4], refs[len(pieces) + 4:]
        if len(pieces) == 1:
            du = _dot(a_refs[0][...], b_ref[...], dims)
        else:
            du = _dot(a_refs[0][...], b_ref[0:widths[0], :])
            for a_ref, r0, w in zip(a_refs[1:], row0[1:], widths[1:]):
                du = du + _dot(a_ref[...], b_ref[r0:r0 + w, :])
        _norm_bwd_tile(pl.program_id(0), du, h_ref, g_ref, dhin_ref, *outs)

    norm_in, norm_out = _norm_bwd_specs(tm)
    return pl.pallas_call(
        body, out_shape=_norm_bwd_shapes(t), grid=(t // tm,),
        in_specs=[BS((tm, w), lambda i: (i, 0)) for w in widths] + [BS(b.shape, lambda i: (0, 0))] + norm_in,
        out_specs=norm_out, name=name, compiler_params=_params("arbitrary"))(*pieces, b, h, gain, dh_in)


def _plain_mm(name, a, b, dims, out_dtype, tm, resid=None, after=()):
    t = a.shape[0]
    tm = min(2 * tm, t)
    n = b.shape[1] if dims == NN else b.shape[0]
    extras = [(resid, BS((tm, n), lambda i: (i, 0)))] if resid is not None else []
    epi = (lambda acc, res: res + acc) if resid is not None else None
    return _mm(name, [(a, BS((tm, a.shape[1]), lambda i: (i, 0)), b, BS(b.shape, lambda i: (0, 0)), dims)],
               grid=(t // tm,), out_shape=SDS((t, n), out_dtype), out_spec=BS((tm, n), lambda i: (i, 0)),
               extras=extras, epilogue=epi, after=after)


def _dw_mm(name, a, b, tm, out_dtype=BF16, after=()):
    t, k = a.shape
    n = b.shape[1]
    tm = min(2 * tm, t)
    return _mm(name, [(a, BS((tm, k), lambda i: (i, 0)), b, BS((tm, n), lambda i: (i, 0)), TN)],
               grid=(t // tm,), red_axis=0, out_shape=SDS((k, n), out_dtype), out_spec=BS((k, n), lambda i: (0, 0)),
               after=after)


def _mix_in_dw(pieces, u, tm):
    t, d = u.shape
    tm = min(2 * tm, t)
    n_steps = t // tm
    widths = [p.shape[1] for p in pieces]
    row0 = [sum(widths[:j]) for j in range(len(pieces))]
    assert sum(widths) == D_FF

    def body(*refs):
        p_refs, u_ref, o_ref, acc_ref = refs[:len(pieces)], refs[len(pieces)], refs[-2], refs[-1]
        k = pl.program_id(0)

        @pl.when(k == 0)
        def _():
            acc_ref[...] = jnp.zeros_like(acc_ref)

        uv = u_ref[...]
        for p_ref, r0, w in zip(p_refs, row0, widths):
            acc_ref[r0:r0 + w, :] += _dot(p_ref[...], uv, TN)

        @pl.when(k == n_steps - 1)
        def _():
            o_ref[...] = acc_ref[...].astype(BF16)

    return pl.pallas_call(
        body, out_shape=SDS((D_FF, d), BF16), grid=(n_steps,),
        in_specs=[BS((tm, w), lambda i: (i, 0)) for w in widths] + [BS((tm, d), lambda i: (i, 0))],
        out_specs=BS((D_FF, d), lambda i: (0, 0)), scratch_shapes=[pltpu.VMEM((D_FF, d), F32)],
        name="mix_in_dw", compiler_params=_params("arbitrary"))(*pieces, u)


POOL_CHUNK = 256
POOL_HALO = 8


def _window_sum(v, width, lead):
    n = v.shape[0]
    s = v
    k = 1
    while k < width:
        s = s + pltpu.roll(s, n - k, 0)
        k *= 2
    return pltpu.roll(s, lead, 0) if lead else s


def _pool_count(base, left, right, t, shape):
    pos = base + lax.broadcasted_iota(jnp.int32, shape, 0)
    lo = jnp.maximum(pos - left, 0)
    hi = jnp.minimum(pos + right + 1, t)
    return (hi - lo).astype(F32)


def _pool_fwd(proj, pool_w, pool_scale):
    t = proj.shape[0]
    c, h = POOL_CHUNK, POOL_HALO
    n_chunks = t // c

    def body(proj_hbm, pw_ref, sc_ref, pooled_ref, mixed_ref, ms_ref, pad_ref, sem):
        cp = pltpu.make_async_copy(proj_hbm.at[:, pl.ds(0, D_POOL)], pad_ref.at[pl.ds(h, t), :], sem)
        cp.start()
        pad_ref[pl.ds(0, h), :] = jnp.zeros((h, D_POOL), F32)
        pad_ref[pl.ds(t + h, h), :] = jnp.zeros((h, D_POOL), F32)
        cp.wait()
        for g, width in enumerate(POOL_WINDOWS):
            left = width // 2
            right = width - 1 - left
            cols = slice(g * POOL_GROUP, (g + 1) * POOL_GROUP)
            wmat = pw_ref[g].astype(BF16)
            scale = sc_ref[:, cols]

            def chunk(ci, carry, left=left, right=right, width=width, cols=cols, wmat=wmat, scale=scale):
                base = pl.multiple_of(ci * c, c)
                v = pad_ref[pl.ds(base, c + 2 * h), cols]
                win = _window_sum(v, width, left)[h:h + c]
                cnt = _pool_count(base, left, right, t, (c, POOL_GROUP))
                pooled = (win / cnt - v[h:h + c]).astype(BF16)
                mixed = _dot(pooled, wmat)
                pooled_ref[pl.ds(base, c), cols] = pooled
                mixed_ref[pl.ds(base, c), cols] = mixed.astype(BF16)
                ms_ref[pl.ds(base, c), cols] = (mixed * scale).astype(BF16)
                return carry

            lax.fori_loop(0, n_chunks, chunk, 0)

    vm = BS(memory_space=pltpu.VMEM)
    shape = SDS((t, D_POOL), BF16)
    return pl.pallas_call(
        body, out_shape=(shape, shape, shape),
        in_specs=[BS(memory_space=pl.ANY), vm, vm], out_specs=(vm, vm, vm),
        scratch_shapes=[pltpu.VMEM((t + 2 * h, D_POOL), F32), pltpu.SemaphoreType.DMA],
        name="pool_fwd", compiler_params=_params())(proj, pool_w, pool_scale)


def _pool_bwd(d_ms, mixed, pooled, pool_w, pool_scale):
    t = d_ms.shape[0]
    c, h = POOL_CHUNK, POOL_HALO
    n_chunks = t // c

    def body(dms_ref, mixed_ref, pooled_ref, pw_ref, sc_ref, dp_ref, dsc_ref, dpw_ref, pad_ref):
        pad_ref[pl.ds(0, h), :] = jnp.zeros((h, D_POOL), F32)
        pad_ref[pl.ds(t + h, h), :] = jnp.zeros((h, D_POOL), F32)
        for g, width in enumerate(POOL_WINDOWS):
            left = width // 2
            right = width - 1 - left
            cols = slice(g * POOL_GROUP, (g + 1) * POOL_GROUP)
            wmat = pw_ref[g].astype(BF16)
            scale = sc_ref[:, cols]

            def first(ci, carry, left=left, right=right, cols=cols, wmat=wmat, scale=scale):
                dsc, dpw = carry
                base = pl.multiple_of(ci * c, c)
                dms = dms_ref[pl.ds(base, c), cols].astype(F32)
                dsc = dsc + jnp.sum(dms * mixed_ref[pl.ds(base, c), cols].astype(F32), axis=0, keepdims=True)
                dmix = (dms * scale).astype(BF16)
                dpw = dpw + _dot(pooled_ref[pl.ds(base, c), cols], dmix, TN)
                dpooled = _dot(dmix, wmat, NT)
                cnt = _pool_count(base, left, right, t, (c, POOL_GROUP))
                pad_ref[pl.ds(base + h, c), cols] = dpooled / cnt
                return dsc, dpw

            dsc, dpw = lax.fori_loop(0, n_chunks, first,
                                     (jnp.zeros((1, POOL_GROUP), F32), jnp.zeros((POOL_GROUP, POOL_GROUP), F32)))
            dsc_ref[:, cols] = dsc
            dpw_ref[g] = dpw

            def second(ci, carry, left=left, right=right, width=width, cols=cols):
                base = pl.multiple_of(ci * c, c)
                v = pad_ref[pl.ds(base, c + 2 * h), cols]
                win = _window_sum(v, width, right)[h:h + c]
                cnt = _pool_count(base, left, right, t, (c, POOL_GROUP))
                dp_ref[pl.ds(base, c), cols] = (win - v[h:h + c] * cnt).astype(BF16)
                return carry

            lax.fori_loop(0, n_chunks, second, 0)

    vm = BS(memory_space=pltpu.VMEM)
    return pl.pallas_call(
        body, out_shape=(SDS((t, D_POOL), BF16), SDS((1, D_POOL), F32), SDS((4, POOL_GROUP, POOL_GROUP), F32)),
        in_specs=[vm] * 5, out_specs=(vm, vm, vm),
        scratch_shapes=[pltpu.VMEM((t + 2 * h, D_POOL), F32)],
        name="pool_bwd", compiler_params=_params())(d_ms, mixed, pooled, pool_w, pool_scale)


SSM_ROWS = 2 * SSM_GROUPS * SSM_GROUP
SSM_HALF = SSM_GROUPS * SSM_GROUP


def _ssm_zoh(a_r, a_i, ldt):
    dt = jnp.exp(ldt)
    mag = jnp.exp(dt * a_r)
    ang = dt * a_i
    cs, sn = jnp.cos(ang), jnp.sin(ang)
    abr, abi = mag * cs, mag * sn
    den = a_r * a_r + a_i * a_i
    nr = abr - 1.0
    qr = (nr * a_r + abi * a_i) / den
    qi = (abi * a_r - nr * a_i) / den
    return dt, mag, cs, sn, abr, abi, den, nr, qr, qi


def _ssm_group_mask():
    row = lax.broadcasted_iota(jnp.int32, (SSM_HALF, SSM_CH), 0)
    col = lax.broadcasted_iota(jnp.int32, (SSM_HALF, SSM_CH), 1)
    return (row // SSM_GROUP) == (col // SSM_STATE)


def _ssm_prep(a_r, a_i, ldt, b_r, b_i, c_r, c_i, after=()):
    after_ops, after_specs = _after_operands(after)

    def body(ar_ref, ai_ref, ldt_ref, br_ref, bi_ref, cr_ref, ci_ref, *rest):
        abr_ref, abi_ref, win_ref, wint_ref, woutt_ref, wout_ref = rest[len(after_ops):]
        *_, abr, abi, _, _, qr, qi = _ssm_zoh(ar_ref[...], ai_ref[...], ldt_ref[...])
        abr_ref[...] = abr
        abi_ref[...] = abi
        b_r, b_i = br_ref[...], bi_ref[...]
        bbr = qr * b_r - qi * b_i
        bbi = qr * b_i + qi * b_r
        mask = _ssm_group_mask()
        state = lax.broadcasted_iota(jnp.int32, (SSM_STATE, SSM_CH), 0)
        col = lax.broadcasted_iota(jnp.int32, (SSM_STATE, SSM_CH), 1)
        every_group = (col % SSM_STATE == state).astype(BF16)

        def spread(x):
            return jnp.where(mask, _dot(x, every_group), 0.0)

        for d in range(2):
            rows = slice(d * SSM_HALF, (d + 1) * SSM_HALF)
            for half, x_in, x_out in ((0, bbr[rows], cr_ref[rows, :]), (1, bbi[rows], -ci_ref[rows, :])):
                cols = slice(half * SSM_CH, (half + 1) * SSM_CH)
                m_in, m_out = spread(x_in), spread(x_out)
                win_ref[d, :, cols] = m_in.astype(BF16)
                wint_ref[d, cols, :] = m_in.T.astype(BF16)
                woutt_ref[d, :, cols] = m_out.astype(BF16)
                wout_ref[d, cols, :] = m_out.T.astype(BF16)

    vm = BS(memory_space=pltpu.VMEM)
    vec = SDS((SSM_ROWS, SSM_STATE), F32)
    wide = SDS((2, SSM_HALF, 2 * SSM_CH), BF16)
    tall = SDS((2, 2 * SSM_CH, SSM_HALF), BF16)
    return pl.pallas_call(body, out_shape=(vec, vec, wide, tall, wide, tall), in_specs=[vm] * 7 + after_specs,
                          out_specs=(vm,) * 6, name="ssm_prep",
                          compiler_params=_params())(a_r, a_i, ldt, b_r, b_i, c_r, c_i, *after_ops)


def _ssm_prep_bwd(a_r, a_i, ldt, b_r, b_i, d_abr, d_abi, d_win, d_woutt):
    def body(ar_ref, ai_ref, ldt_ref, br_ref, bi_ref, *rest):
        (dabr_refs, dabi_refs, dwin_refs, dwoutt_refs), outs = [rest[2 * k:2 * k + 2] for k in range(4)], rest[8:]
        dar_ref, dai_ref, dldt_ref, dbr_ref, dbi_ref, dcr_ref, dci_ref = outs
        a_r, a_i = ar_ref[...], ai_ref[...]
        dt, mag, cs, sn, abr, abi, den, nr, qr, qi = _ssm_zoh(a_r, a_i, ldt_ref[...])
        mask = _ssm_group_mask()
        col = lax.broadcasted_iota(jnp.int32, (SSM_CH, SSM_STATE), 0)
        state = lax.broadcasted_iota(jnp.int32, (SSM_CH, SSM_STATE), 1)
        own_state = (col % SSM_STATE == state).astype(BF16)

        def pick(dense):
            m = jnp.where(mask, dense, 0.0)
            hi = m.astype(BF16)
            lo = m - hi.astype(F32)
            return _dot(hi, own_state) + _dot(lo, own_state)

        def picked(refs, half):
            cols = slice(half * SSM_CH, (half + 1) * SSM_CH)
            return jnp.concatenate([pick(ref[:, cols]) for ref in refs], axis=0)

        first_channel = lax.broadcasted_iota(jnp.int32, (SSM_HALF, SSM_CH), 0) % SSM_GROUP == 0

        def first_rows(refs):
            return jnp.concatenate(
                [pick(jnp.where(first_channel, jnp.broadcast_to(ref[...], (SSM_HALF, SSM_CH)), 0.0)) for ref in refs], axis=0)

        g_r, g_i = picked(dwin_refs, 0), picked(dwin_refs, 1)
        dcr_ref[...] = picked(dwoutt_refs, 0)
        dci_ref[...] = -picked(dwoutt_refs, 1)
        b_r, b_i = br_ref[...], bi_ref[...]
        dbr_ref[...] = g_r * qr + g_i * qi
        dbi_ref[...] = g_i * qr - g_r * qi
        gqr = g_r * b_r + g_i * b_i
        gqi = g_i * b_r - g_r * b_i
        g_nr_num = gqr / den
        g_ni_num = gqi / den
        g_den = -(gqr * qr + gqi * qi) / den
        g_nr = g_nr_num * a_r - g_ni_num * a_i
        g_abi = g_nr_num * a_i + g_ni_num * a_r
        d_ar = g_nr_num * nr + g_ni_num * abi + 2.0 * a_r * g_den
        d_ai = g_nr_num * abi - g_ni_num * nr + 2.0 * a_i * g_den
        g_abr = first_rows(dabr_refs) + g_nr
        g_abi = first_rows(dabi_refs) + g_abi
        g_mag = g_abr * cs + g_abi * sn
        g_ang = mag * (g_abi * cs - g_abr * sn)
        g_e = g_mag * mag
        d_ar = d_ar + g_e * dt
        d_ai = d_ai + g_ang * dt
        g_dt = g_e * a_r + g_ang * a_i
        dar_ref[...] = d_ar
        dai_ref[...] = d_ai
        dldt_ref[...] = g_dt * dt

    vm = BS(memory_space=pltpu.VMEM)
    vec = SDS((SSM_ROWS, SSM_STATE), F32)
    return pl.pallas_call(body, out_shape=(vec,) * 7, in_specs=[vm] * 13, out_specs=(vm,) * 7, name="ssm_prep_bwd",
                          compiler_params=_params())(a_r, a_i, ldt, b_r, b_i, *d_abr, *d_abi, *d_win, *d_woutt)


SCAN_ROWS = 512
SCAN_SUB = 128


def _ssm_scan(name, inp, w1, a_r, a_i, w2, reverse, dr, conj=False):
    t = inp.shape[0]
    rows = min(SCAN_ROWS, t)
    n = t // rows
    n_sub = rows // SCAN_SUB
    ch = SSM_CH
    at = (lambda i: (n - 1 - i, 0)) if reverse else (lambda i: (i, 0))

    def body(in_ref, w1_ref, ar_ref, ai_ref, w2_ref, sb_ref, out_ref, cr_ref, ci_ref, k_ref, st_ref):
        i = pl.program_id(0)

        @pl.when(i == 0)
        def _():
            ar8 = jnp.broadcast_to(ar_ref[...], (8, ch))
            ai8 = jnp.broadcast_to(-ai_ref[...] if conj else ai_ref[...], (8, ch))
            row = lax.broadcasted_iota(jnp.int32, (8, ch), 0)
            rank = (7 - row) if reverse else row
            powers = [(ar8, ai8)]
            for _ in range(7):
                p_r, p_i = powers[-1]
                powers.append((p_r * ar8 - p_i * ai8, p_r * ai8 + p_i * ar8))
            zero = jnp.zeros((8, ch), F32)
            for slot, k in enumerate((1, 2, 4)):
                k_ref[2 * slot] = jnp.where(rank >= k, powers[k - 1][0], zero)
                k_ref[2 * slot + 1] = jnp.where(rank >= k, powers[k - 1][1], zero)
            carry_r, carry_i = zero, zero
            for j in range(8):
                carry_r = jnp.where(rank == j, powers[j][0], carry_r)
                carry_i = jnp.where(rank == j, powers[j][1], carry_i)
            k_ref[6] = carry_r
            k_ref[7] = carry_i
            cr_ref[...] = zero
            ci_ref[...] = zero

        def group(r0, carry):
            c_r, c_i = carry
            x_r = st_ref[pl.ds(r0, 8), 0:ch]
            x_i = st_ref[pl.ds(r0, 8), ch:2 * ch]
            for slot, k in enumerate((1, 2, 4)):
                shift = (8 - k) if reverse else k
                s_r = pltpu.roll(x_r, shift, 0)
                s_i = pltpu.roll(x_i, shift, 0)
                m_r, m_i = k_ref[2 * slot], k_ref[2 * slot + 1]
                x_r, x_i = x_r + m_r * s_r - m_i * s_i, x_i + m_r * s_i + m_i * s_r
            p_r, p_i = k_ref[6], k_ref[7]
            x_r, x_i = x_r + p_r * c_r - p_i * c_i, x_i + p_r * c_i + p_i * c_r
            st_ref[pl.ds(r0, 8), 0:ch] = x_r
            st_ref[pl.ds(r0, 8), ch:2 * ch] = x_i
            last = 0 if reverse else 7
            return (jnp.broadcast_to(x_r[last:last + 1, :], (8, ch)), jnp.broadcast_to(x_i[last:last + 1, :], (8, ch)))

        carry = (cr_ref[...], ci_ref[...])
        for sc in (range(n_sub - 1, -1, -1) if reverse else range(n_sub)):
            part = pl.ds(sc * SCAN_SUB, SCAN_SUB)
            st_ref[part, :] = _dot(in_ref[part, :], w1_ref[...])
            for gi in range(SCAN_SUB // 8):
                g = (SCAN_SUB // 8 - 1 - gi) if reverse else gi
                carry = group(sc * SCAN_SUB + g * 8, carry)
            states = st_ref[part, :].astype(BF16)
            sb_ref[part, :] = states
            out_ref[part, :] = _dot(states, w2_ref[...])
        cr_ref[...] = carry[0]
        ci_ref[...] = carry[1]

    return pl.pallas_call(
        body, out_shape=(SDS((t, 2 * ch), BF16), SDS((t, D_SSM), F32)), grid=(n,),
        in_specs=[BS((rows, D_SSM), at), BS((None, D_SSM, 2 * ch), lambda i: (dr, 0, 0)), BS((None, 1, ch), lambda i: (dr, 0, 0)),
                  BS((None, 1, ch), lambda i: (dr, 0, 0)), BS((None, 2 * ch, D_SSM), lambda i: (dr, 0, 0))],
        out_specs=(BS((rows, 2 * ch), at), BS((rows, D_SSM), at)),
        scratch_shapes=[pltpu.VMEM((8, ch), F32), pltpu.VMEM((8, ch), F32), pltpu.VMEM((8, 8, ch), F32),
                        pltpu.VMEM((rows, 2 * ch), F32)],
        name=name, compiler_params=_params("arbitrary"))(inp, w1, a_r, a_i, w2)


DA_ROWS = 1024


def _ssm_param_grads(name, lam, states, u, dy, reverse, after=()):
    t = lam.shape[0]
    rows = min(DA_ROWS, t)
    n = t // rows
    halo_rows = 16
    nb = rows // halo_rows
    ch = SSM_CH
    if reverse:
        halo_at = lambda i: (jnp.minimum((i + 1) * nb, t // halo_rows - 1), 0)
    else:
        halo_at = lambda i: (jnp.maximum(i * nb - 1, 0), 0)

    after_ops, after_specs = _after_operands(after)

    def body(lam_ref, x_ref, halo_ref, u_ref, dy_ref, *rest):
        dr_ref, di_ref, dwin_ref, dwoutt_ref = rest[len(after_ops):]
        i = pl.program_id(0)

        @pl.when(i == 0)
        def _():
            dr_ref[...] = jnp.zeros_like(dr_ref)
            di_ref[...] = jnp.zeros_like(di_ref)
            dwin_ref[...] = jnp.zeros_like(dwin_ref)
            dwoutt_ref[...] = jnp.zeros_like(dwoutt_ref)

        dwin_ref[...] += _dot(u_ref[...], lam_ref[...], TN)
        dwoutt_ref[...] += _dot(dy_ref[...], x_ref[...], TN)
        row = lax.broadcasted_iota(jnp.int32, (rows, ch), 0)
        if reverse:
            edge, shift, h_row, live = rows - 1, rows - 1, 0, i < n - 1
        else:
            edge, shift, h_row, live = 0, 1, halo_rows - 1, i > 0

        def neighbour(lo):
            halo = halo_ref[:, lo:lo + ch].astype(F32)[h_row:h_row + 1]
            halo = jnp.where(live, halo, 0.0)
            x = x_ref[:, lo:lo + ch].astype(F32)
            return jnp.where(row == edge, jnp.broadcast_to(halo, (rows, ch)), pltpu.roll(x, shift, 0))

        xp_r, xp_i = neighbour(0), neighbour(ch)
        l_r, l_i = lam_ref[:, 0:ch].astype(F32), lam_ref[:, ch:2 * ch].astype(F32)
        dr_ref[...] += jnp.sum(l_r * xp_r + l_i * xp_i, axis=0, keepdims=True)
        di_ref[...] += jnp.sum(l_i * xp_r - l_r * xp_i, axis=0, keepdims=True)

    blk = BS((rows, 2 * ch), lambda i: (i, 0))
    thin = BS((rows, D_SSM), lambda i: (i, 0))
    vec = BS((1, ch), lambda i: (0, 0))
    mat = BS((D_SSM, 2 * ch), lambda i: (0, 0))
    return pl.pallas_call(
        body, out_shape=(SDS((1, ch), F32), SDS((1, ch), F32), SDS((D_SSM, 2 * ch), F32), SDS((D_SSM, 2 * ch), F32)),
        grid=(n,), in_specs=[blk, blk, BS((halo_rows, 2 * ch), halo_at), thin, thin] + after_specs,
        out_specs=(vec, vec, mat, mat),
        name=name, compiler_params=_params("arbitrary"))(lam, states, states, u, dy, *after_ops)


GELU_C = math.sqrt(2.0 / math.pi)
GELU_K = 0.044715


def _ssm_combine(proj, y_fwd, y_bwd, d_skip, tm, after=()):
    t = proj.shape[0]
    after_ops, after_specs = _after_operands(after)

    def body(s_ref, yf_ref, yb_ref, d_ref, *rest):
        yt_ref, g_ref = rest[len(after_ops):]
        y = s_ref[...] * d_ref[...] + yf_ref[...] + yb_ref[...]
        yt_ref[...] = y
        th = jnp.tanh(GELU_C * (y + GELU_K * y * y * y))
        g_ref[...] = (0.5 * y * (1.0 + th)).astype(BF16)

    blk = BS((tm, D_SSM), lambda i: (i, 0))
    return pl.pallas_call(
        body, out_shape=(SDS((t, D_SSM), F32), SDS((t, D_SSM), BF16)), grid=(t // tm,),
        in_specs=[BS((tm, D_SSM), lambda i: (i, D_POOL // D_SSM)), blk, blk, BS((1, D_SSM), lambda i: (0, 0))] + after_specs,
        out_specs=(blk, blk), name="ssm_combine",
        compiler_params=_params("parallel"))(proj, y_fwd, y_bwd, d_skip, *after_ops)


def _ssm_ds(proj, d_yt, du_fwd, du_bwd, d_skip, tm):
    t = proj.shape[0]

    def body(s_ref, dy_ref, duf_ref, dub_ref, d_ref, ds_ref, dd_ref):
        i = pl.program_id(0)
        dy = dy_ref[...]
        ds_ref[...] = (dy * d_ref[...] + duf_ref[...] + dub_ref[...]).astype(BF16)

        @pl.when(i == 0)
        def _():
            dd_ref[...] = jnp.zeros_like(dd_ref)

        dd_ref[...] += jnp.sum(dy * s_ref[...], axis=0, keepdims=True)

    blk = BS((tm, D_SSM), lambda i: (i, 0))
    vec = BS((1, D_SSM), lambda i: (0, 0))
    return pl.pallas_call(
        body, out_shape=(SDS((t, D_SSM), BF16), SDS((1, D_SSM), F32)), grid=(t // tm,),
        in_specs=[BS((tm, D_SSM), lambda i: (i, D_POOL // D_SSM)), blk, blk, blk, vec],
        out_specs=(blk, vec), name="ssm_ds", compiler_params=_params("arbitrary"))(proj, d_yt, du_fwd, du_bwd, d_skip)


G_POOL_AT = D_POOL + D_SSM
G_SSM_AT = G_POOL_AT + D_MODEL
E_VAL, E_GATE = D_POOL, D_POOL + D_SSM


def _merge_specs(tm):
    return [BS((tm, D_POOL), lambda i: (i, 0)), BS((tm, D_SSM), lambda i: (i, 0)),
            BS((N_SHARD, 1024, 256), lambda i: (0, 0, 0)), BS((tm, D_FF), lambda i: (i, 0))]


def _merge_parts(s, ms, yv, w_ref, proj_ref):
    lo = 256 * s
    zp = _dot(ms, w_ref[s, 0:E_VAL, :])
    zv = _dot(yv, w_ref[s, E_VAL:E_GATE, :])
    zg = _dot(yv, w_ref[s, E_GATE:, :])
    return zp, zv, zg, proj_ref[:, G_POOL_AT + lo:G_POOL_AT + lo + 256], proj_ref[:, G_SSM_AT + lo:G_SSM_AT + lo + 256]


def _mixer_merge(ms, yssm, w_e, proj, tm):
    t = ms.shape[0]

    def body(ms_ref, y_ref, w_ref, proj_ref, o_ref):
        msv, yv = ms_ref[...], y_ref[...]
        for s in range(N_SHARD):
            zp, zv, zg, gp, gs = _merge_parts(s, msv, yv, w_ref, proj_ref)
            o_ref[:, 256 * s:256 * (s + 1)] = (_sigmoid(gp) * zp + _sigmoid(gs) * zv * _sigmoid(zg)).astype(BF16)

    row = BS((tm, D_MODEL), lambda i: (i, 0))
    return pl.pallas_call(
        body, out_shape=SDS((t, D_MODEL), BF16), grid=(t // tm,), in_specs=_merge_specs(tm), out_specs=row,
        name="mixer_merge", compiler_params=_params("parallel"))(ms, yssm, w_e, proj)


def _mixer_merge_bwd(ms, yssm, w_e, proj, dmerged, tm):
    t = ms.shape[0]

    def body(ms_ref, y_ref, w_ref, proj_ref, dm_ref, dgp_ref, dgs_ref, dzp_ref, dzv_ref, dzg_ref):
        msv, yv = ms_ref[...], y_ref[...]
        for s in range(N_SHARD):
            cols = slice(256 * s, 256 * (s + 1))
            zp, zv, zg, gp, gs = _merge_parts(s, msv, yv, w_ref, proj_ref)
            dm = dm_ref[:, cols].astype(F32)
            sp, ss, sg = _sigmoid(gp), _sigmoid(gs), _sigmoid(zg)
            dgp_ref[:, cols] = (dm * zp * sp * (1.0 - sp)).astype(BF16)
            dgs_ref[:, cols] = (dm * zv * sg * ss * (1.0 - ss)).astype(BF16)
            dzp_ref[:, cols] = (dm * sp).astype(BF16)
            dz = dm * ss
            dzv_ref[:, cols] = (dz * sg).astype(BF16)
            dzg_ref[:, cols] = (dz * zv * sg * (1.0 - sg)).astype(BF16)

    row = BS((tm, D_MODEL), lambda i: (i, 0))
    shape = SDS((t, D_MODEL), BF16)
    return pl.pallas_call(
        body, out_shape=(shape,) * 5, grid=(t // tm,), in_specs=_merge_specs(tm) + [row],
        out_specs=(row,) * 5, name="mixer_merge_bwd",
        compiler_params=_params("parallel"))(ms, yssm, w_e, proj, dmerged)


def _mixer_dw(ms, yssm, dzp, dzv, dzg, tm):
    t = ms.shape[0]
    tm = min(2 * tm, t)
    n_t = t // tm

    def body(ms_ref, y_ref, dzp_ref, dzv_ref, dzg_ref, o_ref, acc):
        i = pl.program_id(0)

        @pl.when(i == 0)
        def _():
            acc[...] = jnp.zeros_like(acc)

        msv, yv = ms_ref[...], y_ref[...]
        for s in range(N_SHARD):
            cols = slice(256 * s, 256 * (s + 1))
            acc[s, 0:E_VAL, :] += _dot(msv, dzp_ref[:, cols], TN)
            acc[s, E_VAL:E_GATE, :] += _dot(yv, dzv_ref[:, cols], TN)
            acc[s, E_GATE:, :] += _dot(yv, dzg_ref[:, cols], TN)

        @pl.when(i == n_t - 1)
        def _():
            o_ref[...] = acc[...].astype(BF16)

    row = BS((tm, D_MODEL), lambda i: (i, 0))
    full = BS((N_SHARD, 1024, 256), lambda i: (0, 0, 0))
    return pl.pallas_call(
        body, out_shape=SDS((N_SHARD, 1024, 256), BF16), grid=(n_t,),
        in_specs=[BS((tm, D_POOL), lambda i: (i, 0)), BS((tm, D_SSM), lambda i: (i, 0)), row, row, row],
        out_specs=full, scratch_shapes=[pltpu.VMEM((N_SHARD, 1024, 256), F32)],
        name="mixer_dw", compiler_params=_params("arbitrary"))(ms, yssm, dzp, dzv, dzg)


def _mixer_dx(dzp, dzv, dzg, w_e, y_total, tm):
    t = dzp.shape[0]

    def body(dzp_ref, dzv_ref, dzg_ref, w_ref, yt_ref, dms_ref, dy_ref, dyb_ref):
        acc_ms, acc_y = None, None
        for s in range(N_SHARD):
            cols = slice(256 * s, 256 * (s + 1))
            part_ms = _dot(dzp_ref[:, cols], w_ref[s, 0:E_VAL, :], NT)
            part_y = _dot(dzv_ref[:, cols], w_ref[s, E_VAL:E_GATE, :], NT) + _dot(dzg_ref[:, cols], w_ref[s, E_GATE:, :], NT)
            acc_ms = part_ms if s == 0 else acc_ms + part_ms
            acc_y = part_y if s == 0 else acc_y + part_y
        dms_ref[...] = acc_ms.astype(BF16)
        y = yt_ref[...]
        th = jnp.tanh(GELU_C * (y + GELU_K * y * y * y))
        dgelu = 0.5 * (1.0 + th) + 0.5 * y * (1.0 - th * th) * GELU_C * (1.0 + 3.0 * GELU_K * y * y)
        dy = acc_y * dgelu
        dy_ref[...] = dy
        dyb_ref[...] = dy.astype(BF16)

    row = BS((tm, D_MODEL), lambda i: (i, 0))
    narrow = BS((tm, D_SSM), lambda i: (i, 0))
    return pl.pallas_call(
        body, out_shape=(SDS((t, D_POOL), BF16), SDS((t, D_SSM), F32), SDS((t, D_SSM), BF16)), grid=(t // tm,),
        in_specs=[row, row, row, BS((N_SHARD, 1024, 256), lambda i: (0, 0, 0)), narrow],
        out_specs=(BS((tm, D_POOL), lambda i: (i, 0)), narrow, narrow),
        name="mixer_dx", compiler_params=_params("parallel"))(dzp, dzv, dzg, w_e, y_total)


def _attn_probs(q_h, k_h):
    s = _dot(q_h, k_h, NT) * (1.0 / math.sqrt(HEAD_DIM))
    e = jnp.exp(s - jnp.max(s, axis=-1, keepdims=True))
    return e / jnp.sum(e, axis=-1, keepdims=True)


def _attn_fwd(q, kv, tm):
    t = q.shape[0]
    tm = min(2 * tm, t)
    m = kv.shape[0]

    def body(q_ref, kv_ref, o_ref):
        for hd in range(N_HEADS):
            lo = hd * HEAD_DIM
            p = _attn_probs(q_ref[:, lo:lo + HEAD_DIM], kv_ref[:, lo:lo + HEAD_DIM])
            o_ref[:, lo:lo + HEAD_DIM] = _dot(p, kv_ref[:, D_MODEL + lo:D_MODEL + lo + HEAD_DIM]).astype(BF16)

    return pl.pallas_call(
        body, out_shape=SDS((t, D_MODEL), BF16), grid=(t // tm,),
        in_specs=[BS((tm, D_MODEL), lambda i: (i, 0)), BS((m, 2 * D_MODEL), lambda i: (0, 0))],
        out_specs=BS((tm, D_MODEL), lambda i: (i, 0)), name="attn_fwd", compiler_params=_params("parallel"))(q, kv)


def _attn_bwd(q, kv, d_o, tm):
    t = q.shape[0]
    m = kv.shape[0]

    def body(q_ref, kv_ref, do_ref, dq_ref, dkv_ref):
        i = pl.program_id(0)

        @pl.when(i == 0)
        def _():
            dkv_ref[...] = jnp.zeros_like(dkv_ref)

        for hd in range(N_HEADS):
            lo = hd * HEAD_DIM
            q_h = q_ref[:, lo:lo + HEAD_DIM]
            k_h = kv_ref[:, lo:lo + HEAD_DIM]
            v_h = kv_ref[:, D_MODEL + lo:D_MODEL + lo + HEAD_DIM]
            do_h = do_ref[:, lo:lo + HEAD_DIM]
            p = _attn_probs(q_h, k_h)
            dkv_ref[:, D_MODEL + lo:D_MODEL + lo + HEAD_DIM] += _dot(p, do_h, TN)
            dp = _dot(do_h, v_h, NT)
            ds = p * (dp - jnp.sum(dp * p, axis=-1, keepdims=True)) * (1.0 / math.sqrt(HEAD_DIM))
            dq_ref[:, lo:lo + HEAD_DIM] = _dot(ds, k_h).astype(BF16)
            dkv_ref[:, lo:lo + HEAD_DIM] += _dot(ds, q_h, TN)

    row = BS((tm, D_MODEL), lambda i: (i, 0))
    full = BS((m, 2 * D_MODEL), lambda i: (0, 0))
    return pl.pallas_call(
        body, out_shape=(SDS((t, D_MODEL), BF16), SDS((m, 2 * D_MODEL), F32)), grid=(t // tm,),
        in_specs=[row, full, row], out_specs=(row, full), name="attn_bwd",
        compiler_params=_params("arbitrary"))(q, kv, d_o)


TRANSPOSED = ("ffn1_w_gate", "ffn1_w_up", "ffn2_w_gate", "ffn2_w_up", "w_in")
GATHER_PHASES = {"f1a": (("ffn1_w_gate",),),
                 "f1u": (("ffn1_w_up",),),
                 "f1b": (("ffn1_w_down",),),
                 "win": (("w_in",),),
                 "mix": (("w_mix_out",), ("w_q",), ("w_xo",), ("w_kv",), ("w_pool_proj", "w_glu_val", "w_glu_gate")),
                 "f2": (("ffn2_w_gate",), ("ffn2_w_up",), ("ffn2_w_down",))}
REDUCE_GROUPS = (("ffn2_w_gate",), ("ffn2_w_up",), ("ffn2_w_down",), ("w_xo",), ("w_q",), ("w_kv",), ("w_mix_out",),
                 ("w_pool_proj", "w_glu_val", "w_glu_gate"), ("w_in",), ("ffn1_w_gate",), ("ffn1_w_up",), ("ffn1_w_down",))
SMALL = ("ffn1_norm", "mix_norm", "pool_w", "pool_scale", "ssm_a_re", "ssm_a_im", "ssm_log_dt", "ssm_b_re",
         "ssm_b_im", "ssm_c_re", "ssm_c_im", "ssm_d", "xattn_norm", "mem_norm", "ffn2_norm", "final_norm")
WEIGHTS = ("ffn1_norm", "ffn1_w_gate", "ffn1_w_up", "ffn1_w_down", "mix_norm", "w_in", "pool_w", "pool_scale",
           "w_pool_proj", "ssm_a_re", "ssm_a_im", "ssm_log_dt", "ssm_b_re", "ssm_b_im", "ssm_c_re", "ssm_c_im",
           "ssm_d", "w_glu_val", "w_glu_gate", "w_mix_out", "xattn_norm", "mem_norm", "w_q", "w_kv", "w_xo",
           "ffn2_norm", "ffn2_w_gate", "ffn2_w_up", "ffn2_w_down", "final_norm")


def _small_view(a, n):
    return jnp.swapaxes(a, 3, 4) if n in ("ssm_b_re", "ssm_b_im") else a


def _device_step(x, mem, target, wts, sp, reducer=None):
    t = x.shape[0]
    tm = min(TM, t)
    g = {}

    first_gather = wts.start("win", wts.start("f1b", wts.start("f1u", wts.start("f1a"))))
    u1 = _rmsnorm("norm_ffn1", x, sp["ffn1_norm"], tm, after=first_gather)

    def per_channel(a):
        a = a.reshape(2 * SSM_GROUPS, 1, -1)
        return jnp.broadcast_to(a, (2 * SSM_GROUPS, SSM_GROUP, a.shape[-1])).reshape(SSM_ROWS, a.shape[-1])

    ssm_a = per_channel(sp["ssm_a_re"]), per_channel(sp["ssm_a_im"]), per_channel(sp["ssm_log_dt"])
    ssm_b = sp["ssm_b_re"].reshape(SSM_ROWS, SSM_STATE), sp["ssm_b_im"].reshape(SSM_ROWS, SSM_STATE)
    abr, abi, w_in_s, w_in_s_t, w_out_s_t, w_out_s = _ssm_prep(
        *ssm_a, *ssm_b, sp["ssm_c_re"].reshape(SSM_ROWS, SSM_STATE), sp["ssm_c_im"].reshape(SSM_ROWS, SSM_STATE),
        after=first_gather)
    first_rows = (2, SSM_GROUPS, SSM_GROUP, SSM_STATE)
    a_r = abr.reshape(first_rows)[:, :, 0].reshape(2, 1, SSM_CH)
    a_i = abi.reshape(first_rows)[:, :, 0].reshape(2, 1, SSM_CH)
    mem_n = _rmsnorm("norm_mem", mem, sp["mem_norm"], mem.shape[0], after=first_gather + wts.sources(("mix", "f2")))

    whole = (D_FF, D_MODEL)
    (w_g1,) = wts.finish("f1a", [u1, w_in_s, w_in_s_t, w_out_s, w_out_s_t, a_r, a_i, mem_n])
    w_f1 = {"gate": w_g1.reshape(whole)}
    gate1 = _mm("ffn1_gate", [(u1, BS((tm, D_MODEL), lambda i: (i, 0)), w_f1["gate"], BS(whole, lambda i: (0, 0)), NT)],
                grid=(t // tm,), out_shape=SDS((t, D_FF), F32), out_spec=BS((tm, D_FF), lambda i: (i, 0)))
    (w_u1,) = wts.finish("f1u", [gate1])
    w_f1["up"] = w_u1.reshape(whole)
    g1, up1, a1 = _ffn_up("ffn1_up", u1, w_f1, tm, gate=gate1)
    (w_dn,) = wts.finish("f1b", [a1])
    w_f1["down"] = w_dn.reshape(whole)
    h1, u2 = _ffn_down("ffn1_down", a1, w_f1, x, tm, next_gain=sp["mix_norm"])

    (w_in_g,) = wts.finish("win", [u2])
    w_in_t = w_in_g.reshape(D_FF, D_MODEL)
    proj, s_in = _mix_in(u2, w_in_t, tm, after=wts.start("f2", wts.start("mix", [w_in_g])))
    pooled, mixed, ms = _pool_fwd(proj, sp["pool_w"][0], sp["pool_scale"])

    states, y_dirs = [], []
    for dr in range(2):
        st, yd = _ssm_scan(f"ssm_scan_fwd{dr}", s_in, w_in_s, a_r, a_i, w_out_s, reverse=(dr == 1), dr=dr)
        states.append(st)
        y_dirs.append(yd)
    *w_sq, w_kv, w_e = wts.finish("mix", y_dirs)
    w_mo, w_q, w_xo = (a.reshape(D_MODEL, D_MODEL) for a in w_sq)
    w_d = w_kv[:, None]
    y_total, yssm = _ssm_combine(proj, y_dirs[0], y_dirs[1], sp["ssm_d"], tm)

    merged = _mixer_merge(ms, yssm, w_e, proj, tm)
    h2, u3 = _mm_resid_norm("mix_out", merged, w_mo, h1, sp["xattn_norm"], tm)

    q = _plain_mm("attn_q", u3, w_q, NN, BF16, tm)
    n_mem = mem.shape[0]
    kv = _mm("attn_kv", [(mem_n, BS((n_mem, D_MODEL), lambda s: (0, 0)), w_d, BS((None, None, D_MODEL, 512), lambda s: (s, 0, 0, 0)), NN)],
             grid=(N_SHARD,), out_shape=SDS((n_mem, 2 * D_MODEL), BF16), out_spec=BS((n_mem, 512), lambda s: (0, s)))
    o = _attn_fwd(q, kv, tm)
    h3, u4 = _mm_resid_norm("attn_out", o, w_xo, h2, sp["ffn2_norm"], tm)

    w_f2 = dict(zip(("gate", "up", "down"), (a.reshape(whole) for a in wts.finish("f2", [u4]))))
    g2, up2, a2 = _ffn_up("ffn2_up", u4, w_f2, tm)
    loss, dh4, dh4_b, g["final_norm"] = _ffn_down("ffn2_down", a2, w_f2, h3, tm,
                                                  head=(sp["final_norm"].reshape(1, D_MODEL), target))

    dg2, dup2 = _ffn_bwd_act("ffn2_bwd_act", dh4_b, w_f2, g2, up2, tm)
    dw_f2 = _ffn_dw("ffn2_dw", u4, dg2, dup2, a2, dh4_b, tm)
    dh3, dh3_b, g["ffn2_norm"] = _ffn_dx("ffn2_dx", dg2, dup2, w_f2, h3, sp["ffn2_norm"], dh4, tm)

    d_o = _plain_mm("attn_out_dx", dh3_b, w_xo, NT, BF16, tm)
    dw_xo = _dw_mm("attn_out_dw", o, dh3_b, tm)
    dq, dkv = _attn_bwd(q, kv, d_o, tm)
    dw_q = _dw_mm("attn_q_dw", u3, dq, tm)
    dh2, dh2_b, g["xattn_norm"] = _mm_norm_bwd("attn_q_dx", dq, w_q, NT, h2, sp["xattn_norm"], dh3, tm)
    dw_kv = _mm("attn_kv_dw", [(mem_n, BS((n_mem, D_MODEL), lambda s: (0, 0)), dkv, BS((n_mem, 512), lambda s: (0, s)), TN)],
                grid=(N_SHARD,), out_shape=SDS((N_SHARD, D_MODEL, 512), BF16), out_spec=BS((None, D_MODEL, 512), lambda s: (s, 0, 0)))
    dmem_n = _mm("attn_kv_dx", [(dkv, BS((n_mem, 512), lambda s: (0, s)), w_d, BS((None, None, D_MODEL, 512), lambda s: (s, 0, 0, 0)), NT)],
                 grid=(N_SHARD,), red_axis=0, out_shape=SDS((n_mem, D_MODEL), F32), out_spec=BS((n_mem, D_MODEL), lambda s: (0, 0)))
    _, _, g["mem_norm"] = _rmsnorm_bwd("norm_mem_bwd", mem, sp["mem_norm"], dmem_n, None, n_mem)

    square = (N_SHARD, D_MODEL // N_SHARD, D_MODEL)
    sharded = (N_SHARD, FF_SH, D_MODEL)
    early = [a.reshape(sharded) for a in dw_f2] + [dw_xo.reshape(square), dw_q.reshape(square), dw_kv]
    swapping = reducer.swap_start("a1", early) if reducer is not None else []
    dmerged = _plain_mm("mix_out_dx", dh2_b, w_mo, NT, BF16, tm, after=swapping)
    dw_mo = _dw_mm("mix_out_dw", merged, dh2_b, tm)
    d_gp, d_gs, dzp, dzv, dzg = _mixer_merge_bwd(ms, yssm, w_e, proj, dmerged, tm)
    dw_e = _mixer_dw(ms, yssm, dzp, dzv, dzg, tm)
    d_ms, d_yt, d_yt_b = _mixer_dx(dzp, dzv, dzg, w_e, y_total, tm)
    dp, d_scale, d_pw = _pool_bwd(d_ms, mixed, pooled, sp["pool_w"][0], sp["pool_scale"])
    g["pool_scale"] = d_scale
    g["pool_w"] = d_pw[None]

    du_dirs, lams = [], []
    for dr in range(2):
        lam, du = _ssm_scan(f"ssm_scan_bwd{dr}", d_yt_b, w_out_s_t, a_r, a_i, w_in_s_t, reverse=(dr == 0), dr=dr, conj=True)
        du_dirs.append(du)
        lams.append(lam)
    ds, g["ssm_d"] = _ssm_ds(proj, d_yt, du_dirs[0], du_dirs[1], sp["ssm_d"], tm)

    d_proj = [dp, ds, d_gp, d_gs]
    dw_in_t = _mix_in_dw(d_proj, u2, tm)
    dh1, dh1_b, g["mix_norm"] = _mm_norm_bwd("mix_in_dx", d_proj, w_in_t, NN, h1, sp["mix_norm"], dh2, tm)

    early += [dw_mo.reshape(square), dw_e, dw_in_t.reshape(sharded)]
    g["final_norm"] = g["final_norm"].reshape(D_MODEL)

    travelling = reducer.start("a", early[6:], swapped=["a1"], after=list(g.values())) if reducer is not None else []
    d_abr, d_abi, d_cm, d_bm = [], [], [], []
    for dr in range(2):
        da_r, da_i, d_win, d_woutt = _ssm_param_grads(f"ssm_param_grads{dr}", lams[dr], states[dr], s_in, d_yt_b,
                                                      reverse=(dr == 1), after=travelling)
        d_abr.append(da_r)
        d_abi.append(da_i)
        d_bm.append(d_win)
        d_cm.append(d_woutt)

    d_ar, d_ai, d_ldt, d_br, d_bi, d_cr, d_ci = _ssm_prep_bwd(*ssm_a, *ssm_b, d_abr, d_abi, d_bm, d_cm)
    per_group = (2 * SSM_GROUPS, SSM_GROUP * SSM_STATE)
    g["ssm_a_re"] = d_ar.reshape(2 * SSM_GROUPS, SSM_GROUP, SSM_STATE).sum(axis=1).reshape(sp["ssm_a_re"].shape)
    g["ssm_a_im"] = d_ai.reshape(2 * SSM_GROUPS, SSM_GROUP, SSM_STATE).sum(axis=1).reshape(sp["ssm_a_im"].shape)
    g["ssm_log_dt"] = d_ldt.reshape(per_group).sum(axis=1).reshape(sp["ssm_log_dt"].shape)
    g["ssm_b_re"] = d_br.reshape(sp["ssm_b_re"].shape)
    g["ssm_b_im"] = d_bi.reshape(sp["ssm_b_im"].shape)
    g["ssm_c_re"] = d_cr.reshape(sp["ssm_c_re"].shape)
    g["ssm_c_im"] = d_ci.reshape(sp["ssm_c_im"].shape)
    if reducer is not None:
        travelling = travelling + [d_ar, d_br, d_cr]
    dg1, dup1 = _ffn_bwd_act("ffn1_bwd_act", dh1_b, w_f1, g1, up1, tm, after=travelling)
    dw_f1 = [a.reshape(sharded) for a in _ffn_dw("ffn1_dw", u1, dg1, dup1, a1, dh1_b, tm)]
    if reducer is not None:
        joining = reducer.join_start("a", after=reducer.finish("a", reducer.swap_start("b1", dw_f1)))
        travelling = joining + reducer.start("b", [], swapped=["b1"], after=joining)
    grad_x, _, g["ffn1_norm"] = _ffn_dx("ffn1_dx", dg1, dup1, w_f1, x, sp["ffn1_norm"], dh1, tm, after=travelling)
    if reducer is not None:
        reducer.join_finish("a", [grad_x])
    return loss, grad_x, early + dw_f1, g


def _mesh_place():
    x, y, c = lax.axis_index("x"), lax.axis_index("y"), lax.axis_index("c")
    chips = [(1 - x, y), (x, 1 - y), (1 - x, 1 - y)]
    return x, y, c, chips


def _remote(src, dst, send_sems, recv_sems, k, to):
    return pltpu.make_async_remote_copy(src_ref=src, dst_ref=dst, send_sem=send_sems.at[k], recv_sem=recv_sems.at[k],
                                        device_id=to, device_id_type=MESH)


def _sibling_swap_halves(tag, grads, after=()):
    n = len(grads)
    after_ops, after_specs = _after_operands(after)

    def body(*refs):
        ins, outs = refs[:n], refs[n + len(after_ops):2 * n + len(after_ops)]
        send_sems, recv_sems = refs[2 * n + len(after_ops):]
        x, y, c, _ = _mesh_place()
        sibling = (x, y, 1 - c)
        copies = []
        for k in range(n):
            half = grads[k].shape[1] // 2
            theirs = pl.ds(pl.multiple_of((1 - c) * half, 16), half)
            cp = _remote(ins[k].at[:, theirs, :], outs[k], send_sems, recv_sems, k, sibling)
            cp.start()
            copies.append(cp)
        for cp in copies:
            cp.wait_recv()
        for cp in copies:
            cp.wait_send()

    hbm = BS(memory_space=pl.ANY)
    return pl.pallas_call(
        body, out_shape=tuple(SDS((g.shape[0], g.shape[1] // 2, g.shape[2]), g.dtype) for g in grads),
        in_specs=[hbm] * n + after_specs, out_specs=(hbm,) * n,
        scratch_shapes=[pltpu.SemaphoreType.DMA((n,)), pltpu.SemaphoreType.DMA((n,))],
        name="reduce_sibling_send_" + tag, compiler_params=_params())(*grads, *after_ops)


def _row_tile(rows, cap=512):
    return max(r for r in range(16, cap + 1, 16) if rows % r == 0)


REDUCE_STEPS = 2


def _chip_presum(tag, grads, gots, c_idx):
    n = len(grads)
    halves = [g.shape[1] // 2 for g in grads]
    tiles = [(h // REDUCE_STEPS, g.shape[2]) for h, g in zip(halves, grads)]

    def body(c_ref, *refs):
        for k in range(n):
            refs[2 * n + k][...] = (refs[k][...].astype(F32) + refs[n + k][...].astype(F32)).astype(BF16)

    mine = [BS((None, None) + tile, lambda s, i, c_ref: (s, c_ref[0], i, 0)) for tile in tiles]
    plain = [BS((None,) + tile, lambda s, i, c_ref: (s, i, 0)) for tile in tiles]
    return list(pl.pallas_call(
        body, out_shape=tuple(SDS((g.shape[0], h, g.shape[2]), BF16) for g, h in zip(grads, halves)),
        grid_spec=pltpu.PrefetchScalarGridSpec(num_scalar_prefetch=1, grid=(N_SHARD, REDUCE_STEPS),
                                               in_specs=mine + plain, out_specs=plain),
        name="reduce_presum_" + tag, compiler_params=_params("parallel", "parallel"))(
            c_idx, *[g.reshape(g.shape[0], 2, h, g.shape[2]) for g, h in zip(grads, halves)], *gots))


HBM_SPEC = BS(memory_space=pltpu.HBM)
SEM_SPEC = BS(memory_space=pltpu.SEMAPHORE)
DATAFLOW = pltpu.SideEffectType.DATAFLOW_SIDE_EFFECTING


def _chip_exchange_copies(parts, lands, send_sems, recv_sems):
    _, _, c, chips = _mesh_place()
    return [_remote(parts[k].at[2 * px + py], lands[k].at[j], send_sems, recv_sems, 3 * k + j, (px, py, c))
            for k in range(len(parts)) for j, (px, py) in enumerate(chips)]


def _gather_copies(shards, lands, send_sems, recv_sems):
    x, y, c, chips = _mesh_place()
    return [_remote(shards[k], lands[k].at[2 * x + y], send_sems, recv_sems, 3 * k + j, (px, py, c))
            for k in range(len(shards)) for j, (px, py) in enumerate(chips)]


def _gather_half_copies(shards, lands, send_sems, recv_sems):
    x, y, c, chips = _mesh_place()
    out = []
    for k in range(len(shards)):
        half = shards[k].shape[0] // 2
        mine = pl.ds(pl.multiple_of(c * half, 16), half)
        for j, (px, py) in enumerate(chips):
            out.append(_remote(shards[k].at[mine, :], lands[k].at[2 * x + y, mine, :], send_sems, recv_sems,
                               3 * k + j, (px, py, c)))
    return out


def _sibling_fill(tag, lands):
    n = len(lands)

    def body(*refs):
        outs = refs[n:2 * n]
        send_sems, recv_sems = refs[2 * n:]
        x, y, c, chips = _mesh_place()
        copies = []
        for k in range(n):
            half = lands[k].shape[1] // 2
            mine = pl.ds(pl.multiple_of(c * half, 16), half)
            for j, (px, py) in enumerate(chips):
                blk = outs[k].at[2 * px + py, mine, :]
                copies.append(_remote(blk, blk, send_sems, recv_sems, 3 * k + j, (x, y, 1 - c)))
        for cp in copies:
            cp.start()
        for cp in copies:
            cp.wait_recv()
        for cp in copies:
            cp.wait_send()

    hbm = BS(memory_space=pl.ANY)
    return list(pl.pallas_call(
        body, out_shape=tuple(SDS(a.shape, a.dtype) for a in lands),
        in_specs=[hbm] * n, out_specs=(hbm,) * n, input_output_aliases={k: k for k in range(n)},
        scratch_shapes=[pltpu.SemaphoreType.DMA((3 * n,)), pltpu.SemaphoreType.DMA((3 * n,))],
        name="gather_fill_" + tag, compiler_params=_params())(*lands))


def _swap_copies(grads, lands, send_sems, recv_sems):
    x, y, c, _ = _mesh_place()
    out = []
    for k in range(len(grads)):
        half = grads[k].shape[1] // 2
        theirs = pl.ds(pl.multiple_of((1 - c) * half, 16), half)
        out.append(_remote(grads[k].at[:, theirs, :], lands[k], send_sems, recv_sems, k, (x, y, 1 - c)))
    return out


def _join_copies(fulls, same, send_sems, recv_sems):
    x, y, c, _ = _mesh_place()
    out = []
    for k in range(len(fulls)):
        half = fulls[k].shape[0] // 2
        mine = fulls[k].at[pl.ds(pl.multiple_of(c * half, 8), half), :]
        out.append(_remote(mine, mine, send_sems, recv_sems, k, (x, y, 1 - c)))
    return out


def _everyone_copies(packs, lands, send_sems, recv_sems):
    x, y, c, _ = _mesh_place()
    out = []
    for k in range(len(packs)):
        for j in range(N_DEV - 1):
            bx, by, bc = (j + 1) >> 2 & 1, (j + 1) >> 1 & 1, (j + 1) & 1
            peer = (x ^ bx, y ^ by, c ^ bc)
            out.append(_remote(packs[k], lands[k].at[4 * x + 2 * y + c], send_sems, recv_sems, (N_DEV - 1) * k + j, peer))
    return out


def _split_start(name, copies, sources, land_shapes, after=(), fanout=3):
    n = len(sources)
    n_land = len(land_shapes)
    m = n + n_land
    n_sems = fanout * n
    after_ops, after_specs = _after_operands(after)

    def body(*refs):
        ins = refs[:n]
        lands = refs[n:m] if n_land else ins
        send_sems, recv_sems = refs[m + len(after_ops)], refs[m + len(after_ops) + 1]
        token = refs[-1]
        for cp in copies(ins, lands, send_sems, recv_sems):
            cp.start()
        token[...] = jnp.zeros_like(token)

    lands = [pltpu.with_memory_space_constraint(lax.empty(s, d), pltpu.HBM) for s, d in land_shapes]
    sources = [pltpu.with_memory_space_constraint(p, pltpu.HBM) for p in sources]
    thru = [pltpu.HBM(a.shape, a.dtype) for a in sources + lands]
    out = pl.pallas_call(
        body, name=name,
        out_shape=(pltpu.SemaphoreType.DMA((n_sems,)), pltpu.SemaphoreType.DMA((n_sems,)), *thru, SDS((8, 128), F32)),
        in_specs=[HBM_SPEC] * m + after_specs,
        out_specs=(SEM_SPEC, SEM_SPEC, *[HBM_SPEC] * m, BS(memory_space=pltpu.VMEM)),
        input_output_aliases={i: 2 + i for i in range(m)},
        compiler_params=pltpu.CompilerParams(has_side_effects=DATAFLOW))(*sources, *lands, *after_ops)
    return out[0], out[1], list(out[2:2 + n]), list(out[2 + n:2 + m]), out[-1]


def _split_wait(name, copies, send_sems, recv_sems, sources, lands, after):
    n = len(sources)
    m = n + len(lands)
    after_ops, after_specs = _after_operands(after)

    def body(*refs):
        ins = refs[:n]
        zones = refs[n:m] if m > n else ins
        for cp in copies(ins, zones, refs[m], refs[m + 1]):
            cp.wait_send()
            cp.wait_recv()

    out = pl.pallas_call(
        body, name=name,
        out_shape=tuple(pltpu.HBM(a.shape, a.dtype) for a in sources + lands),
        in_specs=[HBM_SPEC] * m + [SEM_SPEC, SEM_SPEC] + after_specs, out_specs=(HBM_SPEC,) * m,
        input_output_aliases={i: i for i in range(m)},
        compiler_params=pltpu.CompilerParams(has_side_effects=DATAFLOW))(*sources, *lands, send_sems, recv_sems, *after_ops)
    return list(out[:n]), list(out[n:])


class _WeightGatherer:
    def __init__(self, shards):
        self.shards, self.open = shards, {}
        self.me = 2 * lax.axis_index("x") + lax.axis_index("y")

    HALVED = ("f1a", "f1u", "win", "mix")

    def start(self, tag, after=()):
        shapes = [((N_SHARD,) + s.shape, s.dtype) for s in self.shards[tag]]
        copies = _gather_half_copies if tag in self.HALVED else _gather_copies
        self.open[tag] = _split_start("gather_start_" + tag, copies, self.shards[tag], shapes, after)
        return [self.open[tag][-1]]

    def sources(self, tags):
        return [s for tag in tags for s in self.shards[tag]]

    def finish(self, tag, after):
        send_sems, recv_sems, shards, lands, _ = self.open.pop(tag)
        copies = _gather_half_copies if tag in self.HALVED else _gather_copies
        shards, lands = _split_wait("gather_wait_" + tag, copies, send_sems, recv_sems, shards, lands, after)
        if tag in self.HALVED:
            lands = _sibling_fill(tag, lands)
        return [lax.dynamic_update_slice(zone, s[None], (self.me, 0, 0)) for zone, s in zip(lands, shards)]


class _GradReducer:
    def __init__(self):
        self.c_idx = lax.axis_index("c").astype(jnp.int32).reshape(1)
        self.place = jnp.stack([2 * lax.axis_index("x") + lax.axis_index("y"), lax.axis_index("c")]).astype(jnp.int32)
        self.swaps, self.open, self.landed, self.joins, self.reduced = {}, {}, {}, {}, []

    def swap_start(self, tag, grads, after=()):
        shapes = [((g.shape[0], g.shape[1] // 2, g.shape[2]), g.dtype) for g in grads]
        self.swaps[tag] = _split_start("reduce_swap_start_" + tag, _swap_copies, grads, shapes, after, fanout=1)
        return [self.swaps[tag][-1]]

    def start(self, tag, grads, after=(), swapped=()):
        pairs = []
        for s in swapped:
            send_sems, recv_sems, early, lands, _ = self.swaps.pop(s)
            behind = grads[-1:] or list(after)
            pairs += zip(*_split_wait("reduce_swap_wait_" + s, _swap_copies, send_sems, recv_sems, early, lands, behind))
        if grads:
            pairs += zip(grads, _sibling_swap_halves(tag, grads, after))
        parts = _chip_presum(tag, [g for g, _ in pairs], [s for _, s in pairs], self.c_idx)
        shapes = [((3,) + p.shape[1:], p.dtype) for p in parts]
        self.open[tag] = _split_start("reduce_exchange_start_" + tag, _chip_exchange_copies, parts, shapes)
        return [self.open[tag][-1]]

    def finish(self, tag, after):
        send_sems, recv_sems, parts, lands, _ = self.open.pop(tag)
        self.landed[tag] = _split_wait("reduce_exchange_wait_" + tag, _chip_exchange_copies, send_sems, recv_sems, parts, lands, after)
        return self.landed[tag][1][:1]

    def _sums(self, tag, after=()):
        parts, landed = self.landed.pop(tag)
        return _chip_sum(tag, parts, landed, self.place, after)

    def join_start(self, tag, after=()):
        self.joins[tag] = _split_start("reduce_join_start_" + tag, _join_copies, self._sums(tag, after), [], fanout=1)
        return [self.joins[tag][-1]]

    def join_finish(self, tag, after):
        send_sems, recv_sems, fulls, _, _ = self.joins.pop(tag)
        self.reduced += _split_wait("reduce_join_wait_" + tag, _join_copies, send_sems, recv_sems, fulls, [], after)[0]

    def join(self, tag, after=()):
        self.reduced += _sibling_join_halves(self._sums(tag), after)


def _chip_sum(tag, parts, gots, place, after=()):
    n = len(parts)
    tiles = [(p.shape[1] // REDUCE_STEPS, p.shape[2]) for p in parts]
    after_ops, after_specs = _after_operands(after)

    def body(place_ref, *refs):
        outs = refs[2 * n + len(after_ops):]
        for k in range(n):
            acc = refs[k][...].astype(F32)
            for j in range(3):
                acc = acc + refs[n + k][j].astype(F32)
            outs[k][...] = acc

    return list(pl.pallas_call(
        body, out_shape=tuple(SDS((2 * p.shape[1], p.shape[2]), F32) for p in parts),
        grid_spec=pltpu.PrefetchScalarGridSpec(
            num_scalar_prefetch=1, grid=(REDUCE_STEPS,),
            in_specs=[BS((None,) + tile, lambda i, place_ref: (place_ref[0], i, 0)) for tile in tiles]
            + [BS((3,) + tile, lambda i, place_ref: (0, i, 0)) for tile in tiles] + after_specs,
            out_specs=[BS(tile, lambda i, place_ref: (place_ref[1] * REDUCE_STEPS + i, 0)) for tile in tiles]),
        name="reduce_sum_" + tag, compiler_params=_params("parallel"))(place, *parts, *gots, *after_ops))


def _sibling_join_halves(fulls, after=()):
    n = len(fulls)
    after_ops, after_specs = _after_operands(after)

    def body(*refs):
        outs = refs[n + len(after_ops):2 * n + len(after_ops)]
        send_sems, recv_sems = refs[2 * n + len(after_ops):]
        copies = _join_copies(outs, outs, send_sems, recv_sems)
        for cp in copies:
            cp.start()
        for cp in copies:
            cp.wait_recv()
        for cp in copies:
            cp.wait_send()

    hbm = BS(memory_space=pl.ANY)
    return list(pl.pallas_call(
        body, out_shape=tuple(SDS(f.shape, f.dtype) for f in fulls),
        in_specs=[hbm] * n + after_specs, out_specs=(hbm,) * n, input_output_aliases={k: k for k in range(n)},
        scratch_shapes=[pltpu.SemaphoreType.DMA((n,)), pltpu.SemaphoreType.DMA((n,))],
        name="reduce_sibling_join", compiler_params=_params())(*fulls, *after_ops))


N_DEV = 8


def _sum_devices(packs):
    _, rows, lanes = packs.shape

    def body(p_ref, o_ref):
        acc = p_ref[0]
        for dev in range(1, N_DEV):
            acc = acc + p_ref[dev]
        o_ref[...] = acc

    vm = BS(memory_space=pltpu.VMEM)
    return pl.pallas_call(body, out_shape=SDS((rows, lanes), F32), in_specs=[vm], out_specs=vm,
                          name="small_sum", compiler_params=_params())(packs)


def _adamw_refs(w_ref, g_ref, m_ref, v_ref, go_ref, d_ref, mo_ref, vo_ref):
    bc1 = 1.0 - ADAM_B1 ** ADAM_STEP
    bc2 = 1.0 - ADAM_B2 ** ADAM_STEP
    g = g_ref[...]
    m_new = ADAM_B1 * m_ref[...] + (1.0 - ADAM_B1) * g
    v_new = ADAM_B2 * v_ref[...] + (1.0 - ADAM_B2) * (g * g)
    go_ref[...] = g
    mo_ref[...] = m_new
    vo_ref[...] = v_new
    d_ref[...] = -ADAM_LR * ((m_new / bc1) / (jnp.sqrt(v_new / bc2) + ADAM_EPS) + ADAM_WD * w_ref[...])


def _adamw_small(ws, gs, ms, vs):
    n = len(ws)

    def body(*refs):
        for k in range(n):
            _adamw_refs(*[refs[j * n + k] for j in range(4)], *refs[4 * n + 4 * k:4 * n + 4 * k + 4])

    vm = BS(memory_space=pltpu.VMEM)
    outs = pl.pallas_call(
        body, out_shape=tuple(SDS(a.shape, F32) for a in ws for _ in range(4)), in_specs=[vm] * (4 * n),
        out_specs=(vm,) * (4 * n), name="adamw_small", compiler_params=_params())(*ws, *gs, *ms, *vs)
    return [outs[4 * k:4 * k + 4] for k in range(n)]


def _adamw(name, w, grad, row0, m, v, after=()):
    rows, cols = w.shape
    tr = rows if rows < 16 else _row_tile(rows, 352)
    after_ops, after_specs = _after_operands(after)

    def body(w_ref, g_ref, m_ref, v_ref, *rest):
        _adamw_refs(w_ref, g_ref, m_ref, v_ref, *rest[len(after_ops):])

    blk = BS((tr, cols), lambda i: (i, 0))
    shape = SDS((rows, cols), F32)
    return pl.pallas_call(
        body, out_shape=(shape,) * 4, grid=(rows // tr,),
        in_specs=[blk, BS((tr, cols), lambda i: (row0 // tr + i, 0)), blk, blk] + after_specs, out_specs=(blk,) * 4,
        name=name, compiler_params=_params("parallel"))(w, grad, m, v, *after_ops)


SMALL_LANES = 128


SMALL_TILE = 8 * SMALL_LANES


def _packed_rows(p):
    return -(-p.size // SMALL_TILE) * 8


def _pack_small(parts):
    tiles = [jnp.pad(jnp.ravel(p), (0, _packed_rows(p) * SMALL_LANES - p.size)).reshape(-1, SMALL_LANES) for p in parts]
    rows = sum(t.shape[0] for t in tiles)
    return jnp.concatenate(tiles + [jnp.zeros((-rows % 64, SMALL_LANES), F32)], axis=0)


def _unpack_small(packed, like):
    out, at = [], 0
    for p in like:
        rows = _packed_rows(p)
        out.append(jnp.ravel(packed[at:at + rows])[:p.size].reshape(p.shape))
        at += rows
    return out


def kernel(x, mem, ffn1_norm, ffn1_w_gate, ffn1_w_up, ffn1_w_down, mix_norm, w_in, pool_w, pool_scale, w_pool_proj, ssm_a_re, ssm_a_im, ssm_log_dt, ssm_b_re, ssm_b_im, ssm_c_re, ssm_c_im, ssm_d, w_glu_val, w_glu_gate, w_mix_out, xattn_norm, mem_norm, w_q, w_kv, w_xo, ffn2_norm, ffn2_w_gate, ffn2_w_up, ffn2_w_down, final_norm, loss_target, m_ffn1_norm, m_ffn1_w_gate, m_ffn1_w_up, m_ffn1_w_down, m_mix_norm, m_w_in, m_pool_w, m_pool_scale, m_w_pool_proj, m_ssm_a_re, m_ssm_a_im, m_ssm_log_dt, m_ssm_b_re, m_ssm_b_im, m_ssm_c_re, m_ssm_c_im, m_ssm_d, m_w_glu_val, m_w_glu_gate, m_w_mix_out, m_xattn_norm, m_mem_norm, m_w_q, m_w_kv, m_w_xo, m_ffn2_norm, m_ffn2_w_gate, m_ffn2_w_up, m_ffn2_w_down, m_final_norm, v_ffn1_norm, v_ffn1_w_gate, v_ffn1_w_up, v_ffn1_w_down, v_mix_norm, v_w_in, v_pool_w, v_pool_scale, v_w_pool_proj, v_ssm_a_re, v_ssm_a_im, v_ssm_log_dt, v_ssm_b_re, v_ssm_b_im, v_ssm_c_re, v_ssm_c_im, v_ssm_d, v_w_glu_val, v_w_glu_gate, v_w_mix_out, v_xattn_norm, v_mem_norm, v_w_q, v_w_kv, v_w_xo, v_ffn2_norm, v_ffn2_w_gate, v_ffn2_w_up, v_ffn2_w_down, v_final_norm):
    given = dict(locals())
    w = {n: given[n] for n in WEIGHTS}
    m = {n: given["m_" + n] for n in WEIGHTS}
    v = {n: given["v_" + n] for n in WEIGHTS}

    def shard_view(a, n):
        return a[0].T if n in TRANSPOSED else a[0]

    def shard_unview(a, n):
        return (a.T if n in TRANSPOSED else a)[None]

    shards = {tag: [jnp.concatenate([shard_view(w[n], n).astype(BF16) for n in grp], axis=0) for grp in arrays]
              for tag, arrays in GATHER_PHASES.items()}
    reducer = _GradReducer()
    ws, ms, vs = ({n: _small_view(a[n], n) for n in SMALL} for a in (w, m, v))
    loss_part, grad_x, _, small = _device_step(x[0], mem[0], loss_target[0], _WeightGatherer(shards), ws, reducer)

    small_like = [ws[n] for n in SMALL] + [loss_part[0, :1]]
    pack = _pack_small([small[n] for n in SMALL] + [loss_part[0, :1]])
    everyone = _split_start("small_start", _everyone_copies, [pack], [((N_DEV,) + pack.shape, F32)], fanout=N_DEV - 1)

    grads, delta, new_m, new_v = {}, {}, {}, {}
    big_done = []

    def update(groups, reduced):
        for grp, red in zip(groups, reduced):
            row0 = 0
            for n in grp:
                w_n = shard_view(w[n], n)
                outs = _adamw("adamw_" + n, w_n, red, row0, shard_view(m[n], n), shard_view(v[n], n), after=everyone[-1:])
                grads[n], delta[n], new_m[n], new_v[n] = (shard_unview(o, n) for o in outs)
                big_done.append(outs[1])
                row0 += w_n.shape[0]

    n_a = len(reducer.reduced)
    update(REDUCE_GROUPS[:n_a], reducer.reduced)
    reducer.finish("b", list(big_done))
    reducer.join("b")
    update(REDUCE_GROUPS[n_a:], reducer.reduced[n_a:])

    send_sems, recv_sems, packs, landed, _ = everyone
    packs, landed = _split_wait("small_wait", _everyone_copies, send_sems, recv_sems, packs, landed, big_done)
    mine = 4 * lax.axis_index("x") + 2 * lax.axis_index("y") + lax.axis_index("c")
    summed = _sum_devices(lax.dynamic_update_slice(landed[0], packs[0][None], (mine, 0, 0)))
    g_small = dict(zip(SMALL + ("loss",), _unpack_small(summed, small_like)))
    loss = g_small.pop("loss").reshape(())
    def two_d(a):
        return a.reshape(-1, a.shape[-1])

    updated = _adamw_small(*([two_d(a[n]) for n in SMALL] for a in (ws, g_small, ms, vs)))
    for n, outs in zip(SMALL, updated):
        grads[n], delta[n], new_m[n], new_v[n] = (_small_view(o.reshape(ws[n].shape), n) for o in outs)

    return (loss, grad_x[None], *[grads[n] for n in WEIGHTS], *[delta[n] for n in WEIGHTS],
            *[new_m[n] for n in WEIGHTS], *[new_v[n] for n in WEIGHTS])
```

```python
import functools
import math

import jax
import jax.numpy as jnp
from jax import lax
from jax.experimental import pallas as pl
from jax.experimental.pallas import tpu as pltpu

F32 = jnp.float32
BF16 = jnp.bfloat16
SDS = jax.ShapeDtypeStruct
BS = pl.BlockSpec
MESH = pl.DeviceIdType.MESH

D_MODEL = 1024
D_FF = 2816
N_SHARD = 4
FF_SH = D_FF // N_SHARD
D_POOL = 512
POOL_WINDOWS = (2, 4, 8, 16)
POOL_GROUP = 128
D_SSM = 256
SSM_GROUPS = 16
SSM_GROUP = 16
SSM_STATE = 64
SSM_CH = SSM_GROUPS * SSM_STATE
N_HEADS = 4
HEAD_DIM = 256
EPS = 1e-6
ADAM_LR, ADAM_B1, ADAM_B2, ADAM_EPS, ADAM_WD, ADAM_STEP = 0.001, 0.9, 0.999, 1e-08, 0.01, 10

VMEM_LIMIT_V7X = 58 * 1024 * 1024
TM = 512

NN = (((1,), (0,)), ((), ()))
NT = (((1,), (1,)), ((), ()))
TN = (((0,), (0,)), ((), ()))


def _params(*sem):
    return pltpu.CompilerParams(dimension_semantics=sem if sem else None, vmem_limit_bytes=VMEM_LIMIT_V7X)


def _dot(a, b, dims=NN):
    return lax.dot_general(a.astype(BF16), b.astype(BF16), dims, preferred_element_type=F32)


def _sigmoid(v):
    return pl.reciprocal(1.0 + jnp.exp(-v), approx=True)


def _block_dims(spec):
    return tuple(d for d in spec.block_shape if d is not None)


def _after_operands(after):
    return list(after), [BS(memory_space=pl.ANY)] * len(after)


def _mm(name, pairs, *, grid, out_shape, out_spec, red_axis=None, extras=(), epilogue=None, after=()):
    n_pairs, n_extra = len(pairs), len(extras)
    n_red = grid[red_axis] if red_axis is not None else 1
    dims = [p[4] for p in pairs]

    def body(*refs):
        ab = refs[:2 * n_pairs]
        ex = refs[2 * n_pairs:2 * n_pairs + n_extra]
        o_ref = refs[2 * n_pairs + n_extra + len(after)]

        def partial():
            acc = None
            for p in range(n_pairs):
                t = _dot(ab[2 * p][...], ab[2 * p + 1][...], dims[p])
                acc = t if acc is None else acc + t
            return acc

        def finish(acc):
            res = epilogue(acc, *[e[...] for e in ex]) if epilogue is not None else acc
            o_ref[...] = res.astype(o_ref.dtype)

        if n_red == 1:
            finish(partial())
        else:
            acc_ref = refs[-1]
            k = pl.program_id(red_axis)

            @pl.when(k == 0)
            def _():
                acc_ref[...] = jnp.zeros_like(acc_ref)

            acc_ref[...] += partial()

            @pl.when(k == n_red - 1)
            def _():
                finish(acc_ref[...])

    operands, in_specs = [], []
    for a, a_spec, b, b_spec, _ in pairs:
        operands += [a, b]
        in_specs += [a_spec, b_spec]
    for e, e_spec in extras:
        operands.append(e)
        in_specs.append(e_spec)
    after_ops, after_specs = _after_operands(after)
    operands += after_ops
    in_specs += after_specs
    scratch = [pltpu.VMEM(_block_dims(out_spec), F32)] if n_red > 1 else []
    sem = tuple("arbitrary" if ax == red_axis else "parallel" for ax in range(len(grid)))
    return pl.pallas_call(body, out_shape=out_shape, grid=grid, in_specs=in_specs, out_specs=out_spec,
                          scratch_shapes=scratch, name=name, compiler_params=_params(*sem))(*operands)


def _rmsnorm(name, h, gain, tm, after=()):
    t, d = h.shape
    after_ops, after_specs = _after_operands(after)

    def body(h_ref, g_ref, *rest):
        u_ref = rest[-1]
        hv = h_ref[...]
        r = lax.rsqrt(jnp.mean(hv * hv, axis=-1, keepdims=True) + EPS)
        u_ref[...] = ((hv * r) * g_ref[...]).astype(u_ref.dtype)

    return pl.pallas_call(
        body, out_shape=SDS((t, d), BF16), grid=(t // tm,),
        in_specs=[BS((tm, d), lambda i: (i, 0)), BS((1, d), lambda i: (0, 0))] + after_specs,
        out_specs=BS((tm, d), lambda i: (i, 0)), name=name, compiler_params=_params("parallel"))(h, gain, *after_ops)


def _rmsnorm_bwd(name, h, gain, du, dh_in, tm):
    t, d = h.shape
    has_in = dh_in is not None

    def body(*refs):
        if has_in:
            h_ref, g_ref, du_ref, dhin_ref, dh_ref, dhb_ref, dg_ref = refs
        else:
            h_ref, g_ref, du_ref, dh_ref, dhb_ref, dg_ref = refs
        i = pl.program_id(0)
        hv = h_ref[...]
        r = lax.rsqrt(jnp.mean(hv * hv, axis=-1, keepdims=True) + EPS)
        n = hv * r
        duv = du_ref[...].astype(F32)
        dn = duv * g_ref[...]
        dh = r * (dn - n * jnp.mean(dn * n, axis=-1, keepdims=True))
        if has_in:
            dh = dhin_ref[...] + dh
        dh_ref[...] = dh
        dhb_ref[...] = dh.astype(BF16)

        @pl.when(i == 0)
        def _():
            dg_ref[...] = jnp.zeros_like(dg_ref)

        dg_ref[...] += jnp.sum(duv * n, axis=0, keepdims=True)

    row = BS((tm, d), lambda i: (i, 0))
    vec = BS((1, d), lambda i: (0, 0))
    operands = [h, gain, du] + ([dh_in] if has_in else [])
    in_specs = [row, vec, row] + ([row] if has_in else [])
    return pl.pallas_call(
        body, out_shape=(SDS((t, d), F32), SDS((t, d), BF16), SDS((1, d), F32)), grid=(t // tm,),
        in_specs=in_specs, out_specs=(row, row, vec), name=name, compiler_params=_params("arbitrary"))(*operands)


def _loss_head_tile(i, hv, g_ref, t_ref, loss_ref, dh_ref, dhb_ref, dg_ref):
    g = g_ref[...]
    r = lax.rsqrt(jnp.mean(hv * hv, axis=-1, keepdims=True) + EPS)
    n = hv * r
    err = n * g - t_ref[...]
    dy = err * (1.0 / hv.shape[-1])
    dn = dy * g
    dh = r * (dn - n * jnp.mean(dn * n, axis=-1, keepdims=True))
    dh_ref[...] = dh
    dhb_ref[...] = dh.astype(BF16)

    @pl.when(i == 0)
    def _():
        dg_ref[...] = jnp.zeros_like(dg_ref)
        loss_ref[...] = jnp.zeros_like(loss_ref)

    dg_ref[...] += jnp.sum(dy * n, axis=0, keepdims=True)
    part = 0.5 * jnp.sum(jnp.mean(err * err, axis=-1, keepdims=True), axis=0, keepdims=True)
    loss_ref[...] += jnp.broadcast_to(part, loss_ref.shape)


def _norm_tile(h, g_ref, u_ref):
    r = lax.rsqrt(jnp.mean(h * h, axis=-1, keepdims=True) + EPS)
    u_ref[...] = ((h * r) * g_ref[...]).astype(u_ref.dtype)


FFN_BLOCK = D_FF // 2


def _ffn_up(name, u, w_f, tm, after=()):
    t, d = u.shape
    after_ops, after_specs = _after_operands(after)

    def body(u_ref, wg_ref, wu_ref, *rest):
        pg_ref, pu_ref, a_ref = rest[len(after_ops):]
        uv = u_ref[...]
        for lo in range(0, D_FF, FFN_BLOCK):
            cols = slice(lo, lo + FFN_BLOCK)
            g = _dot(uv, wg_ref[cols, :], NT)
            up = _dot(uv, wu_ref[cols, :], NT)
            sg = _sigmoid(g)
            silu = g * sg
            a_ref[:, cols] = (silu * up).astype(BF16)
            pu_ref[:, cols] = (0.5 * silu).astype(BF16)
            pg_ref[:, cols] = (0.5 * sg * (1.0 + g * (1.0 - sg)) * up).astype(BF16)

    hid = BS((tm, D_FF), lambda i: (i, 0))
    shape = SDS((t, D_FF), BF16)
    whole = BS((D_FF, d), lambda i: (0, 0))
    return pl.pallas_call(
        body, out_shape=(shape, shape, shape), grid=(t // tm,),
        in_specs=[BS((tm, d), lambda i: (i, 0)), whole, whole] + after_specs,
        out_specs=(hid, hid, hid), name=name,
        compiler_params=_params("parallel"))(u, w_f["gate"], w_f["up"], *after_ops)


def _ffn_down(name, a, w_f, resid, tm, next_gain=None, head=None):
    t, d = resid.shape
    row = BS((tm, d), lambda i: (i, 0))
    vec = BS((1, d), lambda i: (0, 0))

    def body(a_ref, w_ref, res_ref, *rest):
        h = res_ref[...] + 0.5 * _dot(a_ref[...], w_ref[...])
        if head is not None:
            _loss_head_tile(pl.program_id(0), h, *rest)
        else:
            g_ref, h_ref, u_ref = rest
            h_ref[...] = h
            _norm_tile(h, g_ref, u_ref)

    if head is not None:
        extra, extra_specs = list(head), [vec, row]
        out_shape = (SDS((1, 128), F32), SDS((t, d), F32), SDS((t, d), BF16), SDS((1, d), F32))
        out_specs = (BS((1, 128), lambda i: (0, 0)), row, row, vec)
    else:
        extra, extra_specs = [next_gain], [vec]
        out_shape = (SDS((t, d), F32), SDS((t, d), BF16))
        out_specs = (row, row)
    return pl.pallas_call(
        body, out_shape=out_shape, grid=(t // tm,),
        in_specs=[BS((tm, D_FF), lambda i: (i, 0)), BS((D_FF, d), lambda i: (0, 0)), row] + extra_specs,
        out_specs=out_specs, name=name,
        compiler_params=_params("arbitrary" if head is not None else "parallel"))(a, w_f["down"], resid, *extra)


def _mix_in(u, w_t, tm, after=()):
    t, d = u.shape
    after_ops, after_specs = _after_operands(after)

    def body(u_ref, w_ref, *rest):
        o_ref, s_ref = rest[-2:]
        uv = u_ref[...]
        for lo in range(0, D_FF, FFN_BLOCK):
            o_ref[:, lo:lo + FFN_BLOCK] = _dot(uv, w_ref[lo:lo + FFN_BLOCK, :], NT)
        s_ref[...] = o_ref[:, D_POOL:D_POOL + D_SSM].astype(BF16)

    return pl.pallas_call(
        body, out_shape=(SDS((t, D_FF), F32), SDS((t, D_SSM), BF16)), grid=(t // tm,),
        in_specs=[BS((tm, d), lambda i: (i, 0)), BS((D_FF, d), lambda i: (0, 0))] + after_specs,
        out_specs=(BS((tm, D_FF), lambda i: (i, 0)), BS((tm, D_SSM), lambda i: (i, 0))), name="mix_in",
        compiler_params=_params("parallel"))(u, w_t, *after_ops)


def _mm_resid_norm(name, a, b, resid, next_gain, tm):
    t, d = resid.shape
    tm = min(2 * tm, t)

    def body(a_ref, b_ref, res_ref, g_ref, h_ref, u_ref):
        h = res_ref[...] + _dot(a_ref[...], b_ref[...])
        h_ref[...] = h
        _norm_tile(h, g_ref, u_ref)

    row = BS((tm, d), lambda i: (i, 0))
    return pl.pallas_call(
        body, out_shape=(SDS((t, d), F32), SDS((t, d), BF16)), grid=(t // tm,),
        in_specs=[BS((tm, a.shape[1]), lambda i: (i, 0)), BS(b.shape, lambda i: (0, 0)), row, BS((1, d), lambda i: (0, 0))],
        out_specs=(row, row), name=name, compiler_params=_params("parallel"))(a, b, resid, next_gain)


def _ffn_bwd_act(name, dh_b, w_f, pg, pu, tm, after=()):
    t, d = dh_b.shape
    after_ops, after_specs = _after_operands(after)

    def body(dh_ref, wd_ref, pg_ref, pu_ref, *rest):
        dg_ref, dup_ref = rest[len(after_ops):]
        dh = dh_ref[...]
        for lo in range(0, D_FF, FFN_BLOCK):
            cols = slice(lo, lo + FFN_BLOCK)
            da = _dot(dh, wd_ref[cols, :], NT)
            dg_ref[:, cols] = (da * pg_ref[:, cols].astype(F32)).astype(BF16)
            dup_ref[:, cols] = (da * pu_ref[:, cols].astype(F32)).astype(BF16)

    hid = BS((tm, D_FF), lambda i: (i, 0))
    shape = SDS((t, D_FF), BF16)
    return pl.pallas_call(
        body, out_shape=(shape, shape), grid=(t // tm,),
        in_specs=[BS((tm, d), lambda i: (i, 0)), BS((D_FF, d), lambda i: (0, 0)), hid, hid] + after_specs,
        out_specs=(hid, hid), name=name,
        compiler_params=_params("parallel"))(dh_b, w_f["down"], pg, pu, *after_ops)


def _ffn_dw(name, u, dg, dup, a, dh_b, tm):
    t, d = u.shape
    n_t = t // tm

    def body(u_ref, dg_ref, dup_ref, a_ref, dh_ref, og_ref, ou_ref, od_ref, acc):
        i = pl.program_id(1)

        @pl.when(i == 0)
        def _():
            acc[...] = jnp.zeros_like(acc)

        uv = u_ref[...]
        acc[0] += _dot(dg_ref[...], uv, TN)
        acc[1] += _dot(dup_ref[...], uv, TN)
        acc[2] += _dot(a_ref[...], dh_ref[...], TN)

        @pl.when(i == n_t - 1)
        def _():
            og_ref[...] = acc[0].astype(BF16)
            ou_ref[...] = acc[1].astype(BF16)
            od_ref[...] = (0.5 * acc[2]).astype(BF16)

    hid = BS((tm, FFN_BLOCK), lambda j, i: (i, j))
    row = BS((tm, d), lambda j, i: (i, 0))
    out = BS((FFN_BLOCK, d), lambda j, i: (j, 0))
    shape = SDS((D_FF, d), BF16)
    return pl.pallas_call(
        body, out_shape=(shape, shape, shape), grid=(D_FF // FFN_BLOCK, n_t),
        in_specs=[row, hid, hid, hid, row], out_specs=(out, out, out),
        scratch_shapes=[pltpu.VMEM((3, FFN_BLOCK, d), F32)],
        name=name, compiler_params=_params("parallel", "arbitrary"))(u, dg, dup, a, dh_b)


def _norm_bwd_tile(i, du, h_ref, g_ref, dhin_ref, dh_ref, dhb_ref, dg_ref):
    hv = h_ref[...]
    r = lax.rsqrt(jnp.mean(hv * hv, axis=-1, keepdims=True) + EPS)
    n = hv * r
    dn = du * g_ref[...]
    dh = dhin_ref[...] + r * (dn - n * jnp.mean(dn * n, axis=-1, keepdims=True))
    dh_ref[...] = dh
    dhb_ref[...] = dh.astype(BF16)

    @pl.when(i == 0)
    def _():
        dg_ref[...] = jnp.zeros_like(dg_ref)

    dg_ref[...] += jnp.sum(du * n, axis=0, keepdims=True)


def _norm_bwd_specs(tm):
    row = BS((tm, D_MODEL), lambda i: (i, 0))
    vec = BS((1, D_MODEL), lambda i: (0, 0))
    return [row, vec, row], (row, row, vec)


def _norm_bwd_shapes(t):
    return SDS((t, D_MODEL), F32), SDS((t, D_MODEL), BF16), SDS((1, D_MODEL), F32)


def _ffn_dx(name, dg, dup, w_f, h, gain, dh_in, tm, after=()):
    t = dg.shape[0]
    tm = tm // 2
    after_ops, after_specs = _after_operands(after)

    def body(dg_ref, dup_ref, wg_ref, wu_ref, h_ref, g_ref, dhin_ref, *rest):
        du = _dot(dg_ref[...], wg_ref[...]) + _dot(dup_ref[...], wu_ref[...])
        _norm_bwd_tile(pl.program_id(0), du, h_ref, g_ref, dhin_ref, *rest[len(after_ops):])

    hid = BS((tm, D_FF), lambda i: (i, 0))
    whole = BS((D_FF, D_MODEL), lambda i: (0, 0))
    norm_in, norm_out = _norm_bwd_specs(tm)
    return pl.pallas_call(
        body, out_shape=_norm_bwd_shapes(t), grid=(t // tm,),
        in_specs=[hid, hid, whole, whole] + norm_in + after_specs, out_specs=norm_out, name=name,
        compiler_params=_params("arbitrary"))(dg, dup, w_f["gate"], w_f["up"], h, gain, dh_in, *after_ops)


def _mm_norm_bwd(name, a, b, dims, h, gain, dh_in, tm):
    pieces = list(a) if isinstance(a, (list, tuple)) else [a]
    assert len(pieces) == 1 or dims == NN
    t = pieces[0].shape[0]
    widths = [p.shape[1] for p in pieces]
    row0 = [sum(widths[:j]) for j in range(len(pieces))]

    def body(*refs):
        a_refs, (b_ref, h_ref, g_ref, dhin_ref), outs = refs[:len(pieces)], refs[len(pieces):len(pieces) + 4], refs[len(pieces) + 4:]
        if len(pieces) == 1:
            du = _dot(a_refs[0][...], b_ref[...], dims)
        else:
            du = _dot(a_refs[0][...], b_ref[0:widths[0], :])
            for a_ref, r0, w in zip(a_refs[1:], row0[1:], widths[1:]):
                du = du + _dot(a_ref[...], b_ref[r0:r0 + w, :])
        _norm_bwd_tile(pl.program_id(0), du, h_ref, g_ref, dhin_ref, *outs)

    norm_in, norm_out = _norm_bwd_specs(tm)
    return pl.pallas_call(
        body, out_shape=_norm_bwd_shapes(t), grid=(t // tm,),
        in_specs=[BS((tm, w), lambda i: (i, 0)) for w in widths] + [BS(b.shape, lambda i: (0, 0))] + norm_in,
        out_specs=norm_out, name=name, compiler_params=_params("arbitrary"))(*pieces, b, h, gain, dh_in)


def _plain_mm(name, a, b, dims, out_dtype, tm, resid=None, after=()):
    t = a.shape[0]
    tm = min(2 * tm, t)
    n = b.shape[1] if dims == NN else b.shape[0]
    extras = [(resid, BS((tm, n), lambda i: (i, 0)))] if resid is not None else []
    epi = (lambda acc, res: res + acc) if resid is not None else None
    return _mm(name, [(a, BS((tm, a.shape[1]), lambda i: (i, 0)), b, BS(b.shape, lambda i: (0, 0)), dims)],
               grid=(t // tm,), out_shape=SDS((t, n), out_dtype), out_spec=BS((tm, n), lambda i: (i, 0)),
               extras=extras, epilogue=epi, after=after)


def _dw_mm(name, a, b, tm, out_dtype=BF16, after=()):
    t, k = a.shape
    n = b.shape[1]
    tm = min(2 * tm, t)
    return _mm(name, [(a, BS((tm, k), lambda i: (i, 0)), b, BS((tm, n), lambda i: (i, 0)), TN)],
               grid=(t // tm,), red_axis=0, out_shape=SDS((k, n), out_dtype), out_spec=BS((k, n), lambda i: (0, 0)),
               after=after)


def _mix_in_dw(pieces, u, tm):
    t, d = u.shape
    tm = min(2 * tm, t)
    n_steps = t // tm
    widths = [p.shape[1] for p in pieces]
    row0 = [sum(widths[:j]) for j in range(len(pieces))]
    assert sum(widths) == D_FF

    def body(*refs):
        p_refs, u_ref, o_ref, acc_ref = refs[:len(pieces)], refs[len(pieces)], refs[-2], refs[-1]
        k = pl.program_id(0)

        @pl.when(k == 0)
        def _():
            acc_ref[...] = jnp.zeros_like(acc_ref)

        uv = u_ref[...]
        for p_ref, r0, w in zip(p_refs, row0, widths):
            acc_ref[r0:r0 + w, :] += _dot(p_ref[...], uv, TN)

        @pl.when(k == n_steps - 1)
        def _():
            o_ref[...] = acc_ref[...].astype(BF16)

    return pl.pallas_call(
        body, out_shape=SDS((D_FF, d), BF16), grid=(n_steps,),
        in_specs=[BS((tm, w), lambda i: (i, 0)) for w in widths] + [BS((tm, d), lambda i: (i, 0))],
        out_specs=BS((D_FF, d), lambda i: (0, 0)), scratch_shapes=[pltpu.VMEM((D_FF, d), F32)],
        name="mix_in_dw", compiler_params=_params("arbitrary"))(*pieces, u)


POOL_CHUNK = 256
POOL_HALO = 8


def _window_sum(v, width, lead):
    n = v.shape[0]
    s = v
    k = 1
    while k < width:
        s = s + pltpu.roll(s, n - k, 0)
        k *= 2
    return pltpu.roll(s, lead, 0) if lead else s


def _pool_count(base, left, right, t, shape):
    pos = base + lax.broadcasted_iota(jnp.int32, shape, 0)
    lo = jnp.maximum(pos - left, 0)
    hi = jnp.minimum(pos + right + 1, t)
    return (hi - lo).astype(F32)


def _pool_fwd(proj, pool_w, pool_scale):
    t = proj.shape[0]
    c, h = POOL_CHUNK, POOL_HALO
    n_chunks = t // c

    def body(proj_hbm, pw_ref, sc_ref, pooled_ref, mixed_ref, ms_ref, pad_ref, sem):
        cp = pltpu.make_async_copy(proj_hbm.at[:, pl.ds(0, D_POOL)], pad_ref.at[pl.ds(h, t), :], sem)
        cp.start()
        pad_ref[pl.ds(0, h), :] = jnp.zeros((h, D_POOL), F32)
        pad_ref[pl.ds(t + h, h), :] = jnp.zeros((h, D_POOL), F32)
        cp.wait()
        for g, width in enumerate(POOL_WINDOWS):
            left = width // 2
            right = width - 1 - left
            cols = slice(g * POOL_GROUP, (g + 1) * POOL_GROUP)
            wmat = pw_ref[g].astype(BF16)
            scale = sc_ref[:, cols]

            def chunk(ci, carry, left=left, right=right, width=width, cols=cols, wmat=wmat, scale=scale):
                base = pl.multiple_of(ci * c, c)
                v = pad_ref[pl.ds(base, c + 2 * h), cols]
                win = _window_sum(v, width, left)[h:h + c]
                cnt = _pool_count(base, left, right, t, (c, POOL_GROUP))
                pooled = (win / cnt - v[h:h + c]).astype(BF16)
                mixed = _dot(pooled, wmat)
                pooled_ref[pl.ds(base, c), cols] = pooled
                mixed_ref[pl.ds(base, c), cols] = mixed.astype(BF16)
                ms_ref[pl.ds(base, c), cols] = (mixed * scale).astype(BF16)
                return carry

            lax.fori_loop(0, n_chunks, chunk, 0)

    vm = BS(memory_space=pltpu.VMEM)
    shape = SDS((t, D_POOL), BF16)
    return pl.pallas_call(
        body, out_shape=(shape, shape, shape),
        in_specs=[BS(memory_space=pl.ANY), vm, vm], out_specs=(vm, vm, vm),
        scratch_shapes=[pltpu.VMEM((t + 2 * h, D_POOL), F32), pltpu.SemaphoreType.DMA],
        name="pool_fwd", compiler_params=_params())(proj, pool_w, pool_scale)


def _pool_bwd(d_ms, mixed, pooled, pool_w, pool_scale):
    t = d_ms.shape[0]
    c, h = POOL_CHUNK, POOL_HALO
    n_chunks = t // c

    def body(dms_ref, mixed_ref, pooled_ref, pw_ref, sc_ref, dp_ref, dsc_ref, dpw_ref, pad_ref):
        pad_ref[pl.ds(0, h), :] = jnp.zeros((h, D_POOL), F32)
        pad_ref[pl.ds(t + h, h), :] = jnp.zeros((h, D_POOL), F32)
        for g, width in enumerate(POOL_WINDOWS):
            left = width // 2
            right = width - 1 - left
            cols = slice(g * POOL_GROUP, (g + 1) * POOL_GROUP)
            wmat = pw_ref[g].astype(BF16)
            scale = sc_ref[:, cols]

            def first(ci, carry, left=left, right=right, cols=cols, wmat=wmat, scale=scale):
                dsc, dpw = carry
                base = pl.multiple_of(ci * c, c)
                dms = dms_ref[pl.ds(base, c), cols].astype(F32)
                dsc = dsc + jnp.sum(dms * mixed_ref[pl.ds(base, c), cols].astype(F32), axis=0, keepdims=True)
                dmix = (dms * scale).astype(BF16)
                dpw = dpw + _dot(pooled_ref[pl.ds(base, c), cols], dmix, TN)
                dpooled = _dot(dmix, wmat, NT)
                cnt = _pool_count(base, left, right, t, (c, POOL_GROUP))
                pad_ref[pl.ds(base + h, c), cols] = dpooled / cnt
                return dsc, dpw

            dsc, dpw = lax.fori_loop(0, n_chunks, first,
                                     (jnp.zeros((1, POOL_GROUP), F32), jnp.zeros((POOL_GROUP, POOL_GROUP), F32)))
            dsc_ref[:, cols] = dsc
            dpw_ref[g] = dpw

            def second(ci, carry, left=left, right=right, width=width, cols=cols):
                base = pl.multiple_of(ci * c, c)
                v = pad_ref[pl.ds(base, c + 2 * h), cols]
                win = _window_sum(v, width, right)[h:h + c]
                cnt = _pool_count(base, left, right, t, (c, POOL_GROUP))
                dp_ref[pl.ds(base, c), cols] = (win - v[h:h + c] * cnt).astype(BF16)
                return carry

            lax.fori_loop(0, n_chunks, second, 0)

    vm = BS(memory_space=pltpu.VMEM)
    return pl.pallas_call(
        body, out_shape=(SDS((t, D_POOL), BF16), SDS((1, D_POOL), F32), SDS((4, POOL_GROUP, POOL_GROUP), F32)),
        in_specs=[vm] * 5, out_specs=(vm, vm, vm),
        scratch_shapes=[pltpu.VMEM((t + 2 * h, D_POOL), F32)],
        name="pool_bwd", compiler_params=_params())(d_ms, mixed, pooled, pool_w, pool_scale)


SSM_ROWS = 2 * SSM_GROUPS * SSM_GROUP
SSM_HALF = SSM_GROUPS * SSM_GROUP


def _ssm_zoh(a_r, a_i, ldt):
    dt = jnp.exp(ldt)
    mag = jnp.exp(dt * a_r)
    ang = dt * a_i
    cs, sn = jnp.cos(ang), jnp.sin(ang)
    abr, abi = mag * cs, mag * sn
    den = a_r * a_r + a_i * a_i
    nr = abr - 1.0
    qr = (nr * a_r + abi * a_i) / den
    qi = (abi * a_r - nr * a_i) / den
    return dt, mag, cs, sn, abr, abi, den, nr, qr, qi


def _ssm_group_mask():
    row = lax.broadcasted_iota(jnp.int32, (SSM_HALF, SSM_CH), 0)
    col = lax.broadcasted_iota(jnp.int32, (SSM_HALF, SSM_CH), 1)
    return (row // SSM_GROUP) == (col // SSM_STATE)


def _ssm_prep(a_r, a_i, ldt, b_r, b_i, c_r, c_i, after=()):
    after_ops, after_specs = _after_operands(after)

    def body(ar_ref, ai_ref, ldt_ref, br_ref, bi_ref, cr_ref, ci_ref, *rest):
        abr_ref, abi_ref, win_ref, wint_ref, woutt_ref, wout_ref = rest[len(after_ops):]
        *_, abr, abi, _, _, qr, qi = _ssm_zoh(ar_ref[...], ai_ref[...], ldt_ref[...])
        abr_ref[...] = abr
        abi_ref[...] = abi
        b_r, b_i = br_ref[...], bi_ref[...]
        bbr = qr * b_r - qi * b_i
        bbi = qr * b_i + qi * b_r
        mask = _ssm_group_mask()
        state = lax.broadcasted_iota(jnp.int32, (SSM_STATE, SSM_CH), 0)
        col = lax.broadcasted_iota(jnp.int32, (SSM_STATE, SSM_CH), 1)
        every_group = (col % SSM_STATE == state).astype(BF16)

        def spread(x):
            return jnp.where(mask, _dot(x, every_group), 0.0)

        for d in range(2):
            rows = slice(d * SSM_HALF, (d + 1) * SSM_HALF)
            for half, x_in, x_out in ((0, bbr[rows], cr_ref[rows, :]), (1, bbi[rows], -ci_ref[rows, :])):
                cols = slice(half * SSM_CH, (half + 1) * SSM_CH)
                m_in, m_out = spread(x_in), spread(x_out)
                win_ref[d, :, cols] = m_in.astype(BF16)
                wint_ref[d, cols, :] = m_in.T.astype(BF16)
                woutt_ref[d, :, cols] = m_out.astype(BF16)
                wout_ref[d, cols, :] = m_out.T.astype(BF16)

    vm = BS(memory_space=pltpu.VMEM)
    vec = SDS((SSM_ROWS, SSM_STATE), F32)
    wide = SDS((2, SSM_HALF, 2 * SSM_CH), BF16)
    tall = SDS((2, 2 * SSM_CH, SSM_HALF), BF16)
    return pl.pallas_call(body, out_shape=(vec, vec, wide, tall, wide, tall), in_specs=[vm] * 7 + after_specs,
                          out_specs=(vm,) * 6, name="ssm_prep",
                          compiler_params=_params())(a_r, a_i, ldt, b_r, b_i, c_r, c_i, *after_ops)


def _ssm_prep_bwd(a_r, a_i, ldt, b_r, b_i, d_abr, d_abi, d_win, d_woutt):
    def body(ar_ref, ai_ref, ldt_ref, br_ref, bi_ref, *rest):
        (dabr_refs, dabi_refs, dwin_refs, dwoutt_refs), outs = [rest[2 * k:2 * k + 2] for k in range(4)], rest[8:]
        dar_ref, dai_ref, dldt_ref, dbr_ref, dbi_ref, dcr_ref, dci_ref = outs
        a_r, a_i = ar_ref[...], ai_ref[...]
        dt, mag, cs, sn, abr, abi, den, nr, qr, qi = _ssm_zoh(a_r, a_i, ldt_ref[...])
        mask = _ssm_group_mask()
        col = lax.broadcasted_iota(jnp.int32, (SSM_CH, SSM_STATE), 0)
        state = lax.broadcasted_iota(jnp.int32, (SSM_CH, SSM_STATE), 1)
        own_state = (col % SSM_STATE == state).astype(BF16)

        def pick(dense):
            m = jnp.where(mask, dense, 0.0)
            hi = m.astype(BF16)
            lo = m - hi.astype(F32)
            return _dot(hi, own_state) + _dot(lo, own_state)

        def picked(refs, half):
            cols = slice(half * SSM_CH, (half + 1) * SSM_CH)
            return jnp.concatenate([pick(ref[:, cols]) for ref in refs], axis=0)

        first_channel = lax.broadcasted_iota(jnp.int32, (SSM_HALF, SSM_CH), 0) % SSM_GROUP == 0

        def first_rows(refs):
            return jnp.concatenate(
                [pick(jnp.where(first_channel, jnp.broadcast_to(ref[...], (SSM_HALF, SSM_CH)), 0.0)) for ref in refs], axis=0)

        g_r, g_i = picked(dwin_refs, 0), picked(dwin_refs, 1)
        dcr_ref[...] = picked(dwoutt_refs, 0)
        dci_ref[...] = -picked(dwoutt_refs, 1)
        b_r, b_i = br_ref[...], bi_ref[...]
        dbr_ref[...] = g_r * qr + g_i * qi
        dbi_ref[...] = g_i * qr - g_r * qi
        gqr = g_r * b_r + g_i * b_i
        gqi = g_i * b_r - g_r * b_i
        g_nr_num = gqr / den
        g_ni_num = gqi / den
        g_den = -(gqr * qr + gqi * qi) / den
        g_nr = g_nr_num * a_r - g_ni_num * a_i
        g_abi = g_nr_num * a_i + g_ni_num * a_r
        d_ar = g_nr_num * nr + g_ni_num * abi + 2.0 * a_r * g_den
        d_ai = g_nr_num * abi - g_ni_num * nr + 2.0 * a_i * g_den
        g_abr = first_rows(dabr_refs) + g_nr
        g_abi = first_rows(dabi_refs) + g_abi
        g_mag = g_abr * cs + g_abi * sn
        g_ang = mag * (g_abi * cs - g_abr * sn)
        g_e = g_mag * mag
        d_ar = d_ar + g_e * dt
        d_ai = d_ai + g_ang * dt
        g_dt = g_e * a_r + g_ang * a_i
        dar_ref[...] = d_ar
        dai_ref[...] = d_ai
        dldt_ref[...] = g_dt * dt

    vm = BS(memory_space=pltpu.VMEM)
    vec = SDS((SSM_ROWS, SSM_STATE), F32)
    return pl.pallas_call(body, out_shape=(vec,) * 7, in_specs=[vm] * 13, out_specs=(vm,) * 7, name="ssm_prep_bwd",
                          compiler_params=_params())(a_r, a_i, ldt, b_r, b_i, *d_abr, *d_abi, *d_win, *d_woutt)


SCAN_ROWS = 512
SCAN_SUB = 128


def _ssm_scan(name, inp, w1, a_r, a_i, w2, reverse, dr, conj=False):
    t = inp.shape[0]
    rows = min(SCAN_ROWS, t)
    n = t // rows
    n_sub = rows // SCAN_SUB
    ch = SSM_CH
    at = (lambda i: (n - 1 - i, 0)) if reverse else (lambda i: (i, 0))

    def body(in_ref, w1_ref, ar_ref, ai_ref, w2_ref, sb_ref, out_ref, cr_ref, ci_ref, k_ref, st_ref):
        i = pl.program_id(0)

        @pl.when(i == 0)
        def _():
            ar8 = jnp.broadcast_to(ar_ref[...], (8, ch))
            ai8 = jnp.broadcast_to(-ai_ref[...] if conj else ai_ref[...], (8, ch))
            row = lax.broadcasted_iota(jnp.int32, (8, ch), 0)
            rank = (7 - row) if reverse else row
            powers = [(ar8, ai8)]
            for _ in range(7):
                p_r, p_i = powers[-1]
                powers.append((p_r * ar8 - p_i * ai8, p_r * ai8 + p_i * ar8))
            zero = jnp.zeros((8, ch), F32)
            for slot, k in enumerate((1, 2, 4)):
                k_ref[2 * slot] = jnp.where(rank >= k, powers[k - 1][0], zero)
                k_ref[2 * slot + 1] = jnp.where(rank >= k, powers[k - 1][1], zero)
            carry_r, carry_i = zero, zero
            for j in range(8):
                carry_r = jnp.where(rank == j, powers[j][0], carry_r)
                carry_i = jnp.where(rank == j, powers[j][1], carry_i)
            k_ref[6] = carry_r
            k_ref[7] = carry_i
            cr_ref[...] = zero
            ci_ref[...] = zero

        def group(r0, carry):
            c_r, c_i = carry
            x_r = st_ref[pl.ds(r0, 8), 0:ch]
            x_i = st_ref[pl.ds(r0, 8), ch:2 * ch]
            for slot, k in enumerate((1, 2, 4)):
                shift = (8 - k) if reverse else k
                s_r = pltpu.roll(x_r, shift, 0)
                s_i = pltpu.roll(x_i, shift, 0)
                m_r, m_i = k_ref[2 * slot], k_ref[2 * slot + 1]
                x_r, x_i = x_r + m_r * s_r - m_i * s_i, x_i + m_r * s_i + m_i * s_r
            p_r, p_i = k_ref[6], k_ref[7]
            x_r, x_i = x_r + p_r * c_r - p_i * c_i, x_i + p_r * c_i + p_i * c_r
            st_ref[pl.ds(r0, 8), 0:ch] = x_r
            st_ref[pl.ds(r0, 8), ch:2 * ch] = x_i
            last = 0 if reverse else 7
            return (jnp.broadcast_to(x_r[last:last + 1, :], (8, ch)), jnp.broadcast_to(x_i[last:last + 1, :], (8, ch)))

        carry = (cr_ref[...], ci_ref[...])
        for sc in (range(n_sub - 1, -1, -1) if reverse else range(n_sub)):
            part = pl.ds(sc * SCAN_SUB, SCAN_SUB)
            st_ref[part, :] = _dot(in_ref[part, :], w1_ref[...])
            for gi in range(SCAN_SUB // 8):
                g = (SCAN_SUB // 8 - 1 - gi) if reverse else gi
                carry = group(sc * SCAN_SUB + g * 8, carry)
            states = st_ref[part, :].astype(BF16)
            sb_ref[part, :] = states
            out_ref[part, :] = _dot(states, w2_ref[...])
        cr_ref[...] = carry[0]
        ci_ref[...] = carry[1]

    return pl.pallas_call(
        body, out_shape=(SDS((t, 2 * ch), BF16), SDS((t, D_SSM), F32)), grid=(n,),
        in_specs=[BS((rows, D_SSM), at), BS((None, D_SSM, 2 * ch), lambda i: (dr, 0, 0)), BS((None, 1, ch), lambda i: (dr, 0, 0)),
                  BS((None, 1, ch), lambda i: (dr, 0, 0)), BS((None, 2 * ch, D_SSM), lambda i: (dr, 0, 0))],
        out_specs=(BS((rows, 2 * ch), at), BS((rows, D_SSM), at)),
        scratch_shapes=[pltpu.VMEM((8, ch), F32), pltpu.VMEM((8, ch), F32), pltpu.VMEM((8, 8, ch), F32),
                        pltpu.VMEM((rows, 2 * ch), F32)],
        name=name, compiler_params=_params("arbitrary"))(inp, w1, a_r, a_i, w2)


DA_ROWS = 1024


def _ssm_param_grads(name, lam, states, u, dy, reverse, after=()):
    t = lam.shape[0]
    rows = min(DA_ROWS, t)
    n = t // rows
    halo_rows = 16
    nb = rows // halo_rows
    ch = SSM_CH
    if reverse:
        halo_at = lambda i: (jnp.minimum((i + 1) * nb, t // halo_rows - 1), 0)
    else:
        halo_at = lambda i: (jnp.maximum(i * nb - 1, 0), 0)

    after_ops, after_specs = _after_operands(after)

    def body(lam_ref, x_ref, halo_ref, u_ref, dy_ref, *rest):
        dr_ref, di_ref, dwin_ref, dwoutt_ref = rest[len(after_ops):]
        i = pl.program_id(0)

        @pl.when(i == 0)
        def _():
            dr_ref[...] = jnp.zeros_like(dr_ref)
            di_ref[...] = jnp.zeros_like(di_ref)
            dwin_ref[...] = jnp.zeros_like(dwin_ref)
            dwoutt_ref[...] = jnp.zeros_like(dwoutt_ref)

        dwin_ref[...] += _dot(u_ref[...], lam_ref[...], TN)
        dwoutt_ref[...] += _dot(dy_ref[...], x_ref[...], TN)
        row = lax.broadcasted_iota(jnp.int32, (rows, ch), 0)
        if reverse:
            edge, shift, h_row, live = rows - 1, rows - 1, 0, i < n - 1
        else:
            edge, shift, h_row, live = 0, 1, halo_rows - 1, i > 0

        def neighbour(lo):
            halo = halo_ref[:, lo:lo + ch].astype(F32)[h_row:h_row + 1]
            halo = jnp.where(live, halo, 0.0)
            x = x_ref[:, lo:lo + ch].astype(F32)
            return jnp.where(row == edge, jnp.broadcast_to(halo, (rows, ch)), pltpu.roll(x, shift, 0))

        xp_r, xp_i = neighbour(0), neighbour(ch)
        l_r, l_i = lam_ref[:, 0:ch].astype(F32), lam_ref[:, ch:2 * ch].astype(F32)
        dr_ref[...] += jnp.sum(l_r * xp_r + l_i * xp_i, axis=0, keepdims=True)
        di_ref[...] += jnp.sum(l_i * xp_r - l_r * xp_i, axis=0, keepdims=True)

    blk = BS((rows, 2 * ch), lambda i: (i, 0))
    thin = BS((rows, D_SSM), lambda i: (i, 0))
    vec = BS((1, ch), lambda i: (0, 0))
    mat = BS((D_SSM, 2 * ch), lambda i: (0, 0))
    return pl.pallas_call(
        body, out_shape=(SDS((1, ch), F32), SDS((1, ch), F32), SDS((D_SSM, 2 * ch), F32), SDS((D_SSM, 2 * ch), F32)),
        grid=(n,), in_specs=[blk, blk, BS((halo_rows, 2 * ch), halo_at), thin, thin] + after_specs,
        out_specs=(vec, vec, mat, mat),
        name=name, compiler_params=_params("arbitrary"))(lam, states, states, u, dy, *after_ops)


GELU_C = math.sqrt(2.0 / math.pi)
GELU_K = 0.044715


def _ssm_combine(proj, y_fwd, y_bwd, d_skip, tm, after=()):
    t = proj.shape[0]
    after_ops, after_specs = _after_operands(after)

    def body(s_ref, yf_ref, yb_ref, d_ref, *rest):
        yt_ref, g_ref = rest[len(after_ops):]
        y = s_ref[...] * d_ref[...] + yf_ref[...] + yb_ref[...]
        yt_ref[...] = y
        th = jnp.tanh(GELU_C * (y + GELU_K * y * y * y))
        g_ref[...] = (0.5 * y * (1.0 + th)).astype(BF16)

    blk = BS((tm, D_SSM), lambda i: (i, 0))
    return pl.pallas_call(
        body, out_shape=(SDS((t, D_SSM), F32), SDS((t, D_SSM), BF16)), grid=(t // tm,),
        in_specs=[BS((tm, D_SSM), lambda i: (i, D_POOL // D_SSM)), blk, blk, BS((1, D_SSM), lambda i: (0, 0))] + after_specs,
        out_specs=(blk, blk), name="ssm_combine",
        compiler_params=_params("parallel"))(proj, y_fwd, y_bwd, d_skip, *after_ops)


def _ssm_ds(proj, d_yt, du_fwd, du_bwd, d_skip, tm):
    t = proj.shape[0]

    def body(s_ref, dy_ref, duf_ref, dub_ref, d_ref, ds_ref, dd_ref):
        i = pl.program_id(0)
        dy = dy_ref[...]
        ds_ref[...] = (dy * d_ref[...] + duf_ref[...] + dub_ref[...]).astype(BF16)

        @pl.when(i == 0)
        def _():
            dd_ref[...] = jnp.zeros_like(dd_ref)

        dd_ref[...] += jnp.sum(dy * s_ref[...], axis=0, keepdims=True)

    blk = BS((tm, D_SSM), lambda i: (i, 0))
    vec = BS((1, D_SSM), lambda i: (0, 0))
    return pl.pallas_call(
        body, out_shape=(SDS((t, D_SSM), BF16), SDS((1, D_SSM), F32)), grid=(t // tm,),
        in_specs=[BS((tm, D_SSM), lambda i: (i, D_POOL // D_SSM)), blk, blk, blk, vec],
        out_specs=(blk, vec), name="ssm_ds", compiler_params=_params("arbitrary"))(proj, d_yt, du_fwd, du_bwd, d_skip)


G_POOL_AT = D_POOL + D_SSM
G_SSM_AT = G_POOL_AT + D_MODEL
E_VAL, E_GATE = D_POOL, D_POOL + D_SSM


def _merge_specs(tm):
    return [BS((tm, D_POOL), lambda i: (i, 0)), BS((tm, D_SSM), lambda i: (i, 0)),
            BS((N_SHARD, 1024, 256), lambda i: (0, 0, 0)), BS((tm, D_FF), lambda i: (i, 0))]


def _merge_parts(s, ms, yv, w_ref, proj_ref):
    lo = 256 * s
    zp = _dot(ms, w_ref[s, 0:E_VAL, :])
    zv = _dot(yv, w_ref[s, E_VAL:E_GATE, :])
    zg = _dot(yv, w_ref[s, E_GATE:, :])
    return zp, zv, zg, proj_ref[:, G_POOL_AT + lo:G_POOL_AT + lo + 256], proj_ref[:, G_SSM_AT + lo:G_SSM_AT + lo + 256]


def _mixer_merge(ms, yssm, w_e, proj, tm):
    t = ms.shape[0]

    def body(ms_ref, y_ref, w_ref, proj_ref, o_ref):
        msv, yv = ms_ref[...], y_ref[...]
        for s in range(N_SHARD):
            zp, zv, zg, gp, gs = _merge_parts(s, msv, yv, w_ref, proj_ref)
            o_ref[:, 256 * s:256 * (s + 1)] = (_sigmoid(gp) * zp + _sigmoid(gs) * zv * _sigmoid(zg)).astype(BF16)

    row = BS((tm, D_MODEL), lambda i: (i, 0))
    return pl.pallas_call(
        body, out_shape=SDS((t, D_MODEL), BF16), grid=(t // tm,), in_specs=_merge_specs(tm), out_specs=row,
        name="mixer_merge", compiler_params=_params("parallel"))(ms, yssm, w_e, proj)


def _mixer_merge_bwd(ms, yssm, w_e, proj, dmerged, tm):
    t = ms.shape[0]

    def body(ms_ref, y_ref, w_ref, proj_ref, dm_ref, dgp_ref, dgs_ref, dzp_ref, dzv_ref, dzg_ref):
        msv, yv = ms_ref[...], y_ref[...]
        for s in range(N_SHARD):
            cols = slice(256 * s, 256 * (s + 1))
            zp, zv, zg, gp, gs = _merge_parts(s, msv, yv, w_ref, proj_ref)
            dm = dm_ref[:, cols].astype(F32)
            sp, ss, sg = _sigmoid(gp), _sigmoid(gs), _sigmoid(zg)
            dgp_ref[:, cols] = (dm * zp * sp * (1.0 - sp)).astype(BF16)
            dgs_ref[:, cols] = (dm * zv * sg * ss * (1.0 - ss)).astype(BF16)
            dzp_ref[:, cols] = (dm * sp).astype(BF16)
            dz = dm * ss
            dzv_ref[:, cols] = (dz * sg).astype(BF16)
            dzg_ref[:, cols] = (dz * zv * sg * (1.0 - sg)).astype(BF16)

    row = BS((tm, D_MODEL), lambda i: (i, 0))
    shape = SDS((t, D_MODEL), BF16)
    return pl.pallas_call(
        body, out_shape=(shape,) * 5, grid=(t // tm,), in_specs=_merge_specs(tm) + [row],
        out_specs=(row,) * 5, name="mixer_merge_bwd",
        compiler_params=_params("parallel"))(ms, yssm, w_e, proj, dmerged)


def _mixer_dw(ms, yssm, dzp, dzv, dzg, tm):
    t = ms.shape[0]
    tm = min(2 * tm, t)
    n_t = t // tm

    def body(ms_ref, y_ref, dzp_ref, dzv_ref, dzg_ref, o_ref, acc):
        i = pl.program_id(0)

        @pl.when(i == 0)
        def _():
            acc[...] = jnp.zeros_like(acc)

        msv, yv = ms_ref[...], y_ref[...]
        for s in range(N_SHARD):
            cols = slice(256 * s, 256 * (s + 1))
            acc[s, 0:E_VAL, :] += _dot(msv, dzp_ref[:, cols], TN)
            acc[s, E_VAL:E_GATE, :] += _dot(yv, dzv_ref[:, cols], TN)
            acc[s, E_GATE:, :] += _dot(yv, dzg_ref[:, cols], TN)

        @pl.when(i == n_t - 1)
        def _():
            o_ref[...] = acc[...].astype(BF16)

    row = BS((tm, D_MODEL), lambda i: (i, 0))
    full = BS((N_SHARD, 1024, 256), lambda i: (0, 0, 0))
    return pl.pallas_call(
        body, out_shape=SDS((N_SHARD, 1024, 256), BF16), grid=(n_t,),
        in_specs=[BS((tm, D_POOL), lambda i: (i, 0)), BS((tm, D_SSM), lambda i: (i, 0)), row, row, row],
        out_specs=full, scratch_shapes=[pltpu.VMEM((N_SHARD, 1024, 256), F32)],
        name="mixer_dw", compiler_params=_params("arbitrary"))(ms, yssm, dzp, dzv, dzg)


def _mixer_dx(dzp, dzv, dzg, w_e, y_total, tm):
    t = dzp.shape[0]

    def body(dzp_ref, dzv_ref, dzg_ref, w_ref, yt_ref, dms_ref, dy_ref, dyb_ref):
        acc_ms, acc_y = None, None
        for s in range(N_SHARD):
            cols = slice(256 * s, 256 * (s + 1))
            part_ms = _dot(dzp_ref[:, cols], w_ref[s, 0:E_VAL, :], NT)
            part_y = _dot(dzv_ref[:, cols], w_ref[s, E_VAL:E_GATE, :], NT) + _dot(dzg_ref[:, cols], w_ref[s, E_GATE:, :], NT)
            acc_ms = part_ms if s == 0 else acc_ms + part_ms
            acc_y = part_y if s == 0 else acc_y + part_y
        dms_ref[...] = acc_ms.astype(BF16)
        y = yt_ref[...]
        th = jnp.tanh(GELU_C * (y + GELU_K * y * y * y))
        dgelu = 0.5 * (1.0 + th) + 0.5 * y * (1.0 - th * th) * GELU_C * (1.0 + 3.0 * GELU_K * y * y)
        dy = acc_y * dgelu
        dy_ref[...] = dy
        dyb_ref[...] = dy.astype(BF16)

    row = BS((tm, D_MODEL), lambda i: (i, 0))
    narrow = BS((tm, D_SSM), lambda i: (i, 0))
    return pl.pallas_call(
        body, out_shape=(SDS((t, D_POOL), BF16), SDS((t, D_SSM), F32), SDS((t, D_SSM), BF16)), grid=(t // tm,),
        in_specs=[row, row, row, BS((N_SHARD, 1024, 256), lambda i: (0, 0, 0)), narrow],
        out_specs=(BS((tm, D_POOL), lambda i: (i, 0)), narrow, narrow),
        name="mixer_dx", compiler_params=_params("parallel"))(dzp, dzv, dzg, w_e, y_total)


def _attn_probs(q_h, k_h):
    s = _dot(q_h, k_h, NT) * (1.0 / math.sqrt(HEAD_DIM))
    e = jnp.exp(s - jnp.max(s, axis=-1, keepdims=True))
    return e / jnp.sum(e, axis=-1, keepdims=True)


def _attn_fwd(q, kv, tm):
    t = q.shape[0]
    tm = min(2 * tm, t)
    m = kv.shape[0]

    def body(q_ref, kv_ref, o_ref):
        for hd in range(N_HEADS):
            lo = hd * HEAD_DIM
            p = _attn_probs(q_ref[:, lo:lo + HEAD_DIM], kv_ref[:, lo:lo + HEAD_DIM])
            o_ref[:, lo:lo + HEAD_DIM] = _dot(p, kv_ref[:, D_MODEL + lo:D_MODEL + lo + HEAD_DIM]).astype(BF16)

    return pl.pallas_call(
        body, out_shape=SDS((t, D_MODEL), BF16), grid=(t // tm,),
        in_specs=[BS((tm, D_MODEL), lambda i: (i, 0)), BS((m, 2 * D_MODEL), lambda i: (0, 0))],
        out_specs=BS((tm, D_MODEL), lambda i: (i, 0)), name="attn_fwd", compiler_params=_params("parallel"))(q, kv)


def _attn_bwd(q, kv, d_o, tm):
    t = q.shape[0]
    m = kv.shape[0]

    def body(q_ref, kv_ref, do_ref, dq_ref, dkv_ref):
        i = pl.program_id(0)

        @pl.when(i == 0)
        def _():
            dkv_ref[...] = jnp.zeros_like(dkv_ref)

        for hd in range(N_HEADS):
            lo = hd * HEAD_DIM
            q_h = q_ref[:, lo:lo + HEAD_DIM]
            k_h = kv_ref[:, lo:lo + HEAD_DIM]
            v_h = kv_ref[:, D_MODEL + lo:D_MODEL + lo + HEAD_DIM]
            do_h = do_ref[:, lo:lo + HEAD_DIM]
            p = _attn_probs(q_h, k_h)
            dkv_ref[:, D_MODEL + lo:D_MODEL + lo + HEAD_DIM] += _dot(p, do_h, TN)
            dp = _dot(do_h, v_h, NT)
            ds = p * (dp - jnp.sum(dp * p, axis=-1, keepdims=True)) * (1.0 / math.sqrt(HEAD_DIM))
            dq_ref[:, lo:lo + HEAD_DIM] = _dot(ds, k_h).astype(BF16)
            dkv_ref[:, lo:lo + HEAD_DIM] += _dot(ds, q_h, TN)

    row = BS((tm, D_MODEL), lambda i: (i, 0))
    full = BS((m, 2 * D_MODEL), lambda i: (0, 0))
    return pl.pallas_call(
        body, out_shape=(SDS((t, D_MODEL), BF16), SDS((m, 2 * D_MODEL), F32)), grid=(t // tm,),
        in_specs=[row, full, row], out_specs=(row, full), name="attn_bwd",
        compiler_params=_params("arbitrary"))(q, kv, d_o)


TRANSPOSED = ("ffn1_w_gate", "ffn1_w_up", "ffn2_w_gate", "ffn2_w_up", "w_in")
GATHER_PHASES = {"f1a": (("ffn1_w_gate",), ("ffn1_w_up",)),
                 "f1b": (("ffn1_w_down",),),
                 "win": (("w_in",),),
                 "mix": (("w_mix_out",), ("w_q",), ("w_xo",), ("w_kv",), ("w_pool_proj", "w_glu_val", "w_glu_gate")),
                 "f2": (("ffn2_w_gate",), ("ffn2_w_up",), ("ffn2_w_down",))}
REDUCE_GROUPS = (("ffn2_w_gate",), ("ffn2_w_up",), ("ffn2_w_down",), ("w_xo",), ("w_q",), ("w_kv",), ("w_mix_out",),
                 ("w_pool_proj", "w_glu_val", "w_glu_gate"), ("w_in",), ("ffn1_w_gate",), ("ffn1_w_up",), ("ffn1_w_down",))
SMALL = ("ffn1_norm", "mix_norm", "pool_w", "pool_scale", "ssm_a_re", "ssm_a_im", "ssm_log_dt", "ssm_b_re",
         "ssm_b_im", "ssm_c_re", "ssm_c_im", "ssm_d", "xattn_norm", "mem_norm", "ffn2_norm", "final_norm")
WEIGHTS = ("ffn1_norm", "ffn1_w_gate", "ffn1_w_up", "ffn1_w_down", "mix_norm", "w_in", "pool_w", "pool_scale",
           "w_pool_proj", "ssm_a_re", "ssm_a_im", "ssm_log_dt", "ssm_b_re", "ssm_b_im", "ssm_c_re", "ssm_c_im",
           "ssm_d", "w_glu_val", "w_glu_gate", "w_mix_out", "xattn_norm", "mem_norm", "w_q", "w_kv", "w_xo",
           "ffn2_norm", "ffn2_w_gate", "ffn2_w_up", "ffn2_w_down", "final_norm")


def _small_view(a, n):
    return jnp.swapaxes(a, 3, 4) if n in ("ssm_b_re", "ssm_b_im") else a


def _device_step(x, mem, target, wts, sp, reducer=None):
    t = x.shape[0]
    tm = min(TM, t)
    g = {}

    first_gather = wts.start("win", wts.start("f1b", wts.start("f1a")))
    u1 = _rmsnorm("norm_ffn1", x, sp["ffn1_norm"], tm, after=first_gather)

    def per_channel(a):
        a = a.reshape(2 * SSM_GROUPS, 1, -1)
        return jnp.broadcast_to(a, (2 * SSM_GROUPS, SSM_GROUP, a.shape[-1])).reshape(SSM_ROWS, a.shape[-1])

    ssm_a = per_channel(sp["ssm_a_re"]), per_channel(sp["ssm_a_im"]), per_channel(sp["ssm_log_dt"])
    ssm_b = sp["ssm_b_re"].reshape(SSM_ROWS, SSM_STATE), sp["ssm_b_im"].reshape(SSM_ROWS, SSM_STATE)
    abr, abi, w_in_s, w_in_s_t, w_out_s_t, w_out_s = _ssm_prep(
        *ssm_a, *ssm_b, sp["ssm_c_re"].reshape(SSM_ROWS, SSM_STATE), sp["ssm_c_im"].reshape(SSM_ROWS, SSM_STATE),
        after=first_gather)
    first_rows = (2, SSM_GROUPS, SSM_GROUP, SSM_STATE)
    a_r = abr.reshape(first_rows)[:, :, 0].reshape(2, 1, SSM_CH)
    a_i = abi.reshape(first_rows)[:, :, 0].reshape(2, 1, SSM_CH)
    mem_n = _rmsnorm("norm_mem", mem, sp["mem_norm"], mem.shape[0], after=first_gather + wts.sources(("mix", "f2")))

    whole = (D_FF, D_MODEL)
    w_g1, w_u1 = wts.finish("f1a", [u1, w_in_s, w_in_s_t, w_out_s, w_out_s_t, a_r, a_i, mem_n])
    w_f1 = {"gate": w_g1.reshape(whole), "up": w_u1.reshape(whole)}
    g1, up1, a1 = _ffn_up("ffn1_up", u1, w_f1, tm)
    (w_dn,) = wts.finish("f1b", [a1])
    w_f1["down"] = w_dn.reshape(whole)
    h1, u2 = _ffn_down("ffn1_down", a1, w_f1, x, tm, next_gain=sp["mix_norm"])

    (w_in_g,) = wts.finish("win", [u2])
    w_in_t = w_in_g.reshape(D_FF, D_MODEL)
    proj, s_in = _mix_in(u2, w_in_t, tm, after=wts.start("f2", wts.start("mix", [w_in_g])))
    pooled, mixed, ms = _pool_fwd(proj, sp["pool_w"][0], sp["pool_scale"])

    states, y_dirs = [], []
    for dr in range(2):
        st, yd = _ssm_scan(f"ssm_scan_fwd{dr}", s_in, w_in_s, a_r, a_i, w_out_s, reverse=(dr == 1), dr=dr)
        states.append(st)
        y_dirs.append(yd)
    y_total, yssm = _ssm_combine(proj, y_dirs[0], y_dirs[1], sp["ssm_d"], tm, after=wts.fill_start("mix", y_dirs))
    *w_sq, w_kv, w_e = wts.finish("mix", [yssm, ms])
    w_mo, w_q, w_xo = (a.reshape(D_MODEL, D_MODEL) for a in w_sq)
    w_d = w_kv[:, None]

    merged = _mixer_merge(ms, yssm, w_e, proj, tm)
    h2, u3 = _mm_resid_norm("mix_out", merged, w_mo, h1, sp["xattn_norm"], tm)

    q = _plain_mm("attn_q", u3, w_q, NN, BF16, tm)
    n_mem = mem.shape[0]
    kv = _mm("attn_kv", [(mem_n, BS((n_mem, D_MODEL), lambda s: (0, 0)), w_d, BS((None, None, D_MODEL, 512), lambda s: (s, 0, 0, 0)), NN)],
             grid=(N_SHARD,), out_shape=SDS((n_mem, 2 * D_MODEL), BF16), out_spec=BS((n_mem, 512), lambda s: (0, s)))
    o = _attn_fwd(q, kv, tm)
    h3, u4 = _mm_resid_norm("attn_out", o, w_xo, h2, sp["ffn2_norm"], tm)

    w_f2 = dict(zip(("gate", "up", "down"), (a.reshape(whole) for a in wts.finish("f2", [u4]))))
    g2, up2, a2 = _ffn_up("ffn2_up", u4, w_f2, tm)
    loss, dh4, dh4_b, g["final_norm"] = _ffn_down("ffn2_down", a2, w_f2, h3, tm,
                                                  head=(sp["final_norm"].reshape(1, D_MODEL), target))

    dg2, dup2 = _ffn_bwd_act("ffn2_bwd_act", dh4_b, w_f2, g2, up2, tm)
    dw_f2 = _ffn_dw("ffn2_dw", u4, dg2, dup2, a2, dh4_b, tm)
    dh3, dh3_b, g["ffn2_norm"] = _ffn_dx("ffn2_dx", dg2, dup2, w_f2, h3, sp["ffn2_norm"], dh4, tm)

    d_o = _plain_mm("attn_out_dx", dh3_b, w_xo, NT, BF16, tm)
    dw_xo = _dw_mm("attn_out_dw", o, dh3_b, tm)
    dq, dkv = _attn_bwd(q, kv, d_o, tm)
    dw_q = _dw_mm("attn_q_dw", u3, dq, tm)
    dh2, dh2_b, g["xattn_norm"] = _mm_norm_bwd("attn_q_dx", dq, w_q, NT, h2, sp["xattn_norm"], dh3, tm)
    dw_kv = _mm("attn_kv_dw", [(mem_n, BS((n_mem, D_MODEL), lambda s: (0, 0)), dkv, BS((n_mem, 512), lambda s: (0, s)), TN)],
                grid=(N_SHARD,), out_shape=SDS((N_SHARD, D_MODEL, 512), BF16), out_spec=BS((None, D_MODEL, 512), lambda s: (s, 0, 0)))
    dmem_n = _mm("attn_kv_dx", [(dkv, BS((n_mem, 512), lambda s: (0, s)), w_d, BS((None, None, D_MODEL, 512), lambda s: (s, 0, 0, 0)), NT)],
                 grid=(N_SHARD,), red_axis=0, out_shape=SDS((n_mem, D_MODEL), F32), out_spec=BS((n_mem, D_MODEL), lambda s: (0, 0)))
    _, _, g["mem_norm"] = _rmsnorm_bwd("norm_mem_bwd", mem, sp["mem_norm"], dmem_n, None, n_mem)

    square = (N_SHARD, D_MODEL // N_SHARD, D_MODEL)
    sharded = (N_SHARD, FF_SH, D_MODEL)
    early = [a.reshape(sharded) for a in dw_f2] + [dw_xo.reshape(square), dw_q.reshape(square), dw_kv]
    swapping = reducer.swap_start("a1", early) if reducer is not None else []
    dmerged = _plain_mm("mix_out_dx", dh2_b, w_mo, NT, BF16, tm, after=swapping)
    dw_mo = _dw_mm("mix_out_dw", merged, dh2_b, tm)
    d_gp, d_gs, dzp, dzv, dzg = _mixer_merge_bwd(ms, yssm, w_e, proj, dmerged, tm)
    dw_e = _mixer_dw(ms, yssm, dzp, dzv, dzg, tm)
    d_ms, d_yt, d_yt_b = _mixer_dx(dzp, dzv, dzg, w_e, y_total, tm)
    dp, d_scale, d_pw = _pool_bwd(d_ms, mixed, pooled, sp["pool_w"][0], sp["pool_scale"])
    g["pool_scale"] = d_scale
    g["pool_w"] = d_pw[None]

    du_dirs, lams = [], []
    for dr in range(2):
        lam, du = _ssm_scan(f"ssm_scan_bwd{dr}", d_yt_b, w_out_s_t, a_r, a_i, w_in_s_t, reverse=(dr == 0), dr=dr, conj=True)
        du_dirs.append(du)
        lams.append(lam)
    ds, g["ssm_d"] = _ssm_ds(proj, d_yt, du_dirs[0], du_dirs[1], sp["ssm_d"], tm)

    d_proj = [dp, ds, d_gp, d_gs]
    dw_in_t = _mix_in_dw(d_proj, u2, tm)
    dh1, dh1_b, g["mix_norm"] = _mm_norm_bwd("mix_in_dx", d_proj, w_in_t, NN, h1, sp["mix_norm"], dh2, tm)

    early += [dw_mo.reshape(square), dw_e, dw_in_t.reshape(sharded)]
    g["final_norm"] = g["final_norm"].reshape(D_MODEL)

    travelling = reducer.start("a", early[6:], swapped=["a1"], after=list(g.values())) if reducer is not None else []
    d_abr, d_abi, d_cm, d_bm = [], [], [], []
    for dr in range(2):
        da_r, da_i, d_win, d_woutt = _ssm_param_grads(f"ssm_param_grads{dr}", lams[dr], states[dr], s_in, d_yt_b,
                                                      reverse=(dr == 1), after=travelling)
        d_abr.append(da_r)
        d_abi.append(da_i)
        d_bm.append(d_win)
        d_cm.append(d_woutt)

    d_ar, d_ai, d_ldt, d_br, d_bi, d_cr, d_ci = _ssm_prep_bwd(*ssm_a, *ssm_b, d_abr, d_abi, d_bm, d_cm)
    per_group = (2 * SSM_GROUPS, SSM_GROUP * SSM_STATE)
    g["ssm_a_re"] = d_ar.reshape(2 * SSM_GROUPS, SSM_GROUP, SSM_STATE).sum(axis=1).reshape(sp["ssm_a_re"].shape)
    g["ssm_a_im"] = d_ai.reshape(2 * SSM_GROUPS, SSM_GROUP, SSM_STATE).sum(axis=1).reshape(sp["ssm_a_im"].shape)
    g["ssm_log_dt"] = d_ldt.reshape(per_group).sum(axis=1).reshape(sp["ssm_log_dt"].shape)
    g["ssm_b_re"] = d_br.reshape(sp["ssm_b_re"].shape)
    g["ssm_b_im"] = d_bi.reshape(sp["ssm_b_im"].shape)
    g["ssm_c_re"] = d_cr.reshape(sp["ssm_c_re"].shape)
    g["ssm_c_im"] = d_ci.reshape(sp["ssm_c_im"].shape)
    if reducer is not None:
        travelling = travelling + [d_ar, d_br, d_cr]
    dg1, dup1 = _ffn_bwd_act("ffn1_bwd_act", dh1_b, w_f1, g1, up1, tm, after=travelling)
    dw_f1 = [a.reshape(sharded) for a in _ffn_dw("ffn1_dw", u1, dg1, dup1, a1, dh1_b, tm)]
    if reducer is not None:
        joining = reducer.join_start("a", after=reducer.finish("a", reducer.swap_start("b1", dw_f1)))
        travelling = joining + reducer.start("b", [], swapped=["b1"], after=joining)
    grad_x, _, g["ffn1_norm"] = _ffn_dx("ffn1_dx", dg1, dup1, w_f1, x, sp["ffn1_norm"], dh1, tm, after=travelling)
    if reducer is not None:
        reducer.join_finish("a", [grad_x])
    return loss, grad_x, early + dw_f1, g


def _mesh_place():
    x, y, c = lax.axis_index("x"), lax.axis_index("y"), lax.axis_index("c")
    chips = [(1 - x, y), (x, 1 - y), (1 - x, 1 - y)]
    return x, y, c, chips


def _remote(src, dst, send_sems, recv_sems, k, to):
    return pltpu.make_async_remote_copy(src_ref=src, dst_ref=dst, send_sem=send_sems.at[k], recv_sem=recv_sems.at[k],
                                        device_id=to, device_id_type=MESH)


def _sibling_swap_halves(tag, grads, after=()):
    n = len(grads)
    after_ops, after_specs = _after_operands(after)

    def body(*refs):
        ins, outs = refs[:n], refs[n + len(after_ops):2 * n + len(after_ops)]
        send_sems, recv_sems = refs[2 * n + len(after_ops):]
        x, y, c, _ = _mesh_place()
        sibling = (x, y, 1 - c)
        copies = []
        for k in range(n):
            half = grads[k].shape[1] // 2
            theirs = pl.ds(pl.multiple_of((1 - c) * half, 16), half)
            cp = _remote(ins[k].at[:, theirs, :], outs[k], send_sems, recv_sems, k, sibling)
            cp.start()
            copies.append(cp)
        for cp in copies:
            cp.wait_recv()
        for cp in copies:
            cp.wait_send()

    hbm = BS(memory_space=pl.ANY)
    return pl.pallas_call(
        body, out_shape=tuple(SDS((g.shape[0], g.shape[1] // 2, g.shape[2]), g.dtype) for g in grads),
        in_specs=[hbm] * n + after_specs, out_specs=(hbm,) * n,
        scratch_shapes=[pltpu.SemaphoreType.DMA((n,)), pltpu.SemaphoreType.DMA((n,))],
        name="reduce_sibling_send_" + tag, compiler_params=_params())(*grads, *after_ops)


def _row_tile(rows, cap=512):
    return max(r for r in range(16, cap + 1, 16) if rows % r == 0)


REDUCE_STEPS = 2


def _chip_presum(tag, grads, gots, c_idx):
    n = len(grads)
    halves = [g.shape[1] // 2 for g in grads]
    tiles = [(h // REDUCE_STEPS, g.shape[2]) for h, g in zip(halves, grads)]

    def body(c_ref, *refs):
        for k in range(n):
            refs[2 * n + k][...] = (refs[k][...].astype(F32) + refs[n + k][...].astype(F32)).astype(BF16)

    mine = [BS((None, None) + tile, lambda s, i, c_ref: (s, c_ref[0], i, 0)) for tile in tiles]
    plain = [BS((None,) + tile, lambda s, i, c_ref: (s, i, 0)) for tile in tiles]
    return list(pl.pallas_call(
        body, out_shape=tuple(SDS((g.shape[0], h, g.shape[2]), BF16) for g, h in zip(grads, halves)),
        grid_spec=pltpu.PrefetchScalarGridSpec(num_scalar_prefetch=1, grid=(N_SHARD, REDUCE_STEPS),
                                               in_specs=mine + plain, out_specs=plain),
        name="reduce_presum_" + tag, compiler_params=_params("parallel", "parallel"))(
            c_idx, *[g.reshape(g.shape[0], 2, h, g.shape[2]) for g, h in zip(grads, halves)], *gots))


HBM_SPEC = BS(memory_space=pltpu.HBM)
SEM_SPEC = BS(memory_space=pltpu.SEMAPHORE)
DATAFLOW = pltpu.SideEffectType.DATAFLOW_SIDE_EFFECTING


def _chip_exchange_copies(parts, lands, send_sems, recv_sems):
    _, _, c, chips = _mesh_place()
    return [_remote(parts[k].at[2 * px + py], lands[k].at[j], send_sems, recv_sems, 3 * k + j, (px, py, c))
            for k in range(len(parts)) for j, (px, py) in enumerate(chips)]


def _gather_copies(shards, lands, send_sems, recv_sems):
    x, y, c, chips = _mesh_place()
    return [_remote(shards[k], lands[k].at[2 * x + y], send_sems, recv_sems, 3 * k + j, (px, py, c))
            for k in range(len(shards)) for j, (px, py) in enumerate(chips)]


def _gather_half_copies(shards, lands, send_sems, recv_sems):
    x, y, c, chips = _mesh_place()
    out = []
    for k in range(len(shards)):
        half = shards[k].shape[0] // 2
        mine = pl.ds(pl.multiple_of(c * half, 16), half)
        for j, (px, py) in enumerate(chips):
            out.append(_remote(shards[k].at[mine, :], lands[k].at[2 * x + y, mine, :], send_sems, recv_sems,
                               3 * k + j, (px, py, c)))
    return out


def _fill_copies(zones, _, send_sems, recv_sems):
    x, y, c, chips = _mesh_place()
    out = []
    for k in range(len(zones)):
        half = zones[k].shape[1] // 2
        mine = pl.ds(pl.multiple_of(c * half, 16), half)
        for j, (px, py) in enumerate(chips):
            blk = zones[k].at[2 * px + py, mine, :]
            out.append(_remote(blk, blk, send_sems, recv_sems, 3 * k + j, (x, y, 1 - c)))
    return out


def _sibling_fill(tag, lands):
    n = len(lands)

    def body(*refs):
        outs = refs[n:2 * n]
        copies = _fill_copies(outs, outs, *refs[2 * n:])
        for cp in copies:
            cp.start()
        for cp in copies:
            cp.wait_recv()
        for cp in copies:
            cp.wait_send()

    hbm = BS(memory_space=pl.ANY)
    return list(pl.pallas_call(
        body, out_shape=tuple(SDS(a.shape, a.dtype) for a in lands),
        in_specs=[hbm] * n, out_specs=(hbm,) * n, input_output_aliases={k: k for k in range(n)},
        scratch_shapes=[pltpu.SemaphoreType.DMA((3 * n,)), pltpu.SemaphoreType.DMA((3 * n,))],
        name="gather_fill_" + tag, compiler_params=_params())(*lands))


def _swap_copies(grads, lands, send_sems, recv_sems):
    x, y, c, _ = _mesh_place()
    out = []
    for k in range(len(grads)):
        half = grads[k].shape[1] // 2
        theirs = pl.ds(pl.multiple_of((1 - c) * half, 16), half)
        out.append(_remote(grads[k].at[:, theirs, :], lands[k], send_sems, recv_sems, k, (x, y, 1 - c)))
    return out


def _join_copies(fulls, same, send_sems, recv_sems):
    x, y, c, _ = _mesh_place()
    out = []
    for k in range(len(fulls)):
        half = fulls[k].shape[0] // 2
        mine = fulls[k].at[pl.ds(pl.multiple_of(c * half, 8), half), :]
        out.append(_remote(mine, mine, send_sems, recv_sems, k, (x, y, 1 - c)))
    return out


def _everyone_copies(packs, lands, send_sems, recv_sems):
    x, y, c, _ = _mesh_place()
    out = []
    for k in range(len(packs)):
        for j in range(N_DEV - 1):
            bx, by, bc = (j + 1) >> 2 & 1, (j + 1) >> 1 & 1, (j + 1) & 1
            peer = (x ^ bx, y ^ by, c ^ bc)
            out.append(_remote(packs[k], lands[k].at[4 * x + 2 * y + c], send_sems, recv_sems, (N_DEV - 1) * k + j, peer))
    return out


def _split_start(name, copies, sources, land_shapes, after=(), fanout=3):
    n = len(sources)
    n_land = len(land_shapes)
    m = n + n_land
    n_sems = fanout * n
    after_ops, after_specs = _after_operands(after)

    def body(*refs):
        ins = refs[:n]
        lands = refs[n:m] if n_land else ins
        send_sems, recv_sems = refs[m + len(after_ops)], refs[m + len(after_ops) + 1]
        token = refs[-1]
        for cp in copies(ins, lands, send_sems, recv_sems):
            cp.start()
        token[...] = jnp.zeros_like(token)

    lands = [pltpu.with_memory_space_constraint(lax.empty(s, d), pltpu.HBM) for s, d in land_shapes]
    sources = [pltpu.with_memory_space_constraint(p, pltpu.HBM) for p in sources]
    thru = [pltpu.HBM(a.shape, a.dtype) for a in sources + lands]
    out = pl.pallas_call(
        body, name=name,
        out_shape=(pltpu.SemaphoreType.DMA((n_sems,)), pltpu.SemaphoreType.DMA((n_sems,)), *thru, SDS((8, 128), F32)),
        in_specs=[HBM_SPEC] * m + after_specs,
        out_specs=(SEM_SPEC, SEM_SPEC, *[HBM_SPEC] * m, BS(memory_space=pltpu.VMEM)),
        input_output_aliases={i: 2 + i for i in range(m)},
        compiler_params=pltpu.CompilerParams(has_side_effects=DATAFLOW))(*sources, *lands, *after_ops)
    return out[0], out[1], list(out[2:2 + n]), list(out[2 + n:2 + m]), out[-1]


def _split_wait(name, copies, send_sems, recv_sems, sources, lands, after):
    n = len(sources)
    m = n + len(lands)
    after_ops, after_specs = _after_operands(after)

    def body(*refs):
        ins = refs[:n]
        zones = refs[n:m] if m > n else ins
        for cp in copies(ins, zones, refs[m], refs[m + 1]):
            cp.wait_send()
            cp.wait_recv()

    out = pl.pallas_call(
        body, name=name,
        out_shape=tuple(pltpu.HBM(a.shape, a.dtype) for a in sources + lands),
        in_specs=[HBM_SPEC] * m + [SEM_SPEC, SEM_SPEC] + after_specs, out_specs=(HBM_SPEC,) * m,
        input_output_aliases={i: i for i in range(m)},
        compiler_params=pltpu.CompilerParams(has_side_effects=DATAFLOW))(*sources, *lands, send_sems, recv_sems, *after_ops)
    return list(out[:n]), list(out[n:])


class _WeightGatherer:
    def __init__(self, shards):
        self.shards, self.open, self.filling = shards, {}, {}
        self.me = 2 * lax.axis_index("x") + lax.axis_index("y")

    HALVED = ("f1a", "win", "mix")

    def start(self, tag, after=()):
        shapes = [((N_SHARD,) + s.shape, s.dtype) for s in self.shards[tag]]
        copies = _gather_half_copies if tag in self.HALVED else _gather_copies
        self.open[tag] = _split_start("gather_start_" + tag, copies, self.shards[tag], shapes, after)
        return [self.open[tag][-1]]

    def sources(self, tags):
        return [s for tag in tags for s in self.shards[tag]]

    def fill_start(self, tag, after):
        send_sems, recv_sems, shards, lands, _ = self.open.pop(tag)
        shards, lands = _split_wait("gather_wait_" + tag, _gather_half_copies, send_sems, recv_sems, shards, lands, after)
        self.filling[tag] = shards, _split_start("gather_fill_start_" + tag, _fill_copies, lands, [])
        return [self.filling[tag][1][-1]]

    def finish(self, tag, after):
        if tag in self.filling:
            shards, (send_sems, recv_sems, lands, _, _) = self.filling.pop(tag)
            lands, _ = _split_wait("gather_fill_wait_" + tag, _fill_copies, send_sems, recv_sems, lands, [], after)
            return [lax.dynamic_update_slice(zone, s[None], (self.me, 0, 0)) for zone, s in zip(lands, shards)]
        send_sems, recv_sems, shards, lands, _ = self.open.pop(tag)
        copies = _gather_half_copies if tag in self.HALVED else _gather_copies
        shards, lands = _split_wait("gather_wait_" + tag, copies, send_sems, recv_sems, shards, lands, after)
        if tag in self.HALVED:
            lands = _sibling_fill(tag, lands)
        return [lax.dynamic_update_slice(zone, s[None], (self.me, 0, 0)) for zone, s in zip(lands, shards)]


class _GradReducer:
    def __init__(self):
        self.c_idx = lax.axis_index("c").astype(jnp.int32).reshape(1)
        self.place = jnp.stack([2 * lax.axis_index("x") + lax.axis_index("y"), lax.axis_index("c")]).astype(jnp.int32)
        self.swaps, self.open, self.landed, self.joins, self.reduced = {}, {}, {}, {}, []

    def swap_start(self, tag, grads, after=()):
        shapes = [((g.shape[0], g.shape[1] // 2, g.shape[2]), g.dtype) for g in grads]
        self.swaps[tag] = _split_start("reduce_swap_start_" + tag, _swap_copies, grads, shapes, after, fanout=1)
        return [self.swaps[tag][-1]]

    def start(self, tag, grads, after=(), swapped=()):
        pairs = []
        for s in swapped:
            send_sems, recv_sems, early, lands, _ = self.swaps.pop(s)
            behind = grads[-1:] or list(after)
            pairs += zip(*_split_wait("reduce_swap_wait_" + s, _swap_copies, send_sems, recv_sems, early, lands, behind))
        if grads:
            pairs += zip(grads, _sibling_swap_halves(tag, grads, after))
        parts = _chip_presum(tag, [g for g, _ in pairs], [s for _, s in pairs], self.c_idx)
        shapes = [((3,) + p.shape[1:], p.dtype) for p in parts]
        self.open[tag] = _split_start("reduce_exchange_start_" + tag, _chip_exchange_copies, parts, shapes)
        return [self.open[tag][-1]]

    def finish(self, tag, after):
        send_sems, recv_sems, parts, lands, _ = self.open.pop(tag)
        self.landed[tag] = _split_wait("reduce_exchange_wait_" + tag, _chip_exchange_copies, send_sems, recv_sems, parts, lands, after)
        return self.landed[tag][1][:1]

    def _sums(self, tag, after=()):
        parts, landed = self.landed.pop(tag)
        return _chip_sum(tag, parts, landed, self.place, after)

    def join_start(self, tag, after=()):
        self.joins[tag] = _split_start("reduce_join_start_" + tag, _join_copies, self._sums(tag, after), [], fanout=1)
        return [self.joins[tag][-1]]

    def join_finish(self, tag, after):
        send_sems, recv_sems, fulls, _, _ = self.joins.pop(tag)
        self.reduced += _split_wait("reduce_join_wait_" + tag, _join_copies, send_sems, recv_sems, fulls, [], after)[0]

    def join(self, tag, after=()):
        self.reduced += _sibling_join_halves(self._sums(tag), after)


def _chip_sum(tag, parts, gots, place, after=()):
    n = len(parts)
    tiles = [(p.shape[1] // REDUCE_STEPS, p.shape[2]) for p in parts]
    after_ops, after_specs = _after_operands(after)

    def body(place_ref, *refs):
        outs = refs[2 * n + len(after_ops):]
        for k in range(n):
            acc = refs[k][...].astype(F32)
            for j in range(3):
                acc = acc + refs[n + k][j].astype(F32)
            outs[k][...] = acc

    return list(pl.pallas_call(
        body, out_shape=tuple(SDS((2 * p.shape[1], p.shape[2]), F32) for p in parts),
        grid_spec=pltpu.PrefetchScalarGridSpec(
            num_scalar_prefetch=1, grid=(REDUCE_STEPS,),
            in_specs=[BS((None,) + tile, lambda i, place_ref: (place_ref[0], i, 0)) for tile in tiles]
            + [BS((3,) + tile, lambda i, place_ref: (0, i, 0)) for tile in tiles] + after_specs,
            out_specs=[BS(tile, lambda i, place_ref: (place_ref[1] * REDUCE_STEPS + i, 0)) for tile in tiles]),
        name="reduce_sum_" + tag, compiler_params=_params("parallel"))(place, *parts, *gots, *after_ops))


def _sibling_join_halves(fulls, after=()):
    n = len(fulls)
    after_ops, after_specs = _after_operands(after)

    def body(*refs):
        outs = refs[n + len(after_ops):2 * n + len(after_ops)]
        send_sems, recv_sems = refs[2 * n + len(after_ops):]
        copies = _join_copies(outs, outs, send_sems, recv_sems)
        for cp in copies:
            cp.start()
        for cp in copies:
            cp.wait_recv()
        for cp in copies:
            cp.wait_send()

    hbm = BS(memory_space=pl.ANY)
    return list(pl.pallas_call(
        body, out_shape=tuple(SDS(f.shape, f.dtype) for f in fulls),
        in_specs=[hbm] * n + after_specs, out_specs=(hbm,) * n, input_output_aliases={k: k for k in range(n)},
        scratch_shapes=[pltpu.SemaphoreType.DMA((n,)), pltpu.SemaphoreType.DMA((n,))],
        name="reduce_sibling_join", compiler_params=_params())(*fulls, *after_ops))


N_DEV = 8


def _sum_devices(packs):
    _, rows, lanes = packs.shape

    def body(p_ref, o_ref):
        acc = p_ref[0]
        for dev in range(1, N_DEV):
            acc = acc + p_ref[dev]
        o_ref[...] = acc

    vm = BS(memory_space=pltpu.VMEM)
    return pl.pallas_call(body, out_shape=SDS((rows, lanes), F32), in_specs=[vm], out_specs=vm,
                          name="small_sum", compiler_params=_params())(packs)


def _adamw_refs(w_ref, g_ref, m_ref, v_ref, go_ref, d_ref, mo_ref, vo_ref):
    bc1 = 1.0 - ADAM_B1 ** ADAM_STEP
    bc2 = 1.0 - ADAM_B2 ** ADAM_STEP
    g = g_ref[...]
    m_new = ADAM_B1 * m_ref[...] + (1.0 - ADAM_B1) * g
    v_new = ADAM_B2 * v_ref[...] + (1.0 - ADAM_B2) * (g * g)
    go_ref[...] = g
    mo_ref[...] = m_new
    vo_ref[...] = v_new
    d_ref[...] = -ADAM_LR * ((m_new / bc1) / (jnp.sqrt(v_new / bc2) + ADAM_EPS) + ADAM_WD * w_ref[...])


def _adamw_small(ws, gs, ms, vs):
    n = len(ws)

    def body(*refs):
        for k in range(n):
            _adamw_refs(*[refs[j * n + k] for j in range(4)], *refs[4 * n + 4 * k:4 * n + 4 * k + 4])

    vm = BS(memory_space=pltpu.VMEM)
    outs = pl.pallas_call(
        body, out_shape=tuple(SDS(a.shape, F32) for a in ws for _ in range(4)), in_specs=[vm] * (4 * n),
        out_specs=(vm,) * (4 * n), name="adamw_small", compiler_params=_params())(*ws, *gs, *ms, *vs)
    return [outs[4 * k:4 * k + 4] for k in range(n)]


def _adamw(name, w, grad, row0, m, v, after=()):
    rows, cols = w.shape
    tr = rows if rows < 16 else _row_tile(rows, 352)
    after_ops, after_specs = _after_operands(after)

    def body(w_ref, g_ref, m_ref, v_ref, *rest):
        _adamw_refs(w_ref, g_ref, m_ref, v_ref, *rest[len(after_ops):])

    blk = BS((tr, cols), lambda i: (i, 0))
    shape = SDS((rows, cols), F32)
    return pl.pallas_call(
        body, out_shape=(shape,) * 4, grid=(rows // tr,),
        in_specs=[blk, BS((tr, cols), lambda i: (row0 // tr + i, 0)), blk, blk] + after_specs, out_specs=(blk,) * 4,
        name=name, compiler_params=_params("parallel"))(w, grad, m, v, *after_ops)


SMALL_LANES = 128


SMALL_TILE = 8 * SMALL_LANES


def _packed_rows(p):
    return -(-p.size // SMALL_TILE) * 8


def _pack_small(parts):
    tiles = [jnp.pad(jnp.ravel(p), (0, _packed_rows(p) * SMALL_LANES - p.size)).reshape(-1, SMALL_LANES) for p in parts]
    rows = sum(t.shape[0] for t in tiles)
    return jnp.concatenate(tiles + [jnp.zeros((-rows % 64, SMALL_LANES), F32)], axis=0)


def _unpack_small(packed, like):
    out, at = [], 0
    for p in like:
        rows = _packed_rows(p)
        out.append(jnp.ravel(packed[at:at + rows])[:p.size].reshape(p.shape))
        at += rows
    return out


def kernel(x, mem, ffn1_norm, ffn1_w_gate, ffn1_w_up, ffn1_w_down, mix_norm, w_in, pool_w, pool_scale, w_pool_proj, ssm_a_re, ssm_a_im, ssm_log_dt, ssm_b_re, ssm_b_im, ssm_c_re, ssm_c_im, ssm_d, w_glu_val, w_glu_gate, w_mix_out, xattn_norm, mem_norm, w_q, w_kv, w_xo, ffn2_norm, ffn2_w_gate, ffn2_w_up, ffn2_w_down, final_norm, loss_target, m_ffn1_norm, m_ffn1_w_gate, m_ffn1_w_up, m_ffn1_w_down, m_mix_norm, m_w_in, m_pool_w, m_pool_scale, m_w_pool_proj, m_ssm_a_re, m_ssm_a_im, m_ssm_log_dt, m_ssm_b_re, m_ssm_b_im, m_ssm_c_re, m_ssm_c_im, m_ssm_d, m_w_glu_val, m_w_glu_gate, m_w_mix_out, m_xattn_norm, m_mem_norm, m_w_q, m_w_kv, m_w_xo, m_ffn2_norm, m_ffn2_w_gate, m_ffn2_w_up, m_ffn2_w_down, m_final_norm, v_ffn1_norm, v_ffn1_w_gate, v_ffn1_w_up, v_ffn1_w_down, v_mix_norm, v_w_in, v_pool_w, v_pool_scale, v_w_pool_proj, v_ssm_a_re, v_ssm_a_im, v_ssm_log_dt, v_ssm_b_re, v_ssm_b_im, v_ssm_c_re, v_ssm_c_im, v_ssm_d, v_w_glu_val, v_w_glu_gate, v_w_mix_out, v_xattn_norm, v_mem_norm, v_w_q, v_w_kv, v_w_xo, v_ffn2_norm, v_ffn2_w_gate, v_ffn2_w_up, v_ffn2_w_down, v_final_norm):
    given = dict(locals())
    w = {n: given[n] for n in WEIGHTS}
    m = {n: given["m_" + n] for n in WEIGHTS}
    v = {n: given["v_" + n] for n in WEIGHTS}

    def shard_view(a, n):
        return a[0].T if n in TRANSPOSED else a[0]

    def shard_unview(a, n):
        return (a.T if n in TRANSPOSED else a)[None]

    shards = {tag: [jnp.concatenate([shard_view(w[n], n).astype(BF16) for n in grp], axis=0) for grp in arrays]
              for tag, arrays in GATHER_PHASES.items()}
    reducer = _GradReducer()
    ws, ms, vs = ({n: _small_view(a[n], n) for n in SMALL} for a in (w, m, v))
    loss_part, grad_x, _, small = _device_step(x[0], mem[0], loss_target[0], _WeightGatherer(shards), ws, reducer)

    small_like = [ws[n] for n in SMALL] + [loss_part[0, :1]]
    pack = _pack_small([small[n] for n in SMALL] + [loss_part[0, :1]])
    everyone = _split_start("small_start", _everyone_copies, [pack], [((N_DEV,) + pack.shape, F32)], fanout=N_DEV - 1)

    grads, delta, new_m, new_v = {}, {}, {}, {}
    big_done = []

    def update(groups, reduced):
        for grp, red in zip(groups, reduced):
            row0 = 0
            for n in grp:
                w_n = shard_view(w[n], n)
                outs = _adamw("adamw_" + n, w_n, red, row0, shard_view(m[n], n), shard_view(v[n], n), after=everyone[-1:])
                grads[n], delta[n], new_m[n], new_v[n] = (shard_unview(o, n) for o in outs)
                big_done.append(outs[1])
                row0 += w_n.shape[0]

    n_a = len(reducer.reduced)
    update(REDUCE_GROUPS[:n_a], reducer.reduced)
    reducer.finish("b", list(big_done))
    reducer.join("b")
    update(REDUCE_GROUPS[n_a:], reducer.reduced[n_a:])

    send_sems, recv_sems, packs, landed, _ = everyone
    packs, landed = _split_wait("small_wait", _everyone_copies, send_sems, recv_sems, packs, landed, big_done)
    mine = 4 * lax.axis_index("x") + 2 * lax.axis_index("y") + lax.axis_index("c")
    summed = _sum_devices(lax.dynamic_update_slice(landed[0], packs[0][None], (mine, 0, 0)))
    g_small = dict(zip(SMALL + ("loss",), _unpack_small(summed, small_like)))
    loss = g_small.pop("loss").reshape(())
    def two_d(a):
        return a.reshape(-1, a.shape[-1])

    updated = _adamw_small(*([two_d(a[n]) for n in SMALL] for a in (ws, g_small, ms, vs)))
    for n, outs in zip(SMALL, updated):
        grads[n], delta[n], new_m[n], new_v[n] = (_small_view(o.reshape(ws[n].shape), n) for o in outs)

    return (loss, grad_x[None], *[grads[n] for n in WEIGHTS], *[delta[n] for n in WEIGHTS],
            *[new_m[n] for n in WEIGHTS], *[new_v[n] for n in WEIGHTS])
```

```python
import functools
import math

import jax
import jax.numpy as jnp
from jax import lax
from jax.experimental import pallas as pl
from jax.experimental.pallas import tpu as pltpu

F32 = jnp.float32
BF16 = jnp.bfloat16
SDS = jax.ShapeDtypeStruct
BS = pl.BlockSpec
MESH = pl.DeviceIdType.MESH

D_MODEL = 1024
D_FF = 2816
N_SHARD = 4
FF_SH = D_FF // N_SHARD
D_POOL = 512
POOL_WINDOWS = (2, 4, 8, 16)
POOL_GROUP = 128
D_SSM = 256
SSM_GROUPS = 16
SSM_GROUP = 16
SSM_STATE = 64
SSM_CH = SSM_GROUPS * SSM_STATE
N_HEADS = 4
HEAD_DIM = 256
EPS = 1e-6
ADAM_LR, ADAM_B1, ADAM_B2, ADAM_EPS, ADAM_WD, ADAM_STEP = 0.001, 0.9, 0.999, 1e-08, 0.01, 10

VMEM_LIMIT_V7X = 58 * 1024 * 1024
TM = 512

NN = (((1,), (0,)), ((), ()))
NT = (((1,), (1,)), ((), ()))
TN = (((0,), (0,)), ((), ()))


def _params(*sem):
    return pltpu.CompilerParams(dimension_semantics=sem if sem else None, vmem_limit_bytes=VMEM_LIMIT_V7X)


def _dot(a, b, dims=NN):
    return lax.dot_general(a.astype(BF16), b.astype(BF16), dims, preferred_element_type=F32)


def _sigmoid(v):
    return pl.reciprocal(1.0 + jnp.exp(-v), approx=True)


def _block_dims(spec):
    return tuple(d for d in spec.block_shape if d is not None)


def _after_operands(after):
    return list(after), [BS(memory_space=pl.ANY)] * len(after)


def _mm(name, pairs, *, grid, out_shape, out_spec, red_axis=None, extras=(), epilogue=None, after=()):
    n_pairs, n_extra = len(pairs), len(extras)
    n_red = grid[red_axis] if red_axis is not None else 1
    dims = [p[4] for p in pairs]

    def body(*refs):
        ab = refs[:2 * n_pairs]
        ex = refs[2 * n_pairs:2 * n_pairs + n_extra]
        o_ref = refs[2 * n_pairs + n_extra + len(after)]

        def partial():
            acc = None
            for p in range(n_pairs):
                t = _dot(ab[2 * p][...], ab[2 * p + 1][...], dims[p])
                acc = t if acc is None else acc + t
            return acc

        def finish(acc):
            res = epilogue(acc, *[e[...] for e in ex]) if epilogue is not None else acc
            o_ref[...] = res.astype(o_ref.dtype)

        if n_red == 1:
            finish(partial())
        else:
            acc_ref = refs[-1]
            k = pl.program_id(red_axis)

            @pl.when(k == 0)
            def _():
                acc_ref[...] = jnp.zeros_like(acc_ref)

            acc_ref[...] += partial()

            @pl.when(k == n_red - 1)
            def _():
                finish(acc_ref[...])

    operands, in_specs = [], []
    for a, a_spec, b, b_spec, _ in pairs:
        operands += [a, b]
        in_specs += [a_spec, b_spec]
    for e, e_spec in extras:
        operands.append(e)
        in_specs.append(e_spec)
    after_ops, after_specs = _after_operands(after)
    operands += after_ops
    in_specs += after_specs
    scratch = [pltpu.VMEM(_block_dims(out_spec), F32)] if n_red > 1 else []
    sem = tuple("arbitrary" if ax == red_axis else "parallel" for ax in range(len(grid)))
    return pl.pallas_call(body, out_shape=out_shape, grid=grid, in_specs=in_specs, out_specs=out_spec,
                          scratch_shapes=scratch, name=name, compiler_params=_params(*sem))(*operands)


def _rmsnorm(name, h, gain, tm, after=()):
    t, d = h.shape
    after_ops, after_specs = _after_operands(after)

    def body(h_ref, g_ref, *rest):
        u_ref = rest[-1]
        hv = h_ref[...]
        r = lax.rsqrt(jnp.mean(hv * hv, axis=-1, keepdims=True) + EPS)
        u_ref[...] = ((hv * r) * g_ref[...]).astype(u_ref.dtype)

    return pl.pallas_call(
        body, out_shape=SDS((t, d), BF16), grid=(t // tm,),
        in_specs=[BS((tm, d), lambda i: (i, 0)), BS((1, d), lambda i: (0, 0))] + after_specs,
        out_specs=BS((tm, d), lambda i: (i, 0)), name=name, compiler_params=_params("parallel"))(h, gain, *after_ops)


def _rmsnorm_bwd(name, h, gain, du, dh_in, tm):
    t, d = h.shape
    has_in = dh_in is not None

    def body(*refs):
        if has_in:
            h_ref, g_ref, du_ref, dhin_ref, dh_ref, dhb_ref, dg_ref = refs
        else:
            h_ref, g_ref, du_ref, dh_ref, dhb_ref, dg_ref = refs
        i = pl.program_id(0)
        hv = h_ref[...]
        r = lax.rsqrt(jnp.mean(hv * hv, axis=-1, keepdims=True) + EPS)
        n = hv * r
        duv = du_ref[...].astype(F32)
        dn = duv * g_ref[...]
        dh = r * (dn - n * jnp.mean(dn * n, axis=-1, keepdims=True))
        if has_in:
            dh = dhin_ref[...] + dh
        dh_ref[...] = dh
        dhb_ref[...] = dh.astype(BF16)

        @pl.when(i == 0)
        def _():
            dg_ref[...] = jnp.zeros_like(dg_ref)

        dg_ref[...] += jnp.sum(duv * n, axis=0, keepdims=True)

    row = BS((tm, d), lambda i: (i, 0))
    vec = BS((1, d), lambda i: (0, 0))
    operands = [h, gain, du] + ([dh_in] if has_in else [])
    in_specs = [row, vec, row] + ([row] if has_in else [])
    return pl.pallas_call(
        body, out_shape=(SDS((t, d), F32), SDS((t, d), BF16), SDS((1, d), F32)), grid=(t // tm,),
        in_specs=in_specs, out_specs=(row, row, vec), name=name, compiler_params=_params("arbitrary"))(*operands)


def _loss_head_tile(i, hv, g_ref, t_ref, loss_ref, dh_ref, dhb_ref, dg_ref):
    g = g_ref[...]
    r = lax.rsqrt(jnp.mean(hv * hv, axis=-1, keepdims=True) + EPS)
    n = hv * r
    err = n * g - t_ref[...]
    dy = err * (1.0 / hv.shape[-1])
    dn = dy * g
    dh = r * (dn - n * jnp.mean(dn * n, axis=-1, keepdims=True))
    dh_ref[...] = dh
    dhb_ref[...] = dh.astype(BF16)

    @pl.when(i == 0)
    def _():
        dg_ref[...] = jnp.zeros_like(dg_ref)
        loss_ref[...] = jnp.zeros_like(loss_ref)

    dg_ref[...] += jnp.sum(dy * n, axis=0, keepdims=True)
    part = 0.5 * jnp.sum(jnp.mean(err * err, axis=-1, keepdims=True), axis=0, keepdims=True)
    loss_ref[...] += jnp.broadcast_to(part, loss_ref.shape)


def _norm_tile(h, g_ref, u_ref):
    r = lax.rsqrt(jnp.mean(h * h, axis=-1, keepdims=True) + EPS)
    u_ref[...] = ((h * r) * g_ref[...]).astype(u_ref.dtype)


FFN_BLOCK = D_FF // 2


def _ffn_up(name, u, w_f, tm, after=()):
    t, d = u.shape
    after_ops, after_specs = _after_operands(after)

    def body(u_ref, wg_ref, wu_ref, *rest):
        pg_ref, pu_ref, a_ref = rest[len(after_ops):]
        uv = u_ref[...]
        for lo in range(0, D_FF, FFN_BLOCK):
            cols = slice(lo, lo + FFN_BLOCK)
            g = _dot(uv, wg_ref[cols, :], NT)
            up = _dot(uv, wu_ref[cols, :], NT)
            sg = _sigmoid(g)
            silu = g * sg
            a_ref[:, cols] = (silu * up).astype(BF16)
            pu_ref[:, cols] = (0.5 * silu).astype(BF16)
            pg_ref[:, cols] = (0.5 * sg * (1.0 + g * (1.0 - sg)) * up).astype(BF16)

    hid = BS((tm, D_FF), lambda i: (i, 0))
    shape = SDS((t, D_FF), BF16)
    whole = BS((D_FF, d), lambda i: (0, 0))
    return pl.pallas_call(
        body, out_shape=(shape, shape, shape), grid=(t // tm,),
        in_specs=[BS((tm, d), lambda i: (i, 0)), whole, whole] + after_specs,
        out_specs=(hid, hid, hid), name=name,
        compiler_params=_params("parallel"))(u, w_f["gate"], w_f["up"], *after_ops)


def _ffn_down(name, a, w_f, resid, tm, next_gain=None, head=None):
    t, d = resid.shape
    row = BS((tm, d), lambda i: (i, 0))
    vec = BS((1, d), lambda i: (0, 0))

    def body(a_ref, w_ref, res_ref, *rest):
        h = res_ref[...] + 0.5 * _dot(a_ref[...], w_ref[...])
        if head is not None:
            _loss_head_tile(pl.program_id(0), h, *rest)
        else:
            g_ref, h_ref, u_ref = rest
            h_ref[...] = h
            _norm_tile(h, g_ref, u_ref)

    if head is not None:
        extra, extra_specs = list(head), [vec, row]
        out_shape = (SDS((1, 128), F32), SDS((t, d), F32), SDS((t, d), BF16), SDS((1, d), F32))
        out_specs = (BS((1, 128), lambda i: (0, 0)), row, row, vec)
    else:
        extra, extra_specs = [next_gain], [vec]
        out_shape = (SDS((t, d), F32), SDS((t, d), BF16))
        out_specs = (row, row)
    return pl.pallas_call(
        body, out_shape=out_shape, grid=(t // tm,),
        in_specs=[BS((tm, D_FF), lambda i: (i, 0)), BS((D_FF, d), lambda i: (0, 0)), row] + extra_specs,
        out_specs=out_specs, name=name,
        compiler_params=_params("arbitrary" if head is not None else "parallel"))(a, w_f["down"], resid, *extra)


def _mix_in(u, w_t, tm, after=()):
    t, d = u.shape
    after_ops, after_specs = _after_operands(after)

    def body(u_ref, w_ref, *rest):
        o_ref, s_ref = rest[-2:]
        uv = u_ref[...]
        for lo in range(0, D_FF, FFN_BLOCK):
            o_ref[:, lo:lo + FFN_BLOCK] = _dot(uv, w_ref[lo:lo + FFN_BLOCK, :], NT)
        s_ref[...] = o_ref[:, D_POOL:D_POOL + D_SSM].astype(BF16)

    return pl.pallas_call(
        body, out_shape=(SDS((t, D_FF), F32), SDS((t, D_SSM), BF16)), grid=(t // tm,),
        in_specs=[BS((tm, d), lambda i: (i, 0)), BS((D_FF, d), lambda i: (0, 0))] + after_specs,
        out_specs=(BS((tm, D_FF), lambda i: (i, 0)), BS((tm, D_SSM), lambda i: (i, 0))), name="mix_in",
        compiler_params=_params("parallel"))(u, w_t, *after_ops)


def _mm_resid_norm(name, a, b, resid, next_gain, tm):
    t, d = resid.shape
    tm = min(2 * tm, t)

    def body(a_ref, b_ref, res_ref, g_ref, h_ref, u_ref):
        h = res_ref[...] + _dot(a_ref[...], b_ref[...])
        h_ref[...] = h
        _norm_tile(h, g_ref, u_ref)

    row = BS((tm, d), lambda i: (i, 0))
    return pl.pallas_call(
        body, out_shape=(SDS((t, d), F32), SDS((t, d), BF16)), grid=(t // tm,),
        in_specs=[BS((tm, a.shape[1]), lambda i: (i, 0)), BS(b.shape, lambda i: (0, 0)), row, BS((1, d), lambda i: (0, 0))],
        out_specs=(row, row), name=name, compiler_params=_params("parallel"))(a, b, resid, next_gain)


def _ffn_bwd_act(name, dh_b, w_f, pg, pu, tm, after=()):
    t, d = dh_b.shape
    after_ops, after_specs = _after_operands(after)

    def body(dh_ref, wd_ref, pg_ref, pu_ref, *rest):
        dg_ref, dup_ref = rest[len(after_ops):]
        dh = dh_ref[...]
        for lo in range(0, D_FF, FFN_BLOCK):
            cols = slice(lo, lo + FFN_BLOCK)
            da = _dot(dh, wd_ref[cols, :], NT)
            dg_ref[:, cols] = (da * pg_ref[:, cols].astype(F32)).astype(BF16)
            dup_ref[:, cols] = (da * pu_ref[:, cols].astype(F32)).astype(BF16)

    hid = BS((tm, D_FF), lambda i: (i, 0))
    shape = SDS((t, D_FF), BF16)
    return pl.pallas_call(
        body, out_shape=(shape, shape), grid=(t // tm,),
        in_specs=[BS((tm, d), lambda i: (i, 0)), BS((D_FF, d), lambda i: (0, 0)), hid, hid] + after_specs,
        out_specs=(hid, hid), name=name,
        compiler_params=_params("parallel"))(dh_b, w_f["down"], pg, pu, *after_ops)


def _ffn_dw(name, u, dg, dup, a, dh_b, tm):
    t, d = u.shape
    n_t = t // tm

    def body(u_ref, dg_ref, dup_ref, a_ref, dh_ref, og_ref, ou_ref, od_ref, acc):
        i = pl.program_id(1)

        @pl.when(i == 0)
        def _():
            acc[...] = jnp.zeros_like(acc)

        uv = u_ref[...]
        acc[0] += _dot(dg_ref[...], uv, TN)
        acc[1] += _dot(dup_ref[...], uv, TN)
        acc[2] += _dot(a_ref[...], dh_ref[...], TN)

        @pl.when(i == n_t - 1)
        def _():
            og_ref[...] = acc[0].astype(BF16)
            ou_ref[...] = acc[1].astype(BF16)
            od_ref[...] = (0.5 * acc[2]).astype(BF16)

    hid = BS((tm, FFN_BLOCK), lambda j, i: (i, j))
    row = BS((tm, d), lambda j, i: (i, 0))
    out = BS((FFN_BLOCK, d), lambda j, i: (j, 0))
    shape = SDS((D_FF, d), BF16)
    return pl.pallas_call(
        body, out_shape=(shape, shape, shape), grid=(D_FF // FFN_BLOCK, n_t),
        in_specs=[row, hid, hid, hid, row], out_specs=(out, out, out),
        scratch_shapes=[pltpu.VMEM((3, FFN_BLOCK, d), F32)],
        name=name, compiler_params=_params("parallel", "arbitrary"))(u, dg, dup, a, dh_b)


def _norm_bwd_tile(i, du, h_ref, g_ref, dhin_ref, dh_ref, dhb_ref, dg_ref):
    hv = h_ref[...]
    r = lax.rsqrt(jnp.mean(hv * hv, axis=-1, keepdims=True) + EPS)
    n = hv * r
    dn = du * g_ref[...]
    dh = dhin_ref[...] + r * (dn - n * jnp.mean(dn * n, axis=-1, keepdims=True))
    dh_ref[...] = dh
    dhb_ref[...] = dh.astype(BF16)

    @pl.when(i == 0)
    def _():
        dg_ref[...] = jnp.zeros_like(dg_ref)

    dg_ref[...] += jnp.sum(du * n, axis=0, keepdims=True)


def _norm_bwd_specs(tm):
    row = BS((tm, D_MODEL), lambda i: (i, 0))
    vec = BS((1, D_MODEL), lambda i: (0, 0))
    return [row, vec, row], (row, row, vec)


def _norm_bwd_shapes(t):
    return SDS((t, D_MODEL), F32), SDS((t, D_MODEL), BF16), SDS((1, D_MODEL), F32)


def _ffn_dx(name, dg, dup, w_f, h, gain, dh_in, tm, after=()):
    t = dg.shape[0]
    tm = tm // 2
    after_ops, after_specs = _after_operands(after)

    def body(dg_ref, dup_ref, wg_ref, wu_ref, h_ref, g_ref, dhin_ref, *rest):
        du = _dot(dg_ref[...], wg_ref[...]) + _dot(dup_ref[...], wu_ref[...])
        _norm_bwd_tile(pl.program_id(0), du, h_ref, g_ref, dhin_ref, *rest[len(after_ops):])

    hid = BS((tm, D_FF), lambda i: (i, 0))
    whole = BS((D_FF, D_MODEL), lambda i: (0, 0))
    norm_in, norm_out = _norm_bwd_specs(tm)
    return pl.pallas_call(
        body, out_shape=_norm_bwd_shapes(t), grid=(t // tm,),
        in_specs=[hid, hid, whole, whole] + norm_in + after_specs, out_specs=norm_out, name=name,
        compiler_params=_params("arbitrary"))(dg, dup, w_f["gate"], w_f["up"], h, gain, dh_in, *after_ops)


def _mm_norm_bwd(name, a, b, dims, h, gain, dh_in, tm):
    pieces = list(a) if isinstance(a, (list, tuple)) else [a]
    assert len(pieces) == 1 or dims == NN
    t = pieces[0].shape[0]
    widths = [p.shape[1] for p in pieces]
    row0 = [sum(widths[:j]) for j in range(len(pieces))]

    def body(*refs):
        a_refs, (b_ref, h_ref, g_ref, dhin_ref), outs = refs[:len(pieces)], refs[len(pieces):len(pieces) + 4], refs[len(pieces) + 4:]
        if len(pieces) == 1:
            du = _dot(a_refs[0][...], b_ref[...], dims)
        else:
            du = _dot(a_refs[0][...], b_ref[0:widths[0], :])
            for a_ref, r0, w in zip(a_refs[1:], row0[1:], widths[1:]):
                du = du + _dot(a_ref[...], b_ref[r0:r0 + w, :])
        _norm_bwd_tile(pl.program_id(0), du, h_ref, g_ref, dhin_ref, *outs)

    norm_in, norm_out = _norm_bwd_specs(tm)
    return pl.pallas_call(
        body, out_shape=_norm_bwd_shapes(t), grid=(t // tm,),
        in_specs=[BS((tm, w), lambda i: (i, 0)) for w in widths] + [BS(b.shape, lambda i: (0, 0))] + norm_in,
        out_specs=norm_out, name=name, compiler_params=_params("arbitrary"))(*pieces, b, h, gain, dh_in)


def _plain_mm(name, a, b, dims, out_dtype, tm, resid=None, after=()):
    t = a.shape[0]
    tm = min(2 * tm, t)
    n = b.shape[1] if dims == NN else b.shape[0]
    extras = [(resid, BS((tm, n), lambda i: (i, 0)))] if resid is not None else []
    epi = (lambda acc, res: res + acc) if resid is not None else None
    return _mm(name, [(a, BS((tm, a.shape[1]), lambda i: (i, 0)), b, BS(b.shape, lambda i: (0, 0)), dims)],
               grid=(t // tm,), out_shape=SDS((t, n), out_dtype), out_spec=BS((tm, n), lambda i: (i, 0)),
               extras=extras, epilogue=epi, after=after)


def _dw_mm(name, a, b, tm, out_dtype=BF16, after=()):
    t, k = a.shape
    n = b.shape[1]
    tm = min(2 * tm, t)
    return _mm(name, [(a, BS((tm, k), lambda i: (i, 0)), b, BS((tm, n), lambda i: (i, 0)), TN)],
               grid=(t // tm,), red_axis=0, out_shape=SDS((k, n), out_dtype), out_spec=BS((k, n), lambda i: (0, 0)),
               after=after)


def _mix_in_dw(pieces, u, tm):
    t, d = u.shape
    tm = min(2 * tm, t)
    n_steps = t // tm
    widths = [p.shape[1] for p in pieces]
    row0 = [sum(widths[:j]) for j in range(len(pieces))]
    assert sum(widths) == D_FF

    def body(*refs):
        p_refs, u_ref, o_ref, acc_ref = refs[:len(pieces)], refs[len(pieces)], refs[-2], refs[-1]
        k = pl.program_id(0)

        @pl.when(k == 0)
        def _():
            acc_ref[...] = jnp.zeros_like(acc_ref)

        uv = u_ref[...]
        for p_ref, r0, w in zip(p_refs, row0, widths):
            acc_ref[r0:r0 + w, :] += _dot(p_ref[...], uv, TN)

        @pl.when(k == n_steps - 1)
        def _():
            o_ref[...] = acc_ref[...].astype(BF16)

    return pl.pallas_call(
        body, out_shape=SDS((D_FF, d), BF16), grid=(n_steps,),
        in_specs=[BS((tm, w), lambda i: (i, 0)) for w in widths] + [BS((tm, d), lambda i: (i, 0))],
        out_specs=BS((D_FF, d), lambda i: (0, 0)), scratch_shapes=[pltpu.VMEM((D_FF, d), F32)],
        name="mix_in_dw", compiler_params=_params("arbitrary"))(*pieces, u)


POOL_CHUNK = 256
POOL_HALO = 8


def _window_sum(v, width, lead):
    n = v.shape[0]
    s = v
    k = 1
    while k < width:
        s = s + pltpu.roll(s, n - k, 0)
        k *= 2
    return pltpu.roll(s, lead, 0) if lead else s


def _pool_count(base, left, right, t, shape):
    pos = base + lax.broadcasted_iota(jnp.int32, shape, 0)
    lo = jnp.maximum(pos - left, 0)
    hi = jnp.minimum(pos + right + 1, t)
    return (hi - lo).astype(F32)


def _pool_fwd(proj, pool_w, pool_scale):
    t = proj.shape[0]
    c, h = POOL_CHUNK, POOL_HALO
    n_chunks = t // c

    def body(proj_hbm, pw_ref, sc_ref, pooled_ref, mixed_ref, ms_ref, pad_ref, sem):
        cp = pltpu.make_async_copy(proj_hbm.at[:, pl.ds(0, D_POOL)], pad_ref.at[pl.ds(h, t), :], sem)
        cp.start()
        pad_ref[pl.ds(0, h), :] = jnp.zeros((h, D_POOL), F32)
        pad_ref[pl.ds(t + h, h), :] = jnp.zeros((h, D_POOL), F32)
        cp.wait()
        for g, width in enumerate(POOL_WINDOWS):
            left = width // 2
            right = width - 1 - left
            cols = slice(g * POOL_GROUP, (g + 1) * POOL_GROUP)
            wmat = pw_ref[g].astype(BF16)
            scale = sc_ref[:, cols]

            def chunk(ci, carry, left=left, right=right, width=width, cols=cols, wmat=wmat, scale=scale):
                base = pl.multiple_of(ci * c, c)
                v = pad_ref[pl.ds(base, c + 2 * h), cols]
                win = _window_sum(v, width, left)[h:h + c]
                cnt = _pool_count(base, left, right, t, (c, POOL_GROUP))
                pooled = (win / cnt - v[h:h + c]).astype(BF16)
                mixed = _dot(pooled, wmat)
                pooled_ref[pl.ds(base, c), cols] = pooled
                mixed_ref[pl.ds(base, c), cols] = mixed.astype(BF16)
                ms_ref[pl.ds(base, c), cols] = (mixed * scale).astype(BF16)
                return carry

            lax.fori_loop(0, n_chunks, chunk, 0)

    vm = BS(memory_space=pltpu.VMEM)
    shape = SDS((t, D_POOL), BF16)
    return pl.pallas_call(
        body, out_shape=(shape, shape, shape),
        in_specs=[BS(memory_space=pl.ANY), vm, vm], out_specs=(vm, vm, vm),
        scratch_shapes=[pltpu.VMEM((t + 2 * h, D_POOL), F32), pltpu.SemaphoreType.DMA],
        name="pool_fwd", compiler_params=_params())(proj, pool_w, pool_scale)


def _pool_bwd(d_ms, mixed, pooled, pool_w, pool_scale):
    t = d_ms.shape[0]
    c, h = POOL_CHUNK, POOL_HALO
    n_chunks = t // c

    def body(dms_ref, mixed_ref, pooled_ref, pw_ref, sc_ref, dp_ref, dsc_ref, dpw_ref, pad_ref):
        pad_ref[pl.ds(0, h), :] = jnp.zeros((h, D_POOL), F32)
        pad_ref[pl.ds(t + h, h), :] = jnp.zeros((h, D_POOL), F32)
        for g, width in enumerate(POOL_WINDOWS):
            left = width // 2
            right = width - 1 - left
            cols = slice(g * POOL_GROUP, (g + 1) * POOL_GROUP)
            wmat = pw_ref[g].astype(BF16)
            scale = sc_ref[:, cols]

            def first(ci, carry, left=left, right=right, cols=cols, wmat=wmat, scale=scale):
                dsc, dpw = carry
                base = pl.multiple_of(ci * c, c)
                dms = dms_ref[pl.ds(base, c), cols].astype(F32)
                dsc = dsc + jnp.sum(dms * mixed_ref[pl.ds(base, c), cols].astype(F32), axis=0, keepdims=True)
                dmix = (dms * scale).astype(BF16)
                dpw = dpw + _dot(pooled_ref[pl.ds(base, c), cols], dmix, TN)
                dpooled = _dot(dmix, wmat, NT)
                cnt = _pool_count(base, left, right, t, (c, POOL_GROUP))
                pad_ref[pl.ds(base + h, c), cols] = dpooled / cnt
                return dsc, dpw

            dsc, dpw = lax.fori_loop(0, n_chunks, first,
                                     (jnp.zeros((1, POOL_GROUP), F32), jnp.zeros((POOL_GROUP, POOL_GROUP), F32)))
            dsc_ref[:, cols] = dsc
            dpw_ref[g] = dpw

            def second(ci, carry, left=left, right=right, width=width, cols=cols):
                base = pl.multiple_of(ci * c, c)
                v = pad_ref[pl.ds(base, c + 2 * h), cols]
                win = _window_sum(v, width, right)[h:h + c]
                cnt = _pool_count(base, left, right, t, (c, POOL_GROUP))
                dp_ref[pl.ds(base, c), cols] = (win - v[h:h + c] * cnt).astype(BF16)
                return carry

            lax.fori_loop(0, n_chunks, second, 0)

    vm = BS(memory_space=pltpu.VMEM)
    return pl.pallas_call(
        body, out_shape=(SDS((t, D_POOL), BF16), SDS((1, D_POOL), F32), SDS((4, POOL_GROUP, POOL_GROUP), F32)),
        in_specs=[vm] * 5, out_specs=(vm, vm, vm),
        scratch_shapes=[pltpu.VMEM((t + 2 * h, D_POOL), F32)],
        name="pool_bwd", compiler_params=_params())(d_ms, mixed, pooled, pool_w, pool_scale)


SSM_ROWS = 2 * SSM_GROUPS * SSM_GROUP
SSM_HALF = SSM_GROUPS * SSM_GROUP


def _ssm_zoh(a_r, a_i, ldt):
    dt = jnp.exp(ldt)
    mag = jnp.exp(dt * a_r)
    ang = dt * a_i
    cs, sn = jnp.cos(ang), jnp.sin(ang)
    abr, abi = mag * cs, mag * sn
    den = a_r * a_r + a_i * a_i
    nr = abr - 1.0
    qr = (nr * a_r + abi * a_i) / den
    qi = (abi * a_r - nr * a_i) / den
    return dt, mag, cs, sn, abr, abi, den, nr, qr, qi


def _ssm_group_mask():
    row = lax.broadcasted_iota(jnp.int32, (SSM_HALF, SSM_CH), 0)
    col = lax.broadcasted_iota(jnp.int32, (SSM_HALF, SSM_CH), 1)
    return (row // SSM_GROUP) == (col // SSM_STATE)


def _ssm_prep(a_r, a_i, ldt, b_r, b_i, c_r, c_i, after=()):
    after_ops, after_specs = _after_operands(after)

    def body(ar_ref, ai_ref, ldt_ref, br_ref, bi_ref, cr_ref, ci_ref, *rest):
        abr_ref, abi_ref, win_ref, wint_ref, woutt_ref, wout_ref = rest[len(after_ops):]
        *_, abr, abi, _, _, qr, qi = _ssm_zoh(ar_ref[...], ai_ref[...], ldt_ref[...])
        abr_ref[...] = abr
        abi_ref[...] = abi
        b_r, b_i = br_ref[...], bi_ref[...]
        bbr = qr * b_r - qi * b_i
        bbi = qr * b_i + qi * b_r
        mask = _ssm_group_mask()
        state = lax.broadcasted_iota(jnp.int32, (SSM_STATE, SSM_CH), 0)
        col = lax.broadcasted_iota(jnp.int32, (SSM_STATE, SSM_CH), 1)
        every_group = (col % SSM_STATE == state).astype(BF16)

        def spread(x):
            return jnp.where(mask, _dot(x, every_group), 0.0)

        for d in range(2):
            rows = slice(d * SSM_HALF, (d + 1) * SSM_HALF)
            for half, x_in, x_out in ((0, bbr[rows], cr_ref[rows, :]), (1, bbi[rows], -ci_ref[rows, :])):
                cols = slice(half * SSM_CH, (half + 1) * SSM_CH)
                m_in, m_out = spread(x_in), spread(x_out)
                win_ref[d, :, cols] = m_in.astype(BF16)
                wint_ref[d, cols, :] = m_in.T.astype(BF16)
                woutt_ref[d, :, cols] = m_out.astype(BF16)
                wout_ref[d, cols, :] = m_out.T.astype(BF16)

    vm = BS(memory_space=pltpu.VMEM)
    vec = SDS((SSM_ROWS, SSM_STATE), F32)
    wide = SDS((2, SSM_HALF, 2 * SSM_CH), BF16)
    tall = SDS((2, 2 * SSM_CH, SSM_HALF), BF16)
    return pl.pallas_call(body, out_shape=(vec, vec, wide, tall, wide, tall), in_specs=[vm] * 7 + after_specs,
                          out_specs=(vm,) * 6, name="ssm_prep",
                          compiler_params=_params())(a_r, a_i, ldt, b_r, b_i, c_r, c_i, *after_ops)


def _ssm_prep_bwd(a_r, a_i, ldt, b_r, b_i, d_abr, d_abi, d_win, d_woutt):
    def body(ar_ref, ai_ref, ldt_ref, br_ref, bi_ref, *rest):
        (dabr_refs, dabi_refs, dwin_refs, dwoutt_refs), outs = [rest[2 * k:2 * k + 2] for k in range(4)], rest[8:]
        dar_ref, dai_ref, dldt_ref, dbr_ref, dbi_ref, dcr_ref, dci_ref = outs
        a_r, a_i = ar_ref[...], ai_ref[...]
        dt, mag, cs, sn, abr, abi, den, nr, qr, qi = _ssm_zoh(a_r, a_i, ldt_ref[...])
        mask = _ssm_group_mask()
        col = lax.broadcasted_iota(jnp.int32, (SSM_CH, SSM_STATE), 0)
        state = lax.broadcasted_iota(jnp.int32, (SSM_CH, SSM_STATE), 1)
        own_state = (col % SSM_STATE == state).astype(BF16)

        def pick(dense):
            m = jnp.where(mask, dense, 0.0)
            hi = m.astype(BF16)
            lo = m - hi.astype(F32)
            return _dot(hi, own_state) + _dot(lo, own_state)

        def picked(refs, half):
            cols = slice(half * SSM_CH, (half + 1) * SSM_CH)
            return jnp.concatenate([pick(ref[:, cols]) for ref in refs], axis=0)

        first_channel = lax.broadcasted_iota(jnp.int32, (SSM_HALF, SSM_CH), 0) % SSM_GROUP == 0

        def first_rows(refs):
            return jnp.concatenate(
                [pick(jnp.where(first_channel, jnp.broadcast_to(ref[...], (SSM_HALF, SSM_CH)), 0.0)) for ref in refs], axis=0)

        g_r, g_i = picked(dwin_refs, 0), picked(dwin_refs, 1)
        dcr_ref[...] = picked(dwoutt_refs, 0)
        dci_ref[...] = -picked(dwoutt_refs, 1)
        b_r, b_i = br_ref[...], bi_ref[...]
        dbr_ref[...] = g_r * qr + g_i * qi
        dbi_ref[...] = g_i * qr - g_r * qi
        gqr = g_r * b_r + g_i * b_i
        gqi = g_i * b_r - g_r * b_i
        g_nr_num = gqr / den
        g_ni_num = gqi / den
        g_den = -(gqr * qr + gqi * qi) / den
        g_nr = g_nr_num * a_r - g_ni_num * a_i
        g_abi = g_nr_num * a_i + g_ni_num * a_r
        d_ar = g_nr_num * nr + g_ni_num * abi + 2.0 * a_r * g_den
        d_ai = g_nr_num * abi - g_ni_num * nr + 2.0 * a_i * g_den
        g_abr = first_rows(dabr_refs) + g_nr
        g_abi = first_rows(dabi_refs) + g_abi
        g_mag = g_abr * cs + g_abi * sn
        g_ang = mag * (g_abi * cs - g_abr * sn)
        g_e = g_mag * mag
        d_ar = d_ar + g_e * dt
        d_ai = d_ai + g_ang * dt
        g_dt = g_e * a_r + g_ang * a_i
        dar_ref[...] = d_ar
        dai_ref[...] = d_ai
        dldt_ref[...] = g_dt * dt

    vm = BS(memory_space=pltpu.VMEM)
    vec = SDS((SSM_ROWS, SSM_STATE), F32)
    return pl.pallas_call(body, out_shape=(vec,) * 7, in_specs=[vm] * 13, out_specs=(vm,) * 7, name="ssm_prep_bwd",
                          compiler_params=_params())(a_r, a_i, ldt, b_r, b_i, *d_abr, *d_abi, *d_win, *d_woutt)


SCAN_ROWS = 512
SCAN_SUB = 128


def _ssm_scan(name, inp, w1, a_r, a_i, w2, reverse, dr, conj=False):
    t = inp.shape[0]
    rows = min(SCAN_ROWS, t)
    n = t // rows
    n_sub = rows // SCAN_SUB
    ch = SSM_CH
    at = (lambda i: (n - 1 - i, 0)) if reverse else (lambda i: (i, 0))

    def body(in_ref, w1_ref, ar_ref, ai_ref, w2_ref, sb_ref, out_ref, cr_ref, ci_ref, k_ref, st_ref):
        i = pl.program_id(0)

        @pl.when(i == 0)
        def _():
            ar8 = jnp.broadcast_to(ar_ref[...], (8, ch))
            ai8 = jnp.broadcast_to(-ai_ref[...] if conj else ai_ref[...], (8, ch))
            row = lax.broadcasted_iota(jnp.int32, (8, ch), 0)
            rank = (7 - row) if reverse else row
            powers = [(ar8, ai8)]
            for _ in range(7):
                p_r, p_i = powers[-1]
                powers.append((p_r * ar8 - p_i * ai8, p_r * ai8 + p_i * ar8))
            zero = jnp.zeros((8, ch), F32)
            for slot, k in enumerate((1, 2, 4)):
                k_ref[2 * slot] = jnp.where(rank >= k, powers[k - 1][0], zero)
                k_ref[2 * slot + 1] = jnp.where(rank >= k, powers[k - 1][1], zero)
            carry_r, carry_i = zero, zero
            for j in range(8):
                carry_r = jnp.where(rank == j, powers[j][0], carry_r)
                carry_i = jnp.where(rank == j, powers[j][1], carry_i)
            k_ref[6] = carry_r
            k_ref[7] = carry_i
            cr_ref[...] = zero
            ci_ref[...] = zero

        def group(r0, carry):
            c_r, c_i = carry
            x_r = st_ref[pl.ds(r0, 8), 0:ch]
            x_i = st_ref[pl.ds(r0, 8), ch:2 * ch]
            for slot, k in enumerate((1, 2, 4)):
                shift = (8 - k) if reverse else k
                s_r = pltpu.roll(x_r, shift, 0)
                s_i = pltpu.roll(x_i, shift, 0)
                m_r, m_i = k_ref[2 * slot], k_ref[2 * slot + 1]
                x_r, x_i = x_r + m_r * s_r - m_i * s_i, x_i + m_r * s_i + m_i * s_r
            p_r, p_i = k_ref[6], k_ref[7]
            x_r, x_i = x_r + p_r * c_r - p_i * c_i, x_i + p_r * c_i + p_i * c_r
            st_ref[pl.ds(r0, 8), 0:ch] = x_r
            st_ref[pl.ds(r0, 8), ch:2 * ch] = x_i
            last = 0 if reverse else 7
            return (jnp.broadcast_to(x_r[last:last + 1, :], (8, ch)), jnp.broadcast_to(x_i[last:last + 1, :], (8, ch)))

        carry = (cr_ref[...], ci_ref[...])
        for sc in (range(n_sub - 1, -1, -1) if reverse else range(n_sub)):
            part = pl.ds(sc * SCAN_SUB, SCAN_SUB)
            st_ref[part, :] = _dot(in_ref[part, :], w1_ref[...])
            for gi in range(SCAN_SUB // 8):
                g = (SCAN_SUB // 8 - 1 - gi) if reverse else gi
                carry = group(sc * SCAN_SUB + g * 8, carry)
            states = st_ref[part, :].astype(BF16)
            sb_ref[part, :] = states
            out_ref[part, :] = _dot(states, w2_ref[...])
        cr_ref[...] = carry[0]
        ci_ref[...] = carry[1]

    return pl.pallas_call(
        body, out_shape=(SDS((t, 2 * ch), BF16), SDS((t, D_SSM), F32)), grid=(n,),
        in_specs=[BS((rows, D_SSM), at), BS((None, D_SSM, 2 * ch), lambda i: (dr, 0, 0)), BS((None, 1, ch), lambda i: (dr, 0, 0)),
                  BS((None, 1, ch), lambda i: (dr, 0, 0)), BS((None, 2 * ch, D_SSM), lambda i: (dr, 0, 0))],
        out_specs=(BS((rows, 2 * ch), at), BS((rows, D_SSM), at)),
        scratch_shapes=[pltpu.VMEM((8, ch), F32), pltpu.VMEM((8, ch), F32), pltpu.VMEM((8, 8, ch), F32),
                        pltpu.VMEM((rows, 2 * ch), F32)],
        name=name, compiler_params=_params("arbitrary"))(inp, w1, a_r, a_i, w2)


DA_ROWS = 1024


def _ssm_param_grads(name, lam, states, u, dy, reverse, after=()):
    t = lam.shape[0]
    rows = min(DA_ROWS, t)
    n = t // rows
    halo_rows = 16
    nb = rows // halo_rows
    ch = SSM_CH
    if reverse:
        halo_at = lambda i: (jnp.minimum((i + 1) * nb, t // halo_rows - 1), 0)
    else:
        halo_at = lambda i: (jnp.maximum(i * nb - 1, 0), 0)

    after_ops, after_specs = _after_operands(after)

    def body(lam_ref, x_ref, halo_ref, u_ref, dy_ref, *rest):
        dr_ref, di_ref, dwin_ref, dwoutt_ref = rest[len(after_ops):]
        i = pl.program_id(0)

        @pl.when(i == 0)
        def _():
            dr_ref[...] = jnp.zeros_like(dr_ref)
            di_ref[...] = jnp.zeros_like(di_ref)
            dwin_ref[...] = jnp.zeros_like(dwin_ref)
            dwoutt_ref[...] = jnp.zeros_like(dwoutt_ref)

        dwin_ref[...] += _dot(u_ref[...], lam_ref[...], TN)
        dwoutt_ref[...] += _dot(dy_ref[...], x_ref[...], TN)
        row = lax.broadcasted_iota(jnp.int32, (rows, ch), 0)
        if reverse:
            edge, shift, h_row, live = rows - 1, rows - 1, 0, i < n - 1
        else:
            edge, shift, h_row, live = 0, 1, halo_rows - 1, i > 0

        def neighbour(lo):
            halo = halo_ref[:, lo:lo + ch].astype(F32)[h_row:h_row + 1]
            halo = jnp.where(live, halo, 0.0)
            x = x_ref[:, lo:lo + ch].astype(F32)
            return jnp.where(row == edge, jnp.broadcast_to(halo, (rows, ch)), pltpu.roll(x, shift, 0))

        xp_r, xp_i = neighbour(0), neighbour(ch)
        l_r, l_i = lam_ref[:, 0:ch].astype(F32), lam_ref[:, ch:2 * ch].astype(F32)
        dr_ref[...] += jnp.sum(l_r * xp_r + l_i * xp_i, axis=0, keepdims=True)
        di_ref[...] += jnp.sum(l_i * xp_r - l_r * xp_i, axis=0, keepdims=True)

    blk = BS((rows, 2 * ch), lambda i: (i, 0))
    thin = BS((rows, D_SSM), lambda i: (i, 0))
    vec = BS((1, ch), lambda i: (0, 0))
    mat = BS((D_SSM, 2 * ch), lambda i: (0, 0))
    return pl.pallas_call(
        body, out_shape=(SDS((1, ch), F32), SDS((1, ch), F32), SDS((D_SSM, 2 * ch), F32), SDS((D_SSM, 2 * ch), F32)),
        grid=(n,), in_specs=[blk, blk, BS((halo_rows, 2 * ch), halo_at), thin, thin] + after_specs,
        out_specs=(vec, vec, mat, mat),
        name=name, compiler_params=_params("arbitrary"))(lam, states, states, u, dy, *after_ops)


GELU_C = math.sqrt(2.0 / math.pi)
GELU_K = 0.044715


def _ssm_combine(proj, y_fwd, y_bwd, d_skip, tm, after=()):
    t = proj.shape[0]
    after_ops, after_specs = _after_operands(after)

    def body(s_ref, yf_ref, yb_ref, d_ref, *rest):
        yt_ref, g_ref = rest[len(after_ops):]
        y = s_ref[...] * d_ref[...] + yf_ref[...] + yb_ref[...]
        yt_ref[...] = y
        th = jnp.tanh(GELU_C * (y + GELU_K * y * y * y))
        g_ref[...] = (0.5 * y * (1.0 + th)).astype(BF16)

    blk = BS((tm, D_SSM), lambda i: (i, 0))
    return pl.pallas_call(
        body, out_shape=(SDS((t, D_SSM), F32), SDS((t, D_SSM), BF16)), grid=(t // tm,),
        in_specs=[BS((tm, D_SSM), lambda i: (i, D_POOL // D_SSM)), blk, blk, BS((1, D_SSM), lambda i: (0, 0))] + after_specs,
        out_specs=(blk, blk), name="ssm_combine",
        compiler_params=_params("parallel"))(proj, y_fwd, y_bwd, d_skip, *after_ops)


def _ssm_ds(proj, d_yt, du_fwd, du_bwd, d_skip, tm):
    t = proj.shape[0]

    def body(s_ref, dy_ref, duf_ref, dub_ref, d_ref, ds_ref, dd_ref):
        i = pl.program_id(0)
        dy = dy_ref[...]
        ds_ref[...] = (dy * d_ref[...] + duf_ref[...] + dub_ref[...]).astype(BF16)

        @pl.when(i == 0)
        def _():
            dd_ref[...] = jnp.zeros_like(dd_ref)

        dd_ref[...] += jnp.sum(dy * s_ref[...], axis=0, keepdims=True)

    blk = BS((tm, D_SSM), lambda i: (i, 0))
    vec = BS((1, D_SSM), lambda i: (0, 0))
    return pl.pallas_call(
        body, out_shape=(SDS((t, D_SSM), BF16), SDS((1, D_SSM), F32)), grid=(t // tm,),
        in_specs=[BS((tm, D_SSM), lambda i: (i, D_POOL // D_SSM)), blk, blk, blk, vec],
        out_specs=(blk, vec), name="ssm_ds", compiler_params=_params("arbitrary"))(proj, d_yt, du_fwd, du_bwd, d_skip)


G_POOL_AT = D_POOL + D_SSM
G_SSM_AT = G_POOL_AT + D_MODEL
E_VAL, E_GATE = D_POOL, D_POOL + D_SSM


def _merge_specs(tm):
    return [BS((tm, D_POOL), lambda i: (i, 0)), BS((tm, D_SSM), lambda i: (i, 0)),
            BS((N_SHARD, 1024, 256), lambda i: (0, 0, 0)), BS((tm, D_FF), lambda i: (i, 0))]


def _merge_parts(s, ms, yv, w_ref, proj_ref):
    lo = 256 * s
    zp = _dot(ms, w_ref[s, 0:E_VAL, :])
    zv = _dot(yv, w_ref[s, E_VAL:E_GATE, :])
    zg = _dot(yv, w_ref[s, E_GATE:, :])
    return zp, zv, zg, proj_ref[:, G_POOL_AT + lo:G_POOL_AT + lo + 256], proj_ref[:, G_SSM_AT + lo:G_SSM_AT + lo + 256]


def _mixer_merge(ms, yssm, w_e, proj, tm):
    t = ms.shape[0]

    def body(ms_ref, y_ref, w_ref, proj_ref, o_ref):
        msv, yv = ms_ref[...], y_ref[...]
        for s in range(N_SHARD):
            zp, zv, zg, gp, gs = _merge_parts(s, msv, yv, w_ref, proj_ref)
            o_ref[:, 256 * s:256 * (s + 1)] = (_sigmoid(gp) * zp + _sigmoid(gs) * zv * _sigmoid(zg)).astype(BF16)

    row = BS((tm, D_MODEL), lambda i: (i, 0))
    return pl.pallas_call(
        body, out_shape=SDS((t, D_MODEL), BF16), grid=(t // tm,), in_specs=_merge_specs(tm), out_specs=row,
        name="mixer_merge", compiler_params=_params("parallel"))(ms, yssm, w_e, proj)


def _mixer_merge_bwd(ms, yssm, w_e, proj, dmerged, tm):
    t = ms.shape[0]

    def body(ms_ref, y_ref, w_ref, proj_ref, dm_ref, dgp_ref, dgs_ref, dzp_ref, dzv_ref, dzg_ref):
        msv, yv = ms_ref[...], y_ref[...]
        for s in range(N_SHARD):
            cols = slice(256 * s, 256 * (s + 1))
            zp, zv, zg, gp, gs = _merge_parts(s, msv, yv, w_ref, proj_ref)
            dm = dm_ref[:, cols].astype(F32)
            sp, ss, sg = _sigmoid(gp), _sigmoid(gs), _sigmoid(zg)
            dgp_ref[:, cols] = (dm * zp * sp * (1.0 - sp)).astype(BF16)
            dgs_ref[:, cols] = (dm * zv * sg * ss * (1.0 - ss)).astype(BF16)
            dzp_ref[:, cols] = (dm * sp).astype(BF16)
            dz = dm * ss
            dzv_ref[:, cols] = (dz * sg).astype(BF16)
            dzg_ref[:, cols] = (dz * zv * sg * (1.0 - sg)).astype(BF16)

    row = BS((tm, D_MODEL), lambda i: (i, 0))
    shape = SDS((t, D_MODEL), BF16)
    return pl.pallas_call(
        body, out_shape=(shape,) * 5, grid=(t // tm,), in_specs=_merge_specs(tm) + [row],
        out_specs=(row,) * 5, name="mixer_merge_bwd",
        compiler_params=_params("parallel"))(ms, yssm, w_e, proj, dmerged)


def _mixer_dw(ms, yssm, dzp, dzv, dzg, tm):
    t = ms.shape[0]
    tm = min(2 * tm, t)
    n_t = t // tm

    def body(ms_ref, y_ref, dzp_ref, dzv_ref, dzg_ref, o_ref, acc):
        i = pl.program_id(0)

        @pl.when(i == 0)
        def _():
            acc[...] = jnp.zeros_like(acc)

        msv, yv = ms_ref[...], y_ref[...]
        for s in range(N_SHARD):
            cols = slice(256 * s, 256 * (s + 1))
            acc[s, 0:E_VAL, :] += _dot(msv, dzp_ref[:, cols], TN)
            acc[s, E_VAL:E_GATE, :] += _dot(yv, dzv_ref[:, cols], TN)
            acc[s, E_GATE:, :] += _dot(yv, dzg_ref[:, cols], TN)

        @pl.when(i == n_t - 1)
        def _():
            o_ref[...] = acc[...].astype(BF16)

    row = BS((tm, D_MODEL), lambda i: (i, 0))
    full = BS((N_SHARD, 1024, 256), lambda i: (0, 0, 0))
    return pl.pallas_call(
        body, out_shape=SDS((N_SHARD, 1024, 256), BF16), grid=(n_t,),
        in_specs=[BS((tm, D_POOL), lambda i: (i, 0)), BS((tm, D_SSM), lambda i: (i, 0)), row, row, row],
        out_specs=full, scratch_shapes=[pltpu.VMEM((N_SHARD, 1024, 256), F32)],
        name="mixer_dw", compiler_params=_params("arbitrary"))(ms, yssm, dzp, dzv, dzg)


def _mixer_dx(dzp, dzv, dzg, w_e, y_total, tm):
    t = dzp.shape[0]

    def body(dzp_ref, dzv_ref, dzg_ref, w_ref, yt_ref, dms_ref, dy_ref, dyb_ref):
        acc_ms, acc_y = None, None
        for s in range(N_SHARD):
            cols = slice(256 * s, 256 * (s + 1))
            part_ms = _dot(dzp_ref[:, cols], w_ref[s, 0:E_VAL, :], NT)
            part_y = _dot(dzv_ref[:, cols], w_ref[s, E_VAL:E_GATE, :], NT) + _dot(dzg_ref[:, cols], w_ref[s, E_GATE:, :], NT)
            acc_ms = part_ms if s == 0 else acc_ms + part_ms
            acc_y = part_y if s == 0 else acc_y + part_y
        dms_ref[...] = acc_ms.astype(BF16)
        y = yt_ref[...]
        th = jnp.tanh(GELU_C * (y + GELU_K * y * y * y))
        dgelu = 0.5 * (1.0 + th) + 0.5 * y * (1.0 - th * th) * GELU_C * (1.0 + 3.0 * GELU_K * y * y)
        dy = acc_y * dgelu
        dy_ref[...] = dy
        dyb_ref[...] = dy.astype(BF16)

    row = BS((tm, D_MODEL), lambda i: (i, 0))
    narrow = BS((tm, D_SSM), lambda i: (i, 0))
    return pl.pallas_call(
        body, out_shape=(SDS((t, D_POOL), BF16), SDS((t, D_SSM), F32), SDS((t, D_SSM), BF16)), grid=(t // tm,),
        in_specs=[row, row, row, BS((N_SHARD, 1024, 256), lambda i: (0, 0, 0)), narrow],
        out_specs=(BS((tm, D_POOL), lambda i: (i, 0)), narrow, narrow),
        name="mixer_dx", compiler_params=_params("parallel"))(dzp, dzv, dzg, w_e, y_total)


def _attn_probs(q_h, k_h):
    s = _dot(q_h, k_h, NT) * (1.0 / math.sqrt(HEAD_DIM))
    e = jnp.exp(s - jnp.max(s, axis=-1, keepdims=True))
    return e / jnp.sum(e, axis=-1, keepdims=True)


def _attn_fwd(q, kv, tm):
    t = q.shape[0]
    tm = min(2 * tm, t)
    m = kv.shape[0]

    def body(q_ref, kv_ref, o_ref):
        for hd in range(N_HEADS):
            lo = hd * HEAD_DIM
            p = _attn_probs(q_ref[:, lo:lo + HEAD_DIM], kv_ref[:, lo:lo + HEAD_DIM])
            o_ref[:, lo:lo + HEAD_DIM] = _dot(p, kv_ref[:, D_MODEL + lo:D_MODEL + lo + HEAD_DIM]).astype(BF16)

    return pl.pallas_call(
        body, out_shape=SDS((t, D_MODEL), BF16), grid=(t // tm,),
        in_specs=[BS((tm, D_MODEL), lambda i: (i, 0)), BS((m, 2 * D_MODEL), lambda i: (0, 0))],
        out_specs=BS((tm, D_MODEL), lambda i: (i, 0)), name="attn_fwd", compiler_params=_params("parallel"))(q, kv)


def _attn_bwd(q, kv, d_o, tm):
    t = q.shape[0]
    m = kv.shape[0]

    def body(q_ref, kv_ref, do_ref, dq_ref, dkv_ref):
        i = pl.program_id(0)

        @pl.when(i == 0)
        def _():
            dkv_ref[...] = jnp.zeros_like(dkv_ref)

        for hd in range(N_HEADS):
            lo = hd * HEAD_DIM
            q_h = q_ref[:, lo:lo + HEAD_DIM]
            k_h = kv_ref[:, lo:lo + HEAD_DIM]
            v_h = kv_ref[:, D_MODEL + lo:D_MODEL + lo + HEAD_DIM]
            do_h = do_ref[:, lo:lo + HEAD_DIM]
            p = _attn_probs(q_h, k_h)
            dkv_ref[:, D_MODEL + lo:D_MODEL + lo + HEAD_DIM] += _dot(p, do_h, TN)
            dp = _dot(do_h, v_h, NT)
            ds = p * (dp - jnp.sum(dp * p, axis=-1, keepdims=True)) * (1.0 / math.sqrt(HEAD_DIM))
            dq_ref[:, lo:lo + HEAD_DIM] = _dot(ds, k_h).astype(BF16)
            dkv_ref[:, lo:lo + HEAD_DIM] += _dot(ds, q_h, TN)

    row = BS((tm, D_MODEL), lambda i: (i, 0))
    full = BS((m, 2 * D_MODEL), lambda i: (0, 0))
    return pl.pallas_call(
        body, out_shape=(SDS((t, D_MODEL), BF16), SDS((m, 2 * D_MODEL), F32)), grid=(t // tm,),
        in_specs=[row, full, row], out_specs=(row, full), name="attn_bwd",
        compiler_params=_params("arbitrary"))(q, kv, d_o)


TRANSPOSED = ("ffn1_w_gate", "ffn1_w_up", "ffn2_w_gate", "ffn2_w_up", "w_in")
GATHER_PHASES = {"f1a": (("ffn1_w_gate",), ("ffn1_w_up",)),
                 "f1b": (("ffn1_w_down",),),
                 "win": (("w_in",),),
                 "mix": (("w_mix_out",), ("w_q",), ("w_xo",), ("w_kv",), ("w_pool_proj", "w_glu_val", "w_glu_gate")),
                 "f2": (("ffn2_w_gate",), ("ffn2_w_up",), ("ffn2_w_down",))}
REDUCE_GROUPS = (("ffn2_w_gate",), ("ffn2_w_up",), ("ffn2_w_down",), ("w_xo",), ("w_q",), ("w_kv",), ("w_mix_out",),
                 ("w_pool_proj", "w_glu_val", "w_glu_gate"), ("w_in",), ("ffn1_w_gate",), ("ffn1_w_up",), ("ffn1_w_down",))
SMALL = ("ffn1_norm", "mix_norm", "pool_w", "pool_scale", "ssm_a_re", "ssm_a_im", "ssm_log_dt", "ssm_b_re",
         "ssm_b_im", "ssm_c_re", "ssm_c_im", "ssm_d", "xattn_norm", "mem_norm", "ffn2_norm", "final_norm")
WEIGHTS = ("ffn1_norm", "ffn1_w_gate", "ffn1_w_up", "ffn1_w_down", "mix_norm", "w_in", "pool_w", "pool_scale",
           "w_pool_proj", "ssm_a_re", "ssm_a_im", "ssm_log_dt", "ssm_b_re", "ssm_b_im", "ssm_c_re", "ssm_c_im",
           "ssm_d", "w_glu_val", "w_glu_gate", "w_mix_out", "xattn_norm", "mem_norm", "w_q", "w_kv", "w_xo",
           "ffn2_norm", "ffn2_w_gate", "ffn2_w_up", "ffn2_w_down", "final_norm")


def _small_view(a, n):
    return jnp.swapaxes(a, 3, 4) if n in ("ssm_b_re", "ssm_b_im") else a


def _device_step(x, mem, target, wts, sp, reducer=None):
    t = x.shape[0]
    tm = min(TM, t)
    g = {}

    first_gather = wts.start("win", wts.start("f1b", wts.start("f1a")))
    u1 = _rmsnorm("norm_ffn1", x, sp["ffn1_norm"], tm, after=first_gather)

    def per_channel(a):
        a = a.reshape(2 * SSM_GROUPS, 1, -1)
        return jnp.broadcast_to(a, (2 * SSM_GROUPS, SSM_GROUP, a.shape[-1])).reshape(SSM_ROWS, a.shape[-1])

    ssm_a = per_channel(sp["ssm_a_re"]), per_channel(sp["ssm_a_im"]), per_channel(sp["ssm_log_dt"])
    ssm_b = sp["ssm_b_re"].reshape(SSM_ROWS, SSM_STATE), sp["ssm_b_im"].reshape(SSM_ROWS, SSM_STATE)
    abr, abi, w_in_s, w_in_s_t, w_out_s_t, w_out_s = _ssm_prep(
        *ssm_a, *ssm_b, sp["ssm_c_re"].reshape(SSM_ROWS, SSM_STATE), sp["ssm_c_im"].reshape(SSM_ROWS, SSM_STATE),
        after=first_gather)
    first_rows = (2, SSM_GROUPS, SSM_GROUP, SSM_STATE)
    a_r = abr.reshape(first_rows)[:, :, 0].reshape(2, 1, SSM_CH)
    a_i = abi.reshape(first_rows)[:, :, 0].reshape(2, 1, SSM_CH)
    mem_n = _rmsnorm("norm_mem", mem, sp["mem_norm"], mem.shape[0], after=first_gather + wts.sources(("mix", "f2")))

    whole = (D_FF, D_MODEL)
    w_g1, w_u1 = wts.finish("f1a", [u1, w_in_s, w_in_s_t, w_out_s, w_out_s_t, a_r, a_i, mem_n])
    w_f1 = {"gate": w_g1.reshape(whole), "up": w_u1.reshape(whole)}
    g1, up1, a1 = _ffn_up("ffn1_up", u1, w_f1, tm)
    (w_dn,) = wts.finish("f1b", [a1])
    w_f1["down"] = w_dn.reshape(whole)
    h1, u2 = _ffn_down("ffn1_down", a1, w_f1, x, tm, next_gain=sp["mix_norm"])

    (w_in_g,) = wts.finish("win", [u2])
    w_in_t = w_in_g.reshape(D_FF, D_MODEL)
    proj, s_in = _mix_in(u2, w_in_t, tm, after=wts.start("f2", wts.start("mix", [w_in_g])))
    pooled, mixed, ms = _pool_fwd(proj, sp["pool_w"][0], sp["pool_scale"])

    states, y_dirs = [], []
    for dr in range(2):
        st, yd = _ssm_scan(f"ssm_scan_fwd{dr}", s_in, w_in_s, a_r, a_i, w_out_s, reverse=(dr == 1), dr=dr)
        states.append(st)
        y_dirs.append(yd)
    y_total, yssm = _ssm_combine(proj, y_dirs[0], y_dirs[1], sp["ssm_d"], tm, after=wts.fill_start("mix", y_dirs))
    *w_sq, w_kv, w_e = wts.finish("mix", [yssm, ms])
    w_mo, w_q, w_xo = (a.reshape(D_MODEL, D_MODEL) for a in w_sq)
    w_d = w_kv[:, None]

    merged = _mixer_merge(ms, yssm, w_e, proj, tm)
    h2, u3 = _mm_resid_norm("mix_out", merged, w_mo, h1, sp["xattn_norm"], tm)

    q = _plain_mm("attn_q", u3, w_q, NN, BF16, tm)
    n_mem = mem.shape[0]
    kv = _mm("attn_kv", [(mem_n, BS((n_mem, D_MODEL), lambda s: (0, 0)), w_d, BS((None, None, D_MODEL, 512), lambda s: (s, 0, 0, 0)), NN)],
             grid=(N_SHARD,), out_shape=SDS((n_mem, 2 * D_MODEL), BF16), out_spec=BS((n_mem, 512), lambda s: (0, s)))
    o = _attn_fwd(q, kv, tm)
    h3, u4 = _mm_resid_norm("attn_out", o, w_xo, h2, sp["ffn2_norm"], tm)

    w_f2 = dict(zip(("gate", "up", "down"), (a.reshape(whole) for a in wts.finish("f2", [u4]))))
    g2, up2, a2 = _ffn_up("ffn2_up", u4, w_f2, tm)
    loss, dh4, dh4_b, g["final_norm"] = _ffn_down("ffn2_down", a2, w_f2, h3, tm,
                                                  head=(sp["final_norm"].reshape(1, D_MODEL), target))

    dg2, dup2 = _ffn_bwd_act("ffn2_bwd_act", dh4_b, w_f2, g2, up2, tm)
    dw_f2 = _ffn_dw("ffn2_dw", u4, dg2, dup2, a2, dh4_b, tm)
    dh3, dh3_b, g["ffn2_norm"] = _ffn_dx("ffn2_dx", dg2, dup2, w_f2, h3, sp["ffn2_norm"], dh4, tm)

    d_o = _plain_mm("attn_out_dx", dh3_b, w_xo, NT, BF16, tm)
    dw_xo = _dw_mm("attn_out_dw", o, dh3_b, tm)
    dq, dkv = _attn_bwd(q, kv, d_o, tm)
    dw_q = _dw_mm("attn_q_dw", u3, dq, tm)
    dh2, dh2_b, g["xattn_norm"] = _mm_norm_bwd("attn_q_dx", dq, w_q, NT, h2, sp["xattn_norm"], dh3, tm)
    dw_kv = _mm("attn_kv_dw", [(mem_n, BS((n_mem, D_MODEL), lambda s: (0, 0)), dkv, BS((n_mem, 512), lambda s: (0, s)), TN)],
                grid=(N_SHARD,), out_shape=SDS((N_SHARD, D_MODEL, 512), BF16), out_spec=BS((None, D_MODEL, 512), lambda s: (s, 0, 0)))
    dmem_n = _mm("attn_kv_dx", [(dkv, BS((n_mem, 512), lambda s: (0, s)), w_d, BS((None, None, D_MODEL, 512), lambda s: (s, 0, 0, 0)), NT)],
                 grid=(N_SHARD,), red_axis=0, out_shape=SDS((n_mem, D_MODEL), F32), out_spec=BS((n_mem, D_MODEL), lambda s: (0, 0)))
    _, _, g["mem_norm"] = _rmsnorm_bwd("norm_mem_bwd", mem, sp["mem_norm"], dmem_n, None, n_mem)

    square = (N_SHARD, D_MODEL // N_SHARD, D_MODEL)
    sharded = (N_SHARD, FF_SH, D_MODEL)
    early = [a.reshape(sharded) for a in dw_f2] + [dw_xo.reshape(square), dw_q.reshape(square), dw_kv]
    swapping = reducer.swap_start("a1", early) if reducer is not None else []
    dmerged = _plain_mm("mix_out_dx", dh2_b, w_mo, NT, BF16, tm, after=swapping)
    dw_mo = _dw_mm("mix_out_dw", merged, dh2_b, tm)
    d_gp, d_gs, dzp, dzv, dzg = _mixer_merge_bwd(ms, yssm, w_e, proj, dmerged, tm)
    dw_e = _mixer_dw(ms, yssm, dzp, dzv, dzg, tm)
    d_ms, d_yt, d_yt_b = _mixer_dx(dzp, dzv, dzg, w_e, y_total, tm)
    dp, d_scale, d_pw = _pool_bwd(d_ms, mixed, pooled, sp["pool_w"][0], sp["pool_scale"])
    g["pool_scale"] = d_scale
    g["pool_w"] = d_pw[None]

    du_dirs, lams = [], []
    for dr in range(2):
        lam, du = _ssm_scan(f"ssm_scan_bwd{dr}", d_yt_b, w_out_s_t, a_r, a_i, w_in_s_t, reverse=(dr == 0), dr=dr, conj=True)
        du_dirs.append(du)
        lams.append(lam)
    ds, g["ssm_d"] = _ssm_ds(proj, d_yt, du_dirs[0], du_dirs[1], sp["ssm_d"], tm)

    d_proj = [dp, ds, d_gp, d_gs]
    dw_in_t = _mix_in_dw(d_proj, u2, tm)
    dh1, dh1_b, g["mix_norm"] = _mm_norm_bwd("mix_in_dx", d_proj, w_in_t, NN, h1, sp["mix_norm"], dh2, tm)

    early += [dw_mo.reshape(square), dw_e, dw_in_t.reshape(sharded)]
    g["final_norm"] = g["final_norm"].reshape(D_MODEL)

    travelling = reducer.start("a", early[6:], swapped=["a1"], after=list(g.values())) if reducer is not None else []
    d_abr, d_abi, d_cm, d_bm = [], [], [], []
    for dr in range(2):
        da_r, da_i, d_win, d_woutt = _ssm_param_grads(f"ssm_param_grads{dr}", lams[dr], states[dr], s_in, d_yt_b,
                                                      reverse=(dr == 1), after=travelling)
        d_abr.append(da_r)
        d_abi.append(da_i)
        d_bm.append(d_win)
        d_cm.append(d_woutt)

    d_ar, d_ai, d_ldt, d_br, d_bi, d_cr, d_ci = _ssm_prep_bwd(*ssm_a, *ssm_b, d_abr, d_abi, d_bm, d_cm)
    per_group = (2 * SSM_GROUPS, SSM_GROUP * SSM_STATE)
    g["ssm_a_re"] = d_ar.reshape(2 * SSM_GROUPS, SSM_GROUP, SSM_STATE).sum(axis=1).reshape(sp["ssm_a_re"].shape)
    g["ssm_a_im"] = d_ai.reshape(2 * SSM_GROUPS, SSM_GROUP, SSM_STATE).sum(axis=1).reshape(sp["ssm_a_im"].shape)
    g["ssm_log_dt"] = d_ldt.reshape(per_group).sum(axis=1).reshape(sp["ssm_log_dt"].shape)
    g["ssm_b_re"] = d_br.reshape(sp["ssm_b_re"].shape)
    g["ssm_b_im"] = d_bi.reshape(sp["ssm_b_im"].shape)
    g["ssm_c_re"] = d_cr.reshape(sp["ssm_c_re"].shape)
    g["ssm_c_im"] = d_ci.reshape(sp["ssm_c_im"].shape)
    if reducer is not None:
        travelling = travelling + [d_ar, d_br, d_cr]
    dg1, dup1 = _ffn_bwd_act("ffn1_bwd_act", dh1_b, w_f1, g1, up1, tm, after=travelling)
    dw_f1 = [a.reshape(sharded) for a in _ffn_dw("ffn1_dw", u1, dg1, dup1, a1, dh1_b, tm)]
    if reducer is not None:
        joining = reducer.join_start("a", after=reducer.finish("a", reducer.swap_start("b1", dw_f1)))
        travelling = joining + reducer.start("b", [], swapped=["b1"], after=joining)
    grad_x, _, g["ffn1_norm"] = _ffn_dx("ffn1_dx", dg1, dup1, w_f1, x, sp["ffn1_norm"], dh1, tm, after=travelling)
    if reducer is not None:
        reducer.join_finish("a", [grad_x])
    return loss, grad_x, early + dw_f1, g


def _mesh_place():
    x, y, c = lax.axis_index("x"), lax.axis_index("y"), lax.axis_index("c")
    chips = [(1 - x, y), (x, 1 - y), (1 - x, 1 - y)]
    return x, y, c, chips


def _remote(src, dst, send_sems, recv_sems, k, to):
    return pltpu.make_async_remote_copy(src_ref=src, dst_ref=dst, send_sem=send_sems.at[k], recv_sem=recv_sems.at[k],
                                        device_id=to, device_id_type=MESH)


def _sibling_swap_halves(tag, grads, after=()):
    n = len(grads)
    after_ops, after_specs = _after_operands(after)

    def body(*refs):
        ins, outs = refs[:n], refs[n + len(after_ops):2 * n + len(after_ops)]
        send_sems, recv_sems = refs[2 * n + len(after_ops):]
        x, y, c, _ = _mesh_place()
        sibling = (x, y, 1 - c)
        copies = []
        for k in range(n):
            half = grads[k].shape[1] // 2
            theirs = pl.ds(pl.multiple_of((1 - c) * half, 16), half)
            cp = _remote(ins[k].at[:, theirs, :], outs[k], send_sems, recv_sems, k, sibling)
            cp.start()
            copies.append(cp)
        for cp in copies:
            cp.wait_recv()
        for cp in copies:
            cp.wait_send()

    hbm = BS(memory_space=pl.ANY)
    return pl.pallas_call(
        body, out_shape=tuple(SDS((g.shape[0], g.shape[1] // 2, g.shape[2]), g.dtype) for g in grads),
        in_specs=[hbm] * n + after_specs, out_specs=(hbm,) * n,
        scratch_shapes=[pltpu.SemaphoreType.DMA((n,)), pltpu.SemaphoreType.DMA((n,))],
        name="reduce_sibling_send_" + tag, compiler_params=_params())(*grads, *after_ops)


def _row_tile(rows, cap=512):
    return max(r for r in range(16, cap + 1, 16) if rows % r == 0)


REDUCE_STEPS = 2


def _chip_presum(tag, grads, gots, c_idx):
    n = len(grads)
    halves = [g.shape[1] // 2 for g in grads]
    tiles = [(h // REDUCE_STEPS, g.shape[2]) for h, g in zip(halves, grads)]

    def body(c_ref, *refs):
        for k in range(n):
            refs[2 * n + k][...] = (refs[k][...].astype(F32) + refs[n + k][...].astype(F32)).astype(BF16)

    mine = [BS((None, None) + tile, lambda s, i, c_ref: (s, c_ref[0], i, 0)) for tile in tiles]
    plain = [BS((None,) + tile, lambda s, i, c_ref: (s, i, 0)) for tile in tiles]
    return list(pl.pallas_call(
        body, out_shape=tuple(SDS((g.shape[0], h, g.shape[2]), BF16) for g, h in zip(grads, halves)),
        grid_spec=pltpu.PrefetchScalarGridSpec(num_scalar_prefetch=1, grid=(N_SHARD, REDUCE_STEPS),
                                               in_specs=mine + plain, out_specs=plain),
        name="reduce_presum_" + tag, compiler_params=_params("parallel", "parallel"))(
            c_idx, *[g.reshape(g.shape[0], 2, h, g.shape[2]) for g, h in zip(grads, halves)], *gots))


HBM_SPEC = BS(memory_space=pltpu.HBM)
SEM_SPEC = BS(memory_space=pltpu.SEMAPHORE)
DATAFLOW = pltpu.SideEffectType.DATAFLOW_SIDE_EFFECTING


def _chip_exchange_copies(parts, lands, send_sems, recv_sems):
    _, _, c, chips = _mesh_place()
    return [_remote(parts[k].at[2 * px + py], lands[k].at[j], send_sems, recv_sems, 3 * k + j, (px, py, c))
            for k in range(len(parts)) for j, (px, py) in enumerate(chips)]


def _gather_copies(shards, lands, send_sems, recv_sems):
    x, y, c, chips = _mesh_place()
    return [_remote(shards[k], lands[k].at[2 * x + y], send_sems, recv_sems, 3 * k + j, (px, py, c))
            for k in range(len(shards)) for j, (px, py) in enumerate(chips)]


def _gather_half_copies(shards, lands, send_sems, recv_sems):
    x, y, c, chips = _mesh_place()
    out = []
    for k in range(len(shards)):
        half = shards[k].shape[0] // 2
        mine = pl.ds(pl.multiple_of(c * half, 16), half)
        for j, (px, py) in enumerate(chips):
            out.append(_remote(shards[k].at[mine, :], lands[k].at[2 * x + y, mine, :], send_sems, recv_sems,
                               3 * k + j, (px, py, c)))
    return out


def _fill_copies(zones, _, send_sems, recv_sems):
    x, y, c, chips = _mesh_place()
    out = []
    for k in range(len(zones)):
        half = zones[k].shape[1] // 2
        mine = pl.ds(pl.multiple_of(c * half, 16), half)
        for j, (px, py) in enumerate(chips):
            blk = zones[k].at[2 * px + py, mine, :]
            out.append(_remote(blk, blk, send_sems, recv_sems, 3 * k + j, (x, y, 1 - c)))
    return out


def _sibling_fill(tag, lands):
    n = len(lands)

    def body(*refs):
        outs = refs[n:2 * n]
        copies = _fill_copies(outs, outs, *refs[2 * n:])
        for cp in copies:
            cp.start()
        for cp in copies:
            cp.wait_recv()
        for cp in copies:
            cp.wait_send()

    hbm = BS(memory_space=pl.ANY)
    return list(pl.pallas_call(
        body, out_shape=tuple(SDS(a.shape, a.dtype) for a in lands),
        in_specs=[hbm] * n, out_specs=(hbm,) * n, input_output_aliases={k: k for k in range(n)},
        scratch_shapes=[pltpu.SemaphoreType.DMA((3 * n,)), pltpu.SemaphoreType.DMA((3 * n,))],
        name="gather_fill_" + tag, compiler_params=_params())(*lands))


def _swap_copies(grads, lands, send_sems, recv_sems):
    x, y, c, _ = _mesh_place()
    out = []
    for k in range(len(grads)):
        half = grads[k].shape[1] // 2
        theirs = pl.ds(pl.multiple_of((1 - c) * half, 16), half)
        out.append(_remote(grads[k].at[:, theirs, :], lands[k], send_sems, recv_sems, k, (x, y, 1 - c)))
    return out


def _join_copies(fulls, same, send_sems, recv_sems):
    x, y, c, _ = _mesh_place()
    out = []
    for k in range(len(fulls)):
        half = fulls[k].shape[0] // 2
        mine = fulls[k].at[pl.ds(pl.multiple_of(c * half, 8), half), :]
        out.append(_remote(mine, mine, send_sems, recv_sems, k, (x, y, 1 - c)))
    return out


def _everyone_copies(packs, lands, send_sems, recv_sems):
    x, y, c, _ = _mesh_place()
    out = []
    for k in range(len(packs)):
        for j in range(N_DEV - 1):
            bx, by, bc = (j + 1) >> 2 & 1, (j + 1) >> 1 & 1, (j + 1) & 1
            peer = (x ^ bx, y ^ by, c ^ bc)
            out.append(_remote(packs[k], lands[k].at[4 * x + 2 * y + c], send_sems, recv_sems, (N_DEV - 1) * k + j, peer))
    return out


def _split_start(name, copies, sources, land_shapes, after=(), fanout=3):
    n = len(sources)
    n_land = len(land_shapes)
    m = n + n_land
    n_sems = fanout * n
    after_ops, after_specs = _after_operands(after)

    def body(*refs):
        ins = refs[:n]
        lands = refs[n:m] if n_land else ins
        send_sems, recv_sems = refs[m + len(after_ops)], refs[m + len(after_ops) + 1]
        token = refs[-1]
        for cp in copies(ins, lands, send_sems, recv_sems):
            cp.start()
        token[...] = jnp.zeros_like(token)

    lands = [pltpu.with_memory_space_constraint(lax.empty(s, d), pltpu.HBM) for s, d in land_shapes]
    sources = [pltpu.with_memory_space_constraint(p, pltpu.HBM) for p in sources]
    thru = [pltpu.HBM(a.shape, a.dtype) for a in sources + lands]
    out = pl.pallas_call(
        body, name=name,
        out_shape=(pltpu.SemaphoreType.DMA((n_sems,)), pltpu.SemaphoreType.DMA((n_sems,)), *thru, SDS((8, 128), F32)),
        in_specs=[HBM_SPEC] * m + after_specs,
        out_specs=(SEM_SPEC, SEM_SPEC, *[HBM_SPEC] * m, BS(memory_space=pltpu.VMEM)),
        input_output_aliases={i: 2 + i for i in range(m)},
        compiler_params=pltpu.CompilerParams(has_side_effects=DATAFLOW))(*sources, *lands, *after_ops)
    return out[0], out[1], list(out[2:2 + n]), list(out[2 + n:2 + m]), out[-1]


def _split_wait(name, copies, send_sems, recv_sems, sources, lands, after):
    n = len(sources)
    m = n + len(lands)
    after_ops, after_specs = _after_operands(after)

    def body(*refs):
        ins = refs[:n]
        zones = refs[n:m] if m > n else ins
        for cp in copies(ins, zones, refs[m], refs[m + 1]):
            cp.wait_send()
            cp.wait_recv()

    out = pl.pallas_call(
        body, name=name,
        out_shape=tuple(pltpu.HBM(a.shape, a.dtype) for a in sources + lands),
        in_specs=[HBM_SPEC] * m + [SEM_SPEC, SEM_SPEC] + after_specs, out_specs=(HBM_SPEC,) * m,
        input_output_aliases={i: i for i in range(m)},
        compiler_params=pltpu.CompilerParams(has_side_effects=DATAFLOW))(*sources, *lands, send_sems, recv_sems, *after_ops)
    return list(out[:n]), list(out[n:])


class _WeightGatherer:
    def __init__(self, shards):
        self.shards, self.open, self.filling = shards, {}, {}
        self.me = 2 * lax.axis_index("x") + lax.axis_index("y")

    HALVED = ("f1a", "win", "mix")

    def start(self, tag, after=()):
        shapes = [((N_SHARD,) + s.shape, s.dtype) for s in self.shards[tag]]
        copies = _gather_half_copies if tag in self.HALVED else _gather_copies
        self.open[tag] = _split_start("gather_start_" + tag, copies, self.shards[tag], shapes, after)
        return [self.open[tag][-1]]

    def sources(self, tags):
        return [s for tag in tags for s in self.shards[tag]]

    def fill_start(self, tag, after):
        send_sems, recv_sems, shards, lands, _ = self.open.pop(tag)
        shards, lands = _split_wait("gather_wait_" + tag, _gather_half_copies, send_sems, recv_sems, shards, lands, after)
        self.filling[tag] = shards, _split_start("gather_fill_start_" + tag, _fill_copies, lands, [])
        return [self.filling[tag][1][-1]]

    def finish(self, tag, after):
        if tag in self.filling:
            shards, (send_sems, recv_sems, lands, _, _) = self.filling.pop(tag)
            lands, _ = _split_wait("gather_fill_wait_" + tag, _fill_copies, send_sems, recv_sems, lands, [], after)
            return [lax.dynamic_update_slice(zone, s[None], (self.me, 0, 0)) for zone, s in zip(lands, shards)]
        send_sems, recv_sems, shards, lands, _ = self.open.pop(tag)
        copies = _gather_half_copies if tag in self.HALVED else _gather_copies
        shards, lands = _split_wait("gather_wait_" + tag, copies, send_sems, recv_sems, shards, lands, after)
        if tag in self.HALVED:
            lands = _sibling_fill(tag, lands)
        return [lax.dynamic_update_slice(zone, s[None], (self.me, 0, 0)) for zone, s in zip(lands, shards)]


class _GradReducer:
    def __init__(self):
        self.c_idx = lax.axis_index("c").astype(jnp.int32).reshape(1)
        self.place = jnp.stack([2 * lax.axis_index("x") + lax.axis_index("y"), lax.axis_index("c")]).astype(jnp.int32)
        self.swaps, self.open, self.landed, self.joins, self.reduced = {}, {}, {}, {}, []

    def swap_start(self, tag, grads, after=()):
        shapes = [((g.shape[0], g.shape[1] // 2, g.shape[2]), g.dtype) for g in grads]
        self.swaps[tag] = _split_start("reduce_swap_start_" + tag, _swap_copies, grads, shapes, after, fanout=1)
        return [self.swaps[tag][-1]]

    def start(self, tag, grads, after=(), swapped=()):
        pairs = []
        for s in swapped:
            send_sems, recv_sems, early, lands, _ = self.swaps.pop(s)
            behind = grads[-1:] or list(after)
            pairs += zip(*_split_wait("reduce_swap_wait_" + s, _swap_copies, send_sems, recv_sems, early, lands, behind))
        if grads:
            pairs += zip(grads, _sibling_swap_halves(tag, grads, after))
        parts = _chip_presum(tag, [g for g, _ in pairs], [s for _, s in pairs], self.c_idx)
        shapes = [((3,) + p.shape[1:], p.dtype) for p in parts]
        self.open[tag] = _split_start("reduce_exchange_start_" + tag, _chip_exchange_copies, parts, shapes)
        return [self.open[tag][-1]]

    def finish(self, tag, after):
        send_sems, recv_sems, parts, lands, _ = self.open.pop(tag)
        self.landed[tag] = _split_wait("reduce_exchange_wait_" + tag, _chip_exchange_copies, send_sems, recv_sems, parts, lands, after)
        return self.landed[tag][1][:1]

    def _sums(self, tag, after=()):
        parts, landed = self.landed.pop(tag)
        return _chip_sum(tag, parts, landed, self.place, after)

    def join_start(self, tag, after=()):
        self.joins[tag] = _split_start("reduce_join_start_" + tag, _join_copies, self._sums(tag, after), [], fanout=1)
        return [self.joins[tag][-1]]

    def join_finish(self, tag, after):
        send_sems, recv_sems, fulls, _, _ = self.joins.pop(tag)
        self.reduced += _split_wait("reduce_join_wait_" + tag, _join_copies, send_sems, recv_sems, fulls, [], after)[0]


def _chip_sum(tag, parts, gots, place, after=()):
    n = len(parts)
    tiles = [(p.shape[1] // REDUCE_STEPS, p.shape[2]) for p in parts]
    after_ops, after_specs = _after_operands(after)

    def body(place_ref, *refs):
        outs = refs[2 * n + len(after_ops):]
        for k in range(n):
            acc = refs[k][...].astype(F32)
            for j in range(3):
                acc = acc + refs[n + k][j].astype(F32)
            outs[k][...] = acc

    return list(pl.pallas_call(
        body, out_shape=tuple(SDS((2 * p.shape[1], p.shape[2]), F32) for p in parts),
        grid_spec=pltpu.PrefetchScalarGridSpec(
            num_scalar_prefetch=1, grid=(REDUCE_STEPS,),
            in_specs=[BS((None,) + tile, lambda i, place_ref: (place_ref[0], i, 0)) for tile in tiles]
            + [BS((3,) + tile, lambda i, place_ref: (0, i, 0)) for tile in tiles] + after_specs,
            out_specs=[BS(tile, lambda i, place_ref: (place_ref[1] * REDUCE_STEPS + i, 0)) for tile in tiles]),
        name="reduce_sum_" + tag, compiler_params=_params("parallel"))(place, *parts, *gots, *after_ops))


N_DEV = 8


def _sum_devices(packs):
    _, rows, lanes = packs.shape

    def body(p_ref, o_ref):
        acc = p_ref[0]
        for dev in range(1, N_DEV):
            acc = acc + p_ref[dev]
        o_ref[...] = acc

    vm = BS(memory_space=pltpu.VMEM)
    return pl.pallas_call(body, out_shape=SDS((rows, lanes), F32), in_specs=[vm], out_specs=vm,
                          name="small_sum", compiler_params=_params())(packs)


def _adamw_refs(w_ref, g_ref, m_ref, v_ref, go_ref, d_ref, mo_ref, vo_ref):
    bc1 = 1.0 - ADAM_B1 ** ADAM_STEP
    bc2 = 1.0 - ADAM_B2 ** ADAM_STEP
    g = g_ref[...]
    m_new = ADAM_B1 * m_ref[...] + (1.0 - ADAM_B1) * g
    v_new = ADAM_B2 * v_ref[...] + (1.0 - ADAM_B2) * (g * g)
    go_ref[...] = g
    mo_ref[...] = m_new
    vo_ref[...] = v_new
    d_ref[...] = -ADAM_LR * ((m_new / bc1) / (jnp.sqrt(v_new / bc2) + ADAM_EPS) + ADAM_WD * w_ref[...])


def _adamw_small(ws, gs, ms, vs):
    n = len(ws)

    def body(*refs):
        for k in range(n):
            _adamw_refs(*[refs[j * n + k] for j in range(4)], *refs[4 * n + 4 * k:4 * n + 4 * k + 4])

    vm = BS(memory_space=pltpu.VMEM)
    outs = pl.pallas_call(
        body, out_shape=tuple(SDS(a.shape, F32) for a in ws for _ in range(4)), in_specs=[vm] * (4 * n),
        out_specs=(vm,) * (4 * n), name="adamw_small", compiler_params=_params())(*ws, *gs, *ms, *vs)
    return [outs[4 * k:4 * k + 4] for k in range(n)]


def _adamw(name, w, grad, row0, m, v, after=()):
    rows, cols = w.shape
    tr = rows if rows < 16 else _row_tile(rows, 352)
    after_ops, after_specs = _after_operands(after)

    def body(w_ref, g_ref, m_ref, v_ref, *rest):
        _adamw_refs(w_ref, g_ref, m_ref, v_ref, *rest[len(after_ops):])

    blk = BS((tr, cols), lambda i: (i, 0))
    shape = SDS((rows, cols), F32)
    return pl.pallas_call(
        body, out_shape=(shape,) * 4, grid=(rows // tr,),
        in_specs=[blk, BS((tr, cols), lambda i: (row0 // tr + i, 0)), blk, blk] + after_specs, out_specs=(blk,) * 4,
        name=name, compiler_params=_params("parallel"))(w, grad, m, v, *after_ops)


SMALL_LANES = 128


SMALL_TILE = 8 * SMALL_LANES


def _packed_rows(p):
    return -(-p.size // SMALL_TILE) * 8


def _pack_small(parts):
    tiles = [jnp.pad(jnp.ravel(p), (0, _packed_rows(p) * SMALL_LANES - p.size)).reshape(-1, SMALL_LANES) for p in parts]
    rows = sum(t.shape[0] for t in tiles)
    return jnp.concatenate(tiles + [jnp.zeros((-rows % 64, SMALL_LANES), F32)], axis=0)


def _unpack_small(packed, like):
    out, at = [], 0
    for p in like:
        rows = _packed_rows(p)
        out.append(jnp.ravel(packed[at:at + rows])[:p.size].reshape(p.shape))
        at += rows
    return out


def kernel(x, mem, ffn1_norm, ffn1_w_gate, ffn1_w_up, ffn1_w_down, mix_norm, w_in, pool_w, pool_scale, w_pool_proj, ssm_a_re, ssm_a_im, ssm_log_dt, ssm_b_re, ssm_b_im, ssm_c_re, ssm_c_im, ssm_d, w_glu_val, w_glu_gate, w_mix_out, xattn_norm, mem_norm, w_q, w_kv, w_xo, ffn2_norm, ffn2_w_gate, ffn2_w_up, ffn2_w_down, final_norm, loss_target, m_ffn1_norm, m_ffn1_w_gate, m_ffn1_w_up, m_ffn1_w_down, m_mix_norm, m_w_in, m_pool_w, m_pool_scale, m_w_pool_proj, m_ssm_a_re, m_ssm_a_im, m_ssm_log_dt, m_ssm_b_re, m_ssm_b_im, m_ssm_c_re, m_ssm_c_im, m_ssm_d, m_w_glu_val, m_w_glu_gate, m_w_mix_out, m_xattn_norm, m_mem_norm, m_w_q, m_w_kv, m_w_xo, m_ffn2_norm, m_ffn2_w_gate, m_ffn2_w_up, m_ffn2_w_down, m_final_norm, v_ffn1_norm, v_ffn1_w_gate, v_ffn1_w_up, v_ffn1_w_down, v_mix_norm, v_w_in, v_pool_w, v_pool_scale, v_w_pool_proj, v_ssm_a_re, v_ssm_a_im, v_ssm_log_dt, v_ssm_b_re, v_ssm_b_im, v_ssm_c_re, v_ssm_c_im, v_ssm_d, v_w_glu_val, v_w_glu_gate, v_w_mix_out, v_xattn_norm, v_mem_norm, v_w_q, v_w_kv, v_w_xo, v_ffn2_norm, v_ffn2_w_gate, v_ffn2_w_up, v_ffn2_w_down, v_final_norm):
    given = dict(locals())
    w = {n: given[n] for n in WEIGHTS}
    m = {n: given["m_" + n] for n in WEIGHTS}
    v = {n: given["v_" + n] for n in WEIGHTS}

    def shard_view(a, n):
        return a[0].T if n in TRANSPOSED else a[0]

    def shard_unview(a, n):
        return (a.T if n in TRANSPOSED else a)[None]

    shards = {tag: [jnp.concatenate([shard_view(w[n], n).astype(BF16) for n in grp], axis=0) for grp in arrays]
              for tag, arrays in GATHER_PHASES.items()}
    reducer = _GradReducer()
    ws, ms, vs = ({n: _small_view(a[n], n) for n in SMALL} for a in (w, m, v))
    loss_part, grad_x, _, small = _device_step(x[0], mem[0], loss_target[0], _WeightGatherer(shards), ws, reducer)

    small_like = [ws[n] for n in SMALL] + [loss_part[0, :1]]
    pack = _pack_small([small[n] for n in SMALL] + [loss_part[0, :1]])
    everyone = _split_start("small_start", _everyone_copies, [pack], [((N_DEV,) + pack.shape, F32)], fanout=N_DEV - 1)

    grads, delta, new_m, new_v = {}, {}, {}, {}
    big_done = []

    def update(groups, reduced, after=()):
        for grp, red in zip(groups, reduced):
            row0 = 0
            for n in grp:
                w_n = shard_view(w[n], n)
                outs = _adamw("adamw_" + n, w_n, red, row0, shard_view(m[n], n), shard_view(v[n], n),
                              after=[everyone[-1], *after])
                grads[n], delta[n], new_m[n], new_v[n] = (shard_unview(o, n) for o in outs)
                big_done.append(outs[1])
                row0 += w_n.shape[0]

    n_a, n_f2 = len(reducer.reduced), 3
    update(REDUCE_GROUPS[n_f2:n_a], reducer.reduced[n_f2:])
    reducer.finish("b", list(big_done))
    update(REDUCE_GROUPS[:n_f2], reducer.reduced[:n_f2], after=reducer.join_start("b"))
    reducer.join_finish("b", big_done[-n_f2:])
    update(REDUCE_GROUPS[n_a:], reducer.reduced[n_a:])

    send_sems, recv_sems, packs, landed, _ = everyone
    packs, landed = _split_wait("small_wait", _everyone_copies, send_sems, recv_sems, packs, landed, big_done)
    mine = 4 * lax.axis_index("x") + 2 * lax.axis_index("y") + lax.axis_index("c")
    summed = _sum_devices(lax.dynamic_update_slice(landed[0], packs[0][None], (mine, 0, 0)))
    g_small = dict(zip(SMALL + ("loss",), _unpack_small(summed, small_like)))
    loss = g_small.pop("loss").reshape(())
    def two_d(a):
        return a.reshape(-1, a.shape[-1])

    updated = _adamw_small(*([two_d(a[n]) for n in SMALL] for a in (ws, g_small, ms, vs)))
    for n, outs in zip(SMALL, updated):
        grads[n], delta[n], new_m[n], new_v[n] = (_small_view(o.reshape(ws[n].shape), n) for o in outs)

    return (loss, grad_x[None], *[grads[n] for n in WEIGHTS], *[delta[n] for n in WEIGHTS],
            *[new_m[n] for n in WEIGHTS], *[new_v[n] for n in WEIGHTS])
```

```python
import functools
import math

import jax
import jax.numpy as jnp
from jax import lax
from jax.experimental import pallas as pl
from jax.experimental.pallas import tpu as pltpu

F32 = jnp.float32
BF16 = jnp.bfloat16
SDS = jax.ShapeDtypeStruct
BS = pl.BlockSpec
MESH = pl.DeviceIdType.MESH

D_MODEL = 1024
D_FF = 2816
N_SHARD = 4
FF_SH = D_FF // N_SHARD
D_POOL = 512
POOL_WINDOWS = (2, 4, 8, 16)
POOL_GROUP = 128
D_SSM = 256
SSM_GROUPS = 16
SSM_GROUP = 16
SSM_STATE = 64
SSM_CH = SSM_GROUPS * SSM_STATE
N_HEADS = 4
HEAD_DIM = 256
EPS = 1e-6
ADAM_LR, ADAM_B1, ADAM_B2, ADAM_EPS, ADAM_WD, ADAM_STEP = 0.001, 0.9, 0.999, 1e-08, 0.01, 10

VMEM_LIMIT_V7X = 58 * 1024 * 1024
TM = 512

NN = (((1,), (0,)), ((), ()))
NT = (((1,), (1,)), ((), ()))
TN = (((0,), (0,)), ((), ()))


def _params(*sem):
    return pltpu.CompilerParams(dimension_semantics=sem if sem else None, vmem_limit_bytes=VMEM_LIMIT_V7X)


def _dot(a, b, dims=NN):
    return lax.dot_general(a.astype(BF16), b.astype(BF16), dims, preferred_element_type=F32)


def _sigmoid(v):
    return pl.reciprocal(1.0 + jnp.exp(-v), approx=True)


def _block_dims(spec):
    return tuple(d for d in spec.block_shape if d is not None)


def _after_operands(after):
    return list(after), [BS(memory_space=pl.ANY)] * len(after)


def _mm(name, pairs, *, grid, out_shape, out_spec, red_axis=None, extras=(), epilogue=None, after=()):
    n_pairs, n_extra = len(pairs), len(extras)
    n_red = grid[red_axis] if red_axis is not None else 1
    dims = [p[4] for p in pairs]

    def body(*refs):
        ab = refs[:2 * n_pairs]
        ex = refs[2 * n_pairs:2 * n_pairs + n_extra]
        o_ref = refs[2 * n_pairs + n_extra + len(after)]

        def partial():
            acc = None
            for p in range(n_pairs):
                t = _dot(ab[2 * p][...], ab[2 * p + 1][...], dims[p])
                acc = t if acc is None else acc + t
            return acc

        def finish(acc):
            res = epilogue(acc, *[e[...] for e in ex]) if epilogue is not None else acc
            o_ref[...] = res.astype(o_ref.dtype)

        if n_red == 1:
            finish(partial())
        else:
            acc_ref = refs[-1]
            k = pl.program_id(red_axis)

            @pl.when(k == 0)
            def _():
                acc_ref[...] = jnp.zeros_like(acc_ref)

            acc_ref[...] += partial()

            @pl.when(k == n_red - 1)
            def _():
                finish(acc_ref[...])

    operands, in_specs = [], []
    for a, a_spec, b, b_spec, _ in pairs:
        operands += [a, b]
        in_specs += [a_spec, b_spec]
    for e, e_spec in extras:
        operands.append(e)
        in_specs.append(e_spec)
    after_ops, after_specs = _after_operands(after)
    operands += after_ops
    in_specs += after_specs
    scratch = [pltpu.VMEM(_block_dims(out_spec), F32)] if n_red > 1 else []
    sem = tuple("arbitrary" if ax == red_axis else "parallel" for ax in range(len(grid)))
    return pl.pallas_call(body, out_shape=out_shape, grid=grid, in_specs=in_specs, out_specs=out_spec,
                          scratch_shapes=scratch, name=name, compiler_params=_params(*sem))(*operands)


def _rmsnorm(name, h, gain, tm, after=()):
    t, d = h.shape
    after_ops, after_specs = _after_operands(after)

    def body(h_ref, g_ref, *rest):
        u_ref = rest[-1]
        hv = h_ref[...]
        r = lax.rsqrt(jnp.mean(hv * hv, axis=-1, keepdims=True) + EPS)
        u_ref[...] = ((hv * r) * g_ref[...]).astype(u_ref.dtype)

    return pl.pallas_call(
        body, out_shape=SDS((t, d), BF16), grid=(t // tm,),
        in_specs=[BS((tm, d), lambda i: (i, 0)), BS((1, d), lambda i: (0, 0))] + after_specs,
        out_specs=BS((tm, d), lambda i: (i, 0)), name=name, compiler_params=_params("parallel"))(h, gain, *after_ops)


def _rmsnorm_bwd(name, h, gain, du, dh_in, tm):
    t, d = h.shape
    has_in = dh_in is not None

    def body(*refs):
        if has_in:
            h_ref, g_ref, du_ref, dhin_ref, dh_ref, dhb_ref, dg_ref = refs
        else:
            h_ref, g_ref, du_ref, dh_ref, dhb_ref, dg_ref = refs
        i = pl.program_id(0)
        hv = h_ref[...]
        r = lax.rsqrt(jnp.mean(hv * hv, axis=-1, keepdims=True) + EPS)
        n = hv * r
        duv = du_ref[...].astype(F32)
        dn = duv * g_ref[...]
        dh = r * (dn - n * jnp.mean(dn * n, axis=-1, keepdims=True))
        if has_in:
            dh = dhin_ref[...] + dh
        dh_ref[...] = dh
        dhb_ref[...] = dh.astype(BF16)

        @pl.when(i == 0)
        def _():
            dg_ref[...] = jnp.zeros_like(dg_ref)

        dg_ref[...] += jnp.sum(duv * n, axis=0, keepdims=True)

    row = BS((tm, d), lambda i: (i, 0))
    vec = BS((1, d), lambda i: (0, 0))
    operands = [h, gain, du] + ([dh_in] if has_in else [])
    in_specs = [row, vec, row] + ([row] if has_in else [])
    return pl.pallas_call(
        body, out_shape=(SDS((t, d), F32), SDS((t, d), BF16), SDS((1, d), F32)), grid=(t // tm,),
        in_specs=in_specs, out_specs=(row, row, vec), name=name, compiler_params=_params("arbitrary"))(*operands)


def _loss_head_tile(i, hv, g_ref, t_ref, loss_ref, dh_ref, dhb_ref, dg_ref):
    g = g_ref[...]
    r = lax.rsqrt(jnp.mean(hv * hv, axis=-1, keepdims=True) + EPS)
    n = hv * r
    err = n * g - t_ref[...]
    dy = err * (1.0 / hv.shape[-1])
    dn = dy * g
    dh = r * (dn - n * jnp.mean(dn * n, axis=-1, keepdims=True))
    dh_ref[...] = dh
    dhb_ref[...] = dh.astype(BF16)

    @pl.when(i == 0)
    def _():
        dg_ref[...] = jnp.zeros_like(dg_ref)
        loss_ref[...] = jnp.zeros_like(loss_ref)

    dg_ref[...] += jnp.sum(dy * n, axis=0, keepdims=True)
    part = 0.5 * jnp.sum(jnp.mean(err * err, axis=-1, keepdims=True), axis=0, keepdims=True)
    loss_ref[...] += jnp.broadcast_to(part, loss_ref.shape)


def _norm_tile(h, g_ref, u_ref):
    r = lax.rsqrt(jnp.mean(h * h, axis=-1, keepdims=True) + EPS)
    u_ref[...] = ((h * r) * g_ref[...]).astype(u_ref.dtype)


FFN_BLOCK = D_FF // 2


def _ffn_up(name, u, w_f, tm, after=()):
    t, d = u.shape
    after_ops, after_specs = _after_operands(after)

    def body(u_ref, wg_ref, wu_ref, *rest):
        pg_ref, pu_ref, a_ref = rest[len(after_ops):]
        uv = u_ref[...]
        for lo in range(0, D_FF, FFN_BLOCK):
            cols = slice(lo, lo + FFN_BLOCK)
            g = _dot(uv, wg_ref[cols, :], NT)
            up = _dot(uv, wu_ref[cols, :], NT)
            sg = _sigmoid(g)
            silu = g * sg
            a_ref[:, cols] = (silu * up).astype(BF16)
            pu_ref[:, cols] = (0.5 * silu).astype(BF16)
            pg_ref[:, cols] = (0.5 * sg * (1.0 + g * (1.0 - sg)) * up).astype(BF16)

    hid = BS((tm, D_FF), lambda i: (i, 0))
    shape = SDS((t, D_FF), BF16)
    whole = BS((D_FF, d), lambda i: (0, 0))
    return pl.pallas_call(
        body, out_shape=(shape, shape, shape), grid=(t // tm,),
        in_specs=[BS((tm, d), lambda i: (i, 0)), whole, whole] + after_specs,
        out_specs=(hid, hid, hid), name=name,
        compiler_params=_params("parallel"))(u, w_f["gate"], w_f["up"], *after_ops)


def _ffn_down(name, a, w_f, resid, tm, next_gain=None, head=None):
    t, d = resid.shape
    row = BS((tm, d), lambda i: (i, 0))
    vec = BS((1, d), lambda i: (0, 0))

    def body(a_ref, w_ref, res_ref, *rest):
        h = res_ref[...] + 0.5 * _dot(a_ref[...], w_ref[...])
        if head is not None:
            _loss_head_tile(pl.program_id(0), h, *rest)
        else:
            g_ref, h_ref, u_ref = rest
            h_ref[...] = h
            _norm_tile(h, g_ref, u_ref)

    if head is not None:
        extra, extra_specs = list(head), [vec, row]
        out_shape = (SDS((1, 128), F32), SDS((t, d), F32), SDS((t, d), BF16), SDS((1, d), F32))
        out_specs = (BS((1, 128), lambda i: (0, 0)), row, row, vec)
    else:
        extra, extra_specs = [next_gain], [vec]
        out_shape = (SDS((t, d), F32), SDS((t, d), BF16))
        out_specs = (row, row)
    return pl.pallas_call(
        body, out_shape=out_shape, grid=(t // tm,),
        in_specs=[BS((tm, D_FF), lambda i: (i, 0)), BS((D_FF, d), lambda i: (0, 0)), row] + extra_specs,
        out_specs=out_specs, name=name,
        compiler_params=_params("arbitrary" if head is not None else "parallel"))(a, w_f["down"], resid, *extra)


def _mix_in(u, w_t, tm, after=()):
    t, d = u.shape
    after_ops, after_specs = _after_operands(after)

    def body(u_ref, w_ref, *rest):
        o_ref, s_ref = rest[-2:]
        uv = u_ref[...]
        for lo in range(0, D_FF, FFN_BLOCK):
            o_ref[:, lo:lo + FFN_BLOCK] = _dot(uv, w_ref[lo:lo + FFN_BLOCK, :], NT)
        s_ref[...] = o_ref[:, D_POOL:D_POOL + D_SSM].astype(BF16)

    return pl.pallas_call(
        body, out_shape=(SDS((t, D_FF), F32), SDS((t, D_SSM), BF16)), grid=(t // tm,),
        in_specs=[BS((tm, d), lambda i: (i, 0)), BS((D_FF, d), lambda i: (0, 0))] + after_specs,
        out_specs=(BS((tm, D_FF), lambda i: (i, 0)), BS((tm, D_SSM), lambda i: (i, 0))), name="mix_in",
        compiler_params=_params("parallel"))(u, w_t, *after_ops)


def _mm_resid_norm(name, a, b, resid, next_gain, tm):
    t, d = resid.shape
    tm = min(2 * tm, t)

    def body(a_ref, b_ref, res_ref, g_ref, h_ref, u_ref):
        h = res_ref[...] + _dot(a_ref[...], b_ref[...])
        h_ref[...] = h
        _norm_tile(h, g_ref, u_ref)

    row = BS((tm, d), lambda i: (i, 0))
    return pl.pallas_call(
        body, out_shape=(SDS((t, d), F32), SDS((t, d), BF16)), grid=(t // tm,),
        in_specs=[BS((tm, a.shape[1]), lambda i: (i, 0)), BS(b.shape, lambda i: (0, 0)), row, BS((1, d), lambda i: (0, 0))],
        out_specs=(row, row), name=name, compiler_params=_params("parallel"))(a, b, resid, next_gain)


def _ffn_bwd_act(name, dh_b, w_f, pg, pu, tm, after=()):
    t, d = dh_b.shape
    after_ops, after_specs = _after_operands(after)

    def body(dh_ref, wd_ref, pg_ref, pu_ref, *rest):
        dg_ref, dup_ref = rest[len(after_ops):]
        dh = dh_ref[...]
        for lo in range(0, D_FF, FFN_BLOCK):
            cols = slice(lo, lo + FFN_BLOCK)
            da = _dot(dh, wd_ref[cols, :], NT)
            dg_ref[:, cols] = (da * pg_ref[:, cols].astype(F32)).astype(BF16)
            dup_ref[:, cols] = (da * pu_ref[:, cols].astype(F32)).astype(BF16)

    hid = BS((tm, D_FF), lambda i: (i, 0))
    shape = SDS((t, D_FF), BF16)
    return pl.pallas_call(
        body, out_shape=(shape, shape), grid=(t // tm,),
        in_specs=[BS((tm, d), lambda i: (i, 0)), BS((D_FF, d), lambda i: (0, 0)), hid, hid] + after_specs,
        out_specs=(hid, hid), name=name,
        compiler_params=_params("parallel"))(dh_b, w_f["down"], pg, pu, *after_ops)


def _ffn_dw(name, u, dg, dup, a, dh_b, tm):
    t, d = u.shape
    n_t = t // tm

    def body(u_ref, dg_ref, dup_ref, a_ref, dh_ref, og_ref, ou_ref, od_ref, acc):
        i = pl.program_id(1)

        @pl.when(i == 0)
        def _():
            acc[...] = jnp.zeros_like(acc)

        uv = u_ref[...]
        acc[0] += _dot(dg_ref[...], uv, TN)
        acc[1] += _dot(dup_ref[...], uv, TN)
        acc[2] += _dot(a_ref[...], dh_ref[...], TN)

        @pl.when(i == n_t - 1)
        def _():
            og_ref[...] = acc[0].astype(BF16)
            ou_ref[...] = acc[1].astype(BF16)
            od_ref[...] = (0.5 * acc[2]).astype(BF16)

    hid = BS((tm, FFN_BLOCK), lambda j, i: (i, j))
    row = BS((tm, d), lambda j, i: (i, 0))
    out = BS((FFN_BLOCK, d), lambda j, i: (j, 0))
    shape = SDS((D_FF, d), BF16)
    return pl.pallas_call(
        body, out_shape=(shape, shape, shape), grid=(D_FF // FFN_BLOCK, n_t),
        in_specs=[row, hid, hid, hid, row], out_specs=(out, out, out),
        scratch_shapes=[pltpu.VMEM((3, FFN_BLOCK, d), F32)],
        name=name, compiler_params=_params("parallel", "arbitrary"))(u, dg, dup, a, dh_b)


def _norm_bwd_tile(i, du, h_ref, g_ref, dhin_ref, dh_ref, dhb_ref, dg_ref):
    hv = h_ref[...]
    r = lax.rsqrt(jnp.mean(hv * hv, axis=-1, keepdims=True) + EPS)
    n = hv * r
    dn = du * g_ref[...]
    dh = dhin_ref[...] + r * (dn - n * jnp.mean(dn * n, axis=-1, keepdims=True))
    dh_ref[...] = dh
    dhb_ref[...] = dh.astype(BF16)

    @pl.when(i == 0)
    def _():
        dg_ref[...] = jnp.zeros_like(dg_ref)

    dg_ref[...] += jnp.sum(du * n, axis=0, keepdims=True)


def _norm_bwd_specs(tm):
    row = BS((tm, D_MODEL), lambda i: (i, 0))
    vec = BS((1, D_MODEL), lambda i: (0, 0))
    return [row, vec, row], (row, row, vec)


def _norm_bwd_shapes(t):
    return SDS((t, D_MODEL), F32), SDS((t, D_MODEL), BF16), SDS((1, D_MODEL), F32)


def _ffn_dx(name, dg, dup, w_f, h, gain, dh_in, tm, after=()):
    t = dg.shape[0]
    tm = tm // 2
    after_ops, after_specs = _after_operands(after)

    def body(dg_ref, dup_ref, wg_ref, wu_ref, h_ref, g_ref, dhin_ref, *rest):
        du = _dot(dg_ref[...], wg_ref[...]) + _dot(dup_ref[...], wu_ref[...])
        _norm_bwd_tile(pl.program_id(0), du, h_ref, g_ref, dhin_ref, *rest[len(after_ops):])

    hid = BS((tm, D_FF), lambda i: (i, 0))
    whole = BS((D_FF, D_MODEL), lambda i: (0, 0))
    norm_in, norm_out = _norm_bwd_specs(tm)
    return pl.pallas_call(
        body, out_shape=_norm_bwd_shapes(t), grid=(t // tm,),
        in_specs=[hid, hid, whole, whole] + norm_in + after_specs, out_specs=norm_out, name=name,
        compiler_params=_params("arbitrary"))(dg, dup, w_f["gate"], w_f["up"], h, gain, dh_in, *after_ops)


def _mm_norm_bwd(name, a, b, dims, h, gain, dh_in, tm, after=()):
    pieces = list(a) if isinstance(a, (list, tuple)) else [a]
    assert len(pieces) == 1 or dims == NN
    t = pieces[0].shape[0]
    widths = [p.shape[1] for p in pieces]
    row0 = [sum(widths[:j]) for j in range(len(pieces))]
    after_ops, after_specs = _after_operands(after)

    def body(*refs):
        a_refs, (b_ref, h_ref, g_ref, dhin_ref) = refs[:len(pieces)], refs[len(pieces):len(pieces) + 4]
        outs = refs[len(pieces) + 4 + len(after_ops):]
        if len(pieces) == 1:
            du = _dot(a_refs[0][...], b_ref[...], dims)
        else:
            du = _dot(a_refs[0][...], b_ref[0:widths[0], :])
            for a_ref, r0, w in zip(a_refs[1:], row0[1:], widths[1:]):
                du = du + _dot(a_ref[...], b_ref[r0:r0 + w, :])
        _norm_bwd_tile(pl.program_id(0), du, h_ref, g_ref, dhin_ref, *outs)

    norm_in, norm_out = _norm_bwd_specs(tm)
    return pl.pallas_call(
        body, out_shape=_norm_bwd_shapes(t), grid=(t // tm,),
        in_specs=[BS((tm, w), lambda i: (i, 0)) for w in widths] + [BS(b.shape, lambda i: (0, 0))] + norm_in + after_specs,
        out_specs=norm_out, name=name, compiler_params=_params("arbitrary"))(*pieces, b, h, gain, dh_in, *after_ops)


def _plain_mm(name, a, b, dims, out_dtype, tm, resid=None, after=()):
    t = a.shape[0]
    tm = min(2 * tm, t)
    n = b.shape[1] if dims == NN else b.shape[0]
    extras = [(resid, BS((tm, n), lambda i: (i, 0)))] if resid is not None else []
    epi = (lambda acc, res: res + acc) if resid is not None else None
    return _mm(name, [(a, BS((tm, a.shape[1]), lambda i: (i, 0)), b, BS(b.shape, lambda i: (0, 0)), dims)],
               grid=(t // tm,), out_shape=SDS((t, n), out_dtype), out_spec=BS((tm, n), lambda i: (i, 0)),
               extras=extras, epilogue=epi, after=after)


def _dw_mm(name, a, b, tm, out_dtype=BF16, after=()):
    t, k = a.shape
    n = b.shape[1]
    tm = min(2 * tm, t)
    return _mm(name, [(a, BS((tm, k), lambda i: (i, 0)), b, BS((tm, n), lambda i: (i, 0)), TN)],
               grid=(t // tm,), red_axis=0, out_shape=SDS((k, n), out_dtype), out_spec=BS((k, n), lambda i: (0, 0)),
               after=after)


def _mix_in_dw(pieces, u, tm):
    t, d = u.shape
    tm = min(2 * tm, t)
    n_steps = t // tm
    widths = [p.shape[1] for p in pieces]
    row0 = [sum(widths[:j]) for j in range(len(pieces))]
    assert sum(widths) == D_FF

    def body(*refs):
        p_refs, u_ref, o_ref, acc_ref = refs[:len(pieces)], refs[len(pieces)], refs[-2], refs[-1]
        k = pl.program_id(0)

        @pl.when(k == 0)
        def _():
            acc_ref[...] = jnp.zeros_like(acc_ref)

        uv = u_ref[...]
        for p_ref, r0, w in zip(p_refs, row0, widths):
            acc_ref[r0:r0 + w, :] += _dot(p_ref[...], uv, TN)

        @pl.when(k == n_steps - 1)
        def _():
            o_ref[...] = acc_ref[...].astype(BF16)

    return pl.pallas_call(
        body, out_shape=SDS((D_FF, d), BF16), grid=(n_steps,),
        in_specs=[BS((tm, w), lambda i: (i, 0)) for w in widths] + [BS((tm, d), lambda i: (i, 0))],
        out_specs=BS((D_FF, d), lambda i: (0, 0)), scratch_shapes=[pltpu.VMEM((D_FF, d), F32)],
        name="mix_in_dw", compiler_params=_params("arbitrary"))(*pieces, u)


POOL_CHUNK = 256
POOL_HALO = 8


def _window_sum(v, width, lead):
    n = v.shape[0]
    s = v
    k = 1
    while k < width:
        s = s + pltpu.roll(s, n - k, 0)
        k *= 2
    return pltpu.roll(s, lead, 0) if lead else s


def _pool_count(base, left, right, t, shape):
    pos = base + lax.broadcasted_iota(jnp.int32, shape, 0)
    lo = jnp.maximum(pos - left, 0)
    hi = jnp.minimum(pos + right + 1, t)
    return (hi - lo).astype(F32)


def _pool_fwd(proj, pool_w, pool_scale):
    t = proj.shape[0]
    c, h = POOL_CHUNK, POOL_HALO
    n_chunks = t // c

    def body(proj_hbm, pw_ref, sc_ref, pooled_ref, mixed_ref, ms_ref, pad_ref, sem):
        cp = pltpu.make_async_copy(proj_hbm.at[:, pl.ds(0, D_POOL)], pad_ref.at[pl.ds(h, t), :], sem)
        cp.start()
        pad_ref[pl.ds(0, h), :] = jnp.zeros((h, D_POOL), F32)
        pad_ref[pl.ds(t + h, h), :] = jnp.zeros((h, D_POOL), F32)
        cp.wait()
        for g, width in enumerate(POOL_WINDOWS):
            left = width // 2
            right = width - 1 - left
            cols = slice(g * POOL_GROUP, (g + 1) * POOL_GROUP)
            wmat = pw_ref[g].astype(BF16)
            scale = sc_ref[:, cols]

            def chunk(ci, carry, left=left, right=right, width=width, cols=cols, wmat=wmat, scale=scale):
                base = pl.multiple_of(ci * c, c)
                v = pad_ref[pl.ds(base, c + 2 * h), cols]
                win = _window_sum(v, width, left)[h:h + c]
                cnt = _pool_count(base, left, right, t, (c, POOL_GROUP))
                pooled = (win / cnt - v[h:h + c]).astype(BF16)
                mixed = _dot(pooled, wmat)
                pooled_ref[pl.ds(base, c), cols] = pooled
                mixed_ref[pl.ds(base, c), cols] = mixed.astype(BF16)
                ms_ref[pl.ds(base, c), cols] = (mixed * scale).astype(BF16)
                return carry

            lax.fori_loop(0, n_chunks, chunk, 0)

    vm = BS(memory_space=pltpu.VMEM)
    shape = SDS((t, D_POOL), BF16)
    return pl.pallas_call(
        body, out_shape=(shape, shape, shape),
        in_specs=[BS(memory_space=pl.ANY), vm, vm], out_specs=(vm, vm, vm),
        scratch_shapes=[pltpu.VMEM((t + 2 * h, D_POOL), F32), pltpu.SemaphoreType.DMA],
        name="pool_fwd", compiler_params=_params())(proj, pool_w, pool_scale)


def _pool_bwd(d_ms, mixed, pooled, pool_w, pool_scale):
    t = d_ms.shape[0]
    c, h = POOL_CHUNK, POOL_HALO
    n_chunks = t // c

    def body(dms_ref, mixed_ref, pooled_ref, pw_ref, sc_ref, dp_ref, dsc_ref, dpw_ref, pad_ref):
        pad_ref[pl.ds(0, h), :] = jnp.zeros((h, D_POOL), F32)
        pad_ref[pl.ds(t + h, h), :] = jnp.zeros((h, D_POOL), F32)
        for g, width in enumerate(POOL_WINDOWS):
            left = width // 2
            right = width - 1 - left
            cols = slice(g * POOL_GROUP, (g + 1) * POOL_GROUP)
            wmat = pw_ref[g].astype(BF16)
            scale = sc_ref[:, cols]

            def first(ci, carry, left=left, right=right, cols=cols, wmat=wmat, scale=scale):
                dsc, dpw = carry
                base = pl.multiple_of(ci * c, c)
                dms = dms_ref[pl.ds(base, c), cols].astype(F32)
                dsc = dsc + jnp.sum(dms * mixed_ref[pl.ds(base, c), cols].astype(F32), axis=0, keepdims=True)
                dmix = (dms * scale).astype(BF16)
                dpw = dpw + _dot(pooled_ref[pl.ds(base, c), cols], dmix, TN)
                dpooled = _dot(dmix, wmat, NT)
                cnt = _pool_count(base, left, right, t, (c, POOL_GROUP))
                pad_ref[pl.ds(base + h, c), cols] = dpooled / cnt
                return dsc, dpw

            dsc, dpw = lax.fori_loop(0, n_chunks, first,
                                     (jnp.zeros((1, POOL_GROUP), F32), jnp.zeros((POOL_GROUP, POOL_GROUP), F32)))
            dsc_ref[:, cols] = dsc
            dpw_ref[g] = dpw

            def second(ci, carry, left=left, right=right, width=width, cols=cols):
                base = pl.multiple_of(ci * c, c)
                v = pad_ref[pl.ds(base, c + 2 * h), cols]
                win = _window_sum(v, width, right)[h:h + c]
                cnt = _pool_count(base, left, right, t, (c, POOL_GROUP))
                dp_ref[pl.ds(base, c), cols] = (win - v[h:h + c] * cnt).astype(BF16)
                return carry

            lax.fori_loop(0, n_chunks, second, 0)

    vm = BS(memory_space=pltpu.VMEM)
    return pl.pallas_call(
        body, out_shape=(SDS((t, D_POOL), BF16), SDS((1, D_POOL), F32), SDS((4, POOL_GROUP, POOL_GROUP), F32)),
        in_specs=[vm] * 5, out_specs=(vm, vm, vm),
        scratch_shapes=[pltpu.VMEM((t + 2 * h, D_POOL), F32)],
        name="pool_bwd", compiler_params=_params())(d_ms, mixed, pooled, pool_w, pool_scale)


SSM_ROWS = 2 * SSM_GROUPS * SSM_GROUP
SSM_HALF = SSM_GROUPS * SSM_GROUP


def _ssm_zoh(a_r, a_i, ldt):
    dt = jnp.exp(ldt)
    mag = jnp.exp(dt * a_r)
    ang = dt * a_i
    cs, sn = jnp.cos(ang), jnp.sin(ang)
    abr, abi = mag * cs, mag * sn
    den = a_r * a_r + a_i * a_i
    nr = abr - 1.0
    qr = (nr * a_r + abi * a_i) / den
    qi = (abi * a_r - nr * a_i) / den
    return dt, mag, cs, sn, abr, abi, den, nr, qr, qi


def _ssm_group_mask():
    row = lax.broadcasted_iota(jnp.int32, (SSM_HALF, SSM_CH), 0)
    col = lax.broadcasted_iota(jnp.int32, (SSM_HALF, SSM_CH), 1)
    return (row // SSM_GROUP) == (col // SSM_STATE)


def _ssm_prep(a_r, a_i, ldt, b_r, b_i, c_r, c_i, after=()):
    after_ops, after_specs = _after_operands(after)

    def body(ar_ref, ai_ref, ldt_ref, br_ref, bi_ref, cr_ref, ci_ref, *rest):
        abr_ref, abi_ref, win_ref, wint_ref, woutt_ref, wout_ref = rest[len(after_ops):]
        *_, abr, abi, _, _, qr, qi = _ssm_zoh(ar_ref[...], ai_ref[...], ldt_ref[...])
        abr_ref[...] = abr
        abi_ref[...] = abi
        b_r, b_i = br_ref[...], bi_ref[...]
        bbr = qr * b_r - qi * b_i
        bbi = qr * b_i + qi * b_r
        mask = _ssm_group_mask()
        state = lax.broadcasted_iota(jnp.int32, (SSM_STATE, SSM_CH), 0)
        col = lax.broadcasted_iota(jnp.int32, (SSM_STATE, SSM_CH), 1)
        every_group = (col % SSM_STATE == state).astype(BF16)

        def spread(x):
            return jnp.where(mask, _dot(x, every_group), 0.0)

        for d in range(2):
            rows = slice(d * SSM_HALF, (d + 1) * SSM_HALF)
            for half, x_in, x_out in ((0, bbr[rows], cr_ref[rows, :]), (1, bbi[rows], -ci_ref[rows, :])):
                cols = slice(half * SSM_CH, (half + 1) * SSM_CH)
                m_in, m_out = spread(x_in), spread(x_out)
                win_ref[d, :, cols] = m_in.astype(BF16)
                wint_ref[d, cols, :] = m_in.T.astype(BF16)
                woutt_ref[d, :, cols] = m_out.astype(BF16)
                wout_ref[d, cols, :] = m_out.T.astype(BF16)

    vm = BS(memory_space=pltpu.VMEM)
    vec = SDS((SSM_ROWS, SSM_STATE), F32)
    wide = SDS((2, SSM_HALF, 2 * SSM_CH), BF16)
    tall = SDS((2, 2 * SSM_CH, SSM_HALF), BF16)
    return pl.pallas_call(body, out_shape=(vec, vec, wide, tall, wide, tall), in_specs=[vm] * 7 + after_specs,
                          out_specs=(vm,) * 6, name="ssm_prep",
                          compiler_params=_params())(a_r, a_i, ldt, b_r, b_i, c_r, c_i, *after_ops)


def _ssm_prep_bwd(a_r, a_i, ldt, b_r, b_i, d_abr, d_abi, d_win, d_woutt):
    def body(ar_ref, ai_ref, ldt_ref, br_ref, bi_ref, *rest):
        (dabr_refs, dabi_refs, dwin_refs, dwoutt_refs), outs = [rest[2 * k:2 * k + 2] for k in range(4)], rest[8:]
        dar_ref, dai_ref, dldt_ref, dbr_ref, dbi_ref, dcr_ref, dci_ref = outs
        a_r, a_i = ar_ref[...], ai_ref[...]
        dt, mag, cs, sn, abr, abi, den, nr, qr, qi = _ssm_zoh(a_r, a_i, ldt_ref[...])
        mask = _ssm_group_mask()
        col = lax.broadcasted_iota(jnp.int32, (SSM_CH, SSM_STATE), 0)
        state = lax.broadcasted_iota(jnp.int32, (SSM_CH, SSM_STATE), 1)
        own_state = (col % SSM_STATE == state).astype(BF16)

        def pick(dense):
            m = jnp.where(mask, dense, 0.0)
            hi = m.astype(BF16)
            lo = m - hi.astype(F32)
            return _dot(hi, own_state) + _dot(lo, own_state)

        def picked(refs, half):
            cols = slice(half * SSM_CH, (half + 1) * SSM_CH)
            return jnp.concatenate([pick(ref[:, cols]) for ref in refs], axis=0)

        first_channel = lax.broadcasted_iota(jnp.int32, (SSM_HALF, SSM_CH), 0) % SSM_GROUP == 0

        def first_rows(refs):
            return jnp.concatenate(
                [pick(jnp.where(first_channel, jnp.broadcast_to(ref[...], (SSM_HALF, SSM_CH)), 0.0)) for ref in refs], axis=0)

        g_r, g_i = picked(dwin_refs, 0), picked(dwin_refs, 1)
        dcr_ref[...] = picked(dwoutt_refs, 0)
        dci_ref[...] = -picked(dwoutt_refs, 1)
        b_r, b_i = br_ref[...], bi_ref[...]
        dbr_ref[...] = g_r * qr + g_i * qi
        dbi_ref[...] = g_i * qr - g_r * qi
        gqr = g_r * b_r + g_i * b_i
        gqi = g_i * b_r - g_r * b_i
        g_nr_num = gqr / den
        g_ni_num = gqi / den
        g_den = -(gqr * qr + gqi * qi) / den
        g_nr = g_nr_num * a_r - g_ni_num * a_i
        g_abi = g_nr_num * a_i + g_ni_num * a_r
        d_ar = g_nr_num * nr + g_ni_num * abi + 2.0 * a_r * g_den
        d_ai = g_nr_num * abi - g_ni_num * nr + 2.0 * a_i * g_den
        g_abr = first_rows(dabr_refs) + g_nr
        g_abi = first_rows(dabi_refs) + g_abi
        g_mag = g_abr * cs + g_abi * sn
        g_ang = mag * (g_abi * cs - g_abr * sn)
        g_e = g_mag * mag
        d_ar = d_ar + g_e * dt
        d_ai = d_ai + g_ang * dt
        g_dt = g_e * a_r + g_ang * a_i
        dar_ref[...] = d_ar
        dai_ref[...] = d_ai
        dldt_ref[...] = g_dt * dt

    vm = BS(memory_space=pltpu.VMEM)
    vec = SDS((SSM_ROWS, SSM_STATE), F32)
    return pl.pallas_call(body, out_shape=(vec,) * 7, in_specs=[vm] * 13, out_specs=(vm,) * 7, name="ssm_prep_bwd",
                          compiler_params=_params())(a_r, a_i, ldt, b_r, b_i, *d_abr, *d_abi, *d_win, *d_woutt)


SCAN_ROWS = 512
SCAN_SUB = 128


def _ssm_scan(name, inp, w1, a_r, a_i, w2, reverse, dr, conj=False):
    t = inp.shape[0]
    rows = min(SCAN_ROWS, t)
    n = t // rows
    n_sub = rows // SCAN_SUB
    ch = SSM_CH
    at = (lambda i: (n - 1 - i, 0)) if reverse else (lambda i: (i, 0))

    def body(in_ref, w1_ref, ar_ref, ai_ref, w2_ref, sb_ref, out_ref, cr_ref, ci_ref, k_ref, st_ref):
        i = pl.program_id(0)

        @pl.when(i == 0)
        def _():
            ar8 = jnp.broadcast_to(ar_ref[...], (8, ch))
            ai8 = jnp.broadcast_to(-ai_ref[...] if conj else ai_ref[...], (8, ch))
            row = lax.broadcasted_iota(jnp.int32, (8, ch), 0)
            rank = (7 - row) if reverse else row
            powers = [(ar8, ai8)]
            for _ in range(7):
                p_r, p_i = powers[-1]
                powers.append((p_r * ar8 - p_i * ai8, p_r * ai8 + p_i * ar8))
            zero = jnp.zeros((8, ch), F32)
            for slot, k in enumerate((1, 2, 4)):
                k_ref[2 * slot] = jnp.where(rank >= k, powers[k - 1][0], zero)
                k_ref[2 * slot + 1] = jnp.where(rank >= k, powers[k - 1][1], zero)
            carry_r, carry_i = zero, zero
            for j in range(8):
                carry_r = jnp.where(rank == j, powers[j][0], carry_r)
                carry_i = jnp.where(rank == j, powers[j][1], carry_i)
            k_ref[6] = carry_r
            k_ref[7] = carry_i
            cr_ref[...] = zero
            ci_ref[...] = zero

        def group(r0, carry):
            c_r, c_i = carry
            x_r = st_ref[pl.ds(r0, 8), 0:ch]
            x_i = st_ref[pl.ds(r0, 8), ch:2 * ch]
            for slot, k in enumerate((1, 2, 4)):
                shift = (8 - k) if reverse else k
                s_r = pltpu.roll(x_r, shift, 0)
                s_i = pltpu.roll(x_i, shift, 0)
                m_r, m_i = k_ref[2 * slot], k_ref[2 * slot + 1]
                x_r, x_i = x_r + m_r * s_r - m_i * s_i, x_i + m_r * s_i + m_i * s_r
            p_r, p_i = k_ref[6], k_ref[7]
            x_r, x_i = x_r + p_r * c_r - p_i * c_i, x_i + p_r * c_i + p_i * c_r
            st_ref[pl.ds(r0, 8), 0:ch] = x_r
            st_ref[pl.ds(r0, 8), ch:2 * ch] = x_i
            last = 0 if reverse else 7
            return (jnp.broadcast_to(x_r[last:last + 1, :], (8, ch)), jnp.broadcast_to(x_i[last:last + 1, :], (8, ch)))

        carry = (cr_ref[...], ci_ref[...])
        for sc in (range(n_sub - 1, -1, -1) if reverse else range(n_sub)):
            part = pl.ds(sc * SCAN_SUB, SCAN_SUB)
            st_ref[part, :] = _dot(in_ref[part, :], w1_ref[...])
            for gi in range(SCAN_SUB // 8):
                g = (SCAN_SUB // 8 - 1 - gi) if reverse else gi
                carry = group(sc * SCAN_SUB + g * 8, carry)
            states = st_ref[part, :].astype(BF16)
            sb_ref[part, :] = states
            out_ref[part, :] = _dot(states, w2_ref[...])
        cr_ref[...] = carry[0]
        ci_ref[...] = carry[1]

    return pl.pallas_call(
        body, out_shape=(SDS((t, 2 * ch), BF16), SDS((t, D_SSM), F32)), grid=(n,),
        in_specs=[BS((rows, D_SSM), at), BS((None, D_SSM, 2 * ch), lambda i: (dr, 0, 0)), BS((None, 1, ch), lambda i: (dr, 0, 0)),
                  BS((None, 1, ch), lambda i: (dr, 0, 0)), BS((None, 2 * ch, D_SSM), lambda i: (dr, 0, 0))],
        out_specs=(BS((rows, 2 * ch), at), BS((rows, D_SSM), at)),
        scratch_shapes=[pltpu.VMEM((8, ch), F32), pltpu.VMEM((8, ch), F32), pltpu.VMEM((8, 8, ch), F32),
                        pltpu.VMEM((rows, 2 * ch), F32)],
        name=name, compiler_params=_params("arbitrary"))(inp, w1, a_r, a_i, w2)


DA_ROWS = 1024


def _ssm_param_grads(name, lam, states, u, dy, reverse, after=()):
    t = lam.shape[0]
    rows = min(DA_ROWS, t)
    n = t // rows
    halo_rows = 16
    nb = rows // halo_rows
    ch = SSM_CH
    if reverse:
        halo_at = lambda i: (jnp.minimum((i + 1) * nb, t // halo_rows - 1), 0)
    else:
        halo_at = lambda i: (jnp.maximum(i * nb - 1, 0), 0)

    after_ops, after_specs = _after_operands(after)

    def body(lam_ref, x_ref, halo_ref, u_ref, dy_ref, *rest):
        dr_ref, di_ref, dwin_ref, dwoutt_ref = rest[len(after_ops):]
        i = pl.program_id(0)

        @pl.when(i == 0)
        def _():
            dr_ref[...] = jnp.zeros_like(dr_ref)
            di_ref[...] = jnp.zeros_like(di_ref)
            dwin_ref[...] = jnp.zeros_like(dwin_ref)
            dwoutt_ref[...] = jnp.zeros_like(dwoutt_ref)

        dwin_ref[...] += _dot(u_ref[...], lam_ref[...], TN)
        dwoutt_ref[...] += _dot(dy_ref[...], x_ref[...], TN)
        row = lax.broadcasted_iota(jnp.int32, (rows, ch), 0)
        if reverse:
            edge, shift, h_row, live = rows - 1, rows - 1, 0, i < n - 1
        else:
            edge, shift, h_row, live = 0, 1, halo_rows - 1, i > 0

        def neighbour(lo):
            halo = halo_ref[:, lo:lo + ch].astype(F32)[h_row:h_row + 1]
            halo = jnp.where(live, halo, 0.0)
            x = x_ref[:, lo:lo + ch].astype(F32)
            return jnp.where(row == edge, jnp.broadcast_to(halo, (rows, ch)), pltpu.roll(x, shift, 0))

        xp_r, xp_i = neighbour(0), neighbour(ch)
        l_r, l_i = lam_ref[:, 0:ch].astype(F32), lam_ref[:, ch:2 * ch].astype(F32)
        dr_ref[...] += jnp.sum(l_r * xp_r + l_i * xp_i, axis=0, keepdims=True)
        di_ref[...] += jnp.sum(l_i * xp_r - l_r * xp_i, axis=0, keepdims=True)

    blk = BS((rows, 2 * ch), lambda i: (i, 0))
    thin = BS((rows, D_SSM), lambda i: (i, 0))
    vec = BS((1, ch), lambda i: (0, 0))
    mat = BS((D_SSM, 2 * ch), lambda i: (0, 0))
    return pl.pallas_call(
        body, out_shape=(SDS((1, ch), F32), SDS((1, ch), F32), SDS((D_SSM, 2 * ch), F32), SDS((D_SSM, 2 * ch), F32)),
        grid=(n,), in_specs=[blk, blk, BS((halo_rows, 2 * ch), halo_at), thin, thin] + after_specs,
        out_specs=(vec, vec, mat, mat),
        name=name, compiler_params=_params("arbitrary"))(lam, states, states, u, dy, *after_ops)


GELU_C = math.sqrt(2.0 / math.pi)
GELU_K = 0.044715


def _ssm_combine(proj, y_fwd, y_bwd, d_skip, tm, after=()):
    t = proj.shape[0]
    after_ops, after_specs = _after_operands(after)

    def body(s_ref, yf_ref, yb_ref, d_ref, *rest):
        yt_ref, g_ref = rest[len(after_ops):]
        y = s_ref[...] * d_ref[...] + yf_ref[...] + yb_ref[...]
        yt_ref[...] = y
        th = jnp.tanh(GELU_C * (y + GELU_K * y * y * y))
        g_ref[...] = (0.5 * y * (1.0 + th)).astype(BF16)

    blk = BS((tm, D_SSM), lambda i: (i, 0))
    return pl.pallas_call(
        body, out_shape=(SDS((t, D_SSM), F32), SDS((t, D_SSM), BF16)), grid=(t // tm,),
        in_specs=[BS((tm, D_SSM), lambda i: (i, D_POOL // D_SSM)), blk, blk, BS((1, D_SSM), lambda i: (0, 0))] + after_specs,
        out_specs=(blk, blk), name="ssm_combine",
        compiler_params=_params("parallel"))(proj, y_fwd, y_bwd, d_skip, *after_ops)


def _ssm_ds(proj, d_yt, du_fwd, du_bwd, d_skip, tm):
    t = proj.shape[0]

    def body(s_ref, dy_ref, duf_ref, dub_ref, d_ref, ds_ref, dd_ref):
        i = pl.program_id(0)
        dy = dy_ref[...]
        ds_ref[...] = (dy * d_ref[...] + duf_ref[...] + dub_ref[...]).astype(BF16)

        @pl.when(i == 0)
        def _():
            dd_ref[...] = jnp.zeros_like(dd_ref)

        dd_ref[...] += jnp.sum(dy * s_ref[...], axis=0, keepdims=True)

    blk = BS((tm, D_SSM), lambda i: (i, 0))
    vec = BS((1, D_SSM), lambda i: (0, 0))
    return pl.pallas_call(
        body, out_shape=(SDS((t, D_SSM), BF16), SDS((1, D_SSM), F32)), grid=(t // tm,),
        in_specs=[BS((tm, D_SSM), lambda i: (i, D_POOL // D_SSM)), blk, blk, blk, vec],
        out_specs=(blk, vec), name="ssm_ds", compiler_params=_params("arbitrary"))(proj, d_yt, du_fwd, du_bwd, d_skip)


G_POOL_AT = D_POOL + D_SSM
G_SSM_AT = G_POOL_AT + D_MODEL
E_VAL, E_GATE = D_POOL, D_POOL + D_SSM


def _merge_specs(tm):
    return [BS((tm, D_POOL), lambda i: (i, 0)), BS((tm, D_SSM), lambda i: (i, 0)),
            BS((N_SHARD, 1024, 256), lambda i: (0, 0, 0)), BS((tm, D_FF), lambda i: (i, 0))]


def _merge_parts(s, ms, yv, w_ref, proj_ref):
    lo = 256 * s
    zp = _dot(ms, w_ref[s, 0:E_VAL, :])
    zv = _dot(yv, w_ref[s, E_VAL:E_GATE, :])
    zg = _dot(yv, w_ref[s, E_GATE:, :])
    return zp, zv, zg, proj_ref[:, G_POOL_AT + lo:G_POOL_AT + lo + 256], proj_ref[:, G_SSM_AT + lo:G_SSM_AT + lo + 256]


def _mixer_merge(ms, yssm, w_e, proj, tm):
    t = ms.shape[0]

    def body(ms_ref, y_ref, w_ref, proj_ref, o_ref):
        msv, yv = ms_ref[...], y_ref[...]
        for s in range(N_SHARD):
            zp, zv, zg, gp, gs = _merge_parts(s, msv, yv, w_ref, proj_ref)
            o_ref[:, 256 * s:256 * (s + 1)] = (_sigmoid(gp) * zp + _sigmoid(gs) * zv * _sigmoid(zg)).astype(BF16)

    row = BS((tm, D_MODEL), lambda i: (i, 0))
    return pl.pallas_call(
        body, out_shape=SDS((t, D_MODEL), BF16), grid=(t // tm,), in_specs=_merge_specs(tm), out_specs=row,
        name="mixer_merge", compiler_params=_params("parallel"))(ms, yssm, w_e, proj)


def _mixer_merge_bwd(ms, yssm, w_e, proj, dmerged, tm):
    t = ms.shape[0]

    def body(ms_ref, y_ref, w_ref, proj_ref, dm_ref, dgp_ref, dgs_ref, dzp_ref, dzv_ref, dzg_ref):
        msv, yv = ms_ref[...], y_ref[...]
        for s in range(N_SHARD):
            cols = slice(256 * s, 256 * (s + 1))
            zp, zv, zg, gp, gs = _merge_parts(s, msv, yv, w_ref, proj_ref)
            dm = dm_ref[:, cols].astype(F32)
            sp, ss, sg = _sigmoid(gp), _sigmoid(gs), _sigmoid(zg)
            dgp_ref[:, cols] = (dm * zp * sp * (1.0 - sp)).astype(BF16)
            dgs_ref[:, cols] = (dm * zv * sg * ss * (1.0 - ss)).astype(BF16)
            dzp_ref[:, cols] = (dm * sp).astype(BF16)
            dz = dm * ss
            dzv_ref[:, cols] = (dz * sg).astype(BF16)
            dzg_ref[:, cols] = (dz * zv * sg * (1.0 - sg)).astype(BF16)

    row = BS((tm, D_MODEL), lambda i: (i, 0))
    shape = SDS((t, D_MODEL), BF16)
    return pl.pallas_call(
        body, out_shape=(shape,) * 5, grid=(t // tm,), in_specs=_merge_specs(tm) + [row],
        out_specs=(row,) * 5, name="mixer_merge_bwd",
        compiler_params=_params("parallel"))(ms, yssm, w_e, proj, dmerged)


def _mixer_dw(ms, yssm, dzp, dzv, dzg, tm):
    t = ms.shape[0]
    tm = min(2 * tm, t)
    n_t = t // tm

    def body(ms_ref, y_ref, dzp_ref, dzv_ref, dzg_ref, o_ref, acc):
        i = pl.program_id(0)

        @pl.when(i == 0)
        def _():
            acc[...] = jnp.zeros_like(acc)

        msv, yv = ms_ref[...], y_ref[...]
        for s in range(N_SHARD):
            cols = slice(256 * s, 256 * (s + 1))
            acc[s, 0:E_VAL, :] += _dot(msv, dzp_ref[:, cols], TN)
            acc[s, E_VAL:E_GATE, :] += _dot(yv, dzv_ref[:, cols], TN)
            acc[s, E_GATE:, :] += _dot(yv, dzg_ref[:, cols], TN)

        @pl.when(i == n_t - 1)
        def _():
            o_ref[...] = acc[...].astype(BF16)

    row = BS((tm, D_MODEL), lambda i: (i, 0))
    full = BS((N_SHARD, 1024, 256), lambda i: (0, 0, 0))
    return pl.pallas_call(
        body, out_shape=SDS((N_SHARD, 1024, 256), BF16), grid=(n_t,),
        in_specs=[BS((tm, D_POOL), lambda i: (i, 0)), BS((tm, D_SSM), lambda i: (i, 0)), row, row, row],
        out_specs=full, scratch_shapes=[pltpu.VMEM((N_SHARD, 1024, 256), F32)],
        name="mixer_dw", compiler_params=_params("arbitrary"))(ms, yssm, dzp, dzv, dzg)


def _mixer_dx(dzp, dzv, dzg, w_e, y_total, tm):
    t = dzp.shape[0]

    def body(dzp_ref, dzv_ref, dzg_ref, w_ref, yt_ref, dms_ref, dy_ref, dyb_ref):
        acc_ms, acc_y = None, None
        for s in range(N_SHARD):
            cols = slice(256 * s, 256 * (s + 1))
            part_ms = _dot(dzp_ref[:, cols], w_ref[s, 0:E_VAL, :], NT)
            part_y = _dot(dzv_ref[:, cols], w_ref[s, E_VAL:E_GATE, :], NT) + _dot(dzg_ref[:, cols], w_ref[s, E_GATE:, :], NT)
            acc_ms = part_ms if s == 0 else acc_ms + part_ms
            acc_y = part_y if s == 0 else acc_y + part_y
        dms_ref[...] = acc_ms.astype(BF16)
        y = yt_ref[...]
        th = jnp.tanh(GELU_C * (y + GELU_K * y * y * y))
        dgelu = 0.5 * (1.0 + th) + 0.5 * y * (1.0 - th * th) * GELU_C * (1.0 + 3.0 * GELU_K * y * y)
        dy = acc_y * dgelu
        dy_ref[...] = dy
        dyb_ref[...] = dy.astype(BF16)

    row = BS((tm, D_MODEL), lambda i: (i, 0))
    narrow = BS((tm, D_SSM), lambda i: (i, 0))
    return pl.pallas_call(
        body, out_shape=(SDS((t, D_POOL), BF16), SDS((t, D_SSM), F32), SDS((t, D_SSM), BF16)), grid=(t // tm,),
        in_specs=[row, row, row, BS((N_SHARD, 1024, 256), lambda i: (0, 0, 0)), narrow],
        out_specs=(BS((tm, D_POOL), lambda i: (i, 0)), narrow, narrow),
        name="mixer_dx", compiler_params=_params("parallel"))(dzp, dzv, dzg, w_e, y_total)


def _attn_probs(q_h, k_h):
    s = _dot(q_h, k_h, NT) * (1.0 / math.sqrt(HEAD_DIM))
    e = jnp.exp(s - jnp.max(s, axis=-1, keepdims=True))
    return e / jnp.sum(e, axis=-1, keepdims=True)


def _attn_fwd(q, kv, tm):
    t = q.shape[0]
    tm = min(2 * tm, t)
    m = kv.shape[0]

    def body(q_ref, kv_ref, o_ref):
        for hd in range(N_HEADS):
            lo = hd * HEAD_DIM
            p = _attn_probs(q_ref[:, lo:lo + HEAD_DIM], kv_ref[:, lo:lo + HEAD_DIM])
            o_ref[:, lo:lo + HEAD_DIM] = _dot(p, kv_ref[:, D_MODEL + lo:D_MODEL + lo + HEAD_DIM]).astype(BF16)

    return pl.pallas_call(
        body, out_shape=SDS((t, D_MODEL), BF16), grid=(t // tm,),
        in_specs=[BS((tm, D_MODEL), lambda i: (i, 0)), BS((m, 2 * D_MODEL), lambda i: (0, 0))],
        out_specs=BS((tm, D_MODEL), lambda i: (i, 0)), name="attn_fwd", compiler_params=_params("parallel"))(q, kv)


def _attn_bwd(q, kv, d_o, tm):
    t = q.shape[0]
    m = kv.shape[0]

    def body(q_ref, kv_ref, do_ref, dq_ref, dkv_ref):
        i = pl.program_id(0)

        @pl.when(i == 0)
        def _():
            dkv_ref[...] = jnp.zeros_like(dkv_ref)

        for hd in range(N_HEADS):
            lo = hd * HEAD_DIM
            q_h = q_ref[:, lo:lo + HEAD_DIM]
            k_h = kv_ref[:, lo:lo + HEAD_DIM]
            v_h = kv_ref[:, D_MODEL + lo:D_MODEL + lo + HEAD_DIM]
            do_h = do_ref[:, lo:lo + HEAD_DIM]
            p = _attn_probs(q_h, k_h)
            dkv_ref[:, D_MODEL + lo:D_MODEL + lo + HEAD_DIM] += _dot(p, do_h, TN)
            dp = _dot(do_h, v_h, NT)
            ds = p * (dp - jnp.sum(dp * p, axis=-1, keepdims=True)) * (1.0 / math.sqrt(HEAD_DIM))
            dq_ref[:, lo:lo + HEAD_DIM] = _dot(ds, k_h).astype(BF16)
            dkv_ref[:, lo:lo + HEAD_DIM] += _dot(ds, q_h, TN)

    row = BS((tm, D_MODEL), lambda i: (i, 0))
    full = BS((m, 2 * D_MODEL), lambda i: (0, 0))
    return pl.pallas_call(
        body, out_shape=(SDS((t, D_MODEL), BF16), SDS((m, 2 * D_MODEL), F32)), grid=(t // tm,),
        in_specs=[row, full, row], out_specs=(row, full), name="attn_bwd",
        compiler_params=_params("arbitrary"))(q, kv, d_o)


TRANSPOSED = ("ffn1_w_gate", "ffn1_w_up", "ffn2_w_gate", "ffn2_w_up", "w_in")
GATHER_PHASES = {"f1a": (("ffn1_w_gate",), ("ffn1_w_up",)),
                 "f1b": (("ffn1_w_down",),),
                 "win": (("w_in",),),
                 "mix": (("w_mix_out",), ("w_q",), ("w_xo",), ("w_kv",), ("w_pool_proj", "w_glu_val", "w_glu_gate")),
                 "f2": (("ffn2_w_gate",), ("ffn2_w_up",), ("ffn2_w_down",))}
REDUCE_GROUPS = (("ffn2_w_gate",), ("ffn2_w_up",), ("ffn2_w_down",), ("w_xo",), ("w_q",), ("w_kv",), ("w_mix_out",),
                 ("w_pool_proj", "w_glu_val", "w_glu_gate"), ("w_in",), ("ffn1_w_gate",), ("ffn1_w_up",), ("ffn1_w_down",))
SMALL = ("ffn1_norm", "mix_norm", "pool_w", "pool_scale", "ssm_a_re", "ssm_a_im", "ssm_log_dt", "ssm_b_re",
         "ssm_b_im", "ssm_c_re", "ssm_c_im", "ssm_d", "xattn_norm", "mem_norm", "ffn2_norm", "final_norm")
WEIGHTS = ("ffn1_norm", "ffn1_w_gate", "ffn1_w_up", "ffn1_w_down", "mix_norm", "w_in", "pool_w", "pool_scale",
           "w_pool_proj", "ssm_a_re", "ssm_a_im", "ssm_log_dt", "ssm_b_re", "ssm_b_im", "ssm_c_re", "ssm_c_im",
           "ssm_d", "w_glu_val", "w_glu_gate", "w_mix_out", "xattn_norm", "mem_norm", "w_q", "w_kv", "w_xo",
           "ffn2_norm", "ffn2_w_gate", "ffn2_w_up", "ffn2_w_down", "final_norm")


def _small_view(a, n):
    return jnp.swapaxes(a, 3, 4) if n in ("ssm_b_re", "ssm_b_im") else a


def _device_step(x, mem, target, wts, sp, reducer=None):
    t = x.shape[0]
    tm = min(TM, t)
    g = {}

    first_gather = wts.start("win", wts.start("f1b", wts.start("f1a")))
    u1 = _rmsnorm("norm_ffn1", x, sp["ffn1_norm"], tm, after=first_gather)

    def per_channel(a):
        a = a.reshape(2 * SSM_GROUPS, 1, -1)
        return jnp.broadcast_to(a, (2 * SSM_GROUPS, SSM_GROUP, a.shape[-1])).reshape(SSM_ROWS, a.shape[-1])

    ssm_a = per_channel(sp["ssm_a_re"]), per_channel(sp["ssm_a_im"]), per_channel(sp["ssm_log_dt"])
    ssm_b = sp["ssm_b_re"].reshape(SSM_ROWS, SSM_STATE), sp["ssm_b_im"].reshape(SSM_ROWS, SSM_STATE)
    abr, abi, w_in_s, w_in_s_t, w_out_s_t, w_out_s = _ssm_prep(
        *ssm_a, *ssm_b, sp["ssm_c_re"].reshape(SSM_ROWS, SSM_STATE), sp["ssm_c_im"].reshape(SSM_ROWS, SSM_STATE),
        after=first_gather)
    first_rows = (2, SSM_GROUPS, SSM_GROUP, SSM_STATE)
    a_r = abr.reshape(first_rows)[:, :, 0].reshape(2, 1, SSM_CH)
    a_i = abi.reshape(first_rows)[:, :, 0].reshape(2, 1, SSM_CH)
    mem_n = _rmsnorm("norm_mem", mem, sp["mem_norm"], mem.shape[0], after=first_gather + wts.sources(("mix", "f2")))

    whole = (D_FF, D_MODEL)
    w_g1, w_u1 = wts.finish("f1a", [u1, w_in_s, w_in_s_t, w_out_s, w_out_s_t, a_r, a_i, mem_n])
    w_f1 = {"gate": w_g1.reshape(whole), "up": w_u1.reshape(whole)}
    g1, up1, a1 = _ffn_up("ffn1_up", u1, w_f1, tm)
    (w_dn,) = wts.finish("f1b", [a1])
    w_f1["down"] = w_dn.reshape(whole)
    h1, u2 = _ffn_down("ffn1_down", a1, w_f1, x, tm, next_gain=sp["mix_norm"])

    (w_in_g,) = wts.finish("win", [u2])
    w_in_t = w_in_g.reshape(D_FF, D_MODEL)
    proj, s_in = _mix_in(u2, w_in_t, tm, after=wts.start("f2", wts.start("mix", [w_in_g])))
    pooled, mixed, ms = _pool_fwd(proj, sp["pool_w"][0], sp["pool_scale"])

    states, y_dirs = [], []
    for dr in range(2):
        st, yd = _ssm_scan(f"ssm_scan_fwd{dr}", s_in, w_in_s, a_r, a_i, w_out_s, reverse=(dr == 1), dr=dr)
        states.append(st)
        y_dirs.append(yd)
    y_total, yssm = _ssm_combine(proj, y_dirs[0], y_dirs[1], sp["ssm_d"], tm, after=wts.fill_start("mix", y_dirs))
    *w_sq, w_kv, w_e = wts.finish("mix", [yssm, ms])
    w_mo, w_q, w_xo = (a.reshape(D_MODEL, D_MODEL) for a in w_sq)
    w_d = w_kv[:, None]

    merged = _mixer_merge(ms, yssm, w_e, proj, tm)
    h2, u3 = _mm_resid_norm("mix_out", merged, w_mo, h1, sp["xattn_norm"], tm)

    q = _plain_mm("attn_q", u3, w_q, NN, BF16, tm)
    n_mem = mem.shape[0]
    kv = _mm("attn_kv", [(mem_n, BS((n_mem, D_MODEL), lambda s: (0, 0)), w_d, BS((None, None, D_MODEL, 512), lambda s: (s, 0, 0, 0)), NN)],
             grid=(N_SHARD,), out_shape=SDS((n_mem, 2 * D_MODEL), BF16), out_spec=BS((n_mem, 512), lambda s: (0, s)))
    o = _attn_fwd(q, kv, tm)
    h3, u4 = _mm_resid_norm("attn_out", o, w_xo, h2, sp["ffn2_norm"], tm)

    w_f2 = dict(zip(("gate", "up", "down"), (a.reshape(whole) for a in wts.finish("f2", [u4]))))
    g2, up2, a2 = _ffn_up("ffn2_up", u4, w_f2, tm)
    loss, dh4, dh4_b, g["final_norm"] = _ffn_down("ffn2_down", a2, w_f2, h3, tm,
                                                  head=(sp["final_norm"].reshape(1, D_MODEL), target))

    dg2, dup2 = _ffn_bwd_act("ffn2_bwd_act", dh4_b, w_f2, g2, up2, tm)
    dw_f2 = _ffn_dw("ffn2_dw", u4, dg2, dup2, a2, dh4_b, tm)
    dh3, dh3_b, g["ffn2_norm"] = _ffn_dx("ffn2_dx", dg2, dup2, w_f2, h3, sp["ffn2_norm"], dh4, tm)

    d_o = _plain_mm("attn_out_dx", dh3_b, w_xo, NT, BF16, tm)
    dw_xo = _dw_mm("attn_out_dw", o, dh3_b, tm)
    dq, dkv = _attn_bwd(q, kv, d_o, tm)
    dw_q = _dw_mm("attn_q_dw", u3, dq, tm)
    dh2, dh2_b, g["xattn_norm"] = _mm_norm_bwd("attn_q_dx", dq, w_q, NT, h2, sp["xattn_norm"], dh3, tm)
    dw_kv = _mm("attn_kv_dw", [(mem_n, BS((n_mem, D_MODEL), lambda s: (0, 0)), dkv, BS((n_mem, 512), lambda s: (0, s)), TN)],
                grid=(N_SHARD,), out_shape=SDS((N_SHARD, D_MODEL, 512), BF16), out_spec=BS((None, D_MODEL, 512), lambda s: (s, 0, 0)))
    dmem_n = _mm("attn_kv_dx", [(dkv, BS((n_mem, 512), lambda s: (0, s)), w_d, BS((None, None, D_MODEL, 512), lambda s: (s, 0, 0, 0)), NT)],
                 grid=(N_SHARD,), red_axis=0, out_shape=SDS((n_mem, D_MODEL), F32), out_spec=BS((n_mem, D_MODEL), lambda s: (0, 0)))
    _, _, g["mem_norm"] = _rmsnorm_bwd("norm_mem_bwd", mem, sp["mem_norm"], dmem_n, None, n_mem)

    square = (N_SHARD, D_MODEL // N_SHARD, D_MODEL)
    sharded = (N_SHARD, FF_SH, D_MODEL)
    early = [a.reshape(sharded) for a in dw_f2] + [dw_xo.reshape(square), dw_q.reshape(square), dw_kv]
    swapping = reducer.swap_start("a1", early) if reducer is not None else []
    dmerged = _plain_mm("mix_out_dx", dh2_b, w_mo, NT, BF16, tm, after=swapping)
    dw_mo = _dw_mm("mix_out_dw", merged, dh2_b, tm)
    d_gp, d_gs, dzp, dzv, dzg = _mixer_merge_bwd(ms, yssm, w_e, proj, dmerged, tm)
    dw_e = _mixer_dw(ms, yssm, dzp, dzv, dzg, tm)
    d_ms, d_yt, d_yt_b = _mixer_dx(dzp, dzv, dzg, w_e, y_total, tm)
    dp, d_scale, d_pw = _pool_bwd(d_ms, mixed, pooled, sp["pool_w"][0], sp["pool_scale"])
    g["pool_scale"] = d_scale
    g["pool_w"] = d_pw[None]

    du_dirs, lams = [], []
    for dr in range(2):
        lam, du = _ssm_scan(f"ssm_scan_bwd{dr}", d_yt_b, w_out_s_t, a_r, a_i, w_in_s_t, reverse=(dr == 0), dr=dr, conj=True)
        du_dirs.append(du)
        lams.append(lam)
    ds, g["ssm_d"] = _ssm_ds(proj, d_yt, du_dirs[0], du_dirs[1], sp["ssm_d"], tm)

    d_proj = [dp, ds, d_gp, d_gs]
    dw_in_t = _mix_in_dw(d_proj, u2, tm)
    early += [dw_mo.reshape(square), dw_e, dw_in_t.reshape(sharded)]
    swapping = reducer.swap_start("a2", early[6:]) if reducer is not None else []
    dh1, dh1_b, g["mix_norm"] = _mm_norm_bwd("mix_in_dx", d_proj, w_in_t, NN, h1, sp["mix_norm"], dh2, tm, after=swapping)
    g["final_norm"] = g["final_norm"].reshape(D_MODEL)

    travelling = reducer.start("a", [], swapped=["a1", "a2"], after=list(g.values())) if reducer is not None else []
    d_abr, d_abi, d_cm, d_bm = [], [], [], []
    for dr in range(2):
        da_r, da_i, d_win, d_woutt = _ssm_param_grads(f"ssm_param_grads{dr}", lams[dr], states[dr], s_in, d_yt_b,
                                                      reverse=(dr == 1), after=travelling)
        d_abr.append(da_r)
        d_abi.append(da_i)
        d_bm.append(d_win)
        d_cm.append(d_woutt)

    d_ar, d_ai, d_ldt, d_br, d_bi, d_cr, d_ci = _ssm_prep_bwd(*ssm_a, *ssm_b, d_abr, d_abi, d_bm, d_cm)
    per_group = (2 * SSM_GROUPS, SSM_GROUP * SSM_STATE)
    g["ssm_a_re"] = d_ar.reshape(2 * SSM_GROUPS, SSM_GROUP, SSM_STATE).sum(axis=1).reshape(sp["ssm_a_re"].shape)
    g["ssm_a_im"] = d_ai.reshape(2 * SSM_GROUPS, SSM_GROUP, SSM_STATE).sum(axis=1).reshape(sp["ssm_a_im"].shape)
    g["ssm_log_dt"] = d_ldt.reshape(per_group).sum(axis=1).reshape(sp["ssm_log_dt"].shape)
    g["ssm_b_re"] = d_br.reshape(sp["ssm_b_re"].shape)
    g["ssm_b_im"] = d_bi.reshape(sp["ssm_b_im"].shape)
    g["ssm_c_re"] = d_cr.reshape(sp["ssm_c_re"].shape)
    g["ssm_c_im"] = d_ci.reshape(sp["ssm_c_im"].shape)
    if reducer is not None:
        travelling = travelling + [d_ar, d_br, d_cr]
    dg1, dup1 = _ffn_bwd_act("ffn1_bwd_act", dh1_b, w_f1, g1, up1, tm, after=travelling)
    dw_f1 = [a.reshape(sharded) for a in _ffn_dw("ffn1_dw", u1, dg1, dup1, a1, dh1_b, tm)]
    if reducer is not None:
        joining = reducer.join_start("a", after=reducer.finish("a", reducer.swap_start("b1", dw_f1)))
        travelling = joining + reducer.start("b", [], swapped=["b1"], after=joining)
    grad_x, _, g["ffn1_norm"] = _ffn_dx("ffn1_dx", dg1, dup1, w_f1, x, sp["ffn1_norm"], dh1, tm, after=travelling)
    if reducer is not None:
        reducer.join_finish("a", [grad_x])
    return loss, grad_x, early + dw_f1, g


def _mesh_place():
    x, y, c = lax.axis_index("x"), lax.axis_index("y"), lax.axis_index("c")
    chips = [(1 - x, y), (x, 1 - y), (1 - x, 1 - y)]
    return x, y, c, chips


def _remote(src, dst, send_sems, recv_sems, k, to):
    return pltpu.make_async_remote_copy(src_ref=src, dst_ref=dst, send_sem=send_sems.at[k], recv_sem=recv_sems.at[k],
                                        device_id=to, device_id_type=MESH)


def _sibling_swap_halves(tag, grads, after=()):
    n = len(grads)
    after_ops, after_specs = _after_operands(after)

    def body(*refs):
        ins, outs = refs[:n], refs[n + len(after_ops):2 * n + len(after_ops)]
        send_sems, recv_sems = refs[2 * n + len(after_ops):]
        x, y, c, _ = _mesh_place()
        sibling = (x, y, 1 - c)
        copies = []
        for k in range(n):
            half = grads[k].shape[1] // 2
            theirs = pl.ds(pl.multiple_of((1 - c) * half, 16), half)
            cp = _remote(ins[k].at[:, theirs, :], outs[k], send_sems, recv_sems, k, sibling)
            cp.start()
            copies.append(cp)
        for cp in copies:
            cp.wait_recv()
        for cp in copies:
            cp.wait_send()

    hbm = BS(memory_space=pl.ANY)
    return pl.pallas_call(
        body, out_shape=tuple(SDS((g.shape[0], g.shape[1] // 2, g.shape[2]), g.dtype) for g in grads),
        in_specs=[hbm] * n + after_specs, out_specs=(hbm,) * n,
        scratch_shapes=[pltpu.SemaphoreType.DMA((n,)), pltpu.SemaphoreType.DMA((n,))],
        name="reduce_sibling_send_" + tag, compiler_params=_params())(*grads, *after_ops)


def _row_tile(rows, cap=512):
    return max(r for r in range(16, cap + 1, 16) if rows % r == 0)


REDUCE_STEPS = 2


def _chip_presum(tag, grads, gots, c_idx):
    n = len(grads)
    halves = [g.shape[1] // 2 for g in grads]
    tiles = [(h // REDUCE_STEPS, g.shape[2]) for h, g in zip(halves, grads)]

    def body(c_ref, *refs):
        for k in range(n):
            refs[2 * n + k][...] = (refs[k][...].astype(F32) + refs[n + k][...].astype(F32)).astype(BF16)

    mine = [BS((None, None) + tile, lambda s, i, c_ref: (s, c_ref[0], i, 0)) for tile in tiles]
    plain = [BS((None,) + tile, lambda s, i, c_ref: (s, i, 0)) for tile in tiles]
    return list(pl.pallas_call(
        body, out_shape=tuple(SDS((g.shape[0], h, g.shape[2]), BF16) for g, h in zip(grads, halves)),
        grid_spec=pltpu.PrefetchScalarGridSpec(num_scalar_prefetch=1, grid=(N_SHARD, REDUCE_STEPS),
                                               in_specs=mine + plain, out_specs=plain),
        name="reduce_presum_" + tag, compiler_params=_params("parallel", "parallel"))(
            c_idx, *[g.reshape(g.shape[0], 2, h, g.shape[2]) for g, h in zip(grads, halves)], *gots))


HBM_SPEC = BS(memory_space=pltpu.HBM)
SEM_SPEC = BS(memory_space=pltpu.SEMAPHORE)
DATAFLOW = pltpu.SideEffectType.DATAFLOW_SIDE_EFFECTING


def _chip_exchange_copies(parts, lands, send_sems, recv_sems):
    _, _, c, chips = _mesh_place()
    return [_remote(parts[k].at[2 * px + py], lands[k].at[j], send_sems, recv_sems, 3 * k + j, (px, py, c))
            for k in range(len(parts)) for j, (px, py) in enumerate(chips)]


def _gather_copies(shards, lands, send_sems, recv_sems):
    x, y, c, chips = _mesh_place()
    return [_remote(shards[k], lands[k].at[2 * x + y], send_sems, recv_sems, 3 * k + j, (px, py, c))
            for k in range(len(shards)) for j, (px, py) in enumerate(chips)]


def _gather_half_copies(shards, lands, send_sems, recv_sems):
    x, y, c, chips = _mesh_place()
    out = []
    for k in range(len(shards)):
        half = shards[k].shape[0] // 2
        mine = pl.ds(pl.multiple_of(c * half, 16), half)
        for j, (px, py) in enumerate(chips):
            out.append(_remote(shards[k].at[mine, :], lands[k].at[2 * x + y, mine, :], send_sems, recv_sems,
                               3 * k + j, (px, py, c)))
    return out


def _fill_copies(zones, _, send_sems, recv_sems):
    x, y, c, chips = _mesh_place()
    out = []
    for k in range(len(zones)):
        half = zones[k].shape[1] // 2
        mine = pl.ds(pl.multiple_of(c * half, 16), half)
        for j, (px, py) in enumerate(chips):
            blk = zones[k].at[2 * px + py, mine, :]
            out.append(_remote(blk, blk, send_sems, recv_sems, 3 * k + j, (x, y, 1 - c)))
    return out


def _sibling_fill(tag, lands):
    n = len(lands)

    def body(*refs):
        outs = refs[n:2 * n]
        copies = _fill_copies(outs, outs, *refs[2 * n:])
        for cp in copies:
            cp.start()
        for cp in copies:
            cp.wait_recv()
        for cp in copies:
            cp.wait_send()

    hbm = BS(memory_space=pl.ANY)
    return list(pl.pallas_call(
        body, out_shape=tuple(SDS(a.shape, a.dtype) for a in lands),
        in_specs=[hbm] * n, out_specs=(hbm,) * n, input_output_aliases={k: k for k in range(n)},
        scratch_shapes=[pltpu.SemaphoreType.DMA((3 * n,)), pltpu.SemaphoreType.DMA((3 * n,))],
        name="gather_fill_" + tag, compiler_params=_params())(*lands))


def _swap_copies(grads, lands, send_sems, recv_sems):
    x, y, c, _ = _mesh_place()
    out = []
    for k in range(len(grads)):
        half = grads[k].shape[1] // 2
        theirs = pl.ds(pl.multiple_of((1 - c) * half, 16), half)
        out.append(_remote(grads[k].at[:, theirs, :], lands[k], send_sems, recv_sems, k, (x, y, 1 - c)))
    return out


def _join_copies(fulls, same, send_sems, recv_sems):
    x, y, c, _ = _mesh_place()
    out = []
    for k in range(len(fulls)):
        half = fulls[k].shape[0] // 2
        mine = fulls[k].at[pl.ds(pl.multiple_of(c * half, 8), half), :]
        out.append(_remote(mine, mine, send_sems, recv_sems, k, (x, y, 1 - c)))
    return out


def _everyone_copies(packs, lands, send_sems, recv_sems):
    x, y, c, _ = _mesh_place()
    out = []
    for k in range(len(packs)):
        for j in range(N_DEV - 1):
            bx, by, bc = (j + 1) >> 2 & 1, (j + 1) >> 1 & 1, (j + 1) & 1
            peer = (x ^ bx, y ^ by, c ^ bc)
            out.append(_remote(packs[k], lands[k].at[4 * x + 2 * y + c], send_sems, recv_sems, (N_DEV - 1) * k + j, peer))
    return out


def _split_start(name, copies, sources, land_shapes, after=(), fanout=3):
    n = len(sources)
    n_land = len(land_shapes)
    m = n + n_land
    n_sems = fanout * n
    after_ops, after_specs = _after_operands(after)

    def body(*refs):
        ins = refs[:n]
        lands = refs[n:m] if n_land else ins
        send_sems, recv_sems = refs[m + len(after_ops)], refs[m + len(after_ops) + 1]
        token = refs[-1]
        for cp in copies(ins, lands, send_sems, recv_sems):
            cp.start()
        token[...] = jnp.zeros_like(token)

    lands = [pltpu.with_memory_space_constraint(lax.empty(s, d), pltpu.HBM) for s, d in land_shapes]
    sources = [pltpu.with_memory_space_constraint(p, pltpu.HBM) for p in sources]
    thru = [pltpu.HBM(a.shape, a.dtype) for a in sources + lands]
    out = pl.pallas_call(
        body, name=name,
        out_shape=(pltpu.SemaphoreType.DMA((n_sems,)), pltpu.SemaphoreType.DMA((n_sems,)), *thru, SDS((8, 128), F32)),
        in_specs=[HBM_SPEC] * m + after_specs,
        out_specs=(SEM_SPEC, SEM_SPEC, *[HBM_SPEC] * m, BS(memory_space=pltpu.VMEM)),
        input_output_aliases={i: 2 + i for i in range(m)},
        compiler_params=pltpu.CompilerParams(has_side_effects=DATAFLOW))(*sources, *lands, *after_ops)
    return out[0], out[1], list(out[2:2 + n]), list(out[2 + n:2 + m]), out[-1]


def _split_wait(name, copies, send_sems, recv_sems, sources, lands, after):
    n = len(sources)
    m = n + len(lands)
    after_ops, after_specs = _after_operands(after)

    def body(*refs):
        ins = refs[:n]
        zones = refs[n:m] if m > n else ins
        for cp in copies(ins, zones, refs[m], refs[m + 1]):
            cp.wait_send()
            cp.wait_recv()

    out = pl.pallas_call(
        body, name=name,
        out_shape=tuple(pltpu.HBM(a.shape, a.dtype) for a in sources + lands),
        in_specs=[HBM_SPEC] * m + [SEM_SPEC, SEM_SPEC] + after_specs, out_specs=(HBM_SPEC,) * m,
        input_output_aliases={i: i for i in range(m)},
        compiler_params=pltpu.CompilerParams(has_side_effects=DATAFLOW))(*sources, *lands, send_sems, recv_sems, *after_ops)
    return list(out[:n]), list(out[n:])


class _WeightGatherer:
    def __init__(self, shards):
        self.shards, self.open, self.filling = shards, {}, {}
        self.me = 2 * lax.axis_index("x") + lax.axis_index("y")

    HALVED = ("f1a", "win", "mix")

    def start(self, tag, after=()):
        shapes = [((N_SHARD,) + s.shape, s.dtype) for s in self.shards[tag]]
        copies = _gather_half_copies if tag in self.HALVED else _gather_copies
        self.open[tag] = _split_start("gather_start_" + tag, copies, self.shards[tag], shapes, after)
        return [self.open[tag][-1]]

    def sources(self, tags):
        return [s for tag in tags for s in self.shards[tag]]

    def fill_start(self, tag, after):
        send_sems, recv_sems, shards, lands, _ = self.open.pop(tag)
        shards, lands = _split_wait("gather_wait_" + tag, _gather_half_copies, send_sems, recv_sems, shards, lands, after)
        self.filling[tag] = shards, _split_start("gather_fill_start_" + tag, _fill_copies, lands, [])
        return [self.filling[tag][1][-1]]

    def finish(self, tag, after):
        if tag in self.filling:
            shards, (send_sems, recv_sems, lands, _, _) = self.filling.pop(tag)
            lands, _ = _split_wait("gather_fill_wait_" + tag, _fill_copies, send_sems, recv_sems, lands, [], after)
            return [lax.dynamic_update_slice(zone, s[None], (self.me, 0, 0)) for zone, s in zip(lands, shards)]
        send_sems, recv_sems, shards, lands, _ = self.open.pop(tag)
        copies = _gather_half_copies if tag in self.HALVED else _gather_copies
        shards, lands = _split_wait("gather_wait_" + tag, copies, send_sems, recv_sems, shards, lands, after)
        if tag in self.HALVED:
            lands = _sibling_fill(tag, lands)
        return [lax.dynamic_update_slice(zone, s[None], (self.me, 0, 0)) for zone, s in zip(lands, shards)]


class _GradReducer:
    def __init__(self):
        self.c_idx = lax.axis_index("c").astype(jnp.int32).reshape(1)
        self.place = jnp.stack([2 * lax.axis_index("x") + lax.axis_index("y"), lax.axis_index("c")]).astype(jnp.int32)
        self.swaps, self.open, self.landed, self.joins, self.reduced = {}, {}, {}, {}, []

    def swap_start(self, tag, grads, after=()):
        shapes = [((g.shape[0], g.shape[1] // 2, g.shape[2]), g.dtype) for g in grads]
        self.swaps[tag] = _split_start("reduce_swap_start_" + tag, _swap_copies, grads, shapes, after, fanout=1)
        return [self.swaps[tag][-1]]

    def start(self, tag, grads, after=(), swapped=()):
        pairs = []
        for s in swapped:
            send_sems, recv_sems, early, lands, _ = self.swaps.pop(s)
            behind = grads[-1:] or list(after)
            pairs += zip(*_split_wait("reduce_swap_wait_" + s, _swap_copies, send_sems, recv_sems, early, lands, behind))
        if grads:
            pairs += zip(grads, _sibling_swap_halves(tag, grads, after))
        parts = _chip_presum(tag, [g for g, _ in pairs], [s for _, s in pairs], self.c_idx)
        shapes = [((3,) + p.shape[1:], p.dtype) for p in parts]
        self.open[tag] = _split_start("reduce_exchange_start_" + tag, _chip_exchange_copies, parts, shapes)
        return [self.open[tag][-1]]

    def finish(self, tag, after):
        send_sems, recv_sems, parts, lands, _ = self.open.pop(tag)
        self.landed[tag] = _split_wait("reduce_exchange_wait_" + tag, _chip_exchange_copies, send_sems, recv_sems, parts, lands, after)
        return self.landed[tag][1][:1]

    def _sums(self, tag, after=()):
        parts, landed = self.landed.pop(tag)
        return _chip_sum(tag, parts, landed, self.place, after)

    def join_start(self, tag, after=()):
        self.joins[tag] = _split_start("reduce_join_start_" + tag, _join_copies, self._sums(tag, after), [], fanout=1)
        return [self.joins[tag][-1]]

    def join_finish(self, tag, after):
        send_sems, recv_sems, fulls, _, _ = self.joins.pop(tag)
        self.reduced += _split_wait("reduce_join_wait_" + tag, _join_copies, send_sems, recv_sems, fulls, [], after)[0]


def _chip_sum(tag, parts, gots, place, after=()):
    n = len(parts)
    tiles = [(p.shape[1] // REDUCE_STEPS, p.shape[2]) for p in parts]
    after_ops, after_specs = _after_operands(after)

    def body(place_ref, *refs):
        outs = refs[2 * n + len(after_ops):]
        for k in range(n):
            acc = refs[k][...].astype(F32)
            for j in range(3):
                acc = acc + refs[n + k][j].astype(F32)
            outs[k][...] = acc

    return list(pl.pallas_call(
        body, out_shape=tuple(SDS((2 * p.shape[1], p.shape[2]), F32) for p in parts),
        grid_spec=pltpu.PrefetchScalarGridSpec(
            num_scalar_prefetch=1, grid=(REDUCE_STEPS,),
            in_specs=[BS((None,) + tile, lambda i, place_ref: (place_ref[0], i, 0)) for tile in tiles]
            + [BS((3,) + tile, lambda i, place_ref: (0, i, 0)) for tile in tiles] + after_specs,
            out_specs=[BS(tile, lambda i, place_ref: (place_ref[1] * REDUCE_STEPS + i, 0)) for tile in tiles]),
        name="reduce_sum_" + tag, compiler_params=_params("parallel"))(place, *parts, *gots, *after_ops))


N_DEV = 8


def _sum_devices(packs):
    _, rows, lanes = packs.shape

    def body(p_ref, o_ref):
        acc = p_ref[0]
        for dev in range(1, N_DEV):
            acc = acc + p_ref[dev]
        o_ref[...] = acc

    vm = BS(memory_space=pltpu.VMEM)
    return pl.pallas_call(body, out_shape=SDS((rows, lanes), F32), in_specs=[vm], out_specs=vm,
                          name="small_sum", compiler_params=_params())(packs)


def _adamw_refs(w_ref, g_ref, m_ref, v_ref, go_ref, d_ref, mo_ref, vo_ref):
    bc1 = 1.0 - ADAM_B1 ** ADAM_STEP
    bc2 = 1.0 - ADAM_B2 ** ADAM_STEP
    g = g_ref[...]
    m_new = ADAM_B1 * m_ref[...] + (1.0 - ADAM_B1) * g
    v_new = ADAM_B2 * v_ref[...] + (1.0 - ADAM_B2) * (g * g)
    go_ref[...] = g
    mo_ref[...] = m_new
    vo_ref[...] = v_new
    d_ref[...] = -ADAM_LR * ((m_new / bc1) / (jnp.sqrt(v_new / bc2) + ADAM_EPS) + ADAM_WD * w_ref[...])


def _adamw_small(ws, gs, ms, vs):
    n = len(ws)

    def body(*refs):
        for k in range(n):
            _adamw_refs(*[refs[j * n + k] for j in range(4)], *refs[4 * n + 4 * k:4 * n + 4 * k + 4])

    vm = BS(memory_space=pltpu.VMEM)
    outs = pl.pallas_call(
        body, out_shape=tuple(SDS(a.shape, F32) for a in ws for _ in range(4)), in_specs=[vm] * (4 * n),
        out_specs=(vm,) * (4 * n), name="adamw_small", compiler_params=_params())(*ws, *gs, *ms, *vs)
    return [outs[4 * k:4 * k + 4] for k in range(n)]


def _adamw(name, w, grad, row0, m, v, after=()):
    rows, cols = w.shape
    tr = rows if rows < 16 else _row_tile(rows, 352)
    after_ops, after_specs = _after_operands(after)

    def body(w_ref, g_ref, m_ref, v_ref, *rest):
        _adamw_refs(w_ref, g_ref, m_ref, v_ref, *rest[len(after_ops):])

    blk = BS((tr, cols), lambda i: (i, 0))
    shape = SDS((rows, cols), F32)
    return pl.pallas_call(
        body, out_shape=(shape,) * 4, grid=(rows // tr,),
        in_specs=[blk, BS((tr, cols), lambda i: (row0 // tr + i, 0)), blk, blk] + after_specs, out_specs=(blk,) * 4,
        name=name, compiler_params=_params("parallel"))(w, grad, m, v, *after_ops)


SMALL_LANES = 128


SMALL_TILE = 8 * SMALL_LANES


def _packed_rows(p):
    return -(-p.size // SMALL_TILE) * 8


def _pack_small(parts):
    tiles = [jnp.pad(jnp.ravel(p), (0, _packed_rows(p) * SMALL_LANES - p.size)).reshape(-1, SMALL_LANES) for p in parts]
    rows = sum(t.shape[0] for t in tiles)
    return jnp.concatenate(tiles + [jnp.zeros((-rows % 64, SMALL_LANES), F32)], axis=0)


def _unpack_small(packed, like):
    out, at = [], 0
    for p in like:
        rows = _packed_rows(p)
        out.append(jnp.ravel(packed[at:at + rows])[:p.size].reshape(p.shape))
        at += rows
    return out


def kernel(x, mem, ffn1_norm, ffn1_w_gate, ffn1_w_up, ffn1_w_down, mix_norm, w_in, pool_w, pool_scale, w_pool_proj, ssm_a_re, ssm_a_im, ssm_log_dt, ssm_b_re, ssm_b_im, ssm_c_re, ssm_c_im, ssm_d, w_glu_val, w_glu_gate, w_mix_out, xattn_norm, mem_norm, w_q, w_kv, w_xo, ffn2_norm, ffn2_w_gate, ffn2_w_up, ffn2_w_down, final_norm, loss_target, m_ffn1_norm, m_ffn1_w_gate, m_ffn1_w_up, m_ffn1_w_down, m_mix_norm, m_w_in, m_pool_w, m_pool_scale, m_w_pool_proj, m_ssm_a_re, m_ssm_a_im, m_ssm_log_dt, m_ssm_b_re, m_ssm_b_im, m_ssm_c_re, m_ssm_c_im, m_ssm_d, m_w_glu_val, m_w_glu_gate, m_w_mix_out, m_xattn_norm, m_mem_norm, m_w_q, m_w_kv, m_w_xo, m_ffn2_norm, m_ffn2_w_gate, m_ffn2_w_up, m_ffn2_w_down, m_final_norm, v_ffn1_norm, v_ffn1_w_gate, v_ffn1_w_up, v_ffn1_w_down, v_mix_norm, v_w_in, v_pool_w, v_pool_scale, v_w_pool_proj, v_ssm_a_re, v_ssm_a_im, v_ssm_log_dt, v_ssm_b_re, v_ssm_b_im, v_ssm_c_re, v_ssm_c_im, v_ssm_d, v_w_glu_val, v_w_glu_gate, v_w_mix_out, v_xattn_norm, v_mem_norm, v_w_q, v_w_kv, v_w_xo, v_ffn2_norm, v_ffn2_w_gate, v_ffn2_w_up, v_ffn2_w_down, v_final_norm):
    given = dict(locals())
    w = {n: given[n] for n in WEIGHTS}
    m = {n: given["m_" + n] for n in WEIGHTS}
    v = {n: given["v_" + n] for n in WEIGHTS}

    def shard_view(a, n):
        return a[0].T if n in TRANSPOSED else a[0]

    def shard_unview(a, n):
        return (a.T if n in TRANSPOSED else a)[None]

    shards = {tag: [jnp.concatenate([shard_view(w[n], n).astype(BF16) for n in grp], axis=0) for grp in arrays]
              for tag, arrays in GATHER_PHASES.items()}
    reducer = _GradReducer()
    ws, ms, vs = ({n: _small_view(a[n], n) for n in SMALL} for a in (w, m, v))
    loss_part, grad_x, _, small = _device_step(x[0], mem[0], loss_target[0], _WeightGatherer(shards), ws, reducer)

    small_like = [ws[n] for n in SMALL] + [loss_part[0, :1]]
    pack = _pack_small([small[n] for n in SMALL] + [loss_part[0, :1]])
    everyone = _split_start("small_start", _everyone_copies, [pack], [((N_DEV,) + pack.shape, F32)], fanout=N_DEV - 1)

    grads, delta, new_m, new_v = {}, {}, {}, {}
    big_done = []

    def update(groups, reduced, after=()):
        for grp, red in zip(groups, reduced):
            row0 = 0
            for n in grp:
                w_n = shard_view(w[n], n)
                outs = _adamw("adamw_" + n, w_n, red, row0, shard_view(m[n], n), shard_view(v[n], n),
                              after=[everyone[-1], *after])
                grads[n], delta[n], new_m[n], new_v[n] = (shard_unview(o, n) for o in outs)
                big_done.append(outs[1])
                row0 += w_n.shape[0]

    n_a, n_f2 = len(reducer.reduced), 3
    update(REDUCE_GROUPS[n_f2:n_a], reducer.reduced[n_f2:])
    reducer.finish("b", list(big_done))
    update(REDUCE_GROUPS[:n_f2], reducer.reduced[:n_f2], after=reducer.join_start("b"))
    reducer.join_finish("b", big_done[-n_f2:])
    update(REDUCE_GROUPS[n_a:], reducer.reduced[n_a:])

    send_sems, recv_sems, packs, landed, _ = everyone
    packs, landed = _split_wait("small_wait", _everyone_copies, send_sems, recv_sems, packs, landed, big_done)
    mine = 4 * lax.axis_index("x") + 2 * lax.axis_index("y") + lax.axis_index("c")
    summed = _sum_devices(lax.dynamic_update_slice(landed[0], packs[0][None], (mine, 0, 0)))
    g_small = dict(zip(SMALL + ("loss",), _unpack_small(summed, small_like)))
    loss = g_small.pop("loss").reshape(())
    def two_d(a):
        return a.reshape(-1, a.shape[-1])

    updated = _adamw_small(*([two_d(a[n]) for n in SMALL] for a in (ws, g_small, ms, vs)))
    for n, outs in zip(SMALL, updated):
        grads[n], delta[n], new_m[n], new_v[n] = (_small_view(o.reshape(ws[n].shape), n) for o in outs)

    return (loss, grad_x[None], *[grads[n] for n in WEIGHTS], *[delta[n] for n in WEIGHTS],
            *[new_m[n] for n in WEIGHTS], *[new_v[n] for n in WEIGHTS])
```

```python
import functools
import math

import jax
import jax.numpy as jnp
from jax import lax
from jax.experimental import pallas as pl
from jax.experimental.pallas import tpu as pltpu

F32 = jnp.float32
BF16 = jnp.bfloat16
SDS = jax.ShapeDtypeStruct
BS = pl.BlockSpec
MESH = pl.DeviceIdType.MESH

D_MODEL = 1024
D_FF = 2816
N_SHARD = 4
FF_SH = D_FF // N_SHARD
D_POOL = 512
POOL_WINDOWS = (2, 4, 8, 16)
POOL_GROUP = 128
D_SSM = 256
SSM_GROUPS = 16
SSM_GROUP = 16
SSM_STATE = 64
SSM_CH = SSM_GROUPS * SSM_STATE
N_HEADS = 4
HEAD_DIM = 256
EPS = 1e-6
ADAM_LR, ADAM_B1, ADAM_B2, ADAM_EPS, ADAM_WD, ADAM_STEP = 0.001, 0.9, 0.999, 1e-08, 0.01, 10

VMEM_LIMIT_V7X = 58 * 1024 * 1024
TM = 512

NN = (((1,), (0,)), ((), ()))
NT = (((1,), (1,)), ((), ()))
TN = (((0,), (0,)), ((), ()))


def _params(*sem):
    return pltpu.CompilerParams(dimension_semantics=sem if sem else None, vmem_limit_bytes=VMEM_LIMIT_V7X)


def _dot(a, b, dims=NN):
    return lax.dot_general(a.astype(BF16), b.astype(BF16), dims, preferred_element_type=F32)


def _sigmoid(v):
    return pl.reciprocal(1.0 + jnp.exp(-v), approx=True)


def _block_dims(spec):
    return tuple(d for d in spec.block_shape if d is not None)


def _after_operands(after):
    return list(after), [BS(memory_space=pl.ANY)] * len(after)


def _mm(name, pairs, *, grid, out_shape, out_spec, red_axis=None, extras=(), epilogue=None, after=()):
    n_pairs, n_extra = len(pairs), len(extras)
    n_red = grid[red_axis] if red_axis is not None else 1
    dims = [p[4] for p in pairs]

    def body(*refs):
        ab = refs[:2 * n_pairs]
        ex = refs[2 * n_pairs:2 * n_pairs + n_extra]
        o_ref = refs[2 * n_pairs + n_extra + len(after)]

        def partial():
            acc = None
            for p in range(n_pairs):
                t = _dot(ab[2 * p][...], ab[2 * p + 1][...], dims[p])
                acc = t if acc is None else acc + t
            return acc

        def finish(acc):
            res = epilogue(acc, *[e[...] for e in ex]) if epilogue is not None else acc
            o_ref[...] = res.astype(o_ref.dtype)

        if n_red == 1:
            finish(partial())
        else:
            acc_ref = refs[-1]
            k = pl.program_id(red_axis)

            @pl.when(k == 0)
            def _():
                acc_ref[...] = jnp.zeros_like(acc_ref)

            acc_ref[...] += partial()

            @pl.when(k == n_red - 1)
            def _():
                finish(acc_ref[...])

    operands, in_specs = [], []
    for a, a_spec, b, b_spec, _ in pairs:
        operands += [a, b]
        in_specs += [a_spec, b_spec]
    for e, e_spec in extras:
        operands.append(e)
        in_specs.append(e_spec)
    after_ops, after_specs = _after_operands(after)
    operands += after_ops
    in_specs += after_specs
    scratch = [pltpu.VMEM(_block_dims(out_spec), F32)] if n_red > 1 else []
    sem = tuple("arbitrary" if ax == red_axis else "parallel" for ax in range(len(grid)))
    return pl.pallas_call(body, out_shape=out_shape, grid=grid, in_specs=in_specs, out_specs=out_spec,
                          scratch_shapes=scratch, name=name, compiler_params=_params(*sem))(*operands)


def _rmsnorm(name, h, gain, tm, after=()):
    t, d = h.shape
    after_ops, after_specs = _after_operands(after)

    def body(h_ref, g_ref, *rest):
        u_ref = rest[-1]
        hv = h_ref[...]
        r = lax.rsqrt(jnp.mean(hv * hv, axis=-1, keepdims=True) + EPS)
        u_ref[...] = ((hv * r) * g_ref[...]).astype(u_ref.dtype)

    return pl.pallas_call(
        body, out_shape=SDS((t, d), BF16), grid=(t // tm,),
        in_specs=[BS((tm, d), lambda i: (i, 0)), BS((1, d), lambda i: (0, 0))] + after_specs,
        out_specs=BS((tm, d), lambda i: (i, 0)), name=name, compiler_params=_params("parallel"))(h, gain, *after_ops)


def _rmsnorm_bwd(name, h, gain, du, dh_in, tm):
    t, d = h.shape
    has_in = dh_in is not None

    def body(*refs):
        if has_in:
            h_ref, g_ref, du_ref, dhin_ref, dh_ref, dhb_ref, dg_ref = refs
        else:
            h_ref, g_ref, du_ref, dh_ref, dhb_ref, dg_ref = refs
        i = pl.program_id(0)
        hv = h_ref[...]
        r = lax.rsqrt(jnp.mean(hv * hv, axis=-1, keepdims=True) + EPS)
        n = hv * r
        duv = du_ref[...].astype(F32)
        dn = duv * g_ref[...]
        dh = r * (dn - n * jnp.mean(dn * n, axis=-1, keepdims=True))
        if has_in:
            dh = dhin_ref[...] + dh
        dh_ref[...] = dh
        dhb_ref[...] = dh.astype(BF16)

        @pl.when(i == 0)
        def _():
            dg_ref[...] = jnp.zeros_like(dg_ref)

        dg_ref[...] += jnp.sum(duv * n, axis=0, keepdims=True)

    row = BS((tm, d), lambda i: (i, 0))
    vec = BS((1, d), lambda i: (0, 0))
    operands = [h, gain, du] + ([dh_in] if has_in else [])
    in_specs = [row, vec, row] + ([row] if has_in else [])
    return pl.pallas_call(
        body, out_shape=(SDS((t, d), F32), SDS((t, d), BF16), SDS((1, d), F32)), grid=(t // tm,),
        in_specs=in_specs, out_specs=(row, row, vec), name=name, compiler_params=_params("arbitrary"))(*operands)


def _loss_head_tile(i, hv, g_ref, t_ref, loss_ref, dh_ref, dhb_ref, dg_ref):
    g = g_ref[...]
    r = lax.rsqrt(jnp.mean(hv * hv, axis=-1, keepdims=True) + EPS)
    n = hv * r
    err = n * g - t_ref[...]
    dy = err * (1.0 / hv.shape[-1])
    dn = dy * g
    dh = r * (dn - n * jnp.mean(dn * n, axis=-1, keepdims=True))
    dh_ref[...] = dh
    dhb_ref[...] = dh.astype(BF16)

    @pl.when(i == 0)
    def _():
        dg_ref[...] = jnp.zeros_like(dg_ref)
        loss_ref[...] = jnp.zeros_like(loss_ref)

    dg_ref[...] += jnp.sum(dy * n, axis=0, keepdims=True)
    part = 0.5 * jnp.sum(jnp.mean(err * err, axis=-1, keepdims=True), axis=0, keepdims=True)
    loss_ref[...] += jnp.broadcast_to(part, loss_ref.shape)


def _norm_tile(h, g_ref, u_ref):
    r = lax.rsqrt(jnp.mean(h * h, axis=-1, keepdims=True) + EPS)
    u_ref[...] = ((h * r) * g_ref[...]).astype(u_ref.dtype)


FFN_BLOCK = D_FF // 2


def _ffn_up(name, u, w_f, tm, after=()):
    t, d = u.shape
    after_ops, after_specs = _after_operands(after)

    def body(u_ref, wg_ref, wu_ref, *rest):
        pg_ref, pu_ref, a_ref = rest[len(after_ops):]
        uv = u_ref[...]
        for lo in range(0, D_FF, FFN_BLOCK):
            cols = slice(lo, lo + FFN_BLOCK)
            g = _dot(uv, wg_ref[cols, :], NT)
            up = _dot(uv, wu_ref[cols, :], NT)
            sg = _sigmoid(g)
            silu = g * sg
            a_ref[:, cols] = (silu * up).astype(BF16)
            pu_ref[:, cols] = (0.5 * silu).astype(BF16)
            pg_ref[:, cols] = (0.5 * sg * (1.0 + g * (1.0 - sg)) * up).astype(BF16)

    hid = BS((tm, D_FF), lambda i: (i, 0))
    shape = SDS((t, D_FF), BF16)
    whole = BS((D_FF, d), lambda i: (0, 0))
    return pl.pallas_call(
        body, out_shape=(shape, shape, shape), grid=(t // tm,),
        in_specs=[BS((tm, d), lambda i: (i, 0)), whole, whole] + after_specs,
        out_specs=(hid, hid, hid), name=name,
        compiler_params=_params("parallel"))(u, w_f["gate"], w_f["up"], *after_ops)


def _ffn_down(name, a, w_f, resid, tm, next_gain=None, head=None):
    t, d = resid.shape
    row = BS((tm, d), lambda i: (i, 0))
    vec = BS((1, d), lambda i: (0, 0))

    def body(a_ref, w_ref, res_ref, *rest):
        h = res_ref[...] + 0.5 * _dot(a_ref[...], w_ref[...])
        if head is not None:
            _loss_head_tile(pl.program_id(0), h, *rest)
        else:
            g_ref, h_ref, u_ref = rest
            h_ref[...] = h
            _norm_tile(h, g_ref, u_ref)

    if head is not None:
        extra, extra_specs = list(head), [vec, row]
        out_shape = (SDS((1, 128), F32), SDS((t, d), F32), SDS((t, d), BF16), SDS((1, d), F32))
        out_specs = (BS((1, 128), lambda i: (0, 0)), row, row, vec)
    else:
        extra, extra_specs = [next_gain], [vec]
        out_shape = (SDS((t, d), F32), SDS((t, d), BF16))
        out_specs = (row, row)
    return pl.pallas_call(
        body, out_shape=out_shape, grid=(t // tm,),
        in_specs=[BS((tm, D_FF), lambda i: (i, 0)), BS((D_FF, d), lambda i: (0, 0)), row] + extra_specs,
        out_specs=out_specs, name=name,
        compiler_params=_params("arbitrary" if head is not None else "parallel"))(a, w_f["down"], resid, *extra)


def _mix_in(u, w_t, tm, after=()):
    t, d = u.shape
    after_ops, after_specs = _after_operands(after)

    def body(u_ref, w_ref, *rest):
        o_ref, s_ref = rest[-2:]
        uv = u_ref[...]
        for lo in range(0, D_FF, FFN_BLOCK):
            o_ref[:, lo:lo + FFN_BLOCK] = _dot(uv, w_ref[lo:lo + FFN_BLOCK, :], NT)
        s_ref[...] = o_ref[:, D_POOL:D_POOL + D_SSM].astype(BF16)

    return pl.pallas_call(
        body, out_shape=(SDS((t, D_FF), F32), SDS((t, D_SSM), BF16)), grid=(t // tm,),
        in_specs=[BS((tm, d), lambda i: (i, 0)), BS((D_FF, d), lambda i: (0, 0))] + after_specs,
        out_specs=(BS((tm, D_FF), lambda i: (i, 0)), BS((tm, D_SSM), lambda i: (i, 0))), name="mix_in",
        compiler_params=_params("parallel"))(u, w_t, *after_ops)


def _mm_resid_norm(name, a, b, resid, next_gain, tm):
    t, d = resid.shape
    tm = min(2 * tm, t)

    def body(a_ref, b_ref, res_ref, g_ref, h_ref, u_ref):
        h = res_ref[...] + _dot(a_ref[...], b_ref[...])
        h_ref[...] = h
        _norm_tile(h, g_ref, u_ref)

    row = BS((tm, d), lambda i: (i, 0))
    return pl.pallas_call(
        body, out_shape=(SDS((t, d), F32), SDS((t, d), BF16)), grid=(t // tm,),
        in_specs=[BS((tm, a.shape[1]), lambda i: (i, 0)), BS(b.shape, lambda i: (0, 0)), row, BS((1, d), lambda i: (0, 0))],
        out_specs=(row, row), name=name, compiler_params=_params("parallel"))(a, b, resid, next_gain)


def _ffn_bwd_act(name, dh_b, w_f, pg, pu, tm, after=()):
    t, d = dh_b.shape
    after_ops, after_specs = _after_operands(after)

    def body(dh_ref, wd_ref, pg_ref, pu_ref, *rest):
        dg_ref, dup_ref = rest[len(after_ops):]
        dh = dh_ref[...]
        for lo in range(0, D_FF, FFN_BLOCK):
            cols = slice(lo, lo + FFN_BLOCK)
            da = _dot(dh, wd_ref[cols, :], NT)
            dg_ref[:, cols] = (da * pg_ref[:, cols].astype(F32)).astype(BF16)
            dup_ref[:, cols] = (da * pu_ref[:, cols].astype(F32)).astype(BF16)

    hid = BS((tm, D_FF), lambda i: (i, 0))
    shape = SDS((t, D_FF), BF16)
    return pl.pallas_call(
        body, out_shape=(shape, shape), grid=(t // tm,),
        in_specs=[BS((tm, d), lambda i: (i, 0)), BS((D_FF, d), lambda i: (0, 0)), hid, hid] + after_specs,
        out_specs=(hid, hid), name=name,
        compiler_params=_params("parallel"))(dh_b, w_f["down"], pg, pu, *after_ops)


def _ffn_dw(name, u, dg, dup, a, dh_b, tm):
    t, d = u.shape
    n_t = t // tm

    def body(u_ref, dg_ref, dup_ref, a_ref, dh_ref, og_ref, ou_ref, od_ref, acc):
        i = pl.program_id(1)

        @pl.when(i == 0)
        def _():
            acc[...] = jnp.zeros_like(acc)

        uv = u_ref[...]
        acc[0] += _dot(dg_ref[...], uv, TN)
        acc[1] += _dot(dup_ref[...], uv, TN)
        acc[2] += _dot(a_ref[...], dh_ref[...], TN)

        @pl.when(i == n_t - 1)
        def _():
            og_ref[...] = acc[0].astype(BF16)
            ou_ref[...] = acc[1].astype(BF16)
            od_ref[...] = (0.5 * acc[2]).astype(BF16)

    hid = BS((tm, FFN_BLOCK), lambda j, i: (i, j))
    row = BS((tm, d), lambda j, i: (i, 0))
    out = BS((FFN_BLOCK, d), lambda j, i: (j, 0))
    shape = SDS((D_FF, d), BF16)
    return pl.pallas_call(
        body, out_shape=(shape, shape, shape), grid=(D_FF // FFN_BLOCK, n_t),
        in_specs=[row, hid, hid, hid, row], out_specs=(out, out, out),
        scratch_shapes=[pltpu.VMEM((3, FFN_BLOCK, d), F32)],
        name=name, compiler_params=_params("parallel", "arbitrary"))(u, dg, dup, a, dh_b)


def _norm_bwd_tile(i, du, h_ref, g_ref, dhin_ref, dh_ref, dhb_ref, dg_ref):
    hv = h_ref[...]
    r = lax.rsqrt(jnp.mean(hv * hv, axis=-1, keepdims=True) + EPS)
    n = hv * r
    dn = du * g_ref[...]
    dh = dhin_ref[...] + r * (dn - n * jnp.mean(dn * n, axis=-1, keepdims=True))
    dh_ref[...] = dh
    dhb_ref[...] = dh.astype(BF16)

    @pl.when(i == 0)
    def _():
        dg_ref[...] = jnp.zeros_like(dg_ref)

    dg_ref[...] += jnp.sum(du * n, axis=0, keepdims=True)


def _norm_bwd_specs(tm):
    row = BS((tm, D_MODEL), lambda i: (i, 0))
    vec = BS((1, D_MODEL), lambda i: (0, 0))
    return [row, vec, row], (row, row, vec)


def _norm_bwd_shapes(t):
    return SDS((t, D_MODEL), F32), SDS((t, D_MODEL), BF16), SDS((1, D_MODEL), F32)


def _ffn_dx(name, dg, dup, w_f, h, gain, dh_in, tm, after=()):
    t = dg.shape[0]
    tm = tm // 2
    after_ops, after_specs = _after_operands(after)

    def body(dg_ref, dup_ref, wg_ref, wu_ref, h_ref, g_ref, dhin_ref, *rest):
        du = _dot(dg_ref[...], wg_ref[...]) + _dot(dup_ref[...], wu_ref[...])
        _norm_bwd_tile(pl.program_id(0), du, h_ref, g_ref, dhin_ref, *rest[len(after_ops):])

    hid = BS((tm, D_FF), lambda i: (i, 0))
    whole = BS((D_FF, D_MODEL), lambda i: (0, 0))
    norm_in, norm_out = _norm_bwd_specs(tm)
    return pl.pallas_call(
        body, out_shape=_norm_bwd_shapes(t), grid=(t // tm,),
        in_specs=[hid, hid, whole, whole] + norm_in + after_specs, out_specs=norm_out, name=name,
        compiler_params=_params("arbitrary"))(dg, dup, w_f["gate"], w_f["up"], h, gain, dh_in, *after_ops)


def _mm_norm_bwd(name, a, b, dims, h, gain, dh_in, tm, after=()):
    pieces = list(a) if isinstance(a, (list, tuple)) else [a]
    assert len(pieces) == 1 or dims == NN
    t = pieces[0].shape[0]
    widths = [p.shape[1] for p in pieces]
    row0 = [sum(widths[:j]) for j in range(len(pieces))]
    after_ops, after_specs = _after_operands(after)

    def body(*refs):
        a_refs, (b_ref, h_ref, g_ref, dhin_ref) = refs[:len(pieces)], refs[len(pieces):len(pieces) + 4]
        outs = refs[len(pieces) + 4 + len(after_ops):]
        if len(pieces) == 1:
            du = _dot(a_refs[0][...], b_ref[...], dims)
        else:
            du = _dot(a_refs[0][...], b_ref[0:widths[0], :])
            for a_ref, r0, w in zip(a_refs[1:], row0[1:], widths[1:]):
                du = du + _dot(a_ref[...], b_ref[r0:r0 + w, :])
        _norm_bwd_tile(pl.program_id(0), du, h_ref, g_ref, dhin_ref, *outs)

    norm_in, norm_out = _norm_bwd_specs(tm)
    return pl.pallas_call(
        body, out_shape=_norm_bwd_shapes(t), grid=(t // tm,),
        in_specs=[BS((tm, w), lambda i: (i, 0)) for w in widths] + [BS(b.shape, lambda i: (0, 0))] + norm_in + after_specs,
        out_specs=norm_out, name=name, compiler_params=_params("arbitrary"))(*pieces, b, h, gain, dh_in, *after_ops)


def _plain_mm(name, a, b, dims, out_dtype, tm, resid=None, after=()):
    t = a.shape[0]
    tm = min(2 * tm, t)
    n = b.shape[1] if dims == NN else b.shape[0]
    extras = [(resid, BS((tm, n), lambda i: (i, 0)))] if resid is not None else []
    epi = (lambda acc, res: res + acc) if resid is not None else None
    return _mm(name, [(a, BS((tm, a.shape[1]), lambda i: (i, 0)), b, BS(b.shape, lambda i: (0, 0)), dims)],
               grid=(t // tm,), out_shape=SDS((t, n), out_dtype), out_spec=BS((tm, n), lambda i: (i, 0)),
               extras=extras, epilogue=epi, after=after)


def _dw_mm(name, a, b, tm, out_dtype=BF16, after=()):
    t, k = a.shape
    n = b.shape[1]
    tm = min(2 * tm, t)
    return _mm(name, [(a, BS((tm, k), lambda i: (i, 0)), b, BS((tm, n), lambda i: (i, 0)), TN)],
               grid=(t // tm,), red_axis=0, out_shape=SDS((k, n), out_dtype), out_spec=BS((k, n), lambda i: (0, 0)),
               after=after)


def _mix_in_dw(pieces, u, tm):
    t, d = u.shape
    tm = min(2 * tm, t)
    n_steps = t // tm
    widths = [p.shape[1] for p in pieces]
    row0 = [sum(widths[:j]) for j in range(len(pieces))]
    assert sum(widths) == D_FF

    def body(*refs):
        p_refs, u_ref, o_ref, acc_ref = refs[:len(pieces)], refs[len(pieces)], refs[-2], refs[-1]
        k = pl.program_id(0)

        @pl.when(k == 0)
        def _():
            acc_ref[...] = jnp.zeros_like(acc_ref)

        uv = u_ref[...]
        for p_ref, r0, w in zip(p_refs, row0, widths):
            acc_ref[r0:r0 + w, :] += _dot(p_ref[...], uv, TN)

        @pl.when(k == n_steps - 1)
        def _():
            o_ref[...] = acc_ref[...].astype(BF16)

    return pl.pallas_call(
        body, out_shape=SDS((D_FF, d), BF16), grid=(n_steps,),
        in_specs=[BS((tm, w), lambda i: (i, 0)) for w in widths] + [BS((tm, d), lambda i: (i, 0))],
        out_specs=BS((D_FF, d), lambda i: (0, 0)), scratch_shapes=[pltpu.VMEM((D_FF, d), F32)],
        name="mix_in_dw", compiler_params=_params("arbitrary"))(*pieces, u)


POOL_CHUNK = 256
POOL_HALO = 8


def _window_sum(v, width, lead):
    n = v.shape[0]
    s = v
    k = 1
    while k < width:
        s = s + pltpu.roll(s, n - k, 0)
        k *= 2
    return pltpu.roll(s, lead, 0) if lead else s


def _pool_count(base, left, right, t, shape):
    pos = base + lax.broadcasted_iota(jnp.int32, shape, 0)
    lo = jnp.maximum(pos - left, 0)
    hi = jnp.minimum(pos + right + 1, t)
    return (hi - lo).astype(F32)


def _pool_fwd(proj, pool_w, pool_scale):
    t = proj.shape[0]
    c, h = POOL_CHUNK, POOL_HALO
    n_chunks = t // c

    def body(proj_hbm, pw_ref, sc_ref, pooled_ref, mixed_ref, ms_ref, pad_ref, sem):
        cp = pltpu.make_async_copy(proj_hbm.at[:, pl.ds(0, D_POOL)], pad_ref.at[pl.ds(h, t), :], sem)
        cp.start()
        pad_ref[pl.ds(0, h), :] = jnp.zeros((h, D_POOL), F32)
        pad_ref[pl.ds(t + h, h), :] = jnp.zeros((h, D_POOL), F32)
        cp.wait()
        for g, width in enumerate(POOL_WINDOWS):
            left = width // 2
            right = width - 1 - left
            cols = slice(g * POOL_GROUP, (g + 1) * POOL_GROUP)
            wmat = pw_ref[g].astype(BF16)
            scale = sc_ref[:, cols]

            def chunk(ci, carry, left=left, right=right, width=width, cols=cols, wmat=wmat, scale=scale):
                base = pl.multiple_of(ci * c, c)
                v = pad_ref[pl.ds(base, c + 2 * h), cols]
                win = _window_sum(v, width, left)[h:h + c]
                cnt = _pool_count(base, left, right, t, (c, POOL_GROUP))
                pooled = (win / cnt - v[h:h + c]).astype(BF16)
                mixed = _dot(pooled, wmat)
                pooled_ref[pl.ds(base, c), cols] = pooled
                mixed_ref[pl.ds(base, c), cols] = mixed.astype(BF16)
                ms_ref[pl.ds(base, c), cols] = (mixed * scale).astype(BF16)
                return carry

            lax.fori_loop(0, n_chunks, chunk, 0)

    vm = BS(memory_space=pltpu.VMEM)
    shape = SDS((t, D_POOL), BF16)
    return pl.pallas_call(
        body, out_shape=(shape, shape, shape),
        in_specs=[BS(memory_space=pl.ANY), vm, vm], out_specs=(vm, vm, vm),
        scratch_shapes=[pltpu.VMEM((t + 2 * h, D_POOL), F32), pltpu.SemaphoreType.DMA],
        name="pool_fwd", compiler_params=_params())(proj, pool_w, pool_scale)


def _pool_bwd(d_ms, mixed, pooled, pool_w, pool_scale):
    t = d_ms.shape[0]
    c, h = POOL_CHUNK, POOL_HALO
    n_chunks = t // c

    def body(dms_ref, mixed_ref, pooled_ref, pw_ref, sc_ref, dp_ref, dsc_ref, dpw_ref, pad_ref):
        pad_ref[pl.ds(0, h), :] = jnp.zeros((h, D_POOL), F32)
        pad_ref[pl.ds(t + h, h), :] = jnp.zeros((h, D_POOL), F32)
        for g, width in enumerate(POOL_WINDOWS):
            left = width // 2
            right = width - 1 - left
            cols = slice(g * POOL_GROUP, (g + 1) * POOL_GROUP)
            wmat = pw_ref[g].astype(BF16)
            scale = sc_ref[:, cols]

            def first(ci, carry, left=left, right=right, cols=cols, wmat=wmat, scale=scale):
                dsc, dpw = carry
                base = pl.multiple_of(ci * c, c)
                dms = dms_ref[pl.ds(base, c), cols].astype(F32)
                dsc = dsc + jnp.sum(dms * mixed_ref[pl.ds(base, c), cols].astype(F32), axis=0, keepdims=True)
                dmix = (dms * scale).astype(BF16)
                dpw = dpw + _dot(pooled_ref[pl.ds(base, c), cols], dmix, TN)
                dpooled = _dot(dmix, wmat, NT)
                cnt = _pool_count(base, left, right, t, (c, POOL_GROUP))
                pad_ref[pl.ds(base + h, c), cols] = dpooled / cnt
                return dsc, dpw

            dsc, dpw = lax.fori_loop(0, n_chunks, first,
                                     (jnp.zeros((1, POOL_GROUP), F32), jnp.zeros((POOL_GROUP, POOL_GROUP), F32)))
            dsc_ref[:, cols] = dsc
            dpw_ref[g] = dpw

            def second(ci, carry, left=left, right=right, width=width, cols=cols):
                base = pl.multiple_of(ci * c, c)
                v = pad_ref[pl.ds(base, c + 2 * h), cols]
                win = _window_sum(v, width, right)[h:h + c]
                cnt = _pool_count(base, left, right, t, (c, POOL_GROUP))
                dp_ref[pl.ds(base, c), cols] = (win - v[h:h + c] * cnt).astype(BF16)
                return carry

            lax.fori_loop(0, n_chunks, second, 0)

    vm = BS(memory_space=pltpu.VMEM)
    return pl.pallas_call(
        body, out_shape=(SDS((t, D_POOL), BF16), SDS((1, D_POOL), F32), SDS((4, POOL_GROUP, POOL_GROUP), F32)),
        in_specs=[vm] * 5, out_specs=(vm, vm, vm),
        scratch_shapes=[pltpu.VMEM((t + 2 * h, D_POOL), F32)],
        name="pool_bwd", compiler_params=_params())(d_ms, mixed, pooled, pool_w, pool_scale)


SSM_ROWS = 2 * SSM_GROUPS * SSM_GROUP
SSM_HALF = SSM_GROUPS * SSM_GROUP


def _ssm_zoh(a_r, a_i, ldt):
    dt = jnp.exp(ldt)
    mag = jnp.exp(dt * a_r)
    ang = dt * a_i
    cs, sn = jnp.cos(ang), jnp.sin(ang)
    abr, abi = mag * cs, mag * sn
    den = a_r * a_r + a_i * a_i
    nr = abr - 1.0
    qr = (nr * a_r + abi * a_i) / den
    qi = (abi * a_r - nr * a_i) / den
    return dt, mag, cs, sn, abr, abi, den, nr, qr, qi


def _ssm_group_mask():
    row = lax.broadcasted_iota(jnp.int32, (SSM_HALF, SSM_CH), 0)
    col = lax.broadcasted_iota(jnp.int32, (SSM_HALF, SSM_CH), 1)
    return (row // SSM_GROUP) == (col // SSM_STATE)


def _ssm_prep(a_r, a_i, ldt, b_r, b_i, c_r, c_i, after=()):
    after_ops, after_specs = _after_operands(after)

    def body(ar_ref, ai_ref, ldt_ref, br_ref, bi_ref, cr_ref, ci_ref, *rest):
        abr_ref, abi_ref, win_ref, wint_ref, woutt_ref, wout_ref = rest[len(after_ops):]
        *_, abr, abi, _, _, qr, qi = _ssm_zoh(ar_ref[...], ai_ref[...], ldt_ref[...])
        abr_ref[...] = abr
        abi_ref[...] = abi
        b_r, b_i = br_ref[...], bi_ref[...]
        bbr = qr * b_r - qi * b_i
        bbi = qr * b_i + qi * b_r
        mask = _ssm_group_mask()
        state = lax.broadcasted_iota(jnp.int32, (SSM_STATE, SSM_CH), 0)
        col = lax.broadcasted_iota(jnp.int32, (SSM_STATE, SSM_CH), 1)
        every_group = (col % SSM_STATE == state).astype(BF16)

        def spread(x):
            return jnp.where(mask, _dot(x, every_group), 0.0)

        for d in range(2):
            rows = slice(d * SSM_HALF, (d + 1) * SSM_HALF)
            for half, x_in, x_out in ((0, bbr[rows], cr_ref[rows, :]), (1, bbi[rows], -ci_ref[rows, :])):
                cols = slice(half * SSM_CH, (half + 1) * SSM_CH)
                m_in, m_out = spread(x_in), spread(x_out)
                win_ref[d, :, cols] = m_in.astype(BF16)
                wint_ref[d, cols, :] = m_in.T.astype(BF16)
                woutt_ref[d, :, cols] = m_out.astype(BF16)
                wout_ref[d, cols, :] = m_out.T.astype(BF16)

    vm = BS(memory_space=pltpu.VMEM)
    vec = SDS((SSM_ROWS, SSM_STATE), F32)
    wide = SDS((2, SSM_HALF, 2 * SSM_CH), BF16)
    tall = SDS((2, 2 * SSM_CH, SSM_HALF), BF16)
    return pl.pallas_call(body, out_shape=(vec, vec, wide, tall, wide, tall), in_specs=[vm] * 7 + after_specs,
                          out_specs=(vm,) * 6, name="ssm_prep",
                          compiler_params=_params())(a_r, a_i, ldt, b_r, b_i, c_r, c_i, *after_ops)


def _ssm_prep_bwd(a_r, a_i, ldt, b_r, b_i, d_abr, d_abi, d_win, d_woutt):
    def body(ar_ref, ai_ref, ldt_ref, br_ref, bi_ref, *rest):
        (dabr_refs, dabi_refs, dwin_refs, dwoutt_refs), outs = [rest[2 * k:2 * k + 2] for k in range(4)], rest[8:]
        dar_ref, dai_ref, dldt_ref, dbr_ref, dbi_ref, dcr_ref, dci_ref = outs
        a_r, a_i = ar_ref[...], ai_ref[...]
        dt, mag, cs, sn, abr, abi, den, nr, qr, qi = _ssm_zoh(a_r, a_i, ldt_ref[...])
        mask = _ssm_group_mask()
        col = lax.broadcasted_iota(jnp.int32, (SSM_CH, SSM_STATE), 0)
        state = lax.broadcasted_iota(jnp.int32, (SSM_CH, SSM_STATE), 1)
        own_state = (col % SSM_STATE == state).astype(BF16)

        def pick(dense):
            m = jnp.where(mask, dense, 0.0)
            hi = m.astype(BF16)
            lo = m - hi.astype(F32)
            return _dot(hi, own_state) + _dot(lo, own_state)

        def picked(refs, half):
            cols = slice(half * SSM_CH, (half + 1) * SSM_CH)
            return jnp.concatenate([pick(ref[:, cols]) for ref in refs], axis=0)

        first_channel = lax.broadcasted_iota(jnp.int32, (SSM_HALF, SSM_CH), 0) % SSM_GROUP == 0

        def first_rows(refs):
            return jnp.concatenate(
                [pick(jnp.where(first_channel, jnp.broadcast_to(ref[...], (SSM_HALF, SSM_CH)), 0.0)) for ref in refs], axis=0)

        g_r, g_i = picked(dwin_refs, 0), picked(dwin_refs, 1)
        dcr_ref[...] = picked(dwoutt_refs, 0)
        dci_ref[...] = -picked(dwoutt_refs, 1)
        b_r, b_i = br_ref[...], bi_ref[...]
        dbr_ref[...] = g_r * qr + g_i * qi
        dbi_ref[...] = g_i * qr - g_r * qi
        gqr = g_r * b_r + g_i * b_i
        gqi = g_i * b_r - g_r * b_i
        g_nr_num = gqr / den
        g_ni_num = gqi / den
        g_den = -(gqr * qr + gqi * qi) / den
        g_nr = g_nr_num * a_r - g_ni_num * a_i
        g_abi = g_nr_num * a_i + g_ni_num * a_r
        d_ar = g_nr_num * nr + g_ni_num * abi + 2.0 * a_r * g_den
        d_ai = g_nr_num * abi - g_ni_num * nr + 2.0 * a_i * g_den
        g_abr = first_rows(dabr_refs) + g_nr
        g_abi = first_rows(dabi_refs) + g_abi
        g_mag = g_abr * cs + g_abi * sn
        g_ang = mag * (g_abi * cs - g_abr * sn)
        g_e = g_mag * mag
        d_ar = d_ar + g_e * dt
        d_ai = d_ai + g_ang * dt
        g_dt = g_e * a_r + g_ang * a_i
        dar_ref[...] = d_ar
        dai_ref[...] = d_ai
        dldt_ref[...] = g_dt * dt

    vm = BS(memory_space=pltpu.VMEM)
    vec = SDS((SSM_ROWS, SSM_STATE), F32)
    return pl.pallas_call(body, out_shape=(vec,) * 7, in_specs=[vm] * 13, out_specs=(vm,) * 7, name="ssm_prep_bwd",
                          compiler_params=_params())(a_r, a_i, ldt, b_r, b_i, *d_abr, *d_abi, *d_win, *d_woutt)


SCAN_ROWS = 512
SCAN_SUB = 128


def _ssm_scan(name, inp, w1, a_r, a_i, w2, reverse, dr, conj=False):
    t = inp.shape[0]
    rows = min(SCAN_ROWS, t)
    n = t // rows
    n_sub = rows // SCAN_SUB
    ch = SSM_CH
    at = (lambda i: (n - 1 - i, 0)) if reverse else (lambda i: (i, 0))

    def body(in_ref, w1_ref, ar_ref, ai_ref, w2_ref, sb_ref, out_ref, cr_ref, ci_ref, k_ref, st_ref):
        i = pl.program_id(0)

        @pl.when(i == 0)
        def _():
            ar8 = jnp.broadcast_to(ar_ref[...], (8, ch))
            ai8 = jnp.broadcast_to(-ai_ref[...] if conj else ai_ref[...], (8, ch))
            row = lax.broadcasted_iota(jnp.int32, (8, ch), 0)
            rank = (7 - row) if reverse else row
            powers = [(ar8, ai8)]
            for _ in range(7):
                p_r, p_i = powers[-1]
                powers.append((p_r * ar8 - p_i * ai8, p_r * ai8 + p_i * ar8))
            zero = jnp.zeros((8, ch), F32)
            for slot, k in enumerate((1, 2, 4)):
                k_ref[2 * slot] = jnp.where(rank >= k, powers[k - 1][0], zero)
                k_ref[2 * slot + 1] = jnp.where(rank >= k, powers[k - 1][1], zero)
            carry_r, carry_i = zero, zero
            for j in range(8):
                carry_r = jnp.where(rank == j, powers[j][0], carry_r)
                carry_i = jnp.where(rank == j, powers[j][1], carry_i)
            k_ref[6] = carry_r
            k_ref[7] = carry_i
            cr_ref[...] = zero
            ci_ref[...] = zero

        def group(r0, carry):
            c_r, c_i = carry
            x_r = st_ref[pl.ds(r0, 8), 0:ch]
            x_i = st_ref[pl.ds(r0, 8), ch:2 * ch]
            for slot, k in enumerate((1, 2, 4)):
                shift = (8 - k) if reverse else k
                s_r = pltpu.roll(x_r, shift, 0)
                s_i = pltpu.roll(x_i, shift, 0)
                m_r, m_i = k_ref[2 * slot], k_ref[2 * slot + 1]
                x_r, x_i = x_r + m_r * s_r - m_i * s_i, x_i + m_r * s_i + m_i * s_r
            p_r, p_i = k_ref[6], k_ref[7]
            x_r, x_i = x_r + p_r * c_r - p_i * c_i, x_i + p_r * c_i + p_i * c_r
            st_ref[pl.ds(r0, 8), 0:ch] = x_r
            st_ref[pl.ds(r0, 8), ch:2 * ch] = x_i
            last = 0 if reverse else 7
            return (jnp.broadcast_to(x_r[last:last + 1, :], (8, ch)), jnp.broadcast_to(x_i[last:last + 1, :], (8, ch)))

        carry = (cr_ref[...], ci_ref[...])
        for sc in (range(n_sub - 1, -1, -1) if reverse else range(n_sub)):
            part = pl.ds(sc * SCAN_SUB, SCAN_SUB)
            st_ref[part, :] = _dot(in_ref[part, :], w1_ref[...])
            for gi in range(SCAN_SUB // 8):
                g = (SCAN_SUB // 8 - 1 - gi) if reverse else gi
                carry = group(sc * SCAN_SUB + g * 8, carry)
            states = st_ref[part, :].astype(BF16)
            sb_ref[part, :] = states
            out_ref[part, :] = _dot(states, w2_ref[...])
        cr_ref[...] = carry[0]
        ci_ref[...] = carry[1]

    return pl.pallas_call(
        body, out_shape=(SDS((t, 2 * ch), BF16), SDS((t, D_SSM), F32)), grid=(n,),
        in_specs=[BS((rows, D_SSM), at), BS((None, D_SSM, 2 * ch), lambda i: (dr, 0, 0)), BS((None, 1, ch), lambda i: (dr, 0, 0)),
                  BS((None, 1, ch), lambda i: (dr, 0, 0)), BS((None, 2 * ch, D_SSM), lambda i: (dr, 0, 0))],
        out_specs=(BS((rows, 2 * ch), at), BS((rows, D_SSM), at)),
        scratch_shapes=[pltpu.VMEM((8, ch), F32), pltpu.VMEM((8, ch), F32), pltpu.VMEM((8, 8, ch), F32),
                        pltpu.VMEM((rows, 2 * ch), F32)],
        name=name, compiler_params=_params("arbitrary"))(inp, w1, a_r, a_i, w2)


DA_ROWS = 1024


def _ssm_param_grads(name, lam, states, u, dy, reverse, after=()):
    t = lam.shape[0]
    rows = min(DA_ROWS, t)
    n = t // rows
    halo_rows = 16
    nb = rows // halo_rows
    ch = SSM_CH
    if reverse:
        halo_at = lambda i: (jnp.minimum((i + 1) * nb, t // halo_rows - 1), 0)
    else:
        halo_at = lambda i: (jnp.maximum(i * nb - 1, 0), 0)

    after_ops, after_specs = _after_operands(after)

    def body(lam_ref, x_ref, halo_ref, u_ref, dy_ref, *rest):
        dr_ref, di_ref, dwin_ref, dwoutt_ref = rest[len(after_ops):]
        i = pl.program_id(0)

        @pl.when(i == 0)
        def _():
            dr_ref[...] = jnp.zeros_like(dr_ref)
            di_ref[...] = jnp.zeros_like(di_ref)
            dwin_ref[...] = jnp.zeros_like(dwin_ref)
            dwoutt_ref[...] = jnp.zeros_like(dwoutt_ref)

        dwin_ref[...] += _dot(u_ref[...], lam_ref[...], TN)
        dwoutt_ref[...] += _dot(dy_ref[...], x_ref[...], TN)
        row = lax.broadcasted_iota(jnp.int32, (rows, ch), 0)
        if reverse:
            edge, shift, h_row, live = rows - 1, rows - 1, 0, i < n - 1
        else:
            edge, shift, h_row, live = 0, 1, halo_rows - 1, i > 0

        def neighbour(lo):
            halo = halo_ref[:, lo:lo + ch].astype(F32)[h_row:h_row + 1]
            halo = jnp.where(live, halo, 0.0)
            x = x_ref[:, lo:lo + ch].astype(F32)
            return jnp.where(row == edge, jnp.broadcast_to(halo, (rows, ch)), pltpu.roll(x, shift, 0))

        xp_r, xp_i = neighbour(0), neighbour(ch)
        l_r, l_i = lam_ref[:, 0:ch].astype(F32), lam_ref[:, ch:2 * ch].astype(F32)
        dr_ref[...] += jnp.sum(l_r * xp_r + l_i * xp_i, axis=0, keepdims=True)
        di_ref[...] += jnp.sum(l_i * xp_r - l_r * xp_i, axis=0, keepdims=True)

    blk = BS((rows, 2 * ch), lambda i: (i, 0))
    thin = BS((rows, D_SSM), lambda i: (i, 0))
    vec = BS((1, ch), lambda i: (0, 0))
    mat = BS((D_SSM, 2 * ch), lambda i: (0, 0))
    return pl.pallas_call(
        body, out_shape=(SDS((1, ch), F32), SDS((1, ch), F32), SDS((D_SSM, 2 * ch), F32), SDS((D_SSM, 2 * ch), F32)),
        grid=(n,), in_specs=[blk, blk, BS((halo_rows, 2 * ch), halo_at), thin, thin] + after_specs,
        out_specs=(vec, vec, mat, mat),
        name=name, compiler_params=_params("arbitrary"))(lam, states, states, u, dy, *after_ops)


GELU_C = math.sqrt(2.0 / math.pi)
GELU_K = 0.044715


def _ssm_combine(proj, y_fwd, y_bwd, d_skip, tm, after=()):
    t = proj.shape[0]
    after_ops, after_specs = _after_operands(after)

    def body(s_ref, yf_ref, yb_ref, d_ref, *rest):
        yt_ref, g_ref = rest[len(after_ops):]
        y = s_ref[...] * d_ref[...] + yf_ref[...] + yb_ref[...]
        yt_ref[...] = y
        th = jnp.tanh(GELU_C * (y + GELU_K * y * y * y))
        g_ref[...] = (0.5 * y * (1.0 + th)).astype(BF16)

    blk = BS((tm, D_SSM), lambda i: (i, 0))
    return pl.pallas_call(
        body, out_shape=(SDS((t, D_SSM), F32), SDS((t, D_SSM), BF16)), grid=(t // tm,),
        in_specs=[BS((tm, D_SSM), lambda i: (i, D_POOL // D_SSM)), blk, blk, BS((1, D_SSM), lambda i: (0, 0))] + after_specs,
        out_specs=(blk, blk), name="ssm_combine",
        compiler_params=_params("parallel"))(proj, y_fwd, y_bwd, d_skip, *after_ops)


def _ssm_ds(proj, d_yt, du_fwd, du_bwd, d_skip, tm):
    t = proj.shape[0]

    def body(s_ref, dy_ref, duf_ref, dub_ref, d_ref, ds_ref, dd_ref):
        i = pl.program_id(0)
        dy = dy_ref[...]
        ds_ref[...] = (dy * d_ref[...] + duf_ref[...] + dub_ref[...]).astype(BF16)

        @pl.when(i == 0)
        def _():
            dd_ref[...] = jnp.zeros_like(dd_ref)

        dd_ref[...] += jnp.sum(dy * s_ref[...], axis=0, keepdims=True)

    blk = BS((tm, D_SSM), lambda i: (i, 0))
    vec = BS((1, D_SSM), lambda i: (0, 0))
    return pl.pallas_call(
        body, out_shape=(SDS((t, D_SSM), BF16), SDS((1, D_SSM), F32)), grid=(t // tm,),
        in_specs=[BS((tm, D_SSM), lambda i: (i, D_POOL // D_SSM)), blk, blk, blk, vec],
        out_specs=(blk, vec), name="ssm_ds", compiler_params=_params("arbitrary"))(proj, d_yt, du_fwd, du_bwd, d_skip)


G_POOL_AT = D_POOL + D_SSM
G_SSM_AT = G_POOL_AT + D_MODEL
E_VAL, E_GATE = D_POOL, D_POOL + D_SSM


def _merge_specs(tm):
    return [BS((tm, D_POOL), lambda i: (i, 0)), BS((tm, D_SSM), lambda i: (i, 0)),
            BS((N_SHARD, 1024, 256), lambda i: (0, 0, 0)), BS((tm, D_FF), lambda i: (i, 0))]


def _merge_parts(s, ms, yv, w_ref, proj_ref):
    lo = 256 * s
    zp = _dot(ms, w_ref[s, 0:E_VAL, :])
    zv = _dot(yv, w_ref[s, E_VAL:E_GATE, :])
    zg = _dot(yv, w_ref[s, E_GATE:, :])
    return zp, zv, zg, proj_ref[:, G_POOL_AT + lo:G_POOL_AT + lo + 256], proj_ref[:, G_SSM_AT + lo:G_SSM_AT + lo + 256]


def _mixer_merge(ms, yssm, w_e, proj, tm):
    t = ms.shape[0]

    def body(ms_ref, y_ref, w_ref, proj_ref, o_ref):
        msv, yv = ms_ref[...], y_ref[...]
        for s in range(N_SHARD):
            zp, zv, zg, gp, gs = _merge_parts(s, msv, yv, w_ref, proj_ref)
            o_ref[:, 256 * s:256 * (s + 1)] = (_sigmoid(gp) * zp + _sigmoid(gs) * zv * _sigmoid(zg)).astype(BF16)

    row = BS((tm, D_MODEL), lambda i: (i, 0))
    return pl.pallas_call(
        body, out_shape=SDS((t, D_MODEL), BF16), grid=(t // tm,), in_specs=_merge_specs(tm), out_specs=row,
        name="mixer_merge", compiler_params=_params("parallel"))(ms, yssm, w_e, proj)


def _mixer_merge_bwd(ms, yssm, w_e, proj, dmerged, tm):
    t = ms.shape[0]

    def body(ms_ref, y_ref, w_ref, proj_ref, dm_ref, dgp_ref, dgs_ref, dzp_ref, dzv_ref, dzg_ref):
        msv, yv = ms_ref[...], y_ref[...]
        for s in range(N_SHARD):
            cols = slice(256 * s, 256 * (s + 1))
            zp, zv, zg, gp, gs = _merge_parts(s, msv, yv, w_ref, proj_ref)
            dm = dm_ref[:, cols].astype(F32)
            sp, ss, sg = _sigmoid(gp), _sigmoid(gs), _sigmoid(zg)
            dgp_ref[:, cols] = (dm * zp * sp * (1.0 - sp)).astype(BF16)
            dgs_ref[:, cols] = (dm * zv * sg * ss * (1.0 - ss)).astype(BF16)
            dzp_ref[:, cols] = (dm * sp).astype(BF16)
            dz = dm * ss
            dzv_ref[:, cols] = (dz * sg).astype(BF16)
            dzg_ref[:, cols] = (dz * zv * sg * (1.0 - sg)).astype(BF16)

    row = BS((tm, D_MODEL), lambda i: (i, 0))
    shape = SDS((t, D_MODEL), BF16)
    return pl.pallas_call(
        body, out_shape=(shape,) * 5, grid=(t // tm,), in_specs=_merge_specs(tm) + [row],
        out_specs=(row,) * 5, name="mixer_merge_bwd",
        compiler_params=_params("parallel"))(ms, yssm, w_e, proj, dmerged)


def _mixer_dw(ms, yssm, dzp, dzv, dzg, tm):
    t = ms.shape[0]
    tm = min(2 * tm, t)
    n_t = t // tm

    def body(ms_ref, y_ref, dzp_ref, dzv_ref, dzg_ref, o_ref, acc):
        i = pl.program_id(0)

        @pl.when(i == 0)
        def _():
            acc[...] = jnp.zeros_like(acc)

        msv, yv = ms_ref[...], y_ref[...]
        for s in range(N_SHARD):
            cols = slice(256 * s, 256 * (s + 1))
            acc[s, 0:E_VAL, :] += _dot(msv, dzp_ref[:, cols], TN)
            acc[s, E_VAL:E_GATE, :] += _dot(yv, dzv_ref[:, cols], TN)
            acc[s, E_GATE:, :] += _dot(yv, dzg_ref[:, cols], TN)

        @pl.when(i == n_t - 1)
        def _():
            o_ref[...] = acc[...].astype(BF16)

    row = BS((tm, D_MODEL), lambda i: (i, 0))
    full = BS((N_SHARD, 1024, 256), lambda i: (0, 0, 0))
    return pl.pallas_call(
        body, out_shape=SDS((N_SHARD, 1024, 256), BF16), grid=(n_t,),
        in_specs=[BS((tm, D_POOL), lambda i: (i, 0)), BS((tm, D_SSM), lambda i: (i, 0)), row, row, row],
        out_specs=full, scratch_shapes=[pltpu.VMEM((N_SHARD, 1024, 256), F32)],
        name="mixer_dw", compiler_params=_params("arbitrary"))(ms, yssm, dzp, dzv, dzg)


def _mixer_dx(dzp, dzv, dzg, w_e, y_total, tm):
    t = dzp.shape[0]

    def body(dzp_ref, dzv_ref, dzg_ref, w_ref, yt_ref, dms_ref, dy_ref, dyb_ref):
        acc_ms, acc_y = None, None
        for s in range(N_SHARD):
            cols = slice(256 * s, 256 * (s + 1))
            part_ms = _dot(dzp_ref[:, cols], w_ref[s, 0:E_VAL, :], NT)
            part_y = _dot(dzv_ref[:, cols], w_ref[s, E_VAL:E_GATE, :], NT) + _dot(dzg_ref[:, cols], w_ref[s, E_GATE:, :], NT)
            acc_ms = part_ms if s == 0 else acc_ms + part_ms
            acc_y = part_y if s == 0 else acc_y + part_y
        dms_ref[...] = acc_ms.astype(BF16)
        y = yt_ref[...]
        th = jnp.tanh(GELU_C * (y + GELU_K * y * y * y))
        dgelu = 0.5 * (1.0 + th) + 0.5 * y * (1.0 - th * th) * GELU_C * (1.0 + 3.0 * GELU_K * y * y)
        dy = acc_y * dgelu
        dy_ref[...] = dy
        dyb_ref[...] = dy.astype(BF16)

    row = BS((tm, D_MODEL), lambda i: (i, 0))
    narrow = BS((tm, D_SSM), lambda i: (i, 0))
    return pl.pallas_call(
        body, out_shape=(SDS((t, D_POOL), BF16), SDS((t, D_SSM), F32), SDS((t, D_SSM), BF16)), grid=(t // tm,),
        in_specs=[row, row, row, BS((N_SHARD, 1024, 256), lambda i: (0, 0, 0)), narrow],
        out_specs=(BS((tm, D_POOL), lambda i: (i, 0)), narrow, narrow),
        name="mixer_dx", compiler_params=_params("parallel"))(dzp, dzv, dzg, w_e, y_total)


def _attn_probs(q_h, k_h):
    s = _dot(q_h, k_h, NT) * (1.0 / math.sqrt(HEAD_DIM))
    e = jnp.exp(s - jnp.max(s, axis=-1, keepdims=True))
    return e / jnp.sum(e, axis=-1, keepdims=True)


def _attn_fwd(q, kv, tm):
    t = q.shape[0]
    tm = min(2 * tm, t)
    m = kv.shape[0]

    def body(q_ref, kv_ref, o_ref):
        for hd in range(N_HEADS):
            lo = hd * HEAD_DIM
            p = _attn_probs(q_ref[:, lo:lo + HEAD_DIM], kv_ref[:, lo:lo + HEAD_DIM])
            o_ref[:, lo:lo + HEAD_DIM] = _dot(p, kv_ref[:, D_MODEL + lo:D_MODEL + lo + HEAD_DIM]).astype(BF16)

    return pl.pallas_call(
        body, out_shape=SDS((t, D_MODEL), BF16), grid=(t // tm,),
        in_specs=[BS((tm, D_MODEL), lambda i: (i, 0)), BS((m, 2 * D_MODEL), lambda i: (0, 0))],
        out_specs=BS((tm, D_MODEL), lambda i: (i, 0)), name="attn_fwd", compiler_params=_params("parallel"))(q, kv)


def _attn_bwd(q, kv, d_o, tm):
    t = q.shape[0]
    m = kv.shape[0]

    def body(q_ref, kv_ref, do_ref, dq_ref, dkv_ref):
        i = pl.program_id(0)

        @pl.when(i == 0)
        def _():
            dkv_ref[...] = jnp.zeros_like(dkv_ref)

        for hd in range(N_HEADS):
            lo = hd * HEAD_DIM
            q_h = q_ref[:, lo:lo + HEAD_DIM]
            k_h = kv_ref[:, lo:lo + HEAD_DIM]
            v_h = kv_ref[:, D_MODEL + lo:D_MODEL + lo + HEAD_DIM]
            do_h = do_ref[:, lo:lo + HEAD_DIM]
            p = _attn_probs(q_h, k_h)
            dkv_ref[:, D_MODEL + lo:D_MODEL + lo + HEAD_DIM] += _dot(p, do_h, TN)
            dp = _dot(do_h, v_h, NT)
            ds = p * (dp - jnp.sum(dp * p, axis=-1, keepdims=True)) * (1.0 / math.sqrt(HEAD_DIM))
            dq_ref[:, lo:lo + HEAD_DIM] = _dot(ds, k_h).astype(BF16)
            dkv_ref[:, lo:lo + HEAD_DIM] += _dot(ds, q_h, TN)

    row = BS((tm, D_MODEL), lambda i: (i, 0))
    full = BS((m, 2 * D_MODEL), lambda i: (0, 0))
    return pl.pallas_call(
        body, out_shape=(SDS((t, D_MODEL), BF16), SDS((m, 2 * D_MODEL), F32)), grid=(t // tm,),
        in_specs=[row, full, row], out_specs=(row, full), name="attn_bwd",
        compiler_params=_params("arbitrary"))(q, kv, d_o)


TRANSPOSED = ("ffn1_w_gate", "ffn1_w_up", "ffn2_w_gate", "ffn2_w_up", "w_in")
GATHER_PHASES = {"f1a": (("ffn1_w_gate",), ("ffn1_w_up",)),
                 "f1b": (("ffn1_w_down",),),
                 "win": (("w_in",),),
                 "mix": (("w_mix_out",), ("w_q",), ("w_xo",), ("w_kv",), ("w_pool_proj", "w_glu_val", "w_glu_gate")),
                 "f2": (("ffn2_w_gate",), ("ffn2_w_up",), ("ffn2_w_down",))}
REDUCE_GROUPS = (("ffn2_w_gate",), ("ffn2_w_up",), ("ffn2_w_down",), ("w_xo",), ("w_q",), ("w_kv",), ("w_mix_out",),
                 ("w_pool_proj", "w_glu_val", "w_glu_gate"), ("w_in",), ("ffn1_w_gate",), ("ffn1_w_up",), ("ffn1_w_down",))
SMALL = ("ffn1_norm", "mix_norm", "pool_w", "pool_scale", "ssm_a_re", "ssm_a_im", "ssm_log_dt", "ssm_b_re",
         "ssm_b_im", "ssm_c_re", "ssm_c_im", "ssm_d", "xattn_norm", "mem_norm", "ffn2_norm", "final_norm")
WEIGHTS = ("ffn1_norm", "ffn1_w_gate", "ffn1_w_up", "ffn1_w_down", "mix_norm", "w_in", "pool_w", "pool_scale",
           "w_pool_proj", "ssm_a_re", "ssm_a_im", "ssm_log_dt", "ssm_b_re", "ssm_b_im", "ssm_c_re", "ssm_c_im",
           "ssm_d", "w_glu_val", "w_glu_gate", "w_mix_out", "xattn_norm", "mem_norm", "w_q", "w_kv", "w_xo",
           "ffn2_norm", "ffn2_w_gate", "ffn2_w_up", "ffn2_w_down", "final_norm")


def _small_view(a, n):
    return jnp.swapaxes(a, 3, 4) if n in ("ssm_b_re", "ssm_b_im") else a


def _device_step(x, mem, target, wts, sp, reducer=None):
    t = x.shape[0]
    tm = min(TM, t)
    g = {}

    first_gather = wts.start("win", wts.start("f1b", wts.start("f1a")))
    u1 = _rmsnorm("norm_ffn1", x, sp["ffn1_norm"], tm, after=first_gather)

    def per_channel(a):
        a = a.reshape(2 * SSM_GROUPS, 1, -1)
        return jnp.broadcast_to(a, (2 * SSM_GROUPS, SSM_GROUP, a.shape[-1])).reshape(SSM_ROWS, a.shape[-1])

    ssm_a = per_channel(sp["ssm_a_re"]), per_channel(sp["ssm_a_im"]), per_channel(sp["ssm_log_dt"])
    ssm_b = sp["ssm_b_re"].reshape(SSM_ROWS, SSM_STATE), sp["ssm_b_im"].reshape(SSM_ROWS, SSM_STATE)
    abr, abi, w_in_s, w_in_s_t, w_out_s_t, w_out_s = _ssm_prep(
        *ssm_a, *ssm_b, sp["ssm_c_re"].reshape(SSM_ROWS, SSM_STATE), sp["ssm_c_im"].reshape(SSM_ROWS, SSM_STATE),
        after=first_gather)
    first_rows = (2, SSM_GROUPS, SSM_GROUP, SSM_STATE)
    a_r = abr.reshape(first_rows)[:, :, 0].reshape(2, 1, SSM_CH)
    a_i = abi.reshape(first_rows)[:, :, 0].reshape(2, 1, SSM_CH)
    mem_n = _rmsnorm("norm_mem", mem, sp["mem_norm"], mem.shape[0], after=first_gather + wts.sources(("mix", "f2")))

    whole = (D_FF, D_MODEL)
    w_g1, w_u1 = wts.finish("f1a", [u1, w_in_s, w_in_s_t, w_out_s, w_out_s_t, a_r, a_i, mem_n])
    w_f1 = {"gate": w_g1.reshape(whole), "up": w_u1.reshape(whole)}
    g1, up1, a1 = _ffn_up("ffn1_up", u1, w_f1, tm)
    (w_dn,) = wts.finish("f1b", [a1])
    w_f1["down"] = w_dn.reshape(whole)
    h1, u2 = _ffn_down("ffn1_down", a1, w_f1, x, tm, next_gain=sp["mix_norm"])

    (w_in_g,) = wts.finish("win", [u2])
    w_in_t = w_in_g.reshape(D_FF, D_MODEL)
    proj, s_in = _mix_in(u2, w_in_t, tm, after=wts.start("f2", wts.start("mix", [w_in_g])))
    pooled, mixed, ms = _pool_fwd(proj, sp["pool_w"][0], sp["pool_scale"])

    states, y_dirs = [], []
    for dr in range(2):
        st, yd = _ssm_scan(f"ssm_scan_fwd{dr}", s_in, w_in_s, a_r, a_i, w_out_s, reverse=(dr == 1), dr=dr)
        states.append(st)
        y_dirs.append(yd)
    y_total, yssm = _ssm_combine(proj, y_dirs[0], y_dirs[1], sp["ssm_d"], tm, after=wts.fill_start("mix", y_dirs))
    *w_sq, w_kv, w_e = wts.finish("mix", [yssm, ms])
    w_mo, w_q, w_xo = (a.reshape(D_MODEL, D_MODEL) for a in w_sq)
    w_d = w_kv[:, None]

    merged = _mixer_merge(ms, yssm, w_e, proj, tm)
    h2, u3 = _mm_resid_norm("mix_out", merged, w_mo, h1, sp["xattn_norm"], tm)

    q = _plain_mm("attn_q", u3, w_q, NN, BF16, tm)
    n_mem = mem.shape[0]
    kv = _mm("attn_kv", [(mem_n, BS((n_mem, D_MODEL), lambda s: (0, 0)), w_d, BS((None, None, D_MODEL, 512), lambda s: (s, 0, 0, 0)), NN)],
             grid=(N_SHARD,), out_shape=SDS((n_mem, 2 * D_MODEL), BF16), out_spec=BS((n_mem, 512), lambda s: (0, s)))
    o = _attn_fwd(q, kv, tm)
    h3, u4 = _mm_resid_norm("attn_out", o, w_xo, h2, sp["ffn2_norm"], tm)

    w_f2 = dict(zip(("gate", "up", "down"), (a.reshape(whole) for a in wts.finish("f2", [u4]))))
    g2, up2, a2 = _ffn_up("ffn2_up", u4, w_f2, tm)
    loss, dh4, dh4_b, g["final_norm"] = _ffn_down("ffn2_down", a2, w_f2, h3, tm,
                                                  head=(sp["final_norm"].reshape(1, D_MODEL), target))

    dg2, dup2 = _ffn_bwd_act("ffn2_bwd_act", dh4_b, w_f2, g2, up2, tm)
    dw_f2 = _ffn_dw("ffn2_dw", u4, dg2, dup2, a2, dh4_b, tm)
    dh3, dh3_b, g["ffn2_norm"] = _ffn_dx("ffn2_dx", dg2, dup2, w_f2, h3, sp["ffn2_norm"], dh4, tm)

    d_o = _plain_mm("attn_out_dx", dh3_b, w_xo, NT, BF16, tm)
    dw_xo = _dw_mm("attn_out_dw", o, dh3_b, tm)
    dq, dkv = _attn_bwd(q, kv, d_o, tm)
    dw_q = _dw_mm("attn_q_dw", u3, dq, tm)
    dh2, dh2_b, g["xattn_norm"] = _mm_norm_bwd("attn_q_dx", dq, w_q, NT, h2, sp["xattn_norm"], dh3, tm)
    dw_kv = _mm("attn_kv_dw", [(mem_n, BS((n_mem, D_MODEL), lambda s: (0, 0)), dkv, BS((n_mem, 512), lambda s: (0, s)), TN)],
                grid=(N_SHARD,), out_shape=SDS((N_SHARD, D_MODEL, 512), BF16), out_spec=BS((None, D_MODEL, 512), lambda s: (s, 0, 0)))
    dmem_n = _mm("attn_kv_dx", [(dkv, BS((n_mem, 512), lambda s: (0, s)), w_d, BS((None, None, D_MODEL, 512), lambda s: (s, 0, 0, 0)), NT)],
                 grid=(N_SHARD,), red_axis=0, out_shape=SDS((n_mem, D_MODEL), F32), out_spec=BS((n_mem, D_MODEL), lambda s: (0, 0)))
    _, _, g["mem_norm"] = _rmsnorm_bwd("norm_mem_bwd", mem, sp["mem_norm"], dmem_n, None, n_mem)

    square = (N_SHARD, D_MODEL // N_SHARD, D_MODEL)
    sharded = (N_SHARD, FF_SH, D_MODEL)
    early = [a.reshape(sharded) for a in dw_f2] + [dw_xo.reshape(square), dw_q.reshape(square), dw_kv]
    swapping = reducer.swap_start("a1", early) if reducer is not None else []
    dmerged = _plain_mm("mix_out_dx", dh2_b, w_mo, NT, BF16, tm, after=swapping)
    dw_mo = _dw_mm("mix_out_dw", merged, dh2_b, tm)
    d_gp, d_gs, dzp, dzv, dzg = _mixer_merge_bwd(ms, yssm, w_e, proj, dmerged, tm)
    dw_e = _mixer_dw(ms, yssm, dzp, dzv, dzg, tm)
    d_ms, d_yt, d_yt_b = _mixer_dx(dzp, dzv, dzg, w_e, y_total, tm)
    dp, d_scale, d_pw = _pool_bwd(d_ms, mixed, pooled, sp["pool_w"][0], sp["pool_scale"])
    g["pool_scale"] = d_scale
    g["pool_w"] = d_pw[None]

    du_dirs, lams = [], []
    for dr in range(2):
        lam, du = _ssm_scan(f"ssm_scan_bwd{dr}", d_yt_b, w_out_s_t, a_r, a_i, w_in_s_t, reverse=(dr == 0), dr=dr, conj=True)
        du_dirs.append(du)
        lams.append(lam)
    ds, g["ssm_d"] = _ssm_ds(proj, d_yt, du_dirs[0], du_dirs[1], sp["ssm_d"], tm)

    d_proj = [dp, ds, d_gp, d_gs]
    dw_in_t = _mix_in_dw(d_proj, u2, tm)
    early += [dw_mo.reshape(square), dw_e, dw_in_t.reshape(sharded)]
    swapping = reducer.swap_start("a2", early[6:]) if reducer is not None else []
    dh1, dh1_b, g["mix_norm"] = _mm_norm_bwd("mix_in_dx", d_proj, w_in_t, NN, h1, sp["mix_norm"], dh2, tm, after=swapping)
    g["final_norm"] = g["final_norm"].reshape(D_MODEL)

    travelling = reducer.start("a", ["a1", "a2"], after=list(g.values())) if reducer is not None else []
    d_abr, d_abi, d_cm, d_bm = [], [], [], []
    for dr in range(2):
        da_r, da_i, d_win, d_woutt = _ssm_param_grads(f"ssm_param_grads{dr}", lams[dr], states[dr], s_in, d_yt_b,
                                                      reverse=(dr == 1), after=travelling)
        d_abr.append(da_r)
        d_abi.append(da_i)
        d_bm.append(d_win)
        d_cm.append(d_woutt)

    d_ar, d_ai, d_ldt, d_br, d_bi, d_cr, d_ci = _ssm_prep_bwd(*ssm_a, *ssm_b, d_abr, d_abi, d_bm, d_cm)
    per_group = (2 * SSM_GROUPS, SSM_GROUP * SSM_STATE)
    g["ssm_a_re"] = d_ar.reshape(2 * SSM_GROUPS, SSM_GROUP, SSM_STATE).sum(axis=1).reshape(sp["ssm_a_re"].shape)
    g["ssm_a_im"] = d_ai.reshape(2 * SSM_GROUPS, SSM_GROUP, SSM_STATE).sum(axis=1).reshape(sp["ssm_a_im"].shape)
    g["ssm_log_dt"] = d_ldt.reshape(per_group).sum(axis=1).reshape(sp["ssm_log_dt"].shape)
    g["ssm_b_re"] = d_br.reshape(sp["ssm_b_re"].shape)
    g["ssm_b_im"] = d_bi.reshape(sp["ssm_b_im"].shape)
    g["ssm_c_re"] = d_cr.reshape(sp["ssm_c_re"].shape)
    g["ssm_c_im"] = d_ci.reshape(sp["ssm_c_im"].shape)
    if reducer is not None:
        travelling = travelling + [d_ar, d_br, d_cr]
    dg1, dup1 = _ffn_bwd_act("ffn1_bwd_act", dh1_b, w_f1, g1, up1, tm, after=travelling)
    dw_f1 = [a.reshape(sharded) for a in _ffn_dw("ffn1_dw", u1, dg1, dup1, a1, dh1_b, tm)]
    if reducer is not None:
        joining = reducer.join_start("a", after=reducer.finish("a", reducer.swap_start("b1", dw_f1)))
        travelling = joining + reducer.start("b", ["b1"], after=joining)
    grad_x, _, g["ffn1_norm"] = _ffn_dx("ffn1_dx", dg1, dup1, w_f1, x, sp["ffn1_norm"], dh1, tm, after=travelling)
    if reducer is not None:
        reducer.join_finish("a", [grad_x])
    return loss, grad_x, early + dw_f1, g


def _mesh_place():
    x, y, c = lax.axis_index("x"), lax.axis_index("y"), lax.axis_index("c")
    chips = [(1 - x, y), (x, 1 - y), (1 - x, 1 - y)]
    return x, y, c, chips


def _remote(src, dst, send_sems, recv_sems, k, to):
    return pltpu.make_async_remote_copy(src_ref=src, dst_ref=dst, send_sem=send_sems.at[k], recv_sem=recv_sems.at[k],
                                        device_id=to, device_id_type=MESH)


def _row_tile(rows, cap=512):
    return max(r for r in range(16, cap + 1, 16) if rows % r == 0)


REDUCE_STEPS = 2


def _chip_presum(tag, grads, gots, c_idx):
    n = len(grads)
    halves = [g.shape[1] // 2 for g in grads]
    tiles = [(h // REDUCE_STEPS, g.shape[2]) for h, g in zip(halves, grads)]

    def body(c_ref, *refs):
        for k in range(n):
            refs[2 * n + k][...] = (refs[k][...].astype(F32) + refs[n + k][...].astype(F32)).astype(BF16)

    mine = [BS((None, None) + tile, lambda s, i, c_ref: (s, c_ref[0], i, 0)) for tile in tiles]
    plain = [BS((None,) + tile, lambda s, i, c_ref: (s, i, 0)) for tile in tiles]
    return list(pl.pallas_call(
        body, out_shape=tuple(SDS((g.shape[0], h, g.shape[2]), BF16) for g, h in zip(grads, halves)),
        grid_spec=pltpu.PrefetchScalarGridSpec(num_scalar_prefetch=1, grid=(N_SHARD, REDUCE_STEPS),
                                               in_specs=mine + plain, out_specs=plain),
        name="reduce_presum_" + tag, compiler_params=_params("parallel", "parallel"))(
            c_idx, *[g.reshape(g.shape[0], 2, h, g.shape[2]) for g, h in zip(grads, halves)], *gots))


HBM_SPEC = BS(memory_space=pltpu.HBM)
SEM_SPEC = BS(memory_space=pltpu.SEMAPHORE)
DATAFLOW = pltpu.SideEffectType.DATAFLOW_SIDE_EFFECTING


def _chip_exchange_copies(parts, lands, send_sems, recv_sems):
    _, _, c, chips = _mesh_place()
    return [_remote(parts[k].at[2 * px + py], lands[k].at[j], send_sems, recv_sems, 3 * k + j, (px, py, c))
            for k in range(len(parts)) for j, (px, py) in enumerate(chips)]


def _gather_copies(shards, lands, send_sems, recv_sems):
    x, y, c, chips = _mesh_place()
    return [_remote(shards[k], lands[k].at[2 * x + y], send_sems, recv_sems, 3 * k + j, (px, py, c))
            for k in range(len(shards)) for j, (px, py) in enumerate(chips)]


def _gather_half_copies(shards, lands, send_sems, recv_sems):
    x, y, c, chips = _mesh_place()
    out = []
    for k in range(len(shards)):
        half = shards[k].shape[0] // 2
        mine = pl.ds(pl.multiple_of(c * half, 16), half)
        for j, (px, py) in enumerate(chips):
            out.append(_remote(shards[k].at[mine, :], lands[k].at[2 * x + y, mine, :], send_sems, recv_sems,
                               3 * k + j, (px, py, c)))
    return out


def _fill_copies(zones, _, send_sems, recv_sems):
    x, y, c, chips = _mesh_place()
    out = []
    for k in range(len(zones)):
        half = zones[k].shape[1] // 2
        mine = pl.ds(pl.multiple_of(c * half, 16), half)
        for j, (px, py) in enumerate(chips):
            blk = zones[k].at[2 * px + py, mine, :]
            out.append(_remote(blk, blk, send_sems, recv_sems, 3 * k + j, (x, y, 1 - c)))
    return out


def _sibling_fill(tag, lands):
    n = len(lands)

    def body(*refs):
        outs = refs[n:2 * n]
        copies = _fill_copies(outs, outs, *refs[2 * n:])
        for cp in copies:
            cp.start()
        for cp in copies:
            cp.wait_recv()
        for cp in copies:
            cp.wait_send()

    hbm = BS(memory_space=pl.ANY)
    return list(pl.pallas_call(
        body, out_shape=tuple(SDS(a.shape, a.dtype) for a in lands),
        in_specs=[hbm] * n, out_specs=(hbm,) * n, input_output_aliases={k: k for k in range(n)},
        scratch_shapes=[pltpu.SemaphoreType.DMA((3 * n,)), pltpu.SemaphoreType.DMA((3 * n,))],
        name="gather_fill_" + tag, compiler_params=_params())(*lands))


def _swap_copies(grads, lands, send_sems, recv_sems):
    x, y, c, _ = _mesh_place()
    out = []
    for k in range(len(grads)):
        half = grads[k].shape[1] // 2
        theirs = pl.ds(pl.multiple_of((1 - c) * half, 16), half)
        out.append(_remote(grads[k].at[:, theirs, :], lands[k], send_sems, recv_sems, k, (x, y, 1 - c)))
    return out


def _join_copies(fulls, same, send_sems, recv_sems):
    x, y, c, _ = _mesh_place()
    out = []
    for k in range(len(fulls)):
        half = fulls[k].shape[0] // 2
        mine = fulls[k].at[pl.ds(pl.multiple_of(c * half, 8), half), :]
        out.append(_remote(mine, mine, send_sems, recv_sems, k, (x, y, 1 - c)))
    return out


def _everyone_copies(packs, lands, send_sems, recv_sems):
    x, y, c, _ = _mesh_place()
    out = []
    for k in range(len(packs)):
        for j in range(N_DEV - 1):
            bx, by, bc = (j + 1) >> 2 & 1, (j + 1) >> 1 & 1, (j + 1) & 1
            peer = (x ^ bx, y ^ by, c ^ bc)
            out.append(_remote(packs[k], lands[k].at[4 * x + 2 * y + c], send_sems, recv_sems, (N_DEV - 1) * k + j, peer))
    return out


def _split_start(name, copies, sources, land_shapes, after=(), fanout=3):
    n = len(sources)
    n_land = len(land_shapes)
    m = n + n_land
    n_sems = fanout * n
    after_ops, after_specs = _after_operands(after)

    def body(*refs):
        ins = refs[:n]
        lands = refs[n:m] if n_land else ins
        send_sems, recv_sems = refs[m + len(after_ops)], refs[m + len(after_ops) + 1]
        token = refs[-1]
        for cp in copies(ins, lands, send_sems, recv_sems):
            cp.start()
        token[...] = jnp.zeros_like(token)

    lands = [pltpu.with_memory_space_constraint(lax.empty(s, d), pltpu.HBM) for s, d in land_shapes]
    sources = [pltpu.with_memory_space_constraint(p, pltpu.HBM) for p in sources]
    thru = [pltpu.HBM(a.shape, a.dtype) for a in sources + lands]
    out = pl.pallas_call(
        body, name=name,
        out_shape=(pltpu.SemaphoreType.DMA((n_sems,)), pltpu.SemaphoreType.DMA((n_sems,)), *thru, SDS((8, 128), F32)),
        in_specs=[HBM_SPEC] * m + after_specs,
        out_specs=(SEM_SPEC, SEM_SPEC, *[HBM_SPEC] * m, BS(memory_space=pltpu.VMEM)),
        input_output_aliases={i: 2 + i for i in range(m)},
        compiler_params=pltpu.CompilerParams(has_side_effects=DATAFLOW))(*sources, *lands, *after_ops)
    return out[0], out[1], list(out[2:2 + n]), list(out[2 + n:2 + m]), out[-1]


def _split_wait(name, copies, send_sems, recv_sems, sources, lands, after):
    n = len(sources)
    m = n + len(lands)
    after_ops, after_specs = _after_operands(after)

    def body(*refs):
        ins = refs[:n]
        zones = refs[n:m] if m > n else ins
        for cp in copies(ins, zones, refs[m], refs[m + 1]):
            cp.wait_send()
            cp.wait_recv()

    out = pl.pallas_call(
        body, name=name,
        out_shape=tuple(pltpu.HBM(a.shape, a.dtype) for a in sources + lands),
        in_specs=[HBM_SPEC] * m + [SEM_SPEC, SEM_SPEC] + after_specs, out_specs=(HBM_SPEC,) * m,
        input_output_aliases={i: i for i in range(m)},
        compiler_params=pltpu.CompilerParams(has_side_effects=DATAFLOW))(*sources, *lands, send_sems, recv_sems, *after_ops)
    return list(out[:n]), list(out[n:])


class _WeightGatherer:
    def __init__(self, shards):
        self.shards, self.open, self.filling = shards, {}, {}
        self.me = 2 * lax.axis_index("x") + lax.axis_index("y")

    HALVED = ("f1a", "mix")

    def start(self, tag, after=()):
        shapes = [((N_SHARD,) + s.shape, s.dtype) for s in self.shards[tag]]
        copies = _gather_half_copies if tag in self.HALVED else _gather_copies
        self.open[tag] = _split_start("gather_start_" + tag, copies, self.shards[tag], shapes, after)
        return [self.open[tag][-1]]

    def sources(self, tags):
        return [s for tag in tags for s in self.shards[tag]]

    def fill_start(self, tag, after):
        send_sems, recv_sems, shards, lands, _ = self.open.pop(tag)
        shards, lands = _split_wait("gather_wait_" + tag, _gather_half_copies, send_sems, recv_sems, shards, lands, after)
        self.filling[tag] = shards, _split_start("gather_fill_start_" + tag, _fill_copies, lands, [])
        return [self.filling[tag][1][-1]]

    def finish(self, tag, after):
        if tag in self.filling:
            shards, (send_sems, recv_sems, lands, _, _) = self.filling.pop(tag)
            lands, _ = _split_wait("gather_fill_wait_" + tag, _fill_copies, send_sems, recv_sems, lands, [], after)
            return [lax.dynamic_update_slice(zone, s[None], (self.me, 0, 0)) for zone, s in zip(lands, shards)]
        send_sems, recv_sems, shards, lands, _ = self.open.pop(tag)
        copies = _gather_half_copies if tag in self.HALVED else _gather_copies
        shards, lands = _split_wait("gather_wait_" + tag, copies, send_sems, recv_sems, shards, lands, after)
        if tag in self.HALVED:
            lands = _sibling_fill(tag, lands)
        return [lax.dynamic_update_slice(zone, s[None], (self.me, 0, 0)) for zone, s in zip(lands, shards)]


class _GradReducer:
    def __init__(self):
        self.c_idx = lax.axis_index("c").astype(jnp.int32).reshape(1)
        self.place = jnp.stack([2 * lax.axis_index("x") + lax.axis_index("y"), lax.axis_index("c")]).astype(jnp.int32)
        self.swaps, self.open, self.landed, self.joins, self.reduced = {}, {}, {}, {}, []

    def swap_start(self, tag, grads, after=()):
        shapes = [((g.shape[0], g.shape[1] // 2, g.shape[2]), g.dtype) for g in grads]
        self.swaps[tag] = _split_start("reduce_swap_start_" + tag, _swap_copies, grads, shapes, after, fanout=1)
        return [self.swaps[tag][-1]]

    def start(self, tag, swapped, after):
        pairs = []
        for s in swapped:
            send_sems, recv_sems, early, lands, _ = self.swaps.pop(s)
            pairs += zip(*_split_wait("reduce_swap_wait_" + s, _swap_copies, send_sems, recv_sems, early, lands, list(after)))
        parts = _chip_presum(tag, [g for g, _ in pairs], [s for _, s in pairs], self.c_idx)
        shapes = [((3,) + p.shape[1:], p.dtype) for p in parts]
        self.open[tag] = _split_start("reduce_exchange_start_" + tag, _chip_exchange_copies, parts, shapes)
        return [self.open[tag][-1]]

    def finish(self, tag, after):
        send_sems, recv_sems, parts, lands, _ = self.open.pop(tag)
        self.landed[tag] = _split_wait("reduce_exchange_wait_" + tag, _chip_exchange_copies, send_sems, recv_sems, parts, lands, after)
        return self.landed[tag][1][:1]

    def _sums(self, tag, after=()):
        parts, landed = self.landed.pop(tag)
        return _chip_sum(tag, parts, landed, self.place, after)

    def join_start(self, tag, after=()):
        self.joins[tag] = _split_start("reduce_join_start_" + tag, _join_copies, self._sums(tag, after), [], fanout=1)
        return [self.joins[tag][-1]]

    def join_finish(self, tag, after):
        send_sems, recv_sems, fulls, _, _ = self.joins.pop(tag)
        self.reduced += _split_wait("reduce_join_wait_" + tag, _join_copies, send_sems, recv_sems, fulls, [], after)[0]


def _chip_sum(tag, parts, gots, place, after=()):
    n = len(parts)
    tiles = [(p.shape[1] // REDUCE_STEPS, p.shape[2]) for p in parts]
    after_ops, after_specs = _after_operands(after)

    def body(place_ref, *refs):
        outs = refs[2 * n + len(after_ops):]
        for k in range(n):
            acc = refs[k][...].astype(F32)
            for j in range(3):
                acc = acc + refs[n + k][j].astype(F32)
            outs[k][...] = acc

    return list(pl.pallas_call(
        body, out_shape=tuple(SDS((2 * p.shape[1], p.shape[2]), F32) for p in parts),
        grid_spec=pltpu.PrefetchScalarGridSpec(
            num_scalar_prefetch=1, grid=(REDUCE_STEPS,),
            in_specs=[BS((None,) + tile, lambda i, place_ref: (place_ref[0], i, 0)) for tile in tiles]
            + [BS((3,) + tile, lambda i, place_ref: (0, i, 0)) for tile in tiles] + after_specs,
            out_specs=[BS(tile, lambda i, place_ref: (place_ref[1] * REDUCE_STEPS + i, 0)) for tile in tiles]),
        name="reduce_sum_" + tag, compiler_params=_params("parallel"))(place, *parts, *gots, *after_ops))


N_DEV = 8


def _sum_devices(packs):
    _, rows, lanes = packs.shape

    def body(p_ref, o_ref):
        acc = p_ref[0]
        for dev in range(1, N_DEV):
            acc = acc + p_ref[dev]
        o_ref[...] = acc

    vm = BS(memory_space=pltpu.VMEM)
    return pl.pallas_call(body, out_shape=SDS((rows, lanes), F32), in_specs=[vm], out_specs=vm,
                          name="small_sum", compiler_params=_params())(packs)


def _adamw_refs(w_ref, g_ref, m_ref, v_ref, go_ref, d_ref, mo_ref, vo_ref):
    bc1 = 1.0 - ADAM_B1 ** ADAM_STEP
    bc2 = 1.0 - ADAM_B2 ** ADAM_STEP
    g = g_ref[...]
    m_new = ADAM_B1 * m_ref[...] + (1.0 - ADAM_B1) * g
    v_new = ADAM_B2 * v_ref[...] + (1.0 - ADAM_B2) * (g * g)
    go_ref[...] = g
    mo_ref[...] = m_new
    vo_ref[...] = v_new
    d_ref[...] = -ADAM_LR * ((m_new / bc1) / (jnp.sqrt(v_new / bc2) + ADAM_EPS) + ADAM_WD * w_ref[...])


def _adamw_small(ws, gs, ms, vs):
    n = len(ws)

    def body(*refs):
        for k in range(n):
            _adamw_refs(*[refs[j * n + k] for j in range(4)], *refs[4 * n + 4 * k:4 * n + 4 * k + 4])

    vm = BS(memory_space=pltpu.VMEM)
    outs = pl.pallas_call(
        body, out_shape=tuple(SDS(a.shape, F32) for a in ws for _ in range(4)), in_specs=[vm] * (4 * n),
        out_specs=(vm,) * (4 * n), name="adamw_small", compiler_params=_params())(*ws, *gs, *ms, *vs)
    return [outs[4 * k:4 * k + 4] for k in range(n)]


def _adamw(name, w, grad, row0, m, v, after=()):
    rows, cols = w.shape
    tr = rows if rows < 16 else _row_tile(rows, 352)
    after_ops, after_specs = _after_operands(after)

    def body(w_ref, g_ref, m_ref, v_ref, *rest):
        _adamw_refs(w_ref, g_ref, m_ref, v_ref, *rest[len(after_ops):])

    blk = BS((tr, cols), lambda i: (i, 0))
    shape = SDS((rows, cols), F32)
    return pl.pallas_call(
        body, out_shape=(shape,) * 4, grid=(rows // tr,),
        in_specs=[blk, BS((tr, cols), lambda i: (row0 // tr + i, 0)), blk, blk] + after_specs, out_specs=(blk,) * 4,
        name=name, compiler_params=_params("parallel"))(w, grad, m, v, *after_ops)


SMALL_LANES = 128


SMALL_TILE = 8 * SMALL_LANES


def _packed_rows(p):
    return -(-p.size // SMALL_TILE) * 8


def _pack_small(parts):
    tiles = [jnp.pad(jnp.ravel(p), (0, _packed_rows(p) * SMALL_LANES - p.size)).reshape(-1, SMALL_LANES) for p in parts]
    rows = sum(t.shape[0] for t in tiles)
    return jnp.concatenate(tiles + [jnp.zeros((-rows % 64, SMALL_LANES), F32)], axis=0)


def _unpack_small(packed, like):
    out, at = [], 0
    for p in like:
        rows = _packed_rows(p)
        out.append(jnp.ravel(packed[at:at + rows])[:p.size].reshape(p.shape))
        at += rows
    return out


def kernel(x, mem, ffn1_norm, ffn1_w_gate, ffn1_w_up, ffn1_w_down, mix_norm, w_in, pool_w, pool_scale, w_pool_proj, ssm_a_re, ssm_a_im, ssm_log_dt, ssm_b_re, ssm_b_im, ssm_c_re, ssm_c_im, ssm_d, w_glu_val, w_glu_gate, w_mix_out, xattn_norm, mem_norm, w_q, w_kv, w_xo, ffn2_norm, ffn2_w_gate, ffn2_w_up, ffn2_w_down, final_norm, loss_target, m_ffn1_norm, m_ffn1_w_gate, m_ffn1_w_up, m_ffn1_w_down, m_mix_norm, m_w_in, m_pool_w, m_pool_scale, m_w_pool_proj, m_ssm_a_re, m_ssm_a_im, m_ssm_log_dt, m_ssm_b_re, m_ssm_b_im, m_ssm_c_re, m_ssm_c_im, m_ssm_d, m_w_glu_val, m_w_glu_gate, m_w_mix_out, m_xattn_norm, m_mem_norm, m_w_q, m_w_kv, m_w_xo, m_ffn2_norm, m_ffn2_w_gate, m_ffn2_w_up, m_ffn2_w_down, m_final_norm, v_ffn1_norm, v_ffn1_w_gate, v_ffn1_w_up, v_ffn1_w_down, v_mix_norm, v_w_in, v_pool_w, v_pool_scale, v_w_pool_proj, v_ssm_a_re, v_ssm_a_im, v_ssm_log_dt, v_ssm_b_re, v_ssm_b_im, v_ssm_c_re, v_ssm_c_im, v_ssm_d, v_w_glu_val, v_w_glu_gate, v_w_mix_out, v_xattn_norm, v_mem_norm, v_w_q, v_w_kv, v_w_xo, v_ffn2_norm, v_ffn2_w_gate, v_ffn2_w_up, v_ffn2_w_down, v_final_norm):
    given = dict(locals())
    w = {n: given[n] for n in WEIGHTS}
    m = {n: given["m_" + n] for n in WEIGHTS}
    v = {n: given["v_" + n] for n in WEIGHTS}

    def shard_view(a, n):
        return a[0].T if n in TRANSPOSED else a[0]

    def shard_unview(a, n):
        return (a.T if n in TRANSPOSED else a)[None]

    shards = {tag: [jnp.concatenate([shard_view(w[n], n).astype(BF16) for n in grp], axis=0) for grp in arrays]
              for tag, arrays in GATHER_PHASES.items()}
    reducer = _GradReducer()
    ws, ms, vs = ({n: _small_view(a[n], n) for n in SMALL} for a in (w, m, v))
    loss_part, grad_x, _, small = _device_step(x[0], mem[0], loss_target[0], _WeightGatherer(shards), ws, reducer)

    small_like = [ws[n] for n in SMALL] + [loss_part[0, :1]]
    pack = _pack_small([small[n] for n in SMALL] + [loss_part[0, :1]])
    everyone = _split_start("small_start", _everyone_copies, [pack], [((N_DEV,) + pack.shape, F32)], fanout=N_DEV - 1)

    grads, delta, new_m, new_v = {}, {}, {}, {}
    big_done = []

    def update(groups, reduced, after=()):
        for grp, red in zip(groups, reduced):
            row0 = 0
            for n in grp:
                w_n = shard_view(w[n], n)
                outs = _adamw("adamw_" + n, w_n, red, row0, shard_view(m[n], n), shard_view(v[n], n),
                              after=[everyone[-1], *after])
                grads[n], delta[n], new_m[n], new_v[n] = (shard_unview(o, n) for o in outs)
                big_done.append(outs[1])
                row0 += w_n.shape[0]

    n_a, n_f2 = len(reducer.reduced), 3
    update(REDUCE_GROUPS[n_f2:n_a], reducer.reduced[n_f2:])
    reducer.finish("b", list(big_done))
    update(REDUCE_GROUPS[:n_f2], reducer.reduced[:n_f2], after=reducer.join_start("b"))
    reducer.join_finish("b", big_done[-n_f2:])
    update(REDUCE_GROUPS[n_a:], reducer.reduced[n_a:])

    send_sems, recv_sems, packs, landed, _ = everyone
    packs, landed = _split_wait("small_wait", _everyone_copies, send_sems, recv_sems, packs, landed, big_done)
    mine = 4 * lax.axis_index("x") + 2 * lax.axis_index("y") + lax.axis_index("c")
    summed = _sum_devices(lax.dynamic_update_slice(landed[0], packs[0][None], (mine, 0, 0)))
    g_small = dict(zip(SMALL + ("loss",), _unpack_small(summed, small_like)))
    loss = g_small.pop("loss").reshape(())
    def two_d(a):
        return a.reshape(-1, a.shape[-1])

    updated = _adamw_small(*([two_d(a[n]) for n in SMALL] for a in (ws, g_small, ms, vs)))
    for n, outs in zip(SMALL, updated):
        grads[n], delta[n], new_m[n], new_v[n] = (_small_view(o.reshape(ws[n].shape), n) for o in outs)

    return (loss, grad_x[None], *[grads[n] for n in WEIGHTS], *[delta[n] for n in WEIGHTS],
            *[new_m[n] for n in WEIGHTS], *[new_v[n] for n in WEIGHTS])
```
